```python
import math
import jax, jax.numpy as jnp
from jax import lax
import numpy as np

D_MODEL = 1024
BATCH = 8
SEQ = 2048
DEPTH = 1

PLE_DIM = 256
D_MIX = D_MODEL
D_GMLP = D_MIX // 2
GMLP_GROUPS = 4
GMLP_GROUP_DIM = D_GMLP // GMLP_GROUPS
CHUNK = 128
D_ATTN = D_MIX - D_GMLP
HEAD_DIM = 64
N_Q_HEADS = D_ATTN // HEAD_DIM
N_KV_HEADS = 2
Q_PER_KV = N_Q_HEADS // N_KV_HEADS
WINDOW = 128
BLOCK = WINDOW
REL_BUCKETS = 32
REL_MAX_DIST = 128
D_FF = 4 * D_MODEL
D_IN = 2 * D_GMLP + D_ATTN + 2 * N_KV_HEADS * HEAD_DIM
EPS = 1e-6
NEG_INF = -1e30

kernel_name = "hybrid_gmlp_swa_sink_block"


def rmsnorm(x, g):
    xf = x.astype(jnp.float32)
    y = xf * lax.rsqrt(jnp.mean(xf * xf, axis=-1, keepdims=True) + EPS)
    return (y * g.astype(jnp.float32)).astype(x.dtype)


def t5_causal_bucket(n):
    max_exact = REL_BUCKETS // 2
    nf = jnp.maximum(n, 1).astype(jnp.float32)
    large = max_exact + (jnp.log(nf / max_exact) / math.log(REL_MAX_DIST / max_exact)
                         * (REL_BUCKETS - max_exact)).astype(jnp.int32)
    large = jnp.minimum(large, REL_BUCKETS - 1)
    return jnp.where(n < max_exact, n, large)


def band_bias_and_mask(rel_table):
    a = jnp.arange(BLOCK)[:, None]
    j = jnp.arange(2 * BLOCK)[None, :]
    n = BLOCK + a - j
    valid = (n >= 0) & (n < WINDOW)
    bucket = t5_causal_bucket(jnp.maximum(n, 0))
    bias = jnp.transpose(rel_table[bucket], (2, 0, 1))
    return bias, valid


def gmlp_chunk_mixer(u, v, v_gain, w_spatial, b_spatial):
    B, S, _ = u.shape
    nc = S // CHUNK
    vg = v.reshape(B, S, GMLP_GROUPS, GMLP_GROUP_DIM)
    vg = rmsnorm(vg, v_gain.reshape(GMLP_GROUPS, GMLP_GROUP_DIM))
    vc = vg.reshape(B, nc, CHUNK, GMLP_GROUPS, GMLP_GROUP_DIM)
    causal = jnp.tril(jnp.ones((CHUNK, CHUNK), dtype=bool))
    w = jnp.where(causal[None], w_spatial, 0).astype(vc.dtype)
    sv = jnp.einsum('gts,bnsgd->bntgd', w, vc) + jnp.transpose(b_spatial)[:, :, None].astype(vc.dtype)
    return u * sv.reshape(B, S, D_GMLP)


def swa_sink_attention(q, k, v, sinks, rel_table):
    B, S, _ = q.shape
    nb = S // BLOCK
    qb = q.reshape(B, nb, BLOCK, N_KV_HEADS, Q_PER_KV, HEAD_DIM)

    def band(t):
        t = t.reshape(B, S, N_KV_HEADS, HEAD_DIM)
        t = jnp.pad(t, ((0, 0), (BLOCK, 0), (0, 0), (0, 0)))
        t = t.reshape(B, nb + 1, BLOCK, N_KV_HEADS, HEAD_DIM)
        return jnp.concatenate([t[:, :-1], t[:, 1:]], axis=2)

    kb, vb = band(k), band(v)
    logits = jnp.einsum('bnqkgd,bnskd->bkgnqs', qb, kb).astype(jnp.float32) * (HEAD_DIM ** -0.5)
    bias, valid = band_bias_and_mask(rel_table)
    bias = bias.astype(jnp.float32).reshape(N_KV_HEADS, Q_PER_KV, 1, BLOCK, 2 * BLOCK)
    first = (jnp.arange(nb)[:, None, None] == 0) & (jnp.arange(2 * BLOCK)[None, None, :] < BLOCK)
    mask = valid[None] & ~first
    logits = jnp.where(mask, logits + bias, NEG_INF)
    sink = sinks.astype(jnp.float32).reshape(N_KV_HEADS, Q_PER_KV, 1, 1, 1)
    m = jnp.maximum(jnp.max(logits, axis=-1, keepdims=True), sink)
    e = jnp.exp(logits - m)
    denom = jnp.sum(e, axis=-1, keepdims=True) + jnp.exp(sink - m)
    probs = (e / denom).astype(v.dtype)
    out = jnp.einsum('bkgnqs,bnskd->bnqkgd', probs, vb)
    return out.reshape(B, S, D_ATTN)


def _fwd_setup_inputs(seed: int = 0) -> dict:
    key = jax.random.key(seed)
    ks = jax.random.split(key, 16)
    f32 = jnp.float32
    nrm = lambda k, shape, s: jax.random.normal(k, shape, f32) * s
    return {
        "x": nrm(ks[0], (BATCH, SEQ, D_MODEL), 1.0),
        "p": nrm(ks[1], (DEPTH, BATCH, SEQ, PLE_DIM), 1.0),
        "norm1_gain": 1.0 + nrm(ks[2], (DEPTH, D_MODEL), 0.02),
        "w_in": nrm(ks[3], (DEPTH, D_MODEL, D_IN), D_MODEL ** -0.5),
        "gmlp_v_gain": 1.0 + nrm(ks[4], (DEPTH, D_GMLP), 0.02),
        "w_spatial": nrm(ks[5], (DEPTH, GMLP_GROUPS, CHUNK, CHUNK), CHUNK ** -0.5),
        "b_spatial": 1.0 + nrm(ks[6], (DEPTH, GMLP_GROUPS, CHUNK), 0.01),
        "attn_sinks": nrm(ks[7], (DEPTH, N_Q_HEADS), 0.5),
        "rel_bias_table": nrm(ks[8], (REL_BUCKETS, N_Q_HEADS), 0.5),
        "w_out": nrm(ks[9], (DEPTH, D_MIX, D_MODEL), D_MIX ** -0.5),
        "norm2_gain": 1.0 + nrm(ks[10], (DEPTH, D_MODEL), 0.02),
        "w_ff1": nrm(ks[11], (DEPTH, D_MODEL, D_FF), D_MODEL ** -0.5),
        "w_ff2": nrm(ks[12], (DEPTH, D_FF, D_MODEL), D_FF ** -0.5),
        "w_ple_proj": nrm(ks[13], (DEPTH, PLE_DIM, D_MODEL), PLE_DIM ** -0.5),
        "w_ple_gate": nrm(ks[14], (DEPTH, D_MODEL, D_MODEL), D_MODEL ** -0.5),
        "final_gain": 1.0 + nrm(ks[15], (D_MODEL,), 0.02),
    }


def _fwd_reference(x, p, norm1_gain, w_in, gmlp_v_gain, w_spatial, b_spatial, attn_sinks,
              rel_bias_table, w_out, norm2_gain, w_ff1, w_ff2, w_ple_proj, w_ple_gate,
              final_gain):
    h = x
    for i in range(DEPTH):
        hn = rmsnorm(h, norm1_gain[i])
        z = hn @ w_in[i]
        zg = jax.nn.gelu(z[..., :2 * D_GMLP])
        u, vg = zg[..., :D_GMLP], zg[..., D_GMLP:]
        o = 2 * D_GMLP
        q = z[..., o:o + D_ATTN]
        kv_w = N_KV_HEADS * HEAD_DIM
        k = z[..., o + D_ATTN:o + D_ATTN + kv_w]
        v = z[..., o + D_ATTN + kv_w:]
        a_out = gmlp_chunk_mixer(u, vg, gmlp_v_gain[i], w_spatial[i], b_spatial[i])
        b_out = swa_sink_attention(q, k, v, attn_sinks[i], rel_bias_table)
        h = h + jnp.concatenate([a_out, b_out], axis=-1) @ w_out[i]
        hn = rmsnorm(h, norm2_gain[i])
        h = h + jnp.square(jax.nn.relu(hn @ w_ff1[i])) @ w_ff2[i]
        gate = jax.nn.sigmoid(h @ w_ple_gate[i])
        h = h + gate * (p[i] @ w_ple_proj[i])
    return rmsnorm(h, final_gain)


import jax as _jax
import jax.numpy as _jnp

TWIN_FORMAT = 'train_step'
FWD_PARAMS = ['x', 'p', 'norm1_gain', 'w_in', 'gmlp_v_gain', 'w_spatial', 'b_spatial', 'attn_sinks', 'rel_bias_table', 'w_out', 'norm2_gain', 'w_ff1', 'w_ff2', 'w_ple_proj', 'w_ple_gate', 'final_gain']
TWIN_WEIGHTS = ['norm1_gain', 'w_in', 'gmlp_v_gain', 'w_spatial', 'b_spatial', 'attn_sinks', 'rel_bias_table', 'w_out', 'norm2_gain', 'w_ff1', 'w_ff2', 'w_ple_proj', 'w_ple_gate', 'final_gain']
TWIN_DIFF_INPUT = 'x'
TWIN_INPUTS = ['x', 'p', 'norm1_gain', 'w_in', 'gmlp_v_gain', 'w_spatial', 'b_spatial', 'attn_sinks', 'rel_bias_table', 'w_out', 'norm2_gain', 'w_ff1', 'w_ff2', 'w_ple_proj', 'w_ple_gate', 'final_gain', 'loss_target', 'm_norm1_gain', 'm_w_in', 'm_gmlp_v_gain', 'm_w_spatial', 'm_b_spatial', 'm_attn_sinks', 'm_rel_bias_table', 'm_w_out', 'm_norm2_gain', 'm_w_ff1', 'm_w_ff2', 'm_w_ple_proj', 'm_w_ple_gate', 'm_final_gain', 'v_norm1_gain', 'v_w_in', 'v_gmlp_v_gain', 'v_w_spatial', 'v_b_spatial', 'v_attn_sinks', 'v_rel_bias_table', 'v_w_out', 'v_norm2_gain', 'v_w_ff1', 'v_w_ff2', 'v_w_ple_proj', 'v_w_ple_gate', 'v_final_gain']
TWIN_OUTPUTS = ['loss', 'grad_x', 'grad_norm1_gain', 'grad_w_in', 'grad_gmlp_v_gain', 'grad_w_spatial', 'grad_b_spatial', 'grad_attn_sinks', 'grad_rel_bias_table', 'grad_w_out', 'grad_norm2_gain', 'grad_w_ff1', 'grad_w_ff2', 'grad_w_ple_proj', 'grad_w_ple_gate', 'grad_final_gain', 'delta_norm1_gain', 'delta_w_in', 'delta_gmlp_v_gain', 'delta_w_spatial', 'delta_b_spatial', 'delta_attn_sinks', 'delta_rel_bias_table', 'delta_w_out', 'delta_norm2_gain', 'delta_w_ff1', 'delta_w_ff2', 'delta_w_ple_proj', 'delta_w_ple_gate', 'delta_final_gain', 'new_m_norm1_gain', 'new_m_w_in', 'new_m_gmlp_v_gain', 'new_m_w_spatial', 'new_m_b_spatial', 'new_m_attn_sinks', 'new_m_rel_bias_table', 'new_m_w_out', 'new_m_norm2_gain', 'new_m_w_ff1', 'new_m_w_ff2', 'new_m_w_ple_proj', 'new_m_w_ple_gate', 'new_m_final_gain', 'new_v_norm1_gain', 'new_v_w_in', 'new_v_gmlp_v_gain', 'new_v_w_spatial', 'new_v_b_spatial', 'new_v_attn_sinks', 'new_v_rel_bias_table', 'new_v_w_out', 'new_v_norm2_gain', 'new_v_w_ff1', 'new_v_w_ff2', 'new_v_w_ple_proj', 'new_v_w_ple_gate', 'new_v_final_gain']
TWIN_LEAF_KINDS = {'loss': 'loss', 'grad_x': 'grad_x', 'grad_norm1_gain': 'grad_w', 'grad_w_in': 'grad_w', 'grad_gmlp_v_gain': 'grad_w', 'grad_w_spatial': 'grad_w', 'grad_b_spatial': 'grad_w', 'grad_attn_sinks': 'grad_w', 'grad_rel_bias_table': 'grad_w', 'grad_w_out': 'grad_w', 'grad_norm2_gain': 'grad_w', 'grad_w_ff1': 'grad_w', 'grad_w_ff2': 'grad_w', 'grad_w_ple_proj': 'grad_w', 'grad_w_ple_gate': 'grad_w', 'grad_final_gain': 'grad_w', 'delta_norm1_gain': 'delta_w', 'delta_w_in': 'delta_w', 'delta_gmlp_v_gain': 'delta_w', 'delta_w_spatial': 'delta_w', 'delta_b_spatial': 'delta_w', 'delta_attn_sinks': 'delta_w', 'delta_rel_bias_table': 'delta_w', 'delta_w_out': 'delta_w', 'delta_norm2_gain': 'delta_w', 'delta_w_ff1': 'delta_w', 'delta_w_ff2': 'delta_w', 'delta_w_ple_proj': 'delta_w', 'delta_w_ple_gate': 'delta_w', 'delta_final_gain': 'delta_w', 'new_m_norm1_gain': 'new_m', 'new_m_w_in': 'new_m', 'new_m_gmlp_v_gain': 'new_m', 'new_m_w_spatial': 'new_m', 'new_m_b_spatial': 'new_m', 'new_m_attn_sinks': 'new_m', 'new_m_rel_bias_table': 'new_m', 'new_m_w_out': 'new_m', 'new_m_norm2_gain': 'new_m', 'new_m_w_ff1': 'new_m', 'new_m_w_ff2': 'new_m', 'new_m_w_ple_proj': 'new_m', 'new_m_w_ple_gate': 'new_m', 'new_m_final_gain': 'new_m', 'new_v_norm1_gain': 'new_v', 'new_v_w_in': 'new_v', 'new_v_gmlp_v_gain': 'new_v', 'new_v_w_spatial': 'new_v', 'new_v_b_spatial': 'new_v', 'new_v_attn_sinks': 'new_v', 'new_v_rel_bias_table': 'new_v', 'new_v_w_out': 'new_v', 'new_v_norm2_gain': 'new_v', 'new_v_w_ff1': 'new_v', 'new_v_w_ff2': 'new_v', 'new_v_w_ple_proj': 'new_v', 'new_v_w_ple_gate': 'new_v', 'new_v_final_gain': 'new_v'}


def _forward(args):
    return _fwd_reference(*[args[k] for k in FWD_PARAMS])


def _output_shape():
    out = _jax.eval_shape(lambda: _forward(_fwd_setup_inputs(0)))
    return out.shape, out.dtype

N_MICROBATCH = 1
ADAM_LR = 0.001
ADAM_B1 = 0.9
ADAM_B2 = 0.999
ADAM_EPS = 1e-08
ADAM_WD = 0.01
ADAM_STEP = 10
PER_EXAMPLE_BATCH_AXIS = {'x': 0, 'p': 1, 'loss_target': 0}
SHARED_INPUTS = []
_WEIGHT_DTYPES = {'norm1_gain': _jnp.float32, 'w_in': _jnp.float32, 'gmlp_v_gain': _jnp.float32, 'w_spatial': _jnp.float32, 'b_spatial': _jnp.float32, 'attn_sinks': _jnp.float32, 'rel_bias_table': _jnp.float32, 'w_out': _jnp.float32, 'norm2_gain': _jnp.float32, 'w_ff1': _jnp.float32, 'w_ff2': _jnp.float32, 'w_ple_proj': _jnp.float32, 'w_ple_gate': _jnp.float32, 'final_gain': _jnp.float32}
MOMENT_SCALE = {'norm1_gain': 8.075326e-02, 'w_in': 6.042168e-02, 'gmlp_v_gain': 5.239618e-02, 'w_spatial': 5.158209e-02, 'b_spatial': 7.348417e-02, 'attn_sinks': 2.155401e-02, 'rel_bias_table': 3.054722e-02, 'w_out': 6.671294e-02, 'norm2_gain': 1.045905e-01, 'w_ff1': 5.103385e-02, 'w_ff2': 9.663589e-02, 'w_ple_proj': 4.056737e-02, 'w_ple_gate': 2.259457e-02, 'final_gain': 1.612585e+01}


def _to_microbatches(a, axis):
    t = _jnp.moveaxis(a, axis, 0)
    t = t.reshape((N_MICROBATCH, t.shape[0] // N_MICROBATCH) + t.shape[1:])
    return _jnp.moveaxis(t, 1, axis + 1)


def setup_inputs(seed: int = 0) -> dict:
    inp = _fwd_setup_inputs(seed)
    key = _jax.random.fold_in(_jax.random.key(seed), 7919)
    shape, _ = _output_shape()
    out = dict(inp)
    out["loss_target"] = _jax.random.normal(_jax.random.fold_in(key, 0), shape, _jnp.float32)
    for i, name in enumerate(TWIN_WEIGHTS):
        w = inp[name].astype(_jnp.float32)
        if MOMENT_SCALE is None:
            s = _jnp.sqrt(_jnp.mean(_jnp.square(w)) + 1e-30)
        else:
            s = MOMENT_SCALE[name]
        km, kv = _jax.random.split(_jax.random.fold_in(key, i + 1))
        out[name] = w
        out["m_" + name] = s * _jax.random.normal(km, w.shape, _jnp.float32)
        out["v_" + name] = (s * s) * _jax.random.uniform(kv, w.shape, _jnp.float32, 0.5, 1.5)
    if N_MICROBATCH > 1:
        for name, axis in PER_EXAMPLE_BATCH_AXIS.items():
            out[name] = _to_microbatches(out[name], axis)
    return {'x': out['x'], 'p': out['p'], 'norm1_gain': out['norm1_gain'], 'w_in': out['w_in'], 'gmlp_v_gain': out['gmlp_v_gain'], 'w_spatial': out['w_spatial'], 'b_spatial': out['b_spatial'], 'attn_sinks': out['attn_sinks'], 'rel_bias_table': out['rel_bias_table'], 'w_out': out['w_out'], 'norm2_gain': out['norm2_gain'], 'w_ff1': out['w_ff1'], 'w_ff2': out['w_ff2'], 'w_ple_proj': out['w_ple_proj'], 'w_ple_gate': out['w_ple_gate'], 'final_gain': out['final_gain'], 'loss_target': out['loss_target'], 'm_norm1_gain': out['m_norm1_gain'], 'm_w_in': out['m_w_in'], 'm_gmlp_v_gain': out['m_gmlp_v_gain'], 'm_w_spatial': out['m_w_spatial'], 'm_b_spatial': out['m_b_spatial'], 'm_attn_sinks': out['m_attn_sinks'], 'm_rel_bias_table': out['m_rel_bias_table'], 'm_w_out': out['m_w_out'], 'm_norm2_gain': out['m_norm2_gain'], 'm_w_ff1': out['m_w_ff1'], 'm_w_ff2': out['m_w_ff2'], 'm_w_ple_proj': out['m_w_ple_proj'], 'm_w_ple_gate': out['m_w_ple_gate'], 'm_final_gain': out['m_final_gain'], 'v_norm1_gain': out['v_norm1_gain'], 'v_w_in': out['v_w_in'], 'v_gmlp_v_gain': out['v_gmlp_v_gain'], 'v_w_spatial': out['v_w_spatial'], 'v_b_spatial': out['v_b_spatial'], 'v_attn_sinks': out['v_attn_sinks'], 'v_rel_bias_table': out['v_rel_bias_table'], 'v_w_out': out['v_w_out'], 'v_norm2_gain': out['v_norm2_gain'], 'v_w_ff1': out['v_w_ff1'], 'v_w_ff2': out['v_w_ff2'], 'v_w_ple_proj': out['v_w_ple_proj'], 'v_w_ple_gate': out['v_w_ple_gate'], 'v_final_gain': out['v_final_gain']}


def _loss(weights, diff, rest, loss_target):
    with _jax.named_scope("forward"):
        args = {**rest, TWIN_DIFF_INPUT: diff, **{k: w.astype(_WEIGHT_DTYPES[k]) for k, w in weights.items()}}
        y = _forward(args)
    with _jax.named_scope("loss_head"):
        err = _jnp.square(y.astype(_jnp.float32) - loss_target)
        return 0.5 * _jnp.sum(_jnp.mean(err, axis=-1)) if err.ndim else 0.5 * err


def _adamw(w, g, m, v):
    m = ADAM_B1 * m + (1.0 - ADAM_B1) * g
    v = ADAM_B2 * v + (1.0 - ADAM_B2) * _jnp.square(g)
    m_hat = m / (1.0 - ADAM_B1 ** ADAM_STEP)
    v_hat = v / (1.0 - ADAM_B2 ** ADAM_STEP)
    delta = -ADAM_LR * (m_hat / (_jnp.sqrt(v_hat) + ADAM_EPS) + ADAM_WD * w)
    return delta, m, v


def reference(x, p, norm1_gain, w_in, gmlp_v_gain, w_spatial, b_spatial, attn_sinks, rel_bias_table, w_out, norm2_gain, w_ff1, w_ff2, w_ple_proj, w_ple_gate, final_gain, loss_target, m_norm1_gain, m_w_in, m_gmlp_v_gain, m_w_spatial, m_b_spatial, m_attn_sinks, m_rel_bias_table, m_w_out, m_norm2_gain, m_w_ff1, m_w_ff2, m_w_ple_proj, m_w_ple_gate, m_final_gain, v_norm1_gain, v_w_in, v_gmlp_v_gain, v_w_spatial, v_b_spatial, v_attn_sinks, v_rel_bias_table, v_w_out, v_norm2_gain, v_w_ff1, v_w_ff2, v_w_ple_proj, v_w_ple_gate, v_final_gain):
    given = dict(x=x, p=p, norm1_gain=norm1_gain, w_in=w_in, gmlp_v_gain=gmlp_v_gain, w_spatial=w_spatial, b_spatial=b_spatial, attn_sinks=attn_sinks, rel_bias_table=rel_bias_table, w_out=w_out, norm2_gain=norm2_gain, w_ff1=w_ff1, w_ff2=w_ff2, w_ple_proj=w_ple_proj, w_ple_gate=w_ple_gate, final_gain=final_gain, loss_target=loss_target, m_norm1_gain=m_norm1_gain, m_w_in=m_w_in, m_gmlp_v_gain=m_gmlp_v_gain, m_w_spatial=m_w_spatial, m_b_spatial=m_b_spatial, m_attn_sinks=m_attn_sinks, m_rel_bias_table=m_rel_bias_table, m_w_out=m_w_out, m_norm2_gain=m_norm2_gain, m_w_ff1=m_w_ff1, m_w_ff2=m_w_ff2, m_w_ple_proj=m_w_ple_proj, m_w_ple_gate=m_w_ple_gate, m_final_gain=m_final_gain, v_norm1_gain=v_norm1_gain, v_w_in=v_w_in, v_gmlp_v_gain=v_gmlp_v_gain, v_w_spatial=v_w_spatial, v_b_spatial=v_b_spatial, v_attn_sinks=v_attn_sinks, v_rel_bias_table=v_rel_bias_table, v_w_out=v_w_out, v_norm2_gain=v_norm2_gain, v_w_ff1=v_w_ff1, v_w_ff2=v_w_ff2, v_w_ple_proj=v_w_ple_proj, v_w_ple_gate=v_w_ple_gate, v_final_gain=v_final_gain)
    weights = {n: given[n] for n in TWIN_WEIGHTS}
    shared = {n: given[n] for n in SHARED_INPUTS}
    per_example = {n: given[n] for n in ['x', 'p']}
    grad_fn = _jax.value_and_grad(_loss, argnums=(0, 1))

    def one_microbatch(ex, loss_target):
        ex = dict(ex)
        diff = ex.pop(TWIN_DIFF_INPUT)
        return grad_fn(weights, diff, {**shared, **ex}, loss_target)

    if N_MICROBATCH == 1:
        loss, (grad_w, grad_x) = one_microbatch(per_example, given["loss_target"])
    else:
        def body(carry, xs):
            loss_sum, grad_sum = carry
            l_k, (gw_k, gx_k) = one_microbatch(xs[0], xs[1])
            with _jax.named_scope("update"):
                return (loss_sum + l_k, _jax.tree.map(_jnp.add, grad_sum, gw_k)), gx_k

        init = (_jnp.zeros((), _jnp.float32), _jax.tree.map(_jnp.zeros_like, weights))
        (loss, grad_w), grad_x = _jax.lax.scan(body, init, (per_example, given["loss_target"]))
    with _jax.named_scope("update"):
        delta_w, new_m, new_v = {}, {}, {}
        for n in TWIN_WEIGHTS:
            delta_w[n], new_m[n], new_v[n] = _adamw(weights[n], grad_w[n], given["m_" + n], given["v_" + n])
    return (loss, grad_x, *[grad_w[n] for n in TWIN_WEIGHTS], *[delta_w[n] for n in TWIN_WEIGHTS],
            *[new_m[n] for n in TWIN_WEIGHTS], *[new_v[n] for n in TWIN_WEIGHTS])
```

```python
import functools
import math

import numpy as np
import jax
import jax.numpy as jnp
from jax import lax
from jax.experimental import pallas as pl
from jax.experimental.pallas import tpu as pltpu

F32 = jnp.float32
BF = jnp.bfloat16
MESH = pl.DeviceIdType.MESH
N_DEV = 8

D_MODEL = 1024
PLE_DIM = 256
D_GMLP = 512
GROUPS = 4
GDIM = 128
BLOCK = 128
D_ATTN = 512
HEAD_DIM = 64
N_Q = 8
Q_PER_KV = 4
D_KV = 128
D_FF = 4096
D_IN = 1792
D_MAIN = 2 * D_GMLP + D_ATTN
REL_BUCKETS = 32
EPS = 1e-6
NEG_INF = -1e30
SCALE = HEAD_DIM ** -0.5
GELU_C = math.sqrt(2.0 / math.pi)
GELU_A = 0.044715

ADAM_LR = 0.001
ADAM_B1 = 0.9
ADAM_B2 = 0.999
ADAM_EPS = 1e-08
ADAM_WD = 0.01
ADAM_STEP = 10

V7X_VMEM_LIMIT = 60000 * 1024
TOK_TILE = 256


def _call(body, **kw):
    return pl.pallas_call(body, **kw)


def _params(sem=None):
    if sem is None:
        return pltpu.CompilerParams(vmem_limit_bytes=V7X_VMEM_LIMIT)
    return pltpu.CompilerParams(dimension_semantics=sem, vmem_limit_bytes=V7X_VMEM_LIMIT)


def _nn(a, b):
    return jnp.dot(a, b, preferred_element_type=F32)


def _nt(a, b):
    return lax.dot_general(a, b, (((1,), (1,)), ((), ())), preferred_element_type=F32)


def _tn(a, b):
    return lax.dot_general(a, b, (((0,), (0,)), ((), ())), preferred_element_type=F32)


def _gelu_tanh(x):
    return jnp.tanh(GELU_C * (x + GELU_A * (x * x * x)))


def _gelu(x):
    return x * (0.5 * (1.0 + _gelu_tanh(x)))


def _gelu_grad(x):
    t = _gelu_tanh(x)
    return 0.5 * (1.0 + t) + 0.5 * x * (1.0 - t * t) * (GELU_C * (1.0 + 3.0 * GELU_A * (x * x)))


def _rms_scale(x):
    return lax.rsqrt(jnp.mean(x * x, axis=-1, keepdims=True) + EPS)


def _rms_bwd(dxn, x, r):
    return r * dxn - x * ((r * r * r) * jnp.mean(dxn * x, axis=-1, keepdims=True))


def _bucket_table():
    a = np.arange(BLOCK)[:, None]
    j = np.arange(2 * BLOCK)[None, :]
    n = BLOCK + a - j
    valid = (n >= 0) & (n < BLOCK)
    nc = np.maximum(n, 0)
    max_exact = REL_BUCKETS // 2
    nf = np.maximum(nc, 1).astype(np.float32)
    large = max_exact + (
        np.log(nf / np.float32(max_exact)) / np.float32(math.log(BLOCK / max_exact)) * np.float32(REL_BUCKETS - max_exact)
    ).astype(np.int32)
    large = np.minimum(large, REL_BUCKETS - 1)
    bucket = np.where(nc < max_exact, nc, large)
    return np.where(valid, bucket, -1).astype(np.int32)


def _in_proj(x, g1, w_in):
    s = x.shape[0]
    tm = min(TOK_TILE, s)

    def body(x_ref, g_ref, w_ref, zuv_ref, qkv_ref, hn_ref):
        xv = x_ref[...]
        hn = ((xv * _rms_scale(xv)) * g_ref[...]).astype(BF)
        hn_ref[...] = hn
        z = _nn(hn, w_ref[...])
        zuv_ref[...] = z[:, : 2 * D_GMLP]
        qkv_ref[...] = z[:, 2 * D_GMLP:].astype(BF)

    return _call(
        body,
        name="in_proj",
        grid=(s // tm,),
        in_specs=[
            pl.BlockSpec((tm, D_MODEL), lambda i: (i, 0)),
            pl.BlockSpec((1, D_MODEL), lambda i: (0, 0)),
            pl.BlockSpec((D_MODEL, D_IN), lambda i: (0, 0)),
        ],
        out_specs=[
            pl.BlockSpec((tm, 2 * D_GMLP), lambda i: (i, 0)),
            pl.BlockSpec((tm, D_ATTN + 2 * D_KV), lambda i: (i, 0)),
            pl.BlockSpec((tm, D_MODEL), lambda i: (i, 0)),
        ],
        out_shape=[
            jax.ShapeDtypeStruct((s, 2 * D_GMLP), F32),
            jax.ShapeDtypeStruct((s, D_ATTN + 2 * D_KV), BF),
            jax.ShapeDtypeStruct((s, D_MODEL), BF),
        ],
        compiler_params=_params(("arbitrary",)),
    )(x, g1, w_in)


def _build_bias(bias_ref, bucket_ref, table_ref):
    bucket = bucket_ref[...]
    for h in range(N_Q):
        acc = jnp.zeros((BLOCK, 2 * BLOCK), F32)
        for b in range(REL_BUCKETS):
            acc = jnp.where(bucket == b, table_ref[b, h], acc)
        bias_ref[h] = acc


def _window_masks(i):
    row = lax.broadcasted_iota(jnp.int32, (BLOCK, BLOCK), 0)
    col = lax.broadcasted_iota(jnp.int32, (BLOCK, BLOCK), 1)
    return (col > row) & (i > 0), col <= row


def _tril_bf16(w_ref, g):
    row = lax.broadcasted_iota(jnp.int32, (BLOCK, BLOCK), 0)
    col = lax.broadcasted_iota(jnp.int32, (BLOCK, BLOCK), 1)
    return jnp.where(col <= row, w_ref[g], 0.0).astype(BF)


def _attn_probs(q_h, k_prev, k_cur, bias_h, sink, valid_prev, valid_cur):
    l_prev = jnp.where(valid_prev, _nt(q_h, k_prev) * SCALE + bias_h[:, :BLOCK], NEG_INF)
    l_cur = jnp.where(valid_cur, _nt(q_h, k_cur) * SCALE + bias_h[:, BLOCK:], NEG_INF)
    m = jnp.maximum(jnp.maximum(jnp.max(l_prev, axis=-1, keepdims=True), jnp.max(l_cur, axis=-1, keepdims=True)), sink)
    e_prev = jnp.exp(l_prev - m)
    e_cur = jnp.exp(l_cur - m)
    e_sink = jnp.exp(sink - m)
    denom = jnp.sum(e_prev, axis=-1, keepdims=True) + jnp.sum(e_cur, axis=-1, keepdims=True) + e_sink
    return e_prev / denom, e_cur / denom, e_sink / denom


def _mixer_specs(nb):
    cl = lambda i: jnp.minimum(i, nb - 1)
    return [
        pl.BlockSpec((BLOCK, 2 * D_GMLP), lambda i: (cl(i), 0)),
        pl.BlockSpec((BLOCK, D_ATTN), lambda i: (cl(i), 0)),
        pl.BlockSpec((BLOCK, 2 * D_KV), lambda i: (cl(i), D_ATTN // (2 * D_KV))),
        pl.BlockSpec((BLOCK, 2 * D_KV), lambda i: (jnp.maximum(cl(i) - 1, 0), D_ATTN // (2 * D_KV))),
        pl.BlockSpec((1, D_GMLP), lambda i: (0, 0)),
        pl.BlockSpec((GROUPS, BLOCK, BLOCK), lambda i: (0, 0, 0)),
        pl.BlockSpec((GROUPS, BLOCK, 1), lambda i: (0, 0, 0)),
        pl.BlockSpec(memory_space=pltpu.SMEM),
        pl.BlockSpec(memory_space=pltpu.SMEM),
        pl.BlockSpec((BLOCK, 2 * BLOCK), lambda i: (0, 0)),
    ]


def _mixer_fwd(zuv, qkv, gv, w_sp, b_sp, sinks, table, bucket):
    s = zuv.shape[0]
    nb = s // BLOCK

    def body(zuv_ref, q_ref, kvc_ref, kvp_ref, gv_ref, w_ref, b_ref, sink_ref, table_ref, bucket_ref, mix_ref, bias_ref):
        i = pl.program_id(0)

        @pl.when(i == 0)
        def _():
            _build_bias(bias_ref, bucket_ref, table_ref)

        for g in range(GROUPS):
            sl = slice(g * GDIM, (g + 1) * GDIM)
            u = _gelu(zuv_ref[:, sl])
            vg = _gelu(zuv_ref[:, D_GMLP + g * GDIM: D_GMLP + (g + 1) * GDIM])
            vn = ((vg * _rms_scale(vg)) * gv_ref[:, sl]).astype(BF)
            sv = _nn(_tril_bf16(w_ref, g), vn) + b_ref[g]
            mix_ref[:, sl] = (u * sv).astype(BF)

        valid_prev, valid_cur = _window_masks(i)
        for h in range(N_Q):
            kh = h // Q_PER_KV
            q_h = q_ref[:, h * HEAD_DIM:(h + 1) * HEAD_DIM]
            ksl = slice(kh * HEAD_DIM, (kh + 1) * HEAD_DIM)
            vsl = slice(D_KV + kh * HEAD_DIM, D_KV + (kh + 1) * HEAD_DIM)
            p_prev, p_cur, _ = _attn_probs(q_h, kvp_ref[:, ksl], kvc_ref[:, ksl], bias_ref[h], sink_ref[0, h], valid_prev, valid_cur)
            o = _nn(p_prev.astype(BF), kvp_ref[:, vsl]) + _nn(p_cur.astype(BF), kvc_ref[:, vsl])
            mix_ref[:, D_GMLP + h * HEAD_DIM: D_GMLP + (h + 1) * HEAD_DIM] = o.astype(BF)

    return _call(
        body,
        name="mixer_fwd",
        grid=(nb,),
        in_specs=_mixer_specs(nb),
        out_specs=pl.BlockSpec((BLOCK, D_MODEL), lambda i: (i, 0)),
        out_shape=jax.ShapeDtypeStruct((s, D_MODEL), BF),
        scratch_shapes=[pltpu.VMEM((N_Q, BLOCK, 2 * BLOCK), F32)],
        compiler_params=_params(("arbitrary",)),
    )(zuv, qkv, qkv, qkv, gv, w_sp, b_sp, sinks, table, bucket)


def _tail(x, mix, p, t, g2, gf, w_out, w_ff1, w_ff2, w_gate, w_proj):
    s = x.shape[0]
    tm = min(TOK_TILE, s)
    n_ff = w_ff1.shape[0]
    fc = D_FF // n_ff

    def body(x_ref, mix_ref, p_ref, t_ref, g2_ref, gf_ref, wo_ref, w1_ref, w2_ref, wg_ref, wp_ref,
             loss_ref, dg2_ref, dgf_ref, dh1_ref, dh1b_ref, dmix_ref, hn2_ref, a_ref, df_ref, h2_ref, dh2_ref, dgl_ref, dpp_ref, f_ref):
        i = pl.program_id(0)

        @pl.when(i == 0)
        def _():
            loss_ref[...] = jnp.zeros_like(loss_ref)
            dg2_ref[...] = jnp.zeros_like(dg2_ref)
            dgf_ref[...] = jnp.zeros_like(dgf_ref)

        h1 = x_ref[...] + _nn(mix_ref[...], wo_ref[...])
        r2 = _rms_scale(h1)
        hn2 = ((h1 * r2) * g2_ref[...]).astype(BF)
        hn2_ref[...] = hn2
        h2 = h1
        for c in range(n_ff):
            f = _nn(hn2, w1_ref[c])
            f_ref[:, c * fc:(c + 1) * fc] = f
            a = jnp.square(jnp.maximum(f, 0.0)).astype(BF)
            a_ref[:, c * fc:(c + 1) * fc] = a
            h2 = h2 + _nn(a, w2_ref[c * fc:(c + 1) * fc, :])
        h2b = h2.astype(BF)
        h2_ref[...] = h2b
        gate = jax.nn.sigmoid(_nn(h2b, wg_ref[...]))
        pp = _nn(p_ref[...].astype(BF), wp_ref[...])
        h3 = h2 + gate * pp
        rf = _rms_scale(h3)
        gf_v = gf_ref[...]
        err = (h3 * rf) * gf_v - t_ref[...]
        loss_ref[...] += jnp.sum(jnp.sum(err * err, axis=-1, keepdims=True), axis=0, keepdims=True) * (0.5 / D_MODEL)

        dy = err * (1.0 / D_MODEL)
        dgf_ref[...] += jnp.sum(dy * (h3 * rf), axis=0, keepdims=True)
        dh3 = _rms_bwd(dy * gf_v, h3, rf)
        dpp_ref[...] = (dh3 * gate).astype(BF)
        dgl = ((dh3 * pp) * (gate * (1.0 - gate))).astype(BF)
        dgl_ref[...] = dgl
        dh2 = dh3 + _nt(dgl, wg_ref[...])
        dh2b = dh2.astype(BF)
        dh2_ref[...] = dh2b
        dhn2 = jnp.zeros((tm, D_MODEL), F32)
        for c in range(n_ff):
            da = _nt(dh2b, w2_ref[c * fc:(c + 1) * fc, :])
            df = (da * (2.0 * jnp.maximum(f_ref[:, c * fc:(c + 1) * fc], 0.0))).astype(BF)
            df_ref[:, c * fc:(c + 1) * fc] = df
            dhn2 = dhn2 + _nt(df, w1_ref[c])
        dg2_ref[...] += jnp.sum(dhn2 * (h1 * r2), axis=0, keepdims=True)
        dh1 = dh2 + _rms_bwd(dhn2 * g2_ref[...], h1, r2)
        dh1_ref[...] = dh1
        dh1b = dh1.astype(BF)
        dh1b_ref[...] = dh1b
        dmix_ref[...] = _nt(dh1b, wo_ref[...])

    tile = lambda cols: pl.BlockSpec((tm, cols), lambda i: (i, 0))
    whole = lambda shape: pl.BlockSpec(shape, lambda i: (0,) * len(shape), pipeline_mode=pl.Buffered(1))
    row = pl.BlockSpec((1, D_MODEL), lambda i: (0, 0))
    act = lambda cols, dt: jax.ShapeDtypeStruct((s, cols), dt)
    return _call(
        body,
        name="tail",
        grid=(s // tm,),
        in_specs=[tile(D_MODEL), tile(D_MODEL), tile(PLE_DIM), tile(D_MODEL), row, row,
                  whole(w_out.shape), whole(w_ff1.shape), whole(w_ff2.shape), whole(w_gate.shape), whole(w_proj.shape)],
        out_specs=[pl.BlockSpec((1, 1), lambda i: (0, 0)), row, row, tile(D_MODEL), tile(D_MODEL), tile(D_MODEL), tile(D_MODEL), tile(D_FF),
                   tile(D_FF), tile(D_MODEL), tile(D_MODEL), tile(D_MODEL), tile(D_MODEL)],
        out_shape=[jax.ShapeDtypeStruct((1, 1), F32), jax.ShapeDtypeStruct((1, D_MODEL), F32), jax.ShapeDtypeStruct((1, D_MODEL), F32),
                   act(D_MODEL, F32), act(D_MODEL, BF), act(D_MODEL, F32), act(D_MODEL, BF), act(D_FF, BF), act(D_FF, BF), act(D_MODEL, BF),
                   act(D_MODEL, BF), act(D_MODEL, BF), act(D_MODEL, BF)],
        scratch_shapes=[pltpu.VMEM((tm, D_FF), F32)],
        compiler_params=_params(("arbitrary",)),
    )(x, mix, p, t, g2, gf, w_out, w_ff1, w_ff2, w_gate, w_proj)


def _mixer_bwd(zuv, qkv, dmix, gv, w_sp, b_sp, sinks, table, bucket):
    s = zuv.shape[0]
    nb = s // BLOCK

    def body(zuv_ref, q_ref, kvc_ref, kvp_ref, gv_ref, w_ref, b_ref, sink_ref, table_ref, bucket_ref, dmix_ref,
             dzm_ref, dkv_ref, dgv_ref, dw_ref, db_ref, dsink_ref, dtable_ref,
             bias_ref, dbias_ref, carry_ref, dsink_acc):
        i = pl.program_id(0)

        @pl.when(i == 0)
        def _():
            _build_bias(bias_ref, bucket_ref, table_ref)
            dbias_ref[...] = jnp.zeros_like(dbias_ref)
            carry_ref[...] = jnp.zeros_like(carry_ref)
            dsink_acc[...] = jnp.zeros_like(dsink_acc)
            dgv_ref[...] = jnp.zeros_like(dgv_ref)
            dw_ref[...] = jnp.zeros_like(dw_ref)
            db_ref[...] = jnp.zeros_like(db_ref)

        @pl.when(i < nb)
        def _():
            for g in range(GROUPS):
                sl = slice(g * GDIM, (g + 1) * GDIM)
                slv = slice(D_GMLP + g * GDIM, D_GMLP + (g + 1) * GDIM)
                zu = zuv_ref[:, sl]
                zv = zuv_ref[:, slv]
                u = _gelu(zu)
                vg = _gelu(zv)
                rg = _rms_scale(vg)
                vhat = vg * rg
                gain = gv_ref[:, sl]
                vn = (vhat * gain).astype(BF)
                w_g = _tril_bf16(w_ref, g)
                sv = _nn(w_g, vn) + b_ref[g]
                dmix_a = dmix_ref[:, sl]
                dsv = dmix_a * u
                dsvb = dsv.astype(BF)
                db_ref[g] += jnp.sum(dsv, axis=-1, keepdims=True)
                dw_ref[g] += _nt(dsvb, vn)
                dvn = _tn(w_g, dsvb)
                dgv_ref[:, sl] += jnp.sum(dvn * vhat, axis=0, keepdims=True)
                dvg = _rms_bwd(dvn * gain, vg, rg)
                dzm_ref[:, sl] = ((dmix_a * sv) * _gelu_grad(zu)).astype(BF)
                dzm_ref[:, slv] = (dvg * _gelu_grad(zv)).astype(BF)

            valid_prev, valid_cur = _window_masks(i)
            for kh in range(N_Q // Q_PER_KV):
                ksl = slice(kh * HEAD_DIM, (kh + 1) * HEAD_DIM)
                vsl = slice(D_KV + kh * HEAD_DIM, D_KV + (kh + 1) * HEAD_DIM)
                k_prev, k_cur = kvp_ref[:, ksl], kvc_ref[:, ksl]
                v_prev, v_cur = kvp_ref[:, vsl], kvc_ref[:, vsl]
                dk_prev = jnp.zeros((BLOCK, HEAD_DIM), F32)
                dk_cur = jnp.zeros((BLOCK, HEAD_DIM), F32)
                dv_prev = jnp.zeros((BLOCK, HEAD_DIM), F32)
                dv_cur = jnp.zeros((BLOCK, HEAD_DIM), F32)
                for gq in range(Q_PER_KV):
                    h = kh * Q_PER_KV + gq
                    q_h = q_ref[:, h * HEAD_DIM:(h + 1) * HEAD_DIM]
                    p_prev, p_cur, p_sink = _attn_probs(q_h, k_prev, k_cur, bias_ref[h], sink_ref[0, h], valid_prev, valid_cur)
                    do = dmix_ref[:, D_GMLP + h * HEAD_DIM: D_GMLP + (h + 1) * HEAD_DIM].astype(BF)
                    dp_prev = _nt(do, v_prev)
                    dp_cur = _nt(do, v_cur)
                    delta = jnp.sum(p_prev * dp_prev, axis=-1, keepdims=True) + jnp.sum(p_cur * dp_cur, axis=-1, keepdims=True)
                    ds_prev = p_prev * (dp_prev - delta)
                    ds_cur = p_cur * (dp_cur - delta)
                    dsink_acc[h] -= p_sink * delta
                    dbias_ref[h, :, :BLOCK] += ds_prev
                    dbias_ref[h, :, BLOCK:] += ds_cur
                    dsb_prev = ds_prev.astype(BF)
                    dsb_cur = ds_cur.astype(BF)
                    dq = (_nn(dsb_prev, k_prev) + _nn(dsb_cur, k_cur)) * SCALE
                    dzm_ref[:, 2 * D_GMLP + h * HEAD_DIM: 2 * D_GMLP + (h + 1) * HEAD_DIM] = dq.astype(BF)
                    dk_prev = dk_prev + _tn(dsb_prev, q_h)
                    dk_cur = dk_cur + _tn(dsb_cur, q_h)
                    dv_prev = dv_prev + _tn(p_prev.astype(BF), do)
                    dv_cur = dv_cur + _tn(p_cur.astype(BF), do)
                dkv_ref[:, ksl] = (carry_ref[:, ksl] + dk_prev * SCALE).astype(BF)
                dkv_ref[:, vsl] = (carry_ref[:, vsl] + dv_prev).astype(BF)
                carry_ref[:, ksl] = dk_cur * SCALE
                carry_ref[:, vsl] = dv_cur

        @pl.when(i == nb)
        def _():
            dkv_ref[...] = carry_ref[...].astype(BF)
            row = lax.broadcasted_iota(jnp.int32, (BLOCK, BLOCK), 0)
            col = lax.broadcasted_iota(jnp.int32, (BLOCK, BLOCK), 1)
            for g in range(GROUPS):
                dw_ref[g] = jnp.where(col <= row, dw_ref[g], 0.0)
            bucket = bucket_ref[...]
            for h in range(N_Q):
                dsink_ref[0, h] = jnp.sum(dsink_acc[h])
                dbh = dbias_ref[h]
                for b in range(REL_BUCKETS):
                    dtable_ref[b, h] = jnp.sum(jnp.where(bucket == b, dbh, 0.0))

    cl = lambda i: jnp.minimum(i, nb - 1)
    const = lambda shape: pl.BlockSpec(shape, lambda i: (0,) * len(shape))
    return _call(
        body,
        name="mixer_bwd",
        grid=(nb + 1,),
        in_specs=_mixer_specs(nb) + [pl.BlockSpec((BLOCK, D_MODEL), lambda i: (cl(i), 0))],
        out_specs=[
            pl.BlockSpec((BLOCK, D_MAIN), lambda i: (cl(i), 0)),
            pl.BlockSpec((BLOCK, 2 * D_KV), lambda i: (jnp.maximum(i - 1, 0), 0)),
            const((1, D_GMLP)),
            const((GROUPS, BLOCK, BLOCK)),
            const((GROUPS, BLOCK, 1)),
            pl.BlockSpec(memory_space=pltpu.SMEM),
            pl.BlockSpec(memory_space=pltpu.SMEM),
        ],
        out_shape=[
            jax.ShapeDtypeStruct((s, D_MAIN), BF),
            jax.ShapeDtypeStruct((s, 2 * D_KV), BF),
            jax.ShapeDtypeStruct((1, D_GMLP), F32),
            jax.ShapeDtypeStruct((GROUPS, BLOCK, BLOCK), F32),
            jax.ShapeDtypeStruct((GROUPS, BLOCK, 1), F32),
            jax.ShapeDtypeStruct((1, N_Q), F32),
            jax.ShapeDtypeStruct((REL_BUCKETS, N_Q), F32),
        ],
        scratch_shapes=[
            pltpu.VMEM((N_Q, BLOCK, 2 * BLOCK), F32),
            pltpu.VMEM((N_Q, BLOCK, 2 * BLOCK), F32),
            pltpu.VMEM((BLOCK, 2 * D_KV), F32),
            pltpu.VMEM((N_Q, BLOCK, 1), F32),
        ],
        compiler_params=_params(("arbitrary",)),
    )(zuv, qkv, qkv, qkv, gv, w_sp, b_sp, sinks, table, bucket, dmix)


def _in_bwd(x, dh1, dzm, dkv, g1, w_in):
    s = x.shape[0]
    tm = min(TOK_TILE, s)

    def body(x_ref, dh1_ref, dzm_ref, dkv_ref, g_ref, w_ref, dx_ref, dg_ref):
        @pl.when(pl.program_id(0) == 0)
        def _():
            dg_ref[...] = jnp.zeros_like(dg_ref)

        dhn = _nt(dzm_ref[...], w_ref[:, :D_MAIN]) + _nt(dkv_ref[...], w_ref[:, D_MAIN:])
        xv = x_ref[...]
        r = _rms_scale(xv)
        dg_ref[...] += jnp.sum(dhn * (xv * r), axis=0, keepdims=True)
        dx_ref[...] = dh1_ref[...] + _rms_bwd(dhn * g_ref[...], xv, r)

    tile = lambda cols: pl.BlockSpec((tm, cols), lambda i: (i, 0))
    row = pl.BlockSpec((1, D_MODEL), lambda i: (0, 0))
    return _call(
        body,
        name="in_bwd",
        grid=(s // tm,),
        in_specs=[tile(D_MODEL), tile(D_MODEL), tile(D_MAIN), tile(2 * D_KV), row, pl.BlockSpec((D_MODEL, D_IN), lambda i: (0, 0))],
        out_specs=[tile(D_MODEL), row],
        out_shape=[jax.ShapeDtypeStruct((s, D_MODEL), F32), jax.ShapeDtypeStruct((1, D_MODEL), F32)],
        compiler_params=_params(("arbitrary",)),
    )(x, dh1, dzm, dkv, g1, w_in)


def _wgrad(a, b, tm, tn, name, peer_cols=False):
    s, m = a.shape
    n = b.shape[1]

    def body(a_ref, b_ref, o_ref, at_ref):
        @pl.when(pl.program_id(1) == 0)
        def _():
            at_ref[...] = a_ref[...].T

        r = _nn(at_ref[...], b_ref[...])
        if peer_cols:
            o_ref[0] = r
        else:
            o_ref[...] = r

    if peer_cols:
        out_spec = pl.BlockSpec((1, tm, tn), lambda i, j: (j, i, 0))
        out_shape = jax.ShapeDtypeStruct((n // tn, m, tn), F32)
    else:
        out_spec = pl.BlockSpec((tm, tn), lambda i, j: (i, j))
        out_shape = jax.ShapeDtypeStruct((m, n), F32)
    return _call(
        body,
        name=name,
        grid=(m // tm, n // tn),
        in_specs=[pl.BlockSpec((s, tm), lambda i, j: (0, i)), pl.BlockSpec((s, tn), lambda i, j: (0, j))],
        out_specs=out_spec,
        out_shape=out_shape,
        scratch_shapes=[pltpu.VMEM((tm, s), BF)],
        compiler_params=_params(("arbitrary", "arbitrary")),
    )(a, b)


def _adamw_math(w, g, m, v):
    m_new = ADAM_B1 * m + (1.0 - ADAM_B1) * g
    v_new = ADAM_B2 * v + (1.0 - ADAM_B2) * jnp.square(g)
    m_hat = m_new / (1.0 - ADAM_B1 ** ADAM_STEP)
    v_hat = v_new / (1.0 - ADAM_B2 ** ADAM_STEP)
    delta = -ADAM_LR * (m_hat / (jnp.sqrt(v_hat) + ADAM_EPS) + ADAM_WD * w)
    return delta, m_new, v_new


def _adamw(w, g, m, v, name):
    r, c = w.shape
    tr = min(r, 256)

    def body(w_ref, g_ref, m_ref, v_ref, d_ref, mo_ref, vo_ref):
        d_ref[...], mo_ref[...], vo_ref[...] = _adamw_math(w_ref[...], g_ref[...], m_ref[...], v_ref[...])

    spec = pl.BlockSpec((tr, c), lambda i: (i, 0))
    return _call(
        body,
        name=name,
        grid=(r // tr,),
        in_specs=[spec] * 4,
        out_specs=[spec] * 3,
        out_shape=[jax.ShapeDtypeStruct((r, c), F32)] * 3,
        compiler_params=_params(("arbitrary",)),
    )(w, g, m, v)


def _adamw_small(parts, w, m, v):
    r, c = w.shape

    def body(p_ref, w_ref, m_ref, v_ref, g_ref, d_ref, mo_ref, vo_ref):
        g = p_ref[0]
        for d in range(1, N_DEV):
            g = g + p_ref[d]
        g_ref[...] = g
        d_ref[...], mo_ref[...], vo_ref[...] = _adamw_math(w_ref[...], g, m_ref[...], v_ref[...])

    return _call(
        body,
        name="adamw_small",
        out_shape=[jax.ShapeDtypeStruct((r, c), F32)] * 4,
        compiler_params=_params(),
    )(parts, w, m, v)


def _place():
    x, y, c = lax.axis_index("x"), lax.axis_index("y"), lax.axis_index("c")
    return x, y, c, [(1 - x, y), (x, 1 - y), (1 - x, 1 - y)]


def _dev_index(px, py, pc):
    return 4 * px + 2 * py + pc


def _all_gather(shards, out_dtype, name):
    k_n = len(shards)

    def body(*refs):
        ins, outs, stage = refs[:k_n], refs[k_n:2 * k_n], refs[2 * k_n:3 * k_n]
        send_sems, recv_sems, local_sems = refs[3 * k_n:]
        x, y, c, chips = _place()
        me, sibling = (x, y, c), (x, y, 1 - c)

        def copy(k, j, block, to, src=None):
            dst = outs[k].at[_dev_index(*block)]
            return pltpu.make_async_remote_copy(
                src_ref=dst if src is None else src, dst_ref=dst,
                send_sem=send_sems.at[k, j], recv_sem=recv_sems.at[k, j], device_id=to, device_id_type=MESH)

        mine, first = [], []
        for k in range(k_n):
            stage[k][...] = ins[k][...].astype(out_dtype)
            mine.append(pltpu.make_async_copy(stage[k], outs[k].at[_dev_index(*me)], local_sems.at[k]))
            mine[k].start()
            first.append(copy(k, 0, me, sibling, src=stage[k]))
            first += [copy(k, 1 + j, me, (*chip, c), src=stage[k]) for j, chip in enumerate(chips)]
        for cp in first:
            cp.start()
        passed = []
        for k in range(k_n):
            for j, chip in enumerate(chips):
                copy(k, 1 + j, (*chip, c), me).wait_recv()
                passed.append(copy(k, 4 + j, (*chip, c), sibling))
                passed[-1].start()
        for k in range(k_n):
            copy(k, 0, sibling, me).wait_recv()
            for j, chip in enumerate(chips):
                copy(k, 4 + j, (*chip, 1 - c), me).wait_recv()
        for cp in first + passed:
            cp.wait_send()
        for cp in mine:
            cp.wait()

    return _call(
        body,
        name=name,
        in_specs=[pl.BlockSpec(memory_space=pltpu.VMEM)] * k_n,
        out_specs=[pl.BlockSpec(memory_space=pl.ANY)] * k_n,
        out_shape=[jax.ShapeDtypeStruct((N_DEV,) + sh.shape, out_dtype) for sh in shards],
        scratch_shapes=[pltpu.VMEM(sh.shape, out_dtype) for sh in shards]
        + [pltpu.SemaphoreType.DMA((k_n, 7)), pltpu.SemaphoreType.DMA((k_n, 7)), pltpu.SemaphoreType.DMA((k_n,))],
        compiler_params=_params(),
    )(*shards)


def _reduce_scatter(grads):
    k_n = len(grads)
    shapes = [g.shape[1:] for g in grads]

    def body(*refs):
        gs, outs, land1 = refs[:k_n], refs[k_n:2 * k_n], refs[2 * k_n:3 * k_n]
        send2, recv2 = refs[3 * k_n:4 * k_n], refs[4 * k_n:5 * k_n]
        s1_send, s1_recv, s2_send, s2_recv = refs[5 * k_n:]
        x, y, c, others = _place()
        sibling = (x, y, 1 - c)
        chips = [(x, y)] + others

        def copy1(k, j):
            return pltpu.make_async_remote_copy(
                src_ref=gs[k].at[_dev_index(*chips[j], 1 - c)], dst_ref=land1[k].at[j],
                send_sem=s1_send.at[k, j], recv_sem=s1_recv.at[k, j], device_id=sibling, device_id_type=MESH)

        def copy2(k, j):
            return pltpu.make_async_remote_copy(
                src_ref=send2[k].at[j - 1], dst_ref=recv2[k].at[j - 1],
                send_sem=s2_send.at[k, j - 1], recv_sem=s2_recv.at[k, j - 1], device_id=(*chips[j], c), device_id_type=MESH)

        for k in range(k_n):
            for j in range(4):
                copy1(k, j).start()

        def chip_sum(k, j, a_ref, b_ref, sems):
            ca = pltpu.make_async_copy(gs[k].at[_dev_index(*chips[j], c)], a_ref, sems.at[0])
            cb = pltpu.make_async_copy(land1[k].at[j], b_ref, sems.at[1])
            ca.start()
            cb.start()
            ca.wait()
            cb.wait()
            return a_ref[...] + b_ref[...]

        for k in range(k_n):
            def stage(a_ref, b_ref, sems, k=k):
                for j in (1, 2, 3, 0):
                    copy1(k, j).wait_recv()
                    total = chip_sum(k, j, a_ref, b_ref, sems)
                    if j == 0:
                        outs[k][...] = total
                    else:
                        send2[k][j - 1] = total.astype(BF)
                        copy2(k, j).start()

            pl.run_scoped(stage, pltpu.VMEM(shapes[k], F32), pltpu.VMEM(shapes[k], F32), pltpu.SemaphoreType.DMA((2,)))

        for k in range(k_n):
            for j in (1, 2, 3):
                copy2(k, j).wait_recv()
                outs[k][...] += recv2[k][j - 1].astype(F32)
        for k in range(k_n):
            for j in range(4):
                copy1(k, j).wait_send()
            for j in (1, 2, 3):
                copy2(k, j).wait_send()

    res = _call(
        body,
        name="reduce_scatter",
        in_specs=[pl.BlockSpec(memory_space=pl.ANY)] * k_n,
        out_specs=[pl.BlockSpec(memory_space=pltpu.VMEM)] * k_n + [pl.BlockSpec(memory_space=pl.ANY)] * k_n,
        out_shape=[jax.ShapeDtypeStruct(sh, F32) for sh in shapes] + [jax.ShapeDtypeStruct((4,) + sh, F32) for sh in shapes],
        scratch_shapes=[pltpu.VMEM((3,) + sh, BF) for sh in shapes] * 2
        + [pltpu.SemaphoreType.DMA((k_n, 4)), pltpu.SemaphoreType.DMA((k_n, 4)),
           pltpu.SemaphoreType.DMA((k_n, 3)), pltpu.SemaphoreType.DMA((k_n, 3))],
        compiler_params=_params(),
    )(*grads)
    return res[:k_n]


SMALL = ("norm1_gain", "gmlp_v_gain", "w_spatial", "b_spatial", "attn_sinks", "rel_bias_table", "norm2_gain", "final_gain")
LANES = 128


def _pack_small(arrays):
    rows = []
    for a in arrays:
        flat = a.reshape(-1)
        pad = (-flat.shape[0]) % LANES
        rows.append(jnp.pad(flat, (0, pad)).reshape(-1, LANES))
    packed = jnp.concatenate(rows, axis=0)
    return jnp.pad(packed, ((0, (-packed.shape[0]) % 8), (0, 0)))


def _unpack_small(packed, like):
    out, r = [], 0
    for a in like:
        n = a.size
        nr = -(-n // LANES)
        out.append(packed[r:r + nr].reshape(-1)[:n].reshape(a.shape))
        r += nr
    return out


def _local_step(x, p, t, small, w_in, w_out, w_ff1, w_ff2, w_gate, w_proj):
    g1, gv, w_sp, b_sp, sinks, table, g2, gf = (small[n] for n in SMALL)
    bucket = jnp.asarray(_bucket_table())
    b_col = b_sp.reshape(GROUPS, BLOCK, 1)
    gf_row = gf.reshape(1, D_MODEL)

    zuv, qkv, hn1 = _in_proj(x, g1, w_in)
    mix = _mixer_fwd(zuv, qkv, gv, w_sp[0], b_col, sinks, table, bucket)
    (loss, d_g2, d_gf, dh1, dh1b, dmix, hn2, a, df, h2, dh2, dgl, dpp) = _tail(x, mix, p, t, g2, gf_row, w_out, w_ff1, w_ff2, w_gate, w_proj)
    dzm, dkv, d_gv, d_wsp, d_bsp, d_sinks, d_table = _mixer_bwd(zuv, qkv, dmix, gv, w_sp[0], b_col, sinks, table, bucket)
    dx, d_g1 = _in_bwd(x, dh1, dzm, dkv, g1, w_in)

    gw_in = jnp.concatenate([_wgrad(hn1, dzm, 512, 512, "wgrad_in_main"), _wgrad(hn1, dkv, 512, 256, "wgrad_in_kv")], axis=1)
    gw_in = gw_in.reshape(D_MODEL, N_DEV, D_IN // N_DEV).transpose(1, 0, 2)
    gw_out = _wgrad(mix, dh1b, 512, 1024, "wgrad_out").reshape(N_DEV, D_MODEL // N_DEV, D_MODEL)
    gw_ff1 = _wgrad(hn2, df, 512, D_FF // N_DEV, "wgrad_ff1", peer_cols=True)
    gw_ff2 = _wgrad(a, dh2, 512, 1024, "wgrad_ff2").reshape(N_DEV, D_FF // N_DEV, D_MODEL)
    gw_proj = _wgrad(p.astype(BF), dpp, PLE_DIM, D_MODEL // N_DEV, "wgrad_proj", peer_cols=True)
    gw_gate = _wgrad(h2, dgl, 512, 1024, "wgrad_gate").reshape(N_DEV, D_MODEL // N_DEV, D_MODEL)

    small_grads = dict(
        norm1_gain=d_g1, gmlp_v_gain=d_gv, w_spatial=d_wsp.reshape(w_sp.shape), b_spatial=d_bsp.reshape(b_sp.shape),
        attn_sinks=d_sinks, rel_bias_table=d_table, norm2_gain=d_g2, final_gain=d_gf.reshape(gf.shape))
    return loss, dx, small_grads, [gw_in, gw_out, gw_ff1, gw_ff2, gw_proj, gw_gate]


def kernel(x, p, norm1_gain, w_in, gmlp_v_gain, w_spatial, b_spatial, attn_sinks, rel_bias_table, w_out, norm2_gain, w_ff1, w_ff2, w_ple_proj, w_ple_gate, final_gain, loss_target, m_norm1_gain, m_w_in, m_gmlp_v_gain, m_w_spatial, m_b_spatial, m_attn_sinks, m_rel_bias_table, m_w_out, m_norm2_gain, m_w_ff1, m_w_ff2, m_w_ple_proj, m_w_ple_gate, m_final_gain, v_norm1_gain, v_w_in, v_gmlp_v_gain, v_w_spatial, v_b_spatial, v_attn_sinks, v_rel_bias_table, v_w_out, v_norm2_gain, v_w_ff1, v_w_ff2, v_w_ple_proj, v_w_ple_gate, v_final_gain):
    args = dict(locals())
    s = x.shape[1]
    big = ("w_in", "w_out", "w_ff1", "w_ff2", "w_ple_proj", "w_ple_gate")

    gathered = _all_gather([args[n][0] for n in big], BF, "gather_weights")
    g_in, g_out, g_ff1, g_ff2, g_proj, g_gate = gathered
    full_in = g_in.transpose(1, 0, 2).reshape(D_MODEL, D_IN)
    full_proj = g_proj.transpose(1, 0, 2).reshape(PLE_DIM, D_MODEL)
    small = {n: args[n] for n in SMALL}
    loss, dx, small_grads, big_grads = _local_step(
        x.reshape(s, D_MODEL), p.reshape(s, PLE_DIM), loss_target.reshape(s, D_MODEL), small,
        full_in, g_out.reshape(D_MODEL, D_MODEL), g_ff1, g_ff2.reshape(D_FF, D_MODEL),
        g_gate.reshape(D_MODEL, D_MODEL), full_proj)

    loss = lax.psum(loss[0, 0], ("x", "y", "c"))
    grad_x = dx.reshape(x.shape)

    big_g = _reduce_scatter(big_grads)
    grads, deltas, new_m, new_v = {}, {}, {}, {}
    for n, g in zip(big, big_g):
        d, mo, vo = _adamw(args[n][0], g, args["m_" + n][0], args["v_" + n][0], "adamw_" + n)
        grads[n], deltas[n], new_m[n], new_v[n] = g[None], d[None], mo[None], vo[None]

    like = [args[n] for n in SMALL]
    parts = _all_gather([_pack_small([small_grads[n] for n in SMALL])], F32, "gather_small_grads")[0]
    packed = _adamw_small(parts, _pack_small(like), _pack_small([args["m_" + n] for n in SMALL]),
                          _pack_small([args["v_" + n] for n in SMALL]))
    for dst, pk in zip((grads, deltas, new_m, new_v), packed):
        for n, a in zip(SMALL, _unpack_small(pk, like)):
            dst[n] = a

    order = ("norm1_gain", "w_in", "gmlp_v_gain", "w_spatial", "b_spatial", "attn_sinks", "rel_bias_table", "w_out",
             "norm2_gain", "w_ff1", "w_ff2", "w_ple_proj", "w_ple_gate", "final_gain")
    return (loss, grad_x, *[grads[n] for n in order], *[deltas[n] for n in order],
            *[new_m[n] for n in order], *[new_v[n] for n in order])
```

```python
import functools
import math

import numpy as np
import jax
import jax.numpy as jnp
from jax import lax
from jax.experimental import pallas as pl
from jax.experimental.pallas import tpu as pltpu

F32 = jnp.float32
BF = jnp.bfloat16
MESH = pl.DeviceIdType.MESH
N_DEV = 8

D_MODEL = 1024
PLE_DIM = 256
D_GMLP = 512
GROUPS = 4
GDIM = 128
BLOCK = 128
D_ATTN = 512
HEAD_DIM = 64
N_Q = 8
Q_PER_KV = 4
D_KV = 128
D_FF = 4096
D_IN = 1792
D_MAIN = 2 * D_GMLP + D_ATTN
REL_BUCKETS = 32
EPS = 1e-6
NEG_INF = -1e30
SCALE = HEAD_DIM ** -0.5
GELU_C = math.sqrt(2.0 / math.pi)
GELU_A = 0.044715

ADAM_LR = 0.001
ADAM_B1 = 0.9
ADAM_B2 = 0.999
ADAM_EPS = 1e-08
ADAM_WD = 0.01
ADAM_STEP = 10

V7X_VMEM_LIMIT = 60000 * 1024
TOK_TILE = 256


def _call(body, **kw):
    return pl.pallas_call(body, **kw)


def _params(sem=None):
    if sem is None:
        return pltpu.CompilerParams(vmem_limit_bytes=V7X_VMEM_LIMIT)
    return pltpu.CompilerParams(dimension_semantics=sem, vmem_limit_bytes=V7X_VMEM_LIMIT)


def _nn(a, b):
    return jnp.dot(a, b, preferred_element_type=F32)


def _nt(a, b):
    return lax.dot_general(a, b, (((1,), (1,)), ((), ())), preferred_element_type=F32)


def _tn(a, b):
    return lax.dot_general(a, b, (((0,), (0,)), ((), ())), preferred_element_type=F32)


def _gelu_tanh(x):
    return jnp.tanh(GELU_C * (x + GELU_A * (x * x * x)))


def _gelu(x):
    return x * (0.5 * (1.0 + _gelu_tanh(x)))


def _gelu_grad(x):
    t = _gelu_tanh(x)
    return 0.5 * (1.0 + t) + 0.5 * x * (1.0 - t * t) * (GELU_C * (1.0 + 3.0 * GELU_A * (x * x)))


def _rms_scale(x):
    return lax.rsqrt(jnp.mean(x * x, axis=-1, keepdims=True) + EPS)


def _rms_bwd(dxn, x, r):
    return r * dxn - x * ((r * r * r) * jnp.mean(dxn * x, axis=-1, keepdims=True))


def _bucket_table():
    a = np.arange(BLOCK)[:, None]
    j = np.arange(2 * BLOCK)[None, :]
    n = BLOCK + a - j
    valid = (n >= 0) & (n < BLOCK)
    nc = np.maximum(n, 0)
    max_exact = REL_BUCKETS // 2
    nf = np.maximum(nc, 1).astype(np.float32)
    large = max_exact + (
        np.log(nf / np.float32(max_exact)) / np.float32(math.log(BLOCK / max_exact)) * np.float32(REL_BUCKETS - max_exact)
    ).astype(np.int32)
    large = np.minimum(large, REL_BUCKETS - 1)
    bucket = np.where(nc < max_exact, nc, large)
    return np.where(valid, bucket, -1).astype(np.int32)


def _in_proj(x, g1, w_in):
    s = x.shape[0]
    tm = min(TOK_TILE, s)

    def body(x_ref, g_ref, w_ref, zuv_ref, qkv_ref, hn_ref):
        xv = x_ref[...]
        hn = ((xv * _rms_scale(xv)) * g_ref[...]).astype(BF)
        hn_ref[...] = hn
        z = _nn(hn, w_ref[...])
        zuv_ref[...] = z[:, : 2 * D_GMLP]
        qkv_ref[...] = z[:, 2 * D_GMLP:].astype(BF)

    return _call(
        body,
        name="in_proj",
        grid=(s // tm,),
        in_specs=[
            pl.BlockSpec((tm, D_MODEL), lambda i: (i, 0)),
            pl.BlockSpec((1, D_MODEL), lambda i: (0, 0)),
            pl.BlockSpec((D_MODEL, D_IN), lambda i: (0, 0)),
        ],
        out_specs=[
            pl.BlockSpec((tm, 2 * D_GMLP), lambda i: (i, 0)),
            pl.BlockSpec((tm, D_ATTN + 2 * D_KV), lambda i: (i, 0)),
            pl.BlockSpec((tm, D_MODEL), lambda i: (i, 0)),
        ],
        out_shape=[
            jax.ShapeDtypeStruct((s, 2 * D_GMLP), F32),
            jax.ShapeDtypeStruct((s, D_ATTN + 2 * D_KV), BF),
            jax.ShapeDtypeStruct((s, D_MODEL), BF),
        ],
        compiler_params=_params(("arbitrary",)),
    )(x, g1, w_in)


def _build_bias(bias_ref, bucket_ref, table_ref):
    bucket = bucket_ref[...]
    for h in range(N_Q):
        acc = jnp.zeros((BLOCK, 2 * BLOCK), F32)
        for b in range(REL_BUCKETS):
            acc = jnp.where(bucket == b, table_ref[b, h], acc)
        bias_ref[h] = acc


def _window_masks(i):
    row = lax.broadcasted_iota(jnp.int32, (BLOCK, BLOCK), 0)
    col = lax.broadcasted_iota(jnp.int32, (BLOCK, BLOCK), 1)
    return (col > row) & (i > 0), col <= row


def _tril_bf16(w_ref, g):
    row = lax.broadcasted_iota(jnp.int32, (BLOCK, BLOCK), 0)
    col = lax.broadcasted_iota(jnp.int32, (BLOCK, BLOCK), 1)
    return jnp.where(col <= row, w_ref[g], 0.0).astype(BF)


def _attn_probs(q_h, k_prev, k_cur, bias_h, sink, valid_prev, valid_cur):
    l_prev = jnp.where(valid_prev, _nt(q_h, k_prev) * SCALE + bias_h[:, :BLOCK], NEG_INF)
    l_cur = jnp.where(valid_cur, _nt(q_h, k_cur) * SCALE + bias_h[:, BLOCK:], NEG_INF)
    m = jnp.maximum(jnp.maximum(jnp.max(l_prev, axis=-1, keepdims=True), jnp.max(l_cur, axis=-1, keepdims=True)), sink)
    e_prev = jnp.exp(l_prev - m)
    e_cur = jnp.exp(l_cur - m)
    e_sink = jnp.exp(sink - m)
    denom = jnp.sum(e_prev, axis=-1, keepdims=True) + jnp.sum(e_cur, axis=-1, keepdims=True) + e_sink
    return e_prev / denom, e_cur / denom, e_sink / denom


def _mixer_specs(nb):
    cl = lambda i: jnp.minimum(i, nb - 1)
    return [
        pl.BlockSpec((BLOCK, 2 * D_GMLP), lambda i: (cl(i), 0)),
        pl.BlockSpec((BLOCK, D_ATTN), lambda i: (cl(i), 0)),
        pl.BlockSpec((BLOCK, 2 * D_KV), lambda i: (cl(i), D_ATTN // (2 * D_KV))),
        pl.BlockSpec((BLOCK, 2 * D_KV), lambda i: (jnp.maximum(cl(i) - 1, 0), D_ATTN // (2 * D_KV))),
        pl.BlockSpec((1, D_GMLP), lambda i: (0, 0)),
        pl.BlockSpec((GROUPS, BLOCK, BLOCK), lambda i: (0, 0, 0)),
        pl.BlockSpec((GROUPS, BLOCK, 1), lambda i: (0, 0, 0)),
        pl.BlockSpec(memory_space=pltpu.SMEM),
        pl.BlockSpec(memory_space=pltpu.SMEM),
        pl.BlockSpec((BLOCK, 2 * BLOCK), lambda i: (0, 0)),
    ]


def _mixer_fwd(zuv, qkv, gv, w_sp, b_sp, sinks, table, bucket):
    s = zuv.shape[0]
    nb = s // BLOCK

    def body(zuv_ref, q_ref, kvc_ref, kvp_ref, gv_ref, w_ref, b_ref, sink_ref, table_ref, bucket_ref, mix_ref, bias_ref):
        i = pl.program_id(0)

        @pl.when(i == 0)
        def _():
            _build_bias(bias_ref, bucket_ref, table_ref)

        for g in range(GROUPS):
            sl = slice(g * GDIM, (g + 1) * GDIM)
            u = _gelu(zuv_ref[:, sl])
            vg = _gelu(zuv_ref[:, D_GMLP + g * GDIM: D_GMLP + (g + 1) * GDIM])
            vn = ((vg * _rms_scale(vg)) * gv_ref[:, sl]).astype(BF)
            sv = _nn(_tril_bf16(w_ref, g), vn) + b_ref[g]
            mix_ref[:, sl] = (u * sv).astype(BF)

        valid_prev, valid_cur = _window_masks(i)
        for h in range(N_Q):
            kh = h // Q_PER_KV
            q_h = q_ref[:, h * HEAD_DIM:(h + 1) * HEAD_DIM]
            ksl = slice(kh * HEAD_DIM, (kh + 1) * HEAD_DIM)
            vsl = slice(D_KV + kh * HEAD_DIM, D_KV + (kh + 1) * HEAD_DIM)
            p_prev, p_cur, _ = _attn_probs(q_h, kvp_ref[:, ksl], kvc_ref[:, ksl], bias_ref[h], sink_ref[0, h], valid_prev, valid_cur)
            o = _nn(p_prev.astype(BF), kvp_ref[:, vsl]) + _nn(p_cur.astype(BF), kvc_ref[:, vsl])
            mix_ref[:, D_GMLP + h * HEAD_DIM: D_GMLP + (h + 1) * HEAD_DIM] = o.astype(BF)

    return _call(
        body,
        name="mixer_fwd",
        grid=(nb,),
        in_specs=_mixer_specs(nb),
        out_specs=pl.BlockSpec((BLOCK, D_MODEL), lambda i: (i, 0)),
        out_shape=jax.ShapeDtypeStruct((s, D_MODEL), BF),
        scratch_shapes=[pltpu.VMEM((N_Q, BLOCK, 2 * BLOCK), F32)],
        compiler_params=_params(("arbitrary",)),
    )(zuv, qkv, qkv, qkv, gv, w_sp, b_sp, sinks, table, bucket)


def _tail(x, mix, p, t, g2, gf, w_out, w_ff1, w_ff2, w_gate, w_proj):
    s = x.shape[0]
    tm = min(TOK_TILE, s)
    n_ff = w_ff1.shape[0]
    fc = D_FF // n_ff

    def body(x_ref, mix_ref, p_ref, t_ref, g2_ref, gf_ref, wo_ref, w1_ref, w2_ref, wg_ref, wp_ref,
             loss_ref, dg2_ref, dgf_ref, dh1_ref, dh1b_ref, dmix_ref, hn2_ref, a_ref, df_ref, h2_ref, dh2_ref, dgl_ref, dpp_ref, f_ref):
        i = pl.program_id(0)

        @pl.when(i == 0)
        def _():
            loss_ref[...] = jnp.zeros_like(loss_ref)
            dg2_ref[...] = jnp.zeros_like(dg2_ref)
            dgf_ref[...] = jnp.zeros_like(dgf_ref)

        h1 = x_ref[...] + _nn(mix_ref[...], wo_ref[...])
        r2 = _rms_scale(h1)
        hn2 = ((h1 * r2) * g2_ref[...]).astype(BF)
        hn2_ref[...] = hn2
        h2 = h1
        for c in range(n_ff):
            f = _nn(hn2, w1_ref[c])
            f_ref[:, c * fc:(c + 1) * fc] = f
            a = jnp.square(jnp.maximum(f, 0.0)).astype(BF)
            a_ref[:, c * fc:(c + 1) * fc] = a
            h2 = h2 + _nn(a, w2_ref[c * fc:(c + 1) * fc, :])
        h2b = h2.astype(BF)
        h2_ref[...] = h2b
        gate = jax.nn.sigmoid(_nn(h2b, wg_ref[...]))
        pp = _nn(p_ref[...].astype(BF), wp_ref[...])
        h3 = h2 + gate * pp
        rf = _rms_scale(h3)
        gf_v = gf_ref[...]
        err = (h3 * rf) * gf_v - t_ref[...]
        loss_ref[...] += jnp.sum(jnp.sum(err * err, axis=-1, keepdims=True), axis=0, keepdims=True) * (0.5 / D_MODEL)

        dy = err * (1.0 / D_MODEL)
        dgf_ref[...] += jnp.sum(dy * (h3 * rf), axis=0, keepdims=True)
        dh3 = _rms_bwd(dy * gf_v, h3, rf)
        dpp_ref[...] = (dh3 * gate).astype(BF)
        dgl = ((dh3 * pp) * (gate * (1.0 - gate))).astype(BF)
        dgl_ref[...] = dgl
        dh2 = dh3 + _nt(dgl, wg_ref[...])
        dh2b = dh2.astype(BF)
        dh2_ref[...] = dh2b
        dhn2 = jnp.zeros((tm, D_MODEL), F32)
        for c in range(n_ff):
            da = _nt(dh2b, w2_ref[c * fc:(c + 1) * fc, :])
            df = (da * (2.0 * jnp.maximum(f_ref[:, c * fc:(c + 1) * fc], 0.0))).astype(BF)
            df_ref[:, c * fc:(c + 1) * fc] = df
            dhn2 = dhn2 + _nt(df, w1_ref[c])
        dg2_ref[...] += jnp.sum(dhn2 * (h1 * r2), axis=0, keepdims=True)
        dh1 = dh2 + _rms_bwd(dhn2 * g2_ref[...], h1, r2)
        dh1_ref[...] = dh1
        dh1b = dh1.astype(BF)
        dh1b_ref[...] = dh1b
        dmix_ref[...] = _nt(dh1b, wo_ref[...])

    tile = lambda cols: pl.BlockSpec((tm, cols), lambda i: (i, 0))
    whole = lambda shape: pl.BlockSpec(shape, lambda i: (0,) * len(shape), pipeline_mode=pl.Buffered(1))
    row = pl.BlockSpec((1, D_MODEL), lambda i: (0, 0))
    act = lambda cols, dt: jax.ShapeDtypeStruct((s, cols), dt)
    return _call(
        body,
        name="tail",
        grid=(s // tm,),
        in_specs=[tile(D_MODEL), tile(D_MODEL), tile(PLE_DIM), tile(D_MODEL), row, row,
                  whole(w_out.shape), whole(w_ff1.shape), whole(w_ff2.shape), whole(w_gate.shape), whole(w_proj.shape)],
        out_specs=[pl.BlockSpec((1, 1), lambda i: (0, 0)), row, row, tile(D_MODEL), tile(D_MODEL), tile(D_MODEL), tile(D_MODEL), tile(D_FF),
                   tile(D_FF), tile(D_MODEL), tile(D_MODEL), tile(D_MODEL), tile(D_MODEL)],
        out_shape=[jax.ShapeDtypeStruct((1, 1), F32), jax.ShapeDtypeStruct((1, D_MODEL), F32), jax.ShapeDtypeStruct((1, D_MODEL), F32),
                   act(D_MODEL, F32), act(D_MODEL, BF), act(D_MODEL, F32), act(D_MODEL, BF), act(D_FF, BF), act(D_FF, BF), act(D_MODEL, BF),
                   act(D_MODEL, BF), act(D_MODEL, BF), act(D_MODEL, BF)],
        scratch_shapes=[pltpu.VMEM((tm, D_FF), F32)],
        compiler_params=_params(("arbitrary",)),
    )(x, mix, p, t, g2, gf, w_out, w_ff1, w_ff2, w_gate, w_proj)


def _mixer_bwd(zuv, qkv, dmix, gv, w_sp, b_sp, sinks, table, bucket):
    s = zuv.shape[0]
    nb = s // BLOCK

    def body(zuv_ref, q_ref, kvc_ref, kvp_ref, gv_ref, w_ref, b_ref, sink_ref, table_ref, bucket_ref, dmix_ref,
             dzm_ref, dkv_ref, dgv_ref, dw_ref, db_ref, dsink_ref, dtable_ref,
             bias_ref, dbias_ref, carry_ref, dsink_acc):
        i = pl.program_id(0)

        @pl.when(i == 0)
        def _():
            _build_bias(bias_ref, bucket_ref, table_ref)
            dbias_ref[...] = jnp.zeros_like(dbias_ref)
            carry_ref[...] = jnp.zeros_like(carry_ref)
            dsink_acc[...] = jnp.zeros_like(dsink_acc)
            dgv_ref[...] = jnp.zeros_like(dgv_ref)
            dw_ref[...] = jnp.zeros_like(dw_ref)
            db_ref[...] = jnp.zeros_like(db_ref)

        @pl.when(i < nb)
        def _():
            for g in range(GROUPS):
                sl = slice(g * GDIM, (g + 1) * GDIM)
                slv = slice(D_GMLP + g * GDIM, D_GMLP + (g + 1) * GDIM)
                zu = zuv_ref[:, sl]
                zv = zuv_ref[:, slv]
                u = _gelu(zu)
                vg = _gelu(zv)
                rg = _rms_scale(vg)
                vhat = vg * rg
                gain = gv_ref[:, sl]
                vn = (vhat * gain).astype(BF)
                w_g = _tril_bf16(w_ref, g)
                sv = _nn(w_g, vn) + b_ref[g]
                dmix_a = dmix_ref[:, sl]
                dsv = dmix_a * u
                dsvb = dsv.astype(BF)
                db_ref[g] += jnp.sum(dsv, axis=-1, keepdims=True)
                dw_ref[g] += _nt(dsvb, vn)
                dvn = _tn(w_g, dsvb)
                dgv_ref[:, sl] += jnp.sum(dvn * vhat, axis=0, keepdims=True)
                dvg = _rms_bwd(dvn * gain, vg, rg)
                dzm_ref[:, sl] = ((dmix_a * sv) * _gelu_grad(zu)).astype(BF)
                dzm_ref[:, slv] = (dvg * _gelu_grad(zv)).astype(BF)

            valid_prev, valid_cur = _window_masks(i)
            for kh in range(N_Q // Q_PER_KV):
                ksl = slice(kh * HEAD_DIM, (kh + 1) * HEAD_DIM)
                vsl = slice(D_KV + kh * HEAD_DIM, D_KV + (kh + 1) * HEAD_DIM)
                k_prev, k_cur = kvp_ref[:, ksl], kvc_ref[:, ksl]
                v_prev, v_cur = kvp_ref[:, vsl], kvc_ref[:, vsl]
                dk_prev = jnp.zeros((BLOCK, HEAD_DIM), F32)
                dk_cur = jnp.zeros((BLOCK, HEAD_DIM), F32)
                dv_prev = jnp.zeros((BLOCK, HEAD_DIM), F32)
                dv_cur = jnp.zeros((BLOCK, HEAD_DIM), F32)
                for gq in range(Q_PER_KV):
                    h = kh * Q_PER_KV + gq
                    q_h = q_ref[:, h * HEAD_DIM:(h + 1) * HEAD_DIM]
                    p_prev, p_cur, p_sink = _attn_probs(q_h, k_prev, k_cur, bias_ref[h], sink_ref[0, h], valid_prev, valid_cur)
                    do = dmix_ref[:, D_GMLP + h * HEAD_DIM: D_GMLP + (h + 1) * HEAD_DIM].astype(BF)
                    dp_prev = _nt(do, v_prev)
                    dp_cur = _nt(do, v_cur)
                    delta = jnp.sum(p_prev * dp_prev, axis=-1, keepdims=True) + jnp.sum(p_cur * dp_cur, axis=-1, keepdims=True)
                    ds_prev = p_prev * (dp_prev - delta)
                    ds_cur = p_cur * (dp_cur - delta)
                    dsink_acc[h] -= p_sink * delta
                    dbias_ref[h, :, :BLOCK] += ds_prev
                    dbias_ref[h, :, BLOCK:] += ds_cur
                    dsb_prev = ds_prev.astype(BF)
                    dsb_cur = ds_cur.astype(BF)
                    dq = (_nn(dsb_prev, k_prev) + _nn(dsb_cur, k_cur)) * SCALE
                    dzm_ref[:, 2 * D_GMLP + h * HEAD_DIM: 2 * D_GMLP + (h + 1) * HEAD_DIM] = dq.astype(BF)
                    dk_prev = dk_prev + _tn(dsb_prev, q_h)
                    dk_cur = dk_cur + _tn(dsb_cur, q_h)
                    dv_prev = dv_prev + _tn(p_prev.astype(BF), do)
                    dv_cur = dv_cur + _tn(p_cur.astype(BF), do)
                dkv_ref[:, ksl] = (carry_ref[:, ksl] + dk_prev * SCALE).astype(BF)
                dkv_ref[:, vsl] = (carry_ref[:, vsl] + dv_prev).astype(BF)
                carry_ref[:, ksl] = dk_cur * SCALE
                carry_ref[:, vsl] = dv_cur

        @pl.when(i == nb)
        def _():
            dkv_ref[...] = carry_ref[...].astype(BF)
            row = lax.broadcasted_iota(jnp.int32, (BLOCK, BLOCK), 0)
            col = lax.broadcasted_iota(jnp.int32, (BLOCK, BLOCK), 1)
            for g in range(GROUPS):
                dw_ref[g] = jnp.where(col <= row, dw_ref[g], 0.0)
            bucket = bucket_ref[...]
            for h in range(N_Q):
                dsink_ref[0, h] = jnp.sum(dsink_acc[h])
                dbh = dbias_ref[h]
                for b in range(REL_BUCKETS):
                    dtable_ref[b, h] = jnp.sum(jnp.where(bucket == b, dbh, 0.0))

    cl = lambda i: jnp.minimum(i, nb - 1)
    const = lambda shape: pl.BlockSpec(shape, lambda i: (0,) * len(shape))
    return _call(
        body,
        name="mixer_bwd",
        grid=(nb + 1,),
        in_specs=_mixer_specs(nb) + [pl.BlockSpec((BLOCK, D_MODEL), lambda i: (cl(i), 0))],
        out_specs=[
            pl.BlockSpec((BLOCK, D_MAIN), lambda i: (cl(i), 0)),
            pl.BlockSpec((BLOCK, 2 * D_KV), lambda i: (jnp.maximum(i - 1, 0), 0)),
            const((1, D_GMLP)),
            const((GROUPS, BLOCK, BLOCK)),
            const((GROUPS, BLOCK, 1)),
            pl.BlockSpec(memory_space=pltpu.SMEM),
            pl.BlockSpec(memory_space=pltpu.SMEM),
        ],
        out_shape=[
            jax.ShapeDtypeStruct((s, D_MAIN), BF),
            jax.ShapeDtypeStruct((s, 2 * D_KV), BF),
            jax.ShapeDtypeStruct((1, D_GMLP), F32),
            jax.ShapeDtypeStruct((GROUPS, BLOCK, BLOCK), F32),
            jax.ShapeDtypeStruct((GROUPS, BLOCK, 1), F32),
            jax.ShapeDtypeStruct((1, N_Q), F32),
            jax.ShapeDtypeStruct((REL_BUCKETS, N_Q), F32),
        ],
        scratch_shapes=[
            pltpu.VMEM((N_Q, BLOCK, 2 * BLOCK), F32),
            pltpu.VMEM((N_Q, BLOCK, 2 * BLOCK), F32),
            pltpu.VMEM((BLOCK, 2 * D_KV), F32),
            pltpu.VMEM((N_Q, BLOCK, 1), F32),
        ],
        compiler_params=_params(("arbitrary",)),
    )(zuv, qkv, qkv, qkv, gv, w_sp, b_sp, sinks, table, bucket, dmix)


def _in_bwd(x, dh1, dzm, dkv, g1, w_in):
    s = x.shape[0]
    tm = min(TOK_TILE, s)

    def body(x_ref, dh1_ref, dzm_ref, dkv_ref, g_ref, w_ref, dx_ref, dg_ref):
        @pl.when(pl.program_id(0) == 0)
        def _():
            dg_ref[...] = jnp.zeros_like(dg_ref)

        dhn = _nt(dzm_ref[...], w_ref[:, :D_MAIN]) + _nt(dkv_ref[...], w_ref[:, D_MAIN:])
        xv = x_ref[...]
        r = _rms_scale(xv)
        dg_ref[...] += jnp.sum(dhn * (xv * r), axis=0, keepdims=True)
        dx_ref[...] = dh1_ref[...] + _rms_bwd(dhn * g_ref[...], xv, r)

    tile = lambda cols: pl.BlockSpec((tm, cols), lambda i: (i, 0))
    row = pl.BlockSpec((1, D_MODEL), lambda i: (0, 0))
    return _call(
        body,
        name="in_bwd",
        grid=(s // tm,),
        in_specs=[tile(D_MODEL), tile(D_MODEL), tile(D_MAIN), tile(2 * D_KV), row, pl.BlockSpec((D_MODEL, D_IN), lambda i: (0, 0))],
        out_specs=[tile(D_MODEL), row],
        out_shape=[jax.ShapeDtypeStruct((s, D_MODEL), F32), jax.ShapeDtypeStruct((1, D_MODEL), F32)],
        compiler_params=_params(("arbitrary",)),
    )(x, dh1, dzm, dkv, g1, w_in)


def _wgrad(a, b, tm, tn, name, peer_cols=False):
    s, m = a.shape
    n = b.shape[1]

    def body(a_ref, b_ref, o_ref, at_ref):
        @pl.when(pl.program_id(1) == 0)
        def _():
            at_ref[...] = a_ref[...].T

        r = _nn(at_ref[...], b_ref[...])
        if peer_cols:
            o_ref[0] = r
        else:
            o_ref[...] = r

    if peer_cols:
        out_spec = pl.BlockSpec((1, tm, tn), lambda i, j: (j, i, 0))
        out_shape = jax.ShapeDtypeStruct((n // tn, m, tn), F32)
    else:
        out_spec = pl.BlockSpec((tm, tn), lambda i, j: (i, j))
        out_shape = jax.ShapeDtypeStruct((m, n), F32)
    return _call(
        body,
        name=name,
        grid=(m // tm, n // tn),
        in_specs=[pl.BlockSpec((s, tm), lambda i, j: (0, i)), pl.BlockSpec((s, tn), lambda i, j: (0, j))],
        out_specs=out_spec,
        out_shape=out_shape,
        scratch_shapes=[pltpu.VMEM((tm, s), BF)],
        compiler_params=_params(("arbitrary", "arbitrary")),
    )(a, b)


def _adamw_math(w, g, m, v):
    m_new = ADAM_B1 * m + (1.0 - ADAM_B1) * g
    v_new = ADAM_B2 * v + (1.0 - ADAM_B2) * jnp.square(g)
    m_hat = m_new / (1.0 - ADAM_B1 ** ADAM_STEP)
    v_hat = v_new / (1.0 - ADAM_B2 ** ADAM_STEP)
    delta = -ADAM_LR * (m_hat / (jnp.sqrt(v_hat) + ADAM_EPS) + ADAM_WD * w)
    return delta, m_new, v_new


def _adamw(w, g, m, v, name):
    r, c = w.shape
    tr = min(r, 256)

    def body(w_ref, g_ref, m_ref, v_ref, d_ref, mo_ref, vo_ref):
        d_ref[...], mo_ref[...], vo_ref[...] = _adamw_math(w_ref[...], g_ref[...], m_ref[...], v_ref[...])

    spec = pl.BlockSpec((tr, c), lambda i: (i, 0))
    return _call(
        body,
        name=name,
        grid=(r // tr,),
        in_specs=[spec] * 4,
        out_specs=[spec] * 3,
        out_shape=[jax.ShapeDtypeStruct((r, c), F32)] * 3,
        compiler_params=_params(("arbitrary",)),
    )(w, g, m, v)


def _adamw_small(parts, w, m, v):
    r, c = w.shape

    def body(p_ref, w_ref, m_ref, v_ref, g_ref, d_ref, mo_ref, vo_ref):
        g = p_ref[0]
        for d in range(1, N_DEV):
            g = g + p_ref[d]
        g_ref[...] = g
        d_ref[...], mo_ref[...], vo_ref[...] = _adamw_math(w_ref[...], g, m_ref[...], v_ref[...])

    return _call(
        body,
        name="adamw_small",
        out_shape=[jax.ShapeDtypeStruct((r, c), F32)] * 4,
        compiler_params=_params(),
    )(parts, w, m, v)


def _place():
    x, y, c = lax.axis_index("x"), lax.axis_index("y"), lax.axis_index("c")
    return x, y, c, [(1 - x, y), (x, 1 - y), (1 - x, 1 - y)]


def _dev_index(px, py, pc):
    return 4 * px + 2 * py + pc


def _all_gather(shards, out_dtype, name):
    k_n = len(shards)

    def body(*refs):
        ins, outs, stage = refs[:k_n], refs[k_n:2 * k_n], refs[2 * k_n:3 * k_n]
        send_sems, recv_sems, local_sems = refs[3 * k_n:]
        x, y, c, chips = _place()
        me, sibling = (x, y, c), (x, y, 1 - c)

        def copy(k, j, block, to, src=None):
            dst = outs[k].at[_dev_index(*block)]
            return pltpu.make_async_remote_copy(
                src_ref=dst if src is None else src, dst_ref=dst,
                send_sem=send_sems.at[k, j], recv_sem=recv_sems.at[k, j], device_id=to, device_id_type=MESH)

        mine, first = [], []
        for k in range(k_n):
            stage[k][...] = ins[k][...].astype(out_dtype)
            mine.append(pltpu.make_async_copy(stage[k], outs[k].at[_dev_index(*me)], local_sems.at[k]))
            mine[k].start()
            first.append(copy(k, 0, me, sibling, src=stage[k]))
            first += [copy(k, 1 + j, me, (*chip, c), src=stage[k]) for j, chip in enumerate(chips)]
        for cp in first:
            cp.start()
        passed = []
        for k in range(k_n):
            for j, chip in enumerate(chips):
                copy(k, 1 + j, (*chip, c), me).wait_recv()
                passed.append(copy(k, 4 + j, (*chip, c), sibling))
                passed[-1].start()
        for k in range(k_n):
            copy(k, 0, sibling, me).wait_recv()
            for j, chip in enumerate(chips):
                copy(k, 4 + j, (*chip, 1 - c), me).wait_recv()
        for cp in first + passed:
            cp.wait_send()
        for cp in mine:
            cp.wait()

    return _call(
        body,
        name=name,
        in_specs=[pl.BlockSpec(memory_space=pltpu.VMEM)] * k_n,
        out_specs=[pl.BlockSpec(memory_space=pl.ANY)] * k_n,
        out_shape=[jax.ShapeDtypeStruct((N_DEV,) + sh.shape, out_dtype) for sh in shards],
        scratch_shapes=[pltpu.VMEM(sh.shape, out_dtype) for sh in shards]
        + [pltpu.SemaphoreType.DMA((k_n, 7)), pltpu.SemaphoreType.DMA((k_n, 7)), pltpu.SemaphoreType.DMA((k_n,))],
        compiler_params=_params(),
    )(*shards)


HBM_SPEC = pl.BlockSpec(memory_space=pltpu.HBM)
SEM_SPEC = pl.BlockSpec(memory_space=pltpu.SEMAPHORE)
ANY_SPEC = pl.BlockSpec(memory_space=pl.ANY)
DATAFLOW = pltpu.SideEffectType.DATAFLOW_SIDE_EFFECTING


def _hbm(a):
    return pltpu.with_memory_space_constraint(a, pltpu.HBM)


def _prep_weights(shards):
    k_n = len(shards)

    def body(*refs):
        ins, outs, stage, sems = refs[:k_n], refs[k_n:2 * k_n], refs[2 * k_n:3 * k_n], refs[3 * k_n]
        x, y, c, _ = _place()
        copies = []
        for k in range(k_n):
            stage[k][...] = ins[k][...].astype(BF)
            copies.append(pltpu.make_async_copy(stage[k], outs[k].at[_dev_index(x, y, c)], sems.at[k]))
            copies[k].start()
        for cp in copies:
            cp.wait()

    return _call(
        body,
        name="prep_weights",
        in_specs=[pl.BlockSpec(memory_space=pltpu.VMEM)] * k_n,
        out_specs=[ANY_SPEC] * k_n,
        out_shape=[jax.ShapeDtypeStruct((N_DEV,) + sh.shape, BF) for sh in shards],
        scratch_shapes=[pltpu.VMEM(sh.shape, BF) for sh in shards] + [pltpu.SemaphoreType.DMA((k_n,))],
        compiler_params=_params(),
    )(*shards)


def _gather_inplace(lands, name):
    k_n = len(lands)

    def body(*refs):
        outs = refs[k_n:2 * k_n]
        send_sems, recv_sems = refs[2 * k_n:]
        x, y, c, chips = _place()
        me, sibling = (x, y, c), (x, y, 1 - c)

        def copy(k, j, block, to):
            blk = outs[k].at[_dev_index(*block)]
            return pltpu.make_async_remote_copy(
                src_ref=blk, dst_ref=blk, send_sem=send_sems.at[k, j], recv_sem=recv_sems.at[k, j], device_id=to, device_id_type=MESH)

        first = []
        for k in range(k_n):
            first.append(copy(k, 0, me, sibling))
            first += [copy(k, 1 + j, me, (*chip, c)) for j, chip in enumerate(chips)]
        for cp in first:
            cp.start()
        passed = []
        for k in range(k_n):
            for j, chip in enumerate(chips):
                copy(k, 1 + j, (*chip, c), me).wait_recv()
                passed.append(copy(k, 4 + j, (*chip, c), sibling))
                passed[-1].start()
        for k in range(k_n):
            copy(k, 0, sibling, me).wait_recv()
            for j, chip in enumerate(chips):
                copy(k, 4 + j, (*chip, 1 - c), me).wait_recv()
        for cp in first + passed:
            cp.wait_send()

    return _call(
        body,
        name=name,
        in_specs=[ANY_SPEC] * k_n,
        out_specs=[ANY_SPEC] * k_n,
        out_shape=[jax.ShapeDtypeStruct(a.shape, a.dtype) for a in lands],
        input_output_aliases={k: k for k in range(k_n)},
        scratch_shapes=[pltpu.SemaphoreType.DMA((k_n, 7)), pltpu.SemaphoreType.DMA((k_n, 7))],
        compiler_params=_params(),
    )(*lands)


def _ag_start(lands):
    k_n = len(lands)

    def body(*refs):
        land = refs[:k_n]
        send_sems, recv_sems = refs[k_n], refs[k_n + 1]
        token = refs[-1]
        x, y, c, chips = _place()
        targets = [(x, y, 1 - c)] + [(*chip, c) for chip in chips]
        for k in range(k_n):
            blk = land[k].at[_dev_index(x, y, c)]
            for j, to in enumerate(targets):
                pltpu.make_async_remote_copy(
                    src_ref=blk, dst_ref=blk, send_sem=send_sems.at[4 * k + j], recv_sem=recv_sems.at[4 * k + j],
                    device_id=to, device_id_type=MESH).start()
        token[...] = jnp.zeros_like(token)

    outs = pl.pallas_call(
        body,
        name="ag_start",
        in_specs=[HBM_SPEC] * k_n,
        out_specs=(SEM_SPEC, SEM_SPEC, *[HBM_SPEC] * k_n, pl.BlockSpec(memory_space=pltpu.VMEM)),
        out_shape=(pltpu.SemaphoreType.DMA((4 * k_n,)), pltpu.SemaphoreType.DMA((4 * k_n,)),
                   *[pltpu.HBM(a.shape, a.dtype) for a in lands], jax.ShapeDtypeStruct((8, LANES), F32)),
        input_output_aliases={k: 2 + k for k in range(k_n)},
        compiler_params=pltpu.CompilerParams(has_side_effects=DATAFLOW),
    )(*[_hbm(a) for a in lands])
    return outs[0], outs[1], list(outs[2:2 + k_n]), outs[-1]


def _ag_mid(lands, send_sems, recv_sems, after):
    k_n = len(lands)

    def body(*refs):
        land = refs[:k_n]
        send1, recv1 = refs[k_n], refs[k_n + 1]
        fwd_send, fwd_recv = refs[-2], refs[-1]
        x, y, c, chips = _place()
        sources = [(x, y, 1 - c)] + [(*chip, c) for chip in chips]
        for k in range(k_n):
            mine = land[k].at[_dev_index(x, y, c)]
            for j, frm in enumerate(sources):
                got = land[k].at[_dev_index(*frm)]
                cp = pltpu.make_async_remote_copy(
                    src_ref=mine, dst_ref=got, send_sem=send1.at[4 * k + j], recv_sem=recv1.at[4 * k + j], device_id=frm, device_id_type=MESH)
                cp.wait_send()
                cp.wait_recv()
                if j >= 1:
                    pltpu.make_async_remote_copy(
                        src_ref=got, dst_ref=got, send_sem=fwd_send.at[3 * k + j - 1], recv_sem=fwd_recv.at[3 * k + j - 1],
                        device_id=(x, y, 1 - c), device_id_type=MESH).start()

    outs = pl.pallas_call(
        body,
        name="ag_mid",
        in_specs=[HBM_SPEC] * k_n + [SEM_SPEC, SEM_SPEC, ANY_SPEC],
        out_specs=(*[HBM_SPEC] * k_n, SEM_SPEC, SEM_SPEC),
        out_shape=(*[pltpu.HBM(a.shape, a.dtype) for a in lands], pltpu.SemaphoreType.DMA((3 * k_n,)), pltpu.SemaphoreType.DMA((3 * k_n,))),
        input_output_aliases={k: k for k in range(k_n)},
        compiler_params=pltpu.CompilerParams(has_side_effects=DATAFLOW),
    )(*lands, send_sems, recv_sems, after)
    return list(outs[:k_n]), outs[-2], outs[-1]


def _ag_end(lands, fwd_send, fwd_recv):
    k_n = len(lands)

    def body(*refs):
        land = refs[:k_n]
        fsend, frecv = refs[k_n], refs[k_n + 1]
        x, y, c, chips = _place()
        for k in range(k_n):
            for j, chip in enumerate(chips):
                cp = pltpu.make_async_remote_copy(
                    src_ref=land[k].at[_dev_index(*chip, c)], dst_ref=land[k].at[_dev_index(*chip, 1 - c)],
                    send_sem=fsend.at[3 * k + j], recv_sem=frecv.at[3 * k + j], device_id=(x, y, 1 - c), device_id_type=MESH)
                cp.wait_send()
                cp.wait_recv()

    outs = pl.pallas_call(
        body,
        name="ag_end",
        in_specs=[HBM_SPEC] * k_n + [SEM_SPEC, SEM_SPEC],
        out_specs=tuple([HBM_SPEC] * k_n),
        out_shape=tuple(pltpu.HBM(a.shape, a.dtype) for a in lands),
        input_output_aliases={k: k for k in range(k_n)},
        compiler_params=pltpu.CompilerParams(has_side_effects=DATAFLOW),
    )(*lands, fwd_send, fwd_recv)
    return list(outs)


def _reduce_scatter(grads):
    k_n = len(grads)
    shapes = [g.shape[1:] for g in grads]

    def body(*refs):
        gs, outs, land1 = refs[:k_n], refs[k_n:2 * k_n], refs[2 * k_n:3 * k_n]
        send2, recv2 = refs[3 * k_n:4 * k_n], refs[4 * k_n:5 * k_n]
        s1_send, s1_recv, s2_send, s2_recv = refs[5 * k_n:]
        x, y, c, others = _place()
        sibling = (x, y, 1 - c)
        chips = [(x, y)] + others

        def copy1(k, j):
            return pltpu.make_async_remote_copy(
                src_ref=gs[k].at[_dev_index(*chips[j], 1 - c)], dst_ref=land1[k].at[j],
                send_sem=s1_send.at[k, j], recv_sem=s1_recv.at[k, j], device_id=sibling, device_id_type=MESH)

        def copy2(k, j):
            return pltpu.make_async_remote_copy(
                src_ref=send2[k].at[j - 1], dst_ref=recv2[k].at[j - 1],
                send_sem=s2_send.at[k, j - 1], recv_sem=s2_recv.at[k, j - 1], device_id=(*chips[j], c), device_id_type=MESH)

        for k in range(k_n):
            for j in range(4):
                copy1(k, j).start()

        def chip_sum(k, j, a_ref, b_ref, sems):
            ca = pltpu.make_async_copy(gs[k].at[_dev_index(*chips[j], c)], a_ref, sems.at[0])
            cb = pltpu.make_async_copy(land1[k].at[j], b_ref, sems.at[1])
            ca.start()
            cb.start()
            ca.wait()
            cb.wait()
            return a_ref[...] + b_ref[...]

        for k in range(k_n):
            def stage(a_ref, b_ref, sems, k=k):
                for j in (1, 2, 3, 0):
                    copy1(k, j).wait_recv()
                    total = chip_sum(k, j, a_ref, b_ref, sems)
                    if j == 0:
                        outs[k][...] = total
                    else:
                        send2[k][j - 1] = total.astype(BF)
                        copy2(k, j).start()

            pl.run_scoped(stage, pltpu.VMEM(shapes[k], F32), pltpu.VMEM(shapes[k], F32), pltpu.SemaphoreType.DMA((2,)))

        for k in range(k_n):
            for j in (1, 2, 3):
                copy2(k, j).wait_recv()
                outs[k][...] += recv2[k][j - 1].astype(F32)
        for k in range(k_n):
            for j in range(4):
                copy1(k, j).wait_send()
            for j in (1, 2, 3):
                copy2(k, j).wait_send()

    res = _call(
        body,
        name="reduce_scatter",
        in_specs=[pl.BlockSpec(memory_space=pl.ANY)] * k_n,
        out_specs=[pl.BlockSpec(memory_space=pltpu.VMEM)] * k_n + [pl.BlockSpec(memory_space=pl.ANY)] * k_n,
        out_shape=[jax.ShapeDtypeStruct(sh, F32) for sh in shapes] + [jax.ShapeDtypeStruct((4,) + sh, F32) for sh in shapes],
        scratch_shapes=[pltpu.VMEM((3,) + sh, BF) for sh in shapes] * 2
        + [pltpu.SemaphoreType.DMA((k_n, 4)), pltpu.SemaphoreType.DMA((k_n, 4)),
           pltpu.SemaphoreType.DMA((k_n, 3)), pltpu.SemaphoreType.DMA((k_n, 3))],
        compiler_params=_params(),
    )(*grads)
    return res[:k_n]


SMALL = ("norm1_gain", "gmlp_v_gain", "w_spatial", "b_spatial", "attn_sinks", "rel_bias_table", "norm2_gain", "final_gain")
LANES = 128


def _pack_small(arrays):
    rows = []
    for a in arrays:
        flat = a.reshape(-1)
        pad = (-flat.shape[0]) % LANES
        rows.append(jnp.pad(flat, (0, pad)).reshape(-1, LANES))
    packed = jnp.concatenate(rows, axis=0)
    return jnp.pad(packed, ((0, (-packed.shape[0]) % 8), (0, 0)))


def _unpack_small(packed, like):
    out, r = [], 0
    for a in like:
        n = a.size
        nr = -(-n // LANES)
        out.append(packed[r:r + nr].reshape(-1)[:n].reshape(a.shape))
        r += nr
    return out


def _local_step(x, p, t, small, w_in, token, gather_rest):
    g1, gv, w_sp, b_sp, sinks, table, g2, gf = (small[n] for n in SMALL)
    bucket = jnp.asarray(_bucket_table())
    b_col = b_sp.reshape(GROUPS, BLOCK, 1)
    gf_row = gf.reshape(1, D_MODEL)

    zuv, qkv, hn1 = _in_proj(x, g1 + token[0, 0], w_in)
    mix = _mixer_fwd(zuv, qkv, gv, w_sp[0], b_col, sinks, table, bucket)
    w_out, w_ff1, w_ff2, w_gate, w_proj = gather_rest(mix)
    (loss, d_g2, d_gf, dh1, dh1b, dmix, hn2, a, df, h2, dh2, dgl, dpp) = _tail(x, mix, p, t, g2, gf_row, w_out, w_ff1, w_ff2, w_gate, w_proj)
    dzm, dkv, d_gv, d_wsp, d_bsp, d_sinks, d_table = _mixer_bwd(zuv, qkv, dmix, gv, w_sp[0], b_col, sinks, table, bucket)
    dx, d_g1 = _in_bwd(x, dh1, dzm, dkv, g1, w_in)

    gw_in = jnp.concatenate([_wgrad(hn1, dzm, 512, 512, "wgrad_in_main"), _wgrad(hn1, dkv, 512, 256, "wgrad_in_kv")], axis=1)
    gw_in = gw_in.reshape(D_MODEL, N_DEV, D_IN // N_DEV).transpose(1, 0, 2)
    gw_out = _wgrad(mix, dh1b, 512, 1024, "wgrad_out").reshape(N_DEV, D_MODEL // N_DEV, D_MODEL)
    gw_ff1 = _wgrad(hn2, df, 512, D_FF // N_DEV, "wgrad_ff1", peer_cols=True)
    gw_ff2 = _wgrad(a, dh2, 512, 1024, "wgrad_ff2").reshape(N_DEV, D_FF // N_DEV, D_MODEL)
    gw_proj = _wgrad(p.astype(BF), dpp, PLE_DIM, D_MODEL // N_DEV, "wgrad_proj", peer_cols=True)
    gw_gate = _wgrad(h2, dgl, 512, 1024, "wgrad_gate").reshape(N_DEV, D_MODEL // N_DEV, D_MODEL)

    small_grads = dict(
        norm1_gain=d_g1, gmlp_v_gain=d_gv, w_spatial=d_wsp.reshape(w_sp.shape), b_spatial=d_bsp.reshape(b_sp.shape),
        attn_sinks=d_sinks, rel_bias_table=d_table, norm2_gain=d_g2, final_gain=d_gf.reshape(gf.shape))
    return loss, dx, small_grads, [gw_in, gw_out, gw_ff1, gw_ff2, gw_proj, gw_gate]


def kernel(x, p, norm1_gain, w_in, gmlp_v_gain, w_spatial, b_spatial, attn_sinks, rel_bias_table, w_out, norm2_gain, w_ff1, w_ff2, w_ple_proj, w_ple_gate, final_gain, loss_target, m_norm1_gain, m_w_in, m_gmlp_v_gain, m_w_spatial, m_b_spatial, m_attn_sinks, m_rel_bias_table, m_w_out, m_norm2_gain, m_w_ff1, m_w_ff2, m_w_ple_proj, m_w_ple_gate, m_final_gain, v_norm1_gain, v_w_in, v_gmlp_v_gain, v_w_spatial, v_b_spatial, v_attn_sinks, v_rel_bias_table, v_w_out, v_norm2_gain, v_w_ff1, v_w_ff2, v_w_ple_proj, v_w_ple_gate, v_final_gain):
    args = dict(locals())
    s = x.shape[1]
    big = ("w_in", "w_out", "w_ff1", "w_ff2", "w_ple_proj", "w_ple_gate")

    lands = _prep_weights([args[n][0] for n in big])
    g_in = _gather_inplace(lands[:1], "gather_w_in")[0]
    full_in = g_in.transpose(1, 0, 2).reshape(D_MODEL, D_IN)
    send_sems, recv_sems, flying, token = _ag_start(lands[1:])

    def gather_rest(after):
        mid, fwd_send, fwd_recv = _ag_mid(flying, send_sems, recv_sems, after)
        g_out, g_ff1, g_ff2, g_proj, g_gate = _ag_end(mid, fwd_send, fwd_recv)
        return (g_out.reshape(D_MODEL, D_MODEL), g_ff1, g_ff2.reshape(D_FF, D_MODEL), g_gate.reshape(D_MODEL, D_MODEL),
                g_proj.transpose(1, 0, 2).reshape(PLE_DIM, D_MODEL))

    small = {n: args[n] for n in SMALL}
    loss, dx, small_grads, big_grads = _local_step(
        x.reshape(s, D_MODEL), p.reshape(s, PLE_DIM), loss_target.reshape(s, D_MODEL), small, full_in, token, gather_rest)

    loss = lax.psum(loss[0, 0], ("x", "y", "c"))
    grad_x = dx.reshape(x.shape)

    big_g = _reduce_scatter(big_grads)
    grads, deltas, new_m, new_v = {}, {}, {}, {}
    for n, g in zip(big, big_g):
        d, mo, vo = _adamw(args[n][0], g, args["m_" + n][0], args["v_" + n][0], "adamw_" + n)
        grads[n], deltas[n], new_m[n], new_v[n] = g[None], d[None], mo[None], vo[None]

    like = [args[n] for n in SMALL]
    parts = _all_gather([_pack_small([small_grads[n] for n in SMALL])], F32, "gather_small_grads")[0]
    packed = _adamw_small(parts, _pack_small(like), _pack_small([args["m_" + n] for n in SMALL]),
                          _pack_small([args["v_" + n] for n in SMALL]))
    for dst, pk in zip((grads, deltas, new_m, new_v), packed):
        for n, a in zip(SMALL, _unpack_small(pk, like)):
            dst[n] = a

    order = ("norm1_gain", "w_in", "gmlp_v_gain", "w_spatial", "b_spatial", "attn_sinks", "rel_bias_table", "w_out",
             "norm2_gain", "w_ff1", "w_ff2", "w_ple_proj", "w_ple_gate", "final_gain")
    return (loss, grad_x, *[grads[n] for n in order], *[deltas[n] for n in order],
            *[new_m[n] for n in order], *[new_v[n] for n in order])
```

```python
import functools
import math

import numpy as np
import jax
import jax.numpy as jnp
from jax import lax
from jax.experimental import pallas as pl
from jax.experimental.pallas import tpu as pltpu

F32 = jnp.float32
BF = jnp.bfloat16
MESH = pl.DeviceIdType.MESH
N_DEV = 8

D_MODEL = 1024
PLE_DIM = 256
D_GMLP = 512
GROUPS = 4
GDIM = 128
BLOCK = 128
D_ATTN = 512
HEAD_DIM = 64
N_Q = 8
Q_PER_KV = 4
D_KV = 128
D_FF = 4096
D_IN = 1792
D_MAIN = 2 * D_GMLP + D_ATTN
REL_BUCKETS = 32
EPS = 1e-6
NEG_INF = -1e30
SCALE = HEAD_DIM ** -0.5
GELU_C = math.sqrt(2.0 / math.pi)
GELU_A = 0.044715

ADAM_LR = 0.001
ADAM_B1 = 0.9
ADAM_B2 = 0.999
ADAM_EPS = 1e-08
ADAM_WD = 0.01
ADAM_STEP = 10

V7X_VMEM_LIMIT = 60000 * 1024
TOK_TILE = 256


def _call(body, **kw):
    return pl.pallas_call(body, **kw)


def _params(sem=None):
    if sem is None:
        return pltpu.CompilerParams(vmem_limit_bytes=V7X_VMEM_LIMIT)
    return pltpu.CompilerParams(dimension_semantics=sem, vmem_limit_bytes=V7X_VMEM_LIMIT)


def _nn(a, b):
    return jnp.dot(a, b, preferred_element_type=F32)


def _nt(a, b):
    return lax.dot_general(a, b, (((1,), (1,)), ((), ())), preferred_element_type=F32)


def _tn(a, b):
    return lax.dot_general(a, b, (((0,), (0,)), ((), ())), preferred_element_type=F32)


def _gelu_tanh(x):
    return jnp.tanh(GELU_C * (x + GELU_A * (x * x * x)))


def _gelu(x):
    return x * (0.5 * (1.0 + _gelu_tanh(x)))


def _gelu_grad(x):
    t = _gelu_tanh(x)
    return 0.5 * (1.0 + t) + 0.5 * x * (1.0 - t * t) * (GELU_C * (1.0 + 3.0 * GELU_A * (x * x)))


def _rms_scale(x):
    return lax.rsqrt(jnp.mean(x * x, axis=-1, keepdims=True) + EPS)


def _rms_bwd(dxn, x, r):
    return r * dxn - x * ((r * r * r) * jnp.mean(dxn * x, axis=-1, keepdims=True))


def _bucket_table():
    a = np.arange(BLOCK)[:, None]
    j = np.arange(2 * BLOCK)[None, :]
    n = BLOCK + a - j
    valid = (n >= 0) & (n < BLOCK)
    nc = np.maximum(n, 0)
    max_exact = REL_BUCKETS // 2
    nf = np.maximum(nc, 1).astype(np.float32)
    large = max_exact + (
        np.log(nf / np.float32(max_exact)) / np.float32(math.log(BLOCK / max_exact)) * np.float32(REL_BUCKETS - max_exact)
    ).astype(np.int32)
    large = np.minimum(large, REL_BUCKETS - 1)
    bucket = np.where(nc < max_exact, nc, large)
    return np.where(valid, bucket, -1).astype(np.int32)


def _in_proj(x, g1, w_in):
    s = x.shape[0]
    tm = min(TOK_TILE, s)

    def body(x_ref, g_ref, w_ref, zuv_ref, qkv_ref, hn_ref):
        xv = x_ref[...]
        hn = ((xv * _rms_scale(xv)) * g_ref[...]).astype(BF)
        hn_ref[...] = hn
        z = _nn(hn, w_ref[...])
        zuv_ref[...] = z[:, : 2 * D_GMLP]
        qkv_ref[...] = z[:, 2 * D_GMLP:].astype(BF)

    return _call(
        body,
        name="in_proj",
        grid=(s // tm,),
        in_specs=[
            pl.BlockSpec((tm, D_MODEL), lambda i: (i, 0)),
            pl.BlockSpec((1, D_MODEL), lambda i: (0, 0)),
            pl.BlockSpec((D_MODEL, D_IN), lambda i: (0, 0)),
        ],
        out_specs=[
            pl.BlockSpec((tm, 2 * D_GMLP), lambda i: (i, 0)),
            pl.BlockSpec((tm, D_ATTN + 2 * D_KV), lambda i: (i, 0)),
            pl.BlockSpec((tm, D_MODEL), lambda i: (i, 0)),
        ],
        out_shape=[
            jax.ShapeDtypeStruct((s, 2 * D_GMLP), F32),
            jax.ShapeDtypeStruct((s, D_ATTN + 2 * D_KV), BF),
            jax.ShapeDtypeStruct((s, D_MODEL), BF),
        ],
        compiler_params=_params(("arbitrary",)),
    )(x, g1, w_in)


def _build_bias(bias_ref, bucket_ref, table_ref):
    bucket = bucket_ref[...]
    for h in range(N_Q):
        acc = jnp.zeros((BLOCK, 2 * BLOCK), F32)
        for b in range(REL_BUCKETS):
            acc = jnp.where(bucket == b, table_ref[b, h], acc)
        bias_ref[h] = acc


def _window_masks(i):
    row = lax.broadcasted_iota(jnp.int32, (BLOCK, BLOCK), 0)
    col = lax.broadcasted_iota(jnp.int32, (BLOCK, BLOCK), 1)
    return (col > row) & (i > 0), col <= row


def _tril_bf16(w_ref, g):
    row = lax.broadcasted_iota(jnp.int32, (BLOCK, BLOCK), 0)
    col = lax.broadcasted_iota(jnp.int32, (BLOCK, BLOCK), 1)
    return jnp.where(col <= row, w_ref[g], 0.0).astype(BF)


def _attn_probs(q_h, k_prev, k_cur, bias_h, sink, valid_prev, valid_cur):
    l_prev = jnp.where(valid_prev, _nt(q_h, k_prev) * SCALE + bias_h[:, :BLOCK], NEG_INF)
    l_cur = jnp.where(valid_cur, _nt(q_h, k_cur) * SCALE + bias_h[:, BLOCK:], NEG_INF)
    m = jnp.maximum(jnp.maximum(jnp.max(l_prev, axis=-1, keepdims=True), jnp.max(l_cur, axis=-1, keepdims=True)), sink)
    e_prev = jnp.exp(l_prev - m)
    e_cur = jnp.exp(l_cur - m)
    e_sink = jnp.exp(sink - m)
    denom = jnp.sum(e_prev, axis=-1, keepdims=True) + jnp.sum(e_cur, axis=-1, keepdims=True) + e_sink
    return e_prev / denom, e_cur / denom, e_sink / denom


def _mixer_specs(nb):
    cl = lambda i: jnp.minimum(i, nb - 1)
    return [
        pl.BlockSpec((BLOCK, 2 * D_GMLP), lambda i: (cl(i), 0)),
        pl.BlockSpec((BLOCK, D_ATTN), lambda i: (cl(i), 0)),
        pl.BlockSpec((BLOCK, 2 * D_KV), lambda i: (cl(i), D_ATTN // (2 * D_KV))),
        pl.BlockSpec((BLOCK, 2 * D_KV), lambda i: (jnp.maximum(cl(i) - 1, 0), D_ATTN // (2 * D_KV))),
        pl.BlockSpec((1, D_GMLP), lambda i: (0, 0)),
        pl.BlockSpec((GROUPS, BLOCK, BLOCK), lambda i: (0, 0, 0)),
        pl.BlockSpec((GROUPS, BLOCK, 1), lambda i: (0, 0, 0)),
        pl.BlockSpec(memory_space=pltpu.SMEM),
        pl.BlockSpec(memory_space=pltpu.SMEM),
        pl.BlockSpec((BLOCK, 2 * BLOCK), lambda i: (0, 0)),
    ]


def _mixer_fwd(zuv, qkv, gv, w_sp, b_sp, sinks, table, bucket):
    s = zuv.shape[0]
    nb = s // BLOCK

    def body(zuv_ref, q_ref, kvc_ref, kvp_ref, gv_ref, w_ref, b_ref, sink_ref, table_ref, bucket_ref, mix_ref, bias_ref):
        i = pl.program_id(0)

        @pl.when(i == 0)
        def _():
            _build_bias(bias_ref, bucket_ref, table_ref)

        for g in range(GROUPS):
            sl = slice(g * GDIM, (g + 1) * GDIM)
            u = _gelu(zuv_ref[:, sl])
            vg = _gelu(zuv_ref[:, D_GMLP + g * GDIM: D_GMLP + (g + 1) * GDIM])
            vn = ((vg * _rms_scale(vg)) * gv_ref[:, sl]).astype(BF)
            sv = _nn(_tril_bf16(w_ref, g), vn) + b_ref[g]
            mix_ref[:, sl] = (u * sv).astype(BF)

        valid_prev, valid_cur = _window_masks(i)
        for h in range(N_Q):
            kh = h // Q_PER_KV
            q_h = q_ref[:, h * HEAD_DIM:(h + 1) * HEAD_DIM]
            ksl = slice(kh * HEAD_DIM, (kh + 1) * HEAD_DIM)
            vsl = slice(D_KV + kh * HEAD_DIM, D_KV + (kh + 1) * HEAD_DIM)
            p_prev, p_cur, _ = _attn_probs(q_h, kvp_ref[:, ksl], kvc_ref[:, ksl], bias_ref[h], sink_ref[0, h], valid_prev, valid_cur)
            o = _nn(p_prev.astype(BF), kvp_ref[:, vsl]) + _nn(p_cur.astype(BF), kvc_ref[:, vsl])
            mix_ref[:, D_GMLP + h * HEAD_DIM: D_GMLP + (h + 1) * HEAD_DIM] = o.astype(BF)

    return _call(
        body,
        name="mixer_fwd",
        grid=(nb,),
        in_specs=_mixer_specs(nb),
        out_specs=pl.BlockSpec((BLOCK, D_MODEL), lambda i: (i, 0)),
        out_shape=jax.ShapeDtypeStruct((s, D_MODEL), BF),
        scratch_shapes=[pltpu.VMEM((N_Q, BLOCK, 2 * BLOCK), F32)],
        compiler_params=_params(("arbitrary",)),
    )(zuv, qkv, qkv, qkv, gv, w_sp, b_sp, sinks, table, bucket)


def _tail(x, mix, p, t, g2, gf, w_out, w_ff1, w_ff2, w_gate, w_proj):
    s = x.shape[0]
    tm = min(TOK_TILE, s)
    n_ff = w_ff1.shape[0]
    fc = D_FF // n_ff

    def body(x_ref, mix_ref, p_ref, t_ref, g2_ref, gf_ref, wo_ref, w1_ref, w2_ref, wg_ref, wp_ref,
             loss_ref, dg2_ref, dgf_ref, dh1_ref, dh1b_ref, dmix_ref, hn2_ref, a_ref, df_ref, h2_ref, dh2_ref, dgl_ref, dpp_ref, f_ref):
        i = pl.program_id(0)

        @pl.when(i == 0)
        def _():
            loss_ref[...] = jnp.zeros_like(loss_ref)
            dg2_ref[...] = jnp.zeros_like(dg2_ref)
            dgf_ref[...] = jnp.zeros_like(dgf_ref)

        h1 = x_ref[...] + _nn(mix_ref[...], wo_ref[...])
        r2 = _rms_scale(h1)
        hn2 = ((h1 * r2) * g2_ref[...]).astype(BF)
        hn2_ref[...] = hn2
        h2 = h1
        for c in range(n_ff):
            f = _nn(hn2, w1_ref[c])
            f_ref[:, c * fc:(c + 1) * fc] = f
            a = jnp.square(jnp.maximum(f, 0.0)).astype(BF)
            a_ref[:, c * fc:(c + 1) * fc] = a
            h2 = h2 + _nn(a, w2_ref[c * fc:(c + 1) * fc, :])
        h2b = h2.astype(BF)
        h2_ref[...] = h2b
        gate = jax.nn.sigmoid(_nn(h2b, wg_ref[...]))
        pp = _nn(p_ref[...].astype(BF), wp_ref[...])
        h3 = h2 + gate * pp
        rf = _rms_scale(h3)
        gf_v = gf_ref[...]
        err = (h3 * rf) * gf_v - t_ref[...]
        loss_ref[...] += jnp.sum(jnp.sum(err * err, axis=-1, keepdims=True), axis=0, keepdims=True) * (0.5 / D_MODEL)

        dy = err * (1.0 / D_MODEL)
        dgf_ref[...] += jnp.sum(dy * (h3 * rf), axis=0, keepdims=True)
        dh3 = _rms_bwd(dy * gf_v, h3, rf)
        dpp_ref[...] = (dh3 * gate).astype(BF)
        dgl = ((dh3 * pp) * (gate * (1.0 - gate))).astype(BF)
        dgl_ref[...] = dgl
        dh2 = dh3 + _nt(dgl, wg_ref[...])
        dh2b = dh2.astype(BF)
        dh2_ref[...] = dh2b
        dhn2 = jnp.zeros((tm, D_MODEL), F32)
        for c in range(n_ff):
            da = _nt(dh2b, w2_ref[c * fc:(c + 1) * fc, :])
            df = (da * (2.0 * jnp.maximum(f_ref[:, c * fc:(c + 1) * fc], 0.0))).astype(BF)
            df_ref[:, c * fc:(c + 1) * fc] = df
            dhn2 = dhn2 + _nt(df, w1_ref[c])
        dg2_ref[...] += jnp.sum(dhn2 * (h1 * r2), axis=0, keepdims=True)
        dh1 = dh2 + _rms_bwd(dhn2 * g2_ref[...], h1, r2)
        dh1_ref[...] = dh1
        dh1b = dh1.astype(BF)
        dh1b_ref[...] = dh1b
        dmix_ref[...] = _nt(dh1b, wo_ref[...])

    tile = lambda cols: pl.BlockSpec((tm, cols), lambda i: (i, 0))
    whole = lambda shape: pl.BlockSpec(shape, lambda i: (0,) * len(shape), pipeline_mode=pl.Buffered(1))
    row = pl.BlockSpec((1, D_MODEL), lambda i: (0, 0))
    act = lambda cols, dt: jax.ShapeDtypeStruct((s, cols), dt)
    return _call(
        body,
        name="tail",
        grid=(s // tm,),
        in_specs=[tile(D_MODEL), tile(D_MODEL), tile(PLE_DIM), tile(D_MODEL), row, row,
                  whole(w_out.shape), whole(w_ff1.shape), whole(w_ff2.shape), whole(w_gate.shape), whole(w_proj.shape)],
        out_specs=[pl.BlockSpec((1, 1), lambda i: (0, 0)), row, row, tile(D_MODEL), tile(D_MODEL), tile(D_MODEL), tile(D_MODEL), tile(D_FF),
                   tile(D_FF), tile(D_MODEL), tile(D_MODEL), tile(D_MODEL), tile(D_MODEL)],
        out_shape=[jax.ShapeDtypeStruct((1, 1), F32), jax.ShapeDtypeStruct((1, D_MODEL), F32), jax.ShapeDtypeStruct((1, D_MODEL), F32),
                   act(D_MODEL, F32), act(D_MODEL, BF), act(D_MODEL, F32), act(D_MODEL, BF), act(D_FF, BF), act(D_FF, BF), act(D_MODEL, BF),
                   act(D_MODEL, BF), act(D_MODEL, BF), act(D_MODEL, BF)],
        scratch_shapes=[pltpu.VMEM((tm, D_FF), F32)],
        compiler_params=_params(("arbitrary",)),
    )(x, mix, p, t, g2, gf, w_out, w_ff1, w_ff2, w_gate, w_proj)


def _mixer_bwd(zuv, qkv, dmix, gv, w_sp, b_sp, sinks, table, bucket):
    s = zuv.shape[0]
    nb = s // BLOCK

    def body(zuv_ref, q_ref, kvc_ref, kvp_ref, gv_ref, w_ref, b_ref, sink_ref, table_ref, bucket_ref, dmix_ref,
             dzm_ref, dkv_ref, dgv_ref, dw_ref, db_ref, dsink_ref, dtable_ref,
             bias_ref, dbias_ref, carry_ref, dsink_acc):
        i = pl.program_id(0)

        @pl.when(i == 0)
        def _():
            _build_bias(bias_ref, bucket_ref, table_ref)
            dbias_ref[...] = jnp.zeros_like(dbias_ref)
            carry_ref[...] = jnp.zeros_like(carry_ref)
            dsink_acc[...] = jnp.zeros_like(dsink_acc)
            dgv_ref[...] = jnp.zeros_like(dgv_ref)
            dw_ref[...] = jnp.zeros_like(dw_ref)
            db_ref[...] = jnp.zeros_like(db_ref)

        @pl.when(i < nb)
        def _():
            for g in range(GROUPS):
                sl = slice(g * GDIM, (g + 1) * GDIM)
                slv = slice(D_GMLP + g * GDIM, D_GMLP + (g + 1) * GDIM)
                zu = zuv_ref[:, sl]
                zv = zuv_ref[:, slv]
                u = _gelu(zu)
                vg = _gelu(zv)
                rg = _rms_scale(vg)
                vhat = vg * rg
                gain = gv_ref[:, sl]
                vn = (vhat * gain).astype(BF)
                w_g = _tril_bf16(w_ref, g)
                sv = _nn(w_g, vn) + b_ref[g]
                dmix_a = dmix_ref[:, sl]
                dsv = dmix_a * u
                dsvb = dsv.astype(BF)
                db_ref[g] += jnp.sum(dsv, axis=-1, keepdims=True)
                dw_ref[g] += _nt(dsvb, vn)
                dvn = _tn(w_g, dsvb)
                dgv_ref[:, sl] += jnp.sum(dvn * vhat, axis=0, keepdims=True)
                dvg = _rms_bwd(dvn * gain, vg, rg)
                dzm_ref[:, sl] = ((dmix_a * sv) * _gelu_grad(zu)).astype(BF)
                dzm_ref[:, slv] = (dvg * _gelu_grad(zv)).astype(BF)

            valid_prev, valid_cur = _window_masks(i)
            for kh in range(N_Q // Q_PER_KV):
                ksl = slice(kh * HEAD_DIM, (kh + 1) * HEAD_DIM)
                vsl = slice(D_KV + kh * HEAD_DIM, D_KV + (kh + 1) * HEAD_DIM)
                k_prev, k_cur = kvp_ref[:, ksl], kvc_ref[:, ksl]
                v_prev, v_cur = kvp_ref[:, vsl], kvc_ref[:, vsl]
                dk_prev = jnp.zeros((BLOCK, HEAD_DIM), F32)
                dk_cur = jnp.zeros((BLOCK, HEAD_DIM), F32)
                dv_prev = jnp.zeros((BLOCK, HEAD_DIM), F32)
                dv_cur = jnp.zeros((BLOCK, HEAD_DIM), F32)
                for gq in range(Q_PER_KV):
                    h = kh * Q_PER_KV + gq
                    q_h = q_ref[:, h * HEAD_DIM:(h + 1) * HEAD_DIM]
                    p_prev, p_cur, p_sink = _attn_probs(q_h, k_prev, k_cur, bias_ref[h], sink_ref[0, h], valid_prev, valid_cur)
                    do = dmix_ref[:, D_GMLP + h * HEAD_DIM: D_GMLP + (h + 1) * HEAD_DIM].astype(BF)
                    dp_prev = _nt(do, v_prev)
                    dp_cur = _nt(do, v_cur)
                    delta = jnp.sum(p_prev * dp_prev, axis=-1, keepdims=True) + jnp.sum(p_cur * dp_cur, axis=-1, keepdims=True)
                    ds_prev = p_prev * (dp_prev - delta)
                    ds_cur = p_cur * (dp_cur - delta)
                    dsink_acc[h] -= p_sink * delta
                    dbias_ref[h, :, :BLOCK] += ds_prev
                    dbias_ref[h, :, BLOCK:] += ds_cur
                    dsb_prev = ds_prev.astype(BF)
                    dsb_cur = ds_cur.astype(BF)
                    dq = (_nn(dsb_prev, k_prev) + _nn(dsb_cur, k_cur)) * SCALE
                    dzm_ref[:, 2 * D_GMLP + h * HEAD_DIM: 2 * D_GMLP + (h + 1) * HEAD_DIM] = dq.astype(BF)
                    dk_prev = dk_prev + _tn(dsb_prev, q_h)
                    dk_cur = dk_cur + _tn(dsb_cur, q_h)
                    dv_prev = dv_prev + _tn(p_prev.astype(BF), do)
                    dv_cur = dv_cur + _tn(p_cur.astype(BF), do)
                dkv_ref[:, ksl] = (carry_ref[:, ksl] + dk_prev * SCALE).astype(BF)
                dkv_ref[:, vsl] = (carry_ref[:, vsl] + dv_prev).astype(BF)
                carry_ref[:, ksl] = dk_cur * SCALE
                carry_ref[:, vsl] = dv_cur

        @pl.when(i == nb)
        def _():
            dkv_ref[...] = carry_ref[...].astype(BF)
            row = lax.broadcasted_iota(jnp.int32, (BLOCK, BLOCK), 0)
            col = lax.broadcasted_iota(jnp.int32, (BLOCK, BLOCK), 1)
            for g in range(GROUPS):
                dw_ref[g] = jnp.where(col <= row, dw_ref[g], 0.0)
            bucket = bucket_ref[...]
            for h in range(N_Q):
                dsink_ref[0, h] = jnp.sum(dsink_acc[h])
                dbh = dbias_ref[h]
                for b in range(REL_BUCKETS):
                    dtable_ref[b, h] = jnp.sum(jnp.where(bucket == b, dbh, 0.0))

    cl = lambda i: jnp.minimum(i, nb - 1)
    const = lambda shape: pl.BlockSpec(shape, lambda i: (0,) * len(shape))
    return _call(
        body,
        name="mixer_bwd",
        grid=(nb + 1,),
        in_specs=_mixer_specs(nb) + [pl.BlockSpec((BLOCK, D_MODEL), lambda i: (cl(i), 0))],
        out_specs=[
            pl.BlockSpec((BLOCK, D_MAIN), lambda i: (cl(i), 0)),
            pl.BlockSpec((BLOCK, 2 * D_KV), lambda i: (jnp.maximum(i - 1, 0), 0)),
            const((1, D_GMLP)),
            const((GROUPS, BLOCK, BLOCK)),
            const((GROUPS, BLOCK, 1)),
            pl.BlockSpec(memory_space=pltpu.SMEM),
            pl.BlockSpec(memory_space=pltpu.SMEM),
        ],
        out_shape=[
            jax.ShapeDtypeStruct((s, D_MAIN), BF),
            jax.ShapeDtypeStruct((s, 2 * D_KV), BF),
            jax.ShapeDtypeStruct((1, D_GMLP), F32),
            jax.ShapeDtypeStruct((GROUPS, BLOCK, BLOCK), F32),
            jax.ShapeDtypeStruct((GROUPS, BLOCK, 1), F32),
            jax.ShapeDtypeStruct((1, N_Q), F32),
            jax.ShapeDtypeStruct((REL_BUCKETS, N_Q), F32),
        ],
        scratch_shapes=[
            pltpu.VMEM((N_Q, BLOCK, 2 * BLOCK), F32),
            pltpu.VMEM((N_Q, BLOCK, 2 * BLOCK), F32),
            pltpu.VMEM((BLOCK, 2 * D_KV), F32),
            pltpu.VMEM((N_Q, BLOCK, 1), F32),
        ],
        compiler_params=_params(("arbitrary",)),
    )(zuv, qkv, qkv, qkv, gv, w_sp, b_sp, sinks, table, bucket, dmix)


def _in_bwd(x, dh1, dzm, dkv, g1, w_in):
    s = x.shape[0]
    tm = min(TOK_TILE, s)

    def body(x_ref, dh1_ref, dzm_ref, dkv_ref, g_ref, w_ref, dx_ref, dg_ref):
        @pl.when(pl.program_id(0) == 0)
        def _():
            dg_ref[...] = jnp.zeros_like(dg_ref)

        dhn = _nt(dzm_ref[...], w_ref[:, :D_MAIN]) + _nt(dkv_ref[...], w_ref[:, D_MAIN:])
        xv = x_ref[...]
        r = _rms_scale(xv)
        dg_ref[...] += jnp.sum(dhn * (xv * r), axis=0, keepdims=True)
        dx_ref[...] = dh1_ref[...] + _rms_bwd(dhn * g_ref[...], xv, r)

    tile = lambda cols: pl.BlockSpec((tm, cols), lambda i: (i, 0))
    row = pl.BlockSpec((1, D_MODEL), lambda i: (0, 0))
    return _call(
        body,
        name="in_bwd",
        grid=(s // tm,),
        in_specs=[tile(D_MODEL), tile(D_MODEL), tile(D_MAIN), tile(2 * D_KV), row, pl.BlockSpec((D_MODEL, D_IN), lambda i: (0, 0))],
        out_specs=[tile(D_MODEL), row],
        out_shape=[jax.ShapeDtypeStruct((s, D_MODEL), F32), jax.ShapeDtypeStruct((1, D_MODEL), F32)],
        compiler_params=_params(("arbitrary",)),
    )(x, dh1, dzm, dkv, g1, w_in)


def _wgrad(a, b, tm, tn, name, peer_cols=False):
    s, m = a.shape
    n = b.shape[1]

    def body(a_ref, b_ref, o_ref, at_ref):
        @pl.when(pl.program_id(1) == 0)
        def _():
            at_ref[...] = a_ref[...].T

        r = _nn(at_ref[...], b_ref[...])
        if peer_cols:
            o_ref[0] = r
        else:
            o_ref[...] = r

    if peer_cols:
        out_spec = pl.BlockSpec((1, tm, tn), lambda i, j: (j, i, 0))
        out_shape = jax.ShapeDtypeStruct((n // tn, m, tn), F32)
    else:
        out_spec = pl.BlockSpec((tm, tn), lambda i, j: (i, j))
        out_shape = jax.ShapeDtypeStruct((m, n), F32)
    return _call(
        body,
        name=name,
        grid=(m // tm, n // tn),
        in_specs=[pl.BlockSpec((s, tm), lambda i, j: (0, i)), pl.BlockSpec((s, tn), lambda i, j: (0, j))],
        out_specs=out_spec,
        out_shape=out_shape,
        scratch_shapes=[pltpu.VMEM((tm, s), BF)],
        compiler_params=_params(("arbitrary", "arbitrary")),
    )(a, b)


def _adamw_math(w, g, m, v):
    m_new = ADAM_B1 * m + (1.0 - ADAM_B1) * g
    v_new = ADAM_B2 * v + (1.0 - ADAM_B2) * jnp.square(g)
    m_hat = m_new / (1.0 - ADAM_B1 ** ADAM_STEP)
    v_hat = v_new / (1.0 - ADAM_B2 ** ADAM_STEP)
    delta = -ADAM_LR * (m_hat / (jnp.sqrt(v_hat) + ADAM_EPS) + ADAM_WD * w)
    return delta, m_new, v_new


def _adamw(w, g, m, v, name):
    r, c = w.shape
    tr = min(r, 256)

    def body(w_ref, g_ref, m_ref, v_ref, d_ref, mo_ref, vo_ref):
        d_ref[...], mo_ref[...], vo_ref[...] = _adamw_math(w_ref[...], g_ref[...], m_ref[...], v_ref[...])

    spec = pl.BlockSpec((tr, c), lambda i: (i, 0))
    return _call(
        body,
        name=name,
        grid=(r // tr,),
        in_specs=[spec] * 4,
        out_specs=[spec] * 3,
        out_shape=[jax.ShapeDtypeStruct((r, c), F32)] * 3,
        compiler_params=_params(("arbitrary",)),
    )(w, g, m, v)


def _adamw_small(parts, w, m, v):
    r, c = w.shape

    def body(p_ref, w_ref, m_ref, v_ref, g_ref, d_ref, mo_ref, vo_ref):
        g = p_ref[0]
        for d in range(1, N_DEV):
            g = g + p_ref[d]
        g_ref[...] = g
        d_ref[...], mo_ref[...], vo_ref[...] = _adamw_math(w_ref[...], g, m_ref[...], v_ref[...])

    return _call(
        body,
        name="adamw_small",
        out_shape=[jax.ShapeDtypeStruct((r, c), F32)] * 4,
        compiler_params=_params(),
    )(parts, w, m, v)


def _place():
    x, y, c = lax.axis_index("x"), lax.axis_index("y"), lax.axis_index("c")
    return x, y, c, [(1 - x, y), (x, 1 - y), (1 - x, 1 - y)]


def _dev_index(px, py, pc):
    return 4 * px + 2 * py + pc


def _all_gather(shards, out_dtype, name):
    k_n = len(shards)

    def body(*refs):
        ins, outs, stage = refs[:k_n], refs[k_n:2 * k_n], refs[2 * k_n:3 * k_n]
        send_sems, recv_sems, local_sems = refs[3 * k_n:]
        x, y, c, chips = _place()
        me, sibling = (x, y, c), (x, y, 1 - c)

        def copy(k, j, block, to, src=None):
            dst = outs[k].at[_dev_index(*block)]
            return pltpu.make_async_remote_copy(
                src_ref=dst if src is None else src, dst_ref=dst,
                send_sem=send_sems.at[k, j], recv_sem=recv_sems.at[k, j], device_id=to, device_id_type=MESH)

        mine, first = [], []
        for k in range(k_n):
            stage[k][...] = ins[k][...].astype(out_dtype)
            mine.append(pltpu.make_async_copy(stage[k], outs[k].at[_dev_index(*me)], local_sems.at[k]))
            mine[k].start()
            first.append(copy(k, 0, me, sibling, src=stage[k]))
            first += [copy(k, 1 + j, me, (*chip, c), src=stage[k]) for j, chip in enumerate(chips)]
        for cp in first:
            cp.start()
        passed = []
        for k in range(k_n):
            for j, chip in enumerate(chips):
                copy(k, 1 + j, (*chip, c), me).wait_recv()
                passed.append(copy(k, 4 + j, (*chip, c), sibling))
                passed[-1].start()
        for k in range(k_n):
            copy(k, 0, sibling, me).wait_recv()
            for j, chip in enumerate(chips):
                copy(k, 4 + j, (*chip, 1 - c), me).wait_recv()
        for cp in first + passed:
            cp.wait_send()
        for cp in mine:
            cp.wait()

    return _call(
        body,
        name=name,
        in_specs=[pl.BlockSpec(memory_space=pltpu.VMEM)] * k_n,
        out_specs=[pl.BlockSpec(memory_space=pl.ANY)] * k_n,
        out_shape=[jax.ShapeDtypeStruct((N_DEV,) + sh.shape, out_dtype) for sh in shards],
        scratch_shapes=[pltpu.VMEM(sh.shape, out_dtype) for sh in shards]
        + [pltpu.SemaphoreType.DMA((k_n, 7)), pltpu.SemaphoreType.DMA((k_n, 7)), pltpu.SemaphoreType.DMA((k_n,))],
        compiler_params=_params(),
    )(*shards)


HBM_SPEC = pl.BlockSpec(memory_space=pltpu.HBM)
SEM_SPEC = pl.BlockSpec(memory_space=pltpu.SEMAPHORE)
ANY_SPEC = pl.BlockSpec(memory_space=pl.ANY)
DATAFLOW = pltpu.SideEffectType.DATAFLOW_SIDE_EFFECTING


def _hbm(a):
    return pltpu.with_memory_space_constraint(a, pltpu.HBM)


def _prep_weights(shards):
    k_n = len(shards)

    def body(*refs):
        ins, outs, stage, sems = refs[:k_n], refs[k_n:2 * k_n], refs[2 * k_n:3 * k_n], refs[3 * k_n]
        x, y, c, _ = _place()
        copies = []
        for k in range(k_n):
            stage[k][...] = ins[k][...].astype(BF)
            copies.append(pltpu.make_async_copy(stage[k], outs[k].at[_dev_index(x, y, c)], sems.at[k]))
            copies[k].start()
        for cp in copies:
            cp.wait()

    return _call(
        body,
        name="prep_weights",
        in_specs=[pl.BlockSpec(memory_space=pltpu.VMEM)] * k_n,
        out_specs=[ANY_SPEC] * k_n,
        out_shape=[jax.ShapeDtypeStruct((N_DEV,) + sh.shape, BF) for sh in shards],
        scratch_shapes=[pltpu.VMEM(sh.shape, BF) for sh in shards] + [pltpu.SemaphoreType.DMA((k_n,))],
        compiler_params=_params(),
    )(*shards)


def _gather_inplace(lands, name):
    k_n = len(lands)

    def body(*refs):
        outs = refs[k_n:2 * k_n]
        send_sems, recv_sems = refs[2 * k_n:]
        x, y, c, chips = _place()
        me, sibling = (x, y, c), (x, y, 1 - c)

        def copy(k, j, block, to):
            blk = outs[k].at[_dev_index(*block)]
            return pltpu.make_async_remote_copy(
                src_ref=blk, dst_ref=blk, send_sem=send_sems.at[k, j], recv_sem=recv_sems.at[k, j], device_id=to, device_id_type=MESH)

        first = []
        for k in range(k_n):
            first.append(copy(k, 0, me, sibling))
            first += [copy(k, 1 + j, me, (*chip, c)) for j, chip in enumerate(chips)]
        for cp in first:
            cp.start()
        passed = []
        for k in range(k_n):
            for j, chip in enumerate(chips):
                copy(k, 1 + j, (*chip, c), me).wait_recv()
                passed.append(copy(k, 4 + j, (*chip, c), sibling))
                passed[-1].start()
        for k in range(k_n):
            copy(k, 0, sibling, me).wait_recv()
            for j, chip in enumerate(chips):
                copy(k, 4 + j, (*chip, 1 - c), me).wait_recv()
        for cp in first + passed:
            cp.wait_send()

    return _call(
        body,
        name=name,
        in_specs=[ANY_SPEC] * k_n,
        out_specs=[ANY_SPEC] * k_n,
        out_shape=[jax.ShapeDtypeStruct(a.shape, a.dtype) for a in lands],
        input_output_aliases={k: k for k in range(k_n)},
        scratch_shapes=[pltpu.SemaphoreType.DMA((k_n, 7)), pltpu.SemaphoreType.DMA((k_n, 7))],
        compiler_params=_params(),
    )(*lands)


def _ag_start(first, rest):
    lands = list(first) + list(rest)
    k_n, k_first = len(lands), len(first)

    def body(*refs):
        land = refs[:k_n]
        sems = refs[k_n:k_n + 4]
        token = refs[-1]
        x, y, c, chips = _place()
        targets = [(x, y, 1 - c)] + [(*chip, c) for chip in chips]
        for k in range(k_n):
            blk = land[k].at[_dev_index(x, y, c)]
            send_sems, recv_sems, base = (sems[0], sems[1], k) if k < k_first else (sems[2], sems[3], k - k_first)
            for j, to in enumerate(targets):
                pltpu.make_async_remote_copy(
                    src_ref=blk, dst_ref=blk, send_sem=send_sems.at[4 * base + j], recv_sem=recv_sems.at[4 * base + j],
                    device_id=to, device_id_type=MESH).start()
        token[...] = jnp.zeros_like(token)

    k_rest = k_n - k_first
    outs = pl.pallas_call(
        body,
        name="ag_start",
        in_specs=[HBM_SPEC] * k_n,
        out_specs=(SEM_SPEC, SEM_SPEC, SEM_SPEC, SEM_SPEC, *[HBM_SPEC] * k_n, pl.BlockSpec(memory_space=pltpu.VMEM)),
        out_shape=(pltpu.SemaphoreType.DMA((4 * k_first,)), pltpu.SemaphoreType.DMA((4 * k_first,)),
                   pltpu.SemaphoreType.DMA((4 * k_rest,)), pltpu.SemaphoreType.DMA((4 * k_rest,)),
                   *[pltpu.HBM(a.shape, a.dtype) for a in lands], jax.ShapeDtypeStruct((8, LANES), F32)),
        input_output_aliases={k: 4 + k for k in range(k_n)},
        compiler_params=pltpu.CompilerParams(has_side_effects=DATAFLOW),
    )(*[_hbm(a) for a in lands])
    flying = list(outs[4:4 + k_n])
    return (outs[0], outs[1], flying[:k_first]), (outs[2], outs[3], flying[k_first:]), outs[-1]


def _ag_mid(lands, send_sems, recv_sems, after, name):
    k_n = len(lands)

    def body(*refs):
        land = refs[:k_n]
        send1, recv1 = refs[k_n], refs[k_n + 1]
        fwd_send, fwd_recv = refs[-2], refs[-1]
        x, y, c, chips = _place()
        sources = [(x, y, 1 - c)] + [(*chip, c) for chip in chips]
        for k in range(k_n):
            mine = land[k].at[_dev_index(x, y, c)]
            for j, frm in enumerate(sources):
                got = land[k].at[_dev_index(*frm)]
                cp = pltpu.make_async_remote_copy(
                    src_ref=mine, dst_ref=got, send_sem=send1.at[4 * k + j], recv_sem=recv1.at[4 * k + j], device_id=frm, device_id_type=MESH)
                cp.wait_send()
                cp.wait_recv()
                if j >= 1:
                    pltpu.make_async_remote_copy(
                        src_ref=got, dst_ref=got, send_sem=fwd_send.at[3 * k + j - 1], recv_sem=fwd_recv.at[3 * k + j - 1],
                        device_id=(x, y, 1 - c), device_id_type=MESH).start()

    outs = pl.pallas_call(
        body,
        name=name,
        in_specs=[HBM_SPEC] * k_n + [SEM_SPEC, SEM_SPEC, ANY_SPEC],
        out_specs=(*[HBM_SPEC] * k_n, SEM_SPEC, SEM_SPEC),
        out_shape=(*[pltpu.HBM(a.shape, a.dtype) for a in lands], pltpu.SemaphoreType.DMA((3 * k_n,)), pltpu.SemaphoreType.DMA((3 * k_n,))),
        input_output_aliases={k: k for k in range(k_n)},
        compiler_params=pltpu.CompilerParams(has_side_effects=DATAFLOW),
    )(*lands, send_sems, recv_sems, after)
    return list(outs[:k_n]), outs[-2], outs[-1]


def _ag_end(lands, fwd_send, fwd_recv, name):
    k_n = len(lands)

    def body(*refs):
        land = refs[:k_n]
        fsend, frecv = refs[k_n], refs[k_n + 1]
        x, y, c, chips = _place()
        for k in range(k_n):
            for j, chip in enumerate(chips):
                cp = pltpu.make_async_remote_copy(
                    src_ref=land[k].at[_dev_index(*chip, c)], dst_ref=land[k].at[_dev_index(*chip, 1 - c)],
                    send_sem=fsend.at[3 * k + j], recv_sem=frecv.at[3 * k + j], device_id=(x, y, 1 - c), device_id_type=MESH)
                cp.wait_send()
                cp.wait_recv()

    outs = pl.pallas_call(
        body,
        name=name,
        in_specs=[HBM_SPEC] * k_n + [SEM_SPEC, SEM_SPEC],
        out_specs=tuple([HBM_SPEC] * k_n),
        out_shape=tuple(pltpu.HBM(a.shape, a.dtype) for a in lands),
        input_output_aliases={k: k for k in range(k_n)},
        compiler_params=pltpu.CompilerParams(has_side_effects=DATAFLOW),
    )(*lands, fwd_send, fwd_recv)
    return list(outs)


def _reduce_scatter(grads):
    k_n = len(grads)
    shapes = [g.shape[1:] for g in grads]

    def body(*refs):
        gs, outs, land1 = refs[:k_n], refs[k_n:2 * k_n], refs[2 * k_n:3 * k_n]
        send2, recv2 = refs[3 * k_n:4 * k_n], refs[4 * k_n:5 * k_n]
        s1_send, s1_recv, s2_send, s2_recv = refs[5 * k_n:]
        x, y, c, others = _place()
        sibling = (x, y, 1 - c)
        chips = [(x, y)] + others

        def copy1(k, j):
            return pltpu.make_async_remote_copy(
                src_ref=gs[k].at[_dev_index(*chips[j], 1 - c)], dst_ref=land1[k].at[j],
                send_sem=s1_send.at[k, j], recv_sem=s1_recv.at[k, j], device_id=sibling, device_id_type=MESH)

        def copy2(k, j):
            return pltpu.make_async_remote_copy(
                src_ref=send2[k].at[j - 1], dst_ref=recv2[k].at[j - 1],
                send_sem=s2_send.at[k, j - 1], recv_sem=s2_recv.at[k, j - 1], device_id=(*chips[j], c), device_id_type=MESH)

        for k in range(k_n):
            for j in range(4):
                copy1(k, j).start()

        def chip_sum(k, j, a_ref, b_ref, sems):
            ca = pltpu.make_async_copy(gs[k].at[_dev_index(*chips[j], c)], a_ref, sems.at[0])
            cb = pltpu.make_async_copy(land1[k].at[j], b_ref, sems.at[1])
            ca.start()
            cb.start()
            ca.wait()
            cb.wait()
            return a_ref[...] + b_ref[...]

        for k in range(k_n):
            def stage(a_ref, b_ref, sems, k=k):
                for j in (1, 2, 3, 0):
                    copy1(k, j).wait_recv()
                    total = chip_sum(k, j, a_ref, b_ref, sems)
                    if j == 0:
                        outs[k][...] = total
                    else:
                        send2[k][j - 1] = total.astype(BF)
                        copy2(k, j).start()

            pl.run_scoped(stage, pltpu.VMEM(shapes[k], F32), pltpu.VMEM(shapes[k], F32), pltpu.SemaphoreType.DMA((2,)))

        for k in range(k_n):
            for j in (1, 2, 3):
                copy2(k, j).wait_recv()
                outs[k][...] += recv2[k][j - 1].astype(F32)
        for k in range(k_n):
            for j in range(4):
                copy1(k, j).wait_send()
            for j in (1, 2, 3):
                copy2(k, j).wait_send()

    res = _call(
        body,
        name="reduce_scatter",
        in_specs=[pl.BlockSpec(memory_space=pl.ANY)] * k_n,
        out_specs=[pl.BlockSpec(memory_space=pltpu.VMEM)] * k_n + [pl.BlockSpec(memory_space=pl.ANY)] * k_n,
        out_shape=[jax.ShapeDtypeStruct(sh, F32) for sh in shapes] + [jax.ShapeDtypeStruct((4,) + sh, F32) for sh in shapes],
        scratch_shapes=[pltpu.VMEM((3,) + sh, BF) for sh in shapes] * 2
        + [pltpu.SemaphoreType.DMA((k_n, 4)), pltpu.SemaphoreType.DMA((k_n, 4)),
           pltpu.SemaphoreType.DMA((k_n, 3)), pltpu.SemaphoreType.DMA((k_n, 3))],
        compiler_params=_params(),
    )(*grads)
    return res[:k_n]


SMALL = ("norm1_gain", "gmlp_v_gain", "w_spatial", "b_spatial", "attn_sinks", "rel_bias_table", "norm2_gain", "final_gain")
LANES = 128


def _pack_small(arrays):
    rows = []
    for a in arrays:
        flat = a.reshape(-1)
        pad = (-flat.shape[0]) % LANES
        rows.append(jnp.pad(flat, (0, pad)).reshape(-1, LANES))
    packed = jnp.concatenate(rows, axis=0)
    return jnp.pad(packed, ((0, (-packed.shape[0]) % 8), (0, 0)))


def _unpack_small(packed, like):
    out, r = [], 0
    for a in like:
        n = a.size
        nr = -(-n // LANES)
        out.append(packed[r:r + nr].reshape(-1)[:n].reshape(a.shape))
        r += nr
    return out


def _local_step(x, p, t, small, w_in, gather_rest):
    g1, gv, w_sp, b_sp, sinks, table, g2, gf = (small[n] for n in SMALL)
    bucket = jnp.asarray(_bucket_table())
    b_col = b_sp.reshape(GROUPS, BLOCK, 1)
    gf_row = gf.reshape(1, D_MODEL)

    zuv, qkv, hn1 = _in_proj(x, g1, w_in)
    mix = _mixer_fwd(zuv, qkv, gv, w_sp[0], b_col, sinks, table, bucket)
    w_out, w_ff1, w_ff2, w_gate, w_proj = gather_rest(mix)
    (loss, d_g2, d_gf, dh1, dh1b, dmix, hn2, a, df, h2, dh2, dgl, dpp) = _tail(x, mix, p, t, g2, gf_row, w_out, w_ff1, w_ff2, w_gate, w_proj)
    dzm, dkv, d_gv, d_wsp, d_bsp, d_sinks, d_table = _mixer_bwd(zuv, qkv, dmix, gv, w_sp[0], b_col, sinks, table, bucket)
    dx, d_g1 = _in_bwd(x, dh1, dzm, dkv, g1, w_in)

    gw_in = jnp.concatenate([_wgrad(hn1, dzm, 512, 512, "wgrad_in_main"), _wgrad(hn1, dkv, 512, 256, "wgrad_in_kv")], axis=1)
    gw_in = gw_in.reshape(D_MODEL, N_DEV, D_IN // N_DEV).transpose(1, 0, 2)
    gw_out = _wgrad(mix, dh1b, 512, 1024, "wgrad_out").reshape(N_DEV, D_MODEL // N_DEV, D_MODEL)
    gw_ff1 = _wgrad(hn2, df, 512, D_FF // N_DEV, "wgrad_ff1", peer_cols=True)
    gw_ff2 = _wgrad(a, dh2, 512, 1024, "wgrad_ff2").reshape(N_DEV, D_FF // N_DEV, D_MODEL)
    gw_proj = _wgrad(p.astype(BF), dpp, PLE_DIM, D_MODEL // N_DEV, "wgrad_proj", peer_cols=True)
    gw_gate = _wgrad(h2, dgl, 512, 1024, "wgrad_gate").reshape(N_DEV, D_MODEL // N_DEV, D_MODEL)

    small_grads = dict(
        norm1_gain=d_g1, gmlp_v_gain=d_gv, w_spatial=d_wsp.reshape(w_sp.shape), b_spatial=d_bsp.reshape(b_sp.shape),
        attn_sinks=d_sinks, rel_bias_table=d_table, norm2_gain=d_g2, final_gain=d_gf.reshape(gf.shape))
    return loss, dx, small_grads, [gw_in, gw_out, gw_ff1, gw_ff2, gw_proj, gw_gate]


def kernel(x, p, norm1_gain, w_in, gmlp_v_gain, w_spatial, b_spatial, attn_sinks, rel_bias_table, w_out, norm2_gain, w_ff1, w_ff2, w_ple_proj, w_ple_gate, final_gain, loss_target, m_norm1_gain, m_w_in, m_gmlp_v_gain, m_w_spatial, m_b_spatial, m_attn_sinks, m_rel_bias_table, m_w_out, m_norm2_gain, m_w_ff1, m_w_ff2, m_w_ple_proj, m_w_ple_gate, m_final_gain, v_norm1_gain, v_w_in, v_gmlp_v_gain, v_w_spatial, v_b_spatial, v_attn_sinks, v_rel_bias_table, v_w_out, v_norm2_gain, v_w_ff1, v_w_ff2, v_w_ple_proj, v_w_ple_gate, v_final_gain):
    args = dict(locals())
    s = x.shape[1]
    big = ("w_in", "w_out", "w_ff1", "w_ff2", "w_ple_proj", "w_ple_gate")

    lands = _prep_weights([args[n][0] for n in big])
    (send_in, recv_in, fly_in), (send_rest, recv_rest, fly_rest), token = _ag_start(lands[:1], lands[1:])
    mid_in, fwd_send_in, fwd_recv_in = _ag_mid(fly_in, send_in, recv_in, token, "ag_mid_w_in")
    g_in = _ag_end(mid_in, fwd_send_in, fwd_recv_in, "ag_end_w_in")[0]
    full_in = g_in.transpose(1, 0, 2).reshape(D_MODEL, D_IN)

    def gather_rest(after):
        mid, fwd_send, fwd_recv = _ag_mid(fly_rest, send_rest, recv_rest, after, "ag_mid_rest")
        g_out, g_ff1, g_ff2, g_proj, g_gate = _ag_end(mid, fwd_send, fwd_recv, "ag_end_rest")
        return (g_out.reshape(D_MODEL, D_MODEL), g_ff1, g_ff2.reshape(D_FF, D_MODEL), g_gate.reshape(D_MODEL, D_MODEL),
                g_proj.transpose(1, 0, 2).reshape(PLE_DIM, D_MODEL))

    small = {n: args[n] for n in SMALL}
    loss, dx, small_grads, big_grads = _local_step(
        x.reshape(s, D_MODEL), p.reshape(s, PLE_DIM), loss_target.reshape(s, D_MODEL), small, full_in, gather_rest)

    loss = lax.psum(loss[0, 0], ("x", "y", "c"))
    grad_x = dx.reshape(x.shape)

    big_g = _reduce_scatter(big_grads)
    grads, deltas, new_m, new_v = {}, {}, {}, {}
    for n, g in zip(big, big_g):
        d, mo, vo = _adamw(args[n][0], g, args["m_" + n][0], args["v_" + n][0], "adamw_" + n)
        grads[n], deltas[n], new_m[n], new_v[n] = g[None], d[None], mo[None], vo[None]

    like = [args[n] for n in SMALL]
    parts = _all_gather([_pack_small([small_grads[n] for n in SMALL])], F32, "gather_small_grads")[0]
    packed = _adamw_small(parts, _pack_small(like), _pack_small([args["m_" + n] for n in SMALL]),
                          _pack_small([args["v_" + n] for n in SMALL]))
    for dst, pk in zip((grads, deltas, new_m, new_v), packed):
        for n, a in zip(SMALL, _unpack_small(pk, like)):
            dst[n] = a

    order = ("norm1_gain", "w_in", "gmlp_v_gain", "w_spatial", "b_spatial", "attn_sinks", "rel_bias_table", "w_out",
             "norm2_gain", "w_ff1", "w_ff2", "w_ple_proj", "w_ple_gate", "final_gain")
    return (loss, grad_x, *[grads[n] for n in order], *[deltas[n] for n in order],
            *[new_m[n] for n in order], *[new_v[n] for n in order])
```

```python
import functools
import math

import numpy as np
import jax
import jax.numpy as jnp
from jax import lax
from jax.experimental import pallas as pl
from jax.experimental.pallas import tpu as pltpu

F32 = jnp.float32
BF = jnp.bfloat16
MESH = pl.DeviceIdType.MESH
N_DEV = 8

D_MODEL = 1024
PLE_DIM = 256
D_GMLP = 512
GROUPS = 4
GDIM = 128
BLOCK = 128
D_ATTN = 512
HEAD_DIM = 64
N_Q = 8
Q_PER_KV = 4
D_KV = 128
D_FF = 4096
D_IN = 1792
D_MAIN = 2 * D_GMLP + D_ATTN
REL_BUCKETS = 32
EPS = 1e-6
NEG_INF = -1e30
SCALE = HEAD_DIM ** -0.5
GELU_C = math.sqrt(2.0 / math.pi)
GELU_A = 0.044715

ADAM_LR = 0.001
ADAM_B1 = 0.9
ADAM_B2 = 0.999
ADAM_EPS = 1e-08
ADAM_WD = 0.01
ADAM_STEP = 10

V7X_VMEM_LIMIT = 60000 * 1024
TOK_TILE = 256


def _call(body, after=None, **kw):
    if after is None:
        return pl.pallas_call(body, **kw)
    n_in = len(kw["in_specs"])

    def ordered(*refs):
        body(*refs[:n_in], *refs[n_in + 1:])

    kw["in_specs"] = list(kw["in_specs"]) + [pl.BlockSpec(memory_space=pl.ANY)]
    fn = pl.pallas_call(ordered, **kw)
    return lambda *operands: fn(*operands, after)


def _params(sem=None):
    if sem is None:
        return pltpu.CompilerParams(vmem_limit_bytes=V7X_VMEM_LIMIT)
    return pltpu.CompilerParams(dimension_semantics=sem, vmem_limit_bytes=V7X_VMEM_LIMIT)


def _nn(a, b):
    return jnp.dot(a, b, preferred_element_type=F32)


def _nt(a, b):
    return lax.dot_general(a, b, (((1,), (1,)), ((), ())), preferred_element_type=F32)


def _tn(a, b):
    return lax.dot_general(a, b, (((0,), (0,)), ((), ())), preferred_element_type=F32)


def _gelu_tanh(x):
    return jnp.tanh(GELU_C * (x + GELU_A * (x * x * x)))


def _gelu(x):
    return x * (0.5 * (1.0 + _gelu_tanh(x)))


def _gelu_grad(x):
    t = _gelu_tanh(x)
    return 0.5 * (1.0 + t) + 0.5 * x * (1.0 - t * t) * (GELU_C * (1.0 + 3.0 * GELU_A * (x * x)))


def _rms_scale(x):
    return lax.rsqrt(jnp.mean(x * x, axis=-1, keepdims=True) + EPS)


def _rms_bwd(dxn, x, r):
    return r * dxn - x * ((r * r * r) * jnp.mean(dxn * x, axis=-1, keepdims=True))


def _bucket_table():
    a = np.arange(BLOCK)[:, None]
    j = np.arange(2 * BLOCK)[None, :]
    n = BLOCK + a - j
    valid = (n >= 0) & (n < BLOCK)
    nc = np.maximum(n, 0)
    max_exact = REL_BUCKETS // 2
    nf = np.maximum(nc, 1).astype(np.float32)
    large = max_exact + (
        np.log(nf / np.float32(max_exact)) / np.float32(math.log(BLOCK / max_exact)) * np.float32(REL_BUCKETS - max_exact)
    ).astype(np.int32)
    large = np.minimum(large, REL_BUCKETS - 1)
    bucket = np.where(nc < max_exact, nc, large)
    return np.where(valid, bucket, -1).astype(np.int32)


def _in_proj(x, g1, w_in):
    s = x.shape[0]
    tm = min(TOK_TILE, s)

    def body(x_ref, g_ref, w_ref, zuv_ref, qkv_ref, hn_ref):
        xv = x_ref[...]
        hn = ((xv * _rms_scale(xv)) * g_ref[...]).astype(BF)
        hn_ref[...] = hn
        z = _nn(hn, w_ref[...])
        zuv_ref[...] = z[:, : 2 * D_GMLP]
        qkv_ref[...] = z[:, 2 * D_GMLP:].astype(BF)

    return _call(
        body,
        name="in_proj",
        grid=(s // tm,),
        in_specs=[
            pl.BlockSpec((tm, D_MODEL), lambda i: (i, 0)),
            pl.BlockSpec((1, D_MODEL), lambda i: (0, 0)),
            pl.BlockSpec((D_MODEL, D_IN), lambda i: (0, 0)),
        ],
        out_specs=[
            pl.BlockSpec((tm, 2 * D_GMLP), lambda i: (i, 0)),
            pl.BlockSpec((tm, D_ATTN + 2 * D_KV), lambda i: (i, 0)),
            pl.BlockSpec((tm, D_MODEL), lambda i: (i, 0)),
        ],
        out_shape=[
            jax.ShapeDtypeStruct((s, 2 * D_GMLP), F32),
            jax.ShapeDtypeStruct((s, D_ATTN + 2 * D_KV), BF),
            jax.ShapeDtypeStruct((s, D_MODEL), BF),
        ],
        compiler_params=_params(("arbitrary",)),
    )(x, g1, w_in)


def _build_bias(bias_ref, bucket_ref, table_ref):
    bucket = bucket_ref[...]
    for h in range(N_Q):
        acc = jnp.zeros((BLOCK, 2 * BLOCK), F32)
        for b in range(REL_BUCKETS):
            acc = jnp.where(bucket == b, table_ref[b, h], acc)
        bias_ref[h] = acc


def _window_masks(i):
    row = lax.broadcasted_iota(jnp.int32, (BLOCK, BLOCK), 0)
    col = lax.broadcasted_iota(jnp.int32, (BLOCK, BLOCK), 1)
    return (col > row) & (i > 0), col <= row


def _tril_bf16(w_ref, g):
    row = lax.broadcasted_iota(jnp.int32, (BLOCK, BLOCK), 0)
    col = lax.broadcasted_iota(jnp.int32, (BLOCK, BLOCK), 1)
    return jnp.where(col <= row, w_ref[g], 0.0).astype(BF)


def _attn_probs(q_h, k_prev, k_cur, bias_h, sink, valid_prev, valid_cur):
    l_prev = jnp.where(valid_prev, _nt(q_h, k_prev) * SCALE + bias_h[:, :BLOCK], NEG_INF)
    l_cur = jnp.where(valid_cur, _nt(q_h, k_cur) * SCALE + bias_h[:, BLOCK:], NEG_INF)
    m = jnp.maximum(jnp.maximum(jnp.max(l_prev, axis=-1, keepdims=True), jnp.max(l_cur, axis=-1, keepdims=True)), sink)
    e_prev = jnp.exp(l_prev - m)
    e_cur = jnp.exp(l_cur - m)
    e_sink = jnp.exp(sink - m)
    denom = jnp.sum(e_prev, axis=-1, keepdims=True) + jnp.sum(e_cur, axis=-1, keepdims=True) + e_sink
    return e_prev / denom, e_cur / denom, e_sink / denom


def _mixer_specs(nb):
    cl = lambda i: jnp.minimum(i, nb - 1)
    return [
        pl.BlockSpec((BLOCK, 2 * D_GMLP), lambda i: (cl(i), 0)),
        pl.BlockSpec((BLOCK, D_ATTN), lambda i: (cl(i), 0)),
        pl.BlockSpec((BLOCK, 2 * D_KV), lambda i: (cl(i), D_ATTN // (2 * D_KV))),
        pl.BlockSpec((BLOCK, 2 * D_KV), lambda i: (jnp.maximum(cl(i) - 1, 0), D_ATTN // (2 * D_KV))),
        pl.BlockSpec((1, D_GMLP), lambda i: (0, 0)),
        pl.BlockSpec((GROUPS, BLOCK, BLOCK), lambda i: (0, 0, 0)),
        pl.BlockSpec((GROUPS, BLOCK, 1), lambda i: (0, 0, 0)),
        pl.BlockSpec(memory_space=pltpu.SMEM),
        pl.BlockSpec(memory_space=pltpu.SMEM),
        pl.BlockSpec((BLOCK, 2 * BLOCK), lambda i: (0, 0)),
    ]


def _mixer_fwd(zuv, qkv, gv, w_sp, b_sp, sinks, table, bucket):
    s = zuv.shape[0]
    nb = s // BLOCK

    def body(zuv_ref, q_ref, kvc_ref, kvp_ref, gv_ref, w_ref, b_ref, sink_ref, table_ref, bucket_ref, mix_ref, bias_ref):
        i = pl.program_id(0)

        @pl.when(i == 0)
        def _():
            _build_bias(bias_ref, bucket_ref, table_ref)

        for g in range(GROUPS):
            sl = slice(g * GDIM, (g + 1) * GDIM)
            u = _gelu(zuv_ref[:, sl])
            vg = _gelu(zuv_ref[:, D_GMLP + g * GDIM: D_GMLP + (g + 1) * GDIM])
            vn = ((vg * _rms_scale(vg)) * gv_ref[:, sl]).astype(BF)
            sv = _nn(_tril_bf16(w_ref, g), vn) + b_ref[g]
            mix_ref[:, sl] = (u * sv).astype(BF)

        valid_prev, valid_cur = _window_masks(i)
        for h in range(N_Q):
            kh = h // Q_PER_KV
            q_h = q_ref[:, h * HEAD_DIM:(h + 1) * HEAD_DIM]
            ksl = slice(kh * HEAD_DIM, (kh + 1) * HEAD_DIM)
            vsl = slice(D_KV + kh * HEAD_DIM, D_KV + (kh + 1) * HEAD_DIM)
            p_prev, p_cur, _ = _attn_probs(q_h, kvp_ref[:, ksl], kvc_ref[:, ksl], bias_ref[h], sink_ref[0, h], valid_prev, valid_cur)
            o = _nn(p_prev.astype(BF), kvp_ref[:, vsl]) + _nn(p_cur.astype(BF), kvc_ref[:, vsl])
            mix_ref[:, D_GMLP + h * HEAD_DIM: D_GMLP + (h + 1) * HEAD_DIM] = o.astype(BF)

    return _call(
        body,
        name="mixer_fwd",
        grid=(nb,),
        in_specs=_mixer_specs(nb),
        out_specs=pl.BlockSpec((BLOCK, D_MODEL), lambda i: (i, 0)),
        out_shape=jax.ShapeDtypeStruct((s, D_MODEL), BF),
        scratch_shapes=[pltpu.VMEM((N_Q, BLOCK, 2 * BLOCK), F32)],
        compiler_params=_params(("arbitrary",)),
    )(zuv, qkv, qkv, qkv, gv, w_sp, b_sp, sinks, table, bucket)


def _tail(x, mix, p, t, g2, gf, w_out, w_ff1, w_ff2, w_gate, w_proj):
    s = x.shape[0]
    tm = min(TOK_TILE, s)
    n_ff = w_ff1.shape[0]
    fc = D_FF // n_ff

    def body(x_ref, mix_ref, p_ref, t_ref, g2_ref, gf_ref, wo_ref, w1_ref, w2_ref, wg_ref, wp_ref,
             loss_ref, dg2_ref, dgf_ref, dh1_ref, dh1b_ref, dmix_ref, hn2_ref, a_ref, df_ref, h2_ref, dh2_ref, dgl_ref, dpp_ref, f_ref):
        i = pl.program_id(0)

        @pl.when(i == 0)
        def _():
            loss_ref[...] = jnp.zeros_like(loss_ref)
            dg2_ref[...] = jnp.zeros_like(dg2_ref)
            dgf_ref[...] = jnp.zeros_like(dgf_ref)

        h1 = x_ref[...] + _nn(mix_ref[...], wo_ref[...])
        r2 = _rms_scale(h1)
        hn2 = ((h1 * r2) * g2_ref[...]).astype(BF)
        hn2_ref[...] = hn2
        h2 = h1
        for c in range(n_ff):
            f = _nn(hn2, w1_ref[c])
            f_ref[:, c * fc:(c + 1) * fc] = f
            a = jnp.square(jnp.maximum(f, 0.0)).astype(BF)
            a_ref[:, c * fc:(c + 1) * fc] = a
            h2 = h2 + _nn(a, w2_ref[c * fc:(c + 1) * fc, :])
        h2b = h2.astype(BF)
        h2_ref[...] = h2b
        gate = jax.nn.sigmoid(_nn(h2b, wg_ref[...]))
        pp = _nn(p_ref[...].astype(BF), wp_ref[...])
        h3 = h2 + gate * pp
        rf = _rms_scale(h3)
        gf_v = gf_ref[...]
        err = (h3 * rf) * gf_v - t_ref[...]
        loss_ref[...] += jnp.sum(jnp.sum(err * err, axis=-1, keepdims=True), axis=0, keepdims=True) * (0.5 / D_MODEL)

        dy = err * (1.0 / D_MODEL)
        dgf_ref[...] += jnp.sum(dy * (h3 * rf), axis=0, keepdims=True)
        dh3 = _rms_bwd(dy * gf_v, h3, rf)
        dpp_ref[...] = (dh3 * gate).astype(BF)
        dgl = ((dh3 * pp) * (gate * (1.0 - gate))).astype(BF)
        dgl_ref[...] = dgl
        dh2 = dh3 + _nt(dgl, wg_ref[...])
        dh2b = dh2.astype(BF)
        dh2_ref[...] = dh2b
        dhn2 = jnp.zeros((tm, D_MODEL), F32)
        for c in range(n_ff):
            da = _nt(dh2b, w2_ref[c * fc:(c + 1) * fc, :])
            df = (da * (2.0 * jnp.maximum(f_ref[:, c * fc:(c + 1) * fc], 0.0))).astype(BF)
            df_ref[:, c * fc:(c + 1) * fc] = df
            dhn2 = dhn2 + _nt(df, w1_ref[c])
        dg2_ref[...] += jnp.sum(dhn2 * (h1 * r2), axis=0, keepdims=True)
        dh1 = dh2 + _rms_bwd(dhn2 * g2_ref[...], h1, r2)
        dh1_ref[...] = dh1
        dh1b = dh1.astype(BF)
        dh1b_ref[...] = dh1b
        dmix_ref[...] = _nt(dh1b, wo_ref[...])

    tile = lambda cols: pl.BlockSpec((tm, cols), lambda i: (i, 0))
    whole = lambda shape: pl.BlockSpec(shape, lambda i: (0,) * len(shape), pipeline_mode=pl.Buffered(1))
    row = pl.BlockSpec((1, D_MODEL), lambda i: (0, 0))
    act = lambda cols, dt: jax.ShapeDtypeStruct((s, cols), dt)
    return _call(
        body,
        name="tail",
        grid=(s // tm,),
        in_specs=[tile(D_MODEL), tile(D_MODEL), tile(PLE_DIM), tile(D_MODEL), row, row,
                  whole(w_out.shape), whole(w_ff1.shape), whole(w_ff2.shape), whole(w_gate.shape), whole(w_proj.shape)],
        out_specs=[pl.BlockSpec((1, 1), lambda i: (0, 0)), row, row, tile(D_MODEL), tile(D_MODEL), tile(D_MODEL), tile(D_MODEL), tile(D_FF),
                   tile(D_FF), tile(D_MODEL), tile(D_MODEL), tile(D_MODEL), tile(D_MODEL)],
        out_shape=[jax.ShapeDtypeStruct((1, 1), F32), jax.ShapeDtypeStruct((1, D_MODEL), F32), jax.ShapeDtypeStruct((1, D_MODEL), F32),
                   act(D_MODEL, F32), act(D_MODEL, BF), act(D_MODEL, F32), act(D_MODEL, BF), act(D_FF, BF), act(D_FF, BF), act(D_MODEL, BF),
                   act(D_MODEL, BF), act(D_MODEL, BF), act(D_MODEL, BF)],
        scratch_shapes=[pltpu.VMEM((tm, D_FF), F32)],
        compiler_params=_params(("arbitrary",)),
    )(x, mix, p, t, g2, gf, w_out, w_ff1, w_ff2, w_gate, w_proj)


def _mixer_bwd(zuv, qkv, dmix, gv, w_sp, b_sp, sinks, table, bucket, after=None):
    s = zuv.shape[0]
    nb = s // BLOCK

    def body(zuv_ref, q_ref, kvc_ref, kvp_ref, gv_ref, w_ref, b_ref, sink_ref, table_ref, bucket_ref, dmix_ref,
             dzm_ref, dkv_ref, dgv_ref, dw_ref, db_ref, dsink_ref, dtable_ref,
             bias_ref, dbias_ref, carry_ref, dsink_acc):
        i = pl.program_id(0)

        @pl.when(i == 0)
        def _():
            _build_bias(bias_ref, bucket_ref, table_ref)
            dbias_ref[...] = jnp.zeros_like(dbias_ref)
            carry_ref[...] = jnp.zeros_like(carry_ref)
            dsink_acc[...] = jnp.zeros_like(dsink_acc)
            dgv_ref[...] = jnp.zeros_like(dgv_ref)
            dw_ref[...] = jnp.zeros_like(dw_ref)
            db_ref[...] = jnp.zeros_like(db_ref)

        @pl.when(i < nb)
        def _():
            for g in range(GROUPS):
                sl = slice(g * GDIM, (g + 1) * GDIM)
                slv = slice(D_GMLP + g * GDIM, D_GMLP + (g + 1) * GDIM)
                zu = zuv_ref[:, sl]
                zv = zuv_ref[:, slv]
                u = _gelu(zu)
                vg = _gelu(zv)
                rg = _rms_scale(vg)
                vhat = vg * rg
                gain = gv_ref[:, sl]
                vn = (vhat * gain).astype(BF)
                w_g = _tril_bf16(w_ref, g)
                sv = _nn(w_g, vn) + b_ref[g]
                dmix_a = dmix_ref[:, sl]
                dsv = dmix_a * u
                dsvb = dsv.astype(BF)
                db_ref[g] += jnp.sum(dsv, axis=-1, keepdims=True)
                dw_ref[g] += _nt(dsvb, vn)
                dvn = _tn(w_g, dsvb)
                dgv_ref[:, sl] += jnp.sum(dvn * vhat, axis=0, keepdims=True)
                dvg = _rms_bwd(dvn * gain, vg, rg)
                dzm_ref[:, sl] = ((dmix_a * sv) * _gelu_grad(zu)).astype(BF)
                dzm_ref[:, slv] = (dvg * _gelu_grad(zv)).astype(BF)

            valid_prev, valid_cur = _window_masks(i)
            for kh in range(N_Q // Q_PER_KV):
                ksl = slice(kh * HEAD_DIM, (kh + 1) * HEAD_DIM)
                vsl = slice(D_KV + kh * HEAD_DIM, D_KV + (kh + 1) * HEAD_DIM)
                k_prev, k_cur = kvp_ref[:, ksl], kvc_ref[:, ksl]
                v_prev, v_cur = kvp_ref[:, vsl], kvc_ref[:, vsl]
                dk_prev = jnp.zeros((BLOCK, HEAD_DIM), F32)
                dk_cur = jnp.zeros((BLOCK, HEAD_DIM), F32)
                dv_prev = jnp.zeros((BLOCK, HEAD_DIM), F32)
                dv_cur = jnp.zeros((BLOCK, HEAD_DIM), F32)
                for gq in range(Q_PER_KV):
                    h = kh * Q_PER_KV + gq
                    q_h = q_ref[:, h * HEAD_DIM:(h + 1) * HEAD_DIM]
                    p_prev, p_cur, p_sink = _attn_probs(q_h, k_prev, k_cur, bias_ref[h], sink_ref[0, h], valid_prev, valid_cur)
                    do = dmix_ref[:, D_GMLP + h * HEAD_DIM: D_GMLP + (h + 1) * HEAD_DIM].astype(BF)
                    dp_prev = _nt(do, v_prev)
                    dp_cur = _nt(do, v_cur)
                    delta = jnp.sum(p_prev * dp_prev, axis=-1, keepdims=True) + jnp.sum(p_cur * dp_cur, axis=-1, keepdims=True)
                    ds_prev = p_prev * (dp_prev - delta)
                    ds_cur = p_cur * (dp_cur - delta)
                    dsink_acc[h] -= p_sink * delta
                    dbias_ref[h, :, :BLOCK] += ds_prev
                    dbias_ref[h, :, BLOCK:] += ds_cur
                    dsb_prev = ds_prev.astype(BF)
                    dsb_cur = ds_cur.astype(BF)
                    dq = (_nn(dsb_prev, k_prev) + _nn(dsb_cur, k_cur)) * SCALE
                    dzm_ref[:, 2 * D_GMLP + h * HEAD_DIM: 2 * D_GMLP + (h + 1) * HEAD_DIM] = dq.astype(BF)
                    dk_prev = dk_prev + _tn(dsb_prev, q_h)
                    dk_cur = dk_cur + _tn(dsb_cur, q_h)
                    dv_prev = dv_prev + _tn(p_prev.astype(BF), do)
                    dv_cur = dv_cur + _tn(p_cur.astype(BF), do)
                dkv_ref[:, ksl] = (carry_ref[:, ksl] + dk_prev * SCALE).astype(BF)
                dkv_ref[:, vsl] = (carry_ref[:, vsl] + dv_prev).astype(BF)
                carry_ref[:, ksl] = dk_cur * SCALE
                carry_ref[:, vsl] = dv_cur

        @pl.when(i == nb)
        def _():
            dkv_ref[...] = carry_ref[...].astype(BF)
            row = lax.broadcasted_iota(jnp.int32, (BLOCK, BLOCK), 0)
            col = lax.broadcasted_iota(jnp.int32, (BLOCK, BLOCK), 1)
            for g in range(GROUPS):
                dw_ref[g] = jnp.where(col <= row, dw_ref[g], 0.0)
            bucket = bucket_ref[...]
            for h in range(N_Q):
                dsink_ref[0, h] = jnp.sum(dsink_acc[h])
                dbh = dbias_ref[h]
                for b in range(REL_BUCKETS):
                    dtable_ref[b, h] = jnp.sum(jnp.where(bucket == b, dbh, 0.0))

    cl = lambda i: jnp.minimum(i, nb - 1)
    const = lambda shape: pl.BlockSpec(shape, lambda i: (0,) * len(shape))
    return _call(
        body,
        name="mixer_bwd",
        after=after,
        grid=(nb + 1,),
        in_specs=_mixer_specs(nb) + [pl.BlockSpec((BLOCK, D_MODEL), lambda i: (cl(i), 0))],
        out_specs=[
            pl.BlockSpec((BLOCK, D_MAIN), lambda i: (cl(i), 0)),
            pl.BlockSpec((BLOCK, 2 * D_KV), lambda i: (jnp.maximum(i - 1, 0), 0)),
            const((1, D_GMLP)),
            const((GROUPS, BLOCK, BLOCK)),
            const((GROUPS, BLOCK, 1)),
            pl.BlockSpec(memory_space=pltpu.SMEM),
            pl.BlockSpec(memory_space=pltpu.SMEM),
        ],
        out_shape=[
            jax.ShapeDtypeStruct((s, D_MAIN), BF),
            jax.ShapeDtypeStruct((s, 2 * D_KV), BF),
            jax.ShapeDtypeStruct((1, D_GMLP), F32),
            jax.ShapeDtypeStruct((GROUPS, BLOCK, BLOCK), F32),
            jax.ShapeDtypeStruct((GROUPS, BLOCK, 1), F32),
            jax.ShapeDtypeStruct((1, N_Q), F32),
            jax.ShapeDtypeStruct((REL_BUCKETS, N_Q), F32),
        ],
        scratch_shapes=[
            pltpu.VMEM((N_Q, BLOCK, 2 * BLOCK), F32),
            pltpu.VMEM((N_Q, BLOCK, 2 * BLOCK), F32),
            pltpu.VMEM((BLOCK, 2 * D_KV), F32),
            pltpu.VMEM((N_Q, BLOCK, 1), F32),
        ],
        compiler_params=_params(("arbitrary",)),
    )(zuv, qkv, qkv, qkv, gv, w_sp, b_sp, sinks, table, bucket, dmix)


def _in_bwd(x, dh1, dzm, dkv, g1, w_in, after=None):
    s = x.shape[0]
    tm = min(TOK_TILE, s)

    def body(x_ref, dh1_ref, dzm_ref, dkv_ref, g_ref, w_ref, dx_ref, dg_ref):
        @pl.when(pl.program_id(0) == 0)
        def _():
            dg_ref[...] = jnp.zeros_like(dg_ref)

        dhn = _nt(dzm_ref[...], w_ref[:, :D_MAIN]) + _nt(dkv_ref[...], w_ref[:, D_MAIN:])
        xv = x_ref[...]
        r = _rms_scale(xv)
        dg_ref[...] += jnp.sum(dhn * (xv * r), axis=0, keepdims=True)
        dx_ref[...] = dh1_ref[...] + _rms_bwd(dhn * g_ref[...], xv, r)

    tile = lambda cols: pl.BlockSpec((tm, cols), lambda i: (i, 0))
    row = pl.BlockSpec((1, D_MODEL), lambda i: (0, 0))
    return _call(
        body,
        name="in_bwd",
        after=after,
        grid=(s // tm,),
        in_specs=[tile(D_MODEL), tile(D_MODEL), tile(D_MAIN), tile(2 * D_KV), row, pl.BlockSpec((D_MODEL, D_IN), lambda i: (0, 0))],
        out_specs=[tile(D_MODEL), row],
        out_shape=[jax.ShapeDtypeStruct((s, D_MODEL), F32), jax.ShapeDtypeStruct((1, D_MODEL), F32)],
        compiler_params=_params(("arbitrary",)),
    )(x, dh1, dzm, dkv, g1, w_in)


def _wgrad(a, b, tm, tn, name, peer_cols=False, after=None):
    s, m = a.shape
    n = b.shape[1]

    def body(a_ref, b_ref, o_ref, at_ref):
        @pl.when(pl.program_id(1) == 0)
        def _():
            at_ref[...] = a_ref[...].T

        r = _nn(at_ref[...], b_ref[...])
        if peer_cols:
            o_ref[0] = r
        else:
            o_ref[...] = r

    if peer_cols:
        out_spec = pl.BlockSpec((1, tm, tn), lambda i, j: (j, i, 0))
        out_shape = jax.ShapeDtypeStruct((n // tn, m, tn), F32)
    else:
        out_spec = pl.BlockSpec((tm, tn), lambda i, j: (i, j))
        out_shape = jax.ShapeDtypeStruct((m, n), F32)
    return _call(
        body,
        name=name,
        after=after,
        grid=(m // tm, n // tn),
        in_specs=[pl.BlockSpec((s, tm), lambda i, j: (0, i)), pl.BlockSpec((s, tn), lambda i, j: (0, j))],
        out_specs=out_spec,
        out_shape=out_shape,
        scratch_shapes=[pltpu.VMEM((tm, s), BF)],
        compiler_params=_params(("arbitrary", "arbitrary")),
    )(a, b)


def _adamw_math(w, g, m, v):
    m_new = ADAM_B1 * m + (1.0 - ADAM_B1) * g
    v_new = ADAM_B2 * v + (1.0 - ADAM_B2) * jnp.square(g)
    m_hat = m_new / (1.0 - ADAM_B1 ** ADAM_STEP)
    v_hat = v_new / (1.0 - ADAM_B2 ** ADAM_STEP)
    delta = -ADAM_LR * (m_hat / (jnp.sqrt(v_hat) + ADAM_EPS) + ADAM_WD * w)
    return delta, m_new, v_new


def _final_adamw(part, recv, w, m, v, name):
    r, c = w.shape
    tr = min(r, 256)

    def body(p_ref, r_ref, w_ref, m_ref, v_ref, g_ref, d_ref, mo_ref, vo_ref):
        g = p_ref[...]
        for j in range(3):
            g = g + r_ref[j].astype(F32)
        g_ref[...] = g
        d_ref[...], mo_ref[...], vo_ref[...] = _adamw_math(w_ref[...], g, m_ref[...], v_ref[...])

    spec = pl.BlockSpec((tr, c), lambda i: (i, 0))
    return _call(
        body,
        name=name,
        grid=(r // tr,),
        in_specs=[spec, pl.BlockSpec((3, tr, c), lambda i: (0, i, 0)), spec, spec, spec],
        out_specs=[spec] * 4,
        out_shape=[jax.ShapeDtypeStruct((r, c), F32)] * 4,
        compiler_params=_params(("arbitrary",)),
    )(part, recv, w, m, v)


def _rs_sum(g, land, blocks, name):
    _, r, c = g.shape
    tr = min(r, 256)

    def body(blk_ref, g0_ref, g1_ref, g2_ref, g3_ref, l_ref, part_ref, send_ref):
        part_ref[...] = g0_ref[0] + l_ref[0]
        for j, gj_ref in enumerate((g1_ref, g2_ref, g3_ref)):
            send_ref[j] = (gj_ref[0] + l_ref[j + 1]).astype(BF)

    def pick(j):
        return pl.BlockSpec((1, tr, c), lambda i, blk: (blk[j], i, 0))

    return _call(
        body,
        name=name,
        grid_spec=pltpu.PrefetchScalarGridSpec(
            num_scalar_prefetch=1,
            grid=(r // tr,),
            in_specs=[pick(0), pick(1), pick(2), pick(3), pl.BlockSpec((4, tr, c), lambda i, blk: (0, i, 0))],
            out_specs=[pl.BlockSpec((tr, c), lambda i, blk: (i, 0)), pl.BlockSpec((3, tr, c), lambda i, blk: (0, i, 0))],
        ),
        out_shape=[jax.ShapeDtypeStruct((r, c), F32), jax.ShapeDtypeStruct((3, r, c), BF)],
        compiler_params=_params(("arbitrary",)),
    )(blocks, g, g, g, g, land)


def _adamw_small(parts, w, m, v):
    r, c = w.shape

    def body(p_ref, w_ref, m_ref, v_ref, g_ref, d_ref, mo_ref, vo_ref):
        g = p_ref[0]
        for d in range(1, N_DEV):
            g = g + p_ref[d]
        g_ref[...] = g
        d_ref[...], mo_ref[...], vo_ref[...] = _adamw_math(w_ref[...], g, m_ref[...], v_ref[...])

    return _call(
        body,
        name="adamw_small",
        out_shape=[jax.ShapeDtypeStruct((r, c), F32)] * 4,
        compiler_params=_params(),
    )(parts, w, m, v)


def _place():
    x, y, c = lax.axis_index("x"), lax.axis_index("y"), lax.axis_index("c")
    return x, y, c, [(1 - x, y), (x, 1 - y), (1 - x, 1 - y)]


def _dev_index(px, py, pc):
    return 4 * px + 2 * py + pc


def _all_gather(shards, out_dtype, name, after=None):
    k_n = len(shards)

    def body(*refs):
        ins, outs, stage = refs[:k_n], refs[k_n:2 * k_n], refs[2 * k_n:3 * k_n]
        send_sems, recv_sems, local_sems = refs[3 * k_n:]
        x, y, c, chips = _place()
        me, sibling = (x, y, c), (x, y, 1 - c)

        def copy(k, j, block, to, src=None):
            dst = outs[k].at[_dev_index(*block)]
            return pltpu.make_async_remote_copy(
                src_ref=dst if src is None else src, dst_ref=dst,
                send_sem=send_sems.at[k, j], recv_sem=recv_sems.at[k, j], device_id=to, device_id_type=MESH)

        mine, first = [], []
        for k in range(k_n):
            stage[k][...] = ins[k][...].astype(out_dtype)
            mine.append(pltpu.make_async_copy(stage[k], outs[k].at[_dev_index(*me)], local_sems.at[k]))
            mine[k].start()
            first.append(copy(k, 0, me, sibling, src=stage[k]))
            first += [copy(k, 1 + j, me, (*chip, c), src=stage[k]) for j, chip in enumerate(chips)]
        for cp in first:
            cp.start()
        passed = []
        for k in range(k_n):
            for j, chip in enumerate(chips):
                copy(k, 1 + j, (*chip, c), me).wait_recv()
                passed.append(copy(k, 4 + j, (*chip, c), sibling))
                passed[-1].start()
        for k in range(k_n):
            copy(k, 0, sibling, me).wait_recv()
            for j, chip in enumerate(chips):
                copy(k, 4 + j, (*chip, 1 - c), me).wait_recv()
        for cp in first + passed:
            cp.wait_send()
        for cp in mine:
            cp.wait()

    return _call(
        body,
        name=name,
        after=after,
        in_specs=[pl.BlockSpec(memory_space=pltpu.VMEM)] * k_n,
        out_specs=[pl.BlockSpec(memory_space=pl.ANY)] * k_n,
        out_shape=[jax.ShapeDtypeStruct((N_DEV,) + sh.shape, out_dtype) for sh in shards],
        scratch_shapes=[pltpu.VMEM(sh.shape, out_dtype) for sh in shards]
        + [pltpu.SemaphoreType.DMA((k_n, 7)), pltpu.SemaphoreType.DMA((k_n, 7)), pltpu.SemaphoreType.DMA((k_n,))],
        compiler_params=_params(),
    )(*shards)


HBM_SPEC = pl.BlockSpec(memory_space=pltpu.HBM)
SEM_SPEC = pl.BlockSpec(memory_space=pltpu.SEMAPHORE)
ANY_SPEC = pl.BlockSpec(memory_space=pl.ANY)
DATAFLOW = pltpu.SideEffectType.DATAFLOW_SIDE_EFFECTING


def _hbm(a):
    return pltpu.with_memory_space_constraint(a, pltpu.HBM)


def _prep_weights(shards):
    k_n = len(shards)

    def body(*refs):
        ins, outs, stage, sems = refs[:k_n], refs[k_n:2 * k_n], refs[2 * k_n:3 * k_n], refs[3 * k_n]
        x, y, c, _ = _place()
        copies = []
        for k in range(k_n):
            stage[k][...] = ins[k][...].astype(BF)
            copies.append(pltpu.make_async_copy(stage[k], outs[k].at[_dev_index(x, y, c)], sems.at[k]))
            copies[k].start()
        for cp in copies:
            cp.wait()

    return _call(
        body,
        name="prep_weights",
        in_specs=[pl.BlockSpec(memory_space=pltpu.VMEM)] * k_n,
        out_specs=[ANY_SPEC] * k_n,
        out_shape=[jax.ShapeDtypeStruct((N_DEV,) + sh.shape, BF) for sh in shards],
        scratch_shapes=[pltpu.VMEM(sh.shape, BF) for sh in shards] + [pltpu.SemaphoreType.DMA((k_n,))],
        compiler_params=_params(),
    )(*shards)


def _ag_start(first, rest):
    lands = list(first) + list(rest)
    k_n, k_first = len(lands), len(first)

    def body(*refs):
        land = refs[:k_n]
        sems = refs[k_n:k_n + 4]
        token = refs[-1]
        x, y, c, chips = _place()
        targets = [(x, y, 1 - c)] + [(*chip, c) for chip in chips]
        for k in range(k_n):
            blk = land[k].at[_dev_index(x, y, c)]
            send_sems, recv_sems, base = (sems[0], sems[1], k) if k < k_first else (sems[2], sems[3], k - k_first)
            for j, to in enumerate(targets):
                pltpu.make_async_remote_copy(
                    src_ref=blk, dst_ref=blk, send_sem=send_sems.at[4 * base + j], recv_sem=recv_sems.at[4 * base + j],
                    device_id=to, device_id_type=MESH).start()
        token[...] = jnp.zeros_like(token)

    k_rest = k_n - k_first
    outs = pl.pallas_call(
        body,
        name="ag_start",
        in_specs=[HBM_SPEC] * k_n,
        out_specs=(SEM_SPEC, SEM_SPEC, SEM_SPEC, SEM_SPEC, *[HBM_SPEC] * k_n, pl.BlockSpec(memory_space=pltpu.VMEM)),
        out_shape=(pltpu.SemaphoreType.DMA((4 * k_first,)), pltpu.SemaphoreType.DMA((4 * k_first,)),
                   pltpu.SemaphoreType.DMA((4 * k_rest,)), pltpu.SemaphoreType.DMA((4 * k_rest,)),
                   *[pltpu.HBM(a.shape, a.dtype) for a in lands], jax.ShapeDtypeStruct((8, LANES), F32)),
        input_output_aliases={k: 4 + k for k in range(k_n)},
        compiler_params=pltpu.CompilerParams(has_side_effects=DATAFLOW),
    )(*[_hbm(a) for a in lands])
    flying = list(outs[4:4 + k_n])
    return (outs[0], outs[1], flying[:k_first]), (outs[2], outs[3], flying[k_first:]), outs[-1]


def _ag_mid(lands, send_sems, recv_sems, after, name):
    k_n = len(lands)

    def body(*refs):
        land = refs[:k_n]
        send1, recv1 = refs[k_n], refs[k_n + 1]
        fwd_send, fwd_recv = refs[-2], refs[-1]
        x, y, c, chips = _place()
        sources = [(x, y, 1 - c)] + [(*chip, c) for chip in chips]
        for k in range(k_n):
            mine = land[k].at[_dev_index(x, y, c)]
            for j, frm in enumerate(sources):
                got = land[k].at[_dev_index(*frm)]
                cp = pltpu.make_async_remote_copy(
                    src_ref=mine, dst_ref=got, send_sem=send1.at[4 * k + j], recv_sem=recv1.at[4 * k + j], device_id=frm, device_id_type=MESH)
                cp.wait_send()
                cp.wait_recv()
                if j >= 1:
                    pltpu.make_async_remote_copy(
                        src_ref=got, dst_ref=got, send_sem=fwd_send.at[3 * k + j - 1], recv_sem=fwd_recv.at[3 * k + j - 1],
                        device_id=(x, y, 1 - c), device_id_type=MESH).start()

    outs = pl.pallas_call(
        body,
        name=name,
        in_specs=[HBM_SPEC] * k_n + [SEM_SPEC, SEM_SPEC, ANY_SPEC],
        out_specs=(*[HBM_SPEC] * k_n, SEM_SPEC, SEM_SPEC),
        out_shape=(*[pltpu.HBM(a.shape, a.dtype) for a in lands], pltpu.SemaphoreType.DMA((3 * k_n,)), pltpu.SemaphoreType.DMA((3 * k_n,))),
        input_output_aliases={k: k for k in range(k_n)},
        compiler_params=pltpu.CompilerParams(has_side_effects=DATAFLOW),
    )(*lands, send_sems, recv_sems, after)
    return list(outs[:k_n]), outs[-2], outs[-1]


def _ag_end(lands, fwd_send, fwd_recv, name):
    k_n = len(lands)

    def body(*refs):
        land = refs[:k_n]
        fsend, frecv = refs[k_n], refs[k_n + 1]
        x, y, c, chips = _place()
        for k in range(k_n):
            for j, chip in enumerate(chips):
                cp = pltpu.make_async_remote_copy(
                    src_ref=land[k].at[_dev_index(*chip, c)], dst_ref=land[k].at[_dev_index(*chip, 1 - c)],
                    send_sem=fsend.at[3 * k + j], recv_sem=frecv.at[3 * k + j], device_id=(x, y, 1 - c), device_id_type=MESH)
                cp.wait_send()
                cp.wait_recv()

    outs = pl.pallas_call(
        body,
        name=name,
        in_specs=[HBM_SPEC] * k_n + [SEM_SPEC, SEM_SPEC],
        out_specs=tuple([HBM_SPEC] * k_n),
        out_shape=tuple(pltpu.HBM(a.shape, a.dtype) for a in lands),
        input_output_aliases={k: k for k in range(k_n)},
        compiler_params=pltpu.CompilerParams(has_side_effects=DATAFLOW),
    )(*lands, fwd_send, fwd_recv)
    return list(outs)


def _chips4():
    x, y, c, others = _place()
    return x, y, c, [(x, y)] + others


def _route_sibling(j):
    x, y, c, chips = _chips4()
    return _dev_index(*chips[j], 1 - c), j, (x, y, 1 - c)


def _route_chips(j):
    x, y, c, chips = _chips4()
    return j, j, (*chips[j + 1], c)


def _xchg_copy(route, n, k, j, src, dst, send_sems, recv_sems):
    si, di, peer = route(j)
    return pltpu.make_async_remote_copy(
        src_ref=src[k].at[si], dst_ref=dst[k].at[di], send_sem=send_sems.at[n * k + j], recv_sem=recv_sems.at[n * k + j],
        device_id=peer, device_id_type=MESH)


def _xchg_start(srcs, slot_shapes, route, n, name):
    k_n = len(srcs)
    dsts = [lax.empty((n,) + tuple(sh), a.dtype) for sh, a in zip(slot_shapes, srcs)]

    def body(*refs):
        src, dst = refs[:k_n], refs[k_n:2 * k_n]
        send_sems, recv_sems, token = refs[2 * k_n], refs[2 * k_n + 1], refs[-1]
        for k in range(k_n):
            for j in range(n):
                _xchg_copy(route, n, k, j, src, dst, send_sems, recv_sems).start()
        token[...] = jnp.zeros_like(token)

    arrays = list(srcs) + dsts
    outs = pl.pallas_call(
        body,
        name=name,
        in_specs=[HBM_SPEC] * (2 * k_n),
        out_specs=(SEM_SPEC, SEM_SPEC, *[HBM_SPEC] * (2 * k_n), pl.BlockSpec(memory_space=pltpu.VMEM)),
        out_shape=(pltpu.SemaphoreType.DMA((n * k_n,)), pltpu.SemaphoreType.DMA((n * k_n,)),
                   *[pltpu.HBM(a.shape, a.dtype) for a in arrays], jax.ShapeDtypeStruct((8, LANES), F32)),
        input_output_aliases={i: 2 + i for i in range(2 * k_n)},
        compiler_params=pltpu.CompilerParams(has_side_effects=DATAFLOW),
    )(*[_hbm(a) for a in arrays])
    return outs[0], outs[1], list(outs[2:2 + k_n]), list(outs[2 + k_n:2 + 2 * k_n]), outs[-1]


def _xchg_wait(send_sems, recv_sems, srcs, dsts, route, n, after, name):
    k_n = len(srcs)

    def body(*refs):
        src, dst = refs[:k_n], refs[k_n:2 * k_n]
        send, recv = refs[2 * k_n], refs[2 * k_n + 1]
        for k in range(k_n):
            for j in range(n):
                cp = _xchg_copy(route, n, k, j, src, dst, send, recv)
                cp.wait_send()
                cp.wait_recv()

    arrays = list(srcs) + list(dsts)
    outs = pl.pallas_call(
        body,
        name=name,
        in_specs=[HBM_SPEC] * (2 * k_n) + [SEM_SPEC, SEM_SPEC, ANY_SPEC],
        out_specs=tuple([HBM_SPEC] * (2 * k_n)),
        out_shape=tuple(pltpu.HBM(a.shape, a.dtype) for a in arrays),
        input_output_aliases={i: i for i in range(2 * k_n)},
        compiler_params=pltpu.CompilerParams(has_side_effects=DATAFLOW),
    )(*arrays, send_sems, recv_sems, after)
    return list(outs[:k_n]), list(outs[k_n:])


SMALL = ("norm1_gain", "gmlp_v_gain", "w_spatial", "b_spatial", "attn_sinks", "rel_bias_table", "norm2_gain", "final_gain")
LANES = 128


def _pack_small(arrays):
    rows = []
    for a in arrays:
        flat = a.reshape(-1)
        pad = (-flat.shape[0]) % LANES
        rows.append(jnp.pad(flat, (0, pad)).reshape(-1, LANES))
    packed = jnp.concatenate(rows, axis=0)
    return jnp.pad(packed, ((0, (-packed.shape[0]) % 8), (0, 0)))


def _unpack_small(packed, like):
    out, r = [], 0
    for a in like:
        n = a.size
        nr = -(-n // LANES)
        out.append(packed[r:r + nr].reshape(-1)[:n].reshape(a.shape))
        r += nr
    return out


def _reduce_scatter_start(grads, names, tag, during):
    x, y, c, chips = _chips4()
    blocks = jnp.stack([_dev_index(*chip, c) for chip in chips]).astype(jnp.int32)
    shapes = [g.shape[1:] for g in grads]
    send1, recv1, src1, land1, token1 = _xchg_start(grads, shapes, _route_sibling, 4, f"rs_{tag}_swap_start")
    src1, land1 = _xchg_wait(send1, recv1, src1, land1, _route_sibling, 4, during(token1), f"rs_{tag}_swap_wait")
    sums = [_rs_sum(g, land, blocks, f"rs_{tag}_sum_{n}") for g, land, n in zip(src1, land1, names)]
    send2, recv2, src2, land2, token2 = _xchg_start([ps[1] for ps in sums], shapes, _route_chips, 3, f"rs_{tag}_chips_start")
    return ([ps[0] for ps in sums], send2, recv2, src2, land2), token2


def _reduce_scatter_end(state, after, tag):
    parts, send2, recv2, src2, land2 = state
    _, land2 = _xchg_wait(send2, recv2, src2, land2, _route_chips, 3, after, f"rs_{tag}_chips_wait")
    return parts, land2


def kernel(x, p, norm1_gain, w_in, gmlp_v_gain, w_spatial, b_spatial, attn_sinks, rel_bias_table, w_out, norm2_gain, w_ff1, w_ff2, w_ple_proj, w_ple_gate, final_gain, loss_target, m_norm1_gain, m_w_in, m_gmlp_v_gain, m_w_spatial, m_b_spatial, m_attn_sinks, m_rel_bias_table, m_w_out, m_norm2_gain, m_w_ff1, m_w_ff2, m_w_ple_proj, m_w_ple_gate, m_final_gain, v_norm1_gain, v_w_in, v_gmlp_v_gain, v_w_spatial, v_b_spatial, v_attn_sinks, v_rel_bias_table, v_w_out, v_norm2_gain, v_w_ff1, v_w_ff2, v_w_ple_proj, v_w_ple_gate, v_final_gain):
    args = dict(locals())
    s = x.shape[1]
    big = ("w_in", "w_out", "w_ff1", "w_ff2", "w_ple_proj", "w_ple_gate")

    x2, p2, t2 = x.reshape(s, D_MODEL), p.reshape(s, PLE_DIM), loss_target.reshape(s, D_MODEL)
    g1, gv, w_sp, b_sp, sinks, table, g2, gf = (args[n] for n in SMALL)
    bucket = jnp.asarray(_bucket_table())
    b_col = b_sp.reshape(GROUPS, BLOCK, 1)

    lands = _prep_weights([args[n][0] for n in big])
    (send_in, recv_in, fly_in), (send_rest, recv_rest, fly_rest), token = _ag_start(lands[:1], lands[1:])
    mid_in, fwd_send_in, fwd_recv_in = _ag_mid(fly_in, send_in, recv_in, token, "ag_mid_w_in")
    g_in = _ag_end(mid_in, fwd_send_in, fwd_recv_in, "ag_end_w_in")[0]
    full_in = g_in.transpose(1, 0, 2).reshape(D_MODEL, D_IN)

    zuv, qkv, hn1 = _in_proj(x2, g1, full_in)
    mix = _mixer_fwd(zuv, qkv, gv, w_sp[0], b_col, sinks, table, bucket)
    mid, fwd_send, fwd_recv = _ag_mid(fly_rest, send_rest, recv_rest, mix, "ag_mid_rest")
    g_out, g_ff1, g_ff2, g_proj, g_gate = _ag_end(mid, fwd_send, fwd_recv, "ag_end_rest")
    full_out, full_ff2, full_gate = g_out.reshape(D_MODEL, D_MODEL), g_ff2.reshape(D_FF, D_MODEL), g_gate.reshape(D_MODEL, D_MODEL)
    full_proj = g_proj.transpose(1, 0, 2).reshape(PLE_DIM, D_MODEL)

    (loss, d_g2, d_gf, dh1, dh1b, dmix, hn2, a, df, h2, dh2, dgl, dpp) = _tail(
        x2, mix, p2, t2, g2, gf.reshape(1, D_MODEL), full_out, g_ff1, full_ff2, full_gate, full_proj)
    loss = lax.psum(loss[0, 0], ("x", "y", "c"))

    gw_ff1 = _wgrad(hn2, df, 512, D_FF // N_DEV, "wgrad_ff1", peer_cols=True)
    gw_ff2 = _wgrad(a, dh2, 512, 1024, "wgrad_ff2").reshape(N_DEV, D_FF // N_DEV, D_MODEL)
    later = {}

    def while_ffn_swaps(token1):
        later["gate"] = _wgrad(h2, dgl, 512, 1024, "wgrad_gate", after=token1).reshape(N_DEV, D_MODEL // N_DEV, D_MODEL)
        later["proj"] = _wgrad(p2.astype(BF), dpp, PLE_DIM, D_MODEL // N_DEV, "wgrad_proj", peer_cols=True, after=token1)
        later["out"] = _wgrad(mix, dh1b, 512, 1024, "wgrad_out", after=token1).reshape(N_DEV, D_MODEL // N_DEV, D_MODEL)
        return later["out"]

    ffn_names = ("w_ff1", "w_ff2")
    ffn_state, ffn_token = _reduce_scatter_start([gw_ff1, gw_ff2], ffn_names, "ffn", while_ffn_swaps)

    dzm, dkv, d_gv, d_wsp, d_bsp, d_sinks, d_table = _mixer_bwd(
        zuv, qkv, dmix, gv, w_sp[0], b_col, sinks, table, bucket, after=ffn_token)
    gw_in = jnp.concatenate([_wgrad(hn1, dzm, 512, 512, "wgrad_in_main"), _wgrad(hn1, dkv, 512, 256, "wgrad_in_kv")], axis=1)
    gw_in = gw_in.reshape(D_MODEL, N_DEV, D_IN // N_DEV).transpose(1, 0, 2)

    def while_rest_swaps(token1):
        later["dx"], later["d_g1"] = _in_bwd(x2, dh1, dzm, dkv, g1, full_in, after=token1)
        return later["dx"]

    rest_names = ("w_in", "w_out", "w_ple_proj", "w_ple_gate")
    rest_state, rest_token = _reduce_scatter_start(
        [gw_in, later["out"], later["proj"], later["gate"]], rest_names, "rest", while_rest_swaps)
    grad_x = later["dx"].reshape(x.shape)

    small_grads = dict(
        norm1_gain=later["d_g1"], gmlp_v_gain=d_gv, w_spatial=d_wsp.reshape(w_sp.shape), b_spatial=d_bsp.reshape(b_sp.shape),
        attn_sinks=d_sinks, rel_bias_table=d_table, norm2_gain=d_g2, final_gain=d_gf.reshape(gf.shape))
    like = [args[n] for n in SMALL]
    parts = _all_gather([_pack_small([small_grads[n] for n in SMALL])], F32, "gather_small_grads", after=rest_token)[0]
    packed = _adamw_small(parts, _pack_small(like), _pack_small([args["m_" + n] for n in SMALL]),
                          _pack_small([args["v_" + n] for n in SMALL]))
    grads, deltas, new_m, new_v = {}, {}, {}, {}
    for dst, pk in zip((grads, deltas, new_m, new_v), packed):
        for n, arr in zip(SMALL, _unpack_small(pk, like)):
            dst[n] = arr

    after = packed[0]
    for state, names, tag in ((ffn_state, ffn_names, "ffn"), (rest_state, rest_names, "rest")):
        chip_parts, chip_recv = _reduce_scatter_end(state, after, tag)
        for n, part, recv in zip(names, chip_parts, chip_recv):
            g, d, mo, vo = _final_adamw(part, recv, args[n][0], args["m_" + n][0], args["v_" + n][0], "adamw_" + n)
            grads[n], deltas[n], new_m[n], new_v[n] = g[None], d[None], mo[None], vo[None]
            after = d

    order = ("norm1_gain", "w_in", "gmlp_v_gain", "w_spatial", "b_spatial", "attn_sinks", "rel_bias_table", "w_out",
             "norm2_gain", "w_ff1", "w_ff2", "w_ple_proj", "w_ple_gate", "final_gain")
    return (loss, grad_x, *[grads[n] for n in order], *[deltas[n] for n in order],
            *[new_m[n] for n in order], *[new_v[n] for n in order])
```

```python
import functools
import math

import numpy as np
import jax
import jax.numpy as jnp
from jax import lax
from jax.experimental import pallas as pl
from jax.experimental.pallas import tpu as pltpu

F32 = jnp.float32
BF = jnp.bfloat16
MESH = pl.DeviceIdType.MESH
N_DEV = 8

D_MODEL = 1024
PLE_DIM = 256
D_GMLP = 512
GROUPS = 4
GDIM = 128
BLOCK = 128
D_ATTN = 512
HEAD_DIM = 64
N_Q = 8
Q_PER_KV = 4
D_KV = 128
D_FF = 4096
D_IN = 1792
D_MAIN = 2 * D_GMLP + D_ATTN
REL_BUCKETS = 32
EPS = 1e-6
NEG_INF = -1e30
SCALE = HEAD_DIM ** -0.5
GELU_C = math.sqrt(2.0 / math.pi)
GELU_A = 0.044715

ADAM_LR = 0.001
ADAM_B1 = 0.9
ADAM_B2 = 0.999
ADAM_EPS = 1e-08
ADAM_WD = 0.01
ADAM_STEP = 10

V7X_VMEM_LIMIT = 60000 * 1024
TOK_TILE = 256


def _call(body, after=None, **kw):
    if after is None:
        return pl.pallas_call(body, **kw)
    n_in = len(kw["in_specs"])

    def ordered(*refs):
        body(*refs[:n_in], *refs[n_in + 1:])

    kw["in_specs"] = list(kw["in_specs"]) + [pl.BlockSpec(memory_space=pl.ANY)]
    fn = pl.pallas_call(ordered, **kw)
    return lambda *operands: fn(*operands, after)


def _params(sem=None):
    if sem is None:
        return pltpu.CompilerParams(vmem_limit_bytes=V7X_VMEM_LIMIT)
    return pltpu.CompilerParams(dimension_semantics=sem, vmem_limit_bytes=V7X_VMEM_LIMIT)


def _nn(a, b):
    return jnp.dot(a, b, preferred_element_type=F32)


def _nt(a, b):
    return lax.dot_general(a, b, (((1,), (1,)), ((), ())), preferred_element_type=F32)


def _tn(a, b):
    return lax.dot_general(a, b, (((0,), (0,)), ((), ())), preferred_element_type=F32)


def _gelu_tanh(x):
    return jnp.tanh(GELU_C * (x + GELU_A * (x * x * x)))


def _gelu(x):
    return x * (0.5 * (1.0 + _gelu_tanh(x)))


def _gelu_grad(x):
    t = _gelu_tanh(x)
    return 0.5 * (1.0 + t) + 0.5 * x * (1.0 - t * t) * (GELU_C * (1.0 + 3.0 * GELU_A * (x * x)))


def _rms_scale(x):
    return lax.rsqrt(jnp.mean(x * x, axis=-1, keepdims=True) + EPS)


def _rms_bwd(dxn, x, r):
    return r * dxn - x * ((r * r * r) * jnp.mean(dxn * x, axis=-1, keepdims=True))


def _bucket_table():
    a = np.arange(BLOCK)[:, None]
    j = np.arange(2 * BLOCK)[None, :]
    n = BLOCK + a - j
    valid = (n >= 0) & (n < BLOCK)
    nc = np.maximum(n, 0)
    max_exact = REL_BUCKETS // 2
    nf = np.maximum(nc, 1).astype(np.float32)
    large = max_exact + (
        np.log(nf / np.float32(max_exact)) / np.float32(math.log(BLOCK / max_exact)) * np.float32(REL_BUCKETS - max_exact)
    ).astype(np.int32)
    large = np.minimum(large, REL_BUCKETS - 1)
    bucket = np.where(nc < max_exact, nc, large)
    return np.where(valid, bucket, -1).astype(np.int32)


def _in_proj(x, g1, w_in):
    s = x.shape[0]
    tm = min(TOK_TILE, s)

    def body(x_ref, g_ref, w_ref, zuv_ref, qkv_ref, hn_ref):
        xv = x_ref[...]
        hn = ((xv * _rms_scale(xv)) * g_ref[...]).astype(BF)
        hn_ref[...] = hn
        z = _nn(hn, w_ref[...])
        zuv_ref[...] = z[:, : 2 * D_GMLP]
        qkv_ref[...] = z[:, 2 * D_GMLP:].astype(BF)

    return _call(
        body,
        name="in_proj",
        grid=(s // tm,),
        in_specs=[
            pl.BlockSpec((tm, D_MODEL), lambda i: (i, 0)),
            pl.BlockSpec((1, D_MODEL), lambda i: (0, 0)),
            pl.BlockSpec((D_MODEL, D_IN), lambda i: (0, 0)),
        ],
        out_specs=[
            pl.BlockSpec((tm, 2 * D_GMLP), lambda i: (i, 0)),
            pl.BlockSpec((tm, D_ATTN + 2 * D_KV), lambda i: (i, 0)),
            pl.BlockSpec((tm, D_MODEL), lambda i: (i, 0)),
        ],
        out_shape=[
            jax.ShapeDtypeStruct((s, 2 * D_GMLP), F32),
            jax.ShapeDtypeStruct((s, D_ATTN + 2 * D_KV), BF),
            jax.ShapeDtypeStruct((s, D_MODEL), BF),
        ],
        compiler_params=_params(("arbitrary",)),
    )(x, g1, w_in)


def _build_bias(bias_ref, bucket_ref, table_ref):
    bucket = bucket_ref[...]
    for h in range(N_Q):
        acc = jnp.zeros((BLOCK, 2 * BLOCK), F32)
        for b in range(REL_BUCKETS):
            acc = jnp.where(bucket == b, table_ref[b, h], acc)
        bias_ref[h] = acc


def _window_masks(i):
    row = lax.broadcasted_iota(jnp.int32, (BLOCK, BLOCK), 0)
    col = lax.broadcasted_iota(jnp.int32, (BLOCK, BLOCK), 1)
    return (col > row) & (i > 0), col <= row


def _tril_bf16(w_ref, g):
    row = lax.broadcasted_iota(jnp.int32, (BLOCK, BLOCK), 0)
    col = lax.broadcasted_iota(jnp.int32, (BLOCK, BLOCK), 1)
    return jnp.where(col <= row, w_ref[g], 0.0).astype(BF)


def _attn_probs(q_h, k_prev, k_cur, bias_h, sink, valid_prev, valid_cur):
    l_prev = jnp.where(valid_prev, _nt(q_h, k_prev) * SCALE + bias_h[:, :BLOCK], NEG_INF)
    l_cur = jnp.where(valid_cur, _nt(q_h, k_cur) * SCALE + bias_h[:, BLOCK:], NEG_INF)
    m = jnp.maximum(jnp.maximum(jnp.max(l_prev, axis=-1, keepdims=True), jnp.max(l_cur, axis=-1, keepdims=True)), sink)
    e_prev = jnp.exp(l_prev - m)
    e_cur = jnp.exp(l_cur - m)
    e_sink = jnp.exp(sink - m)
    denom = jnp.sum(e_prev, axis=-1, keepdims=True) + jnp.sum(e_cur, axis=-1, keepdims=True) + e_sink
    return e_prev / denom, e_cur / denom, e_sink / denom


def _mixer_specs(nb):
    cl = lambda i: jnp.minimum(i, nb - 1)
    return [
        pl.BlockSpec((BLOCK, 2 * D_GMLP), lambda i: (cl(i), 0)),
        pl.BlockSpec((BLOCK, D_ATTN), lambda i: (cl(i), 0)),
        pl.BlockSpec((BLOCK, 2 * D_KV), lambda i: (cl(i), D_ATTN // (2 * D_KV))),
        pl.BlockSpec((BLOCK, 2 * D_KV), lambda i: (jnp.maximum(cl(i) - 1, 0), D_ATTN // (2 * D_KV))),
        pl.BlockSpec((1, D_GMLP), lambda i: (0, 0)),
        pl.BlockSpec((GROUPS, BLOCK, BLOCK), lambda i: (0, 0, 0)),
        pl.BlockSpec((GROUPS, BLOCK, 1), lambda i: (0, 0, 0)),
        pl.BlockSpec(memory_space=pltpu.SMEM),
        pl.BlockSpec(memory_space=pltpu.SMEM),
        pl.BlockSpec((BLOCK, 2 * BLOCK), lambda i: (0, 0)),
    ]


def _mixer_fwd(zuv, qkv, gv, w_sp, b_sp, sinks, table, bucket):
    s = zuv.shape[0]
    nb = s // BLOCK

    def body(zuv_ref, q_ref, kvc_ref, kvp_ref, gv_ref, w_ref, b_ref, sink_ref, table_ref, bucket_ref, mix_ref, bias_ref):
        i = pl.program_id(0)

        @pl.when(i == 0)
        def _():
            _build_bias(bias_ref, bucket_ref, table_ref)

        for g in range(GROUPS):
            sl = slice(g * GDIM, (g + 1) * GDIM)
            u = _gelu(zuv_ref[:, sl])
            vg = _gelu(zuv_ref[:, D_GMLP + g * GDIM: D_GMLP + (g + 1) * GDIM])
            vn = ((vg * _rms_scale(vg)) * gv_ref[:, sl]).astype(BF)
            sv = _nn(_tril_bf16(w_ref, g), vn) + b_ref[g]
            mix_ref[:, sl] = (u * sv).astype(BF)

        valid_prev, valid_cur = _window_masks(i)
        for h in range(N_Q):
            kh = h // Q_PER_KV
            q_h = q_ref[:, h * HEAD_DIM:(h + 1) * HEAD_DIM]
            ksl = slice(kh * HEAD_DIM, (kh + 1) * HEAD_DIM)
            vsl = slice(D_KV + kh * HEAD_DIM, D_KV + (kh + 1) * HEAD_DIM)
            p_prev, p_cur, _ = _attn_probs(q_h, kvp_ref[:, ksl], kvc_ref[:, ksl], bias_ref[h], sink_ref[0, h], valid_prev, valid_cur)
            o = _nn(p_prev.astype(BF), kvp_ref[:, vsl]) + _nn(p_cur.astype(BF), kvc_ref[:, vsl])
            mix_ref[:, D_GMLP + h * HEAD_DIM: D_GMLP + (h + 1) * HEAD_DIM] = o.astype(BF)

    return _call(
        body,
        name="mixer_fwd",
        grid=(nb,),
        in_specs=_mixer_specs(nb),
        out_specs=pl.BlockSpec((BLOCK, D_MODEL), lambda i: (i, 0)),
        out_shape=jax.ShapeDtypeStruct((s, D_MODEL), BF),
        scratch_shapes=[pltpu.VMEM((N_Q, BLOCK, 2 * BLOCK), F32)],
        compiler_params=_params(("arbitrary",)),
    )(zuv, qkv, qkv, qkv, gv, w_sp, b_sp, sinks, table, bucket)


def _tail(x, mix, p, t, g2, gf, w_out, w_ff1, w_ff2, w_gate, w_proj):
    s = x.shape[0]
    tm = min(TOK_TILE, s)
    n_ff = w_ff1.shape[0]
    fc = D_FF // n_ff

    def body(x_ref, mix_ref, p_ref, t_ref, g2_ref, gf_ref, wo_ref, w1_ref, w2_ref, wg_ref, wp_ref,
             small_ref, dh1_ref, dh1b_ref, dmix_ref, hn2_ref, a_ref, df_ref, h2_ref, dh2_ref, dgl_ref, dpp_ref, f_ref):
        i = pl.program_id(0)

        @pl.when(i == 0)
        def _():
            small_ref[...] = jnp.zeros_like(small_ref)

        h1 = x_ref[...] + _nn(mix_ref[...], wo_ref[...])
        r2 = _rms_scale(h1)
        hn2 = ((h1 * r2) * g2_ref[...]).astype(BF)
        hn2_ref[...] = hn2
        h2 = h1
        for c in range(n_ff):
            f = _nn(hn2, w1_ref[c])
            f_ref[:, c * fc:(c + 1) * fc] = f
            a = jnp.square(jnp.maximum(f, 0.0)).astype(BF)
            a_ref[:, c * fc:(c + 1) * fc] = a
            h2 = h2 + _nn(a, w2_ref[c * fc:(c + 1) * fc, :])
        h2b = h2.astype(BF)
        h2_ref[...] = h2b
        gate = jax.nn.sigmoid(_nn(h2b, wg_ref[...]))
        pp = _nn(p_ref[...].astype(BF), wp_ref[...])
        h3 = h2 + gate * pp
        rf = _rms_scale(h3)
        gf_v = gf_ref[...]
        err = (h3 * rf) * gf_v - t_ref[...]
        small_ref[2:3, :] += jnp.sum(jnp.sum(err * err, axis=-1, keepdims=True), axis=0, keepdims=True) * (0.5 / D_MODEL)

        dy = err * (1.0 / D_MODEL)
        small_ref[1:2, :] += jnp.sum(dy * (h3 * rf), axis=0, keepdims=True)
        dh3 = _rms_bwd(dy * gf_v, h3, rf)
        dpp_ref[...] = (dh3 * gate).astype(BF)
        dgl = ((dh3 * pp) * (gate * (1.0 - gate))).astype(BF)
        dgl_ref[...] = dgl
        dh2 = dh3 + _nt(dgl, wg_ref[...])
        dh2b = dh2.astype(BF)
        dh2_ref[...] = dh2b
        dhn2 = jnp.zeros((tm, D_MODEL), F32)
        for c in range(n_ff):
            da = _nt(dh2b, w2_ref[c * fc:(c + 1) * fc, :])
            df = (da * (2.0 * jnp.maximum(f_ref[:, c * fc:(c + 1) * fc], 0.0))).astype(BF)
            df_ref[:, c * fc:(c + 1) * fc] = df
            dhn2 = dhn2 + _nt(df, w1_ref[c])
        small_ref[0:1, :] += jnp.sum(dhn2 * (h1 * r2), axis=0, keepdims=True)
        dh1 = dh2 + _rms_bwd(dhn2 * g2_ref[...], h1, r2)
        dh1_ref[...] = dh1
        dh1b = dh1.astype(BF)
        dh1b_ref[...] = dh1b
        dmix_ref[...] = _nt(dh1b, wo_ref[...])

    tile = lambda cols: pl.BlockSpec((tm, cols), lambda i: (i, 0))
    whole = lambda shape: pl.BlockSpec(shape, lambda i: (0,) * len(shape), pipeline_mode=pl.Buffered(1))
    row = pl.BlockSpec((1, D_MODEL), lambda i: (0, 0))
    act = lambda cols, dt: jax.ShapeDtypeStruct((s, cols), dt)
    return _call(
        body,
        name="tail",
        grid=(s // tm,),
        in_specs=[tile(D_MODEL), tile(D_MODEL), tile(PLE_DIM), tile(D_MODEL), row, row,
                  whole(w_out.shape), whole(w_ff1.shape), whole(w_ff2.shape), whole(w_gate.shape), whole(w_proj.shape)],
        out_specs=[pl.BlockSpec((8, D_MODEL), lambda i: (0, 0)), tile(D_MODEL), tile(D_MODEL), tile(D_MODEL), tile(D_MODEL), tile(D_FF),
                   tile(D_FF), tile(D_MODEL), tile(D_MODEL), tile(D_MODEL), tile(D_MODEL)],
        out_shape=[jax.ShapeDtypeStruct((8, D_MODEL), F32),
                   act(D_MODEL, F32), act(D_MODEL, BF), act(D_MODEL, F32), act(D_MODEL, BF), act(D_FF, BF), act(D_FF, BF), act(D_MODEL, BF),
                   act(D_MODEL, BF), act(D_MODEL, BF), act(D_MODEL, BF)],
        scratch_shapes=[pltpu.VMEM((tm, D_FF), F32)],
        compiler_params=_params(("arbitrary",)),
    )(x, mix, p, t, g2, gf, w_out, w_ff1, w_ff2, w_gate, w_proj)


def _mixer_bwd(zuv, qkv, dmix, gv, w_sp, b_sp, sinks, table, bucket, after=None):
    s = zuv.shape[0]
    nb = s // BLOCK

    def body(zuv_ref, q_ref, kvc_ref, kvp_ref, gv_ref, w_ref, b_ref, sink_ref, table_ref, bucket_ref, dmix_ref,
             dzm_ref, dkv_ref, dgv_ref, dw_ref, db_ref, dattn_ref,
             bias_ref, dbias_ref, carry_ref, dsink_acc, db_acc):
        i = pl.program_id(0)

        @pl.when(i == 0)
        def _():
            _build_bias(bias_ref, bucket_ref, table_ref)
            dbias_ref[...] = jnp.zeros_like(dbias_ref)
            carry_ref[...] = jnp.zeros_like(carry_ref)
            dsink_acc[...] = jnp.zeros_like(dsink_acc)
            dgv_ref[...] = jnp.zeros_like(dgv_ref)
            dw_ref[...] = jnp.zeros_like(dw_ref)
            db_acc[...] = jnp.zeros_like(db_acc)

        @pl.when(i < nb)
        def _():
            for g in range(GROUPS):
                sl = slice(g * GDIM, (g + 1) * GDIM)
                slv = slice(D_GMLP + g * GDIM, D_GMLP + (g + 1) * GDIM)
                zu = zuv_ref[:, sl]
                zv = zuv_ref[:, slv]
                u = _gelu(zu)
                vg = _gelu(zv)
                rg = _rms_scale(vg)
                vhat = vg * rg
                gain = gv_ref[:, sl]
                vn = (vhat * gain).astype(BF)
                w_g = _tril_bf16(w_ref, g)
                sv = _nn(w_g, vn) + b_ref[g]
                dmix_a = dmix_ref[:, sl]
                dsv = dmix_a * u
                dsvb = dsv.astype(BF)
                db_acc[g] += jnp.sum(dsv, axis=-1, keepdims=True)
                dw_ref[g] += _nt(dsvb, vn)
                dvn = _tn(w_g, dsvb)
                dgv_ref[:, sl] += jnp.sum(dvn * vhat, axis=0, keepdims=True)
                dvg = _rms_bwd(dvn * gain, vg, rg)
                dzm_ref[:, sl] = ((dmix_a * sv) * _gelu_grad(zu)).astype(BF)
                dzm_ref[:, slv] = (dvg * _gelu_grad(zv)).astype(BF)

            valid_prev, valid_cur = _window_masks(i)
            for kh in range(N_Q // Q_PER_KV):
                ksl = slice(kh * HEAD_DIM, (kh + 1) * HEAD_DIM)
                vsl = slice(D_KV + kh * HEAD_DIM, D_KV + (kh + 1) * HEAD_DIM)
                k_prev, k_cur = kvp_ref[:, ksl], kvc_ref[:, ksl]
                v_prev, v_cur = kvp_ref[:, vsl], kvc_ref[:, vsl]
                dk_prev = jnp.zeros((BLOCK, HEAD_DIM), F32)
                dk_cur = jnp.zeros((BLOCK, HEAD_DIM), F32)
                dv_prev = jnp.zeros((BLOCK, HEAD_DIM), F32)
                dv_cur = jnp.zeros((BLOCK, HEAD_DIM), F32)
                for gq in range(Q_PER_KV):
                    h = kh * Q_PER_KV + gq
                    q_h = q_ref[:, h * HEAD_DIM:(h + 1) * HEAD_DIM]
                    p_prev, p_cur, p_sink = _attn_probs(q_h, k_prev, k_cur, bias_ref[h], sink_ref[0, h], valid_prev, valid_cur)
                    do = dmix_ref[:, D_GMLP + h * HEAD_DIM: D_GMLP + (h + 1) * HEAD_DIM].astype(BF)
                    dp_prev = _nt(do, v_prev)
                    dp_cur = _nt(do, v_cur)
                    delta = jnp.sum(p_prev * dp_prev, axis=-1, keepdims=True) + jnp.sum(p_cur * dp_cur, axis=-1, keepdims=True)
                    ds_prev = p_prev * (dp_prev - delta)
                    ds_cur = p_cur * (dp_cur - delta)
                    dsink_acc[h] -= p_sink * delta
                    dbias_ref[h, :, :BLOCK] += ds_prev
                    dbias_ref[h, :, BLOCK:] += ds_cur
                    dsb_prev = ds_prev.astype(BF)
                    dsb_cur = ds_cur.astype(BF)
                    dq = (_nn(dsb_prev, k_prev) + _nn(dsb_cur, k_cur)) * SCALE
                    dzm_ref[:, 2 * D_GMLP + h * HEAD_DIM: 2 * D_GMLP + (h + 1) * HEAD_DIM] = dq.astype(BF)
                    dk_prev = dk_prev + _tn(dsb_prev, q_h)
                    dk_cur = dk_cur + _tn(dsb_cur, q_h)
                    dv_prev = dv_prev + _tn(p_prev.astype(BF), do)
                    dv_cur = dv_cur + _tn(p_cur.astype(BF), do)
                dkv_ref[:, ksl] = (carry_ref[:, ksl] + dk_prev * SCALE).astype(BF)
                dkv_ref[:, vsl] = (carry_ref[:, vsl] + dv_prev).astype(BF)
                carry_ref[:, ksl] = dk_cur * SCALE
                carry_ref[:, vsl] = dv_cur

        @pl.when(i == nb)
        def _():
            dkv_ref[...] = carry_ref[...].astype(BF)
            row = lax.broadcasted_iota(jnp.int32, (BLOCK, BLOCK), 0)
            col = lax.broadcasted_iota(jnp.int32, (BLOCK, BLOCK), 1)
            for g in range(GROUPS):
                dw_ref[g] = jnp.where(col <= row, dw_ref[g], 0.0)
                db_ref[g:g + 1, :] = jnp.sum(jnp.where(col == row, db_acc[g], 0.0), axis=0, keepdims=True)
            bucket = bucket_ref[...]
            for h in range(N_Q):
                dattn_ref[REL_BUCKETS, h] = jnp.sum(dsink_acc[h])
                dbh = dbias_ref[h]
                for b in range(REL_BUCKETS):
                    dattn_ref[b, h] = jnp.sum(jnp.where(bucket == b, dbh, 0.0))

    cl = lambda i: jnp.minimum(i, nb - 1)
    const = lambda shape: pl.BlockSpec(shape, lambda i: (0,) * len(shape))
    return _call(
        body,
        name="mixer_bwd",
        after=after,
        grid=(nb + 1,),
        in_specs=_mixer_specs(nb) + [pl.BlockSpec((BLOCK, D_MODEL), lambda i: (cl(i), 0))],
        out_specs=[
            pl.BlockSpec((BLOCK, D_MAIN), lambda i: (cl(i), 0)),
            pl.BlockSpec((BLOCK, 2 * D_KV), lambda i: (jnp.maximum(i - 1, 0), 0)),
            const((1, D_GMLP)),
            const((GROUPS, BLOCK, BLOCK)),
            const((GROUPS, BLOCK)),
            pl.BlockSpec(memory_space=pltpu.SMEM),
        ],
        out_shape=[
            jax.ShapeDtypeStruct((s, D_MAIN), BF),
            jax.ShapeDtypeStruct((s, 2 * D_KV), BF),
            jax.ShapeDtypeStruct((1, D_GMLP), F32),
            jax.ShapeDtypeStruct((GROUPS, BLOCK, BLOCK), F32),
            jax.ShapeDtypeStruct((GROUPS, BLOCK), F32),
            jax.ShapeDtypeStruct((REL_BUCKETS + 1, N_Q), F32),
        ],
        scratch_shapes=[
            pltpu.VMEM((N_Q, BLOCK, 2 * BLOCK), F32),
            pltpu.VMEM((N_Q, BLOCK, 2 * BLOCK), F32),
            pltpu.VMEM((BLOCK, 2 * D_KV), F32),
            pltpu.VMEM((N_Q, BLOCK, 1), F32),
            pltpu.VMEM((GROUPS, BLOCK, 1), F32),
        ],
        compiler_params=_params(("arbitrary",)),
    )(zuv, qkv, qkv, qkv, gv, w_sp, b_sp, sinks, table, bucket, dmix)


def _in_bwd(x, dh1, dzm, dkv, g1, w_in, after=None):
    s = x.shape[0]
    tm = min(TOK_TILE, s)

    def body(x_ref, dh1_ref, dzm_ref, dkv_ref, g_ref, w_ref, dx_ref, dg_ref):
        @pl.when(pl.program_id(0) == 0)
        def _():
            dg_ref[...] = jnp.zeros_like(dg_ref)

        dhn = _nt(dzm_ref[...], w_ref[:, :D_MAIN]) + _nt(dkv_ref[...], w_ref[:, D_MAIN:])
        xv = x_ref[...]
        r = _rms_scale(xv)
        dg_ref[...] += jnp.sum(dhn * (xv * r), axis=0, keepdims=True)
        dx_ref[...] = dh1_ref[...] + _rms_bwd(dhn * g_ref[...], xv, r)

    tile = lambda cols: pl.BlockSpec((tm, cols), lambda i: (i, 0))
    row = pl.BlockSpec((1, D_MODEL), lambda i: (0, 0))
    return _call(
        body,
        name="in_bwd",
        after=after,
        grid=(s // tm,),
        in_specs=[tile(D_MODEL), tile(D_MODEL), tile(D_MAIN), tile(2 * D_KV), row, pl.BlockSpec((D_MODEL, D_IN), lambda i: (0, 0))],
        out_specs=[tile(D_MODEL), row],
        out_shape=[jax.ShapeDtypeStruct((s, D_MODEL), F32), jax.ShapeDtypeStruct((1, D_MODEL), F32)],
        compiler_params=_params(("arbitrary",)),
    )(x, dh1, dzm, dkv, g1, w_in)


def _wgrad(a, b, tm, tn, name, peer_cols=False, after=None):
    s, m = a.shape
    n = b.shape[1]

    def body(a_ref, b_ref, o_ref, at_ref):
        @pl.when(pl.program_id(1) == 0)
        def _():
            at_ref[...] = a_ref[...].T

        r = _nn(at_ref[...], b_ref[...])
        if peer_cols:
            o_ref[0] = r
        else:
            o_ref[...] = r

    if peer_cols:
        out_spec = pl.BlockSpec((1, tm, tn), lambda i, j: (j, i, 0))
        out_shape = jax.ShapeDtypeStruct((n // tn, m, tn), F32)
    else:
        out_spec = pl.BlockSpec((tm, tn), lambda i, j: (i, j))
        out_shape = jax.ShapeDtypeStruct((m, n), F32)
    return _call(
        body,
        name=name,
        after=after,
        grid=(m // tm, n // tn),
        in_specs=[pl.BlockSpec((s, tm), lambda i, j: (0, i)), pl.BlockSpec((s, tn), lambda i, j: (0, j))],
        out_specs=out_spec,
        out_shape=out_shape,
        scratch_shapes=[pltpu.VMEM((tm, s), BF)],
        compiler_params=_params(("arbitrary", "arbitrary")),
    )(a, b)


def _adamw_math(w, g, m, v):
    m_new = ADAM_B1 * m + (1.0 - ADAM_B1) * g
    v_new = ADAM_B2 * v + (1.0 - ADAM_B2) * jnp.square(g)
    m_hat = m_new / (1.0 - ADAM_B1 ** ADAM_STEP)
    v_hat = v_new / (1.0 - ADAM_B2 ** ADAM_STEP)
    delta = -ADAM_LR * (m_hat / (jnp.sqrt(v_hat) + ADAM_EPS) + ADAM_WD * w)
    return delta, m_new, v_new


def _final_adamw(part, recv, w, m, v, name):
    r, c = w.shape
    tr = min(r, 256)

    def body(p_ref, r_ref, w_ref, m_ref, v_ref, g_ref, d_ref, mo_ref, vo_ref):
        g = p_ref[...]
        for j in range(3):
            g = g + r_ref[j].astype(F32)
        g_ref[...] = g
        d_ref[...], mo_ref[...], vo_ref[...] = _adamw_math(w_ref[...], g, m_ref[...], v_ref[...])

    spec = pl.BlockSpec((tr, c), lambda i: (i, 0))
    return _call(
        body,
        name=name,
        grid=(r // tr,),
        in_specs=[spec, pl.BlockSpec((3, tr, c), lambda i: (0, i, 0)), spec, spec, spec],
        out_specs=[spec] * 4,
        out_shape=[jax.ShapeDtypeStruct((r, c), F32)] * 4,
        compiler_params=_params(("arbitrary",)),
    )(part, recv, w, m, v)


def _rs_sum(g, land, blocks, name):
    _, r, c = g.shape
    tr = min(r, 256)

    def body(blk_ref, g0_ref, g1_ref, g2_ref, g3_ref, l_ref, part_ref, send_ref):
        part_ref[...] = g0_ref[0] + l_ref[0]
        for j, gj_ref in enumerate((g1_ref, g2_ref, g3_ref)):
            send_ref[j] = (gj_ref[0] + l_ref[j + 1]).astype(BF)

    def pick(j):
        return pl.BlockSpec((1, tr, c), lambda i, blk: (blk[j], i, 0))

    return _call(
        body,
        name=name,
        grid_spec=pltpu.PrefetchScalarGridSpec(
            num_scalar_prefetch=1,
            grid=(r // tr,),
            in_specs=[pick(0), pick(1), pick(2), pick(3), pl.BlockSpec((4, tr, c), lambda i, blk: (0, i, 0))],
            out_specs=[pl.BlockSpec((tr, c), lambda i, blk: (i, 0)), pl.BlockSpec((3, tr, c), lambda i, blk: (0, i, 0))],
        ),
        out_shape=[jax.ShapeDtypeStruct((r, c), F32), jax.ShapeDtypeStruct((3, r, c), BF)],
        compiler_params=_params(("arbitrary",)),
    )(blocks, g, g, g, g, land)


def _adamw_small(gathered, weights, moms, vels, after):
    n_w = len(weights)

    def body(*refs):
        tail_ref, g1_ref, gv_ref, wsp_ref, bsp_ref, attn_ref = refs[:6]
        w_refs, m_refs, v_refs = refs[6:6 + n_w], refs[6 + n_w:6 + 2 * n_w], refs[6 + 2 * n_w:6 + 3 * n_w]
        outs = refs[6 + 3 * n_w:]
        loss_ref, g_refs, d_refs = outs[0], outs[1:1 + n_w], outs[1 + n_w:1 + 2 * n_w]
        mo_refs, vo_refs = outs[1 + 2 * n_w:1 + 3 * n_w], outs[1 + 3 * n_w:]

        def total(part):
            acc = part(0)
            for d in range(1, N_DEV):
                acc = acc + part(d)
            return acc

        grads = [
            total(lambda d: g1_ref[d]), total(lambda d: gv_ref[d]), total(lambda d: wsp_ref[d]), total(lambda d: bsp_ref[d]),
            total(lambda d: attn_ref[d, REL_BUCKETS:, :]), total(lambda d: attn_ref[d, :REL_BUCKETS, :]),
            total(lambda d: tail_ref[d, 0:1, :]), total(lambda d: tail_ref[d, 1:2, :])]
        loss_ref[...] = total(lambda d: tail_ref[d, 2:3, 0:1])
        for k in range(n_w):
            g_refs[k][...] = grads[k]
            d_refs[k][...], mo_refs[k][...], vo_refs[k][...] = _adamw_math(w_refs[k][...], grads[k], m_refs[k][...], v_refs[k][...])

    shapes = [jax.ShapeDtypeStruct(w.shape, F32) for w in weights]
    outs = _call(
        body,
        name="adamw_small",
        after=after,
        in_specs=[pl.BlockSpec(memory_space=pltpu.VMEM)] * (6 + 3 * n_w),
        out_shape=[jax.ShapeDtypeStruct((1, 1), F32)] + shapes * 4,
        compiler_params=_params(),
    )(*gathered, *weights, *moms, *vels)
    return outs[0], outs[1:1 + n_w], outs[1 + n_w:1 + 2 * n_w], outs[1 + 2 * n_w:1 + 3 * n_w], outs[1 + 3 * n_w:]


def _place():
    x, y, c = lax.axis_index("x"), lax.axis_index("y"), lax.axis_index("c")
    return x, y, c, [(1 - x, y), (x, 1 - y), (1 - x, 1 - y)]


def _dev_index(px, py, pc):
    return 4 * px + 2 * py + pc


def _all_gather(shards, out_dtype, name, after=None):
    k_n = len(shards)

    def body(*refs):
        ins, outs, stage = refs[:k_n], refs[k_n:2 * k_n], refs[2 * k_n:3 * k_n]
        send_sems, recv_sems, local_sems = refs[3 * k_n:]
        x, y, c, chips = _place()
        me, sibling = (x, y, c), (x, y, 1 - c)

        def copy(k, j, block, to, src=None):
            dst = outs[k].at[_dev_index(*block)]
            return pltpu.make_async_remote_copy(
                src_ref=dst if src is None else src, dst_ref=dst,
                send_sem=send_sems.at[k, j], recv_sem=recv_sems.at[k, j], device_id=to, device_id_type=MESH)

        mine, first = [], []
        for k in range(k_n):
            stage[k][...] = ins[k][...].astype(out_dtype)
            mine.append(pltpu.make_async_copy(stage[k], outs[k].at[_dev_index(*me)], local_sems.at[k]))
            mine[k].start()
            first.append(copy(k, 0, me, sibling, src=stage[k]))
            first += [copy(k, 1 + j, me, (*chip, c), src=stage[k]) for j, chip in enumerate(chips)]
        for cp in first:
            cp.start()
        passed = []
        for k in range(k_n):
            for j, chip in enumerate(chips):
                copy(k, 1 + j, (*chip, c), me).wait_recv()
                passed.append(copy(k, 4 + j, (*chip, c), sibling))
                passed[-1].start()
        for k in range(k_n):
            copy(k, 0, sibling, me).wait_recv()
            for j, chip in enumerate(chips):
                copy(k, 4 + j, (*chip, 1 - c), me).wait_recv()
        for cp in first + passed:
            cp.wait_send()
        for cp in mine:
            cp.wait()

    return _call(
        body,
        name=name,
        after=after,
        in_specs=[pl.BlockSpec(memory_space=pltpu.VMEM)] * k_n,
        out_specs=[pl.BlockSpec(memory_space=pl.ANY)] * k_n,
        out_shape=[jax.ShapeDtypeStruct((N_DEV,) + sh.shape, out_dtype) for sh in shards],
        scratch_shapes=[pltpu.VMEM(sh.shape, out_dtype) for sh in shards]
        + [pltpu.SemaphoreType.DMA((k_n, 7)), pltpu.SemaphoreType.DMA((k_n, 7)), pltpu.SemaphoreType.DMA((k_n,))],
        compiler_params=_params(),
    )(*shards)


HBM_SPEC = pl.BlockSpec(memory_space=pltpu.HBM)
SEM_SPEC = pl.BlockSpec(memory_space=pltpu.SEMAPHORE)
ANY_SPEC = pl.BlockSpec(memory_space=pl.ANY)
DATAFLOW = pltpu.SideEffectType.DATAFLOW_SIDE_EFFECTING


def _hbm(a):
    return pltpu.with_memory_space_constraint(a, pltpu.HBM)


def _prep_weights(shards):
    k_n = len(shards)

    def body(*refs):
        ins, outs, stage, sems = refs[:k_n], refs[k_n:2 * k_n], refs[2 * k_n:3 * k_n], refs[3 * k_n]
        x, y, c, _ = _place()
        copies = []
        for k in range(k_n):
            stage[k][...] = ins[k][...].astype(BF)
            copies.append(pltpu.make_async_copy(stage[k], outs[k].at[_dev_index(x, y, c)], sems.at[k]))
            copies[k].start()
        for cp in copies:
            cp.wait()

    return _call(
        body,
        name="prep_weights",
        in_specs=[pl.BlockSpec(memory_space=pltpu.VMEM)] * k_n,
        out_specs=[ANY_SPEC] * k_n,
        out_shape=[jax.ShapeDtypeStruct((N_DEV,) + sh.shape, BF) for sh in shards],
        scratch_shapes=[pltpu.VMEM(sh.shape, BF) for sh in shards] + [pltpu.SemaphoreType.DMA((k_n,))],
        compiler_params=_params(),
    )(*shards)


def _ag_start(first, rest):
    lands = list(first) + list(rest)
    k_n, k_first = len(lands), len(first)

    def body(*refs):
        land = refs[:k_n]
        sems = refs[k_n:k_n + 4]
        token = refs[-1]
        x, y, c, chips = _place()
        targets = [(x, y, 1 - c)] + [(*chip, c) for chip in chips]
        for k in range(k_n):
            blk = land[k].at[_dev_index(x, y, c)]
            send_sems, recv_sems, base = (sems[0], sems[1], k) if k < k_first else (sems[2], sems[3], k - k_first)
            for j, to in enumerate(targets):
                pltpu.make_async_remote_copy(
                    src_ref=blk, dst_ref=blk, send_sem=send_sems.at[4 * base + j], recv_sem=recv_sems.at[4 * base + j],
                    device_id=to, device_id_type=MESH).start()
        token[...] = jnp.zeros_like(token)

    k_rest = k_n - k_first
    outs = pl.pallas_call(
        body,
        name="ag_start",
        in_specs=[HBM_SPEC] * k_n,
        out_specs=(SEM_SPEC, SEM_SPEC, SEM_SPEC, SEM_SPEC, *[HBM_SPEC] * k_n, pl.BlockSpec(memory_space=pltpu.VMEM)),
        out_shape=(pltpu.SemaphoreType.DMA((4 * k_first,)), pltpu.SemaphoreType.DMA((4 * k_first,)),
                   pltpu.SemaphoreType.DMA((4 * k_rest,)), pltpu.SemaphoreType.DMA((4 * k_rest,)),
                   *[pltpu.HBM(a.shape, a.dtype) for a in lands], jax.ShapeDtypeStruct((8, LANES), F32)),
        input_output_aliases={k: 4 + k for k in range(k_n)},
        compiler_params=pltpu.CompilerParams(has_side_effects=DATAFLOW),
    )(*[_hbm(a) for a in lands])
    flying = list(outs[4:4 + k_n])
    return (outs[0], outs[1], flying[:k_first]), (outs[2], outs[3], flying[k_first:]), outs[-1]


def _ag_mid(lands, send_sems, recv_sems, after, name):
    k_n = len(lands)

    def body(*refs):
        land = refs[:k_n]
        send1, recv1 = refs[k_n], refs[k_n + 1]
        fwd_send, fwd_recv = refs[-2], refs[-1]
        x, y, c, chips = _place()
        sources = [(x, y, 1 - c)] + [(*chip, c) for chip in chips]
        for k in range(k_n):
            mine = land[k].at[_dev_index(x, y, c)]
            for j, frm in enumerate(sources):
                got = land[k].at[_dev_index(*frm)]
                cp = pltpu.make_async_remote_copy(
                    src_ref=mine, dst_ref=got, send_sem=send1.at[4 * k + j], recv_sem=recv1.at[4 * k + j], device_id=frm, device_id_type=MESH)
                cp.wait_send()
                cp.wait_recv()
                if j >= 1:
                    pltpu.make_async_remote_copy(
                        src_ref=got, dst_ref=got, send_sem=fwd_send.at[3 * k + j - 1], recv_sem=fwd_recv.at[3 * k + j - 1],
                        device_id=(x, y, 1 - c), device_id_type=MESH).start()

    outs = pl.pallas_call(
        body,
        name=name,
        in_specs=[HBM_SPEC] * k_n + [SEM_SPEC, SEM_SPEC, ANY_SPEC],
        out_specs=(*[HBM_SPEC] * k_n, SEM_SPEC, SEM_SPEC),
        out_shape=(*[pltpu.HBM(a.shape, a.dtype) for a in lands], pltpu.SemaphoreType.DMA((3 * k_n,)), pltpu.SemaphoreType.DMA((3 * k_n,))),
        input_output_aliases={k: k for k in range(k_n)},
        compiler_params=pltpu.CompilerParams(has_side_effects=DATAFLOW),
    )(*lands, send_sems, recv_sems, after)
    return list(outs[:k_n]), outs[-2], outs[-1]


def _ag_end(lands, fwd_send, fwd_recv, name):
    k_n = len(lands)

    def body(*refs):
        land = refs[:k_n]
        fsend, frecv = refs[k_n], refs[k_n + 1]
        x, y, c, chips = _place()
        for k in range(k_n):
            for j, chip in enumerate(chips):
                cp = pltpu.make_async_remote_copy(
                    src_ref=land[k].at[_dev_index(*chip, c)], dst_ref=land[k].at[_dev_index(*chip, 1 - c)],
                    send_sem=fsend.at[3 * k + j], recv_sem=frecv.at[3 * k + j], device_id=(x, y, 1 - c), device_id_type=MESH)
                cp.wait_send()
                cp.wait_recv()

    outs = pl.pallas_call(
        body,
        name=name,
        in_specs=[HBM_SPEC] * k_n + [SEM_SPEC, SEM_SPEC],
        out_specs=tuple([HBM_SPEC] * k_n),
        out_shape=tuple(pltpu.HBM(a.shape, a.dtype) for a in lands),
        input_output_aliases={k: k for k in range(k_n)},
        compiler_params=pltpu.CompilerParams(has_side_effects=DATAFLOW),
    )(*lands, fwd_send, fwd_recv)
    return list(outs)


def _chips4():
    x, y, c, others = _place()
    return x, y, c, [(x, y)] + others


def _route_sibling(j):
    x, y, c, chips = _chips4()
    return _dev_index(*chips[j], 1 - c), j, (x, y, 1 - c)


def _route_chips(j):
    x, y, c, chips = _chips4()
    return j, j, (*chips[j + 1], c)


def _xchg_copy(route, n, k, j, src, dst, send_sems, recv_sems):
    si, di, peer = route(j)
    return pltpu.make_async_remote_copy(
        src_ref=src[k].at[si], dst_ref=dst[k].at[di], send_sem=send_sems.at[n * k + j], recv_sem=recv_sems.at[n * k + j],
        device_id=peer, device_id_type=MESH)


def _xchg_start(srcs, slot_shapes, route, n, name):
    k_n = len(srcs)
    dsts = [lax.empty((n,) + tuple(sh), a.dtype) for sh, a in zip(slot_shapes, srcs)]

    def body(*refs):
        src, dst = refs[:k_n], refs[k_n:2 * k_n]
        send_sems, recv_sems, token = refs[2 * k_n], refs[2 * k_n + 1], refs[-1]
        for k in range(k_n):
            for j in range(n):
                _xchg_copy(route, n, k, j, src, dst, send_sems, recv_sems).start()
        token[...] = jnp.zeros_like(token)

    arrays = list(srcs) + dsts
    outs = pl.pallas_call(
        body,
        name=name,
        in_specs=[HBM_SPEC] * (2 * k_n),
        out_specs=(SEM_SPEC, SEM_SPEC, *[HBM_SPEC] * (2 * k_n), pl.BlockSpec(memory_space=pltpu.VMEM)),
        out_shape=(pltpu.SemaphoreType.DMA((n * k_n,)), pltpu.SemaphoreType.DMA((n * k_n,)),
                   *[pltpu.HBM(a.shape, a.dtype) for a in arrays], jax.ShapeDtypeStruct((8, LANES), F32)),
        input_output_aliases={i: 2 + i for i in range(2 * k_n)},
        compiler_params=pltpu.CompilerParams(has_side_effects=DATAFLOW),
    )(*[_hbm(a) for a in arrays])
    return outs[0], outs[1], list(outs[2:2 + k_n]), list(outs[2 + k_n:2 + 2 * k_n]), outs[-1]


def _xchg_wait(send_sems, recv_sems, srcs, dsts, route, n, after, name):
    k_n = len(srcs)

    def body(*refs):
        src, dst = refs[:k_n], refs[k_n:2 * k_n]
        send, recv = refs[2 * k_n], refs[2 * k_n + 1]
        for k in range(k_n):
            for j in range(n):
                cp = _xchg_copy(route, n, k, j, src, dst, send, recv)
                cp.wait_send()
                cp.wait_recv()

    arrays = list(srcs) + list(dsts)
    outs = pl.pallas_call(
        body,
        name=name,
        in_specs=[HBM_SPEC] * (2 * k_n) + [SEM_SPEC, SEM_SPEC, ANY_SPEC],
        out_specs=tuple([HBM_SPEC] * (2 * k_n)),
        out_shape=tuple(pltpu.HBM(a.shape, a.dtype) for a in arrays),
        input_output_aliases={i: i for i in range(2 * k_n)},
        compiler_params=pltpu.CompilerParams(has_side_effects=DATAFLOW),
    )(*arrays, send_sems, recv_sems, after)
    return list(outs[:k_n]), list(outs[k_n:])


SMALL = ("norm1_gain", "gmlp_v_gain", "w_spatial", "b_spatial", "attn_sinks", "rel_bias_table", "norm2_gain", "final_gain")
LANES = 128


def _swap_start(grads, tag):
    return _xchg_start(grads, [g.shape[1:] for g in grads], _route_sibling, 4, f"rs_{tag}_swap_start")


def _swap_sums(swap, names, after, tag):
    send1, recv1, src1, land1, _ = swap
    x, y, c, chips = _chips4()
    blocks = jnp.stack([_dev_index(*chip, c) for chip in chips]).astype(jnp.int32)
    src1, land1 = _xchg_wait(send1, recv1, src1, land1, _route_sibling, 4, after, f"rs_{tag}_swap_wait")
    return [_rs_sum(g, land, blocks, f"rs_sum_{n}") for g, land, n in zip(src1, land1, names)]


def _chips_start(sums, tag):
    sends = [ps[1] for ps in sums]
    return _xchg_start(sends, [a.shape[1:] for a in sends], _route_chips, 3, f"rs_{tag}_chips_start")


def _chips_wait(chips, after, tag):
    send2, recv2, src2, land2, _ = chips
    return _xchg_wait(send2, recv2, src2, land2, _route_chips, 3, after, f"rs_{tag}_chips_wait")[1]


def kernel(x, p, norm1_gain, w_in, gmlp_v_gain, w_spatial, b_spatial, attn_sinks, rel_bias_table, w_out, norm2_gain, w_ff1, w_ff2, w_ple_proj, w_ple_gate, final_gain, loss_target, m_norm1_gain, m_w_in, m_gmlp_v_gain, m_w_spatial, m_b_spatial, m_attn_sinks, m_rel_bias_table, m_w_out, m_norm2_gain, m_w_ff1, m_w_ff2, m_w_ple_proj, m_w_ple_gate, m_final_gain, v_norm1_gain, v_w_in, v_gmlp_v_gain, v_w_spatial, v_b_spatial, v_attn_sinks, v_rel_bias_table, v_w_out, v_norm2_gain, v_w_ff1, v_w_ff2, v_w_ple_proj, v_w_ple_gate, v_final_gain):
    args = dict(locals())
    s = x.shape[1]
    big = ("w_in", "w_out", "w_ff1", "w_ff2", "w_ple_proj", "w_ple_gate")

    x2, p2, t2 = x.reshape(s, D_MODEL), p.reshape(s, PLE_DIM), loss_target.reshape(s, D_MODEL)
    g1, gv, w_sp, b_sp, sinks, table, g2, gf = (args[n] for n in SMALL)
    bucket = jnp.asarray(_bucket_table())
    b_col = b_sp.reshape(GROUPS, BLOCK, 1)

    lands = _prep_weights([args[n][0] for n in big])
    (send_in, recv_in, fly_in), (send_rest, recv_rest, fly_rest), token = _ag_start(lands[:1], lands[1:])
    mid_in, fwd_send_in, fwd_recv_in = _ag_mid(fly_in, send_in, recv_in, token, "ag_mid_w_in")
    g_in = _ag_end(mid_in, fwd_send_in, fwd_recv_in, "ag_end_w_in")[0]
    full_in = g_in.transpose(1, 0, 2).reshape(D_MODEL, D_IN)

    zuv, qkv, hn1 = _in_proj(x2, g1, full_in)
    mix = _mixer_fwd(zuv, qkv, gv, w_sp[0], b_col, sinks, table, bucket)
    mid, fwd_send, fwd_recv = _ag_mid(fly_rest, send_rest, recv_rest, mix, "ag_mid_rest")
    g_out, g_ff1, g_ff2, g_proj, g_gate = _ag_end(mid, fwd_send, fwd_recv, "ag_end_rest")
    full_out, full_ff2, full_gate = g_out.reshape(D_MODEL, D_MODEL), g_ff2.reshape(D_FF, D_MODEL), g_gate.reshape(D_MODEL, D_MODEL)
    full_proj = g_proj.transpose(1, 0, 2).reshape(PLE_DIM, D_MODEL)

    (tail_small, dh1, dh1b, dmix, hn2, a, df, h2, dh2, dgl, dpp) = _tail(
        x2, mix, p2, t2, g2, gf.reshape(1, D_MODEL), full_out, g_ff1, full_ff2, full_gate, full_proj)

    gw_ff1 = _wgrad(hn2, df, 512, D_FF // N_DEV, "wgrad_ff1", peer_cols=True)
    gw_ff2 = _wgrad(a, dh2, 512, 1024, "wgrad_ff2").reshape(N_DEV, D_FF // N_DEV, D_MODEL)
    ffn_swap = _swap_start([gw_ff1, gw_ff2], "ffn")
    gw_gate = _wgrad(h2, dgl, 512, 1024, "wgrad_gate", after=ffn_swap[4]).reshape(N_DEV, D_MODEL // N_DEV, D_MODEL)
    gw_proj = _wgrad(p2.astype(BF), dpp, PLE_DIM, D_MODEL // N_DEV, "wgrad_proj", peer_cols=True, after=ffn_swap[4])
    gw_out = _wgrad(mix, dh1b, 512, 1024, "wgrad_out", after=ffn_swap[4]).reshape(N_DEV, D_MODEL // N_DEV, D_MODEL)
    mid_swap = _swap_start([gw_out, gw_proj, gw_gate], "mid")
    early_names = ("w_ff1", "w_ff2", "w_out", "w_ple_proj", "w_ple_gate")
    early_sums = _swap_sums(ffn_swap, early_names[:2], mid_swap[4], "ffn")
    early_sums += _swap_sums(mid_swap, early_names[2:], early_sums[-1][0], "mid")
    early_chips = _chips_start(early_sums, "early")

    dzm, dkv, d_gv, d_wsp, d_bsp, d_attn = _mixer_bwd(
        zuv, qkv, dmix, gv, w_sp[0], b_col, sinks, table, bucket, after=early_chips[4])
    gw_in = jnp.concatenate([_wgrad(hn1, dzm, 512, 512, "wgrad_in_main"), _wgrad(hn1, dkv, 512, 256, "wgrad_in_kv")], axis=1)
    gw_in = gw_in.reshape(D_MODEL, N_DEV, D_IN // N_DEV).transpose(1, 0, 2)
    in_swap = _swap_start([gw_in], "in")
    dx, d_g1 = _in_bwd(x2, dh1, dzm, dkv, g1, full_in, after=in_swap[4])
    grad_x = dx.reshape(x.shape)

    gathered = _all_gather([tail_small, d_g1, d_gv, d_wsp, d_bsp, d_attn], F32, "gather_small_grads")
    in_sums = _swap_sums(in_swap, ("w_in",), gathered[0], "in")
    in_chips = _chips_start(in_sums, "in")
    views = {"w_spatial": (GROUPS, BLOCK, BLOCK), "b_spatial": (GROUPS, BLOCK), "final_gain": (1, D_MODEL)}
    small_in = [[args[pre + n].reshape(views.get(n, args[n].shape)) for n in SMALL] for pre in ("", "m_", "v_")]
    loss, *small_out = _adamw_small(gathered, *small_in, after=in_chips[4])
    grads, deltas, new_m, new_v = {}, {}, {}, {}
    for dst, arrays in zip((grads, deltas, new_m, new_v), small_out):
        for n, arr in zip(SMALL, arrays):
            dst[n] = arr.reshape(args[n].shape)
    loss = loss[0, 0]

    after = small_out[1][0]
    for chips, sums, names, tag in ((early_chips, early_sums, early_names, "early"), (in_chips, in_sums, ("w_in",), "in")):
        for n, (part, _), recv in zip(names, sums, _chips_wait(chips, after, tag)):
            g, d, mo, vo = _final_adamw(part, recv, args[n][0], args["m_" + n][0], args["v_" + n][0], "adamw_" + n)
            grads[n], deltas[n], new_m[n], new_v[n] = g[None], d[None], mo[None], vo[None]
            after = d

    order = ("norm1_gain", "w_in", "gmlp_v_gain", "w_spatial", "b_spatial", "attn_sinks", "rel_bias_table", "w_out",
             "norm2_gain", "w_ff1", "w_ff2", "w_ple_proj", "w_ple_gate", "final_gain")
    return (loss, grad_x, *[grads[n] for n in order], *[deltas[n] for n in order],
            *[new_m[n] for n in order], *[new_v[n] for n in order])
```

```python
import functools
import math

import numpy as np
import jax
import jax.numpy as jnp
from jax import lax
from jax.experimental import pallas as pl
from jax.experimental.pallas import tpu as pltpu

F32 = jnp.float32
BF = jnp.bfloat16
MESH = pl.DeviceIdType.MESH
N_DEV = 8

D_MODEL = 1024
PLE_DIM = 256
D_GMLP = 512
GROUPS = 4
GDIM = 128
BLOCK = 128
D_ATTN = 512
HEAD_DIM = 64
N_Q = 8
Q_PER_KV = 4
N_KV = N_Q // Q_PER_KV
ROWS4 = Q_PER_KV * BLOCK
D_KV = 128
D_FF = 4096
D_IN = 1792
D_MAIN = 2 * D_GMLP + D_ATTN
REL_BUCKETS = 32
EPS = 1e-6
NEG_INF = -1e30
SCALE = HEAD_DIM ** -0.5
GELU_C = math.sqrt(2.0 / math.pi)
GELU_A = 0.044715

ADAM_LR = 0.001
ADAM_B1 = 0.9
ADAM_B2 = 0.999
ADAM_EPS = 1e-08
ADAM_WD = 0.01
ADAM_STEP = 10

V7X_VMEM_LIMIT = 60000 * 1024
TOK_TILE = 256


def _call(body, after=None, **kw):
    if after is None:
        return pl.pallas_call(body, **kw)
    n_in = len(kw["in_specs"])

    def ordered(*refs):
        body(*refs[:n_in], *refs[n_in + 1:])

    kw["in_specs"] = list(kw["in_specs"]) + [pl.BlockSpec(memory_space=pl.ANY)]
    fn = pl.pallas_call(ordered, **kw)
    return lambda *operands: fn(*operands, after)


def _params(sem=None):
    if sem is None:
        return pltpu.CompilerParams(vmem_limit_bytes=V7X_VMEM_LIMIT)
    return pltpu.CompilerParams(dimension_semantics=sem, vmem_limit_bytes=V7X_VMEM_LIMIT)


def _nn(a, b):
    return jnp.dot(a, b, preferred_element_type=F32)


def _nt(a, b):
    return lax.dot_general(a, b, (((1,), (1,)), ((), ())), preferred_element_type=F32)


def _tn(a, b):
    return lax.dot_general(a, b, (((0,), (0,)), ((), ())), preferred_element_type=F32)


def _gelu_tanh(x):
    return jnp.tanh(GELU_C * (x + GELU_A * (x * x * x)))


def _gelu(x):
    return x * (0.5 * (1.0 + _gelu_tanh(x)))


def _gelu_and_grad(x):
    t = _gelu_tanh(x)
    cdf = 0.5 * (1.0 + t)
    return x * cdf, cdf + 0.5 * x * (1.0 - t * t) * (GELU_C * (1.0 + 3.0 * GELU_A * (x * x)))


def _rms_scale(x):
    return lax.rsqrt(jnp.mean(x * x, axis=-1, keepdims=True) + EPS)


def _rms_bwd(dxn, x, r):
    return r * dxn - x * ((r * r * r) * jnp.mean(dxn * x, axis=-1, keepdims=True))


def _bucket_table():
    a = np.arange(BLOCK)[:, None]
    j = np.arange(2 * BLOCK)[None, :]
    n = BLOCK + a - j
    valid = (n >= 0) & (n < BLOCK)
    nc = np.maximum(n, 0)
    max_exact = REL_BUCKETS // 2
    nf = np.maximum(nc, 1).astype(np.float32)
    large = max_exact + (
        np.log(nf / np.float32(max_exact)) / np.float32(math.log(BLOCK / max_exact)) * np.float32(REL_BUCKETS - max_exact)
    ).astype(np.int32)
    large = np.minimum(large, REL_BUCKETS - 1)
    bucket = np.where(nc < max_exact, nc, large)
    return np.where(valid, bucket, -1).astype(np.int32)


def _in_proj(x, g1, w_in):
    s = x.shape[0]
    tm = min(TOK_TILE, s)

    def body(x_ref, g_ref, w_ref, zuv_ref, qkv_ref, hn_ref):
        xv = x_ref[...]
        hn = ((xv * _rms_scale(xv)) * g_ref[...]).astype(BF)
        hn_ref[...] = hn
        z = _nn(hn, w_ref[...])
        zuv_ref[...] = z[:, : 2 * D_GMLP]
        qkv_ref[...] = z[:, 2 * D_GMLP:].astype(BF)

    return _call(
        body,
        name="in_proj",
        grid=(s // tm,),
        in_specs=[
            pl.BlockSpec((tm, D_MODEL), lambda i: (i, 0)),
            pl.BlockSpec((1, D_MODEL), lambda i: (0, 0)),
            pl.BlockSpec((D_MODEL, D_IN), lambda i: (0, 0)),
        ],
        out_specs=[
            pl.BlockSpec((tm, 2 * D_GMLP), lambda i: (i, 0)),
            pl.BlockSpec((tm, D_ATTN + 2 * D_KV), lambda i: (i, 0)),
            pl.BlockSpec((tm, D_MODEL), lambda i: (i, 0)),
        ],
        out_shape=[
            jax.ShapeDtypeStruct((s, 2 * D_GMLP), F32),
            jax.ShapeDtypeStruct((s, D_ATTN + 2 * D_KV), BF),
            jax.ShapeDtypeStruct((s, D_MODEL), BF),
        ],
        compiler_params=_params(("arbitrary",)),
    )(x, g1, w_in)


def _head_rows(h):
    kh, g = divmod(h, Q_PER_KV)
    return kh, slice(g * BLOCK, (g + 1) * BLOCK)


def _build_bias(bias_ref, bucket_ref, table_ref):
    bucket = bucket_ref[...]
    for h in range(N_Q):
        acc = jnp.zeros((BLOCK, 2 * BLOCK), F32)
        for b in range(REL_BUCKETS):
            acc = jnp.where(bucket == b, table_ref[b, h], acc)
        kh, rows = _head_rows(h)
        bias_ref[kh, rows, :] = acc


def _window_masks(i):
    row = lax.broadcasted_iota(jnp.int32, (ROWS4, BLOCK), 0) & (BLOCK - 1)
    col = lax.broadcasted_iota(jnp.int32, (ROWS4, BLOCK), 1)
    return (col > row) & (i > 0), col <= row


def _stack_heads(ref, kh, offset):
    first = offset + kh * Q_PER_KV * HEAD_DIM
    return jnp.concatenate(
        [ref[:, first + g * HEAD_DIM: first + (g + 1) * HEAD_DIM].astype(BF) for g in range(Q_PER_KV)], axis=0)


def _stack_sinks(sink_ref, kh):
    return jnp.concatenate([jnp.full((BLOCK, 1), sink_ref[0, kh * Q_PER_KV + g], F32) for g in range(Q_PER_KV)], axis=0)


def _tril_bf16(w_ref, g):
    row = lax.broadcasted_iota(jnp.int32, (BLOCK, BLOCK), 0)
    col = lax.broadcasted_iota(jnp.int32, (BLOCK, BLOCK), 1)
    return jnp.where(col <= row, w_ref[g], 0.0).astype(BF)


def _attn_probs(q_h, k_prev, k_cur, bias_h, sink, valid_prev, valid_cur):
    l_prev = jnp.where(valid_prev, _nt(q_h, k_prev) * SCALE + bias_h[:, :BLOCK], NEG_INF)
    l_cur = jnp.where(valid_cur, _nt(q_h, k_cur) * SCALE + bias_h[:, BLOCK:], NEG_INF)
    m = jnp.maximum(jnp.maximum(jnp.max(l_prev, axis=-1, keepdims=True), jnp.max(l_cur, axis=-1, keepdims=True)), sink)
    e_prev = jnp.exp(l_prev - m)
    e_cur = jnp.exp(l_cur - m)
    e_sink = jnp.exp(sink - m)
    denom = jnp.sum(e_prev, axis=-1, keepdims=True) + jnp.sum(e_cur, axis=-1, keepdims=True) + e_sink
    return e_prev / denom, e_cur / denom, e_sink / denom


def _mixer_specs(nb):
    cl = lambda i: jnp.minimum(i, nb - 1)
    return [
        pl.BlockSpec((BLOCK, 2 * D_GMLP), lambda i: (cl(i), 0)),
        pl.BlockSpec((BLOCK, D_ATTN), lambda i: (cl(i), 0)),
        pl.BlockSpec((BLOCK, 2 * D_KV), lambda i: (cl(i), D_ATTN // (2 * D_KV))),
        pl.BlockSpec((BLOCK, 2 * D_KV), lambda i: (jnp.maximum(cl(i) - 1, 0), D_ATTN // (2 * D_KV))),
        pl.BlockSpec((1, D_GMLP), lambda i: (0, 0)),
        pl.BlockSpec((GROUPS, BLOCK, BLOCK), lambda i: (0, 0, 0)),
        pl.BlockSpec((GROUPS, BLOCK, 1), lambda i: (0, 0, 0)),
        pl.BlockSpec(memory_space=pltpu.SMEM),
        pl.BlockSpec(memory_space=pltpu.SMEM),
        pl.BlockSpec((BLOCK, 2 * BLOCK), lambda i: (0, 0)),
    ]


def _mixer_fwd(zuv, qkv, gv, w_sp, b_sp, sinks, table, bucket):
    s = zuv.shape[0]
    nb = s // BLOCK

    def body(zuv_ref, q_ref, kvc_ref, kvp_ref, gv_ref, w_ref, b_ref, sink_ref, table_ref, bucket_ref, mix_ref, bias_ref):
        i = pl.program_id(0)

        @pl.when(i == 0)
        def _():
            _build_bias(bias_ref, bucket_ref, table_ref)

        u = _gelu(zuv_ref[:, :D_GMLP])
        vg = _gelu(zuv_ref[:, D_GMLP:])
        for g in range(GROUPS):
            sl = slice(g * GDIM, (g + 1) * GDIM)
            vg_g = vg[:, sl]
            vn = ((vg_g * _rms_scale(vg_g)) * gv_ref[:, sl]).astype(BF)
            sv = _nn(_tril_bf16(w_ref, g), vn) + b_ref[g]
            mix_ref[:, sl] = (u[:, sl] * sv).astype(BF)

        valid_prev, valid_cur = _window_masks(i)
        for kh in range(N_KV):
            ksl = slice(kh * HEAD_DIM, (kh + 1) * HEAD_DIM)
            vsl = slice(D_KV + kh * HEAD_DIM, D_KV + (kh + 1) * HEAD_DIM)
            q4 = _stack_heads(q_ref, kh, 0)
            p_prev, p_cur, _ = _attn_probs(
                q4, kvp_ref[:, ksl], kvc_ref[:, ksl], bias_ref[kh], _stack_sinks(sink_ref, kh), valid_prev, valid_cur)
            o4 = _nn(p_prev.astype(BF), kvp_ref[:, vsl]) + _nn(p_cur.astype(BF), kvc_ref[:, vsl])
            for g in range(Q_PER_KV):
                first = D_GMLP + (kh * Q_PER_KV + g) * HEAD_DIM
                mix_ref[:, first:first + HEAD_DIM] = o4[g * BLOCK:(g + 1) * BLOCK].astype(BF)

    return _call(
        body,
        name="mixer_fwd",
        grid=(nb,),
        in_specs=_mixer_specs(nb),
        out_specs=pl.BlockSpec((BLOCK, D_MODEL), lambda i: (i, 0)),
        out_shape=jax.ShapeDtypeStruct((s, D_MODEL), BF),
        scratch_shapes=[pltpu.VMEM((N_KV, ROWS4, 2 * BLOCK), F32)],
        compiler_params=_params(("arbitrary",)),
    )(zuv, qkv, qkv, qkv, gv, w_sp, b_sp, sinks, table, bucket)


def _tail(x, mix, p, t, g2, gf, w_out, w_ff1, w_ff2, w_gate, w_proj):
    s = x.shape[0]
    tm = min(TOK_TILE, s)
    n_ff = w_ff1.shape[0]
    fc = D_FF // n_ff

    def body(x_ref, mix_ref, p_ref, t_ref, g2_ref, gf_ref, wo_ref, w1_ref, w2_ref, wg_ref, wp_ref,
             small_ref, dh1_ref, dh1b_ref, dmix_ref, hn2_ref, a_ref, df_ref, h2_ref, dh2_ref, dgl_ref, dpp_ref, f_ref):
        i = pl.program_id(0)

        @pl.when(i == 0)
        def _():
            small_ref[...] = jnp.zeros_like(small_ref)

        h1 = x_ref[...] + _nn(mix_ref[...], wo_ref[...])
        r2 = _rms_scale(h1)
        hn2 = ((h1 * r2) * g2_ref[...]).astype(BF)
        hn2_ref[...] = hn2
        h2 = h1
        for c in range(n_ff):
            f = _nn(hn2, w1_ref[c])
            f_ref[:, c * fc:(c + 1) * fc] = f
            a = jnp.square(jnp.maximum(f, 0.0)).astype(BF)
            a_ref[:, c * fc:(c + 1) * fc] = a
            h2 = h2 + _nn(a, w2_ref[c * fc:(c + 1) * fc, :])
        h2b = h2.astype(BF)
        h2_ref[...] = h2b
        gate = jax.nn.sigmoid(_nn(h2b, wg_ref[...]))
        pp = _nn(p_ref[...].astype(BF), wp_ref[...])
        h3 = h2 + gate * pp
        rf = _rms_scale(h3)
        gf_v = gf_ref[...]
        err = (h3 * rf) * gf_v - t_ref[...]
        small_ref[2:3, :] += jnp.sum(jnp.sum(err * err, axis=-1, keepdims=True), axis=0, keepdims=True) * (0.5 / D_MODEL)

        dy = err * (1.0 / D_MODEL)
        small_ref[1:2, :] += jnp.sum(dy * (h3 * rf), axis=0, keepdims=True)
        dh3 = _rms_bwd(dy * gf_v, h3, rf)
        dpp_ref[...] = (dh3 * gate).astype(BF)
        dgl = ((dh3 * pp) * (gate * (1.0 - gate))).astype(BF)
        dgl_ref[...] = dgl
        dh2 = dh3 + _nt(dgl, wg_ref[...])
        dh2b = dh2.astype(BF)
        dh2_ref[...] = dh2b
        dhn2 = jnp.zeros((tm, D_MODEL), F32)
        for c in range(n_ff):
            da = _nt(dh2b, w2_ref[c * fc:(c + 1) * fc, :])
            df = (da * (2.0 * jnp.maximum(f_ref[:, c * fc:(c + 1) * fc], 0.0))).astype(BF)
            df_ref[:, c * fc:(c + 1) * fc] = df
            dhn2 = dhn2 + _nt(df, w1_ref[c])
        small_ref[0:1, :] += jnp.sum(dhn2 * (h1 * r2), axis=0, keepdims=True)
        dh1 = dh2 + _rms_bwd(dhn2 * g2_ref[...], h1, r2)
        dh1_ref[...] = dh1
        dh1b = dh1.astype(BF)
        dh1b_ref[...] = dh1b
        dmix_ref[...] = _nt(dh1b, wo_ref[...])

    tile = lambda cols: pl.BlockSpec((tm, cols), lambda i: (i, 0))
    whole = lambda shape: pl.BlockSpec(shape, lambda i: (0,) * len(shape), pipeline_mode=pl.Buffered(1))
    row = pl.BlockSpec((1, D_MODEL), lambda i: (0, 0))
    act = lambda cols, dt: jax.ShapeDtypeStruct((s, cols), dt)
    return _call(
        body,
        name="tail",
        grid=(s // tm,),
        in_specs=[tile(D_MODEL), tile(D_MODEL), tile(PLE_DIM), tile(D_MODEL), row, row,
                  whole(w_out.shape), whole(w_ff1.shape), whole(w_ff2.shape), whole(w_gate.shape), whole(w_proj.shape)],
        out_specs=[pl.BlockSpec((8, D_MODEL), lambda i: (0, 0)), tile(D_MODEL), tile(D_MODEL), tile(D_MODEL), tile(D_MODEL), tile(D_FF),
                   tile(D_FF), tile(D_MODEL), tile(D_MODEL), tile(D_MODEL), tile(D_MODEL)],
        out_shape=[jax.ShapeDtypeStruct((8, D_MODEL), F32),
                   act(D_MODEL, F32), act(D_MODEL, BF), act(D_MODEL, F32), act(D_MODEL, BF), act(D_FF, BF), act(D_FF, BF), act(D_MODEL, BF),
                   act(D_MODEL, BF), act(D_MODEL, BF), act(D_MODEL, BF)],
        scratch_shapes=[pltpu.VMEM((tm, D_FF), F32)],
        compiler_params=_params(("arbitrary",)),
    )(x, mix, p, t, g2, gf, w_out, w_ff1, w_ff2, w_gate, w_proj)


def _mixer_bwd(zuv, qkv, dmix, gv, w_sp, b_sp, sinks, table, bucket, after=None):
    s = zuv.shape[0]
    nb = s // BLOCK

    def body(zuv_ref, q_ref, kvc_ref, kvp_ref, gv_ref, w_ref, b_ref, sink_ref, table_ref, bucket_ref, dmix_ref,
             dzm_ref, dkv_ref, dgv_ref, dw_ref, db_ref, dattn_ref,
             bias_ref, dbias_ref, carry_ref, dsink_acc, db_acc):
        i = pl.program_id(0)

        @pl.when(i == 0)
        def _():
            _build_bias(bias_ref, bucket_ref, table_ref)
            dbias_ref[...] = jnp.zeros_like(dbias_ref)
            carry_ref[...] = jnp.zeros_like(carry_ref)
            dsink_acc[...] = jnp.zeros_like(dsink_acc)
            dgv_ref[...] = jnp.zeros_like(dgv_ref)
            dw_ref[...] = jnp.zeros_like(dw_ref)
            db_acc[...] = jnp.zeros_like(db_acc)

        @pl.when(i < nb)
        def _():
            u, du_dz = _gelu_and_grad(zuv_ref[:, :D_GMLP])
            vg, dvg_dz = _gelu_and_grad(zuv_ref[:, D_GMLP:])
            for g in range(GROUPS):
                sl = slice(g * GDIM, (g + 1) * GDIM)
                vg_g = vg[:, sl]
                rg = _rms_scale(vg_g)
                vhat = vg_g * rg
                gain = gv_ref[:, sl]
                vn = (vhat * gain).astype(BF)
                w_g = _tril_bf16(w_ref, g)
                sv = _nn(w_g, vn) + b_ref[g]
                dmix_a = dmix_ref[:, sl]
                dsv = dmix_a * u[:, sl]
                dsvb = dsv.astype(BF)
                db_acc[g] += jnp.sum(dsv, axis=-1, keepdims=True)
                dw_ref[g] += _nt(dsvb, vn)
                dvn = _tn(w_g, dsvb)
                dgv_ref[:, sl] += jnp.sum(dvn * vhat, axis=0, keepdims=True)
                dvg = _rms_bwd(dvn * gain, vg_g, rg)
                dzm_ref[:, sl] = ((dmix_a * sv) * du_dz[:, sl]).astype(BF)
                dzm_ref[:, D_GMLP + g * GDIM: D_GMLP + (g + 1) * GDIM] = (dvg * dvg_dz[:, sl]).astype(BF)

            valid_prev, valid_cur = _window_masks(i)
            for kh in range(N_KV):
                ksl = slice(kh * HEAD_DIM, (kh + 1) * HEAD_DIM)
                vsl = slice(D_KV + kh * HEAD_DIM, D_KV + (kh + 1) * HEAD_DIM)
                k_prev, k_cur = kvp_ref[:, ksl], kvc_ref[:, ksl]
                v_prev, v_cur = kvp_ref[:, vsl], kvc_ref[:, vsl]
                q4 = _stack_heads(q_ref, kh, 0)
                p_prev, p_cur, p_sink = _attn_probs(
                    q4, k_prev, k_cur, bias_ref[kh], _stack_sinks(sink_ref, kh), valid_prev, valid_cur)
                do4 = _stack_heads(dmix_ref, kh, D_GMLP)
                dp_prev = _nt(do4, v_prev)
                dp_cur = _nt(do4, v_cur)
                delta = jnp.sum(p_prev * dp_prev, axis=-1, keepdims=True) + jnp.sum(p_cur * dp_cur, axis=-1, keepdims=True)
                ds_prev = p_prev * (dp_prev - delta)
                ds_cur = p_cur * (dp_cur - delta)
                dsink_acc[kh] -= p_sink * delta
                dbias_ref[kh, :, :BLOCK] += ds_prev
                dbias_ref[kh, :, BLOCK:] += ds_cur
                dsb_prev = ds_prev.astype(BF)
                dsb_cur = ds_cur.astype(BF)
                dq4 = (_nn(dsb_prev, k_prev) + _nn(dsb_cur, k_cur)) * SCALE
                for g in range(Q_PER_KV):
                    first = 2 * D_GMLP + (kh * Q_PER_KV + g) * HEAD_DIM
                    dzm_ref[:, first:first + HEAD_DIM] = dq4[g * BLOCK:(g + 1) * BLOCK].astype(BF)
                dkv_ref[:, ksl] = (carry_ref[:, ksl] + _tn(dsb_prev, q4) * SCALE).astype(BF)
                dkv_ref[:, vsl] = (carry_ref[:, vsl] + _tn(p_prev.astype(BF), do4)).astype(BF)
                carry_ref[:, ksl] = _tn(dsb_cur, q4) * SCALE
                carry_ref[:, vsl] = _tn(p_cur.astype(BF), do4)

        @pl.when(i == nb)
        def _():
            dkv_ref[...] = carry_ref[...].astype(BF)
            row = lax.broadcasted_iota(jnp.int32, (BLOCK, BLOCK), 0)
            col = lax.broadcasted_iota(jnp.int32, (BLOCK, BLOCK), 1)
            for g in range(GROUPS):
                dw_ref[g] = jnp.where(col <= row, dw_ref[g], 0.0)
                db_ref[g:g + 1, :] = jnp.sum(jnp.where(col == row, db_acc[g], 0.0), axis=0, keepdims=True)
            bucket = bucket_ref[...]
            for h in range(N_Q):
                kh, rows = _head_rows(h)
                dattn_ref[REL_BUCKETS, h] = jnp.sum(dsink_acc[kh, rows, :])
                dbh = dbias_ref[kh, rows, :]
                for b in range(REL_BUCKETS):
                    dattn_ref[b, h] = jnp.sum(jnp.where(bucket == b, dbh, 0.0))

    cl = lambda i: jnp.minimum(i, nb - 1)
    const = lambda shape: pl.BlockSpec(shape, lambda i: (0,) * len(shape))
    return _call(
        body,
        name="mixer_bwd",
        after=after,
        grid=(nb + 1,),
        in_specs=_mixer_specs(nb) + [pl.BlockSpec((BLOCK, D_MODEL), lambda i: (cl(i), 0))],
        out_specs=[
            pl.BlockSpec((BLOCK, D_MAIN), lambda i: (cl(i), 0)),
            pl.BlockSpec((BLOCK, 2 * D_KV), lambda i: (jnp.maximum(i - 1, 0), 0)),
            const((1, D_GMLP)),
            const((GROUPS, BLOCK, BLOCK)),
            const((GROUPS, BLOCK)),
            pl.BlockSpec(memory_space=pltpu.SMEM),
        ],
        out_shape=[
            jax.ShapeDtypeStruct((s, D_MAIN), BF),
            jax.ShapeDtypeStruct((s, 2 * D_KV), BF),
            jax.ShapeDtypeStruct((1, D_GMLP), F32),
            jax.ShapeDtypeStruct((GROUPS, BLOCK, BLOCK), F32),
            jax.ShapeDtypeStruct((GROUPS, BLOCK), F32),
            jax.ShapeDtypeStruct((REL_BUCKETS + 1, N_Q), F32),
        ],
        scratch_shapes=[
            pltpu.VMEM((N_KV, ROWS4, 2 * BLOCK), F32),
            pltpu.VMEM((N_KV, ROWS4, 2 * BLOCK), F32),
            pltpu.VMEM((BLOCK, 2 * D_KV), F32),
            pltpu.VMEM((N_KV, ROWS4, 1), F32),
            pltpu.VMEM((GROUPS, BLOCK, 1), F32),
        ],
        compiler_params=_params(("arbitrary",)),
    )(zuv, qkv, qkv, qkv, gv, w_sp, b_sp, sinks, table, bucket, dmix)


def _in_bwd(x, dh1, dzm, dkv, g1, w_in, after=None):
    s = x.shape[0]
    tm = min(TOK_TILE, s)

    def body(x_ref, dh1_ref, dzm_ref, dkv_ref, g_ref, w_ref, dx_ref, dg_ref):
        @pl.when(pl.program_id(0) == 0)
        def _():
            dg_ref[...] = jnp.zeros_like(dg_ref)

        dhn = _nt(dzm_ref[...], w_ref[:, :D_MAIN]) + _nt(dkv_ref[...], w_ref[:, D_MAIN:])
        xv = x_ref[...]
        r = _rms_scale(xv)
        dg_ref[...] += jnp.sum(dhn * (xv * r), axis=0, keepdims=True)
        dx_ref[...] = dh1_ref[...] + _rms_bwd(dhn * g_ref[...], xv, r)

    tile = lambda cols: pl.BlockSpec((tm, cols), lambda i: (i, 0))
    row = pl.BlockSpec((1, D_MODEL), lambda i: (0, 0))
    return _call(
        body,
        name="in_bwd",
        after=after,
        grid=(s // tm,),
        in_specs=[tile(D_MODEL), tile(D_MODEL), tile(D_MAIN), tile(2 * D_KV), row, pl.BlockSpec((D_MODEL, D_IN), lambda i: (0, 0))],
        out_specs=[tile(D_MODEL), row],
        out_shape=[jax.ShapeDtypeStruct((s, D_MODEL), F32), jax.ShapeDtypeStruct((1, D_MODEL), F32)],
        compiler_params=_params(("arbitrary",)),
    )(x, dh1, dzm, dkv, g1, w_in)


def _wgrad(a, b, tm, tn, name, peer_cols=False, after=None):
    s, m = a.shape
    n = b.shape[1]

    def body(a_ref, b_ref, o_ref, at_ref):
        @pl.when(pl.program_id(1) == 0)
        def _():
            at_ref[...] = a_ref[...].astype(BF).T

        r = _nn(at_ref[...], b_ref[...])
        if peer_cols:
            o_ref[0] = r
        else:
            o_ref[...] = r

    if peer_cols:
        out_spec = pl.BlockSpec((1, tm, tn), lambda i, j: (j, i, 0))
        out_shape = jax.ShapeDtypeStruct((n // tn, m, tn), F32)
    else:
        out_spec = pl.BlockSpec((tm, tn), lambda i, j: (i, j))
        out_shape = jax.ShapeDtypeStruct((m, n), F32)
    return _call(
        body,
        name=name,
        after=after,
        grid=(m // tm, n // tn),
        in_specs=[pl.BlockSpec((s, tm), lambda i, j: (0, i)), pl.BlockSpec((s, tn), lambda i, j: (0, j))],
        out_specs=out_spec,
        out_shape=out_shape,
        scratch_shapes=[pltpu.VMEM((tm, s), BF)],
        compiler_params=_params(("arbitrary", "arbitrary")),
    )(a, b)


def _adamw_math(w, g, m, v):
    m_new = ADAM_B1 * m + (1.0 - ADAM_B1) * g
    v_new = ADAM_B2 * v + (1.0 - ADAM_B2) * jnp.square(g)
    m_hat = m_new / (1.0 - ADAM_B1 ** ADAM_STEP)
    v_hat = v_new / (1.0 - ADAM_B2 ** ADAM_STEP)
    delta = -ADAM_LR * (m_hat / (jnp.sqrt(v_hat) + ADAM_EPS) + ADAM_WD * w)
    return delta, m_new, v_new


def _final_adamw(part, recv, w, m, v, name):
    r, c = w.shape
    tr = min(r, 256)

    def body(p_ref, r_ref, w_ref, m_ref, v_ref, g_ref, d_ref, mo_ref, vo_ref):
        g = p_ref[...]
        for j in range(3):
            g = g + r_ref[j].astype(F32)
        g_ref[...] = g
        d_ref[...], mo_ref[...], vo_ref[...] = _adamw_math(w_ref[...], g, m_ref[...], v_ref[...])

    spec = pl.BlockSpec((tr, c), lambda i: (i, 0))
    return _call(
        body,
        name=name,
        grid=(r // tr,),
        in_specs=[spec, pl.BlockSpec((3, tr, c), lambda i: (0, i, 0)), spec, spec, spec],
        out_specs=[spec] * 4,
        out_shape=[jax.ShapeDtypeStruct((r, c), F32)] * 4,
        compiler_params=_params(("arbitrary",)),
    )(part, recv, w, m, v)


def _rs_sum(g, land, blocks, name):
    _, r, c = g.shape
    tr = min(r, 256)

    def body(blk_ref, g0_ref, g1_ref, g2_ref, g3_ref, l_ref, part_ref, send_ref):
        part_ref[...] = g0_ref[0] + l_ref[0]
        for j, gj_ref in enumerate((g1_ref, g2_ref, g3_ref)):
            send_ref[j] = (gj_ref[0] + l_ref[j + 1]).astype(BF)

    def pick(j):
        return pl.BlockSpec((1, tr, c), lambda i, blk: (blk[j], i, 0))

    return _call(
        body,
        name=name,
        grid_spec=pltpu.PrefetchScalarGridSpec(
            num_scalar_prefetch=1,
            grid=(r // tr,),
            in_specs=[pick(0), pick(1), pick(2), pick(3), pl.BlockSpec((4, tr, c), lambda i, blk: (0, i, 0))],
            out_specs=[pl.BlockSpec((tr, c), lambda i, blk: (i, 0)), pl.BlockSpec((3, tr, c), lambda i, blk: (0, i, 0))],
        ),
        out_shape=[jax.ShapeDtypeStruct((r, c), F32), jax.ShapeDtypeStruct((3, r, c), BF)],
        compiler_params=_params(("arbitrary",)),
    )(blocks, g, g, g, g, land)


def _adamw_small(gathered, weights, moms, vels, after):
    n_w = len(weights)

    def body(*refs):
        tail_ref, g1_ref, gv_ref, wsp_ref, bsp_ref, attn_ref = refs[:6]
        w_refs, m_refs, v_refs = refs[6:6 + n_w], refs[6 + n_w:6 + 2 * n_w], refs[6 + 2 * n_w:6 + 3 * n_w]
        outs = refs[6 + 3 * n_w:]
        loss_ref, g_refs, d_refs = outs[0], outs[1:1 + n_w], outs[1 + n_w:1 + 2 * n_w]
        mo_refs, vo_refs = outs[1 + 2 * n_w:1 + 3 * n_w], outs[1 + 3 * n_w:]

        def total(part):
            acc = part(0)
            for d in range(1, N_DEV):
                acc = acc + part(d)
            return acc

        grads = [
            total(lambda d: g1_ref[d]), total(lambda d: gv_ref[d]), total(lambda d: wsp_ref[d]), total(lambda d: bsp_ref[d]),
            total(lambda d: attn_ref[d, REL_BUCKETS:, :]), total(lambda d: attn_ref[d, :REL_BUCKETS, :]),
            total(lambda d: tail_ref[d, 0:1, :]), total(lambda d: tail_ref[d, 1:2, :])]
        loss_ref[...] = total(lambda d: tail_ref[d, 2:3, 0:1])
        for k in range(n_w):
            g_refs[k][...] = grads[k]
            d_refs[k][...], mo_refs[k][...], vo_refs[k][...] = _adamw_math(w_refs[k][...], grads[k], m_refs[k][...], v_refs[k][...])

    shapes = [jax.ShapeDtypeStruct(w.shape, F32) for w in weights]
    outs = _call(
        body,
        name="adamw_small",
        after=after,
        in_specs=[pl.BlockSpec(memory_space=pltpu.VMEM)] * (6 + 3 * n_w),
        out_shape=[jax.ShapeDtypeStruct((1, 1), F32)] + shapes * 4,
        compiler_params=_params(),
    )(*gathered, *weights, *moms, *vels)
    return outs[0], outs[1:1 + n_w], outs[1 + n_w:1 + 2 * n_w], outs[1 + 2 * n_w:1 + 3 * n_w], outs[1 + 3 * n_w:]


def _place():
    x, y, c = lax.axis_index("x"), lax.axis_index("y"), lax.axis_index("c")
    return x, y, c, [(1 - x, y), (x, 1 - y), (1 - x, 1 - y)]


def _dev_index(px, py, pc):
    return 4 * px + 2 * py + pc


def _all_gather(shards, out_dtype, name, after=None):
    k_n = len(shards)

    def body(*refs):
        ins, outs, stage = refs[:k_n], refs[k_n:2 * k_n], refs[2 * k_n:3 * k_n]
        send_sems, recv_sems, local_sems = refs[3 * k_n:]
        x, y, c, chips = _place()
        me, sibling = (x, y, c), (x, y, 1 - c)

        def copy(k, j, block, to, src=None):
            dst = outs[k].at[_dev_index(*block)]
            return pltpu.make_async_remote_copy(
                src_ref=dst if src is None else src, dst_ref=dst,
                send_sem=send_sems.at[k, j], recv_sem=recv_sems.at[k, j], device_id=to, device_id_type=MESH)

        mine, first = [], []
        for k in range(k_n):
            stage[k][...] = ins[k][...].astype(out_dtype)
            mine.append(pltpu.make_async_copy(stage[k], outs[k].at[_dev_index(*me)], local_sems.at[k]))
            mine[k].start()
            first.append(copy(k, 0, me, sibling, src=stage[k]))
            first += [copy(k, 1 + j, me, (*chip, c), src=stage[k]) for j, chip in enumerate(chips)]
        for cp in first:
            cp.start()
        passed = []
        for k in range(k_n):
            for j, chip in enumerate(chips):
                copy(k, 1 + j, (*chip, c), me).wait_recv()
                passed.append(copy(k, 4 + j, (*chip, c), sibling))
                passed[-1].start()
        for k in range(k_n):
            copy(k, 0, sibling, me).wait_recv()
            for j, chip in enumerate(chips):
                copy(k, 4 + j, (*chip, 1 - c), me).wait_recv()
        for cp in first + passed:
            cp.wait_send()
        for cp in mine:
            cp.wait()

    return _call(
        body,
        name=name,
        after=after,
        in_specs=[pl.BlockSpec(memory_space=pltpu.VMEM)] * k_n,
        out_specs=[pl.BlockSpec(memory_space=pl.ANY)] * k_n,
        out_shape=[jax.ShapeDtypeStruct((N_DEV,) + sh.shape, out_dtype) for sh in shards],
        scratch_shapes=[pltpu.VMEM(sh.shape, out_dtype) for sh in shards]
        + [pltpu.SemaphoreType.DMA((k_n, 7)), pltpu.SemaphoreType.DMA((k_n, 7)), pltpu.SemaphoreType.DMA((k_n,))],
        compiler_params=_params(),
    )(*shards)


HBM_SPEC = pl.BlockSpec(memory_space=pltpu.HBM)
SEM_SPEC = pl.BlockSpec(memory_space=pltpu.SEMAPHORE)
ANY_SPEC = pl.BlockSpec(memory_space=pl.ANY)
DATAFLOW = pltpu.SideEffectType.DATAFLOW_SIDE_EFFECTING


def _hbm(a):
    return pltpu.with_memory_space_constraint(a, pltpu.HBM)


def _prep_weights(shards):
    k_n = len(shards)

    def body(*refs):
        ins, outs, stage, sems = refs[:k_n], refs[k_n:2 * k_n], refs[2 * k_n:3 * k_n], refs[3 * k_n]
        x, y, c, _ = _place()
        copies = []
        for k in range(k_n):
            stage[k][...] = ins[k][...].astype(BF)
            copies.append(pltpu.make_async_copy(stage[k], outs[k].at[_dev_index(x, y, c)], sems.at[k]))
            copies[k].start()
        for cp in copies:
            cp.wait()

    return _call(
        body,
        name="prep_weights",
        in_specs=[pl.BlockSpec(memory_space=pltpu.VMEM)] * k_n,
        out_specs=[ANY_SPEC] * k_n,
        out_shape=[jax.ShapeDtypeStruct((N_DEV,) + sh.shape, BF) for sh in shards],
        scratch_shapes=[pltpu.VMEM(sh.shape, BF) for sh in shards] + [pltpu.SemaphoreType.DMA((k_n,))],
        compiler_params=_params(),
    )(*shards)


def _ag_start(first, rest):
    lands = list(first) + list(rest)
    k_n, k_first = len(lands), len(first)

    def body(*refs):
        land = refs[:k_n]
        sems = refs[k_n:k_n + 4]
        token = refs[-1]
        x, y, c, chips = _place()
        targets = [(x, y, 1 - c)] + [(*chip, c) for chip in chips]
        for k in range(k_n):
            blk = land[k].at[_dev_index(x, y, c)]
            send_sems, recv_sems, base = (sems[0], sems[1], k) if k < k_first else (sems[2], sems[3], k - k_first)
            for j, to in enumerate(targets):
                pltpu.make_async_remote_copy(
                    src_ref=blk, dst_ref=blk, send_sem=send_sems.at[4 * base + j], recv_sem=recv_sems.at[4 * base + j],
                    device_id=to, device_id_type=MESH).start()
        token[...] = jnp.zeros_like(token)

    k_rest = k_n - k_first
    outs = pl.pallas_call(
        body,
        name="ag_start",
        in_specs=[HBM_SPEC] * k_n,
        out_specs=(SEM_SPEC, SEM_SPEC, SEM_SPEC, SEM_SPEC, *[HBM_SPEC] * k_n, pl.BlockSpec(memory_space=pltpu.VMEM)),
        out_shape=(pltpu.SemaphoreType.DMA((4 * k_first,)), pltpu.SemaphoreType.DMA((4 * k_first,)),
                   pltpu.SemaphoreType.DMA((4 * k_rest,)), pltpu.SemaphoreType.DMA((4 * k_rest,)),
                   *[pltpu.HBM(a.shape, a.dtype) for a in lands], jax.ShapeDtypeStruct((8, LANES), F32)),
        input_output_aliases={k: 4 + k for k in range(k_n)},
        compiler_params=pltpu.CompilerParams(has_side_effects=DATAFLOW),
    )(*[_hbm(a) for a in lands])
    flying = list(outs[4:4 + k_n])
    return (outs[0], outs[1], flying[:k_first]), (outs[2], outs[3], flying[k_first:]), outs[-1]


def _ag_mid(lands, send_sems, recv_sems, after, name):
    k_n = len(lands)

    def body(*refs):
        land = refs[:k_n]
        send1, recv1 = refs[k_n], refs[k_n + 1]
        fwd_send, fwd_recv = refs[-2], refs[-1]
        x, y, c, chips = _place()
        sources = [(x, y, 1 - c)] + [(*chip, c) for chip in chips]
        for k in range(k_n):
            mine = land[k].at[_dev_index(x, y, c)]
            for j, frm in enumerate(sources):
                got = land[k].at[_dev_index(*frm)]
                cp = pltpu.make_async_remote_copy(
                    src_ref=mine, dst_ref=got, send_sem=send1.at[4 * k + j], recv_sem=recv1.at[4 * k + j], device_id=frm, device_id_type=MESH)
                cp.wait_send()
                cp.wait_recv()
                if j >= 1:
                    pltpu.make_async_remote_copy(
                        src_ref=got, dst_ref=got, send_sem=fwd_send.at[3 * k + j - 1], recv_sem=fwd_recv.at[3 * k + j - 1],
                        device_id=(x, y, 1 - c), device_id_type=MESH).start()

    outs = pl.pallas_call(
        body,
        name=name,
        in_specs=[HBM_SPEC] * k_n + [SEM_SPEC, SEM_SPEC, ANY_SPEC],
        out_specs=(*[HBM_SPEC] * k_n, SEM_SPEC, SEM_SPEC),
        out_shape=(*[pltpu.HBM(a.shape, a.dtype) for a in lands], pltpu.SemaphoreType.DMA((3 * k_n,)), pltpu.SemaphoreType.DMA((3 * k_n,))),
        input_output_aliases={k: k for k in range(k_n)},
        compiler_params=pltpu.CompilerParams(has_side_effects=DATAFLOW),
    )(*lands, send_sems, recv_sems, after)
    return list(outs[:k_n]), outs[-2], outs[-1]


def _ag_end(lands, fwd_send, fwd_recv, name):
    k_n = len(lands)

    def body(*refs):
        land = refs[:k_n]
        fsend, frecv = refs[k_n], refs[k_n + 1]
        x, y, c, chips = _place()
        for k in range(k_n):
            for j, chip in enumerate(chips):
                cp = pltpu.make_async_remote_copy(
                    src_ref=land[k].at[_dev_index(*chip, c)], dst_ref=land[k].at[_dev_index(*chip, 1 - c)],
                    send_sem=fsend.at[3 * k + j], recv_sem=frecv.at[3 * k + j], device_id=(x, y, 1 - c), device_id_type=MESH)
                cp.wait_send()
                cp.wait_recv()

    outs = pl.pallas_call(
        body,
        name=name,
        in_specs=[HBM_SPEC] * k_n + [SEM_SPEC, SEM_SPEC],
        out_specs=tuple([HBM_SPEC] * k_n),
        out_shape=tuple(pltpu.HBM(a.shape, a.dtype) for a in lands),
        input_output_aliases={k: k for k in range(k_n)},
        compiler_params=pltpu.CompilerParams(has_side_effects=DATAFLOW),
    )(*lands, fwd_send, fwd_recv)
    return list(outs)


def _chips4():
    x, y, c, others = _place()
    return x, y, c, [(x, y)] + others


def _route_sibling(j):
    x, y, c, chips = _chips4()
    return _dev_index(*chips[j], 1 - c), j, (x, y, 1 - c)


def _route_chips(j):
    x, y, c, chips = _chips4()
    return j, j, (*chips[j + 1], c)


def _xchg_copy(route, n, k, j, src, dst, send_sems, recv_sems):
    si, di, peer = route(j)
    return pltpu.make_async_remote_copy(
        src_ref=src[k].at[si], dst_ref=dst[k].at[di], send_sem=send_sems.at[n * k + j], recv_sem=recv_sems.at[n * k + j],
        device_id=peer, device_id_type=MESH)


def _xchg_start(srcs, slot_shapes, route, n, name):
    k_n = len(srcs)
    dsts = [lax.empty((n,) + tuple(sh), a.dtype) for sh, a in zip(slot_shapes, srcs)]

    def body(*refs):
        src, dst = refs[:k_n], refs[k_n:2 * k_n]
        send_sems, recv_sems, token = refs[2 * k_n], refs[2 * k_n + 1], refs[-1]
        for k in range(k_n):
            for j in range(n):
                _xchg_copy(route, n, k, j, src, dst, send_sems, recv_sems).start()
        token[...] = jnp.zeros_like(token)

    arrays = list(srcs) + dsts
    outs = pl.pallas_call(
        body,
        name=name,
        in_specs=[HBM_SPEC] * (2 * k_n),
        out_specs=(SEM_SPEC, SEM_SPEC, *[HBM_SPEC] * (2 * k_n), pl.BlockSpec(memory_space=pltpu.VMEM)),
        out_shape=(pltpu.SemaphoreType.DMA((n * k_n,)), pltpu.SemaphoreType.DMA((n * k_n,)),
                   *[pltpu.HBM(a.shape, a.dtype) for a in arrays], jax.ShapeDtypeStruct((8, LANES), F32)),
        input_output_aliases={i: 2 + i for i in range(2 * k_n)},
        compiler_params=pltpu.CompilerParams(has_side_effects=DATAFLOW),
    )(*[_hbm(a) for a in arrays])
    return outs[0], outs[1], list(outs[2:2 + k_n]), list(outs[2 + k_n:2 + 2 * k_n]), outs[-1]


def _xchg_wait(send_sems, recv_sems, srcs, dsts, route, n, after, name):
    k_n = len(srcs)

    def body(*refs):
        src, dst = refs[:k_n], refs[k_n:2 * k_n]
        send, recv = refs[2 * k_n], refs[2 * k_n + 1]
        for k in range(k_n):
            for j in range(n):
                cp = _xchg_copy(route, n, k, j, src, dst, send, recv)
                cp.wait_send()
                cp.wait_recv()

    arrays = list(srcs) + list(dsts)
    outs = pl.pallas_call(
        body,
        name=name,
        in_specs=[HBM_SPEC] * (2 * k_n) + [SEM_SPEC, SEM_SPEC, ANY_SPEC],
        out_specs=tuple([HBM_SPEC] * (2 * k_n)),
        out_shape=tuple(pltpu.HBM(a.shape, a.dtype) for a in arrays),
        input_output_aliases={i: i for i in range(2 * k_n)},
        compiler_params=pltpu.CompilerParams(has_side_effects=DATAFLOW),
    )(*arrays, send_sems, recv_sems, after)
    return list(outs[:k_n]), list(outs[k_n:])


SMALL = ("norm1_gain", "gmlp_v_gain", "w_spatial", "b_spatial", "attn_sinks", "rel_bias_table", "norm2_gain", "final_gain")
LANES = 128


def _swap_start(grads, tag):
    return _xchg_start(grads, [g.shape[1:] for g in grads], _route_sibling, 4, f"rs_{tag}_swap_start")


def _swap_sums(swap, names, after, tag):
    send1, recv1, src1, land1, _ = swap
    x, y, c, chips = _chips4()
    blocks = jnp.stack([_dev_index(*chip, c) for chip in chips]).astype(jnp.int32)
    src1, land1 = _xchg_wait(send1, recv1, src1, land1, _route_sibling, 4, after, f"rs_{tag}_swap_wait")
    return [_rs_sum(g, land, blocks, f"rs_sum_{n}") for g, land, n in zip(src1, land1, names)]


def _chips_start(sums, tag):
    sends = [ps[1] for ps in sums]
    return _xchg_start(sends, [a.shape[1:] for a in sends], _route_chips, 3, f"rs_{tag}_chips_start")


def _chips_wait(chips, after, tag):
    send2, recv2, src2, land2, _ = chips
    return _xchg_wait(send2, recv2, src2, land2, _route_chips, 3, after, f"rs_{tag}_chips_wait")[1]


def kernel(x, p, norm1_gain, w_in, gmlp_v_gain, w_spatial, b_spatial, attn_sinks, rel_bias_table, w_out, norm2_gain, w_ff1, w_ff2, w_ple_proj, w_ple_gate, final_gain, loss_target, m_norm1_gain, m_w_in, m_gmlp_v_gain, m_w_spatial, m_b_spatial, m_attn_sinks, m_rel_bias_table, m_w_out, m_norm2_gain, m_w_ff1, m_w_ff2, m_w_ple_proj, m_w_ple_gate, m_final_gain, v_norm1_gain, v_w_in, v_gmlp_v_gain, v_w_spatial, v_b_spatial, v_attn_sinks, v_rel_bias_table, v_w_out, v_norm2_gain, v_w_ff1, v_w_ff2, v_w_ple_proj, v_w_ple_gate, v_final_gain):
    args = dict(locals())
    s = x.shape[1]
    big = ("w_in", "w_out", "w_ff1", "w_ff2", "w_ple_proj", "w_ple_gate")

    x2, p2, t2 = x.reshape(s, D_MODEL), p.reshape(s, PLE_DIM), loss_target.reshape(s, D_MODEL)
    g1, gv, w_sp, b_sp, sinks, table, g2, gf = (args[n] for n in SMALL)
    bucket = jnp.asarray(_bucket_table())
    b_col = b_sp.reshape(GROUPS, BLOCK, 1)

    lands = _prep_weights([args[n][0] for n in big])
    (send_in, recv_in, fly_in), (send_rest, recv_rest, fly_rest), token = _ag_start(lands[:1], lands[1:])
    mid_in, fwd_send_in, fwd_recv_in = _ag_mid(fly_in, send_in, recv_in, token, "ag_mid_w_in")
    g_in = _ag_end(mid_in, fwd_send_in, fwd_recv_in, "ag_end_w_in")[0]
    full_in = g_in.transpose(1, 0, 2).reshape(D_MODEL, D_IN)

    zuv, qkv, hn1 = _in_proj(x2, g1, full_in)
    mix = _mixer_fwd(zuv, qkv, gv, w_sp[0], b_col, sinks, table, bucket)
    mid, fwd_send, fwd_recv = _ag_mid(fly_rest, send_rest, recv_rest, mix, "ag_mid_rest")
    g_out, g_ff1, g_ff2, g_proj, g_gate = _ag_end(mid, fwd_send, fwd_recv, "ag_end_rest")
    full_out, full_ff2, full_gate = g_out.reshape(D_MODEL, D_MODEL), g_ff2.reshape(D_FF, D_MODEL), g_gate.reshape(D_MODEL, D_MODEL)
    full_proj = g_proj.transpose(1, 0, 2).reshape(PLE_DIM, D_MODEL)

    (tail_small, dh1, dh1b, dmix, hn2, a, df, h2, dh2, dgl, dpp) = _tail(
        x2, mix, p2, t2, g2, gf.reshape(1, D_MODEL), full_out, g_ff1, full_ff2, full_gate, full_proj)

    gw_ff1 = _wgrad(hn2, df, D_MODEL, D_FF // N_DEV, "wgrad_ff1", peer_cols=True)
    gw_ff2 = _wgrad(a, dh2, 1024, D_MODEL, "wgrad_ff2").reshape(N_DEV, D_FF // N_DEV, D_MODEL)
    ffn_swap = _swap_start([gw_ff1, gw_ff2], "ffn")
    gw_gate = _wgrad(h2, dgl, D_MODEL, 512, "wgrad_gate", after=ffn_swap[4]).reshape(N_DEV, D_MODEL // N_DEV, D_MODEL)
    gw_proj = _wgrad(p2, dpp, PLE_DIM, D_MODEL // N_DEV, "wgrad_proj", peer_cols=True, after=ffn_swap[4])
    gw_out = _wgrad(mix, dh1b, D_MODEL, 512, "wgrad_out", after=ffn_swap[4]).reshape(N_DEV, D_MODEL // N_DEV, D_MODEL)
    mid_swap = _swap_start([gw_out, gw_proj, gw_gate], "mid")
    early_names = ("w_ff1", "w_ff2", "w_out", "w_ple_proj", "w_ple_gate")
    early_sums = _swap_sums(ffn_swap, early_names[:2], mid_swap[4], "ffn")
    early_sums += _swap_sums(mid_swap, early_names[2:], early_sums[-1][0], "mid")
    early_chips = _chips_start(early_sums, "early")

    dzm, dkv, d_gv, d_wsp, d_bsp, d_attn = _mixer_bwd(
        zuv, qkv, dmix, gv, w_sp[0], b_col, sinks, table, bucket, after=early_chips[4])
    gw_in = jnp.concatenate([_wgrad(hn1, dzm, D_MODEL, 512, "wgrad_in_main"), _wgrad(hn1, dkv, D_MODEL, 256, "wgrad_in_kv")], axis=1)
    gw_in = gw_in.reshape(D_MODEL, N_DEV, D_IN // N_DEV).transpose(1, 0, 2)
    in_swap = _swap_start([gw_in], "in")
    dx, d_g1 = _in_bwd(x2, dh1, dzm, dkv, g1, full_in, after=in_swap[4])
    grad_x = dx.reshape(x.shape)

    gathered = _all_gather([tail_small, d_g1, d_gv, d_wsp, d_bsp, d_attn], F32, "gather_small_grads")
    in_sums = _swap_sums(in_swap, ("w_in",), gathered[0], "in")
    in_chips = _chips_start(in_sums, "in")
    views = {"w_spatial": (GROUPS, BLOCK, BLOCK), "b_spatial": (GROUPS, BLOCK), "final_gain": (1, D_MODEL)}
    small_in = [[args[pre + n].reshape(views.get(n, args[n].shape)) for n in SMALL] for pre in ("", "m_", "v_")]
    loss, *small_out = _adamw_small(gathered, *small_in, after=in_chips[4])
    grads, deltas, new_m, new_v = {}, {}, {}, {}
    for dst, arrays in zip((grads, deltas, new_m, new_v), small_out):
        for n, arr in zip(SMALL, arrays):
            dst[n] = arr.reshape(args[n].shape)
    loss = loss[0, 0]

    after = small_out[1][0]
    for chips, sums, names, tag in ((early_chips, early_sums, early_names, "early"), (in_chips, in_sums, ("w_in",), "in")):
        for n, (part, _), recv in zip(names, sums, _chips_wait(chips, after, tag)):
            g, d, mo, vo = _final_adamw(part, recv, args[n][0], args["m_" + n][0], args["v_" + n][0], "adamw_" + n)
            grads[n], deltas[n], new_m[n], new_v[n] = g[None], d[None], mo[None], vo[None]
            after = d

    order = ("norm1_gain", "w_in", "gmlp_v_gain", "w_spatial", "b_spatial", "attn_sinks", "rel_bias_table", "w_out",
             "norm2_gain", "w_ff1", "w_ff2", "w_ple_proj", "w_ple_gate", "final_gain")
    return (loss, grad_x, *[grads[n] for n in order], *[deltas[n] for n in order],
            *[new_m[n] for n in order], *[new_v[n] for n in order])
```

```python
import functools
import math

import numpy as np
import jax
import jax.numpy as jnp
from jax import lax
from jax.experimental import pallas as pl
from jax.experimental.pallas import tpu as pltpu

F32 = jnp.float32
BF = jnp.bfloat16
MESH = pl.DeviceIdType.MESH
N_DEV = 8

D_MODEL = 1024
PLE_DIM = 256
D_GMLP = 512
GROUPS = 4
GDIM = 128
BLOCK = 128
D_ATTN = 512
HEAD_DIM = 64
N_Q = 8
Q_PER_KV = 4
N_KV = N_Q // Q_PER_KV
ROWS4 = Q_PER_KV * BLOCK
D_KV = 128
D_FF = 4096
D_IN = 1792
D_MAIN = 2 * D_GMLP + D_ATTN
REL_BUCKETS = 32
EPS = 1e-6
NEG_INF = -1e30
SCALE = HEAD_DIM ** -0.5
GELU_C = math.sqrt(2.0 / math.pi)
GELU_A = 0.044715

ADAM_LR = 0.001
ADAM_B1 = 0.9
ADAM_B2 = 0.999
ADAM_EPS = 1e-08
ADAM_WD = 0.01
ADAM_STEP = 10

V7X_VMEM_LIMIT = 60000 * 1024
TOK_TILE = 256


def _call(body, after=None, **kw):
    if after is None:
        return pl.pallas_call(body, **kw)
    n_in = len(kw["in_specs"])

    def ordered(*refs):
        body(*refs[:n_in], *refs[n_in + 1:])

    kw["in_specs"] = list(kw["in_specs"]) + [pl.BlockSpec(memory_space=pl.ANY)]
    fn = pl.pallas_call(ordered, **kw)
    return lambda *operands: fn(*operands, after)


def _params(sem=None):
    if sem is None:
        return pltpu.CompilerParams(vmem_limit_bytes=V7X_VMEM_LIMIT)
    return pltpu.CompilerParams(dimension_semantics=sem, vmem_limit_bytes=V7X_VMEM_LIMIT)


def _nn(a, b):
    return jnp.dot(a, b, preferred_element_type=F32)


def _nt(a, b):
    return lax.dot_general(a, b, (((1,), (1,)), ((), ())), preferred_element_type=F32)


def _tn(a, b):
    return lax.dot_general(a, b, (((0,), (0,)), ((), ())), preferred_element_type=F32)


def _gelu_tanh(x):
    return jnp.tanh(GELU_C * (x + GELU_A * (x * x * x)))


def _gelu(x):
    return x * (0.5 * (1.0 + _gelu_tanh(x)))


def _gelu_and_grad(x):
    t = _gelu_tanh(x)
    cdf = 0.5 * (1.0 + t)
    return x * cdf, cdf + 0.5 * x * (1.0 - t * t) * (GELU_C * (1.0 + 3.0 * GELU_A * (x * x)))


def _rms_scale(x):
    return lax.rsqrt(jnp.mean(x * x, axis=-1, keepdims=True) + EPS)


def _rms_bwd(dxn, x, r):
    return r * dxn - x * ((r * r * r) * jnp.mean(dxn * x, axis=-1, keepdims=True))


def _bucket_table():
    a = np.arange(BLOCK)[:, None]
    j = np.arange(2 * BLOCK)[None, :]
    n = BLOCK + a - j
    valid = (n >= 0) & (n < BLOCK)
    nc = np.maximum(n, 0)
    max_exact = REL_BUCKETS // 2
    nf = np.maximum(nc, 1).astype(np.float32)
    large = max_exact + (
        np.log(nf / np.float32(max_exact)) / np.float32(math.log(BLOCK / max_exact)) * np.float32(REL_BUCKETS - max_exact)
    ).astype(np.int32)
    large = np.minimum(large, REL_BUCKETS - 1)
    bucket = np.where(nc < max_exact, nc, large)
    return np.where(valid, bucket, -1).astype(np.int32)


def _in_proj(x, g1, w_in_t):
    s = x.shape[0]
    tm = min(TOK_TILE, s)

    def body(x_ref, g_ref, w_ref, zuv_ref, qkv_ref, hn_ref):
        xv = x_ref[...]
        hn = ((xv * _rms_scale(xv)) * g_ref[...]).astype(BF)
        hn_ref[...] = hn
        z = _nt(hn, w_ref[...])
        zuv_ref[...] = z[:, : 2 * D_GMLP]
        qkv_ref[...] = z[:, 2 * D_GMLP:].astype(BF)

    return _call(
        body,
        name="in_proj",
        grid=(s // tm,),
        in_specs=[
            pl.BlockSpec((tm, D_MODEL), lambda i: (i, 0)),
            pl.BlockSpec((1, D_MODEL), lambda i: (0, 0)),
            pl.BlockSpec((D_IN, D_MODEL), lambda i: (0, 0)),
        ],
        out_specs=[
            pl.BlockSpec((tm, 2 * D_GMLP), lambda i: (i, 0)),
            pl.BlockSpec((tm, D_ATTN + 2 * D_KV), lambda i: (i, 0)),
            pl.BlockSpec((tm, D_MODEL), lambda i: (i, 0)),
        ],
        out_shape=[
            jax.ShapeDtypeStruct((s, 2 * D_GMLP), F32),
            jax.ShapeDtypeStruct((s, D_ATTN + 2 * D_KV), BF),
            jax.ShapeDtypeStruct((s, D_MODEL), BF),
        ],
        compiler_params=_params(("arbitrary",)),
    )(x, g1, w_in_t)


def _head_rows(h):
    kh, g = divmod(h, Q_PER_KV)
    return kh, slice(g * BLOCK, (g + 1) * BLOCK)


def _build_bias(bias_ref, bucket_ref, table_ref):
    bucket = bucket_ref[...]
    for h in range(N_Q):
        acc = jnp.zeros((BLOCK, 2 * BLOCK), F32)
        for b in range(REL_BUCKETS):
            acc = jnp.where(bucket == b, table_ref[b, h], acc)
        kh, rows = _head_rows(h)
        bias_ref[kh, rows, :] = acc


def _window_masks(i):
    row = lax.broadcasted_iota(jnp.int32, (ROWS4, BLOCK), 0) & (BLOCK - 1)
    col = lax.broadcasted_iota(jnp.int32, (ROWS4, BLOCK), 1)
    return (col > row) & (i > 0), col <= row


def _stack_heads(ref, kh, offset):
    first = offset + kh * Q_PER_KV * HEAD_DIM
    return jnp.concatenate(
        [ref[:, first + g * HEAD_DIM: first + (g + 1) * HEAD_DIM].astype(BF) for g in range(Q_PER_KV)], axis=0)


def _stack_sinks(sink_ref, kh):
    return jnp.concatenate([jnp.full((BLOCK, 1), sink_ref[0, kh * Q_PER_KV + g], F32) for g in range(Q_PER_KV)], axis=0)


def _tril_bf16(w_ref, g):
    row = lax.broadcasted_iota(jnp.int32, (BLOCK, BLOCK), 0)
    col = lax.broadcasted_iota(jnp.int32, (BLOCK, BLOCK), 1)
    return jnp.where(col <= row, w_ref[g], 0.0).astype(BF)


def _attn_probs(q_h, k_prev, k_cur, bias_h, sink, valid_prev, valid_cur):
    l_prev = jnp.where(valid_prev, _nt(q_h, k_prev) * SCALE + bias_h[:, :BLOCK], NEG_INF)
    l_cur = jnp.where(valid_cur, _nt(q_h, k_cur) * SCALE + bias_h[:, BLOCK:], NEG_INF)
    m = jnp.maximum(jnp.maximum(jnp.max(l_prev, axis=-1, keepdims=True), jnp.max(l_cur, axis=-1, keepdims=True)), sink)
    e_prev = jnp.exp(l_prev - m)
    e_cur = jnp.exp(l_cur - m)
    e_sink = jnp.exp(sink - m)
    denom = jnp.sum(e_prev, axis=-1, keepdims=True) + jnp.sum(e_cur, axis=-1, keepdims=True) + e_sink
    return e_prev / denom, e_cur / denom, e_sink / denom


def _mixer_specs(nb):
    cl = lambda i: jnp.minimum(i, nb - 1)
    return [
        pl.BlockSpec((BLOCK, 2 * D_GMLP), lambda i: (cl(i), 0)),
        pl.BlockSpec((BLOCK, D_ATTN), lambda i: (cl(i), 0)),
        pl.BlockSpec((BLOCK, 2 * D_KV), lambda i: (cl(i), D_ATTN // (2 * D_KV))),
        pl.BlockSpec((BLOCK, 2 * D_KV), lambda i: (jnp.maximum(cl(i) - 1, 0), D_ATTN // (2 * D_KV))),
        pl.BlockSpec((1, D_GMLP), lambda i: (0, 0)),
        pl.BlockSpec((GROUPS, BLOCK, BLOCK), lambda i: (0, 0, 0)),
        pl.BlockSpec((GROUPS, BLOCK, 1), lambda i: (0, 0, 0)),
        pl.BlockSpec(memory_space=pltpu.SMEM),
        pl.BlockSpec(memory_space=pltpu.SMEM),
        pl.BlockSpec((BLOCK, 2 * BLOCK), lambda i: (0, 0)),
    ]


def _mixer_fwd(zuv, qkv, gv, w_sp, b_sp, sinks, table, bucket):
    s = zuv.shape[0]
    nb = s // BLOCK

    def body(zuv_ref, q_ref, kvc_ref, kvp_ref, gv_ref, w_ref, b_ref, sink_ref, table_ref, bucket_ref, mix_ref, bias_ref):
        i = pl.program_id(0)

        @pl.when(i == 0)
        def _():
            _build_bias(bias_ref, bucket_ref, table_ref)

        u = _gelu(zuv_ref[:, :D_GMLP])
        vg = _gelu(zuv_ref[:, D_GMLP:])
        for g in range(GROUPS):
            sl = slice(g * GDIM, (g + 1) * GDIM)
            vg_g = vg[:, sl]
            vn = ((vg_g * _rms_scale(vg_g)) * gv_ref[:, sl]).astype(BF)
            sv = _nn(_tril_bf16(w_ref, g), vn) + b_ref[g]
            mix_ref[:, sl] = (u[:, sl] * sv).astype(BF)

        valid_prev, valid_cur = _window_masks(i)
        for kh in range(N_KV):
            ksl = slice(kh * HEAD_DIM, (kh + 1) * HEAD_DIM)
            vsl = slice(D_KV + kh * HEAD_DIM, D_KV + (kh + 1) * HEAD_DIM)
            q4 = _stack_heads(q_ref, kh, 0)
            p_prev, p_cur, _ = _attn_probs(
                q4, kvp_ref[:, ksl], kvc_ref[:, ksl], bias_ref[kh], _stack_sinks(sink_ref, kh), valid_prev, valid_cur)
            o4 = _nn(p_prev.astype(BF), kvp_ref[:, vsl]) + _nn(p_cur.astype(BF), kvc_ref[:, vsl])
            for g in range(Q_PER_KV):
                first = D_GMLP + (kh * Q_PER_KV + g) * HEAD_DIM
                mix_ref[:, first:first + HEAD_DIM] = o4[g * BLOCK:(g + 1) * BLOCK].astype(BF)

    return _call(
        body,
        name="mixer_fwd",
        grid=(nb,),
        in_specs=_mixer_specs(nb),
        out_specs=pl.BlockSpec((BLOCK, D_MODEL), lambda i: (i, 0)),
        out_shape=jax.ShapeDtypeStruct((s, D_MODEL), BF),
        scratch_shapes=[pltpu.VMEM((N_KV, ROWS4, 2 * BLOCK), F32)],
        compiler_params=_params(("arbitrary",)),
    )(zuv, qkv, qkv, qkv, gv, w_sp, b_sp, sinks, table, bucket)


def _tail(x, mix, p, t, g2, gf, w_out, w_ff1, w_ff2, w_gate, w_proj):
    s = x.shape[0]
    tm = min(TOK_TILE, s)
    n_ff = w_ff1.shape[0]
    fc = D_FF // n_ff

    def body(x_ref, mix_ref, p_ref, t_ref, g2_ref, gf_ref, wo_ref, w1_ref, w2_ref, wg_ref, wp_ref,
             small_ref, dh1_ref, dh1b_ref, dmix_ref, hn2_ref, a_ref, df_ref, h2_ref, dh2_ref, dgl_ref, dpp_ref, f_ref):
        i = pl.program_id(0)

        @pl.when(i == 0)
        def _():
            small_ref[...] = jnp.zeros_like(small_ref)

        h1 = x_ref[...] + _nn(mix_ref[...], wo_ref[...])
        r2 = _rms_scale(h1)
        hn2 = ((h1 * r2) * g2_ref[...]).astype(BF)
        hn2_ref[...] = hn2
        h2 = h1
        for c in range(n_ff):
            f = _nn(hn2, w1_ref[c])
            f_ref[:, c * fc:(c + 1) * fc] = f
            a = jnp.square(jnp.maximum(f, 0.0)).astype(BF)
            a_ref[:, c * fc:(c + 1) * fc] = a
            h2 = h2 + _nn(a, w2_ref[c * fc:(c + 1) * fc, :])
        h2b = h2.astype(BF)
        h2_ref[...] = h2b
        gate = jax.nn.sigmoid(_nn(h2b, wg_ref[...]))
        pp = _nn(p_ref[...].astype(BF), wp_ref[...])
        h3 = h2 + gate * pp
        rf = _rms_scale(h3)
        gf_v = gf_ref[...]
        err = (h3 * rf) * gf_v - t_ref[...]
        small_ref[2:3, :] += jnp.sum(jnp.sum(err * err, axis=-1, keepdims=True), axis=0, keepdims=True) * (0.5 / D_MODEL)

        dy = err * (1.0 / D_MODEL)
        small_ref[1:2, :] += jnp.sum(dy * (h3 * rf), axis=0, keepdims=True)
        dh3 = _rms_bwd(dy * gf_v, h3, rf)
        dpp_ref[...] = (dh3 * gate).astype(BF)
        dgl = ((dh3 * pp) * (gate * (1.0 - gate))).astype(BF)
        dgl_ref[...] = dgl
        dh2 = dh3 + _nt(dgl, wg_ref[...])
        dh2b = dh2.astype(BF)
        dh2_ref[...] = dh2b
        dhn2 = jnp.zeros((tm, D_MODEL), F32)
        for c in range(n_ff):
            da = _nt(dh2b, w2_ref[c * fc:(c + 1) * fc, :])
            df = (da * (2.0 * jnp.maximum(f_ref[:, c * fc:(c + 1) * fc], 0.0))).astype(BF)
            df_ref[:, c * fc:(c + 1) * fc] = df
            dhn2 = dhn2 + _nt(df, w1_ref[c])
        small_ref[0:1, :] += jnp.sum(dhn2 * (h1 * r2), axis=0, keepdims=True)
        dh1 = dh2 + _rms_bwd(dhn2 * g2_ref[...], h1, r2)
        dh1_ref[...] = dh1
        dh1b = dh1.astype(BF)
        dh1b_ref[...] = dh1b
        dmix_ref[...] = _nt(dh1b, wo_ref[...])

    tile = lambda cols: pl.BlockSpec((tm, cols), lambda i: (i, 0))
    whole = lambda shape: pl.BlockSpec(shape, lambda i: (0,) * len(shape), pipeline_mode=pl.Buffered(1))
    row = pl.BlockSpec((1, D_MODEL), lambda i: (0, 0))
    act = lambda cols, dt: jax.ShapeDtypeStruct((s, cols), dt)
    return _call(
        body,
        name="tail",
        grid=(s // tm,),
        in_specs=[tile(D_MODEL), tile(D_MODEL), tile(PLE_DIM), tile(D_MODEL), row, row,
                  whole(w_out.shape), whole(w_ff1.shape), whole(w_ff2.shape), whole(w_gate.shape), whole(w_proj.shape)],
        out_specs=[pl.BlockSpec((8, D_MODEL), lambda i: (0, 0)), tile(D_MODEL), tile(D_MODEL), tile(D_MODEL), tile(D_MODEL), tile(D_FF),
                   tile(D_FF), tile(D_MODEL), tile(D_MODEL), tile(D_MODEL), tile(D_MODEL)],
        out_shape=[jax.ShapeDtypeStruct((8, D_MODEL), F32),
                   act(D_MODEL, F32), act(D_MODEL, BF), act(D_MODEL, F32), act(D_MODEL, BF), act(D_FF, BF), act(D_FF, BF), act(D_MODEL, BF),
                   act(D_MODEL, BF), act(D_MODEL, BF), act(D_MODEL, BF)],
        scratch_shapes=[pltpu.VMEM((tm, D_FF), F32)],
        compiler_params=_params(("arbitrary",)),
    )(x, mix, p, t, g2, gf, w_out, w_ff1, w_ff2, w_gate, w_proj)


def _mixer_bwd(zuv, qkv, dmix, gv, w_sp, b_sp, sinks, table, bucket, after=None):
    s = zuv.shape[0]
    nb = s // BLOCK

    def body(zuv_ref, q_ref, kvc_ref, kvp_ref, gv_ref, w_ref, b_ref, sink_ref, table_ref, bucket_ref, dmix_ref,
             dzm_ref, dkv_ref, dgv_ref, dw_ref, db_ref, dattn_ref,
             bias_ref, dbias_ref, carry_ref, dsink_acc, db_acc):
        i = pl.program_id(0)

        @pl.when(i == 0)
        def _():
            _build_bias(bias_ref, bucket_ref, table_ref)
            dbias_ref[...] = jnp.zeros_like(dbias_ref)
            carry_ref[...] = jnp.zeros_like(carry_ref)
            dsink_acc[...] = jnp.zeros_like(dsink_acc)
            dgv_ref[...] = jnp.zeros_like(dgv_ref)
            dw_ref[...] = jnp.zeros_like(dw_ref)
            db_acc[...] = jnp.zeros_like(db_acc)

        @pl.when(i < nb)
        def _():
            u, du_dz = _gelu_and_grad(zuv_ref[:, :D_GMLP])
            vg, dvg_dz = _gelu_and_grad(zuv_ref[:, D_GMLP:])
            for g in range(GROUPS):
                sl = slice(g * GDIM, (g + 1) * GDIM)
                vg_g = vg[:, sl]
                rg = _rms_scale(vg_g)
                vhat = vg_g * rg
                gain = gv_ref[:, sl]
                vn = (vhat * gain).astype(BF)
                w_g = _tril_bf16(w_ref, g)
                sv = _nn(w_g, vn) + b_ref[g]
                dmix_a = dmix_ref[:, sl]
                dsv = dmix_a * u[:, sl]
                dsvb = dsv.astype(BF)
                db_acc[g] += jnp.sum(dsv, axis=-1, keepdims=True)
                dw_ref[g] += _nt(dsvb, vn)
                dvn = _tn(w_g, dsvb)
                dgv_ref[:, sl] += jnp.sum(dvn * vhat, axis=0, keepdims=True)
                dvg = _rms_bwd(dvn * gain, vg_g, rg)
                dzm_ref[:, sl] = ((dmix_a * sv) * du_dz[:, sl]).astype(BF)
                dzm_ref[:, D_GMLP + g * GDIM: D_GMLP + (g + 1) * GDIM] = (dvg * dvg_dz[:, sl]).astype(BF)

            valid_prev, valid_cur = _window_masks(i)
            for kh in range(N_KV):
                ksl = slice(kh * HEAD_DIM, (kh + 1) * HEAD_DIM)
                vsl = slice(D_KV + kh * HEAD_DIM, D_KV + (kh + 1) * HEAD_DIM)
                k_prev, k_cur = kvp_ref[:, ksl], kvc_ref[:, ksl]
                v_prev, v_cur = kvp_ref[:, vsl], kvc_ref[:, vsl]
                q4 = _stack_heads(q_ref, kh, 0)
                p_prev, p_cur, p_sink = _attn_probs(
                    q4, k_prev, k_cur, bias_ref[kh], _stack_sinks(sink_ref, kh), valid_prev, valid_cur)
                do4 = _stack_heads(dmix_ref, kh, D_GMLP)
                dp_prev = _nt(do4, v_prev)
                dp_cur = _nt(do4, v_cur)
                delta = jnp.sum(p_prev * dp_prev, axis=-1, keepdims=True) + jnp.sum(p_cur * dp_cur, axis=-1, keepdims=True)
                ds_prev = p_prev * (dp_prev - delta)
                ds_cur = p_cur * (dp_cur - delta)
                dsink_acc[kh] -= p_sink * delta
                dbias_ref[kh, :, :BLOCK] += ds_prev
                dbias_ref[kh, :, BLOCK:] += ds_cur
                dsb_prev = ds_prev.astype(BF)
                dsb_cur = ds_cur.astype(BF)
                dq4 = (_nn(dsb_prev, k_prev) + _nn(dsb_cur, k_cur)) * SCALE
                for g in range(Q_PER_KV):
                    first = 2 * D_GMLP + (kh * Q_PER_KV + g) * HEAD_DIM
                    dzm_ref[:, first:first + HEAD_DIM] = dq4[g * BLOCK:(g + 1) * BLOCK].astype(BF)
                dkv_ref[:, ksl] = (carry_ref[:, ksl] + _tn(dsb_prev, q4) * SCALE).astype(BF)
                dkv_ref[:, vsl] = (carry_ref[:, vsl] + _tn(p_prev.astype(BF), do4)).astype(BF)
                carry_ref[:, ksl] = _tn(dsb_cur, q4) * SCALE
                carry_ref[:, vsl] = _tn(p_cur.astype(BF), do4)

        @pl.when(i == nb)
        def _():
            dkv_ref[...] = carry_ref[...].astype(BF)
            row = lax.broadcasted_iota(jnp.int32, (BLOCK, BLOCK), 0)
            col = lax.broadcasted_iota(jnp.int32, (BLOCK, BLOCK), 1)
            for g in range(GROUPS):
                dw_ref[g] = jnp.where(col <= row, dw_ref[g], 0.0)
                db_ref[g:g + 1, :] = jnp.sum(jnp.where(col == row, db_acc[g], 0.0), axis=0, keepdims=True)
            bucket = bucket_ref[...]
            for h in range(N_Q):
                kh, rows = _head_rows(h)
                dattn_ref[REL_BUCKETS, h] = jnp.sum(dsink_acc[kh, rows, :])
                dbh = dbias_ref[kh, rows, :]
                for b in range(REL_BUCKETS):
                    dattn_ref[b, h] = jnp.sum(jnp.where(bucket == b, dbh, 0.0))

    cl = lambda i: jnp.minimum(i, nb - 1)
    const = lambda shape: pl.BlockSpec(shape, lambda i: (0,) * len(shape))
    return _call(
        body,
        name="mixer_bwd",
        after=after,
        grid=(nb + 1,),
        in_specs=_mixer_specs(nb) + [pl.BlockSpec((BLOCK, D_MODEL), lambda i: (cl(i), 0))],
        out_specs=[
            pl.BlockSpec((BLOCK, D_MAIN), lambda i: (cl(i), 0)),
            pl.BlockSpec((BLOCK, 2 * D_KV), lambda i: (jnp.maximum(i - 1, 0), 0)),
            const((1, D_GMLP)),
            const((GROUPS, BLOCK, BLOCK)),
            const((GROUPS, BLOCK)),
            pl.BlockSpec(memory_space=pltpu.SMEM),
        ],
        out_shape=[
            jax.ShapeDtypeStruct((s, D_MAIN), BF),
            jax.ShapeDtypeStruct((s, 2 * D_KV), BF),
            jax.ShapeDtypeStruct((1, D_GMLP), F32),
            jax.ShapeDtypeStruct((GROUPS, BLOCK, BLOCK), F32),
            jax.ShapeDtypeStruct((GROUPS, BLOCK), F32),
            jax.ShapeDtypeStruct((REL_BUCKETS + 1, N_Q), F32),
        ],
        scratch_shapes=[
            pltpu.VMEM((N_KV, ROWS4, 2 * BLOCK), F32),
            pltpu.VMEM((N_KV, ROWS4, 2 * BLOCK), F32),
            pltpu.VMEM((BLOCK, 2 * D_KV), F32),
            pltpu.VMEM((N_KV, ROWS4, 1), F32),
            pltpu.VMEM((GROUPS, BLOCK, 1), F32),
        ],
        compiler_params=_params(("arbitrary",)),
    )(zuv, qkv, qkv, qkv, gv, w_sp, b_sp, sinks, table, bucket, dmix)


def _in_bwd(x, dh1, dzm, dkv, g1, w_in_t, after=None):
    s = x.shape[0]
    tm = min(TOK_TILE, s)

    def body(x_ref, dh1_ref, dzm_ref, dkv_ref, g_ref, w_ref, dx_ref, dg_ref):
        @pl.when(pl.program_id(0) == 0)
        def _():
            dg_ref[...] = jnp.zeros_like(dg_ref)

        dhn = _nn(dzm_ref[...], w_ref[:D_MAIN, :]) + _nn(dkv_ref[...], w_ref[D_MAIN:, :])
        xv = x_ref[...]
        r = _rms_scale(xv)
        dg_ref[...] += jnp.sum(dhn * (xv * r), axis=0, keepdims=True)
        dx_ref[...] = dh1_ref[...] + _rms_bwd(dhn * g_ref[...], xv, r)

    tile = lambda cols: pl.BlockSpec((tm, cols), lambda i: (i, 0))
    row = pl.BlockSpec((1, D_MODEL), lambda i: (0, 0))
    return _call(
        body,
        name="in_bwd",
        after=after,
        grid=(s // tm,),
        in_specs=[tile(D_MODEL), tile(D_MODEL), tile(D_MAIN), tile(2 * D_KV), row, pl.BlockSpec((D_IN, D_MODEL), lambda i: (0, 0))],
        out_specs=[tile(D_MODEL), row],
        out_shape=[jax.ShapeDtypeStruct((s, D_MODEL), F32), jax.ShapeDtypeStruct((1, D_MODEL), F32)],
        compiler_params=_params(("arbitrary",)),
    )(x, dh1, dzm, dkv, g1, w_in_t)


def _wgrad_in(dzm, dkv, hn):
    s = hn.shape[0]
    tm = 2 * D_KV
    n_main = D_MAIN // tm

    def body(dzm_ref, dkv_ref, hn_ref, o_ref):
        i = pl.program_id(0)

        @pl.when(i < n_main)
        def _():
            o_ref[...] = _tn(dzm_ref[...], hn_ref[...])

        @pl.when(i == n_main)
        def _():
            o_ref[...] = _tn(dkv_ref[...], hn_ref[...])

    return _call(
        body,
        name="wgrad_in",
        grid=(n_main + 1,),
        in_specs=[
            pl.BlockSpec((s, tm), lambda i: (0, jnp.minimum(i, n_main - 1))),
            pl.BlockSpec((s, tm), lambda i: (0, 0)),
            pl.BlockSpec((s, D_MODEL), lambda i: (0, 0)),
        ],
        out_specs=pl.BlockSpec((tm, D_MODEL), lambda i: (i, 0)),
        out_shape=jax.ShapeDtypeStruct((D_IN, D_MODEL), F32),
        compiler_params=_params(("arbitrary",)),
    )(dzm, dkv, hn)


def _wgrad(a, b, tm, tn, name, peer_cols=False, after=None):
    s, m = a.shape
    n = b.shape[1]

    def body(a_ref, b_ref, o_ref, at_ref):
        @pl.when(pl.program_id(1) == 0)
        def _():
            at_ref[...] = a_ref[...].astype(BF).T

        r = _nn(at_ref[...], b_ref[...])
        if peer_cols:
            o_ref[0] = r
        else:
            o_ref[...] = r

    if peer_cols:
        out_spec = pl.BlockSpec((1, tm, tn), lambda i, j: (j, i, 0))
        out_shape = jax.ShapeDtypeStruct((n // tn, m, tn), F32)
    else:
        out_spec = pl.BlockSpec((tm, tn), lambda i, j: (i, j))
        out_shape = jax.ShapeDtypeStruct((m, n), F32)
    return _call(
        body,
        name=name,
        after=after,
        grid=(m // tm, n // tn),
        in_specs=[pl.BlockSpec((s, tm), lambda i, j: (0, i)), pl.BlockSpec((s, tn), lambda i, j: (0, j))],
        out_specs=out_spec,
        out_shape=out_shape,
        scratch_shapes=[pltpu.VMEM((tm, s), BF)],
        compiler_params=_params(("arbitrary", "arbitrary")),
    )(a, b)


def _adamw_math(w, g, m, v):
    m_new = ADAM_B1 * m + (1.0 - ADAM_B1) * g
    v_new = ADAM_B2 * v + (1.0 - ADAM_B2) * jnp.square(g)
    m_hat = m_new / (1.0 - ADAM_B1 ** ADAM_STEP)
    v_hat = v_new / (1.0 - ADAM_B2 ** ADAM_STEP)
    delta = -ADAM_LR * (m_hat / (jnp.sqrt(v_hat) + ADAM_EPS) + ADAM_WD * w)
    return delta, m_new, v_new


def _final_adamw(part, recv, w, m, v, name):
    r, c = w.shape
    tr = min(r, 512)

    def body(p_ref, r_ref, w_ref, m_ref, v_ref, g_ref, d_ref, mo_ref, vo_ref):
        g = p_ref[...]
        for j in range(3):
            g = g + r_ref[j].astype(F32)
        g_ref[...] = g
        d_ref[...], mo_ref[...], vo_ref[...] = _adamw_math(w_ref[...], g, m_ref[...], v_ref[...])

    spec = pl.BlockSpec((tr, c), lambda i: (i, 0))
    return _call(
        body,
        name=name,
        grid=(r // tr,),
        in_specs=[spec, pl.BlockSpec((3, tr, c), lambda i: (0, i, 0)), spec, spec, spec],
        out_specs=[spec] * 4,
        out_shape=[jax.ShapeDtypeStruct((r, c), F32)] * 4,
        compiler_params=_params(("arbitrary",)),
    )(part, recv, w, m, v)


def _rs_sum(g, land, blocks, name):
    _, r, c = g.shape
    tr = min(r, 256)

    def body(blk_ref, g0_ref, g1_ref, g2_ref, g3_ref, l_ref, part_ref, send_ref):
        part_ref[...] = g0_ref[0] + l_ref[0]
        for j, gj_ref in enumerate((g1_ref, g2_ref, g3_ref)):
            send_ref[j] = (gj_ref[0] + l_ref[j + 1]).astype(BF)

    def pick(j):
        return pl.BlockSpec((1, tr, c), lambda i, blk: (blk[j], i, 0))

    return _call(
        body,
        name=name,
        grid_spec=pltpu.PrefetchScalarGridSpec(
            num_scalar_prefetch=1,
            grid=(r // tr,),
            in_specs=[pick(0), pick(1), pick(2), pick(3), pl.BlockSpec((4, tr, c), lambda i, blk: (0, i, 0))],
            out_specs=[pl.BlockSpec((tr, c), lambda i, blk: (i, 0)), pl.BlockSpec((3, tr, c), lambda i, blk: (0, i, 0))],
        ),
        out_shape=[jax.ShapeDtypeStruct((r, c), F32), jax.ShapeDtypeStruct((3, r, c), BF)],
        compiler_params=_params(("arbitrary",)),
    )(blocks, g, g, g, g, land)


def _adamw_small(gathered, weights, moms, vels, after):
    n_w = len(weights)

    def body(*refs):
        tail_ref, g1_ref, gv_ref, wsp_ref, bsp_ref, attn_ref = refs[:6]
        w_refs, m_refs, v_refs = refs[6:6 + n_w], refs[6 + n_w:6 + 2 * n_w], refs[6 + 2 * n_w:6 + 3 * n_w]
        outs = refs[6 + 3 * n_w:]
        loss_ref, g_refs, d_refs = outs[0], outs[1:1 + n_w], outs[1 + n_w:1 + 2 * n_w]
        mo_refs, vo_refs = outs[1 + 2 * n_w:1 + 3 * n_w], outs[1 + 3 * n_w:]

        def total(part):
            acc = part(0)
            for d in range(1, N_DEV):
                acc = acc + part(d)
            return acc

        grads = [
            total(lambda d: g1_ref[d]), total(lambda d: gv_ref[d]), total(lambda d: wsp_ref[d]), total(lambda d: bsp_ref[d]),
            total(lambda d: attn_ref[d, REL_BUCKETS:, :]), total(lambda d: attn_ref[d, :REL_BUCKETS, :]),
            total(lambda d: tail_ref[d, 0:1, :]), total(lambda d: tail_ref[d, 1:2, :])]
        loss_ref[...] = total(lambda d: tail_ref[d, 2:3, 0:1])
        for k in range(n_w):
            g_refs[k][...] = grads[k]
            d_refs[k][...], mo_refs[k][...], vo_refs[k][...] = _adamw_math(w_refs[k][...], grads[k], m_refs[k][...], v_refs[k][...])

    shapes = [jax.ShapeDtypeStruct(w.shape, F32) for w in weights]
    outs = _call(
        body,
        name="adamw_small",
        after=after,
        in_specs=[pl.BlockSpec(memory_space=pltpu.VMEM)] * (6 + 3 * n_w),
        out_shape=[jax.ShapeDtypeStruct((1, 1), F32)] + shapes * 4,
        compiler_params=_params(),
    )(*gathered, *weights, *moms, *vels)
    return outs[0], outs[1:1 + n_w], outs[1 + n_w:1 + 2 * n_w], outs[1 + 2 * n_w:1 + 3 * n_w], outs[1 + 3 * n_w:]


def _place():
    x, y, c = lax.axis_index("x"), lax.axis_index("y"), lax.axis_index("c")
    return x, y, c, [(1 - x, y), (x, 1 - y), (1 - x, 1 - y)]


def _dev_index(px, py, pc):
    return 4 * px + 2 * py + pc


def _all_gather(shards, out_dtype, name, after=None):
    k_n = len(shards)

    def body(*refs):
        ins, outs, stage = refs[:k_n], refs[k_n:2 * k_n], refs[2 * k_n:3 * k_n]
        send_sems, recv_sems, local_sems = refs[3 * k_n:]
        x, y, c, chips = _place()
        me, sibling = (x, y, c), (x, y, 1 - c)

        def copy(k, j, block, to, src=None):
            dst = outs[k].at[_dev_index(*block)]
            return pltpu.make_async_remote_copy(
                src_ref=dst if src is None else src, dst_ref=dst,
                send_sem=send_sems.at[k, j], recv_sem=recv_sems.at[k, j], device_id=to, device_id_type=MESH)

        mine, first = [], []
        for k in range(k_n):
            stage[k][...] = ins[k][...].astype(out_dtype)
            mine.append(pltpu.make_async_copy(stage[k], outs[k].at[_dev_index(*me)], local_sems.at[k]))
            mine[k].start()
            first.append(copy(k, 0, me, sibling, src=stage[k]))
            first += [copy(k, 1 + j, me, (*chip, c), src=stage[k]) for j, chip in enumerate(chips)]
        for cp in first:
            cp.start()
        passed = []
        for k in range(k_n):
            for j, chip in enumerate(chips):
                copy(k, 1 + j, (*chip, c), me).wait_recv()
                passed.append(copy(k, 4 + j, (*chip, c), sibling))
                passed[-1].start()
        for k in range(k_n):
            copy(k, 0, sibling, me).wait_recv()
            for j, chip in enumerate(chips):
                copy(k, 4 + j, (*chip, 1 - c), me).wait_recv()
        for cp in first + passed:
            cp.wait_send()
        for cp in mine:
            cp.wait()

    return _call(
        body,
        name=name,
        after=after,
        in_specs=[pl.BlockSpec(memory_space=pltpu.VMEM)] * k_n,
        out_specs=[pl.BlockSpec(memory_space=pl.ANY)] * k_n,
        out_shape=[jax.ShapeDtypeStruct((N_DEV,) + sh.shape, out_dtype) for sh in shards],
        scratch_shapes=[pltpu.VMEM(sh.shape, out_dtype) for sh in shards]
        + [pltpu.SemaphoreType.DMA((k_n, 7)), pltpu.SemaphoreType.DMA((k_n, 7)), pltpu.SemaphoreType.DMA((k_n,))],
        compiler_params=_params(),
    )(*shards)


HBM_SPEC = pl.BlockSpec(memory_space=pltpu.HBM)
SEM_SPEC = pl.BlockSpec(memory_space=pltpu.SEMAPHORE)
ANY_SPEC = pl.BlockSpec(memory_space=pl.ANY)
DATAFLOW = pltpu.SideEffectType.DATAFLOW_SIDE_EFFECTING


def _hbm(a):
    return pltpu.with_memory_space_constraint(a, pltpu.HBM)


def _prep_weights(shards):
    k_n = len(shards)

    def body(*refs):
        ins, outs, stage, sems = refs[:k_n], refs[k_n:2 * k_n], refs[2 * k_n:3 * k_n], refs[3 * k_n]
        x, y, c, _ = _place()
        copies = []
        for k in range(k_n):
            stage[k][...] = ins[k][...].astype(BF)
            copies.append(pltpu.make_async_copy(stage[k], outs[k].at[_dev_index(x, y, c)], sems.at[k]))
            copies[k].start()
        for cp in copies:
            cp.wait()

    return _call(
        body,
        name="prep_weights",
        in_specs=[pl.BlockSpec(memory_space=pltpu.VMEM)] * k_n,
        out_specs=[ANY_SPEC] * k_n,
        out_shape=[jax.ShapeDtypeStruct((N_DEV,) + sh.shape, BF) for sh in shards],
        scratch_shapes=[pltpu.VMEM(sh.shape, BF) for sh in shards] + [pltpu.SemaphoreType.DMA((k_n,))],
        compiler_params=_params(),
    )(*shards)


def _ag_start(first, rest):
    lands = list(first) + list(rest)
    k_n, k_first = len(lands), len(first)

    def body(*refs):
        land = refs[:k_n]
        sems = refs[k_n:k_n + 4]
        token = refs[-1]
        x, y, c, chips = _place()
        targets = [(x, y, 1 - c)] + [(*chip, c) for chip in chips]
        for k in range(k_n):
            blk = land[k].at[_dev_index(x, y, c)]
            send_sems, recv_sems, base = (sems[0], sems[1], k) if k < k_first else (sems[2], sems[3], k - k_first)
            for j, to in enumerate(targets):
                pltpu.make_async_remote_copy(
                    src_ref=blk, dst_ref=blk, send_sem=send_sems.at[4 * base + j], recv_sem=recv_sems.at[4 * base + j],
                    device_id=to, device_id_type=MESH).start()
        token[...] = jnp.zeros_like(token)

    k_rest = k_n - k_first
    outs = pl.pallas_call(
        body,
        name="ag_start",
        in_specs=[HBM_SPEC] * k_n,
        out_specs=(SEM_SPEC, SEM_SPEC, SEM_SPEC, SEM_SPEC, *[HBM_SPEC] * k_n, pl.BlockSpec(memory_space=pltpu.VMEM)),
        out_shape=(pltpu.SemaphoreType.DMA((4 * k_first,)), pltpu.SemaphoreType.DMA((4 * k_first,)),
                   pltpu.SemaphoreType.DMA((4 * k_rest,)), pltpu.SemaphoreType.DMA((4 * k_rest,)),
                   *[pltpu.HBM(a.shape, a.dtype) for a in lands], jax.ShapeDtypeStruct((8, LANES), F32)),
        input_output_aliases={k: 4 + k for k in range(k_n)},
        compiler_params=pltpu.CompilerParams(has_side_effects=DATAFLOW),
    )(*[_hbm(a) for a in lands])
    flying = list(outs[4:4 + k_n])
    return (outs[0], outs[1], flying[:k_first]), (outs[2], outs[3], flying[k_first:]), outs[-1]


def _ag_mid(lands, send_sems, recv_sems, after, name):
    k_n = len(lands)

    def body(*refs):
        land = refs[:k_n]
        send1, recv1 = refs[k_n], refs[k_n + 1]
        fwd_send, fwd_recv = refs[-2], refs[-1]
        x, y, c, chips = _place()
        sources = [(x, y, 1 - c)] + [(*chip, c) for chip in chips]
        for k in range(k_n):
            mine = land[k].at[_dev_index(x, y, c)]
            for j, frm in enumerate(sources):
                got = land[k].at[_dev_index(*frm)]
                cp = pltpu.make_async_remote_copy(
                    src_ref=mine, dst_ref=got, send_sem=send1.at[4 * k + j], recv_sem=recv1.at[4 * k + j], device_id=frm, device_id_type=MESH)
                cp.wait_send()
                cp.wait_recv()
                if j >= 1:
                    pltpu.make_async_remote_copy(
                        src_ref=got, dst_ref=got, send_sem=fwd_send.at[3 * k + j - 1], recv_sem=fwd_recv.at[3 * k + j - 1],
                        device_id=(x, y, 1 - c), device_id_type=MESH).start()

    outs = pl.pallas_call(
        body,
        name=name,
        in_specs=[HBM_SPEC] * k_n + [SEM_SPEC, SEM_SPEC, ANY_SPEC],
        out_specs=(*[HBM_SPEC] * k_n, SEM_SPEC, SEM_SPEC),
        out_shape=(*[pltpu.HBM(a.shape, a.dtype) for a in lands], pltpu.SemaphoreType.DMA((3 * k_n,)), pltpu.SemaphoreType.DMA((3 * k_n,))),
        input_output_aliases={k: k for k in range(k_n)},
        compiler_params=pltpu.CompilerParams(has_side_effects=DATAFLOW),
    )(*lands, send_sems, recv_sems, after)
    return list(outs[:k_n]), outs[-2], outs[-1]


def _ag_end(lands, fwd_send, fwd_recv, name):
    k_n = len(lands)

    def body(*refs):
        land = refs[:k_n]
        fsend, frecv = refs[k_n], refs[k_n + 1]
        x, y, c, chips = _place()
        for k in range(k_n):
            for j, chip in enumerate(chips):
                cp = pltpu.make_async_remote_copy(
                    src_ref=land[k].at[_dev_index(*chip, c)], dst_ref=land[k].at[_dev_index(*chip, 1 - c)],
                    send_sem=fsend.at[3 * k + j], recv_sem=frecv.at[3 * k + j], device_id=(x, y, 1 - c), device_id_type=MESH)
                cp.wait_send()
                cp.wait_recv()

    outs = pl.pallas_call(
        body,
        name=name,
        in_specs=[HBM_SPEC] * k_n + [SEM_SPEC, SEM_SPEC],
        out_specs=tuple([HBM_SPEC] * k_n),
        out_shape=tuple(pltpu.HBM(a.shape, a.dtype) for a in lands),
        input_output_aliases={k: k for k in range(k_n)},
        compiler_params=pltpu.CompilerParams(has_side_effects=DATAFLOW),
    )(*lands, fwd_send, fwd_recv)
    return list(outs)


def _chips4():
    x, y, c, others = _place()
    return x, y, c, [(x, y)] + others


def _route_sibling(j):
    x, y, c, chips = _chips4()
    return _dev_index(*chips[j], 1 - c), j, (x, y, 1 - c)


def _route_chips(j):
    x, y, c, chips = _chips4()
    return j, j, (*chips[j + 1], c)


def _xchg_copy(route, n, k, j, src, dst, send_sems, recv_sems):
    si, di, peer = route(j)
    return pltpu.make_async_remote_copy(
        src_ref=src[k].at[si], dst_ref=dst[k].at[di], send_sem=send_sems.at[n * k + j], recv_sem=recv_sems.at[n * k + j],
        device_id=peer, device_id_type=MESH)


def _xchg_start(srcs, slot_shapes, route, n, name):
    k_n = len(srcs)
    dsts = [lax.empty((n,) + tuple(sh), a.dtype) for sh, a in zip(slot_shapes, srcs)]

    def body(*refs):
        src, dst = refs[:k_n], refs[k_n:2 * k_n]
        send_sems, recv_sems, token = refs[2 * k_n], refs[2 * k_n + 1], refs[-1]
        for k in range(k_n):
            for j in range(n):
                _xchg_copy(route, n, k, j, src, dst, send_sems, recv_sems).start()
        token[...] = jnp.zeros_like(token)

    arrays = list(srcs) + dsts
    outs = pl.pallas_call(
        body,
        name=name,
        in_specs=[HBM_SPEC] * (2 * k_n),
        out_specs=(SEM_SPEC, SEM_SPEC, *[HBM_SPEC] * (2 * k_n), pl.BlockSpec(memory_space=pltpu.VMEM)),
        out_shape=(pltpu.SemaphoreType.DMA((n * k_n,)), pltpu.SemaphoreType.DMA((n * k_n,)),
                   *[pltpu.HBM(a.shape, a.dtype) for a in arrays], jax.ShapeDtypeStruct((8, LANES), F32)),
        input_output_aliases={i: 2 + i for i in range(2 * k_n)},
        compiler_params=pltpu.CompilerParams(has_side_effects=DATAFLOW),
    )(*[_hbm(a) for a in arrays])
    return outs[0], outs[1], list(outs[2:2 + k_n]), list(outs[2 + k_n:2 + 2 * k_n]), outs[-1]


def _xchg_wait(send_sems, recv_sems, srcs, dsts, route, n, after, name):
    k_n = len(srcs)

    def body(*refs):
        src, dst = refs[:k_n], refs[k_n:2 * k_n]
        send, recv = refs[2 * k_n], refs[2 * k_n + 1]
        for k in range(k_n):
            for j in range(n):
                cp = _xchg_copy(route, n, k, j, src, dst, send, recv)
                cp.wait_send()
                cp.wait_recv()

    arrays = list(srcs) + list(dsts)
    outs = pl.pallas_call(
        body,
        name=name,
        in_specs=[HBM_SPEC] * (2 * k_n) + [SEM_SPEC, SEM_SPEC, ANY_SPEC],
        out_specs=tuple([HBM_SPEC] * (2 * k_n)),
        out_shape=tuple(pltpu.HBM(a.shape, a.dtype) for a in arrays),
        input_output_aliases={i: i for i in range(2 * k_n)},
        compiler_params=pltpu.CompilerParams(has_side_effects=DATAFLOW),
    )(*arrays, send_sems, recv_sems, after)
    return list(outs[:k_n]), list(outs[k_n:])


SMALL = ("norm1_gain", "gmlp_v_gain", "w_spatial", "b_spatial", "attn_sinks", "rel_bias_table", "norm2_gain", "final_gain")
LANES = 128


def _swap_start(grads, tag):
    return _xchg_start(grads, [g.shape[1:] for g in grads], _route_sibling, 4, f"rs_{tag}_swap_start")


def _swap_sums(swap, names, after, tag):
    send1, recv1, src1, land1, _ = swap
    x, y, c, chips = _chips4()
    blocks = jnp.stack([_dev_index(*chip, c) for chip in chips]).astype(jnp.int32)
    src1, land1 = _xchg_wait(send1, recv1, src1, land1, _route_sibling, 4, after, f"rs_{tag}_swap_wait")
    return [_rs_sum(g, land, blocks, f"rs_sum_{n}") for g, land, n in zip(src1, land1, names)]


def _chips_start(sums, tag):
    sends = [ps[1] for ps in sums]
    return _xchg_start(sends, [a.shape[1:] for a in sends], _route_chips, 3, f"rs_{tag}_chips_start")


def _chips_wait(chips, after, tag):
    send2, recv2, src2, land2, _ = chips
    return _xchg_wait(send2, recv2, src2, land2, _route_chips, 3, after, f"rs_{tag}_chips_wait")[1]


def kernel(x, p, norm1_gain, w_in, gmlp_v_gain, w_spatial, b_spatial, attn_sinks, rel_bias_table, w_out, norm2_gain, w_ff1, w_ff2, w_ple_proj, w_ple_gate, final_gain, loss_target, m_norm1_gain, m_w_in, m_gmlp_v_gain, m_w_spatial, m_b_spatial, m_attn_sinks, m_rel_bias_table, m_w_out, m_norm2_gain, m_w_ff1, m_w_ff2, m_w_ple_proj, m_w_ple_gate, m_final_gain, v_norm1_gain, v_w_in, v_gmlp_v_gain, v_w_spatial, v_b_spatial, v_attn_sinks, v_rel_bias_table, v_w_out, v_norm2_gain, v_w_ff1, v_w_ff2, v_w_ple_proj, v_w_ple_gate, v_final_gain):
    args = dict(locals())
    s = x.shape[1]
    big = ("w_in", "w_out", "w_ff1", "w_ff2", "w_ple_proj", "w_ple_gate")

    x2, p2, t2 = x.reshape(s, D_MODEL), p.reshape(s, PLE_DIM), loss_target.reshape(s, D_MODEL)
    g1, gv, w_sp, b_sp, sinks, table, g2, gf = (args[n] for n in SMALL)
    bucket = jnp.asarray(_bucket_table())
    b_col = b_sp.reshape(GROUPS, BLOCK, 1)

    def shard(name):
        return args[name][0].T if name.endswith("w_in") else args[name][0]

    lands = _prep_weights([shard(n) for n in big])
    (send_in, recv_in, fly_in), (send_rest, recv_rest, fly_rest), token = _ag_start(lands[:1], lands[1:])
    mid_in, fwd_send_in, fwd_recv_in = _ag_mid(fly_in, send_in, recv_in, token, "ag_mid_w_in")
    g_in = _ag_end(mid_in, fwd_send_in, fwd_recv_in, "ag_end_w_in")[0]
    full_in = g_in.reshape(D_IN, D_MODEL)

    zuv, qkv, hn1 = _in_proj(x2, g1, full_in)
    mix = _mixer_fwd(zuv, qkv, gv, w_sp[0], b_col, sinks, table, bucket)
    mid, fwd_send, fwd_recv = _ag_mid(fly_rest, send_rest, recv_rest, mix, "ag_mid_rest")
    g_out, g_ff1, g_ff2, g_proj, g_gate = _ag_end(mid, fwd_send, fwd_recv, "ag_end_rest")
    full_out, full_ff2, full_gate = g_out.reshape(D_MODEL, D_MODEL), g_ff2.reshape(D_FF, D_MODEL), g_gate.reshape(D_MODEL, D_MODEL)
    full_proj = g_proj.transpose(1, 0, 2).reshape(PLE_DIM, D_MODEL)

    (tail_small, dh1, dh1b, dmix, hn2, a, df, h2, dh2, dgl, dpp) = _tail(
        x2, mix, p2, t2, g2, gf.reshape(1, D_MODEL), full_out, g_ff1, full_ff2, full_gate, full_proj)

    gw_ff1 = _wgrad(hn2, df, D_MODEL, D_FF // N_DEV, "wgrad_ff1", peer_cols=True)
    gw_ff2 = _wgrad(a, dh2, 1024, D_MODEL, "wgrad_ff2").reshape(N_DEV, D_FF // N_DEV, D_MODEL)
    ffn_swap = _swap_start([gw_ff1, gw_ff2], "ffn")
    gw_gate = _wgrad(h2, dgl, D_MODEL, 512, "wgrad_gate", after=ffn_swap[4]).reshape(N_DEV, D_MODEL // N_DEV, D_MODEL)
    gw_proj = _wgrad(p2, dpp, PLE_DIM, D_MODEL // N_DEV, "wgrad_proj", peer_cols=True, after=ffn_swap[4])
    gw_out = _wgrad(mix, dh1b, D_MODEL, 512, "wgrad_out", after=ffn_swap[4]).reshape(N_DEV, D_MODEL // N_DEV, D_MODEL)
    mid_swap = _swap_start([gw_out, gw_proj, gw_gate], "mid")
    early_names = ("w_ff1", "w_ff2", "w_out", "w_ple_proj", "w_ple_gate")
    early_sums = _swap_sums(ffn_swap, early_names[:2], mid_swap[4], "ffn")
    early_sums += _swap_sums(mid_swap, early_names[2:], early_sums[-1][0], "mid")
    early_chips = _chips_start(early_sums, "early")

    dzm, dkv, d_gv, d_wsp, d_bsp, d_attn = _mixer_bwd(
        zuv, qkv, dmix, gv, w_sp[0], b_col, sinks, table, bucket, after=early_chips[4])
    gw_in = _wgrad_in(dzm, dkv, hn1).reshape(N_DEV, D_IN // N_DEV, D_MODEL)
    in_swap = _swap_start([gw_in], "in")
    dx, d_g1 = _in_bwd(x2, dh1, dzm, dkv, g1, full_in, after=in_swap[4])
    grad_x = dx.reshape(x.shape)

    gathered = _all_gather([tail_small, d_g1, d_gv, d_wsp, d_bsp, d_attn], F32, "gather_small_grads")
    in_sums = _swap_sums(in_swap, ("w_in",), gathered[0], "in")
    in_chips = _chips_start(in_sums, "in")
    views = {"w_spatial": (GROUPS, BLOCK, BLOCK), "b_spatial": (GROUPS, BLOCK), "final_gain": (1, D_MODEL)}
    small_in = [[args[pre + n].reshape(views.get(n, args[n].shape)) for n in SMALL] for pre in ("", "m_", "v_")]
    loss, *small_out = _adamw_small(gathered, *small_in, after=in_chips[4])
    grads, deltas, new_m, new_v = {}, {}, {}, {}
    for dst, arrays in zip((grads, deltas, new_m, new_v), small_out):
        for n, arr in zip(SMALL, arrays):
            dst[n] = arr.reshape(args[n].shape)
    loss = loss[0, 0]

    after = small_out[1][0]
    for chips, sums, names, tag in ((early_chips, early_sums, early_names, "early"), (in_chips, in_sums, ("w_in",), "in")):
        for n, (part, _), recv in zip(names, sums, _chips_wait(chips, after, tag)):
            g, d, mo, vo = _final_adamw(part, recv, shard(n), shard("m_" + n), shard("v_" + n), "adamw_" + n)
            for dst, arr in zip((grads, deltas, new_m, new_v), (g, d, mo, vo)):
                dst[n] = (arr.T if n == "w_in" else arr)[None]
            after = d

    order = ("norm1_gain", "w_in", "gmlp_v_gain", "w_spatial", "b_spatial", "attn_sinks", "rel_bias_table", "w_out",
             "norm2_gain", "w_ff1", "w_ff2", "w_ple_proj", "w_ple_gate", "final_gain")
    return (loss, grad_x, *[grads[n] for n in order], *[deltas[n] for n in order],
            *[new_m[n] for n in order], *[new_v[n] for n in order])
```

```python
import functools
import math

import numpy as np
import jax
import jax.numpy as jnp
from jax import lax
from jax.experimental import pallas as pl
from jax.experimental.pallas import tpu as pltpu

F32 = jnp.float32
BF = jnp.bfloat16
MESH = pl.DeviceIdType.MESH
N_DEV = 8

D_MODEL = 1024
PLE_DIM = 256
D_GMLP = 512
GROUPS = 4
GDIM = 128
BLOCK = 128
D_ATTN = 512
HEAD_DIM = 64
N_Q = 8
Q_PER_KV = 4
N_KV = N_Q // Q_PER_KV
ROWS4 = Q_PER_KV * BLOCK
D_KV = 128
D_FF = 4096
D_IN = 1792
D_MAIN = 2 * D_GMLP + D_ATTN
REL_BUCKETS = 32
EPS = 1e-6
NEG_INF = -1e30
SCALE = HEAD_DIM ** -0.5
GELU_C = math.sqrt(2.0 / math.pi)
GELU_A = 0.044715

ADAM_LR = 0.001
ADAM_B1 = 0.9
ADAM_B2 = 0.999
ADAM_EPS = 1e-08
ADAM_WD = 0.01
ADAM_STEP = 10

V7X_VMEM_LIMIT = 60000 * 1024
TOK_TILE = 256


def _call(body, after=None, **kw):
    if after is None:
        return pl.pallas_call(body, **kw)
    n_in = len(kw["in_specs"])

    def ordered(*refs):
        body(*refs[:n_in], *refs[n_in + 1:])

    kw["in_specs"] = list(kw["in_specs"]) + [pl.BlockSpec(memory_space=pl.ANY)]
    fn = pl.pallas_call(ordered, **kw)
    return lambda *operands: fn(*operands, after)


def _params(sem=None):
    if sem is None:
        return pltpu.CompilerParams(vmem_limit_bytes=V7X_VMEM_LIMIT)
    return pltpu.CompilerParams(dimension_semantics=sem, vmem_limit_bytes=V7X_VMEM_LIMIT)


def _nn(a, b):
    return jnp.dot(a, b, preferred_element_type=F32)


def _nt(a, b):
    return lax.dot_general(a, b, (((1,), (1,)), ((), ())), preferred_element_type=F32)


def _tn(a, b):
    return lax.dot_general(a, b, (((0,), (0,)), ((), ())), preferred_element_type=F32)


def _gelu_tanh(x):
    return jnp.tanh(GELU_C * (x + GELU_A * (x * x * x)))


def _gelu(x, t):
    return x * (0.5 * (1.0 + t))


def _gelu_and_grad(x, t):
    cdf = 0.5 * (1.0 + t)
    return x * cdf, cdf + 0.5 * x * (1.0 - t * t) * (GELU_C * (1.0 + 3.0 * GELU_A * (x * x)))


def _rms_scale(x):
    return lax.rsqrt(jnp.mean(x * x, axis=-1, keepdims=True) + EPS)


def _rms_bwd(dxn, x, r):
    return r * dxn - x * ((r * r * r) * jnp.mean(dxn * x, axis=-1, keepdims=True))


def _bucket_table():
    a = np.arange(BLOCK)[:, None]
    j = np.arange(2 * BLOCK)[None, :]
    n = BLOCK + a - j
    valid = (n >= 0) & (n < BLOCK)
    nc = np.maximum(n, 0)
    max_exact = REL_BUCKETS // 2
    nf = np.maximum(nc, 1).astype(np.float32)
    large = max_exact + (
        np.log(nf / np.float32(max_exact)) / np.float32(math.log(BLOCK / max_exact)) * np.float32(REL_BUCKETS - max_exact)
    ).astype(np.int32)
    large = np.minimum(large, REL_BUCKETS - 1)
    bucket = np.where(nc < max_exact, nc, large)
    return np.where(valid, bucket, -1).astype(np.int32)


def _in_proj(x, g1, w_in_t):
    s = x.shape[0]
    tm = min(TOK_TILE, s)

    def body(x_ref, g_ref, w_ref, zuv_ref, qkv_ref, hn_ref):
        xv = x_ref[...]
        hn = ((xv * _rms_scale(xv)) * g_ref[...]).astype(BF)
        hn_ref[...] = hn
        z = _nt(hn, w_ref[...])
        zuv_ref[...] = z[:, : 2 * D_GMLP]
        qkv_ref[...] = z[:, 2 * D_GMLP:].astype(BF)

    return _call(
        body,
        name="in_proj",
        grid=(s // tm,),
        in_specs=[
            pl.BlockSpec((tm, D_MODEL), lambda i: (i, 0)),
            pl.BlockSpec((1, D_MODEL), lambda i: (0, 0)),
            pl.BlockSpec((D_IN, D_MODEL), lambda i: (0, 0)),
        ],
        out_specs=[
            pl.BlockSpec((tm, 2 * D_GMLP), lambda i: (i, 0)),
            pl.BlockSpec((tm, D_ATTN + 2 * D_KV), lambda i: (i, 0)),
            pl.BlockSpec((tm, D_MODEL), lambda i: (i, 0)),
        ],
        out_shape=[
            jax.ShapeDtypeStruct((s, 2 * D_GMLP), F32),
            jax.ShapeDtypeStruct((s, D_ATTN + 2 * D_KV), BF),
            jax.ShapeDtypeStruct((s, D_MODEL), BF),
        ],
        compiler_params=_params(("arbitrary",)),
    )(x, g1, w_in_t)


def _head_rows(h):
    kh, g = divmod(h, Q_PER_KV)
    return kh, slice(g * BLOCK, (g + 1) * BLOCK)


def _build_bias(bias_ref, bucket_ref, table_ref):
    bucket = bucket_ref[...]
    for h in range(N_Q):
        acc = jnp.zeros((BLOCK, 2 * BLOCK), F32)
        for b in range(REL_BUCKETS):
            acc = jnp.where(bucket == b, table_ref[b, h], acc)
        kh, rows = _head_rows(h)
        bias_ref[kh, rows, :] = acc


def _window_masks(i):
    row = lax.broadcasted_iota(jnp.int32, (ROWS4, BLOCK), 0) & (BLOCK - 1)
    col = lax.broadcasted_iota(jnp.int32, (ROWS4, BLOCK), 1)
    return (col > row) & (i > 0), col <= row


def _stack_heads(ref, kh, offset):
    first = offset + kh * Q_PER_KV * HEAD_DIM
    return jnp.concatenate(
        [ref[:, first + g * HEAD_DIM: first + (g + 1) * HEAD_DIM].astype(BF) for g in range(Q_PER_KV)], axis=0)


def _stack_sinks(sink_ref, kh):
    return jnp.concatenate([jnp.full((BLOCK, 1), sink_ref[0, kh * Q_PER_KV + g], F32) for g in range(Q_PER_KV)], axis=0)


def _tril_bf16(w_ref, g):
    row = lax.broadcasted_iota(jnp.int32, (BLOCK, BLOCK), 0)
    col = lax.broadcasted_iota(jnp.int32, (BLOCK, BLOCK), 1)
    return jnp.where(col <= row, w_ref[g], 0.0).astype(BF)


def _attn_probs(q_h, k_prev, k_cur, bias_h, sink, valid_prev, valid_cur):
    l_prev = jnp.where(valid_prev, _nt(q_h, k_prev) * SCALE + bias_h[:, :BLOCK], NEG_INF)
    l_cur = jnp.where(valid_cur, _nt(q_h, k_cur) * SCALE + bias_h[:, BLOCK:], NEG_INF)
    m = jnp.maximum(jnp.maximum(jnp.max(l_prev, axis=-1, keepdims=True), jnp.max(l_cur, axis=-1, keepdims=True)), sink)
    e_prev = jnp.exp(l_prev - m)
    e_cur = jnp.exp(l_cur - m)
    e_sink = jnp.exp(sink - m)
    denom = jnp.sum(e_prev, axis=-1, keepdims=True) + jnp.sum(e_cur, axis=-1, keepdims=True) + e_sink
    return e_prev / denom, e_cur / denom, e_sink / denom


def _mixer_specs(nb):
    cl = lambda i: jnp.minimum(i, nb - 1)
    return [
        pl.BlockSpec((BLOCK, 2 * D_GMLP), lambda i: (cl(i), 0)),
        pl.BlockSpec((BLOCK, D_ATTN), lambda i: (cl(i), 0)),
        pl.BlockSpec((BLOCK, 2 * D_KV), lambda i: (cl(i), D_ATTN // (2 * D_KV))),
        pl.BlockSpec((BLOCK, 2 * D_KV), lambda i: (jnp.maximum(cl(i) - 1, 0), D_ATTN // (2 * D_KV))),
        pl.BlockSpec((1, D_GMLP), lambda i: (0, 0)),
        pl.BlockSpec((GROUPS, BLOCK, BLOCK), lambda i: (0, 0, 0)),
        pl.BlockSpec((GROUPS, BLOCK, 1), lambda i: (0, 0, 0)),
        pl.BlockSpec(memory_space=pltpu.SMEM),
        pl.BlockSpec(memory_space=pltpu.SMEM),
        pl.BlockSpec((BLOCK, 2 * BLOCK), lambda i: (0, 0)),
    ]


def _mixer_fwd(zuv, qkv, gv, w_sp, b_sp, sinks, table, bucket):
    s = zuv.shape[0]
    nb = s // BLOCK

    def body(zuv_ref, q_ref, kvc_ref, kvp_ref, gv_ref, w_ref, b_ref, sink_ref, table_ref, bucket_ref,
             mix_ref, tanh_ref, prob_ref, psink_ref, bias_ref):
        i = pl.program_id(0)

        @pl.when(i == 0)
        def _():
            _build_bias(bias_ref, bucket_ref, table_ref)

        t = _gelu_tanh(zuv_ref[...])
        tanh_ref[...] = t
        u = _gelu(zuv_ref[:, :D_GMLP], t[:, :D_GMLP])
        vg = _gelu(zuv_ref[:, D_GMLP:], t[:, D_GMLP:])
        for g in range(GROUPS):
            sl = slice(g * GDIM, (g + 1) * GDIM)
            vg_g = vg[:, sl]
            vn = ((vg_g * _rms_scale(vg_g)) * gv_ref[:, sl]).astype(BF)
            sv = _nn(_tril_bf16(w_ref, g), vn) + b_ref[g]
            mix_ref[:, sl] = (u[:, sl] * sv).astype(BF)

        valid_prev, valid_cur = _window_masks(i)
        for kh in range(N_KV):
            ksl = slice(kh * HEAD_DIM, (kh + 1) * HEAD_DIM)
            vsl = slice(D_KV + kh * HEAD_DIM, D_KV + (kh + 1) * HEAD_DIM)
            q4 = _stack_heads(q_ref, kh, 0)
            p_prev, p_cur, p_sink = _attn_probs(
                q4, kvp_ref[:, ksl], kvc_ref[:, ksl], bias_ref[kh], _stack_sinks(sink_ref, kh), valid_prev, valid_cur)
            prob_ref[0, kh, :, :BLOCK] = p_prev
            prob_ref[0, kh, :, BLOCK:] = p_cur
            psink_ref[0, kh] = jnp.broadcast_to(p_sink, (ROWS4, LANES))
            o4 = _nn(p_prev.astype(BF), kvp_ref[:, vsl]) + _nn(p_cur.astype(BF), kvc_ref[:, vsl])
            for g in range(Q_PER_KV):
                first = D_GMLP + (kh * Q_PER_KV + g) * HEAD_DIM
                mix_ref[:, first:first + HEAD_DIM] = o4[g * BLOCK:(g + 1) * BLOCK].astype(BF)

    return _call(
        body,
        name="mixer_fwd",
        grid=(nb,),
        in_specs=_mixer_specs(nb),
        out_specs=[
            pl.BlockSpec((BLOCK, D_MODEL), lambda i: (i, 0)),
            pl.BlockSpec((BLOCK, 2 * D_GMLP), lambda i: (i, 0)),
            pl.BlockSpec((1, N_KV, ROWS4, 2 * BLOCK), lambda i: (i, 0, 0, 0)),
            pl.BlockSpec((1, N_KV, ROWS4, LANES), lambda i: (i, 0, 0, 0)),
        ],
        out_shape=[
            jax.ShapeDtypeStruct((s, D_MODEL), BF),
            jax.ShapeDtypeStruct((s, 2 * D_GMLP), F32),
            jax.ShapeDtypeStruct((nb, N_KV, ROWS4, 2 * BLOCK), F32),
            jax.ShapeDtypeStruct((nb, N_KV, ROWS4, LANES), F32),
        ],
        scratch_shapes=[pltpu.VMEM((N_KV, ROWS4, 2 * BLOCK), F32)],
        compiler_params=_params(("arbitrary",)),
    )(zuv, qkv, qkv, qkv, gv, w_sp, b_sp, sinks, table, bucket)


def _tail(x, mix, p, t, g2, gf, w_out, w_ff1, w_ff2, w_gate, w_proj):
    s = x.shape[0]
    tm = min(TOK_TILE, s)
    n_ff = w_ff1.shape[0]
    fc = D_FF // n_ff

    def body(x_ref, mix_ref, p_ref, t_ref, g2_ref, gf_ref, wo_ref, w1_ref, w2_ref, wg_ref, wp_ref,
             small_ref, dh1_ref, dh1b_ref, dmix_ref, hn2_ref, a_ref, df_ref, h2_ref, dh2_ref, dgl_ref, dpp_ref, f_ref):
        i = pl.program_id(0)

        @pl.when(i == 0)
        def _():
            small_ref[...] = jnp.zeros_like(small_ref)

        h1 = x_ref[...] + _nn(mix_ref[...], wo_ref[...])
        r2 = _rms_scale(h1)
        hn2 = ((h1 * r2) * g2_ref[...]).astype(BF)
        hn2_ref[...] = hn2
        h2 = h1
        for c in range(n_ff):
            f = _nn(hn2, w1_ref[c])
            f_ref[:, c * fc:(c + 1) * fc] = f
            a = jnp.square(jnp.maximum(f, 0.0)).astype(BF)
            a_ref[:, c * fc:(c + 1) * fc] = a
            h2 = h2 + _nn(a, w2_ref[c * fc:(c + 1) * fc, :])
        h2b = h2.astype(BF)
        h2_ref[...] = h2b
        gate = jax.nn.sigmoid(_nn(h2b, wg_ref[...]))
        pp = _nn(p_ref[...].astype(BF), wp_ref[...])
        h3 = h2 + gate * pp
        rf = _rms_scale(h3)
        gf_v = gf_ref[...]
        err = (h3 * rf) * gf_v - t_ref[...]
        small_ref[2:3, :] += jnp.sum(jnp.sum(err * err, axis=-1, keepdims=True), axis=0, keepdims=True) * (0.5 / D_MODEL)

        dy = err * (1.0 / D_MODEL)
        small_ref[1:2, :] += jnp.sum(dy * (h3 * rf), axis=0, keepdims=True)
        dh3 = _rms_bwd(dy * gf_v, h3, rf)
        dpp_ref[...] = (dh3 * gate).astype(BF)
        dgl = ((dh3 * pp) * (gate * (1.0 - gate))).astype(BF)
        dgl_ref[...] = dgl
        dh2 = dh3 + _nt(dgl, wg_ref[...])
        dh2b = dh2.astype(BF)
        dh2_ref[...] = dh2b
        dhn2 = jnp.zeros((tm, D_MODEL), F32)
        for c in range(n_ff):
            da = _nt(dh2b, w2_ref[c * fc:(c + 1) * fc, :])
            df = (da * (2.0 * jnp.maximum(f_ref[:, c * fc:(c + 1) * fc], 0.0))).astype(BF)
            df_ref[:, c * fc:(c + 1) * fc] = df
            dhn2 = dhn2 + _nt(df, w1_ref[c])
        small_ref[0:1, :] += jnp.sum(dhn2 * (h1 * r2), axis=0, keepdims=True)
        dh1 = dh2 + _rms_bwd(dhn2 * g2_ref[...], h1, r2)
        dh1_ref[...] = dh1
        dh1b = dh1.astype(BF)
        dh1b_ref[...] = dh1b
        dmix_ref[...] = _nt(dh1b, wo_ref[...])

    tile = lambda cols: pl.BlockSpec((tm, cols), lambda i: (i, 0))
    whole = lambda shape: pl.BlockSpec(shape, lambda i: (0,) * len(shape), pipeline_mode=pl.Buffered(1))
    row = pl.BlockSpec((1, D_MODEL), lambda i: (0, 0))
    act = lambda cols, dt: jax.ShapeDtypeStruct((s, cols), dt)
    return _call(
        body,
        name="tail",
        grid=(s // tm,),
        in_specs=[tile(D_MODEL), tile(D_MODEL), tile(PLE_DIM), tile(D_MODEL), row, row,
                  whole(w_out.shape), whole(w_ff1.shape), whole(w_ff2.shape), whole(w_gate.shape), whole(w_proj.shape)],
        out_specs=[pl.BlockSpec((8, D_MODEL), lambda i: (0, 0)), tile(D_MODEL), tile(D_MODEL), tile(D_MODEL), tile(D_MODEL), tile(D_FF),
                   tile(D_FF), tile(D_MODEL), tile(D_MODEL), tile(D_MODEL), tile(D_MODEL)],
        out_shape=[jax.ShapeDtypeStruct((8, D_MODEL), F32),
                   act(D_MODEL, F32), act(D_MODEL, BF), act(D_MODEL, F32), act(D_MODEL, BF), act(D_FF, BF), act(D_FF, BF), act(D_MODEL, BF),
                   act(D_MODEL, BF), act(D_MODEL, BF), act(D_MODEL, BF)],
        scratch_shapes=[pltpu.VMEM((tm, D_FF), F32)],
        compiler_params=_params(("arbitrary",)),
    )(x, mix, p, t, g2, gf, w_out, w_ff1, w_ff2, w_gate, w_proj)


def _mixer_bwd(zuv, qkv, dmix, saved, gv, w_sp, b_sp, bucket, after=None):
    s = zuv.shape[0]
    nb = s // BLOCK

    def body(zuv_ref, q_ref, kvc_ref, kvp_ref, gv_ref, w_ref, b_ref, bucket_ref, dmix_ref, tanh_ref, prob_ref, psink_ref,
             dzm_ref, dkv_ref, dgv_ref, dw_ref, db_ref, dattn_ref,
             dbias_ref, carry_ref, dsink_acc, db_acc):
        i = pl.program_id(0)

        @pl.when(i == 0)
        def _():
            dbias_ref[...] = jnp.zeros_like(dbias_ref)
            carry_ref[...] = jnp.zeros_like(carry_ref)
            dsink_acc[...] = jnp.zeros_like(dsink_acc)
            dgv_ref[...] = jnp.zeros_like(dgv_ref)
            dw_ref[...] = jnp.zeros_like(dw_ref)
            db_acc[...] = jnp.zeros_like(db_acc)

        @pl.when(i < nb)
        def _():
            u, du_dz = _gelu_and_grad(zuv_ref[:, :D_GMLP], tanh_ref[:, :D_GMLP])
            vg, dvg_dz = _gelu_and_grad(zuv_ref[:, D_GMLP:], tanh_ref[:, D_GMLP:])
            for g in range(GROUPS):
                sl = slice(g * GDIM, (g + 1) * GDIM)
                vg_g = vg[:, sl]
                rg = _rms_scale(vg_g)
                vhat = vg_g * rg
                gain = gv_ref[:, sl]
                vn = (vhat * gain).astype(BF)
                w_g = _tril_bf16(w_ref, g)
                sv = _nn(w_g, vn) + b_ref[g]
                dmix_a = dmix_ref[:, sl]
                dsv = dmix_a * u[:, sl]
                dsvb = dsv.astype(BF)
                db_acc[g] += jnp.sum(dsv, axis=-1, keepdims=True)
                dw_ref[g] += _nt(dsvb, vn)
                dvn = _tn(w_g, dsvb)
                dgv_ref[:, sl] += jnp.sum(dvn * vhat, axis=0, keepdims=True)
                dvg = _rms_bwd(dvn * gain, vg_g, rg)
                dzm_ref[:, sl] = ((dmix_a * sv) * du_dz[:, sl]).astype(BF)
                dzm_ref[:, D_GMLP + g * GDIM: D_GMLP + (g + 1) * GDIM] = (dvg * dvg_dz[:, sl]).astype(BF)

            for kh in range(N_KV):
                ksl = slice(kh * HEAD_DIM, (kh + 1) * HEAD_DIM)
                vsl = slice(D_KV + kh * HEAD_DIM, D_KV + (kh + 1) * HEAD_DIM)
                k_prev, k_cur = kvp_ref[:, ksl], kvc_ref[:, ksl]
                v_prev, v_cur = kvp_ref[:, vsl], kvc_ref[:, vsl]
                q4 = _stack_heads(q_ref, kh, 0)
                p_prev, p_cur, p_sink = prob_ref[0, kh, :, :BLOCK], prob_ref[0, kh, :, BLOCK:], psink_ref[0, kh, :, 0:1]
                do4 = _stack_heads(dmix_ref, kh, D_GMLP)
                dp_prev = _nt(do4, v_prev)
                dp_cur = _nt(do4, v_cur)
                delta = jnp.sum(p_prev * dp_prev, axis=-1, keepdims=True) + jnp.sum(p_cur * dp_cur, axis=-1, keepdims=True)
                ds_prev = p_prev * (dp_prev - delta)
                ds_cur = p_cur * (dp_cur - delta)
                dsink_acc[kh] -= p_sink * delta
                dbias_ref[kh, :, :BLOCK] += ds_prev
                dbias_ref[kh, :, BLOCK:] += ds_cur
                dsb_prev = ds_prev.astype(BF)
                dsb_cur = ds_cur.astype(BF)
                dq4 = (_nn(dsb_prev, k_prev) + _nn(dsb_cur, k_cur)) * SCALE
                for g in range(Q_PER_KV):
                    first = 2 * D_GMLP + (kh * Q_PER_KV + g) * HEAD_DIM
                    dzm_ref[:, first:first + HEAD_DIM] = dq4[g * BLOCK:(g + 1) * BLOCK].astype(BF)
                dkv_ref[:, ksl] = (carry_ref[:, ksl] + _tn(dsb_prev, q4) * SCALE).astype(BF)
                dkv_ref[:, vsl] = (carry_ref[:, vsl] + _tn(p_prev.astype(BF), do4)).astype(BF)
                carry_ref[:, ksl] = _tn(dsb_cur, q4) * SCALE
                carry_ref[:, vsl] = _tn(p_cur.astype(BF), do4)

        @pl.when(i == nb)
        def _():
            dkv_ref[...] = carry_ref[...].astype(BF)
            row = lax.broadcasted_iota(jnp.int32, (BLOCK, BLOCK), 0)
            col = lax.broadcasted_iota(jnp.int32, (BLOCK, BLOCK), 1)
            for g in range(GROUPS):
                dw_ref[g] = jnp.where(col <= row, dw_ref[g], 0.0)
                db_ref[g:g + 1, :] = jnp.sum(jnp.where(col == row, db_acc[g], 0.0), axis=0, keepdims=True)
            bucket = bucket_ref[...]
            for h in range(N_Q):
                kh, rows = _head_rows(h)
                dattn_ref[REL_BUCKETS, h] = jnp.sum(dsink_acc[kh, rows, :])
                dbh = dbias_ref[kh, rows, :]
                for b in range(REL_BUCKETS):
                    dattn_ref[b, h] = jnp.sum(jnp.where(bucket == b, dbh, 0.0))

    cl = lambda i: jnp.minimum(i, nb - 1)
    const = lambda shape: pl.BlockSpec(shape, lambda i: (0,) * len(shape))
    return _call(
        body,
        name="mixer_bwd",
        after=after,
        grid=(nb + 1,),
        in_specs=_mixer_specs(nb)[:7] + [
            const((BLOCK, 2 * BLOCK)),
            pl.BlockSpec((BLOCK, D_MODEL), lambda i: (cl(i), 0)),
            pl.BlockSpec((BLOCK, 2 * D_GMLP), lambda i: (cl(i), 0)),
            pl.BlockSpec((1, N_KV, ROWS4, 2 * BLOCK), lambda i: (cl(i), 0, 0, 0)),
            pl.BlockSpec((1, N_KV, ROWS4, LANES), lambda i: (cl(i), 0, 0, 0)),
        ],
        out_specs=[
            pl.BlockSpec((BLOCK, D_MAIN), lambda i: (cl(i), 0)),
            pl.BlockSpec((BLOCK, 2 * D_KV), lambda i: (jnp.maximum(i - 1, 0), 0)),
            const((1, D_GMLP)),
            const((GROUPS, BLOCK, BLOCK)),
            const((GROUPS, BLOCK)),
            pl.BlockSpec(memory_space=pltpu.SMEM),
        ],
        out_shape=[
            jax.ShapeDtypeStruct((s, D_MAIN), BF),
            jax.ShapeDtypeStruct((s, 2 * D_KV), BF),
            jax.ShapeDtypeStruct((1, D_GMLP), F32),
            jax.ShapeDtypeStruct((GROUPS, BLOCK, BLOCK), F32),
            jax.ShapeDtypeStruct((GROUPS, BLOCK), F32),
            jax.ShapeDtypeStruct((REL_BUCKETS + 1, N_Q), F32),
        ],
        scratch_shapes=[
            pltpu.VMEM((N_KV, ROWS4, 2 * BLOCK), F32),
            pltpu.VMEM((BLOCK, 2 * D_KV), F32),
            pltpu.VMEM((N_KV, ROWS4, 1), F32),
            pltpu.VMEM((GROUPS, BLOCK, 1), F32),
        ],
        compiler_params=_params(("arbitrary",)),
    )(zuv, qkv, qkv, qkv, gv, w_sp, b_sp, bucket, dmix, *saved)


def _in_bwd(x, dh1, dzm, dkv, g1, w_in_t, after=None):
    s = x.shape[0]
    tm = min(TOK_TILE, s)

    def body(x_ref, dh1_ref, dzm_ref, dkv_ref, g_ref, w_ref, dx_ref, dg_ref):
        @pl.when(pl.program_id(0) == 0)
        def _():
            dg_ref[...] = jnp.zeros_like(dg_ref)

        dhn = _nn(dzm_ref[...], w_ref[:D_MAIN, :]) + _nn(dkv_ref[...], w_ref[D_MAIN:, :])
        xv = x_ref[...]
        r = _rms_scale(xv)
        dg_ref[...] += jnp.sum(dhn * (xv * r), axis=0, keepdims=True)
        dx_ref[...] = dh1_ref[...] + _rms_bwd(dhn * g_ref[...], xv, r)

    tile = lambda cols: pl.BlockSpec((tm, cols), lambda i: (i, 0))
    row = pl.BlockSpec((1, D_MODEL), lambda i: (0, 0))
    return _call(
        body,
        name="in_bwd",
        after=after,
        grid=(s // tm,),
        in_specs=[tile(D_MODEL), tile(D_MODEL), tile(D_MAIN), tile(2 * D_KV), row, pl.BlockSpec((D_IN, D_MODEL), lambda i: (0, 0))],
        out_specs=[tile(D_MODEL), row],
        out_shape=[jax.ShapeDtypeStruct((s, D_MODEL), F32), jax.ShapeDtypeStruct((1, D_MODEL), F32)],
        compiler_params=_params(("arbitrary",)),
    )(x, dh1, dzm, dkv, g1, w_in_t)


def _wgrad_in(dzm, dkv, hn):
    s = hn.shape[0]
    tm = 2 * D_KV
    n_main = D_MAIN // tm

    def body(dzm_ref, dkv_ref, hn_ref, o_ref):
        i = pl.program_id(0)

        @pl.when(i < n_main)
        def _():
            o_ref[...] = _tn(dzm_ref[...], hn_ref[...])

        @pl.when(i == n_main)
        def _():
            o_ref[...] = _tn(dkv_ref[...], hn_ref[...])

    return _call(
        body,
        name="wgrad_in",
        grid=(n_main + 1,),
        in_specs=[
            pl.BlockSpec((s, tm), lambda i: (0, jnp.minimum(i, n_main - 1))),
            pl.BlockSpec((s, tm), lambda i: (0, 0)),
            pl.BlockSpec((s, D_MODEL), lambda i: (0, 0)),
        ],
        out_specs=pl.BlockSpec((tm, D_MODEL), lambda i: (i, 0)),
        out_shape=jax.ShapeDtypeStruct((D_IN, D_MODEL), F32),
        compiler_params=_params(("arbitrary",)),
    )(dzm, dkv, hn)


def _wgrad(a, b, tm, tn, name, peer_cols=0, after=None):
    s, m = a.shape
    n = b.shape[1]

    def body(a_ref, b_ref, o_ref, at_ref):
        @pl.when(pl.program_id(1) == 0)
        def _():
            at_ref[...] = a_ref[...].astype(BF).T

        r = _nn(at_ref[...], b_ref[...])
        if peer_cols:
            for q in range(tn // peer_cols):
                o_ref[q] = r[:, q * peer_cols:(q + 1) * peer_cols]
        else:
            o_ref[...] = r

    if peer_cols:
        out_spec = pl.BlockSpec((tn // peer_cols, tm, peer_cols), lambda i, j: (j, i, 0))
        out_shape = jax.ShapeDtypeStruct((n // peer_cols, m, peer_cols), F32)
    else:
        out_spec = pl.BlockSpec((tm, tn), lambda i, j: (i, j))
        out_shape = jax.ShapeDtypeStruct((m, n), F32)
    return _call(
        body,
        name=name,
        after=after,
        grid=(m // tm, n // tn),
        in_specs=[pl.BlockSpec((s, tm), lambda i, j: (0, i)), pl.BlockSpec((s, tn), lambda i, j: (0, j))],
        out_specs=out_spec,
        out_shape=out_shape,
        scratch_shapes=[pltpu.VMEM((tm, s), BF)],
        compiler_params=_params(("arbitrary", "arbitrary")),
    )(a, b)


def _adamw_math(w, g, m, v):
    m_new = ADAM_B1 * m + (1.0 - ADAM_B1) * g
    v_new = ADAM_B2 * v + (1.0 - ADAM_B2) * jnp.square(g)
    m_hat = m_new / (1.0 - ADAM_B1 ** ADAM_STEP)
    v_hat = v_new / (1.0 - ADAM_B2 ** ADAM_STEP)
    delta = -ADAM_LR * (m_hat / (jnp.sqrt(v_hat) + ADAM_EPS) + ADAM_WD * w)
    return delta, m_new, v_new


def _final_adamw(part, recv, w, m, v, name):
    r, c = w.shape
    tr = min(r, 512)

    def body(p_ref, r_ref, w_ref, m_ref, v_ref, g_ref, d_ref, mo_ref, vo_ref):
        g = p_ref[...]
        for j in range(3):
            g = g + r_ref[j].astype(F32)
        g_ref[...] = g
        d_ref[...], mo_ref[...], vo_ref[...] = _adamw_math(w_ref[...], g, m_ref[...], v_ref[...])

    spec = pl.BlockSpec((tr, c), lambda i: (i, 0))
    return _call(
        body,
        name=name,
        grid=(r // tr,),
        in_specs=[spec, pl.BlockSpec((3, tr, c), lambda i: (0, i, 0)), spec, spec, spec],
        out_specs=[spec] * 4,
        out_shape=[jax.ShapeDtypeStruct((r, c), F32)] * 4,
        compiler_params=_params(("arbitrary",)),
    )(part, recv, w, m, v)


def _rs_sum(g, land, blocks, name):
    _, r, c = g.shape
    tr = min(r, 256)

    def body(blk_ref, g0_ref, g1_ref, g2_ref, g3_ref, l_ref, part_ref, send_ref):
        part_ref[...] = g0_ref[0] + l_ref[0]
        for j, gj_ref in enumerate((g1_ref, g2_ref, g3_ref)):
            send_ref[j] = (gj_ref[0] + l_ref[j + 1]).astype(BF)

    def pick(j):
        return pl.BlockSpec((1, tr, c), lambda i, blk: (blk[j], i, 0))

    return _call(
        body,
        name=name,
        grid_spec=pltpu.PrefetchScalarGridSpec(
            num_scalar_prefetch=1,
            grid=(r // tr,),
            in_specs=[pick(0), pick(1), pick(2), pick(3), pl.BlockSpec((4, tr, c), lambda i, blk: (0, i, 0))],
            out_specs=[pl.BlockSpec((tr, c), lambda i, blk: (i, 0)), pl.BlockSpec((3, tr, c), lambda i, blk: (0, i, 0))],
        ),
        out_shape=[jax.ShapeDtypeStruct((r, c), F32), jax.ShapeDtypeStruct((3, r, c), BF)],
        compiler_params=_params(("arbitrary",)),
    )(blocks, g, g, g, g, land)


def _adamw_small(gathered, weights, moms, vels, after):
    n_w = len(weights)

    def body(*refs):
        tail_ref, g1_ref, gv_ref, wsp_ref, bsp_ref, attn_ref = refs[:6]
        w_refs, m_refs, v_refs = refs[6:6 + n_w], refs[6 + n_w:6 + 2 * n_w], refs[6 + 2 * n_w:6 + 3 * n_w]
        outs = refs[6 + 3 * n_w:]
        loss_ref, g_refs, d_refs = outs[0], outs[1:1 + n_w], outs[1 + n_w:1 + 2 * n_w]
        mo_refs, vo_refs = outs[1 + 2 * n_w:1 + 3 * n_w], outs[1 + 3 * n_w:]

        def total(part):
            acc = part(0)
            for d in range(1, N_DEV):
                acc = acc + part(d)
            return acc

        grads = [
            total(lambda d: g1_ref[d]), total(lambda d: gv_ref[d]), total(lambda d: wsp_ref[d]), total(lambda d: bsp_ref[d]),
            total(lambda d: attn_ref[d, REL_BUCKETS:, :]), total(lambda d: attn_ref[d, :REL_BUCKETS, :]),
            total(lambda d: tail_ref[d, 0:1, :]), total(lambda d: tail_ref[d, 1:2, :])]
        loss_ref[...] = total(lambda d: tail_ref[d, 2:3, 0:1])
        for k in range(n_w):
            g_refs[k][...] = grads[k]
            d_refs[k][...], mo_refs[k][...], vo_refs[k][...] = _adamw_math(w_refs[k][...], grads[k], m_refs[k][...], v_refs[k][...])

    shapes = [jax.ShapeDtypeStruct(w.shape, F32) for w in weights]
    outs = _call(
        body,
        name="adamw_small",
        after=after,
        in_specs=[pl.BlockSpec(memory_space=pltpu.VMEM)] * (6 + 3 * n_w),
        out_shape=[jax.ShapeDtypeStruct((1, 1), F32)] + shapes * 4,
        compiler_params=_params(),
    )(*gathered, *weights, *moms, *vels)
    return outs[0], outs[1:1 + n_w], outs[1 + n_w:1 + 2 * n_w], outs[1 + 2 * n_w:1 + 3 * n_w], outs[1 + 3 * n_w:]


def _place():
    x, y, c = lax.axis_index("x"), lax.axis_index("y"), lax.axis_index("c")
    return x, y, c, [(1 - x, y), (x, 1 - y), (1 - x, 1 - y)]


def _dev_index(px, py, pc):
    return 4 * px + 2 * py + pc


def _all_gather(shards, out_dtype, name, after=None):
    k_n = len(shards)

    def body(*refs):
        ins, outs, stage = refs[:k_n], refs[k_n:2 * k_n], refs[2 * k_n:3 * k_n]
        send_sems, recv_sems, local_sems = refs[3 * k_n:]
        x, y, c, chips = _place()
        me, sibling = (x, y, c), (x, y, 1 - c)

        def copy(k, j, block, to, src=None):
            dst = outs[k].at[_dev_index(*block)]
            return pltpu.make_async_remote_copy(
                src_ref=dst if src is None else src, dst_ref=dst,
                send_sem=send_sems.at[k, j], recv_sem=recv_sems.at[k, j], device_id=to, device_id_type=MESH)

        mine, first = [], []
        for k in range(k_n):
            stage[k][...] = ins[k][...].astype(out_dtype)
            mine.append(pltpu.make_async_copy(stage[k], outs[k].at[_dev_index(*me)], local_sems.at[k]))
            mine[k].start()
            first.append(copy(k, 0, me, sibling, src=stage[k]))
            first += [copy(k, 1 + j, me, (*chip, c), src=stage[k]) for j, chip in enumerate(chips)]
        for cp in first:
            cp.start()
        passed = []
        for k in range(k_n):
            for j, chip in enumerate(chips):
                copy(k, 1 + j, (*chip, c), me).wait_recv()
                passed.append(copy(k, 4 + j, (*chip, c), sibling))
                passed[-1].start()
        for k in range(k_n):
            copy(k, 0, sibling, me).wait_recv()
            for j, chip in enumerate(chips):
                copy(k, 4 + j, (*chip, 1 - c), me).wait_recv()
        for cp in first + passed:
            cp.wait_send()
        for cp in mine:
            cp.wait()

    return _call(
        body,
        name=name,
        after=after,
        in_specs=[pl.BlockSpec(memory_space=pltpu.VMEM)] * k_n,
        out_specs=[pl.BlockSpec(memory_space=pl.ANY)] * k_n,
        out_shape=[jax.ShapeDtypeStruct((N_DEV,) + sh.shape, out_dtype) for sh in shards],
        scratch_shapes=[pltpu.VMEM(sh.shape, out_dtype) for sh in shards]
        + [pltpu.SemaphoreType.DMA((k_n, 7)), pltpu.SemaphoreType.DMA((k_n, 7)), pltpu.SemaphoreType.DMA((k_n,))],
        compiler_params=_params(),
    )(*shards)


HBM_SPEC = pl.BlockSpec(memory_space=pltpu.HBM)
SEM_SPEC = pl.BlockSpec(memory_space=pltpu.SEMAPHORE)
ANY_SPEC = pl.BlockSpec(memory_space=pl.ANY)
DATAFLOW = pltpu.SideEffectType.DATAFLOW_SIDE_EFFECTING


def _hbm(a):
    return pltpu.with_memory_space_constraint(a, pltpu.HBM)


def _prep_weights(shards):
    k_n = len(shards)

    def body(*refs):
        ins, outs, stage, sems = refs[:k_n], refs[k_n:2 * k_n], refs[2 * k_n:3 * k_n], refs[3 * k_n]
        x, y, c, _ = _place()
        copies = []
        for k in range(k_n):
            stage[k][...] = ins[k][...].astype(BF)
            copies.append(pltpu.make_async_copy(stage[k], outs[k].at[_dev_index(x, y, c)], sems.at[k]))
            copies[k].start()
        for cp in copies:
            cp.wait()

    return _call(
        body,
        name="prep_weights",
        in_specs=[pl.BlockSpec(memory_space=pltpu.VMEM)] * k_n,
        out_specs=[ANY_SPEC] * k_n,
        out_shape=[jax.ShapeDtypeStruct((N_DEV,) + sh.shape, BF) for sh in shards],
        scratch_shapes=[pltpu.VMEM(sh.shape, BF) for sh in shards] + [pltpu.SemaphoreType.DMA((k_n,))],
        compiler_params=_params(),
    )(*shards)


def _ag_start(first, rest):
    lands = list(first) + list(rest)
    k_n, k_first = len(lands), len(first)

    def body(*refs):
        land = refs[:k_n]
        sems = refs[k_n:k_n + 4]
        token = refs[-1]
        x, y, c, chips = _place()
        targets = [(x, y, 1 - c)] + [(*chip, c) for chip in chips]
        for k in range(k_n):
            blk = land[k].at[_dev_index(x, y, c)]
            send_sems, recv_sems, base = (sems[0], sems[1], k) if k < k_first else (sems[2], sems[3], k - k_first)
            for j, to in enumerate(targets):
                pltpu.make_async_remote_copy(
                    src_ref=blk, dst_ref=blk, send_sem=send_sems.at[4 * base + j], recv_sem=recv_sems.at[4 * base + j],
                    device_id=to, device_id_type=MESH).start()
        token[...] = jnp.zeros_like(token)

    k_rest = k_n - k_first
    outs = pl.pallas_call(
        body,
        name="ag_start",
        in_specs=[HBM_SPEC] * k_n,
        out_specs=(SEM_SPEC, SEM_SPEC, SEM_SPEC, SEM_SPEC, *[HBM_SPEC] * k_n, pl.BlockSpec(memory_space=pltpu.VMEM)),
        out_shape=(pltpu.SemaphoreType.DMA((4 * k_first,)), pltpu.SemaphoreType.DMA((4 * k_first,)),
                   pltpu.SemaphoreType.DMA((4 * k_rest,)), pltpu.SemaphoreType.DMA((4 * k_rest,)),
                   *[pltpu.HBM(a.shape, a.dtype) for a in lands], jax.ShapeDtypeStruct((8, LANES), F32)),
        input_output_aliases={k: 4 + k for k in range(k_n)},
        compiler_params=pltpu.CompilerParams(has_side_effects=DATAFLOW),
    )(*[_hbm(a) for a in lands])
    flying = list(outs[4:4 + k_n])
    return (outs[0], outs[1], flying[:k_first]), (outs[2], outs[3], flying[k_first:]), outs[-1]


def _ag_mid(lands, send_sems, recv_sems, after, name):
    k_n = len(lands)

    def body(*refs):
        land = refs[:k_n]
        send1, recv1 = refs[k_n], refs[k_n + 1]
        fwd_send, fwd_recv = refs[-2], refs[-1]
        x, y, c, chips = _place()
        sources = [(x, y, 1 - c)] + [(*chip, c) for chip in chips]
        for k in range(k_n):
            mine = land[k].at[_dev_index(x, y, c)]
            for j, frm in enumerate(sources):
                got = land[k].at[_dev_index(*frm)]
                cp = pltpu.make_async_remote_copy(
                    src_ref=mine, dst_ref=got, send_sem=send1.at[4 * k + j], recv_sem=recv1.at[4 * k + j], device_id=frm, device_id_type=MESH)
                cp.wait_send()
                cp.wait_recv()
                if j >= 1:
                    pltpu.make_async_remote_copy(
                        src_ref=got, dst_ref=got, send_sem=fwd_send.at[3 * k + j - 1], recv_sem=fwd_recv.at[3 * k + j - 1],
                        device_id=(x, y, 1 - c), device_id_type=MESH).start()

    outs = pl.pallas_call(
        body,
        name=name,
        in_specs=[HBM_SPEC] * k_n + [SEM_SPEC, SEM_SPEC, ANY_SPEC],
        out_specs=(*[HBM_SPEC] * k_n, SEM_SPEC, SEM_SPEC),
        out_shape=(*[pltpu.HBM(a.shape, a.dtype) for a in lands], pltpu.SemaphoreType.DMA((3 * k_n,)), pltpu.SemaphoreType.DMA((3 * k_n,))),
        input_output_aliases={k: k for k in range(k_n)},
        compiler_params=pltpu.CompilerParams(has_side_effects=DATAFLOW),
    )(*lands, send_sems, recv_sems, after)
    return list(outs[:k_n]), outs[-2], outs[-1]


def _ag_end(lands, fwd_send, fwd_recv, name):
    k_n = len(lands)

    def body(*refs):
        land = refs[:k_n]
        fsend, frecv = refs[k_n], refs[k_n + 1]
        x, y, c, chips = _place()
        for k in range(k_n):
            for j, chip in enumerate(chips):
                cp = pltpu.make_async_remote_copy(
                    src_ref=land[k].at[_dev_index(*chip, c)], dst_ref=land[k].at[_dev_index(*chip, 1 - c)],
                    send_sem=fsend.at[3 * k + j], recv_sem=frecv.at[3 * k + j], device_id=(x, y, 1 - c), device_id_type=MESH)
                cp.wait_send()
                cp.wait_recv()

    outs = pl.pallas_call(
        body,
        name=name,
        in_specs=[HBM_SPEC] * k_n + [SEM_SPEC, SEM_SPEC],
        out_specs=tuple([HBM_SPEC] * k_n),
        out_shape=tuple(pltpu.HBM(a.shape, a.dtype) for a in lands),
        input_output_aliases={k: k for k in range(k_n)},
        compiler_params=pltpu.CompilerParams(has_side_effects=DATAFLOW),
    )(*lands, fwd_send, fwd_recv)
    return list(outs)


def _chips4():
    x, y, c, others = _place()
    return x, y, c, [(x, y)] + others


def _route_sibling(j):
    x, y, c, chips = _chips4()
    return _dev_index(*chips[j], 1 - c), j, (x, y, 1 - c)


def _route_chips(j):
    x, y, c, chips = _chips4()
    return j, j, (*chips[j + 1], c)


def _xchg_copy(route, n, k, j, src, dst, send_sems, recv_sems):
    si, di, peer = route(j)
    return pltpu.make_async_remote_copy(
        src_ref=src[k].at[si], dst_ref=dst[k].at[di], send_sem=send_sems.at[n * k + j], recv_sem=recv_sems.at[n * k + j],
        device_id=peer, device_id_type=MESH)


def _xchg_start(srcs, slot_shapes, route, n, name):
    k_n = len(srcs)
    dsts = [lax.empty((n,) + tuple(sh), a.dtype) for sh, a in zip(slot_shapes, srcs)]

    def body(*refs):
        src, dst = refs[:k_n], refs[k_n:2 * k_n]
        send_sems, recv_sems, token = refs[2 * k_n], refs[2 * k_n + 1], refs[-1]
        for k in range(k_n):
            for j in range(n):
                _xchg_copy(route, n, k, j, src, dst, send_sems, recv_sems).start()
        token[...] = jnp.zeros_like(token)

    arrays = list(srcs) + dsts
    outs = pl.pallas_call(
        body,
        name=name,
        in_specs=[HBM_SPEC] * (2 * k_n),
        out_specs=(SEM_SPEC, SEM_SPEC, *[HBM_SPEC] * (2 * k_n), pl.BlockSpec(memory_space=pltpu.VMEM)),
        out_shape=(pltpu.SemaphoreType.DMA((n * k_n,)), pltpu.SemaphoreType.DMA((n * k_n,)),
                   *[pltpu.HBM(a.shape, a.dtype) for a in arrays], jax.ShapeDtypeStruct((8, LANES), F32)),
        input_output_aliases={i: 2 + i for i in range(2 * k_n)},
        compiler_params=pltpu.CompilerParams(has_side_effects=DATAFLOW),
    )(*[_hbm(a) for a in arrays])
    return outs[0], outs[1], list(outs[2:2 + k_n]), list(outs[2 + k_n:2 + 2 * k_n]), outs[-1]


def _xchg_wait(send_sems, recv_sems, srcs, dsts, route, n, after, name):
    k_n = len(srcs)

    def body(*refs):
        src, dst = refs[:k_n], refs[k_n:2 * k_n]
        send, recv = refs[2 * k_n], refs[2 * k_n + 1]
        for k in range(k_n):
            for j in range(n):
                cp = _xchg_copy(route, n, k, j, src, dst, send, recv)
                cp.wait_send()
                cp.wait_recv()

    arrays = list(srcs) + list(dsts)
    outs = pl.pallas_call(
        body,
        name=name,
        in_specs=[HBM_SPEC] * (2 * k_n) + [SEM_SPEC, SEM_SPEC, ANY_SPEC],
        out_specs=tuple([HBM_SPEC] * (2 * k_n)),
        out_shape=tuple(pltpu.HBM(a.shape, a.dtype) for a in arrays),
        input_output_aliases={i: i for i in range(2 * k_n)},
        compiler_params=pltpu.CompilerParams(has_side_effects=DATAFLOW),
    )(*arrays, send_sems, recv_sems, after)
    return list(outs[:k_n]), list(outs[k_n:])


SMALL = ("norm1_gain", "gmlp_v_gain", "w_spatial", "b_spatial", "attn_sinks", "rel_bias_table", "norm2_gain", "final_gain")
LANES = 128


def _swap_start(grads, tag):
    return _xchg_start(grads, [g.shape[1:] for g in grads], _route_sibling, 4, f"rs_{tag}_swap_start")


def _swap_sums(swap, names, after, tag):
    send1, recv1, src1, land1, _ = swap
    x, y, c, chips = _chips4()
    blocks = jnp.stack([_dev_index(*chip, c) for chip in chips]).astype(jnp.int32)
    src1, land1 = _xchg_wait(send1, recv1, src1, land1, _route_sibling, 4, after, f"rs_{tag}_swap_wait")
    return [_rs_sum(g, land, blocks, f"rs_sum_{n}") for g, land, n in zip(src1, land1, names)]


def _chips_start(sums, tag):
    sends = [ps[1] for ps in sums]
    return _xchg_start(sends, [a.shape[1:] for a in sends], _route_chips, 3, f"rs_{tag}_chips_start")


def _chips_wait(chips, after, tag):
    send2, recv2, src2, land2, _ = chips
    return _xchg_wait(send2, recv2, src2, land2, _route_chips, 3, after, f"rs_{tag}_chips_wait")[1]


def kernel(x, p, norm1_gain, w_in, gmlp_v_gain, w_spatial, b_spatial, attn_sinks, rel_bias_table, w_out, norm2_gain, w_ff1, w_ff2, w_ple_proj, w_ple_gate, final_gain, loss_target, m_norm1_gain, m_w_in, m_gmlp_v_gain, m_w_spatial, m_b_spatial, m_attn_sinks, m_rel_bias_table, m_w_out, m_norm2_gain, m_w_ff1, m_w_ff2, m_w_ple_proj, m_w_ple_gate, m_final_gain, v_norm1_gain, v_w_in, v_gmlp_v_gain, v_w_spatial, v_b_spatial, v_attn_sinks, v_rel_bias_table, v_w_out, v_norm2_gain, v_w_ff1, v_w_ff2, v_w_ple_proj, v_w_ple_gate, v_final_gain):
    args = dict(locals())
    s = x.shape[1]
    big = ("w_in", "w_out", "w_ff1", "w_ff2", "w_ple_proj", "w_ple_gate")

    x2, p2, t2 = x.reshape(s, D_MODEL), p.reshape(s, PLE_DIM), loss_target.reshape(s, D_MODEL)
    g1, gv, w_sp, b_sp, sinks, table, g2, gf = (args[n] for n in SMALL)
    bucket = jnp.asarray(_bucket_table())
    b_col = b_sp.reshape(GROUPS, BLOCK, 1)

    def shard(name):
        return args[name][0].T if name.endswith("w_in") else args[name][0]

    lands = _prep_weights([shard(n) for n in big])
    (send_in, recv_in, fly_in), (send_rest, recv_rest, fly_rest), token = _ag_start(lands[:1], lands[1:])
    mid_in, fwd_send_in, fwd_recv_in = _ag_mid(fly_in, send_in, recv_in, token, "ag_mid_w_in")
    g_in = _ag_end(mid_in, fwd_send_in, fwd_recv_in, "ag_end_w_in")[0]
    full_in = g_in.reshape(D_IN, D_MODEL)

    zuv, qkv, hn1 = _in_proj(x2, g1, full_in)
    mix, *saved = _mixer_fwd(zuv, qkv, gv, w_sp[0], b_col, sinks, table, bucket)
    mid, fwd_send, fwd_recv = _ag_mid(fly_rest, send_rest, recv_rest, mix, "ag_mid_rest")
    g_out, g_ff1, g_ff2, g_proj, g_gate = _ag_end(mid, fwd_send, fwd_recv, "ag_end_rest")
    full_out, full_ff2, full_gate = g_out.reshape(D_MODEL, D_MODEL), g_ff2.reshape(D_FF, D_MODEL), g_gate.reshape(D_MODEL, D_MODEL)
    full_proj = g_proj.transpose(1, 0, 2).reshape(PLE_DIM, D_MODEL)

    (tail_small, dh1, dh1b, dmix, hn2, a, df, h2, dh2, dgl, dpp) = _tail(
        x2, mix, p2, t2, g2, gf.reshape(1, D_MODEL), full_out, g_ff1, full_ff2, full_gate, full_proj)

    gw_ff1 = _wgrad(hn2, df, D_MODEL, D_FF // N_DEV, "wgrad_ff1", peer_cols=D_FF // N_DEV)
    gw_ff2 = _wgrad(a, dh2, 1024, D_MODEL, "wgrad_ff2").reshape(N_DEV, D_FF // N_DEV, D_MODEL)
    ffn_swap = _swap_start([gw_ff1, gw_ff2], "ffn")
    gw_gate = _wgrad(h2, dgl, D_MODEL, 512, "wgrad_gate", after=ffn_swap[4]).reshape(N_DEV, D_MODEL // N_DEV, D_MODEL)
    gw_proj = _wgrad(p2, dpp, PLE_DIM, D_MODEL // 2, "wgrad_proj", peer_cols=D_MODEL // N_DEV, after=ffn_swap[4])
    gw_out = _wgrad(mix, dh1b, D_MODEL, 512, "wgrad_out", after=ffn_swap[4]).reshape(N_DEV, D_MODEL // N_DEV, D_MODEL)
    mid_swap = _swap_start([gw_out, gw_proj, gw_gate], "mid")
    early_names = ("w_ff1", "w_ff2", "w_out", "w_ple_proj", "w_ple_gate")
    early_sums = _swap_sums(ffn_swap, early_names[:2], mid_swap[4], "ffn")
    early_sums += _swap_sums(mid_swap, early_names[2:], early_sums[-1][0], "mid")
    early_chips = _chips_start(early_sums, "early")

    dzm, dkv, d_gv, d_wsp, d_bsp, d_attn = _mixer_bwd(
        zuv, qkv, dmix, saved, gv, w_sp[0], b_col, bucket, after=early_chips[4])
    gw_in = _wgrad_in(dzm, dkv, hn1).reshape(N_DEV, D_IN // N_DEV, D_MODEL)
    in_swap = _swap_start([gw_in], "in")
    dx, d_g1 = _in_bwd(x2, dh1, dzm, dkv, g1, full_in, after=in_swap[4])
    grad_x = dx.reshape(x.shape)

    gathered = _all_gather([tail_small, d_g1, d_gv, d_wsp, d_bsp, d_attn], F32, "gather_small_grads")
    in_sums = _swap_sums(in_swap, ("w_in",), gathered[0], "in")
    in_chips = _chips_start(in_sums, "in")
    views = {"w_spatial": (GROUPS, BLOCK, BLOCK), "b_spatial": (GROUPS, BLOCK), "final_gain": (1, D_MODEL)}
    small_in = [[args[pre + n].reshape(views.get(n, args[n].shape)) for n in SMALL] for pre in ("", "m_", "v_")]
    loss, *small_out = _adamw_small(gathered, *small_in, after=in_chips[4])
    grads, deltas, new_m, new_v = {}, {}, {}, {}
    for dst, arrays in zip((grads, deltas, new_m, new_v), small_out):
        for n, arr in zip(SMALL, arrays):
            dst[n] = arr.reshape(args[n].shape)
    loss = loss[0, 0]

    after = small_out[1][0]
    for chips, sums, names, tag in ((early_chips, early_sums, early_names, "early"), (in_chips, in_sums, ("w_in",), "in")):
        for n, (part, _), recv in zip(names, sums, _chips_wait(chips, after, tag)):
            g, d, mo, vo = _final_adamw(part, recv, shard(n), shard("m_" + n), shard("v_" + n), "adamw_" + n)
            for dst, arr in zip((grads, deltas, new_m, new_v), (g, d, mo, vo)):
                dst[n] = (arr.T if n == "w_in" else arr)[None]
            after = d

    order = ("norm1_gain", "w_in", "gmlp_v_gain", "w_spatial", "b_spatial", "attn_sinks", "rel_bias_table", "w_out",
             "norm2_gain", "w_ff1", "w_ff2", "w_ple_proj", "w_ple_gate", "final_gain")
    return (loss, grad_x, *[grads[n] for n in order], *[deltas[n] for n in order],
            *[new_m[n] for n in order], *[new_v[n] for n in order])
```

```python
import functools
import math

import numpy as np
import jax
import jax.numpy as jnp
from jax import lax
from jax.experimental import pallas as pl
from jax.experimental.pallas import tpu as pltpu

F32 = jnp.float32
BF = jnp.bfloat16
MESH = pl.DeviceIdType.MESH
N_DEV = 8

D_MODEL = 1024
PLE_DIM = 256
D_GMLP = 512
GROUPS = 4
GDIM = 128
BLOCK = 128
D_ATTN = 512
HEAD_DIM = 64
N_Q = 8
Q_PER_KV = 4
N_KV = N_Q // Q_PER_KV
ROWS4 = Q_PER_KV * BLOCK
D_KV = 128
D_FF = 4096
D_IN = 1792
D_MAIN = 2 * D_GMLP + D_ATTN
REL_BUCKETS = 32
EPS = 1e-6
NEG_INF = -1e30
SCALE = HEAD_DIM ** -0.5
GELU_C = math.sqrt(2.0 / math.pi)
GELU_A = 0.044715

ADAM_LR = 0.001
ADAM_B1 = 0.9
ADAM_B2 = 0.999
ADAM_EPS = 1e-08
ADAM_WD = 0.01
ADAM_STEP = 10

V7X_VMEM_LIMIT = 60000 * 1024
TOK_TILE = 256


def _call(body, after=None, **kw):
    if after is None:
        return pl.pallas_call(body, **kw)
    n_in = len(kw["in_specs"])

    def ordered(*refs):
        body(*refs[:n_in], *refs[n_in + 1:])

    kw["in_specs"] = list(kw["in_specs"]) + [pl.BlockSpec(memory_space=pl.ANY)]
    fn = pl.pallas_call(ordered, **kw)
    return lambda *operands: fn(*operands, after)


def _params(sem=None):
    if sem is None:
        return pltpu.CompilerParams(vmem_limit_bytes=V7X_VMEM_LIMIT)
    return pltpu.CompilerParams(dimension_semantics=sem, vmem_limit_bytes=V7X_VMEM_LIMIT)


def _nn(a, b):
    return jnp.dot(a, b, preferred_element_type=F32)


def _nt(a, b):
    return lax.dot_general(a, b, (((1,), (1,)), ((), ())), preferred_element_type=F32)


def _tn(a, b):
    return lax.dot_general(a, b, (((0,), (0,)), ((), ())), preferred_element_type=F32)


def _gelu_tanh(x):
    return jnp.tanh(GELU_C * (x + GELU_A * (x * x * x)))


def _gelu(x, t):
    return x * (0.5 * (1.0 + t))


def _gelu_and_grad(x, t):
    cdf = 0.5 * (1.0 + t)
    return x * cdf, cdf + 0.5 * x * (1.0 - t * t) * (GELU_C * (1.0 + 3.0 * GELU_A * (x * x)))


def _rms_scale(x):
    return lax.rsqrt(jnp.mean(x * x, axis=-1, keepdims=True) + EPS)


def _rms_bwd(dxn, x, r):
    return r * dxn - x * ((r * r * r) * jnp.mean(dxn * x, axis=-1, keepdims=True))


def _bucket_table():
    a = np.arange(BLOCK)[:, None]
    j = np.arange(2 * BLOCK)[None, :]
    n = BLOCK + a - j
    valid = (n >= 0) & (n < BLOCK)
    nc = np.maximum(n, 0)
    max_exact = REL_BUCKETS // 2
    nf = np.maximum(nc, 1).astype(np.float32)
    large = max_exact + (
        np.log(nf / np.float32(max_exact)) / np.float32(math.log(BLOCK / max_exact)) * np.float32(REL_BUCKETS - max_exact)
    ).astype(np.int32)
    large = np.minimum(large, REL_BUCKETS - 1)
    bucket = np.where(nc < max_exact, nc, large)
    return np.where(valid, bucket, -1).astype(np.int32)


def _in_proj(x, g1, w_in_t):
    s = x.shape[0]
    tm = min(TOK_TILE, s)

    def body(x_ref, g_ref, w_ref, zuv_ref, qkv_ref, hn_ref):
        xv = x_ref[...]
        hn = ((xv * _rms_scale(xv)) * g_ref[...]).astype(BF)
        hn_ref[...] = hn
        z = _nt(hn, w_ref[...])
        zuv_ref[...] = z[:, : 2 * D_GMLP]
        qkv_ref[...] = z[:, 2 * D_GMLP:].astype(BF)

    return _call(
        body,
        name="in_proj",
        grid=(s // tm,),
        in_specs=[
            pl.BlockSpec((tm, D_MODEL), lambda i: (i, 0)),
            pl.BlockSpec((1, D_MODEL), lambda i: (0, 0)),
            pl.BlockSpec((D_IN, D_MODEL), lambda i: (0, 0)),
        ],
        out_specs=[
            pl.BlockSpec((tm, 2 * D_GMLP), lambda i: (i, 0)),
            pl.BlockSpec((tm, D_ATTN + 2 * D_KV), lambda i: (i, 0)),
            pl.BlockSpec((tm, D_MODEL), lambda i: (i, 0)),
        ],
        out_shape=[
            jax.ShapeDtypeStruct((s, 2 * D_GMLP), F32),
            jax.ShapeDtypeStruct((s, D_ATTN + 2 * D_KV), BF),
            jax.ShapeDtypeStruct((s, D_MODEL), BF),
        ],
        compiler_params=_params(("arbitrary",)),
    )(x, g1, w_in_t)


def _head_rows(h):
    kh, g = divmod(h, Q_PER_KV)
    return kh, slice(g * BLOCK, (g + 1) * BLOCK)


def _build_bias(bias_ref, bucket_ref, table_ref):
    bucket = bucket_ref[...]
    for h in range(N_Q):
        acc = jnp.zeros((BLOCK, 2 * BLOCK), F32)
        for b in range(REL_BUCKETS):
            acc = jnp.where(bucket == b, table_ref[b, h], acc)
        kh, rows = _head_rows(h)
        bias_ref[kh, rows, :] = acc


def _window_masks(i):
    row = lax.broadcasted_iota(jnp.int32, (ROWS4, BLOCK), 0) & (BLOCK - 1)
    col = lax.broadcasted_iota(jnp.int32, (ROWS4, BLOCK), 1)
    return (col > row) & (i > 0), col <= row


def _stack_heads(ref, kh, offset):
    first = offset + kh * Q_PER_KV * HEAD_DIM
    return jnp.concatenate(
        [ref[:, first + g * HEAD_DIM: first + (g + 1) * HEAD_DIM].astype(BF) for g in range(Q_PER_KV)], axis=0)


def _stack_sinks(sink_ref, kh):
    return jnp.concatenate([jnp.full((BLOCK, 1), sink_ref[0, kh * Q_PER_KV + g], F32) for g in range(Q_PER_KV)], axis=0)


def _tril_bf16(w_ref, g):
    row = lax.broadcasted_iota(jnp.int32, (BLOCK, BLOCK), 0)
    col = lax.broadcasted_iota(jnp.int32, (BLOCK, BLOCK), 1)
    return jnp.where(col <= row, w_ref[g], 0.0).astype(BF)


def _attn_probs(q_h, k_prev, k_cur, bias_h, sink, valid_prev, valid_cur):
    l_prev = jnp.where(valid_prev, _nt(q_h, k_prev) * SCALE + bias_h[:, :BLOCK], NEG_INF)
    l_cur = jnp.where(valid_cur, _nt(q_h, k_cur) * SCALE + bias_h[:, BLOCK:], NEG_INF)
    m = jnp.maximum(jnp.maximum(jnp.max(l_prev, axis=-1, keepdims=True), jnp.max(l_cur, axis=-1, keepdims=True)), sink)
    e_prev = jnp.exp(l_prev - m)
    e_cur = jnp.exp(l_cur - m)
    e_sink = jnp.exp(sink - m)
    denom = jnp.sum(e_prev, axis=-1, keepdims=True) + jnp.sum(e_cur, axis=-1, keepdims=True) + e_sink
    return e_prev / denom, e_cur / denom, e_sink / denom


def _mixer_specs(nb):
    cl = lambda i: jnp.minimum(i, nb - 1)
    return [
        pl.BlockSpec((BLOCK, 2 * D_GMLP), lambda i: (cl(i), 0)),
        pl.BlockSpec((BLOCK, D_ATTN), lambda i: (cl(i), 0)),
        pl.BlockSpec((BLOCK, 2 * D_KV), lambda i: (cl(i), D_ATTN // (2 * D_KV))),
        pl.BlockSpec((BLOCK, 2 * D_KV), lambda i: (jnp.maximum(cl(i) - 1, 0), D_ATTN // (2 * D_KV))),
        pl.BlockSpec((1, D_GMLP), lambda i: (0, 0)),
        pl.BlockSpec((GROUPS, BLOCK, BLOCK), lambda i: (0, 0, 0)),
        pl.BlockSpec((GROUPS, BLOCK, 1), lambda i: (0, 0, 0)),
        pl.BlockSpec(memory_space=pltpu.SMEM),
        pl.BlockSpec(memory_space=pltpu.SMEM),
        pl.BlockSpec((BLOCK, 2 * BLOCK), lambda i: (0, 0)),
    ]


def _mixer_fwd(zuv, qkv, gv, w_sp, b_sp, sinks, table, bucket):
    s = zuv.shape[0]
    nb = s // BLOCK

    def body(zuv_ref, q_ref, kvc_ref, kvp_ref, gv_ref, w_ref, b_ref, sink_ref, table_ref, bucket_ref,
             mix_ref, tanh_ref, prob_ref, psink_ref, bias_ref):
        i = pl.program_id(0)

        @pl.when(i == 0)
        def _():
            _build_bias(bias_ref, bucket_ref, table_ref)

        t = _gelu_tanh(zuv_ref[...])
        tanh_ref[...] = t
        u = _gelu(zuv_ref[:, :D_GMLP], t[:, :D_GMLP])
        vg = _gelu(zuv_ref[:, D_GMLP:], t[:, D_GMLP:])
        for g in range(GROUPS):
            sl = slice(g * GDIM, (g + 1) * GDIM)
            vg_g = vg[:, sl]
            vn = ((vg_g * _rms_scale(vg_g)) * gv_ref[:, sl]).astype(BF)
            sv = _nn(_tril_bf16(w_ref, g), vn) + b_ref[g]
            mix_ref[:, sl] = (u[:, sl] * sv).astype(BF)

        valid_prev, valid_cur = _window_masks(i)
        for kh in range(N_KV):
            ksl = slice(kh * HEAD_DIM, (kh + 1) * HEAD_DIM)
            vsl = slice(D_KV + kh * HEAD_DIM, D_KV + (kh + 1) * HEAD_DIM)
            q4 = _stack_heads(q_ref, kh, 0)
            p_prev, p_cur, p_sink = _attn_probs(
                q4, kvp_ref[:, ksl], kvc_ref[:, ksl], bias_ref[kh], _stack_sinks(sink_ref, kh), valid_prev, valid_cur)
            prob_ref[0, kh, :, :BLOCK] = p_prev
            prob_ref[0, kh, :, BLOCK:] = p_cur
            psink_ref[0, kh] = jnp.broadcast_to(p_sink, (ROWS4, LANES))
            o4 = _nn(p_prev.astype(BF), kvp_ref[:, vsl]) + _nn(p_cur.astype(BF), kvc_ref[:, vsl])
            for g in range(Q_PER_KV):
                first = D_GMLP + (kh * Q_PER_KV + g) * HEAD_DIM
                mix_ref[:, first:first + HEAD_DIM] = o4[g * BLOCK:(g + 1) * BLOCK].astype(BF)

    return _call(
        body,
        name="mixer_fwd",
        grid=(nb,),
        in_specs=_mixer_specs(nb),
        out_specs=[
            pl.BlockSpec((BLOCK, D_MODEL), lambda i: (i, 0)),
            pl.BlockSpec((BLOCK, 2 * D_GMLP), lambda i: (i, 0)),
            pl.BlockSpec((1, N_KV, ROWS4, 2 * BLOCK), lambda i: (i, 0, 0, 0)),
            pl.BlockSpec((1, N_KV, ROWS4, LANES), lambda i: (i, 0, 0, 0)),
        ],
        out_shape=[
            jax.ShapeDtypeStruct((s, D_MODEL), BF),
            jax.ShapeDtypeStruct((s, 2 * D_GMLP), F32),
            jax.ShapeDtypeStruct((nb, N_KV, ROWS4, 2 * BLOCK), F32),
            jax.ShapeDtypeStruct((nb, N_KV, ROWS4, LANES), F32),
        ],
        scratch_shapes=[pltpu.VMEM((N_KV, ROWS4, 2 * BLOCK), F32)],
        compiler_params=_params(("arbitrary",)),
    )(zuv, qkv, qkv, qkv, gv, w_sp, b_sp, sinks, table, bucket)


def _tail(x, mix, p, t, g2, gf, w_out, w_ff1, w_ff2, w_gate, w_proj):
    s = x.shape[0]
    tm = min(TOK_TILE, s)
    n_ff = w_ff1.shape[0]
    fc = D_FF // n_ff

    def body(x_ref, mix_ref, p_ref, t_ref, g2_ref, gf_ref, wo_ref, w1_ref, w2_ref, wg_ref, wp_ref,
             small_ref, dh1_ref, dh1b_ref, dmix_ref, hn2_ref, a_ref, df_ref, h2_ref, dh2_ref, dgl_ref, dpp_ref, f_ref):
        i = pl.program_id(0)

        @pl.when(i == 0)
        def _():
            small_ref[...] = jnp.zeros_like(small_ref)

        h1 = x_ref[...] + _nn(mix_ref[...], wo_ref[...])
        r2 = _rms_scale(h1)
        hn2 = ((h1 * r2) * g2_ref[...]).astype(BF)
        hn2_ref[...] = hn2
        h2 = h1
        for c in range(n_ff):
            f = _nn(hn2, w1_ref[c])
            f_ref[:, c * fc:(c + 1) * fc] = f
            a = jnp.square(jnp.maximum(f, 0.0)).astype(BF)
            a_ref[:, c * fc:(c + 1) * fc] = a
            h2 = h2 + _nn(a, w2_ref[c * fc:(c + 1) * fc, :])
        h2b = h2.astype(BF)
        h2_ref[...] = h2b
        gate = jax.nn.sigmoid(_nn(h2b, wg_ref[...]))
        pp = _nn(p_ref[...].astype(BF), wp_ref[...])
        h3 = h2 + gate * pp
        rf = _rms_scale(h3)
        gf_v = gf_ref[...]
        err = (h3 * rf) * gf_v - t_ref[...]
        small_ref[2:3, :] += jnp.sum(jnp.sum(err * err, axis=-1, keepdims=True), axis=0, keepdims=True) * (0.5 / D_MODEL)

        dy = err * (1.0 / D_MODEL)
        small_ref[1:2, :] += jnp.sum(dy * (h3 * rf), axis=0, keepdims=True)
        dh3 = _rms_bwd(dy * gf_v, h3, rf)
        dpp_ref[...] = (dh3 * gate).astype(BF)
        dgl = ((dh3 * pp) * (gate * (1.0 - gate))).astype(BF)
        dgl_ref[...] = dgl
        dh2 = dh3 + _nt(dgl, wg_ref[...])
        dh2b = dh2.astype(BF)
        dh2_ref[...] = dh2b
        dhn2 = jnp.zeros((tm, D_MODEL), F32)
        for c in range(n_ff):
            da = _nt(dh2b, w2_ref[c * fc:(c + 1) * fc, :])
            df = (da * (2.0 * jnp.maximum(f_ref[:, c * fc:(c + 1) * fc], 0.0))).astype(BF)
            df_ref[:, c * fc:(c + 1) * fc] = df
            dhn2 = dhn2 + _nt(df, w1_ref[c])
        small_ref[0:1, :] += jnp.sum(dhn2 * (h1 * r2), axis=0, keepdims=True)
        dh1 = dh2 + _rms_bwd(dhn2 * g2_ref[...], h1, r2)
        dh1_ref[...] = dh1
        dh1b = dh1.astype(BF)
        dh1b_ref[...] = dh1b
        dmix_ref[...] = _nt(dh1b, wo_ref[...])

    tile = lambda cols: pl.BlockSpec((tm, cols), lambda i: (i, 0))
    whole = lambda shape: pl.BlockSpec(shape, lambda i: (0,) * len(shape), pipeline_mode=pl.Buffered(1))
    row = pl.BlockSpec((1, D_MODEL), lambda i: (0, 0))
    act = lambda cols, dt: jax.ShapeDtypeStruct((s, cols), dt)
    return _call(
        body,
        name="tail",
        grid=(s // tm,),
        in_specs=[tile(D_MODEL), tile(D_MODEL), tile(PLE_DIM), tile(D_MODEL), row, row,
                  whole(w_out.shape), whole(w_ff1.shape), whole(w_ff2.shape), whole(w_gate.shape), whole(w_proj.shape)],
        out_specs=[pl.BlockSpec((8, D_MODEL), lambda i: (0, 0)), tile(D_MODEL), tile(D_MODEL), tile(D_MODEL), tile(D_MODEL), tile(D_FF),
                   tile(D_FF), tile(D_MODEL), tile(D_MODEL), tile(D_MODEL), tile(D_MODEL)],
        out_shape=[jax.ShapeDtypeStruct((8, D_MODEL), F32),
                   act(D_MODEL, F32), act(D_MODEL, BF), act(D_MODEL, F32), act(D_MODEL, BF), act(D_FF, BF), act(D_FF, BF), act(D_MODEL, BF),
                   act(D_MODEL, BF), act(D_MODEL, BF), act(D_MODEL, BF)],
        scratch_shapes=[pltpu.VMEM((tm, D_FF), F32)],
        compiler_params=_params(("arbitrary",)),
    )(x, mix, p, t, g2, gf, w_out, w_ff1, w_ff2, w_gate, w_proj)


def _mixer_bwd(zuv, qkv, dmix, saved, gv, w_sp, b_sp, bucket, after=None):
    s = zuv.shape[0]
    nb = s // BLOCK

    def body(zuv_ref, q_ref, kvc_ref, kvp_ref, gv_ref, w_ref, b_ref, bucket_ref, dmix_ref, tanh_ref, prob_ref, psink_ref,
             dzm_ref, dkv_ref, dgv_ref, dw_ref, db_ref, dattn_ref,
             dbias_ref, carry_ref, dsink_acc, db_acc):
        i = pl.program_id(0)

        @pl.when(i == 0)
        def _():
            dbias_ref[...] = jnp.zeros_like(dbias_ref)
            carry_ref[...] = jnp.zeros_like(carry_ref)
            dsink_acc[...] = jnp.zeros_like(dsink_acc)
            dgv_ref[...] = jnp.zeros_like(dgv_ref)
            dw_ref[...] = jnp.zeros_like(dw_ref)
            db_acc[...] = jnp.zeros_like(db_acc)

        @pl.when(i < nb)
        def _():
            u, du_dz = _gelu_and_grad(zuv_ref[:, :D_GMLP], tanh_ref[:, :D_GMLP])
            vg, dvg_dz = _gelu_and_grad(zuv_ref[:, D_GMLP:], tanh_ref[:, D_GMLP:])
            for g in range(GROUPS):
                sl = slice(g * GDIM, (g + 1) * GDIM)
                vg_g = vg[:, sl]
                rg = _rms_scale(vg_g)
                vhat = vg_g * rg
                gain = gv_ref[:, sl]
                vn = (vhat * gain).astype(BF)
                w_g = _tril_bf16(w_ref, g)
                sv = _nn(w_g, vn) + b_ref[g]
                dmix_a = dmix_ref[:, sl]
                dsv = dmix_a * u[:, sl]
                dsvb = dsv.astype(BF)
                db_acc[g] += jnp.sum(dsv, axis=-1, keepdims=True)
                dw_ref[g] += _nt(dsvb, vn)
                dvn = _tn(w_g, dsvb)
                dgv_ref[:, sl] += jnp.sum(dvn * vhat, axis=0, keepdims=True)
                dvg = _rms_bwd(dvn * gain, vg_g, rg)
                dzm_ref[:, sl] = ((dmix_a * sv) * du_dz[:, sl]).astype(BF)
                dzm_ref[:, D_GMLP + g * GDIM: D_GMLP + (g + 1) * GDIM] = (dvg * dvg_dz[:, sl]).astype(BF)

            for kh in range(N_KV):
                ksl = slice(kh * HEAD_DIM, (kh + 1) * HEAD_DIM)
                vsl = slice(D_KV + kh * HEAD_DIM, D_KV + (kh + 1) * HEAD_DIM)
                k_prev, k_cur = kvp_ref[:, ksl], kvc_ref[:, ksl]
                v_prev, v_cur = kvp_ref[:, vsl], kvc_ref[:, vsl]
                q4 = _stack_heads(q_ref, kh, 0)
                p_prev, p_cur, p_sink = prob_ref[0, kh, :, :BLOCK], prob_ref[0, kh, :, BLOCK:], psink_ref[0, kh, :, 0:1]
                do4 = _stack_heads(dmix_ref, kh, D_GMLP)
                dp_prev = _nt(do4, v_prev)
                dp_cur = _nt(do4, v_cur)
                delta = jnp.sum(p_prev * dp_prev, axis=-1, keepdims=True) + jnp.sum(p_cur * dp_cur, axis=-1, keepdims=True)
                ds_prev = p_prev * (dp_prev - delta)
                ds_cur = p_cur * (dp_cur - delta)
                dsink_acc[kh] -= p_sink * delta
                dbias_ref[kh, :, :BLOCK] += ds_prev
                dbias_ref[kh, :, BLOCK:] += ds_cur
                dsb_prev = ds_prev.astype(BF)
                dsb_cur = ds_cur.astype(BF)
                dq4 = (_nn(dsb_prev, k_prev) + _nn(dsb_cur, k_cur)) * SCALE
                for g in range(Q_PER_KV):
                    first = 2 * D_GMLP + (kh * Q_PER_KV + g) * HEAD_DIM
                    dzm_ref[:, first:first + HEAD_DIM] = dq4[g * BLOCK:(g + 1) * BLOCK].astype(BF)
                dkv_ref[:, ksl] = (carry_ref[:, ksl] + _tn(dsb_prev, q4) * SCALE).astype(BF)
                dkv_ref[:, vsl] = (carry_ref[:, vsl] + _tn(p_prev.astype(BF), do4)).astype(BF)
                carry_ref[:, ksl] = _tn(dsb_cur, q4) * SCALE
                carry_ref[:, vsl] = _tn(p_cur.astype(BF), do4)

        @pl.when(i == nb)
        def _():
            dkv_ref[...] = carry_ref[...].astype(BF)
            row = lax.broadcasted_iota(jnp.int32, (BLOCK, BLOCK), 0)
            col = lax.broadcasted_iota(jnp.int32, (BLOCK, BLOCK), 1)
            for g in range(GROUPS):
                dw_ref[g] = jnp.where(col <= row, dw_ref[g], 0.0)
                db_ref[g:g + 1, :] = jnp.sum(jnp.where(col == row, db_acc[g], 0.0), axis=0, keepdims=True)
            bucket = bucket_ref[...]
            for h in range(N_Q):
                kh, rows = _head_rows(h)
                dattn_ref[REL_BUCKETS, h] = jnp.sum(dsink_acc[kh, rows, :])
                dbh = dbias_ref[kh, rows, :]
                for b in range(REL_BUCKETS):
                    dattn_ref[b, h] = jnp.sum(jnp.where(bucket == b, dbh, 0.0))

    cl = lambda i: jnp.minimum(i, nb - 1)
    const = lambda shape: pl.BlockSpec(shape, lambda i: (0,) * len(shape))
    return _call(
        body,
        name="mixer_bwd",
        after=after,
        grid=(nb + 1,),
        in_specs=_mixer_specs(nb)[:7] + [
            const((BLOCK, 2 * BLOCK)),
            pl.BlockSpec((BLOCK, D_MODEL), lambda i: (cl(i), 0)),
            pl.BlockSpec((BLOCK, 2 * D_GMLP), lambda i: (cl(i), 0)),
            pl.BlockSpec((1, N_KV, ROWS4, 2 * BLOCK), lambda i: (cl(i), 0, 0, 0)),
            pl.BlockSpec((1, N_KV, ROWS4, LANES), lambda i: (cl(i), 0, 0, 0)),
        ],
        out_specs=[
            pl.BlockSpec((BLOCK, D_MAIN), lambda i: (cl(i), 0)),
            pl.BlockSpec((BLOCK, 2 * D_KV), lambda i: (jnp.maximum(i - 1, 0), 0)),
            const((1, D_GMLP)),
            const((GROUPS, BLOCK, BLOCK)),
            const((GROUPS, BLOCK)),
            pl.BlockSpec(memory_space=pltpu.SMEM),
        ],
        out_shape=[
            jax.ShapeDtypeStruct((s, D_MAIN), BF),
            jax.ShapeDtypeStruct((s, 2 * D_KV), BF),
            jax.ShapeDtypeStruct((1, D_GMLP), F32),
            jax.ShapeDtypeStruct((GROUPS, BLOCK, BLOCK), F32),
            jax.ShapeDtypeStruct((GROUPS, BLOCK), F32),
            jax.ShapeDtypeStruct((REL_BUCKETS + 1, N_Q), F32),
        ],
        scratch_shapes=[
            pltpu.VMEM((N_KV, ROWS4, 2 * BLOCK), F32),
            pltpu.VMEM((BLOCK, 2 * D_KV), F32),
            pltpu.VMEM((N_KV, ROWS4, 1), F32),
            pltpu.VMEM((GROUPS, BLOCK, 1), F32),
        ],
        compiler_params=_params(("arbitrary",)),
    )(zuv, qkv, qkv, qkv, gv, w_sp, b_sp, bucket, dmix, *saved)


def _in_bwd(x, dh1, dzm, dkv, g1, w_in_t, after=None):
    s = x.shape[0]
    tm = min(TOK_TILE, s)

    def body(x_ref, dh1_ref, dzm_ref, dkv_ref, g_ref, w_ref, dx_ref, dg_ref):
        @pl.when(pl.program_id(0) == 0)
        def _():
            dg_ref[...] = jnp.zeros_like(dg_ref)

        dhn = _nn(dzm_ref[...], w_ref[:D_MAIN, :]) + _nn(dkv_ref[...], w_ref[D_MAIN:, :])
        xv = x_ref[...]
        r = _rms_scale(xv)
        dg_ref[...] += jnp.sum(dhn * (xv * r), axis=0, keepdims=True)
        dx_ref[...] = dh1_ref[...] + _rms_bwd(dhn * g_ref[...], xv, r)

    tile = lambda cols: pl.BlockSpec((tm, cols), lambda i: (i, 0))
    row = pl.BlockSpec((1, D_MODEL), lambda i: (0, 0))
    return _call(
        body,
        name="in_bwd",
        after=after,
        grid=(s // tm,),
        in_specs=[tile(D_MODEL), tile(D_MODEL), tile(D_MAIN), tile(2 * D_KV), row, pl.BlockSpec((D_IN, D_MODEL), lambda i: (0, 0))],
        out_specs=[tile(D_MODEL), row],
        out_shape=[jax.ShapeDtypeStruct((s, D_MODEL), F32), jax.ShapeDtypeStruct((1, D_MODEL), F32)],
        compiler_params=_params(("arbitrary",)),
    )(x, dh1, dzm, dkv, g1, w_in_t)


def _wgrad_in(dzm, dkv, hn, after=None):
    s = hn.shape[0]
    tm = 2 * D_KV
    n_main = D_MAIN // tm

    def body(dzm_ref, dkv_ref, hn_ref, o_ref):
        i = pl.program_id(0)

        @pl.when(i < n_main)
        def _():
            o_ref[...] = _tn(dzm_ref[...], hn_ref[...])

        @pl.when(i == n_main)
        def _():
            o_ref[...] = _tn(dkv_ref[...], hn_ref[...])

    return _call(
        body,
        name="wgrad_in",
        after=after,
        grid=(n_main + 1,),
        in_specs=[
            pl.BlockSpec((s, tm), lambda i: (0, jnp.minimum(i, n_main - 1))),
            pl.BlockSpec((s, tm), lambda i: (0, 0)),
            pl.BlockSpec((s, D_MODEL), lambda i: (0, 0)),
        ],
        out_specs=pl.BlockSpec((tm, D_MODEL), lambda i: (i, 0)),
        out_shape=jax.ShapeDtypeStruct((D_IN, D_MODEL), F32),
        compiler_params=_params(("arbitrary",)),
    )(dzm, dkv, hn)


def _wgrad(a, b, tm, tn, name, peer_cols=0, after=None):
    s, m = a.shape
    n = b.shape[1]

    def body(a_ref, b_ref, o_ref, at_ref):
        @pl.when(pl.program_id(1) == 0)
        def _():
            at_ref[...] = a_ref[...].astype(BF).T

        r = _nn(at_ref[...], b_ref[...])
        if peer_cols:
            for q in range(tn // peer_cols):
                o_ref[q] = r[:, q * peer_cols:(q + 1) * peer_cols]
        else:
            o_ref[...] = r

    if peer_cols:
        out_spec = pl.BlockSpec((tn // peer_cols, tm, peer_cols), lambda i, j: (j, i, 0))
        out_shape = jax.ShapeDtypeStruct((n // peer_cols, m, peer_cols), F32)
    else:
        out_spec = pl.BlockSpec((tm, tn), lambda i, j: (i, j))
        out_shape = jax.ShapeDtypeStruct((m, n), F32)
    return _call(
        body,
        name=name,
        after=after,
        grid=(m // tm, n // tn),
        in_specs=[pl.BlockSpec((s, tm), lambda i, j: (0, i)), pl.BlockSpec((s, tn), lambda i, j: (0, j))],
        out_specs=out_spec,
        out_shape=out_shape,
        scratch_shapes=[pltpu.VMEM((tm, s), BF)],
        compiler_params=_params(("arbitrary", "arbitrary")),
    )(a, b)


def _adamw_math(w, g, m, v):
    m_new = ADAM_B1 * m + (1.0 - ADAM_B1) * g
    v_new = ADAM_B2 * v + (1.0 - ADAM_B2) * jnp.square(g)
    m_hat = m_new / (1.0 - ADAM_B1 ** ADAM_STEP)
    v_hat = v_new / (1.0 - ADAM_B2 ** ADAM_STEP)
    delta = -ADAM_LR * (m_hat / (jnp.sqrt(v_hat) + ADAM_EPS) + ADAM_WD * w)
    return delta, m_new, v_new


def _final_adamw(part, recv, w, m, v, name):
    r, c = w.shape
    tr = min(r, 512)

    def body(p_ref, r_ref, w_ref, m_ref, v_ref, g_ref, d_ref, mo_ref, vo_ref):
        g = p_ref[...]
        for j in range(3):
            g = g + r_ref[j].astype(F32)
        g_ref[...] = g
        d_ref[...], mo_ref[...], vo_ref[...] = _adamw_math(w_ref[...], g, m_ref[...], v_ref[...])

    spec = pl.BlockSpec((tr, c), lambda i: (i, 0))
    return _call(
        body,
        name=name,
        grid=(r // tr,),
        in_specs=[spec, pl.BlockSpec((3, tr, c), lambda i: (0, i, 0)), spec, spec, spec],
        out_specs=[spec] * 4,
        out_shape=[jax.ShapeDtypeStruct((r, c), F32)] * 4,
        compiler_params=_params(("arbitrary",)),
    )(part, recv, w, m, v)


def _rs_sum(g, land, blocks, name):
    _, r, c = g.shape
    tr = min(r, 256)

    def body(blk_ref, g0_ref, g1_ref, g2_ref, g3_ref, l_ref, part_ref, send_ref):
        part_ref[...] = g0_ref[0] + l_ref[0]
        for j, gj_ref in enumerate((g1_ref, g2_ref, g3_ref)):
            send_ref[j] = (gj_ref[0] + l_ref[j + 1]).astype(BF)

    def pick(j):
        return pl.BlockSpec((1, tr, c), lambda i, blk: (blk[j], i, 0))

    return _call(
        body,
        name=name,
        grid_spec=pltpu.PrefetchScalarGridSpec(
            num_scalar_prefetch=1,
            grid=(r // tr,),
            in_specs=[pick(0), pick(1), pick(2), pick(3), pl.BlockSpec((4, tr, c), lambda i, blk: (0, i, 0))],
            out_specs=[pl.BlockSpec((tr, c), lambda i, blk: (i, 0)), pl.BlockSpec((3, tr, c), lambda i, blk: (0, i, 0))],
        ),
        out_shape=[jax.ShapeDtypeStruct((r, c), F32), jax.ShapeDtypeStruct((3, r, c), BF)],
        compiler_params=_params(("arbitrary",)),
    )(blocks, g, g, g, g, land)


def _adamw_small(gathered, weights, moms, vels, after):
    n_w = len(weights)

    def body(*refs):
        tail_ref, g1_ref, gv_ref, wsp_ref, bsp_ref, attn_ref = refs[:6]
        w_refs, m_refs, v_refs = refs[6:6 + n_w], refs[6 + n_w:6 + 2 * n_w], refs[6 + 2 * n_w:6 + 3 * n_w]
        outs = refs[6 + 3 * n_w:]
        loss_ref, g_refs, d_refs = outs[0], outs[1:1 + n_w], outs[1 + n_w:1 + 2 * n_w]
        mo_refs, vo_refs = outs[1 + 2 * n_w:1 + 3 * n_w], outs[1 + 3 * n_w:]

        def total(part):
            acc = part(0)
            for d in range(1, N_DEV):
                acc = acc + part(d)
            return acc

        grads = [
            total(lambda d: g1_ref[d]), total(lambda d: gv_ref[d]), total(lambda d: wsp_ref[d]), total(lambda d: bsp_ref[d]),
            total(lambda d: attn_ref[d, REL_BUCKETS:, :]), total(lambda d: attn_ref[d, :REL_BUCKETS, :]),
            total(lambda d: tail_ref[d, 0:1, :]), total(lambda d: tail_ref[d, 1:2, :])]
        loss_ref[...] = total(lambda d: tail_ref[d, 2:3, 0:1])
        for k in range(n_w):
            g_refs[k][...] = grads[k]
            d_refs[k][...], mo_refs[k][...], vo_refs[k][...] = _adamw_math(w_refs[k][...], grads[k], m_refs[k][...], v_refs[k][...])

    shapes = [jax.ShapeDtypeStruct(w.shape, F32) for w in weights]
    outs = _call(
        body,
        name="adamw_small",
        after=after,
        in_specs=[pl.BlockSpec(memory_space=pltpu.VMEM)] * (6 + 3 * n_w),
        out_shape=[jax.ShapeDtypeStruct((1, 1), F32)] + shapes * 4,
        compiler_params=_params(),
    )(*gathered, *weights, *moms, *vels)
    return outs[0], outs[1:1 + n_w], outs[1 + n_w:1 + 2 * n_w], outs[1 + 2 * n_w:1 + 3 * n_w], outs[1 + 3 * n_w:]


def _place():
    x, y, c = lax.axis_index("x"), lax.axis_index("y"), lax.axis_index("c")
    return x, y, c, [(1 - x, y), (x, 1 - y), (1 - x, 1 - y)]


def _dev_index(px, py, pc):
    return 4 * px + 2 * py + pc


def _all_gather(shards, out_dtype, name, after=None):
    k_n = len(shards)

    def body(*refs):
        ins, outs, stage = refs[:k_n], refs[k_n:2 * k_n], refs[2 * k_n:3 * k_n]
        send_sems, recv_sems, local_sems = refs[3 * k_n:]
        x, y, c, chips = _place()
        me, sibling = (x, y, c), (x, y, 1 - c)

        def copy(k, j, block, to, src=None):
            dst = outs[k].at[_dev_index(*block)]
            return pltpu.make_async_remote_copy(
                src_ref=dst if src is None else src, dst_ref=dst,
                send_sem=send_sems.at[k, j], recv_sem=recv_sems.at[k, j], device_id=to, device_id_type=MESH)

        mine, first = [], []
        for k in range(k_n):
            stage[k][...] = ins[k][...].astype(out_dtype)
            mine.append(pltpu.make_async_copy(stage[k], outs[k].at[_dev_index(*me)], local_sems.at[k]))
            mine[k].start()
            first.append(copy(k, 0, me, sibling, src=stage[k]))
            first += [copy(k, 1 + j, me, (*chip, c), src=stage[k]) for j, chip in enumerate(chips)]
        for cp in first:
            cp.start()
        passed = []
        for k in range(k_n):
            for j, chip in enumerate(chips):
                copy(k, 1 + j, (*chip, c), me).wait_recv()
                passed.append(copy(k, 4 + j, (*chip, c), sibling))
                passed[-1].start()
        for k in range(k_n):
            copy(k, 0, sibling, me).wait_recv()
            for j, chip in enumerate(chips):
                copy(k, 4 + j, (*chip, 1 - c), me).wait_recv()
        for cp in first + passed:
            cp.wait_send()
        for cp in mine:
            cp.wait()

    return _call(
        body,
        name=name,
        after=after,
        in_specs=[pl.BlockSpec(memory_space=pltpu.VMEM)] * k_n,
        out_specs=[pl.BlockSpec(memory_space=pl.ANY)] * k_n,
        out_shape=[jax.ShapeDtypeStruct((N_DEV,) + sh.shape, out_dtype) for sh in shards],
        scratch_shapes=[pltpu.VMEM(sh.shape, out_dtype) for sh in shards]
        + [pltpu.SemaphoreType.DMA((k_n, 7)), pltpu.SemaphoreType.DMA((k_n, 7)), pltpu.SemaphoreType.DMA((k_n,))],
        compiler_params=_params(),
    )(*shards)


HBM_SPEC = pl.BlockSpec(memory_space=pltpu.HBM)
SEM_SPEC = pl.BlockSpec(memory_space=pltpu.SEMAPHORE)
ANY_SPEC = pl.BlockSpec(memory_space=pl.ANY)
DATAFLOW = pltpu.SideEffectType.DATAFLOW_SIDE_EFFECTING


def _hbm(a):
    return pltpu.with_memory_space_constraint(a, pltpu.HBM)


def _prep_weights(shards):
    k_n = len(shards)

    def body(*refs):
        ins, outs, stage, sems = refs[:k_n], refs[k_n:2 * k_n], refs[2 * k_n:3 * k_n], refs[3 * k_n]
        x, y, c, _ = _place()
        copies = []
        for k in range(k_n):
            stage[k][...] = ins[k][...].astype(BF)
            copies.append(pltpu.make_async_copy(stage[k], outs[k].at[_dev_index(x, y, c)], sems.at[k]))
            copies[k].start()
        for cp in copies:
            cp.wait()

    return _call(
        body,
        name="prep_weights",
        in_specs=[pl.BlockSpec(memory_space=pltpu.VMEM)] * k_n,
        out_specs=[ANY_SPEC] * k_n,
        out_shape=[jax.ShapeDtypeStruct((N_DEV,) + sh.shape, BF) for sh in shards],
        scratch_shapes=[pltpu.VMEM(sh.shape, BF) for sh in shards] + [pltpu.SemaphoreType.DMA((k_n,))],
        compiler_params=_params(),
    )(*shards)


def _ag_start(first, rest):
    lands = list(first) + list(rest)
    k_n, k_first = len(lands), len(first)

    def body(*refs):
        land = refs[:k_n]
        sems = refs[k_n:k_n + 4]
        token = refs[-1]
        x, y, c, chips = _place()
        targets = [(x, y, 1 - c)] + [(*chip, c) for chip in chips]
        for k in range(k_n):
            blk = land[k].at[_dev_index(x, y, c)]
            send_sems, recv_sems, base = (sems[0], sems[1], k) if k < k_first else (sems[2], sems[3], k - k_first)
            for j, to in enumerate(targets):
                pltpu.make_async_remote_copy(
                    src_ref=blk, dst_ref=blk, send_sem=send_sems.at[4 * base + j], recv_sem=recv_sems.at[4 * base + j],
                    device_id=to, device_id_type=MESH).start()
        token[...] = jnp.zeros_like(token)

    k_rest = k_n - k_first
    outs = pl.pallas_call(
        body,
        name="ag_start",
        in_specs=[HBM_SPEC] * k_n,
        out_specs=(SEM_SPEC, SEM_SPEC, SEM_SPEC, SEM_SPEC, *[HBM_SPEC] * k_n, pl.BlockSpec(memory_space=pltpu.VMEM)),
        out_shape=(pltpu.SemaphoreType.DMA((4 * k_first,)), pltpu.SemaphoreType.DMA((4 * k_first,)),
                   pltpu.SemaphoreType.DMA((4 * k_rest,)), pltpu.SemaphoreType.DMA((4 * k_rest,)),
                   *[pltpu.HBM(a.shape, a.dtype) for a in lands], jax.ShapeDtypeStruct((8, LANES), F32)),
        input_output_aliases={k: 4 + k for k in range(k_n)},
        compiler_params=pltpu.CompilerParams(has_side_effects=DATAFLOW),
    )(*[_hbm(a) for a in lands])
    flying = list(outs[4:4 + k_n])
    return (outs[0], outs[1], flying[:k_first]), (outs[2], outs[3], flying[k_first:]), outs[-1]


def _ag_mid(lands, send_sems, recv_sems, after, name):
    k_n = len(lands)

    def body(*refs):
        land = refs[:k_n]
        send1, recv1 = refs[k_n], refs[k_n + 1]
        fwd_send, fwd_recv = refs[-2], refs[-1]
        x, y, c, chips = _place()
        sources = [(x, y, 1 - c)] + [(*chip, c) for chip in chips]
        for k in range(k_n):
            mine = land[k].at[_dev_index(x, y, c)]
            for j, frm in enumerate(sources):
                got = land[k].at[_dev_index(*frm)]
                cp = pltpu.make_async_remote_copy(
                    src_ref=mine, dst_ref=got, send_sem=send1.at[4 * k + j], recv_sem=recv1.at[4 * k + j], device_id=frm, device_id_type=MESH)
                cp.wait_send()
                cp.wait_recv()
                if j >= 1:
                    pltpu.make_async_remote_copy(
                        src_ref=got, dst_ref=got, send_sem=fwd_send.at[3 * k + j - 1], recv_sem=fwd_recv.at[3 * k + j - 1],
                        device_id=(x, y, 1 - c), device_id_type=MESH).start()

    outs = pl.pallas_call(
        body,
        name=name,
        in_specs=[HBM_SPEC] * k_n + [SEM_SPEC, SEM_SPEC, ANY_SPEC],
        out_specs=(*[HBM_SPEC] * k_n, SEM_SPEC, SEM_SPEC),
        out_shape=(*[pltpu.HBM(a.shape, a.dtype) for a in lands], pltpu.SemaphoreType.DMA((3 * k_n,)), pltpu.SemaphoreType.DMA((3 * k_n,))),
        input_output_aliases={k: k for k in range(k_n)},
        compiler_params=pltpu.CompilerParams(has_side_effects=DATAFLOW),
    )(*lands, send_sems, recv_sems, after)
    return list(outs[:k_n]), outs[-2], outs[-1]


def _ag_end(lands, fwd_send, fwd_recv, name):
    k_n = len(lands)

    def body(*refs):
        land = refs[:k_n]
        fsend, frecv = refs[k_n], refs[k_n + 1]
        x, y, c, chips = _place()
        for k in range(k_n):
            for j, chip in enumerate(chips):
                cp = pltpu.make_async_remote_copy(
                    src_ref=land[k].at[_dev_index(*chip, c)], dst_ref=land[k].at[_dev_index(*chip, 1 - c)],
                    send_sem=fsend.at[3 * k + j], recv_sem=frecv.at[3 * k + j], device_id=(x, y, 1 - c), device_id_type=MESH)
                cp.wait_send()
                cp.wait_recv()

    outs = pl.pallas_call(
        body,
        name=name,
        in_specs=[HBM_SPEC] * k_n + [SEM_SPEC, SEM_SPEC],
        out_specs=tuple([HBM_SPEC] * k_n),
        out_shape=tuple(pltpu.HBM(a.shape, a.dtype) for a in lands),
        input_output_aliases={k: k for k in range(k_n)},
        compiler_params=pltpu.CompilerParams(has_side_effects=DATAFLOW),
    )(*lands, fwd_send, fwd_recv)
    return list(outs)


def _chips4():
    x, y, c, others = _place()
    return x, y, c, [(x, y)] + others


def _route_sibling(j):
    x, y, c, chips = _chips4()
    return _dev_index(*chips[j], 1 - c), j, (x, y, 1 - c)


def _route_chips(j):
    x, y, c, chips = _chips4()
    return j, j, (*chips[j + 1], c)


def _xchg_copy(route, n, k, j, src, dst, send_sems, recv_sems):
    si, di, peer = route(j)
    return pltpu.make_async_remote_copy(
        src_ref=src[k].at[si], dst_ref=dst[k].at[di], send_sem=send_sems.at[n * k + j], recv_sem=recv_sems.at[n * k + j],
        device_id=peer, device_id_type=MESH)


def _xchg_start(srcs, slot_shapes, route, n, name):
    k_n = len(srcs)
    dsts = [lax.empty((n,) + tuple(sh), a.dtype) for sh, a in zip(slot_shapes, srcs)]

    def body(*refs):
        src, dst = refs[:k_n], refs[k_n:2 * k_n]
        send_sems, recv_sems, token = refs[2 * k_n], refs[2 * k_n + 1], refs[-1]
        for k in range(k_n):
            for j in range(n):
                _xchg_copy(route, n, k, j, src, dst, send_sems, recv_sems).start()
        token[...] = jnp.zeros_like(token)

    arrays = list(srcs) + dsts
    outs = pl.pallas_call(
        body,
        name=name,
        in_specs=[HBM_SPEC] * (2 * k_n),
        out_specs=(SEM_SPEC, SEM_SPEC, *[HBM_SPEC] * (2 * k_n), pl.BlockSpec(memory_space=pltpu.VMEM)),
        out_shape=(pltpu.SemaphoreType.DMA((n * k_n,)), pltpu.SemaphoreType.DMA((n * k_n,)),
                   *[pltpu.HBM(a.shape, a.dtype) for a in arrays], jax.ShapeDtypeStruct((8, LANES), F32)),
        input_output_aliases={i: 2 + i for i in range(2 * k_n)},
        compiler_params=pltpu.CompilerParams(has_side_effects=DATAFLOW),
    )(*[_hbm(a) for a in arrays])
    return outs[0], outs[1], list(outs[2:2 + k_n]), list(outs[2 + k_n:2 + 2 * k_n]), outs[-1]


def _xchg_wait(send_sems, recv_sems, srcs, dsts, route, n, after, name):
    k_n = len(srcs)

    def body(*refs):
        src, dst = refs[:k_n], refs[k_n:2 * k_n]
        send, recv = refs[2 * k_n], refs[2 * k_n + 1]
        for k in range(k_n):
            for j in range(n):
                cp = _xchg_copy(route, n, k, j, src, dst, send, recv)
                cp.wait_send()
                cp.wait_recv()

    arrays = list(srcs) + list(dsts)
    outs = pl.pallas_call(
        body,
        name=name,
        in_specs=[HBM_SPEC] * (2 * k_n) + [SEM_SPEC, SEM_SPEC, ANY_SPEC],
        out_specs=tuple([HBM_SPEC] * (2 * k_n)),
        out_shape=tuple(pltpu.HBM(a.shape, a.dtype) for a in arrays),
        input_output_aliases={i: i for i in range(2 * k_n)},
        compiler_params=pltpu.CompilerParams(has_side_effects=DATAFLOW),
    )(*arrays, send_sems, recv_sems, after)
    return list(outs[:k_n]), list(outs[k_n:])


SMALL = ("norm1_gain", "gmlp_v_gain", "w_spatial", "b_spatial", "attn_sinks", "rel_bias_table", "norm2_gain", "final_gain")
LANES = 128


def _swap_start(grads, tag):
    return _xchg_start(grads, [g.shape[1:] for g in grads], _route_sibling, 4, f"rs_{tag}_swap_start")


def _swap_sums(swap, names, after, tag):
    send1, recv1, src1, land1, _ = swap
    x, y, c, chips = _chips4()
    blocks = jnp.stack([_dev_index(*chip, c) for chip in chips]).astype(jnp.int32)
    src1, land1 = _xchg_wait(send1, recv1, src1, land1, _route_sibling, 4, after, f"rs_{tag}_swap_wait")
    return [_rs_sum(g, land, blocks, f"rs_sum_{n}") for g, land, n in zip(src1, land1, names)]


def _chips_start(sums, tag):
    sends = [ps[1] for ps in sums]
    return _xchg_start(sends, [a.shape[1:] for a in sends], _route_chips, 3, f"rs_{tag}_chips_start")


def _chips_wait(chips, after, tag):
    send2, recv2, src2, land2, _ = chips
    return _xchg_wait(send2, recv2, src2, land2, _route_chips, 3, after, f"rs_{tag}_chips_wait")[1]


def kernel(x, p, norm1_gain, w_in, gmlp_v_gain, w_spatial, b_spatial, attn_sinks, rel_bias_table, w_out, norm2_gain, w_ff1, w_ff2, w_ple_proj, w_ple_gate, final_gain, loss_target, m_norm1_gain, m_w_in, m_gmlp_v_gain, m_w_spatial, m_b_spatial, m_attn_sinks, m_rel_bias_table, m_w_out, m_norm2_gain, m_w_ff1, m_w_ff2, m_w_ple_proj, m_w_ple_gate, m_final_gain, v_norm1_gain, v_w_in, v_gmlp_v_gain, v_w_spatial, v_b_spatial, v_attn_sinks, v_rel_bias_table, v_w_out, v_norm2_gain, v_w_ff1, v_w_ff2, v_w_ple_proj, v_w_ple_gate, v_final_gain):
    args = dict(locals())
    s = x.shape[1]
    big = ("w_in", "w_out", "w_ff1", "w_ff2", "w_ple_proj", "w_ple_gate")

    x2, p2, t2 = x.reshape(s, D_MODEL), p.reshape(s, PLE_DIM), loss_target.reshape(s, D_MODEL)
    g1, gv, w_sp, b_sp, sinks, table, g2, gf = (args[n] for n in SMALL)
    bucket = jnp.asarray(_bucket_table())
    b_col = b_sp.reshape(GROUPS, BLOCK, 1)

    def shard(name):
        return args[name][0].T if name.endswith("w_in") else args[name][0]

    lands = _prep_weights([shard(n) for n in big])
    (send_in, recv_in, fly_in), (send_rest, recv_rest, fly_rest), token = _ag_start(lands[:1], lands[1:])
    mid_in, fwd_send_in, fwd_recv_in = _ag_mid(fly_in, send_in, recv_in, token, "ag_mid_w_in")
    g_in = _ag_end(mid_in, fwd_send_in, fwd_recv_in, "ag_end_w_in")[0]
    full_in = g_in.reshape(D_IN, D_MODEL)

    zuv, qkv, hn1 = _in_proj(x2, g1, full_in)
    mix, *saved = _mixer_fwd(zuv, qkv, gv, w_sp[0], b_col, sinks, table, bucket)
    mid, fwd_send, fwd_recv = _ag_mid(fly_rest, send_rest, recv_rest, mix, "ag_mid_rest")
    g_out, g_ff1, g_ff2, g_proj, g_gate = _ag_end(mid, fwd_send, fwd_recv, "ag_end_rest")
    full_out, full_ff2, full_gate = g_out.reshape(D_MODEL, D_MODEL), g_ff2.reshape(D_FF, D_MODEL), g_gate.reshape(D_MODEL, D_MODEL)
    full_proj = g_proj.transpose(1, 0, 2).reshape(PLE_DIM, D_MODEL)

    (tail_small, dh1, dh1b, dmix, hn2, a, df, h2, dh2, dgl, dpp) = _tail(
        x2, mix, p2, t2, g2, gf.reshape(1, D_MODEL), full_out, g_ff1, full_ff2, full_gate, full_proj)

    ffn_names, mid_names = ("w_ff1", "w_ff2"), ("w_out", "w_ple_proj", "w_ple_gate")
    gw_ff1 = _wgrad(hn2, df, D_MODEL, D_FF // N_DEV, "wgrad_ff1", peer_cols=D_FF // N_DEV)
    gw_ff2 = _wgrad(a, dh2, 1024, D_MODEL, "wgrad_ff2").reshape(N_DEV, D_FF // N_DEV, D_MODEL)
    ffn_swap = _swap_start([gw_ff1, gw_ff2], "ffn")
    gw_gate = _wgrad(h2, dgl, D_MODEL, 512, "wgrad_gate", after=ffn_swap[4]).reshape(N_DEV, D_MODEL // N_DEV, D_MODEL)
    gw_proj = _wgrad(p2, dpp, PLE_DIM, D_MODEL // 2, "wgrad_proj", peer_cols=D_MODEL // N_DEV, after=gw_gate)
    ffn_sums = _swap_sums(ffn_swap, ffn_names, gw_proj, "ffn")
    ffn_chips = _chips_start(ffn_sums, "ffn")
    gw_out = _wgrad(mix, dh1b, D_MODEL, 512, "wgrad_out", after=ffn_chips[4]).reshape(N_DEV, D_MODEL // N_DEV, D_MODEL)
    mid_swap = _swap_start([gw_out, gw_proj, gw_gate], "mid")

    dzm, dkv, d_gv, d_wsp, d_bsp, d_attn = _mixer_bwd(
        zuv, qkv, dmix, saved, gv, w_sp[0], b_col, bucket, after=mid_swap[4])
    mid_sums = _swap_sums(mid_swap, mid_names, dzm, "mid")
    mid_chips = _chips_start(mid_sums, "mid")
    gw_in = _wgrad_in(dzm, dkv, hn1, after=mid_chips[4]).reshape(N_DEV, D_IN // N_DEV, D_MODEL)
    in_swap = _swap_start([gw_in], "in")
    dx, d_g1 = _in_bwd(x2, dh1, dzm, dkv, g1, full_in, after=in_swap[4])
    grad_x = dx.reshape(x.shape)

    gathered = _all_gather([tail_small, d_g1, d_gv, d_wsp, d_bsp, d_attn], F32, "gather_small_grads")
    in_sums = _swap_sums(in_swap, ("w_in",), gathered[0], "in")
    in_chips = _chips_start(in_sums, "in")
    views = {"w_spatial": (GROUPS, BLOCK, BLOCK), "b_spatial": (GROUPS, BLOCK), "final_gain": (1, D_MODEL)}
    small_in = [[args[pre + n].reshape(views.get(n, args[n].shape)) for n in SMALL] for pre in ("", "m_", "v_")]
    loss, *small_out = _adamw_small(gathered, *small_in, after=in_chips[4])
    grads, deltas, new_m, new_v = {}, {}, {}, {}
    for dst, arrays in zip((grads, deltas, new_m, new_v), small_out):
        for n, arr in zip(SMALL, arrays):
            dst[n] = arr.reshape(args[n].shape)
    loss = loss[0, 0]

    after = small_out[1][0]
    for chips, sums, names, tag in ((ffn_chips, ffn_sums, ffn_names, "ffn"), (mid_chips, mid_sums, mid_names, "mid"),
                                    (in_chips, in_sums, ("w_in",), "in")):
        for n, (part, _), recv in zip(names, sums, _chips_wait(chips, after, tag)):
            g, d, mo, vo = _final_adamw(part, recv, shard(n), shard("m_" + n), shard("v_" + n), "adamw_" + n)
            for dst, arr in zip((grads, deltas, new_m, new_v), (g, d, mo, vo)):
                dst[n] = (arr.T if n == "w_in" else arr)[None]
            after = d

    order = ("norm1_gain", "w_in", "gmlp_v_gain", "w_spatial", "b_spatial", "attn_sinks", "rel_bias_table", "w_out",
             "norm2_gain", "w_ff1", "w_ff2", "w_ple_proj", "w_ple_gate", "final_gain")
    return (loss, grad_x, *[grads[n] for n in order], *[deltas[n] for n in order],
            *[new_m[n] for n in order], *[new_v[n] for n in order])
```

```python
import functools
import math

import numpy as np
import jax
import jax.numpy as jnp
from jax import lax
from jax.experimental import pallas as pl
from jax.experimental.pallas import tpu as pltpu

F32 = jnp.float32
BF = jnp.bfloat16
MESH = pl.DeviceIdType.MESH
N_DEV = 8

D_MODEL = 1024
PLE_DIM = 256
D_GMLP = 512
GROUPS = 4
GDIM = 128
BLOCK = 128
D_ATTN = 512
HEAD_DIM = 64
N_Q = 8
Q_PER_KV = 4
N_KV = N_Q // Q_PER_KV
ROWS4 = Q_PER_KV * BLOCK
D_KV = 128
D_FF = 4096
D_IN = 1792
D_MAIN = 2 * D_GMLP + D_ATTN
REL_BUCKETS = 32
EPS = 1e-6
NEG_INF = -1e30
SCALE = HEAD_DIM ** -0.5
GELU_C = math.sqrt(2.0 / math.pi)
GELU_A = 0.044715

ADAM_LR = 0.001
ADAM_B1 = 0.9
ADAM_B2 = 0.999
ADAM_EPS = 1e-08
ADAM_WD = 0.01
ADAM_STEP = 10

V7X_VMEM_LIMIT = 60000 * 1024
TOK_TILE = 256
IO_TOK_TILE = 512
WGRAD_TOK_TILE = 512


def _call(body, after=None, **kw):
    if after is None:
        return pl.pallas_call(body, **kw)
    n_in = len(kw["in_specs"])

    def ordered(*refs):
        body(*refs[:n_in], *refs[n_in + 1:])

    kw["in_specs"] = list(kw["in_specs"]) + [pl.BlockSpec(memory_space=pl.ANY)]
    fn = pl.pallas_call(ordered, **kw)
    return lambda *operands: fn(*operands, after)


def _params(sem=None):
    if sem is None:
        return pltpu.CompilerParams(vmem_limit_bytes=V7X_VMEM_LIMIT)
    return pltpu.CompilerParams(dimension_semantics=sem, vmem_limit_bytes=V7X_VMEM_LIMIT)


def _nn(a, b):
    return jnp.dot(a, b, preferred_element_type=F32)


def _nt(a, b):
    return lax.dot_general(a, b, (((1,), (1,)), ((), ())), preferred_element_type=F32)


def _tn(a, b):
    return lax.dot_general(a, b, (((0,), (0,)), ((), ())), preferred_element_type=F32)


def _gelu_tanh(x):
    return jnp.tanh(GELU_C * (x + GELU_A * (x * x * x)))


def _gelu(x, t):
    return x * (0.5 * (1.0 + t))


def _gelu_and_grad(x, t):
    cdf = 0.5 * (1.0 + t)
    return x * cdf, cdf + 0.5 * x * (1.0 - t * t) * (GELU_C * (1.0 + 3.0 * GELU_A * (x * x)))


def _rms_scale(x):
    return lax.rsqrt(jnp.mean(x * x, axis=-1, keepdims=True) + EPS)


def _rms_bwd(dxn, x, r):
    return r * dxn - x * ((r * r * r) * jnp.mean(dxn * x, axis=-1, keepdims=True))


def _bucket_table():
    a = np.arange(BLOCK)[:, None]
    j = np.arange(2 * BLOCK)[None, :]
    n = BLOCK + a - j
    valid = (n >= 0) & (n < BLOCK)
    nc = np.maximum(n, 0)
    max_exact = REL_BUCKETS // 2
    nf = np.maximum(nc, 1).astype(np.float32)
    large = max_exact + (
        np.log(nf / np.float32(max_exact)) / np.float32(math.log(BLOCK / max_exact)) * np.float32(REL_BUCKETS - max_exact)
    ).astype(np.int32)
    large = np.minimum(large, REL_BUCKETS - 1)
    bucket = np.where(nc < max_exact, nc, large)
    return np.where(valid, bucket, -1).astype(np.int32)


def _in_proj(x, g1, w_in_t):
    s = x.shape[0]
    tm = min(IO_TOK_TILE, s)

    def body(x_ref, g_ref, w_ref, zuv_ref, qkv_ref, hn_ref):
        xv = x_ref[...]
        hn = ((xv * _rms_scale(xv)) * g_ref[...]).astype(BF)
        hn_ref[...] = hn
        z = _nt(hn, w_ref[...])
        zuv_ref[...] = z[:, : 2 * D_GMLP]
        qkv_ref[...] = z[:, 2 * D_GMLP:].astype(BF)

    return _call(
        body,
        name="in_proj",
        grid=(s // tm,),
        in_specs=[
            pl.BlockSpec((tm, D_MODEL), lambda i: (i, 0)),
            pl.BlockSpec((1, D_MODEL), lambda i: (0, 0)),
            pl.BlockSpec((D_IN, D_MODEL), lambda i: (0, 0)),
        ],
        out_specs=[
            pl.BlockSpec((tm, 2 * D_GMLP), lambda i: (i, 0)),
            pl.BlockSpec((tm, D_ATTN + 2 * D_KV), lambda i: (i, 0)),
            pl.BlockSpec((tm, D_MODEL), lambda i: (i, 0)),
        ],
        out_shape=[
            jax.ShapeDtypeStruct((s, 2 * D_GMLP), F32),
            jax.ShapeDtypeStruct((s, D_ATTN + 2 * D_KV), BF),
            jax.ShapeDtypeStruct((s, D_MODEL), BF),
        ],
        compiler_params=_params(("arbitrary",)),
    )(x, g1, w_in_t)


def _head_rows(h):
    kh, g = divmod(h, Q_PER_KV)
    return kh, slice(g * BLOCK, (g + 1) * BLOCK)


def _build_bias(bias_ref, bucket_ref, table_ref):
    bucket = bucket_ref[...]
    for h in range(N_Q):
        acc = jnp.zeros((BLOCK, 2 * BLOCK), F32)
        for b in range(REL_BUCKETS):
            acc = jnp.where(bucket == b, table_ref[b, h], acc)
        kh, rows = _head_rows(h)
        bias_ref[kh, rows, :] = acc


def _window_masks(i):
    row = lax.broadcasted_iota(jnp.int32, (ROWS4, BLOCK), 0) & (BLOCK - 1)
    col = lax.broadcasted_iota(jnp.int32, (ROWS4, BLOCK), 1)
    return (col > row) & (i > 0), col <= row


def _stack_heads(ref, kh, offset):
    first = offset + kh * Q_PER_KV * HEAD_DIM
    return jnp.concatenate(
        [ref[:, first + g * HEAD_DIM: first + (g + 1) * HEAD_DIM].astype(BF) for g in range(Q_PER_KV)], axis=0)


def _stack_sinks(sink_ref, kh):
    return jnp.concatenate([jnp.full((BLOCK, 1), sink_ref[0, kh * Q_PER_KV + g], F32) for g in range(Q_PER_KV)], axis=0)


def _tril_bf16(w_ref, g):
    row = lax.broadcasted_iota(jnp.int32, (BLOCK, BLOCK), 0)
    col = lax.broadcasted_iota(jnp.int32, (BLOCK, BLOCK), 1)
    return jnp.where(col <= row, w_ref[g], 0.0).astype(BF)


def _attn_probs(q_h, k_prev, k_cur, bias_h, sink, valid_prev, valid_cur):
    l_prev = jnp.where(valid_prev, _nt(q_h, k_prev) * SCALE + bias_h[:, :BLOCK], NEG_INF)
    l_cur = jnp.where(valid_cur, _nt(q_h, k_cur) * SCALE + bias_h[:, BLOCK:], NEG_INF)
    m = jnp.maximum(jnp.maximum(jnp.max(l_prev, axis=-1, keepdims=True), jnp.max(l_cur, axis=-1, keepdims=True)), sink)
    e_prev = jnp.exp(l_prev - m)
    e_cur = jnp.exp(l_cur - m)
    e_sink = jnp.exp(sink - m)
    denom = jnp.sum(e_prev, axis=-1, keepdims=True) + jnp.sum(e_cur, axis=-1, keepdims=True) + e_sink
    return e_prev / denom, e_cur / denom, e_sink / denom


def _mixer_specs(nb):
    cl = lambda i: jnp.minimum(i, nb - 1)
    return [
        pl.BlockSpec((BLOCK, 2 * D_GMLP), lambda i: (cl(i), 0)),
        pl.BlockSpec((BLOCK, D_ATTN), lambda i: (cl(i), 0)),
        pl.BlockSpec((BLOCK, 2 * D_KV), lambda i: (cl(i), D_ATTN // (2 * D_KV))),
        pl.BlockSpec((BLOCK, 2 * D_KV), lambda i: (jnp.maximum(cl(i) - 1, 0), D_ATTN // (2 * D_KV))),
        pl.BlockSpec((1, D_GMLP), lambda i: (0, 0)),
        pl.BlockSpec((GROUPS, BLOCK, BLOCK), lambda i: (0, 0, 0)),
        pl.BlockSpec((GROUPS, BLOCK, 1), lambda i: (0, 0, 0)),
        pl.BlockSpec(memory_space=pltpu.SMEM),
        pl.BlockSpec(memory_space=pltpu.SMEM),
        pl.BlockSpec((BLOCK, 2 * BLOCK), lambda i: (0, 0)),
    ]


def _mixer_fwd(zuv, qkv, gv, w_sp, b_sp, sinks, table, bucket):
    s = zuv.shape[0]
    nb = s // BLOCK

    def body(zuv_ref, q_ref, kvc_ref, kvp_ref, gv_ref, w_ref, b_ref, sink_ref, table_ref, bucket_ref,
             mix_ref, tanh_ref, prob_ref, psink_ref, bias_ref):
        i = pl.program_id(0)

        @pl.when(i == 0)
        def _():
            _build_bias(bias_ref, bucket_ref, table_ref)

        t = _gelu_tanh(zuv_ref[...])
        tanh_ref[...] = t
        u = _gelu(zuv_ref[:, :D_GMLP], t[:, :D_GMLP])
        vg = _gelu(zuv_ref[:, D_GMLP:], t[:, D_GMLP:])
        for g in range(GROUPS):
            sl = slice(g * GDIM, (g + 1) * GDIM)
            vg_g = vg[:, sl]
            vn = ((vg_g * _rms_scale(vg_g)) * gv_ref[:, sl]).astype(BF)
            sv = _nn(_tril_bf16(w_ref, g), vn) + b_ref[g]
            mix_ref[:, sl] = (u[:, sl] * sv).astype(BF)

        valid_prev, valid_cur = _window_masks(i)
        for kh in range(N_KV):
            ksl = slice(kh * HEAD_DIM, (kh + 1) * HEAD_DIM)
            vsl = slice(D_KV + kh * HEAD_DIM, D_KV + (kh + 1) * HEAD_DIM)
            q4 = _stack_heads(q_ref, kh, 0)
            p_prev, p_cur, p_sink = _attn_probs(
                q4, kvp_ref[:, ksl], kvc_ref[:, ksl], bias_ref[kh], _stack_sinks(sink_ref, kh), valid_prev, valid_cur)
            prob_ref[0, kh, :, :BLOCK] = p_prev
            prob_ref[0, kh, :, BLOCK:] = p_cur
            psink_ref[0, kh] = jnp.broadcast_to(p_sink, (ROWS4, LANES))
            o4 = _nn(p_prev.astype(BF), kvp_ref[:, vsl]) + _nn(p_cur.astype(BF), kvc_ref[:, vsl])
            for g in range(Q_PER_KV):
                first = D_GMLP + (kh * Q_PER_KV + g) * HEAD_DIM
                mix_ref[:, first:first + HEAD_DIM] = o4[g * BLOCK:(g + 1) * BLOCK].astype(BF)

    return _call(
        body,
        name="mixer_fwd",
        grid=(nb,),
        in_specs=_mixer_specs(nb),
        out_specs=[
            pl.BlockSpec((BLOCK, D_MODEL), lambda i: (i, 0)),
            pl.BlockSpec((BLOCK, 2 * D_GMLP), lambda i: (i, 0)),
            pl.BlockSpec((1, N_KV, ROWS4, 2 * BLOCK), lambda i: (i, 0, 0, 0)),
            pl.BlockSpec((1, N_KV, ROWS4, LANES), lambda i: (i, 0, 0, 0)),
        ],
        out_shape=[
            jax.ShapeDtypeStruct((s, D_MODEL), BF),
            jax.ShapeDtypeStruct((s, 2 * D_GMLP), F32),
            jax.ShapeDtypeStruct((nb, N_KV, ROWS4, 2 * BLOCK), F32),
            jax.ShapeDtypeStruct((nb, N_KV, ROWS4, LANES), F32),
        ],
        scratch_shapes=[pltpu.VMEM((N_KV, ROWS4, 2 * BLOCK), F32)],
        compiler_params=_params(("arbitrary",)),
    )(zuv, qkv, qkv, qkv, gv, w_sp, b_sp, sinks, table, bucket)


def _tail(x, mix, p, t, g2, gf, w_out, w_ff1, w_ff2, w_gate, w_proj):
    s = x.shape[0]
    tm = min(TOK_TILE, s)
    n_ff = w_ff1.shape[0]
    fc = D_FF // n_ff

    def body(x_ref, mix_ref, p_ref, t_ref, g2_ref, gf_ref, wo_ref, w1_ref, w2_ref, wg_ref, wp_ref,
             small_ref, dh1_ref, dh1b_ref, dmix_ref, hn2_ref, a_ref, df_ref, h2_ref, dh2_ref, dgl_ref, dpp_ref, f_ref):
        i = pl.program_id(0)

        @pl.when(i == 0)
        def _():
            small_ref[...] = jnp.zeros_like(small_ref)

        h1 = x_ref[...] + _nn(mix_ref[...], wo_ref[...])
        r2 = _rms_scale(h1)
        hn2 = ((h1 * r2) * g2_ref[...]).astype(BF)
        hn2_ref[...] = hn2
        h2 = h1
        for c in range(n_ff):
            f = _nn(hn2, w1_ref[c])
            f_ref[:, c * fc:(c + 1) * fc] = f
            a = jnp.square(jnp.maximum(f, 0.0)).astype(BF)
            a_ref[:, c * fc:(c + 1) * fc] = a
            h2 = h2 + _nn(a, w2_ref[c * fc:(c + 1) * fc, :])
        h2b = h2.astype(BF)
        h2_ref[...] = h2b
        gate = jax.nn.sigmoid(_nn(h2b, wg_ref[...]))
        pp = _nn(p_ref[...].astype(BF), wp_ref[...])
        h3 = h2 + gate * pp
        rf = _rms_scale(h3)
        gf_v = gf_ref[...]
        err = (h3 * rf) * gf_v - t_ref[...]
        small_ref[2:3, :] += jnp.sum(jnp.sum(err * err, axis=-1, keepdims=True), axis=0, keepdims=True) * (0.5 / D_MODEL)

        dy = err * (1.0 / D_MODEL)
        small_ref[1:2, :] += jnp.sum(dy * (h3 * rf), axis=0, keepdims=True)
        dh3 = _rms_bwd(dy * gf_v, h3, rf)
        dpp_ref[...] = (dh3 * gate).astype(BF)
        dgl = ((dh3 * pp) * (gate * (1.0 - gate))).astype(BF)
        dgl_ref[...] = dgl
        dh2 = dh3 + _nt(dgl, wg_ref[...])
        dh2b = dh2.astype(BF)
        dh2_ref[...] = dh2b
        dhn2 = jnp.zeros((tm, D_MODEL), F32)
        for c in range(n_ff):
            da = _nt(dh2b, w2_ref[c * fc:(c + 1) * fc, :])
            df = (da * (2.0 * jnp.maximum(f_ref[:, c * fc:(c + 1) * fc], 0.0))).astype(BF)
            df_ref[:, c * fc:(c + 1) * fc] = df
            dhn2 = dhn2 + _nt(df, w1_ref[c])
        small_ref[0:1, :] += jnp.sum(dhn2 * (h1 * r2), axis=0, keepdims=True)
        dh1 = dh2 + _rms_bwd(dhn2 * g2_ref[...], h1, r2)
        dh1_ref[...] = dh1
        dh1b = dh1.astype(BF)
        dh1b_ref[...] = dh1b
        dmix_ref[...] = _nt(dh1b, wo_ref[...])

    tile = lambda cols: pl.BlockSpec((tm, cols), lambda i: (i, 0))
    whole = lambda shape: pl.BlockSpec(shape, lambda i: (0,) * len(shape), pipeline_mode=pl.Buffered(1))
    row = pl.BlockSpec((1, D_MODEL), lambda i: (0, 0))
    act = lambda cols, dt: jax.ShapeDtypeStruct((s, cols), dt)
    return _call(
        body,
        name="tail",
        grid=(s // tm,),
        in_specs=[tile(D_MODEL), tile(D_MODEL), tile(PLE_DIM), tile(D_MODEL), row, row,
                  whole(w_out.shape), whole(w_ff1.shape), whole(w_ff2.shape), whole(w_gate.shape), whole(w_proj.shape)],
        out_specs=[pl.BlockSpec((8, D_MODEL), lambda i: (0, 0)), tile(D_MODEL), tile(D_MODEL), tile(D_MODEL), tile(D_MODEL), tile(D_FF),
                   tile(D_FF), tile(D_MODEL), tile(D_MODEL), tile(D_MODEL), tile(D_MODEL)],
        out_shape=[jax.ShapeDtypeStruct((8, D_MODEL), F32),
                   act(D_MODEL, F32), act(D_MODEL, BF), act(D_MODEL, F32), act(D_MODEL, BF), act(D_FF, BF), act(D_FF, BF), act(D_MODEL, BF),
                   act(D_MODEL, BF), act(D_MODEL, BF), act(D_MODEL, BF)],
        scratch_shapes=[pltpu.VMEM((tm, D_FF), F32)],
        compiler_params=_params(("arbitrary",)),
    )(x, mix, p, t, g2, gf, w_out, w_ff1, w_ff2, w_gate, w_proj)


def _mixer_bwd(zuv, qkv, dmix, saved, gv, w_sp, b_sp, bucket, after=None):
    s = zuv.shape[0]
    nb = s // BLOCK

    def body(zuv_ref, q_ref, kvc_ref, kvp_ref, gv_ref, w_ref, b_ref, bucket_ref, dmix_ref, tanh_ref, prob_ref, psink_ref,
             dzm_ref, dkv_ref, dgv_ref, dw_ref, db_ref, dattn_ref,
             dbias_ref, carry_ref, dsink_acc, db_acc):
        i = pl.program_id(0)

        @pl.when(i == 0)
        def _():
            dbias_ref[...] = jnp.zeros_like(dbias_ref)
            carry_ref[...] = jnp.zeros_like(carry_ref)
            dsink_acc[...] = jnp.zeros_like(dsink_acc)
            dgv_ref[...] = jnp.zeros_like(dgv_ref)
            dw_ref[...] = jnp.zeros_like(dw_ref)
            db_acc[...] = jnp.zeros_like(db_acc)

        @pl.when(i < nb)
        def _():
            u, du_dz = _gelu_and_grad(zuv_ref[:, :D_GMLP], tanh_ref[:, :D_GMLP])
            vg, dvg_dz = _gelu_and_grad(zuv_ref[:, D_GMLP:], tanh_ref[:, D_GMLP:])
            for g in range(GROUPS):
                sl = slice(g * GDIM, (g + 1) * GDIM)
                vg_g = vg[:, sl]
                rg = _rms_scale(vg_g)
                vhat = vg_g * rg
                gain = gv_ref[:, sl]
                vn = (vhat * gain).astype(BF)
                w_g = _tril_bf16(w_ref, g)
                sv = _nn(w_g, vn) + b_ref[g]
                dmix_a = dmix_ref[:, sl]
                dsv = dmix_a * u[:, sl]
                dsvb = dsv.astype(BF)
                db_acc[g] += jnp.sum(dsv, axis=-1, keepdims=True)
                dw_ref[g] += _nt(dsvb, vn)
                dvn = _tn(w_g, dsvb)
                dgv_ref[:, sl] += jnp.sum(dvn * vhat, axis=0, keepdims=True)
                dvg = _rms_bwd(dvn * gain, vg_g, rg)
                dzm_ref[:, sl] = ((dmix_a * sv) * du_dz[:, sl]).astype(BF)
                dzm_ref[:, D_GMLP + g * GDIM: D_GMLP + (g + 1) * GDIM] = (dvg * dvg_dz[:, sl]).astype(BF)

            for kh in range(N_KV):
                ksl = slice(kh * HEAD_DIM, (kh + 1) * HEAD_DIM)
                vsl = slice(D_KV + kh * HEAD_DIM, D_KV + (kh + 1) * HEAD_DIM)
                k_prev, k_cur = kvp_ref[:, ksl], kvc_ref[:, ksl]
                v_prev, v_cur = kvp_ref[:, vsl], kvc_ref[:, vsl]
                q4 = _stack_heads(q_ref, kh, 0)
                p_prev, p_cur, p_sink = prob_ref[0, kh, :, :BLOCK], prob_ref[0, kh, :, BLOCK:], psink_ref[0, kh, :, 0:1]
                do4 = _stack_heads(dmix_ref, kh, D_GMLP)
                dp_prev = _nt(do4, v_prev)
                dp_cur = _nt(do4, v_cur)
                delta = jnp.sum(p_prev * dp_prev, axis=-1, keepdims=True) + jnp.sum(p_cur * dp_cur, axis=-1, keepdims=True)
                ds_prev = p_prev * (dp_prev - delta)
                ds_cur = p_cur * (dp_cur - delta)
                dsink_acc[kh] -= p_sink * delta
                dbias_ref[kh, :, :BLOCK] += ds_prev
                dbias_ref[kh, :, BLOCK:] += ds_cur
                dsb_prev = ds_prev.astype(BF)
                dsb_cur = ds_cur.astype(BF)
                dq4 = (_nn(dsb_prev, k_prev) + _nn(dsb_cur, k_cur)) * SCALE
                for g in range(Q_PER_KV):
                    first = 2 * D_GMLP + (kh * Q_PER_KV + g) * HEAD_DIM
                    dzm_ref[:, first:first + HEAD_DIM] = dq4[g * BLOCK:(g + 1) * BLOCK].astype(BF)
                dkv_ref[:, ksl] = (carry_ref[:, ksl] + _tn(dsb_prev, q4) * SCALE).astype(BF)
                dkv_ref[:, vsl] = (carry_ref[:, vsl] + _tn(p_prev.astype(BF), do4)).astype(BF)
                carry_ref[:, ksl] = _tn(dsb_cur, q4) * SCALE
                carry_ref[:, vsl] = _tn(p_cur.astype(BF), do4)

        @pl.when(i == nb)
        def _():
            dkv_ref[...] = carry_ref[...].astype(BF)
            row = lax.broadcasted_iota(jnp.int32, (BLOCK, BLOCK), 0)
            col = lax.broadcasted_iota(jnp.int32, (BLOCK, BLOCK), 1)
            for g in range(GROUPS):
                dw_ref[g] = jnp.where(col <= row, dw_ref[g], 0.0)
                db_ref[g:g + 1, :] = jnp.sum(jnp.where(col == row, db_acc[g], 0.0), axis=0, keepdims=True)
            bucket = bucket_ref[...]
            for h in range(N_Q):
                kh, rows = _head_rows(h)
                dattn_ref[REL_BUCKETS, h] = jnp.sum(dsink_acc[kh, rows, :])
                dbh = dbias_ref[kh, rows, :]
                for b in range(REL_BUCKETS):
                    dattn_ref[b, h] = jnp.sum(jnp.where(bucket == b, dbh, 0.0))

    cl = lambda i: jnp.minimum(i, nb - 1)
    const = lambda shape: pl.BlockSpec(shape, lambda i: (0,) * len(shape))
    return _call(
        body,
        name="mixer_bwd",
        after=after,
        grid=(nb + 1,),
        in_specs=_mixer_specs(nb)[:7] + [
            const((BLOCK, 2 * BLOCK)),
            pl.BlockSpec((BLOCK, D_MODEL), lambda i: (cl(i), 0)),
            pl.BlockSpec((BLOCK, 2 * D_GMLP), lambda i: (cl(i), 0)),
            pl.BlockSpec((1, N_KV, ROWS4, 2 * BLOCK), lambda i: (cl(i), 0, 0, 0)),
            pl.BlockSpec((1, N_KV, ROWS4, LANES), lambda i: (cl(i), 0, 0, 0)),
        ],
        out_specs=[
            pl.BlockSpec((BLOCK, D_MAIN), lambda i: (cl(i), 0)),
            pl.BlockSpec((BLOCK, 2 * D_KV), lambda i: (jnp.maximum(i - 1, 0), 0)),
            const((1, D_GMLP)),
            const((GROUPS, BLOCK, BLOCK)),
            const((GROUPS, BLOCK)),
            pl.BlockSpec(memory_space=pltpu.SMEM),
        ],
        out_shape=[
            jax.ShapeDtypeStruct((s, D_MAIN), BF),
            jax.ShapeDtypeStruct((s, 2 * D_KV), BF),
            jax.ShapeDtypeStruct((1, D_GMLP), F32),
            jax.ShapeDtypeStruct((GROUPS, BLOCK, BLOCK), F32),
            jax.ShapeDtypeStruct((GROUPS, BLOCK), F32),
            jax.ShapeDtypeStruct((REL_BUCKETS + 1, N_Q), F32),
        ],
        scratch_shapes=[
            pltpu.VMEM((N_KV, ROWS4, 2 * BLOCK), F32),
            pltpu.VMEM((BLOCK, 2 * D_KV), F32),
            pltpu.VMEM((N_KV, ROWS4, 1), F32),
            pltpu.VMEM((GROUPS, BLOCK, 1), F32),
        ],
        compiler_params=_params(("arbitrary",)),
    )(zuv, qkv, qkv, qkv, gv, w_sp, b_sp, bucket, dmix, *saved)


def _in_bwd(x, dh1, dzm, dkv, g1, w_in_t, after=None):
    s = x.shape[0]
    tm = min(IO_TOK_TILE, s)

    def body(x_ref, dh1_ref, dzm_ref, dkv_ref, g_ref, w_ref, dx_ref, dg_ref):
        @pl.when(pl.program_id(0) == 0)
        def _():
            dg_ref[...] = jnp.zeros_like(dg_ref)

        dhn = _nn(dzm_ref[...], w_ref[:D_MAIN, :]) + _nn(dkv_ref[...], w_ref[D_MAIN:, :])
        xv = x_ref[...]
        r = _rms_scale(xv)
        dg_ref[...] += jnp.sum(dhn * (xv * r), axis=0, keepdims=True)
        dx_ref[...] = dh1_ref[...] + _rms_bwd(dhn * g_ref[...], xv, r)

    tile = lambda cols: pl.BlockSpec((tm, cols), lambda i: (i, 0))
    row = pl.BlockSpec((1, D_MODEL), lambda i: (0, 0))
    return _call(
        body,
        name="in_bwd",
        after=after,
        grid=(s // tm,),
        in_specs=[tile(D_MODEL), tile(D_MODEL), tile(D_MAIN), tile(2 * D_KV), row, pl.BlockSpec((D_IN, D_MODEL), lambda i: (0, 0))],
        out_specs=[tile(D_MODEL), row],
        out_shape=[jax.ShapeDtypeStruct((s, D_MODEL), F32), jax.ShapeDtypeStruct((1, D_MODEL), F32)],
        compiler_params=_params(("arbitrary",)),
    )(x, dh1, dzm, dkv, g1, w_in_t)


def _wgrad_in(dzm, dkv, hn, after=None):
    s = hn.shape[0]
    tk = min(WGRAD_TOK_TILE, s)

    def body(dzm_ref, dkv_ref, hn_ref, o_ref):
        main, kv = _tn(dzm_ref[...], hn_ref[...]), _tn(dkv_ref[...], hn_ref[...])

        @pl.when(pl.program_id(0) == 0)
        def _():
            o_ref[:D_MAIN, :] = main
            o_ref[D_MAIN:, :] = kv

        @pl.when(pl.program_id(0) > 0)
        def _():
            o_ref[:D_MAIN, :] += main
            o_ref[D_MAIN:, :] += kv

    tile = lambda cols: pl.BlockSpec((tk, cols), lambda k: (k, 0))
    return _call(
        body,
        name="wgrad_in",
        after=after,
        grid=(s // tk,),
        in_specs=[tile(D_MAIN), tile(2 * D_KV), tile(D_MODEL)],
        out_specs=pl.BlockSpec((D_IN, D_MODEL), lambda k: (0, 0)),
        out_shape=jax.ShapeDtypeStruct((D_IN, D_MODEL), F32),
        compiler_params=_params(("arbitrary",)),
    )(dzm, dkv, hn)


def _wgrad_acc(a, b, name, after=None):
    s, m = a.shape
    n = b.shape[1]
    tk = min(WGRAD_TOK_TILE, s)

    def body(a_ref, b_ref, o_ref):
        r = _tn(a_ref[...], b_ref[...])

        @pl.when(pl.program_id(0) == 0)
        def _():
            o_ref[...] = r

        @pl.when(pl.program_id(0) > 0)
        def _():
            o_ref[...] += r

    return _call(
        body,
        name=name,
        after=after,
        grid=(s // tk,),
        in_specs=[pl.BlockSpec((tk, m), lambda k: (k, 0)), pl.BlockSpec((tk, n), lambda k: (k, 0))],
        out_specs=pl.BlockSpec((m, n), lambda k: (0, 0)),
        out_shape=jax.ShapeDtypeStruct((m, n), F32),
        compiler_params=_params(("arbitrary",)),
    )(a, b)


def _wgrad(a, b, tm, tn, name, peer_cols=0, after=None):
    s, m = a.shape
    n = b.shape[1]

    def body(a_ref, b_ref, o_ref, at_ref):
        @pl.when(pl.program_id(1) == 0)
        def _():
            at_ref[...] = a_ref[...].astype(BF).T

        r = _nn(at_ref[...], b_ref[...])
        if peer_cols:
            for q in range(tn // peer_cols):
                o_ref[q] = r[:, q * peer_cols:(q + 1) * peer_cols]
        else:
            o_ref[...] = r

    if peer_cols:
        out_spec = pl.BlockSpec((tn // peer_cols, tm, peer_cols), lambda i, j: (j, i, 0))
        out_shape = jax.ShapeDtypeStruct((n // peer_cols, m, peer_cols), F32)
    else:
        out_spec = pl.BlockSpec((tm, tn), lambda i, j: (i, j))
        out_shape = jax.ShapeDtypeStruct((m, n), F32)
    return _call(
        body,
        name=name,
        after=after,
        grid=(m // tm, n // tn),
        in_specs=[pl.BlockSpec((s, tm), lambda i, j: (0, i)), pl.BlockSpec((s, tn), lambda i, j: (0, j))],
        out_specs=out_spec,
        out_shape=out_shape,
        scratch_shapes=[pltpu.VMEM((tm, s), BF)],
        compiler_params=_params(("arbitrary", "arbitrary")),
    )(a, b)


def _adamw_math(w, g, m, v):
    m_new = ADAM_B1 * m + (1.0 - ADAM_B1) * g
    v_new = ADAM_B2 * v + (1.0 - ADAM_B2) * jnp.square(g)
    m_hat = m_new / (1.0 - ADAM_B1 ** ADAM_STEP)
    v_hat = v_new / (1.0 - ADAM_B2 ** ADAM_STEP)
    delta = -ADAM_LR * (m_hat / (jnp.sqrt(v_hat) + ADAM_EPS) + ADAM_WD * w)
    return delta, m_new, v_new


def _final_adamw(part, recv, w, m, v, name):
    r, c = w.shape
    tr = min(r, 512)

    def body(p_ref, r_ref, w_ref, m_ref, v_ref, g_ref, d_ref, mo_ref, vo_ref):
        g = p_ref[...]
        for j in range(3):
            g = g + r_ref[j].astype(F32)
        g_ref[...] = g
        d_ref[...], mo_ref[...], vo_ref[...] = _adamw_math(w_ref[...], g, m_ref[...], v_ref[...])

    spec = pl.BlockSpec((tr, c), lambda i: (i, 0))
    return _call(
        body,
        name=name,
        grid=(r // tr,),
        in_specs=[spec, pl.BlockSpec((3, tr, c), lambda i: (0, i, 0)), spec, spec, spec],
        out_specs=[spec] * 4,
        out_shape=[jax.ShapeDtypeStruct((r, c), F32)] * 4,
        compiler_params=_params(("arbitrary",)),
    )(part, recv, w, m, v)


def _rs_sum(g, land, blocks, name):
    _, r, c = g.shape
    tr = min(r, 256)

    def body(blk_ref, g0_ref, g1_ref, g2_ref, g3_ref, l_ref, part_ref, send_ref):
        part_ref[...] = g0_ref[0] + l_ref[0]
        for j, gj_ref in enumerate((g1_ref, g2_ref, g3_ref)):
            send_ref[j] = (gj_ref[0] + l_ref[j + 1]).astype(BF)

    def pick(j):
        return pl.BlockSpec((1, tr, c), lambda i, blk: (blk[j], i, 0))

    return _call(
        body,
        name=name,
        grid_spec=pltpu.PrefetchScalarGridSpec(
            num_scalar_prefetch=1,
            grid=(r // tr,),
            in_specs=[pick(0), pick(1), pick(2), pick(3), pl.BlockSpec((4, tr, c), lambda i, blk: (0, i, 0))],
            out_specs=[pl.BlockSpec((tr, c), lambda i, blk: (i, 0)), pl.BlockSpec((3, tr, c), lambda i, blk: (0, i, 0))],
        ),
        out_shape=[jax.ShapeDtypeStruct((r, c), F32), jax.ShapeDtypeStruct((3, r, c), BF)],
        compiler_params=_params(("arbitrary",)),
    )(blocks, g, g, g, g, land)


def _adamw_small(gathered, weights, moms, vels, after):
    n_w = len(weights)

    def body(*refs):
        tail_ref, g1_ref, gv_ref, wsp_ref, bsp_ref, attn_ref = refs[:6]
        w_refs, m_refs, v_refs = refs[6:6 + n_w], refs[6 + n_w:6 + 2 * n_w], refs[6 + 2 * n_w:6 + 3 * n_w]
        outs = refs[6 + 3 * n_w:]
        loss_ref, g_refs, d_refs = outs[0], outs[1:1 + n_w], outs[1 + n_w:1 + 2 * n_w]
        mo_refs, vo_refs = outs[1 + 2 * n_w:1 + 3 * n_w], outs[1 + 3 * n_w:]

        def total(part):
            acc = part(0)
            for d in range(1, N_DEV):
                acc = acc + part(d)
            return acc

        grads = [
            total(lambda d: g1_ref[d]), total(lambda d: gv_ref[d]), total(lambda d: wsp_ref[d]), total(lambda d: bsp_ref[d]),
            total(lambda d: attn_ref[d, REL_BUCKETS:, :]), total(lambda d: attn_ref[d, :REL_BUCKETS, :]),
            total(lambda d: tail_ref[d, 0:1, :]), total(lambda d: tail_ref[d, 1:2, :])]
        loss_ref[...] = total(lambda d: tail_ref[d, 2:3, 0:1])
        for k in range(n_w):
            g_refs[k][...] = grads[k]
            d_refs[k][...], mo_refs[k][...], vo_refs[k][...] = _adamw_math(w_refs[k][...], grads[k], m_refs[k][...], v_refs[k][...])

    shapes = [jax.ShapeDtypeStruct(w.shape, F32) for w in weights]
    outs = _call(
        body,
        name="adamw_small",
        after=after,
        in_specs=[pl.BlockSpec(memory_space=pltpu.VMEM)] * (6 + 3 * n_w),
        out_shape=[jax.ShapeDtypeStruct((1, 1), F32)] + shapes * 4,
        compiler_params=_params(),
    )(*gathered, *weights, *moms, *vels)
    return outs[0], outs[1:1 + n_w], outs[1 + n_w:1 + 2 * n_w], outs[1 + 2 * n_w:1 + 3 * n_w], outs[1 + 3 * n_w:]


def _place():
    x, y, c = lax.axis_index("x"), lax.axis_index("y"), lax.axis_index("c")
    return x, y, c, [(1 - x, y), (x, 1 - y), (1 - x, 1 - y)]


def _dev_index(px, py, pc):
    return 4 * px + 2 * py + pc


def _all_gather(shards, out_dtype, name, after=None):
    k_n = len(shards)

    def body(*refs):
        ins, outs, stage = refs[:k_n], refs[k_n:2 * k_n], refs[2 * k_n:3 * k_n]
        send_sems, recv_sems, local_sems = refs[3 * k_n:]
        x, y, c, chips = _place()
        me, sibling = (x, y, c), (x, y, 1 - c)

        def copy(k, j, block, to, src=None):
            dst = outs[k].at[_dev_index(*block)]
            return pltpu.make_async_remote_copy(
                src_ref=dst if src is None else src, dst_ref=dst,
                send_sem=send_sems.at[k, j], recv_sem=recv_sems.at[k, j], device_id=to, device_id_type=MESH)

        mine, first = [], []
        for k in range(k_n):
            stage[k][...] = ins[k][...].astype(out_dtype)
            mine.append(pltpu.make_async_copy(stage[k], outs[k].at[_dev_index(*me)], local_sems.at[k]))
            mine[k].start()
            first.append(copy(k, 0, me, sibling, src=stage[k]))
            first += [copy(k, 1 + j, me, (*chip, c), src=stage[k]) for j, chip in enumerate(chips)]
        for cp in first:
            cp.start()
        passed = []
        for k in range(k_n):
            for j, chip in enumerate(chips):
                copy(k, 1 + j, (*chip, c), me).wait_recv()
                passed.append(copy(k, 4 + j, (*chip, c), sibling))
                passed[-1].start()
        for k in range(k_n):
            copy(k, 0, sibling, me).wait_recv()
            for j, chip in enumerate(chips):
                copy(k, 4 + j, (*chip, 1 - c), me).wait_recv()
        for cp in first + passed:
            cp.wait_send()
        for cp in mine:
            cp.wait()

    return _call(
        body,
        name=name,
        after=after,
        in_specs=[pl.BlockSpec(memory_space=pltpu.VMEM)] * k_n,
        out_specs=[pl.BlockSpec(memory_space=pl.ANY)] * k_n,
        out_shape=[jax.ShapeDtypeStruct((N_DEV,) + sh.shape, out_dtype) for sh in shards],
        scratch_shapes=[pltpu.VMEM(sh.shape, out_dtype) for sh in shards]
        + [pltpu.SemaphoreType.DMA((k_n, 7)), pltpu.SemaphoreType.DMA((k_n, 7)), pltpu.SemaphoreType.DMA((k_n,))],
        compiler_params=_params(),
    )(*shards)


HBM_SPEC = pl.BlockSpec(memory_space=pltpu.HBM)
SEM_SPEC = pl.BlockSpec(memory_space=pltpu.SEMAPHORE)
ANY_SPEC = pl.BlockSpec(memory_space=pl.ANY)
DATAFLOW = pltpu.SideEffectType.DATAFLOW_SIDE_EFFECTING


def _hbm(a):
    return pltpu.with_memory_space_constraint(a, pltpu.HBM)


def _prep_weights(shards):
    k_n = len(shards)

    def body(*refs):
        ins, outs, stage, sems = refs[:k_n], refs[k_n:2 * k_n], refs[2 * k_n:3 * k_n], refs[3 * k_n]
        x, y, c, _ = _place()
        copies = []
        for k in range(k_n):
            stage[k][...] = ins[k][...].astype(BF)
            copies.append(pltpu.make_async_copy(stage[k], outs[k].at[_dev_index(x, y, c)], sems.at[k]))
            copies[k].start()
        for cp in copies:
            cp.wait()

    return _call(
        body,
        name="prep_weights",
        in_specs=[pl.BlockSpec(memory_space=pltpu.VMEM)] * k_n,
        out_specs=[ANY_SPEC] * k_n,
        out_shape=[jax.ShapeDtypeStruct((N_DEV,) + sh.shape, BF) for sh in shards],
        scratch_shapes=[pltpu.VMEM(sh.shape, BF) for sh in shards] + [pltpu.SemaphoreType.DMA((k_n,))],
        compiler_params=_params(),
    )(*shards)


def _ag_start(first, rest):
    lands = list(first) + list(rest)
    k_n, k_first = len(lands), len(first)

    def body(*refs):
        land = refs[:k_n]
        sems = refs[k_n:k_n + 4]
        token = refs[-1]
        x, y, c, chips = _place()
        targets = [(x, y, 1 - c)] + [(*chip, c) for chip in chips]
        for k in range(k_n):
            blk = land[k].at[_dev_index(x, y, c)]
            send_sems, recv_sems, base = (sems[0], sems[1], k) if k < k_first else (sems[2], sems[3], k - k_first)
            for j, to in enumerate(targets):
                pltpu.make_async_remote_copy(
                    src_ref=blk, dst_ref=blk, send_sem=send_sems.at[4 * base + j], recv_sem=recv_sems.at[4 * base + j],
                    device_id=to, device_id_type=MESH).start()
        token[...] = jnp.zeros_like(token)

    k_rest = k_n - k_first
    outs = pl.pallas_call(
        body,
        name="ag_start",
        in_specs=[HBM_SPEC] * k_n,
        out_specs=(SEM_SPEC, SEM_SPEC, SEM_SPEC, SEM_SPEC, *[HBM_SPEC] * k_n, pl.BlockSpec(memory_space=pltpu.VMEM)),
        out_shape=(pltpu.SemaphoreType.DMA((4 * k_first,)), pltpu.SemaphoreType.DMA((4 * k_first,)),
                   pltpu.SemaphoreType.DMA((4 * k_rest,)), pltpu.SemaphoreType.DMA((4 * k_rest,)),
                   *[pltpu.HBM(a.shape, a.dtype) for a in lands], jax.ShapeDtypeStruct((8, LANES), F32)),
        input_output_aliases={k: 4 + k for k in range(k_n)},
        compiler_params=pltpu.CompilerParams(has_side_effects=DATAFLOW),
    )(*[_hbm(a) for a in lands])
    flying = list(outs[4:4 + k_n])
    return (outs[0], outs[1], flying[:k_first]), (outs[2], outs[3], flying[k_first:]), outs[-1]


def _ag_mid(lands, send_sems, recv_sems, after, name):
    k_n = len(lands)

    def body(*refs):
        land = refs[:k_n]
        send1, recv1 = refs[k_n], refs[k_n + 1]
        fwd_send, fwd_recv = refs[-2], refs[-1]
        x, y, c, chips = _place()
        sources = [(x, y, 1 - c)] + [(*chip, c) for chip in chips]
        for k in range(k_n):
            mine = land[k].at[_dev_index(x, y, c)]
            for j, frm in enumerate(sources):
                got = land[k].at[_dev_index(*frm)]
                cp = pltpu.make_async_remote_copy(
                    src_ref=mine, dst_ref=got, send_sem=send1.at[4 * k + j], recv_sem=recv1.at[4 * k + j], device_id=frm, device_id_type=MESH)
                cp.wait_send()
                cp.wait_recv()
                if j >= 1:
                    pltpu.make_async_remote_copy(
                        src_ref=got, dst_ref=got, send_sem=fwd_send.at[3 * k + j - 1], recv_sem=fwd_recv.at[3 * k + j - 1],
                        device_id=(x, y, 1 - c), device_id_type=MESH).start()

    outs = pl.pallas_call(
        body,
        name=name,
        in_specs=[HBM_SPEC] * k_n + [SEM_SPEC, SEM_SPEC, ANY_SPEC],
        out_specs=(*[HBM_SPEC] * k_n, SEM_SPEC, SEM_SPEC),
        out_shape=(*[pltpu.HBM(a.shape, a.dtype) for a in lands], pltpu.SemaphoreType.DMA((3 * k_n,)), pltpu.SemaphoreType.DMA((3 * k_n,))),
        input_output_aliases={k: k for k in range(k_n)},
        compiler_params=pltpu.CompilerParams(has_side_effects=DATAFLOW),
    )(*lands, send_sems, recv_sems, after)
    return list(outs[:k_n]), outs[-2], outs[-1]


def _ag_end(lands, fwd_send, fwd_recv, name):
    k_n = len(lands)

    def body(*refs):
        land = refs[:k_n]
        fsend, frecv = refs[k_n], refs[k_n + 1]
        x, y, c, chips = _place()
        for k in range(k_n):
            for j, chip in enumerate(chips):
                cp = pltpu.make_async_remote_copy(
                    src_ref=land[k].at[_dev_index(*chip, c)], dst_ref=land[k].at[_dev_index(*chip, 1 - c)],
                    send_sem=fsend.at[3 * k + j], recv_sem=frecv.at[3 * k + j], device_id=(x, y, 1 - c), device_id_type=MESH)
                cp.wait_send()
                cp.wait_recv()

    outs = pl.pallas_call(
        body,
        name=name,
        in_specs=[HBM_SPEC] * k_n + [SEM_SPEC, SEM_SPEC],
        out_specs=tuple([HBM_SPEC] * k_n),
        out_shape=tuple(pltpu.HBM(a.shape, a.dtype) for a in lands),
        input_output_aliases={k: k for k in range(k_n)},
        compiler_params=pltpu.CompilerParams(has_side_effects=DATAFLOW),
    )(*lands, fwd_send, fwd_recv)
    return list(outs)


def _chips4():
    x, y, c, others = _place()
    return x, y, c, [(x, y)] + others


def _route_sibling(j):
    x, y, c, chips = _chips4()
    return _dev_index(*chips[j], 1 - c), j, (x, y, 1 - c)


def _route_chips(j):
    x, y, c, chips = _chips4()
    return j, j, (*chips[j + 1], c)


def _xchg_copy(route, n, k, j, src, dst, send_sems, recv_sems):
    si, di, peer = route(j)
    return pltpu.make_async_remote_copy(
        src_ref=src[k].at[si], dst_ref=dst[k].at[di], send_sem=send_sems.at[n * k + j], recv_sem=recv_sems.at[n * k + j],
        device_id=peer, device_id_type=MESH)


def _xchg_start(srcs, slot_shapes, route, n, name):
    k_n = len(srcs)
    dsts = [lax.empty((n,) + tuple(sh), a.dtype) for sh, a in zip(slot_shapes, srcs)]

    def body(*refs):
        src, dst = refs[:k_n], refs[k_n:2 * k_n]
        send_sems, recv_sems, token = refs[2 * k_n], refs[2 * k_n + 1], refs[-1]
        for k in range(k_n):
            for j in range(n):
                _xchg_copy(route, n, k, j, src, dst, send_sems, recv_sems).start()
        token[...] = jnp.zeros_like(token)

    arrays = list(srcs) + dsts
    outs = pl.pallas_call(
        body,
        name=name,
        in_specs=[HBM_SPEC] * (2 * k_n),
        out_specs=(SEM_SPEC, SEM_SPEC, *[HBM_SPEC] * (2 * k_n), pl.BlockSpec(memory_space=pltpu.VMEM)),
        out_shape=(pltpu.SemaphoreType.DMA((n * k_n,)), pltpu.SemaphoreType.DMA((n * k_n,)),
                   *[pltpu.HBM(a.shape, a.dtype) for a in arrays], jax.ShapeDtypeStruct((8, LANES), F32)),
        input_output_aliases={i: 2 + i for i in range(2 * k_n)},
        compiler_params=pltpu.CompilerParams(has_side_effects=DATAFLOW),
    )(*[_hbm(a) for a in arrays])
    return outs[0], outs[1], list(outs[2:2 + k_n]), list(outs[2 + k_n:2 + 2 * k_n]), outs[-1]


def _xchg_wait(send_sems, recv_sems, srcs, dsts, route, n, after, name):
    k_n = len(srcs)

    def body(*refs):
        src, dst = refs[:k_n], refs[k_n:2 * k_n]
        send, recv = refs[2 * k_n], refs[2 * k_n + 1]
        for k in range(k_n):
            for j in range(n):
                cp = _xchg_copy(route, n, k, j, src, dst, send, recv)
                cp.wait_send()
                cp.wait_recv()

    arrays = list(srcs) + list(dsts)
    outs = pl.pallas_call(
        body,
        name=name,
        in_specs=[HBM_SPEC] * (2 * k_n) + [SEM_SPEC, SEM_SPEC, ANY_SPEC],
        out_specs=tuple([HBM_SPEC] * (2 * k_n)),
        out_shape=tuple(pltpu.HBM(a.shape, a.dtype) for a in arrays),
        input_output_aliases={i: i for i in range(2 * k_n)},
        compiler_params=pltpu.CompilerParams(has_side_effects=DATAFLOW),
    )(*arrays, send_sems, recv_sems, after)
    return list(outs[:k_n]), list(outs[k_n:])


SMALL = ("norm1_gain", "gmlp_v_gain", "w_spatial", "b_spatial", "attn_sinks", "rel_bias_table", "norm2_gain", "final_gain")
LANES = 128


def _swap_start(grads, tag):
    return _xchg_start(grads, [g.shape[1:] for g in grads], _route_sibling, 4, f"rs_{tag}_swap_start")


def _swap_sums(swap, names, after, tag):
    send1, recv1, src1, land1, _ = swap
    x, y, c, chips = _chips4()
    blocks = jnp.stack([_dev_index(*chip, c) for chip in chips]).astype(jnp.int32)
    src1, land1 = _xchg_wait(send1, recv1, src1, land1, _route_sibling, 4, after, f"rs_{tag}_swap_wait")
    return [_rs_sum(g, land, blocks, f"rs_sum_{n}") for g, land, n in zip(src1, land1, names)]


def _chips_start(sums, tag):
    sends = [ps[1] for ps in sums]
    return _xchg_start(sends, [a.shape[1:] for a in sends], _route_chips, 3, f"rs_{tag}_chips_start")


def _chips_wait(chips, after, tag):
    send2, recv2, src2, land2, _ = chips
    return _xchg_wait(send2, recv2, src2, land2, _route_chips, 3, after, f"rs_{tag}_chips_wait")[1]


def kernel(x, p, norm1_gain, w_in, gmlp_v_gain, w_spatial, b_spatial, attn_sinks, rel_bias_table, w_out, norm2_gain, w_ff1, w_ff2, w_ple_proj, w_ple_gate, final_gain, loss_target, m_norm1_gain, m_w_in, m_gmlp_v_gain, m_w_spatial, m_b_spatial, m_attn_sinks, m_rel_bias_table, m_w_out, m_norm2_gain, m_w_ff1, m_w_ff2, m_w_ple_proj, m_w_ple_gate, m_final_gain, v_norm1_gain, v_w_in, v_gmlp_v_gain, v_w_spatial, v_b_spatial, v_attn_sinks, v_rel_bias_table, v_w_out, v_norm2_gain, v_w_ff1, v_w_ff2, v_w_ple_proj, v_w_ple_gate, v_final_gain):
    args = dict(locals())
    s = x.shape[1]
    big = ("w_in", "w_out", "w_ff1", "w_ff2", "w_ple_proj", "w_ple_gate")

    x2, p2, t2 = x.reshape(s, D_MODEL), p.reshape(s, PLE_DIM), loss_target.reshape(s, D_MODEL)
    g1, gv, w_sp, b_sp, sinks, table, g2, gf = (args[n] for n in SMALL)
    bucket = jnp.asarray(_bucket_table())
    b_col = b_sp.reshape(GROUPS, BLOCK, 1)

    def shard(name):
        return args[name][0].T if name.endswith("w_in") else args[name][0]

    lands = _prep_weights([shard(n) for n in big])
    (send_in, recv_in, fly_in), (send_rest, recv_rest, fly_rest), token = _ag_start(lands[:1], lands[1:])
    mid_in, fwd_send_in, fwd_recv_in = _ag_mid(fly_in, send_in, recv_in, token, "ag_mid_w_in")
    g_in = _ag_end(mid_in, fwd_send_in, fwd_recv_in, "ag_end_w_in")[0]
    full_in = g_in.reshape(D_IN, D_MODEL)

    zuv, qkv, hn1 = _in_proj(x2, g1, full_in)
    mix, *saved = _mixer_fwd(zuv, qkv, gv, w_sp[0], b_col, sinks, table, bucket)
    mid, fwd_send, fwd_recv = _ag_mid(fly_rest, send_rest, recv_rest, mix, "ag_mid_rest")
    g_out, g_ff1, g_ff2, g_proj, g_gate = _ag_end(mid, fwd_send, fwd_recv, "ag_end_rest")
    full_out, full_ff2, full_gate = g_out.reshape(D_MODEL, D_MODEL), g_ff2.reshape(D_FF, D_MODEL), g_gate.reshape(D_MODEL, D_MODEL)
    full_proj = g_proj.transpose(1, 0, 2).reshape(PLE_DIM, D_MODEL)

    (tail_small, dh1, dh1b, dmix, hn2, a, df, h2, dh2, dgl, dpp) = _tail(
        x2, mix, p2, t2, g2, gf.reshape(1, D_MODEL), full_out, g_ff1, full_ff2, full_gate, full_proj)

    ffn_names, mid_names = ("w_ff1", "w_ff2"), ("w_out", "w_ple_proj", "w_ple_gate")
    gw_ff1 = _wgrad(hn2, df, D_MODEL, D_FF // N_DEV, "wgrad_ff1", peer_cols=D_FF // N_DEV)
    gw_ff2 = _wgrad(a, dh2, 1024, D_MODEL, "wgrad_ff2").reshape(N_DEV, D_FF // N_DEV, D_MODEL)
    ffn_swap = _swap_start([gw_ff1, gw_ff2], "ffn")
    gw_gate = _wgrad_acc(h2, dgl, "wgrad_gate", after=ffn_swap[4]).reshape(N_DEV, D_MODEL // N_DEV, D_MODEL)
    gw_proj = _wgrad(p2, dpp, PLE_DIM, D_MODEL // 2, "wgrad_proj", peer_cols=D_MODEL // N_DEV, after=gw_gate)
    ffn_sums = _swap_sums(ffn_swap, ffn_names, gw_proj, "ffn")
    ffn_chips = _chips_start(ffn_sums, "ffn")
    gw_out = _wgrad_acc(mix, dh1b, "wgrad_out", after=ffn_chips[4]).reshape(N_DEV, D_MODEL // N_DEV, D_MODEL)
    mid_swap = _swap_start([gw_out, gw_proj, gw_gate], "mid")

    dzm, dkv, d_gv, d_wsp, d_bsp, d_attn = _mixer_bwd(
        zuv, qkv, dmix, saved, gv, w_sp[0], b_col, bucket, after=mid_swap[4])
    mid_sums = _swap_sums(mid_swap, mid_names, dzm, "mid")
    mid_chips = _chips_start(mid_sums, "mid")
    gw_in = _wgrad_in(dzm, dkv, hn1, after=mid_chips[4]).reshape(N_DEV, D_IN // N_DEV, D_MODEL)
    in_swap = _swap_start([gw_in], "in")
    dx, d_g1 = _in_bwd(x2, dh1, dzm, dkv, g1, full_in, after=in_swap[4])
    grad_x = dx.reshape(x.shape)

    gathered = _all_gather([tail_small, d_g1, d_gv, d_wsp, d_bsp, d_attn], F32, "gather_small_grads")
    in_sums = _swap_sums(in_swap, ("w_in",), gathered[0], "in")
    in_chips = _chips_start(in_sums, "in")
    views = {"w_spatial": (GROUPS, BLOCK, BLOCK), "b_spatial": (GROUPS, BLOCK), "final_gain": (1, D_MODEL)}
    small_in = [[args[pre + n].reshape(views.get(n, args[n].shape)) for n in SMALL] for pre in ("", "m_", "v_")]
    loss, *small_out = _adamw_small(gathered, *small_in, after=in_chips[4])
    grads, deltas, new_m, new_v = {}, {}, {}, {}
    for dst, arrays in zip((grads, deltas, new_m, new_v), small_out):
        for n, arr in zip(SMALL, arrays):
            dst[n] = arr.reshape(args[n].shape)
    loss = loss[0, 0]

    after = small_out[1][0]
    for chips, sums, names, tag in ((ffn_chips, ffn_sums, ffn_names, "ffn"), (mid_chips, mid_sums, mid_names, "mid"),
                                    (in_chips, in_sums, ("w_in",), "in")):
        for n, (part, _), recv in zip(names, sums, _chips_wait(chips, after, tag)):
            g, d, mo, vo = _final_adamw(part, recv, shard(n), shard("m_" + n), shard("v_" + n), "adamw_" + n)
            for dst, arr in zip((grads, deltas, new_m, new_v), (g, d, mo, vo)):
                dst[n] = (arr.T if n == "w_in" else arr)[None]
            after = d

    order = ("norm1_gain", "w_in", "gmlp_v_gain", "w_spatial", "b_spatial", "attn_sinks", "rel_bias_table", "w_out",
             "norm2_gain", "w_ff1", "w_ff2", "w_ple_proj", "w_ple_gate", "final_gain")
    return (loss, grad_x, *[grads[n] for n in order], *[deltas[n] for n in order],
            *[new_m[n] for n in order], *[new_v[n] for n in order])
```

```python
import functools
import math

import numpy as np
import jax
import jax.numpy as jnp
from jax import lax
from jax.experimental import pallas as pl
from jax.experimental.pallas import tpu as pltpu

F32 = jnp.float32
BF = jnp.bfloat16
MESH = pl.DeviceIdType.MESH
N_DEV = 8

D_MODEL = 1024
PLE_DIM = 256
D_GMLP = 512
GROUPS = 4
GDIM = 128
BLOCK = 128
D_ATTN = 512
HEAD_DIM = 64
N_Q = 8
Q_PER_KV = 4
N_KV = N_Q // Q_PER_KV
ROWS4 = Q_PER_KV * BLOCK
D_KV = 128
D_FF = 4096
D_IN = 1792
D_MAIN = 2 * D_GMLP + D_ATTN
REL_BUCKETS = 32
EPS = 1e-6
NEG_INF = -1e30
SCALE = HEAD_DIM ** -0.5
GELU_C = math.sqrt(2.0 / math.pi)
GELU_A = 0.044715

ADAM_LR = 0.001
ADAM_B1 = 0.9
ADAM_B2 = 0.999
ADAM_EPS = 1e-08
ADAM_WD = 0.01
ADAM_STEP = 10

V7X_VMEM_LIMIT = 60000 * 1024
TOK_TILE = 256
IO_TOK_TILE = 512


def _call(body, after=None, **kw):
    if after is None:
        return pl.pallas_call(body, **kw)
    n_in = len(kw["in_specs"])

    def ordered(*refs):
        body(*refs[:n_in], *refs[n_in + 1:])

    kw["in_specs"] = list(kw["in_specs"]) + [pl.BlockSpec(memory_space=pl.ANY)]
    fn = pl.pallas_call(ordered, **kw)
    return lambda *operands: fn(*operands, after)


def _params(sem=None):
    if sem is None:
        return pltpu.CompilerParams(vmem_limit_bytes=V7X_VMEM_LIMIT)
    return pltpu.CompilerParams(dimension_semantics=sem, vmem_limit_bytes=V7X_VMEM_LIMIT)


def _nn(a, b):
    return jnp.dot(a, b, preferred_element_type=F32)


def _nt(a, b):
    return lax.dot_general(a, b, (((1,), (1,)), ((), ())), preferred_element_type=F32)


def _tn(a, b):
    return lax.dot_general(a, b, (((0,), (0,)), ((), ())), preferred_element_type=F32)


def _gelu_tanh(x):
    return jnp.tanh(GELU_C * (x + GELU_A * (x * x * x)))


def _gelu(x, t):
    return x * (0.5 * (1.0 + t))


def _gelu_and_grad(x, t):
    cdf = 0.5 * (1.0 + t)
    return x * cdf, cdf + 0.5 * x * (1.0 - t * t) * (GELU_C * (1.0 + 3.0 * GELU_A * (x * x)))


def _rms_scale(x):
    return lax.rsqrt(jnp.mean(x * x, axis=-1, keepdims=True) + EPS)


def _rms_bwd(dxn, x, r):
    return r * dxn - x * ((r * r * r) * jnp.mean(dxn * x, axis=-1, keepdims=True))


def _bucket_table():
    a = np.arange(BLOCK)[:, None]
    j = np.arange(2 * BLOCK)[None, :]
    n = BLOCK + a - j
    valid = (n >= 0) & (n < BLOCK)
    nc = np.maximum(n, 0)
    max_exact = REL_BUCKETS // 2
    nf = np.maximum(nc, 1).astype(np.float32)
    large = max_exact + (
        np.log(nf / np.float32(max_exact)) / np.float32(math.log(BLOCK / max_exact)) * np.float32(REL_BUCKETS - max_exact)
    ).astype(np.int32)
    large = np.minimum(large, REL_BUCKETS - 1)
    bucket = np.where(nc < max_exact, nc, large)
    return np.where(valid, bucket, -1).astype(np.int32)


def _in_proj(x, g1, w_in_t):
    s = x.shape[0]
    tm = min(IO_TOK_TILE, s)

    def body(x_ref, g_ref, w_ref, zuv_ref, qkv_ref, hn_ref):
        xv = x_ref[...]
        hn = ((xv * _rms_scale(xv)) * g_ref[...]).astype(BF)
        hn_ref[...] = hn
        z = _nt(hn, w_ref[...])
        zuv_ref[...] = z[:, : 2 * D_GMLP]
        qkv_ref[...] = z[:, 2 * D_GMLP:].astype(BF)

    return _call(
        body,
        name="in_proj",
        grid=(s // tm,),
        in_specs=[
            pl.BlockSpec((tm, D_MODEL), lambda i: (i, 0)),
            pl.BlockSpec((1, D_MODEL), lambda i: (0, 0)),
            pl.BlockSpec((D_IN, D_MODEL), lambda i: (0, 0)),
        ],
        out_specs=[
            pl.BlockSpec((tm, 2 * D_GMLP), lambda i: (i, 0)),
            pl.BlockSpec((tm, D_ATTN + 2 * D_KV), lambda i: (i, 0)),
            pl.BlockSpec((tm, D_MODEL), lambda i: (i, 0)),
        ],
        out_shape=[
            jax.ShapeDtypeStruct((s, 2 * D_GMLP), F32),
            jax.ShapeDtypeStruct((s, D_ATTN + 2 * D_KV), BF),
            jax.ShapeDtypeStruct((s, D_MODEL), BF),
        ],
        compiler_params=_params(("arbitrary",)),
    )(x, g1, w_in_t)


def _head_rows(h):
    kh, g = divmod(h, Q_PER_KV)
    return kh, slice(g * BLOCK, (g + 1) * BLOCK)


def _build_bias(bias_ref, bucket_ref, table_ref):
    bucket = bucket_ref[...]
    for h in range(N_Q):
        acc = jnp.zeros((BLOCK, 2 * BLOCK), F32)
        for b in range(REL_BUCKETS):
            acc = jnp.where(bucket == b, table_ref[b, h], acc)
        kh, rows = _head_rows(h)
        bias_ref[kh, rows, :] = acc


def _window_masks(i):
    row = lax.broadcasted_iota(jnp.int32, (ROWS4, BLOCK), 0) & (BLOCK - 1)
    col = lax.broadcasted_iota(jnp.int32, (ROWS4, BLOCK), 1)
    return (col > row) & (i > 0), col <= row


def _stack_heads(ref, kh, offset):
    first = offset + kh * Q_PER_KV * HEAD_DIM
    return jnp.concatenate(
        [ref[:, first + g * HEAD_DIM: first + (g + 1) * HEAD_DIM].astype(BF) for g in range(Q_PER_KV)], axis=0)


def _stack_sinks(sink_ref, kh):
    return jnp.concatenate([jnp.full((BLOCK, 1), sink_ref[0, kh * Q_PER_KV + g], F32) for g in range(Q_PER_KV)], axis=0)


def _tril_bf16(w_ref, g):
    row = lax.broadcasted_iota(jnp.int32, (BLOCK, BLOCK), 0)
    col = lax.broadcasted_iota(jnp.int32, (BLOCK, BLOCK), 1)
    return jnp.where(col <= row, w_ref[g], 0.0).astype(BF)


def _attn_probs(q_h, k_prev, k_cur, bias_h, sink, valid_prev, valid_cur):
    l_prev = jnp.where(valid_prev, _nt(q_h, k_prev) * SCALE + bias_h[:, :BLOCK], NEG_INF)
    l_cur = jnp.where(valid_cur, _nt(q_h, k_cur) * SCALE + bias_h[:, BLOCK:], NEG_INF)
    m = jnp.maximum(jnp.maximum(jnp.max(l_prev, axis=-1, keepdims=True), jnp.max(l_cur, axis=-1, keepdims=True)), sink)
    e_prev = jnp.exp(l_prev - m)
    e_cur = jnp.exp(l_cur - m)
    e_sink = jnp.exp(sink - m)
    denom = jnp.sum(e_prev, axis=-1, keepdims=True) + jnp.sum(e_cur, axis=-1, keepdims=True) + e_sink
    return e_prev / denom, e_cur / denom, e_sink / denom


def _mixer_specs(nb):
    cl = lambda i: jnp.minimum(i, nb - 1)
    return [
        pl.BlockSpec((BLOCK, 2 * D_GMLP), lambda i: (cl(i), 0)),
        pl.BlockSpec((BLOCK, D_ATTN), lambda i: (cl(i), 0)),
        pl.BlockSpec((BLOCK, 2 * D_KV), lambda i: (cl(i), D_ATTN // (2 * D_KV))),
        pl.BlockSpec((BLOCK, 2 * D_KV), lambda i: (jnp.maximum(cl(i) - 1, 0), D_ATTN // (2 * D_KV))),
        pl.BlockSpec((1, D_GMLP), lambda i: (0, 0)),
        pl.BlockSpec((GROUPS, BLOCK, BLOCK), lambda i: (0, 0, 0)),
        pl.BlockSpec((GROUPS, BLOCK, 1), lambda i: (0, 0, 0)),
        pl.BlockSpec(memory_space=pltpu.SMEM),
        pl.BlockSpec(memory_space=pltpu.SMEM),
        pl.BlockSpec((BLOCK, 2 * BLOCK), lambda i: (0, 0)),
    ]


def _mixer_fwd(zuv, qkv, gv, w_sp, b_sp, sinks, table, bucket, after=None):
    s = zuv.shape[0]
    nb = s // BLOCK

    def body(zuv_ref, q_ref, kvc_ref, kvp_ref, gv_ref, w_ref, b_ref, sink_ref, table_ref, bucket_ref,
             mix_ref, tanh_ref, prob_ref, psink_ref, bias_ref):
        i = pl.program_id(0)

        @pl.when(i == 0)
        def _():
            _build_bias(bias_ref, bucket_ref, table_ref)

        t = _gelu_tanh(zuv_ref[...])
        tanh_ref[...] = t
        u = _gelu(zuv_ref[:, :D_GMLP], t[:, :D_GMLP])
        vg = _gelu(zuv_ref[:, D_GMLP:], t[:, D_GMLP:])
        for g in range(GROUPS):
            sl = slice(g * GDIM, (g + 1) * GDIM)
            vg_g = vg[:, sl]
            vn = ((vg_g * _rms_scale(vg_g)) * gv_ref[:, sl]).astype(BF)
            sv = _nn(_tril_bf16(w_ref, g), vn) + b_ref[g]
            mix_ref[:, sl] = (u[:, sl] * sv).astype(BF)

        valid_prev, valid_cur = _window_masks(i)
        for kh in range(N_KV):
            ksl = slice(kh * HEAD_DIM, (kh + 1) * HEAD_DIM)
            vsl = slice(D_KV + kh * HEAD_DIM, D_KV + (kh + 1) * HEAD_DIM)
            q4 = _stack_heads(q_ref, kh, 0)
            p_prev, p_cur, p_sink = _attn_probs(
                q4, kvp_ref[:, ksl], kvc_ref[:, ksl], bias_ref[kh], _stack_sinks(sink_ref, kh), valid_prev, valid_cur)
            prob_ref[0, kh, :, :BLOCK] = p_prev
            prob_ref[0, kh, :, BLOCK:] = p_cur
            psink_ref[0, kh] = jnp.broadcast_to(p_sink, (ROWS4, LANES))
            o4 = _nn(p_prev.astype(BF), kvp_ref[:, vsl]) + _nn(p_cur.astype(BF), kvc_ref[:, vsl])
            for g in range(Q_PER_KV):
                first = D_GMLP + (kh * Q_PER_KV + g) * HEAD_DIM
                mix_ref[:, first:first + HEAD_DIM] = o4[g * BLOCK:(g + 1) * BLOCK].astype(BF)

    return _call(
        body,
        name="mixer_fwd",
        after=after,
        grid=(nb,),
        in_specs=_mixer_specs(nb),
        out_specs=[
            pl.BlockSpec((BLOCK, D_MODEL), lambda i: (i, 0)),
            pl.BlockSpec((BLOCK, 2 * D_GMLP), lambda i: (i, 0)),
            pl.BlockSpec((1, N_KV, ROWS4, 2 * BLOCK), lambda i: (i, 0, 0, 0)),
            pl.BlockSpec((1, N_KV, ROWS4, LANES), lambda i: (i, 0, 0, 0)),
        ],
        out_shape=[
            jax.ShapeDtypeStruct((s, D_MODEL), BF),
            jax.ShapeDtypeStruct((s, 2 * D_GMLP), F32),
            jax.ShapeDtypeStruct((nb, N_KV, ROWS4, 2 * BLOCK), F32),
            jax.ShapeDtypeStruct((nb, N_KV, ROWS4, LANES), F32),
        ],
        scratch_shapes=[pltpu.VMEM((N_KV, ROWS4, 2 * BLOCK), F32)],
        compiler_params=_params(("arbitrary",)),
    )(zuv, qkv, qkv, qkv, gv, w_sp, b_sp, sinks, table, bucket)


def _tail(x, mix, p, t, g2, gf, w_out, w_ff1, w_ff2, w_gate, w_proj):
    s = x.shape[0]
    tm = min(TOK_TILE, s)
    n_ff = w_ff1.shape[0]
    fc = D_FF // n_ff

    def body(x_ref, mix_ref, p_ref, t_ref, g2_ref, gf_ref, wo_ref, w1_ref, w2_ref, wg_ref, wp_ref,
             small_ref, dh1_ref, dh1b_ref, dmix_ref, hn2_ref, a_ref, df_ref, h2_ref, dh2_ref, dgl_ref, dpp_ref, f_ref):
        i = pl.program_id(0)

        @pl.when(i == 0)
        def _():
            small_ref[...] = jnp.zeros_like(small_ref)

        h1 = x_ref[...] + _nn(mix_ref[...], wo_ref[...])
        r2 = _rms_scale(h1)
        hn2 = ((h1 * r2) * g2_ref[...]).astype(BF)
        hn2_ref[...] = hn2
        h2 = h1
        for c in range(n_ff):
            f = _nn(hn2, w1_ref[c])
            f_ref[:, c * fc:(c + 1) * fc] = f
            a = jnp.square(jnp.maximum(f, 0.0)).astype(BF)
            a_ref[:, c * fc:(c + 1) * fc] = a
            h2 = h2 + _nn(a, w2_ref[c * fc:(c + 1) * fc, :])
        h2b = h2.astype(BF)
        h2_ref[...] = h2b
        gate = jax.nn.sigmoid(_nn(h2b, wg_ref[...]))
        pp = _nn(p_ref[...].astype(BF), wp_ref[...])
        h3 = h2 + gate * pp
        rf = _rms_scale(h3)
        gf_v = gf_ref[...]
        err = (h3 * rf) * gf_v - t_ref[...]
        small_ref[2:3, :] += jnp.sum(jnp.sum(err * err, axis=-1, keepdims=True), axis=0, keepdims=True) * (0.5 / D_MODEL)

        dy = err * (1.0 / D_MODEL)
        small_ref[1:2, :] += jnp.sum(dy * (h3 * rf), axis=0, keepdims=True)
        dh3 = _rms_bwd(dy * gf_v, h3, rf)
        dpp_ref[...] = (dh3 * gate).astype(BF)
        dgl = ((dh3 * pp) * (gate * (1.0 - gate))).astype(BF)
        dgl_ref[...] = dgl
        dh2 = dh3 + _nt(dgl, wg_ref[...])
        dh2b = dh2.astype(BF)
        dh2_ref[...] = dh2b
        dhn2 = jnp.zeros((tm, D_MODEL), F32)
        for c in range(n_ff):
            da = _nt(dh2b, w2_ref[c * fc:(c + 1) * fc, :])
            df = (da * (2.0 * jnp.maximum(f_ref[:, c * fc:(c + 1) * fc], 0.0))).astype(BF)
            df_ref[:, c * fc:(c + 1) * fc] = df
            dhn2 = dhn2 + _nt(df, w1_ref[c])
        small_ref[0:1, :] += jnp.sum(dhn2 * (h1 * r2), axis=0, keepdims=True)
        dh1 = dh2 + _rms_bwd(dhn2 * g2_ref[...], h1, r2)
        dh1_ref[...] = dh1
        dh1b = dh1.astype(BF)
        dh1b_ref[...] = dh1b
        dmix_ref[...] = _nt(dh1b, wo_ref[...])

    tile = lambda cols: pl.BlockSpec((tm, cols), lambda i: (i, 0))
    whole = lambda shape: pl.BlockSpec(shape, lambda i: (0,) * len(shape), pipeline_mode=pl.Buffered(1))
    row = pl.BlockSpec((1, D_MODEL), lambda i: (0, 0))
    act = lambda cols, dt: jax.ShapeDtypeStruct((s, cols), dt)
    return _call(
        body,
        name="tail",
        grid=(s // tm,),
        in_specs=[tile(D_MODEL), tile(D_MODEL), tile(PLE_DIM), tile(D_MODEL), row, row,
                  whole(w_out.shape), whole(w_ff1.shape), whole(w_ff2.shape), whole(w_gate.shape), whole(w_proj.shape)],
        out_specs=[pl.BlockSpec((8, D_MODEL), lambda i: (0, 0)), tile(D_MODEL), tile(D_MODEL), tile(D_MODEL), tile(D_MODEL), tile(D_FF),
                   tile(D_FF), tile(D_MODEL), tile(D_MODEL), tile(D_MODEL), tile(D_MODEL)],
        out_shape=[jax.ShapeDtypeStruct((8, D_MODEL), F32),
                   act(D_MODEL, F32), act(D_MODEL, BF), act(D_MODEL, F32), act(D_MODEL, BF), act(D_FF, BF), act(D_FF, BF), act(D_MODEL, BF),
                   act(D_MODEL, BF), act(D_MODEL, BF), act(D_MODEL, BF)],
        scratch_shapes=[pltpu.VMEM((tm, D_FF), F32)],
        compiler_params=_params(("arbitrary",)),
    )(x, mix, p, t, g2, gf, w_out, w_ff1, w_ff2, w_gate, w_proj)


def _mixer_bwd(zuv, qkv, dmix, saved, gv, w_sp, b_sp, bucket, after=None):
    s = zuv.shape[0]
    nb = s // BLOCK

    def body(zuv_ref, q_ref, kvc_ref, kvp_ref, gv_ref, w_ref, b_ref, bucket_ref, dmix_ref, tanh_ref, prob_ref, psink_ref,
             dzm_ref, dkv_ref, dgv_ref, dw_ref, db_ref, dattn_ref,
             dbias_ref, carry_ref, dsink_acc, db_acc):
        i = pl.program_id(0)

        @pl.when(i == 0)
        def _():
            dbias_ref[...] = jnp.zeros_like(dbias_ref)
            carry_ref[...] = jnp.zeros_like(carry_ref)
            dsink_acc[...] = jnp.zeros_like(dsink_acc)
            dgv_ref[...] = jnp.zeros_like(dgv_ref)
            dw_ref[...] = jnp.zeros_like(dw_ref)
            db_acc[...] = jnp.zeros_like(db_acc)

        @pl.when(i < nb)
        def _():
            u, du_dz = _gelu_and_grad(zuv_ref[:, :D_GMLP], tanh_ref[:, :D_GMLP])
            vg, dvg_dz = _gelu_and_grad(zuv_ref[:, D_GMLP:], tanh_ref[:, D_GMLP:])
            for g in range(GROUPS):
                sl = slice(g * GDIM, (g + 1) * GDIM)
                vg_g = vg[:, sl]
                rg = _rms_scale(vg_g)
                vhat = vg_g * rg
                gain = gv_ref[:, sl]
                vn = (vhat * gain).astype(BF)
                w_g = _tril_bf16(w_ref, g)
                sv = _nn(w_g, vn) + b_ref[g]
                dmix_a = dmix_ref[:, sl]
                dsv = dmix_a * u[:, sl]
                dsvb = dsv.astype(BF)
                db_acc[g] += jnp.sum(dsv, axis=-1, keepdims=True)
                dw_ref[g] += _nt(dsvb, vn)
                dvn = _tn(w_g, dsvb)
                dgv_ref[:, sl] += jnp.sum(dvn * vhat, axis=0, keepdims=True)
                dvg = _rms_bwd(dvn * gain, vg_g, rg)
                dzm_ref[:, sl] = ((dmix_a * sv) * du_dz[:, sl]).astype(BF)
                dzm_ref[:, D_GMLP + g * GDIM: D_GMLP + (g + 1) * GDIM] = (dvg * dvg_dz[:, sl]).astype(BF)

            for kh in range(N_KV):
                ksl = slice(kh * HEAD_DIM, (kh + 1) * HEAD_DIM)
                vsl = slice(D_KV + kh * HEAD_DIM, D_KV + (kh + 1) * HEAD_DIM)
                k_prev, k_cur = kvp_ref[:, ksl], kvc_ref[:, ksl]
                v_prev, v_cur = kvp_ref[:, vsl], kvc_ref[:, vsl]
                q4 = _stack_heads(q_ref, kh, 0)
                p_prev, p_cur, p_sink = prob_ref[0, kh, :, :BLOCK], prob_ref[0, kh, :, BLOCK:], psink_ref[0, kh, :, 0:1]
                do4 = _stack_heads(dmix_ref, kh, D_GMLP)
                dp_prev = _nt(do4, v_prev)
                dp_cur = _nt(do4, v_cur)
                delta = jnp.sum(p_prev * dp_prev, axis=-1, keepdims=True) + jnp.sum(p_cur * dp_cur, axis=-1, keepdims=True)
                ds_prev = p_prev * (dp_prev - delta)
                ds_cur = p_cur * (dp_cur - delta)
                dsink_acc[kh] -= p_sink * delta
                dbias_ref[kh, :, :BLOCK] += ds_prev
                dbias_ref[kh, :, BLOCK:] += ds_cur
                dsb_prev = ds_prev.astype(BF)
                dsb_cur = ds_cur.astype(BF)
                dq4 = (_nn(dsb_prev, k_prev) + _nn(dsb_cur, k_cur)) * SCALE
                for g in range(Q_PER_KV):
                    first = 2 * D_GMLP + (kh * Q_PER_KV + g) * HEAD_DIM
                    dzm_ref[:, first:first + HEAD_DIM] = dq4[g * BLOCK:(g + 1) * BLOCK].astype(BF)
                dkv_ref[:, ksl] = (carry_ref[:, ksl] + _tn(dsb_prev, q4) * SCALE).astype(BF)
                dkv_ref[:, vsl] = (carry_ref[:, vsl] + _tn(p_prev.astype(BF), do4)).astype(BF)
                carry_ref[:, ksl] = _tn(dsb_cur, q4) * SCALE
                carry_ref[:, vsl] = _tn(p_cur.astype(BF), do4)

        @pl.when(i == nb)
        def _():
            dkv_ref[...] = carry_ref[...].astype(BF)
            row = lax.broadcasted_iota(jnp.int32, (BLOCK, BLOCK), 0)
            col = lax.broadcasted_iota(jnp.int32, (BLOCK, BLOCK), 1)
            for g in range(GROUPS):
                dw_ref[g] = jnp.where(col <= row, dw_ref[g], 0.0)
                db_ref[g:g + 1, :] = jnp.sum(jnp.where(col == row, db_acc[g], 0.0), axis=0, keepdims=True)
            bucket = bucket_ref[...]
            for h in range(N_Q):
                kh, rows = _head_rows(h)
                dattn_ref[REL_BUCKETS, h] = jnp.sum(dsink_acc[kh, rows, :])
                dbh = dbias_ref[kh, rows, :]
                for b in range(REL_BUCKETS):
                    dattn_ref[b, h] = jnp.sum(jnp.where(bucket == b, dbh, 0.0))

    cl = lambda i: jnp.minimum(i, nb - 1)
    const = lambda shape: pl.BlockSpec(shape, lambda i: (0,) * len(shape))
    return _call(
        body,
        name="mixer_bwd",
        after=after,
        grid=(nb + 1,),
        in_specs=_mixer_specs(nb)[:7] + [
            const((BLOCK, 2 * BLOCK)),
            pl.BlockSpec((BLOCK, D_MODEL), lambda i: (cl(i), 0)),
            pl.BlockSpec((BLOCK, 2 * D_GMLP), lambda i: (cl(i), 0)),
            pl.BlockSpec((1, N_KV, ROWS4, 2 * BLOCK), lambda i: (cl(i), 0, 0, 0)),
            pl.BlockSpec((1, N_KV, ROWS4, LANES), lambda i: (cl(i), 0, 0, 0)),
        ],
        out_specs=[
            pl.BlockSpec((BLOCK, D_MAIN), lambda i: (cl(i), 0)),
            pl.BlockSpec((BLOCK, 2 * D_KV), lambda i: (jnp.maximum(i - 1, 0), 0)),
            const((1, D_GMLP)),
            const((GROUPS, BLOCK, BLOCK)),
            const((GROUPS, BLOCK)),
            pl.BlockSpec(memory_space=pltpu.SMEM),
        ],
        out_shape=[
            jax.ShapeDtypeStruct((s, D_MAIN), BF),
            jax.ShapeDtypeStruct((s, 2 * D_KV), BF),
            jax.ShapeDtypeStruct((1, D_GMLP), F32),
            jax.ShapeDtypeStruct((GROUPS, BLOCK, BLOCK), F32),
            jax.ShapeDtypeStruct((GROUPS, BLOCK), F32),
            jax.ShapeDtypeStruct((REL_BUCKETS + 1, N_Q), F32),
        ],
        scratch_shapes=[
            pltpu.VMEM((N_KV, ROWS4, 2 * BLOCK), F32),
            pltpu.VMEM((BLOCK, 2 * D_KV), F32),
            pltpu.VMEM((N_KV, ROWS4, 1), F32),
            pltpu.VMEM((GROUPS, BLOCK, 1), F32),
        ],
        compiler_params=_params(("arbitrary",)),
    )(zuv, qkv, qkv, qkv, gv, w_sp, b_sp, bucket, dmix, *saved)


def _in_bwd(x, dh1, dzm, dkv, g1, w_in_t, after=None):
    s = x.shape[0]
    tm = min(IO_TOK_TILE, s)

    def body(x_ref, dh1_ref, dzm_ref, dkv_ref, g_ref, w_ref, dx_ref, dg_ref):
        @pl.when(pl.program_id(0) == 0)
        def _():
            dg_ref[...] = jnp.zeros_like(dg_ref)

        dhn = _nn(dzm_ref[...], w_ref[:D_MAIN, :]) + _nn(dkv_ref[...], w_ref[D_MAIN:, :])
        xv = x_ref[...]
        r = _rms_scale(xv)
        dg_ref[...] += jnp.sum(dhn * (xv * r), axis=0, keepdims=True)
        dx_ref[...] = dh1_ref[...] + _rms_bwd(dhn * g_ref[...], xv, r)

    tile = lambda cols: pl.BlockSpec((tm, cols), lambda i: (i, 0))
    row = pl.BlockSpec((1, D_MODEL), lambda i: (0, 0))
    return _call(
        body,
        name="in_bwd",
        after=after,
        grid=(s // tm,),
        in_specs=[tile(D_MODEL), tile(D_MODEL), tile(D_MAIN), tile(2 * D_KV), row, pl.BlockSpec((D_IN, D_MODEL), lambda i: (0, 0))],
        out_specs=[tile(D_MODEL), row],
        out_shape=[jax.ShapeDtypeStruct((s, D_MODEL), F32), jax.ShapeDtypeStruct((1, D_MODEL), F32)],
        compiler_params=_params(("arbitrary",)),
    )(x, dh1, dzm, dkv, g1, w_in_t)


def _wgrad_in(dzm, dkv, hn, after=None):
    s = hn.shape[0]
    tm = 2 * D_KV
    n_main = D_MAIN // tm

    def body(dzm_ref, dkv_ref, hn_ref, o_ref):
        i = pl.program_id(0)

        @pl.when(i < n_main)
        def _():
            o_ref[...] = _tn(dzm_ref[...], hn_ref[...])

        @pl.when(i == n_main)
        def _():
            o_ref[...] = _tn(dkv_ref[...], hn_ref[...])

    return _call(
        body,
        name="wgrad_in",
        after=after,
        grid=(n_main + 1,),
        in_specs=[
            pl.BlockSpec((s, tm), lambda i: (0, jnp.minimum(i, n_main - 1))),
            pl.BlockSpec((s, tm), lambda i: (0, 0)),
            pl.BlockSpec((s, D_MODEL), lambda i: (0, 0)),
        ],
        out_specs=pl.BlockSpec((tm, D_MODEL), lambda i: (i, 0)),
        out_shape=jax.ShapeDtypeStruct((D_IN, D_MODEL), F32),
        compiler_params=_params(("arbitrary",)),
    )(dzm, dkv, hn)


def _wgrad(a, b, tm, tn, name, peer_cols=0, after=None):
    s, m = a.shape
    n = b.shape[1]

    def body(a_ref, b_ref, o_ref, at_ref):
        @pl.when(pl.program_id(1) == 0)
        def _():
            at_ref[...] = a_ref[...].astype(BF).T

        r = _nn(at_ref[...], b_ref[...])
        if peer_cols:
            for q in range(tn // peer_cols):
                o_ref[q] = r[:, q * peer_cols:(q + 1) * peer_cols]
        else:
            o_ref[...] = r

    if peer_cols:
        out_spec = pl.BlockSpec((tn // peer_cols, tm, peer_cols), lambda i, j: (j, i, 0))
        out_shape = jax.ShapeDtypeStruct((n // peer_cols, m, peer_cols), F32)
    else:
        out_spec = pl.BlockSpec((tm, tn), lambda i, j: (i, j))
        out_shape = jax.ShapeDtypeStruct((m, n), F32)
    return _call(
        body,
        name=name,
        after=after,
        grid=(m // tm, n // tn),
        in_specs=[pl.BlockSpec((s, tm), lambda i, j: (0, i)), pl.BlockSpec((s, tn), lambda i, j: (0, j))],
        out_specs=out_spec,
        out_shape=out_shape,
        scratch_shapes=[pltpu.VMEM((tm, s), BF)],
        compiler_params=_params(("arbitrary", "arbitrary")),
    )(a, b)


def _adamw_math(w, g, m, v):
    m_new = ADAM_B1 * m + (1.0 - ADAM_B1) * g
    v_new = ADAM_B2 * v + (1.0 - ADAM_B2) * jnp.square(g)
    m_hat = m_new / (1.0 - ADAM_B1 ** ADAM_STEP)
    v_hat = v_new / (1.0 - ADAM_B2 ** ADAM_STEP)
    delta = -ADAM_LR * (m_hat / (jnp.sqrt(v_hat) + ADAM_EPS) + ADAM_WD * w)
    return delta, m_new, v_new


def _final_adamw(part, recv, w, m, v, name):
    r, c = w.shape
    tr = min(r, 512)

    def body(p_ref, r_ref, w_ref, m_ref, v_ref, g_ref, d_ref, mo_ref, vo_ref):
        g = p_ref[...]
        for j in range(3):
            g = g + r_ref[j].astype(F32)
        g_ref[...] = g
        d_ref[...], mo_ref[...], vo_ref[...] = _adamw_math(w_ref[...], g, m_ref[...], v_ref[...])

    spec = pl.BlockSpec((tr, c), lambda i: (i, 0))
    return _call(
        body,
        name=name,
        grid=(r // tr,),
        in_specs=[spec, pl.BlockSpec((3, tr, c), lambda i: (0, i, 0)), spec, spec, spec],
        out_specs=[spec] * 4,
        out_shape=[jax.ShapeDtypeStruct((r, c), F32)] * 4,
        compiler_params=_params(("arbitrary",)),
    )(part, recv, w, m, v)


def _rs_sum(g, land, blocks, name):
    _, r, c = g.shape
    tr = min(r, 256)

    def body(blk_ref, g0_ref, g1_ref, g2_ref, g3_ref, l_ref, part_ref, send_ref):
        part_ref[...] = g0_ref[0] + l_ref[0]
        for j, gj_ref in enumerate((g1_ref, g2_ref, g3_ref)):
            send_ref[j] = (gj_ref[0] + l_ref[j + 1]).astype(BF)

    def pick(j):
        return pl.BlockSpec((1, tr, c), lambda i, blk: (blk[j], i, 0))

    return _call(
        body,
        name=name,
        grid_spec=pltpu.PrefetchScalarGridSpec(
            num_scalar_prefetch=1,
            grid=(r // tr,),
            in_specs=[pick(0), pick(1), pick(2), pick(3), pl.BlockSpec((4, tr, c), lambda i, blk: (0, i, 0))],
            out_specs=[pl.BlockSpec((tr, c), lambda i, blk: (i, 0)), pl.BlockSpec((3, tr, c), lambda i, blk: (0, i, 0))],
        ),
        out_shape=[jax.ShapeDtypeStruct((r, c), F32), jax.ShapeDtypeStruct((3, r, c), BF)],
        compiler_params=_params(("arbitrary",)),
    )(blocks, g, g, g, g, land)


def _adamw_small(gathered, weights, moms, vels, after):
    n_w = len(weights)

    def body(*refs):
        tail_ref, g1_ref, gv_ref, wsp_ref, bsp_ref, attn_ref = refs[:6]
        w_refs, m_refs, v_refs = refs[6:6 + n_w], refs[6 + n_w:6 + 2 * n_w], refs[6 + 2 * n_w:6 + 3 * n_w]
        outs = refs[6 + 3 * n_w:]
        loss_ref, g_refs, d_refs = outs[0], outs[1:1 + n_w], outs[1 + n_w:1 + 2 * n_w]
        mo_refs, vo_refs = outs[1 + 2 * n_w:1 + 3 * n_w], outs[1 + 3 * n_w:]

        def total(part):
            acc = part(0)
            for d in range(1, N_DEV):
                acc = acc + part(d)
            return acc

        grads = [
            total(lambda d: g1_ref[d]), total(lambda d: gv_ref[d]), total(lambda d: wsp_ref[d]), total(lambda d: bsp_ref[d]),
            total(lambda d: attn_ref[d, REL_BUCKETS:, :]), total(lambda d: attn_ref[d, :REL_BUCKETS, :]),
            total(lambda d: tail_ref[d, 0:1, :]), total(lambda d: tail_ref[d, 1:2, :])]
        loss_ref[...] = total(lambda d: tail_ref[d, 2:3, 0:1])
        for k in range(n_w):
            g_refs[k][...] = grads[k]
            d_refs[k][...], mo_refs[k][...], vo_refs[k][...] = _adamw_math(w_refs[k][...], grads[k], m_refs[k][...], v_refs[k][...])

    shapes = [jax.ShapeDtypeStruct(w.shape, F32) for w in weights]
    outs = _call(
        body,
        name="adamw_small",
        after=after,
        in_specs=[pl.BlockSpec(memory_space=pltpu.VMEM)] * (6 + 3 * n_w),
        out_shape=[jax.ShapeDtypeStruct((1, 1), F32)] + shapes * 4,
        compiler_params=_params(),
    )(*gathered, *weights, *moms, *vels)
    return outs[0], outs[1:1 + n_w], outs[1 + n_w:1 + 2 * n_w], outs[1 + 2 * n_w:1 + 3 * n_w], outs[1 + 3 * n_w:]


def _place():
    x, y, c = lax.axis_index("x"), lax.axis_index("y"), lax.axis_index("c")
    return x, y, c, [(1 - x, y), (x, 1 - y), (1 - x, 1 - y)]


def _dev_index(px, py, pc):
    return 4 * px + 2 * py + pc


def _all_gather(shards, out_dtype, name, after=None):
    k_n = len(shards)

    def body(*refs):
        ins, outs, stage = refs[:k_n], refs[k_n:2 * k_n], refs[2 * k_n:3 * k_n]
        send_sems, recv_sems, local_sems = refs[3 * k_n:]
        x, y, c, chips = _place()
        me, sibling = (x, y, c), (x, y, 1 - c)

        def copy(k, j, block, to, src=None):
            dst = outs[k].at[_dev_index(*block)]
            return pltpu.make_async_remote_copy(
                src_ref=dst if src is None else src, dst_ref=dst,
                send_sem=send_sems.at[k, j], recv_sem=recv_sems.at[k, j], device_id=to, device_id_type=MESH)

        mine, first = [], []
        for k in range(k_n):
            stage[k][...] = ins[k][...].astype(out_dtype)
            mine.append(pltpu.make_async_copy(stage[k], outs[k].at[_dev_index(*me)], local_sems.at[k]))
            mine[k].start()
            first.append(copy(k, 0, me, sibling, src=stage[k]))
            first += [copy(k, 1 + j, me, (*chip, c), src=stage[k]) for j, chip in enumerate(chips)]
        for cp in first:
            cp.start()
        passed = []
        for k in range(k_n):
            for j, chip in enumerate(chips):
                copy(k, 1 + j, (*chip, c), me).wait_recv()
                passed.append(copy(k, 4 + j, (*chip, c), sibling))
                passed[-1].start()
        for k in range(k_n):
            copy(k, 0, sibling, me).wait_recv()
            for j, chip in enumerate(chips):
                copy(k, 4 + j, (*chip, 1 - c), me).wait_recv()
        for cp in first + passed:
            cp.wait_send()
        for cp in mine:
            cp.wait()

    return _call(
        body,
        name=name,
        after=after,
        in_specs=[pl.BlockSpec(memory_space=pltpu.VMEM)] * k_n,
        out_specs=[pl.BlockSpec(memory_space=pl.ANY)] * k_n,
        out_shape=[jax.ShapeDtypeStruct((N_DEV,) + sh.shape, out_dtype) for sh in shards],
        scratch_shapes=[pltpu.VMEM(sh.shape, out_dtype) for sh in shards]
        + [pltpu.SemaphoreType.DMA((k_n, 7)), pltpu.SemaphoreType.DMA((k_n, 7)), pltpu.SemaphoreType.DMA((k_n,))],
        compiler_params=_params(),
    )(*shards)


HBM_SPEC = pl.BlockSpec(memory_space=pltpu.HBM)
SEM_SPEC = pl.BlockSpec(memory_space=pltpu.SEMAPHORE)
ANY_SPEC = pl.BlockSpec(memory_space=pl.ANY)
DATAFLOW = pltpu.SideEffectType.DATAFLOW_SIDE_EFFECTING


def _hbm(a):
    return pltpu.with_memory_space_constraint(a, pltpu.HBM)


def _prep_weights(shards):
    k_n = len(shards)

    def body(*refs):
        ins, outs, stage, sems = refs[:k_n], refs[k_n:2 * k_n], refs[2 * k_n:3 * k_n], refs[3 * k_n]
        x, y, c, _ = _place()
        copies = []
        for k in range(k_n):
            stage[k][...] = ins[k][...].astype(BF)
            copies.append(pltpu.make_async_copy(stage[k], outs[k].at[_dev_index(x, y, c)], sems.at[k]))
            copies[k].start()
        for cp in copies:
            cp.wait()

    return _call(
        body,
        name="prep_weights",
        in_specs=[pl.BlockSpec(memory_space=pltpu.VMEM)] * k_n,
        out_specs=[ANY_SPEC] * k_n,
        out_shape=[jax.ShapeDtypeStruct((N_DEV,) + sh.shape, BF) for sh in shards],
        scratch_shapes=[pltpu.VMEM(sh.shape, BF) for sh in shards] + [pltpu.SemaphoreType.DMA((k_n,))],
        compiler_params=_params(),
    )(*shards)


def _ag_piece(land_k, block, half, peer, send_sem, recv_sem):
    ref = land_k.at[_dev_index(*block)]
    if half is not None:
        rows = land_k.shape[1] // 2
        ref = ref.at[pl.ds(half * rows, rows)]
    return pltpu.make_async_remote_copy(
        src_ref=ref, dst_ref=ref, send_sem=send_sem, recv_sem=recv_sem, device_id=peer, device_id_type=MESH)


def _ag_plan():
    x, y, c, _ = _place()
    me, sib = (x, y, c), (x, y, 1 - c)
    xn, yn, diag = (1 - x, y, c), (x, 1 - y, c), (1 - x, 1 - y, c)
    return dict(
        relay_halves=[(me, 0, xn), (me, 1, yn)],
        others=[(me, None, sib), (me, 1, xn), (me, 0, yn)],
        relays=[(xn, 0, yn), (yn, 1, xn)],
        near=[(xn, None, sib), (yn, None, sib)],
        far=[(diag, None, sib)],
    )


def _ag_stage(land, stage, send_sems, recv_sems, act):
    copies = _ag_plan()[stage]
    n = len(copies)
    for k in range(len(land)):
        for j, (block, half, peer) in enumerate(copies):
            cp = _ag_piece(land[k], block, half, peer, send_sems.at[n * k + j], recv_sems.at[n * k + j])
            if act == "start":
                cp.start()
            else:
                cp.wait_send()
                cp.wait_recv()


def _sem_shapes(*counts):
    return [pltpu.SemaphoreType.DMA((n,)) for n in counts for _ in range(2)]


def _ag_start(first, rest):
    lands = list(first) + list(rest)
    k_n, k_first = len(lands), len(first)
    k_rest = k_n - k_first

    def body(*refs):
        land = refs[:k_n]
        sems = refs[k_n:k_n + 6]
        token = refs[-1]
        x, y, c, chips = _place()
        targets = [(x, y, 1 - c)] + [(*chip, c) for chip in chips]
        for k in range(k_first):
            for j, to in enumerate(targets):
                _ag_piece(land[k], (x, y, c), None, to, sems[0].at[4 * k + j], sems[1].at[4 * k + j]).start()
        _ag_stage(land[k_first:], "relay_halves", sems[2], sems[3], "start")
        _ag_stage(land[k_first:], "others", sems[4], sems[5], "start")
        token[...] = jnp.zeros_like(token)

    outs = pl.pallas_call(
        body,
        name="ag_start",
        in_specs=[HBM_SPEC] * k_n,
        out_specs=(*[SEM_SPEC] * 6, *[HBM_SPEC] * k_n, pl.BlockSpec(memory_space=pltpu.VMEM)),
        out_shape=(*_sem_shapes(4 * k_first, 2 * k_rest, 3 * k_rest),
                   *[pltpu.HBM(a.shape, a.dtype) for a in lands], jax.ShapeDtypeStruct((8, LANES), F32)),
        input_output_aliases={k: 6 + k for k in range(k_n)},
        compiler_params=pltpu.CompilerParams(has_side_effects=DATAFLOW),
    )(*[_hbm(a) for a in lands])
    flying = list(outs[6:6 + k_n])
    return (outs[0], outs[1], flying[:k_first]), (outs[2:6], flying[k_first:]), outs[-1]


def _ag_split_call(lands, waits, starts, after, name):
    k_n = len(lands)
    plan_sizes = dict(relay_halves=2, others=3, relays=2, near=2, far=1)
    n_in, n_out = 2 * len(waits), 2 * len(starts)

    def body(*refs):
        land = refs[:k_n]
        in_sems = refs[k_n:k_n + n_in]
        out_sems, token = refs[len(refs) - 1 - n_out:len(refs) - 1], refs[-1]
        for w, (stage, _, _) in enumerate(waits):
            _ag_stage(land, stage, in_sems[2 * w], in_sems[2 * w + 1], "wait")
            if w < len(starts):
                _ag_stage(land, starts[w], out_sems[2 * w], out_sems[2 * w + 1], "start")
        token[...] = jnp.zeros_like(token)

    outs = pl.pallas_call(
        body,
        name=name,
        in_specs=[HBM_SPEC] * k_n + [SEM_SPEC] * n_in + [ANY_SPEC],
        out_specs=(*[HBM_SPEC] * k_n, *[SEM_SPEC] * n_out, pl.BlockSpec(memory_space=pltpu.VMEM)),
        out_shape=(*[pltpu.HBM(a.shape, a.dtype) for a in lands], *_sem_shapes(*[plan_sizes[s] * k_n for s in starts]),
                   jax.ShapeDtypeStruct((8, LANES), F32)),
        input_output_aliases={k: k for k in range(k_n)},
        compiler_params=pltpu.CompilerParams(has_side_effects=DATAFLOW),
    )(*lands, *[s for _, a, b in waits for s in (a, b)], after)
    return list(outs[:k_n]), list(outs[k_n:k_n + n_out]), outs[-1]


def _ag_mid(lands, send_sems, recv_sems, after, name):
    k_n = len(lands)

    def body(*refs):
        land = refs[:k_n]
        send1, recv1 = refs[k_n], refs[k_n + 1]
        fwd_send, fwd_recv = refs[-2], refs[-1]
        x, y, c, chips = _place()
        sources = [(x, y, 1 - c)] + [(*chip, c) for chip in chips]
        for k in range(k_n):
            mine = land[k].at[_dev_index(x, y, c)]
            for j, frm in enumerate(sources):
                got = land[k].at[_dev_index(*frm)]
                cp = pltpu.make_async_remote_copy(
                    src_ref=mine, dst_ref=got, send_sem=send1.at[4 * k + j], recv_sem=recv1.at[4 * k + j], device_id=frm, device_id_type=MESH)
                cp.wait_send()
                cp.wait_recv()
                if j >= 1:
                    pltpu.make_async_remote_copy(
                        src_ref=got, dst_ref=got, send_sem=fwd_send.at[3 * k + j - 1], recv_sem=fwd_recv.at[3 * k + j - 1],
                        device_id=(x, y, 1 - c), device_id_type=MESH).start()

    outs = pl.pallas_call(
        body,
        name=name,
        in_specs=[HBM_SPEC] * k_n + [SEM_SPEC, SEM_SPEC, ANY_SPEC],
        out_specs=(*[HBM_SPEC] * k_n, SEM_SPEC, SEM_SPEC),
        out_shape=(*[pltpu.HBM(a.shape, a.dtype) for a in lands], pltpu.SemaphoreType.DMA((3 * k_n,)), pltpu.SemaphoreType.DMA((3 * k_n,))),
        input_output_aliases={k: k for k in range(k_n)},
        compiler_params=pltpu.CompilerParams(has_side_effects=DATAFLOW),
    )(*lands, send_sems, recv_sems, after)
    return list(outs[:k_n]), outs[-2], outs[-1]


def _ag_end(lands, fwd_send, fwd_recv, name):
    k_n = len(lands)

    def body(*refs):
        land = refs[:k_n]
        fsend, frecv = refs[k_n], refs[k_n + 1]
        x, y, c, chips = _place()
        for k in range(k_n):
            for j, chip in enumerate(chips):
                cp = pltpu.make_async_remote_copy(
                    src_ref=land[k].at[_dev_index(*chip, c)], dst_ref=land[k].at[_dev_index(*chip, 1 - c)],
                    send_sem=fsend.at[3 * k + j], recv_sem=frecv.at[3 * k + j], device_id=(x, y, 1 - c), device_id_type=MESH)
                cp.wait_send()
                cp.wait_recv()

    outs = pl.pallas_call(
        body,
        name=name,
        in_specs=[HBM_SPEC] * k_n + [SEM_SPEC, SEM_SPEC],
        out_specs=tuple([HBM_SPEC] * k_n),
        out_shape=tuple(pltpu.HBM(a.shape, a.dtype) for a in lands),
        input_output_aliases={k: k for k in range(k_n)},
        compiler_params=pltpu.CompilerParams(has_side_effects=DATAFLOW),
    )(*lands, fwd_send, fwd_recv)
    return list(outs)


def _chips4():
    x, y, c, others = _place()
    return x, y, c, [(x, y)] + others


def _route_sibling(j):
    x, y, c, chips = _chips4()
    return _dev_index(*chips[j], 1 - c), j, (x, y, 1 - c)


def _route_chips(j):
    x, y, c, chips = _chips4()
    return j, j, (*chips[j + 1], c)


def _xchg_copy(route, n, k, j, src, dst, send_sems, recv_sems):
    si, di, peer = route(j)
    return pltpu.make_async_remote_copy(
        src_ref=src[k].at[si], dst_ref=dst[k].at[di], send_sem=send_sems.at[n * k + j], recv_sem=recv_sems.at[n * k + j],
        device_id=peer, device_id_type=MESH)


def _xchg_start(srcs, slot_shapes, route, n, name):
    k_n = len(srcs)
    dsts = [lax.empty((n,) + tuple(sh), a.dtype) for sh, a in zip(slot_shapes, srcs)]

    def body(*refs):
        src, dst = refs[:k_n], refs[k_n:2 * k_n]
        send_sems, recv_sems, token = refs[2 * k_n], refs[2 * k_n + 1], refs[-1]
        for k in range(k_n):
            for j in range(n):
                _xchg_copy(route, n, k, j, src, dst, send_sems, recv_sems).start()
        token[...] = jnp.zeros_like(token)

    arrays = list(srcs) + dsts
    outs = pl.pallas_call(
        body,
        name=name,
        in_specs=[HBM_SPEC] * (2 * k_n),
        out_specs=(SEM_SPEC, SEM_SPEC, *[HBM_SPEC] * (2 * k_n), pl.BlockSpec(memory_space=pltpu.VMEM)),
        out_shape=(pltpu.SemaphoreType.DMA((n * k_n,)), pltpu.SemaphoreType.DMA((n * k_n,)),
                   *[pltpu.HBM(a.shape, a.dtype) for a in arrays], jax.ShapeDtypeStruct((8, LANES), F32)),
        input_output_aliases={i: 2 + i for i in range(2 * k_n)},
        compiler_params=pltpu.CompilerParams(has_side_effects=DATAFLOW),
    )(*[_hbm(a) for a in arrays])
    return outs[0], outs[1], list(outs[2:2 + k_n]), list(outs[2 + k_n:2 + 2 * k_n]), outs[-1]


def _xchg_wait(send_sems, recv_sems, srcs, dsts, route, n, after, name):
    k_n = len(srcs)

    def body(*refs):
        src, dst = refs[:k_n], refs[k_n:2 * k_n]
        send, recv = refs[2 * k_n], refs[2 * k_n + 1]
        for k in range(k_n):
            for j in range(n):
                cp = _xchg_copy(route, n, k, j, src, dst, send, recv)
                cp.wait_send()
                cp.wait_recv()

    arrays = list(srcs) + list(dsts)
    outs = pl.pallas_call(
        body,
        name=name,
        in_specs=[HBM_SPEC] * (2 * k_n) + [SEM_SPEC, SEM_SPEC, ANY_SPEC],
        out_specs=tuple([HBM_SPEC] * (2 * k_n)),
        out_shape=tuple(pltpu.HBM(a.shape, a.dtype) for a in arrays),
        input_output_aliases={i: i for i in range(2 * k_n)},
        compiler_params=pltpu.CompilerParams(has_side_effects=DATAFLOW),
    )(*arrays, send_sems, recv_sems, after)
    return list(outs[:k_n]), list(outs[k_n:])


SMALL = ("norm1_gain", "gmlp_v_gain", "w_spatial", "b_spatial", "attn_sinks", "rel_bias_table", "norm2_gain", "final_gain")
LANES = 128


def _swap_start(grads, tag):
    return _xchg_start(grads, [g.shape[1:] for g in grads], _route_sibling, 4, f"rs_{tag}_swap_start")


def _swap_sums(swap, names, after, tag):
    send1, recv1, src1, land1, _ = swap
    x, y, c, chips = _chips4()
    blocks = jnp.stack([_dev_index(*chip, c) for chip in chips]).astype(jnp.int32)
    src1, land1 = _xchg_wait(send1, recv1, src1, land1, _route_sibling, 4, after, f"rs_{tag}_swap_wait")
    return [_rs_sum(g, land, blocks, f"rs_sum_{n}") for g, land, n in zip(src1, land1, names)]


def _chips_start(sums, tag):
    sends = [ps[1] for ps in sums]
    return _xchg_start(sends, [a.shape[1:] for a in sends], _route_chips, 3, f"rs_{tag}_chips_start")


def _chips_wait(chips, after, tag):
    send2, recv2, src2, land2, _ = chips
    return _xchg_wait(send2, recv2, src2, land2, _route_chips, 3, after, f"rs_{tag}_chips_wait")[1]


def kernel(x, p, norm1_gain, w_in, gmlp_v_gain, w_spatial, b_spatial, attn_sinks, rel_bias_table, w_out, norm2_gain, w_ff1, w_ff2, w_ple_proj, w_ple_gate, final_gain, loss_target, m_norm1_gain, m_w_in, m_gmlp_v_gain, m_w_spatial, m_b_spatial, m_attn_sinks, m_rel_bias_table, m_w_out, m_norm2_gain, m_w_ff1, m_w_ff2, m_w_ple_proj, m_w_ple_gate, m_final_gain, v_norm1_gain, v_w_in, v_gmlp_v_gain, v_w_spatial, v_b_spatial, v_attn_sinks, v_rel_bias_table, v_w_out, v_norm2_gain, v_w_ff1, v_w_ff2, v_w_ple_proj, v_w_ple_gate, v_final_gain):
    args = dict(locals())
    s = x.shape[1]
    big = ("w_in", "w_out", "w_ff1", "w_ff2", "w_ple_proj", "w_ple_gate")

    x2, p2, t2 = x.reshape(s, D_MODEL), p.reshape(s, PLE_DIM), loss_target.reshape(s, D_MODEL)
    g1, gv, w_sp, b_sp, sinks, table, g2, gf = (args[n] for n in SMALL)
    bucket = jnp.asarray(_bucket_table())
    b_col = b_sp.reshape(GROUPS, BLOCK, 1)

    def shard(name):
        return args[name][0].T if name.endswith("w_in") else args[name][0]

    lands = _prep_weights([shard(n) for n in big])
    (send_in, recv_in, fly_in), (rest_sems, fly_rest), token = _ag_start(lands[:1], lands[1:])
    mid_in, fwd_send_in, fwd_recv_in = _ag_mid(fly_in, send_in, recv_in, token, "ag_mid_w_in")
    g_in = _ag_end(mid_in, fwd_send_in, fwd_recv_in, "ag_end_w_in")[0]
    full_in = g_in.reshape(D_IN, D_MODEL)

    zuv, qkv, hn1 = _in_proj(x2, g1, full_in)
    fly_rest, relay_sems, relayed = _ag_split_call(
        fly_rest, [("relay_halves", *rest_sems[:2])], ["relays"], zuv, "ag_relay")
    mix, *saved = _mixer_fwd(zuv, qkv, gv, w_sp[0], b_col, sinks, table, bucket, after=relayed)
    fly_rest, fwd_sems, _ = _ag_split_call(
        fly_rest, [("others", *rest_sems[2:]), ("relays", *relay_sems)], ["near", "far"], mix, "ag_mid_rest")
    g_out, g_ff1, g_ff2, g_proj, g_gate = _ag_split_call(
        fly_rest, [("near", *fwd_sems[:2]), ("far", *fwd_sems[2:])], [], mix, "ag_end_rest")[0]
    full_out, full_ff2, full_gate = g_out.reshape(D_MODEL, D_MODEL), g_ff2.reshape(D_FF, D_MODEL), g_gate.reshape(D_MODEL, D_MODEL)
    full_proj = g_proj.transpose(1, 0, 2).reshape(PLE_DIM, D_MODEL)

    (tail_small, dh1, dh1b, dmix, hn2, a, df, h2, dh2, dgl, dpp) = _tail(
        x2, mix, p2, t2, g2, gf.reshape(1, D_MODEL), full_out, g_ff1, full_ff2, full_gate, full_proj)

    ffn_names, mid_names = ("w_ff1", "w_ff2"), ("w_out", "w_ple_proj", "w_ple_gate")
    gw_ff1 = _wgrad(hn2, df, D_MODEL, D_FF // N_DEV, "wgrad_ff1", peer_cols=D_FF // N_DEV)
    gw_ff2 = _wgrad(a, dh2, 1024, D_MODEL, "wgrad_ff2").reshape(N_DEV, D_FF // N_DEV, D_MODEL)
    ffn_swap = _swap_start([gw_ff1, gw_ff2], "ffn")
    gw_gate = _wgrad(h2, dgl, D_MODEL, 512, "wgrad_gate", after=ffn_swap[4]).reshape(N_DEV, D_MODEL // N_DEV, D_MODEL)
    gw_proj = _wgrad(p2, dpp, PLE_DIM, D_MODEL // 2, "wgrad_proj", peer_cols=D_MODEL // N_DEV, after=gw_gate)
    ffn_sums = _swap_sums(ffn_swap, ffn_names, gw_proj, "ffn")
    ffn_chips = _chips_start(ffn_sums, "ffn")
    gw_out = _wgrad(mix, dh1b, D_MODEL, 512, "wgrad_out", after=ffn_chips[4]).reshape(N_DEV, D_MODEL // N_DEV, D_MODEL)
    mid_swap = _swap_start([gw_out, gw_proj, gw_gate], "mid")

    dzm, dkv, d_gv, d_wsp, d_bsp, d_attn = _mixer_bwd(
        zuv, qkv, dmix, saved, gv, w_sp[0], b_col, bucket, after=mid_swap[4])
    mid_sums = _swap_sums(mid_swap, mid_names, dzm, "mid")
    mid_chips = _chips_start(mid_sums, "mid")
    gw_in = _wgrad_in(dzm, dkv, hn1, after=mid_chips[4]).reshape(N_DEV, D_IN // N_DEV, D_MODEL)
    in_swap = _swap_start([gw_in], "in")
    dx, d_g1 = _in_bwd(x2, dh1, dzm, dkv, g1, full_in, after=in_swap[4])
    grad_x = dx.reshape(x.shape)

    gathered = _all_gather([tail_small, d_g1, d_gv, d_wsp, d_bsp, d_attn], F32, "gather_small_grads")
    in_sums = _swap_sums(in_swap, ("w_in",), gathered[0], "in")
    in_chips = _chips_start(in_sums, "in")
    views = {"w_spatial": (GROUPS, BLOCK, BLOCK), "b_spatial": (GROUPS, BLOCK), "final_gain": (1, D_MODEL)}
    small_in = [[args[pre + n].reshape(views.get(n, args[n].shape)) for n in SMALL] for pre in ("", "m_", "v_")]
    loss, *small_out = _adamw_small(gathered, *small_in, after=in_chips[4])
    grads, deltas, new_m, new_v = {}, {}, {}, {}
    for dst, arrays in zip((grads, deltas, new_m, new_v), small_out):
        for n, arr in zip(SMALL, arrays):
            dst[n] = arr.reshape(args[n].shape)
    loss = loss[0, 0]

    after = small_out[1][0]
    for chips, sums, names, tag in ((ffn_chips, ffn_sums, ffn_names, "ffn"), (mid_chips, mid_sums, mid_names, "mid"),
                                    (in_chips, in_sums, ("w_in",), "in")):
        for n, (part, _), recv in zip(names, sums, _chips_wait(chips, after, tag)):
            g, d, mo, vo = _final_adamw(part, recv, shard(n), shard("m_" + n), shard("v_" + n), "adamw_" + n)
            for dst, arr in zip((grads, deltas, new_m, new_v), (g, d, mo, vo)):
                dst[n] = (arr.T if n == "w_in" else arr)[None]
            after = d

    order = ("norm1_gain", "w_in", "gmlp_v_gain", "w_spatial", "b_spatial", "attn_sinks", "rel_bias_table", "w_out",
             "norm2_gain", "w_ff1", "w_ff2", "w_ple_proj", "w_ple_gate", "final_gain")
    return (loss, grad_x, *[grads[n] for n in order], *[deltas[n] for n in order],
            *[new_m[n] for n in order], *[new_v[n] for n in order])
```

```python
import functools
import math

import numpy as np
import jax
import jax.numpy as jnp
from jax import lax
from jax.experimental import pallas as pl
from jax.experimental.pallas import tpu as pltpu

F32 = jnp.float32
BF = jnp.bfloat16
MESH = pl.DeviceIdType.MESH
N_DEV = 8

D_MODEL = 1024
PLE_DIM = 256
D_GMLP = 512
GROUPS = 4
GDIM = 128
BLOCK = 128
D_ATTN = 512
HEAD_DIM = 64
N_Q = 8
Q_PER_KV = 4
N_KV = N_Q // Q_PER_KV
ROWS4 = Q_PER_KV * BLOCK
D_KV = 128
D_FF = 4096
D_IN = 1792
D_MAIN = 2 * D_GMLP + D_ATTN
REL_BUCKETS = 32
EPS = 1e-6
NEG_INF = -1e30
SCALE = HEAD_DIM ** -0.5
GELU_C = math.sqrt(2.0 / math.pi)
GELU_A = 0.044715

ADAM_LR = 0.001
ADAM_B1 = 0.9
ADAM_B2 = 0.999
ADAM_EPS = 1e-08
ADAM_WD = 0.01
ADAM_STEP = 10

V7X_VMEM_LIMIT = 60000 * 1024
TOK_TILE = 256
IO_TOK_TILE = 512


def _call(body, after=None, **kw):
    if after is None:
        return pl.pallas_call(body, **kw)
    n_in = len(kw["in_specs"])

    def ordered(*refs):
        body(*refs[:n_in], *refs[n_in + 1:])

    kw["in_specs"] = list(kw["in_specs"]) + [pl.BlockSpec(memory_space=pl.ANY)]
    fn = pl.pallas_call(ordered, **kw)
    return lambda *operands: fn(*operands, after)


def _params(sem=None):
    if sem is None:
        return pltpu.CompilerParams(vmem_limit_bytes=V7X_VMEM_LIMIT)
    return pltpu.CompilerParams(dimension_semantics=sem, vmem_limit_bytes=V7X_VMEM_LIMIT)


def _nn(a, b):
    return jnp.dot(a, b, preferred_element_type=F32)


def _nt(a, b):
    return lax.dot_general(a, b, (((1,), (1,)), ((), ())), preferred_element_type=F32)


def _tn(a, b):
    return lax.dot_general(a, b, (((0,), (0,)), ((), ())), preferred_element_type=F32)


def _gelu_tanh(x):
    return jnp.tanh(GELU_C * (x + GELU_A * (x * x * x)))


def _gelu(x, t):
    return x * (0.5 * (1.0 + t))


def _gelu_and_grad(x, t):
    cdf = 0.5 * (1.0 + t)
    return x * cdf, cdf + 0.5 * x * (1.0 - t * t) * (GELU_C * (1.0 + 3.0 * GELU_A * (x * x)))


def _rms_scale(x):
    return lax.rsqrt(jnp.mean(x * x, axis=-1, keepdims=True) + EPS)


def _rms_bwd(dxn, x, r):
    return r * dxn - x * ((r * r * r) * jnp.mean(dxn * x, axis=-1, keepdims=True))


def _bucket_table():
    a = np.arange(BLOCK)[:, None]
    j = np.arange(2 * BLOCK)[None, :]
    n = BLOCK + a - j
    valid = (n >= 0) & (n < BLOCK)
    nc = np.maximum(n, 0)
    max_exact = REL_BUCKETS // 2
    nf = np.maximum(nc, 1).astype(np.float32)
    large = max_exact + (
        np.log(nf / np.float32(max_exact)) / np.float32(math.log(BLOCK / max_exact)) * np.float32(REL_BUCKETS - max_exact)
    ).astype(np.int32)
    large = np.minimum(large, REL_BUCKETS - 1)
    bucket = np.where(nc < max_exact, nc, large)
    return np.where(valid, bucket, -1).astype(np.int32)


def _in_proj(x, g1, w_in_t):
    s = x.shape[0]
    tm = min(IO_TOK_TILE, s)

    def body(x_ref, g_ref, w_ref, zuv_ref, qkv_ref, hn_ref):
        xv = x_ref[...]
        hn = ((xv * _rms_scale(xv)) * g_ref[...]).astype(BF)
        hn_ref[...] = hn
        z = _nt(hn, w_ref[...])
        zuv_ref[...] = z[:, : 2 * D_GMLP]
        qkv_ref[...] = z[:, 2 * D_GMLP:].astype(BF)

    return _call(
        body,
        name="in_proj",
        grid=(s // tm,),
        in_specs=[
            pl.BlockSpec((tm, D_MODEL), lambda i: (i, 0)),
            pl.BlockSpec((1, D_MODEL), lambda i: (0, 0)),
            pl.BlockSpec((D_IN, D_MODEL), lambda i: (0, 0)),
        ],
        out_specs=[
            pl.BlockSpec((tm, 2 * D_GMLP), lambda i: (i, 0)),
            pl.BlockSpec((tm, D_ATTN + 2 * D_KV), lambda i: (i, 0)),
            pl.BlockSpec((tm, D_MODEL), lambda i: (i, 0)),
        ],
        out_shape=[
            jax.ShapeDtypeStruct((s, 2 * D_GMLP), F32),
            jax.ShapeDtypeStruct((s, D_ATTN + 2 * D_KV), BF),
            jax.ShapeDtypeStruct((s, D_MODEL), BF),
        ],
        compiler_params=_params(("arbitrary",)),
    )(x, g1, w_in_t)


def _head_rows(h):
    kh, g = divmod(h, Q_PER_KV)
    return kh, slice(g * BLOCK, (g + 1) * BLOCK)


def _build_bias(bias_ref, bucket_ref, table_ref):
    bucket = bucket_ref[...]
    for h in range(N_Q):
        acc = jnp.zeros((BLOCK, 2 * BLOCK), F32)
        for b in range(REL_BUCKETS):
            acc = jnp.where(bucket == b, table_ref[b, h], acc)
        kh, rows = _head_rows(h)
        bias_ref[kh, rows, :] = acc


def _window_masks(i):
    row = lax.broadcasted_iota(jnp.int32, (ROWS4, BLOCK), 0) & (BLOCK - 1)
    col = lax.broadcasted_iota(jnp.int32, (ROWS4, BLOCK), 1)
    return (col > row) & (i > 0), col <= row


def _stack_heads(ref, kh, offset):
    first = offset + kh * Q_PER_KV * HEAD_DIM
    return jnp.concatenate(
        [ref[:, first + g * HEAD_DIM: first + (g + 1) * HEAD_DIM].astype(BF) for g in range(Q_PER_KV)], axis=0)


def _stack_sinks(sink_ref, kh):
    return jnp.concatenate([jnp.full((BLOCK, 1), sink_ref[0, kh * Q_PER_KV + g], F32) for g in range(Q_PER_KV)], axis=0)


def _tril_bf16(w_ref, g):
    row = lax.broadcasted_iota(jnp.int32, (BLOCK, BLOCK), 0)
    col = lax.broadcasted_iota(jnp.int32, (BLOCK, BLOCK), 1)
    return jnp.where(col <= row, w_ref[g], 0.0).astype(BF)


def _attn_probs(q_h, k_prev, k_cur, bias_h, sink, valid_prev, valid_cur):
    l_prev = jnp.where(valid_prev, _nt(q_h, k_prev) * SCALE + bias_h[:, :BLOCK], NEG_INF)
    l_cur = jnp.where(valid_cur, _nt(q_h, k_cur) * SCALE + bias_h[:, BLOCK:], NEG_INF)
    m = jnp.maximum(jnp.maximum(jnp.max(l_prev, axis=-1, keepdims=True), jnp.max(l_cur, axis=-1, keepdims=True)), sink)
    e_prev = jnp.exp(l_prev - m)
    e_cur = jnp.exp(l_cur - m)
    e_sink = jnp.exp(sink - m)
    denom = jnp.sum(e_prev, axis=-1, keepdims=True) + jnp.sum(e_cur, axis=-1, keepdims=True) + e_sink
    return e_prev / denom, e_cur / denom, e_sink / denom


def _mixer_specs(nb):
    cl = lambda i: jnp.minimum(i, nb - 1)
    return [
        pl.BlockSpec((BLOCK, 2 * D_GMLP), lambda i: (cl(i), 0)),
        pl.BlockSpec((BLOCK, D_ATTN), lambda i: (cl(i), 0)),
        pl.BlockSpec((BLOCK, 2 * D_KV), lambda i: (cl(i), D_ATTN // (2 * D_KV))),
        pl.BlockSpec((BLOCK, 2 * D_KV), lambda i: (jnp.maximum(cl(i) - 1, 0), D_ATTN // (2 * D_KV))),
        pl.BlockSpec((1, D_GMLP), lambda i: (0, 0)),
        pl.BlockSpec((GROUPS, BLOCK, BLOCK), lambda i: (0, 0, 0)),
        pl.BlockSpec((GROUPS, BLOCK, 1), lambda i: (0, 0, 0)),
        pl.BlockSpec(memory_space=pltpu.SMEM),
        pl.BlockSpec(memory_space=pltpu.SMEM),
        pl.BlockSpec((BLOCK, 2 * BLOCK), lambda i: (0, 0)),
    ]


def _mixer_fwd(zuv, qkv, gv, w_sp, b_sp, sinks, table, bucket, after=None):
    s = zuv.shape[0]
    nb = s // BLOCK

    def body(zuv_ref, q_ref, kvc_ref, kvp_ref, gv_ref, w_ref, b_ref, sink_ref, table_ref, bucket_ref,
             mix_ref, tanh_ref, prob_ref, psink_ref, bias_ref):
        i = pl.program_id(0)

        @pl.when(i == 0)
        def _():
            _build_bias(bias_ref, bucket_ref, table_ref)

        t = _gelu_tanh(zuv_ref[...])
        tanh_ref[...] = t
        u = _gelu(zuv_ref[:, :D_GMLP], t[:, :D_GMLP])
        vg = _gelu(zuv_ref[:, D_GMLP:], t[:, D_GMLP:])
        for g in range(GROUPS):
            sl = slice(g * GDIM, (g + 1) * GDIM)
            vg_g = vg[:, sl]
            vn = ((vg_g * _rms_scale(vg_g)) * gv_ref[:, sl]).astype(BF)
            sv = _nn(_tril_bf16(w_ref, g), vn) + b_ref[g]
            mix_ref[:, sl] = (u[:, sl] * sv).astype(BF)

        valid_prev, valid_cur = _window_masks(i)
        for kh in range(N_KV):
            ksl = slice(kh * HEAD_DIM, (kh + 1) * HEAD_DIM)
            vsl = slice(D_KV + kh * HEAD_DIM, D_KV + (kh + 1) * HEAD_DIM)
            q4 = _stack_heads(q_ref, kh, 0)
            p_prev, p_cur, p_sink = _attn_probs(
                q4, kvp_ref[:, ksl], kvc_ref[:, ksl], bias_ref[kh], _stack_sinks(sink_ref, kh), valid_prev, valid_cur)
            prob_ref[0, kh, :, :BLOCK] = p_prev
            prob_ref[0, kh, :, BLOCK:] = p_cur
            psink_ref[0, kh] = jnp.broadcast_to(p_sink, (ROWS4, LANES))
            o4 = _nn(p_prev.astype(BF), kvp_ref[:, vsl]) + _nn(p_cur.astype(BF), kvc_ref[:, vsl])
            for g in range(Q_PER_KV):
                first = D_GMLP + (kh * Q_PER_KV + g) * HEAD_DIM
                mix_ref[:, first:first + HEAD_DIM] = o4[g * BLOCK:(g + 1) * BLOCK].astype(BF)

    return _call(
        body,
        name="mixer_fwd",
        after=after,
        grid=(nb,),
        in_specs=_mixer_specs(nb),
        out_specs=[
            pl.BlockSpec((BLOCK, D_MODEL), lambda i: (i, 0)),
            pl.BlockSpec((BLOCK, 2 * D_GMLP), lambda i: (i, 0)),
            pl.BlockSpec((1, N_KV, ROWS4, 2 * BLOCK), lambda i: (i, 0, 0, 0)),
            pl.BlockSpec((1, N_KV, ROWS4, LANES), lambda i: (i, 0, 0, 0)),
        ],
        out_shape=[
            jax.ShapeDtypeStruct((s, D_MODEL), BF),
            jax.ShapeDtypeStruct((s, 2 * D_GMLP), F32),
            jax.ShapeDtypeStruct((nb, N_KV, ROWS4, 2 * BLOCK), F32),
            jax.ShapeDtypeStruct((nb, N_KV, ROWS4, LANES), F32),
        ],
        scratch_shapes=[pltpu.VMEM((N_KV, ROWS4, 2 * BLOCK), F32)],
        compiler_params=_params(("arbitrary",)),
    )(zuv, qkv, qkv, qkv, gv, w_sp, b_sp, sinks, table, bucket)


def _tail(x, mix, p, t, g2, gf, w_out, w_ff1, w_ff2, w_gate, w_proj):
    s = x.shape[0]
    tm = min(TOK_TILE, s)
    n_ff = w_ff1.shape[0]
    fc = D_FF // n_ff

    def body(x_ref, mix_ref, p_ref, t_ref, g2_ref, gf_ref, wo_ref, w1_ref, w2_ref, wg_ref, wp_ref,
             small_ref, dh1_ref, dh1b_ref, dmix_ref, hn2_ref, a_ref, df_ref, h2_ref, dh2_ref, dgl_ref, dpp_ref, f_ref):
        i = pl.program_id(0)

        @pl.when(i == 0)
        def _():
            small_ref[...] = jnp.zeros_like(small_ref)

        h1 = x_ref[...] + _nn(mix_ref[...], wo_ref[...])
        r2 = _rms_scale(h1)
        hn2 = ((h1 * r2) * g2_ref[...]).astype(BF)
        hn2_ref[...] = hn2
        h2 = h1
        for c in range(n_ff):
            f = _nn(hn2, w1_ref[c])
            f_ref[:, c * fc:(c + 1) * fc] = f
            a = jnp.square(jnp.maximum(f, 0.0)).astype(BF)
            a_ref[:, c * fc:(c + 1) * fc] = a
            h2 = h2 + _nn(a, w2_ref[c * fc:(c + 1) * fc, :])
        h2b = h2.astype(BF)
        h2_ref[...] = h2b
        gate = jax.nn.sigmoid(_nn(h2b, wg_ref[...]))
        pp = _nn(p_ref[...].astype(BF), wp_ref[...])
        h3 = h2 + gate * pp
        rf = _rms_scale(h3)
        gf_v = gf_ref[...]
        err = (h3 * rf) * gf_v - t_ref[...]
        small_ref[2:3, :] += jnp.sum(jnp.sum(err * err, axis=-1, keepdims=True), axis=0, keepdims=True) * (0.5 / D_MODEL)

        dy = err * (1.0 / D_MODEL)
        small_ref[1:2, :] += jnp.sum(dy * (h3 * rf), axis=0, keepdims=True)
        dh3 = _rms_bwd(dy * gf_v, h3, rf)
        dpp_ref[...] = (dh3 * gate).astype(BF)
        dgl = ((dh3 * pp) * (gate * (1.0 - gate))).astype(BF)
        dgl_ref[...] = dgl
        dh2 = dh3 + _nt(dgl, wg_ref[...])
        dh2b = dh2.astype(BF)
        dh2_ref[...] = dh2b
        dhn2 = jnp.zeros((tm, D_MODEL), F32)
        for c in range(n_ff):
            da = _nt(dh2b, w2_ref[c * fc:(c + 1) * fc, :])
            df = (da * (2.0 * jnp.maximum(f_ref[:, c * fc:(c + 1) * fc], 0.0))).astype(BF)
            df_ref[:, c * fc:(c + 1) * fc] = df
            dhn2 = dhn2 + _nt(df, w1_ref[c])
        small_ref[0:1, :] += jnp.sum(dhn2 * (h1 * r2), axis=0, keepdims=True)
        dh1 = dh2 + _rms_bwd(dhn2 * g2_ref[...], h1, r2)
        dh1_ref[...] = dh1
        dh1b = dh1.astype(BF)
        dh1b_ref[...] = dh1b
        dmix_ref[...] = _nt(dh1b, wo_ref[...])

    tile = lambda cols: pl.BlockSpec((tm, cols), lambda i: (i, 0))
    whole = lambda shape: pl.BlockSpec(shape, lambda i: (0,) * len(shape), pipeline_mode=pl.Buffered(1))
    row = pl.BlockSpec((1, D_MODEL), lambda i: (0, 0))
    act = lambda cols, dt: jax.ShapeDtypeStruct((s, cols), dt)
    return _call(
        body,
        name="tail",
        grid=(s // tm,),
        in_specs=[tile(D_MODEL), tile(D_MODEL), tile(PLE_DIM), tile(D_MODEL), row, row,
                  whole(w_out.shape), whole(w_ff1.shape), whole(w_ff2.shape), whole(w_gate.shape), whole(w_proj.shape)],
        out_specs=[pl.BlockSpec((8, D_MODEL), lambda i: (0, 0)), tile(D_MODEL), tile(D_MODEL), tile(D_MODEL), tile(D_MODEL), tile(D_FF),
                   tile(D_FF), tile(D_MODEL), tile(D_MODEL), tile(D_MODEL), tile(D_MODEL)],
        out_shape=[jax.ShapeDtypeStruct((8, D_MODEL), F32),
                   act(D_MODEL, F32), act(D_MODEL, BF), act(D_MODEL, F32), act(D_MODEL, BF), act(D_FF, BF), act(D_FF, BF), act(D_MODEL, BF),
                   act(D_MODEL, BF), act(D_MODEL, BF), act(D_MODEL, BF)],
        scratch_shapes=[pltpu.VMEM((tm, D_FF), F32)],
        compiler_params=_params(("arbitrary",)),
    )(x, mix, p, t, g2, gf, w_out, w_ff1, w_ff2, w_gate, w_proj)


def _mixer_bwd(zuv, qkv, dmix, saved, gv, w_sp, b_sp, bucket, after=None):
    s = zuv.shape[0]
    nb = s // BLOCK

    def body(zuv_ref, q_ref, kvc_ref, kvp_ref, gv_ref, w_ref, b_ref, bucket_ref, dmix_ref, tanh_ref, prob_ref, psink_ref,
             dzm_ref, dkv_ref, dgv_ref, dw_ref, db_ref, dattn_ref,
             dbias_ref, carry_ref, dsink_acc, db_acc):
        i = pl.program_id(0)

        @pl.when(i == 0)
        def _():
            dbias_ref[...] = jnp.zeros_like(dbias_ref)
            carry_ref[...] = jnp.zeros_like(carry_ref)
            dsink_acc[...] = jnp.zeros_like(dsink_acc)
            dgv_ref[...] = jnp.zeros_like(dgv_ref)
            dw_ref[...] = jnp.zeros_like(dw_ref)
            db_acc[...] = jnp.zeros_like(db_acc)

        @pl.when(i < nb)
        def _():
            u, du_dz = _gelu_and_grad(zuv_ref[:, :D_GMLP], tanh_ref[:, :D_GMLP])
            vg, dvg_dz = _gelu_and_grad(zuv_ref[:, D_GMLP:], tanh_ref[:, D_GMLP:])
            for g in range(GROUPS):
                sl = slice(g * GDIM, (g + 1) * GDIM)
                vg_g = vg[:, sl]
                rg = _rms_scale(vg_g)
                vhat = vg_g * rg
                gain = gv_ref[:, sl]
                vn = (vhat * gain).astype(BF)
                w_g = _tril_bf16(w_ref, g)
                sv = _nn(w_g, vn) + b_ref[g]
                dmix_a = dmix_ref[:, sl]
                dsv = dmix_a * u[:, sl]
                dsvb = dsv.astype(BF)
                db_acc[g] += jnp.sum(dsv, axis=-1, keepdims=True)
                dw_ref[g] += _nt(dsvb, vn)
                dvn = _tn(w_g, dsvb)
                dgv_ref[:, sl] += jnp.sum(dvn * vhat, axis=0, keepdims=True)
                dvg = _rms_bwd(dvn * gain, vg_g, rg)
                dzm_ref[:, sl] = ((dmix_a * sv) * du_dz[:, sl]).astype(BF)
                dzm_ref[:, D_GMLP + g * GDIM: D_GMLP + (g + 1) * GDIM] = (dvg * dvg_dz[:, sl]).astype(BF)

            for kh in range(N_KV):
                ksl = slice(kh * HEAD_DIM, (kh + 1) * HEAD_DIM)
                vsl = slice(D_KV + kh * HEAD_DIM, D_KV + (kh + 1) * HEAD_DIM)
                k_prev, k_cur = kvp_ref[:, ksl], kvc_ref[:, ksl]
                v_prev, v_cur = kvp_ref[:, vsl], kvc_ref[:, vsl]
                q4 = _stack_heads(q_ref, kh, 0)
                p_prev, p_cur, p_sink = prob_ref[0, kh, :, :BLOCK], prob_ref[0, kh, :, BLOCK:], psink_ref[0, kh, :, 0:1]
                do4 = _stack_heads(dmix_ref, kh, D_GMLP)
                dp_prev = _nt(do4, v_prev)
                dp_cur = _nt(do4, v_cur)
                delta = jnp.sum(p_prev * dp_prev, axis=-1, keepdims=True) + jnp.sum(p_cur * dp_cur, axis=-1, keepdims=True)
                ds_prev = p_prev * (dp_prev - delta)
                ds_cur = p_cur * (dp_cur - delta)
                dsink_acc[kh] -= p_sink * delta
                dbias_ref[kh, :, :BLOCK] += ds_prev
                dbias_ref[kh, :, BLOCK:] += ds_cur
                dsb_prev = ds_prev.astype(BF)
                dsb_cur = ds_cur.astype(BF)
                dq4 = (_nn(dsb_prev, k_prev) + _nn(dsb_cur, k_cur)) * SCALE
                for g in range(Q_PER_KV):
                    first = 2 * D_GMLP + (kh * Q_PER_KV + g) * HEAD_DIM
                    dzm_ref[:, first:first + HEAD_DIM] = dq4[g * BLOCK:(g + 1) * BLOCK].astype(BF)
                dkv_ref[:, ksl] = (carry_ref[:, ksl] + _tn(dsb_prev, q4) * SCALE).astype(BF)
                dkv_ref[:, vsl] = (carry_ref[:, vsl] + _tn(p_prev.astype(BF), do4)).astype(BF)
                carry_ref[:, ksl] = _tn(dsb_cur, q4) * SCALE
                carry_ref[:, vsl] = _tn(p_cur.astype(BF), do4)

        @pl.when(i == nb)
        def _():
            dkv_ref[...] = carry_ref[...].astype(BF)
            row = lax.broadcasted_iota(jnp.int32, (BLOCK, BLOCK), 0)
            col = lax.broadcasted_iota(jnp.int32, (BLOCK, BLOCK), 1)
            for g in range(GROUPS):
                dw_ref[g] = jnp.where(col <= row, dw_ref[g], 0.0)
                db_ref[g:g + 1, :] = jnp.sum(jnp.where(col == row, db_acc[g], 0.0), axis=0, keepdims=True)
            bucket = bucket_ref[...]
            for h in range(N_Q):
                kh, rows = _head_rows(h)
                dattn_ref[REL_BUCKETS, h] = jnp.sum(dsink_acc[kh, rows, :])
                dbh = dbias_ref[kh, rows, :]
                for b in range(REL_BUCKETS):
                    dattn_ref[b, h] = jnp.sum(jnp.where(bucket == b, dbh, 0.0))

    cl = lambda i: jnp.minimum(i, nb - 1)
    const = lambda shape: pl.BlockSpec(shape, lambda i: (0,) * len(shape))
    return _call(
        body,
        name="mixer_bwd",
        after=after,
        grid=(nb + 1,),
        in_specs=_mixer_specs(nb)[:7] + [
            const((BLOCK, 2 * BLOCK)),
            pl.BlockSpec((BLOCK, D_MODEL), lambda i: (cl(i), 0)),
            pl.BlockSpec((BLOCK, 2 * D_GMLP), lambda i: (cl(i), 0)),
            pl.BlockSpec((1, N_KV, ROWS4, 2 * BLOCK), lambda i: (cl(i), 0, 0, 0)),
            pl.BlockSpec((1, N_KV, ROWS4, LANES), lambda i: (cl(i), 0, 0, 0)),
        ],
        out_specs=[
            pl.BlockSpec((BLOCK, D_MAIN), lambda i: (cl(i), 0)),
            pl.BlockSpec((BLOCK, 2 * D_KV), lambda i: (jnp.maximum(i - 1, 0), 0)),
            const((1, D_GMLP)),
            const((GROUPS, BLOCK, BLOCK)),
            const((GROUPS, BLOCK)),
            pl.BlockSpec(memory_space=pltpu.SMEM),
        ],
        out_shape=[
            jax.ShapeDtypeStruct((s, D_MAIN), BF),
            jax.ShapeDtypeStruct((s, 2 * D_KV), BF),
            jax.ShapeDtypeStruct((1, D_GMLP), F32),
            jax.ShapeDtypeStruct((GROUPS, BLOCK, BLOCK), F32),
            jax.ShapeDtypeStruct((GROUPS, BLOCK), F32),
            jax.ShapeDtypeStruct((REL_BUCKETS + 1, N_Q), F32),
        ],
        scratch_shapes=[
            pltpu.VMEM((N_KV, ROWS4, 2 * BLOCK), F32),
            pltpu.VMEM((BLOCK, 2 * D_KV), F32),
            pltpu.VMEM((N_KV, ROWS4, 1), F32),
            pltpu.VMEM((GROUPS, BLOCK, 1), F32),
        ],
        compiler_params=_params(("arbitrary",)),
    )(zuv, qkv, qkv, qkv, gv, w_sp, b_sp, bucket, dmix, *saved)


def _in_bwd(x, dh1, dzm, dkv, g1, w_in_t, after=None):
    s = x.shape[0]
    tm = min(IO_TOK_TILE, s)

    def body(x_ref, dh1_ref, dzm_ref, dkv_ref, g_ref, w_ref, dx_ref, dg_ref):
        @pl.when(pl.program_id(0) == 0)
        def _():
            dg_ref[...] = jnp.zeros_like(dg_ref)

        dhn = _nn(dzm_ref[...], w_ref[:D_MAIN, :]) + _nn(dkv_ref[...], w_ref[D_MAIN:, :])
        xv = x_ref[...]
        r = _rms_scale(xv)
        dg_ref[...] += jnp.sum(dhn * (xv * r), axis=0, keepdims=True)
        dx_ref[...] = dh1_ref[...] + _rms_bwd(dhn * g_ref[...], xv, r)

    tile = lambda cols: pl.BlockSpec((tm, cols), lambda i: (i, 0))
    row = pl.BlockSpec((1, D_MODEL), lambda i: (0, 0))
    return _call(
        body,
        name="in_bwd",
        after=after,
        grid=(s // tm,),
        in_specs=[tile(D_MODEL), tile(D_MODEL), tile(D_MAIN), tile(2 * D_KV), row, pl.BlockSpec((D_IN, D_MODEL), lambda i: (0, 0))],
        out_specs=[tile(D_MODEL), row],
        out_shape=[jax.ShapeDtypeStruct((s, D_MODEL), F32), jax.ShapeDtypeStruct((1, D_MODEL), F32)],
        compiler_params=_params(("arbitrary",)),
    )(x, dh1, dzm, dkv, g1, w_in_t)


def _wgrad_in(dzm, dkv, hn, after=None):
    s = hn.shape[0]
    tm = 2 * D_KV
    n_main = D_MAIN // tm

    def body(dzm_ref, dkv_ref, hn_ref, o_ref):
        i = pl.program_id(0)

        @pl.when(i < n_main)
        def _():
            o_ref[...] = _tn(dzm_ref[...], hn_ref[...])

        @pl.when(i == n_main)
        def _():
            o_ref[...] = _tn(dkv_ref[...], hn_ref[...])

    return _call(
        body,
        name="wgrad_in",
        after=after,
        grid=(n_main + 1,),
        in_specs=[
            pl.BlockSpec((s, tm), lambda i: (0, jnp.minimum(i, n_main - 1))),
            pl.BlockSpec((s, tm), lambda i: (0, 0)),
            pl.BlockSpec((s, D_MODEL), lambda i: (0, 0)),
        ],
        out_specs=pl.BlockSpec((tm, D_MODEL), lambda i: (i, 0)),
        out_shape=jax.ShapeDtypeStruct((D_IN, D_MODEL), F32),
        compiler_params=_params(("arbitrary",)),
    )(dzm, dkv, hn)


def _wgrad(a, b, tm, tn, name, peer_cols=0, after=None):
    s, m = a.shape
    n = b.shape[1]

    def body(a_ref, b_ref, o_ref, at_ref):
        @pl.when(pl.program_id(1) == 0)
        def _():
            at_ref[...] = a_ref[...].astype(BF).T

        r = _nn(at_ref[...], b_ref[...])
        if peer_cols:
            for q in range(tn // peer_cols):
                o_ref[q] = r[:, q * peer_cols:(q + 1) * peer_cols]
        else:
            o_ref[...] = r

    if peer_cols:
        out_spec = pl.BlockSpec((tn // peer_cols, tm, peer_cols), lambda i, j: (j, i, 0))
        out_shape = jax.ShapeDtypeStruct((n // peer_cols, m, peer_cols), F32)
    else:
        out_spec = pl.BlockSpec((tm, tn), lambda i, j: (i, j))
        out_shape = jax.ShapeDtypeStruct((m, n), F32)
    return _call(
        body,
        name=name,
        after=after,
        grid=(m // tm, n // tn),
        in_specs=[pl.BlockSpec((s, tm), lambda i, j: (0, i)), pl.BlockSpec((s, tn), lambda i, j: (0, j))],
        out_specs=out_spec,
        out_shape=out_shape,
        scratch_shapes=[pltpu.VMEM((tm, s), BF)],
        compiler_params=_params(("arbitrary", "arbitrary")),
    )(a, b)


def _adamw_math(w, g, m, v):
    m_new = ADAM_B1 * m + (1.0 - ADAM_B1) * g
    v_new = ADAM_B2 * v + (1.0 - ADAM_B2) * jnp.square(g)
    m_hat = m_new / (1.0 - ADAM_B1 ** ADAM_STEP)
    v_hat = v_new / (1.0 - ADAM_B2 ** ADAM_STEP)
    delta = -ADAM_LR * (m_hat / (jnp.sqrt(v_hat) + ADAM_EPS) + ADAM_WD * w)
    return delta, m_new, v_new


def _final_adamw(part, recv, w, m, v, name):
    r, c = w.shape
    tr = min(r, 512)

    def body(p_ref, r_ref, w_ref, m_ref, v_ref, g_ref, d_ref, mo_ref, vo_ref):
        g = p_ref[...]
        for j in range(3):
            g = g + r_ref[j].astype(F32)
        g_ref[...] = g
        d_ref[...], mo_ref[...], vo_ref[...] = _adamw_math(w_ref[...], g, m_ref[...], v_ref[...])

    spec = pl.BlockSpec((tr, c), lambda i: (i, 0))
    return _call(
        body,
        name=name,
        grid=(r // tr,),
        in_specs=[spec, pl.BlockSpec((3, tr, c), lambda i: (0, i, 0)), spec, spec, spec],
        out_specs=[spec] * 4,
        out_shape=[jax.ShapeDtypeStruct((r, c), F32)] * 4,
        compiler_params=_params(("arbitrary",)),
    )(part, recv, w, m, v)


def _rs_sum(g, land, blocks, name):
    _, r, c = g.shape
    tr = min(r, 256)

    def body(blk_ref, g0_ref, g1_ref, g2_ref, g3_ref, l_ref, part_ref, send_ref):
        part_ref[...] = g0_ref[0] + l_ref[0]
        for j, gj_ref in enumerate((g1_ref, g2_ref, g3_ref)):
            send_ref[j] = (gj_ref[0] + l_ref[j + 1]).astype(BF)

    def pick(j):
        return pl.BlockSpec((1, tr, c), lambda i, blk: (blk[j], i, 0))

    return _call(
        body,
        name=name,
        grid_spec=pltpu.PrefetchScalarGridSpec(
            num_scalar_prefetch=1,
            grid=(r // tr,),
            in_specs=[pick(0), pick(1), pick(2), pick(3), pl.BlockSpec((4, tr, c), lambda i, blk: (0, i, 0))],
            out_specs=[pl.BlockSpec((tr, c), lambda i, blk: (i, 0)), pl.BlockSpec((3, tr, c), lambda i, blk: (0, i, 0))],
        ),
        out_shape=[jax.ShapeDtypeStruct((r, c), F32), jax.ShapeDtypeStruct((3, r, c), BF)],
        compiler_params=_params(("arbitrary",)),
    )(blocks, g, g, g, g, land)


def _adamw_small(own, recv, weights, moms, vels, after):
    n_w = len(weights)

    def body(*refs):
        own_refs, recv_refs = refs[:6], refs[6:12]
        w_refs, m_refs, v_refs = refs[12:12 + n_w], refs[12 + n_w:12 + 2 * n_w], refs[12 + 2 * n_w:12 + 3 * n_w]
        outs = refs[12 + 3 * n_w:]
        loss_ref, g_refs, d_refs = outs[0], outs[1:1 + n_w], outs[1 + n_w:1 + 2 * n_w]
        mo_refs, vo_refs = outs[1 + 2 * n_w:1 + 3 * n_w], outs[1 + 3 * n_w:]
        x, y = lax.axis_index("x"), lax.axis_index("y")

        def total(k, *index):
            index = index or (slice(None),) * (len(own_refs[k].shape) - 1)
            across = [own_refs[k][(0, *index)]] + [recv_refs[k][(j, *index)] for j in range(3)]
            acc = None
            for cx in range(2):
                for cy in range(2):
                    flip = (cx ^ x) + 2 * (cy ^ y)
                    part = jnp.where(flip == 0, across[0], jnp.where(flip == 1, across[1], jnp.where(flip == 2, across[2], across[3])))
                    acc = part if acc is None else acc + part
            return acc

        grads = [
            total(1), total(2), total(3), total(4),
            total(5, slice(REL_BUCKETS, None), slice(None)), total(5, slice(None, REL_BUCKETS), slice(None)),
            total(0, slice(0, 1), slice(None)), total(0, slice(1, 2), slice(None))]
        loss_ref[...] = total(0, slice(2, 3), slice(0, 1))
        for k in range(n_w):
            g_refs[k][...] = grads[k]
            d_refs[k][...], mo_refs[k][...], vo_refs[k][...] = _adamw_math(w_refs[k][...], grads[k], m_refs[k][...], v_refs[k][...])

    shapes = [jax.ShapeDtypeStruct(w.shape, F32) for w in weights]
    outs = _call(
        body,
        name="adamw_small",
        after=after,
        in_specs=[pl.BlockSpec(memory_space=pltpu.VMEM)] * (12 + 3 * n_w),
        out_shape=[jax.ShapeDtypeStruct((1, 1), F32)] + shapes * 4,
        compiler_params=_params(),
    )(*own, *recv, *weights, *moms, *vels)
    return outs[0], outs[1:1 + n_w], outs[1 + n_w:1 + 2 * n_w], outs[1 + 2 * n_w:1 + 3 * n_w], outs[1 + 3 * n_w:]


def _place():
    x, y, c = lax.axis_index("x"), lax.axis_index("y"), lax.axis_index("c")
    return x, y, c, [(1 - x, y), (x, 1 - y), (1 - x, 1 - y)]


def _dev_index(px, py, pc):
    return 4 * px + 2 * py + pc


def _all_gather(shards, out_dtype, name, after=None):
    k_n = len(shards)

    def body(*refs):
        ins, outs, stage = refs[:k_n], refs[k_n:2 * k_n], refs[2 * k_n:3 * k_n]
        send_sems, recv_sems, local_sems = refs[3 * k_n:]
        x, y, c, chips = _place()
        me, sibling = (x, y, c), (x, y, 1 - c)

        def copy(k, j, block, to, src=None):
            dst = outs[k].at[_dev_index(*block)]
            return pltpu.make_async_remote_copy(
                src_ref=dst if src is None else src, dst_ref=dst,
                send_sem=send_sems.at[k, j], recv_sem=recv_sems.at[k, j], device_id=to, device_id_type=MESH)

        mine, first = [], []
        for k in range(k_n):
            stage[k][...] = ins[k][...].astype(out_dtype)
            mine.append(pltpu.make_async_copy(stage[k], outs[k].at[_dev_index(*me)], local_sems.at[k]))
            mine[k].start()
            first.append(copy(k, 0, me, sibling, src=stage[k]))
            first += [copy(k, 1 + j, me, (*chip, c), src=stage[k]) for j, chip in enumerate(chips)]
        for cp in first:
            cp.start()
        passed = []
        for k in range(k_n):
            for j, chip in enumerate(chips):
                copy(k, 1 + j, (*chip, c), me).wait_recv()
                passed.append(copy(k, 4 + j, (*chip, c), sibling))
                passed[-1].start()
        for k in range(k_n):
            copy(k, 0, sibling, me).wait_recv()
            for j, chip in enumerate(chips):
                copy(k, 4 + j, (*chip, 1 - c), me).wait_recv()
        for cp in first + passed:
            cp.wait_send()
        for cp in mine:
            cp.wait()

    return _call(
        body,
        name=name,
        after=after,
        in_specs=[pl.BlockSpec(memory_space=pltpu.VMEM)] * k_n,
        out_specs=[pl.BlockSpec(memory_space=pl.ANY)] * k_n,
        out_shape=[jax.ShapeDtypeStruct((N_DEV,) + sh.shape, out_dtype) for sh in shards],
        scratch_shapes=[pltpu.VMEM(sh.shape, out_dtype) for sh in shards]
        + [pltpu.SemaphoreType.DMA((k_n, 7)), pltpu.SemaphoreType.DMA((k_n, 7)), pltpu.SemaphoreType.DMA((k_n,))],
        compiler_params=_params(),
    )(*shards)


HBM_SPEC = pl.BlockSpec(memory_space=pltpu.HBM)
SEM_SPEC = pl.BlockSpec(memory_space=pltpu.SEMAPHORE)
ANY_SPEC = pl.BlockSpec(memory_space=pl.ANY)
DATAFLOW = pltpu.SideEffectType.DATAFLOW_SIDE_EFFECTING


def _hbm(a):
    return pltpu.with_memory_space_constraint(a, pltpu.HBM)


def _prep_weights(shards):
    k_n = len(shards)

    def body(*refs):
        ins, outs, stage, sems = refs[:k_n], refs[k_n:2 * k_n], refs[2 * k_n:3 * k_n], refs[3 * k_n]
        x, y, c, _ = _place()
        copies = []
        for k in range(k_n):
            stage[k][...] = ins[k][...].astype(BF)
            copies.append(pltpu.make_async_copy(stage[k], outs[k].at[_dev_index(x, y, c)], sems.at[k]))
            copies[k].start()
        for cp in copies:
            cp.wait()

    return _call(
        body,
        name="prep_weights",
        in_specs=[pl.BlockSpec(memory_space=pltpu.VMEM)] * k_n,
        out_specs=[ANY_SPEC] * k_n,
        out_shape=[jax.ShapeDtypeStruct((N_DEV,) + sh.shape, BF) for sh in shards],
        scratch_shapes=[pltpu.VMEM(sh.shape, BF) for sh in shards] + [pltpu.SemaphoreType.DMA((k_n,))],
        compiler_params=_params(),
    )(*shards)


def _ag_piece(land_k, block, half, peer, send_sem, recv_sem):
    ref = land_k.at[_dev_index(*block)]
    if half is not None:
        rows = land_k.shape[1] // 2
        ref = ref.at[pl.ds(half * rows, rows)]
    return pltpu.make_async_remote_copy(
        src_ref=ref, dst_ref=ref, send_sem=send_sem, recv_sem=recv_sem, device_id=peer, device_id_type=MESH)


def _ag_plan():
    x, y, c, _ = _place()
    me, sib = (x, y, c), (x, y, 1 - c)
    xn, yn, diag = (1 - x, y, c), (x, 1 - y, c), (1 - x, 1 - y, c)
    return dict(
        relay_halves=[(me, 0, xn), (me, 1, yn)],
        others=[(me, None, sib), (me, 1, xn), (me, 0, yn)],
        relays=[(xn, 0, yn), (yn, 1, xn)],
        near=[(xn, None, sib), (yn, None, sib)],
        far=[(diag, None, sib)],
    )


def _ag_stage(land, stage, send_sems, recv_sems, act):
    copies = _ag_plan()[stage]
    n = len(copies)
    for k in range(len(land)):
        for j, (block, half, peer) in enumerate(copies):
            cp = _ag_piece(land[k], block, half, peer, send_sems.at[n * k + j], recv_sems.at[n * k + j])
            if act == "start":
                cp.start()
            else:
                cp.wait_send()
                cp.wait_recv()


def _sem_shapes(*counts):
    return [pltpu.SemaphoreType.DMA((n,)) for n in counts for _ in range(2)]


def _ag_start(first, rest):
    lands = list(first) + list(rest)
    k_n, k_first = len(lands), len(first)
    k_rest = k_n - k_first

    def body(*refs):
        land = refs[:k_n]
        sems = refs[k_n:k_n + 6]
        token = refs[-1]
        x, y, c, chips = _place()
        targets = [(x, y, 1 - c)] + [(*chip, c) for chip in chips]
        for k in range(k_first):
            for j, to in enumerate(targets):
                _ag_piece(land[k], (x, y, c), None, to, sems[0].at[4 * k + j], sems[1].at[4 * k + j]).start()
        _ag_stage(land[k_first:], "relay_halves", sems[2], sems[3], "start")
        _ag_stage(land[k_first:], "others", sems[4], sems[5], "start")
        token[...] = jnp.zeros_like(token)

    outs = pl.pallas_call(
        body,
        name="ag_start",
        in_specs=[HBM_SPEC] * k_n,
        out_specs=(*[SEM_SPEC] * 6, *[HBM_SPEC] * k_n, pl.BlockSpec(memory_space=pltpu.VMEM)),
        out_shape=(*_sem_shapes(4 * k_first, 2 * k_rest, 3 * k_rest),
                   *[pltpu.HBM(a.shape, a.dtype) for a in lands], jax.ShapeDtypeStruct((8, LANES), F32)),
        input_output_aliases={k: 6 + k for k in range(k_n)},
        compiler_params=pltpu.CompilerParams(has_side_effects=DATAFLOW),
    )(*[_hbm(a) for a in lands])
    flying = list(outs[6:6 + k_n])
    return (outs[0], outs[1], flying[:k_first]), (outs[2:6], flying[k_first:]), outs[-1]


def _ag_split_call(lands, waits, starts, after, name):
    k_n = len(lands)
    plan_sizes = dict(relay_halves=2, others=3, relays=2, near=2, far=1)
    n_in, n_out = 2 * len(waits), 2 * len(starts)

    def body(*refs):
        land = refs[:k_n]
        in_sems = refs[k_n:k_n + n_in]
        out_sems, token = refs[len(refs) - 1 - n_out:len(refs) - 1], refs[-1]
        for w, (stage, _, _) in enumerate(waits):
            _ag_stage(land, stage, in_sems[2 * w], in_sems[2 * w + 1], "wait")
            if w < len(starts):
                _ag_stage(land, starts[w], out_sems[2 * w], out_sems[2 * w + 1], "start")
        token[...] = jnp.zeros_like(token)

    outs = pl.pallas_call(
        body,
        name=name,
        in_specs=[HBM_SPEC] * k_n + [SEM_SPEC] * n_in + [ANY_SPEC],
        out_specs=(*[HBM_SPEC] * k_n, *[SEM_SPEC] * n_out, pl.BlockSpec(memory_space=pltpu.VMEM)),
        out_shape=(*[pltpu.HBM(a.shape, a.dtype) for a in lands], *_sem_shapes(*[plan_sizes[s] * k_n for s in starts]),
                   jax.ShapeDtypeStruct((8, LANES), F32)),
        input_output_aliases={k: k for k in range(k_n)},
        compiler_params=pltpu.CompilerParams(has_side_effects=DATAFLOW),
    )(*lands, *[s for _, a, b in waits for s in (a, b)], after)
    return list(outs[:k_n]), list(outs[k_n:k_n + n_out]), outs[-1]


def _ag_mid(lands, send_sems, recv_sems, after, name):
    k_n = len(lands)

    def body(*refs):
        land = refs[:k_n]
        send1, recv1 = refs[k_n], refs[k_n + 1]
        fwd_send, fwd_recv = refs[-2], refs[-1]
        x, y, c, chips = _place()
        sources = [(x, y, 1 - c)] + [(*chip, c) for chip in chips]
        for k in range(k_n):
            mine = land[k].at[_dev_index(x, y, c)]
            for j, frm in enumerate(sources):
                got = land[k].at[_dev_index(*frm)]
                cp = pltpu.make_async_remote_copy(
                    src_ref=mine, dst_ref=got, send_sem=send1.at[4 * k + j], recv_sem=recv1.at[4 * k + j], device_id=frm, device_id_type=MESH)
                cp.wait_send()
                cp.wait_recv()
                if j >= 1:
                    pltpu.make_async_remote_copy(
                        src_ref=got, dst_ref=got, send_sem=fwd_send.at[3 * k + j - 1], recv_sem=fwd_recv.at[3 * k + j - 1],
                        device_id=(x, y, 1 - c), device_id_type=MESH).start()

    outs = pl.pallas_call(
        body,
        name=name,
        in_specs=[HBM_SPEC] * k_n + [SEM_SPEC, SEM_SPEC, ANY_SPEC],
        out_specs=(*[HBM_SPEC] * k_n, SEM_SPEC, SEM_SPEC),
        out_shape=(*[pltpu.HBM(a.shape, a.dtype) for a in lands], pltpu.SemaphoreType.DMA((3 * k_n,)), pltpu.SemaphoreType.DMA((3 * k_n,))),
        input_output_aliases={k: k for k in range(k_n)},
        compiler_params=pltpu.CompilerParams(has_side_effects=DATAFLOW),
    )(*lands, send_sems, recv_sems, after)
    return list(outs[:k_n]), outs[-2], outs[-1]


def _ag_end(lands, fwd_send, fwd_recv, name):
    k_n = len(lands)

    def body(*refs):
        land = refs[:k_n]
        fsend, frecv = refs[k_n], refs[k_n + 1]
        x, y, c, chips = _place()
        for k in range(k_n):
            for j, chip in enumerate(chips):
                cp = pltpu.make_async_remote_copy(
                    src_ref=land[k].at[_dev_index(*chip, c)], dst_ref=land[k].at[_dev_index(*chip, 1 - c)],
                    send_sem=fsend.at[3 * k + j], recv_sem=frecv.at[3 * k + j], device_id=(x, y, 1 - c), device_id_type=MESH)
                cp.wait_send()
                cp.wait_recv()

    outs = pl.pallas_call(
        body,
        name=name,
        in_specs=[HBM_SPEC] * k_n + [SEM_SPEC, SEM_SPEC],
        out_specs=tuple([HBM_SPEC] * k_n),
        out_shape=tuple(pltpu.HBM(a.shape, a.dtype) for a in lands),
        input_output_aliases={k: k for k in range(k_n)},
        compiler_params=pltpu.CompilerParams(has_side_effects=DATAFLOW),
    )(*lands, fwd_send, fwd_recv)
    return list(outs)


def _chips4():
    x, y, c, others = _place()
    return x, y, c, [(x, y)] + others


def _route_sibling(j):
    x, y, c, chips = _chips4()
    return _dev_index(*chips[j], 1 - c), j, (x, y, 1 - c)


def _route_chips(j):
    x, y, c, chips = _chips4()
    return j, j, (*chips[j + 1], c)


def _route_sibling_whole(j):
    x, y, c, _ = _chips4()
    return 0, 0, (x, y, 1 - c)


def _route_chips_whole(j):
    x, y, c, chips = _chips4()
    return 0, j, (*chips[j + 1], c)


def _xchg_copies(routes, src, dst, send_sems, recv_sems):
    copies, sem = [], 0
    for k, (route, n) in enumerate(routes):
        for j in range(n):
            si, di, peer = route(j)
            copies.append(pltpu.make_async_remote_copy(
                src_ref=src[k].at[si], dst_ref=dst[k].at[di], send_sem=send_sems.at[sem], recv_sem=recv_sems.at[sem],
                device_id=peer, device_id_type=MESH))
            sem += 1
    return copies


def _xchg_start(srcs, slot_shapes, routes, name):
    k_n = len(srcs)
    n_sem = sum(n for _, n in routes)
    dsts = [lax.empty((n,) + tuple(sh), a.dtype) for sh, a, (_, n) in zip(slot_shapes, srcs, routes)]

    def body(*refs):
        src, dst = refs[:k_n], refs[k_n:2 * k_n]
        send_sems, recv_sems, token = refs[2 * k_n], refs[2 * k_n + 1], refs[-1]
        for cp in _xchg_copies(routes, src, dst, send_sems, recv_sems):
            cp.start()
        token[...] = jnp.zeros_like(token)

    arrays = list(srcs) + dsts
    outs = pl.pallas_call(
        body,
        name=name,
        in_specs=[HBM_SPEC] * (2 * k_n),
        out_specs=(SEM_SPEC, SEM_SPEC, *[HBM_SPEC] * (2 * k_n), pl.BlockSpec(memory_space=pltpu.VMEM)),
        out_shape=(pltpu.SemaphoreType.DMA((n_sem,)), pltpu.SemaphoreType.DMA((n_sem,)),
                   *[pltpu.HBM(a.shape, a.dtype) for a in arrays], jax.ShapeDtypeStruct((8, LANES), F32)),
        input_output_aliases={i: 2 + i for i in range(2 * k_n)},
        compiler_params=pltpu.CompilerParams(has_side_effects=DATAFLOW),
    )(*[_hbm(a) for a in arrays])
    return outs[0], outs[1], list(outs[2:2 + k_n]), list(outs[2 + k_n:2 + 2 * k_n]), outs[-1]


def _xchg_wait(send_sems, recv_sems, srcs, dsts, routes, after, name):
    k_n = len(srcs)

    def body(*refs):
        src, dst = refs[:k_n], refs[k_n:2 * k_n]
        for cp in _xchg_copies(routes, src, dst, refs[2 * k_n], refs[2 * k_n + 1]):
            cp.wait_send()
            cp.wait_recv()

    arrays = list(srcs) + list(dsts)
    outs = pl.pallas_call(
        body,
        name=name,
        in_specs=[HBM_SPEC] * (2 * k_n) + [SEM_SPEC, SEM_SPEC, ANY_SPEC],
        out_specs=tuple([HBM_SPEC] * (2 * k_n)),
        out_shape=tuple(pltpu.HBM(a.shape, a.dtype) for a in arrays),
        input_output_aliases={i: i for i in range(2 * k_n)},
        compiler_params=pltpu.CompilerParams(has_side_effects=DATAFLOW),
    )(*arrays, send_sems, recv_sems, after)
    return list(outs[:k_n]), list(outs[k_n:])


SMALL = ("norm1_gain", "gmlp_v_gain", "w_spatial", "b_spatial", "attn_sinks", "rel_bias_table", "norm2_gain", "final_gain")
LANES = 128


def _swap_start(grads, smalls, tag):
    srcs = list(grads) + [a[None] for a in smalls]
    shapes = [g.shape[1:] for g in grads] + [a.shape for a in smalls]
    routes = [(_route_sibling, 4)] * len(grads) + [(_route_sibling_whole, 1)] * len(smalls)
    return _xchg_start(srcs, shapes, routes, f"rs_{tag}_swap_start"), routes, len(grads)


def _swap_sums(swap, names, after, tag):
    (send1, recv1, src1, land1, _), routes, n_rs = swap
    x, y, c, chips = _chips4()
    blocks = jnp.stack([_dev_index(*chip, c) for chip in chips]).astype(jnp.int32)
    src1, land1 = _xchg_wait(send1, recv1, src1, land1, routes, after, f"rs_{tag}_swap_wait")
    sums = [_rs_sum(g, land, blocks, f"rs_sum_{n}") for g, land, n in zip(src1[:n_rs], land1[:n_rs], names)]
    return sums, (_pair_sum(src1[n_rs:], land1[n_rs:], f"rs_{tag}_sum_small") if len(src1) > n_rs else [])


def _pair_sum(mine, theirs, name):
    def body(*refs):
        n = len(refs) // 3
        for k in range(n):
            refs[2 * n + k][...] = refs[k][...] + refs[n + k][...]

    return _call(
        body,
        name=name,
        out_shape=[jax.ShapeDtypeStruct(a.shape, F32) for a in mine],
        compiler_params=_params(),
    )(*mine, *theirs)


def _chips_start(sums, small_sums, tag):
    sends = [ps[1] for ps in sums] + list(small_sums)
    routes = [(_route_chips, 3)] * len(sums) + [(_route_chips_whole, 3)] * len(small_sums)
    return _xchg_start(sends, [a.shape[1:] for a in sends], routes, f"rs_{tag}_chips_start"), routes


def _chips_wait(chips, after, tag):
    (send2, recv2, src2, land2, _), routes = chips
    return _xchg_wait(send2, recv2, src2, land2, routes, after, f"rs_{tag}_chips_wait")[1]


def kernel(x, p, norm1_gain, w_in, gmlp_v_gain, w_spatial, b_spatial, attn_sinks, rel_bias_table, w_out, norm2_gain, w_ff1, w_ff2, w_ple_proj, w_ple_gate, final_gain, loss_target, m_norm1_gain, m_w_in, m_gmlp_v_gain, m_w_spatial, m_b_spatial, m_attn_sinks, m_rel_bias_table, m_w_out, m_norm2_gain, m_w_ff1, m_w_ff2, m_w_ple_proj, m_w_ple_gate, m_final_gain, v_norm1_gain, v_w_in, v_gmlp_v_gain, v_w_spatial, v_b_spatial, v_attn_sinks, v_rel_bias_table, v_w_out, v_norm2_gain, v_w_ff1, v_w_ff2, v_w_ple_proj, v_w_ple_gate, v_final_gain):
    args = dict(locals())
    s = x.shape[1]
    big = ("w_in", "w_out", "w_ff1", "w_ff2", "w_ple_proj", "w_ple_gate")

    x2, p2, t2 = x.reshape(s, D_MODEL), p.reshape(s, PLE_DIM), loss_target.reshape(s, D_MODEL)
    g1, gv, w_sp, b_sp, sinks, table, g2, gf = (args[n] for n in SMALL)
    bucket = jnp.asarray(_bucket_table())
    b_col = b_sp.reshape(GROUPS, BLOCK, 1)

    def shard(name):
        return args[name][0].T if name.endswith("w_in") else args[name][0]

    lands = _prep_weights([shard(n) for n in big])
    (send_in, recv_in, fly_in), (rest_sems, fly_rest), token = _ag_start(lands[:1], lands[1:])
    mid_in, fwd_send_in, fwd_recv_in = _ag_mid(fly_in, send_in, recv_in, token, "ag_mid_w_in")
    g_in = _ag_end(mid_in, fwd_send_in, fwd_recv_in, "ag_end_w_in")[0]
    full_in = g_in.reshape(D_IN, D_MODEL)

    zuv, qkv, hn1 = _in_proj(x2, g1, full_in)
    fly_rest, relay_sems, relayed = _ag_split_call(
        fly_rest, [("relay_halves", *rest_sems[:2])], ["relays"], zuv, "ag_relay")
    mix, *saved = _mixer_fwd(zuv, qkv, gv, w_sp[0], b_col, sinks, table, bucket, after=relayed)
    fly_rest, fwd_sems, _ = _ag_split_call(
        fly_rest, [("others", *rest_sems[2:]), ("relays", *relay_sems)], ["near", "far"], mix, "ag_mid_rest")
    g_out, g_ff1, g_ff2, g_proj, g_gate = _ag_split_call(
        fly_rest, [("near", *fwd_sems[:2]), ("far", *fwd_sems[2:])], [], mix, "ag_end_rest")[0]
    full_out, full_ff2, full_gate = g_out.reshape(D_MODEL, D_MODEL), g_ff2.reshape(D_FF, D_MODEL), g_gate.reshape(D_MODEL, D_MODEL)
    full_proj = g_proj.transpose(1, 0, 2).reshape(PLE_DIM, D_MODEL)

    (tail_small, dh1, dh1b, dmix, hn2, a, df, h2, dh2, dgl, dpp) = _tail(
        x2, mix, p2, t2, g2, gf.reshape(1, D_MODEL), full_out, g_ff1, full_ff2, full_gate, full_proj)

    ffn_names, mid_names = ("w_ff1", "w_ff2"), ("w_out", "w_ple_proj", "w_ple_gate")
    gw_ff1 = _wgrad(hn2, df, D_MODEL, D_FF // N_DEV, "wgrad_ff1", peer_cols=D_FF // N_DEV)
    gw_ff2 = _wgrad(a, dh2, 1024, D_MODEL, "wgrad_ff2").reshape(N_DEV, D_FF // N_DEV, D_MODEL)
    ffn_swap = _swap_start([gw_ff1, gw_ff2], [], "ffn")
    gw_gate = _wgrad(h2, dgl, D_MODEL, 512, "wgrad_gate", after=ffn_swap[0][4]).reshape(N_DEV, D_MODEL // N_DEV, D_MODEL)
    gw_proj = _wgrad(p2, dpp, PLE_DIM, D_MODEL // 2, "wgrad_proj", peer_cols=D_MODEL // N_DEV, after=gw_gate)
    ffn_sums, _ = _swap_sums(ffn_swap, ffn_names, gw_proj, "ffn")
    ffn_chips = _chips_start(ffn_sums, [], "ffn")
    gw_out = _wgrad(mix, dh1b, D_MODEL, 512, "wgrad_out", after=ffn_chips[0][4]).reshape(N_DEV, D_MODEL // N_DEV, D_MODEL)
    mid_swap = _swap_start([gw_out, gw_proj, gw_gate], [tail_small], "mid")

    dzm, dkv, d_gv, d_wsp, d_bsp, d_attn = _mixer_bwd(
        zuv, qkv, dmix, saved, gv, w_sp[0], b_col, bucket, after=mid_swap[0][4])
    mid_sums, mid_small = _swap_sums(mid_swap, mid_names, dzm, "mid")
    mid_chips = _chips_start(mid_sums, mid_small, "mid")
    gw_in = _wgrad_in(dzm, dkv, hn1, after=mid_chips[0][4]).reshape(N_DEV, D_IN // N_DEV, D_MODEL)
    dx, d_g1 = _in_bwd(x2, dh1, dzm, dkv, g1, full_in, after=gw_in)
    grad_x = dx.reshape(x.shape)
    in_swap = _swap_start([gw_in], [d_g1, d_gv, d_wsp, d_bsp, d_attn], "in")
    in_sums, in_small = _swap_sums(in_swap, ("w_in",), in_swap[0][4], "in")
    in_chips = _chips_start(in_sums, in_small, "in")

    grads, deltas, new_m, new_v = {}, {}, {}, {}
    after, small_recv = in_chips[0][4], []
    for chips, sums, names, tag in ((ffn_chips, ffn_sums, ffn_names, "ffn"), (mid_chips, mid_sums, mid_names, "mid"),
                                    (in_chips, in_sums, ("w_in",), "in")):
        recvs = _chips_wait(chips, after, tag)
        small_recv += recvs[len(names):]
        for n, (part, _), recv in zip(names, sums, recvs):
            g, d, mo, vo = _final_adamw(part, recv, shard(n), shard("m_" + n), shard("v_" + n), "adamw_" + n)
            for dst, arr in zip((grads, deltas, new_m, new_v), (g, d, mo, vo)):
                dst[n] = (arr.T if n == "w_in" else arr)[None]
            after = d

    views = {"w_spatial": (GROUPS, BLOCK, BLOCK), "b_spatial": (GROUPS, BLOCK), "final_gain": (1, D_MODEL)}
    small_in = [[args[pre + n].reshape(views.get(n, args[n].shape)) for n in SMALL] for pre in ("", "m_", "v_")]
    loss, *small_out = _adamw_small(mid_small + in_small, small_recv, *small_in, after=after)
    for dst, arrays in zip((grads, deltas, new_m, new_v), small_out):
        for n, arr in zip(SMALL, arrays):
            dst[n] = arr.reshape(args[n].shape)
    loss = loss[0, 0]

    order = ("norm1_gain", "w_in", "gmlp_v_gain", "w_spatial", "b_spatial", "attn_sinks", "rel_bias_table", "w_out",
             "norm2_gain", "w_ff1", "w_ff2", "w_ple_proj", "w_ple_gate", "final_gain")
    return (loss, grad_x, *[grads[n] for n in order], *[deltas[n] for n in order],
            *[new_m[n] for n in order], *[new_v[n] for n in order])
```

```python
import functools
import math

import numpy as np
import jax
import jax.numpy as jnp
from jax import lax
from jax.experimental import pallas as pl
from jax.experimental.pallas import tpu as pltpu

F32 = jnp.float32
BF = jnp.bfloat16
MESH = pl.DeviceIdType.MESH
N_DEV = 8

D_MODEL = 1024
PLE_DIM = 256
D_GMLP = 512
GROUPS = 4
GDIM = 128
BLOCK = 128
D_ATTN = 512
HEAD_DIM = 64
N_Q = 8
Q_PER_KV = 4
N_KV = N_Q // Q_PER_KV
ROWS4 = Q_PER_KV * BLOCK
D_KV = 128
D_FF = 4096
D_IN = 1792
D_MAIN = 2 * D_GMLP + D_ATTN
REL_BUCKETS = 32
EPS = 1e-6
NEG_INF = -1e30
SCALE = HEAD_DIM ** -0.5
GELU_C = math.sqrt(2.0 / math.pi)
GELU_A = 0.044715

ADAM_LR = 0.001
ADAM_B1 = 0.9
ADAM_B2 = 0.999
ADAM_EPS = 1e-08
ADAM_WD = 0.01
ADAM_STEP = 10

V7X_VMEM_LIMIT = 60000 * 1024
TOK_TILE = 256
IO_TOK_TILE = 512


def _call(body, after=None, **kw):
    if after is None:
        return pl.pallas_call(body, **kw)
    n_in = len(kw["in_specs"])

    def ordered(*refs):
        body(*refs[:n_in], *refs[n_in + 1:])

    kw["in_specs"] = list(kw["in_specs"]) + [pl.BlockSpec(memory_space=pl.ANY)]
    fn = pl.pallas_call(ordered, **kw)
    return lambda *operands: fn(*operands, after)


def _params(sem=None):
    if sem is None:
        return pltpu.CompilerParams(vmem_limit_bytes=V7X_VMEM_LIMIT)
    return pltpu.CompilerParams(dimension_semantics=sem, vmem_limit_bytes=V7X_VMEM_LIMIT)


def _nn(a, b):
    return jnp.dot(a, b, preferred_element_type=F32)


def _nt(a, b):
    return lax.dot_general(a, b, (((1,), (1,)), ((), ())), preferred_element_type=F32)


def _tn(a, b):
    return lax.dot_general(a, b, (((0,), (0,)), ((), ())), preferred_element_type=F32)


def _gelu_tanh(x):
    return jnp.tanh(GELU_C * (x + GELU_A * (x * x * x)))


def _gelu(x, t):
    return x * (0.5 * (1.0 + t))


def _gelu_and_grad(x, t):
    cdf = 0.5 * (1.0 + t)
    return x * cdf, cdf + 0.5 * x * (1.0 - t * t) * (GELU_C * (1.0 + 3.0 * GELU_A * (x * x)))


def _rms_scale(x):
    return lax.rsqrt(jnp.mean(x * x, axis=-1, keepdims=True) + EPS)


def _rms_bwd(dxn, x, r):
    return r * dxn - x * ((r * r * r) * jnp.mean(dxn * x, axis=-1, keepdims=True))


def _bucket_table():
    a = np.arange(BLOCK)[:, None]
    j = np.arange(2 * BLOCK)[None, :]
    n = BLOCK + a - j
    valid = (n >= 0) & (n < BLOCK)
    nc = np.maximum(n, 0)
    max_exact = REL_BUCKETS // 2
    nf = np.maximum(nc, 1).astype(np.float32)
    large = max_exact + (
        np.log(nf / np.float32(max_exact)) / np.float32(math.log(BLOCK / max_exact)) * np.float32(REL_BUCKETS - max_exact)
    ).astype(np.int32)
    large = np.minimum(large, REL_BUCKETS - 1)
    bucket = np.where(nc < max_exact, nc, large)
    return np.where(valid, bucket, -1).astype(np.int32)


def _in_proj(x, g1, w_in_t):
    s = x.shape[0]
    tm = min(IO_TOK_TILE, s)

    def body(x_ref, g_ref, w_ref, zuv_ref, qkv_ref, hn_ref):
        xv = x_ref[...]
        hn = ((xv * _rms_scale(xv)) * g_ref[...]).astype(BF)
        hn_ref[...] = hn
        z = _nt(hn, w_ref[...])
        zuv_ref[...] = z[:, : 2 * D_GMLP]
        qkv_ref[...] = z[:, 2 * D_GMLP:].astype(BF)

    return _call(
        body,
        name="in_proj",
        grid=(s // tm,),
        in_specs=[
            pl.BlockSpec((tm, D_MODEL), lambda i: (i, 0)),
            pl.BlockSpec((1, D_MODEL), lambda i: (0, 0)),
            pl.BlockSpec((D_IN, D_MODEL), lambda i: (0, 0)),
        ],
        out_specs=[
            pl.BlockSpec((tm, 2 * D_GMLP), lambda i: (i, 0)),
            pl.BlockSpec((tm, D_ATTN + 2 * D_KV), lambda i: (i, 0)),
            pl.BlockSpec((tm, D_MODEL), lambda i: (i, 0)),
        ],
        out_shape=[
            jax.ShapeDtypeStruct((s, 2 * D_GMLP), F32),
            jax.ShapeDtypeStruct((s, D_ATTN + 2 * D_KV), BF),
            jax.ShapeDtypeStruct((s, D_MODEL), BF),
        ],
        compiler_params=_params(("arbitrary",)),
    )(x, g1, w_in_t)


def _head_rows(h):
    kh, g = divmod(h, Q_PER_KV)
    return kh, slice(g * BLOCK, (g + 1) * BLOCK)


def _build_bias(bias_ref, bucket_ref, table_ref):
    bucket = bucket_ref[...]
    for h in range(N_Q):
        acc = jnp.zeros((BLOCK, 2 * BLOCK), F32)
        for b in range(REL_BUCKETS):
            acc = jnp.where(bucket == b, table_ref[b, h], acc)
        kh, rows = _head_rows(h)
        bias_ref[kh, rows, :] = acc


def _window_masks(i):
    row = lax.broadcasted_iota(jnp.int32, (ROWS4, BLOCK), 0) & (BLOCK - 1)
    col = lax.broadcasted_iota(jnp.int32, (ROWS4, BLOCK), 1)
    return (col > row) & (i > 0), col <= row


def _stack_heads(ref, kh, offset):
    first = offset + kh * Q_PER_KV * HEAD_DIM
    return jnp.concatenate(
        [ref[:, first + g * HEAD_DIM: first + (g + 1) * HEAD_DIM].astype(BF) for g in range(Q_PER_KV)], axis=0)


def _stack_sinks(sink_ref, kh):
    return jnp.concatenate([jnp.full((BLOCK, 1), sink_ref[0, kh * Q_PER_KV + g], F32) for g in range(Q_PER_KV)], axis=0)


def _tril_bf16(w_ref, g):
    row = lax.broadcasted_iota(jnp.int32, (BLOCK, BLOCK), 0)
    col = lax.broadcasted_iota(jnp.int32, (BLOCK, BLOCK), 1)
    return jnp.where(col <= row, w_ref[g], 0.0).astype(BF)


def _attn_probs(q_h, k_prev, k_cur, bias_h, sink, valid_prev, valid_cur):
    l_prev = jnp.where(valid_prev, _nt(q_h, k_prev) * SCALE + bias_h[:, :BLOCK], NEG_INF)
    l_cur = jnp.where(valid_cur, _nt(q_h, k_cur) * SCALE + bias_h[:, BLOCK:], NEG_INF)
    m = jnp.maximum(jnp.maximum(jnp.max(l_prev, axis=-1, keepdims=True), jnp.max(l_cur, axis=-1, keepdims=True)), sink)
    e_prev = jnp.exp(l_prev - m)
    e_cur = jnp.exp(l_cur - m)
    e_sink = jnp.exp(sink - m)
    denom = jnp.sum(e_prev, axis=-1, keepdims=True) + jnp.sum(e_cur, axis=-1, keepdims=True) + e_sink
    return e_prev / denom, e_cur / denom, e_sink / denom


def _mixer_specs(nb):
    cl = lambda i: jnp.minimum(i, nb - 1)
    return [
        pl.BlockSpec((BLOCK, 2 * D_GMLP), lambda i: (cl(i), 0)),
        pl.BlockSpec((BLOCK, D_ATTN), lambda i: (cl(i), 0)),
        pl.BlockSpec((BLOCK, 2 * D_KV), lambda i: (cl(i), D_ATTN // (2 * D_KV))),
        pl.BlockSpec((BLOCK, 2 * D_KV), lambda i: (jnp.maximum(cl(i) - 1, 0), D_ATTN // (2 * D_KV))),
        pl.BlockSpec((1, D_GMLP), lambda i: (0, 0)),
        pl.BlockSpec((GROUPS, BLOCK, BLOCK), lambda i: (0, 0, 0)),
        pl.BlockSpec((GROUPS, BLOCK, 1), lambda i: (0, 0, 0)),
        pl.BlockSpec(memory_space=pltpu.SMEM),
        pl.BlockSpec(memory_space=pltpu.SMEM),
        pl.BlockSpec((BLOCK, 2 * BLOCK), lambda i: (0, 0)),
    ]


def _mixer_fwd(zuv, qkv, gv, w_sp, b_sp, sinks, table, bucket, after=None):
    s = zuv.shape[0]
    nb = s // BLOCK

    def body(zuv_ref, q_ref, kvc_ref, kvp_ref, gv_ref, w_ref, b_ref, sink_ref, table_ref, bucket_ref,
             mix_ref, tanh_ref, prob_ref, psink_ref, bias_ref):
        i = pl.program_id(0)

        @pl.when(i == 0)
        def _():
            _build_bias(bias_ref, bucket_ref, table_ref)

        t = _gelu_tanh(zuv_ref[...])
        tanh_ref[...] = t
        u = _gelu(zuv_ref[:, :D_GMLP], t[:, :D_GMLP])
        vg = _gelu(zuv_ref[:, D_GMLP:], t[:, D_GMLP:])
        for g in range(GROUPS):
            sl = slice(g * GDIM, (g + 1) * GDIM)
            vg_g = vg[:, sl]
            vn = ((vg_g * _rms_scale(vg_g)) * gv_ref[:, sl]).astype(BF)
            sv = _nn(_tril_bf16(w_ref, g), vn) + b_ref[g]
            mix_ref[:, sl] = (u[:, sl] * sv).astype(BF)

        valid_prev, valid_cur = _window_masks(i)
        for kh in range(N_KV):
            ksl = slice(kh * HEAD_DIM, (kh + 1) * HEAD_DIM)
            vsl = slice(D_KV + kh * HEAD_DIM, D_KV + (kh + 1) * HEAD_DIM)
            q4 = _stack_heads(q_ref, kh, 0)
            p_prev, p_cur, p_sink = _attn_probs(
                q4, kvp_ref[:, ksl], kvc_ref[:, ksl], bias_ref[kh], _stack_sinks(sink_ref, kh), valid_prev, valid_cur)
            prob_ref[0, kh, :, :BLOCK] = p_prev
            prob_ref[0, kh, :, BLOCK:] = p_cur
            psink_ref[0, kh] = jnp.broadcast_to(p_sink, (ROWS4, LANES))
            o4 = _nn(p_prev.astype(BF), kvp_ref[:, vsl]) + _nn(p_cur.astype(BF), kvc_ref[:, vsl])
            for g in range(Q_PER_KV):
                first = D_GMLP + (kh * Q_PER_KV + g) * HEAD_DIM
                mix_ref[:, first:first + HEAD_DIM] = o4[g * BLOCK:(g + 1) * BLOCK].astype(BF)

    return _call(
        body,
        name="mixer_fwd",
        after=after,
        grid=(nb,),
        in_specs=_mixer_specs(nb),
        out_specs=[
            pl.BlockSpec((BLOCK, D_MODEL), lambda i: (i, 0)),
            pl.BlockSpec((BLOCK, 2 * D_GMLP), lambda i: (i, 0)),
            pl.BlockSpec((1, N_KV, ROWS4, 2 * BLOCK), lambda i: (i, 0, 0, 0)),
            pl.BlockSpec((1, N_KV, ROWS4, LANES), lambda i: (i, 0, 0, 0)),
        ],
        out_shape=[
            jax.ShapeDtypeStruct((s, D_MODEL), BF),
            jax.ShapeDtypeStruct((s, 2 * D_GMLP), F32),
            jax.ShapeDtypeStruct((nb, N_KV, ROWS4, 2 * BLOCK), F32),
            jax.ShapeDtypeStruct((nb, N_KV, ROWS4, LANES), F32),
        ],
        scratch_shapes=[pltpu.VMEM((N_KV, ROWS4, 2 * BLOCK), F32)],
        compiler_params=_params(("arbitrary",)),
    )(zuv, qkv, qkv, qkv, gv, w_sp, b_sp, sinks, table, bucket)


def _tail(x, mix, p, t, g2, gf, w_out, w_ff1, w_ff2, w_gate, w_proj):
    s = x.shape[0]
    tm = min(TOK_TILE, s)
    n_ff = w_ff1.shape[0]
    fc = D_FF // n_ff

    def body(x_ref, mix_ref, p_ref, t_ref, g2_ref, gf_ref, wo_ref, w1_ref, w2_ref, wg_ref, wp_ref,
             small_ref, dh1_ref, dh1b_ref, dmix_ref, hn2_ref, a_ref, df_ref, h2_ref, dh2_ref, dgl_ref, dpp_ref, f_ref):
        i = pl.program_id(0)

        @pl.when(i == 0)
        def _():
            small_ref[...] = jnp.zeros_like(small_ref)

        h1 = x_ref[...] + _nn(mix_ref[...], wo_ref[...])
        r2 = _rms_scale(h1)
        hn2 = ((h1 * r2) * g2_ref[...]).astype(BF)
        hn2_ref[...] = hn2
        h2 = h1
        for c in range(n_ff):
            f = _nn(hn2, w1_ref[c])
            f_ref[:, c * fc:(c + 1) * fc] = f
            a = jnp.square(jnp.maximum(f, 0.0)).astype(BF)
            a_ref[:, c * fc:(c + 1) * fc] = a
            h2 = h2 + _nn(a, w2_ref[c * fc:(c + 1) * fc, :])
        h2b = h2.astype(BF)
        h2_ref[...] = h2b
        gate = jax.nn.sigmoid(_nn(h2b, wg_ref[...]))
        pp = _nn(p_ref[...].astype(BF), wp_ref[...])
        h3 = h2 + gate * pp
        rf = _rms_scale(h3)
        gf_v = gf_ref[...]
        err = (h3 * rf) * gf_v - t_ref[...]
        small_ref[2:3, :] += jnp.sum(jnp.sum(err * err, axis=-1, keepdims=True), axis=0, keepdims=True) * (0.5 / D_MODEL)

        dy = err * (1.0 / D_MODEL)
        small_ref[1:2, :] += jnp.sum(dy * (h3 * rf), axis=0, keepdims=True)
        dh3 = _rms_bwd(dy * gf_v, h3, rf)
        dpp_ref[...] = (dh3 * gate).astype(BF)
        dgl = ((dh3 * pp) * (gate * (1.0 - gate))).astype(BF)
        dgl_ref[...] = dgl
        dh2 = dh3 + _nt(dgl, wg_ref[...])
        dh2b = dh2.astype(BF)
        dh2_ref[...] = dh2b
        dhn2 = jnp.zeros((tm, D_MODEL), F32)
        for c in range(n_ff):
            da = _nt(dh2b, w2_ref[c * fc:(c + 1) * fc, :])
            df = (da * (2.0 * jnp.maximum(f_ref[:, c * fc:(c + 1) * fc], 0.0))).astype(BF)
            df_ref[:, c * fc:(c + 1) * fc] = df
            dhn2 = dhn2 + _nt(df, w1_ref[c])
        small_ref[0:1, :] += jnp.sum(dhn2 * (h1 * r2), axis=0, keepdims=True)
        dh1 = dh2 + _rms_bwd(dhn2 * g2_ref[...], h1, r2)
        dh1_ref[...] = dh1
        dh1b = dh1.astype(BF)
        dh1b_ref[...] = dh1b
        dmix_ref[...] = _nt(dh1b, wo_ref[...])

    tile = lambda cols: pl.BlockSpec((tm, cols), lambda i: (i, 0))
    whole = lambda shape: pl.BlockSpec(shape, lambda i: (0,) * len(shape), pipeline_mode=pl.Buffered(1))
    row = pl.BlockSpec((1, D_MODEL), lambda i: (0, 0))
    act = lambda cols, dt: jax.ShapeDtypeStruct((s, cols), dt)
    return _call(
        body,
        name="tail",
        grid=(s // tm,),
        in_specs=[tile(D_MODEL), tile(D_MODEL), tile(PLE_DIM), tile(D_MODEL), row, row,
                  whole(w_out.shape), whole(w_ff1.shape), whole(w_ff2.shape), whole(w_gate.shape), whole(w_proj.shape)],
        out_specs=[pl.BlockSpec((8, D_MODEL), lambda i: (0, 0)), tile(D_MODEL), tile(D_MODEL), tile(D_MODEL), tile(D_MODEL), tile(D_FF),
                   tile(D_FF), tile(D_MODEL), tile(D_MODEL), tile(D_MODEL), tile(D_MODEL)],
        out_shape=[jax.ShapeDtypeStruct((8, D_MODEL), F32),
                   act(D_MODEL, F32), act(D_MODEL, BF), act(D_MODEL, F32), act(D_MODEL, BF), act(D_FF, BF), act(D_FF, BF), act(D_MODEL, BF),
                   act(D_MODEL, BF), act(D_MODEL, BF), act(D_MODEL, BF)],
        scratch_shapes=[pltpu.VMEM((tm, D_FF), F32)],
        compiler_params=_params(("arbitrary",)),
    )(x, mix, p, t, g2, gf, w_out, w_ff1, w_ff2, w_gate, w_proj)


def _mixer_bwd(zuv, qkv, dmix, saved, gv, w_sp, b_sp, bucket, after=None):
    s = zuv.shape[0]
    nb = s // BLOCK

    def body(zuv_ref, q_ref, kvc_ref, kvp_ref, gv_ref, w_ref, b_ref, bucket_ref, dmix_ref, tanh_ref, prob_ref, psink_ref,
             dzm_ref, dkv_ref, dgv_ref, dw_ref, db_ref, dattn_ref,
             dbias_ref, carry_ref, dsink_acc, db_acc):
        i = pl.program_id(0)

        @pl.when(i == 0)
        def _():
            dbias_ref[...] = jnp.zeros_like(dbias_ref)
            carry_ref[...] = jnp.zeros_like(carry_ref)
            dsink_acc[...] = jnp.zeros_like(dsink_acc)
            dgv_ref[...] = jnp.zeros_like(dgv_ref)
            dw_ref[...] = jnp.zeros_like(dw_ref)
            db_acc[...] = jnp.zeros_like(db_acc)

        @pl.when(i < nb)
        def _():
            u, du_dz = _gelu_and_grad(zuv_ref[:, :D_GMLP], tanh_ref[:, :D_GMLP])
            vg, dvg_dz = _gelu_and_grad(zuv_ref[:, D_GMLP:], tanh_ref[:, D_GMLP:])
            for g in range(GROUPS):
                sl = slice(g * GDIM, (g + 1) * GDIM)
                vg_g = vg[:, sl]
                rg = _rms_scale(vg_g)
                vhat = vg_g * rg
                gain = gv_ref[:, sl]
                vn = (vhat * gain).astype(BF)
                w_g = _tril_bf16(w_ref, g)
                sv = _nn(w_g, vn) + b_ref[g]
                dmix_a = dmix_ref[:, sl]
                dsv = dmix_a * u[:, sl]
                dsvb = dsv.astype(BF)
                db_acc[g] += jnp.sum(dsv, axis=-1, keepdims=True)
                dw_ref[g] += _nt(dsvb, vn)
                dvn = _tn(w_g, dsvb)
                dgv_ref[:, sl] += jnp.sum(dvn * vhat, axis=0, keepdims=True)
                dvg = _rms_bwd(dvn * gain, vg_g, rg)
                dzm_ref[:, sl] = ((dmix_a * sv) * du_dz[:, sl]).astype(BF)
                dzm_ref[:, D_GMLP + g * GDIM: D_GMLP + (g + 1) * GDIM] = (dvg * dvg_dz[:, sl]).astype(BF)

            for kh in range(N_KV):
                ksl = slice(kh * HEAD_DIM, (kh + 1) * HEAD_DIM)
                vsl = slice(D_KV + kh * HEAD_DIM, D_KV + (kh + 1) * HEAD_DIM)
                k_prev, k_cur = kvp_ref[:, ksl], kvc_ref[:, ksl]
                v_prev, v_cur = kvp_ref[:, vsl], kvc_ref[:, vsl]
                q4 = _stack_heads(q_ref, kh, 0)
                p_prev, p_cur, p_sink = prob_ref[0, kh, :, :BLOCK], prob_ref[0, kh, :, BLOCK:], psink_ref[0, kh, :, 0:1]
                do4 = _stack_heads(dmix_ref, kh, D_GMLP)
                dp_prev = _nt(do4, v_prev)
                dp_cur = _nt(do4, v_cur)
                delta = jnp.sum(p_prev * dp_prev, axis=-1, keepdims=True) + jnp.sum(p_cur * dp_cur, axis=-1, keepdims=True)
                ds_prev = p_prev * (dp_prev - delta)
                ds_cur = p_cur * (dp_cur - delta)
                dsink_acc[kh] -= p_sink * delta
                dbias_ref[kh, :, :BLOCK] += ds_prev
                dbias_ref[kh, :, BLOCK:] += ds_cur
                dsb_prev = ds_prev.astype(BF)
                dsb_cur = ds_cur.astype(BF)
                dq4 = (_nn(dsb_prev, k_prev) + _nn(dsb_cur, k_cur)) * SCALE
                for g in range(Q_PER_KV):
                    first = 2 * D_GMLP + (kh * Q_PER_KV + g) * HEAD_DIM
                    dzm_ref[:, first:first + HEAD_DIM] = dq4[g * BLOCK:(g + 1) * BLOCK].astype(BF)
                dkv_ref[:, ksl] = (carry_ref[:, ksl] + _tn(dsb_prev, q4) * SCALE).astype(BF)
                dkv_ref[:, vsl] = (carry_ref[:, vsl] + _tn(p_prev.astype(BF), do4)).astype(BF)
                carry_ref[:, ksl] = _tn(dsb_cur, q4) * SCALE
                carry_ref[:, vsl] = _tn(p_cur.astype(BF), do4)

        @pl.when(i == nb)
        def _():
            dkv_ref[...] = carry_ref[...].astype(BF)
            row = lax.broadcasted_iota(jnp.int32, (BLOCK, BLOCK), 0)
            col = lax.broadcasted_iota(jnp.int32, (BLOCK, BLOCK), 1)
            for g in range(GROUPS):
                dw_ref[g] = jnp.where(col <= row, dw_ref[g], 0.0)
                db_ref[g:g + 1, :] = jnp.sum(jnp.where(col == row, db_acc[g], 0.0), axis=0, keepdims=True)
            bucket = bucket_ref[...]
            for h in range(N_Q):
                kh, rows = _head_rows(h)
                dattn_ref[REL_BUCKETS, h] = jnp.sum(dsink_acc[kh, rows, :])
                dbh = dbias_ref[kh, rows, :]
                for b in range(REL_BUCKETS):
                    dattn_ref[b, h] = jnp.sum(jnp.where(bucket == b, dbh, 0.0))

    cl = lambda i: jnp.minimum(i, nb - 1)
    const = lambda shape: pl.BlockSpec(shape, lambda i: (0,) * len(shape))
    return _call(
        body,
        name="mixer_bwd",
        after=after,
        grid=(nb + 1,),
        in_specs=_mixer_specs(nb)[:7] + [
            const((BLOCK, 2 * BLOCK)),
            pl.BlockSpec((BLOCK, D_MODEL), lambda i: (cl(i), 0)),
            pl.BlockSpec((BLOCK, 2 * D_GMLP), lambda i: (cl(i), 0)),
            pl.BlockSpec((1, N_KV, ROWS4, 2 * BLOCK), lambda i: (cl(i), 0, 0, 0)),
            pl.BlockSpec((1, N_KV, ROWS4, LANES), lambda i: (cl(i), 0, 0, 0)),
        ],
        out_specs=[
            pl.BlockSpec((BLOCK, D_MAIN), lambda i: (cl(i), 0)),
            pl.BlockSpec((BLOCK, 2 * D_KV), lambda i: (jnp.maximum(i - 1, 0), 0)),
            const((1, D_GMLP)),
            const((GROUPS, BLOCK, BLOCK)),
            const((GROUPS, BLOCK)),
            pl.BlockSpec(memory_space=pltpu.SMEM),
        ],
        out_shape=[
            jax.ShapeDtypeStruct((s, D_MAIN), BF),
            jax.ShapeDtypeStruct((s, 2 * D_KV), BF),
            jax.ShapeDtypeStruct((1, D_GMLP), F32),
            jax.ShapeDtypeStruct((GROUPS, BLOCK, BLOCK), F32),
            jax.ShapeDtypeStruct((GROUPS, BLOCK), F32),
            jax.ShapeDtypeStruct((REL_BUCKETS + 1, N_Q), F32),
        ],
        scratch_shapes=[
            pltpu.VMEM((N_KV, ROWS4, 2 * BLOCK), F32),
            pltpu.VMEM((BLOCK, 2 * D_KV), F32),
            pltpu.VMEM((N_KV, ROWS4, 1), F32),
            pltpu.VMEM((GROUPS, BLOCK, 1), F32),
        ],
        compiler_params=_params(("arbitrary",)),
    )(zuv, qkv, qkv, qkv, gv, w_sp, b_sp, bucket, dmix, *saved)


def _in_bwd(x, dh1, dzm, dkv, g1, w_in_t, after=None):
    s = x.shape[0]
    tm = min(IO_TOK_TILE, s)

    def body(x_ref, dh1_ref, dzm_ref, dkv_ref, g_ref, w_ref, dx_ref, dg_ref):
        @pl.when(pl.program_id(0) == 0)
        def _():
            dg_ref[...] = jnp.zeros_like(dg_ref)

        dhn = _nn(dzm_ref[...], w_ref[:D_MAIN, :]) + _nn(dkv_ref[...], w_ref[D_MAIN:, :])
        xv = x_ref[...]
        r = _rms_scale(xv)
        dg_ref[...] += jnp.sum(dhn * (xv * r), axis=0, keepdims=True)
        dx_ref[...] = dh1_ref[...] + _rms_bwd(dhn * g_ref[...], xv, r)

    tile = lambda cols: pl.BlockSpec((tm, cols), lambda i: (i, 0))
    row = pl.BlockSpec((1, D_MODEL), lambda i: (0, 0))
    return _call(
        body,
        name="in_bwd",
        after=after,
        grid=(s // tm,),
        in_specs=[tile(D_MODEL), tile(D_MODEL), tile(D_MAIN), tile(2 * D_KV), row, pl.BlockSpec((D_IN, D_MODEL), lambda i: (0, 0))],
        out_specs=[tile(D_MODEL), row],
        out_shape=[jax.ShapeDtypeStruct((s, D_MODEL), F32), jax.ShapeDtypeStruct((1, D_MODEL), F32)],
        compiler_params=_params(("arbitrary",)),
    )(x, dh1, dzm, dkv, g1, w_in_t)


def _wgrad_in(dzm, dkv, hn, after=None):
    s = hn.shape[0]
    tm = 2 * D_KV
    n_main = D_MAIN // tm

    def body(dzm_ref, dkv_ref, hn_ref, o_ref):
        i = pl.program_id(0)

        @pl.when(i < n_main)
        def _():
            o_ref[...] = _tn(dzm_ref[...], hn_ref[...])

        @pl.when(i == n_main)
        def _():
            o_ref[...] = _tn(dkv_ref[...], hn_ref[...])

    return _call(
        body,
        name="wgrad_in",
        after=after,
        grid=(n_main + 1,),
        in_specs=[
            pl.BlockSpec((s, tm), lambda i: (0, jnp.minimum(i, n_main - 1))),
            pl.BlockSpec((s, tm), lambda i: (0, 0)),
            pl.BlockSpec((s, D_MODEL), lambda i: (0, 0)),
        ],
        out_specs=pl.BlockSpec((tm, D_MODEL), lambda i: (i, 0)),
        out_shape=jax.ShapeDtypeStruct((D_IN, D_MODEL), F32),
        compiler_params=_params(("arbitrary",)),
    )(dzm, dkv, hn)


def _wgrad(a, b, tm, tn, name, peer_cols=0, after=None):
    s, m = a.shape
    n = b.shape[1]

    def body(a_ref, b_ref, o_ref, at_ref):
        @pl.when(pl.program_id(1) == 0)
        def _():
            at_ref[...] = a_ref[...].astype(BF).T

        r = _nn(at_ref[...], b_ref[...])
        if peer_cols:
            for q in range(tn // peer_cols):
                o_ref[q] = r[:, q * peer_cols:(q + 1) * peer_cols]
        else:
            o_ref[...] = r

    if peer_cols:
        out_spec = pl.BlockSpec((tn // peer_cols, tm, peer_cols), lambda i, j: (j, i, 0))
        out_shape = jax.ShapeDtypeStruct((n // peer_cols, m, peer_cols), F32)
    else:
        out_spec = pl.BlockSpec((tm, tn), lambda i, j: (i, j))
        out_shape = jax.ShapeDtypeStruct((m, n), F32)
    return _call(
        body,
        name=name,
        after=after,
        grid=(m // tm, n // tn),
        in_specs=[pl.BlockSpec((s, tm), lambda i, j: (0, i)), pl.BlockSpec((s, tn), lambda i, j: (0, j))],
        out_specs=out_spec,
        out_shape=out_shape,
        scratch_shapes=[pltpu.VMEM((tm, s), BF)],
        compiler_params=_params(("arbitrary", "arbitrary")),
    )(a, b)


def _adamw_math(w, g, m, v):
    m_new = ADAM_B1 * m + (1.0 - ADAM_B1) * g
    v_new = ADAM_B2 * v + (1.0 - ADAM_B2) * jnp.square(g)
    m_hat = m_new / (1.0 - ADAM_B1 ** ADAM_STEP)
    v_hat = v_new / (1.0 - ADAM_B2 ** ADAM_STEP)
    delta = -ADAM_LR * (m_hat / (jnp.sqrt(v_hat) + ADAM_EPS) + ADAM_WD * w)
    return delta, m_new, v_new


def _final_adamw(part, recv, w, m, v, name):
    r, c = w.shape
    tr = min(r, 512)

    def body(p_ref, r_ref, w_ref, m_ref, v_ref, g_ref, d_ref, mo_ref, vo_ref):
        g = p_ref[...]
        for j in range(3):
            g = g + r_ref[j].astype(F32)
        g_ref[...] = g
        d_ref[...], mo_ref[...], vo_ref[...] = _adamw_math(w_ref[...], g, m_ref[...], v_ref[...])

    spec = pl.BlockSpec((tr, c), lambda i: (i, 0))
    return _call(
        body,
        name=name,
        grid=(r // tr,),
        in_specs=[spec, pl.BlockSpec((3, tr, c), lambda i: (0, i, 0)), spec, spec, spec],
        out_specs=[spec] * 4,
        out_shape=[jax.ShapeDtypeStruct((r, c), F32)] * 4,
        compiler_params=_params(("arbitrary",)),
    )(part, recv, w, m, v)


def _rs_sum(g, land, blocks, name):
    _, r, c = g.shape
    tr = min(r, 256)

    def body(blk_ref, g0_ref, g1_ref, g2_ref, g3_ref, l_ref, part_ref, send_ref):
        part_ref[...] = g0_ref[0] + l_ref[0]
        for j, gj_ref in enumerate((g1_ref, g2_ref, g3_ref)):
            send_ref[j] = (gj_ref[0] + l_ref[j + 1]).astype(BF)

    def pick(j):
        return pl.BlockSpec((1, tr, c), lambda i, blk: (blk[j], i, 0))

    return _call(
        body,
        name=name,
        grid_spec=pltpu.PrefetchScalarGridSpec(
            num_scalar_prefetch=1,
            grid=(r // tr,),
            in_specs=[pick(0), pick(1), pick(2), pick(3), pl.BlockSpec((4, tr, c), lambda i, blk: (0, i, 0))],
            out_specs=[pl.BlockSpec((tr, c), lambda i, blk: (i, 0)), pl.BlockSpec((3, tr, c), lambda i, blk: (0, i, 0))],
        ),
        out_shape=[jax.ShapeDtypeStruct((r, c), F32), jax.ShapeDtypeStruct((3, r, c), BF)],
        compiler_params=_params(("arbitrary",)),
    )(blocks, g, g, g, g, land)


def _adamw_small(own, recv, weights, moms, vels, after):
    n_w = len(weights)

    def body(*refs):
        own_refs, recv_refs = refs[:6], refs[6:12]
        w_refs, m_refs, v_refs = refs[12:12 + n_w], refs[12 + n_w:12 + 2 * n_w], refs[12 + 2 * n_w:12 + 3 * n_w]
        outs = refs[12 + 3 * n_w:]
        loss_ref, g_refs, d_refs = outs[0], outs[1:1 + n_w], outs[1 + n_w:1 + 2 * n_w]
        mo_refs, vo_refs = outs[1 + 2 * n_w:1 + 3 * n_w], outs[1 + 3 * n_w:]
        x, y = lax.axis_index("x"), lax.axis_index("y")

        def total(k, *index):
            index = index or (slice(None),) * (len(own_refs[k].shape) - 1)
            across = [own_refs[k][(0, *index)]] + [recv_refs[k][(j, *index)] for j in range(3)]
            acc = None
            for cx in range(2):
                for cy in range(2):
                    flip = (cx ^ x) + 2 * (cy ^ y)
                    part = jnp.where(flip == 0, across[0], jnp.where(flip == 1, across[1], jnp.where(flip == 2, across[2], across[3])))
                    acc = part if acc is None else acc + part
            return acc

        grads = [
            total(1), total(2), total(3), total(4),
            total(5, slice(REL_BUCKETS, None), slice(None)), total(5, slice(None, REL_BUCKETS), slice(None)),
            total(0, slice(0, 1), slice(None)), total(0, slice(1, 2), slice(None))]
        loss_ref[...] = total(0, slice(2, 3), slice(0, 1))
        for k in range(n_w):
            g_refs[k][...] = grads[k]
            d_refs[k][...], mo_refs[k][...], vo_refs[k][...] = _adamw_math(w_refs[k][...], grads[k], m_refs[k][...], v_refs[k][...])

    shapes = [jax.ShapeDtypeStruct(w.shape, F32) for w in weights]
    outs = _call(
        body,
        name="adamw_small",
        after=after,
        in_specs=[pl.BlockSpec(memory_space=pltpu.VMEM)] * (12 + 3 * n_w),
        out_shape=[jax.ShapeDtypeStruct((1, 1), F32)] + shapes * 4,
        compiler_params=_params(),
    )(*own, *recv, *weights, *moms, *vels)
    return outs[0], outs[1:1 + n_w], outs[1 + n_w:1 + 2 * n_w], outs[1 + 2 * n_w:1 + 3 * n_w], outs[1 + 3 * n_w:]


def _place():
    x, y, c = lax.axis_index("x"), lax.axis_index("y"), lax.axis_index("c")
    return x, y, c, [(1 - x, y), (x, 1 - y), (1 - x, 1 - y)]


def _dev_index(px, py, pc):
    return 4 * px + 2 * py + pc


HBM_SPEC = pl.BlockSpec(memory_space=pltpu.HBM)
SEM_SPEC = pl.BlockSpec(memory_space=pltpu.SEMAPHORE)
ANY_SPEC = pl.BlockSpec(memory_space=pl.ANY)
DATAFLOW = pltpu.SideEffectType.DATAFLOW_SIDE_EFFECTING


def _hbm(a):
    return pltpu.with_memory_space_constraint(a, pltpu.HBM)


def _prep_weights(shards):
    k_n = len(shards)

    def body(*refs):
        ins, outs, stage, sems = refs[:k_n], refs[k_n:2 * k_n], refs[2 * k_n:3 * k_n], refs[3 * k_n]
        x, y, c, _ = _place()
        copies = []
        for k in range(k_n):
            stage[k][...] = ins[k][...].astype(BF)
            copies.append(pltpu.make_async_copy(stage[k], outs[k].at[_dev_index(x, y, c)], sems.at[k]))
            copies[k].start()
        for cp in copies:
            cp.wait()

    return _call(
        body,
        name="prep_weights",
        in_specs=[pl.BlockSpec(memory_space=pltpu.VMEM)] * k_n,
        out_specs=[ANY_SPEC] * k_n,
        out_shape=[jax.ShapeDtypeStruct((N_DEV,) + sh.shape, BF) for sh in shards],
        scratch_shapes=[pltpu.VMEM(sh.shape, BF) for sh in shards] + [pltpu.SemaphoreType.DMA((k_n,))],
        compiler_params=_params(),
    )(*shards)


def _ag_piece(land_k, block, half, peer, send_sem, recv_sem):
    ref = land_k.at[_dev_index(*block)]
    if half is not None:
        rows = land_k.shape[1] // 2
        ref = ref.at[pl.ds(half * rows, rows)]
    return pltpu.make_async_remote_copy(
        src_ref=ref, dst_ref=ref, send_sem=send_sem, recv_sem=recv_sem, device_id=peer, device_id_type=MESH)


def _ag_plan():
    x, y, c, _ = _place()
    me, sib = (x, y, c), (x, y, 1 - c)
    xn, yn, diag = (1 - x, y, c), (x, 1 - y, c), (1 - x, 1 - y, c)
    return dict(
        relay_halves=[(me, 0, xn), (me, 1, yn)],
        others=[(me, None, sib), (me, 1, xn), (me, 0, yn)],
        relays=[(xn, 0, yn), (yn, 1, xn)],
        near=[(xn, None, sib), (yn, None, sib)],
        far=[(diag, None, sib)],
    )


def _ag_stage(land, stage, send_sems, recv_sems, act):
    copies = _ag_plan()[stage]
    n = len(copies)
    for k in range(len(land)):
        for j, (block, half, peer) in enumerate(copies):
            cp = _ag_piece(land[k], block, half, peer, send_sems.at[n * k + j], recv_sems.at[n * k + j])
            if act == "start":
                cp.start()
            else:
                cp.wait_send()
                cp.wait_recv()


def _sem_shapes(*counts):
    return [pltpu.SemaphoreType.DMA((n,)) for n in counts for _ in range(2)]


def _ag_start(first, rest):
    lands = list(first) + list(rest)
    k_n, k_first = len(lands), len(first)
    k_rest = k_n - k_first

    def body(*refs):
        land = refs[:k_n]
        sems = refs[k_n:k_n + 6]
        token = refs[-1]
        x, y, c, chips = _place()
        targets = [(x, y, 1 - c)] + [(*chip, c) for chip in chips]
        for k in range(k_first):
            for j, to in enumerate(targets):
                _ag_piece(land[k], (x, y, c), None, to, sems[0].at[4 * k + j], sems[1].at[4 * k + j]).start()
        _ag_stage(land[k_first:], "relay_halves", sems[2], sems[3], "start")
        _ag_stage(land[k_first:], "others", sems[4], sems[5], "start")
        token[...] = jnp.zeros_like(token)

    outs = pl.pallas_call(
        body,
        name="ag_start",
        in_specs=[HBM_SPEC] * k_n,
        out_specs=(*[SEM_SPEC] * 6, *[HBM_SPEC] * k_n, pl.BlockSpec(memory_space=pltpu.VMEM)),
        out_shape=(*_sem_shapes(4 * k_first, 2 * k_rest, 3 * k_rest),
                   *[pltpu.HBM(a.shape, a.dtype) for a in lands], jax.ShapeDtypeStruct((8, LANES), F32)),
        input_output_aliases={k: 6 + k for k in range(k_n)},
        compiler_params=pltpu.CompilerParams(has_side_effects=DATAFLOW),
    )(*[_hbm(a) for a in lands])
    flying = list(outs[6:6 + k_n])
    return (outs[0], outs[1], flying[:k_first]), (outs[2:6], flying[k_first:]), outs[-1]


def _ag_split_call(lands, waits, starts, after, name):
    k_n = len(lands)
    plan_sizes = dict(relay_halves=2, others=3, relays=2, near=2, far=1)
    n_in, n_out = 2 * len(waits), 2 * len(starts)

    def body(*refs):
        land = refs[:k_n]
        in_sems = refs[k_n:k_n + n_in]
        out_sems, token = refs[len(refs) - 1 - n_out:len(refs) - 1], refs[-1]
        for w, (stage, _, _) in enumerate(waits):
            _ag_stage(land, stage, in_sems[2 * w], in_sems[2 * w + 1], "wait")
            if w < len(starts):
                _ag_stage(land, starts[w], out_sems[2 * w], out_sems[2 * w + 1], "start")
        token[...] = jnp.zeros_like(token)

    outs = pl.pallas_call(
        body,
        name=name,
        in_specs=[HBM_SPEC] * k_n + [SEM_SPEC] * n_in + [ANY_SPEC],
        out_specs=(*[HBM_SPEC] * k_n, *[SEM_SPEC] * n_out, pl.BlockSpec(memory_space=pltpu.VMEM)),
        out_shape=(*[pltpu.HBM(a.shape, a.dtype) for a in lands], *_sem_shapes(*[plan_sizes[s] * k_n for s in starts]),
                   jax.ShapeDtypeStruct((8, LANES), F32)),
        input_output_aliases={k: k for k in range(k_n)},
        compiler_params=pltpu.CompilerParams(has_side_effects=DATAFLOW),
    )(*lands, *[s for _, a, b in waits for s in (a, b)], after)
    return list(outs[:k_n]), list(outs[k_n:k_n + n_out]), outs[-1]


def _ag_mid(lands, send_sems, recv_sems, after, name):
    k_n = len(lands)

    def body(*refs):
        land = refs[:k_n]
        send1, recv1 = refs[k_n], refs[k_n + 1]
        fwd_send, fwd_recv = refs[-2], refs[-1]
        x, y, c, chips = _place()
        sources = [(x, y, 1 - c)] + [(*chip, c) for chip in chips]
        for k in range(k_n):
            mine = land[k].at[_dev_index(x, y, c)]
            for j, frm in enumerate(sources):
                got = land[k].at[_dev_index(*frm)]
                cp = pltpu.make_async_remote_copy(
                    src_ref=mine, dst_ref=got, send_sem=send1.at[4 * k + j], recv_sem=recv1.at[4 * k + j], device_id=frm, device_id_type=MESH)
                cp.wait_send()
                cp.wait_recv()
                if j >= 1:
                    pltpu.make_async_remote_copy(
                        src_ref=got, dst_ref=got, send_sem=fwd_send.at[3 * k + j - 1], recv_sem=fwd_recv.at[3 * k + j - 1],
                        device_id=(x, y, 1 - c), device_id_type=MESH).start()

    outs = pl.pallas_call(
        body,
        name=name,
        in_specs=[HBM_SPEC] * k_n + [SEM_SPEC, SEM_SPEC, ANY_SPEC],
        out_specs=(*[HBM_SPEC] * k_n, SEM_SPEC, SEM_SPEC),
        out_shape=(*[pltpu.HBM(a.shape, a.dtype) for a in lands], pltpu.SemaphoreType.DMA((3 * k_n,)), pltpu.SemaphoreType.DMA((3 * k_n,))),
        input_output_aliases={k: k for k in range(k_n)},
        compiler_params=pltpu.CompilerParams(has_side_effects=DATAFLOW),
    )(*lands, send_sems, recv_sems, after)
    return list(outs[:k_n]), outs[-2], outs[-1]


def _ag_end(lands, fwd_send, fwd_recv, name):
    k_n = len(lands)

    def body(*refs):
        land = refs[:k_n]
        fsend, frecv = refs[k_n], refs[k_n + 1]
        x, y, c, chips = _place()
        for k in range(k_n):
            for j, chip in enumerate(chips):
                cp = pltpu.make_async_remote_copy(
                    src_ref=land[k].at[_dev_index(*chip, c)], dst_ref=land[k].at[_dev_index(*chip, 1 - c)],
                    send_sem=fsend.at[3 * k + j], recv_sem=frecv.at[3 * k + j], device_id=(x, y, 1 - c), device_id_type=MESH)
                cp.wait_send()
                cp.wait_recv()

    outs = pl.pallas_call(
        body,
        name=name,
        in_specs=[HBM_SPEC] * k_n + [SEM_SPEC, SEM_SPEC],
        out_specs=tuple([HBM_SPEC] * k_n),
        out_shape=tuple(pltpu.HBM(a.shape, a.dtype) for a in lands),
        input_output_aliases={k: k for k in range(k_n)},
        compiler_params=pltpu.CompilerParams(has_side_effects=DATAFLOW),
    )(*lands, fwd_send, fwd_recv)
    return list(outs)


def _chips4():
    x, y, c, others = _place()
    return x, y, c, [(x, y)] + others


def _route_sibling(j):
    x, y, c, chips = _chips4()
    return _dev_index(*chips[j], 1 - c), j, (x, y, 1 - c)


def _route_chips(j):
    x, y, c, chips = _chips4()
    return j, j, (*chips[j + 1], c)


def _route_sibling_whole(j):
    x, y, c, _ = _chips4()
    return 0, 0, (x, y, 1 - c)


def _route_chips_whole(j):
    x, y, c, chips = _chips4()
    return 0, j, (*chips[j + 1], c)


def _xchg_copies(routes, src, dst, send_sems, recv_sems):
    copies, sem = [], 0
    for k, (route, n) in enumerate(routes):
        for j in range(n):
            si, di, peer = route(j)
            copies.append(pltpu.make_async_remote_copy(
                src_ref=src[k].at[si], dst_ref=dst[k].at[di], send_sem=send_sems.at[sem], recv_sem=recv_sems.at[sem],
                device_id=peer, device_id_type=MESH))
            sem += 1
    return copies


def _xchg_start(srcs, slot_shapes, routes, name):
    k_n = len(srcs)
    n_sem = sum(n for _, n in routes)
    dsts = [lax.empty((n,) + tuple(sh), a.dtype) for sh, a, (_, n) in zip(slot_shapes, srcs, routes)]

    def body(*refs):
        src, dst = refs[:k_n], refs[k_n:2 * k_n]
        send_sems, recv_sems, token = refs[2 * k_n], refs[2 * k_n + 1], refs[-1]
        for cp in _xchg_copies(routes, src, dst, send_sems, recv_sems):
            cp.start()
        token[...] = jnp.zeros_like(token)

    arrays = list(srcs) + dsts
    outs = pl.pallas_call(
        body,
        name=name,
        in_specs=[HBM_SPEC] * (2 * k_n),
        out_specs=(SEM_SPEC, SEM_SPEC, *[HBM_SPEC] * (2 * k_n), pl.BlockSpec(memory_space=pltpu.VMEM)),
        out_shape=(pltpu.SemaphoreType.DMA((n_sem,)), pltpu.SemaphoreType.DMA((n_sem,)),
                   *[pltpu.HBM(a.shape, a.dtype) for a in arrays], jax.ShapeDtypeStruct((8, LANES), F32)),
        input_output_aliases={i: 2 + i for i in range(2 * k_n)},
        compiler_params=pltpu.CompilerParams(has_side_effects=DATAFLOW),
    )(*[_hbm(a) for a in arrays])
    return outs[0], outs[1], list(outs[2:2 + k_n]), list(outs[2 + k_n:2 + 2 * k_n]), outs[-1]


def _xchg_wait(send_sems, recv_sems, srcs, dsts, routes, after, name):
    k_n = len(srcs)

    def body(*refs):
        src, dst = refs[:k_n], refs[k_n:2 * k_n]
        for cp in _xchg_copies(routes, src, dst, refs[2 * k_n], refs[2 * k_n + 1]):
            cp.wait_send()
            cp.wait_recv()

    arrays = list(srcs) + list(dsts)
    outs = pl.pallas_call(
        body,
        name=name,
        in_specs=[HBM_SPEC] * (2 * k_n) + [SEM_SPEC, SEM_SPEC, ANY_SPEC],
        out_specs=tuple([HBM_SPEC] * (2 * k_n)),
        out_shape=tuple(pltpu.HBM(a.shape, a.dtype) for a in arrays),
        input_output_aliases={i: i for i in range(2 * k_n)},
        compiler_params=pltpu.CompilerParams(has_side_effects=DATAFLOW),
    )(*arrays, send_sems, recv_sems, after)
    return list(outs[:k_n]), list(outs[k_n:])


SMALL = ("norm1_gain", "gmlp_v_gain", "w_spatial", "b_spatial", "attn_sinks", "rel_bias_table", "norm2_gain", "final_gain")
LANES = 128


def _swap_start(grads, smalls, tag):
    srcs = list(grads) + [a[None] for a in smalls]
    shapes = [g.shape[1:] for g in grads] + [a.shape for a in smalls]
    routes = [(_route_sibling, 4)] * len(grads) + [(_route_sibling_whole, 1)] * len(smalls)
    return _xchg_start(srcs, shapes, routes, f"rs_{tag}_swap_start"), routes, len(grads)


def _swap_sums(swap, names, after, tag):
    (send1, recv1, src1, land1, _), routes, n_rs = swap
    x, y, c, chips = _chips4()
    blocks = jnp.stack([_dev_index(*chip, c) for chip in chips]).astype(jnp.int32)
    src1, land1 = _xchg_wait(send1, recv1, src1, land1, routes, after, f"rs_{tag}_swap_wait")
    sums = [_rs_sum(g, land, blocks, f"rs_sum_{n}") for g, land, n in zip(src1[:n_rs], land1[:n_rs], names)]
    return sums, (_pair_sum(src1[n_rs:], land1[n_rs:], f"rs_{tag}_sum_small") if len(src1) > n_rs else [])


def _pair_sum(mine, theirs, name):
    def body(*refs):
        n = len(refs) // 3
        for k in range(n):
            refs[2 * n + k][...] = refs[k][...] + refs[n + k][...]

    return _call(
        body,
        name=name,
        out_shape=[jax.ShapeDtypeStruct(a.shape, F32) for a in mine],
        compiler_params=_params(),
    )(*mine, *theirs)


def _chips_start(sums, small_sums, tag):
    sends = [ps[1] for ps in sums] + list(small_sums)
    routes = [(_route_chips, 3)] * len(sums) + [(_route_chips_whole, 3)] * len(small_sums)
    return _xchg_start(sends, [a.shape[1:] for a in sends], routes, f"rs_{tag}_chips_start"), routes


def _chips_wait(chips, after, tag):
    (send2, recv2, src2, land2, _), routes = chips
    return _xchg_wait(send2, recv2, src2, land2, routes, after, f"rs_{tag}_chips_wait")


def kernel(x, p, norm1_gain, w_in, gmlp_v_gain, w_spatial, b_spatial, attn_sinks, rel_bias_table, w_out, norm2_gain, w_ff1, w_ff2, w_ple_proj, w_ple_gate, final_gain, loss_target, m_norm1_gain, m_w_in, m_gmlp_v_gain, m_w_spatial, m_b_spatial, m_attn_sinks, m_rel_bias_table, m_w_out, m_norm2_gain, m_w_ff1, m_w_ff2, m_w_ple_proj, m_w_ple_gate, m_final_gain, v_norm1_gain, v_w_in, v_gmlp_v_gain, v_w_spatial, v_b_spatial, v_attn_sinks, v_rel_bias_table, v_w_out, v_norm2_gain, v_w_ff1, v_w_ff2, v_w_ple_proj, v_w_ple_gate, v_final_gain):
    args = dict(locals())
    s = x.shape[1]
    big = ("w_in", "w_out", "w_ff1", "w_ff2", "w_ple_proj", "w_ple_gate")

    x2, p2, t2 = x.reshape(s, D_MODEL), p.reshape(s, PLE_DIM), loss_target.reshape(s, D_MODEL)
    g1, gv, w_sp, b_sp, sinks, table, g2, gf = (args[n] for n in SMALL)
    bucket = jnp.asarray(_bucket_table())
    b_col = b_sp.reshape(GROUPS, BLOCK, 1)

    def shard(name):
        return args[name][0].T if name.endswith("w_in") else args[name][0]

    lands = _prep_weights([shard(n) for n in big])
    (send_in, recv_in, fly_in), (rest_sems, fly_rest), token = _ag_start(lands[:1], lands[1:])
    mid_in, fwd_send_in, fwd_recv_in = _ag_mid(fly_in, send_in, recv_in, token, "ag_mid_w_in")
    g_in = _ag_end(mid_in, fwd_send_in, fwd_recv_in, "ag_end_w_in")[0]
    full_in = g_in.reshape(D_IN, D_MODEL)

    zuv, qkv, hn1 = _in_proj(x2, g1, full_in)
    fly_rest, relay_sems, relayed = _ag_split_call(
        fly_rest, [("relay_halves", *rest_sems[:2])], ["relays"], zuv, "ag_relay")
    mix, *saved = _mixer_fwd(zuv, qkv, gv, w_sp[0], b_col, sinks, table, bucket, after=relayed)
    fly_rest, fwd_sems, _ = _ag_split_call(
        fly_rest, [("others", *rest_sems[2:]), ("relays", *relay_sems)], ["near", "far"], mix, "ag_mid_rest")
    g_out, g_ff1, g_ff2, g_proj, g_gate = _ag_split_call(
        fly_rest, [("near", *fwd_sems[:2]), ("far", *fwd_sems[2:])], [], mix, "ag_end_rest")[0]
    full_out, full_ff2, full_gate = g_out.reshape(D_MODEL, D_MODEL), g_ff2.reshape(D_FF, D_MODEL), g_gate.reshape(D_MODEL, D_MODEL)
    full_proj = g_proj.transpose(1, 0, 2).reshape(PLE_DIM, D_MODEL)

    (tail_small, dh1, dh1b, dmix, hn2, a, df, h2, dh2, dgl, dpp) = _tail(
        x2, mix, p2, t2, g2, gf.reshape(1, D_MODEL), full_out, g_ff1, full_ff2, full_gate, full_proj)

    ffn_names, mid_names = ("w_ff1", "w_ff2"), ("w_out", "w_ple_proj", "w_ple_gate")
    gw_ff1 = _wgrad(hn2, df, D_MODEL, D_FF // N_DEV, "wgrad_ff1", peer_cols=D_FF // N_DEV)
    gw_ff2 = _wgrad(a, dh2, 1024, D_MODEL, "wgrad_ff2").reshape(N_DEV, D_FF // N_DEV, D_MODEL)
    ffn_swap = _swap_start([gw_ff1, gw_ff2], [], "ffn")
    gw_gate = _wgrad(h2, dgl, D_MODEL, 512, "wgrad_gate", after=ffn_swap[0][4]).reshape(N_DEV, D_MODEL // N_DEV, D_MODEL)
    gw_proj = _wgrad(p2, dpp, PLE_DIM, D_MODEL // 2, "wgrad_proj", peer_cols=D_MODEL // N_DEV, after=gw_gate)
    ffn_sums, _ = _swap_sums(ffn_swap, ffn_names, gw_proj, "ffn")
    ffn_chips = _chips_start(ffn_sums, [], "ffn")
    gw_out = _wgrad(mix, dh1b, D_MODEL, 512, "wgrad_out", after=ffn_chips[0][4]).reshape(N_DEV, D_MODEL // N_DEV, D_MODEL)
    mid_swap = _swap_start([gw_out, gw_proj, gw_gate], [tail_small], "mid")

    dzm, dkv, d_gv, d_wsp, d_bsp, d_attn = _mixer_bwd(
        zuv, qkv, dmix, saved, gv, w_sp[0], b_col, bucket, after=mid_swap[0][4])
    mid_sums, mid_small = _swap_sums(mid_swap, mid_names, dzm, "mid")
    mid_chips = _chips_start(mid_sums, mid_small, "mid")
    gw_in = _wgrad_in(dzm, dkv, hn1, after=mid_chips[0][4]).reshape(N_DEV, D_IN // N_DEV, D_MODEL)
    dx, d_g1 = _in_bwd(x2, dh1, dzm, dkv, g1, full_in, after=gw_in)
    grad_x = dx.reshape(x.shape)
    in_swap = _swap_start([gw_in], [d_g1, d_gv, d_wsp, d_bsp, d_attn], "in")
    in_sums, in_small = _swap_sums(in_swap, ("w_in",), in_swap[0][4], "in")
    in_chips = _chips_start(in_sums, in_small, "in")

    grads, deltas, new_m, new_v = {}, {}, {}, {}
    after, small_own, small_recv = in_chips[0][4], [], []
    for chips, sums, names, tag in ((ffn_chips, ffn_sums, ffn_names, "ffn"), (mid_chips, mid_sums, mid_names, "mid"),
                                    (in_chips, in_sums, ("w_in",), "in")):
        sent, recvs = _chips_wait(chips, after, tag)
        small_own += sent[len(names):]
        small_recv += recvs[len(names):]
        for n, (part, _), recv in zip(names, sums, recvs):
            g, d, mo, vo = _final_adamw(part, recv, shard(n), shard("m_" + n), shard("v_" + n), "adamw_" + n)
            for dst, arr in zip((grads, deltas, new_m, new_v), (g, d, mo, vo)):
                dst[n] = (arr.T if n == "w_in" else arr)[None]
            after = d

    views = {"w_spatial": (GROUPS, BLOCK, BLOCK), "b_spatial": (GROUPS, BLOCK), "final_gain": (1, D_MODEL)}
    small_in = [[args[pre + n].reshape(views.get(n, args[n].shape)) for n in SMALL] for pre in ("", "m_", "v_")]
    loss, *small_out = _adamw_small(small_own, small_recv, *small_in, after=after)
    for dst, arrays in zip((grads, deltas, new_m, new_v), small_out):
        for n, arr in zip(SMALL, arrays):
            dst[n] = arr.reshape(args[n].shape)
    loss = loss[0, 0]

    order = ("norm1_gain", "w_in", "gmlp_v_gain", "w_spatial", "b_spatial", "attn_sinks", "rel_bias_table", "w_out",
             "norm2_gain", "w_ff1", "w_ff2", "w_ple_proj", "w_ple_gate", "final_gain")
    return (loss, grad_x, *[grads[n] for n in order], *[deltas[n] for n in order],
            *[new_m[n] for n in order], *[new_v[n] for n in order])
```

```python
import functools
import math

import numpy as np
import jax
import jax.numpy as jnp
from jax import lax
from jax.experimental import pallas as pl
from jax.experimental.pallas import tpu as pltpu

F32 = jnp.float32
BF = jnp.bfloat16
MESH = pl.DeviceIdType.MESH
N_DEV = 8

D_MODEL = 1024
PLE_DIM = 256
D_GMLP = 512
GROUPS = 4
GDIM = 128
BLOCK = 128
D_ATTN = 512
HEAD_DIM = 64
N_Q = 8
Q_PER_KV = 4
N_KV = N_Q // Q_PER_KV
ROWS4 = Q_PER_KV * BLOCK
D_KV = 128
D_FF = 4096
D_IN = 1792
D_MAIN = 2 * D_GMLP + D_ATTN
REL_BUCKETS = 32
EPS = 1e-6
NEG_INF = -1e30
SCALE = HEAD_DIM ** -0.5
GELU_C = math.sqrt(2.0 / math.pi)
GELU_A = 0.044715

ADAM_LR = 0.001
ADAM_B1 = 0.9
ADAM_B2 = 0.999
ADAM_EPS = 1e-08
ADAM_WD = 0.01
ADAM_STEP = 10

V7X_VMEM_LIMIT = 60000 * 1024
TOK_TILE = 256
IO_TOK_TILE = 512


def _call(body, after=None, **kw):
    if after is None:
        return pl.pallas_call(body, **kw)
    n_in = len(kw["in_specs"])

    def ordered(*refs):
        body(*refs[:n_in], *refs[n_in + 1:])

    kw["in_specs"] = list(kw["in_specs"]) + [pl.BlockSpec(memory_space=pl.ANY)]
    fn = pl.pallas_call(ordered, **kw)
    return lambda *operands: fn(*operands, after)


def _params(sem=None):
    if sem is None:
        return pltpu.CompilerParams(vmem_limit_bytes=V7X_VMEM_LIMIT)
    return pltpu.CompilerParams(dimension_semantics=sem, vmem_limit_bytes=V7X_VMEM_LIMIT)


def _nn(a, b):
    return jnp.dot(a, b, preferred_element_type=F32)


def _nt(a, b):
    return lax.dot_general(a, b, (((1,), (1,)), ((), ())), preferred_element_type=F32)


def _tn(a, b):
    return lax.dot_general(a, b, (((0,), (0,)), ((), ())), preferred_element_type=F32)


def _gelu_tanh(x):
    return jnp.tanh(GELU_C * (x + GELU_A * (x * x * x)))


def _gelu(x, t):
    return x * (0.5 * (1.0 + t))


def _gelu_and_grad(x, t):
    cdf = 0.5 * (1.0 + t)
    return x * cdf, cdf + 0.5 * x * (1.0 - t * t) * (GELU_C * (1.0 + 3.0 * GELU_A * (x * x)))


def _rms_scale(x):
    return lax.rsqrt(jnp.mean(x * x, axis=-1, keepdims=True) + EPS)


def _rms_bwd(dxn, x, r):
    return r * dxn - x * ((r * r * r) * jnp.mean(dxn * x, axis=-1, keepdims=True))


def _bucket_table():
    a = np.arange(BLOCK)[:, None]
    j = np.arange(2 * BLOCK)[None, :]
    n = BLOCK + a - j
    valid = (n >= 0) & (n < BLOCK)
    nc = np.maximum(n, 0)
    max_exact = REL_BUCKETS // 2
    nf = np.maximum(nc, 1).astype(np.float32)
    large = max_exact + (
        np.log(nf / np.float32(max_exact)) / np.float32(math.log(BLOCK / max_exact)) * np.float32(REL_BUCKETS - max_exact)
    ).astype(np.int32)
    large = np.minimum(large, REL_BUCKETS - 1)
    bucket = np.where(nc < max_exact, nc, large)
    return np.where(valid, bucket, -1).astype(np.int32)


def _in_proj(x, g1, w_in_t):
    s = x.shape[0]
    tm = min(IO_TOK_TILE, s)

    def body(x_ref, g_ref, w_ref, zuv_ref, qkv_ref, hn_ref):
        xv = x_ref[...]
        hn = ((xv * _rms_scale(xv)) * g_ref[...]).astype(BF)
        hn_ref[...] = hn
        z = _nt(hn, w_ref[...])
        zuv_ref[...] = z[:, : 2 * D_GMLP]
        qkv_ref[...] = z[:, 2 * D_GMLP:].astype(BF)

    return _call(
        body,
        name="in_proj",
        grid=(s // tm,),
        in_specs=[
            pl.BlockSpec((tm, D_MODEL), lambda i: (i, 0)),
            pl.BlockSpec((1, D_MODEL), lambda i: (0, 0)),
            pl.BlockSpec((D_IN, D_MODEL), lambda i: (0, 0)),
        ],
        out_specs=[
            pl.BlockSpec((tm, 2 * D_GMLP), lambda i: (i, 0)),
            pl.BlockSpec((tm, D_ATTN + 2 * D_KV), lambda i: (i, 0)),
            pl.BlockSpec((tm, D_MODEL), lambda i: (i, 0)),
        ],
        out_shape=[
            jax.ShapeDtypeStruct((s, 2 * D_GMLP), F32),
            jax.ShapeDtypeStruct((s, D_ATTN + 2 * D_KV), BF),
            jax.ShapeDtypeStruct((s, D_MODEL), BF),
        ],
        compiler_params=_params(("arbitrary",)),
    )(x, g1, w_in_t)


def _head_rows(h):
    kh, g = divmod(h, Q_PER_KV)
    return kh, slice(g * BLOCK, (g + 1) * BLOCK)


def _build_bias(bias_ref, bucket_ref, table_ref):
    bucket = bucket_ref[...]
    for h in range(N_Q):
        acc = jnp.zeros((BLOCK, 2 * BLOCK), F32)
        for b in range(REL_BUCKETS):
            acc = jnp.where(bucket == b, table_ref[h, b], acc)
        kh, rows = _head_rows(h)
        bias_ref[kh, rows, :] = acc


def _window_masks(i):
    row = lax.broadcasted_iota(jnp.int32, (ROWS4, BLOCK), 0) & (BLOCK - 1)
    col = lax.broadcasted_iota(jnp.int32, (ROWS4, BLOCK), 1)
    return (col > row) & (i > 0), col <= row


def _stack_heads(ref, kh, offset):
    first = offset + kh * Q_PER_KV * HEAD_DIM
    return jnp.concatenate(
        [ref[:, first + g * HEAD_DIM: first + (g + 1) * HEAD_DIM].astype(BF) for g in range(Q_PER_KV)], axis=0)


def _stack_sinks(sink_ref, kh):
    return jnp.concatenate([jnp.full((BLOCK, 1), sink_ref[0, kh * Q_PER_KV + g], F32) for g in range(Q_PER_KV)], axis=0)


def _tril_bf16(w_ref, g):
    row = lax.broadcasted_iota(jnp.int32, (BLOCK, BLOCK), 0)
    col = lax.broadcasted_iota(jnp.int32, (BLOCK, BLOCK), 1)
    return jnp.where(col <= row, w_ref[g], 0.0).astype(BF)


def _attn_probs(q_h, k_prev, k_cur, bias_h, sink, valid_prev, valid_cur):
    l_prev = jnp.where(valid_prev, _nt(q_h, k_prev) * SCALE + bias_h[:, :BLOCK], NEG_INF)
    l_cur = jnp.where(valid_cur, _nt(q_h, k_cur) * SCALE + bias_h[:, BLOCK:], NEG_INF)
    m = jnp.maximum(jnp.maximum(jnp.max(l_prev, axis=-1, keepdims=True), jnp.max(l_cur, axis=-1, keepdims=True)), sink)
    e_prev = jnp.exp(l_prev - m)
    e_cur = jnp.exp(l_cur - m)
    e_sink = jnp.exp(sink - m)
    denom = jnp.sum(e_prev, axis=-1, keepdims=True) + jnp.sum(e_cur, axis=-1, keepdims=True) + e_sink
    return e_prev / denom, e_cur / denom, e_sink / denom


def _mixer_specs(nb):
    cl = lambda i: jnp.minimum(i, nb - 1)
    return [
        pl.BlockSpec((BLOCK, 2 * D_GMLP), lambda i: (cl(i), 0)),
        pl.BlockSpec((BLOCK, D_ATTN), lambda i: (cl(i), 0)),
        pl.BlockSpec((BLOCK, 2 * D_KV), lambda i: (cl(i), D_ATTN // (2 * D_KV))),
        pl.BlockSpec((BLOCK, 2 * D_KV), lambda i: (jnp.maximum(cl(i) - 1, 0), D_ATTN // (2 * D_KV))),
        pl.BlockSpec((1, D_GMLP), lambda i: (0, 0)),
        pl.BlockSpec((GROUPS, BLOCK, BLOCK), lambda i: (0, 0, 0)),
        pl.BlockSpec((GROUPS, BLOCK, 1), lambda i: (0, 0, 0)),
        pl.BlockSpec(memory_space=pltpu.SMEM),
        pl.BlockSpec(memory_space=pltpu.SMEM),
        pl.BlockSpec((BLOCK, 2 * BLOCK), lambda i: (0, 0)),
    ]


def _mixer_fwd(zuv, qkv, gv, w_sp, b_sp, sinks, table, bucket, after=None):
    s = zuv.shape[0]
    nb = s // BLOCK

    def body(zuv_ref, q_ref, kvc_ref, kvp_ref, gv_ref, w_ref, b_ref, sink_ref, table_ref, bucket_ref,
             mix_ref, tanh_ref, prob_ref, psink_ref, bias_ref):
        i = pl.program_id(0)

        @pl.when(i == 0)
        def _():
            _build_bias(bias_ref, bucket_ref, table_ref)

        t = _gelu_tanh(zuv_ref[...])
        tanh_ref[...] = t
        u = _gelu(zuv_ref[:, :D_GMLP], t[:, :D_GMLP])
        vg = _gelu(zuv_ref[:, D_GMLP:], t[:, D_GMLP:])
        for g in range(GROUPS):
            sl = slice(g * GDIM, (g + 1) * GDIM)
            vg_g = vg[:, sl]
            vn = ((vg_g * _rms_scale(vg_g)) * gv_ref[:, sl]).astype(BF)
            sv = _nn(_tril_bf16(w_ref, g), vn) + b_ref[g]
            mix_ref[:, sl] = (u[:, sl] * sv).astype(BF)

        valid_prev, valid_cur = _window_masks(i)
        for kh in range(N_KV):
            ksl = slice(kh * HEAD_DIM, (kh + 1) * HEAD_DIM)
            vsl = slice(D_KV + kh * HEAD_DIM, D_KV + (kh + 1) * HEAD_DIM)
            q4 = _stack_heads(q_ref, kh, 0)
            p_prev, p_cur, p_sink = _attn_probs(
                q4, kvp_ref[:, ksl], kvc_ref[:, ksl], bias_ref[kh], _stack_sinks(sink_ref, kh), valid_prev, valid_cur)
            prob_ref[0, kh, :, :BLOCK] = p_prev
            prob_ref[0, kh, :, BLOCK:] = p_cur
            psink_ref[0, kh] = jnp.broadcast_to(p_sink, (ROWS4, LANES))
            o4 = _nn(p_prev.astype(BF), kvp_ref[:, vsl]) + _nn(p_cur.astype(BF), kvc_ref[:, vsl])
            for g in range(Q_PER_KV):
                first = D_GMLP + (kh * Q_PER_KV + g) * HEAD_DIM
                mix_ref[:, first:first + HEAD_DIM] = o4[g * BLOCK:(g + 1) * BLOCK].astype(BF)

    return _call(
        body,
        name="mixer_fwd",
        after=after,
        grid=(nb,),
        in_specs=_mixer_specs(nb),
        out_specs=[
            pl.BlockSpec((BLOCK, D_MODEL), lambda i: (i, 0)),
            pl.BlockSpec((BLOCK, 2 * D_GMLP), lambda i: (i, 0)),
            pl.BlockSpec((1, N_KV, ROWS4, 2 * BLOCK), lambda i: (i, 0, 0, 0)),
            pl.BlockSpec((1, N_KV, ROWS4, LANES), lambda i: (i, 0, 0, 0)),
        ],
        out_shape=[
            jax.ShapeDtypeStruct((s, D_MODEL), BF),
            jax.ShapeDtypeStruct((s, 2 * D_GMLP), F32),
            jax.ShapeDtypeStruct((nb, N_KV, ROWS4, 2 * BLOCK), F32),
            jax.ShapeDtypeStruct((nb, N_KV, ROWS4, LANES), F32),
        ],
        scratch_shapes=[pltpu.VMEM((N_KV, ROWS4, 2 * BLOCK), F32)],
        compiler_params=_params(("arbitrary",)),
    )(zuv, qkv, qkv, qkv, gv, w_sp, b_sp, sinks, table, bucket)


def _tail(x, mix, p, t, g2, gf, w_out, w_ff1, w_ff2, w_gate, w_proj):
    s = x.shape[0]
    tm = min(TOK_TILE, s)
    n_ff = w_ff1.shape[0]
    fc = D_FF // n_ff

    def body(x_ref, mix_ref, p_ref, t_ref, g2_ref, gf_ref, wo_hbm, w1_hbm, w2_hbm, wg_hbm, wp_hbm,
             small_ref, dh1_ref, dh1b_ref, dmix_ref, hn2_ref, a_ref, df_ref, h2_ref, dh2_ref, dgl_ref, dpp_ref,
             f_ref, wo_ref, w1_ref, w2_ref, wg_ref, wp_ref, w_sems):
        i = pl.program_id(0)
        loads = [pltpu.make_async_copy(src, dst, w_sems.at[k]) for k, (src, dst) in enumerate(
            ((wo_hbm, wo_ref), (w1_hbm, w1_ref), (w2_hbm, w2_ref), (wg_hbm, wg_ref), (wp_hbm, wp_ref)))]

        def wait_in_first_step(*which):
            @pl.when(i == 0)
            def _():
                for k in which:
                    loads[k].wait()

        @pl.when(i == 0)
        def _():
            small_ref[...] = jnp.zeros_like(small_ref)
            for load in loads:
                load.start()

        wait_in_first_step(0)
        h1 = x_ref[...] + _nn(mix_ref[...], wo_ref[...])
        r2 = _rms_scale(h1)
        hn2 = ((h1 * r2) * g2_ref[...]).astype(BF)
        hn2_ref[...] = hn2
        wait_in_first_step(1, 2)
        h2 = h1
        for c in range(n_ff):
            f = _nn(hn2, w1_ref[c])
            f_ref[:, c * fc:(c + 1) * fc] = f
            a = jnp.square(jnp.maximum(f, 0.0)).astype(BF)
            a_ref[:, c * fc:(c + 1) * fc] = a
            h2 = h2 + _nn(a, w2_ref[c * fc:(c + 1) * fc, :])
        h2b = h2.astype(BF)
        h2_ref[...] = h2b
        wait_in_first_step(3, 4)
        gate = jax.nn.sigmoid(_nn(h2b, wg_ref[...]))
        pp = _nn(p_ref[...].astype(BF), wp_ref[...])
        h3 = h2 + gate * pp
        rf = _rms_scale(h3)
        gf_v = gf_ref[...]
        err = (h3 * rf) * gf_v - t_ref[...]
        small_ref[2:3, :] += jnp.sum(jnp.sum(err * err, axis=-1, keepdims=True), axis=0, keepdims=True) * (0.5 / D_MODEL)

        dy = err * (1.0 / D_MODEL)
        small_ref[1:2, :] += jnp.sum(dy * (h3 * rf), axis=0, keepdims=True)
        dh3 = _rms_bwd(dy * gf_v, h3, rf)
        dpp_ref[...] = (dh3 * gate).astype(BF)
        dgl = ((dh3 * pp) * (gate * (1.0 - gate))).astype(BF)
        dgl_ref[...] = dgl
        dh2 = dh3 + _nt(dgl, wg_ref[...])
        dh2b = dh2.astype(BF)
        dh2_ref[...] = dh2b
        dhn2 = jnp.zeros((tm, D_MODEL), F32)
        for c in range(n_ff):
            da = _nt(dh2b, w2_ref[c * fc:(c + 1) * fc, :])
            df = (da * (2.0 * jnp.maximum(f_ref[:, c * fc:(c + 1) * fc], 0.0))).astype(BF)
            df_ref[:, c * fc:(c + 1) * fc] = df
            dhn2 = dhn2 + _nt(df, w1_ref[c])
        small_ref[0:1, :] += jnp.sum(dhn2 * (h1 * r2), axis=0, keepdims=True)
        dh1 = dh2 + _rms_bwd(dhn2 * g2_ref[...], h1, r2)
        dh1_ref[...] = dh1
        dh1b = dh1.astype(BF)
        dh1b_ref[...] = dh1b
        dmix_ref[...] = _nt(dh1b, wo_ref[...])

    tile = lambda cols: pl.BlockSpec((tm, cols), lambda i: (i, 0))
    row = pl.BlockSpec((1, D_MODEL), lambda i: (0, 0))
    act = lambda cols, dt: jax.ShapeDtypeStruct((s, cols), dt)
    return _call(
        body,
        name="tail",
        grid=(s // tm,),
        in_specs=[tile(D_MODEL), tile(D_MODEL), tile(PLE_DIM), tile(D_MODEL), row, row,
                  *[pl.BlockSpec(memory_space=pl.ANY)] * 5],
        out_specs=[pl.BlockSpec((8, D_MODEL), lambda i: (0, 0)), tile(D_MODEL), tile(D_MODEL), tile(D_MODEL), tile(D_MODEL), tile(D_FF),
                   tile(D_FF), tile(D_MODEL), tile(D_MODEL), tile(D_MODEL), tile(D_MODEL)],
        out_shape=[jax.ShapeDtypeStruct((8, D_MODEL), F32),
                   act(D_MODEL, F32), act(D_MODEL, BF), act(D_MODEL, F32), act(D_MODEL, BF), act(D_FF, BF), act(D_FF, BF), act(D_MODEL, BF),
                   act(D_MODEL, BF), act(D_MODEL, BF), act(D_MODEL, BF)],
        scratch_shapes=[pltpu.VMEM((tm, D_FF), F32)] + [pltpu.VMEM(w.shape, w.dtype) for w in (w_out, w_ff1, w_ff2, w_gate, w_proj)]
        + [pltpu.SemaphoreType.DMA((5,))],
        compiler_params=_params(("arbitrary",)),
    )(x, mix, p, t, g2, gf, w_out, w_ff1, w_ff2, w_gate, w_proj)


def _mixer_bwd(zuv, qkv, dmix, saved, gv, w_sp, b_sp, bucket, after=None):
    s = zuv.shape[0]
    nb = s // BLOCK

    def body(zuv_ref, q_ref, kvc_ref, kvp_ref, gv_ref, w_ref, b_ref, bucket_ref, dmix_ref, tanh_ref, prob_ref, psink_ref,
             dzm_ref, dkv_ref, dgv_ref, dw_ref, db_ref, dattn_ref,
             dbias_ref, carry_ref, dsink_acc, db_acc):
        i = pl.program_id(0)

        @pl.when(i == 0)
        def _():
            dbias_ref[...] = jnp.zeros_like(dbias_ref)
            carry_ref[...] = jnp.zeros_like(carry_ref)
            dsink_acc[...] = jnp.zeros_like(dsink_acc)
            dgv_ref[...] = jnp.zeros_like(dgv_ref)
            dw_ref[...] = jnp.zeros_like(dw_ref)
            db_acc[...] = jnp.zeros_like(db_acc)

        @pl.when(i < nb)
        def _():
            u, du_dz = _gelu_and_grad(zuv_ref[:, :D_GMLP], tanh_ref[:, :D_GMLP])
            vg, dvg_dz = _gelu_and_grad(zuv_ref[:, D_GMLP:], tanh_ref[:, D_GMLP:])
            for g in range(GROUPS):
                sl = slice(g * GDIM, (g + 1) * GDIM)
                vg_g = vg[:, sl]
                rg = _rms_scale(vg_g)
                vhat = vg_g * rg
                gain = gv_ref[:, sl]
                vn = (vhat * gain).astype(BF)
                w_g = _tril_bf16(w_ref, g)
                sv = _nn(w_g, vn) + b_ref[g]
                dmix_a = dmix_ref[:, sl]
                dsv = dmix_a * u[:, sl]
                dsvb = dsv.astype(BF)
                db_acc[g] += jnp.sum(dsv, axis=-1, keepdims=True)
                dw_ref[g] += _nt(dsvb, vn)
                dvn = _tn(w_g, dsvb)
                dgv_ref[:, sl] += jnp.sum(dvn * vhat, axis=0, keepdims=True)
                dvg = _rms_bwd(dvn * gain, vg_g, rg)
                dzm_ref[:, sl] = ((dmix_a * sv) * du_dz[:, sl]).astype(BF)
                dzm_ref[:, D_GMLP + g * GDIM: D_GMLP + (g + 1) * GDIM] = (dvg * dvg_dz[:, sl]).astype(BF)

            for kh in range(N_KV):
                ksl = slice(kh * HEAD_DIM, (kh + 1) * HEAD_DIM)
                vsl = slice(D_KV + kh * HEAD_DIM, D_KV + (kh + 1) * HEAD_DIM)
                k_prev, k_cur = kvp_ref[:, ksl], kvc_ref[:, ksl]
                v_prev, v_cur = kvp_ref[:, vsl], kvc_ref[:, vsl]
                q4 = _stack_heads(q_ref, kh, 0)
                p_prev, p_cur, p_sink = prob_ref[0, kh, :, :BLOCK], prob_ref[0, kh, :, BLOCK:], psink_ref[0, kh, :, 0:1]
                do4 = _stack_heads(dmix_ref, kh, D_GMLP)
                dp_prev = _nt(do4, v_prev)
                dp_cur = _nt(do4, v_cur)
                delta = jnp.sum(p_prev * dp_prev, axis=-1, keepdims=True) + jnp.sum(p_cur * dp_cur, axis=-1, keepdims=True)
                ds_prev = p_prev * (dp_prev - delta)
                ds_cur = p_cur * (dp_cur - delta)
                dsink_acc[kh] -= p_sink * delta
                dbias_ref[kh, :, :BLOCK] += ds_prev
                dbias_ref[kh, :, BLOCK:] += ds_cur
                dsb_prev = ds_prev.astype(BF)
                dsb_cur = ds_cur.astype(BF)
                dq4 = (_nn(dsb_prev, k_prev) + _nn(dsb_cur, k_cur)) * SCALE
                for g in range(Q_PER_KV):
                    first = 2 * D_GMLP + (kh * Q_PER_KV + g) * HEAD_DIM
                    dzm_ref[:, first:first + HEAD_DIM] = dq4[g * BLOCK:(g + 1) * BLOCK].astype(BF)
                dkv_ref[:, ksl] = (carry_ref[:, ksl] + _tn(dsb_prev, q4) * SCALE).astype(BF)
                dkv_ref[:, vsl] = (carry_ref[:, vsl] + _tn(p_prev.astype(BF), do4)).astype(BF)
                carry_ref[:, ksl] = _tn(dsb_cur, q4) * SCALE
                carry_ref[:, vsl] = _tn(p_cur.astype(BF), do4)

        @pl.when(i == nb)
        def _():
            dkv_ref[...] = carry_ref[...].astype(BF)
            row = lax.broadcasted_iota(jnp.int32, (BLOCK, BLOCK), 0)
            col = lax.broadcasted_iota(jnp.int32, (BLOCK, BLOCK), 1)
            for g in range(GROUPS):
                dw_ref[g] = jnp.where(col <= row, dw_ref[g], 0.0)
                db_ref[g:g + 1, :] = jnp.sum(jnp.where(col == row, db_acc[g], 0.0), axis=0, keepdims=True)
            bucket = bucket_ref[...]
            for b in range(N_Q, REL_BUCKETS):
                dattn_ref[N_Q, b] = 0.0
            for h in range(N_Q):
                kh, rows = _head_rows(h)
                dattn_ref[N_Q, h] = jnp.sum(dsink_acc[kh, rows, :])
                dbh = dbias_ref[kh, rows, :]
                for b in range(REL_BUCKETS):
                    dattn_ref[h, b] = jnp.sum(jnp.where(bucket == b, dbh, 0.0))

    cl = lambda i: jnp.minimum(i, nb - 1)
    const = lambda shape: pl.BlockSpec(shape, lambda i: (0,) * len(shape))
    return _call(
        body,
        name="mixer_bwd",
        after=after,
        grid=(nb + 1,),
        in_specs=_mixer_specs(nb)[:7] + [
            const((BLOCK, 2 * BLOCK)),
            pl.BlockSpec((BLOCK, D_MODEL), lambda i: (cl(i), 0)),
            pl.BlockSpec((BLOCK, 2 * D_GMLP), lambda i: (cl(i), 0)),
            pl.BlockSpec((1, N_KV, ROWS4, 2 * BLOCK), lambda i: (cl(i), 0, 0, 0)),
            pl.BlockSpec((1, N_KV, ROWS4, LANES), lambda i: (cl(i), 0, 0, 0)),
        ],
        out_specs=[
            pl.BlockSpec((BLOCK, D_MAIN), lambda i: (cl(i), 0)),
            pl.BlockSpec((BLOCK, 2 * D_KV), lambda i: (jnp.maximum(i - 1, 0), 0)),
            const((1, D_GMLP)),
            const((GROUPS, BLOCK, BLOCK)),
            const((GROUPS, BLOCK)),
            pl.BlockSpec(memory_space=pltpu.SMEM),
        ],
        out_shape=[
            jax.ShapeDtypeStruct((s, D_MAIN), BF),
            jax.ShapeDtypeStruct((s, 2 * D_KV), BF),
            jax.ShapeDtypeStruct((1, D_GMLP), F32),
            jax.ShapeDtypeStruct((GROUPS, BLOCK, BLOCK), F32),
            jax.ShapeDtypeStruct((GROUPS, BLOCK), F32),
            jax.ShapeDtypeStruct((N_Q + 1, REL_BUCKETS), F32),
        ],
        scratch_shapes=[
            pltpu.VMEM((N_KV, ROWS4, 2 * BLOCK), F32),
            pltpu.VMEM((BLOCK, 2 * D_KV), F32),
            pltpu.VMEM((N_KV, ROWS4, 1), F32),
            pltpu.VMEM((GROUPS, BLOCK, 1), F32),
        ],
        compiler_params=_params(("arbitrary",)),
    )(zuv, qkv, qkv, qkv, gv, w_sp, b_sp, bucket, dmix, *saved)


def _in_bwd(x, dh1, dzm, dkv, g1, w_in_t, after=None):
    s = x.shape[0]
    tm = min(IO_TOK_TILE, s)

    def body(x_ref, dh1_ref, dzm_ref, dkv_ref, g_ref, w_ref, dx_ref, dg_ref):
        @pl.when(pl.program_id(0) == 0)
        def _():
            dg_ref[...] = jnp.zeros_like(dg_ref)

        dhn = _nn(dzm_ref[...], w_ref[:D_MAIN, :]) + _nn(dkv_ref[...], w_ref[D_MAIN:, :])
        xv = x_ref[...]
        r = _rms_scale(xv)
        dg_ref[...] += jnp.sum(dhn * (xv * r), axis=0, keepdims=True)
        dx_ref[...] = dh1_ref[...] + _rms_bwd(dhn * g_ref[...], xv, r)

    tile = lambda cols: pl.BlockSpec((tm, cols), lambda i: (i, 0))
    row = pl.BlockSpec((1, D_MODEL), lambda i: (0, 0))
    return _call(
        body,
        name="in_bwd",
        after=after,
        grid=(s // tm,),
        in_specs=[tile(D_MODEL), tile(D_MODEL), tile(D_MAIN), tile(2 * D_KV), row, pl.BlockSpec((D_IN, D_MODEL), lambda i: (0, 0))],
        out_specs=[tile(D_MODEL), row],
        out_shape=[jax.ShapeDtypeStruct((s, D_MODEL), F32), jax.ShapeDtypeStruct((1, D_MODEL), F32)],
        compiler_params=_params(("arbitrary",)),
    )(x, dh1, dzm, dkv, g1, w_in_t)


def _wgrad_in(dzm, dkv, hn, after=None):
    s = hn.shape[0]
    tm = 2 * D_KV
    n_main = D_MAIN // tm

    def body(dzm_ref, dkv_ref, hn_ref, o_ref):
        i = pl.program_id(0)

        @pl.when(i < n_main)
        def _():
            o_ref[...] = _tn(dzm_ref[...], hn_ref[...])

        @pl.when(i == n_main)
        def _():
            o_ref[...] = _tn(dkv_ref[...], hn_ref[...])

    return _call(
        body,
        name="wgrad_in",
        after=after,
        grid=(n_main + 1,),
        in_specs=[
            pl.BlockSpec((s, tm), lambda i: (0, jnp.minimum(i, n_main - 1))),
            pl.BlockSpec((s, tm), lambda i: (0, 0)),
            pl.BlockSpec((s, D_MODEL), lambda i: (0, 0)),
        ],
        out_specs=pl.BlockSpec((tm, D_MODEL), lambda i: (i, 0)),
        out_shape=jax.ShapeDtypeStruct((D_IN, D_MODEL), F32),
        compiler_params=_params(("arbitrary",)),
    )(dzm, dkv, hn)


def _wgrad(a, b, tm, tn, name, peer_cols=0, after=None):
    s, m = a.shape
    n = b.shape[1]

    def body(a_ref, b_ref, o_ref, at_ref):
        @pl.when(pl.program_id(1) == 0)
        def _():
            at_ref[...] = a_ref[...].astype(BF).T

        r = _nn(at_ref[...], b_ref[...])
        if peer_cols:
            for q in range(tn // peer_cols):
                o_ref[q] = r[:, q * peer_cols:(q + 1) * peer_cols]
        else:
            o_ref[...] = r

    if peer_cols:
        out_spec = pl.BlockSpec((tn // peer_cols, tm, peer_cols), lambda i, j: (j, i, 0))
        out_shape = jax.ShapeDtypeStruct((n // peer_cols, m, peer_cols), F32)
    else:
        out_spec = pl.BlockSpec((tm, tn), lambda i, j: (i, j))
        out_shape = jax.ShapeDtypeStruct((m, n), F32)
    return _call(
        body,
        name=name,
        after=after,
        grid=(m // tm, n // tn),
        in_specs=[pl.BlockSpec((s, tm), lambda i, j: (0, i)), pl.BlockSpec((s, tn), lambda i, j: (0, j))],
        out_specs=out_spec,
        out_shape=out_shape,
        scratch_shapes=[pltpu.VMEM((tm, s), BF)],
        compiler_params=_params(("arbitrary", "arbitrary")),
    )(a, b)


def _adamw_math(w, g, m, v):
    m_new = ADAM_B1 * m + (1.0 - ADAM_B1) * g
    v_new = ADAM_B2 * v + (1.0 - ADAM_B2) * jnp.square(g)
    m_hat = m_new / (1.0 - ADAM_B1 ** ADAM_STEP)
    v_hat = v_new / (1.0 - ADAM_B2 ** ADAM_STEP)
    delta = -ADAM_LR * (m_hat / (jnp.sqrt(v_hat) + ADAM_EPS) + ADAM_WD * w)
    return delta, m_new, v_new


def _final_adamw(part, recv, w, m, v, name):
    r, c = w.shape
    tr = min(r, 512)

    def body(p_ref, r_ref, w_ref, m_ref, v_ref, g_ref, d_ref, mo_ref, vo_ref):
        g = p_ref[...]
        for j in range(3):
            g = g + r_ref[j].astype(F32)
        g_ref[...] = g
        d_ref[...], mo_ref[...], vo_ref[...] = _adamw_math(w_ref[...], g, m_ref[...], v_ref[...])

    spec = pl.BlockSpec((tr, c), lambda i: (i, 0))
    return _call(
        body,
        name=name,
        grid=(r // tr,),
        in_specs=[spec, pl.BlockSpec((3, tr, c), lambda i: (0, i, 0)), spec, spec, spec],
        out_specs=[spec] * 4,
        out_shape=[jax.ShapeDtypeStruct((r, c), F32)] * 4,
        compiler_params=_params(("arbitrary",)),
    )(part, recv, w, m, v)


def _rs_sum(g, land, blocks, name):
    _, r, c = g.shape
    tr = min(r, 256)

    def body(blk_ref, g0_ref, g1_ref, g2_ref, g3_ref, l_ref, part_ref, send_ref):
        part_ref[...] = g0_ref[0] + l_ref[0]
        for j, gj_ref in enumerate((g1_ref, g2_ref, g3_ref)):
            send_ref[j] = (gj_ref[0] + l_ref[j + 1]).astype(BF)

    def pick(j):
        return pl.BlockSpec((1, tr, c), lambda i, blk: (blk[j], i, 0))

    return _call(
        body,
        name=name,
        grid_spec=pltpu.PrefetchScalarGridSpec(
            num_scalar_prefetch=1,
            grid=(r // tr,),
            in_specs=[pick(0), pick(1), pick(2), pick(3), pl.BlockSpec((4, tr, c), lambda i, blk: (0, i, 0))],
            out_specs=[pl.BlockSpec((tr, c), lambda i, blk: (i, 0)), pl.BlockSpec((3, tr, c), lambda i, blk: (0, i, 0))],
        ),
        out_shape=[jax.ShapeDtypeStruct((r, c), F32), jax.ShapeDtypeStruct((3, r, c), BF)],
        compiler_params=_params(("arbitrary",)),
    )(blocks, g, g, g, g, land)


def _adamw_small(own, recv, weights, moms, vels, after):
    n_w = len(weights)

    def body(*refs):
        own_refs, recv_refs = refs[:6], refs[6:12]
        w_refs, m_refs, v_refs = refs[12:12 + n_w], refs[12 + n_w:12 + 2 * n_w], refs[12 + 2 * n_w:12 + 3 * n_w]
        outs = refs[12 + 3 * n_w:]
        loss_ref, g_refs, d_refs = outs[0], outs[1:1 + n_w], outs[1 + n_w:1 + 2 * n_w]
        mo_refs, vo_refs = outs[1 + 2 * n_w:1 + 3 * n_w], outs[1 + 3 * n_w:]
        x, y = lax.axis_index("x"), lax.axis_index("y")

        def total(k, *index):
            index = index or (slice(None),) * (len(own_refs[k].shape) - 1)
            across = [own_refs[k][(0, *index)]] + [recv_refs[k][(j, *index)] for j in range(3)]
            acc = None
            for cx in range(2):
                for cy in range(2):
                    flip = (cx ^ x) + 2 * (cy ^ y)
                    part = jnp.where(flip == 0, across[0], jnp.where(flip == 1, across[1], jnp.where(flip == 2, across[2], across[3])))
                    acc = part if acc is None else acc + part
            return acc

        grads = [
            total(1), total(2), total(3), total(4),
            total(5, slice(N_Q, None), slice(0, N_Q)), total(5, slice(0, N_Q), slice(None)),
            total(0, slice(0, 1), slice(None)), total(0, slice(1, 2), slice(None))]
        loss_ref[...] = total(0, slice(2, 3), slice(0, 1))
        for k in range(n_w):
            g_refs[k][...] = grads[k]
            d_refs[k][...], mo_refs[k][...], vo_refs[k][...] = _adamw_math(w_refs[k][...], grads[k], m_refs[k][...], v_refs[k][...])

    shapes = [jax.ShapeDtypeStruct(w.shape, F32) for w in weights]
    outs = _call(
        body,
        name="adamw_small",
        after=after,
        in_specs=[pl.BlockSpec(memory_space=pltpu.VMEM)] * (12 + 3 * n_w),
        out_shape=[jax.ShapeDtypeStruct((1, 1), F32)] + shapes * 4,
        compiler_params=_params(),
    )(*own, *recv, *weights, *moms, *vels)
    return outs[0], outs[1:1 + n_w], outs[1 + n_w:1 + 2 * n_w], outs[1 + 2 * n_w:1 + 3 * n_w], outs[1 + 3 * n_w:]


def _place():
    x, y, c = lax.axis_index("x"), lax.axis_index("y"), lax.axis_index("c")
    return x, y, c, [(1 - x, y), (x, 1 - y), (1 - x, 1 - y)]


def _dev_index(px, py, pc):
    return 4 * px + 2 * py + pc


HBM_SPEC = pl.BlockSpec(memory_space=pltpu.HBM)
SEM_SPEC = pl.BlockSpec(memory_space=pltpu.SEMAPHORE)
ANY_SPEC = pl.BlockSpec(memory_space=pl.ANY)
DATAFLOW = pltpu.SideEffectType.DATAFLOW_SIDE_EFFECTING


def _hbm(a):
    return pltpu.with_memory_space_constraint(a, pltpu.HBM)


def _prep_weights(shards):
    k_n = len(shards)

    def body(*refs):
        ins, outs, stage, sems = refs[:k_n], refs[k_n:2 * k_n], refs[2 * k_n:3 * k_n], refs[3 * k_n]
        x, y, c, _ = _place()
        copies = []
        for k in range(k_n):
            stage[k][...] = ins[k][...].astype(BF)
            copies.append(pltpu.make_async_copy(stage[k], outs[k].at[_dev_index(x, y, c)], sems.at[k]))
            copies[k].start()
        for cp in copies:
            cp.wait()

    return _call(
        body,
        name="prep_weights",
        in_specs=[pl.BlockSpec(memory_space=pltpu.VMEM)] * k_n,
        out_specs=[ANY_SPEC] * k_n,
        out_shape=[jax.ShapeDtypeStruct((N_DEV,) + sh.shape, BF) for sh in shards],
        scratch_shapes=[pltpu.VMEM(sh.shape, BF) for sh in shards] + [pltpu.SemaphoreType.DMA((k_n,))],
        compiler_params=_params(),
    )(*shards)


def _ag_piece(land_k, block, half, peer, send_sem, recv_sem):
    ref = land_k.at[_dev_index(*block)]
    if half is not None:
        rows = land_k.shape[1] // 2
        ref = ref.at[pl.ds(half * rows, rows)]
    return pltpu.make_async_remote_copy(
        src_ref=ref, dst_ref=ref, send_sem=send_sem, recv_sem=recv_sem, device_id=peer, device_id_type=MESH)


def _ag_plan():
    x, y, c, _ = _place()
    me, sib = (x, y, c), (x, y, 1 - c)
    xn, yn, diag = (1 - x, y, c), (x, 1 - y, c), (1 - x, 1 - y, c)
    return dict(
        relay_halves=[(me, 0, xn), (me, 1, yn)],
        others=[(me, None, sib), (me, 1, xn), (me, 0, yn)],
        relays=[(xn, 0, yn), (yn, 1, xn)],
        near=[(xn, None, sib), (yn, None, sib)],
        far=[(diag, None, sib)],
    )


def _ag_stage(land, stage, send_sems, recv_sems, act):
    copies = _ag_plan()[stage]
    n = len(copies)
    for k in range(len(land)):
        for j, (block, half, peer) in enumerate(copies):
            cp = _ag_piece(land[k], block, half, peer, send_sems.at[n * k + j], recv_sems.at[n * k + j])
            if act == "start":
                cp.start()
            else:
                cp.wait_send()
                cp.wait_recv()


def _sem_shapes(*counts):
    return [pltpu.SemaphoreType.DMA((n,)) for n in counts for _ in range(2)]


def _ag_start(first, rest):
    lands = list(first) + list(rest)
    k_n, k_first = len(lands), len(first)
    k_rest = k_n - k_first

    def body(*refs):
        land = refs[:k_n]
        sems = refs[k_n:k_n + 6]
        token = refs[-1]
        x, y, c, chips = _place()
        targets = [(x, y, 1 - c)] + [(*chip, c) for chip in chips]
        for k in range(k_first):
            for j, to in enumerate(targets):
                _ag_piece(land[k], (x, y, c), None, to, sems[0].at[4 * k + j], sems[1].at[4 * k + j]).start()
        _ag_stage(land[k_first:], "relay_halves", sems[2], sems[3], "start")
        _ag_stage(land[k_first:], "others", sems[4], sems[5], "start")
        token[...] = jnp.zeros_like(token)

    outs = pl.pallas_call(
        body,
        name="ag_start",
        in_specs=[HBM_SPEC] * k_n,
        out_specs=(*[SEM_SPEC] * 6, *[HBM_SPEC] * k_n, pl.BlockSpec(memory_space=pltpu.VMEM)),
        out_shape=(*_sem_shapes(4 * k_first, 2 * k_rest, 3 * k_rest),
                   *[pltpu.HBM(a.shape, a.dtype) for a in lands], jax.ShapeDtypeStruct((8, LANES), F32)),
        input_output_aliases={k: 6 + k for k in range(k_n)},
        compiler_params=pltpu.CompilerParams(has_side_effects=DATAFLOW),
    )(*[_hbm(a) for a in lands])
    flying = list(outs[6:6 + k_n])
    return (outs[0], outs[1], flying[:k_first]), (outs[2:6], flying[k_first:]), outs[-1]


def _ag_split_call(lands, waits, starts, after, name):
    k_n = len(lands)
    plan_sizes = dict(relay_halves=2, others=3, relays=2, near=2, far=1)
    n_in, n_out = 2 * len(waits), 2 * len(starts)

    def body(*refs):
        land = refs[:k_n]
        in_sems = refs[k_n:k_n + n_in]
        out_sems, token = refs[len(refs) - 1 - n_out:len(refs) - 1], refs[-1]
        for w, (stage, _, _) in enumerate(waits):
            _ag_stage(land, stage, in_sems[2 * w], in_sems[2 * w + 1], "wait")
            if w < len(starts):
                _ag_stage(land, starts[w], out_sems[2 * w], out_sems[2 * w + 1], "start")
        token[...] = jnp.zeros_like(token)

    outs = pl.pallas_call(
        body,
        name=name,
        in_specs=[HBM_SPEC] * k_n + [SEM_SPEC] * n_in + [ANY_SPEC],
        out_specs=(*[HBM_SPEC] * k_n, *[SEM_SPEC] * n_out, pl.BlockSpec(memory_space=pltpu.VMEM)),
        out_shape=(*[pltpu.HBM(a.shape, a.dtype) for a in lands], *_sem_shapes(*[plan_sizes[s] * k_n for s in starts]),
                   jax.ShapeDtypeStruct((8, LANES), F32)),
        input_output_aliases={k: k for k in range(k_n)},
        compiler_params=pltpu.CompilerParams(has_side_effects=DATAFLOW),
    )(*lands, *[s for _, a, b in waits for s in (a, b)], after)
    return list(outs[:k_n]), list(outs[k_n:k_n + n_out]), outs[-1]


def _ag_mid(lands, send_sems, recv_sems, after, name):
    k_n = len(lands)

    def body(*refs):
        land = refs[:k_n]
        send1, recv1 = refs[k_n], refs[k_n + 1]
        fwd_send, fwd_recv = refs[-2], refs[-1]
        x, y, c, chips = _place()
        sources = [(x, y, 1 - c)] + [(*chip, c) for chip in chips]
        for k in range(k_n):
            mine = land[k].at[_dev_index(x, y, c)]
            for j, frm in enumerate(sources):
                got = land[k].at[_dev_index(*frm)]
                cp = pltpu.make_async_remote_copy(
                    src_ref=mine, dst_ref=got, send_sem=send1.at[4 * k + j], recv_sem=recv1.at[4 * k + j], device_id=frm, device_id_type=MESH)
                cp.wait_send()
                cp.wait_recv()
                if j >= 1:
                    pltpu.make_async_remote_copy(
                        src_ref=got, dst_ref=got, send_sem=fwd_send.at[3 * k + j - 1], recv_sem=fwd_recv.at[3 * k + j - 1],
                        device_id=(x, y, 1 - c), device_id_type=MESH).start()

    outs = pl.pallas_call(
        body,
        name=name,
        in_specs=[HBM_SPEC] * k_n + [SEM_SPEC, SEM_SPEC, ANY_SPEC],
        out_specs=(*[HBM_SPEC] * k_n, SEM_SPEC, SEM_SPEC),
        out_shape=(*[pltpu.HBM(a.shape, a.dtype) for a in lands], pltpu.SemaphoreType.DMA((3 * k_n,)), pltpu.SemaphoreType.DMA((3 * k_n,))),
        input_output_aliases={k: k for k in range(k_n)},
        compiler_params=pltpu.CompilerParams(has_side_effects=DATAFLOW),
    )(*lands, send_sems, recv_sems, after)
    return list(outs[:k_n]), outs[-2], outs[-1]


def _ag_end(lands, fwd_send, fwd_recv, name):
    k_n = len(lands)

    def body(*refs):
        land = refs[:k_n]
        fsend, frecv = refs[k_n], refs[k_n + 1]
        x, y, c, chips = _place()
        for k in range(k_n):
            for j, chip in enumerate(chips):
                cp = pltpu.make_async_remote_copy(
                    src_ref=land[k].at[_dev_index(*chip, c)], dst_ref=land[k].at[_dev_index(*chip, 1 - c)],
                    send_sem=fsend.at[3 * k + j], recv_sem=frecv.at[3 * k + j], device_id=(x, y, 1 - c), device_id_type=MESH)
                cp.wait_send()
                cp.wait_recv()

    outs = pl.pallas_call(
        body,
        name=name,
        in_specs=[HBM_SPEC] * k_n + [SEM_SPEC, SEM_SPEC],
        out_specs=tuple([HBM_SPEC] * k_n),
        out_shape=tuple(pltpu.HBM(a.shape, a.dtype) for a in lands),
        input_output_aliases={k: k for k in range(k_n)},
        compiler_params=pltpu.CompilerParams(has_side_effects=DATAFLOW),
    )(*lands, fwd_send, fwd_recv)
    return list(outs)


def _chips4():
    x, y, c, others = _place()
    return x, y, c, [(x, y)] + others


def _route_sibling(j):
    x, y, c, chips = _chips4()
    return _dev_index(*chips[j], 1 - c), j, (x, y, 1 - c)


def _route_chips(j):
    x, y, c, chips = _chips4()
    return j, j, (*chips[j + 1], c)


def _route_sibling_whole(j):
    x, y, c, _ = _chips4()
    return 0, 0, (x, y, 1 - c)


def _route_chips_whole(j):
    x, y, c, chips = _chips4()
    return 0, j, (*chips[j + 1], c)


def _xchg_copies(routes, src, dst, send_sems, recv_sems):
    copies, sem = [], 0
    for k, (route, n) in enumerate(routes):
        for j in range(n):
            si, di, peer = route(j)
            copies.append(pltpu.make_async_remote_copy(
                src_ref=src[k].at[si], dst_ref=dst[k].at[di], send_sem=send_sems.at[sem], recv_sem=recv_sems.at[sem],
                device_id=peer, device_id_type=MESH))
            sem += 1
    return copies


def _xchg_start(srcs, slot_shapes, routes, name):
    k_n = len(srcs)
    n_sem = sum(n for _, n in routes)
    dsts = [lax.empty((n,) + tuple(sh), a.dtype) for sh, a, (_, n) in zip(slot_shapes, srcs, routes)]

    def body(*refs):
        src, dst = refs[:k_n], refs[k_n:2 * k_n]
        send_sems, recv_sems, token = refs[2 * k_n], refs[2 * k_n + 1], refs[-1]
        for cp in _xchg_copies(routes, src, dst, send_sems, recv_sems):
            cp.start()
        token[...] = jnp.zeros_like(token)

    arrays = list(srcs) + dsts
    outs = pl.pallas_call(
        body,
        name=name,
        in_specs=[HBM_SPEC] * (2 * k_n),
        out_specs=(SEM_SPEC, SEM_SPEC, *[HBM_SPEC] * (2 * k_n), pl.BlockSpec(memory_space=pltpu.VMEM)),
        out_shape=(pltpu.SemaphoreType.DMA((n_sem,)), pltpu.SemaphoreType.DMA((n_sem,)),
                   *[pltpu.HBM(a.shape, a.dtype) for a in arrays], jax.ShapeDtypeStruct((8, LANES), F32)),
        input_output_aliases={i: 2 + i for i in range(2 * k_n)},
        compiler_params=pltpu.CompilerParams(has_side_effects=DATAFLOW),
    )(*[_hbm(a) for a in arrays])
    return outs[0], outs[1], list(outs[2:2 + k_n]), list(outs[2 + k_n:2 + 2 * k_n]), outs[-1]


def _xchg_wait(send_sems, recv_sems, srcs, dsts, routes, after, name):
    k_n = len(srcs)

    def body(*refs):
        src, dst = refs[:k_n], refs[k_n:2 * k_n]
        for cp in _xchg_copies(routes, src, dst, refs[2 * k_n], refs[2 * k_n + 1]):
            cp.wait_send()
            cp.wait_recv()

    arrays = list(srcs) + list(dsts)
    outs = pl.pallas_call(
        body,
        name=name,
        in_specs=[HBM_SPEC] * (2 * k_n) + [SEM_SPEC, SEM_SPEC, ANY_SPEC],
        out_specs=tuple([HBM_SPEC] * (2 * k_n)),
        out_shape=tuple(pltpu.HBM(a.shape, a.dtype) for a in arrays),
        input_output_aliases={i: i for i in range(2 * k_n)},
        compiler_params=pltpu.CompilerParams(has_side_effects=DATAFLOW),
    )(*arrays, send_sems, recv_sems, after)
    return list(outs[:k_n]), list(outs[k_n:])


SMALL = ("norm1_gain", "gmlp_v_gain", "w_spatial", "b_spatial", "attn_sinks", "rel_bias_table", "norm2_gain", "final_gain")
LANES = 128


def _swap_start(grads, smalls, tag):
    srcs = list(grads) + [a[None] for a in smalls]
    shapes = [g.shape[1:] for g in grads] + [a.shape for a in smalls]
    routes = [(_route_sibling, 4)] * len(grads) + [(_route_sibling_whole, 1)] * len(smalls)
    return _xchg_start(srcs, shapes, routes, f"rs_{tag}_swap_start"), routes, len(grads)


def _swap_sums(swap, names, after, tag):
    (send1, recv1, src1, land1, _), routes, n_rs = swap
    x, y, c, chips = _chips4()
    blocks = jnp.stack([_dev_index(*chip, c) for chip in chips]).astype(jnp.int32)
    src1, land1 = _xchg_wait(send1, recv1, src1, land1, routes, after, f"rs_{tag}_swap_wait")
    sums = [_rs_sum(g, land, blocks, f"rs_sum_{n}") for g, land, n in zip(src1[:n_rs], land1[:n_rs], names)]
    return sums, (_pair_sum(src1[n_rs:], land1[n_rs:], f"rs_{tag}_sum_small") if len(src1) > n_rs else [])


def _pair_sum(mine, theirs, name):
    def body(*refs):
        n = len(refs) // 3
        for k in range(n):
            refs[2 * n + k][...] = refs[k][...] + refs[n + k][...]

    return _call(
        body,
        name=name,
        out_shape=[jax.ShapeDtypeStruct(a.shape, F32) for a in mine],
        compiler_params=_params(),
    )(*mine, *theirs)


def _chips_start(sums, small_sums, tag):
    sends = [ps[1] for ps in sums] + list(small_sums)
    routes = [(_route_chips, 3)] * len(sums) + [(_route_chips_whole, 3)] * len(small_sums)
    return _xchg_start(sends, [a.shape[1:] for a in sends], routes, f"rs_{tag}_chips_start"), routes


def _chips_wait(chips, after, tag):
    (send2, recv2, src2, land2, _), routes = chips
    return _xchg_wait(send2, recv2, src2, land2, routes, after, f"rs_{tag}_chips_wait")


def kernel(x, p, norm1_gain, w_in, gmlp_v_gain, w_spatial, b_spatial, attn_sinks, rel_bias_table, w_out, norm2_gain, w_ff1, w_ff2, w_ple_proj, w_ple_gate, final_gain, loss_target, m_norm1_gain, m_w_in, m_gmlp_v_gain, m_w_spatial, m_b_spatial, m_attn_sinks, m_rel_bias_table, m_w_out, m_norm2_gain, m_w_ff1, m_w_ff2, m_w_ple_proj, m_w_ple_gate, m_final_gain, v_norm1_gain, v_w_in, v_gmlp_v_gain, v_w_spatial, v_b_spatial, v_attn_sinks, v_rel_bias_table, v_w_out, v_norm2_gain, v_w_ff1, v_w_ff2, v_w_ple_proj, v_w_ple_gate, v_final_gain):
    args = dict(locals())
    s = x.shape[1]
    big = ("w_in", "w_out", "w_ff1", "w_ff2", "w_ple_proj", "w_ple_gate")

    x2, p2, t2 = x.reshape(s, D_MODEL), p.reshape(s, PLE_DIM), loss_target.reshape(s, D_MODEL)
    g1, gv, w_sp, b_sp, sinks, table, g2, gf = (args[n] for n in SMALL)
    bucket = jnp.asarray(_bucket_table())
    b_col = b_sp.reshape(GROUPS, BLOCK, 1)

    def shard(name):
        return args[name][0].T if name.endswith("w_in") else args[name][0]

    lands = _prep_weights([shard(n) for n in big])
    (send_in, recv_in, fly_in), (rest_sems, fly_rest), token = _ag_start(lands[:1], lands[1:])
    mid_in, fwd_send_in, fwd_recv_in = _ag_mid(fly_in, send_in, recv_in, token, "ag_mid_w_in")
    g_in = _ag_end(mid_in, fwd_send_in, fwd_recv_in, "ag_end_w_in")[0]
    full_in = g_in.reshape(D_IN, D_MODEL)

    zuv, qkv, hn1 = _in_proj(x2, g1, full_in)
    fly_rest, relay_sems, relayed = _ag_split_call(
        fly_rest, [("relay_halves", *rest_sems[:2])], ["relays"], zuv, "ag_relay")
    mix, *saved = _mixer_fwd(zuv, qkv, gv, w_sp[0], b_col, sinks, table.T, bucket, after=relayed)
    fly_rest, fwd_sems, _ = _ag_split_call(
        fly_rest, [("others", *rest_sems[2:]), ("relays", *relay_sems)], ["near", "far"], mix, "ag_mid_rest")
    g_out, g_ff1, g_ff2, g_proj, g_gate = _ag_split_call(
        fly_rest, [("near", *fwd_sems[:2]), ("far", *fwd_sems[2:])], [], mix, "ag_end_rest")[0]
    full_out, full_ff2, full_gate = g_out.reshape(D_MODEL, D_MODEL), g_ff2.reshape(D_FF, D_MODEL), g_gate.reshape(D_MODEL, D_MODEL)
    full_proj = g_proj.transpose(1, 0, 2).reshape(PLE_DIM, D_MODEL)

    (tail_small, dh1, dh1b, dmix, hn2, a, df, h2, dh2, dgl, dpp) = _tail(
        x2, mix, p2, t2, g2, gf.reshape(1, D_MODEL), full_out, g_ff1, full_ff2, full_gate, full_proj)

    ffn_names, mid_names = ("w_ff1", "w_ff2"), ("w_out", "w_ple_proj", "w_ple_gate")
    gw_ff1 = _wgrad(hn2, df, D_MODEL, D_FF // N_DEV, "wgrad_ff1", peer_cols=D_FF // N_DEV)
    gw_ff2 = _wgrad(a, dh2, 1024, D_MODEL, "wgrad_ff2").reshape(N_DEV, D_FF // N_DEV, D_MODEL)
    ffn_swap = _swap_start([gw_ff1, gw_ff2], [], "ffn")
    gw_gate = _wgrad(h2, dgl, D_MODEL, 512, "wgrad_gate", after=ffn_swap[0][4]).reshape(N_DEV, D_MODEL // N_DEV, D_MODEL)
    gw_proj = _wgrad(p2, dpp, PLE_DIM, D_MODEL // 2, "wgrad_proj", peer_cols=D_MODEL // N_DEV, after=gw_gate)
    gw_out = _wgrad(mix, dh1b, D_MODEL, 512, "wgrad_out", after=gw_proj).reshape(N_DEV, D_MODEL // N_DEV, D_MODEL)
    mid_swap = _swap_start([gw_out, gw_proj, gw_gate], [tail_small], "mid")
    ffn_sums, _ = _swap_sums(ffn_swap, ffn_names, mid_swap[0][4], "ffn")
    ffn_chips = _chips_start(ffn_sums, [], "ffn")

    dzm, dkv, d_gv, d_wsp, d_bsp, d_attn = _mixer_bwd(
        zuv, qkv, dmix, saved, gv, w_sp[0], b_col, bucket, after=ffn_chips[0][4])
    mid_sums, mid_small = _swap_sums(mid_swap, mid_names, dzm, "mid")
    mid_chips = _chips_start(mid_sums, mid_small, "mid")
    gw_in = _wgrad_in(dzm, dkv, hn1, after=mid_chips[0][4]).reshape(N_DEV, D_IN // N_DEV, D_MODEL)
    dx, d_g1 = _in_bwd(x2, dh1, dzm, dkv, g1, full_in, after=gw_in)
    grad_x = dx.reshape(x.shape)
    in_swap = _swap_start([gw_in], [d_g1, d_gv, d_wsp, d_bsp, d_attn], "in")
    in_sums, in_small = _swap_sums(in_swap, ("w_in",), in_swap[0][4], "in")
    in_chips = _chips_start(in_sums, in_small, "in")

    grads, deltas, new_m, new_v = {}, {}, {}, {}
    after, small_own, small_recv = in_chips[0][4], [], []
    for chips, sums, names, tag in ((ffn_chips, ffn_sums, ffn_names, "ffn"), (mid_chips, mid_sums, mid_names, "mid"),
                                    (in_chips, in_sums, ("w_in",), "in")):
        sent, recvs = _chips_wait(chips, after, tag)
        small_own += sent[len(names):]
        small_recv += recvs[len(names):]
        for n, (part, _), recv in zip(names, sums, recvs):
            g, d, mo, vo = _final_adamw(part, recv, shard(n), shard("m_" + n), shard("v_" + n), "adamw_" + n)
            for dst, arr in zip((grads, deltas, new_m, new_v), (g, d, mo, vo)):
                dst[n] = (arr.T if n == "w_in" else arr)[None]
            after = d

    views = {"w_spatial": (GROUPS, BLOCK, BLOCK), "b_spatial": (GROUPS, BLOCK), "final_gain": (1, D_MODEL)}
    def view(name, base):
        return args[name].T if base == "rel_bias_table" else args[name].reshape(views.get(base, args[name].shape))

    small_in = [[view(pre + n, n) for n in SMALL] for pre in ("", "m_", "v_")]
    loss, *small_out = _adamw_small(small_own, small_recv, *small_in, after=after)
    for dst, arrays in zip((grads, deltas, new_m, new_v), small_out):
        for n, arr in zip(SMALL, arrays):
            dst[n] = arr.T if n == "rel_bias_table" else arr.reshape(args[n].shape)
    loss = loss[0, 0]

    order = ("norm1_gain", "w_in", "gmlp_v_gain", "w_spatial", "b_spatial", "attn_sinks", "rel_bias_table", "w_out",
             "norm2_gain", "w_ff1", "w_ff2", "w_ple_proj", "w_ple_gate", "final_gain")
    return (loss, grad_x, *[grads[n] for n in order], *[deltas[n] for n in order],
            *[new_m[n] for n in order], *[new_v[n] for n in order])
```

```python
import functools
import math

import numpy as np
import jax
import jax.numpy as jnp
from jax import lax
from jax.experimental import pallas as pl
from jax.experimental.pallas import tpu as pltpu

F32 = jnp.float32
BF = jnp.bfloat16
MESH = pl.DeviceIdType.MESH
N_DEV = 8

D_MODEL = 1024
PLE_DIM = 256
D_GMLP = 512
GROUPS = 4
GDIM = 128
BLOCK = 128
D_ATTN = 512
HEAD_DIM = 64
N_Q = 8
Q_PER_KV = 4
N_KV = N_Q // Q_PER_KV
ROWS4 = Q_PER_KV * BLOCK
D_KV = 128
D_FF = 4096
D_IN = 1792
D_MAIN = 2 * D_GMLP + D_ATTN
REL_BUCKETS = 32
EPS = 1e-6
NEG_INF = -1e30
SCALE = HEAD_DIM ** -0.5
GELU_C = math.sqrt(2.0 / math.pi)
GELU_A = 0.044715

ADAM_LR = 0.001
ADAM_B1 = 0.9
ADAM_B2 = 0.999
ADAM_EPS = 1e-08
ADAM_WD = 0.01
ADAM_STEP = 10

V7X_VMEM_LIMIT = 60000 * 1024
TOK_TILE = 256
IO_TOK_TILE = 512


def _call(body, after=None, **kw):
    if after is None:
        return pl.pallas_call(body, **kw)
    n_in = len(kw["in_specs"])

    def ordered(*refs):
        body(*refs[:n_in], *refs[n_in + 1:])

    kw["in_specs"] = list(kw["in_specs"]) + [pl.BlockSpec(memory_space=pl.ANY)]
    fn = pl.pallas_call(ordered, **kw)
    return lambda *operands: fn(*operands, after)


def _params(sem=None):
    if sem is None:
        return pltpu.CompilerParams(vmem_limit_bytes=V7X_VMEM_LIMIT)
    return pltpu.CompilerParams(dimension_semantics=sem, vmem_limit_bytes=V7X_VMEM_LIMIT)


def _nn(a, b):
    return jnp.dot(a, b, preferred_element_type=F32)


def _nt(a, b):
    return lax.dot_general(a, b, (((1,), (1,)), ((), ())), preferred_element_type=F32)


def _tn(a, b):
    return lax.dot_general(a, b, (((0,), (0,)), ((), ())), preferred_element_type=F32)


def _gelu_tanh(x):
    return jnp.tanh(GELU_C * (x + GELU_A * (x * x * x)))


def _gelu(x, t):
    return x * (0.5 * (1.0 + t))


def _gelu_and_grad(x, t):
    cdf = 0.5 * (1.0 + t)
    return x * cdf, cdf + 0.5 * x * (1.0 - t * t) * (GELU_C * (1.0 + 3.0 * GELU_A * (x * x)))


def _rms_scale(x):
    return lax.rsqrt(jnp.mean(x * x, axis=-1, keepdims=True) + EPS)


def _rms_bwd(dxn, x, r):
    return r * dxn - x * ((r * r * r) * jnp.mean(dxn * x, axis=-1, keepdims=True))


def _bucket_table():
    a = np.arange(BLOCK)[:, None]
    j = np.arange(2 * BLOCK)[None, :]
    n = BLOCK + a - j
    valid = (n >= 0) & (n < BLOCK)
    nc = np.maximum(n, 0)
    max_exact = REL_BUCKETS // 2
    nf = np.maximum(nc, 1).astype(np.float32)
    large = max_exact + (
        np.log(nf / np.float32(max_exact)) / np.float32(math.log(BLOCK / max_exact)) * np.float32(REL_BUCKETS - max_exact)
    ).astype(np.int32)
    large = np.minimum(large, REL_BUCKETS - 1)
    bucket = np.where(nc < max_exact, nc, large)
    return np.where(valid, bucket, -1).astype(np.int32)


def _in_proj(x, g1, w_in_t):
    s = x.shape[0]
    tm = min(IO_TOK_TILE, s)

    def body(x_ref, g_ref, w_ref, zuv_ref, qkv_ref, hn_ref):
        xv = x_ref[...]
        hn = ((xv * _rms_scale(xv)) * g_ref[...]).astype(BF)
        hn_ref[...] = hn
        z = _nt(hn, w_ref[...])
        zuv_ref[...] = z[:, : 2 * D_GMLP]
        qkv_ref[...] = z[:, 2 * D_GMLP:].astype(BF)

    return _call(
        body,
        name="in_proj",
        grid=(s // tm,),
        in_specs=[
            pl.BlockSpec((tm, D_MODEL), lambda i: (i, 0)),
            pl.BlockSpec((1, D_MODEL), lambda i: (0, 0)),
            pl.BlockSpec((D_IN, D_MODEL), lambda i: (0, 0)),
        ],
        out_specs=[
            pl.BlockSpec((tm, 2 * D_GMLP), lambda i: (i, 0)),
            pl.BlockSpec((tm, D_ATTN + 2 * D_KV), lambda i: (i, 0)),
            pl.BlockSpec((tm, D_MODEL), lambda i: (i, 0)),
        ],
        out_shape=[
            jax.ShapeDtypeStruct((s, 2 * D_GMLP), F32),
            jax.ShapeDtypeStruct((s, D_ATTN + 2 * D_KV), BF),
            jax.ShapeDtypeStruct((s, D_MODEL), BF),
        ],
        compiler_params=_params(("arbitrary",)),
    )(x, g1, w_in_t)


def _head_rows(h):
    kh, g = divmod(h, Q_PER_KV)
    return kh, slice(g * BLOCK, (g + 1) * BLOCK)


def _build_bias(bias_ref, bucket_ref, table_ref):
    bucket = bucket_ref[...]
    for h in range(N_Q):
        acc = jnp.zeros((BLOCK, 2 * BLOCK), F32)
        for b in range(REL_BUCKETS):
            acc = jnp.where(bucket == b, table_ref[h, b], acc)
        kh, rows = _head_rows(h)
        bias_ref[kh, rows, :] = acc


def _window_masks(i):
    row = lax.broadcasted_iota(jnp.int32, (ROWS4, BLOCK), 0) & (BLOCK - 1)
    col = lax.broadcasted_iota(jnp.int32, (ROWS4, BLOCK), 1)
    return (col > row) & (i > 0), col <= row


def _stack_heads(ref, kh, offset):
    first = offset + kh * Q_PER_KV * HEAD_DIM
    return jnp.concatenate(
        [ref[:, first + g * HEAD_DIM: first + (g + 1) * HEAD_DIM].astype(BF) for g in range(Q_PER_KV)], axis=0)


def _stack_sinks(sink_ref, kh):
    return jnp.concatenate([jnp.full((BLOCK, 1), sink_ref[0, kh * Q_PER_KV + g], F32) for g in range(Q_PER_KV)], axis=0)


def _tril_bf16(w_ref, g):
    row = lax.broadcasted_iota(jnp.int32, (BLOCK, BLOCK), 0)
    col = lax.broadcasted_iota(jnp.int32, (BLOCK, BLOCK), 1)
    return jnp.where(col <= row, w_ref[g], 0.0).astype(BF)


def _attn_probs(q_h, k_prev, k_cur, bias_h, sink, valid_prev, valid_cur):
    l_prev = jnp.where(valid_prev, _nt(q_h, k_prev) * SCALE + bias_h[:, :BLOCK], NEG_INF)
    l_cur = jnp.where(valid_cur, _nt(q_h, k_cur) * SCALE + bias_h[:, BLOCK:], NEG_INF)
    m = jnp.maximum(jnp.maximum(jnp.max(l_prev, axis=-1, keepdims=True), jnp.max(l_cur, axis=-1, keepdims=True)), sink)
    e_prev = jnp.exp(l_prev - m)
    e_cur = jnp.exp(l_cur - m)
    e_sink = jnp.exp(sink - m)
    denom = jnp.sum(e_prev, axis=-1, keepdims=True) + jnp.sum(e_cur, axis=-1, keepdims=True) + e_sink
    return e_prev / denom, e_cur / denom, e_sink / denom


def _mixer_specs(nb):
    cl = lambda i: jnp.minimum(i, nb - 1)
    return [
        pl.BlockSpec((BLOCK, 2 * D_GMLP), lambda i: (cl(i), 0)),
        pl.BlockSpec((BLOCK, D_ATTN), lambda i: (cl(i), 0)),
        pl.BlockSpec((BLOCK, 2 * D_KV), lambda i: (cl(i), D_ATTN // (2 * D_KV))),
        pl.BlockSpec((BLOCK, 2 * D_KV), lambda i: (jnp.maximum(cl(i) - 1, 0), D_ATTN // (2 * D_KV))),
        pl.BlockSpec((1, D_GMLP), lambda i: (0, 0)),
        pl.BlockSpec((GROUPS, BLOCK, BLOCK), lambda i: (0, 0, 0)),
        pl.BlockSpec((GROUPS, BLOCK, 1), lambda i: (0, 0, 0)),
        pl.BlockSpec(memory_space=pltpu.SMEM),
        pl.BlockSpec(memory_space=pltpu.SMEM),
        pl.BlockSpec((BLOCK, 2 * BLOCK), lambda i: (0, 0)),
    ]


def _mixer_fwd(zuv, qkv, gv, w_sp, b_sp, sinks, table, bucket, after=None):
    s = zuv.shape[0]
    nb = s // BLOCK

    def body(zuv_ref, q_ref, kvc_ref, kvp_ref, gv_ref, w_ref, b_ref, sink_ref, table_ref, bucket_ref,
             mix_ref, tanh_ref, prob_ref, psink_ref, bias_ref):
        i = pl.program_id(0)

        @pl.when(i == 0)
        def _():
            _build_bias(bias_ref, bucket_ref, table_ref)

        t = _gelu_tanh(zuv_ref[...])
        tanh_ref[...] = t
        u = _gelu(zuv_ref[:, :D_GMLP], t[:, :D_GMLP])
        vg = _gelu(zuv_ref[:, D_GMLP:], t[:, D_GMLP:])
        for g in range(GROUPS):
            sl = slice(g * GDIM, (g + 1) * GDIM)
            vg_g = vg[:, sl]
            vn = ((vg_g * _rms_scale(vg_g)) * gv_ref[:, sl]).astype(BF)
            sv = _nn(_tril_bf16(w_ref, g), vn) + b_ref[g]
            mix_ref[:, sl] = (u[:, sl] * sv).astype(BF)

        valid_prev, valid_cur = _window_masks(i)
        for kh in range(N_KV):
            ksl = slice(kh * HEAD_DIM, (kh + 1) * HEAD_DIM)
            vsl = slice(D_KV + kh * HEAD_DIM, D_KV + (kh + 1) * HEAD_DIM)
            q4 = _stack_heads(q_ref, kh, 0)
            p_prev, p_cur, p_sink = _attn_probs(
                q4, kvp_ref[:, ksl], kvc_ref[:, ksl], bias_ref[kh], _stack_sinks(sink_ref, kh), valid_prev, valid_cur)
            prob_ref[0, kh, :, :BLOCK] = p_prev
            prob_ref[0, kh, :, BLOCK:] = p_cur
            psink_ref[0, kh] = jnp.broadcast_to(p_sink, (ROWS4, LANES))
            o4 = _nn(p_prev.astype(BF), kvp_ref[:, vsl]) + _nn(p_cur.astype(BF), kvc_ref[:, vsl])
            for g in range(Q_PER_KV):
                first = D_GMLP + (kh * Q_PER_KV + g) * HEAD_DIM
                mix_ref[:, first:first + HEAD_DIM] = o4[g * BLOCK:(g + 1) * BLOCK].astype(BF)

    return _call(
        body,
        name="mixer_fwd",
        after=after,
        grid=(nb,),
        in_specs=_mixer_specs(nb),
        out_specs=[
            pl.BlockSpec((BLOCK, D_MODEL), lambda i: (i, 0)),
            pl.BlockSpec((BLOCK, 2 * D_GMLP), lambda i: (i, 0)),
            pl.BlockSpec((1, N_KV, ROWS4, 2 * BLOCK), lambda i: (i, 0, 0, 0)),
            pl.BlockSpec((1, N_KV, ROWS4, LANES), lambda i: (i, 0, 0, 0)),
        ],
        out_shape=[
            jax.ShapeDtypeStruct((s, D_MODEL), BF),
            jax.ShapeDtypeStruct((s, 2 * D_GMLP), F32),
            jax.ShapeDtypeStruct((nb, N_KV, ROWS4, 2 * BLOCK), F32),
            jax.ShapeDtypeStruct((nb, N_KV, ROWS4, LANES), F32),
        ],
        scratch_shapes=[pltpu.VMEM((N_KV, ROWS4, 2 * BLOCK), F32)],
        compiler_params=_params(("arbitrary",)),
    )(zuv, qkv, qkv, qkv, gv, w_sp, b_sp, sinks, table, bucket)


def _tail(x, mix, p, t, g2, gf, w_out, w_ff1, w_ff2, w_gate, w_proj):
    s = x.shape[0]
    tm = min(TOK_TILE, s)
    n_ff = w_ff1.shape[0]
    fc = D_FF // n_ff

    def body(x_ref, mix_ref, p_ref, t_ref, g2_ref, gf_ref, wo_ref, w1_ref, w2_ref, wg_ref, wp_ref,
             small_ref, dh1_ref, dh1b_ref, dmix_ref, hn2_ref, a_ref, df_ref, h2_ref, dh2_ref, dgl_ref, dpp_ref, f_ref):
        i = pl.program_id(0)

        @pl.when(i == 0)
        def _():
            small_ref[...] = jnp.zeros_like(small_ref)

        h1 = x_ref[...] + _nn(mix_ref[...], wo_ref[...])
        r2 = _rms_scale(h1)
        hn2 = ((h1 * r2) * g2_ref[...]).astype(BF)
        hn2_ref[...] = hn2
        h2 = h1
        for c in range(n_ff):
            f = _nn(hn2, w1_ref[c])
            f_ref[:, c * fc:(c + 1) * fc] = f
            a = jnp.square(jnp.maximum(f, 0.0)).astype(BF)
            a_ref[:, c * fc:(c + 1) * fc] = a
            h2 = h2 + _nn(a, w2_ref[c * fc:(c + 1) * fc, :])
        h2b = h2.astype(BF)
        h2_ref[...] = h2b
        gate = jax.nn.sigmoid(_nn(h2b, wg_ref[...]))
        pp = _nn(p_ref[...].astype(BF), wp_ref[...])
        h3 = h2 + gate * pp
        rf = _rms_scale(h3)
        gf_v = gf_ref[...]
        err = (h3 * rf) * gf_v - t_ref[...]
        small_ref[2:3, :] += jnp.sum(jnp.sum(err * err, axis=-1, keepdims=True), axis=0, keepdims=True) * (0.5 / D_MODEL)

        dy = err * (1.0 / D_MODEL)
        small_ref[1:2, :] += jnp.sum(dy * (h3 * rf), axis=0, keepdims=True)
        dh3 = _rms_bwd(dy * gf_v, h3, rf)
        dpp_ref[...] = (dh3 * gate).astype(BF)
        dgl = ((dh3 * pp) * (gate * (1.0 - gate))).astype(BF)
        dgl_ref[...] = dgl
        dh2 = dh3 + _nt(dgl, wg_ref[...])
        dh2b = dh2.astype(BF)
        dh2_ref[...] = dh2b
        dhn2 = jnp.zeros((tm, D_MODEL), F32)
        for c in range(n_ff):
            da = _nt(dh2b, w2_ref[c * fc:(c + 1) * fc, :])
            df = (da * (2.0 * jnp.maximum(f_ref[:, c * fc:(c + 1) * fc], 0.0))).astype(BF)
            df_ref[:, c * fc:(c + 1) * fc] = df
            dhn2 = dhn2 + _nt(df, w1_ref[c])
        small_ref[0:1, :] += jnp.sum(dhn2 * (h1 * r2), axis=0, keepdims=True)
        dh1 = dh2 + _rms_bwd(dhn2 * g2_ref[...], h1, r2)
        dh1_ref[...] = dh1
        dh1b = dh1.astype(BF)
        dh1b_ref[...] = dh1b
        dmix_ref[...] = _nt(dh1b, wo_ref[...])

    tile = lambda cols: pl.BlockSpec((tm, cols), lambda i: (i, 0))
    whole = lambda shape: pl.BlockSpec(shape, lambda i: (0,) * len(shape), pipeline_mode=pl.Buffered(1))
    row = pl.BlockSpec((1, D_MODEL), lambda i: (0, 0))
    act = lambda cols, dt: jax.ShapeDtypeStruct((s, cols), dt)
    return _call(
        body,
        name="tail",
        grid=(s // tm,),
        in_specs=[tile(D_MODEL), tile(D_MODEL), tile(PLE_DIM), tile(D_MODEL), row, row,
                  whole(w_out.shape), whole(w_ff1.shape), whole(w_ff2.shape), whole(w_gate.shape), whole(w_proj.shape)],
        out_specs=[pl.BlockSpec((8, D_MODEL), lambda i: (0, 0)), tile(D_MODEL), tile(D_MODEL), tile(D_MODEL), tile(D_MODEL), tile(D_FF),
                   tile(D_FF), tile(D_MODEL), tile(D_MODEL), tile(D_MODEL), tile(D_MODEL)],
        out_shape=[jax.ShapeDtypeStruct((8, D_MODEL), F32),
                   act(D_MODEL, F32), act(D_MODEL, BF), act(D_MODEL, F32), act(D_MODEL, BF), act(D_FF, BF), act(D_FF, BF), act(D_MODEL, BF),
                   act(D_MODEL, BF), act(D_MODEL, BF), act(D_MODEL, BF)],
        scratch_shapes=[pltpu.VMEM((tm, D_FF), F32)],
        compiler_params=_params(("arbitrary",)),
    )(x, mix, p, t, g2, gf, w_out, w_ff1, w_ff2, w_gate, w_proj)


def _mixer_bwd(zuv, qkv, dmix, saved, gv, w_sp, b_sp, bucket, after=None):
    s = zuv.shape[0]
    nb = s // BLOCK

    def body(zuv_ref, q_ref, kvc_ref, kvp_ref, gv_ref, w_ref, b_ref, bucket_ref, dmix_ref, tanh_ref, prob_ref, psink_ref,
             dzm_ref, dkv_ref, dgv_ref, dw_ref, db_ref, dattn_ref,
             dbias_ref, carry_ref, dsink_acc, db_acc):
        i = pl.program_id(0)

        @pl.when(i == 0)
        def _():
            dbias_ref[...] = jnp.zeros_like(dbias_ref)
            carry_ref[...] = jnp.zeros_like(carry_ref)
            dsink_acc[...] = jnp.zeros_like(dsink_acc)
            dgv_ref[...] = jnp.zeros_like(dgv_ref)
            dw_ref[...] = jnp.zeros_like(dw_ref)
            db_acc[...] = jnp.zeros_like(db_acc)

        @pl.when(i < nb)
        def _():
            u, du_dz = _gelu_and_grad(zuv_ref[:, :D_GMLP], tanh_ref[:, :D_GMLP])
            vg, dvg_dz = _gelu_and_grad(zuv_ref[:, D_GMLP:], tanh_ref[:, D_GMLP:])
            for g in range(GROUPS):
                sl = slice(g * GDIM, (g + 1) * GDIM)
                vg_g = vg[:, sl]
                rg = _rms_scale(vg_g)
                vhat = vg_g * rg
                gain = gv_ref[:, sl]
                vn = (vhat * gain).astype(BF)
                w_g = _tril_bf16(w_ref, g)
                sv = _nn(w_g, vn) + b_ref[g]
                dmix_a = dmix_ref[:, sl]
                dsv = dmix_a * u[:, sl]
                dsvb = dsv.astype(BF)
                db_acc[g] += jnp.sum(dsv, axis=-1, keepdims=True)
                dw_ref[g] += _nt(dsvb, vn)
                dvn = _tn(w_g, dsvb)
                dgv_ref[:, sl] += jnp.sum(dvn * vhat, axis=0, keepdims=True)
                dvg = _rms_bwd(dvn * gain, vg_g, rg)
                dzm_ref[:, sl] = ((dmix_a * sv) * du_dz[:, sl]).astype(BF)
                dzm_ref[:, D_GMLP + g * GDIM: D_GMLP + (g + 1) * GDIM] = (dvg * dvg_dz[:, sl]).astype(BF)

            for kh in range(N_KV):
                ksl = slice(kh * HEAD_DIM, (kh + 1) * HEAD_DIM)
                vsl = slice(D_KV + kh * HEAD_DIM, D_KV + (kh + 1) * HEAD_DIM)
                k_prev, k_cur = kvp_ref[:, ksl], kvc_ref[:, ksl]
                v_prev, v_cur = kvp_ref[:, vsl], kvc_ref[:, vsl]
                q4 = _stack_heads(q_ref, kh, 0)
                p_prev, p_cur, p_sink = prob_ref[0, kh, :, :BLOCK], prob_ref[0, kh, :, BLOCK:], psink_ref[0, kh, :, 0:1]
                do4 = _stack_heads(dmix_ref, kh, D_GMLP)
                dp_prev = _nt(do4, v_prev)
                dp_cur = _nt(do4, v_cur)
                delta = jnp.sum(p_prev * dp_prev, axis=-1, keepdims=True) + jnp.sum(p_cur * dp_cur, axis=-1, keepdims=True)
                ds_prev = p_prev * (dp_prev - delta)
                ds_cur = p_cur * (dp_cur - delta)
                dsink_acc[kh] -= p_sink * delta
                dbias_ref[kh, :, :BLOCK] += ds_prev
                dbias_ref[kh, :, BLOCK:] += ds_cur
                dsb_prev = ds_prev.astype(BF)
                dsb_cur = ds_cur.astype(BF)
                dq4 = (_nn(dsb_prev, k_prev) + _nn(dsb_cur, k_cur)) * SCALE
                for g in range(Q_PER_KV):
                    first = 2 * D_GMLP + (kh * Q_PER_KV + g) * HEAD_DIM
                    dzm_ref[:, first:first + HEAD_DIM] = dq4[g * BLOCK:(g + 1) * BLOCK].astype(BF)
                dkv_ref[:, ksl] = (carry_ref[:, ksl] + _tn(dsb_prev, q4) * SCALE).astype(BF)
                dkv_ref[:, vsl] = (carry_ref[:, vsl] + _tn(p_prev.astype(BF), do4)).astype(BF)
                carry_ref[:, ksl] = _tn(dsb_cur, q4) * SCALE
                carry_ref[:, vsl] = _tn(p_cur.astype(BF), do4)

        @pl.when(i == nb)
        def _():
            dkv_ref[...] = carry_ref[...].astype(BF)
            row = lax.broadcasted_iota(jnp.int32, (BLOCK, BLOCK), 0)
            col = lax.broadcasted_iota(jnp.int32, (BLOCK, BLOCK), 1)
            for g in range(GROUPS):
                dw_ref[g] = jnp.where(col <= row, dw_ref[g], 0.0)
                db_ref[g:g + 1, :] = jnp.sum(jnp.where(col == row, db_acc[g], 0.0), axis=0, keepdims=True)
            bucket = bucket_ref[...]
            for b in range(N_Q, REL_BUCKETS):
                dattn_ref[N_Q, b] = 0.0
            for h in range(N_Q):
                kh, rows = _head_rows(h)
                dattn_ref[N_Q, h] = jnp.sum(dsink_acc[kh, rows, :])
                dbh = dbias_ref[kh, rows, :]
                for b in range(REL_BUCKETS):
                    dattn_ref[h, b] = jnp.sum(jnp.where(bucket == b, dbh, 0.0))

    cl = lambda i: jnp.minimum(i, nb - 1)
    const = lambda shape: pl.BlockSpec(shape, lambda i: (0,) * len(shape))
    return _call(
        body,
        name="mixer_bwd",
        after=after,
        grid=(nb + 1,),
        in_specs=_mixer_specs(nb)[:7] + [
            const((BLOCK, 2 * BLOCK)),
            pl.BlockSpec((BLOCK, D_MODEL), lambda i: (cl(i), 0)),
            pl.BlockSpec((BLOCK, 2 * D_GMLP), lambda i: (cl(i), 0)),
            pl.BlockSpec((1, N_KV, ROWS4, 2 * BLOCK), lambda i: (cl(i), 0, 0, 0)),
            pl.BlockSpec((1, N_KV, ROWS4, LANES), lambda i: (cl(i), 0, 0, 0)),
        ],
        out_specs=[
            pl.BlockSpec((BLOCK, D_MAIN), lambda i: (cl(i), 0)),
            pl.BlockSpec((BLOCK, 2 * D_KV), lambda i: (jnp.maximum(i - 1, 0), 0)),
            const((1, D_GMLP)),
            const((GROUPS, BLOCK, BLOCK)),
            const((GROUPS, BLOCK)),
            pl.BlockSpec(memory_space=pltpu.SMEM),
        ],
        out_shape=[
            jax.ShapeDtypeStruct((s, D_MAIN), BF),
            jax.ShapeDtypeStruct((s, 2 * D_KV), BF),
            jax.ShapeDtypeStruct((1, D_GMLP), F32),
            jax.ShapeDtypeStruct((GROUPS, BLOCK, BLOCK), F32),
            jax.ShapeDtypeStruct((GROUPS, BLOCK), F32),
            jax.ShapeDtypeStruct((N_Q + 1, REL_BUCKETS), F32),
        ],
        scratch_shapes=[
            pltpu.VMEM((N_KV, ROWS4, 2 * BLOCK), F32),
            pltpu.VMEM((BLOCK, 2 * D_KV), F32),
            pltpu.VMEM((N_KV, ROWS4, 1), F32),
            pltpu.VMEM((GROUPS, BLOCK, 1), F32),
        ],
        compiler_params=_params(("arbitrary",)),
    )(zuv, qkv, qkv, qkv, gv, w_sp, b_sp, bucket, dmix, *saved)


def _in_bwd(x, dh1, dzm, dkv, g1, w_in_t, after=None):
    s = x.shape[0]
    tm = min(IO_TOK_TILE, s)

    def body(x_ref, dh1_ref, dzm_ref, dkv_ref, g_ref, w_ref, dx_ref, dg_ref):
        @pl.when(pl.program_id(0) == 0)
        def _():
            dg_ref[...] = jnp.zeros_like(dg_ref)

        dhn = _nn(dzm_ref[...], w_ref[:D_MAIN, :]) + _nn(dkv_ref[...], w_ref[D_MAIN:, :])
        xv = x_ref[...]
        r = _rms_scale(xv)
        dg_ref[...] += jnp.sum(dhn * (xv * r), axis=0, keepdims=True)
        dx_ref[...] = dh1_ref[...] + _rms_bwd(dhn * g_ref[...], xv, r)

    tile = lambda cols: pl.BlockSpec((tm, cols), lambda i: (i, 0))
    row = pl.BlockSpec((1, D_MODEL), lambda i: (0, 0))
    return _call(
        body,
        name="in_bwd",
        after=after,
        grid=(s // tm,),
        in_specs=[tile(D_MODEL), tile(D_MODEL), tile(D_MAIN), tile(2 * D_KV), row, pl.BlockSpec((D_IN, D_MODEL), lambda i: (0, 0))],
        out_specs=[tile(D_MODEL), row],
        out_shape=[jax.ShapeDtypeStruct((s, D_MODEL), F32), jax.ShapeDtypeStruct((1, D_MODEL), F32)],
        compiler_params=_params(("arbitrary",)),
    )(x, dh1, dzm, dkv, g1, w_in_t)


def _wgrad_in(dzm, dkv, hn, after=None):
    s = hn.shape[0]
    tm = 2 * D_KV
    n_main = D_MAIN // tm

    def body(dzm_ref, dkv_ref, hn_ref, o_ref):
        i = pl.program_id(0)

        @pl.when(i < n_main)
        def _():
            o_ref[...] = _tn(dzm_ref[...], hn_ref[...])

        @pl.when(i == n_main)
        def _():
            o_ref[...] = _tn(dkv_ref[...], hn_ref[...])

    return _call(
        body,
        name="wgrad_in",
        after=after,
        grid=(n_main + 1,),
        in_specs=[
            pl.BlockSpec((s, tm), lambda i: (0, jnp.minimum(i, n_main - 1))),
            pl.BlockSpec((s, tm), lambda i: (0, 0)),
            pl.BlockSpec((s, D_MODEL), lambda i: (0, 0)),
        ],
        out_specs=pl.BlockSpec((tm, D_MODEL), lambda i: (i, 0)),
        out_shape=jax.ShapeDtypeStruct((D_IN, D_MODEL), F32),
        compiler_params=_params(("arbitrary",)),
    )(dzm, dkv, hn)


def _wgrad(a, b, tm, tn, name, peer_cols=0, after=None):
    s, m = a.shape
    n = b.shape[1]

    def body(a_ref, b_ref, o_ref, at_ref):
        @pl.when(pl.program_id(1) == 0)
        def _():
            at_ref[...] = a_ref[...].astype(BF).T

        r = _nn(at_ref[...], b_ref[...])
        if peer_cols:
            for q in range(tn // peer_cols):
                o_ref[q] = r[:, q * peer_cols:(q + 1) * peer_cols]
        else:
            o_ref[...] = r

    if peer_cols:
        out_spec = pl.BlockSpec((tn // peer_cols, tm, peer_cols), lambda i, j: (j, i, 0))
        out_shape = jax.ShapeDtypeStruct((n // peer_cols, m, peer_cols), F32)
    else:
        out_spec = pl.BlockSpec((tm, tn), lambda i, j: (i, j))
        out_shape = jax.ShapeDtypeStruct((m, n), F32)
    return _call(
        body,
        name=name,
        after=after,
        grid=(m // tm, n // tn),
        in_specs=[pl.BlockSpec((s, tm), lambda i, j: (0, i)), pl.BlockSpec((s, tn), lambda i, j: (0, j))],
        out_specs=out_spec,
        out_shape=out_shape,
        scratch_shapes=[pltpu.VMEM((tm, s), BF)],
        compiler_params=_params(("arbitrary", "arbitrary")),
    )(a, b)


def _adamw_math(w, g, m, v):
    m_new = ADAM_B1 * m + (1.0 - ADAM_B1) * g
    v_new = ADAM_B2 * v + (1.0 - ADAM_B2) * jnp.square(g)
    m_hat = m_new / (1.0 - ADAM_B1 ** ADAM_STEP)
    v_hat = v_new / (1.0 - ADAM_B2 ** ADAM_STEP)
    delta = -ADAM_LR * (m_hat / (jnp.sqrt(v_hat) + ADAM_EPS) + ADAM_WD * w)
    return delta, m_new, v_new


def _final_adamw(part, recv, w, m, v, name):
    r, c = w.shape
    tr = min(r, 512)

    def body(p_ref, r_ref, w_ref, m_ref, v_ref, g_ref, d_ref, mo_ref, vo_ref):
        g = p_ref[...]
        for j in range(3):
            g = g + r_ref[j].astype(F32)
        g_ref[...] = g
        d_ref[...], mo_ref[...], vo_ref[...] = _adamw_math(w_ref[...], g, m_ref[...], v_ref[...])

    spec = pl.BlockSpec((tr, c), lambda i: (i, 0))
    return _call(
        body,
        name=name,
        grid=(r // tr,),
        in_specs=[spec, pl.BlockSpec((3, tr, c), lambda i: (0, i, 0)), spec, spec, spec],
        out_specs=[spec] * 4,
        out_shape=[jax.ShapeDtypeStruct((r, c), F32)] * 4,
        compiler_params=_params(("arbitrary",)),
    )(part, recv, w, m, v)


def _rs_sum(g, land, blocks, name):
    _, r, c = g.shape
    tr = min(r, 256)

    def body(blk_ref, g0_ref, g1_ref, g2_ref, g3_ref, l_ref, part_ref, send_ref):
        part_ref[...] = g0_ref[0] + l_ref[0]
        for j, gj_ref in enumerate((g1_ref, g2_ref, g3_ref)):
            send_ref[j] = (gj_ref[0] + l_ref[j + 1]).astype(BF)

    def pick(j):
        return pl.BlockSpec((1, tr, c), lambda i, blk: (blk[j], i, 0))

    return _call(
        body,
        name=name,
        grid_spec=pltpu.PrefetchScalarGridSpec(
            num_scalar_prefetch=1,
            grid=(r // tr,),
            in_specs=[pick(0), pick(1), pick(2), pick(3), pl.BlockSpec((4, tr, c), lambda i, blk: (0, i, 0))],
            out_specs=[pl.BlockSpec((tr, c), lambda i, blk: (i, 0)), pl.BlockSpec((3, tr, c), lambda i, blk: (0, i, 0))],
        ),
        out_shape=[jax.ShapeDtypeStruct((r, c), F32), jax.ShapeDtypeStruct((3, r, c), BF)],
        compiler_params=_params(("arbitrary",)),
    )(blocks, g, g, g, g, land)


def _adamw_small(own, recv, weights, moms, vels, after):
    n_w = len(weights)

    def body(*refs):
        own_refs, recv_refs = refs[:6], refs[6:12]
        w_refs, m_refs, v_refs = refs[12:12 + n_w], refs[12 + n_w:12 + 2 * n_w], refs[12 + 2 * n_w:12 + 3 * n_w]
        outs = refs[12 + 3 * n_w:]
        loss_ref, g_refs, d_refs = outs[0], outs[1:1 + n_w], outs[1 + n_w:1 + 2 * n_w]
        mo_refs, vo_refs = outs[1 + 2 * n_w:1 + 3 * n_w], outs[1 + 3 * n_w:]
        x, y = lax.axis_index("x"), lax.axis_index("y")

        def total(k, *index):
            index = index or (slice(None),) * (len(own_refs[k].shape) - 1)
            across = [own_refs[k][(0, *index)]] + [recv_refs[k][(j, *index)] for j in range(3)]
            acc = None
            for cx in range(2):
                for cy in range(2):
                    flip = (cx ^ x) + 2 * (cy ^ y)
                    part = jnp.where(flip == 0, across[0], jnp.where(flip == 1, across[1], jnp.where(flip == 2, across[2], across[3])))
                    acc = part if acc is None else acc + part
            return acc

        grads = [
            total(1), total(2), total(3), total(4),
            total(5, slice(N_Q, None), slice(0, N_Q)), total(5, slice(0, N_Q), slice(None)),
            total(0, slice(0, 1), slice(None)), total(0, slice(1, 2), slice(None))]
        loss_ref[...] = total(0, slice(2, 3), slice(0, 1))
        for k in range(n_w):
            g_refs[k][...] = grads[k]
            d_refs[k][...], mo_refs[k][...], vo_refs[k][...] = _adamw_math(w_refs[k][...], grads[k], m_refs[k][...], v_refs[k][...])

    shapes = [jax.ShapeDtypeStruct(w.shape, F32) for w in weights]
    outs = _call(
        body,
        name="adamw_small",
        after=after,
        in_specs=[pl.BlockSpec(memory_space=pltpu.VMEM)] * (12 + 3 * n_w),
        out_shape=[jax.ShapeDtypeStruct((1, 1), F32)] + shapes * 4,
        compiler_params=_params(),
    )(*own, *recv, *weights, *moms, *vels)
    return outs[0], outs[1:1 + n_w], outs[1 + n_w:1 + 2 * n_w], outs[1 + 2 * n_w:1 + 3 * n_w], outs[1 + 3 * n_w:]


def _place():
    x, y, c = lax.axis_index("x"), lax.axis_index("y"), lax.axis_index("c")
    return x, y, c, [(1 - x, y), (x, 1 - y), (1 - x, 1 - y)]


def _dev_index(px, py, pc):
    return 4 * px + 2 * py + pc


HBM_SPEC = pl.BlockSpec(memory_space=pltpu.HBM)
SEM_SPEC = pl.BlockSpec(memory_space=pltpu.SEMAPHORE)
ANY_SPEC = pl.BlockSpec(memory_space=pl.ANY)
DATAFLOW = pltpu.SideEffectType.DATAFLOW_SIDE_EFFECTING


def _hbm(a):
    return pltpu.with_memory_space_constraint(a, pltpu.HBM)


def _prep_weights(shards):
    k_n = len(shards)

    def body(*refs):
        ins, outs, stage, sems = refs[:k_n], refs[k_n:2 * k_n], refs[2 * k_n:3 * k_n], refs[3 * k_n]
        x, y, c, _ = _place()
        copies = []
        for k in range(k_n):
            stage[k][...] = ins[k][...].astype(BF)
            copies.append(pltpu.make_async_copy(stage[k], outs[k].at[_dev_index(x, y, c)], sems.at[k]))
            copies[k].start()
        for cp in copies:
            cp.wait()

    return _call(
        body,
        name="prep_weights",
        in_specs=[pl.BlockSpec(memory_space=pltpu.VMEM)] * k_n,
        out_specs=[ANY_SPEC] * k_n,
        out_shape=[jax.ShapeDtypeStruct((N_DEV,) + sh.shape, BF) for sh in shards],
        scratch_shapes=[pltpu.VMEM(sh.shape, BF) for sh in shards] + [pltpu.SemaphoreType.DMA((k_n,))],
        compiler_params=_params(),
    )(*shards)


def _ag_piece(land_k, block, half, peer, send_sem, recv_sem):
    ref = land_k.at[_dev_index(*block)]
    if half is not None:
        rows = land_k.shape[1] // 2
        ref = ref.at[pl.ds(half * rows, rows)]
    return pltpu.make_async_remote_copy(
        src_ref=ref, dst_ref=ref, send_sem=send_sem, recv_sem=recv_sem, device_id=peer, device_id_type=MESH)


def _ag_plan():
    x, y, c, _ = _place()
    me, sib = (x, y, c), (x, y, 1 - c)
    xn, yn, diag = (1 - x, y, c), (x, 1 - y, c), (1 - x, 1 - y, c)
    return dict(
        relay_halves=[(me, 0, xn), (me, 1, yn)],
        others=[(me, None, sib), (me, 1, xn), (me, 0, yn)],
        relays=[(xn, 0, yn), (yn, 1, xn)],
        near=[(xn, None, sib), (yn, None, sib)],
        far=[(diag, None, sib)],
    )


def _ag_stage(land, stage, send_sems, recv_sems, act):
    copies = _ag_plan()[stage]
    n = len(copies)
    for k in range(len(land)):
        for j, (block, half, peer) in enumerate(copies):
            cp = _ag_piece(land[k], block, half, peer, send_sems.at[n * k + j], recv_sems.at[n * k + j])
            if act == "start":
                cp.start()
            else:
                cp.wait_send()
                cp.wait_recv()


def _sem_shapes(*counts):
    return [pltpu.SemaphoreType.DMA((n,)) for n in counts for _ in range(2)]


def _ag_start(first, rest):
    lands = list(first) + list(rest)
    k_n, k_first = len(lands), len(first)
    k_rest = k_n - k_first

    def body(*refs):
        land = refs[:k_n]
        sems = refs[k_n:k_n + 6]
        token = refs[-1]
        x, y, c, chips = _place()
        targets = [(x, y, 1 - c)] + [(*chip, c) for chip in chips]
        for k in range(k_first):
            for j, to in enumerate(targets):
                _ag_piece(land[k], (x, y, c), None, to, sems[0].at[4 * k + j], sems[1].at[4 * k + j]).start()
        _ag_stage(land[k_first:], "relay_halves", sems[2], sems[3], "start")
        _ag_stage(land[k_first:], "others", sems[4], sems[5], "start")
        token[...] = jnp.zeros_like(token)

    outs = pl.pallas_call(
        body,
        name="ag_start",
        in_specs=[HBM_SPEC] * k_n,
        out_specs=(*[SEM_SPEC] * 6, *[HBM_SPEC] * k_n, pl.BlockSpec(memory_space=pltpu.VMEM)),
        out_shape=(*_sem_shapes(4 * k_first, 2 * k_rest, 3 * k_rest),
                   *[pltpu.HBM(a.shape, a.dtype) for a in lands], jax.ShapeDtypeStruct((8, LANES), F32)),
        input_output_aliases={k: 6 + k for k in range(k_n)},
        compiler_params=pltpu.CompilerParams(has_side_effects=DATAFLOW),
    )(*[_hbm(a) for a in lands])
    flying = list(outs[6:6 + k_n])
    return (outs[0], outs[1], flying[:k_first]), (outs[2:6], flying[k_first:]), outs[-1]


def _ag_split_call(lands, waits, starts, after, name):
    k_n = len(lands)
    plan_sizes = dict(relay_halves=2, others=3, relays=2, near=2, far=1)
    n_in, n_out = 2 * len(waits), 2 * len(starts)

    def body(*refs):
        land = refs[:k_n]
        in_sems = refs[k_n:k_n + n_in]
        out_sems, token = refs[len(refs) - 1 - n_out:len(refs) - 1], refs[-1]
        for w, (stage, _, _) in enumerate(waits):
            _ag_stage(land, stage, in_sems[2 * w], in_sems[2 * w + 1], "wait")
            if w < len(starts):
                _ag_stage(land, starts[w], out_sems[2 * w], out_sems[2 * w + 1], "start")
        token[...] = jnp.zeros_like(token)

    outs = pl.pallas_call(
        body,
        name=name,
        in_specs=[HBM_SPEC] * k_n + [SEM_SPEC] * n_in + [ANY_SPEC],
        out_specs=(*[HBM_SPEC] * k_n, *[SEM_SPEC] * n_out, pl.BlockSpec(memory_space=pltpu.VMEM)),
        out_shape=(*[pltpu.HBM(a.shape, a.dtype) for a in lands], *_sem_shapes(*[plan_sizes[s] * k_n for s in starts]),
                   jax.ShapeDtypeStruct((8, LANES), F32)),
        input_output_aliases={k: k for k in range(k_n)},
        compiler_params=pltpu.CompilerParams(has_side_effects=DATAFLOW),
    )(*lands, *[s for _, a, b in waits for s in (a, b)], after)
    return list(outs[:k_n]), list(outs[k_n:k_n + n_out]), outs[-1]


def _ag_mid(lands, send_sems, recv_sems, after, name):
    k_n = len(lands)

    def body(*refs):
        land = refs[:k_n]
        send1, recv1 = refs[k_n], refs[k_n + 1]
        fwd_send, fwd_recv = refs[-2], refs[-1]
        x, y, c, chips = _place()
        sources = [(x, y, 1 - c)] + [(*chip, c) for chip in chips]
        for k in range(k_n):
            mine = land[k].at[_dev_index(x, y, c)]
            for j, frm in enumerate(sources):
                got = land[k].at[_dev_index(*frm)]
                cp = pltpu.make_async_remote_copy(
                    src_ref=mine, dst_ref=got, send_sem=send1.at[4 * k + j], recv_sem=recv1.at[4 * k + j], device_id=frm, device_id_type=MESH)
                cp.wait_send()
                cp.wait_recv()
                if j >= 1:
                    pltpu.make_async_remote_copy(
                        src_ref=got, dst_ref=got, send_sem=fwd_send.at[3 * k + j - 1], recv_sem=fwd_recv.at[3 * k + j - 1],
                        device_id=(x, y, 1 - c), device_id_type=MESH).start()

    outs = pl.pallas_call(
        body,
        name=name,
        in_specs=[HBM_SPEC] * k_n + [SEM_SPEC, SEM_SPEC, ANY_SPEC],
        out_specs=(*[HBM_SPEC] * k_n, SEM_SPEC, SEM_SPEC),
        out_shape=(*[pltpu.HBM(a.shape, a.dtype) for a in lands], pltpu.SemaphoreType.DMA((3 * k_n,)), pltpu.SemaphoreType.DMA((3 * k_n,))),
        input_output_aliases={k: k for k in range(k_n)},
        compiler_params=pltpu.CompilerParams(has_side_effects=DATAFLOW),
    )(*lands, send_sems, recv_sems, after)
    return list(outs[:k_n]), outs[-2], outs[-1]


def _ag_end(lands, fwd_send, fwd_recv, name):
    k_n = len(lands)

    def body(*refs):
        land = refs[:k_n]
        fsend, frecv = refs[k_n], refs[k_n + 1]
        x, y, c, chips = _place()
        for k in range(k_n):
            for j, chip in enumerate(chips):
                cp = pltpu.make_async_remote_copy(
                    src_ref=land[k].at[_dev_index(*chip, c)], dst_ref=land[k].at[_dev_index(*chip, 1 - c)],
                    send_sem=fsend.at[3 * k + j], recv_sem=frecv.at[3 * k + j], device_id=(x, y, 1 - c), device_id_type=MESH)
                cp.wait_send()
                cp.wait_recv()

    outs = pl.pallas_call(
        body,
        name=name,
        in_specs=[HBM_SPEC] * k_n + [SEM_SPEC, SEM_SPEC],
        out_specs=tuple([HBM_SPEC] * k_n),
        out_shape=tuple(pltpu.HBM(a.shape, a.dtype) for a in lands),
        input_output_aliases={k: k for k in range(k_n)},
        compiler_params=pltpu.CompilerParams(has_side_effects=DATAFLOW),
    )(*lands, fwd_send, fwd_recv)
    return list(outs)


def _chips4():
    x, y, c, others = _place()
    return x, y, c, [(x, y)] + others


def _route_sibling(j):
    x, y, c, chips = _chips4()
    return _dev_index(*chips[j], 1 - c), j, (x, y, 1 - c)


def _route_chips(j):
    x, y, c, chips = _chips4()
    return j, j, (*chips[j + 1], c)


def _route_sibling_whole(j):
    x, y, c, _ = _chips4()
    return 0, 0, (x, y, 1 - c)


def _route_chips_whole(j):
    x, y, c, chips = _chips4()
    return 0, j, (*chips[j + 1], c)


def _xchg_copies(routes, src, dst, send_sems, recv_sems):
    copies, sem = [], 0
    for k, (route, n) in enumerate(routes):
        for j in range(n):
            si, di, peer = route(j)
            copies.append(pltpu.make_async_remote_copy(
                src_ref=src[k].at[si], dst_ref=dst[k].at[di], send_sem=send_sems.at[sem], recv_sem=recv_sems.at[sem],
                device_id=peer, device_id_type=MESH))
            sem += 1
    return copies


def _xchg_start(srcs, slot_shapes, routes, name):
    k_n = len(srcs)
    n_sem = sum(n for _, n in routes)
    dsts = [lax.empty((n,) + tuple(sh), a.dtype) for sh, a, (_, n) in zip(slot_shapes, srcs, routes)]

    def body(*refs):
        src, dst = refs[:k_n], refs[k_n:2 * k_n]
        send_sems, recv_sems, token = refs[2 * k_n], refs[2 * k_n + 1], refs[-1]
        for cp in _xchg_copies(routes, src, dst, send_sems, recv_sems):
            cp.start()
        token[...] = jnp.zeros_like(token)

    arrays = list(srcs) + dsts
    outs = pl.pallas_call(
        body,
        name=name,
        in_specs=[HBM_SPEC] * (2 * k_n),
        out_specs=(SEM_SPEC, SEM_SPEC, *[HBM_SPEC] * (2 * k_n), pl.BlockSpec(memory_space=pltpu.VMEM)),
        out_shape=(pltpu.SemaphoreType.DMA((n_sem,)), pltpu.SemaphoreType.DMA((n_sem,)),
                   *[pltpu.HBM(a.shape, a.dtype) for a in arrays], jax.ShapeDtypeStruct((8, LANES), F32)),
        input_output_aliases={i: 2 + i for i in range(2 * k_n)},
        compiler_params=pltpu.CompilerParams(has_side_effects=DATAFLOW),
    )(*[_hbm(a) for a in arrays])
    return outs[0], outs[1], list(outs[2:2 + k_n]), list(outs[2 + k_n:2 + 2 * k_n]), outs[-1]


def _xchg_wait(send_sems, recv_sems, srcs, dsts, routes, after, name):
    k_n = len(srcs)

    def body(*refs):
        src, dst = refs[:k_n], refs[k_n:2 * k_n]
        for cp in _xchg_copies(routes, src, dst, refs[2 * k_n], refs[2 * k_n + 1]):
            cp.wait_send()
            cp.wait_recv()

    arrays = list(srcs) + list(dsts)
    outs = pl.pallas_call(
        body,
        name=name,
        in_specs=[HBM_SPEC] * (2 * k_n) + [SEM_SPEC, SEM_SPEC, ANY_SPEC],
        out_specs=tuple([HBM_SPEC] * (2 * k_n)),
        out_shape=tuple(pltpu.HBM(a.shape, a.dtype) for a in arrays),
        input_output_aliases={i: i for i in range(2 * k_n)},
        compiler_params=pltpu.CompilerParams(has_side_effects=DATAFLOW),
    )(*arrays, send_sems, recv_sems, after)
    return list(outs[:k_n]), list(outs[k_n:])


SMALL = ("norm1_gain", "gmlp_v_gain", "w_spatial", "b_spatial", "attn_sinks", "rel_bias_table", "norm2_gain", "final_gain")
LANES = 128


def _swap_start(grads, smalls, tag):
    srcs = list(grads) + [a[None] for a in smalls]
    shapes = [g.shape[1:] for g in grads] + [a.shape for a in smalls]
    routes = [(_route_sibling, 4)] * len(grads) + [(_route_sibling_whole, 1)] * len(smalls)
    return _xchg_start(srcs, shapes, routes, f"rs_{tag}_swap_start"), routes, len(grads)


def _swap_sums(swap, names, after, tag):
    (send1, recv1, src1, land1, _), routes, n_rs = swap
    x, y, c, chips = _chips4()
    blocks = jnp.stack([_dev_index(*chip, c) for chip in chips]).astype(jnp.int32)
    src1, land1 = _xchg_wait(send1, recv1, src1, land1, routes, after, f"rs_{tag}_swap_wait")
    sums = [_rs_sum(g, land, blocks, f"rs_sum_{n}") for g, land, n in zip(src1[:n_rs], land1[:n_rs], names)]
    return sums, (_pair_sum(src1[n_rs:], land1[n_rs:], f"rs_{tag}_sum_small") if len(src1) > n_rs else [])


def _pair_sum(mine, theirs, name):
    def body(*refs):
        n = len(refs) // 3
        for k in range(n):
            refs[2 * n + k][...] = refs[k][...] + refs[n + k][...]

    return _call(
        body,
        name=name,
        out_shape=[jax.ShapeDtypeStruct(a.shape, F32) for a in mine],
        compiler_params=_params(),
    )(*mine, *theirs)


def _chips_start(sums, small_sums, tag):
    sends = [ps[1] for ps in sums] + list(small_sums)
    routes = [(_route_chips, 3)] * len(sums) + [(_route_chips_whole, 3)] * len(small_sums)
    return _xchg_start(sends, [a.shape[1:] for a in sends], routes, f"rs_{tag}_chips_start"), routes


def _chips_wait(chips, after, tag):
    (send2, recv2, src2, land2, _), routes = chips
    return _xchg_wait(send2, recv2, src2, land2, routes, after, f"rs_{tag}_chips_wait")


def kernel(x, p, norm1_gain, w_in, gmlp_v_gain, w_spatial, b_spatial, attn_sinks, rel_bias_table, w_out, norm2_gain, w_ff1, w_ff2, w_ple_proj, w_ple_gate, final_gain, loss_target, m_norm1_gain, m_w_in, m_gmlp_v_gain, m_w_spatial, m_b_spatial, m_attn_sinks, m_rel_bias_table, m_w_out, m_norm2_gain, m_w_ff1, m_w_ff2, m_w_ple_proj, m_w_ple_gate, m_final_gain, v_norm1_gain, v_w_in, v_gmlp_v_gain, v_w_spatial, v_b_spatial, v_attn_sinks, v_rel_bias_table, v_w_out, v_norm2_gain, v_w_ff1, v_w_ff2, v_w_ple_proj, v_w_ple_gate, v_final_gain):
    args = dict(locals())
    s = x.shape[1]
    big = ("w_in", "w_out", "w_ff1", "w_ff2", "w_ple_proj", "w_ple_gate")

    x2, p2, t2 = x.reshape(s, D_MODEL), p.reshape(s, PLE_DIM), loss_target.reshape(s, D_MODEL)
    g1, gv, w_sp, b_sp, sinks, table, g2, gf = (args[n] for n in SMALL)
    bucket = jnp.asarray(_bucket_table())
    b_col = b_sp.reshape(GROUPS, BLOCK, 1)

    def shard(name):
        return args[name][0].T if name.endswith("w_in") else args[name][0]

    lands = _prep_weights([shard(n) for n in big])
    (send_in, recv_in, fly_in), (rest_sems, fly_rest), token = _ag_start(lands[:1], lands[1:])
    mid_in, fwd_send_in, fwd_recv_in = _ag_mid(fly_in, send_in, recv_in, token, "ag_mid_w_in")
    g_in = _ag_end(mid_in, fwd_send_in, fwd_recv_in, "ag_end_w_in")[0]
    full_in = g_in.reshape(D_IN, D_MODEL)

    zuv, qkv, hn1 = _in_proj(x2, g1, full_in)
    fly_rest, relay_sems, relayed = _ag_split_call(
        fly_rest, [("relay_halves", *rest_sems[:2])], ["relays"], zuv, "ag_relay")
    mix, *saved = _mixer_fwd(zuv, qkv, gv, w_sp[0], b_col, sinks, table.T, bucket, after=relayed)
    fly_rest, fwd_sems, _ = _ag_split_call(
        fly_rest, [("others", *rest_sems[2:]), ("relays", *relay_sems)], ["near", "far"], mix, "ag_mid_rest")
    g_out, g_ff1, g_ff2, g_proj, g_gate = _ag_split_call(
        fly_rest, [("near", *fwd_sems[:2]), ("far", *fwd_sems[2:])], [], mix, "ag_end_rest")[0]
    full_out, full_ff2, full_gate = g_out.reshape(D_MODEL, D_MODEL), g_ff2.reshape(D_FF, D_MODEL), g_gate.reshape(D_MODEL, D_MODEL)
    full_proj = g_proj.transpose(1, 0, 2).reshape(PLE_DIM, D_MODEL)

    (tail_small, dh1, dh1b, dmix, hn2, a, df, h2, dh2, dgl, dpp) = _tail(
        x2, mix, p2, t2, g2, gf.reshape(1, D_MODEL), full_out, g_ff1, full_ff2, full_gate, full_proj)

    ffn_names, mid_names = ("w_ff1", "w_ff2"), ("w_out", "w_ple_proj", "w_ple_gate")
    gw_ff1 = _wgrad(hn2, df, D_MODEL, D_FF // N_DEV, "wgrad_ff1", peer_cols=D_FF // N_DEV)
    gw_ff2 = _wgrad(a, dh2, 1024, D_MODEL, "wgrad_ff2").reshape(N_DEV, D_FF // N_DEV, D_MODEL)
    ffn_swap = _swap_start([gw_ff1, gw_ff2], [], "ffn")
    gw_gate = _wgrad(h2, dgl, D_MODEL, 512, "wgrad_gate", after=ffn_swap[0][4]).reshape(N_DEV, D_MODEL // N_DEV, D_MODEL)
    gw_proj = _wgrad(p2, dpp, PLE_DIM, D_MODEL // 2, "wgrad_proj", peer_cols=D_MODEL // N_DEV, after=gw_gate)
    gw_out = _wgrad(mix, dh1b, D_MODEL, 512, "wgrad_out", after=gw_proj).reshape(N_DEV, D_MODEL // N_DEV, D_MODEL)
    mid_swap = _swap_start([gw_out, gw_proj, gw_gate], [tail_small], "mid")
    ffn_sums, _ = _swap_sums(ffn_swap, ffn_names, mid_swap[0][4], "ffn")
    ffn_chips = _chips_start(ffn_sums, [], "ffn")

    dzm, dkv, d_gv, d_wsp, d_bsp, d_attn = _mixer_bwd(
        zuv, qkv, dmix, saved, gv, w_sp[0], b_col, bucket, after=ffn_chips[0][4])
    mid_sums, mid_small = _swap_sums(mid_swap, mid_names, dzm, "mid")
    mid_chips = _chips_start(mid_sums, mid_small, "mid")
    gw_in = _wgrad_in(dzm, dkv, hn1, after=mid_chips[0][4]).reshape(N_DEV, D_IN // N_DEV, D_MODEL)
    dx, d_g1 = _in_bwd(x2, dh1, dzm, dkv, g1, full_in, after=gw_in)
    grad_x = dx.reshape(x.shape)
    in_swap = _swap_start([gw_in], [d_g1, d_gv, d_wsp, d_bsp, d_attn], "in")
    in_sums, in_small = _swap_sums(in_swap, ("w_in",), in_swap[0][4], "in")
    in_chips = _chips_start(in_sums, in_small, "in")

    grads, deltas, new_m, new_v = {}, {}, {}, {}
    after, small_own, small_recv = in_chips[0][4], [], []
    for chips, sums, names, tag in ((ffn_chips, ffn_sums, ffn_names, "ffn"), (mid_chips, mid_sums, mid_names, "mid"),
                                    (in_chips, in_sums, ("w_in",), "in")):
        sent, recvs = _chips_wait(chips, after, tag)
        small_own += sent[len(names):]
        small_recv += recvs[len(names):]
        for n, (part, _), recv in zip(names, sums, recvs):
            g, d, mo, vo = _final_adamw(part, recv, shard(n), shard("m_" + n), shard("v_" + n), "adamw_" + n)
            for dst, arr in zip((grads, deltas, new_m, new_v), (g, d, mo, vo)):
                dst[n] = (arr.T if n == "w_in" else arr)[None]
            after = d

    views = {"w_spatial": (GROUPS, BLOCK, BLOCK), "b_spatial": (GROUPS, BLOCK), "final_gain": (1, D_MODEL)}
    def view(name, base):
        return args[name].T if base == "rel_bias_table" else args[name].reshape(views.get(base, args[name].shape))

    small_in = [[view(pre + n, n) for n in SMALL] for pre in ("", "m_", "v_")]
    loss, *small_out = _adamw_small(small_own, small_recv, *small_in, after=after)
    for dst, arrays in zip((grads, deltas, new_m, new_v), small_out):
        for n, arr in zip(SMALL, arrays):
            dst[n] = arr.T if n == "rel_bias_table" else arr.reshape(args[n].shape)
    loss = loss[0, 0]

    order = ("norm1_gain", "w_in", "gmlp_v_gain", "w_spatial", "b_spatial", "attn_sinks", "rel_bias_table", "w_out",
             "norm2_gain", "w_ff1", "w_ff2", "w_ple_proj", "w_ple_gate", "final_gain")
    return (loss, grad_x, *[grads[n] for n in order], *[deltas[n] for n in order],
            *[new_m[n] for n in order], *[new_v[n] for n in order])
```

```python
import functools
import math

import numpy as np
import jax
import jax.numpy as jnp
from jax import lax
from jax.experimental import pallas as pl
from jax.experimental.pallas import tpu as pltpu

F32 = jnp.float32
BF = jnp.bfloat16
MESH = pl.DeviceIdType.MESH
N_DEV = 8

D_MODEL = 1024
PLE_DIM = 256
D_GMLP = 512
GROUPS = 4
GDIM = 128
BLOCK = 128
D_ATTN = 512
HEAD_DIM = 64
N_Q = 8
Q_PER_KV = 4
N_KV = N_Q // Q_PER_KV
ROWS4 = Q_PER_KV * BLOCK
D_KV = 128
D_FF = 4096
D_IN = 1792
D_MAIN = 2 * D_GMLP + D_ATTN
REL_BUCKETS = 32
EPS = 1e-6
NEG_INF = -1e30
SCALE = HEAD_DIM ** -0.5
GELU_C = math.sqrt(2.0 / math.pi)
GELU_A = 0.044715

ADAM_LR = 0.001
ADAM_B1 = 0.9
ADAM_B2 = 0.999
ADAM_EPS = 1e-08
ADAM_WD = 0.01
ADAM_STEP = 10

V7X_VMEM_LIMIT = 60000 * 1024
TOK_TILE = 256
IO_TOK_TILE = 512


def _call(body, after=None, **kw):
    if after is None:
        return pl.pallas_call(body, **kw)
    n_in = len(kw["in_specs"])

    def ordered(*refs):
        body(*refs[:n_in], *refs[n_in + 1:])

    kw["in_specs"] = list(kw["in_specs"]) + [pl.BlockSpec(memory_space=pl.ANY)]
    fn = pl.pallas_call(ordered, **kw)
    return lambda *operands: fn(*operands, after)


def _params(sem=None):
    if sem is None:
        return pltpu.CompilerParams(vmem_limit_bytes=V7X_VMEM_LIMIT)
    return pltpu.CompilerParams(dimension_semantics=sem, vmem_limit_bytes=V7X_VMEM_LIMIT)


def _nn(a, b):
    return jnp.dot(a, b, preferred_element_type=F32)


def _nt(a, b):
    return lax.dot_general(a, b, (((1,), (1,)), ((), ())), preferred_element_type=F32)


def _tn(a, b):
    return lax.dot_general(a, b, (((0,), (0,)), ((), ())), preferred_element_type=F32)


def _gelu_tanh(x):
    return jnp.tanh(GELU_C * (x + GELU_A * (x * x * x)))


def _gelu(x, t):
    return x * (0.5 * (1.0 + t))


def _gelu_and_grad(x, t):
    cdf = 0.5 * (1.0 + t)
    return x * cdf, cdf + 0.5 * x * (1.0 - t * t) * (GELU_C * (1.0 + 3.0 * GELU_A * (x * x)))


def _rms_scale(x):
    return lax.rsqrt(jnp.mean(x * x, axis=-1, keepdims=True) + EPS)


def _rms_bwd(dxn, x, r):
    return r * dxn - x * ((r * r * r) * jnp.mean(dxn * x, axis=-1, keepdims=True))


def _bucket_table():
    a = np.arange(BLOCK)[:, None]
    j = np.arange(2 * BLOCK)[None, :]
    n = BLOCK + a - j
    valid = (n >= 0) & (n < BLOCK)
    nc = np.maximum(n, 0)
    max_exact = REL_BUCKETS // 2
    nf = np.maximum(nc, 1).astype(np.float32)
    large = max_exact + (
        np.log(nf / np.float32(max_exact)) / np.float32(math.log(BLOCK / max_exact)) * np.float32(REL_BUCKETS - max_exact)
    ).astype(np.int32)
    large = np.minimum(large, REL_BUCKETS - 1)
    bucket = np.where(nc < max_exact, nc, large)
    return np.where(valid, bucket, -1).astype(np.int32)


def _in_proj(x, g1, w_in_t):
    s = x.shape[0]
    tm = min(IO_TOK_TILE, s)

    def body(x_ref, g_ref, w_ref, zuv_ref, qkv_ref, hn_ref):
        xv = x_ref[...]
        hn = ((xv * _rms_scale(xv)) * g_ref[...]).astype(BF)
        hn_ref[...] = hn
        z = _nt(hn, w_ref[...])
        zuv_ref[...] = z[:, : 2 * D_GMLP]
        qkv_ref[...] = z[:, 2 * D_GMLP:].astype(BF)

    return _call(
        body,
        name="in_proj",
        grid=(s // tm,),
        in_specs=[
            pl.BlockSpec((tm, D_MODEL), lambda i: (i, 0)),
            pl.BlockSpec((1, D_MODEL), lambda i: (0, 0)),
            pl.BlockSpec((D_IN, D_MODEL), lambda i: (0, 0)),
        ],
        out_specs=[
            pl.BlockSpec((tm, 2 * D_GMLP), lambda i: (i, 0)),
            pl.BlockSpec((tm, D_ATTN + 2 * D_KV), lambda i: (i, 0)),
            pl.BlockSpec((tm, D_MODEL), lambda i: (i, 0)),
        ],
        out_shape=[
            jax.ShapeDtypeStruct((s, 2 * D_GMLP), F32),
            jax.ShapeDtypeStruct((s, D_ATTN + 2 * D_KV), BF),
            jax.ShapeDtypeStruct((s, D_MODEL), BF),
        ],
        compiler_params=_params(("arbitrary",)),
    )(x, g1, w_in_t)


def _head_rows(h):
    kh, g = divmod(h, Q_PER_KV)
    return kh, slice(g * BLOCK, (g + 1) * BLOCK)


def _build_bias(bias_ref, bucket_ref, table_ref):
    bucket = bucket_ref[...]
    for h in range(N_Q):
        acc = jnp.zeros((BLOCK, 2 * BLOCK), F32)
        for b in range(REL_BUCKETS):
            acc = jnp.where(bucket == b, table_ref[h, b], acc)
        kh, rows = _head_rows(h)
        bias_ref[kh, rows, :] = acc


def _window_masks(i):
    row = lax.broadcasted_iota(jnp.int32, (ROWS4, BLOCK), 0) & (BLOCK - 1)
    col = lax.broadcasted_iota(jnp.int32, (ROWS4, BLOCK), 1)
    return (col > row) & (i > 0), col <= row


def _stack_heads(ref, kh, offset):
    first = offset + kh * Q_PER_KV * HEAD_DIM
    return jnp.concatenate(
        [ref[:, first + g * HEAD_DIM: first + (g + 1) * HEAD_DIM].astype(BF) for g in range(Q_PER_KV)], axis=0)


def _stack_sinks(sink_ref, kh):
    return jnp.concatenate([jnp.full((BLOCK, 1), sink_ref[0, kh * Q_PER_KV + g], F32) for g in range(Q_PER_KV)], axis=0)


def _tril_bf16(w_ref, g):
    row = lax.broadcasted_iota(jnp.int32, (BLOCK, BLOCK), 0)
    col = lax.broadcasted_iota(jnp.int32, (BLOCK, BLOCK), 1)
    return jnp.where(col <= row, w_ref[g], 0.0).astype(BF)


def _attn_probs(q_h, k_prev, k_cur, bias_h, sink, valid_prev, valid_cur):
    l_prev = jnp.where(valid_prev, _nt(q_h, k_prev) * SCALE + bias_h[:, :BLOCK], NEG_INF)
    l_cur = jnp.where(valid_cur, _nt(q_h, k_cur) * SCALE + bias_h[:, BLOCK:], NEG_INF)
    m = jnp.maximum(jnp.maximum(jnp.max(l_prev, axis=-1, keepdims=True), jnp.max(l_cur, axis=-1, keepdims=True)), sink)
    e_prev = jnp.exp(l_prev - m)
    e_cur = jnp.exp(l_cur - m)
    e_sink = jnp.exp(sink - m)
    denom = jnp.sum(e_prev, axis=-1, keepdims=True) + jnp.sum(e_cur, axis=-1, keepdims=True) + e_sink
    return e_prev / denom, e_cur / denom, e_sink / denom


def _mixer_specs(nb):
    cl = lambda i: jnp.minimum(i, nb - 1)
    return [
        pl.BlockSpec((BLOCK, 2 * D_GMLP), lambda i: (cl(i), 0)),
        pl.BlockSpec((BLOCK, D_ATTN), lambda i: (cl(i), 0)),
        pl.BlockSpec((BLOCK, 2 * D_KV), lambda i: (cl(i), D_ATTN // (2 * D_KV))),
        pl.BlockSpec((BLOCK, 2 * D_KV), lambda i: (jnp.maximum(cl(i) - 1, 0), D_ATTN // (2 * D_KV))),
        pl.BlockSpec((1, D_GMLP), lambda i: (0, 0)),
        pl.BlockSpec((GROUPS, BLOCK, BLOCK), lambda i: (0, 0, 0)),
        pl.BlockSpec((GROUPS, BLOCK, 1), lambda i: (0, 0, 0)),
        pl.BlockSpec(memory_space=pltpu.SMEM),
        pl.BlockSpec(memory_space=pltpu.SMEM),
        pl.BlockSpec((BLOCK, 2 * BLOCK), lambda i: (0, 0)),
    ]


def _mixer_fwd(zuv, qkv, gv, w_sp, b_sp, sinks, table, bucket, after=None):
    s = zuv.shape[0]
    nb = s // BLOCK

    def body(zuv_ref, q_ref, kvc_ref, kvp_ref, gv_ref, w_ref, b_ref, sink_ref, table_ref, bucket_ref,
             mix_ref, tanh_ref, prob_ref, psink_ref, bias_ref):
        i = pl.program_id(0)

        @pl.when(i == 0)
        def _():
            _build_bias(bias_ref, bucket_ref, table_ref)

        t = _gelu_tanh(zuv_ref[...])
        tanh_ref[...] = t
        u = _gelu(zuv_ref[:, :D_GMLP], t[:, :D_GMLP])
        vg = _gelu(zuv_ref[:, D_GMLP:], t[:, D_GMLP:])
        for g in range(GROUPS):
            sl = slice(g * GDIM, (g + 1) * GDIM)
            vg_g = vg[:, sl]
            vn = ((vg_g * _rms_scale(vg_g)) * gv_ref[:, sl]).astype(BF)
            sv = _nn(_tril_bf16(w_ref, g), vn) + b_ref[g]
            mix_ref[:, sl] = (u[:, sl] * sv).astype(BF)

        valid_prev, valid_cur = _window_masks(i)
        for kh in range(N_KV):
            ksl = slice(kh * HEAD_DIM, (kh + 1) * HEAD_DIM)
            vsl = slice(D_KV + kh * HEAD_DIM, D_KV + (kh + 1) * HEAD_DIM)
            q4 = _stack_heads(q_ref, kh, 0)
            p_prev, p_cur, p_sink = _attn_probs(
                q4, kvp_ref[:, ksl], kvc_ref[:, ksl], bias_ref[kh], _stack_sinks(sink_ref, kh), valid_prev, valid_cur)
            prob_ref[0, kh, :, :BLOCK] = p_prev
            prob_ref[0, kh, :, BLOCK:] = p_cur
            psink_ref[0, kh] = jnp.broadcast_to(p_sink, (ROWS4, LANES))
            o4 = _nn(p_prev.astype(BF), kvp_ref[:, vsl]) + _nn(p_cur.astype(BF), kvc_ref[:, vsl])
            for g in range(Q_PER_KV):
                first = D_GMLP + (kh * Q_PER_KV + g) * HEAD_DIM
                mix_ref[:, first:first + HEAD_DIM] = o4[g * BLOCK:(g + 1) * BLOCK].astype(BF)

    return _call(
        body,
        name="mixer_fwd",
        after=after,
        grid=(nb,),
        in_specs=_mixer_specs(nb),
        out_specs=[
            pl.BlockSpec((BLOCK, D_MODEL), lambda i: (i, 0)),
            pl.BlockSpec((BLOCK, 2 * D_GMLP), lambda i: (i, 0)),
            pl.BlockSpec((1, N_KV, ROWS4, 2 * BLOCK), lambda i: (i, 0, 0, 0)),
            pl.BlockSpec((1, N_KV, ROWS4, LANES), lambda i: (i, 0, 0, 0)),
        ],
        out_shape=[
            jax.ShapeDtypeStruct((s, D_MODEL), BF),
            jax.ShapeDtypeStruct((s, 2 * D_GMLP), F32),
            jax.ShapeDtypeStruct((nb, N_KV, ROWS4, 2 * BLOCK), F32),
            jax.ShapeDtypeStruct((nb, N_KV, ROWS4, LANES), F32),
        ],
        scratch_shapes=[pltpu.VMEM((N_KV, ROWS4, 2 * BLOCK), F32)],
        compiler_params=_params(("arbitrary",)),
    )(zuv, qkv, qkv, qkv, gv, w_sp, b_sp, sinks, table, bucket)


def _tail(x, mix, p, t, g2, gf, w_out, w_ff1, w_ff2, w_gate, w_proj):
    s = x.shape[0]
    tm = min(TOK_TILE, s)
    n_ff = w_ff1.shape[0]
    fc = D_FF // n_ff

    def body(x_ref, mix_ref, p_ref, t_ref, g2_ref, gf_ref, wo_ref, w1_ref, w2_ref, wg_ref, wp_ref,
             small_ref, dh1_ref, dh1b_ref, dmix_ref, hn2_ref, a_ref, df_ref, h2_ref, dh2_ref, dgl_ref, dpp_ref, f_ref):
        i = pl.program_id(0)

        @pl.when(i == 0)
        def _():
            small_ref[...] = jnp.zeros_like(small_ref)

        h1 = x_ref[...] + _nn(mix_ref[...], wo_ref[...])
        r2 = _rms_scale(h1)
        hn2 = ((h1 * r2) * g2_ref[...]).astype(BF)
        hn2_ref[...] = hn2
        h2 = h1
        for c in range(n_ff):
            f = _nn(hn2, w1_ref[c])
            f_ref[:, c * fc:(c + 1) * fc] = f
            a = jnp.square(jnp.maximum(f, 0.0)).astype(BF)
            a_ref[:, c * fc:(c + 1) * fc] = a
            h2 = h2 + _nn(a, w2_ref[c * fc:(c + 1) * fc, :])
        h2b = h2.astype(BF)
        h2_ref[...] = h2b
        gate = jax.nn.sigmoid(_nn(h2b, wg_ref[...]))
        pp = _nn(p_ref[...].astype(BF), wp_ref[...])
        h3 = h2 + gate * pp
        rf = _rms_scale(h3)
        gf_v = gf_ref[...]
        err = (h3 * rf) * gf_v - t_ref[...]
        small_ref[2:3, :] += jnp.sum(jnp.sum(err * err, axis=-1, keepdims=True), axis=0, keepdims=True) * (0.5 / D_MODEL)

        dy = err * (1.0 / D_MODEL)
        small_ref[1:2, :] += jnp.sum(dy * (h3 * rf), axis=0, keepdims=True)
        dh3 = _rms_bwd(dy * gf_v, h3, rf)
        dpp_ref[...] = (dh3 * gate).astype(BF)
        dgl = ((dh3 * pp) * (gate * (1.0 - gate))).astype(BF)
        dgl_ref[...] = dgl
        dh2 = dh3 + _nt(dgl, wg_ref[...])
        dh2b = dh2.astype(BF)
        dh2_ref[...] = dh2b
        dhn2 = jnp.zeros((tm, D_MODEL), F32)
        for c in range(n_ff):
            da = _nt(dh2b, w2_ref[c * fc:(c + 1) * fc, :])
            df = (da * (2.0 * jnp.maximum(f_ref[:, c * fc:(c + 1) * fc], 0.0))).astype(BF)
            df_ref[:, c * fc:(c + 1) * fc] = df
            dhn2 = dhn2 + _nt(df, w1_ref[c])
        small_ref[0:1, :] += jnp.sum(dhn2 * (h1 * r2), axis=0, keepdims=True)
        dh1 = dh2 + _rms_bwd(dhn2 * g2_ref[...], h1, r2)
        dh1_ref[...] = dh1
        dh1b = dh1.astype(BF)
        dh1b_ref[...] = dh1b
        dmix_ref[...] = _nt(dh1b, wo_ref[...])

    tile = lambda cols: pl.BlockSpec((tm, cols), lambda i: (i, 0))
    whole = lambda shape: pl.BlockSpec(shape, lambda i: (0,) * len(shape), pipeline_mode=pl.Buffered(1))
    row = pl.BlockSpec((1, D_MODEL), lambda i: (0, 0))
    act = lambda cols, dt: jax.ShapeDtypeStruct((s, cols), dt)
    return _call(
        body,
        name="tail",
        grid=(s // tm,),
        in_specs=[tile(D_MODEL), tile(D_MODEL), tile(PLE_DIM), tile(D_MODEL), row, row,
                  whole(w_out.shape), whole(w_ff1.shape), whole(w_ff2.shape), whole(w_gate.shape), whole(w_proj.shape)],
        out_specs=[pl.BlockSpec((8, D_MODEL), lambda i: (0, 0)), tile(D_MODEL), tile(D_MODEL), tile(D_MODEL), tile(D_MODEL), tile(D_FF),
                   tile(D_FF), tile(D_MODEL), tile(D_MODEL), tile(D_MODEL), tile(D_MODEL)],
        out_shape=[jax.ShapeDtypeStruct((8, D_MODEL), F32),
                   act(D_MODEL, F32), act(D_MODEL, BF), act(D_MODEL, F32), act(D_MODEL, BF), act(D_FF, BF), act(D_FF, BF), act(D_MODEL, BF),
                   act(D_MODEL, BF), act(D_MODEL, BF), act(D_MODEL, BF)],
        scratch_shapes=[pltpu.VMEM((tm, D_FF), F32)],
        compiler_params=_params(("arbitrary",)),
    )(x, mix, p, t, g2, gf, w_out, w_ff1, w_ff2, w_gate, w_proj)


def _mixer_bwd(zuv, qkv, dmix, saved, gv, w_sp, b_sp, bucket, after=None):
    s = zuv.shape[0]
    nb = s // BLOCK

    def body(zuv_ref, q_ref, kvc_ref, kvp_ref, gv_ref, w_ref, b_ref, bucket_ref, dmix_ref, tanh_ref, prob_ref, psink_ref,
             dzm_ref, dkv_ref, dgv_ref, dw_ref, db_ref, dattn_ref,
             dbias_ref, carry_ref, dsink_acc, db_acc):
        i = pl.program_id(0)

        @pl.when(i == 0)
        def _():
            dbias_ref[...] = jnp.zeros_like(dbias_ref)
            carry_ref[...] = jnp.zeros_like(carry_ref)
            dsink_acc[...] = jnp.zeros_like(dsink_acc)
            dgv_ref[...] = jnp.zeros_like(dgv_ref)
            dw_ref[...] = jnp.zeros_like(dw_ref)
            db_acc[...] = jnp.zeros_like(db_acc)

        @pl.when(i < nb)
        def _():
            u, du_dz = _gelu_and_grad(zuv_ref[:, :D_GMLP], tanh_ref[:, :D_GMLP])
            vg, dvg_dz = _gelu_and_grad(zuv_ref[:, D_GMLP:], tanh_ref[:, D_GMLP:])
            for g in range(GROUPS):
                sl = slice(g * GDIM, (g + 1) * GDIM)
                vg_g = vg[:, sl]
                rg = _rms_scale(vg_g)
                vhat = vg_g * rg
                gain = gv_ref[:, sl]
                vn = (vhat * gain).astype(BF)
                w_g = _tril_bf16(w_ref, g)
                sv = _nn(w_g, vn) + b_ref[g]
                dmix_a = dmix_ref[:, sl]
                dsv = dmix_a * u[:, sl]
                dsvb = dsv.astype(BF)
                db_acc[g] += jnp.sum(dsv, axis=-1, keepdims=True)
                dw_ref[g] += _nt(dsvb, vn)
                dvn = _tn(w_g, dsvb)
                dgv_ref[:, sl] += jnp.sum(dvn * vhat, axis=0, keepdims=True)
                dvg = _rms_bwd(dvn * gain, vg_g, rg)
                dzm_ref[:, sl] = ((dmix_a * sv) * du_dz[:, sl]).astype(BF)
                dzm_ref[:, D_GMLP + g * GDIM: D_GMLP + (g + 1) * GDIM] = (dvg * dvg_dz[:, sl]).astype(BF)

            for kh in range(N_KV):
                ksl = slice(kh * HEAD_DIM, (kh + 1) * HEAD_DIM)
                vsl = slice(D_KV + kh * HEAD_DIM, D_KV + (kh + 1) * HEAD_DIM)
                k_prev, k_cur = kvp_ref[:, ksl], kvc_ref[:, ksl]
                v_prev, v_cur = kvp_ref[:, vsl], kvc_ref[:, vsl]
                q4 = _stack_heads(q_ref, kh, 0)
                p_prev, p_cur, p_sink = prob_ref[0, kh, :, :BLOCK], prob_ref[0, kh, :, BLOCK:], psink_ref[0, kh, :, 0:1]
                do4 = _stack_heads(dmix_ref, kh, D_GMLP)
                dp_prev = _nt(do4, v_prev)
                dp_cur = _nt(do4, v_cur)
                delta = jnp.sum(p_prev * dp_prev, axis=-1, keepdims=True) + jnp.sum(p_cur * dp_cur, axis=-1, keepdims=True)
                ds_prev = p_prev * (dp_prev - delta)
                ds_cur = p_cur * (dp_cur - delta)
                dsink_acc[kh] -= p_sink * delta
                dbias_ref[kh, :, :BLOCK] += ds_prev
                dbias_ref[kh, :, BLOCK:] += ds_cur
                dsb_prev = ds_prev.astype(BF)
                dsb_cur = ds_cur.astype(BF)
                dq4 = (_nn(dsb_prev, k_prev) + _nn(dsb_cur, k_cur)) * SCALE
                for g in range(Q_PER_KV):
                    first = 2 * D_GMLP + (kh * Q_PER_KV + g) * HEAD_DIM
                    dzm_ref[:, first:first + HEAD_DIM] = dq4[g * BLOCK:(g + 1) * BLOCK].astype(BF)
                dkv_ref[:, ksl] = (carry_ref[:, ksl] + _tn(dsb_prev, q4) * SCALE).astype(BF)
                dkv_ref[:, vsl] = (carry_ref[:, vsl] + _tn(p_prev.astype(BF), do4)).astype(BF)
                carry_ref[:, ksl] = _tn(dsb_cur, q4) * SCALE
                carry_ref[:, vsl] = _tn(p_cur.astype(BF), do4)

        @pl.when(i == nb)
        def _():
            dkv_ref[...] = carry_ref[...].astype(BF)
            row = lax.broadcasted_iota(jnp.int32, (BLOCK, BLOCK), 0)
            col = lax.broadcasted_iota(jnp.int32, (BLOCK, BLOCK), 1)
            for g in range(GROUPS):
                dw_ref[g] = jnp.where(col <= row, dw_ref[g], 0.0)
                db_ref[g:g + 1, :] = jnp.sum(jnp.where(col == row, db_acc[g], 0.0), axis=0, keepdims=True)
            bucket = bucket_ref[...]
            for b in range(N_Q, REL_BUCKETS):
                dattn_ref[N_Q, b] = 0.0
            for h in range(N_Q):
                kh, rows = _head_rows(h)
                dattn_ref[N_Q, h] = jnp.sum(dsink_acc[kh, rows, :])
                dbh = dbias_ref[kh, rows, :]
                for b in range(REL_BUCKETS):
                    dattn_ref[h, b] = jnp.sum(jnp.where(bucket == b, dbh, 0.0))

    cl = lambda i: jnp.minimum(i, nb - 1)
    const = lambda shape: pl.BlockSpec(shape, lambda i: (0,) * len(shape))
    return _call(
        body,
        name="mixer_bwd",
        after=after,
        grid=(nb + 1,),
        in_specs=_mixer_specs(nb)[:7] + [
            const((BLOCK, 2 * BLOCK)),
            pl.BlockSpec((BLOCK, D_MODEL), lambda i: (cl(i), 0)),
            pl.BlockSpec((BLOCK, 2 * D_GMLP), lambda i: (cl(i), 0)),
            pl.BlockSpec((1, N_KV, ROWS4, 2 * BLOCK), lambda i: (cl(i), 0, 0, 0)),
            pl.BlockSpec((1, N_KV, ROWS4, LANES), lambda i: (cl(i), 0, 0, 0)),
        ],
        out_specs=[
            pl.BlockSpec((BLOCK, D_MAIN), lambda i: (cl(i), 0)),
            pl.BlockSpec((BLOCK, 2 * D_KV), lambda i: (jnp.maximum(i - 1, 0), 0)),
            const((1, D_GMLP)),
            const((GROUPS, BLOCK, BLOCK)),
            const((GROUPS, BLOCK)),
            pl.BlockSpec(memory_space=pltpu.SMEM),
        ],
        out_shape=[
            jax.ShapeDtypeStruct((s, D_MAIN), BF),
            jax.ShapeDtypeStruct((s, 2 * D_KV), BF),
            jax.ShapeDtypeStruct((1, D_GMLP), F32),
            jax.ShapeDtypeStruct((GROUPS, BLOCK, BLOCK), F32),
            jax.ShapeDtypeStruct((GROUPS, BLOCK), F32),
            jax.ShapeDtypeStruct((N_Q + 1, REL_BUCKETS), F32),
        ],
        scratch_shapes=[
            pltpu.VMEM((N_KV, ROWS4, 2 * BLOCK), F32),
            pltpu.VMEM((BLOCK, 2 * D_KV), F32),
            pltpu.VMEM((N_KV, ROWS4, 1), F32),
            pltpu.VMEM((GROUPS, BLOCK, 1), F32),
        ],
        compiler_params=_params(("arbitrary",)),
    )(zuv, qkv, qkv, qkv, gv, w_sp, b_sp, bucket, dmix, *saved)


def _in_bwd(x, dh1, dzm, dkv, g1, w_in_t, after=None):
    s = x.shape[0]
    tm = min(IO_TOK_TILE, s)

    def body(x_ref, dh1_ref, dzm_ref, dkv_ref, g_ref, w_ref, dx_ref, dg_ref):
        @pl.when(pl.program_id(0) == 0)
        def _():
            dg_ref[...] = jnp.zeros_like(dg_ref)

        dhn = _nn(dzm_ref[...], w_ref[:D_MAIN, :]) + _nn(dkv_ref[...], w_ref[D_MAIN:, :])
        xv = x_ref[...]
        r = _rms_scale(xv)
        dg_ref[...] += jnp.sum(dhn * (xv * r), axis=0, keepdims=True)
        dx_ref[...] = dh1_ref[...] + _rms_bwd(dhn * g_ref[...], xv, r)

    tile = lambda cols: pl.BlockSpec((tm, cols), lambda i: (i, 0))
    row = pl.BlockSpec((1, D_MODEL), lambda i: (0, 0))
    return _call(
        body,
        name="in_bwd",
        after=after,
        grid=(s // tm,),
        in_specs=[tile(D_MODEL), tile(D_MODEL), tile(D_MAIN), tile(2 * D_KV), row, pl.BlockSpec((D_IN, D_MODEL), lambda i: (0, 0))],
        out_specs=[tile(D_MODEL), row],
        out_shape=[jax.ShapeDtypeStruct((s, D_MODEL), F32), jax.ShapeDtypeStruct((1, D_MODEL), F32)],
        compiler_params=_params(("arbitrary",)),
    )(x, dh1, dzm, dkv, g1, w_in_t)


def _wgrad_in(dzm, dkv, hn, after=None):
    s = hn.shape[0]
    tm = 2 * D_KV
    n_main = D_MAIN // tm

    def body(dzm_ref, dkv_ref, hn_ref, o_ref):
        i = pl.program_id(0)

        @pl.when(i < n_main)
        def _():
            o_ref[...] = _tn(dzm_ref[...], hn_ref[...])

        @pl.when(i == n_main)
        def _():
            o_ref[...] = _tn(dkv_ref[...], hn_ref[...])

    return _call(
        body,
        name="wgrad_in",
        after=after,
        grid=(n_main + 1,),
        in_specs=[
            pl.BlockSpec((s, tm), lambda i: (0, jnp.minimum(i, n_main - 1))),
            pl.BlockSpec((s, tm), lambda i: (0, 0)),
            pl.BlockSpec((s, D_MODEL), lambda i: (0, 0)),
        ],
        out_specs=pl.BlockSpec((tm, D_MODEL), lambda i: (i, 0)),
        out_shape=jax.ShapeDtypeStruct((D_IN, D_MODEL), F32),
        compiler_params=_params(("arbitrary",)),
    )(dzm, dkv, hn)


def _wgrad(a, b, tm, tn, name, peer_cols=0, after=None):
    s, m = a.shape
    n = b.shape[1]

    def body(a_ref, b_ref, o_ref, at_ref):
        @pl.when(pl.program_id(1) == 0)
        def _():
            at_ref[...] = a_ref[...].astype(BF).T

        r = _nn(at_ref[...], b_ref[...])
        if peer_cols:
            for q in range(tn // peer_cols):
                o_ref[q] = r[:, q * peer_cols:(q + 1) * peer_cols]
        else:
            o_ref[...] = r

    if peer_cols:
        out_spec = pl.BlockSpec((tn // peer_cols, tm, peer_cols), lambda i, j: (j, i, 0))
        out_shape = jax.ShapeDtypeStruct((n // peer_cols, m, peer_cols), F32)
    else:
        out_spec = pl.BlockSpec((tm, tn), lambda i, j: (i, j))
        out_shape = jax.ShapeDtypeStruct((m, n), F32)
    return _call(
        body,
        name=name,
        after=after,
        grid=(m // tm, n // tn),
        in_specs=[pl.BlockSpec((s, tm), lambda i, j: (0, i)), pl.BlockSpec((s, tn), lambda i, j: (0, j))],
        out_specs=out_spec,
        out_shape=out_shape,
        scratch_shapes=[pltpu.VMEM((tm, s), BF)],
        compiler_params=_params(("arbitrary", "arbitrary")),
    )(a, b)


def _adamw_math(w, g, m, v):
    m_new = ADAM_B1 * m + (1.0 - ADAM_B1) * g
    v_new = ADAM_B2 * v + (1.0 - ADAM_B2) * jnp.square(g)
    m_hat = m_new / (1.0 - ADAM_B1 ** ADAM_STEP)
    v_hat = v_new / (1.0 - ADAM_B2 ** ADAM_STEP)
    delta = -ADAM_LR * (m_hat / (jnp.sqrt(v_hat) + ADAM_EPS) + ADAM_WD * w)
    return delta, m_new, v_new


def _final_adamw(part, recv, w, m, v, name):
    r, c = w.shape
    tr = min(r, 512)

    def body(p_ref, r_ref, w_ref, m_ref, v_ref, g_ref, d_ref, mo_ref, vo_ref):
        g = p_ref[...]
        for j in range(3):
            g = g + r_ref[j].astype(F32)
        g_ref[...] = g
        d_ref[...], mo_ref[...], vo_ref[...] = _adamw_math(w_ref[...], g, m_ref[...], v_ref[...])

    spec = pl.BlockSpec((tr, c), lambda i: (i, 0))
    return _call(
        body,
        name=name,
        grid=(r // tr,),
        in_specs=[spec, pl.BlockSpec((3, tr, c), lambda i: (0, i, 0)), spec, spec, spec],
        out_specs=[spec] * 4,
        out_shape=[jax.ShapeDtypeStruct((r, c), F32)] * 4,
        compiler_params=_params(("arbitrary",)),
    )(part, recv, w, m, v)


def _rs_sum(g, land, blocks, name):
    _, r, c = g.shape
    tr = min(r, 256)

    def body(blk_ref, g0_ref, g1_ref, g2_ref, g3_ref, l_ref, part_ref, send_ref):
        part_ref[...] = g0_ref[0] + l_ref[0]
        for j, gj_ref in enumerate((g1_ref, g2_ref, g3_ref)):
            send_ref[j] = (gj_ref[0] + l_ref[j + 1]).astype(BF)

    def pick(j):
        return pl.BlockSpec((1, tr, c), lambda i, blk: (blk[j], i, 0))

    return _call(
        body,
        name=name,
        grid_spec=pltpu.PrefetchScalarGridSpec(
            num_scalar_prefetch=1,
            grid=(r // tr,),
            in_specs=[pick(0), pick(1), pick(2), pick(3), pl.BlockSpec((4, tr, c), lambda i, blk: (0, i, 0))],
            out_specs=[pl.BlockSpec((tr, c), lambda i, blk: (i, 0)), pl.BlockSpec((3, tr, c), lambda i, blk: (0, i, 0))],
        ),
        out_shape=[jax.ShapeDtypeStruct((r, c), F32), jax.ShapeDtypeStruct((3, r, c), BF)],
        compiler_params=_params(("arbitrary",)),
    )(blocks, g, g, g, g, land)


def _final_adamw_few(parts, recvs, ws, ms, vs, name):
    n = len(ws)

    def body(*refs):
        ins, outs = refs[:5 * n], refs[5 * n:]
        for k in range(n):
            p_ref, r_ref, w_ref, m_ref, v_ref = (ins[q * n + k] for q in range(5))
            g = p_ref[...]
            for j in range(3):
                g = g + r_ref[j].astype(F32)
            outs[k][...] = g
            outs[n + k][...], outs[2 * n + k][...], outs[3 * n + k][...] = _adamw_math(w_ref[...], g, m_ref[...], v_ref[...])

    outs = _call(
        body,
        name=name,
        out_shape=[jax.ShapeDtypeStruct(w.shape, F32) for w in ws] * 4,
        compiler_params=_params(),
    )(*parts, *recvs, *ws, *ms, *vs)
    return outs[:n], outs[n:2 * n], outs[2 * n:3 * n], outs[3 * n:]


def _rs_sum_few(gs, lands, blocks, name):
    n = len(gs)

    def body(blk_ref, *refs):
        g_refs, l_refs, part_refs, send_refs = refs[:n], refs[n:2 * n], refs[2 * n:3 * n], refs[3 * n:]
        for k in range(n):
            part_refs[k][...] = g_refs[k][blk_ref[0]] + l_refs[k][0]
            for j in range(3):
                send_refs[k][j] = (g_refs[k][blk_ref[j + 1]] + l_refs[k][j + 1]).astype(BF)

    outs = _call(
        body,
        name=name,
        in_specs=[pl.BlockSpec(memory_space=pltpu.SMEM)] + [pl.BlockSpec(memory_space=pltpu.VMEM)] * (2 * n),
        out_shape=[jax.ShapeDtypeStruct(g.shape[1:], F32) for g in gs] + [jax.ShapeDtypeStruct((3,) + g.shape[1:], BF) for g in gs],
        compiler_params=_params(),
    )(blocks, *gs, *lands)
    return [(outs[k], outs[n + k]) for k in range(n)]


def _adamw_small(own, recv, weights, moms, vels, after):
    n_w = len(weights)

    def body(*refs):
        own_refs, recv_refs = refs[:6], refs[6:12]
        w_refs, m_refs, v_refs = refs[12:12 + n_w], refs[12 + n_w:12 + 2 * n_w], refs[12 + 2 * n_w:12 + 3 * n_w]
        outs = refs[12 + 3 * n_w:]
        loss_ref, g_refs, d_refs = outs[0], outs[1:1 + n_w], outs[1 + n_w:1 + 2 * n_w]
        mo_refs, vo_refs = outs[1 + 2 * n_w:1 + 3 * n_w], outs[1 + 3 * n_w:]
        x, y = lax.axis_index("x"), lax.axis_index("y")

        def total(k, *index):
            index = index or (slice(None),) * (len(own_refs[k].shape) - 1)
            across = [own_refs[k][(0, *index)]] + [recv_refs[k][(j, *index)] for j in range(3)]
            acc = None
            for cx in range(2):
                for cy in range(2):
                    flip = (cx ^ x) + 2 * (cy ^ y)
                    part = jnp.where(flip == 0, across[0], jnp.where(flip == 1, across[1], jnp.where(flip == 2, across[2], across[3])))
                    acc = part if acc is None else acc + part
            return acc

        grads = [
            total(1), total(2), total(3), total(4),
            total(5, slice(N_Q, None), slice(0, N_Q)), total(5, slice(0, N_Q), slice(None)),
            total(0, slice(0, 1), slice(None)), total(0, slice(1, 2), slice(None))]
        loss_ref[...] = total(0, slice(2, 3), slice(0, 1))
        for k in range(n_w):
            g_refs[k][...] = grads[k]
            d_refs[k][...], mo_refs[k][...], vo_refs[k][...] = _adamw_math(w_refs[k][...], grads[k], m_refs[k][...], v_refs[k][...])

    shapes = [jax.ShapeDtypeStruct(w.shape, F32) for w in weights]
    outs = _call(
        body,
        name="adamw_small",
        after=after,
        in_specs=[pl.BlockSpec(memory_space=pltpu.VMEM)] * (12 + 3 * n_w),
        out_shape=[jax.ShapeDtypeStruct((1, 1), F32)] + shapes * 4,
        compiler_params=_params(),
    )(*own, *recv, *weights, *moms, *vels)
    return outs[0], outs[1:1 + n_w], outs[1 + n_w:1 + 2 * n_w], outs[1 + 2 * n_w:1 + 3 * n_w], outs[1 + 3 * n_w:]


def _place():
    x, y, c = lax.axis_index("x"), lax.axis_index("y"), lax.axis_index("c")
    return x, y, c, [(1 - x, y), (x, 1 - y), (1 - x, 1 - y)]


def _dev_index(px, py, pc):
    return 4 * px + 2 * py + pc


HBM_SPEC = pl.BlockSpec(memory_space=pltpu.HBM)
SEM_SPEC = pl.BlockSpec(memory_space=pltpu.SEMAPHORE)
ANY_SPEC = pl.BlockSpec(memory_space=pl.ANY)
DATAFLOW = pltpu.SideEffectType.DATAFLOW_SIDE_EFFECTING


def _hbm(a):
    return pltpu.with_memory_space_constraint(a, pltpu.HBM)


def _prep_weights(shards):
    k_n = len(shards)

    def body(*refs):
        ins, outs, stage, sems = refs[:k_n], refs[k_n:2 * k_n], refs[2 * k_n:3 * k_n], refs[3 * k_n]
        x, y, c, _ = _place()
        copies = []
        for k in range(k_n):
            stage[k][...] = ins[k][...].astype(BF)
            copies.append(pltpu.make_async_copy(stage[k], outs[k].at[_dev_index(x, y, c)], sems.at[k]))
            copies[k].start()
        for cp in copies:
            cp.wait()

    return _call(
        body,
        name="prep_weights",
        in_specs=[pl.BlockSpec(memory_space=pltpu.VMEM)] * k_n,
        out_specs=[ANY_SPEC] * k_n,
        out_shape=[jax.ShapeDtypeStruct((N_DEV,) + sh.shape, BF) for sh in shards],
        scratch_shapes=[pltpu.VMEM(sh.shape, BF) for sh in shards] + [pltpu.SemaphoreType.DMA((k_n,))],
        compiler_params=_params(),
    )(*shards)


def _ag_piece(land_k, block, half, peer, send_sem, recv_sem):
    ref = land_k.at[_dev_index(*block)]
    if half is not None:
        rows = land_k.shape[1] // 2
        ref = ref.at[pl.ds(half * rows, rows)]
    return pltpu.make_async_remote_copy(
        src_ref=ref, dst_ref=ref, send_sem=send_sem, recv_sem=recv_sem, device_id=peer, device_id_type=MESH)


def _ag_plan():
    x, y, c, _ = _place()
    me, sib = (x, y, c), (x, y, 1 - c)
    xn, yn, diag = (1 - x, y, c), (x, 1 - y, c), (1 - x, 1 - y, c)
    return dict(
        relay_halves=[(me, 0, xn), (me, 1, yn)],
        others=[(me, None, sib), (me, 1, xn), (me, 0, yn)],
        relays=[(xn, 0, yn), (yn, 1, xn)],
        near=[(xn, None, sib), (yn, None, sib)],
        far=[(diag, None, sib)],
    )


def _ag_stage(land, stage, send_sems, recv_sems, act):
    copies = _ag_plan()[stage]
    n = len(copies)
    for k in range(len(land)):
        for j, (block, half, peer) in enumerate(copies):
            cp = _ag_piece(land[k], block, half, peer, send_sems.at[n * k + j], recv_sems.at[n * k + j])
            if act == "start":
                cp.start()
            else:
                cp.wait_send()
                cp.wait_recv()


def _sem_shapes(*counts):
    return [pltpu.SemaphoreType.DMA((n,)) for n in counts for _ in range(2)]


def _ag_start(first, rest):
    lands = list(first) + list(rest)
    k_n, k_first = len(lands), len(first)
    k_rest = k_n - k_first

    def body(*refs):
        land = refs[:k_n]
        sems = refs[k_n:k_n + 6]
        token = refs[-1]
        x, y, c, chips = _place()
        targets = [(x, y, 1 - c)] + [(*chip, c) for chip in chips]
        for k in range(k_first):
            for j, to in enumerate(targets):
                _ag_piece(land[k], (x, y, c), None, to, sems[0].at[4 * k + j], sems[1].at[4 * k + j]).start()
        _ag_stage(land[k_first:], "relay_halves", sems[2], sems[3], "start")
        _ag_stage(land[k_first:], "others", sems[4], sems[5], "start")
        token[...] = jnp.zeros_like(token)

    outs = pl.pallas_call(
        body,
        name="ag_start",
        in_specs=[HBM_SPEC] * k_n,
        out_specs=(*[SEM_SPEC] * 6, *[HBM_SPEC] * k_n, pl.BlockSpec(memory_space=pltpu.VMEM)),
        out_shape=(*_sem_shapes(4 * k_first, 2 * k_rest, 3 * k_rest),
                   *[pltpu.HBM(a.shape, a.dtype) for a in lands], jax.ShapeDtypeStruct((8, LANES), F32)),
        input_output_aliases={k: 6 + k for k in range(k_n)},
        compiler_params=pltpu.CompilerParams(has_side_effects=DATAFLOW),
    )(*[_hbm(a) for a in lands])
    flying = list(outs[6:6 + k_n])
    return (outs[0], outs[1], flying[:k_first]), (outs[2:6], flying[k_first:]), outs[-1]


def _ag_split_call(lands, waits, starts, after, name):
    k_n = len(lands)
    plan_sizes = dict(relay_halves=2, others=3, relays=2, near=2, far=1)
    n_in, n_out = 2 * len(waits), 2 * len(starts)

    def body(*refs):
        land = refs[:k_n]
        in_sems = refs[k_n:k_n + n_in]
        out_sems, token = refs[len(refs) - 1 - n_out:len(refs) - 1], refs[-1]
        for w, (stage, _, _) in enumerate(waits):
            _ag_stage(land, stage, in_sems[2 * w], in_sems[2 * w + 1], "wait")
            if w < len(starts):
                _ag_stage(land, starts[w], out_sems[2 * w], out_sems[2 * w + 1], "start")
        token[...] = jnp.zeros_like(token)

    outs = pl.pallas_call(
        body,
        name=name,
        in_specs=[HBM_SPEC] * k_n + [SEM_SPEC] * n_in + [ANY_SPEC],
        out_specs=(*[HBM_SPEC] * k_n, *[SEM_SPEC] * n_out, pl.BlockSpec(memory_space=pltpu.VMEM)),
        out_shape=(*[pltpu.HBM(a.shape, a.dtype) for a in lands], *_sem_shapes(*[plan_sizes[s] * k_n for s in starts]),
                   jax.ShapeDtypeStruct((8, LANES), F32)),
        input_output_aliases={k: k for k in range(k_n)},
        compiler_params=pltpu.CompilerParams(has_side_effects=DATAFLOW),
    )(*lands, *[s for _, a, b in waits for s in (a, b)], after)
    return list(outs[:k_n]), list(outs[k_n:k_n + n_out]), outs[-1]


def _ag_mid(lands, send_sems, recv_sems, after, name):
    k_n = len(lands)

    def body(*refs):
        land = refs[:k_n]
        send1, recv1 = refs[k_n], refs[k_n + 1]
        fwd_send, fwd_recv = refs[-2], refs[-1]
        x, y, c, chips = _place()
        sources = [(x, y, 1 - c)] + [(*chip, c) for chip in chips]
        for k in range(k_n):
            mine = land[k].at[_dev_index(x, y, c)]
            for j, frm in enumerate(sources):
                got = land[k].at[_dev_index(*frm)]
                cp = pltpu.make_async_remote_copy(
                    src_ref=mine, dst_ref=got, send_sem=send1.at[4 * k + j], recv_sem=recv1.at[4 * k + j], device_id=frm, device_id_type=MESH)
                cp.wait_send()
                cp.wait_recv()
                if j >= 1:
                    pltpu.make_async_remote_copy(
                        src_ref=got, dst_ref=got, send_sem=fwd_send.at[3 * k + j - 1], recv_sem=fwd_recv.at[3 * k + j - 1],
                        device_id=(x, y, 1 - c), device_id_type=MESH).start()

    outs = pl.pallas_call(
        body,
        name=name,
        in_specs=[HBM_SPEC] * k_n + [SEM_SPEC, SEM_SPEC, ANY_SPEC],
        out_specs=(*[HBM_SPEC] * k_n, SEM_SPEC, SEM_SPEC),
        out_shape=(*[pltpu.HBM(a.shape, a.dtype) for a in lands], pltpu.SemaphoreType.DMA((3 * k_n,)), pltpu.SemaphoreType.DMA((3 * k_n,))),
        input_output_aliases={k: k for k in range(k_n)},
        compiler_params=pltpu.CompilerParams(has_side_effects=DATAFLOW),
    )(*lands, send_sems, recv_sems, after)
    return list(outs[:k_n]), outs[-2], outs[-1]


def _ag_end(lands, fwd_send, fwd_recv, name):
    k_n = len(lands)

    def body(*refs):
        land = refs[:k_n]
        fsend, frecv = refs[k_n], refs[k_n + 1]
        x, y, c, chips = _place()
        for k in range(k_n):
            for j, chip in enumerate(chips):
                cp = pltpu.make_async_remote_copy(
                    src_ref=land[k].at[_dev_index(*chip, c)], dst_ref=land[k].at[_dev_index(*chip, 1 - c)],
                    send_sem=fsend.at[3 * k + j], recv_sem=frecv.at[3 * k + j], device_id=(x, y, 1 - c), device_id_type=MESH)
                cp.wait_send()
                cp.wait_recv()

    outs = pl.pallas_call(
        body,
        name=name,
        in_specs=[HBM_SPEC] * k_n + [SEM_SPEC, SEM_SPEC],
        out_specs=tuple([HBM_SPEC] * k_n),
        out_shape=tuple(pltpu.HBM(a.shape, a.dtype) for a in lands),
        input_output_aliases={k: k for k in range(k_n)},
        compiler_params=pltpu.CompilerParams(has_side_effects=DATAFLOW),
    )(*lands, fwd_send, fwd_recv)
    return list(outs)


def _chips4():
    x, y, c, others = _place()
    return x, y, c, [(x, y)] + others


def _route_sibling(j):
    x, y, c, chips = _chips4()
    return _dev_index(*chips[j], 1 - c), j, (x, y, 1 - c)


def _route_chips(j):
    x, y, c, chips = _chips4()
    return j, j, (*chips[j + 1], c)


def _route_sibling_whole(j):
    x, y, c, _ = _chips4()
    return 0, 0, (x, y, 1 - c)


def _route_chips_whole(j):
    x, y, c, chips = _chips4()
    return 0, j, (*chips[j + 1], c)


def _xchg_copies(routes, src, dst, send_sems, recv_sems):
    copies, sem = [], 0
    for k, (route, n) in enumerate(routes):
        for j in range(n):
            si, di, peer = route(j)
            copies.append(pltpu.make_async_remote_copy(
                src_ref=src[k].at[si], dst_ref=dst[k].at[di], send_sem=send_sems.at[sem], recv_sem=recv_sems.at[sem],
                device_id=peer, device_id_type=MESH))
            sem += 1
    return copies


def _xchg_start(srcs, slot_shapes, routes, name):
    k_n = len(srcs)
    n_sem = sum(n for _, n in routes)
    dsts = [lax.empty((n,) + tuple(sh), a.dtype) for sh, a, (_, n) in zip(slot_shapes, srcs, routes)]

    def body(*refs):
        src, dst = refs[:k_n], refs[k_n:2 * k_n]
        send_sems, recv_sems, token = refs[2 * k_n], refs[2 * k_n + 1], refs[-1]
        for cp in _xchg_copies(routes, src, dst, send_sems, recv_sems):
            cp.start()
        token[...] = jnp.zeros_like(token)

    arrays = list(srcs) + dsts
    outs = pl.pallas_call(
        body,
        name=name,
        in_specs=[HBM_SPEC] * (2 * k_n),
        out_specs=(SEM_SPEC, SEM_SPEC, *[HBM_SPEC] * (2 * k_n), pl.BlockSpec(memory_space=pltpu.VMEM)),
        out_shape=(pltpu.SemaphoreType.DMA((n_sem,)), pltpu.SemaphoreType.DMA((n_sem,)),
                   *[pltpu.HBM(a.shape, a.dtype) for a in arrays], jax.ShapeDtypeStruct((8, LANES), F32)),
        input_output_aliases={i: 2 + i for i in range(2 * k_n)},
        compiler_params=pltpu.CompilerParams(has_side_effects=DATAFLOW),
    )(*[_hbm(a) for a in arrays])
    return outs[0], outs[1], list(outs[2:2 + k_n]), list(outs[2 + k_n:2 + 2 * k_n]), outs[-1]


def _xchg_wait(send_sems, recv_sems, srcs, dsts, routes, after, name):
    k_n = len(srcs)

    def body(*refs):
        src, dst = refs[:k_n], refs[k_n:2 * k_n]
        for cp in _xchg_copies(routes, src, dst, refs[2 * k_n], refs[2 * k_n + 1]):
            cp.wait_send()
            cp.wait_recv()

    arrays = list(srcs) + list(dsts)
    outs = pl.pallas_call(
        body,
        name=name,
        in_specs=[HBM_SPEC] * (2 * k_n) + [SEM_SPEC, SEM_SPEC, ANY_SPEC],
        out_specs=tuple([HBM_SPEC] * (2 * k_n)),
        out_shape=tuple(pltpu.HBM(a.shape, a.dtype) for a in arrays),
        input_output_aliases={i: i for i in range(2 * k_n)},
        compiler_params=pltpu.CompilerParams(has_side_effects=DATAFLOW),
    )(*arrays, send_sems, recv_sems, after)
    return list(outs[:k_n]), list(outs[k_n:])


SMALL = ("norm1_gain", "gmlp_v_gain", "w_spatial", "b_spatial", "attn_sinks", "rel_bias_table", "norm2_gain", "final_gain")
LANES = 128


def _swap_start(grads, smalls, tag):
    srcs = list(grads) + [a[None] for a in smalls]
    shapes = [g.shape[1:] for g in grads] + [a.shape for a in smalls]
    routes = [(_route_sibling, 4)] * len(grads) + [(_route_sibling_whole, 1)] * len(smalls)
    return _xchg_start(srcs, shapes, routes, f"rs_{tag}_swap_start"), routes, len(grads)


def _swap_sums(swap, names, after, tag, few=False):
    (send1, recv1, src1, land1, _), routes, n_rs = swap
    x, y, c, chips = _chips4()
    blocks = jnp.stack([_dev_index(*chip, c) for chip in chips]).astype(jnp.int32)
    src1, land1 = _xchg_wait(send1, recv1, src1, land1, routes, after, f"rs_{tag}_swap_wait")
    if few:
        sums = _rs_sum_few(src1[:n_rs], land1[:n_rs], blocks, f"rs_{tag}_sum")
    else:
        sums = [_rs_sum(g, land, blocks, f"rs_sum_{n}") for g, land, n in zip(src1[:n_rs], land1[:n_rs], names)]
    return sums, (_pair_sum(src1[n_rs:], land1[n_rs:], f"rs_{tag}_sum_small") if len(src1) > n_rs else [])


def _pair_sum(mine, theirs, name):
    def body(*refs):
        n = len(refs) // 3
        for k in range(n):
            refs[2 * n + k][...] = refs[k][...] + refs[n + k][...]

    return _call(
        body,
        name=name,
        out_shape=[jax.ShapeDtypeStruct(a.shape, F32) for a in mine],
        compiler_params=_params(),
    )(*mine, *theirs)


def _chips_start(sums, small_sums, tag):
    sends = [ps[1] for ps in sums] + list(small_sums)
    routes = [(_route_chips, 3)] * len(sums) + [(_route_chips_whole, 3)] * len(small_sums)
    return _xchg_start(sends, [a.shape[1:] for a in sends], routes, f"rs_{tag}_chips_start"), routes


def _chips_wait(chips, after, tag):
    (send2, recv2, src2, land2, _), routes = chips
    return _xchg_wait(send2, recv2, src2, land2, routes, after, f"rs_{tag}_chips_wait")


def kernel(x, p, norm1_gain, w_in, gmlp_v_gain, w_spatial, b_spatial, attn_sinks, rel_bias_table, w_out, norm2_gain, w_ff1, w_ff2, w_ple_proj, w_ple_gate, final_gain, loss_target, m_norm1_gain, m_w_in, m_gmlp_v_gain, m_w_spatial, m_b_spatial, m_attn_sinks, m_rel_bias_table, m_w_out, m_norm2_gain, m_w_ff1, m_w_ff2, m_w_ple_proj, m_w_ple_gate, m_final_gain, v_norm1_gain, v_w_in, v_gmlp_v_gain, v_w_spatial, v_b_spatial, v_attn_sinks, v_rel_bias_table, v_w_out, v_norm2_gain, v_w_ff1, v_w_ff2, v_w_ple_proj, v_w_ple_gate, v_final_gain):
    args = dict(locals())
    s = x.shape[1]
    big = ("w_in", "w_out", "w_ff1", "w_ff2", "w_ple_proj", "w_ple_gate")

    x2, p2, t2 = x.reshape(s, D_MODEL), p.reshape(s, PLE_DIM), loss_target.reshape(s, D_MODEL)
    g1, gv, w_sp, b_sp, sinks, table, g2, gf = (args[n] for n in SMALL)
    bucket = jnp.asarray(_bucket_table())
    b_col = b_sp.reshape(GROUPS, BLOCK, 1)

    def shard(name):
        return args[name][0].T if name.endswith("w_in") else args[name][0]

    lands = _prep_weights([shard(n) for n in big])
    (send_in, recv_in, fly_in), (rest_sems, fly_rest), token = _ag_start(lands[:1], lands[1:])
    mid_in, fwd_send_in, fwd_recv_in = _ag_mid(fly_in, send_in, recv_in, token, "ag_mid_w_in")
    g_in = _ag_end(mid_in, fwd_send_in, fwd_recv_in, "ag_end_w_in")[0]
    full_in = g_in.reshape(D_IN, D_MODEL)

    zuv, qkv, hn1 = _in_proj(x2, g1, full_in)
    fly_rest, relay_sems, relayed = _ag_split_call(
        fly_rest, [("relay_halves", *rest_sems[:2])], ["relays"], zuv, "ag_relay")
    mix, *saved = _mixer_fwd(zuv, qkv, gv, w_sp[0], b_col, sinks, table.T, bucket, after=relayed)
    fly_rest, fwd_sems, _ = _ag_split_call(
        fly_rest, [("others", *rest_sems[2:]), ("relays", *relay_sems)], ["near", "far"], mix, "ag_mid_rest")
    g_out, g_ff1, g_ff2, g_proj, g_gate = _ag_split_call(
        fly_rest, [("near", *fwd_sems[:2]), ("far", *fwd_sems[2:])], [], mix, "ag_end_rest")[0]
    full_out, full_ff2, full_gate = g_out.reshape(D_MODEL, D_MODEL), g_ff2.reshape(D_FF, D_MODEL), g_gate.reshape(D_MODEL, D_MODEL)
    full_proj = g_proj.transpose(1, 0, 2).reshape(PLE_DIM, D_MODEL)

    (tail_small, dh1, dh1b, dmix, hn2, a, df, h2, dh2, dgl, dpp) = _tail(
        x2, mix, p2, t2, g2, gf.reshape(1, D_MODEL), full_out, g_ff1, full_ff2, full_gate, full_proj)

    ffn_names, mid_names = ("w_ff1", "w_ff2"), ("w_out", "w_ple_proj", "w_ple_gate")
    gw_ff1 = _wgrad(hn2, df, D_MODEL, 2 * D_FF // N_DEV, "wgrad_ff1", peer_cols=D_FF // N_DEV)
    gw_ff2 = _wgrad(a, dh2, 1024, D_MODEL, "wgrad_ff2").reshape(N_DEV, D_FF // N_DEV, D_MODEL)
    ffn_swap = _swap_start([gw_ff1, gw_ff2], [], "ffn")
    gw_gate = _wgrad(h2, dgl, D_MODEL, 512, "wgrad_gate", after=ffn_swap[0][4]).reshape(N_DEV, D_MODEL // N_DEV, D_MODEL)
    gw_proj = _wgrad(p2, dpp, PLE_DIM, D_MODEL // 2, "wgrad_proj", peer_cols=D_MODEL // N_DEV, after=gw_gate)
    gw_out = _wgrad(mix, dh1b, D_MODEL, 512, "wgrad_out", after=gw_proj).reshape(N_DEV, D_MODEL // N_DEV, D_MODEL)
    mid_swap = _swap_start([gw_out, gw_proj, gw_gate], [tail_small], "mid")
    ffn_sums, _ = _swap_sums(ffn_swap, ffn_names, mid_swap[0][4], "ffn")
    ffn_chips = _chips_start(ffn_sums, [], "ffn")

    dzm, dkv, d_gv, d_wsp, d_bsp, d_attn = _mixer_bwd(
        zuv, qkv, dmix, saved, gv, w_sp[0], b_col, bucket, after=ffn_chips[0][4])
    mid_sums, mid_small = _swap_sums(mid_swap, mid_names, dzm, "mid", few=True)
    mid_chips = _chips_start(mid_sums, mid_small, "mid")
    gw_in = _wgrad_in(dzm, dkv, hn1, after=mid_chips[0][4]).reshape(N_DEV, D_IN // N_DEV, D_MODEL)
    dx, d_g1 = _in_bwd(x2, dh1, dzm, dkv, g1, full_in, after=gw_in)
    grad_x = dx.reshape(x.shape)
    in_swap = _swap_start([gw_in], [d_g1, d_gv, d_wsp, d_bsp, d_attn], "in")
    in_sums, in_small = _swap_sums(in_swap, ("w_in",), in_swap[0][4], "in")
    in_chips = _chips_start(in_sums, in_small, "in")

    grads, deltas, new_m, new_v = {}, {}, {}, {}
    after, small_own, small_recv = in_chips[0][4], [], []
    for chips, sums, names, tag in ((ffn_chips, ffn_sums, ffn_names, "ffn"), (mid_chips, mid_sums, mid_names, "mid"),
                                    (in_chips, in_sums, ("w_in",), "in")):
        sent, recvs = _chips_wait(chips, after, tag)
        small_own += sent[len(names):]
        small_recv += recvs[len(names):]
        if tag == "mid":
            done = _final_adamw_few([ps[0] for ps in sums], recvs[:len(names)], [shard(n) for n in names],
                                    [shard("m_" + n) for n in names], [shard("v_" + n) for n in names], "adamw_mid")
            done = list(zip(*done))
        else:
            done = [_final_adamw(part, recv, shard(n), shard("m_" + n), shard("v_" + n), "adamw_" + n)
                    for n, (part, _), recv in zip(names, sums, recvs)]
        for n, results in zip(names, done):
            for dst, arr in zip((grads, deltas, new_m, new_v), results):
                dst[n] = (arr.T if n == "w_in" else arr)[None]
            after = results[1]

    views = {"w_spatial": (GROUPS, BLOCK, BLOCK), "b_spatial": (GROUPS, BLOCK), "final_gain": (1, D_MODEL)}
    def view(name, base):
        return args[name].T if base == "rel_bias_table" else args[name].reshape(views.get(base, args[name].shape))

    small_in = [[view(pre + n, n) for n in SMALL] for pre in ("", "m_", "v_")]
    loss, *small_out = _adamw_small(small_own, small_recv, *small_in, after=after)
    for dst, arrays in zip((grads, deltas, new_m, new_v), small_out):
        for n, arr in zip(SMALL, arrays):
            dst[n] = arr.T if n == "rel_bias_table" else arr.reshape(args[n].shape)
    loss = loss[0, 0]

    order = ("norm1_gain", "w_in", "gmlp_v_gain", "w_spatial", "b_spatial", "attn_sinks", "rel_bias_table", "w_out",
             "norm2_gain", "w_ff1", "w_ff2", "w_ple_proj", "w_ple_gate", "final_gain")
    return (loss, grad_x, *[grads[n] for n in order], *[deltas[n] for n in order],
            *[new_m[n] for n in order], *[new_v[n] for n in order])
```

```python
import math

import numpy as np
import jax
import jax.numpy as jnp
from jax import lax
from jax.experimental import pallas as pl
from jax.experimental.pallas import tpu as pltpu

F32 = jnp.float32
BF = jnp.bfloat16
MESH = pl.DeviceIdType.MESH
N_DEV = 8

D_MODEL = 1024
PLE_DIM = 256
D_GMLP = 512
GROUPS = 4
GDIM = 128
BLOCK = 128
D_ATTN = 512
HEAD_DIM = 64
N_Q = 8
Q_PER_KV = 4
N_KV = N_Q // Q_PER_KV
ROWS4 = Q_PER_KV * BLOCK
D_KV = 128
D_FF = 4096
D_IN = 1792
D_MAIN = 2 * D_GMLP + D_ATTN
REL_BUCKETS = 32
EPS = 1e-6
NEG_INF = -1e30
SCALE = HEAD_DIM ** -0.5
GELU_C = math.sqrt(2.0 / math.pi)
GELU_A = 0.044715

ADAM_LR = 0.001
ADAM_B1 = 0.9
ADAM_B2 = 0.999
ADAM_EPS = 1e-08
ADAM_WD = 0.01
ADAM_STEP = 10

V7X_VMEM_LIMIT = 60000 * 1024
TOK_TILE = 256
IO_TOK_TILE = 512


def _call(body, after=None, **kw):
    if after is None:
        return pl.pallas_call(body, **kw)
    n_in = len(kw["in_specs"])

    def ordered(*refs):
        body(*refs[:n_in], *refs[n_in + 1:])

    kw["in_specs"] = list(kw["in_specs"]) + [pl.BlockSpec(memory_space=pl.ANY)]
    fn = pl.pallas_call(ordered, **kw)
    return lambda *operands: fn(*operands, after)


def _params(sem=None):
    if sem is None:
        return pltpu.CompilerParams(vmem_limit_bytes=V7X_VMEM_LIMIT)
    return pltpu.CompilerParams(dimension_semantics=sem, vmem_limit_bytes=V7X_VMEM_LIMIT)


def _nn(a, b):
    return jnp.dot(a, b, preferred_element_type=F32)


def _nt(a, b):
    return lax.dot_general(a, b, (((1,), (1,)), ((), ())), preferred_element_type=F32)


def _tn(a, b):
    return lax.dot_general(a, b, (((0,), (0,)), ((), ())), preferred_element_type=F32)


def _gelu_tanh(x):
    return jnp.tanh(GELU_C * (x + GELU_A * (x * x * x)))


def _gelu(x, t):
    return x * (0.5 * (1.0 + t))


def _gelu_and_grad(x, t):
    cdf = 0.5 * (1.0 + t)
    return x * cdf, cdf + 0.5 * x * (1.0 - t * t) * (GELU_C * (1.0 + 3.0 * GELU_A * (x * x)))


def _rms_scale(x):
    return lax.rsqrt(jnp.mean(x * x, axis=-1, keepdims=True) + EPS)


def _rms_bwd(dxn, x, r):
    return r * dxn - x * ((r * r * r) * jnp.mean(dxn * x, axis=-1, keepdims=True))


def _bucket_table():
    a = np.arange(BLOCK)[:, None]
    j = np.arange(2 * BLOCK)[None, :]
    n = BLOCK + a - j
    valid = (n >= 0) & (n < BLOCK)
    nc = np.maximum(n, 0)
    max_exact = REL_BUCKETS // 2
    nf = np.maximum(nc, 1).astype(np.float32)
    large = max_exact + (
        np.log(nf / np.float32(max_exact)) / np.float32(math.log(BLOCK / max_exact)) * np.float32(REL_BUCKETS - max_exact)
    ).astype(np.int32)
    large = np.minimum(large, REL_BUCKETS - 1)
    bucket = np.where(nc < max_exact, nc, large)
    return np.where(valid, bucket, -1).astype(np.int32)


def _in_proj(x, g1, w_in_t):
    s = x.shape[0]
    tm = min(IO_TOK_TILE, s)

    def body(x_ref, g_ref, w_ref, zuv_ref, qkv_ref, hn_ref):
        xv = x_ref[...]
        hn = ((xv * _rms_scale(xv)) * g_ref[...]).astype(BF)
        hn_ref[...] = hn
        z = _nt(hn, w_ref[...])
        zuv_ref[...] = z[:, : 2 * D_GMLP]
        qkv_ref[...] = z[:, 2 * D_GMLP:].astype(BF)

    return _call(
        body,
        name="in_proj",
        grid=(s // tm,),
        in_specs=[
            pl.BlockSpec((tm, D_MODEL), lambda i: (i, 0)),
            pl.BlockSpec((1, D_MODEL), lambda i: (0, 0)),
            pl.BlockSpec((D_IN, D_MODEL), lambda i: (0, 0)),
        ],
        out_specs=[
            pl.BlockSpec((tm, 2 * D_GMLP), lambda i: (i, 0)),
            pl.BlockSpec((tm, D_ATTN + 2 * D_KV), lambda i: (i, 0)),
            pl.BlockSpec((tm, D_MODEL), lambda i: (i, 0)),
        ],
        out_shape=[
            jax.ShapeDtypeStruct((s, 2 * D_GMLP), F32),
            jax.ShapeDtypeStruct((s, D_ATTN + 2 * D_KV), BF),
            jax.ShapeDtypeStruct((s, D_MODEL), BF),
        ],
        compiler_params=_params(("arbitrary",)),
    )(x, g1, w_in_t)


def _head_rows(h):
    kh, g = divmod(h, Q_PER_KV)
    return kh, slice(g * BLOCK, (g + 1) * BLOCK)


def _build_bias(bias_ref, bucket_ref, table_ref):
    bucket = bucket_ref[...]
    for h in range(N_Q):
        acc = jnp.zeros((BLOCK, 2 * BLOCK), F32)
        for b in range(REL_BUCKETS):
            acc = jnp.where(bucket == b, table_ref[h, b], acc)
        kh, rows = _head_rows(h)
        bias_ref[kh, rows, :] = acc


def _window_masks(i):
    row = lax.broadcasted_iota(jnp.int32, (ROWS4, BLOCK), 0) & (BLOCK - 1)
    col = lax.broadcasted_iota(jnp.int32, (ROWS4, BLOCK), 1)
    return (col > row) & (i > 0), col <= row


def _stack_heads(ref, kh, offset):
    first = offset + kh * Q_PER_KV * HEAD_DIM
    return jnp.concatenate(
        [ref[:, first + g * HEAD_DIM: first + (g + 1) * HEAD_DIM].astype(BF) for g in range(Q_PER_KV)], axis=0)


def _stack_sinks(sink_ref, kh):
    return jnp.concatenate([jnp.full((BLOCK, 1), sink_ref[0, kh * Q_PER_KV + g], F32) for g in range(Q_PER_KV)], axis=0)


def _tril_bf16(w_ref, g):
    row = lax.broadcasted_iota(jnp.int32, (BLOCK, BLOCK), 0)
    col = lax.broadcasted_iota(jnp.int32, (BLOCK, BLOCK), 1)
    return jnp.where(col <= row, w_ref[g], 0.0).astype(BF)


def _attn_probs(q_h, k_prev, k_cur, bias_h, sink, valid_prev, valid_cur):
    l_prev = jnp.where(valid_prev, _nt(q_h, k_prev) * SCALE + bias_h[:, :BLOCK], NEG_INF)
    l_cur = jnp.where(valid_cur, _nt(q_h, k_cur) * SCALE + bias_h[:, BLOCK:], NEG_INF)
    m = jnp.maximum(jnp.maximum(jnp.max(l_prev, axis=-1, keepdims=True), jnp.max(l_cur, axis=-1, keepdims=True)), sink)
    e_prev = jnp.exp(l_prev - m)
    e_cur = jnp.exp(l_cur - m)
    e_sink = jnp.exp(sink - m)
    denom = jnp.sum(e_prev, axis=-1, keepdims=True) + jnp.sum(e_cur, axis=-1, keepdims=True) + e_sink
    return e_prev / denom, e_cur / denom, e_sink / denom


def _mixer_specs(nb):
    cl = lambda i: jnp.minimum(i, nb - 1)
    return [
        pl.BlockSpec((BLOCK, 2 * D_GMLP), lambda i: (cl(i), 0)),
        pl.BlockSpec((BLOCK, D_ATTN), lambda i: (cl(i), 0)),
        pl.BlockSpec((BLOCK, 2 * D_KV), lambda i: (cl(i), D_ATTN // (2 * D_KV))),
        pl.BlockSpec((BLOCK, 2 * D_KV), lambda i: (jnp.maximum(cl(i) - 1, 0), D_ATTN // (2 * D_KV))),
        pl.BlockSpec((1, D_GMLP), lambda i: (0, 0)),
        pl.BlockSpec((GROUPS, BLOCK, BLOCK), lambda i: (0, 0, 0)),
        pl.BlockSpec((GROUPS, BLOCK, 1), lambda i: (0, 0, 0)),
        pl.BlockSpec(memory_space=pltpu.SMEM),
        pl.BlockSpec(memory_space=pltpu.SMEM),
        pl.BlockSpec((BLOCK, 2 * BLOCK), lambda i: (0, 0)),
    ]


def _mixer_fwd(zuv, qkv, gv, w_sp, b_sp, sinks, table, bucket, after=None):
    s = zuv.shape[0]
    nb = s // BLOCK

    def body(zuv_ref, q_ref, kvc_ref, kvp_ref, gv_ref, w_ref, b_ref, sink_ref, table_ref, bucket_ref,
             mix_ref, tanh_ref, prob_ref, psink_ref, bias_ref):
        i = pl.program_id(0)

        @pl.when(i == 0)
        def _():
            _build_bias(bias_ref, bucket_ref, table_ref)

        t = _gelu_tanh(zuv_ref[...])
        tanh_ref[...] = t
        u = _gelu(zuv_ref[:, :D_GMLP], t[:, :D_GMLP])
        vg = _gelu(zuv_ref[:, D_GMLP:], t[:, D_GMLP:])
        for g in range(GROUPS):
            sl = slice(g * GDIM, (g + 1) * GDIM)
            vg_g = vg[:, sl]
            vn = ((vg_g * _rms_scale(vg_g)) * gv_ref[:, sl]).astype(BF)
            sv = _nn(_tril_bf16(w_ref, g), vn) + b_ref[g]
            mix_ref[:, sl] = (u[:, sl] * sv).astype(BF)

        valid_prev, valid_cur = _window_masks(i)
        for kh in range(N_KV):
            ksl = slice(kh * HEAD_DIM, (kh + 1) * HEAD_DIM)
            vsl = slice(D_KV + kh * HEAD_DIM, D_KV + (kh + 1) * HEAD_DIM)
            q4 = _stack_heads(q_ref, kh, 0)
            p_prev, p_cur, p_sink = _attn_probs(
                q4, kvp_ref[:, ksl], kvc_ref[:, ksl], bias_ref[kh], _stack_sinks(sink_ref, kh), valid_prev, valid_cur)
            prob_ref[0, kh, :, :BLOCK] = p_prev
            prob_ref[0, kh, :, BLOCK:] = p_cur
            psink_ref[0, kh] = jnp.broadcast_to(p_sink, (ROWS4, LANES))
            o4 = _nn(p_prev.astype(BF), kvp_ref[:, vsl]) + _nn(p_cur.astype(BF), kvc_ref[:, vsl])
            for g in range(Q_PER_KV):
                first = D_GMLP + (kh * Q_PER_KV + g) * HEAD_DIM
                mix_ref[:, first:first + HEAD_DIM] = o4[g * BLOCK:(g + 1) * BLOCK].astype(BF)

    return _call(
        body,
        name="mixer_fwd",
        after=after,
        grid=(nb,),
        in_specs=_mixer_specs(nb),
        out_specs=[
            pl.BlockSpec((BLOCK, D_MODEL), lambda i: (i, 0)),
            pl.BlockSpec((BLOCK, 2 * D_GMLP), lambda i: (i, 0)),
            pl.BlockSpec((1, N_KV, ROWS4, 2 * BLOCK), lambda i: (i, 0, 0, 0)),
            pl.BlockSpec((1, N_KV, ROWS4, LANES), lambda i: (i, 0, 0, 0)),
        ],
        out_shape=[
            jax.ShapeDtypeStruct((s, D_MODEL), BF),
            jax.ShapeDtypeStruct((s, 2 * D_GMLP), F32),
            jax.ShapeDtypeStruct((nb, N_KV, ROWS4, 2 * BLOCK), F32),
            jax.ShapeDtypeStruct((nb, N_KV, ROWS4, LANES), F32),
        ],
        scratch_shapes=[pltpu.VMEM((N_KV, ROWS4, 2 * BLOCK), F32)],
        compiler_params=_params(("arbitrary",)),
    )(zuv, qkv, qkv, qkv, gv, w_sp, b_sp, sinks, table, bucket)


def _tail(x, mix, p, t, g2, gf, w_out, w_ff1, w_ff2, w_gate, w_proj):
    s = x.shape[0]
    tm = min(TOK_TILE, s)
    n_ff = w_ff1.shape[0]
    fc = D_FF // n_ff

    def body(x_ref, mix_ref, p_ref, t_ref, g2_ref, gf_ref, wo_ref, w1_ref, w2_ref, wg_ref, wp_ref,
             small_ref, dh1_ref, dh1b_ref, dmix_ref, hn2_ref, a_ref, df_ref, h2_ref, dh2_ref, dgl_ref, dpp_ref, f_ref):
        i = pl.program_id(0)

        @pl.when(i == 0)
        def _():
            small_ref[...] = jnp.zeros_like(small_ref)

        h1 = x_ref[...] + _nn(mix_ref[...], wo_ref[...])
        r2 = _rms_scale(h1)
        hn2 = ((h1 * r2) * g2_ref[...]).astype(BF)
        hn2_ref[...] = hn2
        h2 = h1
        for c in range(n_ff):
            f = _nn(hn2, w1_ref[c])
            f_ref[:, c * fc:(c + 1) * fc] = f
            a = jnp.square(jnp.maximum(f, 0.0)).astype(BF)
            a_ref[:, c * fc:(c + 1) * fc] = a
            h2 = h2 + _nn(a, w2_ref[c * fc:(c + 1) * fc, :])
        h2b = h2.astype(BF)
        h2_ref[...] = h2b
        gate = jax.nn.sigmoid(_nn(h2b, wg_ref[...]))
        pp = _nn(p_ref[...].astype(BF), wp_ref[...])
        h3 = h2 + gate * pp
        rf = _rms_scale(h3)
        gf_v = gf_ref[...]
        err = (h3 * rf) * gf_v - t_ref[...]
        small_ref[2:3, :] += jnp.sum(jnp.sum(err * err, axis=-1, keepdims=True), axis=0, keepdims=True) * (0.5 / D_MODEL)

        dy = err * (1.0 / D_MODEL)
        small_ref[1:2, :] += jnp.sum(dy * (h3 * rf), axis=0, keepdims=True)
        dh3 = _rms_bwd(dy * gf_v, h3, rf)
        dpp_ref[...] = (dh3 * gate).astype(BF)
        dgl = ((dh3 * pp) * (gate * (1.0 - gate))).astype(BF)
        dgl_ref[...] = dgl
        dh2 = dh3 + _nt(dgl, wg_ref[...])
        dh2b = dh2.astype(BF)
        dh2_ref[...] = dh2b
        dhn2 = jnp.zeros((tm, D_MODEL), F32)
        for c in range(n_ff):
            da = _nt(dh2b, w2_ref[c * fc:(c + 1) * fc, :])
            df = (da * (2.0 * jnp.maximum(f_ref[:, c * fc:(c + 1) * fc], 0.0))).astype(BF)
            df_ref[:, c * fc:(c + 1) * fc] = df
            dhn2 = dhn2 + _nt(df, w1_ref[c])
        small_ref[0:1, :] += jnp.sum(dhn2 * (h1 * r2), axis=0, keepdims=True)
        dh1 = dh2 + _rms_bwd(dhn2 * g2_ref[...], h1, r2)
        dh1_ref[...] = dh1
        dh1b = dh1.astype(BF)
        dh1b_ref[...] = dh1b
        dmix_ref[...] = _nt(dh1b, wo_ref[...])

    tile = lambda cols: pl.BlockSpec((tm, cols), lambda i: (i, 0))
    whole = lambda shape: pl.BlockSpec(shape, lambda i: (0,) * len(shape), pipeline_mode=pl.Buffered(1))
    row = pl.BlockSpec((1, D_MODEL), lambda i: (0, 0))
    act = lambda cols, dt: jax.ShapeDtypeStruct((s, cols), dt)
    return _call(
        body,
        name="tail",
        grid=(s // tm,),
        in_specs=[tile(D_MODEL), tile(D_MODEL), tile(PLE_DIM), tile(D_MODEL), row, row,
                  whole(w_out.shape), whole(w_ff1.shape), whole(w_ff2.shape), whole(w_gate.shape), whole(w_proj.shape)],
        out_specs=[pl.BlockSpec((8, D_MODEL), lambda i: (0, 0)), tile(D_MODEL), tile(D_MODEL), tile(D_MODEL), tile(D_MODEL), tile(D_FF),
                   tile(D_FF), tile(D_MODEL), tile(D_MODEL), tile(D_MODEL), tile(D_MODEL)],
        out_shape=[jax.ShapeDtypeStruct((8, D_MODEL), F32),
                   act(D_MODEL, F32), act(D_MODEL, BF), act(D_MODEL, F32), act(D_MODEL, BF), act(D_FF, BF), act(D_FF, BF), act(D_MODEL, BF),
                   act(D_MODEL, BF), act(D_MODEL, BF), act(D_MODEL, BF)],
        scratch_shapes=[pltpu.VMEM((tm, D_FF), F32)],
        compiler_params=_params(("arbitrary",)),
    )(x, mix, p, t, g2, gf, w_out, w_ff1, w_ff2, w_gate, w_proj)


def _mixer_bwd(zuv, qkv, dmix, saved, gv, w_sp, b_sp, bucket, after=None):
    s = zuv.shape[0]
    nb = s // BLOCK

    def body(zuv_ref, q_ref, kvc_ref, kvp_ref, gv_ref, w_ref, b_ref, bucket_ref, dmix_ref, tanh_ref, prob_ref, psink_ref,
             dzm_ref, dkv_ref, dgv_ref, dw_ref, db_ref, dattn_ref,
             dbias_ref, carry_ref, dsink_acc, db_acc):
        i = pl.program_id(0)

        @pl.when(i == 0)
        def _():
            dbias_ref[...] = jnp.zeros_like(dbias_ref)
            carry_ref[...] = jnp.zeros_like(carry_ref)
            dsink_acc[...] = jnp.zeros_like(dsink_acc)
            dgv_ref[...] = jnp.zeros_like(dgv_ref)
            dw_ref[...] = jnp.zeros_like(dw_ref)
            db_acc[...] = jnp.zeros_like(db_acc)

        @pl.when(i < nb)
        def _():
            u, du_dz = _gelu_and_grad(zuv_ref[:, :D_GMLP], tanh_ref[:, :D_GMLP])
            vg, dvg_dz = _gelu_and_grad(zuv_ref[:, D_GMLP:], tanh_ref[:, D_GMLP:])
            for g in range(GROUPS):
                sl = slice(g * GDIM, (g + 1) * GDIM)
                vg_g = vg[:, sl]
                rg = _rms_scale(vg_g)
                vhat = vg_g * rg
                gain = gv_ref[:, sl]
                vn = (vhat * gain).astype(BF)
                w_g = _tril_bf16(w_ref, g)
                sv = _nn(w_g, vn) + b_ref[g]
                dmix_a = dmix_ref[:, sl]
                dsv = dmix_a * u[:, sl]
                dsvb = dsv.astype(BF)
                db_acc[g] += jnp.sum(dsv, axis=-1, keepdims=True)
                dw_ref[g] += _nt(dsvb, vn)
                dvn = _tn(w_g, dsvb)
                dgv_ref[:, sl] += jnp.sum(dvn * vhat, axis=0, keepdims=True)
                dvg = _rms_bwd(dvn * gain, vg_g, rg)
                dzm_ref[:, sl] = ((dmix_a * sv) * du_dz[:, sl]).astype(BF)
                dzm_ref[:, D_GMLP + g * GDIM: D_GMLP + (g + 1) * GDIM] = (dvg * dvg_dz[:, sl]).astype(BF)

            for kh in range(N_KV):
                ksl = slice(kh * HEAD_DIM, (kh + 1) * HEAD_DIM)
                vsl = slice(D_KV + kh * HEAD_DIM, D_KV + (kh + 1) * HEAD_DIM)
                k_prev, k_cur = kvp_ref[:, ksl], kvc_ref[:, ksl]
                v_prev, v_cur = kvp_ref[:, vsl], kvc_ref[:, vsl]
                q4 = _stack_heads(q_ref, kh, 0)
                p_prev, p_cur, p_sink = prob_ref[0, kh, :, :BLOCK], prob_ref[0, kh, :, BLOCK:], psink_ref[0, kh, :, 0:1]
                do4 = _stack_heads(dmix_ref, kh, D_GMLP)
                dp_prev = _nt(do4, v_prev)
                dp_cur = _nt(do4, v_cur)
                delta = jnp.sum(p_prev * dp_prev, axis=-1, keepdims=True) + jnp.sum(p_cur * dp_cur, axis=-1, keepdims=True)
                ds_prev = p_prev * (dp_prev - delta)
                ds_cur = p_cur * (dp_cur - delta)
                dsink_acc[kh] -= p_sink * delta
                dbias_ref[kh, :, :BLOCK] += ds_prev
                dbias_ref[kh, :, BLOCK:] += ds_cur
                dsb_prev = ds_prev.astype(BF)
                dsb_cur = ds_cur.astype(BF)
                dq4 = (_nn(dsb_prev, k_prev) + _nn(dsb_cur, k_cur)) * SCALE
                for g in range(Q_PER_KV):
                    first = 2 * D_GMLP + (kh * Q_PER_KV + g) * HEAD_DIM
                    dzm_ref[:, first:first + HEAD_DIM] = dq4[g * BLOCK:(g + 1) * BLOCK].astype(BF)
                dkv_ref[:, ksl] = (carry_ref[:, ksl] + _tn(dsb_prev, q4) * SCALE).astype(BF)
                dkv_ref[:, vsl] = (carry_ref[:, vsl] + _tn(p_prev.astype(BF), do4)).astype(BF)
                carry_ref[:, ksl] = _tn(dsb_cur, q4) * SCALE
                carry_ref[:, vsl] = _tn(p_cur.astype(BF), do4)

        @pl.when(i == nb)
        def _():
            dkv_ref[...] = carry_ref[...].astype(BF)
            row = lax.broadcasted_iota(jnp.int32, (BLOCK, BLOCK), 0)
            col = lax.broadcasted_iota(jnp.int32, (BLOCK, BLOCK), 1)
            for g in range(GROUPS):
                dw_ref[g] = jnp.where(col <= row, dw_ref[g], 0.0)
                db_ref[g:g + 1, :] = jnp.sum(jnp.where(col == row, db_acc[g], 0.0), axis=0, keepdims=True)
            bucket = bucket_ref[...]
            for b in range(N_Q, REL_BUCKETS):
                dattn_ref[N_Q, b] = 0.0
            for h in range(N_Q):
                kh, rows = _head_rows(h)
                dattn_ref[N_Q, h] = jnp.sum(dsink_acc[kh, rows, :])
                dbh = dbias_ref[kh, rows, :]
                for b in range(REL_BUCKETS):
                    dattn_ref[h, b] = jnp.sum(jnp.where(bucket == b, dbh, 0.0))

    cl = lambda i: jnp.minimum(i, nb - 1)
    const = lambda shape: pl.BlockSpec(shape, lambda i: (0,) * len(shape))
    return _call(
        body,
        name="mixer_bwd",
        after=after,
        grid=(nb + 1,),
        in_specs=_mixer_specs(nb)[:7] + [
            const((BLOCK, 2 * BLOCK)),
            pl.BlockSpec((BLOCK, D_MODEL), lambda i: (cl(i), 0)),
            pl.BlockSpec((BLOCK, 2 * D_GMLP), lambda i: (cl(i), 0)),
            pl.BlockSpec((1, N_KV, ROWS4, 2 * BLOCK), lambda i: (cl(i), 0, 0, 0)),
            pl.BlockSpec((1, N_KV, ROWS4, LANES), lambda i: (cl(i), 0, 0, 0)),
        ],
        out_specs=[
            pl.BlockSpec((BLOCK, D_MAIN), lambda i: (cl(i), 0)),
            pl.BlockSpec((BLOCK, 2 * D_KV), lambda i: (jnp.maximum(i - 1, 0), 0)),
            const((1, D_GMLP)),
            const((GROUPS, BLOCK, BLOCK)),
            const((GROUPS, BLOCK)),
            pl.BlockSpec(memory_space=pltpu.SMEM),
        ],
        out_shape=[
            jax.ShapeDtypeStruct((s, D_MAIN), BF),
            jax.ShapeDtypeStruct((s, 2 * D_KV), BF),
            jax.ShapeDtypeStruct((1, D_GMLP), F32),
            jax.ShapeDtypeStruct((GROUPS, BLOCK, BLOCK), F32),
            jax.ShapeDtypeStruct((GROUPS, BLOCK), F32),
            jax.ShapeDtypeStruct((N_Q + 1, REL_BUCKETS), F32),
        ],
        scratch_shapes=[
            pltpu.VMEM((N_KV, ROWS4, 2 * BLOCK), F32),
            pltpu.VMEM((BLOCK, 2 * D_KV), F32),
            pltpu.VMEM((N_KV, ROWS4, 1), F32),
            pltpu.VMEM((GROUPS, BLOCK, 1), F32),
        ],
        compiler_params=_params(("arbitrary",)),
    )(zuv, qkv, qkv, qkv, gv, w_sp, b_sp, bucket, dmix, *saved)


def _in_bwd(x, dh1, dzm, dkv, g1, w_in_t, after=None):
    s = x.shape[0]
    tm = min(IO_TOK_TILE, s)

    def body(x_ref, dh1_ref, dzm_ref, dkv_ref, g_ref, w_ref, dx_ref, dg_ref):
        @pl.when(pl.program_id(0) == 0)
        def _():
            dg_ref[...] = jnp.zeros_like(dg_ref)

        dhn = _nn(dzm_ref[...], w_ref[:D_MAIN, :]) + _nn(dkv_ref[...], w_ref[D_MAIN:, :])
        xv = x_ref[...]
        r = _rms_scale(xv)
        dg_ref[...] += jnp.sum(dhn * (xv * r), axis=0, keepdims=True)
        dx_ref[...] = dh1_ref[...] + _rms_bwd(dhn * g_ref[...], xv, r)

    tile = lambda cols: pl.BlockSpec((tm, cols), lambda i: (i, 0))
    row = pl.BlockSpec((1, D_MODEL), lambda i: (0, 0))
    return _call(
        body,
        name="in_bwd",
        after=after,
        grid=(s // tm,),
        in_specs=[tile(D_MODEL), tile(D_MODEL), tile(D_MAIN), tile(2 * D_KV), row, pl.BlockSpec((D_IN, D_MODEL), lambda i: (0, 0))],
        out_specs=[tile(D_MODEL), row],
        out_shape=[jax.ShapeDtypeStruct((s, D_MODEL), F32), jax.ShapeDtypeStruct((1, D_MODEL), F32)],
        compiler_params=_params(("arbitrary",)),
    )(x, dh1, dzm, dkv, g1, w_in_t)


def _wgrad_in(dzm, dkv, hn, after=None):
    s = hn.shape[0]
    tm = 2 * D_KV
    n_main = D_MAIN // tm

    def body(dzm_ref, dkv_ref, hn_ref, o_ref):
        i = pl.program_id(0)

        @pl.when(i < n_main)
        def _():
            o_ref[...] = _tn(dzm_ref[...], hn_ref[...])

        @pl.when(i == n_main)
        def _():
            o_ref[...] = _tn(dkv_ref[...], hn_ref[...])

    return _call(
        body,
        name="wgrad_in",
        after=after,
        grid=(n_main + 1,),
        in_specs=[
            pl.BlockSpec((s, tm), lambda i: (0, jnp.minimum(i, n_main - 1))),
            pl.BlockSpec((s, tm), lambda i: (0, 0)),
            pl.BlockSpec((s, D_MODEL), lambda i: (0, 0)),
        ],
        out_specs=pl.BlockSpec((tm, D_MODEL), lambda i: (i, 0)),
        out_shape=jax.ShapeDtypeStruct((D_IN, D_MODEL), F32),
        compiler_params=_params(("arbitrary",)),
    )(dzm, dkv, hn)


def _wgrad(a, b, tm, tn, name, peer_cols=0, after=None):
    s, m = a.shape
    n = b.shape[1]

    def body(a_ref, b_ref, o_ref, at_ref):
        @pl.when(pl.program_id(1) == 0)
        def _():
            at_ref[...] = a_ref[...].astype(BF).T

        r = _nn(at_ref[...], b_ref[...])
        if peer_cols:
            for q in range(tn // peer_cols):
                o_ref[q] = r[:, q * peer_cols:(q + 1) * peer_cols]
        else:
            o_ref[...] = r

    if peer_cols:
        out_spec = pl.BlockSpec((tn // peer_cols, tm, peer_cols), lambda i, j: (j, i, 0))
        out_shape = jax.ShapeDtypeStruct((n // peer_cols, m, peer_cols), F32)
    else:
        out_spec = pl.BlockSpec((tm, tn), lambda i, j: (i, j))
        out_shape = jax.ShapeDtypeStruct((m, n), F32)
    return _call(
        body,
        name=name,
        after=after,
        grid=(m // tm, n // tn),
        in_specs=[pl.BlockSpec((s, tm), lambda i, j: (0, i)), pl.BlockSpec((s, tn), lambda i, j: (0, j))],
        out_specs=out_spec,
        out_shape=out_shape,
        scratch_shapes=[pltpu.VMEM((tm, s), BF)],
        compiler_params=_params(("arbitrary", "arbitrary")),
    )(a, b)


def _adamw_math(w, g, m, v):
    m_new = ADAM_B1 * m + (1.0 - ADAM_B1) * g
    v_new = ADAM_B2 * v + (1.0 - ADAM_B2) * jnp.square(g)
    m_hat = m_new / (1.0 - ADAM_B1 ** ADAM_STEP)
    v_hat = v_new / (1.0 - ADAM_B2 ** ADAM_STEP)
    delta = -ADAM_LR * (m_hat / (jnp.sqrt(v_hat) + ADAM_EPS) + ADAM_WD * w)
    return delta, m_new, v_new


def _final_adamw(part, recv, w, m, v, name):
    r, c = w.shape
    tr = min(r, 512)

    def body(p_ref, r_ref, w_ref, m_ref, v_ref, g_ref, d_ref, mo_ref, vo_ref):
        g = p_ref[...]
        for j in range(3):
            g = g + r_ref[j].astype(F32)
        g_ref[...] = g
        d_ref[...], mo_ref[...], vo_ref[...] = _adamw_math(w_ref[...], g, m_ref[...], v_ref[...])

    spec = pl.BlockSpec((tr, c), lambda i: (i, 0))
    return _call(
        body,
        name=name,
        grid=(r // tr,),
        in_specs=[spec, pl.BlockSpec((3, tr, c), lambda i: (0, i, 0)), spec, spec, spec],
        out_specs=[spec] * 4,
        out_shape=[jax.ShapeDtypeStruct((r, c), F32)] * 4,
        compiler_params=_params(("arbitrary",)),
    )(part, recv, w, m, v)


def _rs_sum(g, land, blocks, name):
    _, r, c = g.shape
    tr = min(r, 256)

    def body(blk_ref, g0_ref, g1_ref, g2_ref, g3_ref, l_ref, part_ref, send_ref):
        part_ref[...] = g0_ref[0] + l_ref[0]
        for j, gj_ref in enumerate((g1_ref, g2_ref, g3_ref)):
            send_ref[j] = (gj_ref[0] + l_ref[j + 1]).astype(BF)

    def pick(j):
        return pl.BlockSpec((1, tr, c), lambda i, blk: (blk[j], i, 0))

    return _call(
        body,
        name=name,
        grid_spec=pltpu.PrefetchScalarGridSpec(
            num_scalar_prefetch=1,
            grid=(r // tr,),
            in_specs=[pick(0), pick(1), pick(2), pick(3), pl.BlockSpec((4, tr, c), lambda i, blk: (0, i, 0))],
            out_specs=[pl.BlockSpec((tr, c), lambda i, blk: (i, 0)), pl.BlockSpec((3, tr, c), lambda i, blk: (0, i, 0))],
        ),
        out_shape=[jax.ShapeDtypeStruct((r, c), F32), jax.ShapeDtypeStruct((3, r, c), BF)],
        compiler_params=_params(("arbitrary",)),
    )(blocks, g, g, g, g, land)


def _final_adamw_few(parts, recvs, ws, ms, vs, name):
    n = len(ws)

    def body(*refs):
        ins, outs = refs[:5 * n], refs[5 * n:]
        for k in range(n):
            p_ref, r_ref, w_ref, m_ref, v_ref = (ins[q * n + k] for q in range(5))
            g = p_ref[...]
            for j in range(3):
                g = g + r_ref[j].astype(F32)
            outs[k][...] = g
            outs[n + k][...], outs[2 * n + k][...], outs[3 * n + k][...] = _adamw_math(w_ref[...], g, m_ref[...], v_ref[...])

    outs = _call(
        body,
        name=name,
        out_shape=[jax.ShapeDtypeStruct(w.shape, F32) for w in ws] * 4,
        compiler_params=_params(),
    )(*parts, *recvs, *ws, *ms, *vs)
    return outs[:n], outs[n:2 * n], outs[2 * n:3 * n], outs[3 * n:]


def _rs_sum_few(gs, lands, blocks, name):
    n = len(gs)

    def body(blk_ref, *refs):
        g_refs, l_refs, part_refs, send_refs = refs[:n], refs[n:2 * n], refs[2 * n:3 * n], refs[3 * n:]
        for k in range(n):
            part_refs[k][...] = g_refs[k][blk_ref[0]] + l_refs[k][0]
            for j in range(3):
                send_refs[k][j] = (g_refs[k][blk_ref[j + 1]] + l_refs[k][j + 1]).astype(BF)

    outs = _call(
        body,
        name=name,
        in_specs=[pl.BlockSpec(memory_space=pltpu.SMEM)] + [pl.BlockSpec(memory_space=pltpu.VMEM)] * (2 * n),
        out_shape=[jax.ShapeDtypeStruct(g.shape[1:], F32) for g in gs] + [jax.ShapeDtypeStruct((3,) + g.shape[1:], BF) for g in gs],
        compiler_params=_params(),
    )(blocks, *gs, *lands)
    return [(outs[k], outs[n + k]) for k in range(n)]


def _adamw_small(own, recv, g1_own, g1_recv, weights, moms, vels, after):
    n_w = len(weights)

    def body(*refs):
        own_refs, recv_refs, g1_own_ref, g1_recv_ref = refs[:5], refs[5:10], refs[10], refs[11]
        w_refs, m_refs, v_refs = refs[12:12 + n_w], refs[12 + n_w:12 + 2 * n_w], refs[12 + 2 * n_w:12 + 3 * n_w]
        outs = refs[12 + 3 * n_w:]
        loss_ref, g_refs, d_refs = outs[0], outs[1:1 + n_w], outs[1 + n_w:1 + 2 * n_w]
        mo_refs, vo_refs = outs[1 + 2 * n_w:1 + 3 * n_w], outs[1 + 3 * n_w:]
        x, y, c = lax.axis_index("x"), lax.axis_index("y"), lax.axis_index("c")

        def in_place_order(values, my_place):
            acc = None
            for place in range(len(values)):
                r = place ^ my_place
                term = values[-1]
                for q in range(len(values) - 2, -1, -1):
                    term = jnp.where(r == q, values[q], term)
                acc = term if acc is None else acc + term
            return acc

        def total(k, *index):
            index = index or (slice(None),) * (len(own_refs[k].shape) - 1)
            across = [own_refs[k][(0, *index)], recv_refs[k][(1, *index)], recv_refs[k][(0, *index)], recv_refs[k][(2, *index)]]
            return in_place_order([v.astype(F32) for v in across], 2 * x + y)

        g1 = in_place_order([g1_own_ref[0]] + [g1_recv_ref[j] for j in range(N_DEV - 1)], 4 * x + 2 * y + c)
        grads = [
            g1, total(1), total(2), total(3),
            total(4, slice(N_Q, None), slice(0, N_Q)), total(4, slice(0, N_Q), slice(None)),
            total(0, slice(0, 1), slice(None)), total(0, slice(1, 2), slice(None))]
        loss_ref[...] = total(0, slice(2, 3), slice(0, 1))
        for k in range(n_w):
            g_refs[k][...] = grads[k]
            d_refs[k][...], mo_refs[k][...], vo_refs[k][...] = _adamw_math(w_refs[k][...], grads[k], m_refs[k][...], v_refs[k][...])

    shapes = [jax.ShapeDtypeStruct(w.shape, F32) for w in weights]
    outs = _call(
        body,
        name="adamw_small",
        after=after,
        in_specs=[pl.BlockSpec(memory_space=pltpu.VMEM)] * (12 + 3 * n_w),
        out_shape=[jax.ShapeDtypeStruct((1, 1), F32)] + shapes * 4,
        compiler_params=_params(),
    )(*own, *recv, g1_own, g1_recv, *weights, *moms, *vels)
    return outs[0], outs[1:1 + n_w], outs[1 + n_w:1 + 2 * n_w], outs[1 + 2 * n_w:1 + 3 * n_w], outs[1 + 3 * n_w:]


def _place():
    x, y, c = lax.axis_index("x"), lax.axis_index("y"), lax.axis_index("c")
    return x, y, c, [(1 - x, y), (x, 1 - y), (1 - x, 1 - y)]


def _dev_index(px, py, pc):
    return 4 * px + 2 * py + pc


HBM_SPEC = pl.BlockSpec(memory_space=pltpu.HBM)
SEM_SPEC = pl.BlockSpec(memory_space=pltpu.SEMAPHORE)
ANY_SPEC = pl.BlockSpec(memory_space=pl.ANY)
DATAFLOW = pltpu.SideEffectType.DATAFLOW_SIDE_EFFECTING


def _hbm(a):
    return pltpu.with_memory_space_constraint(a, pltpu.HBM)


def _prep_weights(shards):
    k_n = len(shards)

    def body(*refs):
        ins, outs, stage, sems = refs[:k_n], refs[k_n:2 * k_n], refs[2 * k_n:3 * k_n], refs[3 * k_n]
        x, y, c, _ = _place()
        copies = []
        for k in range(k_n):
            stage[k][...] = ins[k][...].astype(BF)
            copies.append(pltpu.make_async_copy(stage[k], outs[k].at[_dev_index(x, y, c)], sems.at[k]))
            copies[k].start()
        for cp in copies:
            cp.wait()

    return _call(
        body,
        name="prep_weights",
        in_specs=[pl.BlockSpec(memory_space=pltpu.VMEM)] * k_n,
        out_specs=[ANY_SPEC] * k_n,
        out_shape=[jax.ShapeDtypeStruct((N_DEV,) + sh.shape, BF) for sh in shards],
        scratch_shapes=[pltpu.VMEM(sh.shape, BF) for sh in shards] + [pltpu.SemaphoreType.DMA((k_n,))],
        compiler_params=_params(),
    )(*shards)


def _ag_piece(land_k, block, half, peer, send_sem, recv_sem):
    ref = land_k.at[_dev_index(*block)]
    if half is not None:
        rows = land_k.shape[1] // 2
        ref = ref.at[pl.ds(half * rows, rows)]
    return pltpu.make_async_remote_copy(
        src_ref=ref, dst_ref=ref, send_sem=send_sem, recv_sem=recv_sem, device_id=peer, device_id_type=MESH)


def _ag_plan():
    x, y, c, _ = _place()
    me, sib = (x, y, c), (x, y, 1 - c)
    xn, yn, diag = (1 - x, y, c), (x, 1 - y, c), (1 - x, 1 - y, c)
    return dict(
        relay_halves=[(me, 0, xn), (me, 1, yn)],
        others=[(me, None, sib), (me, 1, xn), (me, 0, yn)],
        relays=[(xn, 0, yn), (yn, 1, xn)],
        near=[(xn, None, sib), (yn, None, sib)],
        far=[(diag, None, sib)],
    )


def _ag_stage(land, stage, send_sems, recv_sems, act):
    copies = _ag_plan()[stage]
    n = len(copies)
    for k in range(len(land)):
        for j, (block, half, peer) in enumerate(copies):
            cp = _ag_piece(land[k], block, half, peer, send_sems.at[n * k + j], recv_sems.at[n * k + j])
            if act == "start":
                cp.start()
            else:
                cp.wait_send()
                cp.wait_recv()


def _sem_shapes(*counts):
    return [pltpu.SemaphoreType.DMA((n,)) for n in counts for _ in range(2)]


def _ag_start(first, rest):
    lands = list(first) + list(rest)
    k_n, k_first = len(lands), len(first)
    k_rest = k_n - k_first

    def body(*refs):
        land = refs[:k_n]
        sems = refs[k_n:k_n + 6]
        token = refs[-1]
        x, y, c, chips = _place()
        targets = [(x, y, 1 - c)] + [(*chip, c) for chip in chips]
        for k in range(k_first):
            for j, to in enumerate(targets):
                _ag_piece(land[k], (x, y, c), None, to, sems[0].at[4 * k + j], sems[1].at[4 * k + j]).start()
        _ag_stage(land[k_first:], "relay_halves", sems[2], sems[3], "start")
        _ag_stage(land[k_first:], "others", sems[4], sems[5], "start")
        token[...] = jnp.zeros_like(token)

    outs = pl.pallas_call(
        body,
        name="ag_start",
        in_specs=[HBM_SPEC] * k_n,
        out_specs=(*[SEM_SPEC] * 6, *[HBM_SPEC] * k_n, pl.BlockSpec(memory_space=pltpu.VMEM)),
        out_shape=(*_sem_shapes(4 * k_first, 2 * k_rest, 3 * k_rest),
                   *[pltpu.HBM(a.shape, a.dtype) for a in lands], jax.ShapeDtypeStruct((8, LANES), F32)),
        input_output_aliases={k: 6 + k for k in range(k_n)},
        compiler_params=pltpu.CompilerParams(has_side_effects=DATAFLOW),
    )(*[_hbm(a) for a in lands])
    flying = list(outs[6:6 + k_n])
    return (outs[0], outs[1], flying[:k_first]), (outs[2:6], flying[k_first:]), outs[-1]


def _ag_split_call(lands, waits, starts, after, name):
    k_n = len(lands)
    plan_sizes = dict(relay_halves=2, others=3, relays=2, near=2, far=1)
    n_in, n_out = 2 * len(waits), 2 * len(starts)

    def body(*refs):
        land = refs[:k_n]
        in_sems = refs[k_n:k_n + n_in]
        out_sems, token = refs[len(refs) - 1 - n_out:len(refs) - 1], refs[-1]
        for w, (stage, _, _) in enumerate(waits):
            _ag_stage(land, stage, in_sems[2 * w], in_sems[2 * w + 1], "wait")
            if w < len(starts):
                _ag_stage(land, starts[w], out_sems[2 * w], out_sems[2 * w + 1], "start")
        token[...] = jnp.zeros_like(token)

    outs = pl.pallas_call(
        body,
        name=name,
        in_specs=[HBM_SPEC] * k_n + [SEM_SPEC] * n_in + [ANY_SPEC],
        out_specs=(*[HBM_SPEC] * k_n, *[SEM_SPEC] * n_out, pl.BlockSpec(memory_space=pltpu.VMEM)),
        out_shape=(*[pltpu.HBM(a.shape, a.dtype) for a in lands], *_sem_shapes(*[plan_sizes[s] * k_n for s in starts]),
                   jax.ShapeDtypeStruct((8, LANES), F32)),
        input_output_aliases={k: k for k in range(k_n)},
        compiler_params=pltpu.CompilerParams(has_side_effects=DATAFLOW),
    )(*lands, *[s for _, a, b in waits for s in (a, b)], after)
    return list(outs[:k_n]), list(outs[k_n:k_n + n_out]), outs[-1]


def _ag_mid(lands, send_sems, recv_sems, after, name):
    k_n = len(lands)

    def body(*refs):
        land = refs[:k_n]
        send1, recv1 = refs[k_n], refs[k_n + 1]
        fwd_send, fwd_recv = refs[-2], refs[-1]
        x, y, c, chips = _place()
        sources = [(x, y, 1 - c)] + [(*chip, c) for chip in chips]
        for k in range(k_n):
            mine = land[k].at[_dev_index(x, y, c)]
            for j, frm in enumerate(sources):
                got = land[k].at[_dev_index(*frm)]
                cp = pltpu.make_async_remote_copy(
                    src_ref=mine, dst_ref=got, send_sem=send1.at[4 * k + j], recv_sem=recv1.at[4 * k + j], device_id=frm, device_id_type=MESH)
                cp.wait_send()
                cp.wait_recv()
                if j >= 1:
                    pltpu.make_async_remote_copy(
                        src_ref=got, dst_ref=got, send_sem=fwd_send.at[3 * k + j - 1], recv_sem=fwd_recv.at[3 * k + j - 1],
                        device_id=(x, y, 1 - c), device_id_type=MESH).start()

    outs = pl.pallas_call(
        body,
        name=name,
        in_specs=[HBM_SPEC] * k_n + [SEM_SPEC, SEM_SPEC, ANY_SPEC],
        out_specs=(*[HBM_SPEC] * k_n, SEM_SPEC, SEM_SPEC),
        out_shape=(*[pltpu.HBM(a.shape, a.dtype) for a in lands], pltpu.SemaphoreType.DMA((3 * k_n,)), pltpu.SemaphoreType.DMA((3 * k_n,))),
        input_output_aliases={k: k for k in range(k_n)},
        compiler_params=pltpu.CompilerParams(has_side_effects=DATAFLOW),
    )(*lands, send_sems, recv_sems, after)
    return list(outs[:k_n]), outs[-2], outs[-1]


def _ag_end(lands, fwd_send, fwd_recv, name):
    k_n = len(lands)

    def body(*refs):
        land = refs[:k_n]
        fsend, frecv = refs[k_n], refs[k_n + 1]
        x, y, c, chips = _place()
        for k in range(k_n):
            for j, chip in enumerate(chips):
                cp = pltpu.make_async_remote_copy(
                    src_ref=land[k].at[_dev_index(*chip, c)], dst_ref=land[k].at[_dev_index(*chip, 1 - c)],
                    send_sem=fsend.at[3 * k + j], recv_sem=frecv.at[3 * k + j], device_id=(x, y, 1 - c), device_id_type=MESH)
                cp.wait_send()
                cp.wait_recv()

    outs = pl.pallas_call(
        body,
        name=name,
        in_specs=[HBM_SPEC] * k_n + [SEM_SPEC, SEM_SPEC],
        out_specs=tuple([HBM_SPEC] * k_n),
        out_shape=tuple(pltpu.HBM(a.shape, a.dtype) for a in lands),
        input_output_aliases={k: k for k in range(k_n)},
        compiler_params=pltpu.CompilerParams(has_side_effects=DATAFLOW),
    )(*lands, fwd_send, fwd_recv)
    return list(outs)


def _chips4():
    x, y, c, others = _place()
    return x, y, c, [(x, y)] + others


def _route_sibling(j):
    x, y, c, chips = _chips4()
    return _dev_index(*chips[j], 1 - c), j, (x, y, 1 - c)


def _route_chips(j):
    x, y, c, chips = _chips4()
    return j, j, (*chips[j + 1], c)


def _route_sibling_whole(j):
    x, y, c, _ = _chips4()
    return 0, 0, (x, y, 1 - c)


def _route_chips_whole(j):
    x, y, c, chips = _chips4()
    return 0, j, (*chips[j + 1], c)


def _route_everyone(j):
    x, y, c, _ = _chips4()
    flip = [(j + 1) >> 2 & 1, (j + 1) >> 1 & 1, (j + 1) & 1]
    return 0, j, tuple(1 - v if f else v for v, f in zip((x, y, c), flip))


def _xchg_copies(routes, src, dst, send_sems, recv_sems):
    copies, sem = [], 0
    for k, (route, n) in enumerate(routes):
        for j in range(n):
            si, di, peer = route(j)
            copies.append(pltpu.make_async_remote_copy(
                src_ref=src[k].at[si], dst_ref=dst[k].at[di], send_sem=send_sems.at[sem], recv_sem=recv_sems.at[sem],
                device_id=peer, device_id_type=MESH))
            sem += 1
    return copies


def _xchg_start(srcs, slot_shapes, routes, name):
    k_n = len(srcs)
    n_sem = sum(n for _, n in routes)
    dsts = [lax.empty((n,) + tuple(sh), a.dtype) for sh, a, (_, n) in zip(slot_shapes, srcs, routes)]

    def body(*refs):
        src, dst = refs[:k_n], refs[k_n:2 * k_n]
        send_sems, recv_sems, token = refs[2 * k_n], refs[2 * k_n + 1], refs[-1]
        for cp in _xchg_copies(routes, src, dst, send_sems, recv_sems):
            cp.start()
        token[...] = jnp.zeros_like(token)

    arrays = list(srcs) + dsts
    outs = pl.pallas_call(
        body,
        name=name,
        in_specs=[HBM_SPEC] * (2 * k_n),
        out_specs=(SEM_SPEC, SEM_SPEC, *[HBM_SPEC] * (2 * k_n), pl.BlockSpec(memory_space=pltpu.VMEM)),
        out_shape=(pltpu.SemaphoreType.DMA((n_sem,)), pltpu.SemaphoreType.DMA((n_sem,)),
                   *[pltpu.HBM(a.shape, a.dtype) for a in arrays], jax.ShapeDtypeStruct((8, LANES), F32)),
        input_output_aliases={i: 2 + i for i in range(2 * k_n)},
        compiler_params=pltpu.CompilerParams(has_side_effects=DATAFLOW),
    )(*[_hbm(a) for a in arrays])
    return outs[0], outs[1], list(outs[2:2 + k_n]), list(outs[2 + k_n:2 + 2 * k_n]), outs[-1]


def _xchg_wait(send_sems, recv_sems, srcs, dsts, routes, after, name):
    k_n = len(srcs)

    def body(*refs):
        src, dst = refs[:k_n], refs[k_n:2 * k_n]
        for cp in _xchg_copies(routes, src, dst, refs[2 * k_n], refs[2 * k_n + 1]):
            cp.wait_send()
            cp.wait_recv()

    arrays = list(srcs) + list(dsts)
    outs = pl.pallas_call(
        body,
        name=name,
        in_specs=[HBM_SPEC] * (2 * k_n) + [SEM_SPEC, SEM_SPEC, ANY_SPEC],
        out_specs=tuple([HBM_SPEC] * (2 * k_n)),
        out_shape=tuple(pltpu.HBM(a.shape, a.dtype) for a in arrays),
        input_output_aliases={i: i for i in range(2 * k_n)},
        compiler_params=pltpu.CompilerParams(has_side_effects=DATAFLOW),
    )(*arrays, send_sems, recv_sems, after)
    return list(outs[:k_n]), list(outs[k_n:])


SMALL = ("norm1_gain", "gmlp_v_gain", "w_spatial", "b_spatial", "attn_sinks", "rel_bias_table", "norm2_gain", "final_gain")
LANES = 128


def _swap_start(grads, smalls, tag):
    srcs = list(grads) + [a[None] for a in smalls]
    shapes = [g.shape[1:] for g in grads] + [a.shape for a in smalls]
    routes = [(_route_sibling, 4)] * len(grads) + [(_route_sibling_whole, 1)] * len(smalls)
    return _xchg_start(srcs, shapes, routes, f"rs_{tag}_swap_start"), routes, len(grads)


def _swap_sums(swap, names, after, tag, few=False, small_dtypes=None):
    (send1, recv1, src1, land1, _), routes, n_rs = swap
    x, y, c, chips = _chips4()
    blocks = jnp.stack([_dev_index(*chip, c) for chip in chips]).astype(jnp.int32)
    src1, land1 = _xchg_wait(send1, recv1, src1, land1, routes, after, f"rs_{tag}_swap_wait")
    if few:
        sums = _rs_sum_few(src1[:n_rs], land1[:n_rs], blocks, f"rs_{tag}_sum")
    else:
        sums = [_rs_sum(g, land, blocks, f"rs_sum_{n}") for g, land, n in zip(src1[:n_rs], land1[:n_rs], names)]
    if len(src1) == n_rs:
        return sums, []
    return sums, _pair_sum(src1[n_rs:], land1[n_rs:], small_dtypes or [F32] * (len(src1) - n_rs), f"rs_{tag}_sum_small")


def _pair_sum(mine, theirs, dtypes, name):
    def body(*refs):
        n = len(refs) // 3
        for k in range(n):
            refs[2 * n + k][...] = (refs[k][...] + refs[n + k][...]).astype(dtypes[k])

    return _call(
        body,
        name=name,
        out_shape=[jax.ShapeDtypeStruct(a.shape, dt) for a, dt in zip(mine, dtypes)],
        compiler_params=_params(),
    )(*mine, *theirs)


def _chips_start(sums, small_sums, tag, everyone=()):
    sends = [ps[1] for ps in sums] + list(small_sums) + [a[None] for a in everyone]
    routes = [(_route_chips, 3)] * len(sums) + [(_route_chips_whole, 3)] * len(small_sums) + [(_route_everyone, 7)] * len(everyone)
    return _xchg_start(sends, [a.shape[1:] for a in sends], routes, f"rs_{tag}_chips_start"), routes


def _chips_wait(chips, after, tag):
    (send2, recv2, src2, land2, _), routes = chips
    return _xchg_wait(send2, recv2, src2, land2, routes, after, f"rs_{tag}_chips_wait")


def kernel(x, p, norm1_gain, w_in, gmlp_v_gain, w_spatial, b_spatial, attn_sinks, rel_bias_table, w_out, norm2_gain, w_ff1, w_ff2, w_ple_proj, w_ple_gate, final_gain, loss_target, m_norm1_gain, m_w_in, m_gmlp_v_gain, m_w_spatial, m_b_spatial, m_attn_sinks, m_rel_bias_table, m_w_out, m_norm2_gain, m_w_ff1, m_w_ff2, m_w_ple_proj, m_w_ple_gate, m_final_gain, v_norm1_gain, v_w_in, v_gmlp_v_gain, v_w_spatial, v_b_spatial, v_attn_sinks, v_rel_bias_table, v_w_out, v_norm2_gain, v_w_ff1, v_w_ff2, v_w_ple_proj, v_w_ple_gate, v_final_gain):
    args = dict(locals())
    s = x.shape[1]
    big = ("w_in", "w_out", "w_ff1", "w_ff2", "w_ple_proj", "w_ple_gate")

    x2, p2, t2 = x.reshape(s, D_MODEL), p.reshape(s, PLE_DIM), loss_target.reshape(s, D_MODEL)
    g1, gv, w_sp, b_sp, sinks, table, g2, gf = (args[n] for n in SMALL)
    bucket = jnp.asarray(_bucket_table())
    b_col = b_sp.reshape(GROUPS, BLOCK, 1)

    def shard(name):
        return args[name][0].T if name.endswith("w_in") else args[name][0]

    lands = _prep_weights([shard(n) for n in big])
    (send_in, recv_in, fly_in), (rest_sems, fly_rest), token = _ag_start(lands[:1], lands[1:])
    mid_in, fwd_send_in, fwd_recv_in = _ag_mid(fly_in, send_in, recv_in, token, "ag_mid_w_in")
    g_in = _ag_end(mid_in, fwd_send_in, fwd_recv_in, "ag_end_w_in")[0]
    full_in = g_in.reshape(D_IN, D_MODEL)

    zuv, qkv, hn1 = _in_proj(x2, g1, full_in)
    fly_rest, relay_sems, relayed = _ag_split_call(
        fly_rest, [("relay_halves", *rest_sems[:2])], ["relays"], zuv, "ag_relay")
    mix, *saved = _mixer_fwd(zuv, qkv, gv, w_sp[0], b_col, sinks, table.T, bucket, after=relayed)
    fly_rest, fwd_sems, _ = _ag_split_call(
        fly_rest, [("others", *rest_sems[2:]), ("relays", *relay_sems)], ["near", "far"], mix, "ag_mid_rest")
    g_out, g_ff1, g_ff2, g_proj, g_gate = _ag_split_call(
        fly_rest, [("near", *fwd_sems[:2]), ("far", *fwd_sems[2:])], [], mix, "ag_end_rest")[0]
    full_out, full_ff2, full_gate = g_out.reshape(D_MODEL, D_MODEL), g_ff2.reshape(D_FF, D_MODEL), g_gate.reshape(D_MODEL, D_MODEL)
    full_proj = g_proj.transpose(1, 0, 2).reshape(PLE_DIM, D_MODEL)

    (tail_small, dh1, dh1b, dmix, hn2, a, df, h2, dh2, dgl, dpp) = _tail(
        x2, mix, p2, t2, g2, gf.reshape(1, D_MODEL), full_out, g_ff1, full_ff2, full_gate, full_proj)

    ffn_names, mid_names = ("w_ff1", "w_ff2"), ("w_out", "w_ple_proj", "w_ple_gate")
    gw_ff1 = _wgrad(hn2, df, D_MODEL, 2 * D_FF // N_DEV, "wgrad_ff1", peer_cols=D_FF // N_DEV)
    gw_ff2 = _wgrad(a, dh2, 1024, D_MODEL, "wgrad_ff2").reshape(N_DEV, D_FF // N_DEV, D_MODEL)
    ffn_swap = _swap_start([gw_ff1, gw_ff2], [], "ffn")
    gw_gate = _wgrad(h2, dgl, D_MODEL, 512, "wgrad_gate", after=ffn_swap[0][4]).reshape(N_DEV, D_MODEL // N_DEV, D_MODEL)
    gw_proj = _wgrad(p2, dpp, PLE_DIM, D_MODEL // 2, "wgrad_proj", peer_cols=D_MODEL // N_DEV, after=gw_gate)
    gw_out = _wgrad(mix, dh1b, D_MODEL, 512, "wgrad_out", after=gw_proj).reshape(N_DEV, D_MODEL // N_DEV, D_MODEL)
    mid_swap = _swap_start([gw_out, gw_proj, gw_gate], [tail_small], "mid")
    ffn_sums, _ = _swap_sums(ffn_swap, ffn_names, mid_swap[0][4], "ffn")
    ffn_chips = _chips_start(ffn_sums, [], "ffn")

    dzm, dkv, d_gv, d_wsp, d_bsp, d_attn = _mixer_bwd(
        zuv, qkv, dmix, saved, gv, w_sp[0], b_col, bucket, after=ffn_chips[0][4])
    mid_sums, mid_small = _swap_sums(mid_swap, mid_names, dzm, "mid", few=True)
    mid_chips = _chips_start(mid_sums, mid_small, "mid")
    gw_in = _wgrad_in(dzm, dkv, hn1, after=mid_chips[0][4]).reshape(N_DEV, D_IN // N_DEV, D_MODEL)
    in_swap = _swap_start([gw_in], [d_gv, d_wsp, d_bsp, d_attn], "in")
    dx, d_g1 = _in_bwd(x2, dh1, dzm, dkv, g1, full_in, after=in_swap[0][4])
    grad_x = dx.reshape(x.shape)
    in_sums, in_small = _swap_sums(in_swap, ("w_in",), dx, "in", small_dtypes=[F32, BF, F32, F32])
    in_chips = _chips_start(in_sums, in_small, "in", everyone=[d_g1])

    grads, deltas, new_m, new_v = {}, {}, {}, {}
    after, small_own, small_recv = in_chips[0][4], [], []
    for chips, sums, names, tag in ((ffn_chips, ffn_sums, ffn_names, "ffn"), (mid_chips, mid_sums, mid_names, "mid"),
                                    (in_chips, in_sums, ("w_in",), "in")):
        sent, recvs = _chips_wait(chips, after, tag)
        small_own += sent[len(names):]
        small_recv += recvs[len(names):]
        if tag == "mid":
            done = _final_adamw_few([ps[0] for ps in sums], recvs[:len(names)], [shard(n) for n in names],
                                    [shard("m_" + n) for n in names], [shard("v_" + n) for n in names], "adamw_mid")
            done = list(zip(*done))
        else:
            done = [_final_adamw(part, recv, shard(n), shard("m_" + n), shard("v_" + n), "adamw_" + n)
                    for n, (part, _), recv in zip(names, sums, recvs)]
        for n, results in zip(names, done):
            for dst, arr in zip((grads, deltas, new_m, new_v), results):
                dst[n] = (arr.T if n == "w_in" else arr)[None]
            after = results[1]

    views = {"w_spatial": (GROUPS, BLOCK, BLOCK), "b_spatial": (GROUPS, BLOCK), "final_gain": (1, D_MODEL)}
    def view(name, base):
        return args[name].T if base == "rel_bias_table" else args[name].reshape(views.get(base, args[name].shape))

    small_in = [[view(pre + n, n) for n in SMALL] for pre in ("", "m_", "v_")]
    loss, *small_out = _adamw_small(small_own[:-1], small_recv[:-1], small_own[-1], small_recv[-1], *small_in, after=after)
    for dst, arrays in zip((grads, deltas, new_m, new_v), small_out):
        for n, arr in zip(SMALL, arrays):
            dst[n] = arr.T if n == "rel_bias_table" else arr.reshape(args[n].shape)
    loss = loss[0, 0]

    order = ("norm1_gain", "w_in", "gmlp_v_gain", "w_spatial", "b_spatial", "attn_sinks", "rel_bias_table", "w_out",
             "norm2_gain", "w_ff1", "w_ff2", "w_ple_proj", "w_ple_gate", "final_gain")
    return (loss, grad_x, *[grads[n] for n in order], *[deltas[n] for n in order],
            *[new_m[n] for n in order], *[new_v[n] for n in order])
```

```python
import math

import numpy as np
import jax
import jax.numpy as jnp
from jax import lax
from jax.experimental import pallas as pl
from jax.experimental.pallas import tpu as pltpu

F32 = jnp.float32
BF = jnp.bfloat16
MESH = pl.DeviceIdType.MESH
N_DEV = 8

D_MODEL = 1024
PLE_DIM = 256
D_GMLP = 512
GROUPS = 4
GDIM = 128
BLOCK = 128
D_ATTN = 512
HEAD_DIM = 64
N_Q = 8
Q_PER_KV = 4
N_KV = N_Q // Q_PER_KV
ROWS4 = Q_PER_KV * BLOCK
D_KV = 128
D_FF = 4096
D_IN = 1792
D_MAIN = 2 * D_GMLP + D_ATTN
REL_BUCKETS = 32
EPS = 1e-6
NEG_INF = -1e30
SCALE = HEAD_DIM ** -0.5
GELU_C = math.sqrt(2.0 / math.pi)
GELU_A = 0.044715

ADAM_LR = 0.001
ADAM_B1 = 0.9
ADAM_B2 = 0.999
ADAM_EPS = 1e-08
ADAM_WD = 0.01
ADAM_STEP = 10

V7X_VMEM_LIMIT = 60000 * 1024
TOK_TILE = 256
IO_TOK_TILE = 512


def _call(body, after=None, **kw):
    if after is None:
        return pl.pallas_call(body, **kw)
    n_in = len(kw["in_specs"])

    def ordered(*refs):
        body(*refs[:n_in], *refs[n_in + 1:])

    kw["in_specs"] = list(kw["in_specs"]) + [pl.BlockSpec(memory_space=pl.ANY)]
    fn = pl.pallas_call(ordered, **kw)
    return lambda *operands: fn(*operands, after)


def _params(sem=None):
    if sem is None:
        return pltpu.CompilerParams(vmem_limit_bytes=V7X_VMEM_LIMIT)
    return pltpu.CompilerParams(dimension_semantics=sem, vmem_limit_bytes=V7X_VMEM_LIMIT)


def _nn(a, b):
    return jnp.dot(a, b, preferred_element_type=F32)


def _nt(a, b):
    return lax.dot_general(a, b, (((1,), (1,)), ((), ())), preferred_element_type=F32)


def _tn(a, b):
    return lax.dot_general(a, b, (((0,), (0,)), ((), ())), preferred_element_type=F32)


def _gelu_tanh(x):
    return jnp.tanh(GELU_C * (x + GELU_A * (x * x * x)))


def _gelu(x, t):
    return x * (0.5 * (1.0 + t))


def _gelu_and_grad(x, t):
    cdf = 0.5 * (1.0 + t)
    return x * cdf, cdf + 0.5 * x * (1.0 - t * t) * (GELU_C * (1.0 + 3.0 * GELU_A * (x * x)))


def _rms_scale(x):
    return lax.rsqrt(jnp.mean(x * x, axis=-1, keepdims=True) + EPS)


def _rms_bwd(dxn, x, r):
    return r * dxn - x * ((r * r * r) * jnp.mean(dxn * x, axis=-1, keepdims=True))


def _bucket_table():
    a = np.arange(BLOCK)[:, None]
    j = np.arange(2 * BLOCK)[None, :]
    n = BLOCK + a - j
    valid = (n >= 0) & (n < BLOCK)
    nc = np.maximum(n, 0)
    max_exact = REL_BUCKETS // 2
    nf = np.maximum(nc, 1).astype(np.float32)
    large = max_exact + (
        np.log(nf / np.float32(max_exact)) / np.float32(math.log(BLOCK / max_exact)) * np.float32(REL_BUCKETS - max_exact)
    ).astype(np.int32)
    large = np.minimum(large, REL_BUCKETS - 1)
    bucket = np.where(nc < max_exact, nc, large)
    return np.where(valid, bucket, -1).astype(np.int32)


def _in_proj(x, g1, w_in_t):
    s = x.shape[0]
    tm = min(IO_TOK_TILE, s)

    def body(x_ref, g_ref, w_ref, zuv_ref, qkv_ref, hn_ref):
        xv = x_ref[...]
        hn = ((xv * _rms_scale(xv)) * g_ref[...]).astype(BF)
        hn_ref[...] = hn
        z = _nt(hn, w_ref[...])
        zuv_ref[...] = z[:, : 2 * D_GMLP]
        qkv_ref[...] = z[:, 2 * D_GMLP:].astype(BF)

    return _call(
        body,
        name="in_proj",
        grid=(s // tm,),
        in_specs=[
            pl.BlockSpec((tm, D_MODEL), lambda i: (i, 0)),
            pl.BlockSpec((1, D_MODEL), lambda i: (0, 0)),
            pl.BlockSpec((D_IN, D_MODEL), lambda i: (0, 0)),
        ],
        out_specs=[
            pl.BlockSpec((tm, 2 * D_GMLP), lambda i: (i, 0)),
            pl.BlockSpec((tm, D_ATTN + 2 * D_KV), lambda i: (i, 0)),
            pl.BlockSpec((tm, D_MODEL), lambda i: (i, 0)),
        ],
        out_shape=[
            jax.ShapeDtypeStruct((s, 2 * D_GMLP), F32),
            jax.ShapeDtypeStruct((s, D_ATTN + 2 * D_KV), BF),
            jax.ShapeDtypeStruct((s, D_MODEL), BF),
        ],
        compiler_params=_params(("arbitrary",)),
    )(x, g1, w_in_t)


def _head_rows(h):
    kh, g = divmod(h, Q_PER_KV)
    return kh, slice(g * BLOCK, (g + 1) * BLOCK)


def _build_bias(bias_ref, bucket_ref, table_ref):
    bucket = bucket_ref[...]
    for h in range(N_Q):
        acc = jnp.zeros((BLOCK, 2 * BLOCK), F32)
        for b in range(REL_BUCKETS):
            acc = jnp.where(bucket == b, table_ref[h, b], acc)
        kh, rows = _head_rows(h)
        bias_ref[kh, rows, :] = acc


def _window_masks(i):
    row = lax.broadcasted_iota(jnp.int32, (ROWS4, BLOCK), 0) & (BLOCK - 1)
    col = lax.broadcasted_iota(jnp.int32, (ROWS4, BLOCK), 1)
    return (col > row) & (i > 0), col <= row


def _stack_heads(ref, kh, offset):
    first = offset + kh * Q_PER_KV * HEAD_DIM
    return jnp.concatenate(
        [ref[:, first + g * HEAD_DIM: first + (g + 1) * HEAD_DIM].astype(BF) for g in range(Q_PER_KV)], axis=0)


def _stack_sinks(sink_ref, kh):
    return jnp.concatenate([jnp.full((BLOCK, 1), sink_ref[0, kh * Q_PER_KV + g], F32) for g in range(Q_PER_KV)], axis=0)


def _tril_bf16(w_ref, g):
    row = lax.broadcasted_iota(jnp.int32, (BLOCK, BLOCK), 0)
    col = lax.broadcasted_iota(jnp.int32, (BLOCK, BLOCK), 1)
    return jnp.where(col <= row, w_ref[g], 0.0).astype(BF)


def _attn_probs(q_h, k_prev, k_cur, bias_h, sink, valid_prev, valid_cur):
    l_prev = jnp.where(valid_prev, _nt(q_h, k_prev) * SCALE + bias_h[:, :BLOCK], NEG_INF)
    l_cur = jnp.where(valid_cur, _nt(q_h, k_cur) * SCALE + bias_h[:, BLOCK:], NEG_INF)
    m = jnp.maximum(jnp.maximum(jnp.max(l_prev, axis=-1, keepdims=True), jnp.max(l_cur, axis=-1, keepdims=True)), sink)
    e_prev = jnp.exp(l_prev - m)
    e_cur = jnp.exp(l_cur - m)
    e_sink = jnp.exp(sink - m)
    denom = jnp.sum(e_prev, axis=-1, keepdims=True) + jnp.sum(e_cur, axis=-1, keepdims=True) + e_sink
    return e_prev / denom, e_cur / denom, e_sink / denom


def _mixer_specs(nb):
    cl = lambda i: jnp.minimum(i, nb - 1)
    return [
        pl.BlockSpec((BLOCK, 2 * D_GMLP), lambda i: (cl(i), 0)),
        pl.BlockSpec((BLOCK, D_ATTN), lambda i: (cl(i), 0)),
        pl.BlockSpec((BLOCK, 2 * D_KV), lambda i: (cl(i), D_ATTN // (2 * D_KV))),
        pl.BlockSpec((BLOCK, 2 * D_KV), lambda i: (jnp.maximum(cl(i) - 1, 0), D_ATTN // (2 * D_KV))),
        pl.BlockSpec((1, D_GMLP), lambda i: (0, 0)),
        pl.BlockSpec((GROUPS, BLOCK, BLOCK), lambda i: (0, 0, 0)),
        pl.BlockSpec((GROUPS, BLOCK, 1), lambda i: (0, 0, 0)),
        pl.BlockSpec(memory_space=pltpu.SMEM),
        pl.BlockSpec(memory_space=pltpu.SMEM),
        pl.BlockSpec((BLOCK, 2 * BLOCK), lambda i: (0, 0)),
    ]


def _mixer_fwd(zuv, qkv, gv, w_sp, b_sp, sinks, table, bucket, after=None):
    s = zuv.shape[0]
    nb = s // BLOCK

    def body(zuv_ref, q_ref, kvc_ref, kvp_ref, gv_ref, w_ref, b_ref, sink_ref, table_ref, bucket_ref,
             mix_ref, tanh_ref, prob_ref, psink_ref, bias_ref):
        i = pl.program_id(0)

        @pl.when(i == 0)
        def _():
            _build_bias(bias_ref, bucket_ref, table_ref)

        t = _gelu_tanh(zuv_ref[...])
        tanh_ref[...] = t
        u = _gelu(zuv_ref[:, :D_GMLP], t[:, :D_GMLP])
        vg = _gelu(zuv_ref[:, D_GMLP:], t[:, D_GMLP:])
        for g in range(GROUPS):
            sl = slice(g * GDIM, (g + 1) * GDIM)
            vg_g = vg[:, sl]
            vn = ((vg_g * _rms_scale(vg_g)) * gv_ref[:, sl]).astype(BF)
            sv = _nn(_tril_bf16(w_ref, g), vn) + b_ref[g]
            mix_ref[:, sl] = (u[:, sl] * sv).astype(BF)

        valid_prev, valid_cur = _window_masks(i)
        for kh in range(N_KV):
            ksl = slice(kh * HEAD_DIM, (kh + 1) * HEAD_DIM)
            vsl = slice(D_KV + kh * HEAD_DIM, D_KV + (kh + 1) * HEAD_DIM)
            q4 = _stack_heads(q_ref, kh, 0)
            p_prev, p_cur, p_sink = _attn_probs(
                q4, kvp_ref[:, ksl], kvc_ref[:, ksl], bias_ref[kh], _stack_sinks(sink_ref, kh), valid_prev, valid_cur)
            prob_ref[0, kh, :, :BLOCK] = p_prev
            prob_ref[0, kh, :, BLOCK:] = p_cur
            psink_ref[0, kh] = jnp.broadcast_to(p_sink, (ROWS4, LANES))
            o4 = _nn(p_prev.astype(BF), kvp_ref[:, vsl]) + _nn(p_cur.astype(BF), kvc_ref[:, vsl])
            for g in range(Q_PER_KV):
                first = D_GMLP + (kh * Q_PER_KV + g) * HEAD_DIM
                mix_ref[:, first:first + HEAD_DIM] = o4[g * BLOCK:(g + 1) * BLOCK].astype(BF)

    return _call(
        body,
        name="mixer_fwd",
        after=after,
        grid=(nb,),
        in_specs=_mixer_specs(nb),
        out_specs=[
            pl.BlockSpec((BLOCK, D_MODEL), lambda i: (i, 0)),
            pl.BlockSpec((BLOCK, 2 * D_GMLP), lambda i: (i, 0)),
            pl.BlockSpec((1, N_KV, ROWS4, 2 * BLOCK), lambda i: (i, 0, 0, 0)),
            pl.BlockSpec((1, N_KV, ROWS4, LANES), lambda i: (i, 0, 0, 0)),
        ],
        out_shape=[
            jax.ShapeDtypeStruct((s, D_MODEL), BF),
            jax.ShapeDtypeStruct((s, 2 * D_GMLP), F32),
            jax.ShapeDtypeStruct((nb, N_KV, ROWS4, 2 * BLOCK), F32),
            jax.ShapeDtypeStruct((nb, N_KV, ROWS4, LANES), F32),
        ],
        scratch_shapes=[pltpu.VMEM((N_KV, ROWS4, 2 * BLOCK), F32)],
        compiler_params=_params(("arbitrary",)),
    )(zuv, qkv, qkv, qkv, gv, w_sp, b_sp, sinks, table, bucket)


def _tail(x, mix, p, t, g2, gf, w_out, w_ff1, w_ff2, w_gate, w_proj):
    s = x.shape[0]
    tm = min(TOK_TILE, s)
    n_ff = w_ff1.shape[0]
    fc = D_FF // n_ff

    def body(x_ref, mix_ref, p_ref, t_ref, g2_ref, gf_ref, wo_ref, w1_ref, w2_ref, wg_ref, wp_ref,
             small_ref, dh1_ref, dh1b_ref, dmix_ref, hn2_ref, a_ref, df_ref, h2_ref, dh2_ref, dgl_ref, dpp_ref, f_ref):
        i = pl.program_id(0)

        @pl.when(i == 0)
        def _():
            small_ref[...] = jnp.zeros_like(small_ref)

        h1 = x_ref[...] + _nn(mix_ref[...], wo_ref[...])
        r2 = _rms_scale(h1)
        hn2 = ((h1 * r2) * g2_ref[...]).astype(BF)
        hn2_ref[...] = hn2.T
        h2 = h1
        for c in range(n_ff):
            f = _nn(hn2, w1_ref[c])
            f_ref[:, c * fc:(c + 1) * fc] = f
            a = jnp.square(jnp.maximum(f, 0.0)).astype(BF)
            a_ref[c * fc:(c + 1) * fc, :] = a.T
            h2 = h2 + _nn(a, w2_ref[c * fc:(c + 1) * fc, :])
        h2b = h2.astype(BF)
        h2_ref[...] = h2b
        gate = jax.nn.sigmoid(_nn(h2b, wg_ref[...]))
        pp = _nn(p_ref[...].astype(BF), wp_ref[...])
        h3 = h2 + gate * pp
        rf = _rms_scale(h3)
        gf_v = gf_ref[...]
        err = (h3 * rf) * gf_v - t_ref[...]
        small_ref[2:3, :] += jnp.sum(jnp.sum(err * err, axis=-1, keepdims=True), axis=0, keepdims=True) * (0.5 / D_MODEL)

        dy = err * (1.0 / D_MODEL)
        small_ref[1:2, :] += jnp.sum(dy * (h3 * rf), axis=0, keepdims=True)
        dh3 = _rms_bwd(dy * gf_v, h3, rf)
        dpp_ref[...] = (dh3 * gate).astype(BF)
        dgl = ((dh3 * pp) * (gate * (1.0 - gate))).astype(BF)
        dgl_ref[...] = dgl
        dh2 = dh3 + _nt(dgl, wg_ref[...])
        dh2b = dh2.astype(BF)
        dh2_ref[...] = dh2b
        dhn2 = jnp.zeros((tm, D_MODEL), F32)
        for c in range(n_ff):
            da = _nt(dh2b, w2_ref[c * fc:(c + 1) * fc, :])
            df = (da * (2.0 * jnp.maximum(f_ref[:, c * fc:(c + 1) * fc], 0.0))).astype(BF)
            df_ref[:, c * fc:(c + 1) * fc] = df
            dhn2 = dhn2 + _nt(df, w1_ref[c])
        small_ref[0:1, :] += jnp.sum(dhn2 * (h1 * r2), axis=0, keepdims=True)
        dh1 = dh2 + _rms_bwd(dhn2 * g2_ref[...], h1, r2)
        dh1_ref[...] = dh1
        dh1b = dh1.astype(BF)
        dh1b_ref[...] = dh1b
        dmix_ref[...] = _nt(dh1b, wo_ref[...])

    tile = lambda cols: pl.BlockSpec((tm, cols), lambda i: (i, 0))
    across = lambda rows: pl.BlockSpec((rows, tm), lambda i: (0, i))
    whole = lambda shape: pl.BlockSpec(shape, lambda i: (0,) * len(shape), pipeline_mode=pl.Buffered(1))
    row = pl.BlockSpec((1, D_MODEL), lambda i: (0, 0))
    act = lambda cols, dt: jax.ShapeDtypeStruct((s, cols), dt)
    return _call(
        body,
        name="tail",
        grid=(s // tm,),
        in_specs=[tile(D_MODEL), tile(D_MODEL), tile(PLE_DIM), tile(D_MODEL), row, row,
                  whole(w_out.shape), whole(w_ff1.shape), whole(w_ff2.shape), whole(w_gate.shape), whole(w_proj.shape)],
        out_specs=[pl.BlockSpec((8, D_MODEL), lambda i: (0, 0)), tile(D_MODEL), tile(D_MODEL), tile(D_MODEL), across(D_MODEL),
                   across(D_FF), tile(D_FF), tile(D_MODEL), tile(D_MODEL), tile(D_MODEL), tile(D_MODEL)],
        out_shape=[jax.ShapeDtypeStruct((8, D_MODEL), F32),
                   act(D_MODEL, F32), act(D_MODEL, BF), act(D_MODEL, F32), jax.ShapeDtypeStruct((D_MODEL, s), BF),
                   jax.ShapeDtypeStruct((D_FF, s), BF), act(D_FF, BF), act(D_MODEL, BF),
                   act(D_MODEL, BF), act(D_MODEL, BF), act(D_MODEL, BF)],
        scratch_shapes=[pltpu.VMEM((tm, D_FF), F32)],
        compiler_params=_params(("arbitrary",)),
    )(x, mix, p, t, g2, gf, w_out, w_ff1, w_ff2, w_gate, w_proj)


def _mixer_bwd(zuv, qkv, dmix, saved, gv, w_sp, b_sp, bucket, after=None):
    s = zuv.shape[0]
    nb = s // BLOCK

    def body(zuv_ref, q_ref, kvc_ref, kvp_ref, gv_ref, w_ref, b_ref, bucket_ref, dmix_ref, tanh_ref, prob_ref, psink_ref,
             dzm_ref, dkv_ref, dgv_ref, dw_ref, db_ref, dattn_ref,
             dbias_ref, carry_ref, dsink_acc, db_acc):
        i = pl.program_id(0)

        @pl.when(i == 0)
        def _():
            dbias_ref[...] = jnp.zeros_like(dbias_ref)
            carry_ref[...] = jnp.zeros_like(carry_ref)
            dsink_acc[...] = jnp.zeros_like(dsink_acc)
            dgv_ref[...] = jnp.zeros_like(dgv_ref)
            dw_ref[...] = jnp.zeros_like(dw_ref)
            db_acc[...] = jnp.zeros_like(db_acc)

        @pl.when(i < nb)
        def _():
            u, du_dz = _gelu_and_grad(zuv_ref[:, :D_GMLP], tanh_ref[:, :D_GMLP])
            vg, dvg_dz = _gelu_and_grad(zuv_ref[:, D_GMLP:], tanh_ref[:, D_GMLP:])
            for g in range(GROUPS):
                sl = slice(g * GDIM, (g + 1) * GDIM)
                vg_g = vg[:, sl]
                rg = _rms_scale(vg_g)
                vhat = vg_g * rg
                gain = gv_ref[:, sl]
                vn = (vhat * gain).astype(BF)
                w_g = _tril_bf16(w_ref, g)
                sv = _nn(w_g, vn) + b_ref[g]
                dmix_a = dmix_ref[:, sl]
                dsv = dmix_a * u[:, sl]
                dsvb = dsv.astype(BF)
                db_acc[g] += jnp.sum(dsv, axis=-1, keepdims=True)
                dw_ref[g] += _nt(dsvb, vn)
                dvn = _tn(w_g, dsvb)
                dgv_ref[:, sl] += jnp.sum(dvn * vhat, axis=0, keepdims=True)
                dvg = _rms_bwd(dvn * gain, vg_g, rg)
                dzm_ref[:, sl] = ((dmix_a * sv) * du_dz[:, sl]).astype(BF)
                dzm_ref[:, D_GMLP + g * GDIM: D_GMLP + (g + 1) * GDIM] = (dvg * dvg_dz[:, sl]).astype(BF)

            for kh in range(N_KV):
                ksl = slice(kh * HEAD_DIM, (kh + 1) * HEAD_DIM)
                vsl = slice(D_KV + kh * HEAD_DIM, D_KV + (kh + 1) * HEAD_DIM)
                k_prev, k_cur = kvp_ref[:, ksl], kvc_ref[:, ksl]
                v_prev, v_cur = kvp_ref[:, vsl], kvc_ref[:, vsl]
                q4 = _stack_heads(q_ref, kh, 0)
                p_prev, p_cur, p_sink = prob_ref[0, kh, :, :BLOCK], prob_ref[0, kh, :, BLOCK:], psink_ref[0, kh, :, 0:1]
                do4 = _stack_heads(dmix_ref, kh, D_GMLP)
                dp_prev = _nt(do4, v_prev)
                dp_cur = _nt(do4, v_cur)
                delta = jnp.sum(p_prev * dp_prev, axis=-1, keepdims=True) + jnp.sum(p_cur * dp_cur, axis=-1, keepdims=True)
                ds_prev = p_prev * (dp_prev - delta)
                ds_cur = p_cur * (dp_cur - delta)
                dsink_acc[kh] -= p_sink * delta
                dbias_ref[kh, :, :BLOCK] += ds_prev
                dbias_ref[kh, :, BLOCK:] += ds_cur
                dsb_prev = ds_prev.astype(BF)
                dsb_cur = ds_cur.astype(BF)
                dq4 = (_nn(dsb_prev, k_prev) + _nn(dsb_cur, k_cur)) * SCALE
                for g in range(Q_PER_KV):
                    first = 2 * D_GMLP + (kh * Q_PER_KV + g) * HEAD_DIM
                    dzm_ref[:, first:first + HEAD_DIM] = dq4[g * BLOCK:(g + 1) * BLOCK].astype(BF)
                dkv_ref[:, ksl] = (carry_ref[:, ksl] + _tn(dsb_prev, q4) * SCALE).astype(BF)
                dkv_ref[:, vsl] = (carry_ref[:, vsl] + _tn(p_prev.astype(BF), do4)).astype(BF)
                carry_ref[:, ksl] = _tn(dsb_cur, q4) * SCALE
                carry_ref[:, vsl] = _tn(p_cur.astype(BF), do4)

        @pl.when(i == nb)
        def _():
            dkv_ref[...] = carry_ref[...].astype(BF)
            row = lax.broadcasted_iota(jnp.int32, (BLOCK, BLOCK), 0)
            col = lax.broadcasted_iota(jnp.int32, (BLOCK, BLOCK), 1)
            for g in range(GROUPS):
                dw_ref[g] = jnp.where(col <= row, dw_ref[g], 0.0)
                db_ref[g:g + 1, :] = jnp.sum(jnp.where(col == row, db_acc[g], 0.0), axis=0, keepdims=True)
            bucket = bucket_ref[...]
            for b in range(N_Q, REL_BUCKETS):
                dattn_ref[N_Q, b] = 0.0
            for h in range(N_Q):
                kh, rows = _head_rows(h)
                dattn_ref[N_Q, h] = jnp.sum(dsink_acc[kh, rows, :])
                dbh = dbias_ref[kh, rows, :]
                for b in range(REL_BUCKETS):
                    dattn_ref[h, b] = jnp.sum(jnp.where(bucket == b, dbh, 0.0))

    cl = lambda i: jnp.minimum(i, nb - 1)
    const = lambda shape: pl.BlockSpec(shape, lambda i: (0,) * len(shape))
    return _call(
        body,
        name="mixer_bwd",
        after=after,
        grid=(nb + 1,),
        in_specs=_mixer_specs(nb)[:7] + [
            const((BLOCK, 2 * BLOCK)),
            pl.BlockSpec((BLOCK, D_MODEL), lambda i: (cl(i), 0)),
            pl.BlockSpec((BLOCK, 2 * D_GMLP), lambda i: (cl(i), 0)),
            pl.BlockSpec((1, N_KV, ROWS4, 2 * BLOCK), lambda i: (cl(i), 0, 0, 0)),
            pl.BlockSpec((1, N_KV, ROWS4, LANES), lambda i: (cl(i), 0, 0, 0)),
        ],
        out_specs=[
            pl.BlockSpec((BLOCK, D_MAIN), lambda i: (cl(i), 0)),
            pl.BlockSpec((BLOCK, 2 * D_KV), lambda i: (jnp.maximum(i - 1, 0), 0)),
            const((1, D_GMLP)),
            const((GROUPS, BLOCK, BLOCK)),
            const((GROUPS, BLOCK)),
            pl.BlockSpec(memory_space=pltpu.SMEM),
        ],
        out_shape=[
            jax.ShapeDtypeStruct((s, D_MAIN), BF),
            jax.ShapeDtypeStruct((s, 2 * D_KV), BF),
            jax.ShapeDtypeStruct((1, D_GMLP), F32),
            jax.ShapeDtypeStruct((GROUPS, BLOCK, BLOCK), F32),
            jax.ShapeDtypeStruct((GROUPS, BLOCK), F32),
            jax.ShapeDtypeStruct((N_Q + 1, REL_BUCKETS), F32),
        ],
        scratch_shapes=[
            pltpu.VMEM((N_KV, ROWS4, 2 * BLOCK), F32),
            pltpu.VMEM((BLOCK, 2 * D_KV), F32),
            pltpu.VMEM((N_KV, ROWS4, 1), F32),
            pltpu.VMEM((GROUPS, BLOCK, 1), F32),
        ],
        compiler_params=_params(("arbitrary",)),
    )(zuv, qkv, qkv, qkv, gv, w_sp, b_sp, bucket, dmix, *saved)


def _in_bwd(x, dh1, dzm, dkv, g1, w_in_t, after=None):
    s = x.shape[0]
    tm = min(IO_TOK_TILE, s)

    def body(x_ref, dh1_ref, dzm_ref, dkv_ref, g_ref, w_ref, dx_ref, dg_ref):
        @pl.when(pl.program_id(0) == 0)
        def _():
            dg_ref[...] = jnp.zeros_like(dg_ref)

        dhn = _nn(dzm_ref[...], w_ref[:D_MAIN, :]) + _nn(dkv_ref[...], w_ref[D_MAIN:, :])
        xv = x_ref[...]
        r = _rms_scale(xv)
        dg_ref[...] += jnp.sum(dhn * (xv * r), axis=0, keepdims=True)
        dx_ref[...] = dh1_ref[...] + _rms_bwd(dhn * g_ref[...], xv, r)

    tile = lambda cols: pl.BlockSpec((tm, cols), lambda i: (i, 0))
    row = pl.BlockSpec((1, D_MODEL), lambda i: (0, 0))
    return _call(
        body,
        name="in_bwd",
        after=after,
        grid=(s // tm,),
        in_specs=[tile(D_MODEL), tile(D_MODEL), tile(D_MAIN), tile(2 * D_KV), row, pl.BlockSpec((D_IN, D_MODEL), lambda i: (0, 0))],
        out_specs=[tile(D_MODEL), row],
        out_shape=[jax.ShapeDtypeStruct((s, D_MODEL), F32), jax.ShapeDtypeStruct((1, D_MODEL), F32)],
        compiler_params=_params(("arbitrary",)),
    )(x, dh1, dzm, dkv, g1, w_in_t)


def _wgrad_in(dzm, dkv, hn, after=None):
    s = hn.shape[0]
    tm = 2 * D_KV
    n_main = D_MAIN // tm

    def body(dzm_ref, dkv_ref, hn_ref, o_ref):
        i = pl.program_id(0)

        @pl.when(i < n_main)
        def _():
            o_ref[...] = _tn(dzm_ref[...], hn_ref[...])

        @pl.when(i == n_main)
        def _():
            o_ref[...] = _tn(dkv_ref[...], hn_ref[...])

    return _call(
        body,
        name="wgrad_in",
        after=after,
        grid=(n_main + 1,),
        in_specs=[
            pl.BlockSpec((s, tm), lambda i: (0, jnp.minimum(i, n_main - 1))),
            pl.BlockSpec((s, tm), lambda i: (0, 0)),
            pl.BlockSpec((s, D_MODEL), lambda i: (0, 0)),
        ],
        out_specs=pl.BlockSpec((tm, D_MODEL), lambda i: (i, 0)),
        out_shape=jax.ShapeDtypeStruct((D_IN, D_MODEL), F32),
        compiler_params=_params(("arbitrary",)),
    )(dzm, dkv, hn)


def _wgrad(a, b, tm, tn, name, peer_cols=0, after=None, a_transposed=False):
    m, s = a.shape if a_transposed else a.shape[::-1]
    n = b.shape[1]

    def body(a_ref, b_ref, o_ref, *scratch):
        if a_transposed:
            at_ref = a_ref
        else:
            at_ref, = scratch

            @pl.when(pl.program_id(1) == 0)
            def _():
                at_ref[...] = a_ref[...].astype(BF).T

        r = _nn(at_ref[...], b_ref[...])
        if peer_cols:
            for q in range(tn // peer_cols):
                o_ref[q] = r[:, q * peer_cols:(q + 1) * peer_cols]
        else:
            o_ref[...] = r

    if peer_cols:
        out_spec = pl.BlockSpec((tn // peer_cols, tm, peer_cols), lambda i, j: (j, i, 0))
        out_shape = jax.ShapeDtypeStruct((n // peer_cols, m, peer_cols), F32)
    else:
        out_spec = pl.BlockSpec((tm, tn), lambda i, j: (i, j))
        out_shape = jax.ShapeDtypeStruct((m, n), F32)
    return _call(
        body,
        name=name,
        after=after,
        grid=(m // tm, n // tn),
        in_specs=[pl.BlockSpec((tm, s), lambda i, j: (i, 0)) if a_transposed else pl.BlockSpec((s, tm), lambda i, j: (0, i)),
                  pl.BlockSpec((s, tn), lambda i, j: (0, j))],
        out_specs=out_spec,
        out_shape=out_shape,
        scratch_shapes=[] if a_transposed else [pltpu.VMEM((tm, s), BF)],
        compiler_params=_params(("arbitrary", "arbitrary")),
    )(a, b)


def _adamw_math(w, g, m, v):
    m_new = ADAM_B1 * m + (1.0 - ADAM_B1) * g
    v_new = ADAM_B2 * v + (1.0 - ADAM_B2) * jnp.square(g)
    m_hat = m_new / (1.0 - ADAM_B1 ** ADAM_STEP)
    v_hat = v_new / (1.0 - ADAM_B2 ** ADAM_STEP)
    delta = -ADAM_LR * (m_hat / (jnp.sqrt(v_hat) + ADAM_EPS) + ADAM_WD * w)
    return delta, m_new, v_new


def _final_adamw(part, recv, w, m, v, name):
    r, c = w.shape
    tr = min(r, 512)

    def body(p_ref, r_ref, w_ref, m_ref, v_ref, g_ref, d_ref, mo_ref, vo_ref):
        g = p_ref[...]
        for j in range(3):
            g = g + r_ref[j].astype(F32)
        g_ref[...] = g
        d_ref[...], mo_ref[...], vo_ref[...] = _adamw_math(w_ref[...], g, m_ref[...], v_ref[...])

    spec = pl.BlockSpec((tr, c), lambda i: (i, 0))
    return _call(
        body,
        name=name,
        grid=(r // tr,),
        in_specs=[spec, pl.BlockSpec((3, tr, c), lambda i: (0, i, 0)), spec, spec, spec],
        out_specs=[spec] * 4,
        out_shape=[jax.ShapeDtypeStruct((r, c), F32)] * 4,
        compiler_params=_params(("arbitrary",)),
    )(part, recv, w, m, v)


def _rs_sum(g, land, blocks, name):
    _, r, c = g.shape
    tr = min(r, 256)

    def body(blk_ref, g0_ref, g1_ref, g2_ref, g3_ref, l_ref, part_ref, send_ref):
        part_ref[...] = g0_ref[0] + l_ref[0]
        for j, gj_ref in enumerate((g1_ref, g2_ref, g3_ref)):
            send_ref[j] = (gj_ref[0] + l_ref[j + 1]).astype(BF)

    def pick(j):
        return pl.BlockSpec((1, tr, c), lambda i, blk: (blk[j], i, 0))

    return _call(
        body,
        name=name,
        grid_spec=pltpu.PrefetchScalarGridSpec(
            num_scalar_prefetch=1,
            grid=(r // tr,),
            in_specs=[pick(0), pick(1), pick(2), pick(3), pl.BlockSpec((4, tr, c), lambda i, blk: (0, i, 0))],
            out_specs=[pl.BlockSpec((tr, c), lambda i, blk: (i, 0)), pl.BlockSpec((3, tr, c), lambda i, blk: (0, i, 0))],
        ),
        out_shape=[jax.ShapeDtypeStruct((r, c), F32), jax.ShapeDtypeStruct((3, r, c), BF)],
        compiler_params=_params(("arbitrary",)),
    )(blocks, g, g, g, g, land)


def _final_adamw_few(parts, recvs, ws, ms, vs, name):
    n = len(ws)

    def body(*refs):
        ins, outs = refs[:5 * n], refs[5 * n:]
        for k in range(n):
            p_ref, r_ref, w_ref, m_ref, v_ref = (ins[q * n + k] for q in range(5))
            g = p_ref[...]
            for j in range(3):
                g = g + r_ref[j].astype(F32)
            outs[k][...] = g
            outs[n + k][...], outs[2 * n + k][...], outs[3 * n + k][...] = _adamw_math(w_ref[...], g, m_ref[...], v_ref[...])

    outs = _call(
        body,
        name=name,
        out_shape=[jax.ShapeDtypeStruct(w.shape, F32) for w in ws] * 4,
        compiler_params=_params(),
    )(*parts, *recvs, *ws, *ms, *vs)
    return outs[:n], outs[n:2 * n], outs[2 * n:3 * n], outs[3 * n:]


def _rs_sum_few(gs, lands, blocks, name):
    n = len(gs)

    def body(blk_ref, *refs):
        g_refs, l_refs, part_refs, send_refs = refs[:n], refs[n:2 * n], refs[2 * n:3 * n], refs[3 * n:]
        for k in range(n):
            part_refs[k][...] = g_refs[k][blk_ref[0]] + l_refs[k][0]
            for j in range(3):
                send_refs[k][j] = (g_refs[k][blk_ref[j + 1]] + l_refs[k][j + 1]).astype(BF)

    outs = _call(
        body,
        name=name,
        in_specs=[pl.BlockSpec(memory_space=pltpu.SMEM)] + [pl.BlockSpec(memory_space=pltpu.VMEM)] * (2 * n),
        out_shape=[jax.ShapeDtypeStruct(g.shape[1:], F32) for g in gs] + [jax.ShapeDtypeStruct((3,) + g.shape[1:], BF) for g in gs],
        compiler_params=_params(),
    )(blocks, *gs, *lands)
    return [(outs[k], outs[n + k]) for k in range(n)]


def _adamw_small(own, recv, g1_own, g1_recv, weights, moms, vels, after):
    n_w = len(weights)

    def body(*refs):
        own_refs, recv_refs, g1_own_ref, g1_recv_ref = refs[:5], refs[5:10], refs[10], refs[11]
        w_refs, m_refs, v_refs = refs[12:12 + n_w], refs[12 + n_w:12 + 2 * n_w], refs[12 + 2 * n_w:12 + 3 * n_w]
        outs = refs[12 + 3 * n_w:]
        loss_ref, g_refs, d_refs = outs[0], outs[1:1 + n_w], outs[1 + n_w:1 + 2 * n_w]
        mo_refs, vo_refs = outs[1 + 2 * n_w:1 + 3 * n_w], outs[1 + 3 * n_w:]
        x, y, c = lax.axis_index("x"), lax.axis_index("y"), lax.axis_index("c")

        def in_place_order(values, my_place):
            acc = None
            for place in range(len(values)):
                r = place ^ my_place
                term = values[-1]
                for q in range(len(values) - 2, -1, -1):
                    term = jnp.where(r == q, values[q], term)
                acc = term if acc is None else acc + term
            return acc

        def total(k, *index):
            index = index or (slice(None),) * (len(own_refs[k].shape) - 1)
            across = [own_refs[k][(0, *index)], recv_refs[k][(1, *index)], recv_refs[k][(0, *index)], recv_refs[k][(2, *index)]]
            return in_place_order([v.astype(F32) for v in across], 2 * x + y)

        g1 = in_place_order([g1_own_ref[0]] + [g1_recv_ref[j] for j in range(N_DEV - 1)], 4 * x + 2 * y + c)
        grads = [
            g1, total(1), total(2), total(3),
            total(4, slice(N_Q, None), slice(0, N_Q)), total(4, slice(0, N_Q), slice(None)),
            total(0, slice(0, 1), slice(None)), total(0, slice(1, 2), slice(None))]
        loss_ref[...] = total(0, slice(2, 3), slice(0, 1))
        for k in range(n_w):
            g_refs[k][...] = grads[k]
            d_refs[k][...], mo_refs[k][...], vo_refs[k][...] = _adamw_math(w_refs[k][...], grads[k], m_refs[k][...], v_refs[k][...])

    shapes = [jax.ShapeDtypeStruct(w.shape, F32) for w in weights]
    outs = _call(
        body,
        name="adamw_small",
        after=after,
        in_specs=[pl.BlockSpec(memory_space=pltpu.VMEM)] * (12 + 3 * n_w),
        out_shape=[jax.ShapeDtypeStruct((1, 1), F32)] + shapes * 4,
        compiler_params=_params(),
    )(*own, *recv, g1_own, g1_recv, *weights, *moms, *vels)
    return outs[0], outs[1:1 + n_w], outs[1 + n_w:1 + 2 * n_w], outs[1 + 2 * n_w:1 + 3 * n_w], outs[1 + 3 * n_w:]


def _place():
    x, y, c = lax.axis_index("x"), lax.axis_index("y"), lax.axis_index("c")
    return x, y, c, [(1 - x, y), (x, 1 - y), (1 - x, 1 - y)]


def _dev_index(px, py, pc):
    return 4 * px + 2 * py + pc


HBM_SPEC = pl.BlockSpec(memory_space=pltpu.HBM)
SEM_SPEC = pl.BlockSpec(memory_space=pltpu.SEMAPHORE)
ANY_SPEC = pl.BlockSpec(memory_space=pl.ANY)
DATAFLOW = pltpu.SideEffectType.DATAFLOW_SIDE_EFFECTING


def _hbm(a):
    return pltpu.with_memory_space_constraint(a, pltpu.HBM)


def _prep_weights(shards):
    k_n = len(shards)

    def body(*refs):
        ins, outs, stage, sems = refs[:k_n], refs[k_n:2 * k_n], refs[2 * k_n:3 * k_n], refs[3 * k_n]
        x, y, c, _ = _place()
        copies = []
        for k in range(k_n):
            stage[k][...] = ins[k][...].astype(BF)
            copies.append(pltpu.make_async_copy(stage[k], outs[k].at[_dev_index(x, y, c)], sems.at[k]))
            copies[k].start()
        for cp in copies:
            cp.wait()

    return _call(
        body,
        name="prep_weights",
        in_specs=[pl.BlockSpec(memory_space=pltpu.VMEM)] * k_n,
        out_specs=[ANY_SPEC] * k_n,
        out_shape=[jax.ShapeDtypeStruct((N_DEV,) + sh.shape, BF) for sh in shards],
        scratch_shapes=[pltpu.VMEM(sh.shape, BF) for sh in shards] + [pltpu.SemaphoreType.DMA((k_n,))],
        compiler_params=_params(),
    )(*shards)


def _ag_piece(land_k, block, half, peer, send_sem, recv_sem):
    ref = land_k.at[_dev_index(*block)]
    if half is not None:
        rows = land_k.shape[1] // 2
        ref = ref.at[pl.ds(half * rows, rows)]
    return pltpu.make_async_remote_copy(
        src_ref=ref, dst_ref=ref, send_sem=send_sem, recv_sem=recv_sem, device_id=peer, device_id_type=MESH)


def _ag_plan():
    x, y, c, _ = _place()
    me, sib = (x, y, c), (x, y, 1 - c)
    xn, yn, diag = (1 - x, y, c), (x, 1 - y, c), (1 - x, 1 - y, c)
    return dict(
        relay_halves=[(me, 0, xn), (me, 1, yn)],
        others=[(me, None, sib), (me, 1, xn), (me, 0, yn)],
        relays=[(xn, 0, yn), (yn, 1, xn)],
        near=[(xn, None, sib), (yn, None, sib)],
        far=[(diag, None, sib)],
    )


def _ag_stage(land, stage, send_sems, recv_sems, act):
    copies = _ag_plan()[stage]
    n = len(copies)
    for k in range(len(land)):
        for j, (block, half, peer) in enumerate(copies):
            cp = _ag_piece(land[k], block, half, peer, send_sems.at[n * k + j], recv_sems.at[n * k + j])
            if act == "start":
                cp.start()
            else:
                cp.wait_send()
                cp.wait_recv()


def _sem_shapes(*counts):
    return [pltpu.SemaphoreType.DMA((n,)) for n in counts for _ in range(2)]


def _ag_start(first, rest):
    lands = list(first) + list(rest)
    k_n, k_first = len(lands), len(first)
    k_rest = k_n - k_first

    def body(*refs):
        land = refs[:k_n]
        sems = refs[k_n:k_n + 6]
        token = refs[-1]
        x, y, c, chips = _place()
        targets = [(x, y, 1 - c)] + [(*chip, c) for chip in chips]
        for k in range(k_first):
            for j, to in enumerate(targets):
                _ag_piece(land[k], (x, y, c), None, to, sems[0].at[4 * k + j], sems[1].at[4 * k + j]).start()
        _ag_stage(land[k_first:], "relay_halves", sems[2], sems[3], "start")
        _ag_stage(land[k_first:], "others", sems[4], sems[5], "start")
        token[...] = jnp.zeros_like(token)

    outs = pl.pallas_call(
        body,
        name="ag_start",
        in_specs=[HBM_SPEC] * k_n,
        out_specs=(*[SEM_SPEC] * 6, *[HBM_SPEC] * k_n, pl.BlockSpec(memory_space=pltpu.VMEM)),
        out_shape=(*_sem_shapes(4 * k_first, 2 * k_rest, 3 * k_rest),
                   *[pltpu.HBM(a.shape, a.dtype) for a in lands], jax.ShapeDtypeStruct((8, LANES), F32)),
        input_output_aliases={k: 6 + k for k in range(k_n)},
        compiler_params=pltpu.CompilerParams(has_side_effects=DATAFLOW),
    )(*[_hbm(a) for a in lands])
    flying = list(outs[6:6 + k_n])
    return (outs[0], outs[1], flying[:k_first]), (outs[2:6], flying[k_first:]), outs[-1]


def _ag_split_call(lands, waits, starts, after, name):
    k_n = len(lands)
    plan_sizes = dict(relay_halves=2, others=3, relays=2, near=2, far=1)
    n_in, n_out = 2 * len(waits), 2 * len(starts)

    def body(*refs):
        land = refs[:k_n]
        in_sems = refs[k_n:k_n + n_in]
        out_sems, token = refs[len(refs) - 1 - n_out:len(refs) - 1], refs[-1]
        for w, (stage, _, _) in enumerate(waits):
            _ag_stage(land, stage, in_sems[2 * w], in_sems[2 * w + 1], "wait")
            if w < len(starts):
                _ag_stage(land, starts[w], out_sems[2 * w], out_sems[2 * w + 1], "start")
        token[...] = jnp.zeros_like(token)

    outs = pl.pallas_call(
        body,
        name=name,
        in_specs=[HBM_SPEC] * k_n + [SEM_SPEC] * n_in + [ANY_SPEC],
        out_specs=(*[HBM_SPEC] * k_n, *[SEM_SPEC] * n_out, pl.BlockSpec(memory_space=pltpu.VMEM)),
        out_shape=(*[pltpu.HBM(a.shape, a.dtype) for a in lands], *_sem_shapes(*[plan_sizes[s] * k_n for s in starts]),
                   jax.ShapeDtypeStruct((8, LANES), F32)),
        input_output_aliases={k: k for k in range(k_n)},
        compiler_params=pltpu.CompilerParams(has_side_effects=DATAFLOW),
    )(*lands, *[s for _, a, b in waits for s in (a, b)], after)
    return list(outs[:k_n]), list(outs[k_n:k_n + n_out]), outs[-1]


def _ag_mid(lands, send_sems, recv_sems, after, name):
    k_n = len(lands)

    def body(*refs):
        land = refs[:k_n]
        send1, recv1 = refs[k_n], refs[k_n + 1]
        fwd_send, fwd_recv = refs[-2], refs[-1]
        x, y, c, chips = _place()
        sources = [(x, y, 1 - c)] + [(*chip, c) for chip in chips]
        for k in range(k_n):
            mine = land[k].at[_dev_index(x, y, c)]
            for j, frm in enumerate(sources):
                got = land[k].at[_dev_index(*frm)]
                cp = pltpu.make_async_remote_copy(
                    src_ref=mine, dst_ref=got, send_sem=send1.at[4 * k + j], recv_sem=recv1.at[4 * k + j], device_id=frm, device_id_type=MESH)
                cp.wait_send()
                cp.wait_recv()
                if j >= 1:
                    pltpu.make_async_remote_copy(
                        src_ref=got, dst_ref=got, send_sem=fwd_send.at[3 * k + j - 1], recv_sem=fwd_recv.at[3 * k + j - 1],
                        device_id=(x, y, 1 - c), device_id_type=MESH).start()

    outs = pl.pallas_call(
        body,
        name=name,
        in_specs=[HBM_SPEC] * k_n + [SEM_SPEC, SEM_SPEC, ANY_SPEC],
        out_specs=(*[HBM_SPEC] * k_n, SEM_SPEC, SEM_SPEC),
        out_shape=(*[pltpu.HBM(a.shape, a.dtype) for a in lands], pltpu.SemaphoreType.DMA((3 * k_n,)), pltpu.SemaphoreType.DMA((3 * k_n,))),
        input_output_aliases={k: k for k in range(k_n)},
        compiler_params=pltpu.CompilerParams(has_side_effects=DATAFLOW),
    )(*lands, send_sems, recv_sems, after)
    return list(outs[:k_n]), outs[-2], outs[-1]


def _ag_end(lands, fwd_send, fwd_recv, name):
    k_n = len(lands)

    def body(*refs):
        land = refs[:k_n]
        fsend, frecv = refs[k_n], refs[k_n + 1]
        x, y, c, chips = _place()
        for k in range(k_n):
            for j, chip in enumerate(chips):
                cp = pltpu.make_async_remote_copy(
                    src_ref=land[k].at[_dev_index(*chip, c)], dst_ref=land[k].at[_dev_index(*chip, 1 - c)],
                    send_sem=fsend.at[3 * k + j], recv_sem=frecv.at[3 * k + j], device_id=(x, y, 1 - c), device_id_type=MESH)
                cp.wait_send()
                cp.wait_recv()

    outs = pl.pallas_call(
        body,
        name=name,
        in_specs=[HBM_SPEC] * k_n + [SEM_SPEC, SEM_SPEC],
        out_specs=tuple([HBM_SPEC] * k_n),
        out_shape=tuple(pltpu.HBM(a.shape, a.dtype) for a in lands),
        input_output_aliases={k: k for k in range(k_n)},
        compiler_params=pltpu.CompilerParams(has_side_effects=DATAFLOW),
    )(*lands, fwd_send, fwd_recv)
    return list(outs)


def _chips4():
    x, y, c, others = _place()
    return x, y, c, [(x, y)] + others


def _route_sibling(j):
    x, y, c, chips = _chips4()
    return _dev_index(*chips[j], 1 - c), j, (x, y, 1 - c)


def _route_chips(j):
    x, y, c, chips = _chips4()
    return j, j, (*chips[j + 1], c)


def _route_sibling_whole(j):
    x, y, c, _ = _chips4()
    return 0, 0, (x, y, 1 - c)


def _route_chips_whole(j):
    x, y, c, chips = _chips4()
    return 0, j, (*chips[j + 1], c)


def _route_everyone(j):
    x, y, c, _ = _chips4()
    flip = [(j + 1) >> 2 & 1, (j + 1) >> 1 & 1, (j + 1) & 1]
    return 0, j, tuple(1 - v if f else v for v, f in zip((x, y, c), flip))


def _xchg_copies(routes, src, dst, send_sems, recv_sems):
    copies, sem = [], 0
    for k, (route, n) in enumerate(routes):
        for j in range(n):
            si, di, peer = route(j)
            copies.append(pltpu.make_async_remote_copy(
                src_ref=src[k].at[si], dst_ref=dst[k].at[di], send_sem=send_sems.at[sem], recv_sem=recv_sems.at[sem],
                device_id=peer, device_id_type=MESH))
            sem += 1
    return copies


def _xchg_start(srcs, slot_shapes, routes, name):
    k_n = len(srcs)
    n_sem = sum(n for _, n in routes)
    dsts = [lax.empty((n,) + tuple(sh), a.dtype) for sh, a, (_, n) in zip(slot_shapes, srcs, routes)]

    def body(*refs):
        src, dst = refs[:k_n], refs[k_n:2 * k_n]
        send_sems, recv_sems, token = refs[2 * k_n], refs[2 * k_n + 1], refs[-1]
        for cp in _xchg_copies(routes, src, dst, send_sems, recv_sems):
            cp.start()
        token[...] = jnp.zeros_like(token)

    arrays = list(srcs) + dsts
    outs = pl.pallas_call(
        body,
        name=name,
        in_specs=[HBM_SPEC] * (2 * k_n),
        out_specs=(SEM_SPEC, SEM_SPEC, *[HBM_SPEC] * (2 * k_n), pl.BlockSpec(memory_space=pltpu.VMEM)),
        out_shape=(pltpu.SemaphoreType.DMA((n_sem,)), pltpu.SemaphoreType.DMA((n_sem,)),
                   *[pltpu.HBM(a.shape, a.dtype) for a in arrays], jax.ShapeDtypeStruct((8, LANES), F32)),
        input_output_aliases={i: 2 + i for i in range(2 * k_n)},
        compiler_params=pltpu.CompilerParams(has_side_effects=DATAFLOW),
    )(*[_hbm(a) for a in arrays])
    return outs[0], outs[1], list(outs[2:2 + k_n]), list(outs[2 + k_n:2 + 2 * k_n]), outs[-1]


def _xchg_wait(send_sems, recv_sems, srcs, dsts, routes, after, name):
    k_n = len(srcs)

    def body(*refs):
        src, dst = refs[:k_n], refs[k_n:2 * k_n]
        for cp in _xchg_copies(routes, src, dst, refs[2 * k_n], refs[2 * k_n + 1]):
            cp.wait_send()
            cp.wait_recv()

    arrays = list(srcs) + list(dsts)
    outs = pl.pallas_call(
        body,
        name=name,
        in_specs=[HBM_SPEC] * (2 * k_n) + [SEM_SPEC, SEM_SPEC, ANY_SPEC],
        out_specs=tuple([HBM_SPEC] * (2 * k_n)),
        out_shape=tuple(pltpu.HBM(a.shape, a.dtype) for a in arrays),
        input_output_aliases={i: i for i in range(2 * k_n)},
        compiler_params=pltpu.CompilerParams(has_side_effects=DATAFLOW),
    )(*arrays, send_sems, recv_sems, after)
    return list(outs[:k_n]), list(outs[k_n:])


SMALL = ("norm1_gain", "gmlp_v_gain", "w_spatial", "b_spatial", "attn_sinks", "rel_bias_table", "norm2_gain", "final_gain")
LANES = 128


def _swap_start(grads, smalls, tag):
    srcs = list(grads) + [a[None] for a in smalls]
    shapes = [g.shape[1:] for g in grads] + [a.shape for a in smalls]
    routes = [(_route_sibling, 4)] * len(grads) + [(_route_sibling_whole, 1)] * len(smalls)
    return _xchg_start(srcs, shapes, routes, f"rs_{tag}_swap_start"), routes, len(grads)


def _swap_sums(swap, names, after, tag, few=False, small_dtypes=None):
    (send1, recv1, src1, land1, _), routes, n_rs = swap
    x, y, c, chips = _chips4()
    blocks = jnp.stack([_dev_index(*chip, c) for chip in chips]).astype(jnp.int32)
    src1, land1 = _xchg_wait(send1, recv1, src1, land1, routes, after, f"rs_{tag}_swap_wait")
    if few:
        sums = _rs_sum_few(src1[:n_rs], land1[:n_rs], blocks, f"rs_{tag}_sum")
    else:
        sums = [_rs_sum(g, land, blocks, f"rs_sum_{n}") for g, land, n in zip(src1[:n_rs], land1[:n_rs], names)]
    if len(src1) == n_rs:
        return sums, []
    return sums, _pair_sum(src1[n_rs:], land1[n_rs:], small_dtypes or [F32] * (len(src1) - n_rs), f"rs_{tag}_sum_small")


def _pair_sum(mine, theirs, dtypes, name):
    def body(*refs):
        n = len(refs) // 3
        for k in range(n):
            refs[2 * n + k][...] = (refs[k][...] + refs[n + k][...]).astype(dtypes[k])

    return _call(
        body,
        name=name,
        out_shape=[jax.ShapeDtypeStruct(a.shape, dt) for a, dt in zip(mine, dtypes)],
        compiler_params=_params(),
    )(*mine, *theirs)


def _chips_start(sums, small_sums, tag, everyone=()):
    sends = [ps[1] for ps in sums] + list(small_sums) + [a[None] for a in everyone]
    routes = [(_route_chips, 3)] * len(sums) + [(_route_chips_whole, 3)] * len(small_sums) + [(_route_everyone, 7)] * len(everyone)
    return _xchg_start(sends, [a.shape[1:] for a in sends], routes, f"rs_{tag}_chips_start"), routes


def _chips_wait(chips, after, tag):
    (send2, recv2, src2, land2, _), routes = chips
    return _xchg_wait(send2, recv2, src2, land2, routes, after, f"rs_{tag}_chips_wait")


def kernel(x, p, norm1_gain, w_in, gmlp_v_gain, w_spatial, b_spatial, attn_sinks, rel_bias_table, w_out, norm2_gain, w_ff1, w_ff2, w_ple_proj, w_ple_gate, final_gain, loss_target, m_norm1_gain, m_w_in, m_gmlp_v_gain, m_w_spatial, m_b_spatial, m_attn_sinks, m_rel_bias_table, m_w_out, m_norm2_gain, m_w_ff1, m_w_ff2, m_w_ple_proj, m_w_ple_gate, m_final_gain, v_norm1_gain, v_w_in, v_gmlp_v_gain, v_w_spatial, v_b_spatial, v_attn_sinks, v_rel_bias_table, v_w_out, v_norm2_gain, v_w_ff1, v_w_ff2, v_w_ple_proj, v_w_ple_gate, v_final_gain):
    args = dict(locals())
    s = x.shape[1]
    big = ("w_in", "w_out", "w_ff1", "w_ff2", "w_ple_proj", "w_ple_gate")

    x2, p2, t2 = x.reshape(s, D_MODEL), p.reshape(s, PLE_DIM), loss_target.reshape(s, D_MODEL)
    g1, gv, w_sp, b_sp, sinks, table, g2, gf = (args[n] for n in SMALL)
    bucket = jnp.asarray(_bucket_table())
    b_col = b_sp.reshape(GROUPS, BLOCK, 1)

    def shard(name):
        return args[name][0].T if name.endswith("w_in") else args[name][0]

    lands = _prep_weights([shard(n) for n in big])
    (send_in, recv_in, fly_in), (rest_sems, fly_rest), token = _ag_start(lands[:1], lands[1:])
    mid_in, fwd_send_in, fwd_recv_in = _ag_mid(fly_in, send_in, recv_in, token, "ag_mid_w_in")
    g_in = _ag_end(mid_in, fwd_send_in, fwd_recv_in, "ag_end_w_in")[0]
    full_in = g_in.reshape(D_IN, D_MODEL)

    zuv, qkv, hn1 = _in_proj(x2, g1, full_in)
    fly_rest, relay_sems, relayed = _ag_split_call(
        fly_rest, [("relay_halves", *rest_sems[:2])], ["relays"], zuv, "ag_relay")
    mix, *saved = _mixer_fwd(zuv, qkv, gv, w_sp[0], b_col, sinks, table.T, bucket, after=relayed)
    fly_rest, fwd_sems, _ = _ag_split_call(
        fly_rest, [("others", *rest_sems[2:]), ("relays", *relay_sems)], ["near", "far"], mix, "ag_mid_rest")
    g_out, g_ff1, g_ff2, g_proj, g_gate = _ag_split_call(
        fly_rest, [("near", *fwd_sems[:2]), ("far", *fwd_sems[2:])], [], mix, "ag_end_rest")[0]
    full_out, full_ff2, full_gate = g_out.reshape(D_MODEL, D_MODEL), g_ff2.reshape(D_FF, D_MODEL), g_gate.reshape(D_MODEL, D_MODEL)
    full_proj = g_proj.transpose(1, 0, 2).reshape(PLE_DIM, D_MODEL)

    (tail_small, dh1, dh1b, dmix, hn2_t, a_t, df, h2, dh2, dgl, dpp) = _tail(
        x2, mix, p2, t2, g2, gf.reshape(1, D_MODEL), full_out, g_ff1, full_ff2, full_gate, full_proj)

    ffn_names, mid_names = ("w_ff1", "w_ff2"), ("w_out", "w_ple_proj", "w_ple_gate")
    gw_ff1 = _wgrad(hn2_t, df, D_MODEL, 2 * D_FF // N_DEV, "wgrad_ff1", peer_cols=D_FF // N_DEV, a_transposed=True)
    gw_ff2 = _wgrad(a_t, dh2, 1024, D_MODEL, "wgrad_ff2", a_transposed=True).reshape(N_DEV, D_FF // N_DEV, D_MODEL)
    ffn_swap = _swap_start([gw_ff1, gw_ff2], [], "ffn")
    gw_gate = _wgrad(h2, dgl, D_MODEL, 512, "wgrad_gate", after=ffn_swap[0][4]).reshape(N_DEV, D_MODEL // N_DEV, D_MODEL)
    gw_proj = _wgrad(p2, dpp, PLE_DIM, D_MODEL // 2, "wgrad_proj", peer_cols=D_MODEL // N_DEV, after=gw_gate)
    gw_out = _wgrad(mix, dh1b, D_MODEL, 512, "wgrad_out", after=gw_proj).reshape(N_DEV, D_MODEL // N_DEV, D_MODEL)
    mid_swap = _swap_start([gw_out, gw_proj, gw_gate], [tail_small], "mid")
    ffn_sums, _ = _swap_sums(ffn_swap, ffn_names, mid_swap[0][4], "ffn")
    ffn_chips = _chips_start(ffn_sums, [], "ffn")

    dzm, dkv, d_gv, d_wsp, d_bsp, d_attn = _mixer_bwd(
        zuv, qkv, dmix, saved, gv, w_sp[0], b_col, bucket, after=ffn_chips[0][4])
    mid_sums, mid_small = _swap_sums(mid_swap, mid_names, dzm, "mid", few=True)
    mid_chips = _chips_start(mid_sums, mid_small, "mid")
    gw_in = _wgrad_in(dzm, dkv, hn1, after=mid_chips[0][4]).reshape(N_DEV, D_IN // N_DEV, D_MODEL)
    in_swap = _swap_start([gw_in], [d_gv, d_wsp, d_bsp, d_attn], "in")
    dx, d_g1 = _in_bwd(x2, dh1, dzm, dkv, g1, full_in, after=in_swap[0][4])
    grad_x = dx.reshape(x.shape)
    in_sums, in_small = _swap_sums(in_swap, ("w_in",), dx, "in", small_dtypes=[F32, BF, F32, F32])
    in_chips = _chips_start(in_sums, in_small, "in", everyone=[d_g1])

    grads, deltas, new_m, new_v = {}, {}, {}, {}
    after, small_own, small_recv = in_chips[0][4], [], []
    for chips, sums, names, tag in ((ffn_chips, ffn_sums, ffn_names, "ffn"), (mid_chips, mid_sums, mid_names, "mid"),
                                    (in_chips, in_sums, ("w_in",), "in")):
        sent, recvs = _chips_wait(chips, after, tag)
        small_own += sent[len(names):]
        small_recv += recvs[len(names):]
        if tag == "mid":
            done = _final_adamw_few([ps[0] for ps in sums], recvs[:len(names)], [shard(n) for n in names],
                                    [shard("m_" + n) for n in names], [shard("v_" + n) for n in names], "adamw_mid")
            done = list(zip(*done))
        else:
            done = [_final_adamw(part, recv, shard(n), shard("m_" + n), shard("v_" + n), "adamw_" + n)
                    for n, (part, _), recv in zip(names, sums, recvs)]
        for n, results in zip(names, done):
            for dst, arr in zip((grads, deltas, new_m, new_v), results):
                dst[n] = (arr.T if n == "w_in" else arr)[None]
            after = results[1]

    views = {"w_spatial": (GROUPS, BLOCK, BLOCK), "b_spatial": (GROUPS, BLOCK), "final_gain": (1, D_MODEL)}
    def view(name, base):
        return args[name].T if base == "rel_bias_table" else args[name].reshape(views.get(base, args[name].shape))

    small_in = [[view(pre + n, n) for n in SMALL] for pre in ("", "m_", "v_")]
    loss, *small_out = _adamw_small(small_own[:-1], small_recv[:-1], small_own[-1], small_recv[-1], *small_in, after=after)
    for dst, arrays in zip((grads, deltas, new_m, new_v), small_out):
        for n, arr in zip(SMALL, arrays):
            dst[n] = arr.T if n == "rel_bias_table" else arr.reshape(args[n].shape)
    loss = loss[0, 0]

    order = ("norm1_gain", "w_in", "gmlp_v_gain", "w_spatial", "b_spatial", "attn_sinks", "rel_bias_table", "w_out",
             "norm2_gain", "w_ff1", "w_ff2", "w_ple_proj", "w_ple_gate", "final_gain")
    return (loss, grad_x, *[grads[n] for n in order], *[deltas[n] for n in order],
            *[new_m[n] for n in order], *[new_v[n] for n in order])
```

```python
import math

import numpy as np
import jax
import jax.numpy as jnp
from jax import lax
from jax.experimental import pallas as pl
from jax.experimental.pallas import tpu as pltpu

F32 = jnp.float32
BF = jnp.bfloat16
MESH = pl.DeviceIdType.MESH
N_DEV = 8

D_MODEL = 1024
PLE_DIM = 256
D_GMLP = 512
GROUPS = 4
GDIM = 128
BLOCK = 128
D_ATTN = 512
HEAD_DIM = 64
N_Q = 8
Q_PER_KV = 4
N_KV = N_Q // Q_PER_KV
ROWS4 = Q_PER_KV * BLOCK
D_KV = 128
D_FF = 4096
D_IN = 1792
D_MAIN = 2 * D_GMLP + D_ATTN
REL_BUCKETS = 32
EPS = 1e-6
NEG_INF = -1e30
SCALE = HEAD_DIM ** -0.5
GELU_C = math.sqrt(2.0 / math.pi)
GELU_A = 0.044715

ADAM_LR = 0.001
ADAM_B1 = 0.9
ADAM_B2 = 0.999
ADAM_EPS = 1e-08
ADAM_WD = 0.01
ADAM_STEP = 10

V7X_VMEM_LIMIT = 60000 * 1024
TOK_TILE = 256
IO_TOK_TILE = 512


def _call(body, after=None, **kw):
    if after is None:
        return pl.pallas_call(body, **kw)
    n_in = len(kw["in_specs"])

    def ordered(*refs):
        body(*refs[:n_in], *refs[n_in + 1:])

    kw["in_specs"] = list(kw["in_specs"]) + [pl.BlockSpec(memory_space=pl.ANY)]
    fn = pl.pallas_call(ordered, **kw)
    return lambda *operands: fn(*operands, after)


def _params(sem=None):
    if sem is None:
        return pltpu.CompilerParams(vmem_limit_bytes=V7X_VMEM_LIMIT)
    return pltpu.CompilerParams(dimension_semantics=sem, vmem_limit_bytes=V7X_VMEM_LIMIT)


def _nn(a, b):
    return jnp.dot(a, b, preferred_element_type=F32)


def _nt(a, b):
    return lax.dot_general(a, b, (((1,), (1,)), ((), ())), preferred_element_type=F32)


def _tn(a, b):
    return lax.dot_general(a, b, (((0,), (0,)), ((), ())), preferred_element_type=F32)


def _gelu_tanh(x):
    return jnp.tanh(GELU_C * (x + GELU_A * (x * x * x)))


def _gelu(x, t):
    return x * (0.5 * (1.0 + t))


def _gelu_and_grad(x, t):
    cdf = 0.5 * (1.0 + t)
    return x * cdf, cdf + 0.5 * x * (1.0 - t * t) * (GELU_C * (1.0 + 3.0 * GELU_A * (x * x)))


def _rms_scale(x):
    return lax.rsqrt(jnp.mean(x * x, axis=-1, keepdims=True) + EPS)


def _rms_bwd(dxn, x, r):
    return r * dxn - x * ((r * r * r) * jnp.mean(dxn * x, axis=-1, keepdims=True))


def _bucket_table():
    a = np.arange(BLOCK)[:, None]
    j = np.arange(2 * BLOCK)[None, :]
    n = BLOCK + a - j
    valid = (n >= 0) & (n < BLOCK)
    nc = np.maximum(n, 0)
    max_exact = REL_BUCKETS // 2
    nf = np.maximum(nc, 1).astype(np.float32)
    large = max_exact + (
        np.log(nf / np.float32(max_exact)) / np.float32(math.log(BLOCK / max_exact)) * np.float32(REL_BUCKETS - max_exact)
    ).astype(np.int32)
    large = np.minimum(large, REL_BUCKETS - 1)
    bucket = np.where(nc < max_exact, nc, large)
    return np.where(valid, bucket, -1).astype(np.int32)


def _in_proj(x, g1, w_in_t):
    s = x.shape[0]
    tm = min(IO_TOK_TILE, s)

    def body(x_ref, g_ref, w_ref, zuv_ref, qkv_ref, hn_ref):
        xv = x_ref[...]
        hn = ((xv * _rms_scale(xv)) * g_ref[...]).astype(BF)
        hn_ref[...] = hn
        z = _nt(hn, w_ref[...])
        zuv_ref[...] = z[:, : 2 * D_GMLP]
        qkv_ref[...] = z[:, 2 * D_GMLP:].astype(BF)

    return _call(
        body,
        name="in_proj",
        grid=(s // tm,),
        in_specs=[
            pl.BlockSpec((tm, D_MODEL), lambda i: (i, 0)),
            pl.BlockSpec((1, D_MODEL), lambda i: (0, 0)),
            pl.BlockSpec((D_IN, D_MODEL), lambda i: (0, 0)),
        ],
        out_specs=[
            pl.BlockSpec((tm, 2 * D_GMLP), lambda i: (i, 0)),
            pl.BlockSpec((tm, D_ATTN + 2 * D_KV), lambda i: (i, 0)),
            pl.BlockSpec((tm, D_MODEL), lambda i: (i, 0)),
        ],
        out_shape=[
            jax.ShapeDtypeStruct((s, 2 * D_GMLP), F32),
            jax.ShapeDtypeStruct((s, D_ATTN + 2 * D_KV), BF),
            jax.ShapeDtypeStruct((s, D_MODEL), BF),
        ],
        compiler_params=_params(("arbitrary",)),
    )(x, g1, w_in_t)


def _head_rows(h):
    kh, g = divmod(h, Q_PER_KV)
    return kh, slice(g * BLOCK, (g + 1) * BLOCK)


def _build_bias(bias_ref, bucket_ref, table_ref):
    bucket = bucket_ref[...]
    for h in range(N_Q):
        acc = jnp.zeros((BLOCK, 2 * BLOCK), F32)
        for b in range(REL_BUCKETS):
            acc = jnp.where(bucket == b, table_ref[h, b], acc)
        kh, rows = _head_rows(h)
        bias_ref[kh, rows, :] = acc


def _window_masks(i):
    row = lax.broadcasted_iota(jnp.int32, (ROWS4, BLOCK), 0) & (BLOCK - 1)
    col = lax.broadcasted_iota(jnp.int32, (ROWS4, BLOCK), 1)
    return (col > row) & (i > 0), col <= row


def _stack_heads(ref, kh, offset):
    first = offset + kh * Q_PER_KV * HEAD_DIM
    return jnp.concatenate(
        [ref[:, first + g * HEAD_DIM: first + (g + 1) * HEAD_DIM].astype(BF) for g in range(Q_PER_KV)], axis=0)


def _stack_sinks(sink_ref, kh):
    return jnp.concatenate([jnp.full((BLOCK, 1), sink_ref[0, kh * Q_PER_KV + g], F32) for g in range(Q_PER_KV)], axis=0)


def _tril_bf16(w_ref, g):
    row = lax.broadcasted_iota(jnp.int32, (BLOCK, BLOCK), 0)
    col = lax.broadcasted_iota(jnp.int32, (BLOCK, BLOCK), 1)
    return jnp.where(col <= row, w_ref[g], 0.0).astype(BF)


def _attn_probs(q_h, k_prev, k_cur, bias_h, sink, valid_prev, valid_cur):
    l_prev = jnp.where(valid_prev, _nt(q_h, k_prev) * SCALE + bias_h[:, :BLOCK], NEG_INF)
    l_cur = jnp.where(valid_cur, _nt(q_h, k_cur) * SCALE + bias_h[:, BLOCK:], NEG_INF)
    m = jnp.maximum(jnp.maximum(jnp.max(l_prev, axis=-1, keepdims=True), jnp.max(l_cur, axis=-1, keepdims=True)), sink)
    e_prev = jnp.exp(l_prev - m)
    e_cur = jnp.exp(l_cur - m)
    e_sink = jnp.exp(sink - m)
    denom = jnp.sum(e_prev, axis=-1, keepdims=True) + jnp.sum(e_cur, axis=-1, keepdims=True) + e_sink
    return e_prev / denom, e_cur / denom, e_sink / denom


def _mixer_specs(nb):
    cl = lambda i: jnp.minimum(i, nb - 1)
    return [
        pl.BlockSpec((BLOCK, 2 * D_GMLP), lambda i: (cl(i), 0)),
        pl.BlockSpec((BLOCK, D_ATTN), lambda i: (cl(i), 0)),
        pl.BlockSpec((BLOCK, 2 * D_KV), lambda i: (cl(i), D_ATTN // (2 * D_KV))),
        pl.BlockSpec((BLOCK, 2 * D_KV), lambda i: (jnp.maximum(cl(i) - 1, 0), D_ATTN // (2 * D_KV))),
        pl.BlockSpec((1, D_GMLP), lambda i: (0, 0)),
        pl.BlockSpec((GROUPS, BLOCK, BLOCK), lambda i: (0, 0, 0)),
        pl.BlockSpec((GROUPS, BLOCK, 1), lambda i: (0, 0, 0)),
        pl.BlockSpec(memory_space=pltpu.SMEM),
        pl.BlockSpec(memory_space=pltpu.SMEM),
        pl.BlockSpec((BLOCK, 2 * BLOCK), lambda i: (0, 0)),
    ]


def _mixer_fwd(zuv, qkv, gv, w_sp, b_sp, sinks, table, bucket, after=None):
    s = zuv.shape[0]
    nb = s // BLOCK

    def body(zuv_ref, q_ref, kvc_ref, kvp_ref, gv_ref, w_ref, b_ref, sink_ref, table_ref, bucket_ref,
             mix_ref, tanh_ref, prob_ref, psink_ref, bias_ref):
        i = pl.program_id(0)

        @pl.when(i == 0)
        def _():
            _build_bias(bias_ref, bucket_ref, table_ref)

        t = _gelu_tanh(zuv_ref[...])
        tanh_ref[...] = t
        u = _gelu(zuv_ref[:, :D_GMLP], t[:, :D_GMLP])
        vg = _gelu(zuv_ref[:, D_GMLP:], t[:, D_GMLP:])
        for g in range(GROUPS):
            sl = slice(g * GDIM, (g + 1) * GDIM)
            vg_g = vg[:, sl]
            vn = ((vg_g * _rms_scale(vg_g)) * gv_ref[:, sl]).astype(BF)
            sv = _nn(_tril_bf16(w_ref, g), vn) + b_ref[g]
            mix_ref[:, sl] = (u[:, sl] * sv).astype(BF)

        valid_prev, valid_cur = _window_masks(i)
        for kh in range(N_KV):
            ksl = slice(kh * HEAD_DIM, (kh + 1) * HEAD_DIM)
            vsl = slice(D_KV + kh * HEAD_DIM, D_KV + (kh + 1) * HEAD_DIM)
            q4 = _stack_heads(q_ref, kh, 0)
            p_prev, p_cur, p_sink = _attn_probs(
                q4, kvp_ref[:, ksl], kvc_ref[:, ksl], bias_ref[kh], _stack_sinks(sink_ref, kh), valid_prev, valid_cur)
            prob_ref[0, kh, :, :BLOCK] = p_prev
            prob_ref[0, kh, :, BLOCK:] = p_cur
            psink_ref[0, kh] = jnp.broadcast_to(p_sink, (ROWS4, LANES))
            o4 = _nn(p_prev.astype(BF), kvp_ref[:, vsl]) + _nn(p_cur.astype(BF), kvc_ref[:, vsl])
            for g in range(Q_PER_KV):
                first = D_GMLP + (kh * Q_PER_KV + g) * HEAD_DIM
                mix_ref[:, first:first + HEAD_DIM] = o4[g * BLOCK:(g + 1) * BLOCK].astype(BF)

    return _call(
        body,
        name="mixer_fwd",
        after=after,
        grid=(nb,),
        in_specs=_mixer_specs(nb),
        out_specs=[
            pl.BlockSpec((BLOCK, D_MODEL), lambda i: (i, 0)),
            pl.BlockSpec((BLOCK, 2 * D_GMLP), lambda i: (i, 0)),
            pl.BlockSpec((1, N_KV, ROWS4, 2 * BLOCK), lambda i: (i, 0, 0, 0)),
            pl.BlockSpec((1, N_KV, ROWS4, LANES), lambda i: (i, 0, 0, 0)),
        ],
        out_shape=[
            jax.ShapeDtypeStruct((s, D_MODEL), BF),
            jax.ShapeDtypeStruct((s, 2 * D_GMLP), F32),
            jax.ShapeDtypeStruct((nb, N_KV, ROWS4, 2 * BLOCK), F32),
            jax.ShapeDtypeStruct((nb, N_KV, ROWS4, LANES), F32),
        ],
        scratch_shapes=[pltpu.VMEM((N_KV, ROWS4, 2 * BLOCK), F32)],
        compiler_params=_params(("arbitrary",)),
    )(zuv, qkv, qkv, qkv, gv, w_sp, b_sp, sinks, table, bucket)


def _tail(x, mix, p, t, g2, gf, w_out, w_ff1, w_ff2, w_gate, w_proj):
    s = x.shape[0]
    tm = min(TOK_TILE, s)
    n_ff = w_ff1.shape[0]
    fc = D_FF // n_ff

    def body(x_ref, mix_ref, p_ref, t_ref, g2_ref, gf_ref, wo_ref, w1_ref, w2_ref, wg_ref, wp_ref,
             small_ref, dh1_ref, dh1b_ref, dmix_ref, hn2_ref, a_ref, df_ref, h2_ref, dh2_ref, dgl_ref, dpp_ref, f_ref):
        i = pl.program_id(0)

        @pl.when(i == 0)
        def _():
            small_ref[...] = jnp.zeros_like(small_ref)

        h1 = x_ref[...] + _nn(mix_ref[...], wo_ref[...])
        r2 = _rms_scale(h1)
        hn2 = ((h1 * r2) * g2_ref[...]).astype(BF)
        hn2_ref[...] = hn2
        h2 = h1
        for c in range(n_ff):
            f = _nn(hn2, w1_ref[c])
            f_ref[:, c * fc:(c + 1) * fc] = f
            a = jnp.square(jnp.maximum(f, 0.0)).astype(BF)
            a_ref[:, c * fc:(c + 1) * fc] = a
            h2 = h2 + _nn(a, w2_ref[c * fc:(c + 1) * fc, :])
        h2b = h2.astype(BF)
        h2_ref[...] = h2b
        gate = jax.nn.sigmoid(_nn(h2b, wg_ref[...]))
        pp = _nn(p_ref[...].astype(BF), wp_ref[...])
        h3 = h2 + gate * pp
        rf = _rms_scale(h3)
        gf_v = gf_ref[...]
        err = (h3 * rf) * gf_v - t_ref[...]
        small_ref[2:3, :] += jnp.sum(jnp.sum(err * err, axis=-1, keepdims=True), axis=0, keepdims=True) * (0.5 / D_MODEL)

        dy = err * (1.0 / D_MODEL)
        small_ref[1:2, :] += jnp.sum(dy * (h3 * rf), axis=0, keepdims=True)
        dh3 = _rms_bwd(dy * gf_v, h3, rf)
        dpp_ref[...] = (dh3 * gate).astype(BF)
        dgl = ((dh3 * pp) * (gate * (1.0 - gate))).astype(BF)
        dgl_ref[...] = dgl
        dh2 = dh3 + _nt(dgl, wg_ref[...])
        dh2b = dh2.astype(BF)
        dh2_ref[...] = dh2b
        dhn2 = jnp.zeros((tm, D_MODEL), F32)
        for c in range(n_ff):
            da = _nt(dh2b, w2_ref[c * fc:(c + 1) * fc, :])
            df = (da * (2.0 * jnp.maximum(f_ref[:, c * fc:(c + 1) * fc], 0.0))).astype(BF)
            df_ref[:, c * fc:(c + 1) * fc] = df
            dhn2 = dhn2 + _nt(df, w1_ref[c])
        small_ref[0:1, :] += jnp.sum(dhn2 * (h1 * r2), axis=0, keepdims=True)
        dh1 = dh2 + _rms_bwd(dhn2 * g2_ref[...], h1, r2)
        dh1_ref[...] = dh1
        dh1b = dh1.astype(BF)
        dh1b_ref[...] = dh1b
        dmix_ref[...] = _nt(dh1b, wo_ref[...])

    tile = lambda cols: pl.BlockSpec((tm, cols), lambda i: (i, 0))
    whole = lambda shape: pl.BlockSpec(shape, lambda i: (0,) * len(shape), pipeline_mode=pl.Buffered(1))
    row = pl.BlockSpec((1, D_MODEL), lambda i: (0, 0))
    act = lambda cols, dt: jax.ShapeDtypeStruct((s, cols), dt)
    return _call(
        body,
        name="tail",
        grid=(s // tm,),
        in_specs=[tile(D_MODEL), tile(D_MODEL), tile(PLE_DIM), tile(D_MODEL), row, row,
                  whole(w_out.shape), whole(w_ff1.shape), whole(w_ff2.shape), whole(w_gate.shape), whole(w_proj.shape)],
        out_specs=[pl.BlockSpec((8, D_MODEL), lambda i: (0, 0)), tile(D_MODEL), tile(D_MODEL), tile(D_MODEL), tile(D_MODEL), tile(D_FF),
                   tile(D_FF), tile(D_MODEL), tile(D_MODEL), tile(D_MODEL), tile(D_MODEL)],
        out_shape=[jax.ShapeDtypeStruct((8, D_MODEL), F32),
                   act(D_MODEL, F32), act(D_MODEL, BF), act(D_MODEL, F32), act(D_MODEL, BF), act(D_FF, BF), act(D_FF, BF), act(D_MODEL, BF),
                   act(D_MODEL, BF), act(D_MODEL, BF), act(D_MODEL, BF)],
        scratch_shapes=[pltpu.VMEM((tm, D_FF), F32)],
        compiler_params=_params(("arbitrary",)),
    )(x, mix, p, t, g2, gf, w_out, w_ff1, w_ff2, w_gate, w_proj)


def _mixer_bwd(zuv, qkv, dmix, saved, gv, w_sp, b_sp, bucket, after=None):
    s = zuv.shape[0]
    nb = s // BLOCK

    def body(zuv_ref, q_ref, kvc_ref, kvp_ref, gv_ref, w_ref, b_ref, bucket_ref, dmix_ref, tanh_ref, prob_ref, psink_ref,
             dzm_ref, dkv_ref, dgv_ref, dw_ref, db_ref, dattn_ref,
             dbias_ref, carry_ref, dsink_acc, db_acc):
        i = pl.program_id(0)

        @pl.when(i == 0)
        def _():
            dbias_ref[...] = jnp.zeros_like(dbias_ref)
            carry_ref[...] = jnp.zeros_like(carry_ref)
            dsink_acc[...] = jnp.zeros_like(dsink_acc)
            dgv_ref[...] = jnp.zeros_like(dgv_ref)
            dw_ref[...] = jnp.zeros_like(dw_ref)
            db_acc[...] = jnp.zeros_like(db_acc)

        @pl.when(i < nb)
        def _():
            u, du_dz = _gelu_and_grad(zuv_ref[:, :D_GMLP], tanh_ref[:, :D_GMLP])
            vg, dvg_dz = _gelu_and_grad(zuv_ref[:, D_GMLP:], tanh_ref[:, D_GMLP:])
            for g in range(GROUPS):
                sl = slice(g * GDIM, (g + 1) * GDIM)
                vg_g = vg[:, sl]
                rg = _rms_scale(vg_g)
                vhat = vg_g * rg
                gain = gv_ref[:, sl]
                vn = (vhat * gain).astype(BF)
                w_g = _tril_bf16(w_ref, g)
                sv = _nn(w_g, vn) + b_ref[g]
                dmix_a = dmix_ref[:, sl]
                dsv = dmix_a * u[:, sl]
                dsvb = dsv.astype(BF)
                db_acc[g] += jnp.sum(dsv, axis=-1, keepdims=True)
                dw_ref[g] += _nt(dsvb, vn)
                dvn = _tn(w_g, dsvb)
                dgv_ref[:, sl] += jnp.sum(dvn * vhat, axis=0, keepdims=True)
                dvg = _rms_bwd(dvn * gain, vg_g, rg)
                dzm_ref[:, sl] = ((dmix_a * sv) * du_dz[:, sl]).astype(BF)
                dzm_ref[:, D_GMLP + g * GDIM: D_GMLP + (g + 1) * GDIM] = (dvg * dvg_dz[:, sl]).astype(BF)

            for kh in range(N_KV):
                ksl = slice(kh * HEAD_DIM, (kh + 1) * HEAD_DIM)
                vsl = slice(D_KV + kh * HEAD_DIM, D_KV + (kh + 1) * HEAD_DIM)
                k_prev, k_cur = kvp_ref[:, ksl], kvc_ref[:, ksl]
                v_prev, v_cur = kvp_ref[:, vsl], kvc_ref[:, vsl]
                q4 = _stack_heads(q_ref, kh, 0)
                p_prev, p_cur, p_sink = prob_ref[0, kh, :, :BLOCK], prob_ref[0, kh, :, BLOCK:], psink_ref[0, kh, :, 0:1]
                do4 = _stack_heads(dmix_ref, kh, D_GMLP)
                dp_prev = _nt(do4, v_prev)
                dp_cur = _nt(do4, v_cur)
                delta = jnp.sum(p_prev * dp_prev, axis=-1, keepdims=True) + jnp.sum(p_cur * dp_cur, axis=-1, keepdims=True)
                ds_prev = p_prev * (dp_prev - delta)
                ds_cur = p_cur * (dp_cur - delta)
                dsink_acc[kh] -= p_sink * delta
                dbias_ref[kh, :, :BLOCK] += ds_prev
                dbias_ref[kh, :, BLOCK:] += ds_cur
                dsb_prev = ds_prev.astype(BF)
                dsb_cur = ds_cur.astype(BF)
                dq4 = (_nn(dsb_prev, k_prev) + _nn(dsb_cur, k_cur)) * SCALE
                for g in range(Q_PER_KV):
                    first = 2 * D_GMLP + (kh * Q_PER_KV + g) * HEAD_DIM
                    dzm_ref[:, first:first + HEAD_DIM] = dq4[g * BLOCK:(g + 1) * BLOCK].astype(BF)
                dkv_ref[:, ksl] = (carry_ref[:, ksl] + _tn(dsb_prev, q4) * SCALE).astype(BF)
                dkv_ref[:, vsl] = (carry_ref[:, vsl] + _tn(p_prev.astype(BF), do4)).astype(BF)
                carry_ref[:, ksl] = _tn(dsb_cur, q4) * SCALE
                carry_ref[:, vsl] = _tn(p_cur.astype(BF), do4)

        @pl.when(i == nb)
        def _():
            dkv_ref[...] = carry_ref[...].astype(BF)
            row = lax.broadcasted_iota(jnp.int32, (BLOCK, BLOCK), 0)
            col = lax.broadcasted_iota(jnp.int32, (BLOCK, BLOCK), 1)
            for g in range(GROUPS):
                dw_ref[g] = jnp.where(col <= row, dw_ref[g], 0.0)
                db_ref[g:g + 1, :] = jnp.sum(jnp.where(col == row, db_acc[g], 0.0), axis=0, keepdims=True)
            bucket = bucket_ref[...]
            for b in range(N_Q, REL_BUCKETS):
                dattn_ref[N_Q, b] = 0.0
            for h in range(N_Q):
                kh, rows = _head_rows(h)
                dattn_ref[N_Q, h] = jnp.sum(dsink_acc[kh, rows, :])
                dbh = dbias_ref[kh, rows, :]
                for b in range(REL_BUCKETS):
                    dattn_ref[h, b] = jnp.sum(jnp.where(bucket == b, dbh, 0.0))

    cl = lambda i: jnp.minimum(i, nb - 1)
    const = lambda shape: pl.BlockSpec(shape, lambda i: (0,) * len(shape))
    return _call(
        body,
        name="mixer_bwd",
        after=after,
        grid=(nb + 1,),
        in_specs=_mixer_specs(nb)[:7] + [
            const((BLOCK, 2 * BLOCK)),
            pl.BlockSpec((BLOCK, D_MODEL), lambda i: (cl(i), 0)),
            pl.BlockSpec((BLOCK, 2 * D_GMLP), lambda i: (cl(i), 0)),
            pl.BlockSpec((1, N_KV, ROWS4, 2 * BLOCK), lambda i: (cl(i), 0, 0, 0)),
            pl.BlockSpec((1, N_KV, ROWS4, LANES), lambda i: (cl(i), 0, 0, 0)),
        ],
        out_specs=[
            pl.BlockSpec((BLOCK, D_MAIN), lambda i: (cl(i), 0)),
            pl.BlockSpec((BLOCK, 2 * D_KV), lambda i: (jnp.maximum(i - 1, 0), 0)),
            const((1, D_GMLP)),
            const((GROUPS, BLOCK, BLOCK)),
            const((GROUPS, BLOCK)),
            pl.BlockSpec(memory_space=pltpu.SMEM),
        ],
        out_shape=[
            jax.ShapeDtypeStruct((s, D_MAIN), BF),
            jax.ShapeDtypeStruct((s, 2 * D_KV), BF),
            jax.ShapeDtypeStruct((1, D_GMLP), F32),
            jax.ShapeDtypeStruct((GROUPS, BLOCK, BLOCK), F32),
            jax.ShapeDtypeStruct((GROUPS, BLOCK), F32),
            jax.ShapeDtypeStruct((N_Q + 1, REL_BUCKETS), F32),
        ],
        scratch_shapes=[
            pltpu.VMEM((N_KV, ROWS4, 2 * BLOCK), F32),
            pltpu.VMEM((BLOCK, 2 * D_KV), F32),
            pltpu.VMEM((N_KV, ROWS4, 1), F32),
            pltpu.VMEM((GROUPS, BLOCK, 1), F32),
        ],
        compiler_params=_params(("arbitrary",)),
    )(zuv, qkv, qkv, qkv, gv, w_sp, b_sp, bucket, dmix, *saved)


def _in_bwd(x, dh1, dzm, dkv, g1, w_in_t, after=None):
    s = x.shape[0]
    tm = min(IO_TOK_TILE, s)

    def body(x_ref, dh1_ref, dzm_ref, dkv_ref, g_ref, w_ref, dx_ref, dg_ref):
        @pl.when(pl.program_id(0) == 0)
        def _():
            dg_ref[...] = jnp.zeros_like(dg_ref)

        dhn = _nn(dzm_ref[...], w_ref[:D_MAIN, :]) + _nn(dkv_ref[...], w_ref[D_MAIN:, :])
        xv = x_ref[...]
        r = _rms_scale(xv)
        dg_ref[...] += jnp.sum(dhn * (xv * r), axis=0, keepdims=True)
        dx_ref[...] = dh1_ref[...] + _rms_bwd(dhn * g_ref[...], xv, r)

    tile = lambda cols: pl.BlockSpec((tm, cols), lambda i: (i, 0))
    row = pl.BlockSpec((1, D_MODEL), lambda i: (0, 0))
    return _call(
        body,
        name="in_bwd",
        after=after,
        grid=(s // tm,),
        in_specs=[tile(D_MODEL), tile(D_MODEL), tile(D_MAIN), tile(2 * D_KV), row, pl.BlockSpec((D_IN, D_MODEL), lambda i: (0, 0))],
        out_specs=[tile(D_MODEL), row],
        out_shape=[jax.ShapeDtypeStruct((s, D_MODEL), F32), jax.ShapeDtypeStruct((1, D_MODEL), F32)],
        compiler_params=_params(("arbitrary",)),
    )(x, dh1, dzm, dkv, g1, w_in_t)


def _wgrad_in(dzm, dkv, hn, after=None):
    s = hn.shape[0]
    tm = 2 * D_KV
    n_main = D_MAIN // tm

    def body(dzm_ref, dkv_ref, hn_ref, o_ref):
        i = pl.program_id(0)

        @pl.when(i < n_main)
        def _():
            o_ref[...] = _tn(dzm_ref[...], hn_ref[...])

        @pl.when(i == n_main)
        def _():
            o_ref[...] = _tn(dkv_ref[...], hn_ref[...])

    return _call(
        body,
        name="wgrad_in",
        after=after,
        grid=(n_main + 1,),
        in_specs=[
            pl.BlockSpec((s, tm), lambda i: (0, jnp.minimum(i, n_main - 1))),
            pl.BlockSpec((s, tm), lambda i: (0, 0)),
            pl.BlockSpec((s, D_MODEL), lambda i: (0, 0)),
        ],
        out_specs=pl.BlockSpec((tm, D_MODEL), lambda i: (i, 0)),
        out_shape=jax.ShapeDtypeStruct((D_IN, D_MODEL), F32),
        compiler_params=_params(("arbitrary",)),
    )(dzm, dkv, hn)


def _wgrad(a, b, tm, tn, name, peer_cols=0, after=None):
    s, m = a.shape
    n = b.shape[1]

    def body(a_ref, b_ref, o_ref, at_ref):
        @pl.when(pl.program_id(1) == 0)
        def _():
            at_ref[...] = a_ref[...].astype(BF).T

        r = _nn(at_ref[...], b_ref[...])
        if peer_cols:
            for q in range(tn // peer_cols):
                o_ref[q] = r[:, q * peer_cols:(q + 1) * peer_cols]
        else:
            o_ref[...] = r

    if peer_cols:
        out_spec = pl.BlockSpec((tn // peer_cols, tm, peer_cols), lambda i, j: (j, i, 0))
        out_shape = jax.ShapeDtypeStruct((n // peer_cols, m, peer_cols), F32)
    else:
        out_spec = pl.BlockSpec((tm, tn), lambda i, j: (i, j))
        out_shape = jax.ShapeDtypeStruct((m, n), F32)
    return _call(
        body,
        name=name,
        after=after,
        grid=(m // tm, n // tn),
        in_specs=[pl.BlockSpec((s, tm), lambda i, j: (0, i)), pl.BlockSpec((s, tn), lambda i, j: (0, j))],
        out_specs=out_spec,
        out_shape=out_shape,
        scratch_shapes=[pltpu.VMEM((tm, s), BF)],
        compiler_params=_params(("arbitrary", "arbitrary")),
    )(a, b)


def _adamw_math(w, g, m, v):
    m_new = ADAM_B1 * m + (1.0 - ADAM_B1) * g
    v_new = ADAM_B2 * v + (1.0 - ADAM_B2) * jnp.square(g)
    m_hat = m_new / (1.0 - ADAM_B1 ** ADAM_STEP)
    v_hat = v_new / (1.0 - ADAM_B2 ** ADAM_STEP)
    delta = -ADAM_LR * (m_hat / (jnp.sqrt(v_hat) + ADAM_EPS) + ADAM_WD * w)
    return delta, m_new, v_new


def _final_adamw(part, recv, w, m, v, name):
    r, c = w.shape
    tr = min(r, 512)

    def body(p_ref, r_ref, w_ref, m_ref, v_ref, g_ref, d_ref, mo_ref, vo_ref):
        g = p_ref[...]
        for j in range(3):
            g = g + r_ref[j].astype(F32)
        g_ref[...] = g
        d_ref[...], mo_ref[...], vo_ref[...] = _adamw_math(w_ref[...], g, m_ref[...], v_ref[...])

    spec = pl.BlockSpec((tr, c), lambda i: (i, 0))
    return _call(
        body,
        name=name,
        grid=(r // tr,),
        in_specs=[spec, pl.BlockSpec((3, tr, c), lambda i: (0, i, 0)), spec, spec, spec],
        out_specs=[spec] * 4,
        out_shape=[jax.ShapeDtypeStruct((r, c), F32)] * 4,
        compiler_params=_params(("arbitrary",)),
    )(part, recv, w, m, v)


def _rs_sum(g, land, blocks, name):
    _, r, c = g.shape
    tr = min(r, 256)

    def body(blk_ref, g0_ref, g1_ref, g2_ref, g3_ref, l_ref, part_ref, send_ref):
        part_ref[...] = g0_ref[0] + l_ref[0]
        for j, gj_ref in enumerate((g1_ref, g2_ref, g3_ref)):
            send_ref[j] = (gj_ref[0] + l_ref[j + 1]).astype(BF)

    def pick(j):
        return pl.BlockSpec((1, tr, c), lambda i, blk: (blk[j], i, 0))

    return _call(
        body,
        name=name,
        grid_spec=pltpu.PrefetchScalarGridSpec(
            num_scalar_prefetch=1,
            grid=(r // tr,),
            in_specs=[pick(0), pick(1), pick(2), pick(3), pl.BlockSpec((4, tr, c), lambda i, blk: (0, i, 0))],
            out_specs=[pl.BlockSpec((tr, c), lambda i, blk: (i, 0)), pl.BlockSpec((3, tr, c), lambda i, blk: (0, i, 0))],
        ),
        out_shape=[jax.ShapeDtypeStruct((r, c), F32), jax.ShapeDtypeStruct((3, r, c), BF)],
        compiler_params=_params(("arbitrary",)),
    )(blocks, g, g, g, g, land)


def _final_adamw_few(parts, recvs, ws, ms, vs, name):
    n = len(ws)

    def body(*refs):
        ins, outs = refs[:5 * n], refs[5 * n:]
        for k in range(n):
            p_ref, r_ref, w_ref, m_ref, v_ref = (ins[q * n + k] for q in range(5))
            g = p_ref[...]
            for j in range(3):
                g = g + r_ref[j].astype(F32)
            outs[k][...] = g
            outs[n + k][...], outs[2 * n + k][...], outs[3 * n + k][...] = _adamw_math(w_ref[...], g, m_ref[...], v_ref[...])

    outs = _call(
        body,
        name=name,
        out_shape=[jax.ShapeDtypeStruct(w.shape, F32) for w in ws] * 4,
        compiler_params=_params(),
    )(*parts, *recvs, *ws, *ms, *vs)
    return outs[:n], outs[n:2 * n], outs[2 * n:3 * n], outs[3 * n:]


def _rs_sum_few(gs, lands, blocks, name):
    n = len(gs)

    def body(blk_ref, *refs):
        g_refs, l_refs, part_refs, send_refs = refs[:n], refs[n:2 * n], refs[2 * n:3 * n], refs[3 * n:]
        for k in range(n):
            part_refs[k][...] = g_refs[k][blk_ref[0]] + l_refs[k][0]
            for j in range(3):
                send_refs[k][j] = (g_refs[k][blk_ref[j + 1]] + l_refs[k][j + 1]).astype(BF)

    outs = _call(
        body,
        name=name,
        in_specs=[pl.BlockSpec(memory_space=pltpu.SMEM)] + [pl.BlockSpec(memory_space=pltpu.VMEM)] * (2 * n),
        out_shape=[jax.ShapeDtypeStruct(g.shape[1:], F32) for g in gs] + [jax.ShapeDtypeStruct((3,) + g.shape[1:], BF) for g in gs],
        compiler_params=_params(),
    )(blocks, *gs, *lands)
    return [(outs[k], outs[n + k]) for k in range(n)]


def _adamw_small(own, recv, g1_own, g1_recv, weights, moms, vels, after):
    n_w = len(weights)

    def body(*refs):
        own_refs, recv_refs, g1_own_ref, g1_recv_ref = refs[:5], refs[5:10], refs[10], refs[11]
        w_refs, m_refs, v_refs = refs[12:12 + n_w], refs[12 + n_w:12 + 2 * n_w], refs[12 + 2 * n_w:12 + 3 * n_w]
        outs = refs[12 + 3 * n_w:]
        loss_ref, g_refs, d_refs = outs[0], outs[1:1 + n_w], outs[1 + n_w:1 + 2 * n_w]
        mo_refs, vo_refs = outs[1 + 2 * n_w:1 + 3 * n_w], outs[1 + 3 * n_w:]
        x, y, c = lax.axis_index("x"), lax.axis_index("y"), lax.axis_index("c")

        def in_place_order(values, my_place):
            acc = None
            for place in range(len(values)):
                r = place ^ my_place
                term = values[-1]
                for q in range(len(values) - 2, -1, -1):
                    term = jnp.where(r == q, values[q], term)
                acc = term if acc is None else acc + term
            return acc

        def total(k, *index):
            index = index or (slice(None),) * (len(own_refs[k].shape) - 1)
            across = [own_refs[k][(0, *index)], recv_refs[k][(1, *index)], recv_refs[k][(0, *index)], recv_refs[k][(2, *index)]]
            return in_place_order([v.astype(F32) for v in across], 2 * x + y)

        g1 = in_place_order([g1_own_ref[0]] + [g1_recv_ref[j] for j in range(N_DEV - 1)], 4 * x + 2 * y + c)
        grads = [
            g1, total(1), total(2), total(3),
            total(4, slice(N_Q, None), slice(0, N_Q)), total(4, slice(0, N_Q), slice(None)),
            total(0, slice(0, 1), slice(None)), total(0, slice(1, 2), slice(None))]
        loss_ref[...] = total(0, slice(2, 3), slice(0, 1))
        for k in range(n_w):
            g_refs[k][...] = grads[k]
            d_refs[k][...], mo_refs[k][...], vo_refs[k][...] = _adamw_math(w_refs[k][...], grads[k], m_refs[k][...], v_refs[k][...])

    shapes = [jax.ShapeDtypeStruct(w.shape, F32) for w in weights]
    outs = _call(
        body,
        name="adamw_small",
        after=after,
        in_specs=[pl.BlockSpec(memory_space=pltpu.VMEM)] * (12 + 3 * n_w),
        out_shape=[jax.ShapeDtypeStruct((1, 1), F32)] + shapes * 4,
        compiler_params=_params(),
    )(*own, *recv, g1_own, g1_recv, *weights, *moms, *vels)
    return outs[0], outs[1:1 + n_w], outs[1 + n_w:1 + 2 * n_w], outs[1 + 2 * n_w:1 + 3 * n_w], outs[1 + 3 * n_w:]


def _place():
    x, y, c = lax.axis_index("x"), lax.axis_index("y"), lax.axis_index("c")
    return x, y, c, [(1 - x, y), (x, 1 - y), (1 - x, 1 - y)]


def _dev_index(px, py, pc):
    return 4 * px + 2 * py + pc


HBM_SPEC = pl.BlockSpec(memory_space=pltpu.HBM)
SEM_SPEC = pl.BlockSpec(memory_space=pltpu.SEMAPHORE)
ANY_SPEC = pl.BlockSpec(memory_space=pl.ANY)
DATAFLOW = pltpu.SideEffectType.DATAFLOW_SIDE_EFFECTING


def _hbm(a):
    return pltpu.with_memory_space_constraint(a, pltpu.HBM)


def _prep_weights(shards):
    k_n = len(shards)

    def body(*refs):
        ins, outs, stage, sems = refs[:k_n], refs[k_n:2 * k_n], refs[2 * k_n:3 * k_n], refs[3 * k_n]
        x, y, c, _ = _place()
        copies = []
        for k in range(k_n):
            stage[k][...] = ins[k][...].astype(BF)
            copies.append(pltpu.make_async_copy(stage[k], outs[k].at[_dev_index(x, y, c)], sems.at[k]))
            copies[k].start()
        for cp in copies:
            cp.wait()

    return _call(
        body,
        name="prep_weights",
        in_specs=[pl.BlockSpec(memory_space=pltpu.VMEM)] * k_n,
        out_specs=[ANY_SPEC] * k_n,
        out_shape=[jax.ShapeDtypeStruct((N_DEV,) + sh.shape, BF) for sh in shards],
        scratch_shapes=[pltpu.VMEM(sh.shape, BF) for sh in shards] + [pltpu.SemaphoreType.DMA((k_n,))],
        compiler_params=_params(),
    )(*shards)


def _ag_piece(land_k, block, half, peer, send_sem, recv_sem):
    ref = land_k.at[_dev_index(*block)]
    if half is not None:
        rows = land_k.shape[1] // 2
        ref = ref.at[pl.ds(half * rows, rows)]
    return pltpu.make_async_remote_copy(
        src_ref=ref, dst_ref=ref, send_sem=send_sem, recv_sem=recv_sem, device_id=peer, device_id_type=MESH)


def _ag_plan():
    x, y, c, _ = _place()
    me, sib = (x, y, c), (x, y, 1 - c)
    xn, yn, diag = (1 - x, y, c), (x, 1 - y, c), (1 - x, 1 - y, c)
    return dict(
        relay_halves=[(me, 0, xn), (me, 1, yn)],
        others=[(me, None, sib), (me, 1, xn), (me, 0, yn)],
        relays=[(xn, 0, yn), (yn, 1, xn)],
        near=[(xn, None, sib), (yn, None, sib)],
        far=[(diag, None, sib)],
    )


def _ag_stage(land, stage, send_sems, recv_sems, act):
    copies = _ag_plan()[stage]
    n = len(copies)
    for k in range(len(land)):
        for j, (block, half, peer) in enumerate(copies):
            cp = _ag_piece(land[k], block, half, peer, send_sems.at[n * k + j], recv_sems.at[n * k + j])
            if act == "start":
                cp.start()
            else:
                cp.wait_send()
                cp.wait_recv()


def _sem_shapes(*counts):
    return [pltpu.SemaphoreType.DMA((n,)) for n in counts for _ in range(2)]


def _ag_start(first, rest):
    lands = list(first) + list(rest)
    k_n, k_first = len(lands), len(first)
    k_rest = k_n - k_first

    def body(*refs):
        land = refs[:k_n]
        sems = refs[k_n:k_n + 6]
        token = refs[-1]
        x, y, c, chips = _place()
        targets = [(x, y, 1 - c)] + [(*chip, c) for chip in chips]
        for k in range(k_first):
            for j, to in enumerate(targets):
                _ag_piece(land[k], (x, y, c), None, to, sems[0].at[4 * k + j], sems[1].at[4 * k + j]).start()
        _ag_stage(land[k_first:], "relay_halves", sems[2], sems[3], "start")
        _ag_stage(land[k_first:], "others", sems[4], sems[5], "start")
        token[...] = jnp.zeros_like(token)

    outs = pl.pallas_call(
        body,
        name="ag_start",
        in_specs=[HBM_SPEC] * k_n,
        out_specs=(*[SEM_SPEC] * 6, *[HBM_SPEC] * k_n, pl.BlockSpec(memory_space=pltpu.VMEM)),
        out_shape=(*_sem_shapes(4 * k_first, 2 * k_rest, 3 * k_rest),
                   *[pltpu.HBM(a.shape, a.dtype) for a in lands], jax.ShapeDtypeStruct((8, LANES), F32)),
        input_output_aliases={k: 6 + k for k in range(k_n)},
        compiler_params=pltpu.CompilerParams(has_side_effects=DATAFLOW),
    )(*[_hbm(a) for a in lands])
    flying = list(outs[6:6 + k_n])
    return (outs[0], outs[1], flying[:k_first]), (outs[2:6], flying[k_first:]), outs[-1]


def _ag_split_call(lands, waits, starts, after, name):
    k_n = len(lands)
    plan_sizes = dict(relay_halves=2, others=3, relays=2, near=2, far=1)
    n_in, n_out = 2 * len(waits), 2 * len(starts)

    def body(*refs):
        land = refs[:k_n]
        in_sems = refs[k_n:k_n + n_in]
        out_sems, token = refs[len(refs) - 1 - n_out:len(refs) - 1], refs[-1]
        for w, (stage, _, _) in enumerate(waits):
            _ag_stage(land, stage, in_sems[2 * w], in_sems[2 * w + 1], "wait")
            if w < len(starts):
                _ag_stage(land, starts[w], out_sems[2 * w], out_sems[2 * w + 1], "start")
        token[...] = jnp.zeros_like(token)

    outs = pl.pallas_call(
        body,
        name=name,
        in_specs=[HBM_SPEC] * k_n + [SEM_SPEC] * n_in + [ANY_SPEC],
        out_specs=(*[HBM_SPEC] * k_n, *[SEM_SPEC] * n_out, pl.BlockSpec(memory_space=pltpu.VMEM)),
        out_shape=(*[pltpu.HBM(a.shape, a.dtype) for a in lands], *_sem_shapes(*[plan_sizes[s] * k_n for s in starts]),
                   jax.ShapeDtypeStruct((8, LANES), F32)),
        input_output_aliases={k: k for k in range(k_n)},
        compiler_params=pltpu.CompilerParams(has_side_effects=DATAFLOW),
    )(*lands, *[s for _, a, b in waits for s in (a, b)], after)
    return list(outs[:k_n]), list(outs[k_n:k_n + n_out]), outs[-1]


def _ag_mid(lands, send_sems, recv_sems, after, name):
    k_n = len(lands)

    def body(*refs):
        land = refs[:k_n]
        send1, recv1 = refs[k_n], refs[k_n + 1]
        fwd_send, fwd_recv = refs[-2], refs[-1]
        x, y, c, chips = _place()
        sources = [(x, y, 1 - c)] + [(*chip, c) for chip in chips]
        for k in range(k_n):
            mine = land[k].at[_dev_index(x, y, c)]
            for j, frm in enumerate(sources):
                got = land[k].at[_dev_index(*frm)]
                cp = pltpu.make_async_remote_copy(
                    src_ref=mine, dst_ref=got, send_sem=send1.at[4 * k + j], recv_sem=recv1.at[4 * k + j], device_id=frm, device_id_type=MESH)
                cp.wait_send()
                cp.wait_recv()
                if j >= 1:
                    pltpu.make_async_remote_copy(
                        src_ref=got, dst_ref=got, send_sem=fwd_send.at[3 * k + j - 1], recv_sem=fwd_recv.at[3 * k + j - 1],
                        device_id=(x, y, 1 - c), device_id_type=MESH).start()

    outs = pl.pallas_call(
        body,
        name=name,
        in_specs=[HBM_SPEC] * k_n + [SEM_SPEC, SEM_SPEC, ANY_SPEC],
        out_specs=(*[HBM_SPEC] * k_n, SEM_SPEC, SEM_SPEC),
        out_shape=(*[pltpu.HBM(a.shape, a.dtype) for a in lands], pltpu.SemaphoreType.DMA((3 * k_n,)), pltpu.SemaphoreType.DMA((3 * k_n,))),
        input_output_aliases={k: k for k in range(k_n)},
        compiler_params=pltpu.CompilerParams(has_side_effects=DATAFLOW),
    )(*lands, send_sems, recv_sems, after)
    return list(outs[:k_n]), outs[-2], outs[-1]


def _ag_end(lands, fwd_send, fwd_recv, name):
    k_n = len(lands)

    def body(*refs):
        land = refs[:k_n]
        fsend, frecv = refs[k_n], refs[k_n + 1]
        x, y, c, chips = _place()
        for k in range(k_n):
            for j, chip in enumerate(chips):
                cp = pltpu.make_async_remote_copy(
                    src_ref=land[k].at[_dev_index(*chip, c)], dst_ref=land[k].at[_dev_index(*chip, 1 - c)],
                    send_sem=fsend.at[3 * k + j], recv_sem=frecv.at[3 * k + j], device_id=(x, y, 1 - c), device_id_type=MESH)
                cp.wait_send()
                cp.wait_recv()

    outs = pl.pallas_call(
        body,
        name=name,
        in_specs=[HBM_SPEC] * k_n + [SEM_SPEC, SEM_SPEC],
        out_specs=tuple([HBM_SPEC] * k_n),
        out_shape=tuple(pltpu.HBM(a.shape, a.dtype) for a in lands),
        input_output_aliases={k: k for k in range(k_n)},
        compiler_params=pltpu.CompilerParams(has_side_effects=DATAFLOW),
    )(*lands, fwd_send, fwd_recv)
    return list(outs)


def _chips4():
    x, y, c, others = _place()
    return x, y, c, [(x, y)] + others


def _route_sibling(j):
    x, y, c, chips = _chips4()
    return _dev_index(*chips[j], 1 - c), j, (x, y, 1 - c)


def _route_chips(j):
    x, y, c, chips = _chips4()
    return j, j, (*chips[j + 1], c)


def _route_sibling_whole(j):
    x, y, c, _ = _chips4()
    return 0, 0, (x, y, 1 - c)


def _route_chips_whole(j):
    x, y, c, chips = _chips4()
    return 0, j, (*chips[j + 1], c)


def _route_everyone(j):
    x, y, c, _ = _chips4()
    flip = [(j + 1) >> 2 & 1, (j + 1) >> 1 & 1, (j + 1) & 1]
    return 0, j, tuple(1 - v if f else v for v, f in zip((x, y, c), flip))


def _xchg_copies(routes, src, dst, send_sems, recv_sems):
    copies, sem = [], 0
    for k, (route, n) in enumerate(routes):
        for j in range(n):
            si, di, peer = route(j)
            copies.append(pltpu.make_async_remote_copy(
                src_ref=src[k].at[si], dst_ref=dst[k].at[di], send_sem=send_sems.at[sem], recv_sem=recv_sems.at[sem],
                device_id=peer, device_id_type=MESH))
            sem += 1
    return copies


def _xchg_start(srcs, slot_shapes, routes, name, after=None):
    k_n = len(srcs)
    n_in = 2 * k_n + (after is not None)
    n_sem = sum(n for _, n in routes)
    dsts = [lax.empty((n,) + tuple(sh), a.dtype) for sh, a, (_, n) in zip(slot_shapes, srcs, routes)]

    def body(*refs):
        src, dst = refs[:k_n], refs[k_n:2 * k_n]
        send_sems, recv_sems, token = refs[n_in], refs[n_in + 1], refs[-1]
        for cp in _xchg_copies(routes, src, dst, send_sems, recv_sems):
            cp.start()
        token[...] = jnp.zeros_like(token)

    arrays = list(srcs) + dsts
    outs = pl.pallas_call(
        body,
        name=name,
        in_specs=[HBM_SPEC] * (2 * k_n) + [ANY_SPEC] * (n_in - 2 * k_n),
        out_specs=(SEM_SPEC, SEM_SPEC, *[HBM_SPEC] * (2 * k_n), pl.BlockSpec(memory_space=pltpu.VMEM)),
        out_shape=(pltpu.SemaphoreType.DMA((n_sem,)), pltpu.SemaphoreType.DMA((n_sem,)),
                   *[pltpu.HBM(a.shape, a.dtype) for a in arrays], jax.ShapeDtypeStruct((8, LANES), F32)),
        input_output_aliases={i: 2 + i for i in range(2 * k_n)},
        compiler_params=pltpu.CompilerParams(has_side_effects=DATAFLOW),
    )(*[_hbm(a) for a in arrays], *([] if after is None else [after]))
    return outs[0], outs[1], list(outs[2:2 + k_n]), list(outs[2 + k_n:2 + 2 * k_n]), outs[-1]


def _xchg_wait(send_sems, recv_sems, srcs, dsts, routes, after, name):
    k_n = len(srcs)

    def body(*refs):
        src, dst = refs[:k_n], refs[k_n:2 * k_n]
        for cp in _xchg_copies(routes, src, dst, refs[2 * k_n], refs[2 * k_n + 1]):
            cp.wait_send()
            cp.wait_recv()

    arrays = list(srcs) + list(dsts)
    outs = pl.pallas_call(
        body,
        name=name,
        in_specs=[HBM_SPEC] * (2 * k_n) + [SEM_SPEC, SEM_SPEC, ANY_SPEC],
        out_specs=tuple([HBM_SPEC] * (2 * k_n)),
        out_shape=tuple(pltpu.HBM(a.shape, a.dtype) for a in arrays),
        input_output_aliases={i: i for i in range(2 * k_n)},
        compiler_params=pltpu.CompilerParams(has_side_effects=DATAFLOW),
    )(*arrays, send_sems, recv_sems, after)
    return list(outs[:k_n]), list(outs[k_n:])


SMALL = ("norm1_gain", "gmlp_v_gain", "w_spatial", "b_spatial", "attn_sinks", "rel_bias_table", "norm2_gain", "final_gain")
LANES = 128


def _swap_start(grads, smalls, tag, after=None):
    srcs = list(grads) + [a[None] for a in smalls]
    shapes = [g.shape[1:] for g in grads] + [a.shape for a in smalls]
    routes = [(_route_sibling, 4)] * len(grads) + [(_route_sibling_whole, 1)] * len(smalls)
    return _xchg_start(srcs, shapes, routes, f"rs_{tag}_swap_start", after), routes, len(grads)


def _swap_sums(swap, names, after, tag, few=False, small_dtypes=None):
    (send1, recv1, src1, land1, _), routes, n_rs = swap
    x, y, c, chips = _chips4()
    blocks = jnp.stack([_dev_index(*chip, c) for chip in chips]).astype(jnp.int32)
    src1, land1 = _xchg_wait(send1, recv1, src1, land1, routes, after, f"rs_{tag}_swap_wait")
    if few:
        sums = _rs_sum_few(src1[:n_rs], land1[:n_rs], blocks, f"rs_{tag}_sum")
    else:
        sums = [_rs_sum(g, land, blocks, f"rs_sum_{n}") for g, land, n in zip(src1[:n_rs], land1[:n_rs], names)]
    if len(src1) == n_rs:
        return sums, []
    return sums, _pair_sum(src1[n_rs:], land1[n_rs:], small_dtypes or [F32] * (len(src1) - n_rs), f"rs_{tag}_sum_small")


def _pair_sum(mine, theirs, dtypes, name):
    def body(*refs):
        n = len(refs) // 3
        for k in range(n):
            refs[2 * n + k][...] = (refs[k][...] + refs[n + k][...]).astype(dtypes[k])

    return _call(
        body,
        name=name,
        out_shape=[jax.ShapeDtypeStruct(a.shape, dt) for a, dt in zip(mine, dtypes)],
        compiler_params=_params(),
    )(*mine, *theirs)


def _chips_start(sums, small_sums, tag, everyone=()):
    sends = [ps[1] for ps in sums] + list(small_sums) + [a[None] for a in everyone]
    routes = [(_route_chips, 3)] * len(sums) + [(_route_chips_whole, 3)] * len(small_sums) + [(_route_everyone, 7)] * len(everyone)
    return _xchg_start(sends, [a.shape[1:] for a in sends], routes, f"rs_{tag}_chips_start"), routes


def _chips_wait(chips, after, tag):
    (send2, recv2, src2, land2, _), routes = chips
    return _xchg_wait(send2, recv2, src2, land2, routes, after, f"rs_{tag}_chips_wait")


def kernel(x, p, norm1_gain, w_in, gmlp_v_gain, w_spatial, b_spatial, attn_sinks, rel_bias_table, w_out, norm2_gain, w_ff1, w_ff2, w_ple_proj, w_ple_gate, final_gain, loss_target, m_norm1_gain, m_w_in, m_gmlp_v_gain, m_w_spatial, m_b_spatial, m_attn_sinks, m_rel_bias_table, m_w_out, m_norm2_gain, m_w_ff1, m_w_ff2, m_w_ple_proj, m_w_ple_gate, m_final_gain, v_norm1_gain, v_w_in, v_gmlp_v_gain, v_w_spatial, v_b_spatial, v_attn_sinks, v_rel_bias_table, v_w_out, v_norm2_gain, v_w_ff1, v_w_ff2, v_w_ple_proj, v_w_ple_gate, v_final_gain):
    args = dict(locals())
    s = x.shape[1]
    big = ("w_in", "w_out", "w_ff1", "w_ff2", "w_ple_proj", "w_ple_gate")

    x2, p2, t2 = x.reshape(s, D_MODEL), p.reshape(s, PLE_DIM), loss_target.reshape(s, D_MODEL)
    g1, gv, w_sp, b_sp, sinks, table, g2, gf = (args[n] for n in SMALL)
    bucket = jnp.asarray(_bucket_table())
    b_col = b_sp.reshape(GROUPS, BLOCK, 1)

    def shard(name):
        return args[name][0].T if name.endswith("w_in") else args[name][0]

    lands = _prep_weights([shard(n) for n in big])
    (send_in, recv_in, fly_in), (rest_sems, fly_rest), token = _ag_start(lands[:1], lands[1:])
    mid_in, fwd_send_in, fwd_recv_in = _ag_mid(fly_in, send_in, recv_in, token, "ag_mid_w_in")
    g_in = _ag_end(mid_in, fwd_send_in, fwd_recv_in, "ag_end_w_in")[0]
    full_in = g_in.reshape(D_IN, D_MODEL)

    zuv, qkv, hn1 = _in_proj(x2, g1, full_in)
    fly_rest, relay_sems, relayed = _ag_split_call(
        fly_rest, [("relay_halves", *rest_sems[:2])], ["relays"], zuv, "ag_relay")
    mix, *saved = _mixer_fwd(zuv, qkv, gv, w_sp[0], b_col, sinks, table.T, bucket, after=relayed)
    fly_rest, fwd_sems, _ = _ag_split_call(
        fly_rest, [("others", *rest_sems[2:]), ("relays", *relay_sems)], ["near", "far"], mix, "ag_mid_rest")
    g_out, g_ff1, g_ff2, g_proj, g_gate = _ag_split_call(
        fly_rest, [("near", *fwd_sems[:2]), ("far", *fwd_sems[2:])], [], mix, "ag_end_rest")[0]
    full_out, full_ff2, full_gate = g_out.reshape(D_MODEL, D_MODEL), g_ff2.reshape(D_FF, D_MODEL), g_gate.reshape(D_MODEL, D_MODEL)
    full_proj = g_proj.transpose(1, 0, 2).reshape(PLE_DIM, D_MODEL)

    (tail_small, dh1, dh1b, dmix, hn2, a, df, h2, dh2, dgl, dpp) = _tail(
        x2, mix, p2, t2, g2, gf.reshape(1, D_MODEL), full_out, g_ff1, full_ff2, full_gate, full_proj)

    mid_names = ("w_out", "w_ple_proj", "w_ple_gate")
    gw_ff1 = _wgrad(hn2, df, D_MODEL, 2 * D_FF // N_DEV, "wgrad_ff1", peer_cols=D_FF // N_DEV)
    ff1_swap = _swap_start([gw_ff1], [], "ff1")
    gw_ff2 = _wgrad(a, dh2, 1024, D_MODEL, "wgrad_ff2", after=ff1_swap[0][4]).reshape(N_DEV, D_FF // N_DEV, D_MODEL)
    ff1_sums, _ = _swap_sums(ff1_swap, ("w_ff1",), gw_ff2, "ff1")
    ff1_chips = _chips_start(ff1_sums, [], "ff1")
    ff2_swap = _swap_start([gw_ff2], [], "ff2", after=ff1_chips[0][4])
    gw_gate = _wgrad(h2, dgl, D_MODEL, 512, "wgrad_gate", after=ff2_swap[0][4]).reshape(N_DEV, D_MODEL // N_DEV, D_MODEL)
    gw_proj = _wgrad(p2, dpp, PLE_DIM, D_MODEL // 2, "wgrad_proj", peer_cols=D_MODEL // N_DEV, after=gw_gate)
    ff2_sums, _ = _swap_sums(ff2_swap, ("w_ff2",), gw_proj, "ff2")
    ff2_chips = _chips_start(ff2_sums, [], "ff2")
    gw_out = _wgrad(mix, dh1b, D_MODEL, 512, "wgrad_out", after=ff2_chips[0][4]).reshape(N_DEV, D_MODEL // N_DEV, D_MODEL)
    mid_swap = _swap_start([gw_out, gw_proj, gw_gate], [tail_small], "mid")

    dzm, dkv, d_gv, d_wsp, d_bsp, d_attn = _mixer_bwd(
        zuv, qkv, dmix, saved, gv, w_sp[0], b_col, bucket, after=mid_swap[0][4])
    mid_sums, mid_small = _swap_sums(mid_swap, mid_names, dzm, "mid", few=True)
    mid_chips = _chips_start(mid_sums, mid_small, "mid")
    gw_in = _wgrad_in(dzm, dkv, hn1, after=mid_chips[0][4]).reshape(N_DEV, D_IN // N_DEV, D_MODEL)
    in_swap = _swap_start([gw_in], [d_gv, d_wsp, d_bsp, d_attn], "in")
    dx, d_g1 = _in_bwd(x2, dh1, dzm, dkv, g1, full_in, after=in_swap[0][4])
    grad_x = dx.reshape(x.shape)
    in_sums, in_small = _swap_sums(in_swap, ("w_in",), dx, "in", small_dtypes=[F32, BF, F32, F32])
    in_chips = _chips_start(in_sums, in_small, "in", everyone=[d_g1])

    grads, deltas, new_m, new_v = {}, {}, {}, {}
    after, small_own, small_recv = in_chips[0][4], [], []
    for chips, sums, names, tag in ((ff1_chips, ff1_sums, ("w_ff1",), "ff1"), (ff2_chips, ff2_sums, ("w_ff2",), "ff2"),
                                    (mid_chips, mid_sums, mid_names, "mid"), (in_chips, in_sums, ("w_in",), "in")):
        sent, recvs = _chips_wait(chips, after, tag)
        small_own += sent[len(names):]
        small_recv += recvs[len(names):]
        if tag == "mid":
            done = _final_adamw_few([ps[0] for ps in sums], recvs[:len(names)], [shard(n) for n in names],
                                    [shard("m_" + n) for n in names], [shard("v_" + n) for n in names], "adamw_mid")
            done = list(zip(*done))
        else:
            done = [_final_adamw(part, recv, shard(n), shard("m_" + n), shard("v_" + n), "adamw_" + n)
                    for n, (part, _), recv in zip(names, sums, recvs)]
        for n, results in zip(names, done):
            for dst, arr in zip((grads, deltas, new_m, new_v), results):
                dst[n] = (arr.T if n == "w_in" else arr)[None]
            after = results[1]

    views = {"w_spatial": (GROUPS, BLOCK, BLOCK), "b_spatial": (GROUPS, BLOCK), "final_gain": (1, D_MODEL)}
    def view(name, base):
        return args[name].T if base == "rel_bias_table" else args[name].reshape(views.get(base, args[name].shape))

    small_in = [[view(pre + n, n) for n in SMALL] for pre in ("", "m_", "v_")]
    loss, *small_out = _adamw_small(small_own[:-1], small_recv[:-1], small_own[-1], small_recv[-1], *small_in, after=after)
    for dst, arrays in zip((grads, deltas, new_m, new_v), small_out):
        for n, arr in zip(SMALL, arrays):
            dst[n] = arr.T if n == "rel_bias_table" else arr.reshape(args[n].shape)
    loss = loss[0, 0]

    order = ("norm1_gain", "w_in", "gmlp_v_gain", "w_spatial", "b_spatial", "attn_sinks", "rel_bias_table", "w_out",
             "norm2_gain", "w_ff1", "w_ff2", "w_ple_proj", "w_ple_gate", "final_gain")
    return (loss, grad_x, *[grads[n] for n in order], *[deltas[n] for n in order],
            *[new_m[n] for n in order], *[new_v[n] for n in order])
```

```python
import math

import numpy as np
import jax
import jax.numpy as jnp
from jax import lax
from jax.experimental import pallas as pl
from jax.experimental.pallas import tpu as pltpu

F32 = jnp.float32
BF = jnp.bfloat16
MESH = pl.DeviceIdType.MESH
N_DEV = 8

D_MODEL = 1024
PLE_DIM = 256
D_GMLP = 512
GROUPS = 4
GDIM = 128
BLOCK = 128
D_ATTN = 512
HEAD_DIM = 64
N_Q = 8
Q_PER_KV = 4
N_KV = N_Q // Q_PER_KV
ROWS4 = Q_PER_KV * BLOCK
D_KV = 128
D_FF = 4096
D_IN = 1792
D_MAIN = 2 * D_GMLP + D_ATTN
REL_BUCKETS = 32
EPS = 1e-6
NEG_INF = -1e30
SCALE = HEAD_DIM ** -0.5
GELU_C = math.sqrt(2.0 / math.pi)
GELU_A = 0.044715

ADAM_LR = 0.001
ADAM_B1 = 0.9
ADAM_B2 = 0.999
ADAM_EPS = 1e-08
ADAM_WD = 0.01
ADAM_STEP = 10

V7X_VMEM_LIMIT = 60000 * 1024
TOK_TILE = 256
IO_TOK_TILE = 512


def _call(body, after=None, **kw):
    if after is None:
        return pl.pallas_call(body, **kw)
    n_in = len(kw["in_specs"])

    def ordered(*refs):
        body(*refs[:n_in], *refs[n_in + 1:])

    kw["in_specs"] = list(kw["in_specs"]) + [pl.BlockSpec(memory_space=pl.ANY)]
    fn = pl.pallas_call(ordered, **kw)
    return lambda *operands: fn(*operands, after)


def _params(sem=None):
    if sem is None:
        return pltpu.CompilerParams(vmem_limit_bytes=V7X_VMEM_LIMIT)
    return pltpu.CompilerParams(dimension_semantics=sem, vmem_limit_bytes=V7X_VMEM_LIMIT)


def _nn(a, b):
    return jnp.dot(a, b, preferred_element_type=F32)


def _nt(a, b):
    return lax.dot_general(a, b, (((1,), (1,)), ((), ())), preferred_element_type=F32)


def _tn(a, b):
    return lax.dot_general(a, b, (((0,), (0,)), ((), ())), preferred_element_type=F32)


def _gelu_tanh(x):
    return jnp.tanh(GELU_C * (x + GELU_A * (x * x * x)))


def _gelu(x, t):
    return x * (0.5 * (1.0 + t))


def _gelu_and_grad(x, t):
    cdf = 0.5 * (1.0 + t)
    return x * cdf, cdf + 0.5 * x * (1.0 - t * t) * (GELU_C * (1.0 + 3.0 * GELU_A * (x * x)))


def _rms_scale(x):
    return lax.rsqrt(jnp.mean(x * x, axis=-1, keepdims=True) + EPS)


def _rms_bwd(dxn, x, r):
    return r * dxn - x * ((r * r * r) * jnp.mean(dxn * x, axis=-1, keepdims=True))


def _bucket_table():
    a = np.arange(BLOCK)[:, None]
    j = np.arange(2 * BLOCK)[None, :]
    n = BLOCK + a - j
    valid = (n >= 0) & (n < BLOCK)
    nc = np.maximum(n, 0)
    max_exact = REL_BUCKETS // 2
    nf = np.maximum(nc, 1).astype(np.float32)
    large = max_exact + (
        np.log(nf / np.float32(max_exact)) / np.float32(math.log(BLOCK / max_exact)) * np.float32(REL_BUCKETS - max_exact)
    ).astype(np.int32)
    large = np.minimum(large, REL_BUCKETS - 1)
    bucket = np.where(nc < max_exact, nc, large)
    return np.where(valid, bucket, -1).astype(np.int32)


def _in_proj(x, g1, w_in_t):
    s = x.shape[0]
    tm = min(IO_TOK_TILE, s)

    def body(x_ref, g_ref, w_ref, zuv_ref, qkv_ref, hn_ref):
        xv = x_ref[...]
        hn = ((xv * _rms_scale(xv)) * g_ref[...]).astype(BF)
        hn_ref[...] = hn
        z = _nt(hn, w_ref[...])
        zuv_ref[...] = z[:, : 2 * D_GMLP]
        qkv_ref[...] = z[:, 2 * D_GMLP:].astype(BF)

    return _call(
        body,
        name="in_proj",
        grid=(s // tm,),
        in_specs=[
            pl.BlockSpec((tm, D_MODEL), lambda i: (i, 0)),
            pl.BlockSpec((1, D_MODEL), lambda i: (0, 0)),
            pl.BlockSpec((D_IN, D_MODEL), lambda i: (0, 0)),
        ],
        out_specs=[
            pl.BlockSpec((tm, 2 * D_GMLP), lambda i: (i, 0)),
            pl.BlockSpec((tm, D_ATTN + 2 * D_KV), lambda i: (i, 0)),
            pl.BlockSpec((tm, D_MODEL), lambda i: (i, 0)),
        ],
        out_shape=[
            jax.ShapeDtypeStruct((s, 2 * D_GMLP), F32),
            jax.ShapeDtypeStruct((s, D_ATTN + 2 * D_KV), BF),
            jax.ShapeDtypeStruct((s, D_MODEL), BF),
        ],
        compiler_params=_params(("arbitrary",)),
    )(x, g1, w_in_t)


def _head_rows(h):
    kh, g = divmod(h, Q_PER_KV)
    return kh, slice(g * BLOCK, (g + 1) * BLOCK)


def _build_bias(bias_ref, bucket_ref, table_ref):
    bucket = bucket_ref[...]
    for h in range(N_Q):
        acc = jnp.zeros((BLOCK, 2 * BLOCK), F32)
        for b in range(REL_BUCKETS):
            acc = jnp.where(bucket == b, table_ref[h, b], acc)
        kh, rows = _head_rows(h)
        bias_ref[kh, rows, :] = acc


def _window_masks(i):
    row = lax.broadcasted_iota(jnp.int32, (ROWS4, BLOCK), 0) & (BLOCK - 1)
    col = lax.broadcasted_iota(jnp.int32, (ROWS4, BLOCK), 1)
    return (col > row) & (i > 0), col <= row


def _stack_heads(ref, kh, offset):
    first = offset + kh * Q_PER_KV * HEAD_DIM
    return jnp.concatenate(
        [ref[:, first + g * HEAD_DIM: first + (g + 1) * HEAD_DIM].astype(BF) for g in range(Q_PER_KV)], axis=0)


def _stack_sinks(sink_ref, kh):
    return jnp.concatenate([jnp.full((BLOCK, 1), sink_ref[0, kh * Q_PER_KV + g], F32) for g in range(Q_PER_KV)], axis=0)


def _tril_bf16(w_ref, g):
    row = lax.broadcasted_iota(jnp.int32, (BLOCK, BLOCK), 0)
    col = lax.broadcasted_iota(jnp.int32, (BLOCK, BLOCK), 1)
    return jnp.where(col <= row, w_ref[g], 0.0).astype(BF)


def _attn_probs(q_h, k_prev, k_cur, bias_h, sink, valid_prev, valid_cur):
    l_prev = jnp.where(valid_prev, _nt(q_h, k_prev) * SCALE + bias_h[:, :BLOCK], NEG_INF)
    l_cur = jnp.where(valid_cur, _nt(q_h, k_cur) * SCALE + bias_h[:, BLOCK:], NEG_INF)
    m = jnp.maximum(jnp.maximum(jnp.max(l_prev, axis=-1, keepdims=True), jnp.max(l_cur, axis=-1, keepdims=True)), sink)
    e_prev = jnp.exp(l_prev - m)
    e_cur = jnp.exp(l_cur - m)
    e_sink = jnp.exp(sink - m)
    denom = jnp.sum(e_prev, axis=-1, keepdims=True) + jnp.sum(e_cur, axis=-1, keepdims=True) + e_sink
    return e_prev / denom, e_cur / denom, e_sink / denom


def _mixer_specs(nb):
    cl = lambda i: jnp.minimum(i, nb - 1)
    return [
        pl.BlockSpec((BLOCK, 2 * D_GMLP), lambda i: (cl(i), 0)),
        pl.BlockSpec((BLOCK, D_ATTN), lambda i: (cl(i), 0)),
        pl.BlockSpec((BLOCK, 2 * D_KV), lambda i: (cl(i), D_ATTN // (2 * D_KV))),
        pl.BlockSpec((BLOCK, 2 * D_KV), lambda i: (jnp.maximum(cl(i) - 1, 0), D_ATTN // (2 * D_KV))),
        pl.BlockSpec((1, D_GMLP), lambda i: (0, 0)),
        pl.BlockSpec((GROUPS, BLOCK, BLOCK), lambda i: (0, 0, 0)),
        pl.BlockSpec((GROUPS, BLOCK, 1), lambda i: (0, 0, 0)),
        pl.BlockSpec(memory_space=pltpu.SMEM),
        pl.BlockSpec(memory_space=pltpu.SMEM),
        pl.BlockSpec((BLOCK, 2 * BLOCK), lambda i: (0, 0)),
    ]


def _mixer_fwd(zuv, qkv, gv, w_sp, b_sp, sinks, table, bucket, after=None):
    s = zuv.shape[0]
    nb = s // BLOCK

    def body(zuv_ref, q_ref, kvc_ref, kvp_ref, gv_ref, w_ref, b_ref, sink_ref, table_ref, bucket_ref,
             mix_ref, tanh_ref, prob_ref, psink_ref, bias_ref):
        i = pl.program_id(0)

        @pl.when(i == 0)
        def _():
            _build_bias(bias_ref, bucket_ref, table_ref)

        t = _gelu_tanh(zuv_ref[...])
        tanh_ref[...] = t
        u = _gelu(zuv_ref[:, :D_GMLP], t[:, :D_GMLP])
        vg = _gelu(zuv_ref[:, D_GMLP:], t[:, D_GMLP:])
        for g in range(GROUPS):
            sl = slice(g * GDIM, (g + 1) * GDIM)
            vg_g = vg[:, sl]
            vn = ((vg_g * _rms_scale(vg_g)) * gv_ref[:, sl]).astype(BF)
            sv = _nn(_tril_bf16(w_ref, g), vn) + b_ref[g]
            mix_ref[:, sl] = (u[:, sl] * sv).astype(BF)

        valid_prev, valid_cur = _window_masks(i)
        for kh in range(N_KV):
            ksl = slice(kh * HEAD_DIM, (kh + 1) * HEAD_DIM)
            vsl = slice(D_KV + kh * HEAD_DIM, D_KV + (kh + 1) * HEAD_DIM)
            q4 = _stack_heads(q_ref, kh, 0)
            p_prev, p_cur, p_sink = _attn_probs(
                q4, kvp_ref[:, ksl], kvc_ref[:, ksl], bias_ref[kh], _stack_sinks(sink_ref, kh), valid_prev, valid_cur)
            prob_ref[0, kh, :, :BLOCK] = p_prev
            prob_ref[0, kh, :, BLOCK:] = p_cur
            psink_ref[0, kh] = jnp.broadcast_to(p_sink, (ROWS4, LANES))
            o4 = _nn(p_prev.astype(BF), kvp_ref[:, vsl]) + _nn(p_cur.astype(BF), kvc_ref[:, vsl])
            for g in range(Q_PER_KV):
                first = D_GMLP + (kh * Q_PER_KV + g) * HEAD_DIM
                mix_ref[:, first:first + HEAD_DIM] = o4[g * BLOCK:(g + 1) * BLOCK].astype(BF)

    return _call(
        body,
        name="mixer_fwd",
        after=after,
        grid=(nb,),
        in_specs=_mixer_specs(nb),
        out_specs=[
            pl.BlockSpec((BLOCK, D_MODEL), lambda i: (i, 0)),
            pl.BlockSpec((BLOCK, 2 * D_GMLP), lambda i: (i, 0)),
            pl.BlockSpec((1, N_KV, ROWS4, 2 * BLOCK), lambda i: (i, 0, 0, 0)),
            pl.BlockSpec((1, N_KV, ROWS4, LANES), lambda i: (i, 0, 0, 0)),
        ],
        out_shape=[
            jax.ShapeDtypeStruct((s, D_MODEL), BF),
            jax.ShapeDtypeStruct((s, 2 * D_GMLP), F32),
            jax.ShapeDtypeStruct((nb, N_KV, ROWS4, 2 * BLOCK), F32),
            jax.ShapeDtypeStruct((nb, N_KV, ROWS4, LANES), F32),
        ],
        scratch_shapes=[pltpu.VMEM((N_KV, ROWS4, 2 * BLOCK), F32)],
        compiler_params=_params(("arbitrary",)),
    )(zuv, qkv, qkv, qkv, gv, w_sp, b_sp, sinks, table, bucket)


def _tail(x, mix, p, t, g2, gf, w_out, w_ff1, w_ff2, w_gate, w_proj):
    s = x.shape[0]
    tm = min(TOK_TILE, s)
    n_ff = w_ff1.shape[0]
    fc = D_FF // n_ff

    def body(x_ref, mix_ref, p_ref, t_ref, g2_ref, gf_ref, wo_ref, w1_ref, w2_ref, wg_ref, wp_ref,
             small_ref, dh1_ref, dh1b_ref, dmix_ref, hn2_ref, a_ref, df_ref, h2_ref, dh2_ref, dgl_ref, dpp_ref, f_ref):
        i = pl.program_id(0)

        @pl.when(i == 0)
        def _():
            small_ref[...] = jnp.zeros_like(small_ref)

        h1 = x_ref[...] + _nn(mix_ref[...], wo_ref[...])
        r2 = _rms_scale(h1)
        hn2 = ((h1 * r2) * g2_ref[...]).astype(BF)
        hn2_ref[...] = hn2
        h2 = h1
        for c in range(n_ff):
            f = _nn(hn2, w1_ref[c])
            f_ref[:, c * fc:(c + 1) * fc] = f
            a = jnp.square(jnp.maximum(f, 0.0)).astype(BF)
            a_ref[:, c * fc:(c + 1) * fc] = a
            h2 = h2 + _nn(a, w2_ref[c * fc:(c + 1) * fc, :])
        h2b = h2.astype(BF)
        h2_ref[...] = h2b
        gate = jax.nn.sigmoid(_nn(h2b, wg_ref[...]))
        pp = _nn(p_ref[...].astype(BF), wp_ref[...])
        h3 = h2 + gate * pp
        rf = _rms_scale(h3)
        gf_v = gf_ref[...]
        err = (h3 * rf) * gf_v - t_ref[...]
        small_ref[2:3, :] += jnp.sum(jnp.sum(err * err, axis=-1, keepdims=True), axis=0, keepdims=True) * (0.5 / D_MODEL)

        dy = err * (1.0 / D_MODEL)
        small_ref[1:2, :] += jnp.sum(dy * (h3 * rf), axis=0, keepdims=True)
        dh3 = _rms_bwd(dy * gf_v, h3, rf)
        dpp_ref[...] = (dh3 * gate).astype(BF)
        dgl = ((dh3 * pp) * (gate * (1.0 - gate))).astype(BF)
        dgl_ref[...] = dgl
        dh2 = dh3 + _nt(dgl, wg_ref[...])
        dh2b = dh2.astype(BF)
        dh2_ref[...] = dh2b
        dhn2 = jnp.zeros((tm, D_MODEL), F32)
        for c in range(n_ff):
            da = _nt(dh2b, w2_ref[c * fc:(c + 1) * fc, :])
            df = (da * (2.0 * jnp.maximum(f_ref[:, c * fc:(c + 1) * fc], 0.0))).astype(BF)
            df_ref[:, c * fc:(c + 1) * fc] = df
            dhn2 = dhn2 + _nt(df, w1_ref[c])
        small_ref[0:1, :] += jnp.sum(dhn2 * (h1 * r2), axis=0, keepdims=True)
        dh1 = dh2 + _rms_bwd(dhn2 * g2_ref[...], h1, r2)
        dh1_ref[...] = dh1
        dh1b = dh1.astype(BF)
        dh1b_ref[...] = dh1b
        dmix_ref[...] = _nt(dh1b, wo_ref[...])

    tile = lambda cols: pl.BlockSpec((tm, cols), lambda i: (i, 0))
    whole = lambda shape: pl.BlockSpec(shape, lambda i: (0,) * len(shape), pipeline_mode=pl.Buffered(1))
    row = pl.BlockSpec((1, D_MODEL), lambda i: (0, 0))
    act = lambda cols, dt: jax.ShapeDtypeStruct((s, cols), dt)
    return _call(
        body,
        name="tail",
        grid=(s // tm,),
        in_specs=[tile(D_MODEL), tile(D_MODEL), tile(PLE_DIM), tile(D_MODEL), row, row,
                  whole(w_out.shape), whole(w_ff1.shape), whole(w_ff2.shape), whole(w_gate.shape), whole(w_proj.shape)],
        out_specs=[pl.BlockSpec((8, D_MODEL), lambda i: (0, 0)), tile(D_MODEL), tile(D_MODEL), tile(D_MODEL), tile(D_MODEL), tile(D_FF),
                   tile(D_FF), tile(D_MODEL), tile(D_MODEL), tile(D_MODEL), tile(D_MODEL)],
        out_shape=[jax.ShapeDtypeStruct((8, D_MODEL), F32),
                   act(D_MODEL, F32), act(D_MODEL, BF), act(D_MODEL, F32), act(D_MODEL, BF), act(D_FF, BF), act(D_FF, BF), act(D_MODEL, BF),
                   act(D_MODEL, BF), act(D_MODEL, BF), act(D_MODEL, BF)],
        scratch_shapes=[pltpu.VMEM((tm, D_FF), F32)],
        compiler_params=_params(("arbitrary",)),
    )(x, mix, p, t, g2, gf, w_out, w_ff1, w_ff2, w_gate, w_proj)


def _mixer_bwd(zuv, qkv, dmix, saved, gv, w_sp, b_sp, bucket, x, dh1, g1, w_in_t, after=None):
    s = zuv.shape[0]
    nb = s // BLOCK

    def body(zuv_ref, q_ref, kvc_ref, kvp_ref, gv_ref, w_ref, b_ref, bucket_ref, dmix_ref, tanh_ref, prob_ref, psink_ref,
             x_ref, dh1_ref, g1_ref, win_ref,
             dzm_ref, dkv_ref, dgv_ref, dw_ref, db_ref, dattn_ref, dx_ref, dg1_ref,
             dbias_ref, carry_ref, dsink_acc, db_acc, dzm_before):
        i = pl.program_id(0)

        @pl.when(i == 0)
        def _():
            dbias_ref[...] = jnp.zeros_like(dbias_ref)
            carry_ref[...] = jnp.zeros_like(carry_ref)
            dsink_acc[...] = jnp.zeros_like(dsink_acc)
            dgv_ref[...] = jnp.zeros_like(dgv_ref)
            dg1_ref[...] = jnp.zeros_like(dg1_ref)
            dw_ref[...] = jnp.zeros_like(dw_ref)
            db_acc[...] = jnp.zeros_like(db_acc)

        @pl.when(i < nb)
        def _():
            u, du_dz = _gelu_and_grad(zuv_ref[:, :D_GMLP], tanh_ref[:, :D_GMLP])
            vg, dvg_dz = _gelu_and_grad(zuv_ref[:, D_GMLP:], tanh_ref[:, D_GMLP:])
            for g in range(GROUPS):
                sl = slice(g * GDIM, (g + 1) * GDIM)
                vg_g = vg[:, sl]
                rg = _rms_scale(vg_g)
                vhat = vg_g * rg
                gain = gv_ref[:, sl]
                vn = (vhat * gain).astype(BF)
                w_g = _tril_bf16(w_ref, g)
                sv = _nn(w_g, vn) + b_ref[g]
                dmix_a = dmix_ref[:, sl]
                dsv = dmix_a * u[:, sl]
                dsvb = dsv.astype(BF)
                db_acc[g] += jnp.sum(dsv, axis=-1, keepdims=True)
                dw_ref[g] += _nt(dsvb, vn)
                dvn = _tn(w_g, dsvb)
                dgv_ref[:, sl] += jnp.sum(dvn * vhat, axis=0, keepdims=True)
                dvg = _rms_bwd(dvn * gain, vg_g, rg)
                dzm_ref[:, sl] = ((dmix_a * sv) * du_dz[:, sl]).astype(BF)
                dzm_ref[:, D_GMLP + g * GDIM: D_GMLP + (g + 1) * GDIM] = (dvg * dvg_dz[:, sl]).astype(BF)

            for kh in range(N_KV):
                ksl = slice(kh * HEAD_DIM, (kh + 1) * HEAD_DIM)
                vsl = slice(D_KV + kh * HEAD_DIM, D_KV + (kh + 1) * HEAD_DIM)
                k_prev, k_cur = kvp_ref[:, ksl], kvc_ref[:, ksl]
                v_prev, v_cur = kvp_ref[:, vsl], kvc_ref[:, vsl]
                q4 = _stack_heads(q_ref, kh, 0)
                p_prev, p_cur, p_sink = prob_ref[0, kh, :, :BLOCK], prob_ref[0, kh, :, BLOCK:], psink_ref[0, kh, :, 0:1]
                do4 = _stack_heads(dmix_ref, kh, D_GMLP)
                dp_prev = _nt(do4, v_prev)
                dp_cur = _nt(do4, v_cur)
                delta = jnp.sum(p_prev * dp_prev, axis=-1, keepdims=True) + jnp.sum(p_cur * dp_cur, axis=-1, keepdims=True)
                ds_prev = p_prev * (dp_prev - delta)
                ds_cur = p_cur * (dp_cur - delta)
                dsink_acc[kh] -= p_sink * delta
                dbias_ref[kh, :, :BLOCK] += ds_prev
                dbias_ref[kh, :, BLOCK:] += ds_cur
                dsb_prev = ds_prev.astype(BF)
                dsb_cur = ds_cur.astype(BF)
                dq4 = (_nn(dsb_prev, k_prev) + _nn(dsb_cur, k_cur)) * SCALE
                for g in range(Q_PER_KV):
                    first = 2 * D_GMLP + (kh * Q_PER_KV + g) * HEAD_DIM
                    dzm_ref[:, first:first + HEAD_DIM] = dq4[g * BLOCK:(g + 1) * BLOCK].astype(BF)
                dkv_ref[:, ksl] = (carry_ref[:, ksl] + _tn(dsb_prev, q4) * SCALE).astype(BF)
                dkv_ref[:, vsl] = (carry_ref[:, vsl] + _tn(p_prev.astype(BF), do4)).astype(BF)
                carry_ref[:, ksl] = _tn(dsb_cur, q4) * SCALE
                carry_ref[:, vsl] = _tn(p_cur.astype(BF), do4)

        @pl.when(i == nb)
        def _():
            dkv_ref[...] = carry_ref[...].astype(BF)
            row = lax.broadcasted_iota(jnp.int32, (BLOCK, BLOCK), 0)
            col = lax.broadcasted_iota(jnp.int32, (BLOCK, BLOCK), 1)
            for g in range(GROUPS):
                dw_ref[g] = jnp.where(col <= row, dw_ref[g], 0.0)
                db_ref[g:g + 1, :] = jnp.sum(jnp.where(col == row, db_acc[g], 0.0), axis=0, keepdims=True)
            bucket = bucket_ref[...]
            for b in range(N_Q, REL_BUCKETS):
                dattn_ref[N_Q, b] = 0.0
            for h in range(N_Q):
                kh, rows = _head_rows(h)
                dattn_ref[N_Q, h] = jnp.sum(dsink_acc[kh, rows, :])
                dbh = dbias_ref[kh, rows, :]
                for b in range(REL_BUCKETS):
                    dattn_ref[h, b] = jnp.sum(jnp.where(bucket == b, dbh, 0.0))

        @pl.when(i > 0)
        def _():
            dhn = _nn(dzm_before[...], win_ref[:D_MAIN, :]) + _nn(dkv_ref[...], win_ref[D_MAIN:, :])
            xv = x_ref[...]
            r = _rms_scale(xv)
            dg1_ref[...] += jnp.sum(dhn * (xv * r), axis=0, keepdims=True)
            dx_ref[...] = dh1_ref[...] + _rms_bwd(dhn * g1_ref[...], xv, r)

        @pl.when(i < nb)
        def _():
            dzm_before[...] = dzm_ref[...]

    cl = lambda i: jnp.minimum(i, nb - 1)
    before = lambda i: jnp.maximum(i - 1, 0)
    const = lambda shape: pl.BlockSpec(shape, lambda i: (0,) * len(shape))
    return _call(
        body,
        name="mixer_bwd",
        after=after,
        grid=(nb + 1,),
        in_specs=_mixer_specs(nb)[:7] + [
            const((BLOCK, 2 * BLOCK)),
            pl.BlockSpec((BLOCK, D_MODEL), lambda i: (cl(i), 0)),
            pl.BlockSpec((BLOCK, 2 * D_GMLP), lambda i: (cl(i), 0)),
            pl.BlockSpec((1, N_KV, ROWS4, 2 * BLOCK), lambda i: (cl(i), 0, 0, 0)),
            pl.BlockSpec((1, N_KV, ROWS4, LANES), lambda i: (cl(i), 0, 0, 0)),
            pl.BlockSpec((BLOCK, D_MODEL), lambda i: (before(i), 0)),
            pl.BlockSpec((BLOCK, D_MODEL), lambda i: (before(i), 0)),
            const((1, D_MODEL)),
            const((D_IN, D_MODEL)),
        ],
        out_specs=[
            pl.BlockSpec((BLOCK, D_MAIN), lambda i: (cl(i), 0)),
            pl.BlockSpec((BLOCK, 2 * D_KV), lambda i: (before(i), 0)),
            const((1, D_GMLP)),
            const((GROUPS, BLOCK, BLOCK)),
            const((GROUPS, BLOCK)),
            pl.BlockSpec(memory_space=pltpu.SMEM),
            pl.BlockSpec((BLOCK, D_MODEL), lambda i: (before(i), 0)),
            const((1, D_MODEL)),
        ],
        out_shape=[
            jax.ShapeDtypeStruct((s, D_MAIN), BF),
            jax.ShapeDtypeStruct((s, 2 * D_KV), BF),
            jax.ShapeDtypeStruct((1, D_GMLP), F32),
            jax.ShapeDtypeStruct((GROUPS, BLOCK, BLOCK), F32),
            jax.ShapeDtypeStruct((GROUPS, BLOCK), F32),
            jax.ShapeDtypeStruct((N_Q + 1, REL_BUCKETS), F32),
            jax.ShapeDtypeStruct((s, D_MODEL), F32),
            jax.ShapeDtypeStruct((1, D_MODEL), F32),
        ],
        scratch_shapes=[
            pltpu.VMEM((N_KV, ROWS4, 2 * BLOCK), F32),
            pltpu.VMEM((BLOCK, 2 * D_KV), F32),
            pltpu.VMEM((N_KV, ROWS4, 1), F32),
            pltpu.VMEM((GROUPS, BLOCK, 1), F32),
            pltpu.VMEM((BLOCK, D_MAIN), BF),
        ],
        compiler_params=_params(("arbitrary",)),
    )(zuv, qkv, qkv, qkv, gv, w_sp, b_sp, bucket, dmix, *saved, x, dh1, g1, w_in_t)


def _wgrad_in(dzm, dkv, hn, after=None):
    s = hn.shape[0]
    tm = 2 * D_KV
    n_main = D_MAIN // tm

    def body(dzm_ref, dkv_ref, hn_ref, o_ref):
        i = pl.program_id(0)

        @pl.when(i < n_main)
        def _():
            o_ref[...] = _tn(dzm_ref[...], hn_ref[...])

        @pl.when(i == n_main)
        def _():
            o_ref[...] = _tn(dkv_ref[...], hn_ref[...])

    return _call(
        body,
        name="wgrad_in",
        after=after,
        grid=(n_main + 1,),
        in_specs=[
            pl.BlockSpec((s, tm), lambda i: (0, jnp.minimum(i, n_main - 1))),
            pl.BlockSpec((s, tm), lambda i: (0, 0)),
            pl.BlockSpec((s, D_MODEL), lambda i: (0, 0)),
        ],
        out_specs=pl.BlockSpec((tm, D_MODEL), lambda i: (i, 0)),
        out_shape=jax.ShapeDtypeStruct((D_IN, D_MODEL), F32),
        compiler_params=_params(("arbitrary",)),
    )(dzm, dkv, hn)


def _wgrad(a, b, tm, tn, name, peer_cols=0, after=None):
    s, m = a.shape
    n = b.shape[1]

    def body(a_ref, b_ref, o_ref, at_ref):
        @pl.when(pl.program_id(1) == 0)
        def _():
            at_ref[...] = a_ref[...].astype(BF).T

        r = _nn(at_ref[...], b_ref[...])
        if peer_cols:
            for q in range(tn // peer_cols):
                o_ref[q] = r[:, q * peer_cols:(q + 1) * peer_cols]
        else:
            o_ref[...] = r

    if peer_cols:
        out_spec = pl.BlockSpec((tn // peer_cols, tm, peer_cols), lambda i, j: (j, i, 0))
        out_shape = jax.ShapeDtypeStruct((n // peer_cols, m, peer_cols), F32)
    else:
        out_spec = pl.BlockSpec((tm, tn), lambda i, j: (i, j))
        out_shape = jax.ShapeDtypeStruct((m, n), F32)
    return _call(
        body,
        name=name,
        after=after,
        grid=(m // tm, n // tn),
        in_specs=[pl.BlockSpec((s, tm), lambda i, j: (0, i)), pl.BlockSpec((s, tn), lambda i, j: (0, j))],
        out_specs=out_spec,
        out_shape=out_shape,
        scratch_shapes=[pltpu.VMEM((tm, s), BF)],
        compiler_params=_params(("arbitrary", "arbitrary")),
    )(a, b)


def _adamw_math(w, g, m, v):
    m_new = ADAM_B1 * m + (1.0 - ADAM_B1) * g
    v_new = ADAM_B2 * v + (1.0 - ADAM_B2) * jnp.square(g)
    m_hat = m_new / (1.0 - ADAM_B1 ** ADAM_STEP)
    v_hat = v_new / (1.0 - ADAM_B2 ** ADAM_STEP)
    delta = -ADAM_LR * (m_hat / (jnp.sqrt(v_hat) + ADAM_EPS) + ADAM_WD * w)
    return delta, m_new, v_new


def _final_adamw(part, recv, w, m, v, name):
    r, c = w.shape
    tr = min(r, 512)

    def body(p_ref, r_ref, w_ref, m_ref, v_ref, g_ref, d_ref, mo_ref, vo_ref):
        g = p_ref[...]
        for j in range(3):
            g = g + r_ref[j].astype(F32)
        g_ref[...] = g
        d_ref[...], mo_ref[...], vo_ref[...] = _adamw_math(w_ref[...], g, m_ref[...], v_ref[...])

    spec = pl.BlockSpec((tr, c), lambda i: (i, 0))
    return _call(
        body,
        name=name,
        grid=(r // tr,),
        in_specs=[spec, pl.BlockSpec((3, tr, c), lambda i: (0, i, 0)), spec, spec, spec],
        out_specs=[spec] * 4,
        out_shape=[jax.ShapeDtypeStruct((r, c), F32)] * 4,
        compiler_params=_params(("arbitrary",)),
    )(part, recv, w, m, v)


def _rs_sum(g, land, blocks, name):
    _, r, c = g.shape
    tr = min(r, 256)

    def body(blk_ref, g0_ref, g1_ref, g2_ref, g3_ref, l_ref, part_ref, send_ref):
        part_ref[...] = g0_ref[0] + l_ref[0]
        for j, gj_ref in enumerate((g1_ref, g2_ref, g3_ref)):
            send_ref[j] = (gj_ref[0] + l_ref[j + 1]).astype(BF)

    def pick(j):
        return pl.BlockSpec((1, tr, c), lambda i, blk: (blk[j], i, 0))

    return _call(
        body,
        name=name,
        grid_spec=pltpu.PrefetchScalarGridSpec(
            num_scalar_prefetch=1,
            grid=(r // tr,),
            in_specs=[pick(0), pick(1), pick(2), pick(3), pl.BlockSpec((4, tr, c), lambda i, blk: (0, i, 0))],
            out_specs=[pl.BlockSpec((tr, c), lambda i, blk: (i, 0)), pl.BlockSpec((3, tr, c), lambda i, blk: (0, i, 0))],
        ),
        out_shape=[jax.ShapeDtypeStruct((r, c), F32), jax.ShapeDtypeStruct((3, r, c), BF)],
        compiler_params=_params(("arbitrary",)),
    )(blocks, g, g, g, g, land)


def _final_adamw_few(parts, recvs, ws, ms, vs, name):
    n = len(ws)

    def body(*refs):
        ins, outs = refs[:5 * n], refs[5 * n:]
        for k in range(n):
            p_ref, r_ref, w_ref, m_ref, v_ref = (ins[q * n + k] for q in range(5))
            g = p_ref[...]
            for j in range(3):
                g = g + r_ref[j].astype(F32)
            outs[k][...] = g
            outs[n + k][...], outs[2 * n + k][...], outs[3 * n + k][...] = _adamw_math(w_ref[...], g, m_ref[...], v_ref[...])

    outs = _call(
        body,
        name=name,
        out_shape=[jax.ShapeDtypeStruct(w.shape, F32) for w in ws] * 4,
        compiler_params=_params(),
    )(*parts, *recvs, *ws, *ms, *vs)
    return outs[:n], outs[n:2 * n], outs[2 * n:3 * n], outs[3 * n:]


def _rs_sum_few(gs, lands, blocks, name):
    n = len(gs)

    def body(blk_ref, *refs):
        g_refs, l_refs, part_refs, send_refs = refs[:n], refs[n:2 * n], refs[2 * n:3 * n], refs[3 * n:]
        for k in range(n):
            part_refs[k][...] = g_refs[k][blk_ref[0]] + l_refs[k][0]
            for j in range(3):
                send_refs[k][j] = (g_refs[k][blk_ref[j + 1]] + l_refs[k][j + 1]).astype(BF)

    outs = _call(
        body,
        name=name,
        in_specs=[pl.BlockSpec(memory_space=pltpu.SMEM)] + [pl.BlockSpec(memory_space=pltpu.VMEM)] * (2 * n),
        out_shape=[jax.ShapeDtypeStruct(g.shape[1:], F32) for g in gs] + [jax.ShapeDtypeStruct((3,) + g.shape[1:], BF) for g in gs],
        compiler_params=_params(),
    )(blocks, *gs, *lands)
    return [(outs[k], outs[n + k]) for k in range(n)]


def _adamw_small(own, recv, weights, moms, vels, after):
    n_w = len(weights)

    def body(*refs):
        own_refs, recv_refs = refs[:6], refs[6:12]
        w_refs, m_refs, v_refs = refs[12:12 + n_w], refs[12 + n_w:12 + 2 * n_w], refs[12 + 2 * n_w:12 + 3 * n_w]
        outs = refs[12 + 3 * n_w:]
        loss_ref, g_refs, d_refs = outs[0], outs[1:1 + n_w], outs[1 + n_w:1 + 2 * n_w]
        mo_refs, vo_refs = outs[1 + 2 * n_w:1 + 3 * n_w], outs[1 + 3 * n_w:]
        my_chip = 2 * lax.axis_index("x") + lax.axis_index("y")

        def total(k, *index):
            index = index or (slice(None),) * (len(own_refs[k].shape) - 1)
            across = [own_refs[k][(0, *index)], recv_refs[k][(1, *index)], recv_refs[k][(0, *index)], recv_refs[k][(2, *index)]]
            across = [v.astype(F32) for v in across]
            acc = None
            for chip in range(4):
                r = chip ^ my_chip
                term = jnp.where(r == 0, across[0], jnp.where(r == 1, across[1], jnp.where(r == 2, across[2], across[3])))
                acc = term if acc is None else acc + term
            return acc

        grads = [
            total(5), total(1), total(2), total(3),
            total(4, slice(N_Q, None), slice(0, N_Q)), total(4, slice(0, N_Q), slice(None)),
            total(0, slice(0, 1), slice(None)), total(0, slice(1, 2), slice(None))]
        loss_ref[...] = total(0, slice(2, 3), slice(0, 1))
        for k in range(n_w):
            g_refs[k][...] = grads[k]
            d_refs[k][...], mo_refs[k][...], vo_refs[k][...] = _adamw_math(w_refs[k][...], grads[k], m_refs[k][...], v_refs[k][...])

    shapes = [jax.ShapeDtypeStruct(w.shape, F32) for w in weights]
    outs = _call(
        body,
        name="adamw_small",
        after=after,
        in_specs=[pl.BlockSpec(memory_space=pltpu.VMEM)] * (12 + 3 * n_w),
        out_shape=[jax.ShapeDtypeStruct((1, 1), F32)] + shapes * 4,
        compiler_params=_params(),
    )(*own, *recv, *weights, *moms, *vels)
    return outs[0], outs[1:1 + n_w], outs[1 + n_w:1 + 2 * n_w], outs[1 + 2 * n_w:1 + 3 * n_w], outs[1 + 3 * n_w:]


def _place():
    x, y, c = lax.axis_index("x"), lax.axis_index("y"), lax.axis_index("c")
    return x, y, c, [(1 - x, y), (x, 1 - y), (1 - x, 1 - y)]


def _dev_index(px, py, pc):
    return 4 * px + 2 * py + pc


HBM_SPEC = pl.BlockSpec(memory_space=pltpu.HBM)
SEM_SPEC = pl.BlockSpec(memory_space=pltpu.SEMAPHORE)
ANY_SPEC = pl.BlockSpec(memory_space=pl.ANY)
DATAFLOW = pltpu.SideEffectType.DATAFLOW_SIDE_EFFECTING


def _hbm(a):
    return pltpu.with_memory_space_constraint(a, pltpu.HBM)


def _prep_weights(shards):
    k_n = len(shards)

    def body(*refs):
        ins, outs, stage, sems = refs[:k_n], refs[k_n:2 * k_n], refs[2 * k_n:3 * k_n], refs[3 * k_n]
        x, y, c, _ = _place()
        copies = []
        for k in range(k_n):
            stage[k][...] = ins[k][...].astype(BF)
            copies.append(pltpu.make_async_copy(stage[k], outs[k].at[_dev_index(x, y, c)], sems.at[k]))
            copies[k].start()
        for cp in copies:
            cp.wait()

    return _call(
        body,
        name="prep_weights",
        in_specs=[pl.BlockSpec(memory_space=pltpu.VMEM)] * k_n,
        out_specs=[ANY_SPEC] * k_n,
        out_shape=[jax.ShapeDtypeStruct((N_DEV,) + sh.shape, BF) for sh in shards],
        scratch_shapes=[pltpu.VMEM(sh.shape, BF) for sh in shards] + [pltpu.SemaphoreType.DMA((k_n,))],
        compiler_params=_params(),
    )(*shards)


def _ag_piece(land_k, block, half, peer, send_sem, recv_sem):
    ref = land_k.at[_dev_index(*block)]
    if half is not None:
        rows = land_k.shape[1] // 2
        ref = ref.at[pl.ds(half * rows, rows)]
    return pltpu.make_async_remote_copy(
        src_ref=ref, dst_ref=ref, send_sem=send_sem, recv_sem=recv_sem, device_id=peer, device_id_type=MESH)


def _ag_plan():
    x, y, c, _ = _place()
    me, sib = (x, y, c), (x, y, 1 - c)
    xn, yn, diag = (1 - x, y, c), (x, 1 - y, c), (1 - x, 1 - y, c)
    return dict(
        relay_halves=[(me, 0, xn), (me, 1, yn)],
        others=[(me, None, sib), (me, 1, xn), (me, 0, yn)],
        relays=[(xn, 0, yn), (yn, 1, xn)],
        near=[(xn, None, sib), (yn, None, sib)],
        far=[(diag, None, sib)],
    )


def _ag_stage(land, stage, send_sems, recv_sems, act):
    copies = _ag_plan()[stage]
    n = len(copies)
    for k in range(len(land)):
        for j, (block, half, peer) in enumerate(copies):
            cp = _ag_piece(land[k], block, half, peer, send_sems.at[n * k + j], recv_sems.at[n * k + j])
            if act == "start":
                cp.start()
            else:
                cp.wait_send()
                cp.wait_recv()


def _sem_shapes(*counts):
    return [pltpu.SemaphoreType.DMA((n,)) for n in counts for _ in range(2)]


def _ag_start(first, rest):
    lands = list(first) + list(rest)
    k_n, k_first = len(lands), len(first)
    k_rest = k_n - k_first

    def body(*refs):
        land = refs[:k_n]
        sems = refs[k_n:k_n + 6]
        token = refs[-1]
        x, y, c, chips = _place()
        targets = [(x, y, 1 - c)] + [(*chip, c) for chip in chips]
        for k in range(k_first):
            for j, to in enumerate(targets):
                _ag_piece(land[k], (x, y, c), None, to, sems[0].at[4 * k + j], sems[1].at[4 * k + j]).start()
        _ag_stage(land[k_first:], "relay_halves", sems[2], sems[3], "start")
        _ag_stage(land[k_first:], "others", sems[4], sems[5], "start")
        token[...] = jnp.zeros_like(token)

    outs = pl.pallas_call(
        body,
        name="ag_start",
        in_specs=[HBM_SPEC] * k_n,
        out_specs=(*[SEM_SPEC] * 6, *[HBM_SPEC] * k_n, pl.BlockSpec(memory_space=pltpu.VMEM)),
        out_shape=(*_sem_shapes(4 * k_first, 2 * k_rest, 3 * k_rest),
                   *[pltpu.HBM(a.shape, a.dtype) for a in lands], jax.ShapeDtypeStruct((8, LANES), F32)),
        input_output_aliases={k: 6 + k for k in range(k_n)},
        compiler_params=pltpu.CompilerParams(has_side_effects=DATAFLOW),
    )(*[_hbm(a) for a in lands])
    flying = list(outs[6:6 + k_n])
    return (outs[0], outs[1], flying[:k_first]), (outs[2:6], flying[k_first:]), outs[-1]


def _ag_split_call(lands, waits, starts, after, name):
    k_n = len(lands)
    plan_sizes = dict(relay_halves=2, others=3, relays=2, near=2, far=1)
    n_in, n_out = 2 * len(waits), 2 * len(starts)

    def body(*refs):
        land = refs[:k_n]
        in_sems = refs[k_n:k_n + n_in]
        out_sems, token = refs[len(refs) - 1 - n_out:len(refs) - 1], refs[-1]
        for w, (stage, _, _) in enumerate(waits):
            _ag_stage(land, stage, in_sems[2 * w], in_sems[2 * w + 1], "wait")
            if w < len(starts):
                _ag_stage(land, starts[w], out_sems[2 * w], out_sems[2 * w + 1], "start")
        token[...] = jnp.zeros_like(token)

    outs = pl.pallas_call(
        body,
        name=name,
        in_specs=[HBM_SPEC] * k_n + [SEM_SPEC] * n_in + [ANY_SPEC],
        out_specs=(*[HBM_SPEC] * k_n, *[SEM_SPEC] * n_out, pl.BlockSpec(memory_space=pltpu.VMEM)),
        out_shape=(*[pltpu.HBM(a.shape, a.dtype) for a in lands], *_sem_shapes(*[plan_sizes[s] * k_n for s in starts]),
                   jax.ShapeDtypeStruct((8, LANES), F32)),
        input_output_aliases={k: k for k in range(k_n)},
        compiler_params=pltpu.CompilerParams(has_side_effects=DATAFLOW),
    )(*lands, *[s for _, a, b in waits for s in (a, b)], after)
    return list(outs[:k_n]), list(outs[k_n:k_n + n_out]), outs[-1]


def _ag_mid(lands, send_sems, recv_sems, after, name):
    k_n = len(lands)

    def body(*refs):
        land = refs[:k_n]
        send1, recv1 = refs[k_n], refs[k_n + 1]
        fwd_send, fwd_recv = refs[-2], refs[-1]
        x, y, c, chips = _place()
        sources = [(x, y, 1 - c)] + [(*chip, c) for chip in chips]
        for k in range(k_n):
            mine = land[k].at[_dev_index(x, y, c)]
            for j, frm in enumerate(sources):
                got = land[k].at[_dev_index(*frm)]
                cp = pltpu.make_async_remote_copy(
                    src_ref=mine, dst_ref=got, send_sem=send1.at[4 * k + j], recv_sem=recv1.at[4 * k + j], device_id=frm, device_id_type=MESH)
                cp.wait_send()
                cp.wait_recv()
                if j >= 1:
                    pltpu.make_async_remote_copy(
                        src_ref=got, dst_ref=got, send_sem=fwd_send.at[3 * k + j - 1], recv_sem=fwd_recv.at[3 * k + j - 1],
                        device_id=(x, y, 1 - c), device_id_type=MESH).start()

    outs = pl.pallas_call(
        body,
        name=name,
        in_specs=[HBM_SPEC] * k_n + [SEM_SPEC, SEM_SPEC, ANY_SPEC],
        out_specs=(*[HBM_SPEC] * k_n, SEM_SPEC, SEM_SPEC),
        out_shape=(*[pltpu.HBM(a.shape, a.dtype) for a in lands], pltpu.SemaphoreType.DMA((3 * k_n,)), pltpu.SemaphoreType.DMA((3 * k_n,))),
        input_output_aliases={k: k for k in range(k_n)},
        compiler_params=pltpu.CompilerParams(has_side_effects=DATAFLOW),
    )(*lands, send_sems, recv_sems, after)
    return list(outs[:k_n]), outs[-2], outs[-1]


def _ag_end(lands, fwd_send, fwd_recv, name):
    k_n = len(lands)

    def body(*refs):
        land = refs[:k_n]
        fsend, frecv = refs[k_n], refs[k_n + 1]
        x, y, c, chips = _place()
        for k in range(k_n):
            for j, chip in enumerate(chips):
                cp = pltpu.make_async_remote_copy(
                    src_ref=land[k].at[_dev_index(*chip, c)], dst_ref=land[k].at[_dev_index(*chip, 1 - c)],
                    send_sem=fsend.at[3 * k + j], recv_sem=frecv.at[3 * k + j], device_id=(x, y, 1 - c), device_id_type=MESH)
                cp.wait_send()
                cp.wait_recv()

    outs = pl.pallas_call(
        body,
        name=name,
        in_specs=[HBM_SPEC] * k_n + [SEM_SPEC, SEM_SPEC],
        out_specs=tuple([HBM_SPEC] * k_n),
        out_shape=tuple(pltpu.HBM(a.shape, a.dtype) for a in lands),
        input_output_aliases={k: k for k in range(k_n)},
        compiler_params=pltpu.CompilerParams(has_side_effects=DATAFLOW),
    )(*lands, fwd_send, fwd_recv)
    return list(outs)


def _chips4():
    x, y, c, others = _place()
    return x, y, c, [(x, y)] + others


def _route_sibling(j):
    x, y, c, chips = _chips4()
    return _dev_index(*chips[j], 1 - c), j, (x, y, 1 - c)


def _route_chips(j):
    x, y, c, chips = _chips4()
    return j, j, (*chips[j + 1], c)


def _route_sibling_whole(j):
    x, y, c, _ = _chips4()
    return 0, 0, (x, y, 1 - c)


def _route_chips_whole(j):
    x, y, c, chips = _chips4()
    return 0, j, (*chips[j + 1], c)


def _xchg_copies(routes, src, dst, send_sems, recv_sems):
    copies, sem = [], 0
    for k, (route, n) in enumerate(routes):
        for j in range(n):
            si, di, peer = route(j)
            copies.append(pltpu.make_async_remote_copy(
                src_ref=src[k].at[si], dst_ref=dst[k].at[di], send_sem=send_sems.at[sem], recv_sem=recv_sems.at[sem],
                device_id=peer, device_id_type=MESH))
            sem += 1
    return copies


def _xchg_start(srcs, slot_shapes, routes, name, after=None):
    k_n = len(srcs)
    n_in = 2 * k_n + (after is not None)
    n_sem = sum(n for _, n in routes)
    dsts = [lax.empty((n,) + tuple(sh), a.dtype) for sh, a, (_, n) in zip(slot_shapes, srcs, routes)]

    def body(*refs):
        src, dst = refs[:k_n], refs[k_n:2 * k_n]
        send_sems, recv_sems, token = refs[n_in], refs[n_in + 1], refs[-1]
        for cp in _xchg_copies(routes, src, dst, send_sems, recv_sems):
            cp.start()
        token[...] = jnp.zeros_like(token)

    arrays = list(srcs) + dsts
    outs = pl.pallas_call(
        body,
        name=name,
        in_specs=[HBM_SPEC] * (2 * k_n) + [ANY_SPEC] * (n_in - 2 * k_n),
        out_specs=(SEM_SPEC, SEM_SPEC, *[HBM_SPEC] * (2 * k_n), pl.BlockSpec(memory_space=pltpu.VMEM)),
        out_shape=(pltpu.SemaphoreType.DMA((n_sem,)), pltpu.SemaphoreType.DMA((n_sem,)),
                   *[pltpu.HBM(a.shape, a.dtype) for a in arrays], jax.ShapeDtypeStruct((8, LANES), F32)),
        input_output_aliases={i: 2 + i for i in range(2 * k_n)},
        compiler_params=pltpu.CompilerParams(has_side_effects=DATAFLOW),
    )(*[_hbm(a) for a in arrays], *([] if after is None else [after]))
    return outs[0], outs[1], list(outs[2:2 + k_n]), list(outs[2 + k_n:2 + 2 * k_n]), outs[-1]


def _xchg_wait(send_sems, recv_sems, srcs, dsts, routes, after, name):
    k_n = len(srcs)

    def body(*refs):
        src, dst = refs[:k_n], refs[k_n:2 * k_n]
        for cp in _xchg_copies(routes, src, dst, refs[2 * k_n], refs[2 * k_n + 1]):
            cp.wait_send()
            cp.wait_recv()

    arrays = list(srcs) + list(dsts)
    outs = pl.pallas_call(
        body,
        name=name,
        in_specs=[HBM_SPEC] * (2 * k_n) + [SEM_SPEC, SEM_SPEC, ANY_SPEC],
        out_specs=tuple([HBM_SPEC] * (2 * k_n)),
        out_shape=tuple(pltpu.HBM(a.shape, a.dtype) for a in arrays),
        input_output_aliases={i: i for i in range(2 * k_n)},
        compiler_params=pltpu.CompilerParams(has_side_effects=DATAFLOW),
    )(*arrays, send_sems, recv_sems, after)
    return list(outs[:k_n]), list(outs[k_n:])


SMALL = ("norm1_gain", "gmlp_v_gain", "w_spatial", "b_spatial", "attn_sinks", "rel_bias_table", "norm2_gain", "final_gain")
LANES = 128


def _swap_start(grads, smalls, tag, after=None):
    srcs = list(grads) + [a[None] for a in smalls]
    shapes = [g.shape[1:] for g in grads] + [a.shape for a in smalls]
    routes = [(_route_sibling, 4)] * len(grads) + [(_route_sibling_whole, 1)] * len(smalls)
    return _xchg_start(srcs, shapes, routes, f"rs_{tag}_swap_start", after), routes, len(grads)


def _swap_sums(swap, names, after, tag, few=False, small_dtypes=None):
    (send1, recv1, src1, land1, _), routes, n_rs = swap
    x, y, c, chips = _chips4()
    blocks = jnp.stack([_dev_index(*chip, c) for chip in chips]).astype(jnp.int32)
    src1, land1 = _xchg_wait(send1, recv1, src1, land1, routes, after, f"rs_{tag}_swap_wait")
    if few:
        sums = _rs_sum_few(src1[:n_rs], land1[:n_rs], blocks, f"rs_{tag}_sum")
    else:
        sums = [_rs_sum(g, land, blocks, f"rs_sum_{n}") for g, land, n in zip(src1[:n_rs], land1[:n_rs], names)]
    if len(src1) == n_rs:
        return sums, []
    return sums, _pair_sum(src1[n_rs:], land1[n_rs:], small_dtypes or [F32] * (len(src1) - n_rs), f"rs_{tag}_sum_small")


def _pair_sum(mine, theirs, dtypes, name):
    def body(*refs):
        n = len(refs) // 3
        for k in range(n):
            refs[2 * n + k][...] = (refs[k][...] + refs[n + k][...]).astype(dtypes[k])

    return _call(
        body,
        name=name,
        out_shape=[jax.ShapeDtypeStruct(a.shape, dt) for a, dt in zip(mine, dtypes)],
        compiler_params=_params(),
    )(*mine, *theirs)


def _chips_start(sums, small_sums, tag):
    sends = [ps[1] for ps in sums] + list(small_sums)
    routes = [(_route_chips, 3)] * len(sums) + [(_route_chips_whole, 3)] * len(small_sums)
    return _xchg_start(sends, [a.shape[1:] for a in sends], routes, f"rs_{tag}_chips_start"), routes


def _chips_wait(chips, after, tag):
    (send2, recv2, src2, land2, _), routes = chips
    return _xchg_wait(send2, recv2, src2, land2, routes, after, f"rs_{tag}_chips_wait")


def kernel(x, p, norm1_gain, w_in, gmlp_v_gain, w_spatial, b_spatial, attn_sinks, rel_bias_table, w_out, norm2_gain, w_ff1, w_ff2, w_ple_proj, w_ple_gate, final_gain, loss_target, m_norm1_gain, m_w_in, m_gmlp_v_gain, m_w_spatial, m_b_spatial, m_attn_sinks, m_rel_bias_table, m_w_out, m_norm2_gain, m_w_ff1, m_w_ff2, m_w_ple_proj, m_w_ple_gate, m_final_gain, v_norm1_gain, v_w_in, v_gmlp_v_gain, v_w_spatial, v_b_spatial, v_attn_sinks, v_rel_bias_table, v_w_out, v_norm2_gain, v_w_ff1, v_w_ff2, v_w_ple_proj, v_w_ple_gate, v_final_gain):
    args = dict(locals())
    s = x.shape[1]
    big = ("w_in", "w_out", "w_ff1", "w_ff2", "w_ple_proj", "w_ple_gate")

    x2, p2, t2 = x.reshape(s, D_MODEL), p.reshape(s, PLE_DIM), loss_target.reshape(s, D_MODEL)
    g1, gv, w_sp, b_sp, sinks, table, g2, gf = (args[n] for n in SMALL)
    bucket = jnp.asarray(_bucket_table())
    b_col = b_sp.reshape(GROUPS, BLOCK, 1)

    def shard(name):
        return args[name][0].T if name.endswith("w_in") else args[name][0]

    lands = _prep_weights([shard(n) for n in big])
    (send_in, recv_in, fly_in), (rest_sems, fly_rest), token = _ag_start(lands[:1], lands[1:])
    mid_in, fwd_send_in, fwd_recv_in = _ag_mid(fly_in, send_in, recv_in, token, "ag_mid_w_in")
    g_in = _ag_end(mid_in, fwd_send_in, fwd_recv_in, "ag_end_w_in")[0]
    full_in = g_in.reshape(D_IN, D_MODEL)

    zuv, qkv, hn1 = _in_proj(x2, g1, full_in)
    fly_rest, relay_sems, relayed = _ag_split_call(
        fly_rest, [("relay_halves", *rest_sems[:2])], ["relays"], zuv, "ag_relay")
    mix, *saved = _mixer_fwd(zuv, qkv, gv, w_sp[0], b_col, sinks, table.T, bucket, after=relayed)
    fly_rest, fwd_sems, _ = _ag_split_call(
        fly_rest, [("others", *rest_sems[2:]), ("relays", *relay_sems)], ["near", "far"], mix, "ag_mid_rest")
    g_out, g_ff1, g_ff2, g_proj, g_gate = _ag_split_call(
        fly_rest, [("near", *fwd_sems[:2]), ("far", *fwd_sems[2:])], [], mix, "ag_end_rest")[0]
    full_out, full_ff2, full_gate = g_out.reshape(D_MODEL, D_MODEL), g_ff2.reshape(D_FF, D_MODEL), g_gate.reshape(D_MODEL, D_MODEL)
    full_proj = g_proj.transpose(1, 0, 2).reshape(PLE_DIM, D_MODEL)

    (tail_small, dh1, dh1b, dmix, hn2, a, df, h2, dh2, dgl, dpp) = _tail(
        x2, mix, p2, t2, g2, gf.reshape(1, D_MODEL), full_out, g_ff1, full_ff2, full_gate, full_proj)

    mid_names = ("w_out", "w_ple_proj", "w_ple_gate")
    gw_ff1 = _wgrad(hn2, df, D_MODEL, 2 * D_FF // N_DEV, "wgrad_ff1", peer_cols=D_FF // N_DEV)
    ff1_swap = _swap_start([gw_ff1], [], "ff1")
    gw_ff2 = _wgrad(a, dh2, 1024, D_MODEL, "wgrad_ff2", after=ff1_swap[0][4]).reshape(N_DEV, D_FF // N_DEV, D_MODEL)
    ff1_sums, _ = _swap_sums(ff1_swap, ("w_ff1",), gw_ff2, "ff1")
    ff1_chips = _chips_start(ff1_sums, [], "ff1")
    ff2_swap = _swap_start([gw_ff2], [], "ff2", after=ff1_chips[0][4])
    gw_gate = _wgrad(h2, dgl, D_MODEL, 512, "wgrad_gate", after=ff2_swap[0][4]).reshape(N_DEV, D_MODEL // N_DEV, D_MODEL)
    gw_proj = _wgrad(p2, dpp, PLE_DIM, D_MODEL // 2, "wgrad_proj", peer_cols=D_MODEL // N_DEV, after=gw_gate)
    ff2_sums, _ = _swap_sums(ff2_swap, ("w_ff2",), gw_proj, "ff2")
    ff2_chips = _chips_start(ff2_sums, [], "ff2")
    gw_out = _wgrad(mix, dh1b, D_MODEL, 512, "wgrad_out", after=ff2_chips[0][4]).reshape(N_DEV, D_MODEL // N_DEV, D_MODEL)
    mid_swap = _swap_start([gw_out, gw_proj, gw_gate], [tail_small], "mid")

    dzm, dkv, d_gv, d_wsp, d_bsp, d_attn, dx, d_g1 = _mixer_bwd(
        zuv, qkv, dmix, saved, gv, w_sp[0], b_col, bucket, x2, dh1, g1, full_in, after=mid_swap[0][4])
    grad_x = dx.reshape(x.shape)
    gw_in = _wgrad_in(dzm, dkv, hn1).reshape(N_DEV, D_IN // N_DEV, D_MODEL)
    in_swap = _swap_start([gw_in], [d_gv, d_wsp, d_bsp, d_attn, d_g1], "in")
    mid_sums, mid_small = _swap_sums(mid_swap, mid_names, in_swap[0][4], "mid", few=True)
    mid_chips = _chips_start(mid_sums, mid_small, "mid")
    in_sums, in_small = _swap_sums(in_swap, ("w_in",), mid_chips[0][4], "in", small_dtypes=[F32, BF, F32, F32, F32])
    in_chips = _chips_start(in_sums, in_small, "in")

    grads, deltas, new_m, new_v = {}, {}, {}, {}
    after, small_own, small_recv = in_chips[0][4], [], []
    for chips, sums, names, tag in ((ff1_chips, ff1_sums, ("w_ff1",), "ff1"), (ff2_chips, ff2_sums, ("w_ff2",), "ff2"),
                                    (mid_chips, mid_sums, mid_names, "mid"), (in_chips, in_sums, ("w_in",), "in")):
        sent, recvs = _chips_wait(chips, after, tag)
        small_own += sent[len(names):]
        small_recv += recvs[len(names):]
        if tag == "mid":
            done = _final_adamw_few([ps[0] for ps in sums], recvs[:len(names)], [shard(n) for n in names],
                                    [shard("m_" + n) for n in names], [shard("v_" + n) for n in names], "adamw_mid")
            done = list(zip(*done))
        else:
            done = [_final_adamw(part, recv, shard(n), shard("m_" + n), shard("v_" + n), "adamw_" + n)
                    for n, (part, _), recv in zip(names, sums, recvs)]
        for n, results in zip(names, done):
            for dst, arr in zip((grads, deltas, new_m, new_v), results):
                dst[n] = (arr.T if n == "w_in" else arr)[None]
            after = results[1]

    views = {"w_spatial": (GROUPS, BLOCK, BLOCK), "b_spatial": (GROUPS, BLOCK), "final_gain": (1, D_MODEL)}
    def view(name, base):
        return args[name].T if base == "rel_bias_table" else args[name].reshape(views.get(base, args[name].shape))

    small_in = [[view(pre + n, n) for n in SMALL] for pre in ("", "m_", "v_")]
    loss, *small_out = _adamw_small(small_own, small_recv, *small_in, after=after)
    for dst, arrays in zip((grads, deltas, new_m, new_v), small_out):
        for n, arr in zip(SMALL, arrays):
            dst[n] = arr.T if n == "rel_bias_table" else arr.reshape(args[n].shape)
    loss = loss[0, 0]

    order = ("norm1_gain", "w_in", "gmlp_v_gain", "w_spatial", "b_spatial", "attn_sinks", "rel_bias_table", "w_out",
             "norm2_gain", "w_ff1", "w_ff2", "w_ple_proj", "w_ple_gate", "final_gain")
    return (loss, grad_x, *[grads[n] for n in order], *[deltas[n] for n in order],
            *[new_m[n] for n in order], *[new_v[n] for n in order])
```

```python
import math

import numpy as np
import jax
import jax.numpy as jnp
from jax import lax
from jax.experimental import pallas as pl
from jax.experimental.pallas import tpu as pltpu

F32 = jnp.float32
BF = jnp.bfloat16
MESH = pl.DeviceIdType.MESH
N_DEV = 8

D_MODEL = 1024
PLE_DIM = 256
D_GMLP = 512
GROUPS = 4
GDIM = 128
BLOCK = 128
D_ATTN = 512
HEAD_DIM = 64
N_Q = 8
Q_PER_KV = 4
N_KV = N_Q // Q_PER_KV
ROWS4 = Q_PER_KV * BLOCK
D_KV = 128
D_FF = 4096
D_IN = 1792
D_MAIN = 2 * D_GMLP + D_ATTN
REL_BUCKETS = 32
EPS = 1e-6
NEG_INF = -1e30
SCALE = HEAD_DIM ** -0.5
GELU_C = math.sqrt(2.0 / math.pi)
GELU_A = 0.044715

ADAM_LR = 0.001
ADAM_B1 = 0.9
ADAM_B2 = 0.999
ADAM_EPS = 1e-08
ADAM_WD = 0.01
ADAM_STEP = 10

V7X_VMEM_LIMIT = 60000 * 1024
TOK_TILE = 256
IO_TOK_TILE = 512


def _call(body, after=None, **kw):
    if after is None:
        return pl.pallas_call(body, **kw)
    n_in = len(kw["in_specs"])

    def ordered(*refs):
        body(*refs[:n_in], *refs[n_in + 1:])

    kw["in_specs"] = list(kw["in_specs"]) + [pl.BlockSpec(memory_space=pl.ANY)]
    fn = pl.pallas_call(ordered, **kw)
    return lambda *operands: fn(*operands, after)


def _params(sem=None):
    if sem is None:
        return pltpu.CompilerParams(vmem_limit_bytes=V7X_VMEM_LIMIT)
    return pltpu.CompilerParams(dimension_semantics=sem, vmem_limit_bytes=V7X_VMEM_LIMIT)


def _nn(a, b):
    return jnp.dot(a, b, preferred_element_type=F32)


def _nt(a, b):
    return lax.dot_general(a, b, (((1,), (1,)), ((), ())), preferred_element_type=F32)


def _tn(a, b):
    return lax.dot_general(a, b, (((0,), (0,)), ((), ())), preferred_element_type=F32)


def _gelu_tanh(x):
    return jnp.tanh(GELU_C * (x + GELU_A * (x * x * x)))


def _gelu(x, t):
    return x * (0.5 * (1.0 + t))


def _gelu_and_grad(x, t):
    cdf = 0.5 * (1.0 + t)
    return x * cdf, cdf + 0.5 * x * (1.0 - t * t) * (GELU_C * (1.0 + 3.0 * GELU_A * (x * x)))


def _rms_scale(x):
    return lax.rsqrt(jnp.mean(x * x, axis=-1, keepdims=True) + EPS)


def _rms_bwd(dxn, x, r):
    return r * dxn - x * ((r * r * r) * jnp.mean(dxn * x, axis=-1, keepdims=True))


def _bucket_table():
    a = np.arange(BLOCK)[:, None]
    j = np.arange(2 * BLOCK)[None, :]
    n = BLOCK + a - j
    valid = (n >= 0) & (n < BLOCK)
    nc = np.maximum(n, 0)
    max_exact = REL_BUCKETS // 2
    nf = np.maximum(nc, 1).astype(np.float32)
    large = max_exact + (
        np.log(nf / np.float32(max_exact)) / np.float32(math.log(BLOCK / max_exact)) * np.float32(REL_BUCKETS - max_exact)
    ).astype(np.int32)
    large = np.minimum(large, REL_BUCKETS - 1)
    bucket = np.where(nc < max_exact, nc, large)
    return np.where(valid, bucket, -1).astype(np.int32)


def _in_proj(x, g1, w_in_t):
    s = x.shape[0]
    tm = min(IO_TOK_TILE, s)

    def body(x_ref, g_ref, w_ref, zuv_ref, qkv_ref, hn_ref):
        xv = x_ref[...]
        hn = ((xv * _rms_scale(xv)) * g_ref[...]).astype(BF)
        hn_ref[...] = hn
        z = _nt(hn, w_ref[...])
        zuv_ref[...] = z[:, : 2 * D_GMLP]
        qkv_ref[...] = z[:, 2 * D_GMLP:].astype(BF)

    return _call(
        body,
        name="in_proj",
        grid=(s // tm,),
        in_specs=[
            pl.BlockSpec((tm, D_MODEL), lambda i: (i, 0)),
            pl.BlockSpec((1, D_MODEL), lambda i: (0, 0)),
            pl.BlockSpec((D_IN, D_MODEL), lambda i: (0, 0)),
        ],
        out_specs=[
            pl.BlockSpec((tm, 2 * D_GMLP), lambda i: (i, 0)),
            pl.BlockSpec((tm, D_ATTN + 2 * D_KV), lambda i: (i, 0)),
            pl.BlockSpec((tm, D_MODEL), lambda i: (i, 0)),
        ],
        out_shape=[
            jax.ShapeDtypeStruct((s, 2 * D_GMLP), F32),
            jax.ShapeDtypeStruct((s, D_ATTN + 2 * D_KV), BF),
            jax.ShapeDtypeStruct((s, D_MODEL), BF),
        ],
        compiler_params=_params(("arbitrary",)),
    )(x, g1, w_in_t)


def _head_rows(h):
    kh, g = divmod(h, Q_PER_KV)
    return kh, slice(g * BLOCK, (g + 1) * BLOCK)


def _build_bias(bias_ref, bucket_ref, table_ref):
    bucket = bucket_ref[...]
    for h in range(N_Q):
        acc = jnp.zeros((BLOCK, 2 * BLOCK), F32)
        for b in range(REL_BUCKETS):
            acc = jnp.where(bucket == b, table_ref[h, b], acc)
        kh, rows = _head_rows(h)
        bias_ref[kh, rows, :] = acc


def _window_masks(i):
    row = lax.broadcasted_iota(jnp.int32, (ROWS4, BLOCK), 0) & (BLOCK - 1)
    col = lax.broadcasted_iota(jnp.int32, (ROWS4, BLOCK), 1)
    return (col > row) & (i > 0), col <= row


def _stack_heads(ref, kh, offset):
    first = offset + kh * Q_PER_KV * HEAD_DIM
    return jnp.concatenate(
        [ref[:, first + g * HEAD_DIM: first + (g + 1) * HEAD_DIM].astype(BF) for g in range(Q_PER_KV)], axis=0)


def _stack_sinks(sink_ref, kh):
    return jnp.concatenate([jnp.full((BLOCK, 1), sink_ref[0, kh * Q_PER_KV + g], F32) for g in range(Q_PER_KV)], axis=0)


def _tril_bf16(w_ref, g):
    row = lax.broadcasted_iota(jnp.int32, (BLOCK, BLOCK), 0)
    col = lax.broadcasted_iota(jnp.int32, (BLOCK, BLOCK), 1)
    return jnp.where(col <= row, w_ref[g], 0.0).astype(BF)


def _attn_probs(q_h, k_prev, k_cur, bias_h, sink, valid_prev, valid_cur):
    l_prev = jnp.where(valid_prev, _nt(q_h, k_prev) * SCALE + bias_h[:, :BLOCK], NEG_INF)
    l_cur = jnp.where(valid_cur, _nt(q_h, k_cur) * SCALE + bias_h[:, BLOCK:], NEG_INF)
    m = jnp.maximum(jnp.maximum(jnp.max(l_prev, axis=-1, keepdims=True), jnp.max(l_cur, axis=-1, keepdims=True)), sink)
    e_prev = jnp.exp(l_prev - m)
    e_cur = jnp.exp(l_cur - m)
    e_sink = jnp.exp(sink - m)
    denom = jnp.sum(e_prev, axis=-1, keepdims=True) + jnp.sum(e_cur, axis=-1, keepdims=True) + e_sink
    return e_prev / denom, e_cur / denom, e_sink / denom


def _mixer_specs(nb):
    cl = lambda i: jnp.minimum(i, nb - 1)
    return [
        pl.BlockSpec((BLOCK, 2 * D_GMLP), lambda i: (cl(i), 0)),
        pl.BlockSpec((BLOCK, D_ATTN), lambda i: (cl(i), 0)),
        pl.BlockSpec((BLOCK, 2 * D_KV), lambda i: (cl(i), D_ATTN // (2 * D_KV))),
        pl.BlockSpec((BLOCK, 2 * D_KV), lambda i: (jnp.maximum(cl(i) - 1, 0), D_ATTN // (2 * D_KV))),
        pl.BlockSpec((1, D_GMLP), lambda i: (0, 0)),
        pl.BlockSpec((GROUPS, BLOCK, BLOCK), lambda i: (0, 0, 0)),
        pl.BlockSpec((GROUPS, BLOCK, 1), lambda i: (0, 0, 0)),
        pl.BlockSpec(memory_space=pltpu.SMEM),
        pl.BlockSpec(memory_space=pltpu.SMEM),
        pl.BlockSpec((BLOCK, 2 * BLOCK), lambda i: (0, 0)),
    ]


def _mixer_fwd(zuv, qkv, gv, w_sp, b_sp, sinks, table, bucket, after=None):
    s = zuv.shape[0]
    nb = s // BLOCK

    def body(zuv_ref, q_ref, kvc_ref, kvp_ref, gv_ref, w_ref, b_ref, sink_ref, table_ref, bucket_ref,
             mix_ref, tanh_ref, prob_ref, psink_ref, bias_ref):
        i = pl.program_id(0)

        @pl.when(i == 0)
        def _():
            _build_bias(bias_ref, bucket_ref, table_ref)

        t = _gelu_tanh(zuv_ref[...])
        tanh_ref[...] = t
        u = _gelu(zuv_ref[:, :D_GMLP], t[:, :D_GMLP])
        vg = _gelu(zuv_ref[:, D_GMLP:], t[:, D_GMLP:])
        for g in range(GROUPS):
            sl = slice(g * GDIM, (g + 1) * GDIM)
            vg_g = vg[:, sl]
            vn = ((vg_g * _rms_scale(vg_g)) * gv_ref[:, sl]).astype(BF)
            sv = _nn(_tril_bf16(w_ref, g), vn) + b_ref[g]
            mix_ref[:, sl] = (u[:, sl] * sv).astype(BF)

        valid_prev, valid_cur = _window_masks(i)
        for kh in range(N_KV):
            ksl = slice(kh * HEAD_DIM, (kh + 1) * HEAD_DIM)
            vsl = slice(D_KV + kh * HEAD_DIM, D_KV + (kh + 1) * HEAD_DIM)
            q4 = _stack_heads(q_ref, kh, 0)
            p_prev, p_cur, p_sink = _attn_probs(
                q4, kvp_ref[:, ksl], kvc_ref[:, ksl], bias_ref[kh], _stack_sinks(sink_ref, kh), valid_prev, valid_cur)
            prob_ref[0, kh, :, :BLOCK] = p_prev
            prob_ref[0, kh, :, BLOCK:] = p_cur
            psink_ref[0, kh] = jnp.broadcast_to(p_sink, (ROWS4, LANES))
            o4 = _nn(p_prev.astype(BF), kvp_ref[:, vsl]) + _nn(p_cur.astype(BF), kvc_ref[:, vsl])
            for g in range(Q_PER_KV):
                first = D_GMLP + (kh * Q_PER_KV + g) * HEAD_DIM
                mix_ref[:, first:first + HEAD_DIM] = o4[g * BLOCK:(g + 1) * BLOCK].astype(BF)

    return _call(
        body,
        name="mixer_fwd",
        after=after,
        grid=(nb,),
        in_specs=_mixer_specs(nb),
        out_specs=[
            pl.BlockSpec((BLOCK, D_MODEL), lambda i: (i, 0)),
            pl.BlockSpec((BLOCK, 2 * D_GMLP), lambda i: (i, 0)),
            pl.BlockSpec((1, N_KV, ROWS4, 2 * BLOCK), lambda i: (i, 0, 0, 0)),
            pl.BlockSpec((1, N_KV, ROWS4, LANES), lambda i: (i, 0, 0, 0)),
        ],
        out_shape=[
            jax.ShapeDtypeStruct((s, D_MODEL), BF),
            jax.ShapeDtypeStruct((s, 2 * D_GMLP), F32),
            jax.ShapeDtypeStruct((nb, N_KV, ROWS4, 2 * BLOCK), F32),
            jax.ShapeDtypeStruct((nb, N_KV, ROWS4, LANES), F32),
        ],
        scratch_shapes=[pltpu.VMEM((N_KV, ROWS4, 2 * BLOCK), F32)],
        compiler_params=_params(("arbitrary",)),
    )(zuv, qkv, qkv, qkv, gv, w_sp, b_sp, sinks, table, bucket)


def _tail(x, mix, p, t, g2, gf, w_out, w_ff1, w_ff2, w_gate, w_proj):
    s = x.shape[0]
    tm = min(TOK_TILE, s)
    n_ff = w_ff1.shape[0]
    fc = D_FF // n_ff

    def body(x_ref, mix_ref, p_ref, t_ref, g2_ref, gf_ref, wo_ref, w1_ref, w2_ref, wg_ref, wp_ref,
             small_ref, dh1_ref, dh1b_ref, dmix_ref, hn2_ref, a_ref, df_ref, h2_ref, dh2_ref, dgl_ref, dpp_ref, f_ref):
        i = pl.program_id(0)

        @pl.when(i == 0)
        def _():
            small_ref[...] = jnp.zeros_like(small_ref)

        h1 = x_ref[...] + _nn(mix_ref[...], wo_ref[...])
        r2 = _rms_scale(h1)
        hn2 = ((h1 * r2) * g2_ref[...]).astype(BF)
        hn2_ref[...] = hn2
        h2 = h1
        for c in range(n_ff):
            f = _nn(hn2, w1_ref[c])
            f_ref[:, c * fc:(c + 1) * fc] = f
            a = jnp.square(jnp.maximum(f, 0.0)).astype(BF)
            a_ref[:, c * fc:(c + 1) * fc] = a
            h2 = h2 + _nn(a, w2_ref[c * fc:(c + 1) * fc, :])
        h2b = h2.astype(BF)
        h2_ref[...] = h2b
        gate = jax.nn.sigmoid(_nn(h2b, wg_ref[...]))
        pp = _nn(p_ref[...].astype(BF), wp_ref[...])
        h3 = h2 + gate * pp
        rf = _rms_scale(h3)
        gf_v = gf_ref[...]
        err = (h3 * rf) * gf_v - t_ref[...]
        small_ref[2:3, :] += jnp.sum(jnp.sum(err * err, axis=-1, keepdims=True), axis=0, keepdims=True) * (0.5 / D_MODEL)

        dy = err * (1.0 / D_MODEL)
        small_ref[1:2, :] += jnp.sum(dy * (h3 * rf), axis=0, keepdims=True)
        dh3 = _rms_bwd(dy * gf_v, h3, rf)
        dpp_ref[...] = (dh3 * gate).astype(BF)
        dgl = ((dh3 * pp) * (gate * (1.0 - gate))).astype(BF)
        dgl_ref[...] = dgl
        dh2 = dh3 + _nt(dgl, wg_ref[...])
        dh2b = dh2.astype(BF)
        dh2_ref[...] = dh2b
        dhn2 = jnp.zeros((tm, D_MODEL), F32)
        for c in range(n_ff):
            da = _nt(dh2b, w2_ref[c * fc:(c + 1) * fc, :])
            df = (da * (2.0 * jnp.maximum(f_ref[:, c * fc:(c + 1) * fc], 0.0))).astype(BF)
            df_ref[:, c * fc:(c + 1) * fc] = df
            dhn2 = dhn2 + _nt(df, w1_ref[c])
        small_ref[0:1, :] += jnp.sum(dhn2 * (h1 * r2), axis=0, keepdims=True)
        dh1 = dh2 + _rms_bwd(dhn2 * g2_ref[...], h1, r2)
        dh1_ref[...] = dh1
        dh1b = dh1.astype(BF)
        dh1b_ref[...] = dh1b
        dmix_ref[...] = _nt(dh1b, wo_ref[...])

    tile = lambda cols: pl.BlockSpec((tm, cols), lambda i: (i, 0))
    whole = lambda shape: pl.BlockSpec(shape, lambda i: (0,) * len(shape), pipeline_mode=pl.Buffered(1))
    row = pl.BlockSpec((1, D_MODEL), lambda i: (0, 0))
    act = lambda cols, dt: jax.ShapeDtypeStruct((s, cols), dt)
    return _call(
        body,
        name="tail",
        grid=(s // tm,),
        in_specs=[tile(D_MODEL), tile(D_MODEL), tile(PLE_DIM), tile(D_MODEL), row, row,
                  whole(w_out.shape), whole(w_ff1.shape), whole(w_ff2.shape), whole(w_gate.shape), whole(w_proj.shape)],
        out_specs=[pl.BlockSpec((8, D_MODEL), lambda i: (0, 0)), tile(D_MODEL), tile(D_MODEL), tile(D_MODEL), tile(D_MODEL), tile(D_FF),
                   tile(D_FF), tile(D_MODEL), tile(D_MODEL), tile(D_MODEL), tile(D_MODEL)],
        out_shape=[jax.ShapeDtypeStruct((8, D_MODEL), F32),
                   act(D_MODEL, F32), act(D_MODEL, BF), act(D_MODEL, F32), act(D_MODEL, BF), act(D_FF, BF), act(D_FF, BF), act(D_MODEL, BF),
                   act(D_MODEL, BF), act(D_MODEL, BF), act(D_MODEL, BF)],
        scratch_shapes=[pltpu.VMEM((tm, D_FF), F32)],
        compiler_params=_params(("arbitrary",)),
    )(x, mix, p, t, g2, gf, w_out, w_ff1, w_ff2, w_gate, w_proj)


def _mixer_bwd(zuv, qkv, dmix, saved, gv, w_sp, b_sp, bucket, after=None):
    s = zuv.shape[0]
    nb = s // BLOCK

    def body(zuv_ref, q_ref, kvc_ref, kvp_ref, gv_ref, w_ref, b_ref, bucket_ref, dmix_ref, tanh_ref, prob_ref, psink_ref,
             dzm_ref, dkv_ref, dgv_ref, dw_ref, db_ref, dattn_ref,
             dbias_ref, carry_ref, dsink_acc, db_acc):
        i = pl.program_id(0)

        @pl.when(i == 0)
        def _():
            dbias_ref[...] = jnp.zeros_like(dbias_ref)
            carry_ref[...] = jnp.zeros_like(carry_ref)
            dsink_acc[...] = jnp.zeros_like(dsink_acc)
            dgv_ref[...] = jnp.zeros_like(dgv_ref)
            dw_ref[...] = jnp.zeros_like(dw_ref)
            db_acc[...] = jnp.zeros_like(db_acc)

        @pl.when(i < nb)
        def _():
            u, du_dz = _gelu_and_grad(zuv_ref[:, :D_GMLP], tanh_ref[:, :D_GMLP])
            vg, dvg_dz = _gelu_and_grad(zuv_ref[:, D_GMLP:], tanh_ref[:, D_GMLP:])
            for g in range(GROUPS):
                sl = slice(g * GDIM, (g + 1) * GDIM)
                vg_g = vg[:, sl]
                rg = _rms_scale(vg_g)
                vhat = vg_g * rg
                gain = gv_ref[:, sl]
                vn = (vhat * gain).astype(BF)
                w_g = _tril_bf16(w_ref, g)
                sv = _nn(w_g, vn) + b_ref[g]
                dmix_a = dmix_ref[:, sl]
                dsv = dmix_a * u[:, sl]
                dsvb = dsv.astype(BF)
                db_acc[g] += jnp.sum(dsv, axis=-1, keepdims=True)
                dw_ref[g] += _nt(dsvb, vn)
                dvn = _tn(w_g, dsvb)
                dgv_ref[:, sl] += jnp.sum(dvn * vhat, axis=0, keepdims=True)
                dvg = _rms_bwd(dvn * gain, vg_g, rg)
                dzm_ref[:, sl] = ((dmix_a * sv) * du_dz[:, sl]).astype(BF)
                dzm_ref[:, D_GMLP + g * GDIM: D_GMLP + (g + 1) * GDIM] = (dvg * dvg_dz[:, sl]).astype(BF)

            for kh in range(N_KV):
                ksl = slice(kh * HEAD_DIM, (kh + 1) * HEAD_DIM)
                vsl = slice(D_KV + kh * HEAD_DIM, D_KV + (kh + 1) * HEAD_DIM)
                k_prev, k_cur = kvp_ref[:, ksl], kvc_ref[:, ksl]
                v_prev, v_cur = kvp_ref[:, vsl], kvc_ref[:, vsl]
                q4 = _stack_heads(q_ref, kh, 0)
                p_prev, p_cur, p_sink = prob_ref[0, kh, :, :BLOCK], prob_ref[0, kh, :, BLOCK:], psink_ref[0, kh, :, 0:1]
                do4 = _stack_heads(dmix_ref, kh, D_GMLP)
                dp_prev = _nt(do4, v_prev)
                dp_cur = _nt(do4, v_cur)
                delta = jnp.sum(p_prev * dp_prev, axis=-1, keepdims=True) + jnp.sum(p_cur * dp_cur, axis=-1, keepdims=True)
                ds_prev = p_prev * (dp_prev - delta)
                ds_cur = p_cur * (dp_cur - delta)
                dsink_acc[kh] -= p_sink * delta
                dbias_ref[kh, :, :BLOCK] += ds_prev
                dbias_ref[kh, :, BLOCK:] += ds_cur
                dsb_prev = ds_prev.astype(BF)
                dsb_cur = ds_cur.astype(BF)
                dq4 = (_nn(dsb_prev, k_prev) + _nn(dsb_cur, k_cur)) * SCALE
                for g in range(Q_PER_KV):
                    first = 2 * D_GMLP + (kh * Q_PER_KV + g) * HEAD_DIM
                    dzm_ref[:, first:first + HEAD_DIM] = dq4[g * BLOCK:(g + 1) * BLOCK].astype(BF)
                dkv_ref[:, ksl] = (carry_ref[:, ksl] + _tn(dsb_prev, q4) * SCALE).astype(BF)
                dkv_ref[:, vsl] = (carry_ref[:, vsl] + _tn(p_prev.astype(BF), do4)).astype(BF)
                carry_ref[:, ksl] = _tn(dsb_cur, q4) * SCALE
                carry_ref[:, vsl] = _tn(p_cur.astype(BF), do4)

        @pl.when(i == nb)
        def _():
            dkv_ref[...] = carry_ref[...].astype(BF)
            row = lax.broadcasted_iota(jnp.int32, (BLOCK, BLOCK), 0)
            col = lax.broadcasted_iota(jnp.int32, (BLOCK, BLOCK), 1)
            for g in range(GROUPS):
                dw_ref[g] = jnp.where(col <= row, dw_ref[g], 0.0)
                db_ref[g:g + 1, :] = jnp.sum(jnp.where(col == row, db_acc[g], 0.0), axis=0, keepdims=True)
            bucket = bucket_ref[...]
            for b in range(N_Q, REL_BUCKETS):
                dattn_ref[N_Q, b] = 0.0
            for h in range(N_Q):
                kh, rows = _head_rows(h)
                dattn_ref[N_Q, h] = jnp.sum(dsink_acc[kh, rows, :])
                dbh = dbias_ref[kh, rows, :]
                for b in range(REL_BUCKETS):
                    dattn_ref[h, b] = jnp.sum(jnp.where(bucket == b, dbh, 0.0))

    cl = lambda i: jnp.minimum(i, nb - 1)
    const = lambda shape: pl.BlockSpec(shape, lambda i: (0,) * len(shape))
    return _call(
        body,
        name="mixer_bwd",
        after=after,
        grid=(nb + 1,),
        in_specs=_mixer_specs(nb)[:7] + [
            const((BLOCK, 2 * BLOCK)),
            pl.BlockSpec((BLOCK, D_MODEL), lambda i: (cl(i), 0)),
            pl.BlockSpec((BLOCK, 2 * D_GMLP), lambda i: (cl(i), 0)),
            pl.BlockSpec((1, N_KV, ROWS4, 2 * BLOCK), lambda i: (cl(i), 0, 0, 0)),
            pl.BlockSpec((1, N_KV, ROWS4, LANES), lambda i: (cl(i), 0, 0, 0)),
        ],
        out_specs=[
            pl.BlockSpec((BLOCK, D_MAIN), lambda i: (cl(i), 0)),
            pl.BlockSpec((BLOCK, 2 * D_KV), lambda i: (jnp.maximum(i - 1, 0), 0)),
            const((1, D_GMLP)),
            const((GROUPS, BLOCK, BLOCK)),
            const((GROUPS, BLOCK)),
            pl.BlockSpec(memory_space=pltpu.SMEM),
        ],
        out_shape=[
            jax.ShapeDtypeStruct((s, D_MAIN), BF),
            jax.ShapeDtypeStruct((s, 2 * D_KV), BF),
            jax.ShapeDtypeStruct((1, D_GMLP), F32),
            jax.ShapeDtypeStruct((GROUPS, BLOCK, BLOCK), F32),
            jax.ShapeDtypeStruct((GROUPS, BLOCK), F32),
            jax.ShapeDtypeStruct((N_Q + 1, REL_BUCKETS), F32),
        ],
        scratch_shapes=[
            pltpu.VMEM((N_KV, ROWS4, 2 * BLOCK), F32),
            pltpu.VMEM((BLOCK, 2 * D_KV), F32),
            pltpu.VMEM((N_KV, ROWS4, 1), F32),
            pltpu.VMEM((GROUPS, BLOCK, 1), F32),
        ],
        compiler_params=_params(("arbitrary",)),
    )(zuv, qkv, qkv, qkv, gv, w_sp, b_sp, bucket, dmix, *saved)


def _in_bwd(x, dh1, dzm, dkv, g1, w_in_t, after=None):
    s = x.shape[0]
    tm = min(IO_TOK_TILE, s)

    def body(x_ref, dh1_ref, dzm_ref, dkv_ref, g_ref, w_ref, dx_ref, dg_ref):
        @pl.when(pl.program_id(0) == 0)
        def _():
            dg_ref[...] = jnp.zeros_like(dg_ref)

        dhn = _nn(dzm_ref[...], w_ref[:D_MAIN, :]) + _nn(dkv_ref[...], w_ref[D_MAIN:, :])
        xv = x_ref[...]
        r = _rms_scale(xv)
        dg_ref[...] += jnp.sum(dhn * (xv * r), axis=0, keepdims=True)
        dx_ref[...] = dh1_ref[...] + _rms_bwd(dhn * g_ref[...], xv, r)

    tile = lambda cols: pl.BlockSpec((tm, cols), lambda i: (i, 0))
    row = pl.BlockSpec((1, D_MODEL), lambda i: (0, 0))
    return _call(
        body,
        name="in_bwd",
        after=after,
        grid=(s // tm,),
        in_specs=[tile(D_MODEL), tile(D_MODEL), tile(D_MAIN), tile(2 * D_KV), row, pl.BlockSpec((D_IN, D_MODEL), lambda i: (0, 0))],
        out_specs=[tile(D_MODEL), row],
        out_shape=[jax.ShapeDtypeStruct((s, D_MODEL), F32), jax.ShapeDtypeStruct((1, D_MODEL), F32)],
        compiler_params=_params(("arbitrary",)),
    )(x, dh1, dzm, dkv, g1, w_in_t)


def _wgrad_in(dzm, dkv, hn, after=None):
    s = hn.shape[0]
    tm = 2 * D_KV
    n_main = D_MAIN // tm

    def body(dzm_ref, dkv_ref, hn_ref, o_ref):
        i = pl.program_id(0)

        @pl.when(i < n_main)
        def _():
            o_ref[...] = _tn(dzm_ref[...], hn_ref[...])

        @pl.when(i == n_main)
        def _():
            o_ref[...] = _tn(dkv_ref[...], hn_ref[...])

    return _call(
        body,
        name="wgrad_in",
        after=after,
        grid=(n_main + 1,),
        in_specs=[
            pl.BlockSpec((s, tm), lambda i: (0, jnp.minimum(i, n_main - 1))),
            pl.BlockSpec((s, tm), lambda i: (0, 0)),
            pl.BlockSpec((s, D_MODEL), lambda i: (0, 0)),
        ],
        out_specs=pl.BlockSpec((tm, D_MODEL), lambda i: (i, 0)),
        out_shape=jax.ShapeDtypeStruct((D_IN, D_MODEL), F32),
        compiler_params=_params(("arbitrary",)),
    )(dzm, dkv, hn)


def _wgrad_cols(a, b, tm, tn, name, peer_cols, after=None):
    s, m = a.shape
    n = b.shape[1]

    def body(a_ref, b_ref, o_ref, at_ref):
        @pl.when(pl.program_id(1) == 0)
        def _():
            at_ref[...] = a_ref[...].astype(BF).T

        r = _nn(at_ref[...], b_ref[...])
        for q in range(tn // peer_cols):
            o_ref[q] = r[:, q * peer_cols:(q + 1) * peer_cols]

    return _call(
        body,
        name=name,
        after=after,
        grid=(m // tm, n // tn),
        in_specs=[pl.BlockSpec((s, tm), lambda i, j: (0, i)), pl.BlockSpec((s, tn), lambda i, j: (0, j))],
        out_specs=pl.BlockSpec((tn // peer_cols, tm, peer_cols), lambda i, j: (j, i, 0)),
        out_shape=jax.ShapeDtypeStruct((n // peer_cols, m, peer_cols), F32),
        scratch_shapes=[pltpu.VMEM((tm, s), BF)],
        compiler_params=_params(("arbitrary", "arbitrary")),
    )(a, b)


def _wgrad_rows(a, b, tm, name, after=None):
    s, m = a.shape
    n = b.shape[1]
    steps = m // tm

    def body(a_ref, a_next_ref, b_ref, o_ref, at_ref):
        i = pl.program_id(0)

        @pl.when(i == 0)
        def _():
            at_ref[0] = a_ref[...].T

        o_ref[...] = _nn(at_ref[i % 2], b_ref[...])
        at_ref[(i + 1) % 2] = a_next_ref[...].T

    return _call(
        body,
        name=name,
        after=after,
        grid=(steps,),
        in_specs=[pl.BlockSpec((s, tm), lambda i: (0, i)),
                  pl.BlockSpec((s, tm), lambda i: (0, jnp.minimum(i + 1, steps - 1))),
                  pl.BlockSpec((s, n), lambda i: (0, 0))],
        out_specs=pl.BlockSpec((tm, n), lambda i: (i, 0)),
        out_shape=jax.ShapeDtypeStruct((m, n), F32),
        scratch_shapes=[pltpu.VMEM((2, tm, s), BF)],
        compiler_params=_params(("arbitrary",)),
    )(a, a, b)


def _adamw_math(w, g, m, v):
    m_new = ADAM_B1 * m + (1.0 - ADAM_B1) * g
    v_new = ADAM_B2 * v + (1.0 - ADAM_B2) * jnp.square(g)
    m_hat = m_new / (1.0 - ADAM_B1 ** ADAM_STEP)
    v_hat = v_new / (1.0 - ADAM_B2 ** ADAM_STEP)
    delta = -ADAM_LR * (m_hat / (jnp.sqrt(v_hat) + ADAM_EPS) + ADAM_WD * w)
    return delta, m_new, v_new


def _final_adamw(part, recv, w, m, v, name):
    r, c = w.shape
    tr = min(r, 512)

    def body(p_ref, r_ref, w_ref, m_ref, v_ref, g_ref, d_ref, mo_ref, vo_ref):
        g = p_ref[...]
        for j in range(3):
            g = g + r_ref[j].astype(F32)
        g_ref[...] = g
        d_ref[...], mo_ref[...], vo_ref[...] = _adamw_math(w_ref[...], g, m_ref[...], v_ref[...])

    spec = pl.BlockSpec((tr, c), lambda i: (i, 0))
    return _call(
        body,
        name=name,
        grid=(r // tr,),
        in_specs=[spec, pl.BlockSpec((3, tr, c), lambda i: (0, i, 0)), spec, spec, spec],
        out_specs=[spec] * 4,
        out_shape=[jax.ShapeDtypeStruct((r, c), F32)] * 4,
        compiler_params=_params(("arbitrary",)),
    )(part, recv, w, m, v)


def _rs_sum(g, land, blocks, name):
    _, r, c = g.shape
    tr = min(r, 256)

    def body(blk_ref, g0_ref, g1_ref, g2_ref, g3_ref, l_ref, part_ref, send_ref):
        part_ref[...] = g0_ref[0] + l_ref[0]
        for j, gj_ref in enumerate((g1_ref, g2_ref, g3_ref)):
            send_ref[j] = (gj_ref[0] + l_ref[j + 1]).astype(BF)

    def pick(j):
        return pl.BlockSpec((1, tr, c), lambda i, blk: (blk[j], i, 0))

    return _call(
        body,
        name=name,
        grid_spec=pltpu.PrefetchScalarGridSpec(
            num_scalar_prefetch=1,
            grid=(r // tr,),
            in_specs=[pick(0), pick(1), pick(2), pick(3), pl.BlockSpec((4, tr, c), lambda i, blk: (0, i, 0))],
            out_specs=[pl.BlockSpec((tr, c), lambda i, blk: (i, 0)), pl.BlockSpec((3, tr, c), lambda i, blk: (0, i, 0))],
        ),
        out_shape=[jax.ShapeDtypeStruct((r, c), F32), jax.ShapeDtypeStruct((3, r, c), BF)],
        compiler_params=_params(("arbitrary",)),
    )(blocks, g, g, g, g, land)


def _final_adamw_few(parts, recvs, ws, ms, vs, name):
    n = len(ws)

    def body(*refs):
        ins, outs = refs[:5 * n], refs[5 * n:]
        for k in range(n):
            p_ref, r_ref, w_ref, m_ref, v_ref = (ins[q * n + k] for q in range(5))
            g = p_ref[...]
            for j in range(3):
                g = g + r_ref[j].astype(F32)
            outs[k][...] = g
            outs[n + k][...], outs[2 * n + k][...], outs[3 * n + k][...] = _adamw_math(w_ref[...], g, m_ref[...], v_ref[...])

    outs = _call(
        body,
        name=name,
        out_shape=[jax.ShapeDtypeStruct(w.shape, F32) for w in ws] * 4,
        compiler_params=_params(),
    )(*parts, *recvs, *ws, *ms, *vs)
    return outs[:n], outs[n:2 * n], outs[2 * n:3 * n], outs[3 * n:]


def _rs_sum_few(gs, lands, blocks, name):
    n = len(gs)

    def body(blk_ref, *refs):
        g_refs, l_refs, part_refs, send_refs = refs[:n], refs[n:2 * n], refs[2 * n:3 * n], refs[3 * n:]
        for k in range(n):
            part_refs[k][...] = g_refs[k][blk_ref[0]] + l_refs[k][0]
            for j in range(3):
                send_refs[k][j] = (g_refs[k][blk_ref[j + 1]] + l_refs[k][j + 1]).astype(BF)

    outs = _call(
        body,
        name=name,
        in_specs=[pl.BlockSpec(memory_space=pltpu.SMEM)] + [pl.BlockSpec(memory_space=pltpu.VMEM)] * (2 * n),
        out_shape=[jax.ShapeDtypeStruct(g.shape[1:], F32) for g in gs] + [jax.ShapeDtypeStruct((3,) + g.shape[1:], BF) for g in gs],
        compiler_params=_params(),
    )(blocks, *gs, *lands)
    return [(outs[k], outs[n + k]) for k in range(n)]


def _adamw_small(own, recv, g1_own, g1_recv, weights, moms, vels, after):
    n_w = len(weights)

    def body(*refs):
        own_refs, recv_refs, g1_own_ref, g1_recv_ref = refs[:5], refs[5:10], refs[10], refs[11]
        w_refs, m_refs, v_refs = refs[12:12 + n_w], refs[12 + n_w:12 + 2 * n_w], refs[12 + 2 * n_w:12 + 3 * n_w]
        outs = refs[12 + 3 * n_w:]
        loss_ref, g_refs, d_refs = outs[0], outs[1:1 + n_w], outs[1 + n_w:1 + 2 * n_w]
        mo_refs, vo_refs = outs[1 + 2 * n_w:1 + 3 * n_w], outs[1 + 3 * n_w:]
        x, y, c = lax.axis_index("x"), lax.axis_index("y"), lax.axis_index("c")

        def in_place_order(values, my_place):
            acc = None
            for place in range(len(values)):
                r = place ^ my_place
                term = values[-1]
                for q in range(len(values) - 2, -1, -1):
                    term = jnp.where(r == q, values[q], term)
                acc = term if acc is None else acc + term
            return acc

        def total(k, *index):
            index = index or (slice(None),) * (len(own_refs[k].shape) - 1)
            across = [own_refs[k][(0, *index)], recv_refs[k][(1, *index)], recv_refs[k][(0, *index)], recv_refs[k][(2, *index)]]
            return in_place_order([v.astype(F32) for v in across], 2 * x + y)

        g1 = in_place_order([g1_own_ref[0]] + [g1_recv_ref[j] for j in range(N_DEV - 1)], 4 * x + 2 * y + c)
        grads = [
            g1, total(1), total(2), total(3),
            total(4, slice(N_Q, None), slice(0, N_Q)), total(4, slice(0, N_Q), slice(None)),
            total(0, slice(0, 1), slice(None)), total(0, slice(1, 2), slice(None))]
        loss_ref[...] = total(0, slice(2, 3), slice(0, 1))
        for k in range(n_w):
            g_refs[k][...] = grads[k]
            d_refs[k][...], mo_refs[k][...], vo_refs[k][...] = _adamw_math(w_refs[k][...], grads[k], m_refs[k][...], v_refs[k][...])

    shapes = [jax.ShapeDtypeStruct(w.shape, F32) for w in weights]
    outs = _call(
        body,
        name="adamw_small",
        after=after,
        in_specs=[pl.BlockSpec(memory_space=pltpu.VMEM)] * (12 + 3 * n_w),
        out_shape=[jax.ShapeDtypeStruct((1, 1), F32)] + shapes * 4,
        compiler_params=_params(),
    )(*own, *recv, g1_own, g1_recv, *weights, *moms, *vels)
    return outs[0], outs[1:1 + n_w], outs[1 + n_w:1 + 2 * n_w], outs[1 + 2 * n_w:1 + 3 * n_w], outs[1 + 3 * n_w:]


def _place():
    x, y, c = lax.axis_index("x"), lax.axis_index("y"), lax.axis_index("c")
    return x, y, c, [(1 - x, y), (x, 1 - y), (1 - x, 1 - y)]


def _dev_index(px, py, pc):
    return 4 * px + 2 * py + pc


HBM_SPEC = pl.BlockSpec(memory_space=pltpu.HBM)
SEM_SPEC = pl.BlockSpec(memory_space=pltpu.SEMAPHORE)
ANY_SPEC = pl.BlockSpec(memory_space=pl.ANY)
DATAFLOW = pltpu.SideEffectType.DATAFLOW_SIDE_EFFECTING


def _hbm(a):
    return pltpu.with_memory_space_constraint(a, pltpu.HBM)


def _prep_weights(shards):
    k_n = len(shards)

    def body(*refs):
        ins, outs, stage, sems = refs[:k_n], refs[k_n:2 * k_n], refs[2 * k_n:3 * k_n], refs[3 * k_n]
        x, y, c, _ = _place()
        copies = []
        for k in range(k_n):
            stage[k][...] = ins[k][...].astype(BF)
            copies.append(pltpu.make_async_copy(stage[k], outs[k].at[_dev_index(x, y, c)], sems.at[k]))
            copies[k].start()
        for cp in copies:
            cp.wait()

    return _call(
        body,
        name="prep_weights",
        in_specs=[pl.BlockSpec(memory_space=pltpu.VMEM)] * k_n,
        out_specs=[ANY_SPEC] * k_n,
        out_shape=[jax.ShapeDtypeStruct((N_DEV,) + sh.shape, BF) for sh in shards],
        scratch_shapes=[pltpu.VMEM(sh.shape, BF) for sh in shards] + [pltpu.SemaphoreType.DMA((k_n,))],
        compiler_params=_params(),
    )(*shards)


def _ag_piece(land_k, block, half, peer, send_sem, recv_sem):
    ref = land_k.at[_dev_index(*block)]
    if half is not None:
        rows = land_k.shape[1] // 2
        ref = ref.at[pl.ds(half * rows, rows)]
    return pltpu.make_async_remote_copy(
        src_ref=ref, dst_ref=ref, send_sem=send_sem, recv_sem=recv_sem, device_id=peer, device_id_type=MESH)


def _ag_plan():
    x, y, c, _ = _place()
    me, sib = (x, y, c), (x, y, 1 - c)
    xn, yn, diag = (1 - x, y, c), (x, 1 - y, c), (1 - x, 1 - y, c)
    return dict(
        relay_halves=[(me, 0, xn), (me, 1, yn)],
        others=[(me, None, sib), (me, 1, xn), (me, 0, yn)],
        relays=[(xn, 0, yn), (yn, 1, xn)],
        near=[(xn, None, sib), (yn, None, sib)],
        far=[(diag, None, sib)],
    )


def _ag_stage(land, stage, send_sems, recv_sems, act):
    copies = _ag_plan()[stage]
    n = len(copies)
    for k in range(len(land)):
        for j, (block, half, peer) in enumerate(copies):
            cp = _ag_piece(land[k], block, half, peer, send_sems.at[n * k + j], recv_sems.at[n * k + j])
            if act == "start":
                cp.start()
            else:
                cp.wait_send()
                cp.wait_recv()


def _sem_shapes(*counts):
    return [pltpu.SemaphoreType.DMA((n,)) for n in counts for _ in range(2)]


def _ag_start(first, rest):
    lands = list(first) + list(rest)
    k_n, k_first = len(lands), len(first)
    k_rest = k_n - k_first

    def body(*refs):
        land = refs[:k_n]
        sems = refs[k_n:k_n + 6]
        token = refs[-1]
        x, y, c, chips = _place()
        targets = [(x, y, 1 - c)] + [(*chip, c) for chip in chips]
        for k in range(k_first):
            for j, to in enumerate(targets):
                _ag_piece(land[k], (x, y, c), None, to, sems[0].at[4 * k + j], sems[1].at[4 * k + j]).start()
        _ag_stage(land[k_first:], "relay_halves", sems[2], sems[3], "start")
        _ag_stage(land[k_first:], "others", sems[4], sems[5], "start")
        token[...] = jnp.zeros_like(token)

    outs = pl.pallas_call(
        body,
        name="ag_start",
        in_specs=[HBM_SPEC] * k_n,
        out_specs=(*[SEM_SPEC] * 6, *[HBM_SPEC] * k_n, pl.BlockSpec(memory_space=pltpu.VMEM)),
        out_shape=(*_sem_shapes(4 * k_first, 2 * k_rest, 3 * k_rest),
                   *[pltpu.HBM(a.shape, a.dtype) for a in lands], jax.ShapeDtypeStruct((8, LANES), F32)),
        input_output_aliases={k: 6 + k for k in range(k_n)},
        compiler_params=pltpu.CompilerParams(has_side_effects=DATAFLOW),
    )(*[_hbm(a) for a in lands])
    flying = list(outs[6:6 + k_n])
    return (outs[0], outs[1], flying[:k_first]), (outs[2:6], flying[k_first:]), outs[-1]


def _ag_split_call(lands, waits, starts, after, name):
    k_n = len(lands)
    plan_sizes = dict(relay_halves=2, others=3, relays=2, near=2, far=1)
    n_in, n_out = 2 * len(waits), 2 * len(starts)

    def body(*refs):
        land = refs[:k_n]
        in_sems = refs[k_n:k_n + n_in]
        out_sems, token = refs[len(refs) - 1 - n_out:len(refs) - 1], refs[-1]
        for w, (stage, _, _) in enumerate(waits):
            _ag_stage(land, stage, in_sems[2 * w], in_sems[2 * w + 1], "wait")
            if w < len(starts):
                _ag_stage(land, starts[w], out_sems[2 * w], out_sems[2 * w + 1], "start")
        token[...] = jnp.zeros_like(token)

    outs = pl.pallas_call(
        body,
        name=name,
        in_specs=[HBM_SPEC] * k_n + [SEM_SPEC] * n_in + [ANY_SPEC],
        out_specs=(*[HBM_SPEC] * k_n, *[SEM_SPEC] * n_out, pl.BlockSpec(memory_space=pltpu.VMEM)),
        out_shape=(*[pltpu.HBM(a.shape, a.dtype) for a in lands], *_sem_shapes(*[plan_sizes[s] * k_n for s in starts]),
                   jax.ShapeDtypeStruct((8, LANES), F32)),
        input_output_aliases={k: k for k in range(k_n)},
        compiler_params=pltpu.CompilerParams(has_side_effects=DATAFLOW),
    )(*lands, *[s for _, a, b in waits for s in (a, b)], after)
    return list(outs[:k_n]), list(outs[k_n:k_n + n_out]), outs[-1]


def _ag_mid(lands, send_sems, recv_sems, after, name):
    k_n = len(lands)

    def body(*refs):
        land = refs[:k_n]
        send1, recv1 = refs[k_n], refs[k_n + 1]
        fwd_send, fwd_recv = refs[-2], refs[-1]
        x, y, c, chips = _place()
        sources = [(x, y, 1 - c)] + [(*chip, c) for chip in chips]
        for k in range(k_n):
            mine = land[k].at[_dev_index(x, y, c)]
            for j, frm in enumerate(sources):
                got = land[k].at[_dev_index(*frm)]
                cp = pltpu.make_async_remote_copy(
                    src_ref=mine, dst_ref=got, send_sem=send1.at[4 * k + j], recv_sem=recv1.at[4 * k + j], device_id=frm, device_id_type=MESH)
                cp.wait_send()
                cp.wait_recv()
                if j >= 1:
                    pltpu.make_async_remote_copy(
                        src_ref=got, dst_ref=got, send_sem=fwd_send.at[3 * k + j - 1], recv_sem=fwd_recv.at[3 * k + j - 1],
                        device_id=(x, y, 1 - c), device_id_type=MESH).start()

    outs = pl.pallas_call(
        body,
        name=name,
        in_specs=[HBM_SPEC] * k_n + [SEM_SPEC, SEM_SPEC, ANY_SPEC],
        out_specs=(*[HBM_SPEC] * k_n, SEM_SPEC, SEM_SPEC),
        out_shape=(*[pltpu.HBM(a.shape, a.dtype) for a in lands], pltpu.SemaphoreType.DMA((3 * k_n,)), pltpu.SemaphoreType.DMA((3 * k_n,))),
        input_output_aliases={k: k for k in range(k_n)},
        compiler_params=pltpu.CompilerParams(has_side_effects=DATAFLOW),
    )(*lands, send_sems, recv_sems, after)
    return list(outs[:k_n]), outs[-2], outs[-1]


def _ag_end(lands, fwd_send, fwd_recv, name):
    k_n = len(lands)

    def body(*refs):
        land = refs[:k_n]
        fsend, frecv = refs[k_n], refs[k_n + 1]
        x, y, c, chips = _place()
        for k in range(k_n):
            for j, chip in enumerate(chips):
                cp = pltpu.make_async_remote_copy(
                    src_ref=land[k].at[_dev_index(*chip, c)], dst_ref=land[k].at[_dev_index(*chip, 1 - c)],
                    send_sem=fsend.at[3 * k + j], recv_sem=frecv.at[3 * k + j], device_id=(x, y, 1 - c), device_id_type=MESH)
                cp.wait_send()
                cp.wait_recv()

    outs = pl.pallas_call(
        body,
        name=name,
        in_specs=[HBM_SPEC] * k_n + [SEM_SPEC, SEM_SPEC],
        out_specs=tuple([HBM_SPEC] * k_n),
        out_shape=tuple(pltpu.HBM(a.shape, a.dtype) for a in lands),
        input_output_aliases={k: k for k in range(k_n)},
        compiler_params=pltpu.CompilerParams(has_side_effects=DATAFLOW),
    )(*lands, fwd_send, fwd_recv)
    return list(outs)


def _chips4():
    x, y, c, others = _place()
    return x, y, c, [(x, y)] + others


def _route_sibling(j):
    x, y, c, chips = _chips4()
    return _dev_index(*chips[j], 1 - c), j, (x, y, 1 - c)


def _route_chips(j):
    x, y, c, chips = _chips4()
    return j, j, (*chips[j + 1], c)


def _route_sibling_whole(j):
    x, y, c, _ = _chips4()
    return 0, 0, (x, y, 1 - c)


def _route_chips_whole(j):
    x, y, c, chips = _chips4()
    return 0, j, (*chips[j + 1], c)


def _route_everyone(j):
    x, y, c, _ = _chips4()
    flip = [(j + 1) >> 2 & 1, (j + 1) >> 1 & 1, (j + 1) & 1]
    return 0, j, tuple(1 - v if f else v for v, f in zip((x, y, c), flip))


def _xchg_copies(routes, src, dst, send_sems, recv_sems):
    copies, sem = [], 0
    for k, (route, n) in enumerate(routes):
        for j in range(n):
            si, di, peer = route(j)
            copies.append(pltpu.make_async_remote_copy(
                src_ref=src[k].at[si], dst_ref=dst[k].at[di], send_sem=send_sems.at[sem], recv_sem=recv_sems.at[sem],
                device_id=peer, device_id_type=MESH))
            sem += 1
    return copies


def _xchg_start(srcs, slot_shapes, routes, name):
    k_n = len(srcs)
    n_sem = sum(n for _, n in routes)
    dsts = [lax.empty((n,) + tuple(sh), a.dtype) for sh, a, (_, n) in zip(slot_shapes, srcs, routes)]

    def body(*refs):
        src, dst = refs[:k_n], refs[k_n:2 * k_n]
        send_sems, recv_sems, token = refs[2 * k_n], refs[2 * k_n + 1], refs[-1]
        for cp in _xchg_copies(routes, src, dst, send_sems, recv_sems):
            cp.start()
        token[...] = jnp.zeros_like(token)

    arrays = list(srcs) + dsts
    outs = pl.pallas_call(
        body,
        name=name,
        in_specs=[HBM_SPEC] * (2 * k_n),
        out_specs=(SEM_SPEC, SEM_SPEC, *[HBM_SPEC] * (2 * k_n), pl.BlockSpec(memory_space=pltpu.VMEM)),
        out_shape=(pltpu.SemaphoreType.DMA((n_sem,)), pltpu.SemaphoreType.DMA((n_sem,)),
                   *[pltpu.HBM(a.shape, a.dtype) for a in arrays], jax.ShapeDtypeStruct((8, LANES), F32)),
        input_output_aliases={i: 2 + i for i in range(2 * k_n)},
        compiler_params=pltpu.CompilerParams(has_side_effects=DATAFLOW),
    )(*[_hbm(a) for a in arrays])
    return outs[0], outs[1], list(outs[2:2 + k_n]), list(outs[2 + k_n:2 + 2 * k_n]), outs[-1]


def _xchg_wait(send_sems, recv_sems, srcs, dsts, routes, after, name):
    k_n = len(srcs)

    def body(*refs):
        src, dst = refs[:k_n], refs[k_n:2 * k_n]
        for cp in _xchg_copies(routes, src, dst, refs[2 * k_n], refs[2 * k_n + 1]):
            cp.wait_send()
            cp.wait_recv()

    arrays = list(srcs) + list(dsts)
    outs = pl.pallas_call(
        body,
        name=name,
        in_specs=[HBM_SPEC] * (2 * k_n) + [SEM_SPEC, SEM_SPEC, ANY_SPEC],
        out_specs=tuple([HBM_SPEC] * (2 * k_n)),
        out_shape=tuple(pltpu.HBM(a.shape, a.dtype) for a in arrays),
        input_output_aliases={i: i for i in range(2 * k_n)},
        compiler_params=pltpu.CompilerParams(has_side_effects=DATAFLOW),
    )(*arrays, send_sems, recv_sems, after)
    return list(outs[:k_n]), list(outs[k_n:])


SMALL = ("norm1_gain", "gmlp_v_gain", "w_spatial", "b_spatial", "attn_sinks", "rel_bias_table", "norm2_gain", "final_gain")
LANES = 128


def _swap_start(grads, smalls, tag):
    srcs = list(grads) + [a[None] for a in smalls]
    shapes = [g.shape[1:] for g in grads] + [a.shape for a in smalls]
    routes = [(_route_sibling, 4)] * len(grads) + [(_route_sibling_whole, 1)] * len(smalls)
    return _xchg_start(srcs, shapes, routes, f"rs_{tag}_swap_start"), routes, len(grads)


def _swap_sums(swap, names, after, tag, few=False, small_dtypes=None):
    (send1, recv1, src1, land1, _), routes, n_rs = swap
    x, y, c, chips = _chips4()
    blocks = jnp.stack([_dev_index(*chip, c) for chip in chips]).astype(jnp.int32)
    src1, land1 = _xchg_wait(send1, recv1, src1, land1, routes, after, f"rs_{tag}_swap_wait")
    if few:
        sums = _rs_sum_few(src1[:n_rs], land1[:n_rs], blocks, f"rs_{tag}_sum")
    else:
        sums = [_rs_sum(g, land, blocks, f"rs_sum_{n}") for g, land, n in zip(src1[:n_rs], land1[:n_rs], names)]
    if len(src1) == n_rs:
        return sums, []
    return sums, _pair_sum(src1[n_rs:], land1[n_rs:], small_dtypes or [F32] * (len(src1) - n_rs), f"rs_{tag}_sum_small")


def _pair_sum(mine, theirs, dtypes, name):
    def body(*refs):
        n = len(refs) // 3
        for k in range(n):
            refs[2 * n + k][...] = (refs[k][...] + refs[n + k][...]).astype(dtypes[k])

    return _call(
        body,
        name=name,
        out_shape=[jax.ShapeDtypeStruct(a.shape, dt) for a, dt in zip(mine, dtypes)],
        compiler_params=_params(),
    )(*mine, *theirs)


def _chips_start(sums, small_sums, tag, everyone=()):
    sends = [ps[1] for ps in sums] + list(small_sums) + [a[None] for a in everyone]
    routes = [(_route_chips, 3)] * len(sums) + [(_route_chips_whole, 3)] * len(small_sums) + [(_route_everyone, 7)] * len(everyone)
    return _xchg_start(sends, [a.shape[1:] for a in sends], routes, f"rs_{tag}_chips_start"), routes


def _chips_wait(chips, after, tag):
    (send2, recv2, src2, land2, _), routes = chips
    return _xchg_wait(send2, recv2, src2, land2, routes, after, f"rs_{tag}_chips_wait")


def kernel(x, p, norm1_gain, w_in, gmlp_v_gain, w_spatial, b_spatial, attn_sinks, rel_bias_table, w_out, norm2_gain, w_ff1, w_ff2, w_ple_proj, w_ple_gate, final_gain, loss_target, m_norm1_gain, m_w_in, m_gmlp_v_gain, m_w_spatial, m_b_spatial, m_attn_sinks, m_rel_bias_table, m_w_out, m_norm2_gain, m_w_ff1, m_w_ff2, m_w_ple_proj, m_w_ple_gate, m_final_gain, v_norm1_gain, v_w_in, v_gmlp_v_gain, v_w_spatial, v_b_spatial, v_attn_sinks, v_rel_bias_table, v_w_out, v_norm2_gain, v_w_ff1, v_w_ff2, v_w_ple_proj, v_w_ple_gate, v_final_gain):
    args = dict(locals())
    s = x.shape[1]
    big = ("w_in", "w_out", "w_ff1", "w_ff2", "w_ple_proj", "w_ple_gate")

    x2, p2, t2 = x.reshape(s, D_MODEL), p.reshape(s, PLE_DIM), loss_target.reshape(s, D_MODEL)
    g1, gv, w_sp, b_sp, sinks, table, g2, gf = (args[n] for n in SMALL)
    bucket = jnp.asarray(_bucket_table())
    b_col = b_sp.reshape(GROUPS, BLOCK, 1)

    def shard(name):
        return args[name][0].T if name.endswith("w_in") else args[name][0]

    lands = _prep_weights([shard(n) for n in big])
    (send_in, recv_in, fly_in), (rest_sems, fly_rest), token = _ag_start(lands[:1], lands[1:])
    mid_in, fwd_send_in, fwd_recv_in = _ag_mid(fly_in, send_in, recv_in, token, "ag_mid_w_in")
    g_in = _ag_end(mid_in, fwd_send_in, fwd_recv_in, "ag_end_w_in")[0]
    full_in = g_in.reshape(D_IN, D_MODEL)

    zuv, qkv, hn1 = _in_proj(x2, g1, full_in)
    fly_rest, relay_sems, relayed = _ag_split_call(
        fly_rest, [("relay_halves", *rest_sems[:2])], ["relays"], zuv, "ag_relay")
    mix, *saved = _mixer_fwd(zuv, qkv, gv, w_sp[0], b_col, sinks, table.T, bucket, after=relayed)
    fly_rest, fwd_sems, _ = _ag_split_call(
        fly_rest, [("others", *rest_sems[2:]), ("relays", *relay_sems)], ["near", "far"], mix, "ag_mid_rest")
    g_out, g_ff1, g_ff2, g_proj, g_gate = _ag_split_call(
        fly_rest, [("near", *fwd_sems[:2]), ("far", *fwd_sems[2:])], [], mix, "ag_end_rest")[0]
    full_out, full_ff2, full_gate = g_out.reshape(D_MODEL, D_MODEL), g_ff2.reshape(D_FF, D_MODEL), g_gate.reshape(D_MODEL, D_MODEL)
    full_proj = g_proj.transpose(1, 0, 2).reshape(PLE_DIM, D_MODEL)

    (tail_small, dh1, dh1b, dmix, hn2, a, df, h2, dh2, dgl, dpp) = _tail(
        x2, mix, p2, t2, g2, gf.reshape(1, D_MODEL), full_out, g_ff1, full_ff2, full_gate, full_proj)

    ffn_names, mid_names = ("w_ff1", "w_ff2"), ("w_out", "w_ple_proj", "w_ple_gate")
    gw_ff1 = _wgrad_cols(hn2, df, D_MODEL, 2 * D_FF // N_DEV, "wgrad_ff1", D_FF // N_DEV)
    gw_ff2 = _wgrad_rows(a, dh2, 512, "wgrad_ff2").reshape(N_DEV, D_FF // N_DEV, D_MODEL)
    ffn_swap = _swap_start([gw_ff1, gw_ff2], [], "ffn")
    gw_gate = _wgrad_rows(h2, dgl, 256, "wgrad_gate", after=ffn_swap[0][4]).reshape(N_DEV, D_MODEL // N_DEV, D_MODEL)
    gw_proj = _wgrad_cols(p2, dpp, PLE_DIM, D_MODEL // 2, "wgrad_proj", D_MODEL // N_DEV, after=gw_gate)
    gw_out = _wgrad_rows(mix, dh1b, 256, "wgrad_out", after=gw_proj).reshape(N_DEV, D_MODEL // N_DEV, D_MODEL)
    mid_swap = _swap_start([gw_out, gw_proj, gw_gate], [tail_small], "mid")
    ffn_sums, _ = _swap_sums(ffn_swap, ffn_names, mid_swap[0][4], "ffn")
    ffn_chips = _chips_start(ffn_sums, [], "ffn")

    dzm, dkv, d_gv, d_wsp, d_bsp, d_attn = _mixer_bwd(
        zuv, qkv, dmix, saved, gv, w_sp[0], b_col, bucket, after=ffn_chips[0][4])
    mid_sums, mid_small = _swap_sums(mid_swap, mid_names, dzm, "mid", few=True)
    mid_chips = _chips_start(mid_sums, mid_small, "mid")
    gw_in = _wgrad_in(dzm, dkv, hn1, after=mid_chips[0][4]).reshape(N_DEV, D_IN // N_DEV, D_MODEL)
    in_swap = _swap_start([gw_in], [d_gv, d_wsp, d_bsp, d_attn], "in")
    dx, d_g1 = _in_bwd(x2, dh1, dzm, dkv, g1, full_in, after=in_swap[0][4])
    grad_x = dx.reshape(x.shape)
    in_sums, in_small = _swap_sums(in_swap, ("w_in",), dx, "in", small_dtypes=[F32, BF, F32, F32])
    in_chips = _chips_start(in_sums, in_small, "in", everyone=[d_g1])

    grads, deltas, new_m, new_v = {}, {}, {}, {}
    after, small_own, small_recv = in_chips[0][4], [], []
    for chips, sums, names, tag in ((ffn_chips, ffn_sums, ffn_names, "ffn"), (mid_chips, mid_sums, mid_names, "mid"),
                                    (in_chips, in_sums, ("w_in",), "in")):
        sent, recvs = _chips_wait(chips, after, tag)
        small_own += sent[len(names):]
        small_recv += recvs[len(names):]
        if tag == "mid":
            done = _final_adamw_few([ps[0] for ps in sums], recvs[:len(names)], [shard(n) for n in names],
                                    [shard("m_" + n) for n in names], [shard("v_" + n) for n in names], "adamw_mid")
            done = list(zip(*done))
        else:
            done = [_final_adamw(part, recv, shard(n), shard("m_" + n), shard("v_" + n), "adamw_" + n)
                    for n, (part, _), recv in zip(names, sums, recvs)]
        for n, results in zip(names, done):
            for dst, arr in zip((grads, deltas, new_m, new_v), results):
                dst[n] = (arr.T if n == "w_in" else arr)[None]
            after = results[1]

    views = {"w_spatial": (GROUPS, BLOCK, BLOCK), "b_spatial": (GROUPS, BLOCK), "final_gain": (1, D_MODEL)}
    def view(name, base):
        return args[name].T if base == "rel_bias_table" else args[name].reshape(views.get(base, args[name].shape))

    small_in = [[view(pre + n, n) for n in SMALL] for pre in ("", "m_", "v_")]
    loss, *small_out = _adamw_small(small_own[:-1], small_recv[:-1], small_own[-1], small_recv[-1], *small_in, after=after)
    for dst, arrays in zip((grads, deltas, new_m, new_v), small_out):
        for n, arr in zip(SMALL, arrays):
            dst[n] = arr.T if n == "rel_bias_table" else arr.reshape(args[n].shape)
    loss = loss[0, 0]

    order = ("norm1_gain", "w_in", "gmlp_v_gain", "w_spatial", "b_spatial", "attn_sinks", "rel_bias_table", "w_out",
             "norm2_gain", "w_ff1", "w_ff2", "w_ple_proj", "w_ple_gate", "final_gain")
    return (loss, grad_x, *[grads[n] for n in order], *[deltas[n] for n in order],
            *[new_m[n] for n in order], *[new_v[n] for n in order])
```

```python
import math

import numpy as np
import jax
import jax.numpy as jnp
from jax import lax
from jax.experimental import pallas as pl
from jax.experimental.pallas import tpu as pltpu

F32 = jnp.float32
BF = jnp.bfloat16
MESH = pl.DeviceIdType.MESH
N_DEV = 8

D_MODEL = 1024
PLE_DIM = 256
D_GMLP = 512
GROUPS = 4
GDIM = 128
BLOCK = 128
D_ATTN = 512
HEAD_DIM = 64
N_Q = 8
Q_PER_KV = 4
N_KV = N_Q // Q_PER_KV
ROWS4 = Q_PER_KV * BLOCK
D_KV = 128
D_FF = 4096
D_IN = 1792
D_MAIN = 2 * D_GMLP + D_ATTN
REL_BUCKETS = 32
EPS = 1e-6
NEG_INF = -1e30
SCALE = HEAD_DIM ** -0.5
GELU_C = math.sqrt(2.0 / math.pi)
GELU_A = 0.044715

ADAM_LR = 0.001
ADAM_B1 = 0.9
ADAM_B2 = 0.999
ADAM_EPS = 1e-08
ADAM_WD = 0.01
ADAM_STEP = 10

V7X_VMEM_LIMIT = 60000 * 1024
TOK_TILE = 256
IO_TOK_TILE = 512


def _call(body, after=None, **kw):
    if after is None:
        return pl.pallas_call(body, **kw)
    n_in = len(kw["in_specs"])

    def ordered(*refs):
        body(*refs[:n_in], *refs[n_in + 1:])

    kw["in_specs"] = list(kw["in_specs"]) + [pl.BlockSpec(memory_space=pl.ANY)]
    fn = pl.pallas_call(ordered, **kw)
    return lambda *operands: fn(*operands, after)


def _params(sem=None):
    if sem is None:
        return pltpu.CompilerParams(vmem_limit_bytes=V7X_VMEM_LIMIT)
    return pltpu.CompilerParams(dimension_semantics=sem, vmem_limit_bytes=V7X_VMEM_LIMIT)


def _nn(a, b):
    return jnp.dot(a, b, preferred_element_type=F32)


def _nt(a, b):
    return lax.dot_general(a, b, (((1,), (1,)), ((), ())), preferred_element_type=F32)


def _tn(a, b):
    return lax.dot_general(a, b, (((0,), (0,)), ((), ())), preferred_element_type=F32)


def _gelu_tanh(x):
    return jnp.tanh(GELU_C * (x + GELU_A * (x * x * x)))


def _gelu(x, t):
    return x * (0.5 * (1.0 + t))


def _gelu_and_grad(x, t):
    cdf = 0.5 * (1.0 + t)
    return x * cdf, cdf + 0.5 * x * (1.0 - t * t) * (GELU_C * (1.0 + 3.0 * GELU_A * (x * x)))


def _rms_scale(x):
    return lax.rsqrt(jnp.mean(x * x, axis=-1, keepdims=True) + EPS)


def _rms_bwd(dxn, x, r):
    return r * dxn - x * ((r * r * r) * jnp.mean(dxn * x, axis=-1, keepdims=True))


def _bucket_table():
    a = np.arange(BLOCK)[:, None]
    j = np.arange(2 * BLOCK)[None, :]
    n = BLOCK + a - j
    valid = (n >= 0) & (n < BLOCK)
    nc = np.maximum(n, 0)
    max_exact = REL_BUCKETS // 2
    nf = np.maximum(nc, 1).astype(np.float32)
    large = max_exact + (
        np.log(nf / np.float32(max_exact)) / np.float32(math.log(BLOCK / max_exact)) * np.float32(REL_BUCKETS - max_exact)
    ).astype(np.int32)
    large = np.minimum(large, REL_BUCKETS - 1)
    bucket = np.where(nc < max_exact, nc, large)
    return np.where(valid, bucket, -1).astype(np.int32)


def _in_proj(x, g1, w_in_t):
    s = x.shape[0]
    tm = min(IO_TOK_TILE, s)

    def body(x_ref, g_ref, w_ref, zuv_ref, qkv_ref, hn_ref):
        xv = x_ref[...]
        hn = ((xv * _rms_scale(xv)) * g_ref[...]).astype(BF)
        hn_ref[...] = hn
        z = _nt(hn, w_ref[...])
        zuv_ref[...] = z[:, : 2 * D_GMLP]
        qkv_ref[...] = z[:, 2 * D_GMLP:].astype(BF)

    return _call(
        body,
        name="in_proj",
        grid=(s // tm,),
        in_specs=[
            pl.BlockSpec((tm, D_MODEL), lambda i: (i, 0)),
            pl.BlockSpec((1, D_MODEL), lambda i: (0, 0)),
            pl.BlockSpec((D_IN, D_MODEL), lambda i: (0, 0)),
        ],
        out_specs=[
            pl.BlockSpec((tm, 2 * D_GMLP), lambda i: (i, 0)),
            pl.BlockSpec((tm, D_ATTN + 2 * D_KV), lambda i: (i, 0)),
            pl.BlockSpec((tm, D_MODEL), lambda i: (i, 0)),
        ],
        out_shape=[
            jax.ShapeDtypeStruct((s, 2 * D_GMLP), F32),
            jax.ShapeDtypeStruct((s, D_ATTN + 2 * D_KV), BF),
            jax.ShapeDtypeStruct((s, D_MODEL), BF),
        ],
        compiler_params=_params(("arbitrary",)),
    )(x, g1, w_in_t)


def _head_rows(h):
    kh, g = divmod(h, Q_PER_KV)
    return kh, slice(g * BLOCK, (g + 1) * BLOCK)


def _build_bias(bias_ref, bucket_ref, table_ref):
    bucket = bucket_ref[...]
    for h in range(N_Q):
        acc = jnp.zeros((BLOCK, 2 * BLOCK), F32)
        for b in range(REL_BUCKETS):
            acc = jnp.where(bucket == b, table_ref[h, b], acc)
        kh, rows = _head_rows(h)
        bias_ref[kh, rows, :] = acc


def _window_masks(i):
    row = lax.broadcasted_iota(jnp.int32, (ROWS4, BLOCK), 0) & (BLOCK - 1)
    col = lax.broadcasted_iota(jnp.int32, (ROWS4, BLOCK), 1)
    return (col > row) & (i > 0), col <= row


def _stack_heads(ref, kh, offset):
    first = offset + kh * Q_PER_KV * HEAD_DIM
    return jnp.concatenate(
        [ref[:, first + g * HEAD_DIM: first + (g + 1) * HEAD_DIM].astype(BF) for g in range(Q_PER_KV)], axis=0)


def _stack_sinks(sink_ref, kh):
    return jnp.concatenate([jnp.full((BLOCK, 1), sink_ref[0, kh * Q_PER_KV + g], F32) for g in range(Q_PER_KV)], axis=0)


def _tril_bf16(w_ref, g):
    row = lax.broadcasted_iota(jnp.int32, (BLOCK, BLOCK), 0)
    col = lax.broadcasted_iota(jnp.int32, (BLOCK, BLOCK), 1)
    return jnp.where(col <= row, w_ref[g], 0.0).astype(BF)


def _attn_probs(q_h, k_prev, k_cur, bias_h, sink, valid_prev, valid_cur):
    l_prev = jnp.where(valid_prev, _nt(q_h, k_prev) * SCALE + bias_h[:, :BLOCK], NEG_INF)
    l_cur = jnp.where(valid_cur, _nt(q_h, k_cur) * SCALE + bias_h[:, BLOCK:], NEG_INF)
    m = jnp.maximum(jnp.maximum(jnp.max(l_prev, axis=-1, keepdims=True), jnp.max(l_cur, axis=-1, keepdims=True)), sink)
    e_prev = jnp.exp(l_prev - m)
    e_cur = jnp.exp(l_cur - m)
    e_sink = jnp.exp(sink - m)
    denom = jnp.sum(e_prev, axis=-1, keepdims=True) + jnp.sum(e_cur, axis=-1, keepdims=True) + e_sink
    return e_prev / denom, e_cur / denom, e_sink / denom


def _mixer_specs(nb):
    cl = lambda i: jnp.minimum(i, nb - 1)
    return [
        pl.BlockSpec((BLOCK, 2 * D_GMLP), lambda i: (cl(i), 0)),
        pl.BlockSpec((BLOCK, D_ATTN), lambda i: (cl(i), 0)),
        pl.BlockSpec((BLOCK, 2 * D_KV), lambda i: (cl(i), D_ATTN // (2 * D_KV))),
        pl.BlockSpec((BLOCK, 2 * D_KV), lambda i: (jnp.maximum(cl(i) - 1, 0), D_ATTN // (2 * D_KV))),
        pl.BlockSpec((1, D_GMLP), lambda i: (0, 0)),
        pl.BlockSpec((GROUPS, BLOCK, BLOCK), lambda i: (0, 0, 0)),
        pl.BlockSpec((GROUPS, BLOCK, 1), lambda i: (0, 0, 0)),
        pl.BlockSpec(memory_space=pltpu.SMEM),
        pl.BlockSpec(memory_space=pltpu.SMEM),
        pl.BlockSpec((BLOCK, 2 * BLOCK), lambda i: (0, 0)),
    ]


def _mixer_fwd(zuv, qkv, gv, w_sp, b_sp, sinks, table, bucket, after=None):
    s = zuv.shape[0]
    nb = s // BLOCK

    def body(zuv_ref, q_ref, kvc_ref, kvp_ref, gv_ref, w_ref, b_ref, sink_ref, table_ref, bucket_ref,
             mix_ref, tanh_ref, prob_ref, psink_ref, bias_ref):
        i = pl.program_id(0)

        @pl.when(i == 0)
        def _():
            _build_bias(bias_ref, bucket_ref, table_ref)

        t = _gelu_tanh(zuv_ref[...])
        tanh_ref[...] = t
        u = _gelu(zuv_ref[:, :D_GMLP], t[:, :D_GMLP])
        vg = _gelu(zuv_ref[:, D_GMLP:], t[:, D_GMLP:])
        for g in range(GROUPS):
            sl = slice(g * GDIM, (g + 1) * GDIM)
            vg_g = vg[:, sl]
            vn = ((vg_g * _rms_scale(vg_g)) * gv_ref[:, sl]).astype(BF)
            sv = _nn(_tril_bf16(w_ref, g), vn) + b_ref[g]
            mix_ref[:, sl] = (u[:, sl] * sv).astype(BF)

        valid_prev, valid_cur = _window_masks(i)
        for kh in range(N_KV):
            ksl = slice(kh * HEAD_DIM, (kh + 1) * HEAD_DIM)
            vsl = slice(D_KV + kh * HEAD_DIM, D_KV + (kh + 1) * HEAD_DIM)
            q4 = _stack_heads(q_ref, kh, 0)
            p_prev, p_cur, p_sink = _attn_probs(
                q4, kvp_ref[:, ksl], kvc_ref[:, ksl], bias_ref[kh], _stack_sinks(sink_ref, kh), valid_prev, valid_cur)
            prob_ref[0, kh, :, :BLOCK] = p_prev
            prob_ref[0, kh, :, BLOCK:] = p_cur
            psink_ref[0, kh] = jnp.broadcast_to(p_sink, (ROWS4, LANES))
            o4 = _nn(p_prev.astype(BF), kvp_ref[:, vsl]) + _nn(p_cur.astype(BF), kvc_ref[:, vsl])
            for g in range(Q_PER_KV):
                first = D_GMLP + (kh * Q_PER_KV + g) * HEAD_DIM
                mix_ref[:, first:first + HEAD_DIM] = o4[g * BLOCK:(g + 1) * BLOCK].astype(BF)

    return _call(
        body,
        name="mixer_fwd",
        after=after,
        grid=(nb,),
        in_specs=_mixer_specs(nb),
        out_specs=[
            pl.BlockSpec((BLOCK, D_MODEL), lambda i: (i, 0)),
            pl.BlockSpec((BLOCK, 2 * D_GMLP), lambda i: (i, 0)),
            pl.BlockSpec((1, N_KV, ROWS4, 2 * BLOCK), lambda i: (i, 0, 0, 0)),
            pl.BlockSpec((1, N_KV, ROWS4, LANES), lambda i: (i, 0, 0, 0)),
        ],
        out_shape=[
            jax.ShapeDtypeStruct((s, D_MODEL), BF),
            jax.ShapeDtypeStruct((s, 2 * D_GMLP), F32),
            jax.ShapeDtypeStruct((nb, N_KV, ROWS4, 2 * BLOCK), F32),
            jax.ShapeDtypeStruct((nb, N_KV, ROWS4, LANES), F32),
        ],
        scratch_shapes=[pltpu.VMEM((N_KV, ROWS4, 2 * BLOCK), F32)],
        compiler_params=_params(("arbitrary",)),
    )(zuv, qkv, qkv, qkv, gv, w_sp, b_sp, sinks, table, bucket)


def _tail(x, mix, p, t, g2, gf, w_out, w_ff1, w_ff2, w_gate, w_proj):
    s = x.shape[0]
    tm = min(TOK_TILE, s)
    n_ff = w_ff1.shape[0]
    fc = D_FF // n_ff
    pc = D_MODEL // N_DEV

    def body(x_ref, mix_ref, p_ref, t_ref, g2_ref, gf_ref, wo_ref, w1_ref, w2_ref, wg_ref, wp_ref,
             small_ref, dh1_ref, dh1b_ref, dmix_ref, hn2_ref, a_ref, df_ref, dh2_ref, gwp_ref, gwg_ref, f_ref):
        i = pl.program_id(0)

        @pl.when(i == 0)
        def _():
            small_ref[...] = jnp.zeros_like(small_ref)
            gwp_ref[...] = jnp.zeros_like(gwp_ref)
            gwg_ref[...] = jnp.zeros_like(gwg_ref)

        h1 = x_ref[...] + _nn(mix_ref[...], wo_ref[...])
        r2 = _rms_scale(h1)
        hn2 = ((h1 * r2) * g2_ref[...]).astype(BF)
        hn2_ref[...] = hn2
        h2 = h1
        for c in range(n_ff):
            f = _nn(hn2, w1_ref[c])
            f_ref[:, c * fc:(c + 1) * fc] = f
            a = jnp.square(jnp.maximum(f, 0.0)).astype(BF)
            a_ref[:, c * fc:(c + 1) * fc] = a
            h2 = h2 + _nn(a, w2_ref[c * fc:(c + 1) * fc, :])
        h2b = h2.astype(BF)
        gate = jax.nn.sigmoid(_nn(h2b, wg_ref[...]))
        pb = p_ref[...].astype(BF)
        pp = _nn(pb, wp_ref[...])
        h3 = h2 + gate * pp
        rf = _rms_scale(h3)
        gf_v = gf_ref[...]
        err = (h3 * rf) * gf_v - t_ref[...]
        small_ref[2:3, :] += jnp.sum(jnp.sum(err * err, axis=-1, keepdims=True), axis=0, keepdims=True) * (0.5 / D_MODEL)

        dy = err * (1.0 / D_MODEL)
        small_ref[1:2, :] += jnp.sum(dy * (h3 * rf), axis=0, keepdims=True)
        dh3 = _rms_bwd(dy * gf_v, h3, rf)
        gw_proj = _tn(pb, (dh3 * gate).astype(BF))
        for q in range(N_DEV):
            gwp_ref[q] += gw_proj[:, q * pc:(q + 1) * pc]
        dgl = ((dh3 * pp) * (gate * (1.0 - gate))).astype(BF)
        gwg_ref[...] += _tn(h2b, dgl)
        dh2 = dh3 + _nt(dgl, wg_ref[...])
        dh2b = dh2.astype(BF)
        dh2_ref[...] = dh2b
        dhn2 = jnp.zeros((tm, D_MODEL), F32)
        for c in range(n_ff):
            da = _nt(dh2b, w2_ref[c * fc:(c + 1) * fc, :])
            df = (da * (2.0 * jnp.maximum(f_ref[:, c * fc:(c + 1) * fc], 0.0))).astype(BF)
            df_ref[:, c * fc:(c + 1) * fc] = df
            dhn2 = dhn2 + _nt(df, w1_ref[c])
        small_ref[0:1, :] += jnp.sum(dhn2 * (h1 * r2), axis=0, keepdims=True)
        dh1 = dh2 + _rms_bwd(dhn2 * g2_ref[...], h1, r2)
        dh1_ref[...] = dh1
        dh1b = dh1.astype(BF)
        dh1b_ref[...] = dh1b
        dmix_ref[...] = _nt(dh1b, wo_ref[...])

    tile = lambda cols: pl.BlockSpec((tm, cols), lambda i: (i, 0))
    whole = lambda shape: pl.BlockSpec(shape, lambda i: (0,) * len(shape), pipeline_mode=pl.Buffered(1))
    total = lambda shape: pl.BlockSpec(shape, lambda i: (0,) * len(shape))
    row = pl.BlockSpec((1, D_MODEL), lambda i: (0, 0))
    act = lambda cols, dt: jax.ShapeDtypeStruct((s, cols), dt)
    gw_shapes = [(N_DEV, PLE_DIM, pc), (D_MODEL, D_MODEL)]
    return _call(
        body,
        name="tail",
        grid=(s // tm,),
        in_specs=[tile(D_MODEL), tile(D_MODEL), tile(PLE_DIM), tile(D_MODEL), row, row,
                  whole(w_out.shape), whole(w_ff1.shape), whole(w_ff2.shape), whole(w_gate.shape), whole(w_proj.shape)],
        out_specs=[total((8, D_MODEL)), tile(D_MODEL), tile(D_MODEL), tile(D_MODEL), tile(D_MODEL), tile(D_FF), tile(D_FF),
                   tile(D_MODEL), *[total(shape) for shape in gw_shapes]],
        out_shape=[jax.ShapeDtypeStruct((8, D_MODEL), F32),
                   act(D_MODEL, F32), act(D_MODEL, BF), act(D_MODEL, F32), act(D_MODEL, BF), act(D_FF, BF), act(D_FF, BF),
                   act(D_MODEL, BF), *[jax.ShapeDtypeStruct(shape, F32) for shape in gw_shapes]],
        scratch_shapes=[pltpu.VMEM((tm, D_FF), F32)],
        compiler_params=_params(("arbitrary",)),
    )(x, mix, p, t, g2, gf, w_out, w_ff1, w_ff2, w_gate, w_proj)


def _mixer_bwd(zuv, qkv, dmix, saved, gv, w_sp, b_sp, bucket, after=None):
    s = zuv.shape[0]
    nb = s // BLOCK

    def body(zuv_ref, q_ref, kvc_ref, kvp_ref, gv_ref, w_ref, b_ref, bucket_ref, dmix_ref, tanh_ref, prob_ref, psink_ref,
             dzm_ref, dkv_ref, dgv_ref, dw_ref, db_ref, dattn_ref,
             dbias_ref, carry_ref, dsink_acc, db_acc):
        i = pl.program_id(0)

        @pl.when(i == 0)
        def _():
            dbias_ref[...] = jnp.zeros_like(dbias_ref)
            carry_ref[...] = jnp.zeros_like(carry_ref)
            dsink_acc[...] = jnp.zeros_like(dsink_acc)
            dgv_ref[...] = jnp.zeros_like(dgv_ref)
            dw_ref[...] = jnp.zeros_like(dw_ref)
            db_acc[...] = jnp.zeros_like(db_acc)

        @pl.when(i < nb)
        def _():
            u, du_dz = _gelu_and_grad(zuv_ref[:, :D_GMLP], tanh_ref[:, :D_GMLP])
            vg, dvg_dz = _gelu_and_grad(zuv_ref[:, D_GMLP:], tanh_ref[:, D_GMLP:])
            for g in range(GROUPS):
                sl = slice(g * GDIM, (g + 1) * GDIM)
                vg_g = vg[:, sl]
                rg = _rms_scale(vg_g)
                vhat = vg_g * rg
                gain = gv_ref[:, sl]
                vn = (vhat * gain).astype(BF)
                w_g = _tril_bf16(w_ref, g)
                sv = _nn(w_g, vn) + b_ref[g]
                dmix_a = dmix_ref[:, sl]
                dsv = dmix_a * u[:, sl]
                dsvb = dsv.astype(BF)
                db_acc[g] += jnp.sum(dsv, axis=-1, keepdims=True)
                dw_ref[g] += _nt(dsvb, vn)
                dvn = _tn(w_g, dsvb)
                dgv_ref[:, sl] += jnp.sum(dvn * vhat, axis=0, keepdims=True)
                dvg = _rms_bwd(dvn * gain, vg_g, rg)
                dzm_ref[:, sl] = ((dmix_a * sv) * du_dz[:, sl]).astype(BF)
                dzm_ref[:, D_GMLP + g * GDIM: D_GMLP + (g + 1) * GDIM] = (dvg * dvg_dz[:, sl]).astype(BF)

            for kh in range(N_KV):
                ksl = slice(kh * HEAD_DIM, (kh + 1) * HEAD_DIM)
                vsl = slice(D_KV + kh * HEAD_DIM, D_KV + (kh + 1) * HEAD_DIM)
                k_prev, k_cur = kvp_ref[:, ksl], kvc_ref[:, ksl]
                v_prev, v_cur = kvp_ref[:, vsl], kvc_ref[:, vsl]
                q4 = _stack_heads(q_ref, kh, 0)
                p_prev, p_cur, p_sink = prob_ref[0, kh, :, :BLOCK], prob_ref[0, kh, :, BLOCK:], psink_ref[0, kh, :, 0:1]
                do4 = _stack_heads(dmix_ref, kh, D_GMLP)
                dp_prev = _nt(do4, v_prev)
                dp_cur = _nt(do4, v_cur)
                delta = jnp.sum(p_prev * dp_prev, axis=-1, keepdims=True) + jnp.sum(p_cur * dp_cur, axis=-1, keepdims=True)
                ds_prev = p_prev * (dp_prev - delta)
                ds_cur = p_cur * (dp_cur - delta)
                dsink_acc[kh] -= p_sink * delta
                dbias_ref[kh, :, :BLOCK] += ds_prev
                dbias_ref[kh, :, BLOCK:] += ds_cur
                dsb_prev = ds_prev.astype(BF)
                dsb_cur = ds_cur.astype(BF)
                dq4 = (_nn(dsb_prev, k_prev) + _nn(dsb_cur, k_cur)) * SCALE
                for g in range(Q_PER_KV):
                    first = 2 * D_GMLP + (kh * Q_PER_KV + g) * HEAD_DIM
                    dzm_ref[:, first:first + HEAD_DIM] = dq4[g * BLOCK:(g + 1) * BLOCK].astype(BF)
                dkv_ref[:, ksl] = (carry_ref[:, ksl] + _tn(dsb_prev, q4) * SCALE).astype(BF)
                dkv_ref[:, vsl] = (carry_ref[:, vsl] + _tn(p_prev.astype(BF), do4)).astype(BF)
                carry_ref[:, ksl] = _tn(dsb_cur, q4) * SCALE
                carry_ref[:, vsl] = _tn(p_cur.astype(BF), do4)

        @pl.when(i == nb)
        def _():
            dkv_ref[...] = carry_ref[...].astype(BF)
            row = lax.broadcasted_iota(jnp.int32, (BLOCK, BLOCK), 0)
            col = lax.broadcasted_iota(jnp.int32, (BLOCK, BLOCK), 1)
            for g in range(GROUPS):
                dw_ref[g] = jnp.where(col <= row, dw_ref[g], 0.0)
                db_ref[g:g + 1, :] = jnp.sum(jnp.where(col == row, db_acc[g], 0.0), axis=0, keepdims=True)
            bucket = bucket_ref[...]
            for b in range(N_Q, REL_BUCKETS):
                dattn_ref[N_Q, b] = 0.0
            for h in range(N_Q):
                kh, rows = _head_rows(h)
                dattn_ref[N_Q, h] = jnp.sum(dsink_acc[kh, rows, :])
                dbh = dbias_ref[kh, rows, :]
                for b in range(REL_BUCKETS):
                    dattn_ref[h, b] = jnp.sum(jnp.where(bucket == b, dbh, 0.0))

    cl = lambda i: jnp.minimum(i, nb - 1)
    const = lambda shape: pl.BlockSpec(shape, lambda i: (0,) * len(shape))
    return _call(
        body,
        name="mixer_bwd",
        after=after,
        grid=(nb + 1,),
        in_specs=_mixer_specs(nb)[:7] + [
            const((BLOCK, 2 * BLOCK)),
            pl.BlockSpec((BLOCK, D_MODEL), lambda i: (cl(i), 0)),
            pl.BlockSpec((BLOCK, 2 * D_GMLP), lambda i: (cl(i), 0)),
            pl.BlockSpec((1, N_KV, ROWS4, 2 * BLOCK), lambda i: (cl(i), 0, 0, 0)),
            pl.BlockSpec((1, N_KV, ROWS4, LANES), lambda i: (cl(i), 0, 0, 0)),
        ],
        out_specs=[
            pl.BlockSpec((BLOCK, D_MAIN), lambda i: (cl(i), 0)),
            pl.BlockSpec((BLOCK, 2 * D_KV), lambda i: (jnp.maximum(i - 1, 0), 0)),
            const((1, D_GMLP)),
            const((GROUPS, BLOCK, BLOCK)),
            const((GROUPS, BLOCK)),
            pl.BlockSpec(memory_space=pltpu.SMEM),
        ],
        out_shape=[
            jax.ShapeDtypeStruct((s, D_MAIN), BF),
            jax.ShapeDtypeStruct((s, 2 * D_KV), BF),
            jax.ShapeDtypeStruct((1, D_GMLP), F32),
            jax.ShapeDtypeStruct((GROUPS, BLOCK, BLOCK), F32),
            jax.ShapeDtypeStruct((GROUPS, BLOCK), F32),
            jax.ShapeDtypeStruct((N_Q + 1, REL_BUCKETS), F32),
        ],
        scratch_shapes=[
            pltpu.VMEM((N_KV, ROWS4, 2 * BLOCK), F32),
            pltpu.VMEM((BLOCK, 2 * D_KV), F32),
            pltpu.VMEM((N_KV, ROWS4, 1), F32),
            pltpu.VMEM((GROUPS, BLOCK, 1), F32),
        ],
        compiler_params=_params(("arbitrary",)),
    )(zuv, qkv, qkv, qkv, gv, w_sp, b_sp, bucket, dmix, *saved)


def _in_bwd(x, dh1, dzm, dkv, g1, w_in_t, after=None):
    s = x.shape[0]
    tm = min(IO_TOK_TILE, s)

    def body(x_ref, dh1_ref, dzm_ref, dkv_ref, g_ref, w_ref, dx_ref, dg_ref):
        @pl.when(pl.program_id(0) == 0)
        def _():
            dg_ref[...] = jnp.zeros_like(dg_ref)

        dhn = _nn(dzm_ref[...], w_ref[:D_MAIN, :]) + _nn(dkv_ref[...], w_ref[D_MAIN:, :])
        xv = x_ref[...]
        r = _rms_scale(xv)
        dg_ref[...] += jnp.sum(dhn * (xv * r), axis=0, keepdims=True)
        dx_ref[...] = dh1_ref[...] + _rms_bwd(dhn * g_ref[...], xv, r)

    tile = lambda cols: pl.BlockSpec((tm, cols), lambda i: (i, 0))
    row = pl.BlockSpec((1, D_MODEL), lambda i: (0, 0))
    return _call(
        body,
        name="in_bwd",
        after=after,
        grid=(s // tm,),
        in_specs=[tile(D_MODEL), tile(D_MODEL), tile(D_MAIN), tile(2 * D_KV), row, pl.BlockSpec((D_IN, D_MODEL), lambda i: (0, 0))],
        out_specs=[tile(D_MODEL), row],
        out_shape=[jax.ShapeDtypeStruct((s, D_MODEL), F32), jax.ShapeDtypeStruct((1, D_MODEL), F32)],
        compiler_params=_params(("arbitrary",)),
    )(x, dh1, dzm, dkv, g1, w_in_t)


def _wgrad_in(dzm, dkv, hn, after=None):
    s = hn.shape[0]
    tm = 2 * D_KV
    n_main = D_MAIN // tm

    def body(dzm_ref, dkv_ref, hn_ref, o_ref):
        i = pl.program_id(0)

        @pl.when(i < n_main)
        def _():
            o_ref[...] = _tn(dzm_ref[...], hn_ref[...])

        @pl.when(i == n_main)
        def _():
            o_ref[...] = _tn(dkv_ref[...], hn_ref[...])

    return _call(
        body,
        name="wgrad_in",
        after=after,
        grid=(n_main + 1,),
        in_specs=[
            pl.BlockSpec((s, tm), lambda i: (0, jnp.minimum(i, n_main - 1))),
            pl.BlockSpec((s, tm), lambda i: (0, 0)),
            pl.BlockSpec((s, D_MODEL), lambda i: (0, 0)),
        ],
        out_specs=pl.BlockSpec((tm, D_MODEL), lambda i: (i, 0)),
        out_shape=jax.ShapeDtypeStruct((D_IN, D_MODEL), F32),
        compiler_params=_params(("arbitrary",)),
    )(dzm, dkv, hn)


def _wgrad(a, b, tm, tn, name, peer_cols=0, after=None):
    s, m = a.shape
    n = b.shape[1]

    def body(a_ref, b_ref, o_ref, at_ref):
        @pl.when(pl.program_id(1) == 0)
        def _():
            at_ref[...] = a_ref[...].astype(BF).T

        r = _nn(at_ref[...], b_ref[...])
        if peer_cols:
            for q in range(tn // peer_cols):
                o_ref[q] = r[:, q * peer_cols:(q + 1) * peer_cols]
        else:
            o_ref[...] = r

    if peer_cols:
        out_spec = pl.BlockSpec((tn // peer_cols, tm, peer_cols), lambda i, j: (j, i, 0))
        out_shape = jax.ShapeDtypeStruct((n // peer_cols, m, peer_cols), F32)
    else:
        out_spec = pl.BlockSpec((tm, tn), lambda i, j: (i, j))
        out_shape = jax.ShapeDtypeStruct((m, n), F32)
    return _call(
        body,
        name=name,
        after=after,
        grid=(m // tm, n // tn),
        in_specs=[pl.BlockSpec((s, tm), lambda i, j: (0, i)), pl.BlockSpec((s, tn), lambda i, j: (0, j))],
        out_specs=out_spec,
        out_shape=out_shape,
        scratch_shapes=[pltpu.VMEM((tm, s), BF)],
        compiler_params=_params(("arbitrary", "arbitrary")),
    )(a, b)


def _adamw_math(w, g, m, v):
    m_new = ADAM_B1 * m + (1.0 - ADAM_B1) * g
    v_new = ADAM_B2 * v + (1.0 - ADAM_B2) * jnp.square(g)
    m_hat = m_new / (1.0 - ADAM_B1 ** ADAM_STEP)
    v_hat = v_new / (1.0 - ADAM_B2 ** ADAM_STEP)
    delta = -ADAM_LR * (m_hat / (jnp.sqrt(v_hat) + ADAM_EPS) + ADAM_WD * w)
    return delta, m_new, v_new


def _final_adamw(part, recv, w, m, v, name):
    r, c = w.shape
    tr = min(r, 512)

    def body(p_ref, r_ref, w_ref, m_ref, v_ref, g_ref, d_ref, mo_ref, vo_ref):
        g = p_ref[...]
        for j in range(3):
            g = g + r_ref[j].astype(F32)
        g_ref[...] = g
        d_ref[...], mo_ref[...], vo_ref[...] = _adamw_math(w_ref[...], g, m_ref[...], v_ref[...])

    spec = pl.BlockSpec((tr, c), lambda i: (i, 0))
    return _call(
        body,
        name=name,
        grid=(r // tr,),
        in_specs=[spec, pl.BlockSpec((3, tr, c), lambda i: (0, i, 0)), spec, spec, spec],
        out_specs=[spec] * 4,
        out_shape=[jax.ShapeDtypeStruct((r, c), F32)] * 4,
        compiler_params=_params(("arbitrary",)),
    )(part, recv, w, m, v)


def _rs_sum(g, land, blocks, name):
    _, r, c = g.shape
    tr = min(r, 256)

    def body(blk_ref, g0_ref, g1_ref, g2_ref, g3_ref, l_ref, part_ref, send_ref):
        part_ref[...] = g0_ref[0] + l_ref[0]
        for j, gj_ref in enumerate((g1_ref, g2_ref, g3_ref)):
            send_ref[j] = (gj_ref[0] + l_ref[j + 1]).astype(BF)

    def pick(j):
        return pl.BlockSpec((1, tr, c), lambda i, blk: (blk[j], i, 0))

    return _call(
        body,
        name=name,
        grid_spec=pltpu.PrefetchScalarGridSpec(
            num_scalar_prefetch=1,
            grid=(r // tr,),
            in_specs=[pick(0), pick(1), pick(2), pick(3), pl.BlockSpec((4, tr, c), lambda i, blk: (0, i, 0))],
            out_specs=[pl.BlockSpec((tr, c), lambda i, blk: (i, 0)), pl.BlockSpec((3, tr, c), lambda i, blk: (0, i, 0))],
        ),
        out_shape=[jax.ShapeDtypeStruct((r, c), F32), jax.ShapeDtypeStruct((3, r, c), BF)],
        compiler_params=_params(("arbitrary",)),
    )(blocks, g, g, g, g, land)


def _final_adamw_few(parts, recvs, ws, ms, vs, name):
    n = len(ws)

    def body(*refs):
        ins, outs = refs[:5 * n], refs[5 * n:]
        for k in range(n):
            p_ref, r_ref, w_ref, m_ref, v_ref = (ins[q * n + k] for q in range(5))
            g = p_ref[...]
            for j in range(3):
                g = g + r_ref[j].astype(F32)
            outs[k][...] = g
            outs[n + k][...], outs[2 * n + k][...], outs[3 * n + k][...] = _adamw_math(w_ref[...], g, m_ref[...], v_ref[...])

    outs = _call(
        body,
        name=name,
        out_shape=[jax.ShapeDtypeStruct(w.shape, F32) for w in ws] * 4,
        compiler_params=_params(),
    )(*parts, *recvs, *ws, *ms, *vs)
    return outs[:n], outs[n:2 * n], outs[2 * n:3 * n], outs[3 * n:]


def _rs_sum_few(gs, lands, blocks, name):
    n = len(gs)

    def body(blk_ref, *refs):
        g_refs, l_refs, part_refs, send_refs = refs[:n], refs[n:2 * n], refs[2 * n:3 * n], refs[3 * n:]
        for k in range(n):
            part_refs[k][...] = g_refs[k][blk_ref[0]] + l_refs[k][0]
            for j in range(3):
                send_refs[k][j] = (g_refs[k][blk_ref[j + 1]] + l_refs[k][j + 1]).astype(BF)

    outs = _call(
        body,
        name=name,
        in_specs=[pl.BlockSpec(memory_space=pltpu.SMEM)] + [pl.BlockSpec(memory_space=pltpu.VMEM)] * (2 * n),
        out_shape=[jax.ShapeDtypeStruct(g.shape[1:], F32) for g in gs] + [jax.ShapeDtypeStruct((3,) + g.shape[1:], BF) for g in gs],
        compiler_params=_params(),
    )(blocks, *gs, *lands)
    return [(outs[k], outs[n + k]) for k in range(n)]


def _adamw_small(own, recv, g1_own, g1_recv, weights, moms, vels, after):
    n_w = len(weights)

    def body(*refs):
        own_refs, recv_refs, g1_own_ref, g1_recv_ref = refs[:5], refs[5:10], refs[10], refs[11]
        w_refs, m_refs, v_refs = refs[12:12 + n_w], refs[12 + n_w:12 + 2 * n_w], refs[12 + 2 * n_w:12 + 3 * n_w]
        outs = refs[12 + 3 * n_w:]
        loss_ref, g_refs, d_refs = outs[0], outs[1:1 + n_w], outs[1 + n_w:1 + 2 * n_w]
        mo_refs, vo_refs = outs[1 + 2 * n_w:1 + 3 * n_w], outs[1 + 3 * n_w:]
        x, y, c = lax.axis_index("x"), lax.axis_index("y"), lax.axis_index("c")

        def in_place_order(values, my_place):
            acc = None
            for place in range(len(values)):
                r = place ^ my_place
                term = values[-1]
                for q in range(len(values) - 2, -1, -1):
                    term = jnp.where(r == q, values[q], term)
                acc = term if acc is None else acc + term
            return acc

        def total(k, *index):
            index = index or (slice(None),) * (len(own_refs[k].shape) - 1)
            across = [own_refs[k][(0, *index)], recv_refs[k][(1, *index)], recv_refs[k][(0, *index)], recv_refs[k][(2, *index)]]
            return in_place_order([v.astype(F32) for v in across], 2 * x + y)

        g1 = in_place_order([g1_own_ref[0]] + [g1_recv_ref[j] for j in range(N_DEV - 1)], 4 * x + 2 * y + c)
        grads = [
            g1, total(1), total(2), total(3),
            total(4, slice(N_Q, None), slice(0, N_Q)), total(4, slice(0, N_Q), slice(None)),
            total(0, slice(0, 1), slice(None)), total(0, slice(1, 2), slice(None))]
        loss_ref[...] = total(0, slice(2, 3), slice(0, 1))
        for k in range(n_w):
            g_refs[k][...] = grads[k]
            d_refs[k][...], mo_refs[k][...], vo_refs[k][...] = _adamw_math(w_refs[k][...], grads[k], m_refs[k][...], v_refs[k][...])

    shapes = [jax.ShapeDtypeStruct(w.shape, F32) for w in weights]
    outs = _call(
        body,
        name="adamw_small",
        after=after,
        in_specs=[pl.BlockSpec(memory_space=pltpu.VMEM)] * (12 + 3 * n_w),
        out_shape=[jax.ShapeDtypeStruct((1, 1), F32)] + shapes * 4,
        compiler_params=_params(),
    )(*own, *recv, g1_own, g1_recv, *weights, *moms, *vels)
    return outs[0], outs[1:1 + n_w], outs[1 + n_w:1 + 2 * n_w], outs[1 + 2 * n_w:1 + 3 * n_w], outs[1 + 3 * n_w:]


def _place():
    x, y, c = lax.axis_index("x"), lax.axis_index("y"), lax.axis_index("c")
    return x, y, c, [(1 - x, y), (x, 1 - y), (1 - x, 1 - y)]


def _dev_index(px, py, pc):
    return 4 * px + 2 * py + pc


HBM_SPEC = pl.BlockSpec(memory_space=pltpu.HBM)
SEM_SPEC = pl.BlockSpec(memory_space=pltpu.SEMAPHORE)
ANY_SPEC = pl.BlockSpec(memory_space=pl.ANY)
DATAFLOW = pltpu.SideEffectType.DATAFLOW_SIDE_EFFECTING


def _hbm(a):
    return pltpu.with_memory_space_constraint(a, pltpu.HBM)


def _prep_weights(shards):
    k_n = len(shards)

    def body(*refs):
        ins, outs, stage, sems = refs[:k_n], refs[k_n:2 * k_n], refs[2 * k_n:3 * k_n], refs[3 * k_n]
        x, y, c, _ = _place()
        copies = []
        for k in range(k_n):
            stage[k][...] = ins[k][...].astype(BF)
            copies.append(pltpu.make_async_copy(stage[k], outs[k].at[_dev_index(x, y, c)], sems.at[k]))
            copies[k].start()
        for cp in copies:
            cp.wait()

    return _call(
        body,
        name="prep_weights",
        in_specs=[pl.BlockSpec(memory_space=pltpu.VMEM)] * k_n,
        out_specs=[ANY_SPEC] * k_n,
        out_shape=[jax.ShapeDtypeStruct((N_DEV,) + sh.shape, BF) for sh in shards],
        scratch_shapes=[pltpu.VMEM(sh.shape, BF) for sh in shards] + [pltpu.SemaphoreType.DMA((k_n,))],
        compiler_params=_params(),
    )(*shards)


def _ag_piece(land_k, block, half, peer, send_sem, recv_sem):
    ref = land_k.at[_dev_index(*block)]
    if half is not None:
        rows = land_k.shape[1] // 2
        ref = ref.at[pl.ds(half * rows, rows)]
    return pltpu.make_async_remote_copy(
        src_ref=ref, dst_ref=ref, send_sem=send_sem, recv_sem=recv_sem, device_id=peer, device_id_type=MESH)


def _ag_plan():
    x, y, c, _ = _place()
    me, sib = (x, y, c), (x, y, 1 - c)
    xn, yn, diag = (1 - x, y, c), (x, 1 - y, c), (1 - x, 1 - y, c)
    return dict(
        relay_halves=[(me, 0, xn), (me, 1, yn)],
        others=[(me, None, sib), (me, 1, xn), (me, 0, yn)],
        relays=[(xn, 0, yn), (yn, 1, xn)],
        near=[(xn, None, sib), (yn, None, sib)],
        far=[(diag, None, sib)],
    )


def _ag_stage(land, stage, send_sems, recv_sems, act):
    copies = _ag_plan()[stage]
    n = len(copies)
    for k in range(len(land)):
        for j, (block, half, peer) in enumerate(copies):
            cp = _ag_piece(land[k], block, half, peer, send_sems.at[n * k + j], recv_sems.at[n * k + j])
            if act == "start":
                cp.start()
            else:
                cp.wait_send()
                cp.wait_recv()


def _sem_shapes(*counts):
    return [pltpu.SemaphoreType.DMA((n,)) for n in counts for _ in range(2)]


def _ag_start(first, rest):
    lands = list(first) + list(rest)
    k_n, k_first = len(lands), len(first)
    k_rest = k_n - k_first

    def body(*refs):
        land = refs[:k_n]
        sems = refs[k_n:k_n + 6]
        token = refs[-1]
        x, y, c, chips = _place()
        targets = [(x, y, 1 - c)] + [(*chip, c) for chip in chips]
        for k in range(k_first):
            for j, to in enumerate(targets):
                _ag_piece(land[k], (x, y, c), None, to, sems[0].at[4 * k + j], sems[1].at[4 * k + j]).start()
        _ag_stage(land[k_first:], "relay_halves", sems[2], sems[3], "start")
        _ag_stage(land[k_first:], "others", sems[4], sems[5], "start")
        token[...] = jnp.zeros_like(token)

    outs = pl.pallas_call(
        body,
        name="ag_start",
        in_specs=[HBM_SPEC] * k_n,
        out_specs=(*[SEM_SPEC] * 6, *[HBM_SPEC] * k_n, pl.BlockSpec(memory_space=pltpu.VMEM)),
        out_shape=(*_sem_shapes(4 * k_first, 2 * k_rest, 3 * k_rest),
                   *[pltpu.HBM(a.shape, a.dtype) for a in lands], jax.ShapeDtypeStruct((8, LANES), F32)),
        input_output_aliases={k: 6 + k for k in range(k_n)},
        compiler_params=pltpu.CompilerParams(has_side_effects=DATAFLOW),
    )(*[_hbm(a) for a in lands])
    flying = list(outs[6:6 + k_n])
    return (outs[0], outs[1], flying[:k_first]), (outs[2:6], flying[k_first:]), outs[-1]


def _ag_split_call(lands, waits, starts, after, name):
    k_n = len(lands)
    plan_sizes = dict(relay_halves=2, others=3, relays=2, near=2, far=1)
    n_in, n_out = 2 * len(waits), 2 * len(starts)

    def body(*refs):
        land = refs[:k_n]
        in_sems = refs[k_n:k_n + n_in]
        out_sems, token = refs[len(refs) - 1 - n_out:len(refs) - 1], refs[-1]
        for w, (stage, _, _) in enumerate(waits):
            _ag_stage(land, stage, in_sems[2 * w], in_sems[2 * w + 1], "wait")
            if w < len(starts):
                _ag_stage(land, starts[w], out_sems[2 * w], out_sems[2 * w + 1], "start")
        token[...] = jnp.zeros_like(token)

    outs = pl.pallas_call(
        body,
        name=name,
        in_specs=[HBM_SPEC] * k_n + [SEM_SPEC] * n_in + [ANY_SPEC],
        out_specs=(*[HBM_SPEC] * k_n, *[SEM_SPEC] * n_out, pl.BlockSpec(memory_space=pltpu.VMEM)),
        out_shape=(*[pltpu.HBM(a.shape, a.dtype) for a in lands], *_sem_shapes(*[plan_sizes[s] * k_n for s in starts]),
                   jax.ShapeDtypeStruct((8, LANES), F32)),
        input_output_aliases={k: k for k in range(k_n)},
        compiler_params=pltpu.CompilerParams(has_side_effects=DATAFLOW),
    )(*lands, *[s for _, a, b in waits for s in (a, b)], after)
    return list(outs[:k_n]), list(outs[k_n:k_n + n_out]), outs[-1]


def _ag_mid(lands, send_sems, recv_sems, after, name):
    k_n = len(lands)

    def body(*refs):
        land = refs[:k_n]
        send1, recv1 = refs[k_n], refs[k_n + 1]
        fwd_send, fwd_recv = refs[-2], refs[-1]
        x, y, c, chips = _place()
        sources = [(x, y, 1 - c)] + [(*chip, c) for chip in chips]
        for k in range(k_n):
            mine = land[k].at[_dev_index(x, y, c)]
            for j, frm in enumerate(sources):
                got = land[k].at[_dev_index(*frm)]
                cp = pltpu.make_async_remote_copy(
                    src_ref=mine, dst_ref=got, send_sem=send1.at[4 * k + j], recv_sem=recv1.at[4 * k + j], device_id=frm, device_id_type=MESH)
                cp.wait_send()
                cp.wait_recv()
                if j >= 1:
                    pltpu.make_async_remote_copy(
                        src_ref=got, dst_ref=got, send_sem=fwd_send.at[3 * k + j - 1], recv_sem=fwd_recv.at[3 * k + j - 1],
                        device_id=(x, y, 1 - c), device_id_type=MESH).start()

    outs = pl.pallas_call(
        body,
        name=name,
        in_specs=[HBM_SPEC] * k_n + [SEM_SPEC, SEM_SPEC, ANY_SPEC],
        out_specs=(*[HBM_SPEC] * k_n, SEM_SPEC, SEM_SPEC),
        out_shape=(*[pltpu.HBM(a.shape, a.dtype) for a in lands], pltpu.SemaphoreType.DMA((3 * k_n,)), pltpu.SemaphoreType.DMA((3 * k_n,))),
        input_output_aliases={k: k for k in range(k_n)},
        compiler_params=pltpu.CompilerParams(has_side_effects=DATAFLOW),
    )(*lands, send_sems, recv_sems, after)
    return list(outs[:k_n]), outs[-2], outs[-1]


def _ag_end(lands, fwd_send, fwd_recv, name):
    k_n = len(lands)

    def body(*refs):
        land = refs[:k_n]
        fsend, frecv = refs[k_n], refs[k_n + 1]
        x, y, c, chips = _place()
        for k in range(k_n):
            for j, chip in enumerate(chips):
                cp = pltpu.make_async_remote_copy(
                    src_ref=land[k].at[_dev_index(*chip, c)], dst_ref=land[k].at[_dev_index(*chip, 1 - c)],
                    send_sem=fsend.at[3 * k + j], recv_sem=frecv.at[3 * k + j], device_id=(x, y, 1 - c), device_id_type=MESH)
                cp.wait_send()
                cp.wait_recv()

    outs = pl.pallas_call(
        body,
        name=name,
        in_specs=[HBM_SPEC] * k_n + [SEM_SPEC, SEM_SPEC],
        out_specs=tuple([HBM_SPEC] * k_n),
        out_shape=tuple(pltpu.HBM(a.shape, a.dtype) for a in lands),
        input_output_aliases={k: k for k in range(k_n)},
        compiler_params=pltpu.CompilerParams(has_side_effects=DATAFLOW),
    )(*lands, fwd_send, fwd_recv)
    return list(outs)


def _chips4():
    x, y, c, others = _place()
    return x, y, c, [(x, y)] + others


def _route_sibling(j):
    x, y, c, chips = _chips4()
    return _dev_index(*chips[j], 1 - c), j, (x, y, 1 - c)


def _route_chips(j):
    x, y, c, chips = _chips4()
    return j, j, (*chips[j + 1], c)


def _route_sibling_whole(j):
    x, y, c, _ = _chips4()
    return 0, 0, (x, y, 1 - c)


def _route_chips_whole(j):
    x, y, c, chips = _chips4()
    return 0, j, (*chips[j + 1], c)


def _route_everyone(j):
    x, y, c, _ = _chips4()
    flip = [(j + 1) >> 2 & 1, (j + 1) >> 1 & 1, (j + 1) & 1]
    return 0, j, tuple(1 - v if f else v for v, f in zip((x, y, c), flip))


def _xchg_copies(routes, src, dst, send_sems, recv_sems):
    copies, sem = [], 0
    for k, (route, n) in enumerate(routes):
        for j in range(n):
            si, di, peer = route(j)
            copies.append(pltpu.make_async_remote_copy(
                src_ref=src[k].at[si], dst_ref=dst[k].at[di], send_sem=send_sems.at[sem], recv_sem=recv_sems.at[sem],
                device_id=peer, device_id_type=MESH))
            sem += 1
    return copies


def _xchg_start(srcs, slot_shapes, routes, name, after=None):
    k_n = len(srcs)
    n_in = 2 * k_n + (after is not None)
    n_sem = sum(n for _, n in routes)
    dsts = [lax.empty((n,) + tuple(sh), a.dtype) for sh, a, (_, n) in zip(slot_shapes, srcs, routes)]

    def body(*refs):
        src, dst = refs[:k_n], refs[k_n:2 * k_n]
        send_sems, recv_sems, token = refs[n_in], refs[n_in + 1], refs[-1]
        for cp in _xchg_copies(routes, src, dst, send_sems, recv_sems):
            cp.start()
        token[...] = jnp.zeros_like(token)

    arrays = list(srcs) + dsts
    outs = pl.pallas_call(
        body,
        name=name,
        in_specs=[HBM_SPEC] * (2 * k_n) + [ANY_SPEC] * (n_in - 2 * k_n),
        out_specs=(SEM_SPEC, SEM_SPEC, *[HBM_SPEC] * (2 * k_n), pl.BlockSpec(memory_space=pltpu.VMEM)),
        out_shape=(pltpu.SemaphoreType.DMA((n_sem,)), pltpu.SemaphoreType.DMA((n_sem,)),
                   *[pltpu.HBM(a.shape, a.dtype) for a in arrays], jax.ShapeDtypeStruct((8, LANES), F32)),
        input_output_aliases={i: 2 + i for i in range(2 * k_n)},
        compiler_params=pltpu.CompilerParams(has_side_effects=DATAFLOW),
    )(*[_hbm(a) for a in arrays], *([] if after is None else [after]))
    return outs[0], outs[1], list(outs[2:2 + k_n]), list(outs[2 + k_n:2 + 2 * k_n]), outs[-1]


def _xchg_wait(send_sems, recv_sems, srcs, dsts, routes, after, name):
    k_n = len(srcs)

    def body(*refs):
        src, dst = refs[:k_n], refs[k_n:2 * k_n]
        for cp in _xchg_copies(routes, src, dst, refs[2 * k_n], refs[2 * k_n + 1]):
            cp.wait_send()
            cp.wait_recv()

    arrays = list(srcs) + list(dsts)
    outs = pl.pallas_call(
        body,
        name=name,
        in_specs=[HBM_SPEC] * (2 * k_n) + [SEM_SPEC, SEM_SPEC, ANY_SPEC],
        out_specs=tuple([HBM_SPEC] * (2 * k_n)),
        out_shape=tuple(pltpu.HBM(a.shape, a.dtype) for a in arrays),
        input_output_aliases={i: i for i in range(2 * k_n)},
        compiler_params=pltpu.CompilerParams(has_side_effects=DATAFLOW),
    )(*arrays, send_sems, recv_sems, after)
    return list(outs[:k_n]), list(outs[k_n:])


SMALL = ("norm1_gain", "gmlp_v_gain", "w_spatial", "b_spatial", "attn_sinks", "rel_bias_table", "norm2_gain", "final_gain")
LANES = 128


def _swap_start(grads, smalls, tag, after=None):
    srcs = list(grads) + [a[None] for a in smalls]
    shapes = [g.shape[1:] for g in grads] + [a.shape for a in smalls]
    routes = [(_route_sibling, 4)] * len(grads) + [(_route_sibling_whole, 1)] * len(smalls)
    return _xchg_start(srcs, shapes, routes, f"rs_{tag}_swap_start", after), routes, len(grads)


def _swap_sums(swap, names, after, tag, few=False, small_dtypes=None):
    (send1, recv1, src1, land1, _), routes, n_rs = swap
    x, y, c, chips = _chips4()
    blocks = jnp.stack([_dev_index(*chip, c) for chip in chips]).astype(jnp.int32)
    src1, land1 = _xchg_wait(send1, recv1, src1, land1, routes, after, f"rs_{tag}_swap_wait")
    if few:
        sums = _rs_sum_few(src1[:n_rs], land1[:n_rs], blocks, f"rs_{tag}_sum")
    else:
        sums = [_rs_sum(g, land, blocks, f"rs_sum_{n}") for g, land, n in zip(src1[:n_rs], land1[:n_rs], names)]
    if len(src1) == n_rs:
        return sums, []
    return sums, _pair_sum(src1[n_rs:], land1[n_rs:], small_dtypes or [F32] * (len(src1) - n_rs), f"rs_{tag}_sum_small")


def _pair_sum(mine, theirs, dtypes, name):
    def body(*refs):
        n = len(refs) // 3
        for k in range(n):
            refs[2 * n + k][...] = (refs[k][...] + refs[n + k][...]).astype(dtypes[k])

    return _call(
        body,
        name=name,
        out_shape=[jax.ShapeDtypeStruct(a.shape, dt) for a, dt in zip(mine, dtypes)],
        compiler_params=_params(),
    )(*mine, *theirs)


def _chips_start(sums, small_sums, tag, everyone=()):
    sends = [ps[1] for ps in sums] + list(small_sums) + [a[None] for a in everyone]
    routes = [(_route_chips, 3)] * len(sums) + [(_route_chips_whole, 3)] * len(small_sums) + [(_route_everyone, 7)] * len(everyone)
    return _xchg_start(sends, [a.shape[1:] for a in sends], routes, f"rs_{tag}_chips_start"), routes


def _chips_wait(chips, after, tag):
    (send2, recv2, src2, land2, _), routes = chips
    return _xchg_wait(send2, recv2, src2, land2, routes, after, f"rs_{tag}_chips_wait")


def kernel(x, p, norm1_gain, w_in, gmlp_v_gain, w_spatial, b_spatial, attn_sinks, rel_bias_table, w_out, norm2_gain, w_ff1, w_ff2, w_ple_proj, w_ple_gate, final_gain, loss_target, m_norm1_gain, m_w_in, m_gmlp_v_gain, m_w_spatial, m_b_spatial, m_attn_sinks, m_rel_bias_table, m_w_out, m_norm2_gain, m_w_ff1, m_w_ff2, m_w_ple_proj, m_w_ple_gate, m_final_gain, v_norm1_gain, v_w_in, v_gmlp_v_gain, v_w_spatial, v_b_spatial, v_attn_sinks, v_rel_bias_table, v_w_out, v_norm2_gain, v_w_ff1, v_w_ff2, v_w_ple_proj, v_w_ple_gate, v_final_gain):
    args = dict(locals())
    s = x.shape[1]
    big = ("w_in", "w_out", "w_ff1", "w_ff2", "w_ple_proj", "w_ple_gate")

    x2, p2, t2 = x.reshape(s, D_MODEL), p.reshape(s, PLE_DIM), loss_target.reshape(s, D_MODEL)
    g1, gv, w_sp, b_sp, sinks, table, g2, gf = (args[n] for n in SMALL)
    bucket = jnp.asarray(_bucket_table())
    b_col = b_sp.reshape(GROUPS, BLOCK, 1)

    def shard(name):
        return args[name][0].T if name.endswith("w_in") else args[name][0]

    lands = _prep_weights([shard(n) for n in big])
    (send_in, recv_in, fly_in), (rest_sems, fly_rest), token = _ag_start(lands[:1], lands[1:])
    mid_in, fwd_send_in, fwd_recv_in = _ag_mid(fly_in, send_in, recv_in, token, "ag_mid_w_in")
    g_in = _ag_end(mid_in, fwd_send_in, fwd_recv_in, "ag_end_w_in")[0]
    full_in = g_in.reshape(D_IN, D_MODEL)

    zuv, qkv, hn1 = _in_proj(x2, g1, full_in)
    fly_rest, relay_sems, relayed = _ag_split_call(
        fly_rest, [("relay_halves", *rest_sems[:2])], ["relays"], zuv, "ag_relay")
    mix, *saved = _mixer_fwd(zuv, qkv, gv, w_sp[0], b_col, sinks, table.T, bucket, after=relayed)
    fly_rest, fwd_sems, _ = _ag_split_call(
        fly_rest, [("others", *rest_sems[2:]), ("relays", *relay_sems)], ["near", "far"], mix, "ag_mid_rest")
    g_out, g_ff1, g_ff2, g_proj, g_gate = _ag_split_call(
        fly_rest, [("near", *fwd_sems[:2]), ("far", *fwd_sems[2:])], [], mix, "ag_end_rest")[0]
    full_out, full_ff2, full_gate = g_out.reshape(D_MODEL, D_MODEL), g_ff2.reshape(D_FF, D_MODEL), g_gate.reshape(D_MODEL, D_MODEL)
    full_proj = g_proj.transpose(1, 0, 2).reshape(PLE_DIM, D_MODEL)

    tail_small, dh1, dh1b, dmix, hn2, a, df, dh2, gw_proj, gw_gate = _tail(
        x2, mix, p2, t2, g2, gf.reshape(1, D_MODEL), full_out, g_ff1, full_ff2, full_gate, full_proj)
    gw_gate = gw_gate.reshape(N_DEV, D_MODEL // N_DEV, D_MODEL)

    mid_names = ("w_out", "w_ple_proj", "w_ple_gate")
    gw_ff1 = _wgrad(hn2, df, D_MODEL, 2 * D_FF // N_DEV, "wgrad_ff1", peer_cols=D_FF // N_DEV)
    ff1_swap = _swap_start([gw_ff1], [], "ff1")
    gw_ff2 = _wgrad(a, dh2, 1024, D_MODEL, "wgrad_ff2", after=ff1_swap[0][4]).reshape(N_DEV, D_FF // N_DEV, D_MODEL)
    ff1_sums, _ = _swap_sums(ff1_swap, ("w_ff1",), gw_ff2, "ff1")
    ff1_chips = _chips_start(ff1_sums, [], "ff1")
    ff2_swap = _swap_start([gw_ff2], [], "ff2", after=ff1_chips[0][4])
    gw_out = _wgrad(mix, dh1b, D_MODEL, 512, "wgrad_out", after=ff2_swap[0][4]).reshape(N_DEV, D_MODEL // N_DEV, D_MODEL)
    ff2_sums, _ = _swap_sums(ff2_swap, ("w_ff2",), gw_out, "ff2")
    ff2_chips = _chips_start(ff2_sums, [], "ff2")
    mid_swap = _swap_start([gw_out, gw_proj, gw_gate], [tail_small], "mid", after=ff2_chips[0][4])

    dzm, dkv, d_gv, d_wsp, d_bsp, d_attn = _mixer_bwd(
        zuv, qkv, dmix, saved, gv, w_sp[0], b_col, bucket, after=mid_swap[0][4])
    mid_sums, mid_small = _swap_sums(mid_swap, mid_names, dzm, "mid", few=True)
    mid_chips = _chips_start(mid_sums, mid_small, "mid")
    gw_in = _wgrad_in(dzm, dkv, hn1, after=mid_chips[0][4]).reshape(N_DEV, D_IN // N_DEV, D_MODEL)
    in_swap = _swap_start([gw_in], [d_gv, d_wsp, d_bsp, d_attn], "in")
    dx, d_g1 = _in_bwd(x2, dh1, dzm, dkv, g1, full_in, after=in_swap[0][4])
    grad_x = dx.reshape(x.shape)
    in_sums, in_small = _swap_sums(in_swap, ("w_in",), dx, "in", small_dtypes=[F32, BF, F32, F32])
    in_chips = _chips_start(in_sums, in_small, "in", everyone=[d_g1])

    grads, deltas, new_m, new_v = {}, {}, {}, {}
    after, small_own, small_recv = in_chips[0][4], [], []
    for chips, sums, names, tag in ((ff1_chips, ff1_sums, ("w_ff1",), "ff1"), (ff2_chips, ff2_sums, ("w_ff2",), "ff2"),
                                    (mid_chips, mid_sums, mid_names, "mid"), (in_chips, in_sums, ("w_in",), "in")):
        sent, recvs = _chips_wait(chips, after, tag)
        small_own += sent[len(names):]
        small_recv += recvs[len(names):]
        if tag == "mid":
            done = _final_adamw_few([ps[0] for ps in sums], recvs[:len(names)], [shard(n) for n in names],
                                    [shard("m_" + n) for n in names], [shard("v_" + n) for n in names], "adamw_mid")
            done = list(zip(*done))
        else:
            done = [_final_adamw(part, recv, shard(n), shard("m_" + n), shard("v_" + n), "adamw_" + n)
                    for n, (part, _), recv in zip(names, sums, recvs)]
        for n, results in zip(names, done):
            for dst, arr in zip((grads, deltas, new_m, new_v), results):
                dst[n] = (arr.T if n == "w_in" else arr)[None]
            after = results[1]

    views = {"w_spatial": (GROUPS, BLOCK, BLOCK), "b_spatial": (GROUPS, BLOCK), "final_gain": (1, D_MODEL)}
    def view(name, base):
        return args[name].T if base == "rel_bias_table" else args[name].reshape(views.get(base, args[name].shape))

    small_in = [[view(pre + n, n) for n in SMALL] for pre in ("", "m_", "v_")]
    loss, *small_out = _adamw_small(small_own[:-1], small_recv[:-1], small_own[-1], small_recv[-1], *small_in, after=after)
    for dst, arrays in zip((grads, deltas, new_m, new_v), small_out):
        for n, arr in zip(SMALL, arrays):
            dst[n] = arr.T if n == "rel_bias_table" else arr.reshape(args[n].shape)
    loss = loss[0, 0]

    order = ("norm1_gain", "w_in", "gmlp_v_gain", "w_spatial", "b_spatial", "attn_sinks", "rel_bias_table", "w_out",
             "norm2_gain", "w_ff1", "w_ff2", "w_ple_proj", "w_ple_gate", "final_gain")
    return (loss, grad_x, *[grads[n] for n in order], *[deltas[n] for n in order],
            *[new_m[n] for n in order], *[new_v[n] for n in order])
```

```python
import math

import numpy as np
import jax
import jax.numpy as jnp
from jax import lax
from jax.experimental import pallas as pl
from jax.experimental.pallas import tpu as pltpu

F32 = jnp.float32
BF = jnp.bfloat16
MESH = pl.DeviceIdType.MESH
N_DEV = 8

D_MODEL = 1024
PLE_DIM = 256
D_GMLP = 512
GROUPS = 4
GDIM = 128
BLOCK = 128
D_ATTN = 512
HEAD_DIM = 64
N_Q = 8
Q_PER_KV = 4
N_KV = N_Q // Q_PER_KV
ROWS4 = Q_PER_KV * BLOCK
D_KV = 128
D_FF = 4096
D_IN = 1792
D_MAIN = 2 * D_GMLP + D_ATTN
REL_BUCKETS = 32
EPS = 1e-6
NEG_INF = -1e30
SCALE = HEAD_DIM ** -0.5
GELU_C = math.sqrt(2.0 / math.pi)
GELU_A = 0.044715

ADAM_LR = 0.001
ADAM_B1 = 0.9
ADAM_B2 = 0.999
ADAM_EPS = 1e-08
ADAM_WD = 0.01
ADAM_STEP = 10

V7X_VMEM_LIMIT = 61440 * 1024
TOK_TILE = 256
IO_TOK_TILE = 512


def _call(body, after=None, **kw):
    if after is None:
        return pl.pallas_call(body, **kw)
    n_in = len(kw["in_specs"])

    def ordered(*refs):
        body(*refs[:n_in], *refs[n_in + 1:])

    kw["in_specs"] = list(kw["in_specs"]) + [pl.BlockSpec(memory_space=pl.ANY)]
    fn = pl.pallas_call(ordered, **kw)
    return lambda *operands: fn(*operands, after)


def _params(sem=None):
    if sem is None:
        return pltpu.CompilerParams(vmem_limit_bytes=V7X_VMEM_LIMIT)
    return pltpu.CompilerParams(dimension_semantics=sem, vmem_limit_bytes=V7X_VMEM_LIMIT)


def _nn(a, b):
    return jnp.dot(a, b, preferred_element_type=F32)


def _nt(a, b):
    return lax.dot_general(a, b, (((1,), (1,)), ((), ())), preferred_element_type=F32)


def _tn(a, b):
    return lax.dot_general(a, b, (((0,), (0,)), ((), ())), preferred_element_type=F32)


def _gelu_tanh(x):
    return jnp.tanh(GELU_C * (x + GELU_A * (x * x * x)))


def _gelu(x, t):
    return x * (0.5 * (1.0 + t))


def _gelu_and_grad(x, t):
    cdf = 0.5 * (1.0 + t)
    return x * cdf, cdf + 0.5 * x * (1.0 - t * t) * (GELU_C * (1.0 + 3.0 * GELU_A * (x * x)))


def _rms_scale(x):
    return lax.rsqrt(jnp.mean(x * x, axis=-1, keepdims=True) + EPS)


def _rms_bwd(dxn, x, r):
    return r * dxn - x * ((r * r * r) * jnp.mean(dxn * x, axis=-1, keepdims=True))


def _bucket_table():
    a = np.arange(BLOCK)[:, None]
    j = np.arange(2 * BLOCK)[None, :]
    n = BLOCK + a - j
    valid = (n >= 0) & (n < BLOCK)
    nc = np.maximum(n, 0)
    max_exact = REL_BUCKETS // 2
    nf = np.maximum(nc, 1).astype(np.float32)
    large = max_exact + (
        np.log(nf / np.float32(max_exact)) / np.float32(math.log(BLOCK / max_exact)) * np.float32(REL_BUCKETS - max_exact)
    ).astype(np.int32)
    large = np.minimum(large, REL_BUCKETS - 1)
    bucket = np.where(nc < max_exact, nc, large)
    return np.where(valid, bucket, -1).astype(np.int32)


def _in_proj(x, g1, w_in_t):
    s = x.shape[0]
    tm = min(IO_TOK_TILE, s)

    def body(x_ref, g_ref, w_ref, zuv_ref, qkv_ref, hn_ref):
        xv = x_ref[...]
        hn = ((xv * _rms_scale(xv)) * g_ref[...]).astype(BF)
        hn_ref[...] = hn
        z = _nt(hn, w_ref[...])
        zuv_ref[...] = z[:, : 2 * D_GMLP]
        qkv_ref[...] = z[:, 2 * D_GMLP:].astype(BF)

    return _call(
        body,
        name="in_proj",
        grid=(s // tm,),
        in_specs=[
            pl.BlockSpec((tm, D_MODEL), lambda i: (i, 0)),
            pl.BlockSpec((1, D_MODEL), lambda i: (0, 0)),
            pl.BlockSpec((D_IN, D_MODEL), lambda i: (0, 0)),
        ],
        out_specs=[
            pl.BlockSpec((tm, 2 * D_GMLP), lambda i: (i, 0)),
            pl.BlockSpec((tm, D_ATTN + 2 * D_KV), lambda i: (i, 0)),
            pl.BlockSpec((tm, D_MODEL), lambda i: (i, 0)),
        ],
        out_shape=[
            jax.ShapeDtypeStruct((s, 2 * D_GMLP), F32),
            jax.ShapeDtypeStruct((s, D_ATTN + 2 * D_KV), BF),
            jax.ShapeDtypeStruct((s, D_MODEL), BF),
        ],
        compiler_params=_params(("arbitrary",)),
    )(x, g1, w_in_t)


def _head_rows(h):
    kh, g = divmod(h, Q_PER_KV)
    return kh, slice(g * BLOCK, (g + 1) * BLOCK)


def _build_bias(bias_ref, bucket_ref, table_ref):
    bucket = bucket_ref[...]
    for h in range(N_Q):
        acc = jnp.zeros((BLOCK, 2 * BLOCK), F32)
        for b in range(REL_BUCKETS):
            acc = jnp.where(bucket == b, table_ref[h, b], acc)
        kh, rows = _head_rows(h)
        bias_ref[kh, rows, :] = acc


def _window_masks(i):
    row = lax.broadcasted_iota(jnp.int32, (ROWS4, BLOCK), 0) & (BLOCK - 1)
    col = lax.broadcasted_iota(jnp.int32, (ROWS4, BLOCK), 1)
    return (col > row) & (i > 0), col <= row


def _stack_heads(ref, kh, offset):
    first = offset + kh * Q_PER_KV * HEAD_DIM
    return jnp.concatenate(
        [ref[:, first + g * HEAD_DIM: first + (g + 1) * HEAD_DIM].astype(BF) for g in range(Q_PER_KV)], axis=0)


def _stack_sinks(sink_ref, kh):
    return jnp.concatenate([jnp.full((BLOCK, 1), sink_ref[0, kh * Q_PER_KV + g], F32) for g in range(Q_PER_KV)], axis=0)


def _tril_bf16(w_ref, g):
    row = lax.broadcasted_iota(jnp.int32, (BLOCK, BLOCK), 0)
    col = lax.broadcasted_iota(jnp.int32, (BLOCK, BLOCK), 1)
    return jnp.where(col <= row, w_ref[g], 0.0).astype(BF)


def _attn_probs(q_h, k_prev, k_cur, bias_h, sink, valid_prev, valid_cur):
    l_prev = jnp.where(valid_prev, _nt(q_h, k_prev) * SCALE + bias_h[:, :BLOCK], NEG_INF)
    l_cur = jnp.where(valid_cur, _nt(q_h, k_cur) * SCALE + bias_h[:, BLOCK:], NEG_INF)
    m = jnp.maximum(jnp.maximum(jnp.max(l_prev, axis=-1, keepdims=True), jnp.max(l_cur, axis=-1, keepdims=True)), sink)
    e_prev = jnp.exp(l_prev - m)
    e_cur = jnp.exp(l_cur - m)
    e_sink = jnp.exp(sink - m)
    denom = jnp.sum(e_prev, axis=-1, keepdims=True) + jnp.sum(e_cur, axis=-1, keepdims=True) + e_sink
    return e_prev / denom, e_cur / denom, e_sink / denom


def _mixer_specs(nb):
    cl = lambda i: jnp.minimum(i, nb - 1)
    return [
        pl.BlockSpec((BLOCK, 2 * D_GMLP), lambda i: (cl(i), 0)),
        pl.BlockSpec((BLOCK, D_ATTN), lambda i: (cl(i), 0)),
        pl.BlockSpec((BLOCK, 2 * D_KV), lambda i: (cl(i), D_ATTN // (2 * D_KV))),
        pl.BlockSpec((BLOCK, 2 * D_KV), lambda i: (jnp.maximum(cl(i) - 1, 0), D_ATTN // (2 * D_KV))),
        pl.BlockSpec((1, D_GMLP), lambda i: (0, 0)),
        pl.BlockSpec((GROUPS, BLOCK, BLOCK), lambda i: (0, 0, 0)),
        pl.BlockSpec((GROUPS, BLOCK, 1), lambda i: (0, 0, 0)),
        pl.BlockSpec(memory_space=pltpu.SMEM),
        pl.BlockSpec(memory_space=pltpu.SMEM),
        pl.BlockSpec((BLOCK, 2 * BLOCK), lambda i: (0, 0)),
    ]


def _mixer_fwd(zuv, qkv, gv, w_sp, b_sp, sinks, table, bucket, after=None):
    s = zuv.shape[0]
    nb = s // BLOCK

    def body(zuv_ref, q_ref, kvc_ref, kvp_ref, gv_ref, w_ref, b_ref, sink_ref, table_ref, bucket_ref,
             mix_ref, tanh_ref, prob_ref, psink_ref, bias_ref):
        i = pl.program_id(0)

        @pl.when(i == 0)
        def _():
            _build_bias(bias_ref, bucket_ref, table_ref)

        t = _gelu_tanh(zuv_ref[...])
        tanh_ref[...] = t
        u = _gelu(zuv_ref[:, :D_GMLP], t[:, :D_GMLP])
        vg = _gelu(zuv_ref[:, D_GMLP:], t[:, D_GMLP:])
        for g in range(GROUPS):
            sl = slice(g * GDIM, (g + 1) * GDIM)
            vg_g = vg[:, sl]
            vn = ((vg_g * _rms_scale(vg_g)) * gv_ref[:, sl]).astype(BF)
            sv = _nn(_tril_bf16(w_ref, g), vn) + b_ref[g]
            mix_ref[:, sl] = (u[:, sl] * sv).astype(BF)

        valid_prev, valid_cur = _window_masks(i)
        for kh in range(N_KV):
            ksl = slice(kh * HEAD_DIM, (kh + 1) * HEAD_DIM)
            vsl = slice(D_KV + kh * HEAD_DIM, D_KV + (kh + 1) * HEAD_DIM)
            q4 = _stack_heads(q_ref, kh, 0)
            p_prev, p_cur, p_sink = _attn_probs(
                q4, kvp_ref[:, ksl], kvc_ref[:, ksl], bias_ref[kh], _stack_sinks(sink_ref, kh), valid_prev, valid_cur)
            prob_ref[0, kh, :, :BLOCK] = p_prev
            prob_ref[0, kh, :, BLOCK:] = p_cur
            psink_ref[0, kh] = jnp.broadcast_to(p_sink, (ROWS4, LANES))
            o4 = _nn(p_prev.astype(BF), kvp_ref[:, vsl]) + _nn(p_cur.astype(BF), kvc_ref[:, vsl])
            for g in range(Q_PER_KV):
                first = D_GMLP + (kh * Q_PER_KV + g) * HEAD_DIM
                mix_ref[:, first:first + HEAD_DIM] = o4[g * BLOCK:(g + 1) * BLOCK].astype(BF)

    return _call(
        body,
        name="mixer_fwd",
        after=after,
        grid=(nb,),
        in_specs=_mixer_specs(nb),
        out_specs=[
            pl.BlockSpec((BLOCK, D_MODEL), lambda i: (i, 0)),
            pl.BlockSpec((BLOCK, 2 * D_GMLP), lambda i: (i, 0)),
            pl.BlockSpec((1, N_KV, ROWS4, 2 * BLOCK), lambda i: (i, 0, 0, 0)),
            pl.BlockSpec((1, N_KV, ROWS4, LANES), lambda i: (i, 0, 0, 0)),
        ],
        out_shape=[
            jax.ShapeDtypeStruct((s, D_MODEL), BF),
            jax.ShapeDtypeStruct((s, 2 * D_GMLP), F32),
            jax.ShapeDtypeStruct((nb, N_KV, ROWS4, 2 * BLOCK), F32),
            jax.ShapeDtypeStruct((nb, N_KV, ROWS4, LANES), F32),
        ],
        scratch_shapes=[pltpu.VMEM((N_KV, ROWS4, 2 * BLOCK), F32)],
        compiler_params=_params(("arbitrary",)),
    )(zuv, qkv, qkv, qkv, gv, w_sp, b_sp, sinks, table, bucket)


def _tail(x, mix, p, t, g2, gf, w_out, w_ff1, w_ff2, w_gate, w_proj):
    s = x.shape[0]
    tm = min(TOK_TILE, s)
    n_ff = w_ff1.shape[0]
    fc = D_FF // n_ff
    pc = D_MODEL // N_DEV

    def body(x_ref, mix_ref, p_ref, t_ref, g2_ref, gf_ref, wo_ref, w1_ref, w2_ref, wg_ref, wp_ref,
             small_ref, dh1_ref, dmix_ref, hn2_ref, a_ref, df_ref, dh2_ref, gwo_ref, gwp_ref, gwg_ref, f_ref):
        i = pl.program_id(0)

        @pl.when(i == 0)
        def _():
            small_ref[...] = jnp.zeros_like(small_ref)
            gwo_ref[...] = jnp.zeros_like(gwo_ref)
            gwp_ref[...] = jnp.zeros_like(gwp_ref)
            gwg_ref[...] = jnp.zeros_like(gwg_ref)

        h1 = x_ref[...] + _nn(mix_ref[...], wo_ref[...])
        dh1_ref[...] = h1
        r2 = _rms_scale(h1)
        hn2_ref[...] = ((h1 * r2) * g2_ref[...]).astype(BF)
        h2 = h1
        for c in range(n_ff):
            f = _nn(hn2_ref[...], w1_ref[c])
            f_ref[:, c * fc:(c + 1) * fc] = f
            a = jnp.square(jnp.maximum(f, 0.0)).astype(BF)
            a_ref[:, c * fc:(c + 1) * fc] = a
            h2 = h2 + _nn(a, w2_ref[c * fc:(c + 1) * fc, :])
        h2b = h2.astype(BF)
        gate = jax.nn.sigmoid(_nn(h2b, wg_ref[...]))
        pb = p_ref[...].astype(BF)
        pp = _nn(pb, wp_ref[...])
        h3 = h2 + gate * pp
        rf = _rms_scale(h3)
        gf_v = gf_ref[...]
        err = (h3 * rf) * gf_v - t_ref[...]
        small_ref[2:3, :] += jnp.sum(jnp.sum(err * err, axis=-1, keepdims=True), axis=0, keepdims=True) * (0.5 / D_MODEL)

        dy = err * (1.0 / D_MODEL)
        small_ref[1:2, :] += jnp.sum(dy * (h3 * rf), axis=0, keepdims=True)
        dh3 = _rms_bwd(dy * gf_v, h3, rf)
        gw_proj = _tn(pb, (dh3 * gate).astype(BF))
        for q in range(N_DEV):
            gwp_ref[q] += gw_proj[:, q * pc:(q + 1) * pc]
        dgl = ((dh3 * pp) * (gate * (1.0 - gate))).astype(BF)
        gwg_ref[...] += _tn(h2b, dgl)
        dh2 = dh3 + _nt(dgl, wg_ref[...])
        dmix_ref[...] = dh2
        dh2_ref[...] = dh2.astype(BF)
        dhn2 = jnp.zeros((tm, D_MODEL), F32)
        for c in range(n_ff):
            da = _nt(dh2_ref[...], w2_ref[c * fc:(c + 1) * fc, :])
            df = (da * (2.0 * jnp.maximum(f_ref[:, c * fc:(c + 1) * fc], 0.0))).astype(BF)
            df_ref[:, c * fc:(c + 1) * fc] = df
            dhn2 = dhn2 + _nt(df, w1_ref[c])
        h1 = dh1_ref[...]
        small_ref[0:1, :] += jnp.sum(dhn2 * (h1 * r2), axis=0, keepdims=True)
        dh1 = dmix_ref[...] + _rms_bwd(dhn2 * g2_ref[...], h1, r2)
        dh1_ref[...] = dh1
        dh1b = dh1.astype(BF)
        gwo_ref[...] += _tn(mix_ref[...], dh1b)
        dmix_ref[...] = _nt(dh1b, wo_ref[...])

    tile = lambda cols: pl.BlockSpec((tm, cols), lambda i: (i, 0))
    whole = lambda shape: pl.BlockSpec(shape, lambda i: (0,) * len(shape), pipeline_mode=pl.Buffered(1))
    total = lambda shape: pl.BlockSpec(shape, lambda i: (0,) * len(shape))
    row = pl.BlockSpec((1, D_MODEL), lambda i: (0, 0))
    act = lambda cols, dt: jax.ShapeDtypeStruct((s, cols), dt)
    gw_shapes = [(D_MODEL, D_MODEL), (N_DEV, PLE_DIM, pc), (D_MODEL, D_MODEL)]
    return _call(
        body,
        name="tail",
        grid=(s // tm,),
        in_specs=[tile(D_MODEL), tile(D_MODEL), tile(PLE_DIM), tile(D_MODEL), row, row,
                  whole(w_out.shape), whole(w_ff1.shape), whole(w_ff2.shape), whole(w_gate.shape), whole(w_proj.shape)],
        out_specs=[total((8, D_MODEL)), tile(D_MODEL), tile(D_MODEL), tile(D_MODEL), tile(D_FF), tile(D_FF), tile(D_MODEL),
                   *[total(shape) for shape in gw_shapes]],
        out_shape=[jax.ShapeDtypeStruct((8, D_MODEL), F32),
                   act(D_MODEL, F32), act(D_MODEL, F32), act(D_MODEL, BF), act(D_FF, BF), act(D_FF, BF), act(D_MODEL, BF),
                   *[jax.ShapeDtypeStruct(shape, F32) for shape in gw_shapes]],
        scratch_shapes=[pltpu.VMEM((tm, D_FF), F32)],
        compiler_params=_params(("arbitrary",)),
    )(x, mix, p, t, g2, gf, w_out, w_ff1, w_ff2, w_gate, w_proj)


def _mixer_bwd(zuv, qkv, dmix, saved, gv, w_sp, b_sp, bucket, after=None):
    s = zuv.shape[0]
    nb = s // BLOCK

    def body(zuv_ref, q_ref, kvc_ref, kvp_ref, gv_ref, w_ref, b_ref, bucket_ref, dmix_ref, tanh_ref, prob_ref, psink_ref,
             dzm_ref, dkv_ref, dgv_ref, dw_ref, db_ref, dattn_ref,
             dbias_ref, carry_ref, dsink_acc, db_acc):
        i = pl.program_id(0)

        @pl.when(i == 0)
        def _():
            dbias_ref[...] = jnp.zeros_like(dbias_ref)
            carry_ref[...] = jnp.zeros_like(carry_ref)
            dsink_acc[...] = jnp.zeros_like(dsink_acc)
            dgv_ref[...] = jnp.zeros_like(dgv_ref)
            dw_ref[...] = jnp.zeros_like(dw_ref)
            db_acc[...] = jnp.zeros_like(db_acc)

        @pl.when(i < nb)
        def _():
            u, du_dz = _gelu_and_grad(zuv_ref[:, :D_GMLP], tanh_ref[:, :D_GMLP])
            vg, dvg_dz = _gelu_and_grad(zuv_ref[:, D_GMLP:], tanh_ref[:, D_GMLP:])
            for g in range(GROUPS):
                sl = slice(g * GDIM, (g + 1) * GDIM)
                vg_g = vg[:, sl]
                rg = _rms_scale(vg_g)
                vhat = vg_g * rg
                gain = gv_ref[:, sl]
                vn = (vhat * gain).astype(BF)
                w_g = _tril_bf16(w_ref, g)
                sv = _nn(w_g, vn) + b_ref[g]
                dmix_a = dmix_ref[:, sl]
                dsv = dmix_a * u[:, sl]
                dsvb = dsv.astype(BF)
                db_acc[g] += jnp.sum(dsv, axis=-1, keepdims=True)
                dw_ref[g] += _nt(dsvb, vn)
                dvn = _tn(w_g, dsvb)
                dgv_ref[:, sl] += jnp.sum(dvn * vhat, axis=0, keepdims=True)
                dvg = _rms_bwd(dvn * gain, vg_g, rg)
                dzm_ref[:, sl] = ((dmix_a * sv) * du_dz[:, sl]).astype(BF)
                dzm_ref[:, D_GMLP + g * GDIM: D_GMLP + (g + 1) * GDIM] = (dvg * dvg_dz[:, sl]).astype(BF)

            for kh in range(N_KV):
                ksl = slice(kh * HEAD_DIM, (kh + 1) * HEAD_DIM)
                vsl = slice(D_KV + kh * HEAD_DIM, D_KV + (kh + 1) * HEAD_DIM)
                k_prev, k_cur = kvp_ref[:, ksl], kvc_ref[:, ksl]
                v_prev, v_cur = kvp_ref[:, vsl], kvc_ref[:, vsl]
                q4 = _stack_heads(q_ref, kh, 0)
                p_prev, p_cur, p_sink = prob_ref[0, kh, :, :BLOCK], prob_ref[0, kh, :, BLOCK:], psink_ref[0, kh, :, 0:1]
                do4 = _stack_heads(dmix_ref, kh, D_GMLP)
                dp_prev = _nt(do4, v_prev)
                dp_cur = _nt(do4, v_cur)
                delta = jnp.sum(p_prev * dp_prev, axis=-1, keepdims=True) + jnp.sum(p_cur * dp_cur, axis=-1, keepdims=True)
                ds_prev = p_prev * (dp_prev - delta)
                ds_cur = p_cur * (dp_cur - delta)
                dsink_acc[kh] -= p_sink * delta
                dbias_ref[kh, :, :BLOCK] += ds_prev
                dbias_ref[kh, :, BLOCK:] += ds_cur
                dsb_prev = ds_prev.astype(BF)
                dsb_cur = ds_cur.astype(BF)
                dq4 = (_nn(dsb_prev, k_prev) + _nn(dsb_cur, k_cur)) * SCALE
                for g in range(Q_PER_KV):
                    first = 2 * D_GMLP + (kh * Q_PER_KV + g) * HEAD_DIM
                    dzm_ref[:, first:first + HEAD_DIM] = dq4[g * BLOCK:(g + 1) * BLOCK].astype(BF)
                dkv_ref[:, ksl] = (carry_ref[:, ksl] + _tn(dsb_prev, q4) * SCALE).astype(BF)
                dkv_ref[:, vsl] = (carry_ref[:, vsl] + _tn(p_prev.astype(BF), do4)).astype(BF)
                carry_ref[:, ksl] = _tn(dsb_cur, q4) * SCALE
                carry_ref[:, vsl] = _tn(p_cur.astype(BF), do4)

        @pl.when(i == nb)
        def _():
            dkv_ref[...] = carry_ref[...].astype(BF)
            row = lax.broadcasted_iota(jnp.int32, (BLOCK, BLOCK), 0)
            col = lax.broadcasted_iota(jnp.int32, (BLOCK, BLOCK), 1)
            for g in range(GROUPS):
                dw_ref[g] = jnp.where(col <= row, dw_ref[g], 0.0)
                db_ref[g:g + 1, :] = jnp.sum(jnp.where(col == row, db_acc[g], 0.0), axis=0, keepdims=True)
            bucket = bucket_ref[...]
            for b in range(N_Q, REL_BUCKETS):
                dattn_ref[N_Q, b] = 0.0
            for h in range(N_Q):
                kh, rows = _head_rows(h)
                dattn_ref[N_Q, h] = jnp.sum(dsink_acc[kh, rows, :])
                dbh = dbias_ref[kh, rows, :]
                for b in range(REL_BUCKETS):
                    dattn_ref[h, b] = jnp.sum(jnp.where(bucket == b, dbh, 0.0))

    cl = lambda i: jnp.minimum(i, nb - 1)
    const = lambda shape: pl.BlockSpec(shape, lambda i: (0,) * len(shape))
    return _call(
        body,
        name="mixer_bwd",
        after=after,
        grid=(nb + 1,),
        in_specs=_mixer_specs(nb)[:7] + [
            const((BLOCK, 2 * BLOCK)),
            pl.BlockSpec((BLOCK, D_MODEL), lambda i: (cl(i), 0)),
            pl.BlockSpec((BLOCK, 2 * D_GMLP), lambda i: (cl(i), 0)),
            pl.BlockSpec((1, N_KV, ROWS4, 2 * BLOCK), lambda i: (cl(i), 0, 0, 0)),
            pl.BlockSpec((1, N_KV, ROWS4, LANES), lambda i: (cl(i), 0, 0, 0)),
        ],
        out_specs=[
            pl.BlockSpec((BLOCK, D_MAIN), lambda i: (cl(i), 0)),
            pl.BlockSpec((BLOCK, 2 * D_KV), lambda i: (jnp.maximum(i - 1, 0), 0)),
            const((1, D_GMLP)),
            const((GROUPS, BLOCK, BLOCK)),
            const((GROUPS, BLOCK)),
            pl.BlockSpec(memory_space=pltpu.SMEM),
        ],
        out_shape=[
            jax.ShapeDtypeStruct((s, D_MAIN), BF),
            jax.ShapeDtypeStruct((s, 2 * D_KV), BF),
            jax.ShapeDtypeStruct((1, D_GMLP), F32),
            jax.ShapeDtypeStruct((GROUPS, BLOCK, BLOCK), F32),
            jax.ShapeDtypeStruct((GROUPS, BLOCK), F32),
            jax.ShapeDtypeStruct((N_Q + 1, REL_BUCKETS), F32),
        ],
        scratch_shapes=[
            pltpu.VMEM((N_KV, ROWS4, 2 * BLOCK), F32),
            pltpu.VMEM((BLOCK, 2 * D_KV), F32),
            pltpu.VMEM((N_KV, ROWS4, 1), F32),
            pltpu.VMEM((GROUPS, BLOCK, 1), F32),
        ],
        compiler_params=_params(("arbitrary",)),
    )(zuv, qkv, qkv, qkv, gv, w_sp, b_sp, bucket, dmix, *saved)


def _in_bwd(x, dh1, dzm, dkv, g1, w_in_t, after=None):
    s = x.shape[0]
    tm = min(IO_TOK_TILE, s)

    def body(x_ref, dh1_ref, dzm_ref, dkv_ref, g_ref, w_ref, dx_ref, dg_ref):
        @pl.when(pl.program_id(0) == 0)
        def _():
            dg_ref[...] = jnp.zeros_like(dg_ref)

        dhn = _nn(dzm_ref[...], w_ref[:D_MAIN, :]) + _nn(dkv_ref[...], w_ref[D_MAIN:, :])
        xv = x_ref[...]
        r = _rms_scale(xv)
        dg_ref[...] += jnp.sum(dhn * (xv * r), axis=0, keepdims=True)
        dx_ref[...] = dh1_ref[...] + _rms_bwd(dhn * g_ref[...], xv, r)

    tile = lambda cols: pl.BlockSpec((tm, cols), lambda i: (i, 0))
    row = pl.BlockSpec((1, D_MODEL), lambda i: (0, 0))
    return _call(
        body,
        name="in_bwd",
        after=after,
        grid=(s // tm,),
        in_specs=[tile(D_MODEL), tile(D_MODEL), tile(D_MAIN), tile(2 * D_KV), row, pl.BlockSpec((D_IN, D_MODEL), lambda i: (0, 0))],
        out_specs=[tile(D_MODEL), row],
        out_shape=[jax.ShapeDtypeStruct((s, D_MODEL), F32), jax.ShapeDtypeStruct((1, D_MODEL), F32)],
        compiler_params=_params(("arbitrary",)),
    )(x, dh1, dzm, dkv, g1, w_in_t)


def _wgrad_in(dzm, dkv, hn):
    s = hn.shape[0]
    tm = 2 * D_KV
    n_main = D_MAIN // tm

    def body(dzm_ref, dkv_ref, hn_ref, o_ref):
        i = pl.program_id(0)

        @pl.when(i < n_main)
        def _():
            o_ref[...] = _tn(dzm_ref[...], hn_ref[...])

        @pl.when(i == n_main)
        def _():
            o_ref[...] = _tn(dkv_ref[...], hn_ref[...])

    return _call(
        body,
        name="wgrad_in",
        grid=(n_main + 1,),
        in_specs=[
            pl.BlockSpec((s, tm), lambda i: (0, jnp.minimum(i, n_main - 1))),
            pl.BlockSpec((s, tm), lambda i: (0, 0)),
            pl.BlockSpec((s, D_MODEL), lambda i: (0, 0)),
        ],
        out_specs=pl.BlockSpec((tm, D_MODEL), lambda i: (i, 0)),
        out_shape=jax.ShapeDtypeStruct((D_IN, D_MODEL), F32),
        compiler_params=_params(("arbitrary",)),
    )(dzm, dkv, hn)


def _wgrad(a, b, tm, tn, name, peer_cols=0, after=None):
    s, m = a.shape
    n = b.shape[1]

    def body(a_ref, b_ref, o_ref, at_ref):
        @pl.when(pl.program_id(1) == 0)
        def _():
            at_ref[...] = a_ref[...].astype(BF).T

        r = _nn(at_ref[...], b_ref[...])
        if peer_cols:
            for q in range(tn // peer_cols):
                o_ref[q] = r[:, q * peer_cols:(q + 1) * peer_cols]
        else:
            o_ref[...] = r

    if peer_cols:
        out_spec = pl.BlockSpec((tn // peer_cols, tm, peer_cols), lambda i, j: (j, i, 0))
        out_shape = jax.ShapeDtypeStruct((n // peer_cols, m, peer_cols), F32)
    else:
        out_spec = pl.BlockSpec((tm, tn), lambda i, j: (i, j))
        out_shape = jax.ShapeDtypeStruct((m, n), F32)
    return _call(
        body,
        name=name,
        after=after,
        grid=(m // tm, n // tn),
        in_specs=[pl.BlockSpec((s, tm), lambda i, j: (0, i)), pl.BlockSpec((s, tn), lambda i, j: (0, j))],
        out_specs=out_spec,
        out_shape=out_shape,
        scratch_shapes=[pltpu.VMEM((tm, s), BF)],
        compiler_params=_params(("arbitrary", "arbitrary")),
    )(a, b)


def _adamw_math(w, g, m, v):
    m_new = ADAM_B1 * m + (1.0 - ADAM_B1) * g
    v_new = ADAM_B2 * v + (1.0 - ADAM_B2) * jnp.square(g)
    m_hat = m_new / (1.0 - ADAM_B1 ** ADAM_STEP)
    v_hat = v_new / (1.0 - ADAM_B2 ** ADAM_STEP)
    delta = -ADAM_LR * (m_hat / (jnp.sqrt(v_hat) + ADAM_EPS) + ADAM_WD * w)
    return delta, m_new, v_new


def _final_adamw(part, recv, w, m, v, name):
    r, c = w.shape
    tr = min(r, 512)

    def body(p_ref, r_ref, w_ref, m_ref, v_ref, g_ref, d_ref, mo_ref, vo_ref):
        g = p_ref[...]
        for j in range(3):
            g = g + r_ref[j].astype(F32)
        g_ref[...] = g
        d_ref[...], mo_ref[...], vo_ref[...] = _adamw_math(w_ref[...], g, m_ref[...], v_ref[...])

    spec = pl.BlockSpec((tr, c), lambda i: (i, 0))
    return _call(
        body,
        name=name,
        grid=(r // tr,),
        in_specs=[spec, pl.BlockSpec((3, tr, c), lambda i: (0, i, 0)), spec, spec, spec],
        out_specs=[spec] * 4,
        out_shape=[jax.ShapeDtypeStruct((r, c), F32)] * 4,
        compiler_params=_params(("arbitrary",)),
    )(part, recv, w, m, v)


def _rs_sum(g, land, blocks, name):
    _, r, c = g.shape
    tr = min(r, 256)

    def body(blk_ref, g0_ref, g1_ref, g2_ref, g3_ref, l_ref, part_ref, send_ref):
        part_ref[...] = g0_ref[0] + l_ref[0]
        for j, gj_ref in enumerate((g1_ref, g2_ref, g3_ref)):
            send_ref[j] = (gj_ref[0] + l_ref[j + 1]).astype(BF)

    def pick(j):
        return pl.BlockSpec((1, tr, c), lambda i, blk: (blk[j], i, 0))

    return _call(
        body,
        name=name,
        grid_spec=pltpu.PrefetchScalarGridSpec(
            num_scalar_prefetch=1,
            grid=(r // tr,),
            in_specs=[pick(0), pick(1), pick(2), pick(3), pl.BlockSpec((4, tr, c), lambda i, blk: (0, i, 0))],
            out_specs=[pl.BlockSpec((tr, c), lambda i, blk: (i, 0)), pl.BlockSpec((3, tr, c), lambda i, blk: (0, i, 0))],
        ),
        out_shape=[jax.ShapeDtypeStruct((r, c), F32), jax.ShapeDtypeStruct((3, r, c), BF)],
        compiler_params=_params(("arbitrary",)),
    )(blocks, g, g, g, g, land)


def _final_adamw_few(parts, recvs, ws, ms, vs, name):
    n = len(ws)

    def body(*refs):
        ins, outs = refs[:5 * n], refs[5 * n:]
        for k in range(n):
            p_ref, r_ref, w_ref, m_ref, v_ref = (ins[q * n + k] for q in range(5))
            g = p_ref[...]
            for j in range(3):
                g = g + r_ref[j].astype(F32)
            outs[k][...] = g
            outs[n + k][...], outs[2 * n + k][...], outs[3 * n + k][...] = _adamw_math(w_ref[...], g, m_ref[...], v_ref[...])

    outs = _call(
        body,
        name=name,
        out_shape=[jax.ShapeDtypeStruct(w.shape, F32) for w in ws] * 4,
        compiler_params=_params(),
    )(*parts, *recvs, *ws, *ms, *vs)
    return outs[:n], outs[n:2 * n], outs[2 * n:3 * n], outs[3 * n:]


def _rs_sum_few(gs, lands, blocks, name):
    n = len(gs)

    def body(blk_ref, *refs):
        g_refs, l_refs, part_refs, send_refs = refs[:n], refs[n:2 * n], refs[2 * n:3 * n], refs[3 * n:]
        for k in range(n):
            part_refs[k][...] = g_refs[k][blk_ref[0]] + l_refs[k][0]
            for j in range(3):
                send_refs[k][j] = (g_refs[k][blk_ref[j + 1]] + l_refs[k][j + 1]).astype(BF)

    outs = _call(
        body,
        name=name,
        in_specs=[pl.BlockSpec(memory_space=pltpu.SMEM)] + [pl.BlockSpec(memory_space=pltpu.VMEM)] * (2 * n),
        out_shape=[jax.ShapeDtypeStruct(g.shape[1:], F32) for g in gs] + [jax.ShapeDtypeStruct((3,) + g.shape[1:], BF) for g in gs],
        compiler_params=_params(),
    )(blocks, *gs, *lands)
    return [(outs[k], outs[n + k]) for k in range(n)]


def _adamw_small(own, recv, g1_own, g1_recv, weights, moms, vels, after):
    n_w = len(weights)

    def body(*refs):
        own_refs, recv_refs, g1_own_ref, g1_recv_ref = refs[:5], refs[5:10], refs[10], refs[11]
        w_refs, m_refs, v_refs = refs[12:12 + n_w], refs[12 + n_w:12 + 2 * n_w], refs[12 + 2 * n_w:12 + 3 * n_w]
        outs = refs[12 + 3 * n_w:]
        loss_ref, g_refs, d_refs = outs[0], outs[1:1 + n_w], outs[1 + n_w:1 + 2 * n_w]
        mo_refs, vo_refs = outs[1 + 2 * n_w:1 + 3 * n_w], outs[1 + 3 * n_w:]
        x, y, c = lax.axis_index("x"), lax.axis_index("y"), lax.axis_index("c")

        def in_place_order(values, my_place):
            acc = None
            for place in range(len(values)):
                r = place ^ my_place
                term = values[-1]
                for q in range(len(values) - 2, -1, -1):
                    term = jnp.where(r == q, values[q], term)
                acc = term if acc is None else acc + term
            return acc

        def total(k, *index):
            index = index or (slice(None),) * (len(own_refs[k].shape) - 1)
            across = [own_refs[k][(0, *index)], recv_refs[k][(1, *index)], recv_refs[k][(0, *index)], recv_refs[k][(2, *index)]]
            return in_place_order([v.astype(F32) for v in across], 2 * x + y)

        g1 = in_place_order([g1_own_ref[0]] + [g1_recv_ref[j] for j in range(N_DEV - 1)], 4 * x + 2 * y + c)
        grads = [
            g1, total(1), total(2), total(3),
            total(4, slice(N_Q, None), slice(0, N_Q)), total(4, slice(0, N_Q), slice(None)),
            total(0, slice(0, 1), slice(None)), total(0, slice(1, 2), slice(None))]
        loss_ref[...] = total(0, slice(2, 3), slice(0, 1))
        for k in range(n_w):
            g_refs[k][...] = grads[k]
            d_refs[k][...], mo_refs[k][...], vo_refs[k][...] = _adamw_math(w_refs[k][...], grads[k], m_refs[k][...], v_refs[k][...])

    shapes = [jax.ShapeDtypeStruct(w.shape, F32) for w in weights]
    outs = _call(
        body,
        name="adamw_small",
        after=after,
        in_specs=[pl.BlockSpec(memory_space=pltpu.VMEM)] * (12 + 3 * n_w),
        out_shape=[jax.ShapeDtypeStruct((1, 1), F32)] + shapes * 4,
        compiler_params=_params(),
    )(*own, *recv, g1_own, g1_recv, *weights, *moms, *vels)
    return outs[0], outs[1:1 + n_w], outs[1 + n_w:1 + 2 * n_w], outs[1 + 2 * n_w:1 + 3 * n_w], outs[1 + 3 * n_w:]


def _place():
    x, y, c = lax.axis_index("x"), lax.axis_index("y"), lax.axis_index("c")
    return x, y, c, [(1 - x, y), (x, 1 - y), (1 - x, 1 - y)]


def _dev_index(px, py, pc):
    return 4 * px + 2 * py + pc


HBM_SPEC = pl.BlockSpec(memory_space=pltpu.HBM)
SEM_SPEC = pl.BlockSpec(memory_space=pltpu.SEMAPHORE)
ANY_SPEC = pl.BlockSpec(memory_space=pl.ANY)
DATAFLOW = pltpu.SideEffectType.DATAFLOW_SIDE_EFFECTING


def _hbm(a):
    return pltpu.with_memory_space_constraint(a, pltpu.HBM)


def _prep_weights(shards):
    k_n = len(shards)

    def body(*refs):
        ins, outs, stage, sems = refs[:k_n], refs[k_n:2 * k_n], refs[2 * k_n:3 * k_n], refs[3 * k_n]
        x, y, c, _ = _place()
        copies = []
        for k in range(k_n):
            stage[k][...] = ins[k][...].astype(BF)
            copies.append(pltpu.make_async_copy(stage[k], outs[k].at[_dev_index(x, y, c)], sems.at[k]))
            copies[k].start()
        for cp in copies:
            cp.wait()

    return _call(
        body,
        name="prep_weights",
        in_specs=[pl.BlockSpec(memory_space=pltpu.VMEM)] * k_n,
        out_specs=[ANY_SPEC] * k_n,
        out_shape=[jax.ShapeDtypeStruct((N_DEV,) + sh.shape, BF) for sh in shards],
        scratch_shapes=[pltpu.VMEM(sh.shape, BF) for sh in shards] + [pltpu.SemaphoreType.DMA((k_n,))],
        compiler_params=_params(),
    )(*shards)


def _ag_piece(land_k, block, half, peer, send_sem, recv_sem):
    ref = land_k.at[_dev_index(*block)]
    if half is not None:
        rows = land_k.shape[1] // 2
        ref = ref.at[pl.ds(half * rows, rows)]
    return pltpu.make_async_remote_copy(
        src_ref=ref, dst_ref=ref, send_sem=send_sem, recv_sem=recv_sem, device_id=peer, device_id_type=MESH)


def _ag_plan():
    x, y, c, _ = _place()
    me, sib = (x, y, c), (x, y, 1 - c)
    xn, yn, diag = (1 - x, y, c), (x, 1 - y, c), (1 - x, 1 - y, c)
    return dict(
        relay_halves=[(me, 0, xn), (me, 1, yn)],
        others=[(me, None, sib), (me, 1, xn), (me, 0, yn)],
        relays=[(xn, 0, yn), (yn, 1, xn)],
        near=[(xn, None, sib), (yn, None, sib)],
        far=[(diag, None, sib)],
    )


def _ag_stage(land, stage, send_sems, recv_sems, act):
    copies = _ag_plan()[stage]
    n = len(copies)
    for k in range(len(land)):
        for j, (block, half, peer) in enumerate(copies):
            cp = _ag_piece(land[k], block, half, peer, send_sems.at[n * k + j], recv_sems.at[n * k + j])
            if act == "start":
                cp.start()
            else:
                cp.wait_send()
                cp.wait_recv()


def _sem_shapes(*counts):
    return [pltpu.SemaphoreType.DMA((n,)) for n in counts for _ in range(2)]


def _ag_start(first, rest):
    lands = list(first) + list(rest)
    k_n, k_first = len(lands), len(first)
    k_rest = k_n - k_first

    def body(*refs):
        land = refs[:k_n]
        sems = refs[k_n:k_n + 6]
        token = refs[-1]
        x, y, c, chips = _place()
        targets = [(x, y, 1 - c)] + [(*chip, c) for chip in chips]
        for k in range(k_first):
            for j, to in enumerate(targets):
                _ag_piece(land[k], (x, y, c), None, to, sems[0].at[4 * k + j], sems[1].at[4 * k + j]).start()
        _ag_stage(land[k_first:], "relay_halves", sems[2], sems[3], "start")
        _ag_stage(land[k_first:], "others", sems[4], sems[5], "start")
        token[...] = jnp.zeros_like(token)

    outs = pl.pallas_call(
        body,
        name="ag_start",
        in_specs=[HBM_SPEC] * k_n,
        out_specs=(*[SEM_SPEC] * 6, *[HBM_SPEC] * k_n, pl.BlockSpec(memory_space=pltpu.VMEM)),
        out_shape=(*_sem_shapes(4 * k_first, 2 * k_rest, 3 * k_rest),
                   *[pltpu.HBM(a.shape, a.dtype) for a in lands], jax.ShapeDtypeStruct((8, LANES), F32)),
        input_output_aliases={k: 6 + k for k in range(k_n)},
        compiler_params=pltpu.CompilerParams(has_side_effects=DATAFLOW),
    )(*[_hbm(a) for a in lands])
    flying = list(outs[6:6 + k_n])
    return (outs[0], outs[1], flying[:k_first]), (outs[2:6], flying[k_first:]), outs[-1]


def _ag_split_call(lands, waits, starts, after, name):
    k_n = len(lands)
    plan_sizes = dict(relay_halves=2, others=3, relays=2, near=2, far=1)
    n_in, n_out = 2 * len(waits), 2 * len(starts)

    def body(*refs):
        land = refs[:k_n]
        in_sems = refs[k_n:k_n + n_in]
        out_sems, token = refs[len(refs) - 1 - n_out:len(refs) - 1], refs[-1]
        for w, (stage, _, _) in enumerate(waits):
            _ag_stage(land, stage, in_sems[2 * w], in_sems[2 * w + 1], "wait")
            if w < len(starts):
                _ag_stage(land, starts[w], out_sems[2 * w], out_sems[2 * w + 1], "start")
        token[...] = jnp.zeros_like(token)

    outs = pl.pallas_call(
        body,
        name=name,
        in_specs=[HBM_SPEC] * k_n + [SEM_SPEC] * n_in + [ANY_SPEC],
        out_specs=(*[HBM_SPEC] * k_n, *[SEM_SPEC] * n_out, pl.BlockSpec(memory_space=pltpu.VMEM)),
        out_shape=(*[pltpu.HBM(a.shape, a.dtype) for a in lands], *_sem_shapes(*[plan_sizes[s] * k_n for s in starts]),
                   jax.ShapeDtypeStruct((8, LANES), F32)),
        input_output_aliases={k: k for k in range(k_n)},
        compiler_params=pltpu.CompilerParams(has_side_effects=DATAFLOW),
    )(*lands, *[s for _, a, b in waits for s in (a, b)], after)
    return list(outs[:k_n]), list(outs[k_n:k_n + n_out]), outs[-1]


def _ag_mid(lands, send_sems, recv_sems, after, name):
    k_n = len(lands)

    def body(*refs):
        land = refs[:k_n]
        send1, recv1 = refs[k_n], refs[k_n + 1]
        fwd_send, fwd_recv = refs[-2], refs[-1]
        x, y, c, chips = _place()
        sources = [(x, y, 1 - c)] + [(*chip, c) for chip in chips]
        for k in range(k_n):
            mine = land[k].at[_dev_index(x, y, c)]
            for j, frm in enumerate(sources):
                got = land[k].at[_dev_index(*frm)]
                cp = pltpu.make_async_remote_copy(
                    src_ref=mine, dst_ref=got, send_sem=send1.at[4 * k + j], recv_sem=recv1.at[4 * k + j], device_id=frm, device_id_type=MESH)
                cp.wait_send()
                cp.wait_recv()
                if j >= 1:
                    pltpu.make_async_remote_copy(
                        src_ref=got, dst_ref=got, send_sem=fwd_send.at[3 * k + j - 1], recv_sem=fwd_recv.at[3 * k + j - 1],
                        device_id=(x, y, 1 - c), device_id_type=MESH).start()

    outs = pl.pallas_call(
        body,
        name=name,
        in_specs=[HBM_SPEC] * k_n + [SEM_SPEC, SEM_SPEC, ANY_SPEC],
        out_specs=(*[HBM_SPEC] * k_n, SEM_SPEC, SEM_SPEC),
        out_shape=(*[pltpu.HBM(a.shape, a.dtype) for a in lands], pltpu.SemaphoreType.DMA((3 * k_n,)), pltpu.SemaphoreType.DMA((3 * k_n,))),
        input_output_aliases={k: k for k in range(k_n)},
        compiler_params=pltpu.CompilerParams(has_side_effects=DATAFLOW),
    )(*lands, send_sems, recv_sems, after)
    return list(outs[:k_n]), outs[-2], outs[-1]


def _ag_end(lands, fwd_send, fwd_recv, name):
    k_n = len(lands)

    def body(*refs):
        land = refs[:k_n]
        fsend, frecv = refs[k_n], refs[k_n + 1]
        x, y, c, chips = _place()
        for k in range(k_n):
            for j, chip in enumerate(chips):
                cp = pltpu.make_async_remote_copy(
                    src_ref=land[k].at[_dev_index(*chip, c)], dst_ref=land[k].at[_dev_index(*chip, 1 - c)],
                    send_sem=fsend.at[3 * k + j], recv_sem=frecv.at[3 * k + j], device_id=(x, y, 1 - c), device_id_type=MESH)
                cp.wait_send()
                cp.wait_recv()

    outs = pl.pallas_call(
        body,
        name=name,
        in_specs=[HBM_SPEC] * k_n + [SEM_SPEC, SEM_SPEC],
        out_specs=tuple([HBM_SPEC] * k_n),
        out_shape=tuple(pltpu.HBM(a.shape, a.dtype) for a in lands),
        input_output_aliases={k: k for k in range(k_n)},
        compiler_params=pltpu.CompilerParams(has_side_effects=DATAFLOW),
    )(*lands, fwd_send, fwd_recv)
    return list(outs)


def _chips4():
    x, y, c, others = _place()
    return x, y, c, [(x, y)] + others


def _route_sibling(j):
    x, y, c, chips = _chips4()
    return _dev_index(*chips[j], 1 - c), j, (x, y, 1 - c)


def _route_chips(j):
    x, y, c, chips = _chips4()
    return j, j, (*chips[j + 1], c)


def _route_sibling_whole(j):
    x, y, c, _ = _chips4()
    return 0, 0, (x, y, 1 - c)


def _route_chips_whole(j):
    x, y, c, chips = _chips4()
    return 0, j, (*chips[j + 1], c)


def _route_everyone(j):
    x, y, c, _ = _chips4()
    flip = [(j + 1) >> 2 & 1, (j + 1) >> 1 & 1, (j + 1) & 1]
    return 0, j, tuple(1 - v if f else v for v, f in zip((x, y, c), flip))


def _xchg_copies(routes, src, dst, send_sems, recv_sems):
    copies, sem = [], 0
    for k, (route, n) in enumerate(routes):
        for j in range(n):
            si, di, peer = route(j)
            copies.append(pltpu.make_async_remote_copy(
                src_ref=src[k].at[si], dst_ref=dst[k].at[di], send_sem=send_sems.at[sem], recv_sem=recv_sems.at[sem],
                device_id=peer, device_id_type=MESH))
            sem += 1
    return copies


def _xchg_start(srcs, slot_shapes, routes, name):
    k_n = len(srcs)
    n_sem = sum(n for _, n in routes)
    dsts = [lax.empty((n,) + tuple(sh), a.dtype) for sh, a, (_, n) in zip(slot_shapes, srcs, routes)]

    def body(*refs):
        src, dst = refs[:k_n], refs[k_n:2 * k_n]
        send_sems, recv_sems, token = refs[2 * k_n], refs[2 * k_n + 1], refs[-1]
        for cp in _xchg_copies(routes, src, dst, send_sems, recv_sems):
            cp.start()
        token[...] = jnp.zeros_like(token)

    arrays = list(srcs) + dsts
    outs = pl.pallas_call(
        body,
        name=name,
        in_specs=[HBM_SPEC] * (2 * k_n),
        out_specs=(SEM_SPEC, SEM_SPEC, *[HBM_SPEC] * (2 * k_n), pl.BlockSpec(memory_space=pltpu.VMEM)),
        out_shape=(pltpu.SemaphoreType.DMA((n_sem,)), pltpu.SemaphoreType.DMA((n_sem,)),
                   *[pltpu.HBM(a.shape, a.dtype) for a in arrays], jax.ShapeDtypeStruct((8, LANES), F32)),
        input_output_aliases={i: 2 + i for i in range(2 * k_n)},
        compiler_params=pltpu.CompilerParams(has_side_effects=DATAFLOW),
    )(*[_hbm(a) for a in arrays])
    return outs[0], outs[1], list(outs[2:2 + k_n]), list(outs[2 + k_n:2 + 2 * k_n]), outs[-1]


def _xchg_wait(send_sems, recv_sems, srcs, dsts, routes, after, name):
    k_n = len(srcs)

    def body(*refs):
        src, dst = refs[:k_n], refs[k_n:2 * k_n]
        for cp in _xchg_copies(routes, src, dst, refs[2 * k_n], refs[2 * k_n + 1]):
            cp.wait_send()
            cp.wait_recv()

    arrays = list(srcs) + list(dsts)
    outs = pl.pallas_call(
        body,
        name=name,
        in_specs=[HBM_SPEC] * (2 * k_n) + [SEM_SPEC, SEM_SPEC, ANY_SPEC],
        out_specs=tuple([HBM_SPEC] * (2 * k_n)),
        out_shape=tuple(pltpu.HBM(a.shape, a.dtype) for a in arrays),
        input_output_aliases={i: i for i in range(2 * k_n)},
        compiler_params=pltpu.CompilerParams(has_side_effects=DATAFLOW),
    )(*arrays, send_sems, recv_sems, after)
    return list(outs[:k_n]), list(outs[k_n:])


SMALL = ("norm1_gain", "gmlp_v_gain", "w_spatial", "b_spatial", "attn_sinks", "rel_bias_table", "norm2_gain", "final_gain")
LANES = 128


def _swap_start(grads, smalls, tag):
    srcs = list(grads) + [a[None] for a in smalls]
    shapes = [g.shape[1:] for g in grads] + [a.shape for a in smalls]
    routes = [(_route_sibling, 4)] * len(grads) + [(_route_sibling_whole, 1)] * len(smalls)
    return _xchg_start(srcs, shapes, routes, f"rs_{tag}_swap_start"), routes, len(grads)


def _swap_sums(swap, names, after, tag, few=False, small_dtypes=None):
    (send1, recv1, src1, land1, _), routes, n_rs = swap
    x, y, c, chips = _chips4()
    blocks = jnp.stack([_dev_index(*chip, c) for chip in chips]).astype(jnp.int32)
    src1, land1 = _xchg_wait(send1, recv1, src1, land1, routes, after, f"rs_{tag}_swap_wait")
    if few:
        sums = _rs_sum_few(src1[:n_rs], land1[:n_rs], blocks, f"rs_{tag}_sum")
    else:
        sums = [_rs_sum(g, land, blocks, f"rs_sum_{n}") for g, land, n in zip(src1[:n_rs], land1[:n_rs], names)]
    if len(src1) == n_rs:
        return sums, []
    return sums, _pair_sum(src1[n_rs:], land1[n_rs:], small_dtypes or [F32] * (len(src1) - n_rs), f"rs_{tag}_sum_small")


def _pair_sum(mine, theirs, dtypes, name):
    def body(*refs):
        n = len(refs) // 3
        for k in range(n):
            refs[2 * n + k][...] = (refs[k][...] + refs[n + k][...]).astype(dtypes[k])

    return _call(
        body,
        name=name,
        out_shape=[jax.ShapeDtypeStruct(a.shape, dt) for a, dt in zip(mine, dtypes)],
        compiler_params=_params(),
    )(*mine, *theirs)


def _chips_start(sums, small_sums, tag, everyone=()):
    sends = [ps[1] for ps in sums] + list(small_sums) + [a[None] for a in everyone]
    routes = [(_route_chips, 3)] * len(sums) + [(_route_chips_whole, 3)] * len(small_sums) + [(_route_everyone, 7)] * len(everyone)
    return _xchg_start(sends, [a.shape[1:] for a in sends], routes, f"rs_{tag}_chips_start"), routes


def _chips_wait(chips, after, tag):
    (send2, recv2, src2, land2, _), routes = chips
    return _xchg_wait(send2, recv2, src2, land2, routes, after, f"rs_{tag}_chips_wait")


def kernel(x, p, norm1_gain, w_in, gmlp_v_gain, w_spatial, b_spatial, attn_sinks, rel_bias_table, w_out, norm2_gain, w_ff1, w_ff2, w_ple_proj, w_ple_gate, final_gain, loss_target, m_norm1_gain, m_w_in, m_gmlp_v_gain, m_w_spatial, m_b_spatial, m_attn_sinks, m_rel_bias_table, m_w_out, m_norm2_gain, m_w_ff1, m_w_ff2, m_w_ple_proj, m_w_ple_gate, m_final_gain, v_norm1_gain, v_w_in, v_gmlp_v_gain, v_w_spatial, v_b_spatial, v_attn_sinks, v_rel_bias_table, v_w_out, v_norm2_gain, v_w_ff1, v_w_ff2, v_w_ple_proj, v_w_ple_gate, v_final_gain):
    args = dict(locals())
    s = x.shape[1]
    big = ("w_in", "w_out", "w_ff1", "w_ff2", "w_ple_proj", "w_ple_gate")

    x2, p2, t2 = x.reshape(s, D_MODEL), p.reshape(s, PLE_DIM), loss_target.reshape(s, D_MODEL)
    g1, gv, w_sp, b_sp, sinks, table, g2, gf = (args[n] for n in SMALL)
    bucket = jnp.asarray(_bucket_table())
    b_col = b_sp.reshape(GROUPS, BLOCK, 1)

    def shard(name):
        return args[name][0].T if name.endswith("w_in") else args[name][0]

    lands = _prep_weights([shard(n) for n in big])
    (send_in, recv_in, fly_in), (rest_sems, fly_rest), token = _ag_start(lands[:1], lands[1:])
    mid_in, fwd_send_in, fwd_recv_in = _ag_mid(fly_in, send_in, recv_in, token, "ag_mid_w_in")
    g_in = _ag_end(mid_in, fwd_send_in, fwd_recv_in, "ag_end_w_in")[0]
    full_in = g_in.reshape(D_IN, D_MODEL)

    zuv, qkv, hn1 = _in_proj(x2, g1, full_in)
    fly_rest, relay_sems, relayed = _ag_split_call(
        fly_rest, [("relay_halves", *rest_sems[:2])], ["relays"], zuv, "ag_relay")
    mix, *saved = _mixer_fwd(zuv, qkv, gv, w_sp[0], b_col, sinks, table.T, bucket, after=relayed)
    fly_rest, fwd_sems, _ = _ag_split_call(
        fly_rest, [("others", *rest_sems[2:]), ("relays", *relay_sems)], ["near", "far"], mix, "ag_mid_rest")
    g_out, g_ff1, g_ff2, g_proj, g_gate = _ag_split_call(
        fly_rest, [("near", *fwd_sems[:2]), ("far", *fwd_sems[2:])], [], mix, "ag_end_rest")[0]
    full_out, full_ff2, full_gate = g_out.reshape(D_MODEL, D_MODEL), g_ff2.reshape(D_FF, D_MODEL), g_gate.reshape(D_MODEL, D_MODEL)
    full_proj = g_proj.transpose(1, 0, 2).reshape(PLE_DIM, D_MODEL)

    tail_small, dh1, dmix, hn2, a, df, dh2, gw_out, gw_proj, gw_gate = _tail(
        x2, mix, p2, t2, g2, gf.reshape(1, D_MODEL), full_out, g_ff1, full_ff2, full_gate, full_proj)
    gw_out, gw_gate = (g.reshape(N_DEV, D_MODEL // N_DEV, D_MODEL) for g in (gw_out, gw_gate))

    mid_names = ("w_out", "w_ple_proj", "w_ple_gate")
    mid_swap = _swap_start([gw_out, gw_proj, gw_gate], [tail_small], "mid")
    gw_ff1 = _wgrad(hn2, df, D_MODEL, 2 * D_FF // N_DEV, "wgrad_ff1", peer_cols=D_FF // N_DEV, after=mid_swap[0][4])
    ff1_swap = _swap_start([gw_ff1], [], "ff1")
    gw_ff2 = _wgrad(a, dh2, 1024, D_MODEL, "wgrad_ff2", after=ff1_swap[0][4]).reshape(N_DEV, D_FF // N_DEV, D_MODEL)
    ff2_swap = _swap_start([gw_ff2], [], "ff2")
    ff1_sums, _ = _swap_sums(ff1_swap, ("w_ff1",), ff2_swap[0][4], "ff1")
    ff1_chips = _chips_start(ff1_sums, [], "ff1")
    ff2_sums, _ = _swap_sums(ff2_swap, ("w_ff2",), ff1_chips[0][4], "ff2")
    ff2_chips = _chips_start(ff2_sums, [], "ff2")
    mid_sums, mid_small = _swap_sums(mid_swap, mid_names, ff2_chips[0][4], "mid", few=True)
    mid_chips = _chips_start(mid_sums, mid_small, "mid")

    dzm, dkv, d_gv, d_wsp, d_bsp, d_attn = _mixer_bwd(
        zuv, qkv, dmix, saved, gv, w_sp[0], b_col, bucket, after=mid_chips[0][4])
    gw_in = _wgrad_in(dzm, dkv, hn1).reshape(N_DEV, D_IN // N_DEV, D_MODEL)
    in_swap = _swap_start([gw_in], [d_gv, d_wsp, d_bsp, d_attn], "in")
    dx, d_g1 = _in_bwd(x2, dh1, dzm, dkv, g1, full_in, after=in_swap[0][4])
    grad_x = dx.reshape(x.shape)
    in_sums, in_small = _swap_sums(in_swap, ("w_in",), dx, "in", small_dtypes=[F32, BF, F32, F32])
    in_chips = _chips_start(in_sums, in_small, "in", everyone=[d_g1])

    grads, deltas, new_m, new_v = {}, {}, {}, {}
    after, small_own, small_recv = in_chips[0][4], [], []
    for chips, sums, names, tag in ((ff1_chips, ff1_sums, ("w_ff1",), "ff1"), (ff2_chips, ff2_sums, ("w_ff2",), "ff2"),
                                    (mid_chips, mid_sums, mid_names, "mid"), (in_chips, in_sums, ("w_in",), "in")):
        sent, recvs = _chips_wait(chips, after, tag)
        small_own += sent[len(names):]
        small_recv += recvs[len(names):]
        if tag == "mid":
            done = _final_adamw_few([ps[0] for ps in sums], recvs[:len(names)], [shard(n) for n in names],
                                    [shard("m_" + n) for n in names], [shard("v_" + n) for n in names], "adamw_mid")
            done = list(zip(*done))
        else:
            done = [_final_adamw(part, recv, shard(n), shard("m_" + n), shard("v_" + n), "adamw_" + n)
                    for n, (part, _), recv in zip(names, sums, recvs)]
        for n, results in zip(names, done):
            for dst, arr in zip((grads, deltas, new_m, new_v), results):
                dst[n] = (arr.T if n == "w_in" else arr)[None]
            after = results[1]

    views = {"w_spatial": (GROUPS, BLOCK, BLOCK), "b_spatial": (GROUPS, BLOCK), "final_gain": (1, D_MODEL)}
    def view(name, base):
        return args[name].T if base == "rel_bias_table" else args[name].reshape(views.get(base, args[name].shape))

    small_in = [[view(pre + n, n) for n in SMALL] for pre in ("", "m_", "v_")]
    loss, *small_out = _adamw_small(small_own[:-1], small_recv[:-1], small_own[-1], small_recv[-1], *small_in, after=after)
    for dst, arrays in zip((grads, deltas, new_m, new_v), small_out):
        for n, arr in zip(SMALL, arrays):
            dst[n] = arr.T if n == "rel_bias_table" else arr.reshape(args[n].shape)
    loss = loss[0, 0]

    order = ("norm1_gain", "w_in", "gmlp_v_gain", "w_spatial", "b_spatial", "attn_sinks", "rel_bias_table", "w_out",
             "norm2_gain", "w_ff1", "w_ff2", "w_ple_proj", "w_ple_gate", "final_gain")
    return (loss, grad_x, *[grads[n] for n in order], *[deltas[n] for n in order],
            *[new_m[n] for n in order], *[new_v[n] for n in order])
```

```python
import math

import numpy as np
import jax
import jax.numpy as jnp
from jax import lax
from jax.experimental import pallas as pl
from jax.experimental.pallas import tpu as pltpu

F32 = jnp.float32
BF = jnp.bfloat16
MESH = pl.DeviceIdType.MESH
N_DEV = 8

D_MODEL = 1024
PLE_DIM = 256
D_GMLP = 512
GROUPS = 4
GDIM = 128
BLOCK = 128
D_ATTN = 512
HEAD_DIM = 64
N_Q = 8
Q_PER_KV = 4
N_KV = N_Q // Q_PER_KV
ROWS4 = Q_PER_KV * BLOCK
D_KV = 128
D_FF = 4096
D_IN = 1792
D_MAIN = 2 * D_GMLP + D_ATTN
REL_BUCKETS = 32
EPS = 1e-6
NEG_INF = -1e30
SCALE = HEAD_DIM ** -0.5
GELU_C = math.sqrt(2.0 / math.pi)
GELU_A = 0.044715

ADAM_LR = 0.001
ADAM_B1 = 0.9
ADAM_B2 = 0.999
ADAM_EPS = 1e-08
ADAM_WD = 0.01
ADAM_STEP = 10

V7X_VMEM_LIMIT = 61440 * 1024
TOK_TILE = 256
IO_TOK_TILE = 512


def _call(body, after=None, **kw):
    if after is None:
        return pl.pallas_call(body, **kw)
    n_in = len(kw["in_specs"])

    def ordered(*refs):
        body(*refs[:n_in], *refs[n_in + 1:])

    kw["in_specs"] = list(kw["in_specs"]) + [pl.BlockSpec(memory_space=pl.ANY)]
    fn = pl.pallas_call(ordered, **kw)
    return lambda *operands: fn(*operands, after)


def _params(sem=None):
    if sem is None:
        return pltpu.CompilerParams(vmem_limit_bytes=V7X_VMEM_LIMIT)
    return pltpu.CompilerParams(dimension_semantics=sem, vmem_limit_bytes=V7X_VMEM_LIMIT)


def _nn(a, b):
    return jnp.dot(a, b, preferred_element_type=F32)


def _nt(a, b):
    return lax.dot_general(a, b, (((1,), (1,)), ((), ())), preferred_element_type=F32)


def _tn(a, b):
    return lax.dot_general(a, b, (((0,), (0,)), ((), ())), preferred_element_type=F32)


def _gelu_tanh(x):
    return jnp.tanh(GELU_C * (x + GELU_A * (x * x * x)))


def _gelu(x, t):
    return x * (0.5 * (1.0 + t))


def _gelu_and_grad(x, t):
    cdf = 0.5 * (1.0 + t)
    return x * cdf, cdf + 0.5 * x * (1.0 - t * t) * (GELU_C * (1.0 + 3.0 * GELU_A * (x * x)))


def _rms_scale(x):
    return lax.rsqrt(jnp.mean(x * x, axis=-1, keepdims=True) + EPS)


def _rms_bwd(dxn, x, r):
    return r * dxn - x * ((r * r * r) * jnp.mean(dxn * x, axis=-1, keepdims=True))


def _bucket_table():
    a = np.arange(BLOCK)[:, None]
    j = np.arange(2 * BLOCK)[None, :]
    n = BLOCK + a - j
    valid = (n >= 0) & (n < BLOCK)
    nc = np.maximum(n, 0)
    max_exact = REL_BUCKETS // 2
    nf = np.maximum(nc, 1).astype(np.float32)
    large = max_exact + (
        np.log(nf / np.float32(max_exact)) / np.float32(math.log(BLOCK / max_exact)) * np.float32(REL_BUCKETS - max_exact)
    ).astype(np.int32)
    large = np.minimum(large, REL_BUCKETS - 1)
    bucket = np.where(nc < max_exact, nc, large)
    return np.where(valid, bucket, -1).astype(np.int32)


def _in_proj(x, g1, w_in_t):
    s = x.shape[0]
    tm = min(IO_TOK_TILE, s)

    def body(x_ref, g_ref, w_ref, zuv_ref, qkv_ref, hn_ref):
        xv = x_ref[...]
        hn = ((xv * _rms_scale(xv)) * g_ref[...]).astype(BF)
        hn_ref[...] = hn
        z = _nt(hn, w_ref[...])
        zuv_ref[...] = z[:, : 2 * D_GMLP]
        qkv_ref[...] = z[:, 2 * D_GMLP:].astype(BF)

    return _call(
        body,
        name="in_proj",
        grid=(s // tm,),
        in_specs=[
            pl.BlockSpec((tm, D_MODEL), lambda i: (i, 0)),
            pl.BlockSpec((1, D_MODEL), lambda i: (0, 0)),
            pl.BlockSpec((D_IN, D_MODEL), lambda i: (0, 0)),
        ],
        out_specs=[
            pl.BlockSpec((tm, 2 * D_GMLP), lambda i: (i, 0)),
            pl.BlockSpec((tm, D_ATTN + 2 * D_KV), lambda i: (i, 0)),
            pl.BlockSpec((tm, D_MODEL), lambda i: (i, 0)),
        ],
        out_shape=[
            jax.ShapeDtypeStruct((s, 2 * D_GMLP), F32),
            jax.ShapeDtypeStruct((s, D_ATTN + 2 * D_KV), BF),
            jax.ShapeDtypeStruct((s, D_MODEL), BF),
        ],
        compiler_params=_params(("arbitrary",)),
    )(x, g1, w_in_t)


def _head_rows(h):
    kh, g = divmod(h, Q_PER_KV)
    return kh, slice(g * BLOCK, (g + 1) * BLOCK)


def _build_bias(bias_ref, bucket_ref, table_ref):
    bucket = bucket_ref[...]
    for h in range(N_Q):
        acc = jnp.zeros((BLOCK, 2 * BLOCK), F32)
        for b in range(REL_BUCKETS):
            acc = jnp.where(bucket == b, table_ref[h, b], acc)
        kh, rows = _head_rows(h)
        bias_ref[kh, rows, :] = acc


def _window_masks(i):
    row = lax.broadcasted_iota(jnp.int32, (ROWS4, BLOCK), 0) & (BLOCK - 1)
    col = lax.broadcasted_iota(jnp.int32, (ROWS4, BLOCK), 1)
    return (col > row) & (i > 0), col <= row


def _stack_heads(ref, kh, offset):
    first = offset + kh * Q_PER_KV * HEAD_DIM
    return jnp.concatenate(
        [ref[:, first + g * HEAD_DIM: first + (g + 1) * HEAD_DIM].astype(BF) for g in range(Q_PER_KV)], axis=0)


def _stack_sinks(sink_ref, kh):
    return jnp.concatenate([jnp.full((BLOCK, 1), sink_ref[0, kh * Q_PER_KV + g], F32) for g in range(Q_PER_KV)], axis=0)


def _tril_bf16(w_ref, g):
    row = lax.broadcasted_iota(jnp.int32, (BLOCK, BLOCK), 0)
    col = lax.broadcasted_iota(jnp.int32, (BLOCK, BLOCK), 1)
    return jnp.where(col <= row, w_ref[g], 0.0).astype(BF)


def _attn_probs(q_h, k_prev, k_cur, bias_h, sink, valid_prev, valid_cur):
    l_prev = jnp.where(valid_prev, _nt(q_h, k_prev) * SCALE + bias_h[:, :BLOCK], NEG_INF)
    l_cur = jnp.where(valid_cur, _nt(q_h, k_cur) * SCALE + bias_h[:, BLOCK:], NEG_INF)
    m = jnp.maximum(jnp.maximum(jnp.max(l_prev, axis=-1, keepdims=True), jnp.max(l_cur, axis=-1, keepdims=True)), sink)
    e_prev = jnp.exp(l_prev - m)
    e_cur = jnp.exp(l_cur - m)
    e_sink = jnp.exp(sink - m)
    denom = jnp.sum(e_prev, axis=-1, keepdims=True) + jnp.sum(e_cur, axis=-1, keepdims=True) + e_sink
    return e_prev / denom, e_cur / denom, e_sink / denom


def _mixer_specs(nb):
    cl = lambda i: jnp.minimum(i, nb - 1)
    return [
        pl.BlockSpec((BLOCK, 2 * D_GMLP), lambda i: (cl(i), 0)),
        pl.BlockSpec((BLOCK, D_ATTN), lambda i: (cl(i), 0)),
        pl.BlockSpec((BLOCK, 2 * D_KV), lambda i: (cl(i), D_ATTN // (2 * D_KV))),
        pl.BlockSpec((BLOCK, 2 * D_KV), lambda i: (jnp.maximum(cl(i) - 1, 0), D_ATTN // (2 * D_KV))),
        pl.BlockSpec((1, D_GMLP), lambda i: (0, 0)),
        pl.BlockSpec((GROUPS, BLOCK, BLOCK), lambda i: (0, 0, 0)),
        pl.BlockSpec((GROUPS, BLOCK, 1), lambda i: (0, 0, 0)),
        pl.BlockSpec(memory_space=pltpu.SMEM),
        pl.BlockSpec(memory_space=pltpu.SMEM),
        pl.BlockSpec((BLOCK, 2 * BLOCK), lambda i: (0, 0)),
    ]


def _mixer_fwd(zuv, qkv, gv, w_sp, b_sp, sinks, table, bucket, after=None):
    s = zuv.shape[0]
    nb = s // BLOCK

    def body(zuv_ref, q_ref, kvc_ref, kvp_ref, gv_ref, w_ref, b_ref, sink_ref, table_ref, bucket_ref,
             mix_ref, tanh_ref, prob_ref, psink_ref, bias_ref):
        i = pl.program_id(0)

        @pl.when(i == 0)
        def _():
            _build_bias(bias_ref, bucket_ref, table_ref)

        t = _gelu_tanh(zuv_ref[...])
        tanh_ref[...] = t
        u = _gelu(zuv_ref[:, :D_GMLP], t[:, :D_GMLP])
        vg = _gelu(zuv_ref[:, D_GMLP:], t[:, D_GMLP:])
        for g in range(GROUPS):
            sl = slice(g * GDIM, (g + 1) * GDIM)
            vg_g = vg[:, sl]
            vn = ((vg_g * _rms_scale(vg_g)) * gv_ref[:, sl]).astype(BF)
            sv = _nn(_tril_bf16(w_ref, g), vn) + b_ref[g]
            mix_ref[:, sl] = (u[:, sl] * sv).astype(BF)

        valid_prev, valid_cur = _window_masks(i)
        for kh in range(N_KV):
            ksl = slice(kh * HEAD_DIM, (kh + 1) * HEAD_DIM)
            vsl = slice(D_KV + kh * HEAD_DIM, D_KV + (kh + 1) * HEAD_DIM)
            q4 = _stack_heads(q_ref, kh, 0)
            p_prev, p_cur, p_sink = _attn_probs(
                q4, kvp_ref[:, ksl], kvc_ref[:, ksl], bias_ref[kh], _stack_sinks(sink_ref, kh), valid_prev, valid_cur)
            prob_ref[0, kh, :, :BLOCK] = p_prev
            prob_ref[0, kh, :, BLOCK:] = p_cur
            psink_ref[0, kh] = jnp.broadcast_to(p_sink, (ROWS4, LANES))
            o4 = _nn(p_prev.astype(BF), kvp_ref[:, vsl]) + _nn(p_cur.astype(BF), kvc_ref[:, vsl])
            for g in range(Q_PER_KV):
                first = D_GMLP + (kh * Q_PER_KV + g) * HEAD_DIM
                mix_ref[:, first:first + HEAD_DIM] = o4[g * BLOCK:(g + 1) * BLOCK].astype(BF)

    return _call(
        body,
        name="mixer_fwd",
        after=after,
        grid=(nb,),
        in_specs=_mixer_specs(nb),
        out_specs=[
            pl.BlockSpec((BLOCK, D_MODEL), lambda i: (i, 0)),
            pl.BlockSpec((BLOCK, 2 * D_GMLP), lambda i: (i, 0)),
            pl.BlockSpec((1, N_KV, ROWS4, 2 * BLOCK), lambda i: (i, 0, 0, 0)),
            pl.BlockSpec((1, N_KV, ROWS4, LANES), lambda i: (i, 0, 0, 0)),
        ],
        out_shape=[
            jax.ShapeDtypeStruct((s, D_MODEL), BF),
            jax.ShapeDtypeStruct((s, 2 * D_GMLP), F32),
            jax.ShapeDtypeStruct((nb, N_KV, ROWS4, 2 * BLOCK), F32),
            jax.ShapeDtypeStruct((nb, N_KV, ROWS4, LANES), F32),
        ],
        scratch_shapes=[pltpu.VMEM((N_KV, ROWS4, 2 * BLOCK), F32)],
        compiler_params=_params(("arbitrary",)),
    )(zuv, qkv, qkv, qkv, gv, w_sp, b_sp, sinks, table, bucket)


def _tail(x, mix, p, t, g2, gf, w_out, w_ff1, w_ff2, w_gate, w_proj):
    s = x.shape[0]
    tm = min(TOK_TILE, s)
    n_ff = w_ff1.shape[0]
    fc = D_FF // n_ff
    pc = D_MODEL // N_DEV

    def body(x_ref, mix_ref, p_ref, t_ref, g2_ref, gf_ref, wo_ref, w1_ref, w2_ref, wg_ref, wp_ref,
             small_ref, dh1_ref, dmix_ref, hn2_ref, a_ref, df_ref, dh2_ref, gwo_ref, gwp_ref, gwg_ref, f_ref):
        i = pl.program_id(0)

        @pl.when(i == 0)
        def _():
            small_ref[...] = jnp.zeros_like(small_ref)
            gwo_ref[...] = jnp.zeros_like(gwo_ref)
            gwp_ref[...] = jnp.zeros_like(gwp_ref)
            gwg_ref[...] = jnp.zeros_like(gwg_ref)

        h1 = x_ref[...] + _nn(mix_ref[...], wo_ref[...])
        dh1_ref[...] = h1
        r2 = _rms_scale(h1)
        hn2_ref[...] = ((h1 * r2) * g2_ref[...]).astype(BF)
        h2 = h1
        for c in range(n_ff):
            f = _nn(hn2_ref[...], w1_ref[c])
            f_ref[:, c * fc:(c + 1) * fc] = f
            a = jnp.square(jnp.maximum(f, 0.0)).astype(BF)
            a_ref[:, c * fc:(c + 1) * fc] = a
            h2 = h2 + _nn(a, w2_ref[c * fc:(c + 1) * fc, :])
        h2b = h2.astype(BF)
        gate = jax.nn.sigmoid(_nn(h2b, wg_ref[...]))
        pb = p_ref[...].astype(BF)
        pp = _nn(pb, wp_ref[...])
        h3 = h2 + gate * pp
        rf = _rms_scale(h3)
        gf_v = gf_ref[...]
        err = (h3 * rf) * gf_v - t_ref[...]
        small_ref[2:3, :] += jnp.sum(jnp.sum(err * err, axis=-1, keepdims=True), axis=0, keepdims=True) * (0.5 / D_MODEL)

        dy = err * (1.0 / D_MODEL)
        small_ref[1:2, :] += jnp.sum(dy * (h3 * rf), axis=0, keepdims=True)
        dh3 = _rms_bwd(dy * gf_v, h3, rf)
        gw_proj = _tn(pb, (dh3 * gate).astype(BF))
        for q in range(N_DEV):
            gwp_ref[q] += gw_proj[:, q * pc:(q + 1) * pc]
        dgl = ((dh3 * pp) * (gate * (1.0 - gate))).astype(BF)
        gwg_ref[...] += _tn(h2b, dgl)
        dh2 = dh3 + _nt(dgl, wg_ref[...])
        dmix_ref[...] = dh2
        dh2_ref[...] = dh2.astype(BF)
        dhn2 = jnp.zeros((tm, D_MODEL), F32)
        for c in range(n_ff):
            da = _nt(dh2_ref[...], w2_ref[c * fc:(c + 1) * fc, :])
            df = (da * (2.0 * jnp.maximum(f_ref[:, c * fc:(c + 1) * fc], 0.0))).astype(BF)
            df_ref[:, c * fc:(c + 1) * fc] = df
            dhn2 = dhn2 + _nt(df, w1_ref[c])
        h1 = dh1_ref[...]
        small_ref[0:1, :] += jnp.sum(dhn2 * (h1 * r2), axis=0, keepdims=True)
        dh1 = dmix_ref[...] + _rms_bwd(dhn2 * g2_ref[...], h1, r2)
        dh1_ref[...] = dh1
        dh1b = dh1.astype(BF)
        gwo_ref[...] += _tn(mix_ref[...], dh1b)
        dmix_ref[...] = _nt(dh1b, wo_ref[...])

    tile = lambda cols: pl.BlockSpec((tm, cols), lambda i: (i, 0))
    whole = lambda shape: pl.BlockSpec(shape, lambda i: (0,) * len(shape), pipeline_mode=pl.Buffered(1))
    total = lambda shape: pl.BlockSpec(shape, lambda i: (0,) * len(shape))
    row = pl.BlockSpec((1, D_MODEL), lambda i: (0, 0))
    act = lambda cols, dt: jax.ShapeDtypeStruct((s, cols), dt)
    gw_shapes = [(D_MODEL, D_MODEL), (N_DEV, PLE_DIM, pc), (D_MODEL, D_MODEL)]
    return _call(
        body,
        name="tail",
        grid=(s // tm,),
        in_specs=[tile(D_MODEL), tile(D_MODEL), tile(PLE_DIM), tile(D_MODEL), row, row,
                  whole(w_out.shape), whole(w_ff1.shape), whole(w_ff2.shape), whole(w_gate.shape), whole(w_proj.shape)],
        out_specs=[total((8, D_MODEL)), tile(D_MODEL), tile(D_MODEL), tile(D_MODEL), tile(D_FF), tile(D_FF), tile(D_MODEL),
                   *[total(shape) for shape in gw_shapes]],
        out_shape=[jax.ShapeDtypeStruct((8, D_MODEL), F32),
                   act(D_MODEL, F32), act(D_MODEL, F32), act(D_MODEL, BF), act(D_FF, BF), act(D_FF, BF), act(D_MODEL, BF),
                   *[jax.ShapeDtypeStruct(shape, F32) for shape in gw_shapes]],
        scratch_shapes=[pltpu.VMEM((tm, D_FF), F32)],
        compiler_params=_params(("arbitrary",)),
    )(x, mix, p, t, g2, gf, w_out, w_ff1, w_ff2, w_gate, w_proj)


def _mixer_bwd(zuv, qkv, dmix, saved, gv, w_sp, b_sp, bucket, after=None):
    s = zuv.shape[0]
    nb = s // BLOCK

    def body(zuv_ref, q_ref, kvc_ref, kvp_ref, gv_ref, w_ref, b_ref, bucket_ref, dmix_ref, tanh_ref, prob_ref, psink_ref,
             dzm_ref, dkv_ref, dgv_ref, dw_ref, db_ref, dattn_ref,
             dbias_ref, carry_ref, dsink_acc, db_acc):
        i = pl.program_id(0)

        @pl.when(i == 0)
        def _():
            dbias_ref[...] = jnp.zeros_like(dbias_ref)
            carry_ref[...] = jnp.zeros_like(carry_ref)
            dsink_acc[...] = jnp.zeros_like(dsink_acc)
            dgv_ref[...] = jnp.zeros_like(dgv_ref)
            dw_ref[...] = jnp.zeros_like(dw_ref)
            db_acc[...] = jnp.zeros_like(db_acc)

        @pl.when(i < nb)
        def _():
            u, du_dz = _gelu_and_grad(zuv_ref[:, :D_GMLP], tanh_ref[:, :D_GMLP])
            vg, dvg_dz = _gelu_and_grad(zuv_ref[:, D_GMLP:], tanh_ref[:, D_GMLP:])
            for g in range(GROUPS):
                sl = slice(g * GDIM, (g + 1) * GDIM)
                vg_g = vg[:, sl]
                rg = _rms_scale(vg_g)
                vhat = vg_g * rg
                gain = gv_ref[:, sl]
                vn = (vhat * gain).astype(BF)
                w_g = _tril_bf16(w_ref, g)
                sv = _nn(w_g, vn) + b_ref[g]
                dmix_a = dmix_ref[:, sl]
                dsv = dmix_a * u[:, sl]
                dsvb = dsv.astype(BF)
                db_acc[g] += jnp.sum(dsv, axis=-1, keepdims=True)
                dw_ref[g] += _nt(dsvb, vn)
                dvn = _tn(w_g, dsvb)
                dgv_ref[:, sl] += jnp.sum(dvn * vhat, axis=0, keepdims=True)
                dvg = _rms_bwd(dvn * gain, vg_g, rg)
                dzm_ref[:, sl] = ((dmix_a * sv) * du_dz[:, sl]).astype(BF)
                dzm_ref[:, D_GMLP + g * GDIM: D_GMLP + (g + 1) * GDIM] = (dvg * dvg_dz[:, sl]).astype(BF)

            for kh in range(N_KV):
                ksl = slice(kh * HEAD_DIM, (kh + 1) * HEAD_DIM)
                vsl = slice(D_KV + kh * HEAD_DIM, D_KV + (kh + 1) * HEAD_DIM)
                k_prev, k_cur = kvp_ref[:, ksl], kvc_ref[:, ksl]
                v_prev, v_cur = kvp_ref[:, vsl], kvc_ref[:, vsl]
                q4 = _stack_heads(q_ref, kh, 0)
                p_prev, p_cur, p_sink = prob_ref[0, kh, :, :BLOCK], prob_ref[0, kh, :, BLOCK:], psink_ref[0, kh, :, 0:1]
                do4 = _stack_heads(dmix_ref, kh, D_GMLP)
                dp_prev = _nt(do4, v_prev)
                dp_cur = _nt(do4, v_cur)
                delta = jnp.sum(p_prev * dp_prev, axis=-1, keepdims=True) + jnp.sum(p_cur * dp_cur, axis=-1, keepdims=True)
                ds_prev = p_prev * (dp_prev - delta)
                ds_cur = p_cur * (dp_cur - delta)
                dsink_acc[kh] -= p_sink * delta
                dbias_ref[kh, :, :BLOCK] += ds_prev
                dbias_ref[kh, :, BLOCK:] += ds_cur
                dsb_prev = ds_prev.astype(BF)
                dsb_cur = ds_cur.astype(BF)
                dq4 = (_nn(dsb_prev, k_prev) + _nn(dsb_cur, k_cur)) * SCALE
                for g in range(Q_PER_KV):
                    first = 2 * D_GMLP + (kh * Q_PER_KV + g) * HEAD_DIM
                    dzm_ref[:, first:first + HEAD_DIM] = dq4[g * BLOCK:(g + 1) * BLOCK].astype(BF)
                dkv_ref[:, ksl] = (carry_ref[:, ksl] + _tn(dsb_prev, q4) * SCALE).astype(BF)
                dkv_ref[:, vsl] = (carry_ref[:, vsl] + _tn(p_prev.astype(BF), do4)).astype(BF)
                carry_ref[:, ksl] = _tn(dsb_cur, q4) * SCALE
                carry_ref[:, vsl] = _tn(p_cur.astype(BF), do4)

        @pl.when(i == nb)
        def _():
            dkv_ref[...] = carry_ref[...].astype(BF)
            row = lax.broadcasted_iota(jnp.int32, (BLOCK, BLOCK), 0)
            col = lax.broadcasted_iota(jnp.int32, (BLOCK, BLOCK), 1)
            for g in range(GROUPS):
                dw_ref[g] = jnp.where(col <= row, dw_ref[g], 0.0)
                db_ref[g:g + 1, :] = jnp.sum(jnp.where(col == row, db_acc[g], 0.0), axis=0, keepdims=True)
            bucket = bucket_ref[...]
            for b in range(N_Q, REL_BUCKETS):
                dattn_ref[N_Q, b] = 0.0
            for h in range(N_Q):
                kh, rows = _head_rows(h)
                dattn_ref[N_Q, h] = jnp.sum(dsink_acc[kh, rows, :])
                dbh = dbias_ref[kh, rows, :]
                for b in range(REL_BUCKETS):
                    dattn_ref[h, b] = jnp.sum(jnp.where(bucket == b, dbh, 0.0))

    cl = lambda i: jnp.minimum(i, nb - 1)
    const = lambda shape: pl.BlockSpec(shape, lambda i: (0,) * len(shape))
    return _call(
        body,
        name="mixer_bwd",
        after=after,
        grid=(nb + 1,),
        in_specs=_mixer_specs(nb)[:7] + [
            const((BLOCK, 2 * BLOCK)),
            pl.BlockSpec((BLOCK, D_MODEL), lambda i: (cl(i), 0)),
            pl.BlockSpec((BLOCK, 2 * D_GMLP), lambda i: (cl(i), 0)),
            pl.BlockSpec((1, N_KV, ROWS4, 2 * BLOCK), lambda i: (cl(i), 0, 0, 0)),
            pl.BlockSpec((1, N_KV, ROWS4, LANES), lambda i: (cl(i), 0, 0, 0)),
        ],
        out_specs=[
            pl.BlockSpec((BLOCK, D_MAIN), lambda i: (cl(i), 0)),
            pl.BlockSpec((BLOCK, 2 * D_KV), lambda i: (jnp.maximum(i - 1, 0), 0)),
            const((1, D_GMLP)),
            const((GROUPS, BLOCK, BLOCK)),
            const((GROUPS, BLOCK)),
            pl.BlockSpec(memory_space=pltpu.SMEM),
        ],
        out_shape=[
            jax.ShapeDtypeStruct((s, D_MAIN), BF),
            jax.ShapeDtypeStruct((s, 2 * D_KV), BF),
            jax.ShapeDtypeStruct((1, D_GMLP), F32),
            jax.ShapeDtypeStruct((GROUPS, BLOCK, BLOCK), F32),
            jax.ShapeDtypeStruct((GROUPS, BLOCK), F32),
            jax.ShapeDtypeStruct((N_Q + 1, REL_BUCKETS), F32),
        ],
        scratch_shapes=[
            pltpu.VMEM((N_KV, ROWS4, 2 * BLOCK), F32),
            pltpu.VMEM((BLOCK, 2 * D_KV), F32),
            pltpu.VMEM((N_KV, ROWS4, 1), F32),
            pltpu.VMEM((GROUPS, BLOCK, 1), F32),
        ],
        compiler_params=_params(("arbitrary",)),
    )(zuv, qkv, qkv, qkv, gv, w_sp, b_sp, bucket, dmix, *saved)


def _in_bwd(x, dh1, dzm, dkv, g1, w_in_t, after=None):
    s = x.shape[0]
    tm = min(IO_TOK_TILE, s)

    def body(x_ref, dh1_ref, dzm_ref, dkv_ref, g_ref, w_ref, dx_ref, dg_ref):
        @pl.when(pl.program_id(0) == 0)
        def _():
            dg_ref[...] = jnp.zeros_like(dg_ref)

        dhn = _nn(dzm_ref[...], w_ref[:D_MAIN, :]) + _nn(dkv_ref[...], w_ref[D_MAIN:, :])
        xv = x_ref[...]
        r = _rms_scale(xv)
        dg_ref[...] += jnp.sum(dhn * (xv * r), axis=0, keepdims=True)
        dx_ref[...] = dh1_ref[...] + _rms_bwd(dhn * g_ref[...], xv, r)

    tile = lambda cols: pl.BlockSpec((tm, cols), lambda i: (i, 0))
    row = pl.BlockSpec((1, D_MODEL), lambda i: (0, 0))
    return _call(
        body,
        name="in_bwd",
        after=after,
        grid=(s // tm,),
        in_specs=[tile(D_MODEL), tile(D_MODEL), tile(D_MAIN), tile(2 * D_KV), row, pl.BlockSpec((D_IN, D_MODEL), lambda i: (0, 0))],
        out_specs=[tile(D_MODEL), row],
        out_shape=[jax.ShapeDtypeStruct((s, D_MODEL), F32), jax.ShapeDtypeStruct((1, D_MODEL), F32)],
        compiler_params=_params(("arbitrary",)),
    )(x, dh1, dzm, dkv, g1, w_in_t)


def _wgrad_in(dzm, dkv, hn):
    s = hn.shape[0]
    tm = 2 * D_KV
    n_main = D_MAIN // tm

    def body(dzm_ref, dkv_ref, hn_ref, o_ref):
        i = pl.program_id(0)

        @pl.when(i < n_main)
        def _():
            o_ref[...] = _tn(dzm_ref[...], hn_ref[...])

        @pl.when(i == n_main)
        def _():
            o_ref[...] = _tn(dkv_ref[...], hn_ref[...])

    return _call(
        body,
        name="wgrad_in",
        grid=(n_main + 1,),
        in_specs=[
            pl.BlockSpec((s, tm), lambda i: (0, jnp.minimum(i, n_main - 1))),
            pl.BlockSpec((s, tm), lambda i: (0, 0)),
            pl.BlockSpec((s, D_MODEL), lambda i: (0, 0)),
        ],
        out_specs=pl.BlockSpec((tm, D_MODEL), lambda i: (i, 0)),
        out_shape=jax.ShapeDtypeStruct((D_IN, D_MODEL), F32),
        compiler_params=_params(("arbitrary",)),
    )(dzm, dkv, hn)


def _wgrad(a, b, tm, tn, name, peer_cols=0, after=None):
    s, m = a.shape
    n = b.shape[1]

    def body(a_ref, b_ref, o_ref, at_ref):
        @pl.when(pl.program_id(1) == 0)
        def _():
            at_ref[...] = a_ref[...].astype(BF).T

        r = _nn(at_ref[...], b_ref[...])
        if peer_cols:
            for q in range(tn // peer_cols):
                o_ref[q] = r[:, q * peer_cols:(q + 1) * peer_cols]
        else:
            o_ref[...] = r

    if peer_cols:
        out_spec = pl.BlockSpec((tn // peer_cols, tm, peer_cols), lambda i, j: (j, i, 0))
        out_shape = jax.ShapeDtypeStruct((n // peer_cols, m, peer_cols), F32)
    else:
        out_spec = pl.BlockSpec((tm, tn), lambda i, j: (i, j))
        out_shape = jax.ShapeDtypeStruct((m, n), F32)
    return _call(
        body,
        name=name,
        after=after,
        grid=(m // tm, n // tn),
        in_specs=[pl.BlockSpec((s, tm), lambda i, j: (0, i)), pl.BlockSpec((s, tn), lambda i, j: (0, j))],
        out_specs=out_spec,
        out_shape=out_shape,
        scratch_shapes=[pltpu.VMEM((tm, s), BF)],
        compiler_params=_params(("arbitrary", "arbitrary")),
    )(a, b)


def _adamw_math(w, g, m, v):
    m_new = ADAM_B1 * m + (1.0 - ADAM_B1) * g
    v_new = ADAM_B2 * v + (1.0 - ADAM_B2) * jnp.square(g)
    m_hat = m_new / (1.0 - ADAM_B1 ** ADAM_STEP)
    v_hat = v_new / (1.0 - ADAM_B2 ** ADAM_STEP)
    delta = -ADAM_LR * (m_hat / (jnp.sqrt(v_hat) + ADAM_EPS) + ADAM_WD * w)
    return delta, m_new, v_new


def _final_adamw(part, recv, w, m, v, name):
    r, c = w.shape
    tr = min(r, 512)

    def body(p_ref, r_ref, w_ref, m_ref, v_ref, g_ref, d_ref, mo_ref, vo_ref):
        g = p_ref[...]
        for j in range(3):
            g = g + r_ref[j].astype(F32)
        g_ref[...] = g
        d_ref[...], mo_ref[...], vo_ref[...] = _adamw_math(w_ref[...], g, m_ref[...], v_ref[...])

    spec = pl.BlockSpec((tr, c), lambda i: (i, 0))
    return _call(
        body,
        name=name,
        grid=(r // tr,),
        in_specs=[spec, pl.BlockSpec((3, tr, c), lambda i: (0, i, 0)), spec, spec, spec],
        out_specs=[spec] * 4,
        out_shape=[jax.ShapeDtypeStruct((r, c), F32)] * 4,
        compiler_params=_params(("arbitrary",)),
    )(part, recv, w, m, v)


def _rs_sum(g, land, blocks, name):
    _, r, c = g.shape
    tr = min(r, 256)

    def body(blk_ref, g0_ref, g1_ref, g2_ref, g3_ref, l_ref, part_ref, send_ref):
        part_ref[...] = g0_ref[0] + l_ref[0]
        for j, gj_ref in enumerate((g1_ref, g2_ref, g3_ref)):
            send_ref[j] = (gj_ref[0] + l_ref[j + 1]).astype(BF)

    def pick(j):
        return pl.BlockSpec((1, tr, c), lambda i, blk: (blk[j], i, 0))

    return _call(
        body,
        name=name,
        grid_spec=pltpu.PrefetchScalarGridSpec(
            num_scalar_prefetch=1,
            grid=(r // tr,),
            in_specs=[pick(0), pick(1), pick(2), pick(3), pl.BlockSpec((4, tr, c), lambda i, blk: (0, i, 0))],
            out_specs=[pl.BlockSpec((tr, c), lambda i, blk: (i, 0)), pl.BlockSpec((3, tr, c), lambda i, blk: (0, i, 0))],
        ),
        out_shape=[jax.ShapeDtypeStruct((r, c), F32), jax.ShapeDtypeStruct((3, r, c), BF)],
        compiler_params=_params(("arbitrary",)),
    )(blocks, g, g, g, g, land)


def _final_adamw_few(parts, recvs, ws, ms, vs, name):
    n = len(ws)

    def body(*refs):
        ins, outs = refs[:5 * n], refs[5 * n:]
        for k in range(n):
            p_ref, r_ref, w_ref, m_ref, v_ref = (ins[q * n + k] for q in range(5))
            g = p_ref[...]
            for j in range(3):
                g = g + r_ref[j].astype(F32)
            outs[k][...] = g
            outs[n + k][...], outs[2 * n + k][...], outs[3 * n + k][...] = _adamw_math(w_ref[...], g, m_ref[...], v_ref[...])

    outs = _call(
        body,
        name=name,
        out_shape=[jax.ShapeDtypeStruct(w.shape, F32) for w in ws] * 4,
        compiler_params=_params(),
    )(*parts, *recvs, *ws, *ms, *vs)
    return outs[:n], outs[n:2 * n], outs[2 * n:3 * n], outs[3 * n:]


def _rs_sum_few(gs, lands, blocks, name):
    n = len(gs)

    def body(blk_ref, *refs):
        g_refs, l_refs, part_refs, send_refs = refs[:n], refs[n:2 * n], refs[2 * n:3 * n], refs[3 * n:]
        for k in range(n):
            part_refs[k][...] = g_refs[k][blk_ref[0]] + l_refs[k][0]
            for j in range(3):
                send_refs[k][j] = (g_refs[k][blk_ref[j + 1]] + l_refs[k][j + 1]).astype(BF)

    outs = _call(
        body,
        name=name,
        in_specs=[pl.BlockSpec(memory_space=pltpu.SMEM)] + [pl.BlockSpec(memory_space=pltpu.VMEM)] * (2 * n),
        out_shape=[jax.ShapeDtypeStruct(g.shape[1:], F32) for g in gs] + [jax.ShapeDtypeStruct((3,) + g.shape[1:], BF) for g in gs],
        compiler_params=_params(),
    )(blocks, *gs, *lands)
    return [(outs[k], outs[n + k]) for k in range(n)]


def _adamw_small(own, recv, g1_own, g1_recv, weights, moms, vels, after):
    n_w = len(weights)

    def body(*refs):
        own_refs, recv_refs, g1_own_ref, g1_recv_ref = refs[:5], refs[5:10], refs[10], refs[11]
        w_refs, m_refs, v_refs = refs[12:12 + n_w], refs[12 + n_w:12 + 2 * n_w], refs[12 + 2 * n_w:12 + 3 * n_w]
        outs = refs[12 + 3 * n_w:]
        loss_ref, g_refs, d_refs = outs[0], outs[1:1 + n_w], outs[1 + n_w:1 + 2 * n_w]
        mo_refs, vo_refs = outs[1 + 2 * n_w:1 + 3 * n_w], outs[1 + 3 * n_w:]
        x, y, c = lax.axis_index("x"), lax.axis_index("y"), lax.axis_index("c")

        def in_place_order(values, my_place):
            acc = None
            for place in range(len(values)):
                r = place ^ my_place
                term = values[-1]
                for q in range(len(values) - 2, -1, -1):
                    term = jnp.where(r == q, values[q], term)
                acc = term if acc is None else acc + term
            return acc

        def total(k, *index):
            index = index or (slice(None),) * (len(own_refs[k].shape) - 1)
            across = [own_refs[k][(0, *index)], recv_refs[k][(1, *index)], recv_refs[k][(0, *index)], recv_refs[k][(2, *index)]]
            return in_place_order([v.astype(F32) for v in across], 2 * x + y)

        g1 = in_place_order([g1_own_ref[0]] + [g1_recv_ref[j] for j in range(N_DEV - 1)], 4 * x + 2 * y + c)
        grads = [
            g1, total(1), total(2), total(3),
            total(4, slice(N_Q, None), slice(0, N_Q)), total(4, slice(0, N_Q), slice(None)),
            total(0, slice(0, 1), slice(None)), total(0, slice(1, 2), slice(None))]
        loss_ref[...] = total(0, slice(2, 3), slice(0, 1))
        for k in range(n_w):
            g_refs[k][...] = grads[k]
            d_refs[k][...], mo_refs[k][...], vo_refs[k][...] = _adamw_math(w_refs[k][...], grads[k], m_refs[k][...], v_refs[k][...])

    shapes = [jax.ShapeDtypeStruct(w.shape, F32) for w in weights]
    outs = _call(
        body,
        name="adamw_small",
        after=after,
        in_specs=[pl.BlockSpec(memory_space=pltpu.VMEM)] * (12 + 3 * n_w),
        out_shape=[jax.ShapeDtypeStruct((1, 1), F32)] + shapes * 4,
        compiler_params=_params(),
    )(*own, *recv, g1_own, g1_recv, *weights, *moms, *vels)
    return outs[0], outs[1:1 + n_w], outs[1 + n_w:1 + 2 * n_w], outs[1 + 2 * n_w:1 + 3 * n_w], outs[1 + 3 * n_w:]


def _place():
    x, y, c = lax.axis_index("x"), lax.axis_index("y"), lax.axis_index("c")
    return x, y, c, [(1 - x, y), (x, 1 - y), (1 - x, 1 - y)]


def _dev_index(px, py, pc):
    return 4 * px + 2 * py + pc


HBM_SPEC = pl.BlockSpec(memory_space=pltpu.HBM)
SEM_SPEC = pl.BlockSpec(memory_space=pltpu.SEMAPHORE)
ANY_SPEC = pl.BlockSpec(memory_space=pl.ANY)
DATAFLOW = pltpu.SideEffectType.DATAFLOW_SIDE_EFFECTING


def _hbm(a):
    return pltpu.with_memory_space_constraint(a, pltpu.HBM)


def _prep_weights(shards):
    k_n = len(shards)

    def body(*refs):
        ins, outs, stage, sems = refs[:k_n], refs[k_n:2 * k_n], refs[2 * k_n:3 * k_n], refs[3 * k_n]
        x, y, c, _ = _place()
        copies = []
        for k in range(k_n):
            stage[k][...] = ins[k][...].astype(BF)
            copies.append(pltpu.make_async_copy(stage[k], outs[k].at[_dev_index(x, y, c)], sems.at[k]))
            copies[k].start()
        for cp in copies:
            cp.wait()

    return _call(
        body,
        name="prep_weights",
        in_specs=[pl.BlockSpec(memory_space=pltpu.VMEM)] * k_n,
        out_specs=[ANY_SPEC] * k_n,
        out_shape=[jax.ShapeDtypeStruct((N_DEV,) + sh.shape, BF) for sh in shards],
        scratch_shapes=[pltpu.VMEM(sh.shape, BF) for sh in shards] + [pltpu.SemaphoreType.DMA((k_n,))],
        compiler_params=_params(),
    )(*shards)


def _ag_piece(land_k, block, half, peer, send_sem, recv_sem):
    ref = land_k.at[_dev_index(*block)]
    if half is not None:
        rows = land_k.shape[1] // 2
        ref = ref.at[pl.ds(half * rows, rows)]
    return pltpu.make_async_remote_copy(
        src_ref=ref, dst_ref=ref, send_sem=send_sem, recv_sem=recv_sem, device_id=peer, device_id_type=MESH)


def _ag_plan():
    x, y, c, _ = _place()
    me, sib = (x, y, c), (x, y, 1 - c)
    xn, yn, diag = (1 - x, y, c), (x, 1 - y, c), (1 - x, 1 - y, c)
    return dict(
        relay_halves=[(me, 0, xn), (me, 1, yn)],
        others=[(me, None, sib), (me, 1, xn), (me, 0, yn)],
        relays=[(xn, 0, yn), (yn, 1, xn)],
        near=[(xn, None, sib), (yn, None, sib)],
        far=[(diag, None, sib)],
    )


def _ag_stage(land, stage, send_sems, recv_sems, act):
    copies = _ag_plan()[stage]
    n = len(copies)
    for k in range(len(land)):
        for j, (block, half, peer) in enumerate(copies):
            cp = _ag_piece(land[k], block, half, peer, send_sems.at[n * k + j], recv_sems.at[n * k + j])
            if act == "start":
                cp.start()
            else:
                cp.wait_send()
                cp.wait_recv()


def _sem_shapes(*counts):
    return [pltpu.SemaphoreType.DMA((n,)) for n in counts for _ in range(2)]


def _ag_start(first, rest):
    lands = list(first) + list(rest)
    k_n, k_first = len(lands), len(first)
    k_rest = k_n - k_first

    def body(*refs):
        land = refs[:k_n]
        sems = refs[k_n:k_n + 6]
        token = refs[-1]
        x, y, c, chips = _place()
        targets = [(x, y, 1 - c)] + [(*chip, c) for chip in chips]
        for k in range(k_first):
            for j, to in enumerate(targets):
                _ag_piece(land[k], (x, y, c), None, to, sems[0].at[4 * k + j], sems[1].at[4 * k + j]).start()
        _ag_stage(land[k_first:], "relay_halves", sems[2], sems[3], "start")
        _ag_stage(land[k_first:], "others", sems[4], sems[5], "start")
        token[...] = jnp.zeros_like(token)

    outs = pl.pallas_call(
        body,
        name="ag_start",
        in_specs=[HBM_SPEC] * k_n,
        out_specs=(*[SEM_SPEC] * 6, *[HBM_SPEC] * k_n, pl.BlockSpec(memory_space=pltpu.VMEM)),
        out_shape=(*_sem_shapes(4 * k_first, 2 * k_rest, 3 * k_rest),
                   *[pltpu.HBM(a.shape, a.dtype) for a in lands], jax.ShapeDtypeStruct((8, LANES), F32)),
        input_output_aliases={k: 6 + k for k in range(k_n)},
        compiler_params=pltpu.CompilerParams(has_side_effects=DATAFLOW),
    )(*[_hbm(a) for a in lands])
    flying = list(outs[6:6 + k_n])
    return (outs[0], outs[1], flying[:k_first]), (outs[2:6], flying[k_first:]), outs[-1]


def _ag_split_call(lands, waits, starts, after, name):
    k_n = len(lands)
    plan_sizes = dict(relay_halves=2, others=3, relays=2, near=2, far=1)
    n_in, n_out = 2 * len(waits), 2 * len(starts)

    def body(*refs):
        land = refs[:k_n]
        in_sems = refs[k_n:k_n + n_in]
        out_sems, token = refs[len(refs) - 1 - n_out:len(refs) - 1], refs[-1]
        for w, (stage, _, _) in enumerate(waits):
            _ag_stage(land, stage, in_sems[2 * w], in_sems[2 * w + 1], "wait")
            if w < len(starts):
                _ag_stage(land, starts[w], out_sems[2 * w], out_sems[2 * w + 1], "start")
        token[...] = jnp.zeros_like(token)

    outs = pl.pallas_call(
        body,
        name=name,
        in_specs=[HBM_SPEC] * k_n + [SEM_SPEC] * n_in + [ANY_SPEC],
        out_specs=(*[HBM_SPEC] * k_n, *[SEM_SPEC] * n_out, pl.BlockSpec(memory_space=pltpu.VMEM)),
        out_shape=(*[pltpu.HBM(a.shape, a.dtype) for a in lands], *_sem_shapes(*[plan_sizes[s] * k_n for s in starts]),
                   jax.ShapeDtypeStruct((8, LANES), F32)),
        input_output_aliases={k: k for k in range(k_n)},
        compiler_params=pltpu.CompilerParams(has_side_effects=DATAFLOW),
    )(*lands, *[s for _, a, b in waits for s in (a, b)], after)
    return list(outs[:k_n]), list(outs[k_n:k_n + n_out]), outs[-1]


def _ag_mid(lands, send_sems, recv_sems, after, name):
    k_n = len(lands)

    def body(*refs):
        land = refs[:k_n]
        send1, recv1 = refs[k_n], refs[k_n + 1]
        fwd_send, fwd_recv = refs[-2], refs[-1]
        x, y, c, chips = _place()
        sources = [(x, y, 1 - c)] + [(*chip, c) for chip in chips]
        for k in range(k_n):
            mine = land[k].at[_dev_index(x, y, c)]
            for j, frm in enumerate(sources):
                got = land[k].at[_dev_index(*frm)]
                cp = pltpu.make_async_remote_copy(
                    src_ref=mine, dst_ref=got, send_sem=send1.at[4 * k + j], recv_sem=recv1.at[4 * k + j], device_id=frm, device_id_type=MESH)
                cp.wait_send()
                cp.wait_recv()
                if j >= 1:
                    pltpu.make_async_remote_copy(
                        src_ref=got, dst_ref=got, send_sem=fwd_send.at[3 * k + j - 1], recv_sem=fwd_recv.at[3 * k + j - 1],
                        device_id=(x, y, 1 - c), device_id_type=MESH).start()

    outs = pl.pallas_call(
        body,
        name=name,
        in_specs=[HBM_SPEC] * k_n + [SEM_SPEC, SEM_SPEC, ANY_SPEC],
        out_specs=(*[HBM_SPEC] * k_n, SEM_SPEC, SEM_SPEC),
        out_shape=(*[pltpu.HBM(a.shape, a.dtype) for a in lands], pltpu.SemaphoreType.DMA((3 * k_n,)), pltpu.SemaphoreType.DMA((3 * k_n,))),
        input_output_aliases={k: k for k in range(k_n)},
        compiler_params=pltpu.CompilerParams(has_side_effects=DATAFLOW),
    )(*lands, send_sems, recv_sems, after)
    return list(outs[:k_n]), outs[-2], outs[-1]


def _ag_end(lands, fwd_send, fwd_recv, name):
    k_n = len(lands)

    def body(*refs):
        land = refs[:k_n]
        fsend, frecv = refs[k_n], refs[k_n + 1]
        x, y, c, chips = _place()
        for k in range(k_n):
            for j, chip in enumerate(chips):
                cp = pltpu.make_async_remote_copy(
                    src_ref=land[k].at[_dev_index(*chip, c)], dst_ref=land[k].at[_dev_index(*chip, 1 - c)],
                    send_sem=fsend.at[3 * k + j], recv_sem=frecv.at[3 * k + j], device_id=(x, y, 1 - c), device_id_type=MESH)
                cp.wait_send()
                cp.wait_recv()

    outs = pl.pallas_call(
        body,
        name=name,
        in_specs=[HBM_SPEC] * k_n + [SEM_SPEC, SEM_SPEC],
        out_specs=tuple([HBM_SPEC] * k_n),
        out_shape=tuple(pltpu.HBM(a.shape, a.dtype) for a in lands),
        input_output_aliases={k: k for k in range(k_n)},
        compiler_params=pltpu.CompilerParams(has_side_effects=DATAFLOW),
    )(*lands, fwd_send, fwd_recv)
    return list(outs)


def _chips4():
    x, y, c, others = _place()
    return x, y, c, [(x, y)] + others


def _route_sibling(j):
    x, y, c, chips = _chips4()
    return _dev_index(*chips[j], 1 - c), j, (x, y, 1 - c)


def _route_chips(j):
    x, y, c, chips = _chips4()
    return j, j, (*chips[j + 1], c)


def _route_sibling_whole(j):
    x, y, c, _ = _chips4()
    return 0, 0, (x, y, 1 - c)


def _route_chips_whole(j):
    x, y, c, chips = _chips4()
    return 0, j, (*chips[j + 1], c)


def _route_everyone(j):
    x, y, c, _ = _chips4()
    flip = [(j + 1) >> 2 & 1, (j + 1) >> 1 & 1, (j + 1) & 1]
    return 0, j, tuple(1 - v if f else v for v, f in zip((x, y, c), flip))


def _xchg_copies(routes, src, dst, send_sems, recv_sems):
    copies, sem = [], 0
    for k, (route, n) in enumerate(routes):
        for j in range(n):
            si, di, peer = route(j)
            copies.append(pltpu.make_async_remote_copy(
                src_ref=src[k].at[si], dst_ref=dst[k].at[di], send_sem=send_sems.at[sem], recv_sem=recv_sems.at[sem],
                device_id=peer, device_id_type=MESH))
            sem += 1
    return copies


def _xchg_start(srcs, slot_shapes, routes, name):
    k_n = len(srcs)
    n_sem = sum(n for _, n in routes)
    dsts = [lax.empty((n,) + tuple(sh), a.dtype) for sh, a, (_, n) in zip(slot_shapes, srcs, routes)]

    def body(*refs):
        src, dst = refs[:k_n], refs[k_n:2 * k_n]
        send_sems, recv_sems, token = refs[2 * k_n], refs[2 * k_n + 1], refs[-1]
        for cp in _xchg_copies(routes, src, dst, send_sems, recv_sems):
            cp.start()
        token[...] = jnp.zeros_like(token)

    arrays = list(srcs) + dsts
    outs = pl.pallas_call(
        body,
        name=name,
        in_specs=[HBM_SPEC] * (2 * k_n),
        out_specs=(SEM_SPEC, SEM_SPEC, *[HBM_SPEC] * (2 * k_n), pl.BlockSpec(memory_space=pltpu.VMEM)),
        out_shape=(pltpu.SemaphoreType.DMA((n_sem,)), pltpu.SemaphoreType.DMA((n_sem,)),
                   *[pltpu.HBM(a.shape, a.dtype) for a in arrays], jax.ShapeDtypeStruct((8, LANES), F32)),
        input_output_aliases={i: 2 + i for i in range(2 * k_n)},
        compiler_params=pltpu.CompilerParams(has_side_effects=DATAFLOW),
    )(*[_hbm(a) for a in arrays])
    return outs[0], outs[1], list(outs[2:2 + k_n]), list(outs[2 + k_n:2 + 2 * k_n]), outs[-1]


def _xchg_wait(send_sems, recv_sems, srcs, dsts, routes, after, name):
    k_n = len(srcs)

    def body(*refs):
        src, dst = refs[:k_n], refs[k_n:2 * k_n]
        for cp in _xchg_copies(routes, src, dst, refs[2 * k_n], refs[2 * k_n + 1]):
            cp.wait_send()
            cp.wait_recv()

    arrays = list(srcs) + list(dsts)
    outs = pl.pallas_call(
        body,
        name=name,
        in_specs=[HBM_SPEC] * (2 * k_n) + [SEM_SPEC, SEM_SPEC, ANY_SPEC],
        out_specs=tuple([HBM_SPEC] * (2 * k_n)),
        out_shape=tuple(pltpu.HBM(a.shape, a.dtype) for a in arrays),
        input_output_aliases={i: i for i in range(2 * k_n)},
        compiler_params=pltpu.CompilerParams(has_side_effects=DATAFLOW),
    )(*arrays, send_sems, recv_sems, after)
    return list(outs[:k_n]), list(outs[k_n:])


SMALL = ("norm1_gain", "gmlp_v_gain", "w_spatial", "b_spatial", "attn_sinks", "rel_bias_table", "norm2_gain", "final_gain")
LANES = 128


def _swap_start(grads, smalls, tag):
    srcs = list(grads) + [a[None] for a in smalls]
    shapes = [g.shape[1:] for g in grads] + [a.shape for a in smalls]
    routes = [(_route_sibling, 4)] * len(grads) + [(_route_sibling_whole, 1)] * len(smalls)
    return _xchg_start(srcs, shapes, routes, f"rs_{tag}_swap_start"), routes, len(grads)


def _swap_sums(swap, names, after, tag, few=0, small_dtypes=None):
    (send1, recv1, src1, land1, _), routes, n_rs = swap
    x, y, c, chips = _chips4()
    blocks = jnp.stack([_dev_index(*chip, c) for chip in chips]).astype(jnp.int32)
    src1, land1 = _xchg_wait(send1, recv1, src1, land1, routes, after, f"rs_{tag}_swap_wait")
    n_tiled = n_rs - few
    sums = [_rs_sum(g, land, blocks, f"rs_sum_{n}") for g, land, n in zip(src1[:n_tiled], land1[:n_tiled], names)]
    if few:
        sums += _rs_sum_few(src1[n_tiled:n_rs], land1[n_tiled:n_rs], blocks, f"rs_{tag}_sum")
    if len(src1) == n_rs:
        return sums, []
    return sums, _pair_sum(src1[n_rs:], land1[n_rs:], small_dtypes or [F32] * (len(src1) - n_rs), f"rs_{tag}_sum_small")


def _pair_sum(mine, theirs, dtypes, name):
    def body(*refs):
        n = len(refs) // 3
        for k in range(n):
            refs[2 * n + k][...] = (refs[k][...] + refs[n + k][...]).astype(dtypes[k])

    return _call(
        body,
        name=name,
        out_shape=[jax.ShapeDtypeStruct(a.shape, dt) for a, dt in zip(mine, dtypes)],
        compiler_params=_params(),
    )(*mine, *theirs)


def _chips_start(sums, small_sums, tag, everyone=()):
    sends = [ps[1] for ps in sums] + list(small_sums) + [a[None] for a in everyone]
    routes = [(_route_chips, 3)] * len(sums) + [(_route_chips_whole, 3)] * len(small_sums) + [(_route_everyone, 7)] * len(everyone)
    return _xchg_start(sends, [a.shape[1:] for a in sends], routes, f"rs_{tag}_chips_start"), routes


def _chips_wait(chips, after, tag):
    (send2, recv2, src2, land2, _), routes = chips
    return _xchg_wait(send2, recv2, src2, land2, routes, after, f"rs_{tag}_chips_wait")


def kernel(x, p, norm1_gain, w_in, gmlp_v_gain, w_spatial, b_spatial, attn_sinks, rel_bias_table, w_out, norm2_gain, w_ff1, w_ff2, w_ple_proj, w_ple_gate, final_gain, loss_target, m_norm1_gain, m_w_in, m_gmlp_v_gain, m_w_spatial, m_b_spatial, m_attn_sinks, m_rel_bias_table, m_w_out, m_norm2_gain, m_w_ff1, m_w_ff2, m_w_ple_proj, m_w_ple_gate, m_final_gain, v_norm1_gain, v_w_in, v_gmlp_v_gain, v_w_spatial, v_b_spatial, v_attn_sinks, v_rel_bias_table, v_w_out, v_norm2_gain, v_w_ff1, v_w_ff2, v_w_ple_proj, v_w_ple_gate, v_final_gain):
    args = dict(locals())
    s = x.shape[1]
    big = ("w_in", "w_out", "w_ff1", "w_ff2", "w_ple_proj", "w_ple_gate")

    x2, p2, t2 = x.reshape(s, D_MODEL), p.reshape(s, PLE_DIM), loss_target.reshape(s, D_MODEL)
    g1, gv, w_sp, b_sp, sinks, table, g2, gf = (args[n] for n in SMALL)
    bucket = jnp.asarray(_bucket_table())
    b_col = b_sp.reshape(GROUPS, BLOCK, 1)

    def shard(name):
        return args[name][0].T if name.endswith("w_in") else args[name][0]

    lands = _prep_weights([shard(n) for n in big])
    (send_in, recv_in, fly_in), (rest_sems, fly_rest), token = _ag_start(lands[:1], lands[1:])
    mid_in, fwd_send_in, fwd_recv_in = _ag_mid(fly_in, send_in, recv_in, token, "ag_mid_w_in")
    g_in = _ag_end(mid_in, fwd_send_in, fwd_recv_in, "ag_end_w_in")[0]
    full_in = g_in.reshape(D_IN, D_MODEL)

    zuv, qkv, hn1 = _in_proj(x2, g1, full_in)
    fly_rest, relay_sems, relayed = _ag_split_call(
        fly_rest, [("relay_halves", *rest_sems[:2])], ["relays"], zuv, "ag_relay")
    mix, *saved = _mixer_fwd(zuv, qkv, gv, w_sp[0], b_col, sinks, table.T, bucket, after=relayed)
    fly_rest, fwd_sems, _ = _ag_split_call(
        fly_rest, [("others", *rest_sems[2:]), ("relays", *relay_sems)], ["near", "far"], mix, "ag_mid_rest")
    g_out, g_ff1, g_ff2, g_proj, g_gate = _ag_split_call(
        fly_rest, [("near", *fwd_sems[:2]), ("far", *fwd_sems[2:])], [], mix, "ag_end_rest")[0]
    full_out, full_ff2, full_gate = g_out.reshape(D_MODEL, D_MODEL), g_ff2.reshape(D_FF, D_MODEL), g_gate.reshape(D_MODEL, D_MODEL)
    full_proj = g_proj.transpose(1, 0, 2).reshape(PLE_DIM, D_MODEL)

    tail_small, dh1, dmix, hn2, a, df, dh2, gw_out, gw_proj, gw_gate = _tail(
        x2, mix, p2, t2, g2, gf.reshape(1, D_MODEL), full_out, g_ff1, full_ff2, full_gate, full_proj)
    gw_out, gw_gate = (g.reshape(N_DEV, D_MODEL // N_DEV, D_MODEL) for g in (gw_out, gw_gate))

    few_names = ("w_out", "w_ple_proj", "w_ple_gate")
    ff1_names = ("w_ff1",) + few_names
    gw_ff1 = _wgrad(hn2, df, D_MODEL, 2 * D_FF // N_DEV, "wgrad_ff1", peer_cols=D_FF // N_DEV)
    ff1_swap = _swap_start([gw_ff1, gw_out, gw_proj, gw_gate], [tail_small], "ff1")
    gw_ff2 = _wgrad(a, dh2, 1024, D_MODEL, "wgrad_ff2", after=ff1_swap[0][4]).reshape(N_DEV, D_FF // N_DEV, D_MODEL)
    ff2_swap = _swap_start([gw_ff2], [], "ff2")
    ff1_sums, ff1_small = _swap_sums(ff1_swap, ff1_names, ff2_swap[0][4], "ff1", few=len(few_names))
    ff1_chips = _chips_start(ff1_sums, ff1_small, "ff1")
    ff2_sums, _ = _swap_sums(ff2_swap, ("w_ff2",), ff1_chips[0][4], "ff2")
    ff2_chips = _chips_start(ff2_sums, [], "ff2")

    dzm, dkv, d_gv, d_wsp, d_bsp, d_attn = _mixer_bwd(
        zuv, qkv, dmix, saved, gv, w_sp[0], b_col, bucket, after=ff2_chips[0][4])
    gw_in = _wgrad_in(dzm, dkv, hn1).reshape(N_DEV, D_IN // N_DEV, D_MODEL)
    in_swap = _swap_start([gw_in], [d_gv, d_wsp, d_bsp, d_attn], "in")
    dx, d_g1 = _in_bwd(x2, dh1, dzm, dkv, g1, full_in, after=in_swap[0][4])
    grad_x = dx.reshape(x.shape)
    in_sums, in_small = _swap_sums(in_swap, ("w_in",), dx, "in", small_dtypes=[F32, BF, F32, F32])
    in_chips = _chips_start(in_sums, in_small, "in", everyone=[d_g1])

    grads, deltas, new_m, new_v = {}, {}, {}, {}
    after, small_own, small_recv = in_chips[0][4], [], []
    for chips, sums, names, tag in ((ff1_chips, ff1_sums, ff1_names, "ff1"), (ff2_chips, ff2_sums, ("w_ff2",), "ff2"),
                                    (in_chips, in_sums, ("w_in",), "in")):
        sent, recvs = _chips_wait(chips, after, tag)
        small_own += sent[len(names):]
        small_recv += recvs[len(names):]
        tiled = [n for n in names if n not in few_names]
        done = [_final_adamw(part, recv, shard(n), shard("m_" + n), shard("v_" + n), "adamw_" + n)
                for n, (part, _), recv in zip(tiled, sums, recvs)]
        few = names[len(tiled):]
        if few:
            done += zip(*_final_adamw_few(
                [ps[0] for ps in sums[len(tiled):]], recvs[len(tiled):len(names)], [shard(n) for n in few],
                [shard("m_" + n) for n in few], [shard("v_" + n) for n in few], "adamw_few"))
        for n, results in zip(names, done):
            for dst, arr in zip((grads, deltas, new_m, new_v), results):
                dst[n] = (arr.T if n == "w_in" else arr)[None]
            after = results[1]

    views = {"w_spatial": (GROUPS, BLOCK, BLOCK), "b_spatial": (GROUPS, BLOCK), "final_gain": (1, D_MODEL)}
    def view(name, base):
        return args[name].T if base == "rel_bias_table" else args[name].reshape(views.get(base, args[name].shape))

    small_in = [[view(pre + n, n) for n in SMALL] for pre in ("", "m_", "v_")]
    loss, *small_out = _adamw_small(small_own[:-1], small_recv[:-1], small_own[-1], small_recv[-1], *small_in, after=after)
    for dst, arrays in zip((grads, deltas, new_m, new_v), small_out):
        for n, arr in zip(SMALL, arrays):
            dst[n] = arr.T if n == "rel_bias_table" else arr.reshape(args[n].shape)
    loss = loss[0, 0]

    order = ("norm1_gain", "w_in", "gmlp_v_gain", "w_spatial", "b_spatial", "attn_sinks", "rel_bias_table", "w_out",
             "norm2_gain", "w_ff1", "w_ff2", "w_ple_proj", "w_ple_gate", "final_gain")
    return (loss, grad_x, *[grads[n] for n in order], *[deltas[n] for n in order],
            *[new_m[n] for n in order], *[new_v[n] for n in order])
```

```python
import math

import numpy as np
import jax
import jax.numpy as jnp
from jax import lax
from jax.experimental import pallas as pl
from jax.experimental.pallas import tpu as pltpu

F32 = jnp.float32
BF = jnp.bfloat16
MESH = pl.DeviceIdType.MESH
N_DEV = 8

D_MODEL = 1024
PLE_DIM = 256
D_GMLP = 512
GROUPS = 4
GDIM = 128
BLOCK = 128
D_ATTN = 512
HEAD_DIM = 64
N_Q = 8
Q_PER_KV = 4
N_KV = N_Q // Q_PER_KV
ROWS4 = Q_PER_KV * BLOCK
D_KV = 128
D_FF = 4096
D_IN = 1792
D_MAIN = 2 * D_GMLP + D_ATTN
REL_BUCKETS = 32
EPS = 1e-6
NEG_INF = -1e30
SCALE = HEAD_DIM ** -0.5
GELU_C = math.sqrt(2.0 / math.pi)
GELU_A = 0.044715

ADAM_LR = 0.001
ADAM_B1 = 0.9
ADAM_B2 = 0.999
ADAM_EPS = 1e-08
ADAM_WD = 0.01
ADAM_STEP = 10

V7X_VMEM_LIMIT = 61440 * 1024
TOK_TILE = 256
IO_TOK_TILE = 512


def _call(body, after=None, **kw):
    if after is None:
        return pl.pallas_call(body, **kw)
    n_in = len(kw["in_specs"])

    def ordered(*refs):
        body(*refs[:n_in], *refs[n_in + 1:])

    kw["in_specs"] = list(kw["in_specs"]) + [pl.BlockSpec(memory_space=pl.ANY)]
    fn = pl.pallas_call(ordered, **kw)
    return lambda *operands: fn(*operands, after)


def _params(sem=None):
    if sem is None:
        return pltpu.CompilerParams(vmem_limit_bytes=V7X_VMEM_LIMIT)
    return pltpu.CompilerParams(dimension_semantics=sem, vmem_limit_bytes=V7X_VMEM_LIMIT)


def _nn(a, b):
    return jnp.dot(a, b, preferred_element_type=F32)


def _nt(a, b):
    return lax.dot_general(a, b, (((1,), (1,)), ((), ())), preferred_element_type=F32)


def _tn(a, b):
    return lax.dot_general(a, b, (((0,), (0,)), ((), ())), preferred_element_type=F32)


def _gelu_tanh(x):
    return jnp.tanh(GELU_C * (x + GELU_A * (x * x * x)))


def _gelu(x, t):
    return x * (0.5 * (1.0 + t))


def _gelu_and_grad(x, t):
    cdf = 0.5 * (1.0 + t)
    return x * cdf, cdf + 0.5 * x * (1.0 - t * t) * (GELU_C * (1.0 + 3.0 * GELU_A * (x * x)))


def _rms_scale(x):
    return lax.rsqrt(jnp.mean(x * x, axis=-1, keepdims=True) + EPS)


def _rms_bwd(dxn, x, r):
    return r * dxn - x * ((r * r * r) * jnp.mean(dxn * x, axis=-1, keepdims=True))


def _bucket_table():
    a = np.arange(BLOCK)[:, None]
    j = np.arange(2 * BLOCK)[None, :]
    n = BLOCK + a - j
    valid = (n >= 0) & (n < BLOCK)
    nc = np.maximum(n, 0)
    max_exact = REL_BUCKETS // 2
    nf = np.maximum(nc, 1).astype(np.float32)
    large = max_exact + (
        np.log(nf / np.float32(max_exact)) / np.float32(math.log(BLOCK / max_exact)) * np.float32(REL_BUCKETS - max_exact)
    ).astype(np.int32)
    large = np.minimum(large, REL_BUCKETS - 1)
    bucket = np.where(nc < max_exact, nc, large)
    return np.where(valid, bucket, -1).astype(np.int32)


def _in_proj(x, g1, w_in_t):
    s = x.shape[0]
    tm = min(IO_TOK_TILE, s)

    def body(x_ref, g_ref, w_ref, zuv_ref, qkv_ref, hn_ref):
        xv = x_ref[...]
        hn = ((xv * _rms_scale(xv)) * g_ref[...]).astype(BF)
        hn_ref[...] = hn
        z = _nt(hn, w_ref[...])
        zuv_ref[...] = z[:, : 2 * D_GMLP]
        qkv_ref[...] = z[:, 2 * D_GMLP:].astype(BF)

    return _call(
        body,
        name="in_proj",
        grid=(s // tm,),
        in_specs=[
            pl.BlockSpec((tm, D_MODEL), lambda i: (i, 0)),
            pl.BlockSpec((1, D_MODEL), lambda i: (0, 0)),
            pl.BlockSpec((D_IN, D_MODEL), lambda i: (0, 0)),
        ],
        out_specs=[
            pl.BlockSpec((tm, 2 * D_GMLP), lambda i: (i, 0)),
            pl.BlockSpec((tm, D_ATTN + 2 * D_KV), lambda i: (i, 0)),
            pl.BlockSpec((tm, D_MODEL), lambda i: (i, 0)),
        ],
        out_shape=[
            jax.ShapeDtypeStruct((s, 2 * D_GMLP), F32),
            jax.ShapeDtypeStruct((s, D_ATTN + 2 * D_KV), BF),
            jax.ShapeDtypeStruct((s, D_MODEL), BF),
        ],
        compiler_params=_params(("arbitrary",)),
    )(x, g1, w_in_t)


def _head_rows(h):
    kh, g = divmod(h, Q_PER_KV)
    return kh, slice(g * BLOCK, (g + 1) * BLOCK)


def _build_bias(bias_ref, bucket_ref, table_ref):
    bucket = bucket_ref[...]
    for h in range(N_Q):
        acc = jnp.zeros((BLOCK, 2 * BLOCK), F32)
        for b in range(REL_BUCKETS):
            acc = jnp.where(bucket == b, table_ref[h, b], acc)
        kh, rows = _head_rows(h)
        bias_ref[kh, rows, :] = acc


def _window_masks(i):
    row = lax.broadcasted_iota(jnp.int32, (ROWS4, BLOCK), 0) & (BLOCK - 1)
    col = lax.broadcasted_iota(jnp.int32, (ROWS4, BLOCK), 1)
    return (col > row) & (i > 0), col <= row


def _stack_heads(ref, kh, offset):
    first = offset + kh * Q_PER_KV * HEAD_DIM
    return jnp.concatenate(
        [ref[:, first + g * HEAD_DIM: first + (g + 1) * HEAD_DIM].astype(BF) for g in range(Q_PER_KV)], axis=0)


def _stack_sinks(sink_ref, kh):
    return jnp.concatenate([jnp.full((BLOCK, 1), sink_ref[0, kh * Q_PER_KV + g], F32) for g in range(Q_PER_KV)], axis=0)


def _tril_bf16(w_ref, g):
    row = lax.broadcasted_iota(jnp.int32, (BLOCK, BLOCK), 0)
    col = lax.broadcasted_iota(jnp.int32, (BLOCK, BLOCK), 1)
    return jnp.where(col <= row, w_ref[g], 0.0).astype(BF)


def _attn_probs(q_h, k_prev, k_cur, bias_h, sink, valid_prev, valid_cur):
    l_prev = jnp.where(valid_prev, _nt(q_h, k_prev) * SCALE + bias_h[:, :BLOCK], NEG_INF)
    l_cur = jnp.where(valid_cur, _nt(q_h, k_cur) * SCALE + bias_h[:, BLOCK:], NEG_INF)
    m = jnp.maximum(jnp.maximum(jnp.max(l_prev, axis=-1, keepdims=True), jnp.max(l_cur, axis=-1, keepdims=True)), sink)
    e_prev = jnp.exp(l_prev - m)
    e_cur = jnp.exp(l_cur - m)
    e_sink = jnp.exp(sink - m)
    denom = jnp.sum(e_prev, axis=-1, keepdims=True) + jnp.sum(e_cur, axis=-1, keepdims=True) + e_sink
    return e_prev / denom, e_cur / denom, e_sink / denom


def _mixer_specs(nb):
    cl = lambda i: jnp.minimum(i, nb - 1)
    return [
        pl.BlockSpec((BLOCK, 2 * D_GMLP), lambda i: (cl(i), 0)),
        pl.BlockSpec((BLOCK, D_ATTN), lambda i: (cl(i), 0)),
        pl.BlockSpec((BLOCK, 2 * D_KV), lambda i: (cl(i), D_ATTN // (2 * D_KV))),
        pl.BlockSpec((BLOCK, 2 * D_KV), lambda i: (jnp.maximum(cl(i) - 1, 0), D_ATTN // (2 * D_KV))),
        pl.BlockSpec((1, D_GMLP), lambda i: (0, 0)),
        pl.BlockSpec((GROUPS, BLOCK, BLOCK), lambda i: (0, 0, 0)),
        pl.BlockSpec((GROUPS, BLOCK, 1), lambda i: (0, 0, 0)),
        pl.BlockSpec(memory_space=pltpu.SMEM),
        pl.BlockSpec(memory_space=pltpu.SMEM),
        pl.BlockSpec((BLOCK, 2 * BLOCK), lambda i: (0, 0)),
    ]


def _mixer_fwd(zuv, qkv, gv, w_sp, b_sp, sinks, table, bucket, after=None):
    s = zuv.shape[0]
    nb = s // BLOCK

    def body(zuv_ref, q_ref, kvc_ref, kvp_ref, gv_ref, w_ref, b_ref, sink_ref, table_ref, bucket_ref,
             mix_ref, tanh_ref, prob_ref, psink_ref, bias_ref):
        i = pl.program_id(0)

        @pl.when(i == 0)
        def _():
            _build_bias(bias_ref, bucket_ref, table_ref)

        t = _gelu_tanh(zuv_ref[...])
        tanh_ref[...] = t
        u = _gelu(zuv_ref[:, :D_GMLP], t[:, :D_GMLP])
        vg = _gelu(zuv_ref[:, D_GMLP:], t[:, D_GMLP:])
        for g in range(GROUPS):
            sl = slice(g * GDIM, (g + 1) * GDIM)
            vg_g = vg[:, sl]
            vn = ((vg_g * _rms_scale(vg_g)) * gv_ref[:, sl]).astype(BF)
            sv = _nn(_tril_bf16(w_ref, g), vn) + b_ref[g]
            mix_ref[:, sl] = (u[:, sl] * sv).astype(BF)

        valid_prev, valid_cur = _window_masks(i)
        for kh in range(N_KV):
            ksl = slice(kh * HEAD_DIM, (kh + 1) * HEAD_DIM)
            vsl = slice(D_KV + kh * HEAD_DIM, D_KV + (kh + 1) * HEAD_DIM)
            q4 = _stack_heads(q_ref, kh, 0)
            p_prev, p_cur, p_sink = _attn_probs(
                q4, kvp_ref[:, ksl], kvc_ref[:, ksl], bias_ref[kh], _stack_sinks(sink_ref, kh), valid_prev, valid_cur)
            prob_ref[0, kh, :, :BLOCK] = p_prev
            prob_ref[0, kh, :, BLOCK:] = p_cur
            psink_ref[0, kh] = jnp.broadcast_to(p_sink, (ROWS4, LANES))
            o4 = _nn(p_prev.astype(BF), kvp_ref[:, vsl]) + _nn(p_cur.astype(BF), kvc_ref[:, vsl])
            for g in range(Q_PER_KV):
                first = D_GMLP + (kh * Q_PER_KV + g) * HEAD_DIM
                mix_ref[:, first:first + HEAD_DIM] = o4[g * BLOCK:(g + 1) * BLOCK].astype(BF)

    return _call(
        body,
        name="mixer_fwd",
        after=after,
        grid=(nb,),
        in_specs=_mixer_specs(nb),
        out_specs=[
            pl.BlockSpec((BLOCK, D_MODEL), lambda i: (i, 0)),
            pl.BlockSpec((BLOCK, 2 * D_GMLP), lambda i: (i, 0)),
            pl.BlockSpec((1, N_KV, ROWS4, 2 * BLOCK), lambda i: (i, 0, 0, 0)),
            pl.BlockSpec((1, N_KV, ROWS4, LANES), lambda i: (i, 0, 0, 0)),
        ],
        out_shape=[
            jax.ShapeDtypeStruct((s, D_MODEL), BF),
            jax.ShapeDtypeStruct((s, 2 * D_GMLP), F32),
            jax.ShapeDtypeStruct((nb, N_KV, ROWS4, 2 * BLOCK), F32),
            jax.ShapeDtypeStruct((nb, N_KV, ROWS4, LANES), F32),
        ],
        scratch_shapes=[pltpu.VMEM((N_KV, ROWS4, 2 * BLOCK), F32)],
        compiler_params=_params(("arbitrary",)),
    )(zuv, qkv, qkv, qkv, gv, w_sp, b_sp, sinks, table, bucket)


def _tail(x, mix, p, t, g2, gf, w_out, w_ff1, w_ff2, w_gate, w_proj):
    s = x.shape[0]
    tm = min(TOK_TILE, s)
    n_ff = w_ff1.shape[0]
    fc = D_FF // n_ff
    pc = D_MODEL // N_DEV

    def body(x_ref, mix_ref, p_ref, t_ref, g2_ref, gf_ref, wo_ref, w1_ref, w2_ref, wg_ref, wp_ref,
             small_ref, dh1_ref, dmix_ref, hn2_ref, a_ref, df_ref, dh2_ref, gwo_ref, gwp_ref, gwg_ref, f_ref):
        i = pl.program_id(0)

        @pl.when(i == 0)
        def _():
            small_ref[...] = jnp.zeros_like(small_ref)
            gwo_ref[...] = jnp.zeros_like(gwo_ref)
            gwp_ref[...] = jnp.zeros_like(gwp_ref)
            gwg_ref[...] = jnp.zeros_like(gwg_ref)

        h1 = x_ref[...] + _nn(mix_ref[...], wo_ref[...])
        dh1_ref[...] = h1
        r2 = _rms_scale(h1)
        hn2_ref[...] = ((h1 * r2) * g2_ref[...]).astype(BF)
        h2 = h1
        for c in range(n_ff):
            f = _nn(hn2_ref[...], w1_ref[c])
            f_ref[:, c * fc:(c + 1) * fc] = f
            a = jnp.square(jnp.maximum(f, 0.0)).astype(BF)
            a_ref[:, c * fc:(c + 1) * fc] = a
            h2 = h2 + _nn(a, w2_ref[c * fc:(c + 1) * fc, :])
        h2b = h2.astype(BF)
        gate = jax.nn.sigmoid(_nn(h2b, wg_ref[...]))
        pb = p_ref[...].astype(BF)
        pp = _nn(pb, wp_ref[...])
        h3 = h2 + gate * pp
        rf = _rms_scale(h3)
        gf_v = gf_ref[...]
        err = (h3 * rf) * gf_v - t_ref[...]
        small_ref[2:3, :] += jnp.sum(jnp.sum(err * err, axis=-1, keepdims=True), axis=0, keepdims=True) * (0.5 / D_MODEL)

        dy = err * (1.0 / D_MODEL)
        small_ref[1:2, :] += jnp.sum(dy * (h3 * rf), axis=0, keepdims=True)
        dh3 = _rms_bwd(dy * gf_v, h3, rf)
        gw_proj = _tn(pb, (dh3 * gate).astype(BF))
        for q in range(N_DEV):
            gwp_ref[q] += gw_proj[:, q * pc:(q + 1) * pc]
        dgl = ((dh3 * pp) * (gate * (1.0 - gate))).astype(BF)
        gwg_ref[...] += _tn(h2b, dgl)
        dh2 = dh3 + _nt(dgl, wg_ref[...])
        dmix_ref[...] = dh2
        dh2_ref[...] = dh2.astype(BF)
        dhn2 = jnp.zeros((tm, D_MODEL), F32)
        for c in range(n_ff):
            da = _nt(dh2_ref[...], w2_ref[c * fc:(c + 1) * fc, :])
            df = (da * (2.0 * jnp.maximum(f_ref[:, c * fc:(c + 1) * fc], 0.0))).astype(BF)
            df_ref[:, c * fc:(c + 1) * fc] = df
            dhn2 = dhn2 + _nt(df, w1_ref[c])
        h1 = dh1_ref[...]
        small_ref[0:1, :] += jnp.sum(dhn2 * (h1 * r2), axis=0, keepdims=True)
        dh1 = dmix_ref[...] + _rms_bwd(dhn2 * g2_ref[...], h1, r2)
        dh1_ref[...] = dh1
        dh1b = dh1.astype(BF)
        gwo_ref[...] += _tn(mix_ref[...], dh1b)
        dmix_ref[...] = _nt(dh1b, wo_ref[...])

    tile = lambda cols: pl.BlockSpec((tm, cols), lambda i: (i, 0))
    whole = lambda shape: pl.BlockSpec(shape, lambda i: (0,) * len(shape), pipeline_mode=pl.Buffered(1))
    total = lambda shape: pl.BlockSpec(shape, lambda i: (0,) * len(shape))
    row = pl.BlockSpec((1, D_MODEL), lambda i: (0, 0))
    act = lambda cols, dt: jax.ShapeDtypeStruct((s, cols), dt)
    gw_shapes = [(D_MODEL, D_MODEL), (N_DEV, PLE_DIM, pc), (D_MODEL, D_MODEL)]
    return _call(
        body,
        name="tail",
        grid=(s // tm,),
        in_specs=[tile(D_MODEL), tile(D_MODEL), tile(PLE_DIM), tile(D_MODEL), row, row,
                  whole(w_out.shape), whole(w_ff1.shape), whole(w_ff2.shape), whole(w_gate.shape), whole(w_proj.shape)],
        out_specs=[total((8, D_MODEL)), tile(D_MODEL), tile(D_MODEL), tile(D_MODEL), tile(D_FF), tile(D_FF), tile(D_MODEL),
                   *[total(shape) for shape in gw_shapes]],
        out_shape=[jax.ShapeDtypeStruct((8, D_MODEL), F32),
                   act(D_MODEL, F32), act(D_MODEL, F32), act(D_MODEL, BF), act(D_FF, BF), act(D_FF, BF), act(D_MODEL, BF),
                   *[jax.ShapeDtypeStruct(shape, F32) for shape in gw_shapes]],
        scratch_shapes=[pltpu.VMEM((tm, D_FF), F32)],
        compiler_params=_params(("arbitrary",)),
    )(x, mix, p, t, g2, gf, w_out, w_ff1, w_ff2, w_gate, w_proj)


def _mixer_bwd(zuv, qkv, dmix, saved, gv, w_sp, b_sp, bucket, after=None):
    s = zuv.shape[0]
    nb = s // BLOCK

    def body(zuv_ref, q_ref, kvc_ref, kvp_ref, gv_ref, w_ref, b_ref, bucket_ref, dmix_ref, tanh_ref, prob_ref, psink_ref,
             dzm_ref, dkv_ref, dgv_ref, dw_ref, db_ref, dattn_ref,
             dbias_ref, carry_ref, dsink_acc, db_acc):
        i = pl.program_id(0)

        @pl.when(i == 0)
        def _():
            dbias_ref[...] = jnp.zeros_like(dbias_ref)
            carry_ref[...] = jnp.zeros_like(carry_ref)
            dsink_acc[...] = jnp.zeros_like(dsink_acc)
            dgv_ref[...] = jnp.zeros_like(dgv_ref)
            dw_ref[...] = jnp.zeros_like(dw_ref)
            db_acc[...] = jnp.zeros_like(db_acc)

        @pl.when(i < nb)
        def _():
            u, du_dz = _gelu_and_grad(zuv_ref[:, :D_GMLP], tanh_ref[:, :D_GMLP])
            vg, dvg_dz = _gelu_and_grad(zuv_ref[:, D_GMLP:], tanh_ref[:, D_GMLP:])
            for g in range(GROUPS):
                sl = slice(g * GDIM, (g + 1) * GDIM)
                vg_g = vg[:, sl]
                rg = _rms_scale(vg_g)
                vhat = vg_g * rg
                gain = gv_ref[:, sl]
                vn = (vhat * gain).astype(BF)
                w_g = _tril_bf16(w_ref, g)
                sv = _nn(w_g, vn) + b_ref[g]
                dmix_a = dmix_ref[:, sl]
                dsv = dmix_a * u[:, sl]
                dsvb = dsv.astype(BF)
                db_acc[g] += jnp.sum(dsv, axis=-1, keepdims=True)
                dw_ref[g] += _nt(dsvb, vn)
                dvn = _tn(w_g, dsvb)
                dgv_ref[:, sl] += jnp.sum(dvn * vhat, axis=0, keepdims=True)
                dvg = _rms_bwd(dvn * gain, vg_g, rg)
                dzm_ref[:, sl] = ((dmix_a * sv) * du_dz[:, sl]).astype(BF)
                dzm_ref[:, D_GMLP + g * GDIM: D_GMLP + (g + 1) * GDIM] = (dvg * dvg_dz[:, sl]).astype(BF)

            for kh in range(N_KV):
                ksl = slice(kh * HEAD_DIM, (kh + 1) * HEAD_DIM)
                vsl = slice(D_KV + kh * HEAD_DIM, D_KV + (kh + 1) * HEAD_DIM)
                k_prev, k_cur = kvp_ref[:, ksl], kvc_ref[:, ksl]
                v_prev, v_cur = kvp_ref[:, vsl], kvc_ref[:, vsl]
                q4 = _stack_heads(q_ref, kh, 0)
                p_prev, p_cur, p_sink = prob_ref[0, kh, :, :BLOCK], prob_ref[0, kh, :, BLOCK:], psink_ref[0, kh, :, 0:1]
                do4 = _stack_heads(dmix_ref, kh, D_GMLP)
                dp_prev = _nt(do4, v_prev)
                dp_cur = _nt(do4, v_cur)
                delta = jnp.sum(p_prev * dp_prev, axis=-1, keepdims=True) + jnp.sum(p_cur * dp_cur, axis=-1, keepdims=True)
                ds_prev = p_prev * (dp_prev - delta)
                ds_cur = p_cur * (dp_cur - delta)
                dsink_acc[kh] -= p_sink * delta
                dbias_ref[kh, :, :BLOCK] += ds_prev
                dbias_ref[kh, :, BLOCK:] += ds_cur
                dsb_prev = ds_prev.astype(BF)
                dsb_cur = ds_cur.astype(BF)
                dq4 = (_nn(dsb_prev, k_prev) + _nn(dsb_cur, k_cur)) * SCALE
                for g in range(Q_PER_KV):
                    first = 2 * D_GMLP + (kh * Q_PER_KV + g) * HEAD_DIM
                    dzm_ref[:, first:first + HEAD_DIM] = dq4[g * BLOCK:(g + 1) * BLOCK].astype(BF)
                dkv_ref[:, ksl] = (carry_ref[:, ksl] + _tn(dsb_prev, q4) * SCALE).astype(BF)
                dkv_ref[:, vsl] = (carry_ref[:, vsl] + _tn(p_prev.astype(BF), do4)).astype(BF)
                carry_ref[:, ksl] = _tn(dsb_cur, q4) * SCALE
                carry_ref[:, vsl] = _tn(p_cur.astype(BF), do4)

        @pl.when(i == nb)
        def _():
            dkv_ref[...] = carry_ref[...].astype(BF)
            row = lax.broadcasted_iota(jnp.int32, (BLOCK, BLOCK), 0)
            col = lax.broadcasted_iota(jnp.int32, (BLOCK, BLOCK), 1)
            for g in range(GROUPS):
                dw_ref[g] = jnp.where(col <= row, dw_ref[g], 0.0)
                db_ref[g:g + 1, :] = jnp.sum(jnp.where(col == row, db_acc[g], 0.0), axis=0, keepdims=True)
            bucket = bucket_ref[...]
            for b in range(N_Q, REL_BUCKETS):
                dattn_ref[N_Q, b] = 0.0
            for h in range(N_Q):
                kh, rows = _head_rows(h)
                dattn_ref[N_Q, h] = jnp.sum(dsink_acc[kh, rows, :])
                dbh = dbias_ref[kh, rows, :]
                for b in range(REL_BUCKETS):
                    dattn_ref[h, b] = jnp.sum(jnp.where(bucket == b, dbh, 0.0))

    cl = lambda i: jnp.minimum(i, nb - 1)
    const = lambda shape: pl.BlockSpec(shape, lambda i: (0,) * len(shape))
    return _call(
        body,
        name="mixer_bwd",
        after=after,
        grid=(nb + 1,),
        in_specs=_mixer_specs(nb)[:7] + [
            const((BLOCK, 2 * BLOCK)),
            pl.BlockSpec((BLOCK, D_MODEL), lambda i: (cl(i), 0)),
            pl.BlockSpec((BLOCK, 2 * D_GMLP), lambda i: (cl(i), 0)),
            pl.BlockSpec((1, N_KV, ROWS4, 2 * BLOCK), lambda i: (cl(i), 0, 0, 0)),
            pl.BlockSpec((1, N_KV, ROWS4, LANES), lambda i: (cl(i), 0, 0, 0)),
        ],
        out_specs=[
            pl.BlockSpec((BLOCK, D_MAIN), lambda i: (cl(i), 0)),
            pl.BlockSpec((BLOCK, 2 * D_KV), lambda i: (jnp.maximum(i - 1, 0), 0)),
            const((1, D_GMLP)),
            const((GROUPS, BLOCK, BLOCK)),
            const((GROUPS, BLOCK)),
            pl.BlockSpec(memory_space=pltpu.SMEM),
        ],
        out_shape=[
            jax.ShapeDtypeStruct((s, D_MAIN), BF),
            jax.ShapeDtypeStruct((s, 2 * D_KV), BF),
            jax.ShapeDtypeStruct((1, D_GMLP), F32),
            jax.ShapeDtypeStruct((GROUPS, BLOCK, BLOCK), F32),
            jax.ShapeDtypeStruct((GROUPS, BLOCK), F32),
            jax.ShapeDtypeStruct((N_Q + 1, REL_BUCKETS), F32),
        ],
        scratch_shapes=[
            pltpu.VMEM((N_KV, ROWS4, 2 * BLOCK), F32),
            pltpu.VMEM((BLOCK, 2 * D_KV), F32),
            pltpu.VMEM((N_KV, ROWS4, 1), F32),
            pltpu.VMEM((GROUPS, BLOCK, 1), F32),
        ],
        compiler_params=_params(("arbitrary",)),
    )(zuv, qkv, qkv, qkv, gv, w_sp, b_sp, bucket, dmix, *saved)


def _in_bwd(x, dh1, dzm, dkv, g1, w_in_t, after=None):
    s = x.shape[0]
    tm = min(IO_TOK_TILE, s)

    def body(x_ref, dh1_ref, dzm_ref, dkv_ref, g_ref, w_ref, dx_ref, dg_ref):
        @pl.when(pl.program_id(0) == 0)
        def _():
            dg_ref[...] = jnp.zeros_like(dg_ref)

        dhn = _nn(dzm_ref[...], w_ref[:D_MAIN, :]) + _nn(dkv_ref[...], w_ref[D_MAIN:, :])
        xv = x_ref[...]
        r = _rms_scale(xv)
        dg_ref[...] += jnp.sum(dhn * (xv * r), axis=0, keepdims=True)
        dx_ref[...] = dh1_ref[...] + _rms_bwd(dhn * g_ref[...], xv, r)

    tile = lambda cols: pl.BlockSpec((tm, cols), lambda i: (i, 0))
    row = pl.BlockSpec((1, D_MODEL), lambda i: (0, 0))
    return _call(
        body,
        name="in_bwd",
        after=after,
        grid=(s // tm,),
        in_specs=[tile(D_MODEL), tile(D_MODEL), tile(D_MAIN), tile(2 * D_KV), row, pl.BlockSpec((D_IN, D_MODEL), lambda i: (0, 0))],
        out_specs=[tile(D_MODEL), row],
        out_shape=[jax.ShapeDtypeStruct((s, D_MODEL), F32), jax.ShapeDtypeStruct((1, D_MODEL), F32)],
        compiler_params=_params(("arbitrary",)),
    )(x, dh1, dzm, dkv, g1, w_in_t)


def _wgrad_in(dzm, dkv, hn):
    s = hn.shape[0]
    tm = 2 * D_KV
    n_main = D_MAIN // tm

    def body(dzm_ref, dkv_ref, hn_ref, o_ref):
        i = pl.program_id(0)

        @pl.when(i < n_main)
        def _():
            o_ref[...] = _tn(dzm_ref[...], hn_ref[...])

        @pl.when(i == n_main)
        def _():
            o_ref[...] = _tn(dkv_ref[...], hn_ref[...])

    return _call(
        body,
        name="wgrad_in",
        grid=(n_main + 1,),
        in_specs=[
            pl.BlockSpec((s, tm), lambda i: (0, jnp.minimum(i, n_main - 1))),
            pl.BlockSpec((s, tm), lambda i: (0, 0)),
            pl.BlockSpec((s, D_MODEL), lambda i: (0, 0)),
        ],
        out_specs=pl.BlockSpec((tm, D_MODEL), lambda i: (i, 0)),
        out_shape=jax.ShapeDtypeStruct((D_IN, D_MODEL), F32),
        compiler_params=_params(("arbitrary",)),
    )(dzm, dkv, hn)


def _wgrad(a, b, tm, tn, name, peer_cols=0, after=None):
    s, m = a.shape
    n = b.shape[1]

    def body(a_ref, b_ref, o_ref, at_ref):
        @pl.when(pl.program_id(1) == 0)
        def _():
            at_ref[...] = a_ref[...].astype(BF).T

        r = _nn(at_ref[...], b_ref[...])
        if peer_cols:
            for q in range(tn // peer_cols):
                o_ref[q] = r[:, q * peer_cols:(q + 1) * peer_cols]
        else:
            o_ref[...] = r

    if peer_cols:
        out_spec = pl.BlockSpec((tn // peer_cols, tm, peer_cols), lambda i, j: (j, i, 0))
        out_shape = jax.ShapeDtypeStruct((n // peer_cols, m, peer_cols), F32)
    else:
        out_spec = pl.BlockSpec((tm, tn), lambda i, j: (i, j))
        out_shape = jax.ShapeDtypeStruct((m, n), F32)
    return _call(
        body,
        name=name,
        after=after,
        grid=(m // tm, n // tn),
        in_specs=[pl.BlockSpec((s, tm), lambda i, j: (0, i)), pl.BlockSpec((s, tn), lambda i, j: (0, j))],
        out_specs=out_spec,
        out_shape=out_shape,
        scratch_shapes=[pltpu.VMEM((tm, s), BF)],
        compiler_params=_params(("arbitrary", "arbitrary")),
    )(a, b)


def _adamw_math(w, g, m, v):
    m_new = ADAM_B1 * m + (1.0 - ADAM_B1) * g
    v_new = ADAM_B2 * v + (1.0 - ADAM_B2) * jnp.square(g)
    m_hat = m_new / (1.0 - ADAM_B1 ** ADAM_STEP)
    v_hat = v_new / (1.0 - ADAM_B2 ** ADAM_STEP)
    delta = -ADAM_LR * (m_hat / (jnp.sqrt(v_hat) + ADAM_EPS) + ADAM_WD * w)
    return delta, m_new, v_new


def _final_adamw(part, recv, w, m, v, name):
    r, c = w.shape
    tr = min(r, 512)

    def body(p_ref, r_ref, w_ref, m_ref, v_ref, g_ref, d_ref, mo_ref, vo_ref):
        g = p_ref[...]
        for j in range(3):
            g = g + r_ref[j].astype(F32)
        g_ref[...] = g
        d_ref[...], mo_ref[...], vo_ref[...] = _adamw_math(w_ref[...], g, m_ref[...], v_ref[...])

    spec = pl.BlockSpec((tr, c), lambda i: (i, 0))
    return _call(
        body,
        name=name,
        grid=(r // tr,),
        in_specs=[spec, pl.BlockSpec((3, tr, c), lambda i: (0, i, 0)), spec, spec, spec],
        out_specs=[spec] * 4,
        out_shape=[jax.ShapeDtypeStruct((r, c), F32)] * 4,
        compiler_params=_params(("arbitrary",)),
    )(part, recv, w, m, v)


def _rs_sum(g, land, blocks, name):
    _, r, c = g.shape
    tr = min(r, 256)

    def body(blk_ref, g0_ref, g1_ref, g2_ref, g3_ref, l_ref, part_ref, send_ref):
        part_ref[...] = g0_ref[0] + l_ref[0]
        for j, gj_ref in enumerate((g1_ref, g2_ref, g3_ref)):
            send_ref[j] = (gj_ref[0] + l_ref[j + 1]).astype(BF)

    def pick(j):
        return pl.BlockSpec((1, tr, c), lambda i, blk: (blk[j], i, 0))

    return _call(
        body,
        name=name,
        grid_spec=pltpu.PrefetchScalarGridSpec(
            num_scalar_prefetch=1,
            grid=(r // tr,),
            in_specs=[pick(0), pick(1), pick(2), pick(3), pl.BlockSpec((4, tr, c), lambda i, blk: (0, i, 0))],
            out_specs=[pl.BlockSpec((tr, c), lambda i, blk: (i, 0)), pl.BlockSpec((3, tr, c), lambda i, blk: (0, i, 0))],
        ),
        out_shape=[jax.ShapeDtypeStruct((r, c), F32), jax.ShapeDtypeStruct((3, r, c), BF)],
        compiler_params=_params(("arbitrary",)),
    )(blocks, g, g, g, g, land)


def _final_adamw_few(parts, recvs, ws, ms, vs, name):
    n = len(ws)

    def body(*refs):
        ins, outs = refs[:5 * n], refs[5 * n:]
        for k in range(n):
            p_ref, r_ref, w_ref, m_ref, v_ref = (ins[q * n + k] for q in range(5))
            g = p_ref[...]
            for j in range(3):
                g = g + r_ref[j].astype(F32)
            outs[k][...] = g
            outs[n + k][...], outs[2 * n + k][...], outs[3 * n + k][...] = _adamw_math(w_ref[...], g, m_ref[...], v_ref[...])

    outs = _call(
        body,
        name=name,
        out_shape=[jax.ShapeDtypeStruct(w.shape, F32) for w in ws] * 4,
        compiler_params=_params(),
    )(*parts, *recvs, *ws, *ms, *vs)
    return outs[:n], outs[n:2 * n], outs[2 * n:3 * n], outs[3 * n:]


def _rs_sum_few(gs, lands, blocks, name):
    n = len(gs)

    def body(blk_ref, *refs):
        g_refs, l_refs, part_refs, send_refs = refs[:n], refs[n:2 * n], refs[2 * n:3 * n], refs[3 * n:]
        for k in range(n):
            part_refs[k][...] = g_refs[k][blk_ref[0]] + l_refs[k][0]
            for j in range(3):
                send_refs[k][j] = (g_refs[k][blk_ref[j + 1]] + l_refs[k][j + 1]).astype(BF)

    outs = _call(
        body,
        name=name,
        in_specs=[pl.BlockSpec(memory_space=pltpu.SMEM)] + [pl.BlockSpec(memory_space=pltpu.VMEM)] * (2 * n),
        out_shape=[jax.ShapeDtypeStruct(g.shape[1:], F32) for g in gs] + [jax.ShapeDtypeStruct((3,) + g.shape[1:], BF) for g in gs],
        compiler_params=_params(),
    )(blocks, *gs, *lands)
    return [(outs[k], outs[n + k]) for k in range(n)]


def _adamw_small(own, recv, g1_own, g1_recv, weights, moms, vels, after):
    n_w = len(weights)

    def body(*refs):
        own_refs, recv_refs, g1_own_ref, g1_recv_ref = refs[:5], refs[5:10], refs[10], refs[11]
        w_refs, m_refs, v_refs = refs[12:12 + n_w], refs[12 + n_w:12 + 2 * n_w], refs[12 + 2 * n_w:12 + 3 * n_w]
        outs = refs[12 + 3 * n_w:]
        loss_ref, g_refs, d_refs = outs[0], outs[1:1 + n_w], outs[1 + n_w:1 + 2 * n_w]
        mo_refs, vo_refs = outs[1 + 2 * n_w:1 + 3 * n_w], outs[1 + 3 * n_w:]
        x, y, c = lax.axis_index("x"), lax.axis_index("y"), lax.axis_index("c")

        def in_place_order(values, my_place):
            acc = None
            for place in range(len(values)):
                r = place ^ my_place
                term = values[-1]
                for q in range(len(values) - 2, -1, -1):
                    term = jnp.where(r == q, values[q], term)
                acc = term if acc is None else acc + term
            return acc

        def total(k, *index):
            index = index or (slice(None),) * (len(own_refs[k].shape) - 1)
            across = [own_refs[k][(0, *index)], recv_refs[k][(1, *index)], recv_refs[k][(0, *index)], recv_refs[k][(2, *index)]]
            return in_place_order([v.astype(F32) for v in across], 2 * x + y)

        g1 = in_place_order([g1_own_ref[0]] + [g1_recv_ref[j] for j in range(N_DEV - 1)], 4 * x + 2 * y + c)
        grads = [
            g1, total(1), total(2), total(3),
            total(4, slice(N_Q, None), slice(0, N_Q)), total(4, slice(0, N_Q), slice(None)),
            total(0, slice(0, 1), slice(None)), total(0, slice(1, 2), slice(None))]
        loss_ref[...] = total(0, slice(2, 3), slice(0, 1))
        for k in range(n_w):
            g_refs[k][...] = grads[k]
            d_refs[k][...], mo_refs[k][...], vo_refs[k][...] = _adamw_math(w_refs[k][...], grads[k], m_refs[k][...], v_refs[k][...])

    shapes = [jax.ShapeDtypeStruct(w.shape, F32) for w in weights]
    outs = _call(
        body,
        name="adamw_small",
        after=after,
        in_specs=[pl.BlockSpec(memory_space=pltpu.VMEM)] * (12 + 3 * n_w),
        out_shape=[jax.ShapeDtypeStruct((1, 1), F32)] + shapes * 4,
        compiler_params=_params(),
    )(*own, *recv, g1_own, g1_recv, *weights, *moms, *vels)
    return outs[0], outs[1:1 + n_w], outs[1 + n_w:1 + 2 * n_w], outs[1 + 2 * n_w:1 + 3 * n_w], outs[1 + 3 * n_w:]


def _place():
    x, y, c = lax.axis_index("x"), lax.axis_index("y"), lax.axis_index("c")
    return x, y, c, [(1 - x, y), (x, 1 - y), (1 - x, 1 - y)]


def _dev_index(px, py, pc):
    return 4 * px + 2 * py + pc


HBM_SPEC = pl.BlockSpec(memory_space=pltpu.HBM)
SEM_SPEC = pl.BlockSpec(memory_space=pltpu.SEMAPHORE)
ANY_SPEC = pl.BlockSpec(memory_space=pl.ANY)
DATAFLOW = pltpu.SideEffectType.DATAFLOW_SIDE_EFFECTING


def _hbm(a):
    return pltpu.with_memory_space_constraint(a, pltpu.HBM)


def _prep_weights(shards):
    k_n = len(shards)

    def body(*refs):
        ins, outs, stage, sems = refs[:k_n], refs[k_n:2 * k_n], refs[2 * k_n:3 * k_n], refs[3 * k_n]
        x, y, c, _ = _place()
        copies = []
        for k in range(k_n):
            stage[k][...] = ins[k][...].astype(BF)
            copies.append(pltpu.make_async_copy(stage[k], outs[k].at[_dev_index(x, y, c)], sems.at[k]))
            copies[k].start()
        for cp in copies:
            cp.wait()

    return _call(
        body,
        name="prep_weights",
        in_specs=[pl.BlockSpec(memory_space=pltpu.VMEM)] * k_n,
        out_specs=[ANY_SPEC] * k_n,
        out_shape=[jax.ShapeDtypeStruct((N_DEV,) + sh.shape, BF) for sh in shards],
        scratch_shapes=[pltpu.VMEM(sh.shape, BF) for sh in shards] + [pltpu.SemaphoreType.DMA((k_n,))],
        compiler_params=_params(),
    )(*shards)


def _ag_piece(land_k, block, half, peer, send_sem, recv_sem):
    ref = land_k.at[_dev_index(*block)]
    if half is not None:
        rows = land_k.shape[1] // 2
        ref = ref.at[pl.ds(half * rows, rows)]
    return pltpu.make_async_remote_copy(
        src_ref=ref, dst_ref=ref, send_sem=send_sem, recv_sem=recv_sem, device_id=peer, device_id_type=MESH)


def _ag_plan():
    x, y, c, _ = _place()
    me, sib = (x, y, c), (x, y, 1 - c)
    xn, yn, diag = (1 - x, y, c), (x, 1 - y, c), (1 - x, 1 - y, c)
    return dict(
        relay_halves=[(me, 0, xn), (me, 1, yn)],
        others=[(me, None, sib), (me, 1, xn), (me, 0, yn)],
        relays=[(xn, 0, yn), (yn, 1, xn)],
        near=[(xn, None, sib), (yn, None, sib)],
        far=[(diag, None, sib)],
    )


def _ag_stage(land, stage, send_sems, recv_sems, act):
    copies = _ag_plan()[stage]
    n = len(copies)
    for k in range(len(land)):
        for j, (block, half, peer) in enumerate(copies):
            cp = _ag_piece(land[k], block, half, peer, send_sems.at[n * k + j], recv_sems.at[n * k + j])
            if act == "start":
                cp.start()
            else:
                cp.wait_send()
                cp.wait_recv()


def _sem_shapes(*counts):
    return [pltpu.SemaphoreType.DMA((n,)) for n in counts for _ in range(2)]


def _ag_start(first, rest):
    lands = list(first) + list(rest)
    k_n, k_first = len(lands), len(first)
    k_rest = k_n - k_first

    def body(*refs):
        land = refs[:k_n]
        sems = refs[k_n:k_n + 6]
        token = refs[-1]
        x, y, c, chips = _place()
        targets = [(x, y, 1 - c)] + [(*chip, c) for chip in chips]
        for k in range(k_first):
            for j, to in enumerate(targets):
                _ag_piece(land[k], (x, y, c), None, to, sems[0].at[4 * k + j], sems[1].at[4 * k + j]).start()
        _ag_stage(land[k_first:], "relay_halves", sems[2], sems[3], "start")
        _ag_stage(land[k_first:], "others", sems[4], sems[5], "start")
        token[...] = jnp.zeros_like(token)

    outs = pl.pallas_call(
        body,
        name="ag_start",
        in_specs=[HBM_SPEC] * k_n,
        out_specs=(*[SEM_SPEC] * 6, *[HBM_SPEC] * k_n, pl.BlockSpec(memory_space=pltpu.VMEM)),
        out_shape=(*_sem_shapes(4 * k_first, 2 * k_rest, 3 * k_rest),
                   *[pltpu.HBM(a.shape, a.dtype) for a in lands], jax.ShapeDtypeStruct((8, LANES), F32)),
        input_output_aliases={k: 6 + k for k in range(k_n)},
        compiler_params=pltpu.CompilerParams(has_side_effects=DATAFLOW),
    )(*[_hbm(a) for a in lands])
    flying = list(outs[6:6 + k_n])
    return (outs[0], outs[1], flying[:k_first]), (outs[2:6], flying[k_first:]), outs[-1]


def _ag_split_call(lands, waits, starts, after, name):
    k_n = len(lands)
    plan_sizes = dict(relay_halves=2, others=3, relays=2, near=2, far=1)
    n_in, n_out = 2 * len(waits), 2 * len(starts)

    def body(*refs):
        land = refs[:k_n]
        in_sems = refs[k_n:k_n + n_in]
        out_sems, token = refs[len(refs) - 1 - n_out:len(refs) - 1], refs[-1]
        for w, (stage, _, _) in enumerate(waits):
            _ag_stage(land, stage, in_sems[2 * w], in_sems[2 * w + 1], "wait")
            if w < len(starts):
                _ag_stage(land, starts[w], out_sems[2 * w], out_sems[2 * w + 1], "start")
        token[...] = jnp.zeros_like(token)

    outs = pl.pallas_call(
        body,
        name=name,
        in_specs=[HBM_SPEC] * k_n + [SEM_SPEC] * n_in + [ANY_SPEC],
        out_specs=(*[HBM_SPEC] * k_n, *[SEM_SPEC] * n_out, pl.BlockSpec(memory_space=pltpu.VMEM)),
        out_shape=(*[pltpu.HBM(a.shape, a.dtype) for a in lands], *_sem_shapes(*[plan_sizes[s] * k_n for s in starts]),
                   jax.ShapeDtypeStruct((8, LANES), F32)),
        input_output_aliases={k: k for k in range(k_n)},
        compiler_params=pltpu.CompilerParams(has_side_effects=DATAFLOW),
    )(*lands, *[s for _, a, b in waits for s in (a, b)], after)
    return list(outs[:k_n]), list(outs[k_n:k_n + n_out]), outs[-1]


def _ag_mid(lands, send_sems, recv_sems, after, name):
    k_n = len(lands)

    def body(*refs):
        land = refs[:k_n]
        send1, recv1 = refs[k_n], refs[k_n + 1]
        fwd_send, fwd_recv = refs[-2], refs[-1]
        x, y, c, chips = _place()
        sources = [(x, y, 1 - c)] + [(*chip, c) for chip in chips]
        for k in range(k_n):
            mine = land[k].at[_dev_index(x, y, c)]
            for j, frm in enumerate(sources):
                got = land[k].at[_dev_index(*frm)]
                cp = pltpu.make_async_remote_copy(
                    src_ref=mine, dst_ref=got, send_sem=send1.at[4 * k + j], recv_sem=recv1.at[4 * k + j], device_id=frm, device_id_type=MESH)
                cp.wait_send()
                cp.wait_recv()
                if j >= 1:
                    pltpu.make_async_remote_copy(
                        src_ref=got, dst_ref=got, send_sem=fwd_send.at[3 * k + j - 1], recv_sem=fwd_recv.at[3 * k + j - 1],
                        device_id=(x, y, 1 - c), device_id_type=MESH).start()

    outs = pl.pallas_call(
        body,
        name=name,
        in_specs=[HBM_SPEC] * k_n + [SEM_SPEC, SEM_SPEC, ANY_SPEC],
        out_specs=(*[HBM_SPEC] * k_n, SEM_SPEC, SEM_SPEC),
        out_shape=(*[pltpu.HBM(a.shape, a.dtype) for a in lands], pltpu.SemaphoreType.DMA((3 * k_n,)), pltpu.SemaphoreType.DMA((3 * k_n,))),
        input_output_aliases={k: k for k in range(k_n)},
        compiler_params=pltpu.CompilerParams(has_side_effects=DATAFLOW),
    )(*lands, send_sems, recv_sems, after)
    return list(outs[:k_n]), outs[-2], outs[-1]


def _ag_end(lands, fwd_send, fwd_recv, name):
    k_n = len(lands)

    def body(*refs):
        land = refs[:k_n]
        fsend, frecv = refs[k_n], refs[k_n + 1]
        x, y, c, chips = _place()
        for k in range(k_n):
            for j, chip in enumerate(chips):
                cp = pltpu.make_async_remote_copy(
                    src_ref=land[k].at[_dev_index(*chip, c)], dst_ref=land[k].at[_dev_index(*chip, 1 - c)],
                    send_sem=fsend.at[3 * k + j], recv_sem=frecv.at[3 * k + j], device_id=(x, y, 1 - c), device_id_type=MESH)
                cp.wait_send()
                cp.wait_recv()

    outs = pl.pallas_call(
        body,
        name=name,
        in_specs=[HBM_SPEC] * k_n + [SEM_SPEC, SEM_SPEC],
        out_specs=tuple([HBM_SPEC] * k_n),
        out_shape=tuple(pltpu.HBM(a.shape, a.dtype) for a in lands),
        input_output_aliases={k: k for k in range(k_n)},
        compiler_params=pltpu.CompilerParams(has_side_effects=DATAFLOW),
    )(*lands, fwd_send, fwd_recv)
    return list(outs)


def _chips4():
    x, y, c, others = _place()
    return x, y, c, [(x, y)] + others


def _route_sibling(j):
    x, y, c, chips = _chips4()
    return _dev_index(*chips[j], 1 - c), j, (x, y, 1 - c)


def _route_chips(j):
    x, y, c, chips = _chips4()
    return j, j, (*chips[j + 1], c)


def _route_sibling_whole(j):
    x, y, c, _ = _chips4()
    return 0, 0, (x, y, 1 - c)


def _route_chips_whole(j):
    x, y, c, chips = _chips4()
    return 0, j, (*chips[j + 1], c)


def _route_everyone(j):
    x, y, c, _ = _chips4()
    flip = [(j + 1) >> 2 & 1, (j + 1) >> 1 & 1, (j + 1) & 1]
    return 0, j, tuple(1 - v if f else v for v, f in zip((x, y, c), flip))


def _xchg_copies(routes, src, dst, send_sems, recv_sems):
    copies, sem = [], 0
    for k, (route, n) in enumerate(routes):
        for j in range(n):
            si, di, peer = route(j)
            copies.append(pltpu.make_async_remote_copy(
                src_ref=src[k].at[si], dst_ref=dst[k].at[di], send_sem=send_sems.at[sem], recv_sem=recv_sems.at[sem],
                device_id=peer, device_id_type=MESH))
            sem += 1
    return copies


def _xchg_start(srcs, slot_shapes, routes, name):
    k_n = len(srcs)
    n_sem = sum(n for _, n in routes)
    dsts = [lax.empty((n,) + tuple(sh), a.dtype) for sh, a, (_, n) in zip(slot_shapes, srcs, routes)]

    def body(*refs):
        src, dst = refs[:k_n], refs[k_n:2 * k_n]
        send_sems, recv_sems, token = refs[2 * k_n], refs[2 * k_n + 1], refs[-1]
        for cp in _xchg_copies(routes, src, dst, send_sems, recv_sems):
            cp.start()
        token[...] = jnp.zeros_like(token)

    arrays = list(srcs) + dsts
    outs = pl.pallas_call(
        body,
        name=name,
        in_specs=[HBM_SPEC] * (2 * k_n),
        out_specs=(SEM_SPEC, SEM_SPEC, *[HBM_SPEC] * (2 * k_n), pl.BlockSpec(memory_space=pltpu.VMEM)),
        out_shape=(pltpu.SemaphoreType.DMA((n_sem,)), pltpu.SemaphoreType.DMA((n_sem,)),
                   *[pltpu.HBM(a.shape, a.dtype) for a in arrays], jax.ShapeDtypeStruct((8, LANES), F32)),
        input_output_aliases={i: 2 + i for i in range(2 * k_n)},
        compiler_params=pltpu.CompilerParams(has_side_effects=DATAFLOW),
    )(*[_hbm(a) for a in arrays])
    return outs[0], outs[1], list(outs[2:2 + k_n]), list(outs[2 + k_n:2 + 2 * k_n]), outs[-1]


def _xchg_wait(send_sems, recv_sems, srcs, dsts, routes, after, name):
    k_n = len(srcs)

    def body(*refs):
        src, dst = refs[:k_n], refs[k_n:2 * k_n]
        for cp in _xchg_copies(routes, src, dst, refs[2 * k_n], refs[2 * k_n + 1]):
            cp.wait_send()
            cp.wait_recv()

    arrays = list(srcs) + list(dsts)
    outs = pl.pallas_call(
        body,
        name=name,
        in_specs=[HBM_SPEC] * (2 * k_n) + [SEM_SPEC, SEM_SPEC, ANY_SPEC],
        out_specs=tuple([HBM_SPEC] * (2 * k_n)),
        out_shape=tuple(pltpu.HBM(a.shape, a.dtype) for a in arrays),
        input_output_aliases={i: i for i in range(2 * k_n)},
        compiler_params=pltpu.CompilerParams(has_side_effects=DATAFLOW),
    )(*arrays, send_sems, recv_sems, after)
    return list(outs[:k_n]), list(outs[k_n:])


SMALL = ("norm1_gain", "gmlp_v_gain", "w_spatial", "b_spatial", "attn_sinks", "rel_bias_table", "norm2_gain", "final_gain")
LANES = 128


def _swap_start(grads, smalls, tag):
    srcs = list(grads) + [a[None] for a in smalls]
    shapes = [g.shape[1:] for g in grads] + [a.shape for a in smalls]
    routes = [(_route_sibling, 4)] * len(grads) + [(_route_sibling_whole, 1)] * len(smalls)
    return _xchg_start(srcs, shapes, routes, f"rs_{tag}_swap_start"), routes, len(grads)


def _swap_sums(swap, names, after, tag, few=0, small_dtypes=None):
    (send1, recv1, src1, land1, _), routes, n_rs = swap
    x, y, c, chips = _chips4()
    blocks = jnp.stack([_dev_index(*chip, c) for chip in chips]).astype(jnp.int32)
    src1, land1 = _xchg_wait(send1, recv1, src1, land1, routes, after, f"rs_{tag}_swap_wait")
    n_tiled = n_rs - few
    sums = [_rs_sum(g, land, blocks, f"rs_sum_{n}") for g, land, n in zip(src1[:n_tiled], land1[:n_tiled], names)]
    if few:
        sums += _rs_sum_few(src1[n_tiled:n_rs], land1[n_tiled:n_rs], blocks, f"rs_{tag}_sum")
    if len(src1) == n_rs:
        return sums, []
    return sums, _pair_sum(src1[n_rs:], land1[n_rs:], small_dtypes or [F32] * (len(src1) - n_rs), f"rs_{tag}_sum_small")


def _pair_sum(mine, theirs, dtypes, name):
    def body(*refs):
        n = len(refs) // 3
        for k in range(n):
            refs[2 * n + k][...] = (refs[k][...] + refs[n + k][...]).astype(dtypes[k])

    return _call(
        body,
        name=name,
        out_shape=[jax.ShapeDtypeStruct(a.shape, dt) for a, dt in zip(mine, dtypes)],
        compiler_params=_params(),
    )(*mine, *theirs)


def _chips_start(sums, small_sums, tag, everyone=()):
    sends = [ps[1] for ps in sums] + list(small_sums) + [a[None] for a in everyone]
    routes = [(_route_chips, 3)] * len(sums) + [(_route_chips_whole, 3)] * len(small_sums) + [(_route_everyone, 7)] * len(everyone)
    return _xchg_start(sends, [a.shape[1:] for a in sends], routes, f"rs_{tag}_chips_start"), routes


def _chips_wait(chips, after, tag):
    (send2, recv2, src2, land2, _), routes = chips
    return _xchg_wait(send2, recv2, src2, land2, routes, after, f"rs_{tag}_chips_wait")


def kernel(x, p, norm1_gain, w_in, gmlp_v_gain, w_spatial, b_spatial, attn_sinks, rel_bias_table, w_out, norm2_gain, w_ff1, w_ff2, w_ple_proj, w_ple_gate, final_gain, loss_target, m_norm1_gain, m_w_in, m_gmlp_v_gain, m_w_spatial, m_b_spatial, m_attn_sinks, m_rel_bias_table, m_w_out, m_norm2_gain, m_w_ff1, m_w_ff2, m_w_ple_proj, m_w_ple_gate, m_final_gain, v_norm1_gain, v_w_in, v_gmlp_v_gain, v_w_spatial, v_b_spatial, v_attn_sinks, v_rel_bias_table, v_w_out, v_norm2_gain, v_w_ff1, v_w_ff2, v_w_ple_proj, v_w_ple_gate, v_final_gain):
    args = dict(locals())
    s = x.shape[1]
    big = ("w_in", "w_out", "w_ff1", "w_ff2", "w_ple_proj", "w_ple_gate")

    x2, p2, t2 = x.reshape(s, D_MODEL), p.reshape(s, PLE_DIM), loss_target.reshape(s, D_MODEL)
    g1, gv, w_sp, b_sp, sinks, table, g2, gf = (args[n] for n in SMALL)
    bucket = jnp.asarray(_bucket_table())
    b_col = b_sp.reshape(GROUPS, BLOCK, 1)

    def shard(name):
        return args[name][0].T if name.endswith("w_in") else args[name][0]

    lands = _prep_weights([shard(n) for n in big])
    (send_in, recv_in, fly_in), (rest_sems, fly_rest), token = _ag_start(lands[:1], lands[1:])
    mid_in, fwd_send_in, fwd_recv_in = _ag_mid(fly_in, send_in, recv_in, token, "ag_mid_w_in")
    g_in = _ag_end(mid_in, fwd_send_in, fwd_recv_in, "ag_end_w_in")[0]
    full_in = g_in.reshape(D_IN, D_MODEL)

    zuv, qkv, hn1 = _in_proj(x2, g1, full_in)
    fly_rest, relay_sems, relayed = _ag_split_call(
        fly_rest, [("relay_halves", *rest_sems[:2])], ["relays"], zuv, "ag_relay")
    mix, *saved = _mixer_fwd(zuv, qkv, gv, w_sp[0], b_col, sinks, table.T, bucket, after=relayed)
    fly_rest, fwd_sems, _ = _ag_split_call(
        fly_rest, [("others", *rest_sems[2:]), ("relays", *relay_sems)], ["near", "far"], mix, "ag_mid_rest")
    g_out, g_ff1, g_ff2, g_proj, g_gate = _ag_split_call(
        fly_rest, [("near", *fwd_sems[:2]), ("far", *fwd_sems[2:])], [], mix, "ag_end_rest")[0]
    full_out, full_ff2, full_gate = g_out.reshape(D_MODEL, D_MODEL), g_ff2.reshape(D_FF, D_MODEL), g_gate.reshape(D_MODEL, D_MODEL)
    full_proj = g_proj.transpose(1, 0, 2).reshape(PLE_DIM, D_MODEL)

    tail_small, dh1, dmix, hn2, a, df, dh2, gw_out, gw_proj, gw_gate = _tail(
        x2, mix, p2, t2, g2, gf.reshape(1, D_MODEL), full_out, g_ff1, full_ff2, full_gate, full_proj)
    gw_out, gw_gate = (g.reshape(N_DEV, D_MODEL // N_DEV, D_MODEL) for g in (gw_out, gw_gate))

    few_names = ("w_out", "w_ple_proj", "w_ple_gate")
    ff1_names = ("w_ff1",) + few_names
    gw_ff1 = _wgrad(hn2, df, D_MODEL, 2 * D_FF // N_DEV, "wgrad_ff1", peer_cols=D_FF // N_DEV)
    ff1_swap = _swap_start([gw_ff1, gw_out, gw_proj, gw_gate], [tail_small], "ff1")
    gw_ff2 = _wgrad(a, dh2, 1024, D_MODEL, "wgrad_ff2", after=ff1_swap[0][4]).reshape(N_DEV, D_FF // N_DEV, D_MODEL)
    ff2_swap = _swap_start([gw_ff2], [], "ff2")
    ff1_sums, ff1_small = _swap_sums(ff1_swap, ff1_names, ff2_swap[0][4], "ff1", few=len(few_names))
    ff1_chips = _chips_start(ff1_sums, ff1_small, "ff1")
    ff2_sums, _ = _swap_sums(ff2_swap, ("w_ff2",), ff1_chips[0][4], "ff2")
    ff2_chips = _chips_start(ff2_sums, [], "ff2")

    dzm, dkv, d_gv, d_wsp, d_bsp, d_attn = _mixer_bwd(
        zuv, qkv, dmix, saved, gv, w_sp[0], b_col, bucket, after=ff2_chips[0][4])
    gw_in = _wgrad_in(dzm, dkv, hn1).reshape(N_DEV, D_IN // N_DEV, D_MODEL)
    in_swap = _swap_start([gw_in], [d_gv, d_wsp, d_bsp, d_attn], "in")

    grads, deltas, new_m, new_v = {}, {}, {}, {}
    small_own, small_recv = [], []

    def finish(chips, sums, names, tag, after):
        sent, recvs = _chips_wait(chips, after, tag)
        small_own.extend(sent[len(names):])
        small_recv.extend(recvs[len(names):])
        tiled = [n for n in names if n not in few_names]
        done = [_final_adamw(part, recv, shard(n), shard("m_" + n), shard("v_" + n), "adamw_" + n)
                for n, (part, _), recv in zip(tiled, sums, recvs)]
        few = names[len(tiled):]
        if few:
            done += zip(*_final_adamw_few(
                [ps[0] for ps in sums[len(tiled):]], recvs[len(tiled):len(names)], [shard(n) for n in few],
                [shard("m_" + n) for n in few], [shard("v_" + n) for n in few], "adamw_few"))
        for n, results in zip(names, done):
            for dst, arr in zip((grads, deltas, new_m, new_v), results):
                dst[n] = (arr.T if n == "w_in" else arr)[None]
        return results[1]

    after = finish(ff1_chips, ff1_sums, ff1_names, "ff1", in_swap[0][4])
    in_sums, in_small = _swap_sums(in_swap, ("w_in",), after, "in", small_dtypes=[F32, BF, F32, F32])
    in_chips = _chips_start(in_sums, in_small, "in")
    dx, d_g1 = _in_bwd(x2, dh1, dzm, dkv, g1, full_in, after=in_chips[0][4])
    grad_x = dx.reshape(x.shape)
    g1_chips = _chips_start([], [], "g1", everyone=[d_g1])
    after = finish(ff2_chips, ff2_sums, ("w_ff2",), "ff2", g1_chips[0][4])
    after = finish(in_chips, in_sums, ("w_in",), "in", after)
    g1_own, g1_recv = _chips_wait(g1_chips, after, "g1")

    views = {"w_spatial": (GROUPS, BLOCK, BLOCK), "b_spatial": (GROUPS, BLOCK), "final_gain": (1, D_MODEL)}
    def view(name, base):
        return args[name].T if base == "rel_bias_table" else args[name].reshape(views.get(base, args[name].shape))

    small_in = [[view(pre + n, n) for n in SMALL] for pre in ("", "m_", "v_")]
    loss, *small_out = _adamw_small(small_own, small_recv, g1_own[0], g1_recv[0], *small_in, after=after)
    for dst, arrays in zip((grads, deltas, new_m, new_v), small_out):
        for n, arr in zip(SMALL, arrays):
            dst[n] = arr.T if n == "rel_bias_table" else arr.reshape(args[n].shape)
    loss = loss[0, 0]

    order = ("norm1_gain", "w_in", "gmlp_v_gain", "w_spatial", "b_spatial", "attn_sinks", "rel_bias_table", "w_out",
             "norm2_gain", "w_ff1", "w_ff2", "w_ple_proj", "w_ple_gate", "final_gain")
    return (loss, grad_x, *[grads[n] for n in order], *[deltas[n] for n in order],
            *[new_m[n] for n in order], *[new_v[n] for n in order])
```

```python
import math

import numpy as np
import jax
import jax.numpy as jnp
from jax import lax
from jax.experimental import pallas as pl
from jax.experimental.pallas import tpu as pltpu

F32 = jnp.float32
BF = jnp.bfloat16
MESH = pl.DeviceIdType.MESH
N_DEV = 8

D_MODEL = 1024
PLE_DIM = 256
D_GMLP = 512
GROUPS = 4
GDIM = 128
BLOCK = 128
D_ATTN = 512
HEAD_DIM = 64
N_Q = 8
Q_PER_KV = 4
N_KV = N_Q // Q_PER_KV
ROWS4 = Q_PER_KV * BLOCK
D_KV = 128
D_FF = 4096
D_IN = 1792
D_MAIN = 2 * D_GMLP + D_ATTN
REL_BUCKETS = 32
EPS = 1e-6
NEG_INF = -1e30
SCALE = HEAD_DIM ** -0.5
GELU_C = math.sqrt(2.0 / math.pi)
GELU_A = 0.044715

ADAM_LR = 0.001
ADAM_B1 = 0.9
ADAM_B2 = 0.999
ADAM_EPS = 1e-08
ADAM_WD = 0.01
ADAM_STEP = 10

V7X_VMEM_LIMIT = 61440 * 1024
TOK_TILE = 256
IO_TOK_TILE = 512


def _call(body, after=None, **kw):
    if after is None:
        return pl.pallas_call(body, **kw)
    n_in = len(kw["in_specs"])

    def ordered(*refs):
        body(*refs[:n_in], *refs[n_in + 1:])

    kw["in_specs"] = list(kw["in_specs"]) + [pl.BlockSpec(memory_space=pl.ANY)]
    fn = pl.pallas_call(ordered, **kw)
    return lambda *operands: fn(*operands, after)


def _params(sem=None):
    if sem is None:
        return pltpu.CompilerParams(vmem_limit_bytes=V7X_VMEM_LIMIT)
    return pltpu.CompilerParams(dimension_semantics=sem, vmem_limit_bytes=V7X_VMEM_LIMIT)


def _nn(a, b):
    return jnp.dot(a, b, preferred_element_type=F32)


def _nt(a, b):
    return lax.dot_general(a, b, (((1,), (1,)), ((), ())), preferred_element_type=F32)


def _tn(a, b):
    return lax.dot_general(a, b, (((0,), (0,)), ((), ())), preferred_element_type=F32)


def _gelu_tanh(x):
    return jnp.tanh(GELU_C * (x + GELU_A * (x * x * x)))


def _gelu(x, t):
    return x * (0.5 * (1.0 + t))


def _gelu_and_grad(x, t):
    cdf = 0.5 * (1.0 + t)
    return x * cdf, cdf + 0.5 * x * (1.0 - t * t) * (GELU_C * (1.0 + 3.0 * GELU_A * (x * x)))


def _rms_scale(x):
    return lax.rsqrt(jnp.mean(x * x, axis=-1, keepdims=True) + EPS)


def _rms_bwd(dxn, x, r):
    return r * dxn - x * ((r * r * r) * jnp.mean(dxn * x, axis=-1, keepdims=True))


def _bucket_table():
    a = np.arange(BLOCK)[:, None]
    j = np.arange(2 * BLOCK)[None, :]
    n = BLOCK + a - j
    valid = (n >= 0) & (n < BLOCK)
    nc = np.maximum(n, 0)
    max_exact = REL_BUCKETS // 2
    nf = np.maximum(nc, 1).astype(np.float32)
    large = max_exact + (
        np.log(nf / np.float32(max_exact)) / np.float32(math.log(BLOCK / max_exact)) * np.float32(REL_BUCKETS - max_exact)
    ).astype(np.int32)
    large = np.minimum(large, REL_BUCKETS - 1)
    bucket = np.where(nc < max_exact, nc, large)
    return np.where(valid, bucket, -1).astype(np.int32)


def _in_proj(x, g1, w_in_t):
    s = x.shape[0]
    tm = min(IO_TOK_TILE, s)

    def body(x_ref, g_ref, w_ref, zuv_ref, qkv_ref, hn_ref):
        xv = x_ref[...]
        hn = ((xv * _rms_scale(xv)) * g_ref[...]).astype(BF)
        hn_ref[...] = hn
        z = _nt(hn, w_ref[...])
        zuv_ref[...] = z[:, : 2 * D_GMLP]
        qkv_ref[...] = z[:, 2 * D_GMLP:].astype(BF)

    return _call(
        body,
        name="in_proj",
        grid=(s // tm,),
        in_specs=[
            pl.BlockSpec((tm, D_MODEL), lambda i: (i, 0)),
            pl.BlockSpec((1, D_MODEL), lambda i: (0, 0)),
            pl.BlockSpec((D_IN, D_MODEL), lambda i: (0, 0)),
        ],
        out_specs=[
            pl.BlockSpec((tm, 2 * D_GMLP), lambda i: (i, 0)),
            pl.BlockSpec((tm, D_ATTN + 2 * D_KV), lambda i: (i, 0)),
            pl.BlockSpec((tm, D_MODEL), lambda i: (i, 0)),
        ],
        out_shape=[
            jax.ShapeDtypeStruct((s, 2 * D_GMLP), F32),
            jax.ShapeDtypeStruct((s, D_ATTN + 2 * D_KV), BF),
            jax.ShapeDtypeStruct((s, D_MODEL), BF),
        ],
        compiler_params=_params(("arbitrary",)),
    )(x, g1, w_in_t)


def _head_rows(h):
    kh, g = divmod(h, Q_PER_KV)
    return kh, slice(g * BLOCK, (g + 1) * BLOCK)


def _build_bias(bias_ref, bucket_ref, table_ref):
    bucket = bucket_ref[...]
    for h in range(N_Q):
        acc = jnp.zeros((BLOCK, 2 * BLOCK), F32)
        for b in range(REL_BUCKETS):
            acc = jnp.where(bucket == b, table_ref[h, b], acc)
        kh, rows = _head_rows(h)
        bias_ref[kh, rows, :] = acc


def _window_masks(i):
    row = lax.broadcasted_iota(jnp.int32, (ROWS4, BLOCK), 0) & (BLOCK - 1)
    col = lax.broadcasted_iota(jnp.int32, (ROWS4, BLOCK), 1)
    return (col > row) & (i > 0), col <= row


def _stack_heads(ref, kh, offset):
    first = offset + kh * Q_PER_KV * HEAD_DIM
    return jnp.concatenate(
        [ref[:, first + g * HEAD_DIM: first + (g + 1) * HEAD_DIM].astype(BF) for g in range(Q_PER_KV)], axis=0)


def _stack_sinks(sink_ref, kh):
    return jnp.concatenate([jnp.full((BLOCK, 1), sink_ref[0, kh * Q_PER_KV + g], F32) for g in range(Q_PER_KV)], axis=0)


def _tril_bf16(w_ref, g):
    row = lax.broadcasted_iota(jnp.int32, (BLOCK, BLOCK), 0)
    col = lax.broadcasted_iota(jnp.int32, (BLOCK, BLOCK), 1)
    return jnp.where(col <= row, w_ref[g], 0.0).astype(BF)


def _attn_probs(q_h, k_prev, k_cur, bias_h, sink, valid_prev, valid_cur):
    l_prev = jnp.where(valid_prev, _nt(q_h, k_prev) * SCALE + bias_h[:, :BLOCK], NEG_INF)
    l_cur = jnp.where(valid_cur, _nt(q_h, k_cur) * SCALE + bias_h[:, BLOCK:], NEG_INF)
    m = jnp.maximum(jnp.maximum(jnp.max(l_prev, axis=-1, keepdims=True), jnp.max(l_cur, axis=-1, keepdims=True)), sink)
    e_prev = jnp.exp(l_prev - m)
    e_cur = jnp.exp(l_cur - m)
    e_sink = jnp.exp(sink - m)
    denom = jnp.sum(e_prev, axis=-1, keepdims=True) + jnp.sum(e_cur, axis=-1, keepdims=True) + e_sink
    return e_prev / denom, e_cur / denom, e_sink / denom


def _mixer_specs(nb):
    cl = lambda i: jnp.minimum(i, nb - 1)
    return [
        pl.BlockSpec((BLOCK, 2 * D_GMLP), lambda i: (cl(i), 0)),
        pl.BlockSpec((BLOCK, D_ATTN), lambda i: (cl(i), 0)),
        pl.BlockSpec((BLOCK, 2 * D_KV), lambda i: (cl(i), D_ATTN // (2 * D_KV))),
        pl.BlockSpec((BLOCK, 2 * D_KV), lambda i: (jnp.maximum(cl(i) - 1, 0), D_ATTN // (2 * D_KV))),
        pl.BlockSpec((1, D_GMLP), lambda i: (0, 0)),
        pl.BlockSpec((GROUPS, BLOCK, BLOCK), lambda i: (0, 0, 0)),
        pl.BlockSpec((GROUPS, BLOCK, 1), lambda i: (0, 0, 0)),
        pl.BlockSpec(memory_space=pltpu.SMEM),
        pl.BlockSpec(memory_space=pltpu.SMEM),
        pl.BlockSpec((BLOCK, 2 * BLOCK), lambda i: (0, 0)),
    ]


def _mixer_fwd(zuv, qkv, gv, w_sp, b_sp, sinks, table, bucket, after=None):
    s = zuv.shape[0]
    nb = s // BLOCK

    def body(zuv_ref, q_ref, kvc_ref, kvp_ref, gv_ref, w_ref, b_ref, sink_ref, table_ref, bucket_ref,
             mix_ref, tanh_ref, prob_ref, psink_ref, bias_ref):
        i = pl.program_id(0)

        @pl.when(i == 0)
        def _():
            _build_bias(bias_ref, bucket_ref, table_ref)

        t = _gelu_tanh(zuv_ref[...])
        tanh_ref[...] = t
        u = _gelu(zuv_ref[:, :D_GMLP], t[:, :D_GMLP])
        vg = _gelu(zuv_ref[:, D_GMLP:], t[:, D_GMLP:])
        for g in range(GROUPS):
            sl = slice(g * GDIM, (g + 1) * GDIM)
            vg_g = vg[:, sl]
            vn = ((vg_g * _rms_scale(vg_g)) * gv_ref[:, sl]).astype(BF)
            sv = _nn(_tril_bf16(w_ref, g), vn) + b_ref[g]
            mix_ref[:, sl] = (u[:, sl] * sv).astype(BF)

        valid_prev, valid_cur = _window_masks(i)
        for kh in range(N_KV):
            ksl = slice(kh * HEAD_DIM, (kh + 1) * HEAD_DIM)
            vsl = slice(D_KV + kh * HEAD_DIM, D_KV + (kh + 1) * HEAD_DIM)
            q4 = _stack_heads(q_ref, kh, 0)
            p_prev, p_cur, p_sink = _attn_probs(
                q4, kvp_ref[:, ksl], kvc_ref[:, ksl], bias_ref[kh], _stack_sinks(sink_ref, kh), valid_prev, valid_cur)
            prob_ref[0, kh, :, :BLOCK] = p_prev
            prob_ref[0, kh, :, BLOCK:] = p_cur
            psink_ref[0, kh] = jnp.broadcast_to(p_sink, (ROWS4, LANES))
            o4 = _nn(p_prev.astype(BF), kvp_ref[:, vsl]) + _nn(p_cur.astype(BF), kvc_ref[:, vsl])
            for g in range(Q_PER_KV):
                first = D_GMLP + (kh * Q_PER_KV + g) * HEAD_DIM
                mix_ref[:, first:first + HEAD_DIM] = o4[g * BLOCK:(g + 1) * BLOCK].astype(BF)

    return _call(
        body,
        name="mixer_fwd",
        after=after,
        grid=(nb,),
        in_specs=_mixer_specs(nb),
        out_specs=[
            pl.BlockSpec((BLOCK, D_MODEL), lambda i: (i, 0)),
            pl.BlockSpec((BLOCK, 2 * D_GMLP), lambda i: (i, 0)),
            pl.BlockSpec((1, N_KV, ROWS4, 2 * BLOCK), lambda i: (i, 0, 0, 0)),
            pl.BlockSpec((1, N_KV, ROWS4, LANES), lambda i: (i, 0, 0, 0)),
        ],
        out_shape=[
            jax.ShapeDtypeStruct((s, D_MODEL), BF),
            jax.ShapeDtypeStruct((s, 2 * D_GMLP), F32),
            jax.ShapeDtypeStruct((nb, N_KV, ROWS4, 2 * BLOCK), F32),
            jax.ShapeDtypeStruct((nb, N_KV, ROWS4, LANES), F32),
        ],
        scratch_shapes=[pltpu.VMEM((N_KV, ROWS4, 2 * BLOCK), F32)],
        compiler_params=_params(("arbitrary",)),
    )(zuv, qkv, qkv, qkv, gv, w_sp, b_sp, sinks, table, bucket)


def _tail(x, mix, p, t, g2, gf, w_out, w_ff1, w_ff2, w_gate, w_proj):
    s = x.shape[0]
    tm = min(TOK_TILE, s)
    n_ff = w_ff1.shape[0]
    fc = D_FF // n_ff
    pc = D_MODEL // N_DEV

    def body(x_ref, mix_ref, p_ref, t_ref, g2_ref, gf_ref, wo_ref, w1_ref, w2_ref, wg_ref, wp_ref,
             small_ref, dh1_ref, dmix_ref, hn2_ref, a_ref, df_ref, dh2_ref, gwo_ref, gwp_ref, gwg_ref, f_ref):
        i = pl.program_id(0)

        @pl.when(i == 0)
        def _():
            small_ref[...] = jnp.zeros_like(small_ref)
            gwo_ref[...] = jnp.zeros_like(gwo_ref)
            gwp_ref[...] = jnp.zeros_like(gwp_ref)
            gwg_ref[...] = jnp.zeros_like(gwg_ref)

        h1 = x_ref[...] + _nn(mix_ref[...], wo_ref[...])
        dh1_ref[...] = h1
        r2 = _rms_scale(h1)
        hn2_ref[...] = ((h1 * r2) * g2_ref[...]).astype(BF)
        h2 = h1
        for c in range(n_ff):
            f = _nn(hn2_ref[...], w1_ref[c])
            f_ref[:, c * fc:(c + 1) * fc] = f
            a = jnp.square(jnp.maximum(f, 0.0)).astype(BF)
            a_ref[:, c * fc:(c + 1) * fc] = a
            h2 = h2 + _nn(a, w2_ref[c * fc:(c + 1) * fc, :])
        h2b = h2.astype(BF)
        gate = jax.nn.sigmoid(_nn(h2b, wg_ref[...]))
        pb = p_ref[...].astype(BF)
        pp = _nn(pb, wp_ref[...])
        h3 = h2 + gate * pp
        rf = _rms_scale(h3)
        gf_v = gf_ref[...]
        err = (h3 * rf) * gf_v - t_ref[...]
        small_ref[2:3, :] += jnp.sum(jnp.sum(err * err, axis=-1, keepdims=True), axis=0, keepdims=True) * (0.5 / D_MODEL)

        dy = err * (1.0 / D_MODEL)
        small_ref[1:2, :] += jnp.sum(dy * (h3 * rf), axis=0, keepdims=True)
        dh3 = _rms_bwd(dy * gf_v, h3, rf)
        gw_proj = _tn(pb, (dh3 * gate).astype(BF))
        for q in range(N_DEV):
            gwp_ref[q] += gw_proj[:, q * pc:(q + 1) * pc]
        dgl = ((dh3 * pp) * (gate * (1.0 - gate))).astype(BF)
        gwg_ref[...] += _tn(h2b, dgl)
        dh2 = dh3 + _nt(dgl, wg_ref[...])
        dmix_ref[...] = dh2
        dh2_ref[...] = dh2.astype(BF)
        dhn2 = jnp.zeros((tm, D_MODEL), F32)
        for c in range(n_ff):
            da = _nt(dh2_ref[...], w2_ref[c * fc:(c + 1) * fc, :])
            df = (da * (2.0 * jnp.maximum(f_ref[:, c * fc:(c + 1) * fc], 0.0))).astype(BF)
            df_ref[:, c * fc:(c + 1) * fc] = df
            dhn2 = dhn2 + _nt(df, w1_ref[c])
        h1 = dh1_ref[...]
        small_ref[0:1, :] += jnp.sum(dhn2 * (h1 * r2), axis=0, keepdims=True)
        dh1 = dmix_ref[...] + _rms_bwd(dhn2 * g2_ref[...], h1, r2)
        dh1_ref[...] = dh1
        dh1b = dh1.astype(BF)
        gwo_ref[...] += _tn(mix_ref[...], dh1b)
        dmix_ref[...] = _nt(dh1b, wo_ref[...])

    tile = lambda cols: pl.BlockSpec((tm, cols), lambda i: (i, 0))
    whole = lambda shape: pl.BlockSpec(shape, lambda i: (0,) * len(shape), pipeline_mode=pl.Buffered(1))
    total = lambda shape: pl.BlockSpec(shape, lambda i: (0,) * len(shape))
    row = pl.BlockSpec((1, D_MODEL), lambda i: (0, 0))
    act = lambda cols, dt: jax.ShapeDtypeStruct((s, cols), dt)
    gw_shapes = [(D_MODEL, D_MODEL), (N_DEV, PLE_DIM, pc), (D_MODEL, D_MODEL)]
    return _call(
        body,
        name="tail",
        grid=(s // tm,),
        in_specs=[tile(D_MODEL), tile(D_MODEL), tile(PLE_DIM), tile(D_MODEL), row, row,
                  whole(w_out.shape), whole(w_ff1.shape), whole(w_ff2.shape), whole(w_gate.shape), whole(w_proj.shape)],
        out_specs=[total((8, D_MODEL)), tile(D_MODEL), tile(D_MODEL), tile(D_MODEL), tile(D_FF), tile(D_FF), tile(D_MODEL),
                   *[total(shape) for shape in gw_shapes]],
        out_shape=[jax.ShapeDtypeStruct((8, D_MODEL), F32),
                   act(D_MODEL, F32), act(D_MODEL, F32), act(D_MODEL, BF), act(D_FF, BF), act(D_FF, BF), act(D_MODEL, BF),
                   *[jax.ShapeDtypeStruct(shape, F32) for shape in gw_shapes]],
        scratch_shapes=[pltpu.VMEM((tm, D_FF), F32)],
        compiler_params=_params(("arbitrary",)),
    )(x, mix, p, t, g2, gf, w_out, w_ff1, w_ff2, w_gate, w_proj)


def _mixer_bwd(zuv, qkv, dmix, saved, gv, w_sp, b_sp, bucket, after=None):
    s = zuv.shape[0]
    nb = s // BLOCK

    def body(zuv_ref, q_ref, kvc_ref, kvp_ref, gv_ref, w_ref, b_ref, bucket_ref, dmix_ref, tanh_ref, prob_ref, psink_ref,
             dzm_ref, dkv_ref, dgv_ref, dw_ref, db_ref, dattn_ref,
             dbias_ref, carry_ref, dsink_acc, db_acc):
        i = pl.program_id(0)

        @pl.when(i == 0)
        def _():
            dbias_ref[...] = jnp.zeros_like(dbias_ref)
            carry_ref[...] = jnp.zeros_like(carry_ref)
            dsink_acc[...] = jnp.zeros_like(dsink_acc)
            dgv_ref[...] = jnp.zeros_like(dgv_ref)
            dw_ref[...] = jnp.zeros_like(dw_ref)
            db_acc[...] = jnp.zeros_like(db_acc)

        @pl.when(i < nb)
        def _():
            u, du_dz = _gelu_and_grad(zuv_ref[:, :D_GMLP], tanh_ref[:, :D_GMLP])
            vg, dvg_dz = _gelu_and_grad(zuv_ref[:, D_GMLP:], tanh_ref[:, D_GMLP:])
            for g in range(GROUPS):
                sl = slice(g * GDIM, (g + 1) * GDIM)
                vg_g = vg[:, sl]
                rg = _rms_scale(vg_g)
                vhat = vg_g * rg
                gain = gv_ref[:, sl]
                vn = (vhat * gain).astype(BF)
                w_g = _tril_bf16(w_ref, g)
                sv = _nn(w_g, vn) + b_ref[g]
                dmix_a = dmix_ref[:, sl]
                dsv = dmix_a * u[:, sl]
                dsvb = dsv.astype(BF)
                db_acc[g] += jnp.sum(dsv, axis=-1, keepdims=True)
                dw_ref[g] += _nt(dsvb, vn)
                dvn = _tn(w_g, dsvb)
                dgv_ref[:, sl] += jnp.sum(dvn * vhat, axis=0, keepdims=True)
                dvg = _rms_bwd(dvn * gain, vg_g, rg)
                dzm_ref[:, sl] = ((dmix_a * sv) * du_dz[:, sl]).astype(BF)
                dzm_ref[:, D_GMLP + g * GDIM: D_GMLP + (g + 1) * GDIM] = (dvg * dvg_dz[:, sl]).astype(BF)

            for kh in range(N_KV):
                ksl = slice(kh * HEAD_DIM, (kh + 1) * HEAD_DIM)
                vsl = slice(D_KV + kh * HEAD_DIM, D_KV + (kh + 1) * HEAD_DIM)
                k_prev, k_cur = kvp_ref[:, ksl], kvc_ref[:, ksl]
                v_prev, v_cur = kvp_ref[:, vsl], kvc_ref[:, vsl]
                q4 = _stack_heads(q_ref, kh, 0)
                p_prev, p_cur, p_sink = prob_ref[0, kh, :, :BLOCK], prob_ref[0, kh, :, BLOCK:], psink_ref[0, kh, :, 0:1]
                do4 = _stack_heads(dmix_ref, kh, D_GMLP)
                dp_prev = _nt(do4, v_prev)
                dp_cur = _nt(do4, v_cur)
                delta = jnp.sum(p_prev * dp_prev, axis=-1, keepdims=True) + jnp.sum(p_cur * dp_cur, axis=-1, keepdims=True)
                ds_prev = p_prev * (dp_prev - delta)
                ds_cur = p_cur * (dp_cur - delta)
                dsink_acc[kh] -= p_sink * delta
                dbias_ref[kh, :, :BLOCK] += ds_prev
                dbias_ref[kh, :, BLOCK:] += ds_cur
                dsb_prev = ds_prev.astype(BF)
                dsb_cur = ds_cur.astype(BF)
                dq4 = (_nn(dsb_prev, k_prev) + _nn(dsb_cur, k_cur)) * SCALE
                for g in range(Q_PER_KV):
                    first = 2 * D_GMLP + (kh * Q_PER_KV + g) * HEAD_DIM
                    dzm_ref[:, first:first + HEAD_DIM] = dq4[g * BLOCK:(g + 1) * BLOCK].astype(BF)
                dkv_ref[:, ksl] = (carry_ref[:, ksl] + _tn(dsb_prev, q4) * SCALE).astype(BF)
                dkv_ref[:, vsl] = (carry_ref[:, vsl] + _tn(p_prev.astype(BF), do4)).astype(BF)
                carry_ref[:, ksl] = _tn(dsb_cur, q4) * SCALE
                carry_ref[:, vsl] = _tn(p_cur.astype(BF), do4)

        @pl.when(i == nb)
        def _():
            dkv_ref[...] = carry_ref[...].astype(BF)
            row = lax.broadcasted_iota(jnp.int32, (BLOCK, BLOCK), 0)
            col = lax.broadcasted_iota(jnp.int32, (BLOCK, BLOCK), 1)
            for g in range(GROUPS):
                dw_ref[g] = jnp.where(col <= row, dw_ref[g], 0.0)
                db_ref[g:g + 1, :] = jnp.sum(jnp.where(col == row, db_acc[g], 0.0), axis=0, keepdims=True)
            bucket = bucket_ref[...]
            for b in range(N_Q, REL_BUCKETS):
                dattn_ref[N_Q, b] = 0.0
            for h in range(N_Q):
                kh, rows = _head_rows(h)
                dattn_ref[N_Q, h] = jnp.sum(dsink_acc[kh, rows, :])
                dbh = dbias_ref[kh, rows, :]
                for b in range(REL_BUCKETS):
                    dattn_ref[h, b] = jnp.sum(jnp.where(bucket == b, dbh, 0.0))

    cl = lambda i: jnp.minimum(i, nb - 1)
    const = lambda shape: pl.BlockSpec(shape, lambda i: (0,) * len(shape))
    return _call(
        body,
        name="mixer_bwd",
        after=after,
        grid=(nb + 1,),
        in_specs=_mixer_specs(nb)[:7] + [
            const((BLOCK, 2 * BLOCK)),
            pl.BlockSpec((BLOCK, D_MODEL), lambda i: (cl(i), 0)),
            pl.BlockSpec((BLOCK, 2 * D_GMLP), lambda i: (cl(i), 0)),
            pl.BlockSpec((1, N_KV, ROWS4, 2 * BLOCK), lambda i: (cl(i), 0, 0, 0)),
            pl.BlockSpec((1, N_KV, ROWS4, LANES), lambda i: (cl(i), 0, 0, 0)),
        ],
        out_specs=[
            pl.BlockSpec((BLOCK, D_MAIN), lambda i: (cl(i), 0)),
            pl.BlockSpec((BLOCK, 2 * D_KV), lambda i: (jnp.maximum(i - 1, 0), 0)),
            const((1, D_GMLP)),
            const((GROUPS, BLOCK, BLOCK)),
            const((GROUPS, BLOCK)),
            pl.BlockSpec(memory_space=pltpu.SMEM),
        ],
        out_shape=[
            jax.ShapeDtypeStruct((s, D_MAIN), BF),
            jax.ShapeDtypeStruct((s, 2 * D_KV), BF),
            jax.ShapeDtypeStruct((1, D_GMLP), F32),
            jax.ShapeDtypeStruct((GROUPS, BLOCK, BLOCK), F32),
            jax.ShapeDtypeStruct((GROUPS, BLOCK), F32),
            jax.ShapeDtypeStruct((N_Q + 1, REL_BUCKETS), F32),
        ],
        scratch_shapes=[
            pltpu.VMEM((N_KV, ROWS4, 2 * BLOCK), F32),
            pltpu.VMEM((BLOCK, 2 * D_KV), F32),
            pltpu.VMEM((N_KV, ROWS4, 1), F32),
            pltpu.VMEM((GROUPS, BLOCK, 1), F32),
        ],
        compiler_params=_params(("arbitrary",)),
    )(zuv, qkv, qkv, qkv, gv, w_sp, b_sp, bucket, dmix, *saved)


def _in_bwd(x, dh1, dzm, dkv, g1, w_in_t, after=None):
    s = x.shape[0]
    tm = min(IO_TOK_TILE, s)

    def body(x_ref, dh1_ref, dzm_ref, dkv_ref, g_ref, w_ref, dx_ref, dg_ref):
        @pl.when(pl.program_id(0) == 0)
        def _():
            dg_ref[...] = jnp.zeros_like(dg_ref)

        dhn = _nn(dzm_ref[...], w_ref[:D_MAIN, :]) + _nn(dkv_ref[...], w_ref[D_MAIN:, :])
        xv = x_ref[...]
        r = _rms_scale(xv)
        dg_ref[...] += jnp.sum(dhn * (xv * r), axis=0, keepdims=True)
        dx_ref[...] = dh1_ref[...] + _rms_bwd(dhn * g_ref[...], xv, r)

    tile = lambda cols: pl.BlockSpec((tm, cols), lambda i: (i, 0))
    row = pl.BlockSpec((1, D_MODEL), lambda i: (0, 0))
    return _call(
        body,
        name="in_bwd",
        after=after,
        grid=(s // tm,),
        in_specs=[tile(D_MODEL), tile(D_MODEL), tile(D_MAIN), tile(2 * D_KV), row, pl.BlockSpec((D_IN, D_MODEL), lambda i: (0, 0))],
        out_specs=[tile(D_MODEL), row],
        out_shape=[jax.ShapeDtypeStruct((s, D_MODEL), F32), jax.ShapeDtypeStruct((1, D_MODEL), F32)],
        compiler_params=_params(("arbitrary",)),
    )(x, dh1, dzm, dkv, g1, w_in_t)


def _wgrad_in(dzm, dkv, hn, after=None):
    s = hn.shape[0]
    tm = 2 * D_KV
    n_main = D_MAIN // tm

    def body(dzm_ref, dkv_ref, hn_ref, o_ref):
        i = pl.program_id(0)

        @pl.when(i < n_main)
        def _():
            o_ref[...] = _tn(dzm_ref[...], hn_ref[...])

        @pl.when(i == n_main)
        def _():
            o_ref[...] = _tn(dkv_ref[...], hn_ref[...])

    return _call(
        body,
        name="wgrad_in",
        after=after,
        grid=(n_main + 1,),
        in_specs=[
            pl.BlockSpec((s, tm), lambda i: (0, jnp.minimum(i, n_main - 1))),
            pl.BlockSpec((s, tm), lambda i: (0, 0)),
            pl.BlockSpec((s, D_MODEL), lambda i: (0, 0)),
        ],
        out_specs=pl.BlockSpec((tm, D_MODEL), lambda i: (i, 0)),
        out_shape=jax.ShapeDtypeStruct((D_IN, D_MODEL), F32),
        compiler_params=_params(("arbitrary",)),
    )(dzm, dkv, hn)


def _wgrad(a, b, tm, tn, name, peer_cols=0, after=None):
    s, m = a.shape
    n = b.shape[1]

    def body(a_ref, b_ref, o_ref, at_ref):
        @pl.when(pl.program_id(1) == 0)
        def _():
            at_ref[...] = a_ref[...].astype(BF).T

        r = _nn(at_ref[...], b_ref[...])
        if peer_cols:
            for q in range(tn // peer_cols):
                o_ref[q] = r[:, q * peer_cols:(q + 1) * peer_cols]
        else:
            o_ref[...] = r

    if peer_cols:
        out_spec = pl.BlockSpec((tn // peer_cols, tm, peer_cols), lambda i, j: (j, i, 0))
        out_shape = jax.ShapeDtypeStruct((n // peer_cols, m, peer_cols), F32)
    else:
        out_spec = pl.BlockSpec((tm, tn), lambda i, j: (i, j))
        out_shape = jax.ShapeDtypeStruct((m, n), F32)
    return _call(
        body,
        name=name,
        after=after,
        grid=(m // tm, n // tn),
        in_specs=[pl.BlockSpec((s, tm), lambda i, j: (0, i)), pl.BlockSpec((s, tn), lambda i, j: (0, j))],
        out_specs=out_spec,
        out_shape=out_shape,
        scratch_shapes=[pltpu.VMEM((tm, s), BF)],
        compiler_params=_params(("arbitrary", "arbitrary")),
    )(a, b)


def _adamw_math(w, g, m, v):
    m_new = ADAM_B1 * m + (1.0 - ADAM_B1) * g
    v_new = ADAM_B2 * v + (1.0 - ADAM_B2) * jnp.square(g)
    m_hat = m_new / (1.0 - ADAM_B1 ** ADAM_STEP)
    v_hat = v_new / (1.0 - ADAM_B2 ** ADAM_STEP)
    delta = -ADAM_LR * (m_hat / (jnp.sqrt(v_hat) + ADAM_EPS) + ADAM_WD * w)
    return delta, m_new, v_new


def _final_adamw(part, recv, w, m, v, name):
    r, c = w.shape
    tr = min(r, 512)

    def body(p_ref, r_ref, w_ref, m_ref, v_ref, g_ref, d_ref, mo_ref, vo_ref):
        g = p_ref[...]
        for j in range(3):
            g = g + r_ref[j].astype(F32)
        g_ref[...] = g
        d_ref[...], mo_ref[...], vo_ref[...] = _adamw_math(w_ref[...], g, m_ref[...], v_ref[...])

    spec = pl.BlockSpec((tr, c), lambda i: (i, 0))
    return _call(
        body,
        name=name,
        grid=(r // tr,),
        in_specs=[spec, pl.BlockSpec((3, tr, c), lambda i: (0, i, 0)), spec, spec, spec],
        out_specs=[spec] * 4,
        out_shape=[jax.ShapeDtypeStruct((r, c), F32)] * 4,
        compiler_params=_params(("arbitrary",)),
    )(part, recv, w, m, v)


def _rs_sum(g, land, blocks, name):
    _, r, c = g.shape
    tr = min(r, 256)

    def body(blk_ref, g0_ref, g1_ref, g2_ref, g3_ref, l_ref, part_ref, send_ref):
        part_ref[...] = g0_ref[0] + l_ref[0]
        for j, gj_ref in enumerate((g1_ref, g2_ref, g3_ref)):
            send_ref[j] = (gj_ref[0] + l_ref[j + 1]).astype(BF)

    def pick(j):
        return pl.BlockSpec((1, tr, c), lambda i, blk: (blk[j], i, 0))

    return _call(
        body,
        name=name,
        grid_spec=pltpu.PrefetchScalarGridSpec(
            num_scalar_prefetch=1,
            grid=(r // tr,),
            in_specs=[pick(0), pick(1), pick(2), pick(3), pl.BlockSpec((4, tr, c), lambda i, blk: (0, i, 0))],
            out_specs=[pl.BlockSpec((tr, c), lambda i, blk: (i, 0)), pl.BlockSpec((3, tr, c), lambda i, blk: (0, i, 0))],
        ),
        out_shape=[jax.ShapeDtypeStruct((r, c), F32), jax.ShapeDtypeStruct((3, r, c), BF)],
        compiler_params=_params(("arbitrary",)),
    )(blocks, g, g, g, g, land)


def _final_adamw_few(parts, recvs, ws, ms, vs, name):
    n = len(ws)

    def body(*refs):
        ins, outs = refs[:5 * n], refs[5 * n:]
        for k in range(n):
            p_ref, r_ref, w_ref, m_ref, v_ref = (ins[q * n + k] for q in range(5))
            g = p_ref[...]
            for j in range(3):
                g = g + r_ref[j].astype(F32)
            outs[k][...] = g
            outs[n + k][...], outs[2 * n + k][...], outs[3 * n + k][...] = _adamw_math(w_ref[...], g, m_ref[...], v_ref[...])

    outs = _call(
        body,
        name=name,
        out_shape=[jax.ShapeDtypeStruct(w.shape, F32) for w in ws] * 4,
        compiler_params=_params(),
    )(*parts, *recvs, *ws, *ms, *vs)
    return outs[:n], outs[n:2 * n], outs[2 * n:3 * n], outs[3 * n:]


def _rs_sum_few(gs, lands, blocks, name):
    n = len(gs)

    def body(blk_ref, *refs):
        g_refs, l_refs, part_refs, send_refs = refs[:n], refs[n:2 * n], refs[2 * n:3 * n], refs[3 * n:]
        for k in range(n):
            part_refs[k][...] = g_refs[k][blk_ref[0]] + l_refs[k][0]
            for j in range(3):
                send_refs[k][j] = (g_refs[k][blk_ref[j + 1]] + l_refs[k][j + 1]).astype(BF)

    outs = _call(
        body,
        name=name,
        in_specs=[pl.BlockSpec(memory_space=pltpu.SMEM)] + [pl.BlockSpec(memory_space=pltpu.VMEM)] * (2 * n),
        out_shape=[jax.ShapeDtypeStruct(g.shape[1:], F32) for g in gs] + [jax.ShapeDtypeStruct((3,) + g.shape[1:], BF) for g in gs],
        compiler_params=_params(),
    )(blocks, *gs, *lands)
    return [(outs[k], outs[n + k]) for k in range(n)]


def _adamw_small(own, recv, g1_own, g1_recv, weights, moms, vels, after):
    n_w = len(weights)

    def body(*refs):
        own_refs, recv_refs, g1_own_ref, g1_recv_ref = refs[:5], refs[5:10], refs[10], refs[11]
        w_refs, m_refs, v_refs = refs[12:12 + n_w], refs[12 + n_w:12 + 2 * n_w], refs[12 + 2 * n_w:12 + 3 * n_w]
        outs = refs[12 + 3 * n_w:]
        loss_ref, g_refs, d_refs = outs[0], outs[1:1 + n_w], outs[1 + n_w:1 + 2 * n_w]
        mo_refs, vo_refs = outs[1 + 2 * n_w:1 + 3 * n_w], outs[1 + 3 * n_w:]
        x, y, c = lax.axis_index("x"), lax.axis_index("y"), lax.axis_index("c")

        def in_place_order(values, my_place):
            acc = None
            for place in range(len(values)):
                r = place ^ my_place
                term = values[-1]
                for q in range(len(values) - 2, -1, -1):
                    term = jnp.where(r == q, values[q], term)
                acc = term if acc is None else acc + term
            return acc

        def total(k, *index):
            index = index or (slice(None),) * (len(own_refs[k].shape) - 1)
            across = [own_refs[k][(0, *index)], recv_refs[k][(1, *index)], recv_refs[k][(0, *index)], recv_refs[k][(2, *index)]]
            return in_place_order([v.astype(F32) for v in across], 2 * x + y)

        g1 = in_place_order([g1_own_ref[0]] + [g1_recv_ref[j] for j in range(N_DEV - 1)], 4 * x + 2 * y + c)
        grads = [
            g1, total(1), total(2), total(3),
            total(4, slice(N_Q, None), slice(0, N_Q)), total(4, slice(0, N_Q), slice(None)),
            total(0, slice(0, 1), slice(None)), total(0, slice(1, 2), slice(None))]
        loss_ref[...] = total(0, slice(2, 3), slice(0, 1))
        for k in range(n_w):
            g_refs[k][...] = grads[k]
            d_refs[k][...], mo_refs[k][...], vo_refs[k][...] = _adamw_math(w_refs[k][...], grads[k], m_refs[k][...], v_refs[k][...])

    shapes = [jax.ShapeDtypeStruct(w.shape, F32) for w in weights]
    outs = _call(
        body,
        name="adamw_small",
        after=after,
        in_specs=[pl.BlockSpec(memory_space=pltpu.VMEM)] * (12 + 3 * n_w),
        out_shape=[jax.ShapeDtypeStruct((1, 1), F32)] + shapes * 4,
        compiler_params=_params(),
    )(*own, *recv, g1_own, g1_recv, *weights, *moms, *vels)
    return outs[0], outs[1:1 + n_w], outs[1 + n_w:1 + 2 * n_w], outs[1 + 2 * n_w:1 + 3 * n_w], outs[1 + 3 * n_w:]


def _place():
    x, y, c = lax.axis_index("x"), lax.axis_index("y"), lax.axis_index("c")
    return x, y, c, [(1 - x, y), (x, 1 - y), (1 - x, 1 - y)]


def _dev_index(px, py, pc):
    return 4 * px + 2 * py + pc


HBM_SPEC = pl.BlockSpec(memory_space=pltpu.HBM)
SEM_SPEC = pl.BlockSpec(memory_space=pltpu.SEMAPHORE)
ANY_SPEC = pl.BlockSpec(memory_space=pl.ANY)
DATAFLOW = pltpu.SideEffectType.DATAFLOW_SIDE_EFFECTING


def _hbm(a):
    return pltpu.with_memory_space_constraint(a, pltpu.HBM)


def _prep_weights(shards):
    k_n = len(shards)

    def body(*refs):
        ins, outs, stage, sems = refs[:k_n], refs[k_n:2 * k_n], refs[2 * k_n:3 * k_n], refs[3 * k_n]
        x, y, c, _ = _place()
        copies = []
        for k in range(k_n):
            stage[k][...] = ins[k][...].astype(BF)
            copies.append(pltpu.make_async_copy(stage[k], outs[k].at[_dev_index(x, y, c)], sems.at[k]))
            copies[k].start()
        for cp in copies:
            cp.wait()

    return _call(
        body,
        name="prep_weights",
        in_specs=[pl.BlockSpec(memory_space=pltpu.VMEM)] * k_n,
        out_specs=[ANY_SPEC] * k_n,
        out_shape=[jax.ShapeDtypeStruct((N_DEV,) + sh.shape, BF) for sh in shards],
        scratch_shapes=[pltpu.VMEM(sh.shape, BF) for sh in shards] + [pltpu.SemaphoreType.DMA((k_n,))],
        compiler_params=_params(),
    )(*shards)


def _ag_piece(land_k, block, half, peer, send_sem, recv_sem):
    ref = land_k.at[_dev_index(*block)]
    if half is not None:
        rows = land_k.shape[1] // 2
        ref = ref.at[pl.ds(half * rows, rows)]
    return pltpu.make_async_remote_copy(
        src_ref=ref, dst_ref=ref, send_sem=send_sem, recv_sem=recv_sem, device_id=peer, device_id_type=MESH)


def _ag_plan():
    x, y, c, _ = _place()
    me, sib = (x, y, c), (x, y, 1 - c)
    xn, yn, diag = (1 - x, y, c), (x, 1 - y, c), (1 - x, 1 - y, c)
    return dict(
        relay_halves=[(me, 0, xn), (me, 1, yn)],
        others=[(me, None, sib), (me, 1, xn), (me, 0, yn)],
        relays=[(xn, 0, yn), (yn, 1, xn)],
        near=[(xn, None, sib), (yn, None, sib)],
        far=[(diag, None, sib)],
    )


def _ag_stage(land, stage, send_sems, recv_sems, act):
    copies = _ag_plan()[stage]
    n = len(copies)
    for k in range(len(land)):
        for j, (block, half, peer) in enumerate(copies):
            cp = _ag_piece(land[k], block, half, peer, send_sems.at[n * k + j], recv_sems.at[n * k + j])
            if act == "start":
                cp.start()
            else:
                cp.wait_send()
                cp.wait_recv()


def _sem_shapes(*counts):
    return [pltpu.SemaphoreType.DMA((n,)) for n in counts for _ in range(2)]


def _ag_start(first, rest):
    lands = list(first) + list(rest)
    k_n, k_first = len(lands), len(first)
    k_rest = k_n - k_first

    def body(*refs):
        land = refs[:k_n]
        sems = refs[k_n:k_n + 6]
        token = refs[-1]
        x, y, c, chips = _place()
        targets = [(x, y, 1 - c)] + [(*chip, c) for chip in chips]
        for k in range(k_first):
            for j, to in enumerate(targets):
                _ag_piece(land[k], (x, y, c), None, to, sems[0].at[4 * k + j], sems[1].at[4 * k + j]).start()
        _ag_stage(land[k_first:], "relay_halves", sems[2], sems[3], "start")
        _ag_stage(land[k_first:], "others", sems[4], sems[5], "start")
        token[...] = jnp.zeros_like(token)

    outs = pl.pallas_call(
        body,
        name="ag_start",
        in_specs=[HBM_SPEC] * k_n,
        out_specs=(*[SEM_SPEC] * 6, *[HBM_SPEC] * k_n, pl.BlockSpec(memory_space=pltpu.VMEM)),
        out_shape=(*_sem_shapes(4 * k_first, 2 * k_rest, 3 * k_rest),
                   *[pltpu.HBM(a.shape, a.dtype) for a in lands], jax.ShapeDtypeStruct((8, LANES), F32)),
        input_output_aliases={k: 6 + k for k in range(k_n)},
        compiler_params=pltpu.CompilerParams(has_side_effects=DATAFLOW),
    )(*[_hbm(a) for a in lands])
    flying = list(outs[6:6 + k_n])
    return (outs[0], outs[1], flying[:k_first]), (outs[2:6], flying[k_first:]), outs[-1]


def _ag_split_call(lands, waits, starts, after, name):
    k_n = len(lands)
    plan_sizes = dict(relay_halves=2, others=3, relays=2, near=2, far=1)
    n_in, n_out = 2 * len(waits), 2 * len(starts)

    def body(*refs):
        land = refs[:k_n]
        in_sems = refs[k_n:k_n + n_in]
        out_sems, token = refs[len(refs) - 1 - n_out:len(refs) - 1], refs[-1]
        for w, (stage, _, _) in enumerate(waits):
            _ag_stage(land, stage, in_sems[2 * w], in_sems[2 * w + 1], "wait")
            if w < len(starts):
                _ag_stage(land, starts[w], out_sems[2 * w], out_sems[2 * w + 1], "start")
        token[...] = jnp.zeros_like(token)

    outs = pl.pallas_call(
        body,
        name=name,
        in_specs=[HBM_SPEC] * k_n + [SEM_SPEC] * n_in + [ANY_SPEC],
        out_specs=(*[HBM_SPEC] * k_n, *[SEM_SPEC] * n_out, pl.BlockSpec(memory_space=pltpu.VMEM)),
        out_shape=(*[pltpu.HBM(a.shape, a.dtype) for a in lands], *_sem_shapes(*[plan_sizes[s] * k_n for s in starts]),
                   jax.ShapeDtypeStruct((8, LANES), F32)),
        input_output_aliases={k: k for k in range(k_n)},
        compiler_params=pltpu.CompilerParams(has_side_effects=DATAFLOW),
    )(*lands, *[s for _, a, b in waits for s in (a, b)], after)
    return list(outs[:k_n]), list(outs[k_n:k_n + n_out]), outs[-1]


def _ag_mid(lands, send_sems, recv_sems, after, name):
    k_n = len(lands)

    def body(*refs):
        land = refs[:k_n]
        send1, recv1 = refs[k_n], refs[k_n + 1]
        fwd_send, fwd_recv = refs[-2], refs[-1]
        x, y, c, chips = _place()
        sources = [(x, y, 1 - c)] + [(*chip, c) for chip in chips]
        for k in range(k_n):
            mine = land[k].at[_dev_index(x, y, c)]
            for j, frm in enumerate(sources):
                got = land[k].at[_dev_index(*frm)]
                cp = pltpu.make_async_remote_copy(
                    src_ref=mine, dst_ref=got, send_sem=send1.at[4 * k + j], recv_sem=recv1.at[4 * k + j], device_id=frm, device_id_type=MESH)
                cp.wait_send()
                cp.wait_recv()
                if j >= 1:
                    pltpu.make_async_remote_copy(
                        src_ref=got, dst_ref=got, send_sem=fwd_send.at[3 * k + j - 1], recv_sem=fwd_recv.at[3 * k + j - 1],
                        device_id=(x, y, 1 - c), device_id_type=MESH).start()

    outs = pl.pallas_call(
        body,
        name=name,
        in_specs=[HBM_SPEC] * k_n + [SEM_SPEC, SEM_SPEC, ANY_SPEC],
        out_specs=(*[HBM_SPEC] * k_n, SEM_SPEC, SEM_SPEC),
        out_shape=(*[pltpu.HBM(a.shape, a.dtype) for a in lands], pltpu.SemaphoreType.DMA((3 * k_n,)), pltpu.SemaphoreType.DMA((3 * k_n,))),
        input_output_aliases={k: k for k in range(k_n)},
        compiler_params=pltpu.CompilerParams(has_side_effects=DATAFLOW),
    )(*lands, send_sems, recv_sems, after)
    return list(outs[:k_n]), outs[-2], outs[-1]


def _ag_end(lands, fwd_send, fwd_recv, name):
    k_n = len(lands)

    def body(*refs):
        land = refs[:k_n]
        fsend, frecv = refs[k_n], refs[k_n + 1]
        x, y, c, chips = _place()
        for k in range(k_n):
            for j, chip in enumerate(chips):
                cp = pltpu.make_async_remote_copy(
                    src_ref=land[k].at[_dev_index(*chip, c)], dst_ref=land[k].at[_dev_index(*chip, 1 - c)],
                    send_sem=fsend.at[3 * k + j], recv_sem=frecv.at[3 * k + j], device_id=(x, y, 1 - c), device_id_type=MESH)
                cp.wait_send()
                cp.wait_recv()

    outs = pl.pallas_call(
        body,
        name=name,
        in_specs=[HBM_SPEC] * k_n + [SEM_SPEC, SEM_SPEC],
        out_specs=tuple([HBM_SPEC] * k_n),
        out_shape=tuple(pltpu.HBM(a.shape, a.dtype) for a in lands),
        input_output_aliases={k: k for k in range(k_n)},
        compiler_params=pltpu.CompilerParams(has_side_effects=DATAFLOW),
    )(*lands, fwd_send, fwd_recv)
    return list(outs)


def _chips4():
    x, y, c, others = _place()
    return x, y, c, [(x, y)] + others


def _route_sibling(j):
    x, y, c, chips = _chips4()
    return _dev_index(*chips[j], 1 - c), j, (x, y, 1 - c)


def _route_chips(j):
    x, y, c, chips = _chips4()
    return j, j, (*chips[j + 1], c)


def _route_sibling_whole(j):
    x, y, c, _ = _chips4()
    return 0, 0, (x, y, 1 - c)


def _route_chips_whole(j):
    x, y, c, chips = _chips4()
    return 0, j, (*chips[j + 1], c)


def _route_everyone(j):
    x, y, c, _ = _chips4()
    flip = [(j + 1) >> 2 & 1, (j + 1) >> 1 & 1, (j + 1) & 1]
    return 0, j, tuple(1 - v if f else v for v, f in zip((x, y, c), flip))


def _xchg_copies(routes, src, dst, send_sems, recv_sems):
    copies, sem = [], 0
    for k, (route, n) in enumerate(routes):
        for j in range(n):
            si, di, peer = route(j)
            copies.append(pltpu.make_async_remote_copy(
                src_ref=src[k].at[si], dst_ref=dst[k].at[di], send_sem=send_sems.at[sem], recv_sem=recv_sems.at[sem],
                device_id=peer, device_id_type=MESH))
            sem += 1
    return copies


def _xchg_start(srcs, slot_shapes, routes, name):
    k_n = len(srcs)
    n_sem = sum(n for _, n in routes)
    dsts = [lax.empty((n,) + tuple(sh), a.dtype) for sh, a, (_, n) in zip(slot_shapes, srcs, routes)]

    def body(*refs):
        src, dst = refs[:k_n], refs[k_n:2 * k_n]
        send_sems, recv_sems, token = refs[2 * k_n], refs[2 * k_n + 1], refs[-1]
        for cp in _xchg_copies(routes, src, dst, send_sems, recv_sems):
            cp.start()
        token[...] = jnp.zeros_like(token)

    arrays = list(srcs) + dsts
    outs = pl.pallas_call(
        body,
        name=name,
        in_specs=[HBM_SPEC] * (2 * k_n),
        out_specs=(SEM_SPEC, SEM_SPEC, *[HBM_SPEC] * (2 * k_n), pl.BlockSpec(memory_space=pltpu.VMEM)),
        out_shape=(pltpu.SemaphoreType.DMA((n_sem,)), pltpu.SemaphoreType.DMA((n_sem,)),
                   *[pltpu.HBM(a.shape, a.dtype) for a in arrays], jax.ShapeDtypeStruct((8, LANES), F32)),
        input_output_aliases={i: 2 + i for i in range(2 * k_n)},
        compiler_params=pltpu.CompilerParams(has_side_effects=DATAFLOW),
    )(*[_hbm(a) for a in arrays])
    return outs[0], outs[1], list(outs[2:2 + k_n]), list(outs[2 + k_n:2 + 2 * k_n]), outs[-1]


def _xchg_wait(send_sems, recv_sems, srcs, dsts, routes, after, name):
    k_n = len(srcs)

    def body(*refs):
        src, dst = refs[:k_n], refs[k_n:2 * k_n]
        for cp in _xchg_copies(routes, src, dst, refs[2 * k_n], refs[2 * k_n + 1]):
            cp.wait_send()
            cp.wait_recv()

    arrays = list(srcs) + list(dsts)
    outs = pl.pallas_call(
        body,
        name=name,
        in_specs=[HBM_SPEC] * (2 * k_n) + [SEM_SPEC, SEM_SPEC, ANY_SPEC],
        out_specs=tuple([HBM_SPEC] * (2 * k_n)),
        out_shape=tuple(pltpu.HBM(a.shape, a.dtype) for a in arrays),
        input_output_aliases={i: i for i in range(2 * k_n)},
        compiler_params=pltpu.CompilerParams(has_side_effects=DATAFLOW),
    )(*arrays, send_sems, recv_sems, after)
    return list(outs[:k_n]), list(outs[k_n:])


SMALL = ("norm1_gain", "gmlp_v_gain", "w_spatial", "b_spatial", "attn_sinks", "rel_bias_table", "norm2_gain", "final_gain")
LANES = 128


def _swap_start(grads, smalls, tag):
    srcs = list(grads) + [a[None] for a in smalls]
    shapes = [g.shape[1:] for g in grads] + [a.shape for a in smalls]
    routes = [(_route_sibling, 4)] * len(grads) + [(_route_sibling_whole, 1)] * len(smalls)
    return _xchg_start(srcs, shapes, routes, f"rs_{tag}_swap_start"), routes, len(grads)


def _swap_sums(swap, names, after, tag, few=0, small_dtypes=None):
    (send1, recv1, src1, land1, _), routes, n_rs = swap
    x, y, c, chips = _chips4()
    blocks = jnp.stack([_dev_index(*chip, c) for chip in chips]).astype(jnp.int32)
    src1, land1 = _xchg_wait(send1, recv1, src1, land1, routes, after, f"rs_{tag}_swap_wait")
    n_tiled = n_rs - few
    sums = [_rs_sum(g, land, blocks, f"rs_sum_{n}") for g, land, n in zip(src1[:n_tiled], land1[:n_tiled], names)]
    if few:
        sums += _rs_sum_few(src1[n_tiled:n_rs], land1[n_tiled:n_rs], blocks, f"rs_{tag}_sum")
    if len(src1) == n_rs:
        return sums, []
    return sums, _pair_sum(src1[n_rs:], land1[n_rs:], small_dtypes or [F32] * (len(src1) - n_rs), f"rs_{tag}_sum_small")


def _pair_sum(mine, theirs, dtypes, name):
    def body(*refs):
        n = len(refs) // 3
        for k in range(n):
            refs[2 * n + k][...] = (refs[k][...] + refs[n + k][...]).astype(dtypes[k])

    return _call(
        body,
        name=name,
        out_shape=[jax.ShapeDtypeStruct(a.shape, dt) for a, dt in zip(mine, dtypes)],
        compiler_params=_params(),
    )(*mine, *theirs)


def _chips_start(sums, small_sums, tag, everyone=()):
    sends = [ps[1] for ps in sums] + list(small_sums) + [a[None] for a in everyone]
    routes = [(_route_chips, 3)] * len(sums) + [(_route_chips_whole, 3)] * len(small_sums) + [(_route_everyone, 7)] * len(everyone)
    return _xchg_start(sends, [a.shape[1:] for a in sends], routes, f"rs_{tag}_chips_start"), routes


def _chips_wait(chips, after, tag):
    (send2, recv2, src2, land2, _), routes = chips
    return _xchg_wait(send2, recv2, src2, land2, routes, after, f"rs_{tag}_chips_wait")


def kernel(x, p, norm1_gain, w_in, gmlp_v_gain, w_spatial, b_spatial, attn_sinks, rel_bias_table, w_out, norm2_gain, w_ff1, w_ff2, w_ple_proj, w_ple_gate, final_gain, loss_target, m_norm1_gain, m_w_in, m_gmlp_v_gain, m_w_spatial, m_b_spatial, m_attn_sinks, m_rel_bias_table, m_w_out, m_norm2_gain, m_w_ff1, m_w_ff2, m_w_ple_proj, m_w_ple_gate, m_final_gain, v_norm1_gain, v_w_in, v_gmlp_v_gain, v_w_spatial, v_b_spatial, v_attn_sinks, v_rel_bias_table, v_w_out, v_norm2_gain, v_w_ff1, v_w_ff2, v_w_ple_proj, v_w_ple_gate, v_final_gain):
    args = dict(locals())
    s = x.shape[1]
    big = ("w_in", "w_out", "w_ff1", "w_ff2", "w_ple_proj", "w_ple_gate")

    x2, p2, t2 = x.reshape(s, D_MODEL), p.reshape(s, PLE_DIM), loss_target.reshape(s, D_MODEL)
    g1, gv, w_sp, b_sp, sinks, table, g2, gf = (args[n] for n in SMALL)
    bucket = jnp.asarray(_bucket_table())
    b_col = b_sp.reshape(GROUPS, BLOCK, 1)

    def shard(name):
        return args[name][0].T if name.endswith("w_in") else args[name][0]

    lands = _prep_weights([shard(n) for n in big])
    (send_in, recv_in, fly_in), (rest_sems, fly_rest), token = _ag_start(lands[:1], lands[1:])
    mid_in, fwd_send_in, fwd_recv_in = _ag_mid(fly_in, send_in, recv_in, token, "ag_mid_w_in")
    g_in = _ag_end(mid_in, fwd_send_in, fwd_recv_in, "ag_end_w_in")[0]
    full_in = g_in.reshape(D_IN, D_MODEL)

    zuv, qkv, hn1 = _in_proj(x2, g1, full_in)
    fly_rest, relay_sems, relayed = _ag_split_call(
        fly_rest, [("relay_halves", *rest_sems[:2])], ["relays"], zuv, "ag_relay")
    mix, *saved = _mixer_fwd(zuv, qkv, gv, w_sp[0], b_col, sinks, table.T, bucket, after=relayed)
    fly_rest, fwd_sems, _ = _ag_split_call(
        fly_rest, [("others", *rest_sems[2:]), ("relays", *relay_sems)], ["near", "far"], mix, "ag_mid_rest")
    g_out, g_ff1, g_ff2, g_proj, g_gate = _ag_split_call(
        fly_rest, [("near", *fwd_sems[:2]), ("far", *fwd_sems[2:])], [], mix, "ag_end_rest")[0]
    full_out, full_ff2, full_gate = g_out.reshape(D_MODEL, D_MODEL), g_ff2.reshape(D_FF, D_MODEL), g_gate.reshape(D_MODEL, D_MODEL)
    full_proj = g_proj.transpose(1, 0, 2).reshape(PLE_DIM, D_MODEL)

    tail_small, dh1, dmix, hn2, a, df, dh2, gw_out, gw_proj, gw_gate = _tail(
        x2, mix, p2, t2, g2, gf.reshape(1, D_MODEL), full_out, g_ff1, full_ff2, full_gate, full_proj)
    gw_out, gw_gate = (g.reshape(N_DEV, D_MODEL // N_DEV, D_MODEL) for g in (gw_out, gw_gate))

    few_names = ("w_out", "w_ple_proj", "w_ple_gate")
    ff1_names = ("w_ff1",) + few_names
    gw_ff1 = _wgrad(hn2, df, D_MODEL, 2 * D_FF // N_DEV, "wgrad_ff1", peer_cols=D_FF // N_DEV)
    ff1_swap = _swap_start([gw_ff1, gw_out, gw_proj, gw_gate], [tail_small], "ff1")
    gw_ff2 = _wgrad(a, dh2, 1024, D_MODEL, "wgrad_ff2", after=ff1_swap[0][4]).reshape(N_DEV, D_FF // N_DEV, D_MODEL)
    ff2_swap = _swap_start([gw_ff2], [], "ff2")
    ff1_sums, ff1_small = _swap_sums(ff1_swap, ff1_names, ff2_swap[0][4], "ff1", few=len(few_names))
    ff1_chips = _chips_start(ff1_sums, ff1_small, "ff1")

    dzm, dkv, d_gv, d_wsp, d_bsp, d_attn = _mixer_bwd(
        zuv, qkv, dmix, saved, gv, w_sp[0], b_col, bucket, after=ff1_chips[0][4])
    ff2_sums, _ = _swap_sums(ff2_swap, ("w_ff2",), dzm, "ff2")
    ff2_chips = _chips_start(ff2_sums, [], "ff2")
    gw_in = _wgrad_in(dzm, dkv, hn1, after=ff2_chips[0][4]).reshape(N_DEV, D_IN // N_DEV, D_MODEL)
    in_swap = _swap_start([gw_in], [d_gv, d_wsp, d_bsp, d_attn], "in")

    grads, deltas, new_m, new_v = {}, {}, {}, {}
    small_own, small_recv = [], []

    def finish(chips, sums, names, tag, after):
        sent, recvs = _chips_wait(chips, after, tag)
        small_own.extend(sent[len(names):])
        small_recv.extend(recvs[len(names):])
        tiled = [n for n in names if n not in few_names]
        done = [_final_adamw(part, recv, shard(n), shard("m_" + n), shard("v_" + n), "adamw_" + n)
                for n, (part, _), recv in zip(tiled, sums, recvs)]
        few = names[len(tiled):]
        if few:
            done += zip(*_final_adamw_few(
                [ps[0] for ps in sums[len(tiled):]], recvs[len(tiled):len(names)], [shard(n) for n in few],
                [shard("m_" + n) for n in few], [shard("v_" + n) for n in few], "adamw_few"))
        for n, results in zip(names, done):
            for dst, arr in zip((grads, deltas, new_m, new_v), results):
                dst[n] = (arr.T if n == "w_in" else arr)[None]
        return results[1]

    after = finish(ff1_chips, ff1_sums, ff1_names, "ff1", in_swap[0][4])
    in_sums, in_small = _swap_sums(in_swap, ("w_in",), after, "in", small_dtypes=[F32, BF, F32, F32])
    in_chips = _chips_start(in_sums, in_small, "in")
    dx, d_g1 = _in_bwd(x2, dh1, dzm, dkv, g1, full_in, after=in_chips[0][4])
    grad_x = dx.reshape(x.shape)
    g1_chips = _chips_start([], [], "g1", everyone=[d_g1])
    after = finish(ff2_chips, ff2_sums, ("w_ff2",), "ff2", g1_chips[0][4])
    after = finish(in_chips, in_sums, ("w_in",), "in", after)
    g1_own, g1_recv = _chips_wait(g1_chips, after, "g1")

    views = {"w_spatial": (GROUPS, BLOCK, BLOCK), "b_spatial": (GROUPS, BLOCK), "final_gain": (1, D_MODEL)}
    def view(name, base):
        return args[name].T if base == "rel_bias_table" else args[name].reshape(views.get(base, args[name].shape))

    small_in = [[view(pre + n, n) for n in SMALL] for pre in ("", "m_", "v_")]
    loss, *small_out = _adamw_small(small_own, small_recv, g1_own[0], g1_recv[0], *small_in, after=after)
    for dst, arrays in zip((grads, deltas, new_m, new_v), small_out):
        for n, arr in zip(SMALL, arrays):
            dst[n] = arr.T if n == "rel_bias_table" else arr.reshape(args[n].shape)
    loss = loss[0, 0]

    order = ("norm1_gain", "w_in", "gmlp_v_gain", "w_spatial", "b_spatial", "attn_sinks", "rel_bias_table", "w_out",
             "norm2_gain", "w_ff1", "w_ff2", "w_ple_proj", "w_ple_gate", "final_gain")
    return (loss, grad_x, *[grads[n] for n in order], *[deltas[n] for n in order],
            *[new_m[n] for n in order], *[new_v[n] for n in order])
```

```python
import math

import numpy as np
import jax
import jax.numpy as jnp
from jax import lax
from jax.experimental import pallas as pl
from jax.experimental.pallas import tpu as pltpu

F32 = jnp.float32
BF = jnp.bfloat16
MESH = pl.DeviceIdType.MESH
N_DEV = 8

D_MODEL = 1024
PLE_DIM = 256
D_GMLP = 512
GROUPS = 4
GDIM = 128
BLOCK = 128
D_ATTN = 512
HEAD_DIM = 64
N_Q = 8
Q_PER_KV = 4
N_KV = N_Q // Q_PER_KV
ROWS4 = Q_PER_KV * BLOCK
D_KV = 128
D_FF = 4096
D_IN = 1792
D_MAIN = 2 * D_GMLP + D_ATTN
REL_BUCKETS = 32
EPS = 1e-6
NEG_INF = -1e30
SCALE = HEAD_DIM ** -0.5
GELU_C = math.sqrt(2.0 / math.pi)
GELU_A = 0.044715

ADAM_LR = 0.001
ADAM_B1 = 0.9
ADAM_B2 = 0.999
ADAM_EPS = 1e-08
ADAM_WD = 0.01
ADAM_STEP = 10

V7X_VMEM_LIMIT = 61440 * 1024
TOK_TILE = 256
IO_TOK_TILE = 512


def _call(body, after=None, **kw):
    if after is None:
        return pl.pallas_call(body, **kw)
    n_in = len(kw["in_specs"])

    def ordered(*refs):
        body(*refs[:n_in], *refs[n_in + 1:])

    kw["in_specs"] = list(kw["in_specs"]) + [pl.BlockSpec(memory_space=pl.ANY)]
    fn = pl.pallas_call(ordered, **kw)
    return lambda *operands: fn(*operands, after)


def _params(sem=None):
    if sem is None:
        return pltpu.CompilerParams(vmem_limit_bytes=V7X_VMEM_LIMIT)
    return pltpu.CompilerParams(dimension_semantics=sem, vmem_limit_bytes=V7X_VMEM_LIMIT)


def _nn(a, b):
    return jnp.dot(a, b, preferred_element_type=F32)


def _nt(a, b):
    return lax.dot_general(a, b, (((1,), (1,)), ((), ())), preferred_element_type=F32)


def _tn(a, b):
    return lax.dot_general(a, b, (((0,), (0,)), ((), ())), preferred_element_type=F32)


def _gelu_tanh(x):
    return jnp.tanh(GELU_C * (x + GELU_A * (x * x * x)))


def _gelu(x, t):
    return x * (0.5 * (1.0 + t))


def _gelu_and_grad(x, t):
    cdf = 0.5 * (1.0 + t)
    return x * cdf, cdf + 0.5 * x * (1.0 - t * t) * (GELU_C * (1.0 + 3.0 * GELU_A * (x * x)))


def _rms_scale(x):
    return lax.rsqrt(jnp.mean(x * x, axis=-1, keepdims=True) + EPS)


def _rms_bwd(dxn, x, r):
    return r * dxn - x * ((r * r * r) * jnp.mean(dxn * x, axis=-1, keepdims=True))


def _bucket_table():
    a = np.arange(BLOCK)[:, None]
    j = np.arange(2 * BLOCK)[None, :]
    n = BLOCK + a - j
    valid = (n >= 0) & (n < BLOCK)
    nc = np.maximum(n, 0)
    max_exact = REL_BUCKETS // 2
    nf = np.maximum(nc, 1).astype(np.float32)
    large = max_exact + (
        np.log(nf / np.float32(max_exact)) / np.float32(math.log(BLOCK / max_exact)) * np.float32(REL_BUCKETS - max_exact)
    ).astype(np.int32)
    large = np.minimum(large, REL_BUCKETS - 1)
    bucket = np.where(nc < max_exact, nc, large)
    return np.where(valid, bucket, -1).astype(np.int32)


def _in_proj(x, g1, w_in_t):
    s = x.shape[0]
    tm = min(IO_TOK_TILE, s)

    def body(x_ref, g_ref, w_ref, zuv_ref, qkv_ref, hn_ref):
        xv = x_ref[...]
        hn = ((xv * _rms_scale(xv)) * g_ref[...]).astype(BF)
        hn_ref[...] = hn
        z = _nt(hn, w_ref[...])
        zuv_ref[...] = z[:, : 2 * D_GMLP]
        qkv_ref[...] = z[:, 2 * D_GMLP:].astype(BF)

    return _call(
        body,
        name="in_proj",
        grid=(s // tm,),
        in_specs=[
            pl.BlockSpec((tm, D_MODEL), lambda i: (i, 0)),
            pl.BlockSpec((1, D_MODEL), lambda i: (0, 0)),
            pl.BlockSpec((D_IN, D_MODEL), lambda i: (0, 0)),
        ],
        out_specs=[
            pl.BlockSpec((tm, 2 * D_GMLP), lambda i: (i, 0)),
            pl.BlockSpec((tm, D_ATTN + 2 * D_KV), lambda i: (i, 0)),
            pl.BlockSpec((tm, D_MODEL), lambda i: (i, 0)),
        ],
        out_shape=[
            jax.ShapeDtypeStruct((s, 2 * D_GMLP), F32),
            jax.ShapeDtypeStruct((s, D_ATTN + 2 * D_KV), BF),
            jax.ShapeDtypeStruct((s, D_MODEL), BF),
        ],
        compiler_params=_params(("arbitrary",)),
    )(x, g1, w_in_t)


def _head_rows(h):
    kh, g = divmod(h, Q_PER_KV)
    return kh, slice(g * BLOCK, (g + 1) * BLOCK)


def _build_bias(bias_ref, bucket_ref, table_ref):
    bucket = bucket_ref[...]
    for h in range(N_Q):
        acc = jnp.zeros((BLOCK, 2 * BLOCK), F32)
        for b in range(REL_BUCKETS):
            acc = jnp.where(bucket == b, table_ref[h, b], acc)
        kh, rows = _head_rows(h)
        bias_ref[kh, rows, :] = acc


def _window_masks(i):
    row = lax.broadcasted_iota(jnp.int32, (ROWS4, BLOCK), 0) & (BLOCK - 1)
    col = lax.broadcasted_iota(jnp.int32, (ROWS4, BLOCK), 1)
    return (col > row) & (i > 0), col <= row


def _stack_heads(ref, kh, offset):
    first = offset + kh * Q_PER_KV * HEAD_DIM
    return jnp.concatenate(
        [ref[:, first + g * HEAD_DIM: first + (g + 1) * HEAD_DIM].astype(BF) for g in range(Q_PER_KV)], axis=0)


def _stack_sinks(sink_ref, kh):
    return jnp.concatenate([jnp.full((BLOCK, 1), sink_ref[0, kh * Q_PER_KV + g], F32) for g in range(Q_PER_KV)], axis=0)


def _tril_bf16(w_ref, g):
    row = lax.broadcasted_iota(jnp.int32, (BLOCK, BLOCK), 0)
    col = lax.broadcasted_iota(jnp.int32, (BLOCK, BLOCK), 1)
    return jnp.where(col <= row, w_ref[g], 0.0).astype(BF)


def _attn_probs(q_h, k_prev, k_cur, bias_h, sink, valid_prev, valid_cur):
    l_prev = jnp.where(valid_prev, _nt(q_h, k_prev) * SCALE + bias_h[:, :BLOCK], NEG_INF)
    l_cur = jnp.where(valid_cur, _nt(q_h, k_cur) * SCALE + bias_h[:, BLOCK:], NEG_INF)
    m = jnp.maximum(jnp.maximum(jnp.max(l_prev, axis=-1, keepdims=True), jnp.max(l_cur, axis=-1, keepdims=True)), sink)
    e_prev = jnp.exp(l_prev - m)
    e_cur = jnp.exp(l_cur - m)
    e_sink = jnp.exp(sink - m)
    denom = jnp.sum(e_prev, axis=-1, keepdims=True) + jnp.sum(e_cur, axis=-1, keepdims=True) + e_sink
    return e_prev / denom, e_cur / denom, e_sink / denom


def _mixer_specs(nb):
    cl = lambda i: jnp.minimum(i, nb - 1)
    return [
        pl.BlockSpec((BLOCK, 2 * D_GMLP), lambda i: (cl(i), 0)),
        pl.BlockSpec((BLOCK, D_ATTN), lambda i: (cl(i), 0)),
        pl.BlockSpec((BLOCK, 2 * D_KV), lambda i: (cl(i), D_ATTN // (2 * D_KV))),
        pl.BlockSpec((BLOCK, 2 * D_KV), lambda i: (jnp.maximum(cl(i) - 1, 0), D_ATTN // (2 * D_KV))),
        pl.BlockSpec((1, D_GMLP), lambda i: (0, 0)),
        pl.BlockSpec((GROUPS, BLOCK, BLOCK), lambda i: (0, 0, 0)),
        pl.BlockSpec((GROUPS, BLOCK, 1), lambda i: (0, 0, 0)),
        pl.BlockSpec(memory_space=pltpu.SMEM),
        pl.BlockSpec(memory_space=pltpu.SMEM),
        pl.BlockSpec((BLOCK, 2 * BLOCK), lambda i: (0, 0)),
    ]


def _mixer_fwd(zuv, qkv, gv, w_sp, b_sp, sinks, table, bucket, after=None):
    s = zuv.shape[0]
    nb = s // BLOCK

    def body(zuv_ref, q_ref, kvc_ref, kvp_ref, gv_ref, w_ref, b_ref, sink_ref, table_ref, bucket_ref,
             mix_ref, tanh_ref, prob_ref, psink_ref, bias_ref):
        i = pl.program_id(0)

        @pl.when(i == 0)
        def _():
            _build_bias(bias_ref, bucket_ref, table_ref)

        t = _gelu_tanh(zuv_ref[...])
        tanh_ref[...] = t
        u = _gelu(zuv_ref[:, :D_GMLP], t[:, :D_GMLP])
        vg = _gelu(zuv_ref[:, D_GMLP:], t[:, D_GMLP:])
        for g in range(GROUPS):
            sl = slice(g * GDIM, (g + 1) * GDIM)
            vg_g = vg[:, sl]
            vn = ((vg_g * _rms_scale(vg_g)) * gv_ref[:, sl]).astype(BF)
            sv = _nn(_tril_bf16(w_ref, g), vn) + b_ref[g]
            mix_ref[:, sl] = (u[:, sl] * sv).astype(BF)

        valid_prev, valid_cur = _window_masks(i)
        for kh in range(N_KV):
            ksl = slice(kh * HEAD_DIM, (kh + 1) * HEAD_DIM)
            vsl = slice(D_KV + kh * HEAD_DIM, D_KV + (kh + 1) * HEAD_DIM)
            q4 = _stack_heads(q_ref, kh, 0)
            p_prev, p_cur, p_sink = _attn_probs(
                q4, kvp_ref[:, ksl], kvc_ref[:, ksl], bias_ref[kh], _stack_sinks(sink_ref, kh), valid_prev, valid_cur)
            prob_ref[0, kh, :, :BLOCK] = p_prev
            prob_ref[0, kh, :, BLOCK:] = p_cur
            psink_ref[0, kh] = jnp.broadcast_to(p_sink, (ROWS4, LANES))
            o4 = _nn(p_prev.astype(BF), kvp_ref[:, vsl]) + _nn(p_cur.astype(BF), kvc_ref[:, vsl])
            for g in range(Q_PER_KV):
                first = D_GMLP + (kh * Q_PER_KV + g) * HEAD_DIM
                mix_ref[:, first:first + HEAD_DIM] = o4[g * BLOCK:(g + 1) * BLOCK].astype(BF)

    return _call(
        body,
        name="mixer_fwd",
        after=after,
        grid=(nb,),
        in_specs=_mixer_specs(nb),
        out_specs=[
            pl.BlockSpec((BLOCK, D_MODEL), lambda i: (i, 0)),
            pl.BlockSpec((BLOCK, 2 * D_GMLP), lambda i: (i, 0)),
            pl.BlockSpec((1, N_KV, ROWS4, 2 * BLOCK), lambda i: (i, 0, 0, 0)),
            pl.BlockSpec((1, N_KV, ROWS4, LANES), lambda i: (i, 0, 0, 0)),
        ],
        out_shape=[
            jax.ShapeDtypeStruct((s, D_MODEL), BF),
            jax.ShapeDtypeStruct((s, 2 * D_GMLP), F32),
            jax.ShapeDtypeStruct((nb, N_KV, ROWS4, 2 * BLOCK), F32),
            jax.ShapeDtypeStruct((nb, N_KV, ROWS4, LANES), F32),
        ],
        scratch_shapes=[pltpu.VMEM((N_KV, ROWS4, 2 * BLOCK), F32)],
        compiler_params=_params(("arbitrary",)),
    )(zuv, qkv, qkv, qkv, gv, w_sp, b_sp, sinks, table, bucket)


def _tail(x, mix, p, t, g2, gf, w_out, w_ff1, w_ff2, w_gate, w_proj):
    s = x.shape[0]
    tm = min(TOK_TILE, s)
    n_ff = w_ff1.shape[0]
    fc = D_FF // n_ff
    pc = D_MODEL // N_DEV

    def body(x_ref, mix_ref, p_ref, t_ref, g2_ref, gf_ref, wo_ref, w1_ref, w2_ref, wg_ref, wp_ref,
             small_ref, dh1_ref, dmix_ref, hn2_ref, a_ref, df_ref, dh2_ref, gwo_ref, gwp_ref, gwg_ref, f_ref):
        i = pl.program_id(0)

        @pl.when(i == 0)
        def _():
            small_ref[...] = jnp.zeros_like(small_ref)
            gwo_ref[...] = jnp.zeros_like(gwo_ref)
            gwp_ref[...] = jnp.zeros_like(gwp_ref)
            gwg_ref[...] = jnp.zeros_like(gwg_ref)

        h1 = x_ref[...] + _nn(mix_ref[...], wo_ref[...])
        dh1_ref[...] = h1
        r2 = _rms_scale(h1)
        hn2_ref[...] = ((h1 * r2) * g2_ref[...]).astype(BF)
        h2 = h1
        for c in range(n_ff):
            f = _nn(hn2_ref[...], w1_ref[c])
            f_ref[:, c * fc:(c + 1) * fc] = f
            a = jnp.square(jnp.maximum(f, 0.0)).astype(BF)
            a_ref[:, c * fc:(c + 1) * fc] = a
            h2 = h2 + _nn(a, w2_ref[c * fc:(c + 1) * fc, :])
        h2b = h2.astype(BF)
        gate = jax.nn.sigmoid(_nn(h2b, wg_ref[...]))
        pb = p_ref[...].astype(BF)
        pp = _nn(pb, wp_ref[...])
        h3 = h2 + gate * pp
        rf = _rms_scale(h3)
        gf_v = gf_ref[...]
        err = (h3 * rf) * gf_v - t_ref[...]
        small_ref[2:3, :] += jnp.sum(jnp.sum(err * err, axis=-1, keepdims=True), axis=0, keepdims=True) * (0.5 / D_MODEL)

        dy = err * (1.0 / D_MODEL)
        small_ref[1:2, :] += jnp.sum(dy * (h3 * rf), axis=0, keepdims=True)
        dh3 = _rms_bwd(dy * gf_v, h3, rf)
        gw_proj = _tn(pb, (dh3 * gate).astype(BF))
        for q in range(N_DEV):
            gwp_ref[q] += gw_proj[:, q * pc:(q + 1) * pc]
        dgl = ((dh3 * pp) * (gate * (1.0 - gate))).astype(BF)
        gwg_ref[...] += _tn(h2b, dgl)
        dh2 = dh3 + _nt(dgl, wg_ref[...])
        dmix_ref[...] = dh2
        dh2_ref[...] = dh2.astype(BF)
        dhn2 = jnp.zeros((tm, D_MODEL), F32)
        for c in range(n_ff):
            da = _nt(dh2_ref[...], w2_ref[c * fc:(c + 1) * fc, :])
            df = (da * (2.0 * jnp.maximum(f_ref[:, c * fc:(c + 1) * fc], 0.0))).astype(BF)
            df_ref[:, c * fc:(c + 1) * fc] = df
            dhn2 = dhn2 + _nt(df, w1_ref[c])
        h1 = dh1_ref[...]
        small_ref[0:1, :] += jnp.sum(dhn2 * (h1 * r2), axis=0, keepdims=True)
        dh1 = dmix_ref[...] + _rms_bwd(dhn2 * g2_ref[...], h1, r2)
        dh1_ref[...] = dh1
        dh1b = dh1.astype(BF)
        gwo_ref[...] += _tn(mix_ref[...], dh1b)
        dmix_ref[...] = _nt(dh1b, wo_ref[...])

    tile = lambda cols: pl.BlockSpec((tm, cols), lambda i: (i, 0))
    whole = lambda shape: pl.BlockSpec(shape, lambda i: (0,) * len(shape), pipeline_mode=pl.Buffered(1))
    total = lambda shape: pl.BlockSpec(shape, lambda i: (0,) * len(shape))
    row = pl.BlockSpec((1, D_MODEL), lambda i: (0, 0))
    act = lambda cols, dt: jax.ShapeDtypeStruct((s, cols), dt)
    gw_shapes = [(D_MODEL, D_MODEL), (N_DEV, PLE_DIM, pc), (D_MODEL, D_MODEL)]
    return _call(
        body,
        name="tail",
        grid=(s // tm,),
        in_specs=[tile(D_MODEL), tile(D_MODEL), tile(PLE_DIM), tile(D_MODEL), row, row,
                  whole(w_out.shape), whole(w_ff1.shape), whole(w_ff2.shape), whole(w_gate.shape), whole(w_proj.shape)],
        out_specs=[total((8, D_MODEL)), tile(D_MODEL), tile(D_MODEL), tile(D_MODEL), tile(D_FF), tile(D_FF), tile(D_MODEL),
                   *[total(shape) for shape in gw_shapes]],
        out_shape=[jax.ShapeDtypeStruct((8, D_MODEL), F32),
                   act(D_MODEL, F32), act(D_MODEL, F32), act(D_MODEL, BF), act(D_FF, BF), act(D_FF, BF), act(D_MODEL, BF),
                   *[jax.ShapeDtypeStruct(shape, F32) for shape in gw_shapes]],
        scratch_shapes=[pltpu.VMEM((tm, D_FF), F32)],
        compiler_params=_params(("arbitrary",)),
    )(x, mix, p, t, g2, gf, w_out, w_ff1, w_ff2, w_gate, w_proj)


def _mixer_bwd(zuv, qkv, dmix, saved, gv, w_sp, b_sp, bucket, after=None):
    s = zuv.shape[0]
    nb = s // BLOCK

    def body(zuv_ref, q_ref, kvc_ref, kvp_ref, gv_ref, w_ref, b_ref, bucket_ref, dmix_ref, tanh_ref, prob_ref, psink_ref,
             dzm_ref, dkv_ref, dgv_ref, dw_ref, db_ref, dattn_ref,
             dbias_ref, carry_ref, dsink_acc, db_acc):
        i = pl.program_id(0)

        @pl.when(i == 0)
        def _():
            dbias_ref[...] = jnp.zeros_like(dbias_ref)
            carry_ref[...] = jnp.zeros_like(carry_ref)
            dsink_acc[...] = jnp.zeros_like(dsink_acc)
            dgv_ref[...] = jnp.zeros_like(dgv_ref)
            dw_ref[...] = jnp.zeros_like(dw_ref)
            db_acc[...] = jnp.zeros_like(db_acc)

        @pl.when(i < nb)
        def _():
            u, du_dz = _gelu_and_grad(zuv_ref[:, :D_GMLP], tanh_ref[:, :D_GMLP])
            vg, dvg_dz = _gelu_and_grad(zuv_ref[:, D_GMLP:], tanh_ref[:, D_GMLP:])
            for g in range(GROUPS):
                sl = slice(g * GDIM, (g + 1) * GDIM)
                vg_g = vg[:, sl]
                rg = _rms_scale(vg_g)
                vhat = vg_g * rg
                gain = gv_ref[:, sl]
                vn = (vhat * gain).astype(BF)
                w_g = _tril_bf16(w_ref, g)
                sv = _nn(w_g, vn) + b_ref[g]
                dmix_a = dmix_ref[:, sl]
                dsv = dmix_a * u[:, sl]
                dsvb = dsv.astype(BF)
                db_acc[g] += jnp.sum(dsv, axis=-1, keepdims=True)
                dw_ref[g] += _nt(dsvb, vn)
                dvn = _tn(w_g, dsvb)
                dgv_ref[:, sl] += jnp.sum(dvn * vhat, axis=0, keepdims=True)
                dvg = _rms_bwd(dvn * gain, vg_g, rg)
                dzm_ref[:, sl] = ((dmix_a * sv) * du_dz[:, sl]).astype(BF)
                dzm_ref[:, D_GMLP + g * GDIM: D_GMLP + (g + 1) * GDIM] = (dvg * dvg_dz[:, sl]).astype(BF)

            for kh in range(N_KV):
                ksl = slice(kh * HEAD_DIM, (kh + 1) * HEAD_DIM)
                vsl = slice(D_KV + kh * HEAD_DIM, D_KV + (kh + 1) * HEAD_DIM)
                k_prev, k_cur = kvp_ref[:, ksl], kvc_ref[:, ksl]
                v_prev, v_cur = kvp_ref[:, vsl], kvc_ref[:, vsl]
                q4 = _stack_heads(q_ref, kh, 0)
                p_prev, p_cur, p_sink = prob_ref[0, kh, :, :BLOCK], prob_ref[0, kh, :, BLOCK:], psink_ref[0, kh, :, 0:1]
                do4 = _stack_heads(dmix_ref, kh, D_GMLP)
                dp_prev = _nt(do4, v_prev)
                dp_cur = _nt(do4, v_cur)
                delta = jnp.sum(p_prev * dp_prev, axis=-1, keepdims=True) + jnp.sum(p_cur * dp_cur, axis=-1, keepdims=True)
                ds_prev = p_prev * (dp_prev - delta)
                ds_cur = p_cur * (dp_cur - delta)
                dsink_acc[kh] -= p_sink * delta
                dbias_ref[kh, :, :BLOCK] += ds_prev
                dbias_ref[kh, :, BLOCK:] += ds_cur
                dsb_prev = ds_prev.astype(BF)
                dsb_cur = ds_cur.astype(BF)
                dq4 = (_nn(dsb_prev, k_prev) + _nn(dsb_cur, k_cur)) * SCALE
                for g in range(Q_PER_KV):
                    first = 2 * D_GMLP + (kh * Q_PER_KV + g) * HEAD_DIM
                    dzm_ref[:, first:first + HEAD_DIM] = dq4[g * BLOCK:(g + 1) * BLOCK].astype(BF)
                dkv_ref[:, ksl] = (carry_ref[:, ksl] + _tn(dsb_prev, q4) * SCALE).astype(BF)
                dkv_ref[:, vsl] = (carry_ref[:, vsl] + _tn(p_prev.astype(BF), do4)).astype(BF)
                carry_ref[:, ksl] = _tn(dsb_cur, q4) * SCALE
                carry_ref[:, vsl] = _tn(p_cur.astype(BF), do4)

        @pl.when(i == nb)
        def _():
            dkv_ref[...] = carry_ref[...].astype(BF)
            row = lax.broadcasted_iota(jnp.int32, (BLOCK, BLOCK), 0)
            col = lax.broadcasted_iota(jnp.int32, (BLOCK, BLOCK), 1)
            for g in range(GROUPS):
                dw_ref[g] = jnp.where(col <= row, dw_ref[g], 0.0)
                db_ref[g:g + 1, :] = jnp.sum(jnp.where(col == row, db_acc[g], 0.0), axis=0, keepdims=True)
            bucket = bucket_ref[...]
            for b in range(N_Q, REL_BUCKETS):
                dattn_ref[N_Q, b] = 0.0
            for h in range(N_Q):
                kh, rows = _head_rows(h)
                dattn_ref[N_Q, h] = jnp.sum(dsink_acc[kh, rows, :])
                dbh = dbias_ref[kh, rows, :]
                for b in range(REL_BUCKETS):
                    dattn_ref[h, b] = jnp.sum(jnp.where(bucket == b, dbh, 0.0))

    cl = lambda i: jnp.minimum(i, nb - 1)
    const = lambda shape: pl.BlockSpec(shape, lambda i: (0,) * len(shape))
    return _call(
        body,
        name="mixer_bwd",
        after=after,
        grid=(nb + 1,),
        in_specs=_mixer_specs(nb)[:7] + [
            const((BLOCK, 2 * BLOCK)),
            pl.BlockSpec((BLOCK, D_MODEL), lambda i: (cl(i), 0)),
            pl.BlockSpec((BLOCK, 2 * D_GMLP), lambda i: (cl(i), 0)),
            pl.BlockSpec((1, N_KV, ROWS4, 2 * BLOCK), lambda i: (cl(i), 0, 0, 0)),
            pl.BlockSpec((1, N_KV, ROWS4, LANES), lambda i: (cl(i), 0, 0, 0)),
        ],
        out_specs=[
            pl.BlockSpec((BLOCK, D_MAIN), lambda i: (cl(i), 0)),
            pl.BlockSpec((BLOCK, 2 * D_KV), lambda i: (jnp.maximum(i - 1, 0), 0)),
            const((1, D_GMLP)),
            const((GROUPS, BLOCK, BLOCK)),
            const((GROUPS, BLOCK)),
            pl.BlockSpec(memory_space=pltpu.SMEM),
        ],
        out_shape=[
            jax.ShapeDtypeStruct((s, D_MAIN), BF),
            jax.ShapeDtypeStruct((s, 2 * D_KV), BF),
            jax.ShapeDtypeStruct((1, D_GMLP), F32),
            jax.ShapeDtypeStruct((GROUPS, BLOCK, BLOCK), F32),
            jax.ShapeDtypeStruct((GROUPS, BLOCK), F32),
            jax.ShapeDtypeStruct((N_Q + 1, REL_BUCKETS), F32),
        ],
        scratch_shapes=[
            pltpu.VMEM((N_KV, ROWS4, 2 * BLOCK), F32),
            pltpu.VMEM((BLOCK, 2 * D_KV), F32),
            pltpu.VMEM((N_KV, ROWS4, 1), F32),
            pltpu.VMEM((GROUPS, BLOCK, 1), F32),
        ],
        compiler_params=_params(("arbitrary",)),
    )(zuv, qkv, qkv, qkv, gv, w_sp, b_sp, bucket, dmix, *saved)


def _in_bwd(x, dh1, dzm, dkv, g1, w_in_t, after=None):
    s = x.shape[0]
    tm = min(IO_TOK_TILE, s)

    def body(x_ref, dh1_ref, dzm_ref, dkv_ref, g_ref, w_ref, dx_ref, dg_ref):
        @pl.when(pl.program_id(0) == 0)
        def _():
            dg_ref[...] = jnp.zeros_like(dg_ref)

        dhn = _nn(dzm_ref[...], w_ref[:D_MAIN, :]) + _nn(dkv_ref[...], w_ref[D_MAIN:, :])
        xv = x_ref[...]
        r = _rms_scale(xv)
        dg_ref[...] += jnp.sum(dhn * (xv * r), axis=0, keepdims=True)
        dx_ref[...] = dh1_ref[...] + _rms_bwd(dhn * g_ref[...], xv, r)

    tile = lambda cols: pl.BlockSpec((tm, cols), lambda i: (i, 0))
    row = pl.BlockSpec((1, D_MODEL), lambda i: (0, 0))
    return _call(
        body,
        name="in_bwd",
        after=after,
        grid=(s // tm,),
        in_specs=[tile(D_MODEL), tile(D_MODEL), tile(D_MAIN), tile(2 * D_KV), row, pl.BlockSpec((D_IN, D_MODEL), lambda i: (0, 0))],
        out_specs=[tile(D_MODEL), row],
        out_shape=[jax.ShapeDtypeStruct((s, D_MODEL), F32), jax.ShapeDtypeStruct((1, D_MODEL), F32)],
        compiler_params=_params(("arbitrary",)),
    )(x, dh1, dzm, dkv, g1, w_in_t)


def _wgrad_in(dzm, dkv, hn, after=None):
    s = hn.shape[0]
    tm = 2 * D_KV
    n_main = D_MAIN // tm

    def body(dzm_ref, dkv_ref, hn_ref, o_ref):
        i = pl.program_id(0)

        @pl.when(i < n_main)
        def _():
            o_ref[...] = _tn(dzm_ref[...], hn_ref[...])

        @pl.when(i == n_main)
        def _():
            o_ref[...] = _tn(dkv_ref[...], hn_ref[...])

    return _call(
        body,
        name="wgrad_in",
        after=after,
        grid=(n_main + 1,),
        in_specs=[
            pl.BlockSpec((s, tm), lambda i: (0, jnp.minimum(i, n_main - 1))),
            pl.BlockSpec((s, tm), lambda i: (0, 0)),
            pl.BlockSpec((s, D_MODEL), lambda i: (0, 0)),
        ],
        out_specs=pl.BlockSpec((tm, D_MODEL), lambda i: (i, 0)),
        out_shape=jax.ShapeDtypeStruct((D_IN, D_MODEL), F32),
        compiler_params=_params(("arbitrary",)),
    )(dzm, dkv, hn)


def _wgrad(a, b, tm, tn, name, peer_cols=0, after=None):
    s, m = a.shape
    n = b.shape[1]

    def body(a_ref, b_ref, o_ref, at_ref):
        @pl.when(pl.program_id(1) == 0)
        def _():
            at_ref[...] = a_ref[...].astype(BF).T

        r = _nn(at_ref[...], b_ref[...])
        if peer_cols:
            for q in range(tn // peer_cols):
                o_ref[q] = r[:, q * peer_cols:(q + 1) * peer_cols]
        else:
            o_ref[...] = r

    if peer_cols:
        out_spec = pl.BlockSpec((tn // peer_cols, tm, peer_cols), lambda i, j: (j, i, 0))
        out_shape = jax.ShapeDtypeStruct((n // peer_cols, m, peer_cols), F32)
    else:
        out_spec = pl.BlockSpec((tm, tn), lambda i, j: (i, j))
        out_shape = jax.ShapeDtypeStruct((m, n), F32)
    return _call(
        body,
        name=name,
        after=after,
        grid=(m // tm, n // tn),
        in_specs=[pl.BlockSpec((s, tm), lambda i, j: (0, i)), pl.BlockSpec((s, tn), lambda i, j: (0, j))],
        out_specs=out_spec,
        out_shape=out_shape,
        scratch_shapes=[pltpu.VMEM((tm, s), BF)],
        compiler_params=_params(("arbitrary", "arbitrary")),
    )(a, b)


def _adamw_math(w, g, m, v):
    m_new = ADAM_B1 * m + (1.0 - ADAM_B1) * g
    v_new = ADAM_B2 * v + (1.0 - ADAM_B2) * jnp.square(g)
    m_hat = m_new / (1.0 - ADAM_B1 ** ADAM_STEP)
    v_hat = v_new / (1.0 - ADAM_B2 ** ADAM_STEP)
    delta = -ADAM_LR * (m_hat / (jnp.sqrt(v_hat) + ADAM_EPS) + ADAM_WD * w)
    return delta, m_new, v_new


def _final_adamw(part, recv, w, m, v, name):
    r, c = w.shape
    tr = min(r, 512)

    def body(p_ref, r_ref, w_ref, m_ref, v_ref, g_ref, d_ref, mo_ref, vo_ref):
        g = p_ref[...]
        for j in range(3):
            g = g + r_ref[j].astype(F32)
        g_ref[...] = g
        d_ref[...], mo_ref[...], vo_ref[...] = _adamw_math(w_ref[...], g, m_ref[...], v_ref[...])

    spec = pl.BlockSpec((tr, c), lambda i: (i, 0))
    return _call(
        body,
        name=name,
        grid=(r // tr,),
        in_specs=[spec, pl.BlockSpec((3, tr, c), lambda i: (0, i, 0)), spec, spec, spec],
        out_specs=[spec] * 4,
        out_shape=[jax.ShapeDtypeStruct((r, c), F32)] * 4,
        compiler_params=_params(("arbitrary",)),
    )(part, recv, w, m, v)


def _rs_sum(g, land, blocks, name):
    _, r, c = g.shape
    tr = min(r, 256)

    def body(blk_ref, g0_ref, g1_ref, g2_ref, g3_ref, l_ref, part_ref, send_ref):
        part_ref[...] = g0_ref[0] + l_ref[0]
        for j, gj_ref in enumerate((g1_ref, g2_ref, g3_ref)):
            send_ref[j] = (gj_ref[0] + l_ref[j + 1]).astype(BF)

    def pick(j):
        return pl.BlockSpec((1, tr, c), lambda i, blk: (blk[j], i, 0))

    return _call(
        body,
        name=name,
        grid_spec=pltpu.PrefetchScalarGridSpec(
            num_scalar_prefetch=1,
            grid=(r // tr,),
            in_specs=[pick(0), pick(1), pick(2), pick(3), pl.BlockSpec((4, tr, c), lambda i, blk: (0, i, 0))],
            out_specs=[pl.BlockSpec((tr, c), lambda i, blk: (i, 0)), pl.BlockSpec((3, tr, c), lambda i, blk: (0, i, 0))],
        ),
        out_shape=[jax.ShapeDtypeStruct((r, c), F32), jax.ShapeDtypeStruct((3, r, c), BF)],
        compiler_params=_params(("arbitrary",)),
    )(blocks, g, g, g, g, land)


def _final_adamw_few(parts, recvs, ws, ms, vs, name):
    n = len(ws)

    def body(*refs):
        ins, outs = refs[:5 * n], refs[5 * n:]
        for k in range(n):
            p_ref, r_ref, w_ref, m_ref, v_ref = (ins[q * n + k] for q in range(5))
            g = p_ref[...]
            for j in range(3):
                g = g + r_ref[j].astype(F32)
            outs[k][...] = g
            outs[n + k][...], outs[2 * n + k][...], outs[3 * n + k][...] = _adamw_math(w_ref[...], g, m_ref[...], v_ref[...])

    outs = _call(
        body,
        name=name,
        out_shape=[jax.ShapeDtypeStruct(w.shape, F32) for w in ws] * 4,
        compiler_params=_params(),
    )(*parts, *recvs, *ws, *ms, *vs)
    return outs[:n], outs[n:2 * n], outs[2 * n:3 * n], outs[3 * n:]


def _rs_sum_few(gs, lands, blocks, name):
    n = len(gs)

    def body(blk_ref, *refs):
        g_refs, l_refs, part_refs, send_refs = refs[:n], refs[n:2 * n], refs[2 * n:3 * n], refs[3 * n:]
        for k in range(n):
            part_refs[k][...] = g_refs[k][blk_ref[0]] + l_refs[k][0]
            for j in range(3):
                send_refs[k][j] = (g_refs[k][blk_ref[j + 1]] + l_refs[k][j + 1]).astype(BF)

    outs = _call(
        body,
        name=name,
        in_specs=[pl.BlockSpec(memory_space=pltpu.SMEM)] + [pl.BlockSpec(memory_space=pltpu.VMEM)] * (2 * n),
        out_shape=[jax.ShapeDtypeStruct(g.shape[1:], F32) for g in gs] + [jax.ShapeDtypeStruct((3,) + g.shape[1:], BF) for g in gs],
        compiler_params=_params(),
    )(blocks, *gs, *lands)
    return [(outs[k], outs[n + k]) for k in range(n)]


def _adamw_small(own, recv, g1_own, g1_recv, weights, moms, vels, after):
    n_w = len(weights)

    def body(*refs):
        own_refs, recv_refs, g1_own_ref, g1_recv_ref = refs[:5], refs[5:10], refs[10], refs[11]
        w_refs, m_refs, v_refs = refs[12:12 + n_w], refs[12 + n_w:12 + 2 * n_w], refs[12 + 2 * n_w:12 + 3 * n_w]
        outs = refs[12 + 3 * n_w:]
        loss_ref, g_refs, d_refs = outs[0], outs[1:1 + n_w], outs[1 + n_w:1 + 2 * n_w]
        mo_refs, vo_refs = outs[1 + 2 * n_w:1 + 3 * n_w], outs[1 + 3 * n_w:]
        x, y, c = lax.axis_index("x"), lax.axis_index("y"), lax.axis_index("c")

        def in_place_order(values, my_place):
            acc = None
            for place in range(len(values)):
                r = place ^ my_place
                term = values[-1]
                for q in range(len(values) - 2, -1, -1):
                    term = jnp.where(r == q, values[q], term)
                acc = term if acc is None else acc + term
            return acc

        def total(k, *index):
            index = index or (slice(None),) * (len(own_refs[k].shape) - 1)
            across = [own_refs[k][(0, *index)], recv_refs[k][(1, *index)], recv_refs[k][(0, *index)], recv_refs[k][(2, *index)]]
            return in_place_order([v.astype(F32) for v in across], 2 * x + y)

        g1 = in_place_order([g1_own_ref[0]] + [g1_recv_ref[j] for j in range(N_DEV - 1)], 4 * x + 2 * y + c)
        grads = [
            g1, total(1), total(2), total(3),
            total(4, slice(N_Q, None), slice(0, N_Q)), total(4, slice(0, N_Q), slice(None)),
            total(0, slice(0, 1), slice(None)), total(0, slice(1, 2), slice(None))]
        loss_ref[...] = total(0, slice(2, 3), slice(0, 1))
        for k in range(n_w):
            g_refs[k][...] = grads[k]
            d_refs[k][...], mo_refs[k][...], vo_refs[k][...] = _adamw_math(w_refs[k][...], grads[k], m_refs[k][...], v_refs[k][...])

    shapes = [jax.ShapeDtypeStruct(w.shape, F32) for w in weights]
    outs = _call(
        body,
        name="adamw_small",
        after=after,
        in_specs=[pl.BlockSpec(memory_space=pltpu.VMEM)] * (12 + 3 * n_w),
        out_shape=[jax.ShapeDtypeStruct((1, 1), F32)] + shapes * 4,
        compiler_params=_params(),
    )(*own, *recv, g1_own, g1_recv, *weights, *moms, *vels)
    return outs[0], outs[1:1 + n_w], outs[1 + n_w:1 + 2 * n_w], outs[1 + 2 * n_w:1 + 3 * n_w], outs[1 + 3 * n_w:]


def _place():
    x, y, c = lax.axis_index("x"), lax.axis_index("y"), lax.axis_index("c")
    return x, y, c, [(1 - x, y), (x, 1 - y), (1 - x, 1 - y)]


def _dev_index(px, py, pc):
    return 4 * px + 2 * py + pc


HBM_SPEC = pl.BlockSpec(memory_space=pltpu.HBM)
SEM_SPEC = pl.BlockSpec(memory_space=pltpu.SEMAPHORE)
ANY_SPEC = pl.BlockSpec(memory_space=pl.ANY)
DATAFLOW = pltpu.SideEffectType.DATAFLOW_SIDE_EFFECTING


def _hbm(a):
    return pltpu.with_memory_space_constraint(a, pltpu.HBM)


def _prep_weights(shards):
    k_n = len(shards)

    def body(*refs):
        ins, outs, stage, sems = refs[:k_n], refs[k_n:2 * k_n], refs[2 * k_n:3 * k_n], refs[3 * k_n]
        x, y, c, _ = _place()
        copies = []
        for k in range(k_n):
            stage[k][...] = ins[k][...].astype(BF)
            copies.append(pltpu.make_async_copy(stage[k], outs[k].at[_dev_index(x, y, c)], sems.at[k]))
            copies[k].start()
        for cp in copies:
            cp.wait()

    return _call(
        body,
        name="prep_weights",
        in_specs=[pl.BlockSpec(memory_space=pltpu.VMEM)] * k_n,
        out_specs=[ANY_SPEC] * k_n,
        out_shape=[jax.ShapeDtypeStruct((N_DEV,) + sh.shape, BF) for sh in shards],
        scratch_shapes=[pltpu.VMEM(sh.shape, BF) for sh in shards] + [pltpu.SemaphoreType.DMA((k_n,))],
        compiler_params=_params(),
    )(*shards)


def _ag_piece(land_k, block, half, peer, send_sem, recv_sem):
    ref = land_k.at[_dev_index(*block)]
    if half is not None:
        rows = land_k.shape[1] // 2
        ref = ref.at[pl.ds(half * rows, rows)]
    return pltpu.make_async_remote_copy(
        src_ref=ref, dst_ref=ref, send_sem=send_sem, recv_sem=recv_sem, device_id=peer, device_id_type=MESH)


def _ag_plan():
    x, y, c, _ = _place()
    me, sib = (x, y, c), (x, y, 1 - c)
    xn, yn, diag = (1 - x, y, c), (x, 1 - y, c), (1 - x, 1 - y, c)
    return dict(
        relay_halves=[(me, 0, xn), (me, 1, yn)],
        others=[(me, None, sib), (me, 1, xn), (me, 0, yn)],
        relays=[(xn, 0, yn), (yn, 1, xn)],
        near=[(xn, None, sib), (yn, None, sib)],
        far=[(diag, None, sib)],
    )


def _ag_stage(land, stage, send_sems, recv_sems, act):
    copies = _ag_plan()[stage]
    n = len(copies)
    for k in range(len(land)):
        for j, (block, half, peer) in enumerate(copies):
            cp = _ag_piece(land[k], block, half, peer, send_sems.at[n * k + j], recv_sems.at[n * k + j])
            if act == "start":
                cp.start()
            else:
                cp.wait_send()
                cp.wait_recv()


def _sem_shapes(*counts):
    return [pltpu.SemaphoreType.DMA((n,)) for n in counts for _ in range(2)]


def _ag_start(first, rest):
    lands = list(first) + list(rest)
    k_n, k_first = len(lands), len(first)
    k_rest = k_n - k_first

    def body(*refs):
        land = refs[:k_n]
        sems = refs[k_n:k_n + 6]
        token = refs[-1]
        x, y, c, chips = _place()
        targets = [(x, y, 1 - c)] + [(*chip, c) for chip in chips]
        for k in range(k_first):
            for j, to in enumerate(targets):
                _ag_piece(land[k], (x, y, c), None, to, sems[0].at[4 * k + j], sems[1].at[4 * k + j]).start()
        _ag_stage(land[k_first:], "relay_halves", sems[2], sems[3], "start")
        _ag_stage(land[k_first:], "others", sems[4], sems[5], "start")
        token[...] = jnp.zeros_like(token)

    outs = pl.pallas_call(
        body,
        name="ag_start",
        in_specs=[HBM_SPEC] * k_n,
        out_specs=(*[SEM_SPEC] * 6, *[HBM_SPEC] * k_n, pl.BlockSpec(memory_space=pltpu.VMEM)),
        out_shape=(*_sem_shapes(4 * k_first, 2 * k_rest, 3 * k_rest),
                   *[pltpu.HBM(a.shape, a.dtype) for a in lands], jax.ShapeDtypeStruct((8, LANES), F32)),
        input_output_aliases={k: 6 + k for k in range(k_n)},
        compiler_params=pltpu.CompilerParams(has_side_effects=DATAFLOW),
    )(*[_hbm(a) for a in lands])
    flying = list(outs[6:6 + k_n])
    return (outs[0], outs[1], flying[:k_first]), (outs[2:6], flying[k_first:]), outs[-1]


def _ag_split_call(lands, waits, starts, after, name):
    k_n = len(lands)
    plan_sizes = dict(relay_halves=2, others=3, relays=2, near=2, far=1)
    n_in, n_out = 2 * len(waits), 2 * len(starts)

    def body(*refs):
        land = refs[:k_n]
        in_sems = refs[k_n:k_n + n_in]
        out_sems, token = refs[len(refs) - 1 - n_out:len(refs) - 1], refs[-1]
        for w, (stage, _, _) in enumerate(waits):
            _ag_stage(land, stage, in_sems[2 * w], in_sems[2 * w + 1], "wait")
            if w < len(starts):
                _ag_stage(land, starts[w], out_sems[2 * w], out_sems[2 * w + 1], "start")
        token[...] = jnp.zeros_like(token)

    outs = pl.pallas_call(
        body,
        name=name,
        in_specs=[HBM_SPEC] * k_n + [SEM_SPEC] * n_in + [ANY_SPEC],
        out_specs=(*[HBM_SPEC] * k_n, *[SEM_SPEC] * n_out, pl.BlockSpec(memory_space=pltpu.VMEM)),
        out_shape=(*[pltpu.HBM(a.shape, a.dtype) for a in lands], *_sem_shapes(*[plan_sizes[s] * k_n for s in starts]),
                   jax.ShapeDtypeStruct((8, LANES), F32)),
        input_output_aliases={k: k for k in range(k_n)},
        compiler_params=pltpu.CompilerParams(has_side_effects=DATAFLOW),
    )(*lands, *[s for _, a, b in waits for s in (a, b)], after)
    return list(outs[:k_n]), list(outs[k_n:k_n + n_out]), outs[-1]


def _ag_mid(lands, send_sems, recv_sems, after, name):
    k_n = len(lands)

    def body(*refs):
        land = refs[:k_n]
        send1, recv1 = refs[k_n], refs[k_n + 1]
        fwd_send, fwd_recv = refs[-2], refs[-1]
        x, y, c, chips = _place()
        sources = [(x, y, 1 - c)] + [(*chip, c) for chip in chips]
        for k in range(k_n):
            mine = land[k].at[_dev_index(x, y, c)]
            for j, frm in enumerate(sources):
                got = land[k].at[_dev_index(*frm)]
                cp = pltpu.make_async_remote_copy(
                    src_ref=mine, dst_ref=got, send_sem=send1.at[4 * k + j], recv_sem=recv1.at[4 * k + j], device_id=frm, device_id_type=MESH)
                cp.wait_send()
                cp.wait_recv()
                if j >= 1:
                    pltpu.make_async_remote_copy(
                        src_ref=got, dst_ref=got, send_sem=fwd_send.at[3 * k + j - 1], recv_sem=fwd_recv.at[3 * k + j - 1],
                        device_id=(x, y, 1 - c), device_id_type=MESH).start()

    outs = pl.pallas_call(
        body,
        name=name,
        in_specs=[HBM_SPEC] * k_n + [SEM_SPEC, SEM_SPEC, ANY_SPEC],
        out_specs=(*[HBM_SPEC] * k_n, SEM_SPEC, SEM_SPEC),
        out_shape=(*[pltpu.HBM(a.shape, a.dtype) for a in lands], pltpu.SemaphoreType.DMA((3 * k_n,)), pltpu.SemaphoreType.DMA((3 * k_n,))),
        input_output_aliases={k: k for k in range(k_n)},
        compiler_params=pltpu.CompilerParams(has_side_effects=DATAFLOW),
    )(*lands, send_sems, recv_sems, after)
    return list(outs[:k_n]), outs[-2], outs[-1]


def _ag_end(lands, fwd_send, fwd_recv, name):
    k_n = len(lands)

    def body(*refs):
        land = refs[:k_n]
        fsend, frecv = refs[k_n], refs[k_n + 1]
        x, y, c, chips = _place()
        for k in range(k_n):
            for j, chip in enumerate(chips):
                cp = pltpu.make_async_remote_copy(
                    src_ref=land[k].at[_dev_index(*chip, c)], dst_ref=land[k].at[_dev_index(*chip, 1 - c)],
                    send_sem=fsend.at[3 * k + j], recv_sem=frecv.at[3 * k + j], device_id=(x, y, 1 - c), device_id_type=MESH)
                cp.wait_send()
                cp.wait_recv()

    outs = pl.pallas_call(
        body,
        name=name,
        in_specs=[HBM_SPEC] * k_n + [SEM_SPEC, SEM_SPEC],
        out_specs=tuple([HBM_SPEC] * k_n),
        out_shape=tuple(pltpu.HBM(a.shape, a.dtype) for a in lands),
        input_output_aliases={k: k for k in range(k_n)},
        compiler_params=pltpu.CompilerParams(has_side_effects=DATAFLOW),
    )(*lands, fwd_send, fwd_recv)
    return list(outs)


def _chips4():
    x, y, c, others = _place()
    return x, y, c, [(x, y)] + others


def _route_sibling(j):
    x, y, c, chips = _chips4()
    return _dev_index(*chips[j], 1 - c), j, (x, y, 1 - c)


def _route_chips(j):
    x, y, c, chips = _chips4()
    return j, j, (*chips[j + 1], c)


def _route_sibling_whole(j):
    x, y, c, _ = _chips4()
    return 0, 0, (x, y, 1 - c)


def _route_chips_whole(j):
    x, y, c, chips = _chips4()
    return 0, j, (*chips[j + 1], c)


def _route_everyone(j):
    x, y, c, _ = _chips4()
    flip = [(j + 1) >> 2 & 1, (j + 1) >> 1 & 1, (j + 1) & 1]
    return 0, j, tuple(1 - v if f else v for v, f in zip((x, y, c), flip))


def _xchg_copies(routes, src, dst, send_sems, recv_sems):
    copies, sem = [], 0
    for k, (route, n) in enumerate(routes):
        for j in range(n):
            si, di, peer = route(j)
            copies.append(pltpu.make_async_remote_copy(
                src_ref=src[k].at[si], dst_ref=dst[k].at[di], send_sem=send_sems.at[sem], recv_sem=recv_sems.at[sem],
                device_id=peer, device_id_type=MESH))
            sem += 1
    return copies


def _xchg_start(srcs, slot_shapes, routes, name, after=None):
    k_n = len(srcs)
    n_in = 2 * k_n + (after is not None)
    n_sem = sum(n for _, n in routes)
    dsts = [lax.empty((n,) + tuple(sh), a.dtype) for sh, a, (_, n) in zip(slot_shapes, srcs, routes)]

    def body(*refs):
        src, dst = refs[:k_n], refs[k_n:2 * k_n]
        send_sems, recv_sems, token = refs[n_in], refs[n_in + 1], refs[-1]
        for cp in _xchg_copies(routes, src, dst, send_sems, recv_sems):
            cp.start()
        token[...] = jnp.zeros_like(token)

    arrays = list(srcs) + dsts
    outs = pl.pallas_call(
        body,
        name=name,
        in_specs=[HBM_SPEC] * (2 * k_n) + [ANY_SPEC] * (n_in - 2 * k_n),
        out_specs=(SEM_SPEC, SEM_SPEC, *[HBM_SPEC] * (2 * k_n), pl.BlockSpec(memory_space=pltpu.VMEM)),
        out_shape=(pltpu.SemaphoreType.DMA((n_sem,)), pltpu.SemaphoreType.DMA((n_sem,)),
                   *[pltpu.HBM(a.shape, a.dtype) for a in arrays], jax.ShapeDtypeStruct((8, LANES), F32)),
        input_output_aliases={i: 2 + i for i in range(2 * k_n)},
        compiler_params=pltpu.CompilerParams(has_side_effects=DATAFLOW),
    )(*[_hbm(a) for a in arrays], *([] if after is None else [after]))
    return outs[0], outs[1], list(outs[2:2 + k_n]), list(outs[2 + k_n:2 + 2 * k_n]), outs[-1]


def _xchg_wait(send_sems, recv_sems, srcs, dsts, routes, after, name):
    k_n = len(srcs)

    def body(*refs):
        src, dst = refs[:k_n], refs[k_n:2 * k_n]
        for cp in _xchg_copies(routes, src, dst, refs[2 * k_n], refs[2 * k_n + 1]):
            cp.wait_send()
            cp.wait_recv()

    arrays = list(srcs) + list(dsts)
    outs = pl.pallas_call(
        body,
        name=name,
        in_specs=[HBM_SPEC] * (2 * k_n) + [SEM_SPEC, SEM_SPEC, ANY_SPEC],
        out_specs=tuple([HBM_SPEC] * (2 * k_n)),
        out_shape=tuple(pltpu.HBM(a.shape, a.dtype) for a in arrays),
        input_output_aliases={i: i for i in range(2 * k_n)},
        compiler_params=pltpu.CompilerParams(has_side_effects=DATAFLOW),
    )(*arrays, send_sems, recv_sems, after)
    return list(outs[:k_n]), list(outs[k_n:])


SMALL = ("norm1_gain", "gmlp_v_gain", "w_spatial", "b_spatial", "attn_sinks", "rel_bias_table", "norm2_gain", "final_gain")
LANES = 128


def _swap_start(grads, smalls, tag, after=None):
    srcs = list(grads) + [a[None] for a in smalls]
    shapes = [g.shape[1:] for g in grads] + [a.shape for a in smalls]
    routes = [(_route_sibling, 4)] * len(grads) + [(_route_sibling_whole, 1)] * len(smalls)
    return _xchg_start(srcs, shapes, routes, f"rs_{tag}_swap_start", after), routes, len(grads)


def _swap_sums(swap, names, after, tag, few=0, small_dtypes=None):
    (send1, recv1, src1, land1, _), routes, n_rs = swap
    x, y, c, chips = _chips4()
    blocks = jnp.stack([_dev_index(*chip, c) for chip in chips]).astype(jnp.int32)
    src1, land1 = _xchg_wait(send1, recv1, src1, land1, routes, after, f"rs_{tag}_swap_wait")
    n_tiled = n_rs - few
    sums = [_rs_sum(g, land, blocks, f"rs_sum_{n}") for g, land, n in zip(src1[:n_tiled], land1[:n_tiled], names)]
    if few:
        sums += _rs_sum_few(src1[n_tiled:n_rs], land1[n_tiled:n_rs], blocks, f"rs_{tag}_sum")
    if len(src1) == n_rs:
        return sums, []
    return sums, _pair_sum(src1[n_rs:], land1[n_rs:], small_dtypes or [F32] * (len(src1) - n_rs), f"rs_{tag}_sum_small")


def _pair_sum(mine, theirs, dtypes, name):
    def body(*refs):
        n = len(refs) // 3
        for k in range(n):
            refs[2 * n + k][...] = (refs[k][...] + refs[n + k][...]).astype(dtypes[k])

    return _call(
        body,
        name=name,
        out_shape=[jax.ShapeDtypeStruct(a.shape, dt) for a, dt in zip(mine, dtypes)],
        compiler_params=_params(),
    )(*mine, *theirs)


def _chips_start(sums, small_sums, tag, everyone=()):
    sends = [ps[1] for ps in sums] + list(small_sums) + [a[None] for a in everyone]
    routes = [(_route_chips, 3)] * len(sums) + [(_route_chips_whole, 3)] * len(small_sums) + [(_route_everyone, 7)] * len(everyone)
    return _xchg_start(sends, [a.shape[1:] for a in sends], routes, f"rs_{tag}_chips_start"), routes


def _chips_wait(chips, after, tag):
    (send2, recv2, src2, land2, _), routes = chips
    return _xchg_wait(send2, recv2, src2, land2, routes, after, f"rs_{tag}_chips_wait")


def kernel(x, p, norm1_gain, w_in, gmlp_v_gain, w_spatial, b_spatial, attn_sinks, rel_bias_table, w_out, norm2_gain, w_ff1, w_ff2, w_ple_proj, w_ple_gate, final_gain, loss_target, m_norm1_gain, m_w_in, m_gmlp_v_gain, m_w_spatial, m_b_spatial, m_attn_sinks, m_rel_bias_table, m_w_out, m_norm2_gain, m_w_ff1, m_w_ff2, m_w_ple_proj, m_w_ple_gate, m_final_gain, v_norm1_gain, v_w_in, v_gmlp_v_gain, v_w_spatial, v_b_spatial, v_attn_sinks, v_rel_bias_table, v_w_out, v_norm2_gain, v_w_ff1, v_w_ff2, v_w_ple_proj, v_w_ple_gate, v_final_gain):
    args = dict(locals())
    s = x.shape[1]
    big = ("w_in", "w_out", "w_ff1", "w_ff2", "w_ple_proj", "w_ple_gate")

    x2, p2, t2 = x.reshape(s, D_MODEL), p.reshape(s, PLE_DIM), loss_target.reshape(s, D_MODEL)
    g1, gv, w_sp, b_sp, sinks, table, g2, gf = (args[n] for n in SMALL)
    bucket = jnp.asarray(_bucket_table())
    b_col = b_sp.reshape(GROUPS, BLOCK, 1)

    def shard(name):
        return args[name][0].T if name.endswith("w_in") else args[name][0]

    lands = _prep_weights([shard(n) for n in big])
    (send_in, recv_in, fly_in), (rest_sems, fly_rest), token = _ag_start(lands[:1], lands[1:])
    mid_in, fwd_send_in, fwd_recv_in = _ag_mid(fly_in, send_in, recv_in, token, "ag_mid_w_in")
    g_in = _ag_end(mid_in, fwd_send_in, fwd_recv_in, "ag_end_w_in")[0]
    full_in = g_in.reshape(D_IN, D_MODEL)

    zuv, qkv, hn1 = _in_proj(x2, g1, full_in)
    fly_rest, relay_sems, relayed = _ag_split_call(
        fly_rest, [("relay_halves", *rest_sems[:2])], ["relays"], zuv, "ag_relay")
    mix, *saved = _mixer_fwd(zuv, qkv, gv, w_sp[0], b_col, sinks, table.T, bucket, after=relayed)
    fly_rest, fwd_sems, _ = _ag_split_call(
        fly_rest, [("others", *rest_sems[2:]), ("relays", *relay_sems)], ["near", "far"], mix, "ag_mid_rest")
    g_out, g_ff1, g_ff2, g_proj, g_gate = _ag_split_call(
        fly_rest, [("near", *fwd_sems[:2]), ("far", *fwd_sems[2:])], [], mix, "ag_end_rest")[0]
    full_out, full_ff2, full_gate = g_out.reshape(D_MODEL, D_MODEL), g_ff2.reshape(D_FF, D_MODEL), g_gate.reshape(D_MODEL, D_MODEL)
    full_proj = g_proj.transpose(1, 0, 2).reshape(PLE_DIM, D_MODEL)

    tail_small, dh1, dmix, hn2, a, df, dh2, gw_out, gw_proj, gw_gate = _tail(
        x2, mix, p2, t2, g2, gf.reshape(1, D_MODEL), full_out, g_ff1, full_ff2, full_gate, full_proj)
    gw_out, gw_gate = (g.reshape(N_DEV, D_MODEL // N_DEV, D_MODEL) for g in (gw_out, gw_gate))

    few_names = ("w_out", "w_ple_proj", "w_ple_gate")
    ff1_names = ("w_ff1",) + few_names
    gw_ff1 = _wgrad(hn2, df, D_MODEL, 2 * D_FF // N_DEV, "wgrad_ff1", peer_cols=D_FF // N_DEV)
    ff1_swap = _swap_start([gw_ff1, gw_out, gw_proj, gw_gate], [tail_small], "ff1")
    gw_ff2 = _wgrad(a, dh2, 1024, D_MODEL, "wgrad_ff2", after=ff1_swap[0][4]).reshape(N_DEV, D_FF // N_DEV, D_MODEL)
    ff1_sums, ff1_small = _swap_sums(ff1_swap, ff1_names, gw_ff2, "ff1", few=len(few_names))
    ff1_chips = _chips_start(ff1_sums, ff1_small, "ff1")
    ff2_swap = _swap_start([gw_ff2], [], "ff2", after=ff1_chips[0][4])
    dzm, dkv, d_gv, d_wsp, d_bsp, d_attn = _mixer_bwd(
        zuv, qkv, dmix, saved, gv, w_sp[0], b_col, bucket, after=ff2_swap[0][4])
    ff2_sums, _ = _swap_sums(ff2_swap, ("w_ff2",), dzm, "ff2")
    ff2_chips = _chips_start(ff2_sums, [], "ff2")
    gw_in = _wgrad_in(dzm, dkv, hn1, after=ff2_chips[0][4]).reshape(N_DEV, D_IN // N_DEV, D_MODEL)
    in_swap = _swap_start([gw_in], [d_gv, d_wsp, d_bsp, d_attn], "in")
    dx, d_g1 = _in_bwd(x2, dh1, dzm, dkv, g1, full_in, after=in_swap[0][4])
    grad_x = dx.reshape(x.shape)
    in_sums, in_small = _swap_sums(in_swap, ("w_in",), dx, "in", small_dtypes=[F32, BF, F32, F32])
    in_chips = _chips_start(in_sums, in_small, "in", everyone=[d_g1])

    grads, deltas, new_m, new_v = {}, {}, {}, {}
    after, small_own, small_recv = in_chips[0][4], [], []
    for chips, sums, names, tag in ((ff1_chips, ff1_sums, ff1_names, "ff1"), (ff2_chips, ff2_sums, ("w_ff2",), "ff2"),
                                    (in_chips, in_sums, ("w_in",), "in")):
        sent, recvs = _chips_wait(chips, after, tag)
        small_own += sent[len(names):]
        small_recv += recvs[len(names):]
        tiled = [n for n in names if n not in few_names]
        done = [_final_adamw(part, recv, shard(n), shard("m_" + n), shard("v_" + n), "adamw_" + n)
                for n, (part, _), recv in zip(tiled, sums, recvs)]
        few = names[len(tiled):]
        if few:
            done += zip(*_final_adamw_few(
                [ps[0] for ps in sums[len(tiled):]], recvs[len(tiled):len(names)], [shard(n) for n in few],
                [shard("m_" + n) for n in few], [shard("v_" + n) for n in few], "adamw_few"))
        for n, results in zip(names, done):
            for dst, arr in zip((grads, deltas, new_m, new_v), results):
                dst[n] = (arr.T if n == "w_in" else arr)[None]
            after = results[1]

    views = {"w_spatial": (GROUPS, BLOCK, BLOCK), "b_spatial": (GROUPS, BLOCK), "final_gain": (1, D_MODEL)}
    def view(name, base):
        return args[name].T if base == "rel_bias_table" else args[name].reshape(views.get(base, args[name].shape))

    small_in = [[view(pre + n, n) for n in SMALL] for pre in ("", "m_", "v_")]
    loss, *small_out = _adamw_small(small_own[:-1], small_recv[:-1], small_own[-1], small_recv[-1], *small_in, after=after)
    for dst, arrays in zip((grads, deltas, new_m, new_v), small_out):
        for n, arr in zip(SMALL, arrays):
            dst[n] = arr.T if n == "rel_bias_table" else arr.reshape(args[n].shape)
    loss = loss[0, 0]

    order = ("norm1_gain", "w_in", "gmlp_v_gain", "w_spatial", "b_spatial", "attn_sinks", "rel_bias_table", "w_out",
             "norm2_gain", "w_ff1", "w_ff2", "w_ple_proj", "w_ple_gate", "final_gain")
    return (loss, grad_x, *[grads[n] for n in order], *[deltas[n] for n in order],
            *[new_m[n] for n in order], *[new_v[n] for n in order])
```

```python
import math

import numpy as np
import jax
import jax.numpy as jnp
from jax import lax
from jax.experimental import pallas as pl
from jax.experimental.pallas import tpu as pltpu

F32 = jnp.float32
BF = jnp.bfloat16
MESH = pl.DeviceIdType.MESH
N_DEV = 8

D_MODEL = 1024
PLE_DIM = 256
D_GMLP = 512
GROUPS = 4
GDIM = 128
BLOCK = 128
D_ATTN = 512
HEAD_DIM = 64
N_Q = 8
Q_PER_KV = 4
N_KV = N_Q // Q_PER_KV
ROWS4 = Q_PER_KV * BLOCK
D_KV = 128
D_FF = 4096
D_IN = 1792
D_MAIN = 2 * D_GMLP + D_ATTN
REL_BUCKETS = 32
EPS = 1e-6
NEG_INF = -1e30
SCALE = HEAD_DIM ** -0.5
GELU_C = math.sqrt(2.0 / math.pi)
GELU_A = 0.044715

ADAM_LR = 0.001
ADAM_B1 = 0.9
ADAM_B2 = 0.999
ADAM_EPS = 1e-08
ADAM_WD = 0.01
ADAM_STEP = 10

V7X_VMEM_LIMIT = 61440 * 1024
TOK_TILE = 256
IO_TOK_TILE = 512


def _call(body, after=None, **kw):
    if after is None:
        return pl.pallas_call(body, **kw)
    n_in = len(kw["in_specs"])

    def ordered(*refs):
        body(*refs[:n_in], *refs[n_in + 1:])

    kw["in_specs"] = list(kw["in_specs"]) + [pl.BlockSpec(memory_space=pl.ANY)]
    fn = pl.pallas_call(ordered, **kw)
    return lambda *operands: fn(*operands, after)


def _params(sem=None):
    if sem is None:
        return pltpu.CompilerParams(vmem_limit_bytes=V7X_VMEM_LIMIT)
    return pltpu.CompilerParams(dimension_semantics=sem, vmem_limit_bytes=V7X_VMEM_LIMIT)


def _nn(a, b):
    return jnp.dot(a, b, preferred_element_type=F32)


def _nt(a, b):
    return lax.dot_general(a, b, (((1,), (1,)), ((), ())), preferred_element_type=F32)


def _tn(a, b):
    return lax.dot_general(a, b, (((0,), (0,)), ((), ())), preferred_element_type=F32)


def _gelu_tanh(x):
    return jnp.tanh(GELU_C * (x + GELU_A * (x * x * x)))


def _gelu(x, t):
    return x * (0.5 * (1.0 + t))


def _gelu_and_grad(x, t):
    cdf = 0.5 * (1.0 + t)
    return x * cdf, cdf + 0.5 * x * (1.0 - t * t) * (GELU_C * (1.0 + 3.0 * GELU_A * (x * x)))


def _rms_scale(x):
    return lax.rsqrt(jnp.mean(x * x, axis=-1, keepdims=True) + EPS)


def _rms_bwd(dxn, x, r):
    return r * dxn - x * ((r * r * r) * jnp.mean(dxn * x, axis=-1, keepdims=True))


def _bucket_table():
    a = np.arange(BLOCK)[:, None]
    j = np.arange(2 * BLOCK)[None, :]
    n = BLOCK + a - j
    valid = (n >= 0) & (n < BLOCK)
    nc = np.maximum(n, 0)
    max_exact = REL_BUCKETS // 2
    nf = np.maximum(nc, 1).astype(np.float32)
    large = max_exact + (
        np.log(nf / np.float32(max_exact)) / np.float32(math.log(BLOCK / max_exact)) * np.float32(REL_BUCKETS - max_exact)
    ).astype(np.int32)
    large = np.minimum(large, REL_BUCKETS - 1)
    bucket = np.where(nc < max_exact, nc, large)
    return np.where(valid, bucket, -1).astype(np.int32)


def _in_proj(x, g1, w_in_t):
    s = x.shape[0]
    tm = min(IO_TOK_TILE, s)

    def body(x_ref, g_ref, w_ref, zuv_ref, qkv_ref, hn_ref):
        xv = x_ref[...]
        hn = ((xv * _rms_scale(xv)) * g_ref[...]).astype(BF)
        hn_ref[...] = hn
        z = _nt(hn, w_ref[...])
        zuv_ref[...] = z[:, : 2 * D_GMLP]
        qkv_ref[...] = z[:, 2 * D_GMLP:].astype(BF)

    return _call(
        body,
        name="in_proj",
        grid=(s // tm,),
        in_specs=[
            pl.BlockSpec((tm, D_MODEL), lambda i: (i, 0)),
            pl.BlockSpec((1, D_MODEL), lambda i: (0, 0)),
            pl.BlockSpec((D_IN, D_MODEL), lambda i: (0, 0)),
        ],
        out_specs=[
            pl.BlockSpec((tm, 2 * D_GMLP), lambda i: (i, 0)),
            pl.BlockSpec((tm, D_ATTN + 2 * D_KV), lambda i: (i, 0)),
            pl.BlockSpec((tm, D_MODEL), lambda i: (i, 0)),
        ],
        out_shape=[
            jax.ShapeDtypeStruct((s, 2 * D_GMLP), F32),
            jax.ShapeDtypeStruct((s, D_ATTN + 2 * D_KV), BF),
            jax.ShapeDtypeStruct((s, D_MODEL), BF),
        ],
        compiler_params=_params(("arbitrary",)),
    )(x, g1, w_in_t)


def _head_rows(h):
    kh, g = divmod(h, Q_PER_KV)
    return kh, slice(g * BLOCK, (g + 1) * BLOCK)


def _build_bias(bias_ref, bucket_ref, table_ref):
    bucket = bucket_ref[...]
    for h in range(N_Q):
        acc = jnp.zeros((BLOCK, 2 * BLOCK), F32)
        for b in range(REL_BUCKETS):
            acc = jnp.where(bucket == b, table_ref[h, b], acc)
        kh, rows = _head_rows(h)
        bias_ref[kh, rows, :] = acc


def _window_masks(i):
    row = lax.broadcasted_iota(jnp.int32, (ROWS4, BLOCK), 0) & (BLOCK - 1)
    col = lax.broadcasted_iota(jnp.int32, (ROWS4, BLOCK), 1)
    return (col > row) & (i > 0), col <= row


def _stack_heads(ref, kh, offset):
    first = offset + kh * Q_PER_KV * HEAD_DIM
    return jnp.concatenate(
        [ref[:, first + g * HEAD_DIM: first + (g + 1) * HEAD_DIM].astype(BF) for g in range(Q_PER_KV)], axis=0)


def _stack_sinks(sink_ref, kh):
    return jnp.concatenate([jnp.full((BLOCK, 1), sink_ref[0, kh * Q_PER_KV + g], F32) for g in range(Q_PER_KV)], axis=0)


def _tril_bf16(w_ref, g):
    row = lax.broadcasted_iota(jnp.int32, (BLOCK, BLOCK), 0)
    col = lax.broadcasted_iota(jnp.int32, (BLOCK, BLOCK), 1)
    return jnp.where(col <= row, w_ref[g], 0.0).astype(BF)


def _attn_probs(q_h, k_prev, k_cur, bias_h, sink, valid_prev, valid_cur):
    l_prev = jnp.where(valid_prev, _nt(q_h, k_prev) * SCALE + bias_h[:, :BLOCK], NEG_INF)
    l_cur = jnp.where(valid_cur, _nt(q_h, k_cur) * SCALE + bias_h[:, BLOCK:], NEG_INF)
    m = jnp.maximum(jnp.maximum(jnp.max(l_prev, axis=-1, keepdims=True), jnp.max(l_cur, axis=-1, keepdims=True)), sink)
    e_prev = jnp.exp(l_prev - m)
    e_cur = jnp.exp(l_cur - m)
    e_sink = jnp.exp(sink - m)
    denom = jnp.sum(e_prev, axis=-1, keepdims=True) + jnp.sum(e_cur, axis=-1, keepdims=True) + e_sink
    return e_prev / denom, e_cur / denom, e_sink / denom


def _mixer_specs(nb):
    cl = lambda i: jnp.minimum(i, nb - 1)
    return [
        pl.BlockSpec((BLOCK, 2 * D_GMLP), lambda i: (cl(i), 0)),
        pl.BlockSpec((BLOCK, D_ATTN), lambda i: (cl(i), 0)),
        pl.BlockSpec((BLOCK, 2 * D_KV), lambda i: (cl(i), D_ATTN // (2 * D_KV))),
        pl.BlockSpec((BLOCK, 2 * D_KV), lambda i: (jnp.maximum(cl(i) - 1, 0), D_ATTN // (2 * D_KV))),
        pl.BlockSpec((1, D_GMLP), lambda i: (0, 0)),
        pl.BlockSpec((GROUPS, BLOCK, BLOCK), lambda i: (0, 0, 0)),
        pl.BlockSpec((GROUPS, BLOCK, 1), lambda i: (0, 0, 0)),
        pl.BlockSpec(memory_space=pltpu.SMEM),
        pl.BlockSpec(memory_space=pltpu.SMEM),
        pl.BlockSpec((BLOCK, 2 * BLOCK), lambda i: (0, 0)),
    ]


def _mixer_fwd(zuv, qkv, gv, w_sp, b_sp, sinks, table, bucket, after=None):
    s = zuv.shape[0]
    nb = s // BLOCK

    def body(zuv_ref, q_ref, kvc_ref, kvp_ref, gv_ref, w_ref, b_ref, sink_ref, table_ref, bucket_ref,
             mix_ref, tanh_ref, prob_ref, psink_ref, bias_ref):
        i = pl.program_id(0)

        @pl.when(i == 0)
        def _():
            _build_bias(bias_ref, bucket_ref, table_ref)

        t = _gelu_tanh(zuv_ref[...])
        tanh_ref[...] = t
        u = _gelu(zuv_ref[:, :D_GMLP], t[:, :D_GMLP])
        vg = _gelu(zuv_ref[:, D_GMLP:], t[:, D_GMLP:])
        for g in range(GROUPS):
            sl = slice(g * GDIM, (g + 1) * GDIM)
            vg_g = vg[:, sl]
            vn = ((vg_g * _rms_scale(vg_g)) * gv_ref[:, sl]).astype(BF)
            sv = _nn(_tril_bf16(w_ref, g), vn) + b_ref[g]
            mix_ref[:, sl] = (u[:, sl] * sv).astype(BF)

        valid_prev, valid_cur = _window_masks(i)
        for kh in range(N_KV):
            ksl = slice(kh * HEAD_DIM, (kh + 1) * HEAD_DIM)
            vsl = slice(D_KV + kh * HEAD_DIM, D_KV + (kh + 1) * HEAD_DIM)
            q4 = _stack_heads(q_ref, kh, 0)
            p_prev, p_cur, p_sink = _attn_probs(
                q4, kvp_ref[:, ksl], kvc_ref[:, ksl], bias_ref[kh], _stack_sinks(sink_ref, kh), valid_prev, valid_cur)
            prob_ref[0, kh, :, :BLOCK] = p_prev
            prob_ref[0, kh, :, BLOCK:] = p_cur
            psink_ref[0, kh] = jnp.broadcast_to(p_sink, (ROWS4, LANES))
            o4 = _nn(p_prev.astype(BF), kvp_ref[:, vsl]) + _nn(p_cur.astype(BF), kvc_ref[:, vsl])
            for g in range(Q_PER_KV):
                first = D_GMLP + (kh * Q_PER_KV + g) * HEAD_DIM
                mix_ref[:, first:first + HEAD_DIM] = o4[g * BLOCK:(g + 1) * BLOCK].astype(BF)

    return _call(
        body,
        name="mixer_fwd",
        after=after,
        grid=(nb,),
        in_specs=_mixer_specs(nb),
        out_specs=[
            pl.BlockSpec((BLOCK, D_MODEL), lambda i: (i, 0)),
            pl.BlockSpec((BLOCK, 2 * D_GMLP), lambda i: (i, 0)),
            pl.BlockSpec((1, N_KV, ROWS4, 2 * BLOCK), lambda i: (i, 0, 0, 0)),
            pl.BlockSpec((1, N_KV, ROWS4, LANES), lambda i: (i, 0, 0, 0)),
        ],
        out_shape=[
            jax.ShapeDtypeStruct((s, D_MODEL), BF),
            jax.ShapeDtypeStruct((s, 2 * D_GMLP), F32),
            jax.ShapeDtypeStruct((nb, N_KV, ROWS4, 2 * BLOCK), F32),
            jax.ShapeDtypeStruct((nb, N_KV, ROWS4, LANES), F32),
        ],
        scratch_shapes=[pltpu.VMEM((N_KV, ROWS4, 2 * BLOCK), F32)],
        compiler_params=_params(("arbitrary",)),
    )(zuv, qkv, qkv, qkv, gv, w_sp, b_sp, sinks, table, bucket)


def _tail(x, mix, p, t, g2, gf, w_out, w_ff1, w_ff2, w_gate, w_proj):
    s = x.shape[0]
    tm = min(TOK_TILE, s)
    n_ff = w_ff1.shape[0]
    fc = D_FF // n_ff
    pc = D_MODEL // N_DEV

    def body(x_ref, mix_ref, p_ref, t_ref, g2_ref, gf_ref, wo_ref, w1_ref, w2_ref, wg_ref, wp_ref,
             small_ref, dh1_ref, dmix_ref, hn2_ref, a_ref, df_ref, dh2_ref, gwo_ref, gwp_ref, gwg_ref, f_ref):
        i = pl.program_id(0)

        @pl.when(i == 0)
        def _():
            small_ref[...] = jnp.zeros_like(small_ref)
            gwo_ref[...] = jnp.zeros_like(gwo_ref)
            gwp_ref[...] = jnp.zeros_like(gwp_ref)
            gwg_ref[...] = jnp.zeros_like(gwg_ref)

        h1 = x_ref[...] + _nn(mix_ref[...], wo_ref[...])
        dh1_ref[...] = h1
        r2 = _rms_scale(h1)
        hn2_ref[...] = ((h1 * r2) * g2_ref[...]).astype(BF)
        h2 = h1
        for c in range(n_ff):
            f = _nn(hn2_ref[...], w1_ref[c])
            f_ref[:, c * fc:(c + 1) * fc] = f
            a = jnp.square(jnp.maximum(f, 0.0)).astype(BF)
            a_ref[:, c * fc:(c + 1) * fc] = a
            h2 = h2 + _nn(a, w2_ref[c * fc:(c + 1) * fc, :])
        h2b = h2.astype(BF)
        gate = jax.nn.sigmoid(_nn(h2b, wg_ref[...]))
        pb = p_ref[...].astype(BF)
        pp = _nn(pb, wp_ref[...])
        h3 = h2 + gate * pp
        rf = _rms_scale(h3)
        gf_v = gf_ref[...]
        err = (h3 * rf) * gf_v - t_ref[...]
        small_ref[2:3, :] += jnp.sum(jnp.sum(err * err, axis=-1, keepdims=True), axis=0, keepdims=True) * (0.5 / D_MODEL)

        dy = err * (1.0 / D_MODEL)
        small_ref[1:2, :] += jnp.sum(dy * (h3 * rf), axis=0, keepdims=True)
        dh3 = _rms_bwd(dy * gf_v, h3, rf)
        gw_proj = _tn(pb, (dh3 * gate).astype(BF))
        for q in range(N_DEV):
            gwp_ref[q] += gw_proj[:, q * pc:(q + 1) * pc]
        dgl = ((dh3 * pp) * (gate * (1.0 - gate))).astype(BF)
        gwg_ref[...] += _tn(h2b, dgl)
        dh2 = dh3 + _nt(dgl, wg_ref[...])
        dmix_ref[...] = dh2
        dh2_ref[...] = dh2.astype(BF)
        dhn2 = jnp.zeros((tm, D_MODEL), F32)
        for c in range(n_ff):
            da = _nt(dh2_ref[...], w2_ref[c * fc:(c + 1) * fc, :])
            df = (da * (2.0 * jnp.maximum(f_ref[:, c * fc:(c + 1) * fc], 0.0))).astype(BF)
            df_ref[:, c * fc:(c + 1) * fc] = df
            dhn2 = dhn2 + _nt(df, w1_ref[c])
        h1 = dh1_ref[...]
        small_ref[0:1, :] += jnp.sum(dhn2 * (h1 * r2), axis=0, keepdims=True)
        dh1 = dmix_ref[...] + _rms_bwd(dhn2 * g2_ref[...], h1, r2)
        dh1_ref[...] = dh1
        dh1b = dh1.astype(BF)
        gwo_ref[...] += _tn(mix_ref[...], dh1b)
        dmix_ref[...] = _nt(dh1b, wo_ref[...])

    tile = lambda cols: pl.BlockSpec((tm, cols), lambda i: (i, 0))
    whole = lambda shape: pl.BlockSpec(shape, lambda i: (0,) * len(shape), pipeline_mode=pl.Buffered(1))
    total = lambda shape: pl.BlockSpec(shape, lambda i: (0,) * len(shape))
    row = pl.BlockSpec((1, D_MODEL), lambda i: (0, 0))
    act = lambda cols, dt: jax.ShapeDtypeStruct((s, cols), dt)
    gw_shapes = [(D_MODEL, D_MODEL), (N_DEV, PLE_DIM, pc), (D_MODEL, D_MODEL)]
    return _call(
        body,
        name="tail",
        grid=(s // tm,),
        in_specs=[tile(D_MODEL), tile(D_MODEL), tile(PLE_DIM), tile(D_MODEL), row, row,
                  whole(w_out.shape), whole(w_ff1.shape), whole(w_ff2.shape), whole(w_gate.shape), whole(w_proj.shape)],
        out_specs=[total((8, D_MODEL)), tile(D_MODEL), tile(D_MODEL), tile(D_MODEL), tile(D_FF), tile(D_FF), tile(D_MODEL),
                   *[total(shape) for shape in gw_shapes]],
        out_shape=[jax.ShapeDtypeStruct((8, D_MODEL), F32),
                   act(D_MODEL, F32), act(D_MODEL, F32), act(D_MODEL, BF), act(D_FF, BF), act(D_FF, BF), act(D_MODEL, BF),
                   *[jax.ShapeDtypeStruct(shape, F32) for shape in gw_shapes]],
        scratch_shapes=[pltpu.VMEM((tm, D_FF), F32)],
        compiler_params=_params(("arbitrary",)),
    )(x, mix, p, t, g2, gf, w_out, w_ff1, w_ff2, w_gate, w_proj)


def _mixer_bwd(zuv, qkv, dmix, saved, gv, w_sp, b_sp, bucket, after=None):
    s = zuv.shape[0]
    nb = s // BLOCK

    def body(zuv_ref, q_ref, kvc_ref, kvp_ref, gv_ref, w_ref, b_ref, bucket_ref, dmix_ref, tanh_ref, prob_ref, psink_ref,
             dzm_ref, dkv_ref, dgv_ref, dw_ref, db_ref, dattn_ref,
             dbias_ref, carry_ref, dsink_acc, db_acc):
        i = pl.program_id(0)

        @pl.when(i == 0)
        def _():
            dbias_ref[...] = jnp.zeros_like(dbias_ref)
            carry_ref[...] = jnp.zeros_like(carry_ref)
            dsink_acc[...] = jnp.zeros_like(dsink_acc)
            dgv_ref[...] = jnp.zeros_like(dgv_ref)
            dw_ref[...] = jnp.zeros_like(dw_ref)
            db_acc[...] = jnp.zeros_like(db_acc)

        @pl.when(i < nb)
        def _():
            u, du_dz = _gelu_and_grad(zuv_ref[:, :D_GMLP], tanh_ref[:, :D_GMLP])
            vg, dvg_dz = _gelu_and_grad(zuv_ref[:, D_GMLP:], tanh_ref[:, D_GMLP:])
            for g in range(GROUPS):
                sl = slice(g * GDIM, (g + 1) * GDIM)
                vg_g = vg[:, sl]
                rg = _rms_scale(vg_g)
                vhat = vg_g * rg
                gain = gv_ref[:, sl]
                vn = (vhat * gain).astype(BF)
                w_g = _tril_bf16(w_ref, g)
                sv = _nn(w_g, vn) + b_ref[g]
                dmix_a = dmix_ref[:, sl]
                dsv = dmix_a * u[:, sl]
                dsvb = dsv.astype(BF)
                db_acc[g] += jnp.sum(dsv, axis=-1, keepdims=True)
                dw_ref[g] += _nt(dsvb, vn)
                dvn = _tn(w_g, dsvb)
                dgv_ref[:, sl] += jnp.sum(dvn * vhat, axis=0, keepdims=True)
                dvg = _rms_bwd(dvn * gain, vg_g, rg)
                dzm_ref[:, sl] = ((dmix_a * sv) * du_dz[:, sl]).astype(BF)
                dzm_ref[:, D_GMLP + g * GDIM: D_GMLP + (g + 1) * GDIM] = (dvg * dvg_dz[:, sl]).astype(BF)

            for kh in range(N_KV):
                ksl = slice(kh * HEAD_DIM, (kh + 1) * HEAD_DIM)
                vsl = slice(D_KV + kh * HEAD_DIM, D_KV + (kh + 1) * HEAD_DIM)
                k_prev, k_cur = kvp_ref[:, ksl], kvc_ref[:, ksl]
                v_prev, v_cur = kvp_ref[:, vsl], kvc_ref[:, vsl]
                q4 = _stack_heads(q_ref, kh, 0)
                p_prev, p_cur, p_sink = prob_ref[0, kh, :, :BLOCK], prob_ref[0, kh, :, BLOCK:], psink_ref[0, kh, :, 0:1]
                do4 = _stack_heads(dmix_ref, kh, D_GMLP)
                dp_prev = _nt(do4, v_prev)
                dp_cur = _nt(do4, v_cur)
                delta = jnp.sum(p_prev * dp_prev, axis=-1, keepdims=True) + jnp.sum(p_cur * dp_cur, axis=-1, keepdims=True)
                ds_prev = p_prev * (dp_prev - delta)
                ds_cur = p_cur * (dp_cur - delta)
                dsink_acc[kh] -= p_sink * delta
                dbias_ref[kh, :, :BLOCK] += ds_prev
                dbias_ref[kh, :, BLOCK:] += ds_cur
                dsb_prev = ds_prev.astype(BF)
                dsb_cur = ds_cur.astype(BF)
                dq4 = (_nn(dsb_prev, k_prev) + _nn(dsb_cur, k_cur)) * SCALE
                for g in range(Q_PER_KV):
                    first = 2 * D_GMLP + (kh * Q_PER_KV + g) * HEAD_DIM
                    dzm_ref[:, first:first + HEAD_DIM] = dq4[g * BLOCK:(g + 1) * BLOCK].astype(BF)
                dkv_ref[:, ksl] = (carry_ref[:, ksl] + _tn(dsb_prev, q4) * SCALE).astype(BF)
                dkv_ref[:, vsl] = (carry_ref[:, vsl] + _tn(p_prev.astype(BF), do4)).astype(BF)
                carry_ref[:, ksl] = _tn(dsb_cur, q4) * SCALE
                carry_ref[:, vsl] = _tn(p_cur.astype(BF), do4)

        @pl.when(i == nb)
        def _():
            dkv_ref[...] = carry_ref[...].astype(BF)
            row = lax.broadcasted_iota(jnp.int32, (BLOCK, BLOCK), 0)
            col = lax.broadcasted_iota(jnp.int32, (BLOCK, BLOCK), 1)
            for g in range(GROUPS):
                dw_ref[g] = jnp.where(col <= row, dw_ref[g], 0.0)
                db_ref[g:g + 1, :] = jnp.sum(jnp.where(col == row, db_acc[g], 0.0), axis=0, keepdims=True)
            bucket = bucket_ref[...]
            for b in range(N_Q, REL_BUCKETS):
                dattn_ref[N_Q, b] = 0.0
            for h in range(N_Q):
                kh, rows = _head_rows(h)
                dattn_ref[N_Q, h] = jnp.sum(dsink_acc[kh, rows, :])
                dbh = dbias_ref[kh, rows, :]
                for b in range(REL_BUCKETS):
                    dattn_ref[h, b] = jnp.sum(jnp.where(bucket == b, dbh, 0.0))

    cl = lambda i: jnp.minimum(i, nb - 1)
    const = lambda shape: pl.BlockSpec(shape, lambda i: (0,) * len(shape))
    return _call(
        body,
        name="mixer_bwd",
        after=after,
        grid=(nb + 1,),
        in_specs=_mixer_specs(nb)[:7] + [
            const((BLOCK, 2 * BLOCK)),
            pl.BlockSpec((BLOCK, D_MODEL), lambda i: (cl(i), 0)),
            pl.BlockSpec((BLOCK, 2 * D_GMLP), lambda i: (cl(i), 0)),
            pl.BlockSpec((1, N_KV, ROWS4, 2 * BLOCK), lambda i: (cl(i), 0, 0, 0)),
            pl.BlockSpec((1, N_KV, ROWS4, LANES), lambda i: (cl(i), 0, 0, 0)),
        ],
        out_specs=[
            pl.BlockSpec((BLOCK, D_MAIN), lambda i: (cl(i), 0)),
            pl.BlockSpec((BLOCK, 2 * D_KV), lambda i: (jnp.maximum(i - 1, 0), 0)),
            const((1, D_GMLP)),
            const((GROUPS, BLOCK, BLOCK)),
            const((GROUPS, BLOCK)),
            pl.BlockSpec(memory_space=pltpu.SMEM),
        ],
        out_shape=[
            jax.ShapeDtypeStruct((s, D_MAIN), BF),
            jax.ShapeDtypeStruct((s, 2 * D_KV), BF),
            jax.ShapeDtypeStruct((1, D_GMLP), F32),
            jax.ShapeDtypeStruct((GROUPS, BLOCK, BLOCK), F32),
            jax.ShapeDtypeStruct((GROUPS, BLOCK), F32),
            jax.ShapeDtypeStruct((N_Q + 1, REL_BUCKETS), F32),
        ],
        scratch_shapes=[
            pltpu.VMEM((N_KV, ROWS4, 2 * BLOCK), F32),
            pltpu.VMEM((BLOCK, 2 * D_KV), F32),
            pltpu.VMEM((N_KV, ROWS4, 1), F32),
            pltpu.VMEM((GROUPS, BLOCK, 1), F32),
        ],
        compiler_params=_params(("arbitrary",)),
    )(zuv, qkv, qkv, qkv, gv, w_sp, b_sp, bucket, dmix, *saved)


def _in_bwd(x, dh1, dzm, dkv, g1, w_in_t, after=None):
    s = x.shape[0]
    tm = min(IO_TOK_TILE, s)

    def body(x_ref, dh1_ref, dzm_ref, dkv_ref, g_ref, w_ref, dx_ref, dg_ref):
        @pl.when(pl.program_id(0) == 0)
        def _():
            dg_ref[...] = jnp.zeros_like(dg_ref)

        dhn = _nn(dzm_ref[...], w_ref[:D_MAIN, :]) + _nn(dkv_ref[...], w_ref[D_MAIN:, :])
        xv = x_ref[...]
        r = _rms_scale(xv)
        dg_ref[...] += jnp.sum(dhn * (xv * r), axis=0, keepdims=True)
        dx_ref[...] = dh1_ref[...] + _rms_bwd(dhn * g_ref[...], xv, r)

    tile = lambda cols: pl.BlockSpec((tm, cols), lambda i: (i, 0))
    row = pl.BlockSpec((1, D_MODEL), lambda i: (0, 0))
    return _call(
        body,
        name="in_bwd",
        after=after,
        grid=(s // tm,),
        in_specs=[tile(D_MODEL), tile(D_MODEL), tile(D_MAIN), tile(2 * D_KV), row, pl.BlockSpec((D_IN, D_MODEL), lambda i: (0, 0))],
        out_specs=[tile(D_MODEL), row],
        out_shape=[jax.ShapeDtypeStruct((s, D_MODEL), F32), jax.ShapeDtypeStruct((1, D_MODEL), F32)],
        compiler_params=_params(("arbitrary",)),
    )(x, dh1, dzm, dkv, g1, w_in_t)


def _wgrad_in(dzm, dkv, hn, after=None):
    s = hn.shape[0]
    tm = 2 * D_KV
    n_main = D_MAIN // tm

    def body(dzm_ref, dkv_ref, hn_ref, o_ref):
        i = pl.program_id(0)

        @pl.when(i < n_main)
        def _():
            o_ref[...] = _tn(dzm_ref[...], hn_ref[...])

        @pl.when(i == n_main)
        def _():
            o_ref[...] = _tn(dkv_ref[...], hn_ref[...])

    return _call(
        body,
        name="wgrad_in",
        after=after,
        grid=(n_main + 1,),
        in_specs=[
            pl.BlockSpec((s, tm), lambda i: (0, jnp.minimum(i, n_main - 1))),
            pl.BlockSpec((s, tm), lambda i: (0, 0)),
            pl.BlockSpec((s, D_MODEL), lambda i: (0, 0)),
        ],
        out_specs=pl.BlockSpec((tm, D_MODEL), lambda i: (i, 0)),
        out_shape=jax.ShapeDtypeStruct((D_IN, D_MODEL), F32),
        compiler_params=_params(("arbitrary",)),
    )(dzm, dkv, hn)


def _wgrad(a, b, tm, tn, name, peer_cols=0, after=None):
    s, m = a.shape
    n = b.shape[1]

    def body(a_ref, b_ref, o_ref, at_ref):
        @pl.when(pl.program_id(1) == 0)
        def _():
            at_ref[...] = a_ref[...].astype(BF).T

        r = _nn(at_ref[...], b_ref[...])
        if peer_cols:
            for q in range(tn // peer_cols):
                o_ref[q] = r[:, q * peer_cols:(q + 1) * peer_cols]
        else:
            o_ref[...] = r

    if peer_cols:
        out_spec = pl.BlockSpec((tn // peer_cols, tm, peer_cols), lambda i, j: (j, i, 0))
        out_shape = jax.ShapeDtypeStruct((n // peer_cols, m, peer_cols), F32)
    else:
        out_spec = pl.BlockSpec((tm, tn), lambda i, j: (i, j))
        out_shape = jax.ShapeDtypeStruct((m, n), F32)
    return _call(
        body,
        name=name,
        after=after,
        grid=(m // tm, n // tn),
        in_specs=[pl.BlockSpec((s, tm), lambda i, j: (0, i)), pl.BlockSpec((s, tn), lambda i, j: (0, j))],
        out_specs=out_spec,
        out_shape=out_shape,
        scratch_shapes=[pltpu.VMEM((tm, s), BF)],
        compiler_params=_params(("arbitrary", "arbitrary")),
    )(a, b)


def _adamw_math(w, g, m, v):
    m_new = ADAM_B1 * m + (1.0 - ADAM_B1) * g
    v_new = ADAM_B2 * v + (1.0 - ADAM_B2) * jnp.square(g)
    m_hat = m_new / (1.0 - ADAM_B1 ** ADAM_STEP)
    v_hat = v_new / (1.0 - ADAM_B2 ** ADAM_STEP)
    delta = -ADAM_LR * (m_hat / (jnp.sqrt(v_hat) + ADAM_EPS) + ADAM_WD * w)
    return delta, m_new, v_new


def _final_adamw(part, recv, w, m, v, name):
    r, c = w.shape
    tr = min(r, 512)

    def body(p_ref, r_ref, w_ref, m_ref, v_ref, g_ref, d_ref, mo_ref, vo_ref):
        g = p_ref[...]
        for j in range(3):
            g = g + r_ref[j].astype(F32)
        g_ref[...] = g
        d_ref[...], mo_ref[...], vo_ref[...] = _adamw_math(w_ref[...], g, m_ref[...], v_ref[...])

    spec = pl.BlockSpec((tr, c), lambda i: (i, 0))
    return _call(
        body,
        name=name,
        grid=(r // tr,),
        in_specs=[spec, pl.BlockSpec((3, tr, c), lambda i: (0, i, 0)), spec, spec, spec],
        out_specs=[spec] * 4,
        out_shape=[jax.ShapeDtypeStruct((r, c), F32)] * 4,
        compiler_params=_params(("arbitrary",)),
    )(part, recv, w, m, v)


def _rs_sum(gs, lands, blocks, name):
    n = len(gs)
    steps = pl.cdiv(gs[0].shape[1], 256)
    assert all(g.shape[1] % steps == 0 for g in gs)

    def body(blk_ref, *refs):
        ins, outs = refs[:5 * n], refs[5 * n:]
        for k in range(n):
            g0_ref, g1_ref, g2_ref, g3_ref, l_ref = ins[5 * k:5 * k + 5]
            outs[2 * k][...] = g0_ref[0] + l_ref[0]
            for j, gj_ref in enumerate((g1_ref, g2_ref, g3_ref)):
                outs[2 * k + 1][j] = (gj_ref[0] + l_ref[j + 1]).astype(BF)

    in_specs, out_specs, out_shape, operands = [], [], [], []
    for g, land in zip(gs, lands):
        _, r, c = g.shape
        tr = r // steps
        in_specs += [pl.BlockSpec((1, tr, c), lambda i, blk, j=j: (blk[j], i, 0)) for j in range(4)]
        in_specs.append(pl.BlockSpec((4, tr, c), lambda i, blk: (0, i, 0)))
        out_specs += [pl.BlockSpec((tr, c), lambda i, blk: (i, 0)), pl.BlockSpec((3, tr, c), lambda i, blk: (0, i, 0))]
        out_shape += [jax.ShapeDtypeStruct((r, c), F32), jax.ShapeDtypeStruct((3, r, c), BF)]
        operands += [g, g, g, g, land]
    outs = _call(
        body,
        name=name,
        grid_spec=pltpu.PrefetchScalarGridSpec(num_scalar_prefetch=1, grid=(steps,), in_specs=in_specs, out_specs=out_specs),
        out_shape=out_shape,
        compiler_params=_params(("arbitrary",)),
    )(blocks, *operands)
    return [(outs[2 * k], outs[2 * k + 1]) for k in range(n)]


def _final_adamw_few(parts, recvs, ws, ms, vs, name):
    n = len(ws)

    def body(*refs):
        ins, outs = refs[:5 * n], refs[5 * n:]
        for k in range(n):
            p_ref, r_ref, w_ref, m_ref, v_ref = (ins[q * n + k] for q in range(5))
            g = p_ref[...]
            for j in range(3):
                g = g + r_ref[j].astype(F32)
            outs[k][...] = g
            outs[n + k][...], outs[2 * n + k][...], outs[3 * n + k][...] = _adamw_math(w_ref[...], g, m_ref[...], v_ref[...])

    outs = _call(
        body,
        name=name,
        out_shape=[jax.ShapeDtypeStruct(w.shape, F32) for w in ws] * 4,
        compiler_params=_params(),
    )(*parts, *recvs, *ws, *ms, *vs)
    return outs[:n], outs[n:2 * n], outs[2 * n:3 * n], outs[3 * n:]


def _adamw_small(own, recv, g1_own, g1_recv, weights, moms, vels, after):
    n_w = len(weights)

    def body(*refs):
        own_refs, recv_refs, g1_own_ref, g1_recv_ref = refs[:5], refs[5:10], refs[10], refs[11]
        w_refs, m_refs, v_refs = refs[12:12 + n_w], refs[12 + n_w:12 + 2 * n_w], refs[12 + 2 * n_w:12 + 3 * n_w]
        outs = refs[12 + 3 * n_w:]
        loss_ref, g_refs, d_refs = outs[0], outs[1:1 + n_w], outs[1 + n_w:1 + 2 * n_w]
        mo_refs, vo_refs = outs[1 + 2 * n_w:1 + 3 * n_w], outs[1 + 3 * n_w:]
        x, y, c = lax.axis_index("x"), lax.axis_index("y"), lax.axis_index("c")

        def in_place_order(values, my_place):
            acc = None
            for place in range(len(values)):
                r = place ^ my_place
                term = values[-1]
                for q in range(len(values) - 2, -1, -1):
                    term = jnp.where(r == q, values[q], term)
                acc = term if acc is None else acc + term
            return acc

        def total(k, *index):
            index = index or (slice(None),) * (len(own_refs[k].shape) - 1)
            across = [own_refs[k][(0, *index)], recv_refs[k][(1, *index)], recv_refs[k][(0, *index)], recv_refs[k][(2, *index)]]
            return in_place_order([v.astype(F32) for v in across], 2 * x + y)

        g1 = in_place_order([g1_own_ref[0]] + [g1_recv_ref[j] for j in range(N_DEV - 1)], 4 * x + 2 * y + c)
        grads = [
            g1, total(1), total(2), total(3),
            total(4, slice(N_Q, None), slice(0, N_Q)), total(4, slice(0, N_Q), slice(None)),
            total(0, slice(0, 1), slice(None)), total(0, slice(1, 2), slice(None))]
        loss_ref[...] = total(0, slice(2, 3), slice(0, 1))
        for k in range(n_w):
            g_refs[k][...] = grads[k]
            d_refs[k][...], mo_refs[k][...], vo_refs[k][...] = _adamw_math(w_refs[k][...], grads[k], m_refs[k][...], v_refs[k][...])

    shapes = [jax.ShapeDtypeStruct(w.shape, F32) for w in weights]
    outs = _call(
        body,
        name="adamw_small",
        after=after,
        in_specs=[pl.BlockSpec(memory_space=pltpu.VMEM)] * (12 + 3 * n_w),
        out_shape=[jax.ShapeDtypeStruct((1, 1), F32)] + shapes * 4,
        compiler_params=_params(),
    )(*own, *recv, g1_own, g1_recv, *weights, *moms, *vels)
    return outs[0], outs[1:1 + n_w], outs[1 + n_w:1 + 2 * n_w], outs[1 + 2 * n_w:1 + 3 * n_w], outs[1 + 3 * n_w:]


def _place():
    x, y, c = lax.axis_index("x"), lax.axis_index("y"), lax.axis_index("c")
    return x, y, c, [(1 - x, y), (x, 1 - y), (1 - x, 1 - y)]


def _dev_index(px, py, pc):
    return 4 * px + 2 * py + pc


HBM_SPEC = pl.BlockSpec(memory_space=pltpu.HBM)
SEM_SPEC = pl.BlockSpec(memory_space=pltpu.SEMAPHORE)
ANY_SPEC = pl.BlockSpec(memory_space=pl.ANY)
DATAFLOW = pltpu.SideEffectType.DATAFLOW_SIDE_EFFECTING


def _hbm(a):
    return pltpu.with_memory_space_constraint(a, pltpu.HBM)


def _prep_weights(shards):
    k_n = len(shards)

    def body(*refs):
        ins, outs, stage, sems = refs[:k_n], refs[k_n:2 * k_n], refs[2 * k_n:3 * k_n], refs[3 * k_n]
        x, y, c, _ = _place()
        copies = []
        for k in range(k_n):
            stage[k][...] = ins[k][...].astype(BF)
            copies.append(pltpu.make_async_copy(stage[k], outs[k].at[_dev_index(x, y, c)], sems.at[k]))
            copies[k].start()
        for cp in copies:
            cp.wait()

    return _call(
        body,
        name="prep_weights",
        in_specs=[pl.BlockSpec(memory_space=pltpu.VMEM)] * k_n,
        out_specs=[ANY_SPEC] * k_n,
        out_shape=[jax.ShapeDtypeStruct((N_DEV,) + sh.shape, BF) for sh in shards],
        scratch_shapes=[pltpu.VMEM(sh.shape, BF) for sh in shards] + [pltpu.SemaphoreType.DMA((k_n,))],
        compiler_params=_params(),
    )(*shards)


def _ag_piece(land_k, block, half, peer, send_sem, recv_sem):
    ref = land_k.at[_dev_index(*block)]
    if half is not None:
        rows = land_k.shape[1] // 2
        ref = ref.at[pl.ds(half * rows, rows)]
    return pltpu.make_async_remote_copy(
        src_ref=ref, dst_ref=ref, send_sem=send_sem, recv_sem=recv_sem, device_id=peer, device_id_type=MESH)


def _ag_plan():
    x, y, c, _ = _place()
    me, sib = (x, y, c), (x, y, 1 - c)
    xn, yn, diag = (1 - x, y, c), (x, 1 - y, c), (1 - x, 1 - y, c)
    return dict(
        relay_halves=[(me, 0, xn), (me, 1, yn)],
        others=[(me, None, sib), (me, 1, xn), (me, 0, yn)],
        relays=[(xn, 0, yn), (yn, 1, xn)],
        near=[(xn, None, sib), (yn, None, sib)],
        far=[(diag, None, sib)],
    )


def _ag_stage(land, stage, send_sems, recv_sems, act):
    copies = _ag_plan()[stage]
    n = len(copies)
    for k in range(len(land)):
        for j, (block, half, peer) in enumerate(copies):
            cp = _ag_piece(land[k], block, half, peer, send_sems.at[n * k + j], recv_sems.at[n * k + j])
            if act == "start":
                cp.start()
            else:
                cp.wait_send()
                cp.wait_recv()


def _sem_shapes(*counts):
    return [pltpu.SemaphoreType.DMA((n,)) for n in counts for _ in range(2)]


def _ag_start(first, rest):
    lands = list(first) + list(rest)
    k_n, k_first = len(lands), len(first)
    k_rest = k_n - k_first

    def body(*refs):
        land = refs[:k_n]
        sems = refs[k_n:k_n + 6]
        token = refs[-1]
        x, y, c, chips = _place()
        targets = [(x, y, 1 - c)] + [(*chip, c) for chip in chips]
        for k in range(k_first):
            for j, to in enumerate(targets):
                _ag_piece(land[k], (x, y, c), None, to, sems[0].at[4 * k + j], sems[1].at[4 * k + j]).start()
        _ag_stage(land[k_first:], "relay_halves", sems[2], sems[3], "start")
        _ag_stage(land[k_first:], "others", sems[4], sems[5], "start")
        token[...] = jnp.zeros_like(token)

    outs = pl.pallas_call(
        body,
        name="ag_start",
        in_specs=[HBM_SPEC] * k_n,
        out_specs=(*[SEM_SPEC] * 6, *[HBM_SPEC] * k_n, pl.BlockSpec(memory_space=pltpu.VMEM)),
        out_shape=(*_sem_shapes(4 * k_first, 2 * k_rest, 3 * k_rest),
                   *[pltpu.HBM(a.shape, a.dtype) for a in lands], jax.ShapeDtypeStruct((8, LANES), F32)),
        input_output_aliases={k: 6 + k for k in range(k_n)},
        compiler_params=pltpu.CompilerParams(has_side_effects=DATAFLOW),
    )(*[_hbm(a) for a in lands])
    flying = list(outs[6:6 + k_n])
    return (outs[0], outs[1], flying[:k_first]), (outs[2:6], flying[k_first:]), outs[-1]


def _ag_split_call(lands, waits, starts, after, name):
    k_n = len(lands)
    plan_sizes = dict(relay_halves=2, others=3, relays=2, near=2, far=1)
    n_in, n_out = 2 * len(waits), 2 * len(starts)

    def body(*refs):
        land = refs[:k_n]
        in_sems = refs[k_n:k_n + n_in]
        out_sems, token = refs[len(refs) - 1 - n_out:len(refs) - 1], refs[-1]
        for w, (stage, _, _) in enumerate(waits):
            _ag_stage(land, stage, in_sems[2 * w], in_sems[2 * w + 1], "wait")
            if w < len(starts):
                _ag_stage(land, starts[w], out_sems[2 * w], out_sems[2 * w + 1], "start")
        token[...] = jnp.zeros_like(token)

    outs = pl.pallas_call(
        body,
        name=name,
        in_specs=[HBM_SPEC] * k_n + [SEM_SPEC] * n_in + [ANY_SPEC],
        out_specs=(*[HBM_SPEC] * k_n, *[SEM_SPEC] * n_out, pl.BlockSpec(memory_space=pltpu.VMEM)),
        out_shape=(*[pltpu.HBM(a.shape, a.dtype) for a in lands], *_sem_shapes(*[plan_sizes[s] * k_n for s in starts]),
                   jax.ShapeDtypeStruct((8, LANES), F32)),
        input_output_aliases={k: k for k in range(k_n)},
        compiler_params=pltpu.CompilerParams(has_side_effects=DATAFLOW),
    )(*lands, *[s for _, a, b in waits for s in (a, b)], after)
    return list(outs[:k_n]), list(outs[k_n:k_n + n_out]), outs[-1]


def _ag_mid(lands, send_sems, recv_sems, after, name):
    k_n = len(lands)

    def body(*refs):
        land = refs[:k_n]
        send1, recv1 = refs[k_n], refs[k_n + 1]
        fwd_send, fwd_recv = refs[-2], refs[-1]
        x, y, c, chips = _place()
        sources = [(x, y, 1 - c)] + [(*chip, c) for chip in chips]
        for k in range(k_n):
            mine = land[k].at[_dev_index(x, y, c)]
            for j, frm in enumerate(sources):
                got = land[k].at[_dev_index(*frm)]
                cp = pltpu.make_async_remote_copy(
                    src_ref=mine, dst_ref=got, send_sem=send1.at[4 * k + j], recv_sem=recv1.at[4 * k + j], device_id=frm, device_id_type=MESH)
                cp.wait_send()
                cp.wait_recv()
                if j >= 1:
                    pltpu.make_async_remote_copy(
                        src_ref=got, dst_ref=got, send_sem=fwd_send.at[3 * k + j - 1], recv_sem=fwd_recv.at[3 * k + j - 1],
                        device_id=(x, y, 1 - c), device_id_type=MESH).start()

    outs = pl.pallas_call(
        body,
        name=name,
        in_specs=[HBM_SPEC] * k_n + [SEM_SPEC, SEM_SPEC, ANY_SPEC],
        out_specs=(*[HBM_SPEC] * k_n, SEM_SPEC, SEM_SPEC),
        out_shape=(*[pltpu.HBM(a.shape, a.dtype) for a in lands], pltpu.SemaphoreType.DMA((3 * k_n,)), pltpu.SemaphoreType.DMA((3 * k_n,))),
        input_output_aliases={k: k for k in range(k_n)},
        compiler_params=pltpu.CompilerParams(has_side_effects=DATAFLOW),
    )(*lands, send_sems, recv_sems, after)
    return list(outs[:k_n]), outs[-2], outs[-1]


def _ag_end(lands, fwd_send, fwd_recv, name):
    k_n = len(lands)

    def body(*refs):
        land = refs[:k_n]
        fsend, frecv = refs[k_n], refs[k_n + 1]
        x, y, c, chips = _place()
        for k in range(k_n):
            for j, chip in enumerate(chips):
                cp = pltpu.make_async_remote_copy(
                    src_ref=land[k].at[_dev_index(*chip, c)], dst_ref=land[k].at[_dev_index(*chip, 1 - c)],
                    send_sem=fsend.at[3 * k + j], recv_sem=frecv.at[3 * k + j], device_id=(x, y, 1 - c), device_id_type=MESH)
                cp.wait_send()
                cp.wait_recv()

    outs = pl.pallas_call(
        body,
        name=name,
        in_specs=[HBM_SPEC] * k_n + [SEM_SPEC, SEM_SPEC],
        out_specs=tuple([HBM_SPEC] * k_n),
        out_shape=tuple(pltpu.HBM(a.shape, a.dtype) for a in lands),
        input_output_aliases={k: k for k in range(k_n)},
        compiler_params=pltpu.CompilerParams(has_side_effects=DATAFLOW),
    )(*lands, fwd_send, fwd_recv)
    return list(outs)


def _chips4():
    x, y, c, others = _place()
    return x, y, c, [(x, y)] + others


def _route_sibling(j):
    x, y, c, chips = _chips4()
    return _dev_index(*chips[j], 1 - c), j, (x, y, 1 - c)


def _route_chips(j):
    x, y, c, chips = _chips4()
    return j, j, (*chips[j + 1], c)


def _route_sibling_whole(j):
    x, y, c, _ = _chips4()
    return 0, 0, (x, y, 1 - c)


def _route_chips_whole(j):
    x, y, c, chips = _chips4()
    return 0, j, (*chips[j + 1], c)


def _route_everyone(j):
    x, y, c, _ = _chips4()
    flip = [(j + 1) >> 2 & 1, (j + 1) >> 1 & 1, (j + 1) & 1]
    return 0, j, tuple(1 - v if f else v for v, f in zip((x, y, c), flip))


def _xchg_copies(routes, src, dst, send_sems, recv_sems):
    copies, sem = [], 0
    for k, (route, n) in enumerate(routes):
        for j in range(n):
            si, di, peer = route(j)
            copies.append(pltpu.make_async_remote_copy(
                src_ref=src[k].at[si], dst_ref=dst[k].at[di], send_sem=send_sems.at[sem], recv_sem=recv_sems.at[sem],
                device_id=peer, device_id_type=MESH))
            sem += 1
    return copies


def _xchg_start(srcs, slot_shapes, routes, name, after=None):
    k_n = len(srcs)
    n_in = 2 * k_n + (after is not None)
    n_sem = sum(n for _, n in routes)
    dsts = [lax.empty((n,) + tuple(sh), a.dtype) for sh, a, (_, n) in zip(slot_shapes, srcs, routes)]

    def body(*refs):
        src, dst = refs[:k_n], refs[k_n:2 * k_n]
        send_sems, recv_sems, token = refs[n_in], refs[n_in + 1], refs[-1]
        for cp in _xchg_copies(routes, src, dst, send_sems, recv_sems):
            cp.start()
        token[...] = jnp.zeros_like(token)

    arrays = list(srcs) + dsts
    outs = pl.pallas_call(
        body,
        name=name,
        in_specs=[HBM_SPEC] * (2 * k_n) + [ANY_SPEC] * (n_in - 2 * k_n),
        out_specs=(SEM_SPEC, SEM_SPEC, *[HBM_SPEC] * (2 * k_n), pl.BlockSpec(memory_space=pltpu.VMEM)),
        out_shape=(pltpu.SemaphoreType.DMA((n_sem,)), pltpu.SemaphoreType.DMA((n_sem,)),
                   *[pltpu.HBM(a.shape, a.dtype) for a in arrays], jax.ShapeDtypeStruct((8, LANES), F32)),
        input_output_aliases={i: 2 + i for i in range(2 * k_n)},
        compiler_params=pltpu.CompilerParams(has_side_effects=DATAFLOW),
    )(*[_hbm(a) for a in arrays], *([] if after is None else [after]))
    return outs[0], outs[1], list(outs[2:2 + k_n]), list(outs[2 + k_n:2 + 2 * k_n]), outs[-1]


def _xchg_wait(send_sems, recv_sems, srcs, dsts, routes, after, name):
    k_n = len(srcs)

    def body(*refs):
        src, dst = refs[:k_n], refs[k_n:2 * k_n]
        for cp in _xchg_copies(routes, src, dst, refs[2 * k_n], refs[2 * k_n + 1]):
            cp.wait_send()
            cp.wait_recv()

    arrays = list(srcs) + list(dsts)
    outs = pl.pallas_call(
        body,
        name=name,
        in_specs=[HBM_SPEC] * (2 * k_n) + [SEM_SPEC, SEM_SPEC, ANY_SPEC],
        out_specs=tuple([HBM_SPEC] * (2 * k_n)),
        out_shape=tuple(pltpu.HBM(a.shape, a.dtype) for a in arrays),
        input_output_aliases={i: i for i in range(2 * k_n)},
        compiler_params=pltpu.CompilerParams(has_side_effects=DATAFLOW),
    )(*arrays, send_sems, recv_sems, after)
    return list(outs[:k_n]), list(outs[k_n:])


SMALL = ("norm1_gain", "gmlp_v_gain", "w_spatial", "b_spatial", "attn_sinks", "rel_bias_table", "norm2_gain", "final_gain")
LANES = 128


def _swap_start(grads, smalls, tag, after=None):
    srcs = list(grads) + [a[None] for a in smalls]
    shapes = [g.shape[1:] for g in grads] + [a.shape for a in smalls]
    routes = [(_route_sibling, 4)] * len(grads) + [(_route_sibling_whole, 1)] * len(smalls)
    return _xchg_start(srcs, shapes, routes, f"rs_{tag}_swap_start", after), routes, len(grads)


def _swap_sums(swap, after, tag, small_dtypes=None):
    (send1, recv1, src1, land1, _), routes, n_rs = swap
    x, y, c, chips = _chips4()
    blocks = jnp.stack([_dev_index(*chip, c) for chip in chips]).astype(jnp.int32)
    src1, land1 = _xchg_wait(send1, recv1, src1, land1, routes, after, f"rs_{tag}_swap_wait")
    sums = _rs_sum(src1[:n_rs], land1[:n_rs], blocks, f"rs_{tag}_sum")
    if len(src1) == n_rs:
        return sums, []
    return sums, _pair_sum(src1[n_rs:], land1[n_rs:], small_dtypes or [F32] * (len(src1) - n_rs), f"rs_{tag}_sum_small")


def _pair_sum(mine, theirs, dtypes, name):
    def body(*refs):
        n = len(refs) // 3
        for k in range(n):
            refs[2 * n + k][...] = (refs[k][...] + refs[n + k][...]).astype(dtypes[k])

    return _call(
        body,
        name=name,
        out_shape=[jax.ShapeDtypeStruct(a.shape, dt) for a, dt in zip(mine, dtypes)],
        compiler_params=_params(),
    )(*mine, *theirs)


def _chips_start(sums, small_sums, tag, everyone=()):
    sends = [ps[1] for ps in sums] + list(small_sums) + [a[None] for a in everyone]
    routes = [(_route_chips, 3)] * len(sums) + [(_route_chips_whole, 3)] * len(small_sums) + [(_route_everyone, 7)] * len(everyone)
    return _xchg_start(sends, [a.shape[1:] for a in sends], routes, f"rs_{tag}_chips_start"), routes


def _chips_wait(chips, after, tag):
    (send2, recv2, src2, land2, _), routes = chips
    return _xchg_wait(send2, recv2, src2, land2, routes, after, f"rs_{tag}_chips_wait")


def kernel(x, p, norm1_gain, w_in, gmlp_v_gain, w_spatial, b_spatial, attn_sinks, rel_bias_table, w_out, norm2_gain, w_ff1, w_ff2, w_ple_proj, w_ple_gate, final_gain, loss_target, m_norm1_gain, m_w_in, m_gmlp_v_gain, m_w_spatial, m_b_spatial, m_attn_sinks, m_rel_bias_table, m_w_out, m_norm2_gain, m_w_ff1, m_w_ff2, m_w_ple_proj, m_w_ple_gate, m_final_gain, v_norm1_gain, v_w_in, v_gmlp_v_gain, v_w_spatial, v_b_spatial, v_attn_sinks, v_rel_bias_table, v_w_out, v_norm2_gain, v_w_ff1, v_w_ff2, v_w_ple_proj, v_w_ple_gate, v_final_gain):
    args = dict(locals())
    s = x.shape[1]
    big = ("w_in", "w_out", "w_ff1", "w_ff2", "w_ple_proj", "w_ple_gate")

    x2, p2, t2 = x.reshape(s, D_MODEL), p.reshape(s, PLE_DIM), loss_target.reshape(s, D_MODEL)
    g1, gv, w_sp, b_sp, sinks, table, g2, gf = (args[n] for n in SMALL)
    bucket = jnp.asarray(_bucket_table())
    b_col = b_sp.reshape(GROUPS, BLOCK, 1)

    def shard(name):
        return args[name][0].T if name.endswith("w_in") else args[name][0]

    lands = _prep_weights([shard(n) for n in big])
    (send_in, recv_in, fly_in), (rest_sems, fly_rest), token = _ag_start(lands[:1], lands[1:])
    mid_in, fwd_send_in, fwd_recv_in = _ag_mid(fly_in, send_in, recv_in, token, "ag_mid_w_in")
    g_in = _ag_end(mid_in, fwd_send_in, fwd_recv_in, "ag_end_w_in")[0]
    full_in = g_in.reshape(D_IN, D_MODEL)

    zuv, qkv, hn1 = _in_proj(x2, g1, full_in)
    fly_rest, relay_sems, relayed = _ag_split_call(
        fly_rest, [("relay_halves", *rest_sems[:2])], ["relays"], zuv, "ag_relay")
    mix, *saved = _mixer_fwd(zuv, qkv, gv, w_sp[0], b_col, sinks, table.T, bucket, after=relayed)
    fly_rest, fwd_sems, _ = _ag_split_call(
        fly_rest, [("others", *rest_sems[2:]), ("relays", *relay_sems)], ["near", "far"], mix, "ag_mid_rest")
    g_out, g_ff1, g_ff2, g_proj, g_gate = _ag_split_call(
        fly_rest, [("near", *fwd_sems[:2]), ("far", *fwd_sems[2:])], [], mix, "ag_end_rest")[0]
    full_out, full_ff2, full_gate = g_out.reshape(D_MODEL, D_MODEL), g_ff2.reshape(D_FF, D_MODEL), g_gate.reshape(D_MODEL, D_MODEL)
    full_proj = g_proj.transpose(1, 0, 2).reshape(PLE_DIM, D_MODEL)

    tail_small, dh1, dmix, hn2, a, df, dh2, gw_out, gw_proj, gw_gate = _tail(
        x2, mix, p2, t2, g2, gf.reshape(1, D_MODEL), full_out, g_ff1, full_ff2, full_gate, full_proj)
    gw_out, gw_gate = (g.reshape(N_DEV, D_MODEL // N_DEV, D_MODEL) for g in (gw_out, gw_gate))

    few_names = ("w_out", "w_ple_proj", "w_ple_gate")
    ff1_names = ("w_ff1",) + few_names
    gw_ff1 = _wgrad(hn2, df, D_MODEL, 2 * D_FF // N_DEV, "wgrad_ff1", peer_cols=D_FF // N_DEV)
    ff1_swap = _swap_start([gw_ff1, gw_out, gw_proj, gw_gate], [tail_small], "ff1")
    gw_ff2 = _wgrad(a, dh2, 1024, D_MODEL, "wgrad_ff2", after=ff1_swap[0][4]).reshape(N_DEV, D_FF // N_DEV, D_MODEL)
    ff1_sums, ff1_small = _swap_sums(ff1_swap, gw_ff2, "ff1")
    ff1_chips = _chips_start(ff1_sums, ff1_small, "ff1")
    ff2_swap = _swap_start([gw_ff2], [], "ff2", after=ff1_chips[0][4])
    dzm, dkv, d_gv, d_wsp, d_bsp, d_attn = _mixer_bwd(
        zuv, qkv, dmix, saved, gv, w_sp[0], b_col, bucket, after=ff2_swap[0][4])
    ff2_sums, _ = _swap_sums(ff2_swap, dzm, "ff2")
    ff2_chips = _chips_start(ff2_sums, [], "ff2")
    gw_in = _wgrad_in(dzm, dkv, hn1, after=ff2_chips[0][4]).reshape(N_DEV, D_IN // N_DEV, D_MODEL)
    in_swap = _swap_start([gw_in], [d_gv, d_wsp, d_bsp, d_attn], "in")
    dx, d_g1 = _in_bwd(x2, dh1, dzm, dkv, g1, full_in, after=in_swap[0][4])
    grad_x = dx.reshape(x.shape)
    in_sums, in_small = _swap_sums(in_swap, dx, "in", small_dtypes=[F32, BF, F32, F32])
    in_chips = _chips_start(in_sums, in_small, "in", everyone=[d_g1])

    grads, deltas, new_m, new_v = {}, {}, {}, {}
    after, small_own, small_recv = in_chips[0][4], [], []
    for chips, sums, names, tag in ((ff1_chips, ff1_sums, ff1_names, "ff1"), (ff2_chips, ff2_sums, ("w_ff2",), "ff2"),
                                    (in_chips, in_sums, ("w_in",), "in")):
        sent, recvs = _chips_wait(chips, after, tag)
        small_own += sent[len(names):]
        small_recv += recvs[len(names):]
        tiled = [n for n in names if n not in few_names]
        done = [_final_adamw(part, recv, shard(n), shard("m_" + n), shard("v_" + n), "adamw_" + n)
                for n, (part, _), recv in zip(tiled, sums, recvs)]
        few = names[len(tiled):]
        if few:
            done += zip(*_final_adamw_few(
                [ps[0] for ps in sums[len(tiled):]], recvs[len(tiled):len(names)], [shard(n) for n in few],
                [shard("m_" + n) for n in few], [shard("v_" + n) for n in few], "adamw_few"))
        for n, results in zip(names, done):
            for dst, arr in zip((grads, deltas, new_m, new_v), results):
                dst[n] = (arr.T if n == "w_in" else arr)[None]
            after = results[1]

    views = {"w_spatial": (GROUPS, BLOCK, BLOCK), "b_spatial": (GROUPS, BLOCK), "final_gain": (1, D_MODEL)}
    def view(name, base):
        return args[name].T if base == "rel_bias_table" else args[name].reshape(views.get(base, args[name].shape))

    small_in = [[view(pre + n, n) for n in SMALL] for pre in ("", "m_", "v_")]
    loss, *small_out = _adamw_small(small_own[:-1], small_recv[:-1], small_own[-1], small_recv[-1], *small_in, after=after)
    for dst, arrays in zip((grads, deltas, new_m, new_v), small_out):
        for n, arr in zip(SMALL, arrays):
            dst[n] = arr.T if n == "rel_bias_table" else arr.reshape(args[n].shape)
    loss = loss[0, 0]

    order = ("norm1_gain", "w_in", "gmlp_v_gain", "w_spatial", "b_spatial", "attn_sinks", "rel_bias_table", "w_out",
             "norm2_gain", "w_ff1", "w_ff2", "w_ple_proj", "w_ple_gate", "final_gain")
    return (loss, grad_x, *[grads[n] for n in order], *[deltas[n] for n in order],
            *[new_m[n] for n in order], *[new_v[n] for n in order])
```

```python
import math

import numpy as np
import jax
import jax.numpy as jnp
from jax import lax
from jax.experimental import pallas as pl
from jax.experimental.pallas import tpu as pltpu

F32 = jnp.float32
BF = jnp.bfloat16
MESH = pl.DeviceIdType.MESH
N_DEV = 8

D_MODEL = 1024
PLE_DIM = 256
D_GMLP = 512
GROUPS = 4
GDIM = 128
BLOCK = 128
D_ATTN = 512
HEAD_DIM = 64
N_Q = 8
Q_PER_KV = 4
N_KV = N_Q // Q_PER_KV
ROWS4 = Q_PER_KV * BLOCK
D_KV = 128
D_FF = 4096
D_IN = 1792
D_MAIN = 2 * D_GMLP + D_ATTN
REL_BUCKETS = 32
EPS = 1e-6
NEG_INF = -1e30
SCALE = HEAD_DIM ** -0.5
GELU_C = math.sqrt(2.0 / math.pi)
GELU_A = 0.044715

ADAM_LR = 0.001
ADAM_B1 = 0.9
ADAM_B2 = 0.999
ADAM_EPS = 1e-08
ADAM_WD = 0.01
ADAM_STEP = 10

V7X_VMEM_LIMIT = 61440 * 1024
TOK_TILE = 256
IO_TOK_TILE = 512


def _call(body, after=None, **kw):
    if after is None:
        return pl.pallas_call(body, **kw)
    n_in = len(kw["in_specs"])

    def ordered(*refs):
        body(*refs[:n_in], *refs[n_in + 1:])

    kw["in_specs"] = list(kw["in_specs"]) + [pl.BlockSpec(memory_space=pl.ANY)]
    fn = pl.pallas_call(ordered, **kw)
    return lambda *operands: fn(*operands, after)


def _params(sem=None):
    if sem is None:
        return pltpu.CompilerParams(vmem_limit_bytes=V7X_VMEM_LIMIT)
    return pltpu.CompilerParams(dimension_semantics=sem, vmem_limit_bytes=V7X_VMEM_LIMIT)


def _nn(a, b):
    return jnp.dot(a, b, preferred_element_type=F32)


def _nt(a, b):
    return lax.dot_general(a, b, (((1,), (1,)), ((), ())), preferred_element_type=F32)


def _tn(a, b):
    return lax.dot_general(a, b, (((0,), (0,)), ((), ())), preferred_element_type=F32)


def _gelu_tanh(x):
    return jnp.tanh(GELU_C * (x + GELU_A * (x * x * x)))


def _gelu(x, t):
    return x * (0.5 * (1.0 + t))


def _gelu_and_grad(x, t):
    cdf = 0.5 * (1.0 + t)
    return x * cdf, cdf + 0.5 * x * (1.0 - t * t) * (GELU_C * (1.0 + 3.0 * GELU_A * (x * x)))


def _rms_scale(x):
    return lax.rsqrt(jnp.mean(x * x, axis=-1, keepdims=True) + EPS)


def _rms_bwd(dxn, x, r):
    return r * dxn - x * ((r * r * r) * jnp.mean(dxn * x, axis=-1, keepdims=True))


def _bucket_table():
    a = np.arange(BLOCK)[:, None]
    j = np.arange(2 * BLOCK)[None, :]
    n = BLOCK + a - j
    valid = (n >= 0) & (n < BLOCK)
    nc = np.maximum(n, 0)
    max_exact = REL_BUCKETS // 2
    nf = np.maximum(nc, 1).astype(np.float32)
    large = max_exact + (
        np.log(nf / np.float32(max_exact)) / np.float32(math.log(BLOCK / max_exact)) * np.float32(REL_BUCKETS - max_exact)
    ).astype(np.int32)
    large = np.minimum(large, REL_BUCKETS - 1)
    bucket = np.where(nc < max_exact, nc, large)
    return np.where(valid, bucket, -1).astype(np.int32)


def _in_proj(x, g1, w_in_t):
    s = x.shape[0]
    tm = min(IO_TOK_TILE, s)

    def body(x_ref, g_ref, w_ref, zuv_ref, qkv_ref, hn_ref):
        xv = x_ref[...]
        hn = ((xv * _rms_scale(xv)) * g_ref[...]).astype(BF)
        hn_ref[...] = hn
        z = _nt(hn, w_ref[...])
        zuv_ref[...] = z[:, : 2 * D_GMLP]
        qkv_ref[...] = z[:, 2 * D_GMLP:].astype(BF)

    return _call(
        body,
        name="in_proj",
        grid=(s // tm,),
        in_specs=[
            pl.BlockSpec((tm, D_MODEL), lambda i: (i, 0)),
            pl.BlockSpec((1, D_MODEL), lambda i: (0, 0)),
            pl.BlockSpec((D_IN, D_MODEL), lambda i: (0, 0)),
        ],
        out_specs=[
            pl.BlockSpec((tm, 2 * D_GMLP), lambda i: (i, 0)),
            pl.BlockSpec((tm, D_ATTN + 2 * D_KV), lambda i: (i, 0)),
            pl.BlockSpec((tm, D_MODEL), lambda i: (i, 0)),
        ],
        out_shape=[
            jax.ShapeDtypeStruct((s, 2 * D_GMLP), F32),
            jax.ShapeDtypeStruct((s, D_ATTN + 2 * D_KV), BF),
            jax.ShapeDtypeStruct((s, D_MODEL), BF),
        ],
        compiler_params=_params(("arbitrary",)),
    )(x, g1, w_in_t)


def _head_rows(h):
    kh, g = divmod(h, Q_PER_KV)
    return kh, slice(g * BLOCK, (g + 1) * BLOCK)


def _build_bias(bias_ref, bucket_ref, table_ref):
    bucket = bucket_ref[...]
    for h in range(N_Q):
        acc = jnp.zeros((BLOCK, 2 * BLOCK), F32)
        for b in range(REL_BUCKETS):
            acc = jnp.where(bucket == b, table_ref[h, b], acc)
        kh, rows = _head_rows(h)
        bias_ref[kh, rows, :] = acc


def _window_masks(i):
    row = lax.broadcasted_iota(jnp.int32, (ROWS4, BLOCK), 0) & (BLOCK - 1)
    col = lax.broadcasted_iota(jnp.int32, (ROWS4, BLOCK), 1)
    return (col > row) & (i > 0), col <= row


def _stack_heads(ref, kh, offset):
    first = offset + kh * Q_PER_KV * HEAD_DIM
    return jnp.concatenate(
        [ref[:, first + g * HEAD_DIM: first + (g + 1) * HEAD_DIM].astype(BF) for g in range(Q_PER_KV)], axis=0)


def _stack_sinks(sink_ref, kh):
    return jnp.concatenate([jnp.full((BLOCK, 1), sink_ref[0, kh * Q_PER_KV + g], F32) for g in range(Q_PER_KV)], axis=0)


def _tril_bf16(w_ref, g):
    row = lax.broadcasted_iota(jnp.int32, (BLOCK, BLOCK), 0)
    col = lax.broadcasted_iota(jnp.int32, (BLOCK, BLOCK), 1)
    return jnp.where(col <= row, w_ref[g], 0.0).astype(BF)


def _attn_probs(q_h, k_prev, k_cur, bias_h, sink, valid_prev, valid_cur):
    l_prev = jnp.where(valid_prev, _nt(q_h, k_prev) * SCALE + bias_h[:, :BLOCK], NEG_INF)
    l_cur = jnp.where(valid_cur, _nt(q_h, k_cur) * SCALE + bias_h[:, BLOCK:], NEG_INF)
    m = jnp.maximum(jnp.maximum(jnp.max(l_prev, axis=-1, keepdims=True), jnp.max(l_cur, axis=-1, keepdims=True)), sink)
    e_prev = jnp.exp(l_prev - m)
    e_cur = jnp.exp(l_cur - m)
    e_sink = jnp.exp(sink - m)
    denom = jnp.sum(e_prev, axis=-1, keepdims=True) + jnp.sum(e_cur, axis=-1, keepdims=True) + e_sink
    return e_prev / denom, e_cur / denom, e_sink / denom


def _mixer_specs(nb):
    cl = lambda i: jnp.minimum(i, nb - 1)
    return [
        pl.BlockSpec((BLOCK, 2 * D_GMLP), lambda i: (cl(i), 0)),
        pl.BlockSpec((BLOCK, D_ATTN), lambda i: (cl(i), 0)),
        pl.BlockSpec((BLOCK, 2 * D_KV), lambda i: (cl(i), D_ATTN // (2 * D_KV))),
        pl.BlockSpec((BLOCK, 2 * D_KV), lambda i: (jnp.maximum(cl(i) - 1, 0), D_ATTN // (2 * D_KV))),
        pl.BlockSpec((1, D_GMLP), lambda i: (0, 0)),
        pl.BlockSpec((GROUPS, BLOCK, BLOCK), lambda i: (0, 0, 0)),
        pl.BlockSpec((GROUPS, BLOCK, 1), lambda i: (0, 0, 0)),
        pl.BlockSpec(memory_space=pltpu.SMEM),
        pl.BlockSpec(memory_space=pltpu.SMEM),
        pl.BlockSpec((BLOCK, 2 * BLOCK), lambda i: (0, 0)),
    ]


def _mixer_fwd(zuv, qkv, gv, w_sp, b_sp, sinks, table, bucket, after=None):
    s = zuv.shape[0]
    nb = s // BLOCK

    def body(zuv_ref, q_ref, kvc_ref, kvp_ref, gv_ref, w_ref, b_ref, sink_ref, table_ref, bucket_ref,
             mix_ref, tanh_ref, prob_ref, psink_ref, bias_ref):
        i = pl.program_id(0)

        @pl.when(i == 0)
        def _():
            _build_bias(bias_ref, bucket_ref, table_ref)

        t = _gelu_tanh(zuv_ref[...])
        tanh_ref[...] = t
        u = _gelu(zuv_ref[:, :D_GMLP], t[:, :D_GMLP])
        vg = _gelu(zuv_ref[:, D_GMLP:], t[:, D_GMLP:])
        for g in range(GROUPS):
            sl = slice(g * GDIM, (g + 1) * GDIM)
            vg_g = vg[:, sl]
            vn = ((vg_g * _rms_scale(vg_g)) * gv_ref[:, sl]).astype(BF)
            sv = _nn(_tril_bf16(w_ref, g), vn) + b_ref[g]
            mix_ref[:, sl] = (u[:, sl] * sv).astype(BF)

        valid_prev, valid_cur = _window_masks(i)
        for kh in range(N_KV):
            ksl = slice(kh * HEAD_DIM, (kh + 1) * HEAD_DIM)
            vsl = slice(D_KV + kh * HEAD_DIM, D_KV + (kh + 1) * HEAD_DIM)
            q4 = _stack_heads(q_ref, kh, 0)
            p_prev, p_cur, p_sink = _attn_probs(
                q4, kvp_ref[:, ksl], kvc_ref[:, ksl], bias_ref[kh], _stack_sinks(sink_ref, kh), valid_prev, valid_cur)
            prob_ref[0, kh, :, :BLOCK] = p_prev
            prob_ref[0, kh, :, BLOCK:] = p_cur
            psink_ref[0, kh] = jnp.broadcast_to(p_sink, (ROWS4, LANES))
            o4 = _nn(p_prev.astype(BF), kvp_ref[:, vsl]) + _nn(p_cur.astype(BF), kvc_ref[:, vsl])
            for g in range(Q_PER_KV):
                first = D_GMLP + (kh * Q_PER_KV + g) * HEAD_DIM
                mix_ref[:, first:first + HEAD_DIM] = o4[g * BLOCK:(g + 1) * BLOCK].astype(BF)

    return _call(
        body,
        name="mixer_fwd",
        after=after,
        grid=(nb,),
        in_specs=_mixer_specs(nb),
        out_specs=[
            pl.BlockSpec((BLOCK, D_MODEL), lambda i: (i, 0)),
            pl.BlockSpec((BLOCK, 2 * D_GMLP), lambda i: (i, 0)),
            pl.BlockSpec((1, N_KV, ROWS4, 2 * BLOCK), lambda i: (i, 0, 0, 0)),
            pl.BlockSpec((1, N_KV, ROWS4, LANES), lambda i: (i, 0, 0, 0)),
        ],
        out_shape=[
            jax.ShapeDtypeStruct((s, D_MODEL), BF),
            jax.ShapeDtypeStruct((s, 2 * D_GMLP), F32),
            jax.ShapeDtypeStruct((nb, N_KV, ROWS4, 2 * BLOCK), F32),
            jax.ShapeDtypeStruct((nb, N_KV, ROWS4, LANES), F32),
        ],
        scratch_shapes=[pltpu.VMEM((N_KV, ROWS4, 2 * BLOCK), F32)],
        compiler_params=_params(("arbitrary",)),
    )(zuv, qkv, qkv, qkv, gv, w_sp, b_sp, sinks, table, bucket)


def _tail(x, mix, p, t, g2, gf, w_out, w_ff1, w_ff2, w_gate, w_proj):
    s = x.shape[0]
    tm = min(TOK_TILE, s)
    n_ff = w_ff1.shape[0]
    fc = D_FF // n_ff
    pc = D_MODEL // N_DEV

    def body(x_ref, mix_ref, p_ref, t_ref, g2_ref, gf_ref, wo_ref, w1_ref, w2_ref, wg_ref, wp_ref,
             small_ref, dh1_ref, dmix_ref, hn2_ref, a_ref, df_ref, dh2_ref, gwo_ref, gwp_ref, gwg_ref, f_ref):
        i = pl.program_id(0)

        @pl.when(i == 0)
        def _():
            small_ref[...] = jnp.zeros_like(small_ref)
            gwo_ref[...] = jnp.zeros_like(gwo_ref)
            gwp_ref[...] = jnp.zeros_like(gwp_ref)
            gwg_ref[...] = jnp.zeros_like(gwg_ref)

        h1 = x_ref[...] + _nn(mix_ref[...], wo_ref[...])
        dh1_ref[...] = h1
        r2 = _rms_scale(h1)
        hn2_ref[...] = ((h1 * r2) * g2_ref[...]).astype(BF)
        h2 = h1
        for c in range(n_ff):
            f = _nn(hn2_ref[...], w1_ref[c])
            f_ref[:, c * fc:(c + 1) * fc] = f
            a = jnp.square(jnp.maximum(f, 0.0)).astype(BF)
            a_ref[:, c * fc:(c + 1) * fc] = a
            h2 = h2 + _nn(a, w2_ref[c * fc:(c + 1) * fc, :])
        h2b = h2.astype(BF)
        gate = jax.nn.sigmoid(_nn(h2b, wg_ref[...]))
        pb = p_ref[...].astype(BF)
        pp = _nn(pb, wp_ref[...])
        h3 = h2 + gate * pp
        rf = _rms_scale(h3)
        gf_v = gf_ref[...]
        err = (h3 * rf) * gf_v - t_ref[...]
        small_ref[2:3, :] += jnp.sum(jnp.sum(err * err, axis=-1, keepdims=True), axis=0, keepdims=True) * (0.5 / D_MODEL)

        dy = err * (1.0 / D_MODEL)
        small_ref[1:2, :] += jnp.sum(dy * (h3 * rf), axis=0, keepdims=True)
        dh3 = _rms_bwd(dy * gf_v, h3, rf)
        gw_proj = _tn(pb, (dh3 * gate).astype(BF))
        for q in range(N_DEV):
            gwp_ref[q] += gw_proj[:, q * pc:(q + 1) * pc]
        dgl = ((dh3 * pp) * (gate * (1.0 - gate))).astype(BF)
        gwg_ref[...] += _tn(h2b, dgl)
        dh2 = dh3 + _nt(dgl, wg_ref[...])
        dmix_ref[...] = dh2
        dh2_ref[...] = dh2.astype(BF)
        dhn2 = jnp.zeros((tm, D_MODEL), F32)
        for c in range(n_ff):
            da = _nt(dh2_ref[...], w2_ref[c * fc:(c + 1) * fc, :])
            df = (da * (2.0 * jnp.maximum(f_ref[:, c * fc:(c + 1) * fc], 0.0))).astype(BF)
            df_ref[:, c * fc:(c + 1) * fc] = df
            dhn2 = dhn2 + _nt(df, w1_ref[c])
        h1 = dh1_ref[...]
        small_ref[0:1, :] += jnp.sum(dhn2 * (h1 * r2), axis=0, keepdims=True)
        dh1 = dmix_ref[...] + _rms_bwd(dhn2 * g2_ref[...], h1, r2)
        dh1_ref[...] = dh1
        dh1b = dh1.astype(BF)
        gwo_ref[...] += _tn(mix_ref[...], dh1b)
        dmix_ref[...] = _nt(dh1b, wo_ref[...])

    tile = lambda cols: pl.BlockSpec((tm, cols), lambda i: (i, 0))
    whole = lambda shape: pl.BlockSpec(shape, lambda i: (0,) * len(shape), pipeline_mode=pl.Buffered(1))
    total = lambda shape: pl.BlockSpec(shape, lambda i: (0,) * len(shape))
    row = pl.BlockSpec((1, D_MODEL), lambda i: (0, 0))
    act = lambda cols, dt: jax.ShapeDtypeStruct((s, cols), dt)
    gw_shapes = [(D_MODEL, D_MODEL), (N_DEV, PLE_DIM, pc), (D_MODEL, D_MODEL)]
    return _call(
        body,
        name="tail",
        grid=(s // tm,),
        in_specs=[tile(D_MODEL), tile(D_MODEL), tile(PLE_DIM), tile(D_MODEL), row, row,
                  whole(w_out.shape), whole(w_ff1.shape), whole(w_ff2.shape), whole(w_gate.shape), whole(w_proj.shape)],
        out_specs=[total((8, D_MODEL)), tile(D_MODEL), tile(D_MODEL), tile(D_MODEL), tile(D_FF), tile(D_FF), tile(D_MODEL),
                   *[total(shape) for shape in gw_shapes]],
        out_shape=[jax.ShapeDtypeStruct((8, D_MODEL), F32),
                   act(D_MODEL, F32), act(D_MODEL, F32), act(D_MODEL, BF), act(D_FF, BF), act(D_FF, BF), act(D_MODEL, BF),
                   *[jax.ShapeDtypeStruct(shape, F32) for shape in gw_shapes]],
        scratch_shapes=[pltpu.VMEM((tm, D_FF), F32)],
        compiler_params=_params(("arbitrary",)),
    )(x, mix, p, t, g2, gf, w_out, w_ff1, w_ff2, w_gate, w_proj)


def _mixer_bwd(zuv, qkv, dmix, saved, gv, w_sp, b_sp, bucket, after=None):
    s = zuv.shape[0]
    nb = s // BLOCK

    def body(zuv_ref, q_ref, kvc_ref, kvp_ref, gv_ref, w_ref, b_ref, bucket_ref, dmix_ref, tanh_ref, prob_ref, psink_ref,
             dzm_ref, dkv_ref, dgv_ref, dw_ref, db_ref, dattn_ref,
             dbias_ref, carry_ref, dsink_acc, db_acc):
        i = pl.program_id(0)

        @pl.when(i == 0)
        def _():
            dbias_ref[...] = jnp.zeros_like(dbias_ref)
            carry_ref[...] = jnp.zeros_like(carry_ref)
            dsink_acc[...] = jnp.zeros_like(dsink_acc)
            dgv_ref[...] = jnp.zeros_like(dgv_ref)
            dw_ref[...] = jnp.zeros_like(dw_ref)
            db_acc[...] = jnp.zeros_like(db_acc)

        @pl.when(i < nb)
        def _():
            u, du_dz = _gelu_and_grad(zuv_ref[:, :D_GMLP], tanh_ref[:, :D_GMLP])
            vg, dvg_dz = _gelu_and_grad(zuv_ref[:, D_GMLP:], tanh_ref[:, D_GMLP:])
            for g in range(GROUPS):
                sl = slice(g * GDIM, (g + 1) * GDIM)
                vg_g = vg[:, sl]
                rg = _rms_scale(vg_g)
                vhat = vg_g * rg
                gain = gv_ref[:, sl]
                vn = (vhat * gain).astype(BF)
                w_g = _tril_bf16(w_ref, g)
                sv = _nn(w_g, vn) + b_ref[g]
                dmix_a = dmix_ref[:, sl]
                dsv = dmix_a * u[:, sl]
                dsvb = dsv.astype(BF)
                db_acc[g] += jnp.sum(dsv, axis=-1, keepdims=True)
                dw_ref[g] += _nt(dsvb, vn)
                dvn = _tn(w_g, dsvb)
                dgv_ref[:, sl] += jnp.sum(dvn * vhat, axis=0, keepdims=True)
                dvg = _rms_bwd(dvn * gain, vg_g, rg)
                dzm_ref[:, sl] = ((dmix_a * sv) * du_dz[:, sl]).astype(BF)
                dzm_ref[:, D_GMLP + g * GDIM: D_GMLP + (g + 1) * GDIM] = (dvg * dvg_dz[:, sl]).astype(BF)

            for kh in range(N_KV):
                ksl = slice(kh * HEAD_DIM, (kh + 1) * HEAD_DIM)
                vsl = slice(D_KV + kh * HEAD_DIM, D_KV + (kh + 1) * HEAD_DIM)
                k_prev, k_cur = kvp_ref[:, ksl], kvc_ref[:, ksl]
                v_prev, v_cur = kvp_ref[:, vsl], kvc_ref[:, vsl]
                q4 = _stack_heads(q_ref, kh, 0)
                p_prev, p_cur, p_sink = prob_ref[0, kh, :, :BLOCK], prob_ref[0, kh, :, BLOCK:], psink_ref[0, kh, :, 0:1]
                do4 = _stack_heads(dmix_ref, kh, D_GMLP)
                dp_prev = _nt(do4, v_prev)
                dp_cur = _nt(do4, v_cur)
                delta = jnp.sum(p_prev * dp_prev, axis=-1, keepdims=True) + jnp.sum(p_cur * dp_cur, axis=-1, keepdims=True)
                ds_prev = p_prev * (dp_prev - delta)
                ds_cur = p_cur * (dp_cur - delta)
                dsink_acc[kh] -= p_sink * delta
                dbias_ref[kh, :, :BLOCK] += ds_prev
                dbias_ref[kh, :, BLOCK:] += ds_cur
                dsb_prev = ds_prev.astype(BF)
                dsb_cur = ds_cur.astype(BF)
                dq4 = (_nn(dsb_prev, k_prev) + _nn(dsb_cur, k_cur)) * SCALE
                for g in range(Q_PER_KV):
                    first = 2 * D_GMLP + (kh * Q_PER_KV + g) * HEAD_DIM
                    dzm_ref[:, first:first + HEAD_DIM] = dq4[g * BLOCK:(g + 1) * BLOCK].astype(BF)
                dkv_ref[:, ksl] = (carry_ref[:, ksl] + _tn(dsb_prev, q4) * SCALE).astype(BF)
                dkv_ref[:, vsl] = (carry_ref[:, vsl] + _tn(p_prev.astype(BF), do4)).astype(BF)
                carry_ref[:, ksl] = _tn(dsb_cur, q4) * SCALE
                carry_ref[:, vsl] = _tn(p_cur.astype(BF), do4)

        @pl.when(i == nb)
        def _():
            dkv_ref[...] = carry_ref[...].astype(BF)
            row = lax.broadcasted_iota(jnp.int32, (BLOCK, BLOCK), 0)
            col = lax.broadcasted_iota(jnp.int32, (BLOCK, BLOCK), 1)
            for g in range(GROUPS):
                dw_ref[g] = jnp.where(col <= row, dw_ref[g], 0.0)
                db_ref[g:g + 1, :] = jnp.sum(jnp.where(col == row, db_acc[g], 0.0), axis=0, keepdims=True)
            bucket = bucket_ref[...]
            for b in range(N_Q, REL_BUCKETS):
                dattn_ref[N_Q, b] = 0.0
            for h in range(N_Q):
                kh, rows = _head_rows(h)
                dattn_ref[N_Q, h] = jnp.sum(dsink_acc[kh, rows, :])
                dbh = dbias_ref[kh, rows, :]
                for b in range(REL_BUCKETS):
                    dattn_ref[h, b] = jnp.sum(jnp.where(bucket == b, dbh, 0.0))

    cl = lambda i: jnp.minimum(i, nb - 1)
    const = lambda shape: pl.BlockSpec(shape, lambda i: (0,) * len(shape))
    return _call(
        body,
        name="mixer_bwd",
        after=after,
        grid=(nb + 1,),
        in_specs=_mixer_specs(nb)[:7] + [
            const((BLOCK, 2 * BLOCK)),
            pl.BlockSpec((BLOCK, D_MODEL), lambda i: (cl(i), 0)),
            pl.BlockSpec((BLOCK, 2 * D_GMLP), lambda i: (cl(i), 0)),
            pl.BlockSpec((1, N_KV, ROWS4, 2 * BLOCK), lambda i: (cl(i), 0, 0, 0)),
            pl.BlockSpec((1, N_KV, ROWS4, LANES), lambda i: (cl(i), 0, 0, 0)),
        ],
        out_specs=[
            pl.BlockSpec((BLOCK, D_MAIN), lambda i: (cl(i), 0)),
            pl.BlockSpec((BLOCK, 2 * D_KV), lambda i: (jnp.maximum(i - 1, 0), 0)),
            const((1, D_GMLP)),
            const((GROUPS, BLOCK, BLOCK)),
            const((GROUPS, BLOCK)),
            pl.BlockSpec(memory_space=pltpu.SMEM),
        ],
        out_shape=[
            jax.ShapeDtypeStruct((s, D_MAIN), BF),
            jax.ShapeDtypeStruct((s, 2 * D_KV), BF),
            jax.ShapeDtypeStruct((1, D_GMLP), F32),
            jax.ShapeDtypeStruct((GROUPS, BLOCK, BLOCK), F32),
            jax.ShapeDtypeStruct((GROUPS, BLOCK), F32),
            jax.ShapeDtypeStruct((N_Q + 1, REL_BUCKETS), F32),
        ],
        scratch_shapes=[
            pltpu.VMEM((N_KV, ROWS4, 2 * BLOCK), F32),
            pltpu.VMEM((BLOCK, 2 * D_KV), F32),
            pltpu.VMEM((N_KV, ROWS4, 1), F32),
            pltpu.VMEM((GROUPS, BLOCK, 1), F32),
        ],
        compiler_params=_params(("arbitrary",)),
    )(zuv, qkv, qkv, qkv, gv, w_sp, b_sp, bucket, dmix, *saved)


def _in_bwd(x, dh1, dzm, dkv, g1, w_in_t, after=None):
    s = x.shape[0]
    tm = min(IO_TOK_TILE, s)

    def body(x_ref, dh1_ref, dzm_ref, dkv_ref, g_ref, w_ref, dx_ref, dg_ref):
        @pl.when(pl.program_id(0) == 0)
        def _():
            dg_ref[...] = jnp.zeros_like(dg_ref)

        dhn = _nn(dzm_ref[...], w_ref[:D_MAIN, :]) + _nn(dkv_ref[...], w_ref[D_MAIN:, :])
        xv = x_ref[...]
        r = _rms_scale(xv)
        dg_ref[...] += jnp.sum(dhn * (xv * r), axis=0, keepdims=True)
        dx_ref[...] = dh1_ref[...] + _rms_bwd(dhn * g_ref[...], xv, r)

    tile = lambda cols: pl.BlockSpec((tm, cols), lambda i: (i, 0))
    row = pl.BlockSpec((1, D_MODEL), lambda i: (0, 0))
    return _call(
        body,
        name="in_bwd",
        after=after,
        grid=(s // tm,),
        in_specs=[tile(D_MODEL), tile(D_MODEL), tile(D_MAIN), tile(2 * D_KV), row, pl.BlockSpec((D_IN, D_MODEL), lambda i: (0, 0))],
        out_specs=[tile(D_MODEL), row],
        out_shape=[jax.ShapeDtypeStruct((s, D_MODEL), F32), jax.ShapeDtypeStruct((1, D_MODEL), F32)],
        compiler_params=_params(("arbitrary",)),
    )(x, dh1, dzm, dkv, g1, w_in_t)


def _wgrad_in(dzm, dkv, hn, after=None):
    s = hn.shape[0]
    tm = 2 * D_KV
    n_main = D_MAIN // tm

    def body(dzm_ref, dkv_ref, hn_ref, o_ref):
        i = pl.program_id(0)

        @pl.when(i < n_main)
        def _():
            o_ref[...] = _tn(dzm_ref[...], hn_ref[...])

        @pl.when(i == n_main)
        def _():
            o_ref[...] = _tn(dkv_ref[...], hn_ref[...])

    return _call(
        body,
        name="wgrad_in",
        after=after,
        grid=(n_main + 1,),
        in_specs=[
            pl.BlockSpec((s, tm), lambda i: (0, jnp.minimum(i, n_main - 1))),
            pl.BlockSpec((s, tm), lambda i: (0, 0)),
            pl.BlockSpec((s, D_MODEL), lambda i: (0, 0)),
        ],
        out_specs=pl.BlockSpec((tm, D_MODEL), lambda i: (i, 0)),
        out_shape=jax.ShapeDtypeStruct((D_IN, D_MODEL), F32),
        compiler_params=_params(("arbitrary",)),
    )(dzm, dkv, hn)


def _wgrad(a, b, tm, tn, name, peer_cols=0, after=None, tiles=None, into=None):
    s, m = a.shape
    n = b.shape[1]
    first, count = tiles or (0, m // tm)

    def body(a_ref, b_ref, *refs):
        o_ref, at_ref = refs[-2:]

        @pl.when(pl.program_id(1) == 0)
        def _():
            at_ref[...] = a_ref[...].astype(BF).T

        r = _nn(at_ref[...], b_ref[...])
        if peer_cols:
            for q in range(tn // peer_cols):
                o_ref[q] = r[:, q * peer_cols:(q + 1) * peer_cols]
        else:
            o_ref[...] = r

    if peer_cols:
        out_spec = pl.BlockSpec((tn // peer_cols, tm, peer_cols), lambda i, j: (j, first + i, 0))
        out_shape = jax.ShapeDtypeStruct((n // peer_cols, m, peer_cols), F32)
    else:
        out_spec = pl.BlockSpec((tm, tn), lambda i, j: (first + i, j))
        out_shape = jax.ShapeDtypeStruct((m, n), F32)
    in_specs = [pl.BlockSpec((s, tm), lambda i, j: (0, first + i)), pl.BlockSpec((s, tn), lambda i, j: (0, j))]
    earlier = [] if into is None else [into]
    return _call(
        body,
        name=name,
        after=after,
        grid=(count, n // tn),
        in_specs=in_specs + [pl.BlockSpec(memory_space=pl.ANY)] * len(earlier),
        out_specs=out_spec,
        out_shape=out_shape,
        scratch_shapes=[pltpu.VMEM((tm, s), BF)],
        input_output_aliases={2: 0} if earlier else {},
        compiler_params=_params(("arbitrary", "arbitrary")),
    )(a, b, *earlier)


def _adamw_math(w, g, m, v):
    m_new = ADAM_B1 * m + (1.0 - ADAM_B1) * g
    v_new = ADAM_B2 * v + (1.0 - ADAM_B2) * jnp.square(g)
    m_hat = m_new / (1.0 - ADAM_B1 ** ADAM_STEP)
    v_hat = v_new / (1.0 - ADAM_B2 ** ADAM_STEP)
    delta = -ADAM_LR * (m_hat / (jnp.sqrt(v_hat) + ADAM_EPS) + ADAM_WD * w)
    return delta, m_new, v_new


def _final_adamw(part, recv, w, m, v, name):
    r, c = w.shape
    tr = min(r, 512)

    def body(p_ref, r_ref, w_ref, m_ref, v_ref, g_ref, d_ref, mo_ref, vo_ref):
        g = p_ref[...]
        for j in range(3):
            g = g + r_ref[j].astype(F32)
        g_ref[...] = g
        d_ref[...], mo_ref[...], vo_ref[...] = _adamw_math(w_ref[...], g, m_ref[...], v_ref[...])

    spec = pl.BlockSpec((tr, c), lambda i: (i, 0))
    return _call(
        body,
        name=name,
        grid=(r // tr,),
        in_specs=[spec, pl.BlockSpec((3, tr, c), lambda i: (0, i, 0)), spec, spec, spec],
        out_specs=[spec] * 4,
        out_shape=[jax.ShapeDtypeStruct((r, c), F32)] * 4,
        compiler_params=_params(("arbitrary",)),
    )(part, recv, w, m, v)


def _rs_sum(gs, lands, blocks, name):
    n = len(gs)
    steps = pl.cdiv(gs[0].shape[1], 256)
    assert all(g.shape[1] % steps == 0 for g in gs)

    def body(blk_ref, *refs):
        ins, outs = refs[:5 * n], refs[5 * n:]
        for k in range(n):
            g0_ref, g1_ref, g2_ref, g3_ref, l_ref = ins[5 * k:5 * k + 5]
            outs[2 * k][...] = g0_ref[0] + l_ref[0]
            for j, gj_ref in enumerate((g1_ref, g2_ref, g3_ref)):
                outs[2 * k + 1][j] = (gj_ref[0] + l_ref[j + 1]).astype(BF)

    in_specs, out_specs, out_shape, operands = [], [], [], []
    for g, land in zip(gs, lands):
        _, r, c = g.shape
        tr = r // steps
        in_specs += [pl.BlockSpec((1, tr, c), lambda i, blk, j=j: (blk[j], i, 0)) for j in range(4)]
        in_specs.append(pl.BlockSpec((4, tr, c), lambda i, blk: (0, i, 0)))
        out_specs += [pl.BlockSpec((tr, c), lambda i, blk: (i, 0)), pl.BlockSpec((3, tr, c), lambda i, blk: (0, i, 0))]
        out_shape += [jax.ShapeDtypeStruct((r, c), F32), jax.ShapeDtypeStruct((3, r, c), BF)]
        operands += [g, g, g, g, land]
    outs = _call(
        body,
        name=name,
        grid_spec=pltpu.PrefetchScalarGridSpec(num_scalar_prefetch=1, grid=(steps,), in_specs=in_specs, out_specs=out_specs),
        out_shape=out_shape,
        compiler_params=_params(("arbitrary",)),
    )(blocks, *operands)
    return [(outs[2 * k], outs[2 * k + 1]) for k in range(n)]


def _final_adamw_few(parts, recvs, ws, ms, vs, name):
    n = len(ws)

    def body(*refs):
        ins, outs = refs[:5 * n], refs[5 * n:]
        for k in range(n):
            p_ref, r_ref, w_ref, m_ref, v_ref = (ins[q * n + k] for q in range(5))
            g = p_ref[...]
            for j in range(3):
                g = g + r_ref[j].astype(F32)
            outs[k][...] = g
            outs[n + k][...], outs[2 * n + k][...], outs[3 * n + k][...] = _adamw_math(w_ref[...], g, m_ref[...], v_ref[...])

    outs = _call(
        body,
        name=name,
        out_shape=[jax.ShapeDtypeStruct(w.shape, F32) for w in ws] * 4,
        compiler_params=_params(),
    )(*parts, *recvs, *ws, *ms, *vs)
    return outs[:n], outs[n:2 * n], outs[2 * n:3 * n], outs[3 * n:]


def _adamw_small(own, recv, g1_own, g1_recv, weights, moms, vels, after):
    n_w = len(weights)

    def body(*refs):
        own_refs, recv_refs, g1_own_ref, g1_recv_ref = refs[:5], refs[5:10], refs[10], refs[11]
        w_refs, m_refs, v_refs = refs[12:12 + n_w], refs[12 + n_w:12 + 2 * n_w], refs[12 + 2 * n_w:12 + 3 * n_w]
        outs = refs[12 + 3 * n_w:]
        loss_ref, g_refs, d_refs = outs[0], outs[1:1 + n_w], outs[1 + n_w:1 + 2 * n_w]
        mo_refs, vo_refs = outs[1 + 2 * n_w:1 + 3 * n_w], outs[1 + 3 * n_w:]
        x, y, c = lax.axis_index("x"), lax.axis_index("y"), lax.axis_index("c")

        def in_place_order(values, my_place):
            acc = None
            for place in range(len(values)):
                r = place ^ my_place
                term = values[-1]
                for q in range(len(values) - 2, -1, -1):
                    term = jnp.where(r == q, values[q], term)
                acc = term if acc is None else acc + term
            return acc

        def total(k, *index):
            index = index or (slice(None),) * (len(own_refs[k].shape) - 1)
            across = [own_refs[k][(0, *index)], recv_refs[k][(1, *index)], recv_refs[k][(0, *index)], recv_refs[k][(2, *index)]]
            return in_place_order([v.astype(F32) for v in across], 2 * x + y)

        g1 = in_place_order([g1_own_ref[0]] + [g1_recv_ref[j] for j in range(N_DEV - 1)], 4 * x + 2 * y + c)
        grads = [
            g1, total(1), total(2), total(3),
            total(4, slice(N_Q, None), slice(0, N_Q)), total(4, slice(0, N_Q), slice(None)),
            total(0, slice(0, 1), slice(None)), total(0, slice(1, 2), slice(None))]
        loss_ref[...] = total(0, slice(2, 3), slice(0, 1))
        for k in range(n_w):
            g_refs[k][...] = grads[k]
            d_refs[k][...], mo_refs[k][...], vo_refs[k][...] = _adamw_math(w_refs[k][...], grads[k], m_refs[k][...], v_refs[k][...])

    shapes = [jax.ShapeDtypeStruct(w.shape, F32) for w in weights]
    outs = _call(
        body,
        name="adamw_small",
        after=after,
        in_specs=[pl.BlockSpec(memory_space=pltpu.VMEM)] * (12 + 3 * n_w),
        out_shape=[jax.ShapeDtypeStruct((1, 1), F32)] + shapes * 4,
        compiler_params=_params(),
    )(*own, *recv, g1_own, g1_recv, *weights, *moms, *vels)
    return outs[0], outs[1:1 + n_w], outs[1 + n_w:1 + 2 * n_w], outs[1 + 2 * n_w:1 + 3 * n_w], outs[1 + 3 * n_w:]


def _place():
    x, y, c = lax.axis_index("x"), lax.axis_index("y"), lax.axis_index("c")
    return x, y, c, [(1 - x, y), (x, 1 - y), (1 - x, 1 - y)]


def _dev_index(px, py, pc):
    return 4 * px + 2 * py + pc


HBM_SPEC = pl.BlockSpec(memory_space=pltpu.HBM)
SEM_SPEC = pl.BlockSpec(memory_space=pltpu.SEMAPHORE)
ANY_SPEC = pl.BlockSpec(memory_space=pl.ANY)
DATAFLOW = pltpu.SideEffectType.DATAFLOW_SIDE_EFFECTING


def _hbm(a):
    return pltpu.with_memory_space_constraint(a, pltpu.HBM)


def _prep_weights(shards):
    k_n = len(shards)

    def body(*refs):
        ins, outs, stage, sems = refs[:k_n], refs[k_n:2 * k_n], refs[2 * k_n:3 * k_n], refs[3 * k_n]
        x, y, c, _ = _place()
        copies = []
        for k in range(k_n):
            stage[k][...] = ins[k][...].astype(BF)
            copies.append(pltpu.make_async_copy(stage[k], outs[k].at[_dev_index(x, y, c)], sems.at[k]))
            copies[k].start()
        for cp in copies:
            cp.wait()

    return _call(
        body,
        name="prep_weights",
        in_specs=[pl.BlockSpec(memory_space=pltpu.VMEM)] * k_n,
        out_specs=[ANY_SPEC] * k_n,
        out_shape=[jax.ShapeDtypeStruct((N_DEV,) + sh.shape, BF) for sh in shards],
        scratch_shapes=[pltpu.VMEM(sh.shape, BF) for sh in shards] + [pltpu.SemaphoreType.DMA((k_n,))],
        compiler_params=_params(),
    )(*shards)


def _ag_piece(land_k, block, half, peer, send_sem, recv_sem):
    ref = land_k.at[_dev_index(*block)]
    if half is not None:
        rows = land_k.shape[1] // 2
        ref = ref.at[pl.ds(half * rows, rows)]
    return pltpu.make_async_remote_copy(
        src_ref=ref, dst_ref=ref, send_sem=send_sem, recv_sem=recv_sem, device_id=peer, device_id_type=MESH)


def _ag_plan():
    x, y, c, _ = _place()
    me, sib = (x, y, c), (x, y, 1 - c)
    xn, yn, diag = (1 - x, y, c), (x, 1 - y, c), (1 - x, 1 - y, c)
    return dict(
        relay_halves=[(me, 0, xn), (me, 1, yn)],
        others=[(me, None, sib), (me, 1, xn), (me, 0, yn)],
        relays=[(xn, 0, yn), (yn, 1, xn)],
        near=[(xn, None, sib), (yn, None, sib)],
        far=[(diag, None, sib)],
    )


def _ag_stage(land, stage, send_sems, recv_sems, act):
    copies = _ag_plan()[stage]
    n = len(copies)
    for k in range(len(land)):
        for j, (block, half, peer) in enumerate(copies):
            cp = _ag_piece(land[k], block, half, peer, send_sems.at[n * k + j], recv_sems.at[n * k + j])
            if act == "start":
                cp.start()
            else:
                cp.wait_send()
                cp.wait_recv()


def _sem_shapes(*counts):
    return [pltpu.SemaphoreType.DMA((n,)) for n in counts for _ in range(2)]


def _ag_start(first, rest):
    lands = list(first) + list(rest)
    k_n, k_first = len(lands), len(first)
    k_rest = k_n - k_first

    def body(*refs):
        land = refs[:k_n]
        sems = refs[k_n:k_n + 6]
        token = refs[-1]
        x, y, c, chips = _place()
        targets = [(x, y, 1 - c)] + [(*chip, c) for chip in chips]
        for k in range(k_first):
            for j, to in enumerate(targets):
                _ag_piece(land[k], (x, y, c), None, to, sems[0].at[4 * k + j], sems[1].at[4 * k + j]).start()
        _ag_stage(land[k_first:], "relay_halves", sems[2], sems[3], "start")
        _ag_stage(land[k_first:], "others", sems[4], sems[5], "start")
        token[...] = jnp.zeros_like(token)

    outs = pl.pallas_call(
        body,
        name="ag_start",
        in_specs=[HBM_SPEC] * k_n,
        out_specs=(*[SEM_SPEC] * 6, *[HBM_SPEC] * k_n, pl.BlockSpec(memory_space=pltpu.VMEM)),
        out_shape=(*_sem_shapes(4 * k_first, 2 * k_rest, 3 * k_rest),
                   *[pltpu.HBM(a.shape, a.dtype) for a in lands], jax.ShapeDtypeStruct((8, LANES), F32)),
        input_output_aliases={k: 6 + k for k in range(k_n)},
        compiler_params=pltpu.CompilerParams(has_side_effects=DATAFLOW),
    )(*[_hbm(a) for a in lands])
    flying = list(outs[6:6 + k_n])
    return (outs[0], outs[1], flying[:k_first]), (outs[2:6], flying[k_first:]), outs[-1]


def _ag_split_call(lands, waits, starts, after, name):
    k_n = len(lands)
    plan_sizes = dict(relay_halves=2, others=3, relays=2, near=2, far=1)
    n_in, n_out = 2 * len(waits), 2 * len(starts)

    def body(*refs):
        land = refs[:k_n]
        in_sems = refs[k_n:k_n + n_in]
        out_sems, token = refs[len(refs) - 1 - n_out:len(refs) - 1], refs[-1]
        for w, (stage, _, _) in enumerate(waits):
            _ag_stage(land, stage, in_sems[2 * w], in_sems[2 * w + 1], "wait")
            if w < len(starts):
                _ag_stage(land, starts[w], out_sems[2 * w], out_sems[2 * w + 1], "start")
        token[...] = jnp.zeros_like(token)

    outs = pl.pallas_call(
        body,
        name=name,
        in_specs=[HBM_SPEC] * k_n + [SEM_SPEC] * n_in + [ANY_SPEC],
        out_specs=(*[HBM_SPEC] * k_n, *[SEM_SPEC] * n_out, pl.BlockSpec(memory_space=pltpu.VMEM)),
        out_shape=(*[pltpu.HBM(a.shape, a.dtype) for a in lands], *_sem_shapes(*[plan_sizes[s] * k_n for s in starts]),
                   jax.ShapeDtypeStruct((8, LANES), F32)),
        input_output_aliases={k: k for k in range(k_n)},
        compiler_params=pltpu.CompilerParams(has_side_effects=DATAFLOW),
    )(*lands, *[s for _, a, b in waits for s in (a, b)], after)
    return list(outs[:k_n]), list(outs[k_n:k_n + n_out]), outs[-1]


def _ag_mid(lands, send_sems, recv_sems, after, name):
    k_n = len(lands)

    def body(*refs):
        land = refs[:k_n]
        send1, recv1 = refs[k_n], refs[k_n + 1]
        fwd_send, fwd_recv = refs[-2], refs[-1]
        x, y, c, chips = _place()
        sources = [(x, y, 1 - c)] + [(*chip, c) for chip in chips]
        for k in range(k_n):
            mine = land[k].at[_dev_index(x, y, c)]
            for j, frm in enumerate(sources):
                got = land[k].at[_dev_index(*frm)]
                cp = pltpu.make_async_remote_copy(
                    src_ref=mine, dst_ref=got, send_sem=send1.at[4 * k + j], recv_sem=recv1.at[4 * k + j], device_id=frm, device_id_type=MESH)
                cp.wait_send()
                cp.wait_recv()
                if j >= 1:
                    pltpu.make_async_remote_copy(
                        src_ref=got, dst_ref=got, send_sem=fwd_send.at[3 * k + j - 1], recv_sem=fwd_recv.at[3 * k + j - 1],
                        device_id=(x, y, 1 - c), device_id_type=MESH).start()

    outs = pl.pallas_call(
        body,
        name=name,
        in_specs=[HBM_SPEC] * k_n + [SEM_SPEC, SEM_SPEC, ANY_SPEC],
        out_specs=(*[HBM_SPEC] * k_n, SEM_SPEC, SEM_SPEC),
        out_shape=(*[pltpu.HBM(a.shape, a.dtype) for a in lands], pltpu.SemaphoreType.DMA((3 * k_n,)), pltpu.SemaphoreType.DMA((3 * k_n,))),
        input_output_aliases={k: k for k in range(k_n)},
        compiler_params=pltpu.CompilerParams(has_side_effects=DATAFLOW),
    )(*lands, send_sems, recv_sems, after)
    return list(outs[:k_n]), outs[-2], outs[-1]


def _ag_end(lands, fwd_send, fwd_recv, name):
    k_n = len(lands)

    def body(*refs):
        land = refs[:k_n]
        fsend, frecv = refs[k_n], refs[k_n + 1]
        x, y, c, chips = _place()
        for k in range(k_n):
            for j, chip in enumerate(chips):
                cp = pltpu.make_async_remote_copy(
                    src_ref=land[k].at[_dev_index(*chip, c)], dst_ref=land[k].at[_dev_index(*chip, 1 - c)],
                    send_sem=fsend.at[3 * k + j], recv_sem=frecv.at[3 * k + j], device_id=(x, y, 1 - c), device_id_type=MESH)
                cp.wait_send()
                cp.wait_recv()

    outs = pl.pallas_call(
        body,
        name=name,
        in_specs=[HBM_SPEC] * k_n + [SEM_SPEC, SEM_SPEC],
        out_specs=tuple([HBM_SPEC] * k_n),
        out_shape=tuple(pltpu.HBM(a.shape, a.dtype) for a in lands),
        input_output_aliases={k: k for k in range(k_n)},
        compiler_params=pltpu.CompilerParams(has_side_effects=DATAFLOW),
    )(*lands, fwd_send, fwd_recv)
    return list(outs)


def _chips4():
    x, y, c, others = _place()
    return x, y, c, [(x, y)] + others


def _route_sibling(j):
    x, y, c, chips = _chips4()
    return _dev_index(*chips[j], 1 - c), j, (x, y, 1 - c)


def _route_chips(j):
    x, y, c, chips = _chips4()
    return j, j, (*chips[j + 1], c)


def _route_sibling_whole(j):
    x, y, c, _ = _chips4()
    return 0, 0, (x, y, 1 - c)


def _route_chips_whole(j):
    x, y, c, chips = _chips4()
    return 0, j, (*chips[j + 1], c)


def _route_everyone(j):
    x, y, c, _ = _chips4()
    flip = [(j + 1) >> 2 & 1, (j + 1) >> 1 & 1, (j + 1) & 1]
    return 0, j, tuple(1 - v if f else v for v, f in zip((x, y, c), flip))


def _xchg_copies(routes, src, dst, send_sems, recv_sems):
    copies, sem = [], 0
    for k, (route, n) in enumerate(routes):
        for j in range(n):
            si, di, peer = route(j)
            copies.append(pltpu.make_async_remote_copy(
                src_ref=src[k].at[si], dst_ref=dst[k].at[di], send_sem=send_sems.at[sem], recv_sem=recv_sems.at[sem],
                device_id=peer, device_id_type=MESH))
            sem += 1
    return copies


def _xchg_start(srcs, slot_shapes, routes, name):
    k_n = len(srcs)
    n_sem = sum(n for _, n in routes)
    dsts = [lax.empty((n,) + tuple(sh), a.dtype) for sh, a, (_, n) in zip(slot_shapes, srcs, routes)]

    def body(*refs):
        src, dst = refs[:k_n], refs[k_n:2 * k_n]
        send_sems, recv_sems, token = refs[2 * k_n], refs[2 * k_n + 1], refs[-1]
        for cp in _xchg_copies(routes, src, dst, send_sems, recv_sems):
            cp.start()
        token[...] = jnp.zeros_like(token)

    arrays = list(srcs) + dsts
    outs = pl.pallas_call(
        body,
        name=name,
        in_specs=[HBM_SPEC] * (2 * k_n),
        out_specs=(SEM_SPEC, SEM_SPEC, *[HBM_SPEC] * (2 * k_n), pl.BlockSpec(memory_space=pltpu.VMEM)),
        out_shape=(pltpu.SemaphoreType.DMA((n_sem,)), pltpu.SemaphoreType.DMA((n_sem,)),
                   *[pltpu.HBM(a.shape, a.dtype) for a in arrays], jax.ShapeDtypeStruct((8, LANES), F32)),
        input_output_aliases={i: 2 + i for i in range(2 * k_n)},
        compiler_params=pltpu.CompilerParams(has_side_effects=DATAFLOW),
    )(*[_hbm(a) for a in arrays])
    return outs[0], outs[1], list(outs[2:2 + k_n]), list(outs[2 + k_n:2 + 2 * k_n]), outs[-1]


def _xchg_wait(send_sems, recv_sems, srcs, dsts, routes, after, name):
    k_n = len(srcs)

    def body(*refs):
        src, dst = refs[:k_n], refs[k_n:2 * k_n]
        for cp in _xchg_copies(routes, src, dst, refs[2 * k_n], refs[2 * k_n + 1]):
            cp.wait_send()
            cp.wait_recv()

    arrays = list(srcs) + list(dsts)
    outs = pl.pallas_call(
        body,
        name=name,
        in_specs=[HBM_SPEC] * (2 * k_n) + [SEM_SPEC, SEM_SPEC, ANY_SPEC],
        out_specs=tuple([HBM_SPEC] * (2 * k_n)),
        out_shape=tuple(pltpu.HBM(a.shape, a.dtype) for a in arrays),
        input_output_aliases={i: i for i in range(2 * k_n)},
        compiler_params=pltpu.CompilerParams(has_side_effects=DATAFLOW),
    )(*arrays, send_sems, recv_sems, after)
    return list(outs[:k_n]), list(outs[k_n:])


SMALL = ("norm1_gain", "gmlp_v_gain", "w_spatial", "b_spatial", "attn_sinks", "rel_bias_table", "norm2_gain", "final_gain")
LANES = 128


def _swap_start(grads, smalls, tag):
    srcs = list(grads) + [a[None] for a in smalls]
    shapes = [g.shape[1:] for g in grads] + [a.shape for a in smalls]
    routes = [(_route_sibling, 4)] * len(grads) + [(_route_sibling_whole, 1)] * len(smalls)
    return _xchg_start(srcs, shapes, routes, f"rs_{tag}_swap_start"), routes, len(grads)


def _swap_sums(swap, after, tag, small_dtypes=None):
    (send1, recv1, src1, land1, _), routes, n_rs = swap
    x, y, c, chips = _chips4()
    blocks = jnp.stack([_dev_index(*chip, c) for chip in chips]).astype(jnp.int32)
    src1, land1 = _xchg_wait(send1, recv1, src1, land1, routes, after, f"rs_{tag}_swap_wait")
    sums = _rs_sum(src1[:n_rs], land1[:n_rs], blocks, f"rs_{tag}_sum")
    if len(src1) == n_rs:
        return sums, []
    return sums, _pair_sum(src1[n_rs:], land1[n_rs:], small_dtypes or [F32] * (len(src1) - n_rs), f"rs_{tag}_sum_small")


def _pair_sum(mine, theirs, dtypes, name):
    def body(*refs):
        n = len(refs) // 3
        for k in range(n):
            refs[2 * n + k][...] = (refs[k][...] + refs[n + k][...]).astype(dtypes[k])

    return _call(
        body,
        name=name,
        out_shape=[jax.ShapeDtypeStruct(a.shape, dt) for a, dt in zip(mine, dtypes)],
        compiler_params=_params(),
    )(*mine, *theirs)


def _chips_start(sums, small_sums, tag, everyone=()):
    sends = [ps[1] for ps in sums] + list(small_sums) + [a[None] for a in everyone]
    routes = [(_route_chips, 3)] * len(sums) + [(_route_chips_whole, 3)] * len(small_sums) + [(_route_everyone, 7)] * len(everyone)
    return _xchg_start(sends, [a.shape[1:] for a in sends], routes, f"rs_{tag}_chips_start"), routes


def _chips_wait(chips, after, tag):
    (send2, recv2, src2, land2, _), routes = chips
    return _xchg_wait(send2, recv2, src2, land2, routes, after, f"rs_{tag}_chips_wait")


def kernel(x, p, norm1_gain, w_in, gmlp_v_gain, w_spatial, b_spatial, attn_sinks, rel_bias_table, w_out, norm2_gain, w_ff1, w_ff2, w_ple_proj, w_ple_gate, final_gain, loss_target, m_norm1_gain, m_w_in, m_gmlp_v_gain, m_w_spatial, m_b_spatial, m_attn_sinks, m_rel_bias_table, m_w_out, m_norm2_gain, m_w_ff1, m_w_ff2, m_w_ple_proj, m_w_ple_gate, m_final_gain, v_norm1_gain, v_w_in, v_gmlp_v_gain, v_w_spatial, v_b_spatial, v_attn_sinks, v_rel_bias_table, v_w_out, v_norm2_gain, v_w_ff1, v_w_ff2, v_w_ple_proj, v_w_ple_gate, v_final_gain):
    args = dict(locals())
    s = x.shape[1]
    big = ("w_in", "w_out", "w_ff1", "w_ff2", "w_ple_proj", "w_ple_gate")

    x2, p2, t2 = x.reshape(s, D_MODEL), p.reshape(s, PLE_DIM), loss_target.reshape(s, D_MODEL)
    g1, gv, w_sp, b_sp, sinks, table, g2, gf = (args[n] for n in SMALL)
    bucket = jnp.asarray(_bucket_table())
    b_col = b_sp.reshape(GROUPS, BLOCK, 1)

    def shard(name):
        return args[name][0].T if name.endswith("w_in") else args[name][0]

    lands = _prep_weights([shard(n) for n in big])
    (send_in, recv_in, fly_in), (rest_sems, fly_rest), token = _ag_start(lands[:1], lands[1:])
    mid_in, fwd_send_in, fwd_recv_in = _ag_mid(fly_in, send_in, recv_in, token, "ag_mid_w_in")
    g_in = _ag_end(mid_in, fwd_send_in, fwd_recv_in, "ag_end_w_in")[0]
    full_in = g_in.reshape(D_IN, D_MODEL)

    zuv, qkv, hn1 = _in_proj(x2, g1, full_in)
    fly_rest, relay_sems, relayed = _ag_split_call(
        fly_rest, [("relay_halves", *rest_sems[:2])], ["relays"], zuv, "ag_relay")
    mix, *saved = _mixer_fwd(zuv, qkv, gv, w_sp[0], b_col, sinks, table.T, bucket, after=relayed)
    fly_rest, fwd_sems, _ = _ag_split_call(
        fly_rest, [("others", *rest_sems[2:]), ("relays", *relay_sems)], ["near", "far"], mix, "ag_mid_rest")
    g_out, g_ff1, g_ff2, g_proj, g_gate = _ag_split_call(
        fly_rest, [("near", *fwd_sems[:2]), ("far", *fwd_sems[2:])], [], mix, "ag_end_rest")[0]
    full_out, full_ff2, full_gate = g_out.reshape(D_MODEL, D_MODEL), g_ff2.reshape(D_FF, D_MODEL), g_gate.reshape(D_MODEL, D_MODEL)
    full_proj = g_proj.transpose(1, 0, 2).reshape(PLE_DIM, D_MODEL)

    tail_small, dh1, dmix, hn2, a, df, dh2, gw_out, gw_proj, gw_gate = _tail(
        x2, mix, p2, t2, g2, gf.reshape(1, D_MODEL), full_out, g_ff1, full_ff2, full_gate, full_proj)
    gw_out, gw_gate = (g.reshape(N_DEV, D_MODEL // N_DEV, D_MODEL) for g in (gw_out, gw_gate))

    few_names = ("w_out", "w_ple_proj", "w_ple_gate")
    ff1_names = ("w_ff1",) + few_names
    gw_ff1 = _wgrad(hn2, df, D_MODEL, 2 * D_FF // N_DEV, "wgrad_ff1", peer_cols=D_FF // N_DEV)
    ff1_swap = _swap_start([gw_ff1, gw_out, gw_proj, gw_gate], [tail_small], "ff1")
    gw_ff2 = _wgrad(a, dh2, 1024, D_MODEL, "wgrad_ff2_lo", after=ff1_swap[0][4], tiles=(0, 2))
    ff1_sums, ff1_small = _swap_sums(ff1_swap, gw_ff2, "ff1")
    ff1_chips = _chips_start(ff1_sums, ff1_small, "ff1")
    gw_ff2 = _wgrad(a, dh2, 1024, D_MODEL, "wgrad_ff2_hi", after=ff1_chips[0][4], tiles=(2, 2), into=gw_ff2)
    ff2_swap = _swap_start([gw_ff2.reshape(N_DEV, D_FF // N_DEV, D_MODEL)], [], "ff2")
    dzm, dkv, d_gv, d_wsp, d_bsp, d_attn = _mixer_bwd(
        zuv, qkv, dmix, saved, gv, w_sp[0], b_col, bucket, after=ff2_swap[0][4])
    ff2_sums, _ = _swap_sums(ff2_swap, dzm, "ff2")
    ff2_chips = _chips_start(ff2_sums, [], "ff2")
    gw_in = _wgrad_in(dzm, dkv, hn1, after=ff2_chips[0][4]).reshape(N_DEV, D_IN // N_DEV, D_MODEL)
    in_swap = _swap_start([gw_in], [d_gv, d_wsp, d_bsp, d_attn], "in")
    dx, d_g1 = _in_bwd(x2, dh1, dzm, dkv, g1, full_in, after=in_swap[0][4])
    grad_x = dx.reshape(x.shape)
    in_sums, in_small = _swap_sums(in_swap, dx, "in", small_dtypes=[F32, BF, F32, F32])
    in_chips = _chips_start(in_sums, in_small, "in", everyone=[d_g1])

    grads, deltas, new_m, new_v = {}, {}, {}, {}
    after, small_own, small_recv = in_chips[0][4], [], []
    for chips, sums, names, tag in ((ff1_chips, ff1_sums, ff1_names, "ff1"), (ff2_chips, ff2_sums, ("w_ff2",), "ff2"),
                                    (in_chips, in_sums, ("w_in",), "in")):
        sent, recvs = _chips_wait(chips, after, tag)
        small_own += sent[len(names):]
        small_recv += recvs[len(names):]
        tiled = [n for n in names if n not in few_names]
        done = [_final_adamw(part, recv, shard(n), shard("m_" + n), shard("v_" + n), "adamw_" + n)
                for n, (part, _), recv in zip(tiled, sums, recvs)]
        few = names[len(tiled):]
        if few:
            done += zip(*_final_adamw_few(
                [ps[0] for ps in sums[len(tiled):]], recvs[len(tiled):len(names)], [shard(n) for n in few],
                [shard("m_" + n) for n in few], [shard("v_" + n) for n in few], "adamw_few"))
        for n, results in zip(names, done):
            for dst, arr in zip((grads, deltas, new_m, new_v), results):
                dst[n] = (arr.T if n == "w_in" else arr)[None]
            after = results[1]

    views = {"w_spatial": (GROUPS, BLOCK, BLOCK), "b_spatial": (GROUPS, BLOCK), "final_gain": (1, D_MODEL)}
    def view(name, base):
        return args[name].T if base == "rel_bias_table" else args[name].reshape(views.get(base, args[name].shape))

    small_in = [[view(pre + n, n) for n in SMALL] for pre in ("", "m_", "v_")]
    loss, *small_out = _adamw_small(small_own[:-1], small_recv[:-1], small_own[-1], small_recv[-1], *small_in, after=after)
    for dst, arrays in zip((grads, deltas, new_m, new_v), small_out):
        for n, arr in zip(SMALL, arrays):
            dst[n] = arr.T if n == "rel_bias_table" else arr.reshape(args[n].shape)
    loss = loss[0, 0]

    order = ("norm1_gain", "w_in", "gmlp_v_gain", "w_spatial", "b_spatial", "attn_sinks", "rel_bias_table", "w_out",
             "norm2_gain", "w_ff1", "w_ff2", "w_ple_proj", "w_ple_gate", "final_gain")
    return (loss, grad_x, *[grads[n] for n in order], *[deltas[n] for n in order],
            *[new_m[n] for n in order], *[new_v[n] for n in order])
```

```python
import math

import numpy as np
import jax
import jax.numpy as jnp
from jax import lax
from jax.experimental import pallas as pl
from jax.experimental.pallas import tpu as pltpu

F32 = jnp.float32
BF = jnp.bfloat16
MESH = pl.DeviceIdType.MESH
N_DEV = 8

D_MODEL = 1024
PLE_DIM = 256
D_GMLP = 512
GROUPS = 4
GDIM = 128
BLOCK = 128
D_ATTN = 512
HEAD_DIM = 64
N_Q = 8
Q_PER_KV = 4
N_KV = N_Q // Q_PER_KV
ROWS4 = Q_PER_KV * BLOCK
D_KV = 128
D_FF = 4096
D_IN = 1792
D_MAIN = 2 * D_GMLP + D_ATTN
REL_BUCKETS = 32
EPS = 1e-6
NEG_INF = -1e30
SCALE = HEAD_DIM ** -0.5
GELU_C = math.sqrt(2.0 / math.pi)
GELU_A = 0.044715

ADAM_LR = 0.001
ADAM_B1 = 0.9
ADAM_B2 = 0.999
ADAM_EPS = 1e-08
ADAM_WD = 0.01
ADAM_STEP = 10

V7X_VMEM_LIMIT = 61440 * 1024
TOK_TILE = 256
IO_TOK_TILE = 512


def _call(body, after=None, **kw):
    if after is None:
        return pl.pallas_call(body, **kw)
    n_in = len(kw["in_specs"])

    def ordered(*refs):
        body(*refs[:n_in], *refs[n_in + 1:])

    kw["in_specs"] = list(kw["in_specs"]) + [pl.BlockSpec(memory_space=pl.ANY)]
    fn = pl.pallas_call(ordered, **kw)
    return lambda *operands: fn(*operands, after)


def _params(sem=None):
    if sem is None:
        return pltpu.CompilerParams(vmem_limit_bytes=V7X_VMEM_LIMIT)
    return pltpu.CompilerParams(dimension_semantics=sem, vmem_limit_bytes=V7X_VMEM_LIMIT)


def _nn(a, b):
    return jnp.dot(a, b, preferred_element_type=F32)


def _nt(a, b):
    return lax.dot_general(a, b, (((1,), (1,)), ((), ())), preferred_element_type=F32)


def _tn(a, b):
    return lax.dot_general(a, b, (((0,), (0,)), ((), ())), preferred_element_type=F32)


def _gelu_tanh(x):
    return jnp.tanh(GELU_C * (x + GELU_A * (x * x * x)))


def _gelu(x, t):
    return x * (0.5 * (1.0 + t))


def _gelu_and_grad(x, t):
    cdf = 0.5 * (1.0 + t)
    return x * cdf, cdf + 0.5 * x * (1.0 - t * t) * (GELU_C * (1.0 + 3.0 * GELU_A * (x * x)))


def _rms_scale(x):
    return lax.rsqrt(jnp.mean(x * x, axis=-1, keepdims=True) + EPS)


def _rms_bwd(dxn, x, r):
    return r * dxn - x * ((r * r * r) * jnp.mean(dxn * x, axis=-1, keepdims=True))


def _bucket_table():
    a = np.arange(BLOCK)[:, None]
    j = np.arange(2 * BLOCK)[None, :]
    n = BLOCK + a - j
    valid = (n >= 0) & (n < BLOCK)
    nc = np.maximum(n, 0)
    max_exact = REL_BUCKETS // 2
    nf = np.maximum(nc, 1).astype(np.float32)
    large = max_exact + (
        np.log(nf / np.float32(max_exact)) / np.float32(math.log(BLOCK / max_exact)) * np.float32(REL_BUCKETS - max_exact)
    ).astype(np.int32)
    large = np.minimum(large, REL_BUCKETS - 1)
    bucket = np.where(nc < max_exact, nc, large)
    return np.where(valid, bucket, -1).astype(np.int32)


def _in_proj(x, g1, w_in_t):
    s = x.shape[0]
    tm = min(IO_TOK_TILE, s)

    def body(x_ref, g_ref, w_ref, zuv_ref, qkv_ref, hn_ref):
        xv = x_ref[...]
        hn = ((xv * _rms_scale(xv)) * g_ref[...]).astype(BF)
        hn_ref[...] = hn
        z = _nt(hn, w_ref[...])
        zuv_ref[...] = z[:, : 2 * D_GMLP]
        qkv_ref[...] = z[:, 2 * D_GMLP:].astype(BF)

    return _call(
        body,
        name="in_proj",
        grid=(s // tm,),
        in_specs=[
            pl.BlockSpec((tm, D_MODEL), lambda i: (i, 0)),
            pl.BlockSpec((1, D_MODEL), lambda i: (0, 0)),
            pl.BlockSpec((D_IN, D_MODEL), lambda i: (0, 0)),
        ],
        out_specs=[
            pl.BlockSpec((tm, 2 * D_GMLP), lambda i: (i, 0)),
            pl.BlockSpec((tm, D_ATTN + 2 * D_KV), lambda i: (i, 0)),
            pl.BlockSpec((tm, D_MODEL), lambda i: (i, 0)),
        ],
        out_shape=[
            jax.ShapeDtypeStruct((s, 2 * D_GMLP), F32),
            jax.ShapeDtypeStruct((s, D_ATTN + 2 * D_KV), BF),
            jax.ShapeDtypeStruct((s, D_MODEL), BF),
        ],
        compiler_params=_params(("arbitrary",)),
    )(x, g1, w_in_t)


def _head_rows(h):
    kh, g = divmod(h, Q_PER_KV)
    return kh, slice(g * BLOCK, (g + 1) * BLOCK)


def _build_bias(bias_ref, bucket_ref, table_ref):
    bucket = bucket_ref[...]
    for h in range(N_Q):
        acc = jnp.zeros((BLOCK, 2 * BLOCK), F32)
        for b in range(REL_BUCKETS):
            acc = jnp.where(bucket == b, table_ref[h, b], acc)
        kh, rows = _head_rows(h)
        bias_ref[kh, rows, :] = acc


def _window_masks(i):
    row = lax.broadcasted_iota(jnp.int32, (ROWS4, BLOCK), 0) & (BLOCK - 1)
    col = lax.broadcasted_iota(jnp.int32, (ROWS4, BLOCK), 1)
    return (col > row) & (i > 0), col <= row


def _stack_heads(ref, kh, offset):
    first = offset + kh * Q_PER_KV * HEAD_DIM
    return jnp.concatenate(
        [ref[:, first + g * HEAD_DIM: first + (g + 1) * HEAD_DIM].astype(BF) for g in range(Q_PER_KV)], axis=0)


def _stack_sinks(sink_ref, kh):
    return jnp.concatenate([jnp.full((BLOCK, 1), sink_ref[0, kh * Q_PER_KV + g], F32) for g in range(Q_PER_KV)], axis=0)


def _tril_bf16(w_ref, g):
    row = lax.broadcasted_iota(jnp.int32, (BLOCK, BLOCK), 0)
    col = lax.broadcasted_iota(jnp.int32, (BLOCK, BLOCK), 1)
    return jnp.where(col <= row, w_ref[g], 0.0).astype(BF)


def _attn_probs(q_h, k_prev, k_cur, bias_h, sink, valid_prev, valid_cur):
    l_prev = jnp.where(valid_prev, _nt(q_h, k_prev) * SCALE + bias_h[:, :BLOCK], NEG_INF)
    l_cur = jnp.where(valid_cur, _nt(q_h, k_cur) * SCALE + bias_h[:, BLOCK:], NEG_INF)
    m = jnp.maximum(jnp.maximum(jnp.max(l_prev, axis=-1, keepdims=True), jnp.max(l_cur, axis=-1, keepdims=True)), sink)
    e_prev = jnp.exp(l_prev - m)
    e_cur = jnp.exp(l_cur - m)
    e_sink = jnp.exp(sink - m)
    denom = jnp.sum(e_prev, axis=-1, keepdims=True) + jnp.sum(e_cur, axis=-1, keepdims=True) + e_sink
    return e_prev / denom, e_cur / denom, e_sink / denom


def _mixer_specs(nb):
    cl = lambda i: jnp.minimum(i, nb - 1)
    return [
        pl.BlockSpec((BLOCK, 2 * D_GMLP), lambda i: (cl(i), 0)),
        pl.BlockSpec((BLOCK, D_ATTN), lambda i: (cl(i), 0)),
        pl.BlockSpec((BLOCK, 2 * D_KV), lambda i: (cl(i), D_ATTN // (2 * D_KV))),
        pl.BlockSpec((BLOCK, 2 * D_KV), lambda i: (jnp.maximum(cl(i) - 1, 0), D_ATTN // (2 * D_KV))),
        pl.BlockSpec((1, D_GMLP), lambda i: (0, 0)),
        pl.BlockSpec((GROUPS, BLOCK, BLOCK), lambda i: (0, 0, 0)),
        pl.BlockSpec((GROUPS, BLOCK, 1), lambda i: (0, 0, 0)),
        pl.BlockSpec(memory_space=pltpu.SMEM),
        pl.BlockSpec(memory_space=pltpu.SMEM),
        pl.BlockSpec((BLOCK, 2 * BLOCK), lambda i: (0, 0)),
    ]


def _mixer_fwd(zuv, qkv, gv, w_sp, b_sp, sinks, table, bucket, after=None):
    s = zuv.shape[0]
    nb = s // BLOCK

    def body(zuv_ref, q_ref, kvc_ref, kvp_ref, gv_ref, w_ref, b_ref, sink_ref, table_ref, bucket_ref,
             mix_ref, tanh_ref, prob_ref, psink_ref, bias_ref):
        i = pl.program_id(0)

        @pl.when(i == 0)
        def _():
            _build_bias(bias_ref, bucket_ref, table_ref)

        t = _gelu_tanh(zuv_ref[...])
        tanh_ref[...] = t
        u = _gelu(zuv_ref[:, :D_GMLP], t[:, :D_GMLP])
        vg = _gelu(zuv_ref[:, D_GMLP:], t[:, D_GMLP:])
        for g in range(GROUPS):
            sl = slice(g * GDIM, (g + 1) * GDIM)
            vg_g = vg[:, sl]
            vn = ((vg_g * _rms_scale(vg_g)) * gv_ref[:, sl]).astype(BF)
            sv = _nn(_tril_bf16(w_ref, g), vn) + b_ref[g]
            mix_ref[:, sl] = (u[:, sl] * sv).astype(BF)

        valid_prev, valid_cur = _window_masks(i)
        for kh in range(N_KV):
            ksl = slice(kh * HEAD_DIM, (kh + 1) * HEAD_DIM)
            vsl = slice(D_KV + kh * HEAD_DIM, D_KV + (kh + 1) * HEAD_DIM)
            q4 = _stack_heads(q_ref, kh, 0)
            p_prev, p_cur, p_sink = _attn_probs(
                q4, kvp_ref[:, ksl], kvc_ref[:, ksl], bias_ref[kh], _stack_sinks(sink_ref, kh), valid_prev, valid_cur)
            prob_ref[0, kh, :, :BLOCK] = p_prev
            prob_ref[0, kh, :, BLOCK:] = p_cur
            psink_ref[0, kh] = jnp.broadcast_to(p_sink, (ROWS4, LANES))
            o4 = _nn(p_prev.astype(BF), kvp_ref[:, vsl]) + _nn(p_cur.astype(BF), kvc_ref[:, vsl])
            for g in range(Q_PER_KV):
                first = D_GMLP + (kh * Q_PER_KV + g) * HEAD_DIM
                mix_ref[:, first:first + HEAD_DIM] = o4[g * BLOCK:(g + 1) * BLOCK].astype(BF)

    return _call(
        body,
        name="mixer_fwd",
        after=after,
        grid=(nb,),
        in_specs=_mixer_specs(nb),
        out_specs=[
            pl.BlockSpec((BLOCK, D_MODEL), lambda i: (i, 0)),
            pl.BlockSpec((BLOCK, 2 * D_GMLP), lambda i: (i, 0)),
            pl.BlockSpec((1, N_KV, ROWS4, 2 * BLOCK), lambda i: (i, 0, 0, 0)),
            pl.BlockSpec((1, N_KV, ROWS4, LANES), lambda i: (i, 0, 0, 0)),
        ],
        out_shape=[
            jax.ShapeDtypeStruct((s, D_MODEL), BF),
            jax.ShapeDtypeStruct((s, 2 * D_GMLP), F32),
            jax.ShapeDtypeStruct((nb, N_KV, ROWS4, 2 * BLOCK), F32),
            jax.ShapeDtypeStruct((nb, N_KV, ROWS4, LANES), F32),
        ],
        scratch_shapes=[pltpu.VMEM((N_KV, ROWS4, 2 * BLOCK), F32)],
        compiler_params=_params(("arbitrary",)),
    )(zuv, qkv, qkv, qkv, gv, w_sp, b_sp, sinks, table, bucket)


def _tail(x, mix, p, t, g2, gf, w_out, w_ff1, w_ff2, w_gate, w_proj):
    s = x.shape[0]
    tm = min(TOK_TILE, s)
    n_ff = w_ff1.shape[0]
    fc = D_FF // n_ff
    pc = D_MODEL // N_DEV

    def body(x_ref, mix_ref, p_ref, t_ref, g2_ref, gf_ref, wo_ref, w1_ref, w2_ref, wg_ref, wp_ref,
             small_ref, dh1_ref, dmix_ref, hn2_ref, a_ref, df_ref, dh2_ref, gwo_ref, gwp_ref, gwg_ref, f_ref):
        i = pl.program_id(0)

        @pl.when(i == 0)
        def _():
            small_ref[...] = jnp.zeros_like(small_ref)
            gwo_ref[...] = jnp.zeros_like(gwo_ref)
            gwp_ref[...] = jnp.zeros_like(gwp_ref)
            gwg_ref[...] = jnp.zeros_like(gwg_ref)

        h1 = x_ref[...] + _nn(mix_ref[...], wo_ref[...])
        dh1_ref[...] = h1
        r2 = _rms_scale(h1)
        hn2_ref[...] = ((h1 * r2) * g2_ref[...]).astype(BF)
        h2 = h1
        for c in range(n_ff):
            f = _nn(hn2_ref[...], w1_ref[c])
            f_ref[:, c * fc:(c + 1) * fc] = f
            a = jnp.square(jnp.maximum(f, 0.0)).astype(BF)
            a_ref[:, c * fc:(c + 1) * fc] = a
            h2 = h2 + _nn(a, w2_ref[c * fc:(c + 1) * fc, :])
        h2b = h2.astype(BF)
        gate = jax.nn.sigmoid(_nn(h2b, wg_ref[...]))
        pb = p_ref[...].astype(BF)
        pp = _nn(pb, wp_ref[...])
        h3 = h2 + gate * pp
        rf = _rms_scale(h3)
        gf_v = gf_ref[...]
        err = (h3 * rf) * gf_v - t_ref[...]
        small_ref[2:3, :] += jnp.sum(jnp.sum(err * err, axis=-1, keepdims=True), axis=0, keepdims=True) * (0.5 / D_MODEL)

        dy = err * (1.0 / D_MODEL)
        small_ref[1:2, :] += jnp.sum(dy * (h3 * rf), axis=0, keepdims=True)
        dh3 = _rms_bwd(dy * gf_v, h3, rf)
        gw_proj = _tn(pb, (dh3 * gate).astype(BF))
        for q in range(N_DEV):
            gwp_ref[q] += gw_proj[:, q * pc:(q + 1) * pc]
        dgl = ((dh3 * pp) * (gate * (1.0 - gate))).astype(BF)
        gwg_ref[...] += _tn(h2b, dgl)
        dh2 = dh3 + _nt(dgl, wg_ref[...])
        dmix_ref[...] = dh2
        dh2_ref[...] = dh2.astype(BF)
        dhn2 = jnp.zeros((tm, D_MODEL), F32)
        for c in range(n_ff):
            da = _nt(dh2_ref[...], w2_ref[c * fc:(c + 1) * fc, :])
            df = (da * (2.0 * jnp.maximum(f_ref[:, c * fc:(c + 1) * fc], 0.0))).astype(BF)
            df_ref[:, c * fc:(c + 1) * fc] = df
            dhn2 = dhn2 + _nt(df, w1_ref[c])
        h1 = dh1_ref[...]
        small_ref[0:1, :] += jnp.sum(dhn2 * (h1 * r2), axis=0, keepdims=True)
        dh1 = dmix_ref[...] + _rms_bwd(dhn2 * g2_ref[...], h1, r2)
        dh1_ref[...] = dh1
        dh1b = dh1.astype(BF)
        gwo_ref[...] += _tn(mix_ref[...], dh1b)
        dmix_ref[...] = _nt(dh1b, wo_ref[...])

    tile = lambda cols: pl.BlockSpec((tm, cols), lambda i: (i, 0))
    whole = lambda shape: pl.BlockSpec(shape, lambda i: (0,) * len(shape), pipeline_mode=pl.Buffered(1))
    total = lambda shape: pl.BlockSpec(shape, lambda i: (0,) * len(shape))
    row = pl.BlockSpec((1, D_MODEL), lambda i: (0, 0))
    act = lambda cols, dt: jax.ShapeDtypeStruct((s, cols), dt)
    gw_shapes = [(D_MODEL, D_MODEL), (N_DEV, PLE_DIM, pc), (D_MODEL, D_MODEL)]
    return _call(
        body,
        name="tail",
        grid=(s // tm,),
        in_specs=[tile(D_MODEL), tile(D_MODEL), tile(PLE_DIM), tile(D_MODEL), row, row,
                  whole(w_out.shape), whole(w_ff1.shape), whole(w_ff2.shape), whole(w_gate.shape), whole(w_proj.shape)],
        out_specs=[total((8, D_MODEL)), tile(D_MODEL), tile(D_MODEL), tile(D_MODEL), tile(D_FF), tile(D_FF), tile(D_MODEL),
                   *[total(shape) for shape in gw_shapes]],
        out_shape=[jax.ShapeDtypeStruct((8, D_MODEL), F32),
                   act(D_MODEL, F32), act(D_MODEL, F32), act(D_MODEL, BF), act(D_FF, BF), act(D_FF, BF), act(D_MODEL, BF),
                   *[jax.ShapeDtypeStruct(shape, F32) for shape in gw_shapes]],
        scratch_shapes=[pltpu.VMEM((tm, D_FF), F32)],
        compiler_params=_params(("arbitrary",)),
    )(x, mix, p, t, g2, gf, w_out, w_ff1, w_ff2, w_gate, w_proj)


def _mixer_bwd(zuv, qkv, dmix, saved, gv, w_sp, b_sp, bucket, after=None):
    s = zuv.shape[0]
    nb = s // BLOCK

    def body(zuv_ref, q_ref, kvc_ref, kvp_ref, gv_ref, w_ref, b_ref, bucket_ref, dmix_ref, tanh_ref, prob_ref, psink_ref,
             dzm_ref, dkv_ref, dgv_ref, dw_ref, db_ref, dattn_ref,
             dbias_ref, carry_ref, dsink_acc, db_acc):
        i = pl.program_id(0)

        @pl.when(i == 0)
        def _():
            dbias_ref[...] = jnp.zeros_like(dbias_ref)
            carry_ref[...] = jnp.zeros_like(carry_ref)
            dsink_acc[...] = jnp.zeros_like(dsink_acc)
            dgv_ref[...] = jnp.zeros_like(dgv_ref)
            dw_ref[...] = jnp.zeros_like(dw_ref)
            db_acc[...] = jnp.zeros_like(db_acc)

        @pl.when(i < nb)
        def _():
            u, du_dz = _gelu_and_grad(zuv_ref[:, :D_GMLP], tanh_ref[:, :D_GMLP])
            vg, dvg_dz = _gelu_and_grad(zuv_ref[:, D_GMLP:], tanh_ref[:, D_GMLP:])
            for g in range(GROUPS):
                sl = slice(g * GDIM, (g + 1) * GDIM)
                vg_g = vg[:, sl]
                rg = _rms_scale(vg_g)
                vhat = vg_g * rg
                gain = gv_ref[:, sl]
                vn = (vhat * gain).astype(BF)
                w_g = _tril_bf16(w_ref, g)
                sv = _nn(w_g, vn) + b_ref[g]
                dmix_a = dmix_ref[:, sl]
                dsv = dmix_a * u[:, sl]
                dsvb = dsv.astype(BF)
                db_acc[g] += jnp.sum(dsv, axis=-1, keepdims=True)
                dw_ref[g] += _nt(dsvb, vn)
                dvn = _tn(w_g, dsvb)
                dgv_ref[:, sl] += jnp.sum(dvn * vhat, axis=0, keepdims=True)
                dvg = _rms_bwd(dvn * gain, vg_g, rg)
                dzm_ref[:, sl] = ((dmix_a * sv) * du_dz[:, sl]).astype(BF)
                dzm_ref[:, D_GMLP + g * GDIM: D_GMLP + (g + 1) * GDIM] = (dvg * dvg_dz[:, sl]).astype(BF)

            for kh in range(N_KV):
                ksl = slice(kh * HEAD_DIM, (kh + 1) * HEAD_DIM)
                vsl = slice(D_KV + kh * HEAD_DIM, D_KV + (kh + 1) * HEAD_DIM)
                k_prev, k_cur = kvp_ref[:, ksl], kvc_ref[:, ksl]
                v_prev, v_cur = kvp_ref[:, vsl], kvc_ref[:, vsl]
                q4 = _stack_heads(q_ref, kh, 0)
                p_prev, p_cur, p_sink = prob_ref[0, kh, :, :BLOCK], prob_ref[0, kh, :, BLOCK:], psink_ref[0, kh, :, 0:1]
                do4 = _stack_heads(dmix_ref, kh, D_GMLP)
                dp_prev = _nt(do4, v_prev)
                dp_cur = _nt(do4, v_cur)
                delta = jnp.sum(p_prev * dp_prev, axis=-1, keepdims=True) + jnp.sum(p_cur * dp_cur, axis=-1, keepdims=True)
                ds_prev = p_prev * (dp_prev - delta)
                ds_cur = p_cur * (dp_cur - delta)
                dsink_acc[kh] -= p_sink * delta
                dbias_ref[kh, :, :BLOCK] += ds_prev
                dbias_ref[kh, :, BLOCK:] += ds_cur
                dsb_prev = ds_prev.astype(BF)
                dsb_cur = ds_cur.astype(BF)
                dq4 = (_nn(dsb_prev, k_prev) + _nn(dsb_cur, k_cur)) * SCALE
                for g in range(Q_PER_KV):
                    first = 2 * D_GMLP + (kh * Q_PER_KV + g) * HEAD_DIM
                    dzm_ref[:, first:first + HEAD_DIM] = dq4[g * BLOCK:(g + 1) * BLOCK].astype(BF)
                dkv_ref[:, ksl] = (carry_ref[:, ksl] + _tn(dsb_prev, q4) * SCALE).astype(BF)
                dkv_ref[:, vsl] = (carry_ref[:, vsl] + _tn(p_prev.astype(BF), do4)).astype(BF)
                carry_ref[:, ksl] = _tn(dsb_cur, q4) * SCALE
                carry_ref[:, vsl] = _tn(p_cur.astype(BF), do4)

        @pl.when(i == nb)
        def _():
            dkv_ref[...] = carry_ref[...].astype(BF)
            row = lax.broadcasted_iota(jnp.int32, (BLOCK, BLOCK), 0)
            col = lax.broadcasted_iota(jnp.int32, (BLOCK, BLOCK), 1)
            for g in range(GROUPS):
                dw_ref[g] = jnp.where(col <= row, dw_ref[g], 0.0)
                db_ref[g:g + 1, :] = jnp.sum(jnp.where(col == row, db_acc[g], 0.0), axis=0, keepdims=True)
            bucket = bucket_ref[...]
            for b in range(N_Q, REL_BUCKETS):
                dattn_ref[N_Q, b] = 0.0
            for h in range(N_Q):
                kh, rows = _head_rows(h)
                dattn_ref[N_Q, h] = jnp.sum(dsink_acc[kh, rows, :])
                dbh = dbias_ref[kh, rows, :]
                for b in range(REL_BUCKETS):
                    dattn_ref[h, b] = jnp.sum(jnp.where(bucket == b, dbh, 0.0))

    cl = lambda i: jnp.minimum(i, nb - 1)
    const = lambda shape: pl.BlockSpec(shape, lambda i: (0,) * len(shape))
    return _call(
        body,
        name="mixer_bwd",
        after=after,
        grid=(nb + 1,),
        in_specs=_mixer_specs(nb)[:7] + [
            const((BLOCK, 2 * BLOCK)),
            pl.BlockSpec((BLOCK, D_MODEL), lambda i: (cl(i), 0)),
            pl.BlockSpec((BLOCK, 2 * D_GMLP), lambda i: (cl(i), 0)),
            pl.BlockSpec((1, N_KV, ROWS4, 2 * BLOCK), lambda i: (cl(i), 0, 0, 0)),
            pl.BlockSpec((1, N_KV, ROWS4, LANES), lambda i: (cl(i), 0, 0, 0)),
        ],
        out_specs=[
            pl.BlockSpec((BLOCK, D_MAIN), lambda i: (cl(i), 0)),
            pl.BlockSpec((BLOCK, 2 * D_KV), lambda i: (jnp.maximum(i - 1, 0), 0)),
            const((1, D_GMLP)),
            const((GROUPS, BLOCK, BLOCK)),
            const((GROUPS, BLOCK)),
            pl.BlockSpec(memory_space=pltpu.SMEM),
        ],
        out_shape=[
            jax.ShapeDtypeStruct((s, D_MAIN), BF),
            jax.ShapeDtypeStruct((s, 2 * D_KV), BF),
            jax.ShapeDtypeStruct((1, D_GMLP), F32),
            jax.ShapeDtypeStruct((GROUPS, BLOCK, BLOCK), F32),
            jax.ShapeDtypeStruct((GROUPS, BLOCK), F32),
            jax.ShapeDtypeStruct((N_Q + 1, REL_BUCKETS), F32),
        ],
        scratch_shapes=[
            pltpu.VMEM((N_KV, ROWS4, 2 * BLOCK), F32),
            pltpu.VMEM((BLOCK, 2 * D_KV), F32),
            pltpu.VMEM((N_KV, ROWS4, 1), F32),
            pltpu.VMEM((GROUPS, BLOCK, 1), F32),
        ],
        compiler_params=_params(("arbitrary",)),
    )(zuv, qkv, qkv, qkv, gv, w_sp, b_sp, bucket, dmix, *saved)


def _in_bwd(x, dh1, dzm, dkv, g1, w_in_t, after=None):
    s = x.shape[0]
    tm = min(IO_TOK_TILE, s)

    def body(x_ref, dh1_ref, dzm_ref, dkv_ref, g_ref, w_ref, dx_ref, dg_ref):
        @pl.when(pl.program_id(0) == 0)
        def _():
            dg_ref[...] = jnp.zeros_like(dg_ref)

        dhn = _nn(dzm_ref[...], w_ref[:D_MAIN, :]) + _nn(dkv_ref[...], w_ref[D_MAIN:, :])
        xv = x_ref[...]
        r = _rms_scale(xv)
        dg_ref[...] += jnp.sum(dhn * (xv * r), axis=0, keepdims=True)
        dx_ref[...] = dh1_ref[...] + _rms_bwd(dhn * g_ref[...], xv, r)

    tile = lambda cols: pl.BlockSpec((tm, cols), lambda i: (i, 0))
    row = pl.BlockSpec((1, D_MODEL), lambda i: (0, 0))
    return _call(
        body,
        name="in_bwd",
        after=after,
        grid=(s // tm,),
        in_specs=[tile(D_MODEL), tile(D_MODEL), tile(D_MAIN), tile(2 * D_KV), row, pl.BlockSpec((D_IN, D_MODEL), lambda i: (0, 0))],
        out_specs=[tile(D_MODEL), row],
        out_shape=[jax.ShapeDtypeStruct((s, D_MODEL), F32), jax.ShapeDtypeStruct((1, D_MODEL), F32)],
        compiler_params=_params(("arbitrary",)),
    )(x, dh1, dzm, dkv, g1, w_in_t)


def _wgrad_in(dzm, dkv, hn, after=None):
    s = hn.shape[0]
    tm = 2 * D_KV
    n_main = D_MAIN // tm

    def body(dzm_ref, dkv_ref, hn_ref, o_ref):
        i = pl.program_id(0)

        @pl.when(i < n_main)
        def _():
            o_ref[...] = _tn(dzm_ref[...], hn_ref[...])

        @pl.when(i == n_main)
        def _():
            o_ref[...] = _tn(dkv_ref[...], hn_ref[...])

    return _call(
        body,
        name="wgrad_in",
        after=after,
        grid=(n_main + 1,),
        in_specs=[
            pl.BlockSpec((s, tm), lambda i: (0, jnp.minimum(i, n_main - 1))),
            pl.BlockSpec((s, tm), lambda i: (0, 0)),
            pl.BlockSpec((s, D_MODEL), lambda i: (0, 0)),
        ],
        out_specs=pl.BlockSpec((tm, D_MODEL), lambda i: (i, 0)),
        out_shape=jax.ShapeDtypeStruct((D_IN, D_MODEL), F32),
        compiler_params=_params(("arbitrary",)),
    )(dzm, dkv, hn)


def _wgrad(a, b, tm, tn, name, peer_cols=0, after=None):
    s, m = a.shape
    n = b.shape[1]

    def body(a_ref, b_ref, o_ref, at_ref):
        @pl.when(pl.program_id(1) == 0)
        def _():
            at_ref[...] = a_ref[...].astype(BF).T

        r = _nn(at_ref[...], b_ref[...])
        if peer_cols:
            for q in range(tn // peer_cols):
                o_ref[q] = r[:, q * peer_cols:(q + 1) * peer_cols]
        else:
            o_ref[...] = r

    if peer_cols:
        out_spec = pl.BlockSpec((tn // peer_cols, tm, peer_cols), lambda i, j: (j, i, 0))
        out_shape = jax.ShapeDtypeStruct((n // peer_cols, m, peer_cols), F32)
    else:
        out_spec = pl.BlockSpec((tm, tn), lambda i, j: (i, j))
        out_shape = jax.ShapeDtypeStruct((m, n), F32)
    return _call(
        body,
        name=name,
        after=after,
        grid=(m // tm, n // tn),
        in_specs=[pl.BlockSpec((s, tm), lambda i, j: (0, i)), pl.BlockSpec((s, tn), lambda i, j: (0, j))],
        out_specs=out_spec,
        out_shape=out_shape,
        scratch_shapes=[pltpu.VMEM((tm, s), BF)],
        compiler_params=_params(("arbitrary", "arbitrary")),
    )(a, b)


def _adamw_math(w, g, m, v):
    m_new = ADAM_B1 * m + (1.0 - ADAM_B1) * g
    v_new = ADAM_B2 * v + (1.0 - ADAM_B2) * jnp.square(g)
    m_hat = m_new / (1.0 - ADAM_B1 ** ADAM_STEP)
    v_hat = v_new / (1.0 - ADAM_B2 ** ADAM_STEP)
    delta = -ADAM_LR * (m_hat / (jnp.sqrt(v_hat) + ADAM_EPS) + ADAM_WD * w)
    return delta, m_new, v_new


def _final_adamw(part, recv, w, m, v, name):
    r, c = w.shape
    tr = min(r, 512)

    def body(p_ref, r_ref, w_ref, m_ref, v_ref, g_ref, d_ref, mo_ref, vo_ref):
        g = p_ref[...]
        for j in range(3):
            g = g + r_ref[j].astype(F32)
        g_ref[...] = g
        d_ref[...], mo_ref[...], vo_ref[...] = _adamw_math(w_ref[...], g, m_ref[...], v_ref[...])

    spec = pl.BlockSpec((tr, c), lambda i: (i, 0))
    return _call(
        body,
        name=name,
        grid=(r // tr,),
        in_specs=[spec, pl.BlockSpec((3, tr, c), lambda i: (0, i, 0)), spec, spec, spec],
        out_specs=[spec] * 4,
        out_shape=[jax.ShapeDtypeStruct((r, c), F32)] * 4,
        compiler_params=_params(("arbitrary",)),
    )(part, recv, w, m, v)


def _rs_sum(gs, lands, blocks, name, smalls, small_lands, small_dtypes):
    n, n_small = len(gs), len(smalls)
    steps = pl.cdiv(gs[0].shape[1], 256)
    assert all(g.shape[1] % steps == 0 for g in gs)

    def body(blk_ref, *refs):
        ins, small_ins, outs = refs[:5 * n], refs[5 * n:5 * n + 2 * n_small], refs[5 * n + 2 * n_small:]
        for k in range(n):
            g0_ref, g1_ref, g2_ref, g3_ref, l_ref = ins[5 * k:5 * k + 5]
            outs[2 * k][...] = g0_ref[0] + l_ref[0]
            for j, gj_ref in enumerate((g1_ref, g2_ref, g3_ref)):
                outs[2 * k + 1][j] = (gj_ref[0] + l_ref[j + 1]).astype(BF)
        for q in range(n_small):
            outs[2 * n + q][...] = (small_ins[q][...] + small_ins[n_small + q][...]).astype(small_dtypes[q])

    def whole(a):
        return pl.BlockSpec(a.shape, lambda i, blk: (0,) * a.ndim)

    in_specs, out_specs, out_shape, operands = [], [], [], []
    for g, land in zip(gs, lands):
        _, r, c = g.shape
        tr = r // steps
        in_specs += [pl.BlockSpec((1, tr, c), lambda i, blk, j=j: (blk[j], i, 0)) for j in range(4)]
        in_specs.append(pl.BlockSpec((4, tr, c), lambda i, blk: (0, i, 0)))
        out_specs += [pl.BlockSpec((tr, c), lambda i, blk: (i, 0)), pl.BlockSpec((3, tr, c), lambda i, blk: (0, i, 0))]
        out_shape += [jax.ShapeDtypeStruct((r, c), F32), jax.ShapeDtypeStruct((3, r, c), BF)]
        operands += [g, g, g, g, land]
    in_specs += [whole(a) for a in (*smalls, *small_lands)]
    out_specs += [whole(a) for a in smalls]
    out_shape += [jax.ShapeDtypeStruct(a.shape, dt) for a, dt in zip(smalls, small_dtypes)]
    outs = _call(
        body,
        name=name,
        grid_spec=pltpu.PrefetchScalarGridSpec(num_scalar_prefetch=1, grid=(steps,), in_specs=in_specs, out_specs=out_specs),
        out_shape=out_shape,
        compiler_params=_params(("arbitrary",)),
    )(blocks, *operands, *smalls, *small_lands)
    return [(outs[2 * k], outs[2 * k + 1]) for k in range(n)], list(outs[2 * n:])


def _final_adamw_few(parts, recvs, ws, ms, vs, name):
    n = len(ws)

    def body(*refs):
        ins, outs = refs[:5 * n], refs[5 * n:]
        for k in range(n):
            p_ref, r_ref, w_ref, m_ref, v_ref = (ins[q * n + k] for q in range(5))
            g = p_ref[...]
            for j in range(3):
                g = g + r_ref[j].astype(F32)
            outs[k][...] = g
            outs[n + k][...], outs[2 * n + k][...], outs[3 * n + k][...] = _adamw_math(w_ref[...], g, m_ref[...], v_ref[...])

    outs = _call(
        body,
        name=name,
        out_shape=[jax.ShapeDtypeStruct(w.shape, F32) for w in ws] * 4,
        compiler_params=_params(),
    )(*parts, *recvs, *ws, *ms, *vs)
    return outs[:n], outs[n:2 * n], outs[2 * n:3 * n], outs[3 * n:]


def _adamw_small(own, recv, g1_own, g1_recv, weights, moms, vels, after):
    n_w = len(weights)

    def body(*refs):
        own_refs, recv_refs, g1_own_ref, g1_recv_ref = refs[:5], refs[5:10], refs[10], refs[11]
        w_refs, m_refs, v_refs = refs[12:12 + n_w], refs[12 + n_w:12 + 2 * n_w], refs[12 + 2 * n_w:12 + 3 * n_w]
        outs = refs[12 + 3 * n_w:]
        loss_ref, g_refs, d_refs = outs[0], outs[1:1 + n_w], outs[1 + n_w:1 + 2 * n_w]
        mo_refs, vo_refs = outs[1 + 2 * n_w:1 + 3 * n_w], outs[1 + 3 * n_w:]
        x, y, c = lax.axis_index("x"), lax.axis_index("y"), lax.axis_index("c")

        def in_place_order(values, my_place):
            acc = None
            for place in range(len(values)):
                r = place ^ my_place
                term = values[-1]
                for q in range(len(values) - 2, -1, -1):
                    term = jnp.where(r == q, values[q], term)
                acc = term if acc is None else acc + term
            return acc

        def total(k, *index):
            index = index or (slice(None),) * (len(own_refs[k].shape) - 1)
            across = [own_refs[k][(0, *index)], recv_refs[k][(1, *index)], recv_refs[k][(0, *index)], recv_refs[k][(2, *index)]]
            return in_place_order([v.astype(F32) for v in across], 2 * x + y)

        g1 = in_place_order([g1_own_ref[0]] + [g1_recv_ref[j] for j in range(N_DEV - 1)], 4 * x + 2 * y + c)
        grads = [
            g1, total(1), total(2), total(3),
            total(4, slice(N_Q, None), slice(0, N_Q)), total(4, slice(0, N_Q), slice(None)),
            total(0, slice(0, 1), slice(None)), total(0, slice(1, 2), slice(None))]
        loss_ref[...] = total(0, slice(2, 3), slice(0, 1))
        for k in range(n_w):
            g_refs[k][...] = grads[k]
            d_refs[k][...], mo_refs[k][...], vo_refs[k][...] = _adamw_math(w_refs[k][...], grads[k], m_refs[k][...], v_refs[k][...])

    shapes = [jax.ShapeDtypeStruct(w.shape, F32) for w in weights]
    outs = _call(
        body,
        name="adamw_small",
        after=after,
        in_specs=[pl.BlockSpec(memory_space=pltpu.VMEM)] * (12 + 3 * n_w),
        out_shape=[jax.ShapeDtypeStruct((1, 1), F32)] + shapes * 4,
        compiler_params=_params(),
    )(*own, *recv, g1_own, g1_recv, *weights, *moms, *vels)
    return outs[0], outs[1:1 + n_w], outs[1 + n_w:1 + 2 * n_w], outs[1 + 2 * n_w:1 + 3 * n_w], outs[1 + 3 * n_w:]


def _place():
    x, y, c = lax.axis_index("x"), lax.axis_index("y"), lax.axis_index("c")
    return x, y, c, [(1 - x, y), (x, 1 - y), (1 - x, 1 - y)]


def _dev_index(px, py, pc):
    return 4 * px + 2 * py + pc


HBM_SPEC = pl.BlockSpec(memory_space=pltpu.HBM)
SEM_SPEC = pl.BlockSpec(memory_space=pltpu.SEMAPHORE)
ANY_SPEC = pl.BlockSpec(memory_space=pl.ANY)
DATAFLOW = pltpu.SideEffectType.DATAFLOW_SIDE_EFFECTING


def _hbm(a):
    return pltpu.with_memory_space_constraint(a, pltpu.HBM)


def _prep_weights(shards):
    k_n = len(shards)

    def body(*refs):
        ins, outs, stage, sems = refs[:k_n], refs[k_n:2 * k_n], refs[2 * k_n:3 * k_n], refs[3 * k_n]
        x, y, c, _ = _place()
        copies = []
        for k in range(k_n):
            stage[k][...] = ins[k][...].astype(BF)
            copies.append(pltpu.make_async_copy(stage[k], outs[k].at[_dev_index(x, y, c)], sems.at[k]))
            copies[k].start()
        for cp in copies:
            cp.wait()

    return _call(
        body,
        name="prep_weights",
        in_specs=[pl.BlockSpec(memory_space=pltpu.VMEM)] * k_n,
        out_specs=[ANY_SPEC] * k_n,
        out_shape=[jax.ShapeDtypeStruct((N_DEV,) + sh.shape, BF) for sh in shards],
        scratch_shapes=[pltpu.VMEM(sh.shape, BF) for sh in shards] + [pltpu.SemaphoreType.DMA((k_n,))],
        compiler_params=_params(),
    )(*shards)


def _ag_piece(land_k, block, half, peer, send_sem, recv_sem):
    ref = land_k.at[_dev_index(*block)]
    if half is not None:
        rows = land_k.shape[1] // 2
        ref = ref.at[pl.ds(half * rows, rows)]
    return pltpu.make_async_remote_copy(
        src_ref=ref, dst_ref=ref, send_sem=send_sem, recv_sem=recv_sem, device_id=peer, device_id_type=MESH)


def _ag_plan():
    x, y, c, _ = _place()
    me, sib = (x, y, c), (x, y, 1 - c)
    xn, yn, diag = (1 - x, y, c), (x, 1 - y, c), (1 - x, 1 - y, c)
    return dict(
        relay_halves=[(me, 0, xn), (me, 1, yn)],
        others=[(me, None, sib), (me, 1, xn), (me, 0, yn)],
        relays=[(xn, 0, yn), (yn, 1, xn)],
        near=[(xn, None, sib), (yn, None, sib)],
        far=[(diag, None, sib)],
    )


def _ag_stage(land, stage, send_sems, recv_sems, act):
    copies = _ag_plan()[stage]
    n = len(copies)
    for k in range(len(land)):
        for j, (block, half, peer) in enumerate(copies):
            cp = _ag_piece(land[k], block, half, peer, send_sems.at[n * k + j], recv_sems.at[n * k + j])
            if act == "start":
                cp.start()
            else:
                cp.wait_send()
                cp.wait_recv()


def _sem_shapes(*counts):
    return [pltpu.SemaphoreType.DMA((n,)) for n in counts for _ in range(2)]


def _ag_start(first, rest):
    lands = list(first) + list(rest)
    k_n, k_first = len(lands), len(first)
    k_rest = k_n - k_first

    def body(*refs):
        land = refs[:k_n]
        sems = refs[k_n:k_n + 6]
        token = refs[-1]
        x, y, c, chips = _place()
        targets = [(x, y, 1 - c)] + [(*chip, c) for chip in chips]
        for k in range(k_first):
            for j, to in enumerate(targets):
                _ag_piece(land[k], (x, y, c), None, to, sems[0].at[4 * k + j], sems[1].at[4 * k + j]).start()
        _ag_stage(land[k_first:], "relay_halves", sems[2], sems[3], "start")
        _ag_stage(land[k_first:], "others", sems[4], sems[5], "start")
        token[...] = jnp.zeros_like(token)

    outs = pl.pallas_call(
        body,
        name="ag_start",
        in_specs=[HBM_SPEC] * k_n,
        out_specs=(*[SEM_SPEC] * 6, *[HBM_SPEC] * k_n, pl.BlockSpec(memory_space=pltpu.VMEM)),
        out_shape=(*_sem_shapes(4 * k_first, 2 * k_rest, 3 * k_rest),
                   *[pltpu.HBM(a.shape, a.dtype) for a in lands], jax.ShapeDtypeStruct((8, LANES), F32)),
        input_output_aliases={k: 6 + k for k in range(k_n)},
        compiler_params=pltpu.CompilerParams(has_side_effects=DATAFLOW),
    )(*[_hbm(a) for a in lands])
    flying = list(outs[6:6 + k_n])
    return (outs[0], outs[1], flying[:k_first]), (outs[2:6], flying[k_first:]), outs[-1]


def _ag_split_call(lands, waits, starts, after, name):
    k_n = len(lands)
    plan_sizes = dict(relay_halves=2, others=3, relays=2, near=2, far=1)
    n_in, n_out = 2 * len(waits), 2 * len(starts)

    def body(*refs):
        land = refs[:k_n]
        in_sems = refs[k_n:k_n + n_in]
        out_sems, token = refs[len(refs) - 1 - n_out:len(refs) - 1], refs[-1]
        for w, (stage, _, _) in enumerate(waits):
            _ag_stage(land, stage, in_sems[2 * w], in_sems[2 * w + 1], "wait")
            if w < len(starts):
                _ag_stage(land, starts[w], out_sems[2 * w], out_sems[2 * w + 1], "start")
        token[...] = jnp.zeros_like(token)

    outs = pl.pallas_call(
        body,
        name=name,
        in_specs=[HBM_SPEC] * k_n + [SEM_SPEC] * n_in + [ANY_SPEC],
        out_specs=(*[HBM_SPEC] * k_n, *[SEM_SPEC] * n_out, pl.BlockSpec(memory_space=pltpu.VMEM)),
        out_shape=(*[pltpu.HBM(a.shape, a.dtype) for a in lands], *_sem_shapes(*[plan_sizes[s] * k_n for s in starts]),
                   jax.ShapeDtypeStruct((8, LANES), F32)),
        input_output_aliases={k: k for k in range(k_n)},
        compiler_params=pltpu.CompilerParams(has_side_effects=DATAFLOW),
    )(*lands, *[s for _, a, b in waits for s in (a, b)], after)
    return list(outs[:k_n]), list(outs[k_n:k_n + n_out]), outs[-1]


def _ag_mid(lands, send_sems, recv_sems, after, name):
    k_n = len(lands)

    def body(*refs):
        land = refs[:k_n]
        send1, recv1 = refs[k_n], refs[k_n + 1]
        fwd_send, fwd_recv = refs[-2], refs[-1]
        x, y, c, chips = _place()
        sources = [(x, y, 1 - c)] + [(*chip, c) for chip in chips]
        for k in range(k_n):
            mine = land[k].at[_dev_index(x, y, c)]
            for j, frm in enumerate(sources):
                got = land[k].at[_dev_index(*frm)]
                cp = pltpu.make_async_remote_copy(
                    src_ref=mine, dst_ref=got, send_sem=send1.at[4 * k + j], recv_sem=recv1.at[4 * k + j], device_id=frm, device_id_type=MESH)
                cp.wait_send()
                cp.wait_recv()
                if j >= 1:
                    pltpu.make_async_remote_copy(
                        src_ref=got, dst_ref=got, send_sem=fwd_send.at[3 * k + j - 1], recv_sem=fwd_recv.at[3 * k + j - 1],
                        device_id=(x, y, 1 - c), device_id_type=MESH).start()

    outs = pl.pallas_call(
        body,
        name=name,
        in_specs=[HBM_SPEC] * k_n + [SEM_SPEC, SEM_SPEC, ANY_SPEC],
        out_specs=(*[HBM_SPEC] * k_n, SEM_SPEC, SEM_SPEC),
        out_shape=(*[pltpu.HBM(a.shape, a.dtype) for a in lands], pltpu.SemaphoreType.DMA((3 * k_n,)), pltpu.SemaphoreType.DMA((3 * k_n,))),
        input_output_aliases={k: k for k in range(k_n)},
        compiler_params=pltpu.CompilerParams(has_side_effects=DATAFLOW),
    )(*lands, send_sems, recv_sems, after)
    return list(outs[:k_n]), outs[-2], outs[-1]


def _ag_end(lands, fwd_send, fwd_recv, name):
    k_n = len(lands)

    def body(*refs):
        land = refs[:k_n]
        fsend, frecv = refs[k_n], refs[k_n + 1]
        x, y, c, chips = _place()
        for k in range(k_n):
            for j, chip in enumerate(chips):
                cp = pltpu.make_async_remote_copy(
                    src_ref=land[k].at[_dev_index(*chip, c)], dst_ref=land[k].at[_dev_index(*chip, 1 - c)],
                    send_sem=fsend.at[3 * k + j], recv_sem=frecv.at[3 * k + j], device_id=(x, y, 1 - c), device_id_type=MESH)
                cp.wait_send()
                cp.wait_recv()

    outs = pl.pallas_call(
        body,
        name=name,
        in_specs=[HBM_SPEC] * k_n + [SEM_SPEC, SEM_SPEC],
        out_specs=tuple([HBM_SPEC] * k_n),
        out_shape=tuple(pltpu.HBM(a.shape, a.dtype) for a in lands),
        input_output_aliases={k: k for k in range(k_n)},
        compiler_params=pltpu.CompilerParams(has_side_effects=DATAFLOW),
    )(*lands, fwd_send, fwd_recv)
    return list(outs)


def _chips4():
    x, y, c, others = _place()
    return x, y, c, [(x, y)] + others


def _route_sibling(j):
    x, y, c, chips = _chips4()
    return _dev_index(*chips[j], 1 - c), j, (x, y, 1 - c)


def _route_chips(j):
    x, y, c, chips = _chips4()
    return j, j, (*chips[j + 1], c)


def _route_sibling_whole(j):
    x, y, c, _ = _chips4()
    return 0, 0, (x, y, 1 - c)


def _route_chips_whole(j):
    x, y, c, chips = _chips4()
    return 0, j, (*chips[j + 1], c)


def _route_everyone(j):
    x, y, c, _ = _chips4()
    flip = [(j + 1) >> 2 & 1, (j + 1) >> 1 & 1, (j + 1) & 1]
    return 0, j, tuple(1 - v if f else v for v, f in zip((x, y, c), flip))


def _xchg_copies(routes, src, dst, send_sems, recv_sems):
    copies, sem = [], 0
    for k, (route, n) in enumerate(routes):
        for j in range(n):
            si, di, peer = route(j)
            copies.append(pltpu.make_async_remote_copy(
                src_ref=src[k].at[si], dst_ref=dst[k].at[di], send_sem=send_sems.at[sem], recv_sem=recv_sems.at[sem],
                device_id=peer, device_id_type=MESH))
            sem += 1
    return copies


def _xchg_start(srcs, slot_shapes, routes, name, after=None):
    k_n = len(srcs)
    n_in = 2 * k_n + (after is not None)
    n_sem = sum(n for _, n in routes)
    dsts = [lax.empty((n,) + tuple(sh), a.dtype) for sh, a, (_, n) in zip(slot_shapes, srcs, routes)]

    def body(*refs):
        src, dst = refs[:k_n], refs[k_n:2 * k_n]
        send_sems, recv_sems, token = refs[n_in], refs[n_in + 1], refs[-1]
        for cp in _xchg_copies(routes, src, dst, send_sems, recv_sems):
            cp.start()
        token[...] = jnp.zeros_like(token)

    arrays = list(srcs) + dsts
    outs = pl.pallas_call(
        body,
        name=name,
        in_specs=[HBM_SPEC] * (2 * k_n) + [ANY_SPEC] * (n_in - 2 * k_n),
        out_specs=(SEM_SPEC, SEM_SPEC, *[HBM_SPEC] * (2 * k_n), pl.BlockSpec(memory_space=pltpu.VMEM)),
        out_shape=(pltpu.SemaphoreType.DMA((n_sem,)), pltpu.SemaphoreType.DMA((n_sem,)),
                   *[pltpu.HBM(a.shape, a.dtype) for a in arrays], jax.ShapeDtypeStruct((8, LANES), F32)),
        input_output_aliases={i: 2 + i for i in range(2 * k_n)},
        compiler_params=pltpu.CompilerParams(has_side_effects=DATAFLOW),
    )(*[_hbm(a) for a in arrays], *([] if after is None else [after]))
    return outs[0], outs[1], list(outs[2:2 + k_n]), list(outs[2 + k_n:2 + 2 * k_n]), outs[-1]


def _xchg_wait(send_sems, recv_sems, srcs, dsts, routes, after, name):
    k_n = len(srcs)

    def body(*refs):
        src, dst = refs[:k_n], refs[k_n:2 * k_n]
        for cp in _xchg_copies(routes, src, dst, refs[2 * k_n], refs[2 * k_n + 1]):
            cp.wait_send()
            cp.wait_recv()

    arrays = list(srcs) + list(dsts)
    outs = pl.pallas_call(
        body,
        name=name,
        in_specs=[HBM_SPEC] * (2 * k_n) + [SEM_SPEC, SEM_SPEC, ANY_SPEC],
        out_specs=tuple([HBM_SPEC] * (2 * k_n)),
        out_shape=tuple(pltpu.HBM(a.shape, a.dtype) for a in arrays),
        input_output_aliases={i: i for i in range(2 * k_n)},
        compiler_params=pltpu.CompilerParams(has_side_effects=DATAFLOW),
    )(*arrays, send_sems, recv_sems, after)
    return list(outs[:k_n]), list(outs[k_n:])


SMALL = ("norm1_gain", "gmlp_v_gain", "w_spatial", "b_spatial", "attn_sinks", "rel_bias_table", "norm2_gain", "final_gain")
LANES = 128


def _swap_start(grads, smalls, tag, after=None):
    srcs = list(grads) + [a[None] for a in smalls]
    shapes = [g.shape[1:] for g in grads] + [a.shape for a in smalls]
    routes = [(_route_sibling, 4)] * len(grads) + [(_route_sibling_whole, 1)] * len(smalls)
    return _xchg_start(srcs, shapes, routes, f"rs_{tag}_swap_start", after), routes, len(grads)


def _swap_sums(swap, after, tag, small_dtypes=None):
    (send1, recv1, src1, land1, _), routes, n_rs = swap
    x, y, c, chips = _chips4()
    blocks = jnp.stack([_dev_index(*chip, c) for chip in chips]).astype(jnp.int32)
    src1, land1 = _xchg_wait(send1, recv1, src1, land1, routes, after, f"rs_{tag}_swap_wait")
    return _rs_sum(src1[:n_rs], land1[:n_rs], blocks, f"rs_{tag}_sum", src1[n_rs:], land1[n_rs:],
                   small_dtypes or [F32] * (len(src1) - n_rs))


def _chips_start(sums, small_sums, tag, everyone=()):
    sends = [ps[1] for ps in sums] + list(small_sums) + [a[None] for a in everyone]
    routes = [(_route_chips, 3)] * len(sums) + [(_route_chips_whole, 3)] * len(small_sums) + [(_route_everyone, 7)] * len(everyone)
    return _xchg_start(sends, [a.shape[1:] for a in sends], routes, f"rs_{tag}_chips_start"), routes


def _chips_wait(chips, after, tag):
    (send2, recv2, src2, land2, _), routes = chips
    return _xchg_wait(send2, recv2, src2, land2, routes, after, f"rs_{tag}_chips_wait")


def kernel(x, p, norm1_gain, w_in, gmlp_v_gain, w_spatial, b_spatial, attn_sinks, rel_bias_table, w_out, norm2_gain, w_ff1, w_ff2, w_ple_proj, w_ple_gate, final_gain, loss_target, m_norm1_gain, m_w_in, m_gmlp_v_gain, m_w_spatial, m_b_spatial, m_attn_sinks, m_rel_bias_table, m_w_out, m_norm2_gain, m_w_ff1, m_w_ff2, m_w_ple_proj, m_w_ple_gate, m_final_gain, v_norm1_gain, v_w_in, v_gmlp_v_gain, v_w_spatial, v_b_spatial, v_attn_sinks, v_rel_bias_table, v_w_out, v_norm2_gain, v_w_ff1, v_w_ff2, v_w_ple_proj, v_w_ple_gate, v_final_gain):
    args = dict(locals())
    s = x.shape[1]
    big = ("w_in", "w_out", "w_ff1", "w_ff2", "w_ple_proj", "w_ple_gate")

    x2, p2, t2 = x.reshape(s, D_MODEL), p.reshape(s, PLE_DIM), loss_target.reshape(s, D_MODEL)
    g1, gv, w_sp, b_sp, sinks, table, g2, gf = (args[n] for n in SMALL)
    bucket = jnp.asarray(_bucket_table())
    b_col = b_sp.reshape(GROUPS, BLOCK, 1)

    def shard(name):
        return args[name][0].T if name.endswith("w_in") else args[name][0]

    lands = _prep_weights([shard(n) for n in big])
    (send_in, recv_in, fly_in), (rest_sems, fly_rest), token = _ag_start(lands[:1], lands[1:])
    mid_in, fwd_send_in, fwd_recv_in = _ag_mid(fly_in, send_in, recv_in, token, "ag_mid_w_in")
    g_in = _ag_end(mid_in, fwd_send_in, fwd_recv_in, "ag_end_w_in")[0]
    full_in = g_in.reshape(D_IN, D_MODEL)

    zuv, qkv, hn1 = _in_proj(x2, g1, full_in)
    fly_rest, relay_sems, relayed = _ag_split_call(
        fly_rest, [("relay_halves", *rest_sems[:2])], ["relays"], zuv, "ag_relay")
    mix, *saved = _mixer_fwd(zuv, qkv, gv, w_sp[0], b_col, sinks, table.T, bucket, after=relayed)
    fly_rest, fwd_sems, _ = _ag_split_call(
        fly_rest, [("others", *rest_sems[2:]), ("relays", *relay_sems)], ["near", "far"], mix, "ag_mid_rest")
    g_out, g_ff1, g_ff2, g_proj, g_gate = _ag_split_call(
        fly_rest, [("near", *fwd_sems[:2]), ("far", *fwd_sems[2:])], [], mix, "ag_end_rest")[0]
    full_out, full_ff2, full_gate = g_out.reshape(D_MODEL, D_MODEL), g_ff2.reshape(D_FF, D_MODEL), g_gate.reshape(D_MODEL, D_MODEL)
    full_proj = g_proj.transpose(1, 0, 2).reshape(PLE_DIM, D_MODEL)

    tail_small, dh1, dmix, hn2, a, df, dh2, gw_out, gw_proj, gw_gate = _tail(
        x2, mix, p2, t2, g2, gf.reshape(1, D_MODEL), full_out, g_ff1, full_ff2, full_gate, full_proj)
    gw_out, gw_gate = (g.reshape(N_DEV, D_MODEL // N_DEV, D_MODEL) for g in (gw_out, gw_gate))

    few_names = ("w_out", "w_ple_proj", "w_ple_gate")
    ff1_names = ("w_ff1",) + few_names
    gw_ff1 = _wgrad(hn2, df, D_MODEL, 2 * D_FF // N_DEV, "wgrad_ff1", peer_cols=D_FF // N_DEV)
    ff1_swap = _swap_start([gw_ff1, gw_out, gw_proj, gw_gate], [tail_small], "ff1")
    gw_ff2 = _wgrad(a, dh2, 1024, D_MODEL, "wgrad_ff2", after=ff1_swap[0][4]).reshape(N_DEV, D_FF // N_DEV, D_MODEL)
    ff1_sums, ff1_small = _swap_sums(ff1_swap, gw_ff2, "ff1")
    ff1_chips = _chips_start(ff1_sums, ff1_small, "ff1")
    ff2_swap = _swap_start([gw_ff2], [], "ff2", after=ff1_chips[0][4])
    dzm, dkv, d_gv, d_wsp, d_bsp, d_attn = _mixer_bwd(
        zuv, qkv, dmix, saved, gv, w_sp[0], b_col, bucket, after=ff2_swap[0][4])
    ff2_sums, _ = _swap_sums(ff2_swap, dzm, "ff2")
    ff2_chips = _chips_start(ff2_sums, [], "ff2")
    gw_in = _wgrad_in(dzm, dkv, hn1, after=ff2_chips[0][4]).reshape(N_DEV, D_IN // N_DEV, D_MODEL)
    in_swap = _swap_start([gw_in], [d_gv, d_wsp, d_bsp, d_attn], "in")
    dx, d_g1 = _in_bwd(x2, dh1, dzm, dkv, g1, full_in, after=in_swap[0][4])
    grad_x = dx.reshape(x.shape)
    in_sums, in_small = _swap_sums(in_swap, dx, "in", small_dtypes=[F32, BF, F32, F32])
    in_chips = _chips_start(in_sums, in_small, "in", everyone=[d_g1])

    grads, deltas, new_m, new_v = {}, {}, {}, {}
    after, small_own, small_recv = in_chips[0][4], [], []
    for chips, sums, names, tag in ((ff1_chips, ff1_sums, ff1_names, "ff1"), (ff2_chips, ff2_sums, ("w_ff2",), "ff2"),
                                    (in_chips, in_sums, ("w_in",), "in")):
        sent, recvs = _chips_wait(chips, after, tag)
        small_own += sent[len(names):]
        small_recv += recvs[len(names):]
        tiled = [n for n in names if n not in few_names]
        done = [_final_adamw(part, recv, shard(n), shard("m_" + n), shard("v_" + n), "adamw_" + n)
                for n, (part, _), recv in zip(tiled, sums, recvs)]
        few = names[len(tiled):]
        if few:
            done += zip(*_final_adamw_few(
                [ps[0] for ps in sums[len(tiled):]], recvs[len(tiled):len(names)], [shard(n) for n in few],
                [shard("m_" + n) for n in few], [shard("v_" + n) for n in few], "adamw_few"))
        for n, results in zip(names, done):
            for dst, arr in zip((grads, deltas, new_m, new_v), results):
                dst[n] = (arr.T if n == "w_in" else arr)[None]
            after = results[1]

    views = {"w_spatial": (GROUPS, BLOCK, BLOCK), "b_spatial": (GROUPS, BLOCK), "final_gain": (1, D_MODEL)}
    def view(name, base):
        return args[name].T if base == "rel_bias_table" else args[name].reshape(views.get(base, args[name].shape))

    small_in = [[view(pre + n, n) for n in SMALL] for pre in ("", "m_", "v_")]
    loss, *small_out = _adamw_small(small_own[:-1], small_recv[:-1], small_own[-1], small_recv[-1], *small_in, after=after)
    for dst, arrays in zip((grads, deltas, new_m, new_v), small_out):
        for n, arr in zip(SMALL, arrays):
            dst[n] = arr.T if n == "rel_bias_table" else arr.reshape(args[n].shape)
    loss = loss[0, 0]

    order = ("norm1_gain", "w_in", "gmlp_v_gain", "w_spatial", "b_spatial", "attn_sinks", "rel_bias_table", "w_out",
             "norm2_gain", "w_ff1", "w_ff2", "w_ple_proj", "w_ple_gate", "final_gain")
    return (loss, grad_x, *[grads[n] for n in order], *[deltas[n] for n in order],
            *[new_m[n] for n in order], *[new_v[n] for n in order])
```

```python
import math

import numpy as np
import jax
import jax.numpy as jnp
from jax import lax
from jax.experimental import pallas as pl
from jax.experimental.pallas import tpu as pltpu

F32 = jnp.float32
BF = jnp.bfloat16
MESH = pl.DeviceIdType.MESH
N_DEV = 8

D_MODEL = 1024
PLE_DIM = 256
D_GMLP = 512
GROUPS = 4
GDIM = 128
BLOCK = 128
D_ATTN = 512
HEAD_DIM = 64
N_Q = 8
Q_PER_KV = 4
N_KV = N_Q // Q_PER_KV
ROWS4 = Q_PER_KV * BLOCK
D_KV = 128
D_FF = 4096
D_IN = 1792
D_MAIN = 2 * D_GMLP + D_ATTN
REL_BUCKETS = 32
EPS = 1e-6
NEG_INF = -1e30
SCALE = HEAD_DIM ** -0.5
GELU_C = math.sqrt(2.0 / math.pi)
GELU_A = 0.044715

ADAM_LR = 0.001
ADAM_B1 = 0.9
ADAM_B2 = 0.999
ADAM_EPS = 1e-08
ADAM_WD = 0.01
ADAM_STEP = 10

V7X_VMEM_LIMIT = 61440 * 1024
TOK_TILE = 256
IO_TOK_TILE = 512


def _call(body, after=None, **kw):
    if after is None:
        return pl.pallas_call(body, **kw)
    n_in = len(kw["in_specs"])

    def ordered(*refs):
        body(*refs[:n_in], *refs[n_in + 1:])

    kw["in_specs"] = list(kw["in_specs"]) + [pl.BlockSpec(memory_space=pl.ANY)]
    fn = pl.pallas_call(ordered, **kw)
    return lambda *operands: fn(*operands, after)


def _params(sem=None):
    if sem is None:
        return pltpu.CompilerParams(vmem_limit_bytes=V7X_VMEM_LIMIT)
    return pltpu.CompilerParams(dimension_semantics=sem, vmem_limit_bytes=V7X_VMEM_LIMIT)


def _nn(a, b):
    return jnp.dot(a, b, preferred_element_type=F32)


def _nt(a, b):
    return lax.dot_general(a, b, (((1,), (1,)), ((), ())), preferred_element_type=F32)


def _tn(a, b):
    return lax.dot_general(a, b, (((0,), (0,)), ((), ())), preferred_element_type=F32)


def _gelu_tanh(x):
    return jnp.tanh(GELU_C * (x + GELU_A * (x * x * x)))


def _gelu(x, t):
    return x * (0.5 * (1.0 + t))


def _gelu_and_grad(x, t):
    cdf = 0.5 * (1.0 + t)
    return x * cdf, cdf + 0.5 * x * (1.0 - t * t) * (GELU_C * (1.0 + 3.0 * GELU_A * (x * x)))


def _rms_scale(x):
    return lax.rsqrt(jnp.mean(x * x, axis=-1, keepdims=True) + EPS)


def _rms_bwd(dxn, x, r):
    return r * dxn - x * ((r * r * r) * jnp.mean(dxn * x, axis=-1, keepdims=True))


def _bucket_table():
    a = np.arange(BLOCK)[:, None]
    j = np.arange(2 * BLOCK)[None, :]
    n = BLOCK + a - j
    valid = (n >= 0) & (n < BLOCK)
    nc = np.maximum(n, 0)
    max_exact = REL_BUCKETS // 2
    nf = np.maximum(nc, 1).astype(np.float32)
    large = max_exact + (
        np.log(nf / np.float32(max_exact)) / np.float32(math.log(BLOCK / max_exact)) * np.float32(REL_BUCKETS - max_exact)
    ).astype(np.int32)
    large = np.minimum(large, REL_BUCKETS - 1)
    bucket = np.where(nc < max_exact, nc, large)
    return np.where(valid, bucket, -1).astype(np.int32)


def _in_proj(x, g1, w_in_t):
    s = x.shape[0]
    tm = min(IO_TOK_TILE, s)

    def body(x_ref, g_ref, w_ref, zuv_ref, qkv_ref, hn_ref):
        xv = x_ref[...]
        hn = ((xv * _rms_scale(xv)) * g_ref[...]).astype(BF)
        hn_ref[...] = hn
        z = _nt(hn, w_ref[...])
        zuv_ref[...] = z[:, : 2 * D_GMLP]
        qkv_ref[...] = z[:, 2 * D_GMLP:].astype(BF)

    return _call(
        body,
        name="in_proj",
        grid=(s // tm,),
        in_specs=[
            pl.BlockSpec((tm, D_MODEL), lambda i: (i, 0)),
            pl.BlockSpec((1, D_MODEL), lambda i: (0, 0)),
            pl.BlockSpec((D_IN, D_MODEL), lambda i: (0, 0)),
        ],
        out_specs=[
            pl.BlockSpec((tm, 2 * D_GMLP), lambda i: (i, 0)),
            pl.BlockSpec((tm, D_ATTN + 2 * D_KV), lambda i: (i, 0)),
            pl.BlockSpec((tm, D_MODEL), lambda i: (i, 0)),
        ],
        out_shape=[
            jax.ShapeDtypeStruct((s, 2 * D_GMLP), F32),
            jax.ShapeDtypeStruct((s, D_ATTN + 2 * D_KV), BF),
            jax.ShapeDtypeStruct((s, D_MODEL), BF),
        ],
        compiler_params=_params(("arbitrary",)),
    )(x, g1, w_in_t)


def _head_rows(h):
    kh, g = divmod(h, Q_PER_KV)
    return kh, slice(g * BLOCK, (g + 1) * BLOCK)


def _build_bias(bias_ref, bucket_ref, table_ref):
    bucket = bucket_ref[...]
    for h in range(N_Q):
        acc = jnp.zeros((BLOCK, 2 * BLOCK), F32)
        for b in range(REL_BUCKETS):
            acc = jnp.where(bucket == b, table_ref[h, b], acc)
        kh, rows = _head_rows(h)
        bias_ref[kh, rows, :] = acc


def _window_masks(i):
    row = lax.broadcasted_iota(jnp.int32, (ROWS4, BLOCK), 0) & (BLOCK - 1)
    col = lax.broadcasted_iota(jnp.int32, (ROWS4, BLOCK), 1)
    return (col > row) & (i > 0), col <= row


def _stack_heads(ref, kh, offset):
    first = offset + kh * Q_PER_KV * HEAD_DIM
    return jnp.concatenate(
        [ref[:, first + g * HEAD_DIM: first + (g + 1) * HEAD_DIM].astype(BF) for g in range(Q_PER_KV)], axis=0)


def _stack_sinks(sink_ref, kh):
    return jnp.concatenate([jnp.full((BLOCK, 1), sink_ref[0, kh * Q_PER_KV + g], F32) for g in range(Q_PER_KV)], axis=0)


def _tril_bf16(w_ref, g):
    row = lax.broadcasted_iota(jnp.int32, (BLOCK, BLOCK), 0)
    col = lax.broadcasted_iota(jnp.int32, (BLOCK, BLOCK), 1)
    return jnp.where(col <= row, w_ref[g], 0.0).astype(BF)


def _attn_probs(q_h, k_prev, k_cur, bias_h, sink, valid_prev, valid_cur):
    l_prev = jnp.where(valid_prev, _nt(q_h, k_prev) * SCALE + bias_h[:, :BLOCK], NEG_INF)
    l_cur = jnp.where(valid_cur, _nt(q_h, k_cur) * SCALE + bias_h[:, BLOCK:], NEG_INF)
    m = jnp.maximum(jnp.maximum(jnp.max(l_prev, axis=-1, keepdims=True), jnp.max(l_cur, axis=-1, keepdims=True)), sink)
    e_prev = jnp.exp(l_prev - m)
    e_cur = jnp.exp(l_cur - m)
    e_sink = jnp.exp(sink - m)
    denom = jnp.sum(e_prev, axis=-1, keepdims=True) + jnp.sum(e_cur, axis=-1, keepdims=True) + e_sink
    return e_prev / denom, e_cur / denom, e_sink / denom


def _mixer_specs(nb):
    cl = lambda i: jnp.minimum(i, nb - 1)
    return [
        pl.BlockSpec((BLOCK, 2 * D_GMLP), lambda i: (cl(i), 0)),
        pl.BlockSpec((BLOCK, D_ATTN), lambda i: (cl(i), 0)),
        pl.BlockSpec((BLOCK, 2 * D_KV), lambda i: (cl(i), D_ATTN // (2 * D_KV))),
        pl.BlockSpec((BLOCK, 2 * D_KV), lambda i: (jnp.maximum(cl(i) - 1, 0), D_ATTN // (2 * D_KV))),
        pl.BlockSpec((1, D_GMLP), lambda i: (0, 0)),
        pl.BlockSpec((GROUPS, BLOCK, BLOCK), lambda i: (0, 0, 0)),
        pl.BlockSpec((GROUPS, BLOCK, 1), lambda i: (0, 0, 0)),
        pl.BlockSpec(memory_space=pltpu.SMEM),
        pl.BlockSpec(memory_space=pltpu.SMEM),
        pl.BlockSpec((BLOCK, 2 * BLOCK), lambda i: (0, 0)),
    ]


def _mixer_fwd(zuv, qkv, gv, w_sp, b_sp, sinks, table, bucket, after=None):
    s = zuv.shape[0]
    nb = s // BLOCK

    def body(zuv_ref, q_ref, kvc_ref, kvp_ref, gv_ref, w_ref, b_ref, sink_ref, table_ref, bucket_ref,
             mix_ref, tanh_ref, prob_ref, psink_ref, bias_ref):
        i = pl.program_id(0)

        @pl.when(i == 0)
        def _():
            _build_bias(bias_ref, bucket_ref, table_ref)

        t = _gelu_tanh(zuv_ref[...])
        tanh_ref[...] = t
        u = _gelu(zuv_ref[:, :D_GMLP], t[:, :D_GMLP])
        vg = _gelu(zuv_ref[:, D_GMLP:], t[:, D_GMLP:])
        for g in range(GROUPS):
            sl = slice(g * GDIM, (g + 1) * GDIM)
            vg_g = vg[:, sl]
            vn = ((vg_g * _rms_scale(vg_g)) * gv_ref[:, sl]).astype(BF)
            sv = _nn(_tril_bf16(w_ref, g), vn) + b_ref[g]
            mix_ref[:, sl] = (u[:, sl] * sv).astype(BF)

        valid_prev, valid_cur = _window_masks(i)
        for kh in range(N_KV):
            ksl = slice(kh * HEAD_DIM, (kh + 1) * HEAD_DIM)
            vsl = slice(D_KV + kh * HEAD_DIM, D_KV + (kh + 1) * HEAD_DIM)
            q4 = _stack_heads(q_ref, kh, 0)
            p_prev, p_cur, p_sink = _attn_probs(
                q4, kvp_ref[:, ksl], kvc_ref[:, ksl], bias_ref[kh], _stack_sinks(sink_ref, kh), valid_prev, valid_cur)
            prob_ref[0, kh, :, :BLOCK] = p_prev
            prob_ref[0, kh, :, BLOCK:] = p_cur
            psink_ref[0, kh] = jnp.broadcast_to(p_sink, (ROWS4, LANES))
            o4 = _nn(p_prev.astype(BF), kvp_ref[:, vsl]) + _nn(p_cur.astype(BF), kvc_ref[:, vsl])
            for g in range(Q_PER_KV):
                first = D_GMLP + (kh * Q_PER_KV + g) * HEAD_DIM
                mix_ref[:, first:first + HEAD_DIM] = o4[g * BLOCK:(g + 1) * BLOCK].astype(BF)

    return _call(
        body,
        name="mixer_fwd",
        after=after,
        grid=(nb,),
        in_specs=_mixer_specs(nb),
        out_specs=[
            pl.BlockSpec((BLOCK, D_MODEL), lambda i: (i, 0)),
            pl.BlockSpec((BLOCK, 2 * D_GMLP), lambda i: (i, 0)),
            pl.BlockSpec((1, N_KV, ROWS4, 2 * BLOCK), lambda i: (i, 0, 0, 0)),
            pl.BlockSpec((1, N_KV, ROWS4, LANES), lambda i: (i, 0, 0, 0)),
        ],
        out_shape=[
            jax.ShapeDtypeStruct((s, D_MODEL), BF),
            jax.ShapeDtypeStruct((s, 2 * D_GMLP), F32),
            jax.ShapeDtypeStruct((nb, N_KV, ROWS4, 2 * BLOCK), F32),
            jax.ShapeDtypeStruct((nb, N_KV, ROWS4, LANES), F32),
        ],
        scratch_shapes=[pltpu.VMEM((N_KV, ROWS4, 2 * BLOCK), F32)],
        compiler_params=_params(("arbitrary",)),
    )(zuv, qkv, qkv, qkv, gv, w_sp, b_sp, sinks, table, bucket)


def _tail(x, mix, p, t, g2, gf, w_out, w_ff1, w_ff2, w_gate, w_proj):
    s = x.shape[0]
    tm = min(TOK_TILE, s)
    n_ff = w_ff1.shape[0]
    fc = D_FF // n_ff
    pc = D_MODEL // N_DEV

    def body(x_ref, mix_ref, p_ref, t_ref, g2_ref, gf_ref, wo_ref, w1_ref, w2_ref, wg_ref, wp_ref,
             small_ref, dh1_ref, dmix_ref, hn2_ref, a_ref, df_ref, dh2_ref, gwo_ref, gwp_ref, gwg_ref, f_ref):
        i = pl.program_id(0)

        @pl.when(i == 0)
        def _():
            small_ref[...] = jnp.zeros_like(small_ref)
            gwo_ref[...] = jnp.zeros_like(gwo_ref)
            gwp_ref[...] = jnp.zeros_like(gwp_ref)
            gwg_ref[...] = jnp.zeros_like(gwg_ref)

        h1 = x_ref[...] + _nn(mix_ref[...], wo_ref[...])
        dh1_ref[...] = h1
        r2 = _rms_scale(h1)
        hn2_ref[...] = ((h1 * r2) * g2_ref[...]).astype(BF)
        h2 = h1
        for c in range(n_ff):
            f = _nn(hn2_ref[...], w1_ref[c])
            f_ref[:, c * fc:(c + 1) * fc] = f
            a = jnp.square(jnp.maximum(f, 0.0)).astype(BF)
            a_ref[:, c * fc:(c + 1) * fc] = a
            h2 = h2 + _nn(a, w2_ref[c * fc:(c + 1) * fc, :])
        h2b = h2.astype(BF)
        gate = jax.nn.sigmoid(_nn(h2b, wg_ref[...]))
        pb = p_ref[...].astype(BF)
        pp = _nn(pb, wp_ref[...])
        h3 = h2 + gate * pp
        rf = _rms_scale(h3)
        gf_v = gf_ref[...]
        err = (h3 * rf) * gf_v - t_ref[...]
        small_ref[2:3, :] += jnp.sum(jnp.sum(err * err, axis=-1, keepdims=True), axis=0, keepdims=True) * (0.5 / D_MODEL)

        dy = err * (1.0 / D_MODEL)
        small_ref[1:2, :] += jnp.sum(dy * (h3 * rf), axis=0, keepdims=True)
        dh3 = _rms_bwd(dy * gf_v, h3, rf)
        gw_proj = _tn(pb, (dh3 * gate).astype(BF))
        for q in range(N_DEV):
            gwp_ref[q] += gw_proj[:, q * pc:(q + 1) * pc]
        dgl = ((dh3 * pp) * (gate * (1.0 - gate))).astype(BF)
        gwg_ref[...] += _tn(h2b, dgl)
        dh2 = dh3 + _nt(dgl, wg_ref[...])
        dmix_ref[...] = dh2
        dh2_ref[...] = dh2.astype(BF)
        dhn2 = jnp.zeros((tm, D_MODEL), F32)
        for c in range(n_ff):
            da = _nt(dh2_ref[...], w2_ref[c * fc:(c + 1) * fc, :])
            df = (da * (2.0 * jnp.maximum(f_ref[:, c * fc:(c + 1) * fc], 0.0))).astype(BF)
            df_ref[:, c * fc:(c + 1) * fc] = df
            dhn2 = dhn2 + _nt(df, w1_ref[c])
        h1 = dh1_ref[...]
        small_ref[0:1, :] += jnp.sum(dhn2 * (h1 * r2), axis=0, keepdims=True)
        dh1 = dmix_ref[...] + _rms_bwd(dhn2 * g2_ref[...], h1, r2)
        dh1_ref[...] = dh1
        dh1b = dh1.astype(BF)
        gwo_ref[...] += _tn(mix_ref[...], dh1b)
        dmix_ref[...] = _nt(dh1b, wo_ref[...])

    tile = lambda cols: pl.BlockSpec((tm, cols), lambda i: (i, 0))
    whole = lambda shape: pl.BlockSpec(shape, lambda i: (0,) * len(shape), pipeline_mode=pl.Buffered(1))
    total = lambda shape: pl.BlockSpec(shape, lambda i: (0,) * len(shape))
    row = pl.BlockSpec((1, D_MODEL), lambda i: (0, 0))
    act = lambda cols, dt: jax.ShapeDtypeStruct((s, cols), dt)
    gw_shapes = [(D_MODEL, D_MODEL), (N_DEV, PLE_DIM, pc), (D_MODEL, D_MODEL)]
    return _call(
        body,
        name="tail",
        grid=(s // tm,),
        in_specs=[tile(D_MODEL), tile(D_MODEL), tile(PLE_DIM), tile(D_MODEL), row, row,
                  whole(w_out.shape), whole(w_ff1.shape), whole(w_ff2.shape), whole(w_gate.shape), whole(w_proj.shape)],
        out_specs=[total((8, D_MODEL)), tile(D_MODEL), tile(D_MODEL), tile(D_MODEL), tile(D_FF), tile(D_FF), tile(D_MODEL),
                   *[total(shape) for shape in gw_shapes]],
        out_shape=[jax.ShapeDtypeStruct((8, D_MODEL), F32),
                   act(D_MODEL, F32), act(D_MODEL, F32), act(D_MODEL, BF), act(D_FF, BF), act(D_FF, BF), act(D_MODEL, BF),
                   *[jax.ShapeDtypeStruct(shape, F32) for shape in gw_shapes]],
        scratch_shapes=[pltpu.VMEM((tm, D_FF), F32)],
        compiler_params=_params(("arbitrary",)),
    )(x, mix, p, t, g2, gf, w_out, w_ff1, w_ff2, w_gate, w_proj)


def _mixer_bwd(zuv, qkv, dmix, saved, gv, w_sp, b_sp, bucket, after=None):
    s = zuv.shape[0]
    nb = s // BLOCK

    def body(zuv_ref, q_ref, kvc_ref, kvp_ref, gv_ref, w_ref, b_ref, bucket_ref, dmix_ref, tanh_ref, prob_ref, psink_ref,
             dzm_ref, dkv_ref, dgv_ref, dw_ref, db_ref, dattn_ref,
             dbias_ref, carry_ref, dsink_acc, db_acc):
        i = pl.program_id(0)

        @pl.when(i == 0)
        def _():
            dbias_ref[...] = jnp.zeros_like(dbias_ref)
            carry_ref[...] = jnp.zeros_like(carry_ref)
            dsink_acc[...] = jnp.zeros_like(dsink_acc)
            dgv_ref[...] = jnp.zeros_like(dgv_ref)
            dw_ref[...] = jnp.zeros_like(dw_ref)
            db_acc[...] = jnp.zeros_like(db_acc)

        @pl.when(i < nb)
        def _():
            u, du_dz = _gelu_and_grad(zuv_ref[:, :D_GMLP], tanh_ref[:, :D_GMLP])
            vg, dvg_dz = _gelu_and_grad(zuv_ref[:, D_GMLP:], tanh_ref[:, D_GMLP:])
            for g in range(GROUPS):
                sl = slice(g * GDIM, (g + 1) * GDIM)
                vg_g = vg[:, sl]
                rg = _rms_scale(vg_g)
                vhat = vg_g * rg
                gain = gv_ref[:, sl]
                vn = (vhat * gain).astype(BF)
                w_g = _tril_bf16(w_ref, g)
                sv = _nn(w_g, vn) + b_ref[g]
                dmix_a = dmix_ref[:, sl]
                dsv = dmix_a * u[:, sl]
                dsvb = dsv.astype(BF)
                db_acc[g] += jnp.sum(dsv, axis=-1, keepdims=True)
                dw_ref[g] += _nt(dsvb, vn)
                dvn = _tn(w_g, dsvb)
                dgv_ref[:, sl] += jnp.sum(dvn * vhat, axis=0, keepdims=True)
                dvg = _rms_bwd(dvn * gain, vg_g, rg)
                dzm_ref[:, sl] = ((dmix_a * sv) * du_dz[:, sl]).astype(BF)
                dzm_ref[:, D_GMLP + g * GDIM: D_GMLP + (g + 1) * GDIM] = (dvg * dvg_dz[:, sl]).astype(BF)

            for kh in range(N_KV):
                ksl = slice(kh * HEAD_DIM, (kh + 1) * HEAD_DIM)
                vsl = slice(D_KV + kh * HEAD_DIM, D_KV + (kh + 1) * HEAD_DIM)
                k_prev, k_cur = kvp_ref[:, ksl], kvc_ref[:, ksl]
                v_prev, v_cur = kvp_ref[:, vsl], kvc_ref[:, vsl]
                q4 = _stack_heads(q_ref, kh, 0)
                p_prev, p_cur, p_sink = prob_ref[0, kh, :, :BLOCK], prob_ref[0, kh, :, BLOCK:], psink_ref[0, kh, :, 0:1]
                do4 = _stack_heads(dmix_ref, kh, D_GMLP)
                dp_prev = _nt(do4, v_prev)
                dp_cur = _nt(do4, v_cur)
                delta = jnp.sum(p_prev * dp_prev, axis=-1, keepdims=True) + jnp.sum(p_cur * dp_cur, axis=-1, keepdims=True)
                ds_prev = p_prev * (dp_prev - delta)
                ds_cur = p_cur * (dp_cur - delta)
                dsink_acc[kh] -= p_sink * delta
                dbias_ref[kh, :, :BLOCK] += ds_prev
                dbias_ref[kh, :, BLOCK:] += ds_cur
                dsb_prev = ds_prev.astype(BF)
                dsb_cur = ds_cur.astype(BF)
                dq4 = (_nn(dsb_prev, k_prev) + _nn(dsb_cur, k_cur)) * SCALE
                for g in range(Q_PER_KV):
                    first = 2 * D_GMLP + (kh * Q_PER_KV + g) * HEAD_DIM
                    dzm_ref[:, first:first + HEAD_DIM] = dq4[g * BLOCK:(g + 1) * BLOCK].astype(BF)
                dkv_ref[:, ksl] = (carry_ref[:, ksl] + _tn(dsb_prev, q4) * SCALE).astype(BF)
                dkv_ref[:, vsl] = (carry_ref[:, vsl] + _tn(p_prev.astype(BF), do4)).astype(BF)
                carry_ref[:, ksl] = _tn(dsb_cur, q4) * SCALE
                carry_ref[:, vsl] = _tn(p_cur.astype(BF), do4)

        @pl.when(i == nb)
        def _():
            dkv_ref[...] = carry_ref[...].astype(BF)
            row = lax.broadcasted_iota(jnp.int32, (BLOCK, BLOCK), 0)
            col = lax.broadcasted_iota(jnp.int32, (BLOCK, BLOCK), 1)
            for g in range(GROUPS):
                dw_ref[g] = jnp.where(col <= row, dw_ref[g], 0.0)
                db_ref[g:g + 1, :] = jnp.sum(jnp.where(col == row, db_acc[g], 0.0), axis=0, keepdims=True)
            bucket = bucket_ref[...]
            for b in range(N_Q, REL_BUCKETS):
                dattn_ref[N_Q, b] = 0.0
            for h in range(N_Q):
                kh, rows = _head_rows(h)
                dattn_ref[N_Q, h] = jnp.sum(dsink_acc[kh, rows, :])
                dbh = dbias_ref[kh, rows, :]
                for b in range(REL_BUCKETS):
                    dattn_ref[h, b] = jnp.sum(jnp.where(bucket == b, dbh, 0.0))

    cl = lambda i: jnp.minimum(i, nb - 1)
    const = lambda shape: pl.BlockSpec(shape, lambda i: (0,) * len(shape))
    return _call(
        body,
        name="mixer_bwd",
        after=after,
        grid=(nb + 1,),
        in_specs=_mixer_specs(nb)[:7] + [
            const((BLOCK, 2 * BLOCK)),
            pl.BlockSpec((BLOCK, D_MODEL), lambda i: (cl(i), 0)),
            pl.BlockSpec((BLOCK, 2 * D_GMLP), lambda i: (cl(i), 0)),
            pl.BlockSpec((1, N_KV, ROWS4, 2 * BLOCK), lambda i: (cl(i), 0, 0, 0)),
            pl.BlockSpec((1, N_KV, ROWS4, LANES), lambda i: (cl(i), 0, 0, 0)),
        ],
        out_specs=[
            pl.BlockSpec((BLOCK, D_MAIN), lambda i: (cl(i), 0)),
            pl.BlockSpec((BLOCK, 2 * D_KV), lambda i: (jnp.maximum(i - 1, 0), 0)),
            const((1, D_GMLP)),
            const((GROUPS, BLOCK, BLOCK)),
            const((GROUPS, BLOCK)),
            pl.BlockSpec(memory_space=pltpu.SMEM),
        ],
        out_shape=[
            jax.ShapeDtypeStruct((s, D_MAIN), BF),
            jax.ShapeDtypeStruct((s, 2 * D_KV), BF),
            jax.ShapeDtypeStruct((1, D_GMLP), F32),
            jax.ShapeDtypeStruct((GROUPS, BLOCK, BLOCK), F32),
            jax.ShapeDtypeStruct((GROUPS, BLOCK), F32),
            jax.ShapeDtypeStruct((N_Q + 1, REL_BUCKETS), F32),
        ],
        scratch_shapes=[
            pltpu.VMEM((N_KV, ROWS4, 2 * BLOCK), F32),
            pltpu.VMEM((BLOCK, 2 * D_KV), F32),
            pltpu.VMEM((N_KV, ROWS4, 1), F32),
            pltpu.VMEM((GROUPS, BLOCK, 1), F32),
        ],
        compiler_params=_params(("arbitrary",)),
    )(zuv, qkv, qkv, qkv, gv, w_sp, b_sp, bucket, dmix, *saved)


def _in_bwd(x, dh1, dzm, dkv, g1, w_in_t, after=None):
    s = x.shape[0]
    tm = min(IO_TOK_TILE, s)

    def body(x_ref, dh1_ref, dzm_ref, dkv_ref, g_ref, w_ref, dx_ref, dg_ref):
        @pl.when(pl.program_id(0) == 0)
        def _():
            dg_ref[...] = jnp.zeros_like(dg_ref)

        dhn = _nn(dzm_ref[...], w_ref[:D_MAIN, :]) + _nn(dkv_ref[...], w_ref[D_MAIN:, :])
        xv = x_ref[...]
        r = _rms_scale(xv)
        dg_ref[...] += jnp.sum(dhn * (xv * r), axis=0, keepdims=True)
        dx_ref[...] = dh1_ref[...] + _rms_bwd(dhn * g_ref[...], xv, r)

    tile = lambda cols: pl.BlockSpec((tm, cols), lambda i: (i, 0))
    row = pl.BlockSpec((1, D_MODEL), lambda i: (0, 0))
    return _call(
        body,
        name="in_bwd",
        after=after,
        grid=(s // tm,),
        in_specs=[tile(D_MODEL), tile(D_MODEL), tile(D_MAIN), tile(2 * D_KV), row, pl.BlockSpec((D_IN, D_MODEL), lambda i: (0, 0))],
        out_specs=[tile(D_MODEL), row],
        out_shape=[jax.ShapeDtypeStruct((s, D_MODEL), F32), jax.ShapeDtypeStruct((1, D_MODEL), F32)],
        compiler_params=_params(("arbitrary",)),
    )(x, dh1, dzm, dkv, g1, w_in_t)


def _wgrad_in(dzm, dkv, hn, after=None):
    s = hn.shape[0]
    tm = 2 * D_KV
    n_main = D_MAIN // tm

    def body(dzm_ref, dkv_ref, hn_ref, o_ref):
        i = pl.program_id(0)

        @pl.when(i < n_main)
        def _():
            o_ref[...] = _tn(dzm_ref[...], hn_ref[...])

        @pl.when(i == n_main)
        def _():
            o_ref[...] = _tn(dkv_ref[...], hn_ref[...])

    return _call(
        body,
        name="wgrad_in",
        after=after,
        grid=(n_main + 1,),
        in_specs=[
            pl.BlockSpec((s, tm), lambda i: (0, jnp.minimum(i, n_main - 1))),
            pl.BlockSpec((s, tm), lambda i: (0, 0)),
            pl.BlockSpec((s, D_MODEL), lambda i: (0, 0)),
        ],
        out_specs=pl.BlockSpec((tm, D_MODEL), lambda i: (i, 0)),
        out_shape=jax.ShapeDtypeStruct((D_IN, D_MODEL), F32),
        compiler_params=_params(("arbitrary",)),
    )(dzm, dkv, hn)


def _wgrad(a, b, tm, tn, name, peer_cols=0, after=None):
    s, m = a.shape
    n = b.shape[1]

    def body(a_ref, b_ref, o_ref, at_ref):
        @pl.when(pl.program_id(1) == 0)
        def _():
            at_ref[...] = a_ref[...].astype(BF).T

        r = _nn(at_ref[...], b_ref[...])
        if peer_cols:
            for q in range(tn // peer_cols):
                o_ref[q] = r[:, q * peer_cols:(q + 1) * peer_cols]
        else:
            o_ref[...] = r

    if peer_cols:
        out_spec = pl.BlockSpec((tn // peer_cols, tm, peer_cols), lambda i, j: (j, i, 0))
        out_shape = jax.ShapeDtypeStruct((n // peer_cols, m, peer_cols), F32)
    else:
        out_spec = pl.BlockSpec((tm, tn), lambda i, j: (i, j))
        out_shape = jax.ShapeDtypeStruct((m, n), F32)
    return _call(
        body,
        name=name,
        after=after,
        grid=(m // tm, n // tn),
        in_specs=[pl.BlockSpec((s, tm), lambda i, j: (0, i)), pl.BlockSpec((s, tn), lambda i, j: (0, j))],
        out_specs=out_spec,
        out_shape=out_shape,
        scratch_shapes=[pltpu.VMEM((tm, s), BF)],
        compiler_params=_params(("arbitrary", "arbitrary")),
    )(a, b)


def _adamw_math(w, g, m, v):
    m_new = ADAM_B1 * m + (1.0 - ADAM_B1) * g
    v_new = ADAM_B2 * v + (1.0 - ADAM_B2) * jnp.square(g)
    m_hat = m_new / (1.0 - ADAM_B1 ** ADAM_STEP)
    v_hat = v_new / (1.0 - ADAM_B2 ** ADAM_STEP)
    delta = -ADAM_LR * (m_hat / (jnp.sqrt(v_hat) + ADAM_EPS) + ADAM_WD * w)
    return delta, m_new, v_new


def _final_adamw(part, recv, w, m, v, name):
    r, c = w.shape
    tr = min(r, 512)

    def body(p_ref, r_ref, w_ref, m_ref, v_ref, g_ref, d_ref, mo_ref, vo_ref):
        g = p_ref[...]
        for j in range(3):
            g = g + r_ref[j].astype(F32)
        g_ref[...] = g
        d_ref[...], mo_ref[...], vo_ref[...] = _adamw_math(w_ref[...], g, m_ref[...], v_ref[...])

    spec = pl.BlockSpec((tr, c), lambda i: (i, 0))
    return _call(
        body,
        name=name,
        grid=(r // tr,),
        in_specs=[spec, pl.BlockSpec((3, tr, c), lambda i: (0, i, 0)), spec, spec, spec],
        out_specs=[spec] * 4,
        out_shape=[jax.ShapeDtypeStruct((r, c), F32)] * 4,
        compiler_params=_params(("arbitrary",)),
    )(part, recv, w, m, v)


def _rs_sum(gs, lands, blocks, name, smalls, small_lands, small_dtypes):
    n, n_small = len(gs), len(smalls)
    steps = pl.cdiv(gs[0].shape[1], 256)
    assert all(g.shape[1] % steps == 0 for g in gs)

    def body(blk_ref, *refs):
        ins, small_ins, outs = refs[:5 * n], refs[5 * n:5 * n + 2 * n_small], refs[5 * n + 2 * n_small:]
        for k in range(n):
            g0_ref, g1_ref, g2_ref, g3_ref, l_ref = ins[5 * k:5 * k + 5]
            outs[2 * k][...] = g0_ref[0] + l_ref[0]
            for j, gj_ref in enumerate((g1_ref, g2_ref, g3_ref)):
                outs[2 * k + 1][j] = (gj_ref[0] + l_ref[j + 1]).astype(BF)
        for q in range(n_small):
            outs[2 * n + q][...] = (small_ins[q][...] + small_ins[n_small + q][...]).astype(small_dtypes[q])

    def whole(a):
        return pl.BlockSpec(a.shape, lambda i, blk: (0,) * a.ndim)

    in_specs, out_specs, out_shape, operands = [], [], [], []
    for g, land in zip(gs, lands):
        _, r, c = g.shape
        tr = r // steps
        in_specs += [pl.BlockSpec((1, tr, c), lambda i, blk, j=j: (blk[j], i, 0)) for j in range(4)]
        in_specs.append(pl.BlockSpec((4, tr, c), lambda i, blk: (0, i, 0)))
        out_specs += [pl.BlockSpec((tr, c), lambda i, blk: (i, 0)), pl.BlockSpec((3, tr, c), lambda i, blk: (0, i, 0))]
        out_shape += [jax.ShapeDtypeStruct((r, c), F32), jax.ShapeDtypeStruct((3, r, c), BF)]
        operands += [g, g, g, g, land]
    in_specs += [whole(a) for a in (*smalls, *small_lands)]
    out_specs += [whole(a) for a in smalls]
    out_shape += [jax.ShapeDtypeStruct(a.shape, dt) for a, dt in zip(smalls, small_dtypes)]
    outs = _call(
        body,
        name=name,
        grid_spec=pltpu.PrefetchScalarGridSpec(num_scalar_prefetch=1, grid=(steps,), in_specs=in_specs, out_specs=out_specs),
        out_shape=out_shape,
        compiler_params=_params(("arbitrary",)),
    )(blocks, *operands, *smalls, *small_lands)
    return [(outs[2 * k], outs[2 * k + 1]) for k in range(n)], list(outs[2 * n:])


def _final_adamw_few(parts, recvs, ws, ms, vs, name):
    n = len(ws)

    def body(*refs):
        ins, outs = refs[:5 * n], refs[5 * n:]
        for k in range(n):
            p_ref, r_ref, w_ref, m_ref, v_ref = (ins[q * n + k] for q in range(5))
            g = p_ref[...]
            for j in range(3):
                g = g + r_ref[j].astype(F32)
            outs[k][...] = g
            outs[n + k][...], outs[2 * n + k][...], outs[3 * n + k][...] = _adamw_math(w_ref[...], g, m_ref[...], v_ref[...])

    outs = _call(
        body,
        name=name,
        out_shape=[jax.ShapeDtypeStruct(w.shape, F32) for w in ws] * 4,
        compiler_params=_params(),
    )(*parts, *recvs, *ws, *ms, *vs)
    return outs[:n], outs[n:2 * n], outs[2 * n:3 * n], outs[3 * n:]


def _adamw_small(own, recv, g1_own, g1_recv, weights, moms, vels, after):
    n_w = len(weights)

    def body(*refs):
        own_refs, recv_refs, g1_own_ref, g1_recv_ref = refs[:5], refs[5:10], refs[10], refs[11]
        w_refs, m_refs, v_refs = refs[12:12 + n_w], refs[12 + n_w:12 + 2 * n_w], refs[12 + 2 * n_w:12 + 3 * n_w]
        outs = refs[12 + 3 * n_w:]
        loss_ref, g_refs, d_refs = outs[0], outs[1:1 + n_w], outs[1 + n_w:1 + 2 * n_w]
        mo_refs, vo_refs = outs[1 + 2 * n_w:1 + 3 * n_w], outs[1 + 3 * n_w:]
        x, y, c = lax.axis_index("x"), lax.axis_index("y"), lax.axis_index("c")

        def in_place_order(values, my_place):
            acc = None
            for place in range(len(values)):
                r = place ^ my_place
                term = values[-1]
                for q in range(len(values) - 2, -1, -1):
                    term = jnp.where(r == q, values[q], term)
                acc = term if acc is None else acc + term
            return acc

        def total(k, *index):
            index = index or (slice(None),) * (len(own_refs[k].shape) - 1)
            across = [own_refs[k][(0, *index)], recv_refs[k][(1, *index)], recv_refs[k][(0, *index)], recv_refs[k][(2, *index)]]
            return in_place_order([v.astype(F32) for v in across], 2 * x + y)

        g1 = in_place_order([g1_own_ref[0]] + [g1_recv_ref[j] for j in range(N_DEV - 1)], 4 * x + 2 * y + c)
        grads = [
            g1, total(1), total(2), total(3),
            total(4, slice(N_Q, None), slice(0, N_Q)), total(4, slice(0, N_Q), slice(None)),
            total(0, slice(0, 1), slice(None)), total(0, slice(1, 2), slice(None))]
        loss_ref[...] = total(0, slice(2, 3), slice(0, 1))
        for k in range(n_w):
            g_refs[k][...] = grads[k]
            d_refs[k][...], mo_refs[k][...], vo_refs[k][...] = _adamw_math(w_refs[k][...], grads[k], m_refs[k][...], v_refs[k][...])

    shapes = [jax.ShapeDtypeStruct(w.shape, F32) for w in weights]
    outs = _call(
        body,
        name="adamw_small",
        after=after,
        in_specs=[pl.BlockSpec(memory_space=pltpu.VMEM)] * (12 + 3 * n_w),
        out_shape=[jax.ShapeDtypeStruct((1, 1), F32)] + shapes * 4,
        compiler_params=_params(),
    )(*own, *recv, g1_own, g1_recv, *weights, *moms, *vels)
    return outs[0], outs[1:1 + n_w], outs[1 + n_w:1 + 2 * n_w], outs[1 + 2 * n_w:1 + 3 * n_w], outs[1 + 3 * n_w:]


def _place():
    x, y, c = lax.axis_index("x"), lax.axis_index("y"), lax.axis_index("c")
    return x, y, c, [(1 - x, y), (x, 1 - y), (1 - x, 1 - y)]


def _dev_index(px, py, pc):
    return 4 * px + 2 * py + pc


HBM_SPEC = pl.BlockSpec(memory_space=pltpu.HBM)
SEM_SPEC = pl.BlockSpec(memory_space=pltpu.SEMAPHORE)
ANY_SPEC = pl.BlockSpec(memory_space=pl.ANY)
DATAFLOW = pltpu.SideEffectType.DATAFLOW_SIDE_EFFECTING


def _hbm(a):
    return pltpu.with_memory_space_constraint(a, pltpu.HBM)


def _prep_weights(shards):
    k_n = len(shards)

    def body(*refs):
        ins, outs, stage, sems = refs[:k_n], refs[k_n:2 * k_n], refs[2 * k_n:3 * k_n], refs[3 * k_n]
        x, y, c, _ = _place()
        copies = []
        for k in range(k_n):
            stage[k][...] = ins[k][...].astype(BF)
            copies.append(pltpu.make_async_copy(stage[k], outs[k].at[_dev_index(x, y, c)], sems.at[k]))
            copies[k].start()
        for cp in copies:
            cp.wait()

    return _call(
        body,
        name="prep_weights",
        in_specs=[pl.BlockSpec(memory_space=pltpu.VMEM)] * k_n,
        out_specs=[ANY_SPEC] * k_n,
        out_shape=[jax.ShapeDtypeStruct((N_DEV,) + sh.shape, BF) for sh in shards],
        scratch_shapes=[pltpu.VMEM(sh.shape, BF) for sh in shards] + [pltpu.SemaphoreType.DMA((k_n,))],
        compiler_params=_params(),
    )(*shards)


def _ag_piece(land_k, block, half, peer, send_sem, recv_sem):
    ref = land_k.at[_dev_index(*block)]
    if half is not None:
        rows = land_k.shape[1] // 2
        ref = ref.at[pl.ds(half * rows, rows)]
    return pltpu.make_async_remote_copy(
        src_ref=ref, dst_ref=ref, send_sem=send_sem, recv_sem=recv_sem, device_id=peer, device_id_type=MESH)


def _ag_plan():
    x, y, c, _ = _place()
    me, sib = (x, y, c), (x, y, 1 - c)
    xn, yn, diag = (1 - x, y, c), (x, 1 - y, c), (1 - x, 1 - y, c)
    return dict(
        relay_halves=[(me, 0, xn), (me, 1, yn)],
        others=[(me, None, sib), (me, 1, xn), (me, 0, yn)],
        relays=[(xn, 0, yn), (yn, 1, xn)],
        near=[(xn, None, sib), (yn, None, sib)],
        far=[(diag, None, sib)],
    )


def _ag_stage(land, stage, send_sems, recv_sems, act):
    copies = _ag_plan()[stage]
    n = len(copies)
    for k in range(len(land)):
        for j, (block, half, peer) in enumerate(copies):
            cp = _ag_piece(land[k], block, half, peer, send_sems.at[n * k + j], recv_sems.at[n * k + j])
            if act == "start":
                cp.start()
            else:
                cp.wait_send()
                cp.wait_recv()


def _sem_shapes(*counts):
    return [pltpu.SemaphoreType.DMA((n,)) for n in counts for _ in range(2)]


def _ag_start(first, rest):
    lands = list(first) + list(rest)
    k_n, k_first = len(lands), len(first)
    k_rest = k_n - k_first

    def body(*refs):
        land = refs[:k_n]
        sems = refs[k_n:k_n + 6]
        token = refs[-1]
        x, y, c, chips = _place()
        targets = [(x, y, 1 - c)] + [(*chip, c) for chip in chips]
        for k in range(k_first):
            for j, to in enumerate(targets):
                _ag_piece(land[k], (x, y, c), None, to, sems[0].at[4 * k + j], sems[1].at[4 * k + j]).start()
        _ag_stage(land[k_first:], "relay_halves", sems[2], sems[3], "start")
        _ag_stage(land[k_first:], "others", sems[4], sems[5], "start")
        token[...] = jnp.zeros_like(token)

    outs = pl.pallas_call(
        body,
        name="ag_start",
        in_specs=[HBM_SPEC] * k_n,
        out_specs=(*[SEM_SPEC] * 6, *[HBM_SPEC] * k_n, pl.BlockSpec(memory_space=pltpu.VMEM)),
        out_shape=(*_sem_shapes(4 * k_first, 2 * k_rest, 3 * k_rest),
                   *[pltpu.HBM(a.shape, a.dtype) for a in lands], jax.ShapeDtypeStruct((8, LANES), F32)),
        input_output_aliases={k: 6 + k for k in range(k_n)},
        compiler_params=pltpu.CompilerParams(has_side_effects=DATAFLOW),
    )(*[_hbm(a) for a in lands])
    flying = list(outs[6:6 + k_n])
    return (outs[0], outs[1], flying[:k_first]), (outs[2:6], flying[k_first:]), outs[-1]


def _ag_split_call(lands, waits, starts, after, name):
    k_n = len(lands)
    plan_sizes = dict(relay_halves=2, others=3, relays=2, near=2, far=1)
    n_in, n_out = 2 * len(waits), 2 * len(starts)

    def body(*refs):
        land = refs[:k_n]
        in_sems = refs[k_n:k_n + n_in]
        out_sems, token = refs[len(refs) - 1 - n_out:len(refs) - 1], refs[-1]
        for w, (stage, _, _) in enumerate(waits):
            _ag_stage(land, stage, in_sems[2 * w], in_sems[2 * w + 1], "wait")
            if w < len(starts):
                _ag_stage(land, starts[w], out_sems[2 * w], out_sems[2 * w + 1], "start")
        token[...] = jnp.zeros_like(token)

    outs = pl.pallas_call(
        body,
        name=name,
        in_specs=[HBM_SPEC] * k_n + [SEM_SPEC] * n_in + [ANY_SPEC],
        out_specs=(*[HBM_SPEC] * k_n, *[SEM_SPEC] * n_out, pl.BlockSpec(memory_space=pltpu.VMEM)),
        out_shape=(*[pltpu.HBM(a.shape, a.dtype) for a in lands], *_sem_shapes(*[plan_sizes[s] * k_n for s in starts]),
                   jax.ShapeDtypeStruct((8, LANES), F32)),
        input_output_aliases={k: k for k in range(k_n)},
        compiler_params=pltpu.CompilerParams(has_side_effects=DATAFLOW),
    )(*lands, *[s for _, a, b in waits for s in (a, b)], after)
    return list(outs[:k_n]), list(outs[k_n:k_n + n_out]), outs[-1]


def _ag_mid(lands, send_sems, recv_sems, after, name):
    k_n = len(lands)

    def body(*refs):
        land = refs[:k_n]
        send1, recv1 = refs[k_n], refs[k_n + 1]
        fwd_send, fwd_recv = refs[-2], refs[-1]
        x, y, c, chips = _place()
        sources = [(x, y, 1 - c)] + [(*chip, c) for chip in chips]
        for k in range(k_n):
            mine = land[k].at[_dev_index(x, y, c)]
            for j, frm in enumerate(sources):
                got = land[k].at[_dev_index(*frm)]
                cp = pltpu.make_async_remote_copy(
                    src_ref=mine, dst_ref=got, send_sem=send1.at[4 * k + j], recv_sem=recv1.at[4 * k + j], device_id=frm, device_id_type=MESH)
                cp.wait_send()
                cp.wait_recv()
                if j >= 1:
                    pltpu.make_async_remote_copy(
                        src_ref=got, dst_ref=got, send_sem=fwd_send.at[3 * k + j - 1], recv_sem=fwd_recv.at[3 * k + j - 1],
                        device_id=(x, y, 1 - c), device_id_type=MESH).start()

    outs = pl.pallas_call(
        body,
        name=name,
        in_specs=[HBM_SPEC] * k_n + [SEM_SPEC, SEM_SPEC, ANY_SPEC],
        out_specs=(*[HBM_SPEC] * k_n, SEM_SPEC, SEM_SPEC),
        out_shape=(*[pltpu.HBM(a.shape, a.dtype) for a in lands], pltpu.SemaphoreType.DMA((3 * k_n,)), pltpu.SemaphoreType.DMA((3 * k_n,))),
        input_output_aliases={k: k for k in range(k_n)},
        compiler_params=pltpu.CompilerParams(has_side_effects=DATAFLOW),
    )(*lands, send_sems, recv_sems, after)
    return list(outs[:k_n]), outs[-2], outs[-1]


def _ag_end(lands, fwd_send, fwd_recv, name):
    k_n = len(lands)

    def body(*refs):
        land = refs[:k_n]
        fsend, frecv = refs[k_n], refs[k_n + 1]
        x, y, c, chips = _place()
        for k in range(k_n):
            for j, chip in enumerate(chips):
                cp = pltpu.make_async_remote_copy(
                    src_ref=land[k].at[_dev_index(*chip, c)], dst_ref=land[k].at[_dev_index(*chip, 1 - c)],
                    send_sem=fsend.at[3 * k + j], recv_sem=frecv.at[3 * k + j], device_id=(x, y, 1 - c), device_id_type=MESH)
                cp.wait_send()
                cp.wait_recv()

    outs = pl.pallas_call(
        body,
        name=name,
        in_specs=[HBM_SPEC] * k_n + [SEM_SPEC, SEM_SPEC],
        out_specs=tuple([HBM_SPEC] * k_n),
        out_shape=tuple(pltpu.HBM(a.shape, a.dtype) for a in lands),
        input_output_aliases={k: k for k in range(k_n)},
        compiler_params=pltpu.CompilerParams(has_side_effects=DATAFLOW),
    )(*lands, fwd_send, fwd_recv)
    return list(outs)


def _chips4():
    x, y, c, others = _place()
    return x, y, c, [(x, y)] + others


def _route_sibling(j):
    x, y, c, chips = _chips4()
    return _dev_index(*chips[j], 1 - c), j, (x, y, 1 - c)


def _route_chips(j):
    x, y, c, chips = _chips4()
    return j, j, (*chips[j + 1], c)


def _route_sibling_whole(j):
    x, y, c, _ = _chips4()
    return 0, 0, (x, y, 1 - c)


def _route_chips_whole(j):
    x, y, c, chips = _chips4()
    return 0, j, (*chips[j + 1], c)


def _route_everyone(j):
    x, y, c, _ = _chips4()
    flip = [(j + 1) >> 2 & 1, (j + 1) >> 1 & 1, (j + 1) & 1]
    return 0, j, tuple(1 - v if f else v for v, f in zip((x, y, c), flip))


def _xchg_copies(routes, src, dst, send_sems, recv_sems):
    copies, sem = [], 0
    for k, (route, n) in enumerate(routes):
        for j in range(n):
            si, di, peer = route(j)
            copies.append(pltpu.make_async_remote_copy(
                src_ref=src[k].at[si], dst_ref=dst[k].at[di], send_sem=send_sems.at[sem], recv_sem=recv_sems.at[sem],
                device_id=peer, device_id_type=MESH))
            sem += 1
    return copies


def _xchg_start(exchanges, name):
    n_ex = len(exchanges)
    counts = [len(srcs) for srcs, _, _ in exchanges]
    starts = [sum(counts[:e]) for e in range(n_ex)]
    srcs = [a for ex in exchanges for a in ex[0]]
    k_n = len(srcs)
    dsts = [lax.empty((n,) + tuple(sh), a.dtype)
            for ex_srcs, shapes, routes in exchanges for sh, a, (_, n) in zip(shapes, ex_srcs, routes)]

    def body(*refs):
        src, dst = refs[:k_n], refs[k_n:2 * k_n]
        sems, token = refs[2 * k_n:2 * k_n + 2 * n_ex], refs[-1]
        for e, (_, _, routes) in enumerate(exchanges):
            mine = slice(starts[e], starts[e] + counts[e])
            for cp in _xchg_copies(routes, src[mine], dst[mine], sems[2 * e], sems[2 * e + 1]):
                cp.start()
        token[...] = jnp.zeros_like(token)

    arrays = srcs + dsts
    outs = pl.pallas_call(
        body,
        name=name,
        in_specs=[HBM_SPEC] * (2 * k_n),
        out_specs=(*[SEM_SPEC] * (2 * n_ex), *[HBM_SPEC] * (2 * k_n), pl.BlockSpec(memory_space=pltpu.VMEM)),
        out_shape=(*_sem_shapes(*[sum(n for _, n in routes) for _, _, routes in exchanges]),
                   *[pltpu.HBM(a.shape, a.dtype) for a in arrays], jax.ShapeDtypeStruct((8, LANES), F32)),
        input_output_aliases={i: 2 * n_ex + i for i in range(2 * k_n)},
        compiler_params=pltpu.CompilerParams(has_side_effects=DATAFLOW),
    )(*[_hbm(a) for a in arrays])
    flying = outs[2 * n_ex:2 * n_ex + 2 * k_n]
    return [(outs[2 * e], outs[2 * e + 1], list(flying[starts[e]:starts[e] + counts[e]]),
             list(flying[k_n + starts[e]:k_n + starts[e] + counts[e]]), outs[-1]) for e in range(n_ex)]


def _xchg_wait(send_sems, recv_sems, srcs, dsts, routes, after, name):
    k_n = len(srcs)

    def body(*refs):
        src, dst = refs[:k_n], refs[k_n:2 * k_n]
        for cp in _xchg_copies(routes, src, dst, refs[2 * k_n], refs[2 * k_n + 1]):
            cp.wait_send()
            cp.wait_recv()

    arrays = list(srcs) + list(dsts)
    outs = pl.pallas_call(
        body,
        name=name,
        in_specs=[HBM_SPEC] * (2 * k_n) + [SEM_SPEC, SEM_SPEC, ANY_SPEC],
        out_specs=tuple([HBM_SPEC] * (2 * k_n)),
        out_shape=tuple(pltpu.HBM(a.shape, a.dtype) for a in arrays),
        input_output_aliases={i: i for i in range(2 * k_n)},
        compiler_params=pltpu.CompilerParams(has_side_effects=DATAFLOW),
    )(*arrays, send_sems, recv_sems, after)
    return list(outs[:k_n]), list(outs[k_n:])


SMALL = ("norm1_gain", "gmlp_v_gain", "w_spatial", "b_spatial", "attn_sinks", "rel_bias_table", "norm2_gain", "final_gain")
LANES = 128


def _swap_plan(grads, smalls):
    srcs = list(grads) + [a[None] for a in smalls]
    shapes = [g.shape[1:] for g in grads] + [a.shape for a in smalls]
    return srcs, shapes, [(_route_sibling, 4)] * len(grads) + [(_route_sibling_whole, 1)] * len(smalls)


def _swap_start(grads, smalls, tag):
    plan = _swap_plan(grads, smalls)
    return _xchg_start([plan], f"rs_{tag}_swap_start")[0], plan[2], len(grads)


def _swap_sums(swap, after, tag, small_dtypes=None):
    (send1, recv1, src1, land1, _), routes, n_rs = swap
    x, y, c, chips = _chips4()
    blocks = jnp.stack([_dev_index(*chip, c) for chip in chips]).astype(jnp.int32)
    src1, land1 = _xchg_wait(send1, recv1, src1, land1, routes, after, f"rs_{tag}_swap_wait")
    return _rs_sum(src1[:n_rs], land1[:n_rs], blocks, f"rs_{tag}_sum", src1[n_rs:], land1[n_rs:],
                   small_dtypes or [F32] * (len(src1) - n_rs))


def _chips_start(sums, small_sums, tag, everyone=(), then_swap=None):
    sends = [ps[1] for ps in sums] + list(small_sums) + [a[None] for a in everyone]
    routes = [(_route_chips, 3)] * len(sums) + [(_route_chips_whole, 3)] * len(small_sums) + [(_route_everyone, 7)] * len(everyone)
    plan = (sends, [a.shape[1:] for a in sends], routes)
    if then_swap is None:
        return _xchg_start([plan], f"rs_{tag}_chips_start")[0], routes
    swap_plan = _swap_plan(then_swap, [])
    chips, swap = _xchg_start([plan, swap_plan], f"rs_{tag}_chips_start_next_swap_start")
    return (chips, routes), (swap, swap_plan[2], len(then_swap))


def _chips_wait(chips, after, tag):
    (send2, recv2, src2, land2, _), routes = chips
    return _xchg_wait(send2, recv2, src2, land2, routes, after, f"rs_{tag}_chips_wait")


def kernel(x, p, norm1_gain, w_in, gmlp_v_gain, w_spatial, b_spatial, attn_sinks, rel_bias_table, w_out, norm2_gain, w_ff1, w_ff2, w_ple_proj, w_ple_gate, final_gain, loss_target, m_norm1_gain, m_w_in, m_gmlp_v_gain, m_w_spatial, m_b_spatial, m_attn_sinks, m_rel_bias_table, m_w_out, m_norm2_gain, m_w_ff1, m_w_ff2, m_w_ple_proj, m_w_ple_gate, m_final_gain, v_norm1_gain, v_w_in, v_gmlp_v_gain, v_w_spatial, v_b_spatial, v_attn_sinks, v_rel_bias_table, v_w_out, v_norm2_gain, v_w_ff1, v_w_ff2, v_w_ple_proj, v_w_ple_gate, v_final_gain):
    args = dict(locals())
    s = x.shape[1]
    big = ("w_in", "w_out", "w_ff1", "w_ff2", "w_ple_proj", "w_ple_gate")

    x2, p2, t2 = x.reshape(s, D_MODEL), p.reshape(s, PLE_DIM), loss_target.reshape(s, D_MODEL)
    g1, gv, w_sp, b_sp, sinks, table, g2, gf = (args[n] for n in SMALL)
    bucket = jnp.asarray(_bucket_table())
    b_col = b_sp.reshape(GROUPS, BLOCK, 1)

    def shard(name):
        return args[name][0].T if name.endswith("w_in") else args[name][0]

    lands = _prep_weights([shard(n) for n in big])
    (send_in, recv_in, fly_in), (rest_sems, fly_rest), token = _ag_start(lands[:1], lands[1:])
    mid_in, fwd_send_in, fwd_recv_in = _ag_mid(fly_in, send_in, recv_in, token, "ag_mid_w_in")
    g_in = _ag_end(mid_in, fwd_send_in, fwd_recv_in, "ag_end_w_in")[0]
    full_in = g_in.reshape(D_IN, D_MODEL)

    zuv, qkv, hn1 = _in_proj(x2, g1, full_in)
    fly_rest, relay_sems, relayed = _ag_split_call(
        fly_rest, [("relay_halves", *rest_sems[:2])], ["relays"], zuv, "ag_relay")
    mix, *saved = _mixer_fwd(zuv, qkv, gv, w_sp[0], b_col, sinks, table.T, bucket, after=relayed)
    fly_rest, fwd_sems, _ = _ag_split_call(
        fly_rest, [("others", *rest_sems[2:]), ("relays", *relay_sems)], ["near", "far"], mix, "ag_mid_rest")
    g_out, g_ff1, g_ff2, g_proj, g_gate = _ag_split_call(
        fly_rest, [("near", *fwd_sems[:2]), ("far", *fwd_sems[2:])], [], mix, "ag_end_rest")[0]
    full_out, full_ff2, full_gate = g_out.reshape(D_MODEL, D_MODEL), g_ff2.reshape(D_FF, D_MODEL), g_gate.reshape(D_MODEL, D_MODEL)
    full_proj = g_proj.transpose(1, 0, 2).reshape(PLE_DIM, D_MODEL)

    tail_small, dh1, dmix, hn2, a, df, dh2, gw_out, gw_proj, gw_gate = _tail(
        x2, mix, p2, t2, g2, gf.reshape(1, D_MODEL), full_out, g_ff1, full_ff2, full_gate, full_proj)
    gw_out, gw_gate = (g.reshape(N_DEV, D_MODEL // N_DEV, D_MODEL) for g in (gw_out, gw_gate))

    few_names = ("w_out", "w_ple_proj", "w_ple_gate")
    ff1_names = ("w_ff1",) + few_names
    gw_ff1 = _wgrad(hn2, df, D_MODEL, 2 * D_FF // N_DEV, "wgrad_ff1", peer_cols=D_FF // N_DEV)
    ff1_swap = _swap_start([gw_ff1, gw_out, gw_proj, gw_gate], [tail_small], "ff1")
    gw_ff2 = _wgrad(a, dh2, 1024, D_MODEL, "wgrad_ff2", after=ff1_swap[0][4]).reshape(N_DEV, D_FF // N_DEV, D_MODEL)
    ff1_sums, ff1_small = _swap_sums(ff1_swap, gw_ff2, "ff1")
    ff1_chips, ff2_swap = _chips_start(ff1_sums, ff1_small, "ff1", then_swap=[gw_ff2])
    dzm, dkv, d_gv, d_wsp, d_bsp, d_attn = _mixer_bwd(
        zuv, qkv, dmix, saved, gv, w_sp[0], b_col, bucket, after=ff2_swap[0][4])
    ff2_sums, _ = _swap_sums(ff2_swap, dzm, "ff2")
    ff2_chips = _chips_start(ff2_sums, [], "ff2")
    gw_in = _wgrad_in(dzm, dkv, hn1, after=ff2_chips[0][4]).reshape(N_DEV, D_IN // N_DEV, D_MODEL)
    in_swap = _swap_start([gw_in], [d_gv, d_wsp, d_bsp, d_attn], "in")
    dx, d_g1 = _in_bwd(x2, dh1, dzm, dkv, g1, full_in, after=in_swap[0][4])
    grad_x = dx.reshape(x.shape)
    in_sums, in_small = _swap_sums(in_swap, dx, "in", small_dtypes=[F32, BF, F32, F32])
    in_chips = _chips_start(in_sums, in_small, "in", everyone=[d_g1])

    grads, deltas, new_m, new_v = {}, {}, {}, {}
    after, small_own, small_recv = in_chips[0][4], [], []
    for chips, sums, names, tag in ((ff1_chips, ff1_sums, ff1_names, "ff1"), (ff2_chips, ff2_sums, ("w_ff2",), "ff2"),
                                    (in_chips, in_sums, ("w_in",), "in")):
        sent, recvs = _chips_wait(chips, after, tag)
        small_own += sent[len(names):]
        small_recv += recvs[len(names):]
        tiled = [n for n in names if n not in few_names]
        done = [_final_adamw(part, recv, shard(n), shard("m_" + n), shard("v_" + n), "adamw_" + n)
                for n, (part, _), recv in zip(tiled, sums, recvs)]
        few = names[len(tiled):]
        if few:
            done += zip(*_final_adamw_few(
                [ps[0] for ps in sums[len(tiled):]], recvs[len(tiled):len(names)], [shard(n) for n in few],
                [shard("m_" + n) for n in few], [shard("v_" + n) for n in few], "adamw_few"))
        for n, results in zip(names, done):
            for dst, arr in zip((grads, deltas, new_m, new_v), results):
                dst[n] = (arr.T if n == "w_in" else arr)[None]
            after = results[1]

    views = {"w_spatial": (GROUPS, BLOCK, BLOCK), "b_spatial": (GROUPS, BLOCK), "final_gain": (1, D_MODEL)}
    def view(name, base):
        return args[name].T if base == "rel_bias_table" else args[name].reshape(views.get(base, args[name].shape))

    small_in = [[view(pre + n, n) for n in SMALL] for pre in ("", "m_", "v_")]
    loss, *small_out = _adamw_small(small_own[:-1], small_recv[:-1], small_own[-1], small_recv[-1], *small_in, after=after)
    for dst, arrays in zip((grads, deltas, new_m, new_v), small_out):
        for n, arr in zip(SMALL, arrays):
            dst[n] = arr.T if n == "rel_bias_table" else arr.reshape(args[n].shape)
    loss = loss[0, 0]

    order = ("norm1_gain", "w_in", "gmlp_v_gain", "w_spatial", "b_spatial", "attn_sinks", "rel_bias_table", "w_out",
             "norm2_gain", "w_ff1", "w_ff2", "w_ple_proj", "w_ple_gate", "final_gain")
    return (loss, grad_x, *[grads[n] for n in order], *[deltas[n] for n in order],
            *[new_m[n] for n in order], *[new_v[n] for n in order])
```

```python
import math

import numpy as np
import jax
import jax.numpy as jnp
from jax import lax
from jax.experimental import pallas as pl
from jax.experimental.pallas import tpu as pltpu

F32 = jnp.float32
BF = jnp.bfloat16
MESH = pl.DeviceIdType.MESH
N_DEV = 8

D_MODEL = 1024
PLE_DIM = 256
D_GMLP = 512
GROUPS = 4
GDIM = 128
BLOCK = 128
D_ATTN = 512
HEAD_DIM = 64
N_Q = 8
Q_PER_KV = 4
N_KV = N_Q // Q_PER_KV
ROWS4 = Q_PER_KV * BLOCK
D_KV = 128
D_FF = 4096
D_IN = 1792
D_MAIN = 2 * D_GMLP + D_ATTN
REL_BUCKETS = 32
EPS = 1e-6
NEG_INF = -1e30
SCALE = HEAD_DIM ** -0.5
GELU_C = math.sqrt(2.0 / math.pi)
GELU_A = 0.044715

ADAM_LR = 0.001
ADAM_B1 = 0.9
ADAM_B2 = 0.999
ADAM_EPS = 1e-08
ADAM_WD = 0.01
ADAM_STEP = 10

V7X_VMEM_LIMIT = 61440 * 1024
TOK_TILE = 256
IO_TOK_TILE = 512


def _call(body, after=None, **kw):
    if after is None:
        return pl.pallas_call(body, **kw)
    n_in = len(kw["in_specs"])

    def ordered(*refs):
        body(*refs[:n_in], *refs[n_in + 1:])

    kw["in_specs"] = list(kw["in_specs"]) + [pl.BlockSpec(memory_space=pl.ANY)]
    fn = pl.pallas_call(ordered, **kw)
    return lambda *operands: fn(*operands, after)


def _params(sem=None):
    if sem is None:
        return pltpu.CompilerParams(vmem_limit_bytes=V7X_VMEM_LIMIT)
    return pltpu.CompilerParams(dimension_semantics=sem, vmem_limit_bytes=V7X_VMEM_LIMIT)


def _nn(a, b):
    return jnp.dot(a, b, preferred_element_type=F32)


def _nt(a, b):
    return lax.dot_general(a, b, (((1,), (1,)), ((), ())), preferred_element_type=F32)


def _tn(a, b):
    return lax.dot_general(a, b, (((0,), (0,)), ((), ())), preferred_element_type=F32)


def _gelu_tanh(x):
    return jnp.tanh(GELU_C * (x + GELU_A * (x * x * x)))


def _gelu(x, t):
    return x * (0.5 * (1.0 + t))


def _gelu_and_grad(x, t):
    cdf = 0.5 * (1.0 + t)
    return x * cdf, cdf + 0.5 * x * (1.0 - t * t) * (GELU_C * (1.0 + 3.0 * GELU_A * (x * x)))


def _rms_scale(x):
    return lax.rsqrt(jnp.mean(x * x, axis=-1, keepdims=True) + EPS)


def _rms_bwd(dxn, x, r):
    return r * dxn - x * ((r * r * r) * jnp.mean(dxn * x, axis=-1, keepdims=True))


def _bucket_table():
    a = np.arange(BLOCK)[:, None]
    j = np.arange(2 * BLOCK)[None, :]
    n = BLOCK + a - j
    valid = (n >= 0) & (n < BLOCK)
    nc = np.maximum(n, 0)
    max_exact = REL_BUCKETS // 2
    nf = np.maximum(nc, 1).astype(np.float32)
    large = max_exact + (
        np.log(nf / np.float32(max_exact)) / np.float32(math.log(BLOCK / max_exact)) * np.float32(REL_BUCKETS - max_exact)
    ).astype(np.int32)
    large = np.minimum(large, REL_BUCKETS - 1)
    bucket = np.where(nc < max_exact, nc, large)
    return np.where(valid, bucket, -1).astype(np.int32)


def _in_proj(x, g1, w_in_t):
    s = x.shape[0]
    tm = min(IO_TOK_TILE, s)

    def body(x_ref, g_ref, w_ref, zuv_ref, qkv_ref, hn_ref):
        xv = x_ref[...]
        hn = ((xv * _rms_scale(xv)) * g_ref[...]).astype(BF)
        hn_ref[...] = hn
        z = _nt(hn, w_ref[...])
        zuv_ref[...] = z[:, : 2 * D_GMLP]
        qkv_ref[...] = z[:, 2 * D_GMLP:].astype(BF)

    return _call(
        body,
        name="in_proj",
        grid=(s // tm,),
        in_specs=[
            pl.BlockSpec((tm, D_MODEL), lambda i: (i, 0)),
            pl.BlockSpec((1, D_MODEL), lambda i: (0, 0)),
            pl.BlockSpec((D_IN, D_MODEL), lambda i: (0, 0)),
        ],
        out_specs=[
            pl.BlockSpec((tm, 2 * D_GMLP), lambda i: (i, 0)),
            pl.BlockSpec((tm, D_ATTN + 2 * D_KV), lambda i: (i, 0)),
            pl.BlockSpec((tm, D_MODEL), lambda i: (i, 0)),
        ],
        out_shape=[
            jax.ShapeDtypeStruct((s, 2 * D_GMLP), F32),
            jax.ShapeDtypeStruct((s, D_ATTN + 2 * D_KV), BF),
            jax.ShapeDtypeStruct((s, D_MODEL), BF),
        ],
        compiler_params=_params(("arbitrary",)),
    )(x, g1, w_in_t)


def _head_rows(h):
    kh, g = divmod(h, Q_PER_KV)
    return kh, slice(g * BLOCK, (g + 1) * BLOCK)


def _build_bias(bias_ref, bucket_ref, table_ref):
    bucket = bucket_ref[...]
    for h in range(N_Q):
        acc = jnp.zeros((BLOCK, 2 * BLOCK), F32)
        for b in range(REL_BUCKETS):
            acc = jnp.where(bucket == b, table_ref[h, b], acc)
        kh, rows = _head_rows(h)
        bias_ref[kh, rows, :] = acc


def _window_masks(i):
    row = lax.broadcasted_iota(jnp.int32, (ROWS4, BLOCK), 0) & (BLOCK - 1)
    col = lax.broadcasted_iota(jnp.int32, (ROWS4, BLOCK), 1)
    return (col > row) & (i > 0), col <= row


def _stack_heads(ref, kh, offset):
    first = offset + kh * Q_PER_KV * HEAD_DIM
    return jnp.concatenate(
        [ref[:, first + g * HEAD_DIM: first + (g + 1) * HEAD_DIM].astype(BF) for g in range(Q_PER_KV)], axis=0)


def _stack_sinks(sink_ref, kh):
    return jnp.concatenate([jnp.full((BLOCK, 1), sink_ref[0, kh * Q_PER_KV + g], F32) for g in range(Q_PER_KV)], axis=0)


def _tril_bf16(w_ref, g):
    row = lax.broadcasted_iota(jnp.int32, (BLOCK, BLOCK), 0)
    col = lax.broadcasted_iota(jnp.int32, (BLOCK, BLOCK), 1)
    return jnp.where(col <= row, w_ref[g], 0.0).astype(BF)


def _attn_probs(q_h, k_prev, k_cur, bias_h, sink, valid_prev, valid_cur):
    l_prev = jnp.where(valid_prev, _nt(q_h, k_prev) * SCALE + bias_h[:, :BLOCK], NEG_INF)
    l_cur = jnp.where(valid_cur, _nt(q_h, k_cur) * SCALE + bias_h[:, BLOCK:], NEG_INF)
    m = jnp.maximum(jnp.maximum(jnp.max(l_prev, axis=-1, keepdims=True), jnp.max(l_cur, axis=-1, keepdims=True)), sink)
    e_prev = jnp.exp(l_prev - m)
    e_cur = jnp.exp(l_cur - m)
    e_sink = jnp.exp(sink - m)
    denom = jnp.sum(e_prev, axis=-1, keepdims=True) + jnp.sum(e_cur, axis=-1, keepdims=True) + e_sink
    return e_prev / denom, e_cur / denom, e_sink / denom


def _mixer_specs(nb):
    cl = lambda i: jnp.minimum(i, nb - 1)
    return [
        pl.BlockSpec((BLOCK, 2 * D_GMLP), lambda i: (cl(i), 0)),
        pl.BlockSpec((BLOCK, D_ATTN), lambda i: (cl(i), 0)),
        pl.BlockSpec((BLOCK, 2 * D_KV), lambda i: (cl(i), D_ATTN // (2 * D_KV))),
        pl.BlockSpec((BLOCK, 2 * D_KV), lambda i: (jnp.maximum(cl(i) - 1, 0), D_ATTN // (2 * D_KV))),
        pl.BlockSpec((1, D_GMLP), lambda i: (0, 0)),
        pl.BlockSpec((GROUPS, BLOCK, BLOCK), lambda i: (0, 0, 0)),
        pl.BlockSpec((GROUPS, BLOCK, 1), lambda i: (0, 0, 0)),
        pl.BlockSpec(memory_space=pltpu.SMEM),
        pl.BlockSpec(memory_space=pltpu.SMEM),
        pl.BlockSpec((BLOCK, 2 * BLOCK), lambda i: (0, 0)),
    ]


def _mixer_fwd(zuv, qkv, gv, w_sp, b_sp, sinks, table, bucket, after=None):
    s = zuv.shape[0]
    nb = s // BLOCK

    def body(zuv_ref, q_ref, kvc_ref, kvp_ref, gv_ref, w_ref, b_ref, sink_ref, table_ref, bucket_ref,
             mix_ref, tanh_ref, prob_ref, psink_ref, bias_ref):
        i = pl.program_id(0)

        @pl.when(i == 0)
        def _():
            _build_bias(bias_ref, bucket_ref, table_ref)

        t = _gelu_tanh(zuv_ref[...])
        tanh_ref[...] = t
        u = _gelu(zuv_ref[:, :D_GMLP], t[:, :D_GMLP])
        vg = _gelu(zuv_ref[:, D_GMLP:], t[:, D_GMLP:])
        for g in range(GROUPS):
            sl = slice(g * GDIM, (g + 1) * GDIM)
            vg_g = vg[:, sl]
            vn = ((vg_g * _rms_scale(vg_g)) * gv_ref[:, sl]).astype(BF)
            sv = _nn(_tril_bf16(w_ref, g), vn) + b_ref[g]
            mix_ref[:, sl] = (u[:, sl] * sv).astype(BF)

        valid_prev, valid_cur = _window_masks(i)
        for kh in range(N_KV):
            ksl = slice(kh * HEAD_DIM, (kh + 1) * HEAD_DIM)
            vsl = slice(D_KV + kh * HEAD_DIM, D_KV + (kh + 1) * HEAD_DIM)
            q4 = _stack_heads(q_ref, kh, 0)
            p_prev, p_cur, p_sink = _attn_probs(
                q4, kvp_ref[:, ksl], kvc_ref[:, ksl], bias_ref[kh], _stack_sinks(sink_ref, kh), valid_prev, valid_cur)
            prob_ref[0, kh, :, :BLOCK] = p_prev
            prob_ref[0, kh, :, BLOCK:] = p_cur
            psink_ref[0, kh] = jnp.broadcast_to(p_sink, (ROWS4, LANES))
            o4 = _nn(p_prev.astype(BF), kvp_ref[:, vsl]) + _nn(p_cur.astype(BF), kvc_ref[:, vsl])
            for g in range(Q_PER_KV):
                first = D_GMLP + (kh * Q_PER_KV + g) * HEAD_DIM
                mix_ref[:, first:first + HEAD_DIM] = o4[g * BLOCK:(g + 1) * BLOCK].astype(BF)

    return _call(
        body,
        name="mixer_fwd",
        after=after,
        grid=(nb,),
        in_specs=_mixer_specs(nb),
        out_specs=[
            pl.BlockSpec((BLOCK, D_MODEL), lambda i: (i, 0)),
            pl.BlockSpec((BLOCK, 2 * D_GMLP), lambda i: (i, 0)),
            pl.BlockSpec((1, N_KV, ROWS4, 2 * BLOCK), lambda i: (i, 0, 0, 0)),
            pl.BlockSpec((1, N_KV, ROWS4, LANES), lambda i: (i, 0, 0, 0)),
        ],
        out_shape=[
            jax.ShapeDtypeStruct((s, D_MODEL), BF),
            jax.ShapeDtypeStruct((s, 2 * D_GMLP), F32),
            jax.ShapeDtypeStruct((nb, N_KV, ROWS4, 2 * BLOCK), F32),
            jax.ShapeDtypeStruct((nb, N_KV, ROWS4, LANES), F32),
        ],
        scratch_shapes=[pltpu.VMEM((N_KV, ROWS4, 2 * BLOCK), F32)],
        compiler_params=_params(("arbitrary",)),
    )(zuv, qkv, qkv, qkv, gv, w_sp, b_sp, sinks, table, bucket)


def _tail(x, mix, p, t, g2, gf, w_out, w_ff1, w_ff2, w_gate, w_proj):
    s = x.shape[0]
    tm = min(TOK_TILE, s)
    n_ff = w_ff1.shape[0]
    fc = D_FF // n_ff
    pc = D_MODEL // N_DEV

    def body(x_ref, mix_ref, p_ref, t_ref, g2_ref, gf_ref, wo_ref, w1_ref, w2_ref, wg_ref, wp_ref,
             small_ref, dh1_ref, dmix_ref, hn2_ref, a_ref, df_ref, dh2_ref, gwo_ref, gwp_ref, gwg_ref, f_ref):
        i = pl.program_id(0)

        @pl.when(i == 0)
        def _():
            small_ref[...] = jnp.zeros_like(small_ref)
            gwo_ref[...] = jnp.zeros_like(gwo_ref)
            gwp_ref[...] = jnp.zeros_like(gwp_ref)
            gwg_ref[...] = jnp.zeros_like(gwg_ref)

        h1 = x_ref[...] + _nn(mix_ref[...], wo_ref[...])
        dh1_ref[...] = h1
        r2 = _rms_scale(h1)
        hn2_ref[...] = ((h1 * r2) * g2_ref[...]).astype(BF)
        h2 = h1
        for c in range(n_ff):
            f = _nn(hn2_ref[...], w1_ref[c])
            f_ref[:, c * fc:(c + 1) * fc] = f
            a = jnp.square(jnp.maximum(f, 0.0)).astype(BF)
            a_ref[:, c * fc:(c + 1) * fc] = a
            h2 = h2 + _nn(a, w2_ref[c * fc:(c + 1) * fc, :])
        h2b = h2.astype(BF)
        gate = jax.nn.sigmoid(_nn(h2b, wg_ref[...]))
        pb = p_ref[...].astype(BF)
        pp = _nn(pb, wp_ref[...])
        h3 = h2 + gate * pp
        rf = _rms_scale(h3)
        gf_v = gf_ref[...]
        err = (h3 * rf) * gf_v - t_ref[...]
        small_ref[2:3, :] += jnp.sum(jnp.sum(err * err, axis=-1, keepdims=True), axis=0, keepdims=True) * (0.5 / D_MODEL)

        dy = err * (1.0 / D_MODEL)
        small_ref[1:2, :] += jnp.sum(dy * (h3 * rf), axis=0, keepdims=True)
        dh3 = _rms_bwd(dy * gf_v, h3, rf)
        gw_proj = _tn(pb, (dh3 * gate).astype(BF))
        for q in range(N_DEV):
            gwp_ref[q] += gw_proj[:, q * pc:(q + 1) * pc]
        dgl = ((dh3 * pp) * (gate * (1.0 - gate))).astype(BF)
        gwg_ref[...] += _tn(h2b, dgl)
        dh2 = dh3 + _nt(dgl, wg_ref[...])
        dmix_ref[...] = dh2
        dh2_ref[...] = dh2.astype(BF)
        dhn2 = jnp.zeros((tm, D_MODEL), F32)
        for c in range(n_ff):
            da = _nt(dh2_ref[...], w2_ref[c * fc:(c + 1) * fc, :])
            df = (da * (2.0 * jnp.maximum(f_ref[:, c * fc:(c + 1) * fc], 0.0))).astype(BF)
            df_ref[:, c * fc:(c + 1) * fc] = df
            dhn2 = dhn2 + _nt(df, w1_ref[c])
        h1 = dh1_ref[...]
        small_ref[0:1, :] += jnp.sum(dhn2 * (h1 * r2), axis=0, keepdims=True)
        dh1 = dmix_ref[...] + _rms_bwd(dhn2 * g2_ref[...], h1, r2)
        dh1_ref[...] = dh1
        dh1b = dh1.astype(BF)
        gwo_ref[...] += _tn(mix_ref[...], dh1b)
        dmix_ref[...] = _nt(dh1b, wo_ref[...])

    tile = lambda cols: pl.BlockSpec((tm, cols), lambda i: (i, 0))
    whole = lambda shape: pl.BlockSpec(shape, lambda i: (0,) * len(shape), pipeline_mode=pl.Buffered(1))
    total = lambda shape: pl.BlockSpec(shape, lambda i: (0,) * len(shape))
    row = pl.BlockSpec((1, D_MODEL), lambda i: (0, 0))
    act = lambda cols, dt: jax.ShapeDtypeStruct((s, cols), dt)
    gw_shapes = [(D_MODEL, D_MODEL), (N_DEV, PLE_DIM, pc), (D_MODEL, D_MODEL)]
    return _call(
        body,
        name="tail",
        grid=(s // tm,),
        in_specs=[tile(D_MODEL), tile(D_MODEL), tile(PLE_DIM), tile(D_MODEL), row, row,
                  whole(w_out.shape), whole(w_ff1.shape), whole(w_ff2.shape), whole(w_gate.shape), whole(w_proj.shape)],
        out_specs=[total((8, D_MODEL)), tile(D_MODEL), tile(D_MODEL), tile(D_MODEL), tile(D_FF), tile(D_FF), tile(D_MODEL),
                   *[total(shape) for shape in gw_shapes]],
        out_shape=[jax.ShapeDtypeStruct((8, D_MODEL), F32),
                   act(D_MODEL, F32), act(D_MODEL, F32), act(D_MODEL, BF), act(D_FF, BF), act(D_FF, BF), act(D_MODEL, BF),
                   *[jax.ShapeDtypeStruct(shape, F32) for shape in gw_shapes]],
        scratch_shapes=[pltpu.VMEM((tm, D_FF), F32)],
        compiler_params=_params(("arbitrary",)),
    )(x, mix, p, t, g2, gf, w_out, w_ff1, w_ff2, w_gate, w_proj)


def _mixer_bwd(zuv, qkv, dmix, saved, gv, w_sp, b_sp, bucket, after=None):
    s = zuv.shape[0]
    nb = s // BLOCK

    def body(zuv_ref, q_ref, kvc_ref, kvp_ref, gv_ref, w_ref, b_ref, bucket_ref, dmix_ref, tanh_ref, prob_ref, psink_ref,
             dzm_ref, dkv_ref, dgv_ref, dw_ref, db_ref, dattn_ref,
             dbias_ref, carry_ref, dsink_acc, db_acc):
        i = pl.program_id(0)

        @pl.when(i == 0)
        def _():
            dbias_ref[...] = jnp.zeros_like(dbias_ref)
            carry_ref[...] = jnp.zeros_like(carry_ref)
            dsink_acc[...] = jnp.zeros_like(dsink_acc)
            dgv_ref[...] = jnp.zeros_like(dgv_ref)
            dw_ref[...] = jnp.zeros_like(dw_ref)
            db_acc[...] = jnp.zeros_like(db_acc)

        @pl.when(i < nb)
        def _():
            u, du_dz = _gelu_and_grad(zuv_ref[:, :D_GMLP], tanh_ref[:, :D_GMLP])
            vg, dvg_dz = _gelu_and_grad(zuv_ref[:, D_GMLP:], tanh_ref[:, D_GMLP:])
            for g in range(GROUPS):
                sl = slice(g * GDIM, (g + 1) * GDIM)
                vg_g = vg[:, sl]
                rg = _rms_scale(vg_g)
                vhat = vg_g * rg
                gain = gv_ref[:, sl]
                vn = (vhat * gain).astype(BF)
                w_g = _tril_bf16(w_ref, g)
                sv = _nn(w_g, vn) + b_ref[g]
                dmix_a = dmix_ref[:, sl]
                dsv = dmix_a * u[:, sl]
                dsvb = dsv.astype(BF)
                db_acc[g] += jnp.sum(dsv, axis=-1, keepdims=True)
                dw_ref[g] += _nt(dsvb, vn)
                dvn = _tn(w_g, dsvb)
                dgv_ref[:, sl] += jnp.sum(dvn * vhat, axis=0, keepdims=True)
                dvg = _rms_bwd(dvn * gain, vg_g, rg)
                dzm_ref[:, sl] = ((dmix_a * sv) * du_dz[:, sl]).astype(BF)
                dzm_ref[:, D_GMLP + g * GDIM: D_GMLP + (g + 1) * GDIM] = (dvg * dvg_dz[:, sl]).astype(BF)

            for kh in range(N_KV):
                ksl = slice(kh * HEAD_DIM, (kh + 1) * HEAD_DIM)
                vsl = slice(D_KV + kh * HEAD_DIM, D_KV + (kh + 1) * HEAD_DIM)
                k_prev, k_cur = kvp_ref[:, ksl], kvc_ref[:, ksl]
                v_prev, v_cur = kvp_ref[:, vsl], kvc_ref[:, vsl]
                q4 = _stack_heads(q_ref, kh, 0)
                p_prev, p_cur, p_sink = prob_ref[0, kh, :, :BLOCK], prob_ref[0, kh, :, BLOCK:], psink_ref[0, kh, :, 0:1]
                do4 = _stack_heads(dmix_ref, kh, D_GMLP)
                dp_prev = _nt(do4, v_prev)
                dp_cur = _nt(do4, v_cur)
                delta = jnp.sum(p_prev * dp_prev, axis=-1, keepdims=True) + jnp.sum(p_cur * dp_cur, axis=-1, keepdims=True)
                ds_prev = p_prev * (dp_prev - delta)
                ds_cur = p_cur * (dp_cur - delta)
                dsink_acc[kh] -= p_sink * delta
                dbias_ref[kh, :, :BLOCK] += ds_prev
                dbias_ref[kh, :, BLOCK:] += ds_cur
                dsb_prev = ds_prev.astype(BF)
                dsb_cur = ds_cur.astype(BF)
                dq4 = (_nn(dsb_prev, k_prev) + _nn(dsb_cur, k_cur)) * SCALE
                for g in range(Q_PER_KV):
                    first = 2 * D_GMLP + (kh * Q_PER_KV + g) * HEAD_DIM
                    dzm_ref[:, first:first + HEAD_DIM] = dq4[g * BLOCK:(g + 1) * BLOCK].astype(BF)
                dkv_ref[:, ksl] = (carry_ref[:, ksl] + _tn(dsb_prev, q4) * SCALE).astype(BF)
                dkv_ref[:, vsl] = (carry_ref[:, vsl] + _tn(p_prev.astype(BF), do4)).astype(BF)
                carry_ref[:, ksl] = _tn(dsb_cur, q4) * SCALE
                carry_ref[:, vsl] = _tn(p_cur.astype(BF), do4)

        @pl.when(i == nb)
        def _():
            dkv_ref[...] = carry_ref[...].astype(BF)
            row = lax.broadcasted_iota(jnp.int32, (BLOCK, BLOCK), 0)
            col = lax.broadcasted_iota(jnp.int32, (BLOCK, BLOCK), 1)
            for g in range(GROUPS):
                dw_ref[g] = jnp.where(col <= row, dw_ref[g], 0.0)
                db_ref[g:g + 1, :] = jnp.sum(jnp.where(col == row, db_acc[g], 0.0), axis=0, keepdims=True)
            bucket = bucket_ref[...]
            for b in range(N_Q, REL_BUCKETS):
                dattn_ref[N_Q, b] = 0.0
            for h in range(N_Q):
                kh, rows = _head_rows(h)
                dattn_ref[N_Q, h] = jnp.sum(dsink_acc[kh, rows, :])
                dbh = dbias_ref[kh, rows, :]
                for b in range(REL_BUCKETS):
                    dattn_ref[h, b] = jnp.sum(jnp.where(bucket == b, dbh, 0.0))

    cl = lambda i: jnp.minimum(i, nb - 1)
    const = lambda shape: pl.BlockSpec(shape, lambda i: (0,) * len(shape))
    return _call(
        body,
        name="mixer_bwd",
        after=after,
        grid=(nb + 1,),
        in_specs=_mixer_specs(nb)[:7] + [
            const((BLOCK, 2 * BLOCK)),
            pl.BlockSpec((BLOCK, D_MODEL), lambda i: (cl(i), 0)),
            pl.BlockSpec((BLOCK, 2 * D_GMLP), lambda i: (cl(i), 0)),
            pl.BlockSpec((1, N_KV, ROWS4, 2 * BLOCK), lambda i: (cl(i), 0, 0, 0)),
            pl.BlockSpec((1, N_KV, ROWS4, LANES), lambda i: (cl(i), 0, 0, 0)),
        ],
        out_specs=[
            pl.BlockSpec((BLOCK, D_MAIN), lambda i: (cl(i), 0)),
            pl.BlockSpec((BLOCK, 2 * D_KV), lambda i: (jnp.maximum(i - 1, 0), 0)),
            const((1, D_GMLP)),
            const((GROUPS, BLOCK, BLOCK)),
            const((GROUPS, BLOCK)),
            pl.BlockSpec(memory_space=pltpu.SMEM),
        ],
        out_shape=[
            jax.ShapeDtypeStruct((s, D_MAIN), BF),
            jax.ShapeDtypeStruct((s, 2 * D_KV), BF),
            jax.ShapeDtypeStruct((1, D_GMLP), F32),
            jax.ShapeDtypeStruct((GROUPS, BLOCK, BLOCK), F32),
            jax.ShapeDtypeStruct((GROUPS, BLOCK), F32),
            jax.ShapeDtypeStruct((N_Q + 1, REL_BUCKETS), F32),
        ],
        scratch_shapes=[
            pltpu.VMEM((N_KV, ROWS4, 2 * BLOCK), F32),
            pltpu.VMEM((BLOCK, 2 * D_KV), F32),
            pltpu.VMEM((N_KV, ROWS4, 1), F32),
            pltpu.VMEM((GROUPS, BLOCK, 1), F32),
        ],
        compiler_params=_params(("arbitrary",)),
    )(zuv, qkv, qkv, qkv, gv, w_sp, b_sp, bucket, dmix, *saved)


def _in_bwd(x, dh1, dzm, dkv, g1, w_in_t, after=None):
    s = x.shape[0]
    tm = min(IO_TOK_TILE, s)

    def body(x_ref, dh1_ref, dzm_ref, dkv_ref, g_ref, w_ref, dx_ref, dg_ref):
        @pl.when(pl.program_id(0) == 0)
        def _():
            dg_ref[...] = jnp.zeros_like(dg_ref)

        dhn = _nn(dzm_ref[...], w_ref[:D_MAIN, :]) + _nn(dkv_ref[...], w_ref[D_MAIN:, :])
        xv = x_ref[...]
        r = _rms_scale(xv)
        dg_ref[...] += jnp.sum(dhn * (xv * r), axis=0, keepdims=True)
        dx_ref[...] = dh1_ref[...] + _rms_bwd(dhn * g_ref[...], xv, r)

    tile = lambda cols: pl.BlockSpec((tm, cols), lambda i: (i, 0))
    row = pl.BlockSpec((1, D_MODEL), lambda i: (0, 0))
    return _call(
        body,
        name="in_bwd",
        after=after,
        grid=(s // tm,),
        in_specs=[tile(D_MODEL), tile(D_MODEL), tile(D_MAIN), tile(2 * D_KV), row, pl.BlockSpec((D_IN, D_MODEL), lambda i: (0, 0))],
        out_specs=[tile(D_MODEL), row],
        out_shape=[jax.ShapeDtypeStruct((s, D_MODEL), F32), jax.ShapeDtypeStruct((1, D_MODEL), F32)],
        compiler_params=_params(("arbitrary",)),
    )(x, dh1, dzm, dkv, g1, w_in_t)


def _wgrad_in(dzm, dkv, hn, after=None):
    s = hn.shape[0]
    tm = 2 * D_KV
    n_main = D_MAIN // tm

    def body(dzm_ref, dkv_ref, hn_ref, o_ref):
        i = pl.program_id(0)

        @pl.when(i < n_main)
        def _():
            o_ref[...] = _tn(dzm_ref[...], hn_ref[...])

        @pl.when(i == n_main)
        def _():
            o_ref[...] = _tn(dkv_ref[...], hn_ref[...])

    return _call(
        body,
        name="wgrad_in",
        after=after,
        grid=(n_main + 1,),
        in_specs=[
            pl.BlockSpec((s, tm), lambda i: (0, jnp.minimum(i, n_main - 1))),
            pl.BlockSpec((s, tm), lambda i: (0, 0)),
            pl.BlockSpec((s, D_MODEL), lambda i: (0, 0)),
        ],
        out_specs=pl.BlockSpec((tm, D_MODEL), lambda i: (i, 0)),
        out_shape=jax.ShapeDtypeStruct((D_IN, D_MODEL), F32),
        compiler_params=_params(("arbitrary",)),
    )(dzm, dkv, hn)


def _wgrad(a, b, tm, tn, name, peer_cols=0, after=None):
    s, m = a.shape
    n = b.shape[1]

    def body(a_ref, b_ref, o_ref, at_ref):
        @pl.when(pl.program_id(1) == 0)
        def _():
            at_ref[...] = a_ref[...].astype(BF).T

        r = _nn(at_ref[...], b_ref[...])
        if peer_cols:
            for q in range(tn // peer_cols):
                o_ref[q] = r[:, q * peer_cols:(q + 1) * peer_cols]
        else:
            o_ref[...] = r

    if peer_cols:
        out_spec = pl.BlockSpec((tn // peer_cols, tm, peer_cols), lambda i, j: (j, i, 0))
        out_shape = jax.ShapeDtypeStruct((n // peer_cols, m, peer_cols), F32)
    else:
        out_spec = pl.BlockSpec((tm, tn), lambda i, j: (i, j))
        out_shape = jax.ShapeDtypeStruct((m, n), F32)
    return _call(
        body,
        name=name,
        after=after,
        grid=(m // tm, n // tn),
        in_specs=[pl.BlockSpec((s, tm), lambda i, j: (0, i)), pl.BlockSpec((s, tn), lambda i, j: (0, j))],
        out_specs=out_spec,
        out_shape=out_shape,
        scratch_shapes=[pltpu.VMEM((tm, s), BF)],
        compiler_params=_params(("arbitrary", "arbitrary")),
    )(a, b)


def _adamw_math(w, g, m, v):
    m_new = ADAM_B1 * m + (1.0 - ADAM_B1) * g
    v_new = ADAM_B2 * v + (1.0 - ADAM_B2) * jnp.square(g)
    m_hat = m_new / (1.0 - ADAM_B1 ** ADAM_STEP)
    v_hat = v_new / (1.0 - ADAM_B2 ** ADAM_STEP)
    delta = -ADAM_LR * (m_hat / (jnp.sqrt(v_hat) + ADAM_EPS) + ADAM_WD * w)
    return delta, m_new, v_new


def _final_adamw(part, recv, w, m, v, name):
    r, c = w.shape
    tr = min(r, 512)

    def body(p_ref, r_ref, w_ref, m_ref, v_ref, g_ref, d_ref, mo_ref, vo_ref):
        g = p_ref[...]
        for j in range(3):
            g = g + r_ref[j].astype(F32)
        g_ref[...] = g
        d_ref[...], mo_ref[...], vo_ref[...] = _adamw_math(w_ref[...], g, m_ref[...], v_ref[...])

    spec = pl.BlockSpec((tr, c), lambda i: (i, 0))
    return _call(
        body,
        name=name,
        grid=(r // tr,),
        in_specs=[spec, pl.BlockSpec((3, tr, c), lambda i: (0, i, 0)), spec, spec, spec],
        out_specs=[spec] * 4,
        out_shape=[jax.ShapeDtypeStruct((r, c), F32)] * 4,
        compiler_params=_params(("arbitrary",)),
    )(part, recv, w, m, v)


def _rs_sum(gs, lands, blocks, name, smalls, small_lands, small_dtypes):
    n, n_small = len(gs), len(smalls)
    steps = pl.cdiv(gs[0].shape[1], 256)
    assert all(g.shape[1] % steps == 0 for g in gs)

    def body(blk_ref, *refs):
        ins, small_ins, outs = refs[:5 * n], refs[5 * n:5 * n + 2 * n_small], refs[5 * n + 2 * n_small:]
        for k in range(n):
            g0_ref, g1_ref, g2_ref, g3_ref, l_ref = ins[5 * k:5 * k + 5]
            outs[2 * k][...] = g0_ref[0] + l_ref[0]
            for j, gj_ref in enumerate((g1_ref, g2_ref, g3_ref)):
                outs[2 * k + 1][j] = (gj_ref[0] + l_ref[j + 1]).astype(BF)
        for q in range(n_small):
            outs[2 * n + q][...] = (small_ins[q][...] + small_ins[n_small + q][...]).astype(small_dtypes[q])

    def whole(a):
        return pl.BlockSpec(a.shape, lambda i, blk: (0,) * a.ndim)

    in_specs, out_specs, out_shape, operands = [], [], [], []
    for g, land in zip(gs, lands):
        _, r, c = g.shape
        tr = r // steps
        in_specs += [pl.BlockSpec((1, tr, c), lambda i, blk, j=j: (blk[j], i, 0)) for j in range(4)]
        in_specs.append(pl.BlockSpec((4, tr, c), lambda i, blk: (0, i, 0)))
        out_specs += [pl.BlockSpec((tr, c), lambda i, blk: (i, 0)), pl.BlockSpec((3, tr, c), lambda i, blk: (0, i, 0))]
        out_shape += [jax.ShapeDtypeStruct((r, c), F32), jax.ShapeDtypeStruct((3, r, c), BF)]
        operands += [g, g, g, g, land]
    in_specs += [whole(a) for a in (*smalls, *small_lands)]
    out_specs += [whole(a) for a in smalls]
    out_shape += [jax.ShapeDtypeStruct(a.shape, dt) for a, dt in zip(smalls, small_dtypes)]
    outs = _call(
        body,
        name=name,
        grid_spec=pltpu.PrefetchScalarGridSpec(num_scalar_prefetch=1, grid=(steps,), in_specs=in_specs, out_specs=out_specs),
        out_shape=out_shape,
        compiler_params=_params(("arbitrary",)),
    )(blocks, *operands, *smalls, *small_lands)
    return [(outs[2 * k], outs[2 * k + 1]) for k in range(n)], list(outs[2 * n:])


def _final_adamw_few(parts, recvs, ws, ms, vs, name):
    n = len(ws)

    def body(*refs):
        ins, outs = refs[:5 * n], refs[5 * n:]
        for k in range(n):
            p_ref, r_ref, w_ref, m_ref, v_ref = (ins[q * n + k] for q in range(5))
            g = p_ref[...]
            for j in range(3):
                g = g + r_ref[j].astype(F32)
            outs[k][...] = g
            outs[n + k][...], outs[2 * n + k][...], outs[3 * n + k][...] = _adamw_math(w_ref[...], g, m_ref[...], v_ref[...])

    outs = _call(
        body,
        name=name,
        out_shape=[jax.ShapeDtypeStruct(w.shape, F32) for w in ws] * 4,
        compiler_params=_params(),
    )(*parts, *recvs, *ws, *ms, *vs)
    return outs[:n], outs[n:2 * n], outs[2 * n:3 * n], outs[3 * n:]


def _adamw_small(own, recv, g1_own, g1_recv, weights, moms, vels, after):
    n_w = len(weights)

    def body(*refs):
        own_refs, recv_refs, g1_own_ref, g1_recv_ref = refs[:5], refs[5:10], refs[10], refs[11]
        w_refs, m_refs, v_refs = refs[12:12 + n_w], refs[12 + n_w:12 + 2 * n_w], refs[12 + 2 * n_w:12 + 3 * n_w]
        outs = refs[12 + 3 * n_w:]
        loss_ref, g_refs, d_refs = outs[0], outs[1:1 + n_w], outs[1 + n_w:1 + 2 * n_w]
        mo_refs, vo_refs = outs[1 + 2 * n_w:1 + 3 * n_w], outs[1 + 3 * n_w:]
        x, y, c = lax.axis_index("x"), lax.axis_index("y"), lax.axis_index("c")

        def in_place_order(values, my_place):
            acc = None
            for place in range(len(values)):
                r = place ^ my_place
                term = values[-1]
                for q in range(len(values) - 2, -1, -1):
                    term = jnp.where(r == q, values[q], term)
                acc = term if acc is None else acc + term
            return acc

        def total(k, *index):
            index = index or (slice(None),) * (len(own_refs[k].shape) - 1)
            across = [own_refs[k][(0, *index)], recv_refs[k][(1, *index)], recv_refs[k][(0, *index)], recv_refs[k][(2, *index)]]
            return in_place_order([v.astype(F32) for v in across], 2 * x + y)

        g1 = in_place_order([g1_own_ref[0]] + [g1_recv_ref[j] for j in range(N_DEV - 1)], 4 * x + 2 * y + c)
        grads = [
            g1, total(1), total(2), total(3),
            total(4, slice(N_Q, None), slice(0, N_Q)), total(4, slice(0, N_Q), slice(None)),
            total(0, slice(0, 1), slice(None)), total(0, slice(1, 2), slice(None))]
        loss_ref[...] = total(0, slice(2, 3), slice(0, 1))
        for k in range(n_w):
            g_refs[k][...] = grads[k]
            d_refs[k][...], mo_refs[k][...], vo_refs[k][...] = _adamw_math(w_refs[k][...], grads[k], m_refs[k][...], v_refs[k][...])

    shapes = [jax.ShapeDtypeStruct(w.shape, F32) for w in weights]
    outs = _call(
        body,
        name="adamw_small",
        after=after,
        in_specs=[pl.BlockSpec(memory_space=pltpu.VMEM)] * (12 + 3 * n_w),
        out_shape=[jax.ShapeDtypeStruct((1, 1), F32)] + shapes * 4,
        compiler_params=_params(),
    )(*own, *recv, g1_own, g1_recv, *weights, *moms, *vels)
    return outs[0], outs[1:1 + n_w], outs[1 + n_w:1 + 2 * n_w], outs[1 + 2 * n_w:1 + 3 * n_w], outs[1 + 3 * n_w:]


def _place():
    x, y, c = lax.axis_index("x"), lax.axis_index("y"), lax.axis_index("c")
    return x, y, c, [(1 - x, y), (x, 1 - y), (1 - x, 1 - y)]


def _dev_index(px, py, pc):
    return 4 * px + 2 * py + pc


HBM_SPEC = pl.BlockSpec(memory_space=pltpu.HBM)
SEM_SPEC = pl.BlockSpec(memory_space=pltpu.SEMAPHORE)
ANY_SPEC = pl.BlockSpec(memory_space=pl.ANY)
DATAFLOW = pltpu.SideEffectType.DATAFLOW_SIDE_EFFECTING


def _hbm(a):
    return pltpu.with_memory_space_constraint(a, pltpu.HBM)


def _ag_piece(land_k, block, half, peer, send_sem, recv_sem):
    ref = land_k.at[_dev_index(*block)]
    if half is not None:
        rows = land_k.shape[1] // 2
        ref = ref.at[pl.ds(half * rows, rows)]
    return pltpu.make_async_remote_copy(
        src_ref=ref, dst_ref=ref, send_sem=send_sem, recv_sem=recv_sem, device_id=peer, device_id_type=MESH)


def _ag_plan():
    x, y, c, _ = _place()
    me, sib = (x, y, c), (x, y, 1 - c)
    xn, yn, diag = (1 - x, y, c), (x, 1 - y, c), (1 - x, 1 - y, c)
    return dict(
        relay_halves=[(me, 0, xn), (me, 1, yn)],
        others=[(me, None, sib), (me, 1, xn), (me, 0, yn)],
        relays=[(xn, 0, yn), (yn, 1, xn)],
        near=[(xn, None, sib), (yn, None, sib)],
        far=[(diag, None, sib)],
    )


def _ag_stage(land, stage, send_sems, recv_sems, act):
    copies = _ag_plan()[stage]
    n = len(copies)
    for k in range(len(land)):
        for j, (block, half, peer) in enumerate(copies):
            cp = _ag_piece(land[k], block, half, peer, send_sems.at[n * k + j], recv_sems.at[n * k + j])
            if act == "start":
                cp.start()
            else:
                cp.wait_send()
                cp.wait_recv()


def _sem_shapes(*counts):
    return [pltpu.SemaphoreType.DMA((n,)) for n in counts for _ in range(2)]


def _ag_start(first, rest):
    shards = list(first) + list(rest)
    k_n, k_first = len(shards), len(first)
    k_rest = k_n - k_first

    def body(*refs):
        ins, sems, land = refs[:k_n], refs[k_n:k_n + 6], refs[k_n + 6:2 * k_n + 6]
        token, stage, own_sems = refs[2 * k_n + 6], refs[2 * k_n + 7:3 * k_n + 7], refs[3 * k_n + 7]
        x, y, c, chips = _place()

        def place(ks):
            copies = []
            for k in ks:
                stage[k][...] = ins[k][...].astype(BF)
                copies.append(pltpu.make_async_copy(stage[k], land[k].at[_dev_index(x, y, c)], own_sems.at[k]))
                copies[-1].start()
            for cp in copies:
                cp.wait()

        place(range(k_first))
        targets = [(x, y, 1 - c)] + [(*chip, c) for chip in chips]
        for k in range(k_first):
            for j, to in enumerate(targets):
                _ag_piece(land[k], (x, y, c), None, to, sems[0].at[4 * k + j], sems[1].at[4 * k + j]).start()
        place(range(k_first, k_n))
        _ag_stage(land[k_first:], "relay_halves", sems[2], sems[3], "start")
        _ag_stage(land[k_first:], "others", sems[4], sems[5], "start")
        token[...] = jnp.zeros_like(token)

    outs = pl.pallas_call(
        body,
        name="ag_start",
        in_specs=[pl.BlockSpec(memory_space=pltpu.VMEM)] * k_n,
        out_specs=(*[SEM_SPEC] * 6, *[HBM_SPEC] * k_n, pl.BlockSpec(memory_space=pltpu.VMEM)),
        out_shape=(*_sem_shapes(4 * k_first, 2 * k_rest, 3 * k_rest),
                   *[pltpu.HBM((N_DEV,) + a.shape, BF) for a in shards], jax.ShapeDtypeStruct((8, LANES), F32)),
        scratch_shapes=[pltpu.VMEM(a.shape, BF) for a in shards] + [pltpu.SemaphoreType.DMA((k_n,))],
        compiler_params=pltpu.CompilerParams(has_side_effects=DATAFLOW, vmem_limit_bytes=V7X_VMEM_LIMIT),
    )(*shards)
    flying = list(outs[6:6 + k_n])
    return (outs[0], outs[1], flying[:k_first]), (outs[2:6], flying[k_first:]), outs[-1]


def _ag_split_call(lands, waits, starts, after, name):
    k_n = len(lands)
    plan_sizes = dict(relay_halves=2, others=3, relays=2, near=2, far=1)
    n_in, n_out = 2 * len(waits), 2 * len(starts)

    def body(*refs):
        land = refs[:k_n]
        in_sems = refs[k_n:k_n + n_in]
        out_sems, token = refs[len(refs) - 1 - n_out:len(refs) - 1], refs[-1]
        for w, (stage, _, _) in enumerate(waits):
            _ag_stage(land, stage, in_sems[2 * w], in_sems[2 * w + 1], "wait")
            if w < len(starts):
                _ag_stage(land, starts[w], out_sems[2 * w], out_sems[2 * w + 1], "start")
        token[...] = jnp.zeros_like(token)

    outs = pl.pallas_call(
        body,
        name=name,
        in_specs=[HBM_SPEC] * k_n + [SEM_SPEC] * n_in + [ANY_SPEC],
        out_specs=(*[HBM_SPEC] * k_n, *[SEM_SPEC] * n_out, pl.BlockSpec(memory_space=pltpu.VMEM)),
        out_shape=(*[pltpu.HBM(a.shape, a.dtype) for a in lands], *_sem_shapes(*[plan_sizes[s] * k_n for s in starts]),
                   jax.ShapeDtypeStruct((8, LANES), F32)),
        input_output_aliases={k: k for k in range(k_n)},
        compiler_params=pltpu.CompilerParams(has_side_effects=DATAFLOW),
    )(*lands, *[s for _, a, b in waits for s in (a, b)], after)
    return list(outs[:k_n]), list(outs[k_n:k_n + n_out]), outs[-1]


def _ag_mid(lands, send_sems, recv_sems, after, name):
    k_n = len(lands)

    def body(*refs):
        land = refs[:k_n]
        send1, recv1 = refs[k_n], refs[k_n + 1]
        fwd_send, fwd_recv = refs[-2], refs[-1]
        x, y, c, chips = _place()
        sources = [(x, y, 1 - c)] + [(*chip, c) for chip in chips]
        for k in range(k_n):
            mine = land[k].at[_dev_index(x, y, c)]
            for j, frm in enumerate(sources):
                got = land[k].at[_dev_index(*frm)]
                cp = pltpu.make_async_remote_copy(
                    src_ref=mine, dst_ref=got, send_sem=send1.at[4 * k + j], recv_sem=recv1.at[4 * k + j], device_id=frm, device_id_type=MESH)
                cp.wait_send()
                cp.wait_recv()
                if j >= 1:
                    pltpu.make_async_remote_copy(
                        src_ref=got, dst_ref=got, send_sem=fwd_send.at[3 * k + j - 1], recv_sem=fwd_recv.at[3 * k + j - 1],
                        device_id=(x, y, 1 - c), device_id_type=MESH).start()

    outs = pl.pallas_call(
        body,
        name=name,
        in_specs=[HBM_SPEC] * k_n + [SEM_SPEC, SEM_SPEC, ANY_SPEC],
        out_specs=(*[HBM_SPEC] * k_n, SEM_SPEC, SEM_SPEC),
        out_shape=(*[pltpu.HBM(a.shape, a.dtype) for a in lands], pltpu.SemaphoreType.DMA((3 * k_n,)), pltpu.SemaphoreType.DMA((3 * k_n,))),
        input_output_aliases={k: k for k in range(k_n)},
        compiler_params=pltpu.CompilerParams(has_side_effects=DATAFLOW),
    )(*lands, send_sems, recv_sems, after)
    return list(outs[:k_n]), outs[-2], outs[-1]


def _ag_end(lands, fwd_send, fwd_recv, name):
    k_n = len(lands)

    def body(*refs):
        land = refs[:k_n]
        fsend, frecv = refs[k_n], refs[k_n + 1]
        x, y, c, chips = _place()
        for k in range(k_n):
            for j, chip in enumerate(chips):
                cp = pltpu.make_async_remote_copy(
                    src_ref=land[k].at[_dev_index(*chip, c)], dst_ref=land[k].at[_dev_index(*chip, 1 - c)],
                    send_sem=fsend.at[3 * k + j], recv_sem=frecv.at[3 * k + j], device_id=(x, y, 1 - c), device_id_type=MESH)
                cp.wait_send()
                cp.wait_recv()

    outs = pl.pallas_call(
        body,
        name=name,
        in_specs=[HBM_SPEC] * k_n + [SEM_SPEC, SEM_SPEC],
        out_specs=tuple([HBM_SPEC] * k_n),
        out_shape=tuple(pltpu.HBM(a.shape, a.dtype) for a in lands),
        input_output_aliases={k: k for k in range(k_n)},
        compiler_params=pltpu.CompilerParams(has_side_effects=DATAFLOW),
    )(*lands, fwd_send, fwd_recv)
    return list(outs)


def _chips4():
    x, y, c, others = _place()
    return x, y, c, [(x, y)] + others


def _route_sibling(j):
    x, y, c, chips = _chips4()
    return _dev_index(*chips[j], 1 - c), j, (x, y, 1 - c)


def _route_chips(j):
    x, y, c, chips = _chips4()
    return j, j, (*chips[j + 1], c)


def _route_sibling_whole(j):
    x, y, c, _ = _chips4()
    return 0, 0, (x, y, 1 - c)


def _route_chips_whole(j):
    x, y, c, chips = _chips4()
    return 0, j, (*chips[j + 1], c)


def _route_everyone(j):
    x, y, c, _ = _chips4()
    flip = [(j + 1) >> 2 & 1, (j + 1) >> 1 & 1, (j + 1) & 1]
    return 0, j, tuple(1 - v if f else v for v, f in zip((x, y, c), flip))


def _xchg_copies(routes, src, dst, send_sems, recv_sems):
    copies, sem = [], 0
    for k, (route, n) in enumerate(routes):
        for j in range(n):
            si, di, peer = route(j)
            copies.append(pltpu.make_async_remote_copy(
                src_ref=src[k].at[si], dst_ref=dst[k].at[di], send_sem=send_sems.at[sem], recv_sem=recv_sems.at[sem],
                device_id=peer, device_id_type=MESH))
            sem += 1
    return copies


def _xchg_start(srcs, slot_shapes, routes, name, after=None):
    k_n = len(srcs)
    n_in = 2 * k_n + (after is not None)
    n_sem = sum(n for _, n in routes)
    dsts = [lax.empty((n,) + tuple(sh), a.dtype) for sh, a, (_, n) in zip(slot_shapes, srcs, routes)]

    def body(*refs):
        src, dst = refs[:k_n], refs[k_n:2 * k_n]
        send_sems, recv_sems, token = refs[n_in], refs[n_in + 1], refs[-1]
        for cp in _xchg_copies(routes, src, dst, send_sems, recv_sems):
            cp.start()
        token[...] = jnp.zeros_like(token)

    arrays = list(srcs) + dsts
    outs = pl.pallas_call(
        body,
        name=name,
        in_specs=[HBM_SPEC] * (2 * k_n) + [ANY_SPEC] * (n_in - 2 * k_n),
        out_specs=(SEM_SPEC, SEM_SPEC, *[HBM_SPEC] * (2 * k_n), pl.BlockSpec(memory_space=pltpu.VMEM)),
        out_shape=(pltpu.SemaphoreType.DMA((n_sem,)), pltpu.SemaphoreType.DMA((n_sem,)),
                   *[pltpu.HBM(a.shape, a.dtype) for a in arrays], jax.ShapeDtypeStruct((8, LANES), F32)),
        input_output_aliases={i: 2 + i for i in range(2 * k_n)},
        compiler_params=pltpu.CompilerParams(has_side_effects=DATAFLOW),
    )(*[_hbm(a) for a in arrays], *([] if after is None else [after]))
    return outs[0], outs[1], list(outs[2:2 + k_n]), list(outs[2 + k_n:2 + 2 * k_n]), outs[-1]


def _xchg_wait(send_sems, recv_sems, srcs, dsts, routes, after, name):
    k_n = len(srcs)

    def body(*refs):
        src, dst = refs[:k_n], refs[k_n:2 * k_n]
        for cp in _xchg_copies(routes, src, dst, refs[2 * k_n], refs[2 * k_n + 1]):
            cp.wait_send()
            cp.wait_recv()

    arrays = list(srcs) + list(dsts)
    outs = pl.pallas_call(
        body,
        name=name,
        in_specs=[HBM_SPEC] * (2 * k_n) + [SEM_SPEC, SEM_SPEC, ANY_SPEC],
        out_specs=tuple([HBM_SPEC] * (2 * k_n)),
        out_shape=tuple(pltpu.HBM(a.shape, a.dtype) for a in arrays),
        input_output_aliases={i: i for i in range(2 * k_n)},
        compiler_params=pltpu.CompilerParams(has_side_effects=DATAFLOW),
    )(*arrays, send_sems, recv_sems, after)
    return list(outs[:k_n]), list(outs[k_n:])


SMALL = ("norm1_gain", "gmlp_v_gain", "w_spatial", "b_spatial", "attn_sinks", "rel_bias_table", "norm2_gain", "final_gain")
LANES = 128


def _swap_start(grads, smalls, tag, after=None):
    srcs = list(grads) + [a[None] for a in smalls]
    shapes = [g.shape[1:] for g in grads] + [a.shape for a in smalls]
    routes = [(_route_sibling, 4)] * len(grads) + [(_route_sibling_whole, 1)] * len(smalls)
    return _xchg_start(srcs, shapes, routes, f"rs_{tag}_swap_start", after), routes, len(grads)


def _swap_sums(swap, after, tag, small_dtypes=None):
    (send1, recv1, src1, land1, _), routes, n_rs = swap
    x, y, c, chips = _chips4()
    blocks = jnp.stack([_dev_index(*chip, c) for chip in chips]).astype(jnp.int32)
    src1, land1 = _xchg_wait(send1, recv1, src1, land1, routes, after, f"rs_{tag}_swap_wait")
    return _rs_sum(src1[:n_rs], land1[:n_rs], blocks, f"rs_{tag}_sum", src1[n_rs:], land1[n_rs:],
                   small_dtypes or [F32] * (len(src1) - n_rs))


def _chips_start(sums, small_sums, tag, everyone=()):
    sends = [ps[1] for ps in sums] + list(small_sums) + [a[None] for a in everyone]
    routes = [(_route_chips, 3)] * len(sums) + [(_route_chips_whole, 3)] * len(small_sums) + [(_route_everyone, 7)] * len(everyone)
    return _xchg_start(sends, [a.shape[1:] for a in sends], routes, f"rs_{tag}_chips_start"), routes


def _chips_wait(chips, after, tag):
    (send2, recv2, src2, land2, _), routes = chips
    return _xchg_wait(send2, recv2, src2, land2, routes, after, f"rs_{tag}_chips_wait")


def kernel(x, p, norm1_gain, w_in, gmlp_v_gain, w_spatial, b_spatial, attn_sinks, rel_bias_table, w_out, norm2_gain, w_ff1, w_ff2, w_ple_proj, w_ple_gate, final_gain, loss_target, m_norm1_gain, m_w_in, m_gmlp_v_gain, m_w_spatial, m_b_spatial, m_attn_sinks, m_rel_bias_table, m_w_out, m_norm2_gain, m_w_ff1, m_w_ff2, m_w_ple_proj, m_w_ple_gate, m_final_gain, v_norm1_gain, v_w_in, v_gmlp_v_gain, v_w_spatial, v_b_spatial, v_attn_sinks, v_rel_bias_table, v_w_out, v_norm2_gain, v_w_ff1, v_w_ff2, v_w_ple_proj, v_w_ple_gate, v_final_gain):
    args = dict(locals())
    s = x.shape[1]
    big = ("w_in", "w_out", "w_ff1", "w_ff2", "w_ple_proj", "w_ple_gate")

    x2, p2, t2 = x.reshape(s, D_MODEL), p.reshape(s, PLE_DIM), loss_target.reshape(s, D_MODEL)
    g1, gv, w_sp, b_sp, sinks, table, g2, gf = (args[n] for n in SMALL)
    bucket = jnp.asarray(_bucket_table())
    b_col = b_sp.reshape(GROUPS, BLOCK, 1)

    def shard(name):
        return args[name][0].T if name.endswith("w_in") else args[name][0]

    (send_in, recv_in, fly_in), (rest_sems, fly_rest), token = _ag_start([shard(big[0])], [shard(n) for n in big[1:]])
    mid_in, fwd_send_in, fwd_recv_in = _ag_mid(fly_in, send_in, recv_in, token, "ag_mid_w_in")
    g_in = _ag_end(mid_in, fwd_send_in, fwd_recv_in, "ag_end_w_in")[0]
    full_in = g_in.reshape(D_IN, D_MODEL)

    zuv, qkv, hn1 = _in_proj(x2, g1, full_in)
    fly_rest, relay_sems, relayed = _ag_split_call(
        fly_rest, [("relay_halves", *rest_sems[:2])], ["relays"], zuv, "ag_relay")
    mix, *saved = _mixer_fwd(zuv, qkv, gv, w_sp[0], b_col, sinks, table.T, bucket, after=relayed)
    fly_rest, fwd_sems, _ = _ag_split_call(
        fly_rest, [("others", *rest_sems[2:]), ("relays", *relay_sems)], ["near", "far"], mix, "ag_mid_rest")
    g_out, g_ff1, g_ff2, g_proj, g_gate = _ag_split_call(
        fly_rest, [("near", *fwd_sems[:2]), ("far", *fwd_sems[2:])], [], mix, "ag_end_rest")[0]
    full_out, full_ff2, full_gate = g_out.reshape(D_MODEL, D_MODEL), g_ff2.reshape(D_FF, D_MODEL), g_gate.reshape(D_MODEL, D_MODEL)
    full_proj = g_proj.transpose(1, 0, 2).reshape(PLE_DIM, D_MODEL)

    tail_small, dh1, dmix, hn2, a, df, dh2, gw_out, gw_proj, gw_gate = _tail(
        x2, mix, p2, t2, g2, gf.reshape(1, D_MODEL), full_out, g_ff1, full_ff2, full_gate, full_proj)
    gw_out, gw_gate = (g.reshape(N_DEV, D_MODEL // N_DEV, D_MODEL) for g in (gw_out, gw_gate))

    few_names = ("w_out", "w_ple_proj", "w_ple_gate")
    ff1_names = ("w_ff1",) + few_names
    gw_ff1 = _wgrad(hn2, df, D_MODEL, 2 * D_FF // N_DEV, "wgrad_ff1", peer_cols=D_FF // N_DEV)
    ff1_swap = _swap_start([gw_ff1, gw_out, gw_proj, gw_gate], [tail_small], "ff1")
    gw_ff2 = _wgrad(a, dh2, 1024, D_MODEL, "wgrad_ff2", after=ff1_swap[0][4]).reshape(N_DEV, D_FF // N_DEV, D_MODEL)
    ff1_sums, ff1_small = _swap_sums(ff1_swap, gw_ff2, "ff1")
    ff1_chips = _chips_start(ff1_sums, ff1_small, "ff1")
    ff2_swap = _swap_start([gw_ff2], [], "ff2", after=ff1_chips[0][4])
    dzm, dkv, d_gv, d_wsp, d_bsp, d_attn = _mixer_bwd(
        zuv, qkv, dmix, saved, gv, w_sp[0], b_col, bucket, after=ff2_swap[0][4])
    ff2_sums, _ = _swap_sums(ff2_swap, dzm, "ff2")
    ff2_chips = _chips_start(ff2_sums, [], "ff2")
    gw_in = _wgrad_in(dzm, dkv, hn1, after=ff2_chips[0][4]).reshape(N_DEV, D_IN // N_DEV, D_MODEL)
    in_swap = _swap_start([gw_in], [d_gv, d_wsp, d_bsp, d_attn], "in")
    dx, d_g1 = _in_bwd(x2, dh1, dzm, dkv, g1, full_in, after=in_swap[0][4])
    grad_x = dx.reshape(x.shape)
    in_sums, in_small = _swap_sums(in_swap, dx, "in", small_dtypes=[F32, BF, F32, F32])
    in_chips = _chips_start(in_sums, in_small, "in", everyone=[d_g1])

    grads, deltas, new_m, new_v = {}, {}, {}, {}
    after, small_own, small_recv = in_chips[0][4], [], []
    for chips, sums, names, tag in ((ff1_chips, ff1_sums, ff1_names, "ff1"), (ff2_chips, ff2_sums, ("w_ff2",), "ff2"),
                                    (in_chips, in_sums, ("w_in",), "in")):
        sent, recvs = _chips_wait(chips, after, tag)
        small_own += sent[len(names):]
        small_recv += recvs[len(names):]
        tiled = [n for n in names if n not in few_names]
        done = [_final_adamw(part, recv, shard(n), shard("m_" + n), shard("v_" + n), "adamw_" + n)
                for n, (part, _), recv in zip(tiled, sums, recvs)]
        few = names[len(tiled):]
        if few:
            done += zip(*_final_adamw_few(
                [ps[0] for ps in sums[len(tiled):]], recvs[len(tiled):len(names)], [shard(n) for n in few],
                [shard("m_" + n) for n in few], [shard("v_" + n) for n in few], "adamw_few"))
        for n, results in zip(names, done):
            for dst, arr in zip((grads, deltas, new_m, new_v), results):
                dst[n] = (arr.T if n == "w_in" else arr)[None]
            after = results[1]

    views = {"w_spatial": (GROUPS, BLOCK, BLOCK), "b_spatial": (GROUPS, BLOCK), "final_gain": (1, D_MODEL)}
    def view(name, base):
        return args[name].T if base == "rel_bias_table" else args[name].reshape(views.get(base, args[name].shape))

    small_in = [[view(pre + n, n) for n in SMALL] for pre in ("", "m_", "v_")]
    loss, *small_out = _adamw_small(small_own[:-1], small_recv[:-1], small_own[-1], small_recv[-1], *small_in, after=after)
    for dst, arrays in zip((grads, deltas, new_m, new_v), small_out):
        for n, arr in zip(SMALL, arrays):
            dst[n] = arr.T if n == "rel_bias_table" else arr.reshape(args[n].shape)
    loss = loss[0, 0]

    order = ("norm1_gain", "w_in", "gmlp_v_gain", "w_spatial", "b_spatial", "attn_sinks", "rel_bias_table", "w_out",
             "norm2_gain", "w_ff1", "w_ff2", "w_ple_proj", "w_ple_gate", "final_gain")
    return (loss, grad_x, *[grads[n] for n in order], *[deltas[n] for n in order],
            *[new_m[n] for n in order], *[new_v[n] for n in order])
```

```python
import math

import numpy as np
import jax
import jax.numpy as jnp
from jax import lax
from jax.experimental import pallas as pl
from jax.experimental.pallas import tpu as pltpu

F32 = jnp.float32
BF = jnp.bfloat16
MESH = pl.DeviceIdType.MESH
N_DEV = 8

D_MODEL = 1024
PLE_DIM = 256
D_GMLP = 512
GROUPS = 4
GDIM = 128
BLOCK = 128
D_ATTN = 512
HEAD_DIM = 64
N_Q = 8
Q_PER_KV = 4
N_KV = N_Q // Q_PER_KV
ROWS4 = Q_PER_KV * BLOCK
D_KV = 128
D_FF = 4096
D_IN = 1792
D_MAIN = 2 * D_GMLP + D_ATTN
REL_BUCKETS = 32
EPS = 1e-6
NEG_INF = -1e30
SCALE = HEAD_DIM ** -0.5
GELU_C = math.sqrt(2.0 / math.pi)
GELU_A = 0.044715

ADAM_LR = 0.001
ADAM_B1 = 0.9
ADAM_B2 = 0.999
ADAM_EPS = 1e-08
ADAM_WD = 0.01
ADAM_STEP = 10

V7X_VMEM_LIMIT = 61440 * 1024
TOK_TILE = 256
IO_TOK_TILE = 512


def _call(body, after=None, **kw):
    if after is None:
        return pl.pallas_call(body, **kw)
    n_in = len(kw["in_specs"])

    def ordered(*refs):
        body(*refs[:n_in], *refs[n_in + 1:])

    kw["in_specs"] = list(kw["in_specs"]) + [pl.BlockSpec(memory_space=pl.ANY)]
    fn = pl.pallas_call(ordered, **kw)
    return lambda *operands: fn(*operands, after)


def _params(sem=None):
    if sem is None:
        return pltpu.CompilerParams(vmem_limit_bytes=V7X_VMEM_LIMIT)
    return pltpu.CompilerParams(dimension_semantics=sem, vmem_limit_bytes=V7X_VMEM_LIMIT)


def _nn(a, b):
    return jnp.dot(a, b, preferred_element_type=F32)


def _nt(a, b):
    return lax.dot_general(a, b, (((1,), (1,)), ((), ())), preferred_element_type=F32)


def _tn(a, b):
    return lax.dot_general(a, b, (((0,), (0,)), ((), ())), preferred_element_type=F32)


def _gelu_tanh(x):
    return jnp.tanh(GELU_C * (x + GELU_A * (x * x * x)))


def _gelu(x, t):
    return x * (0.5 * (1.0 + t))


def _gelu_and_grad(x, t):
    cdf = 0.5 * (1.0 + t)
    return x * cdf, cdf + 0.5 * x * (1.0 - t * t) * (GELU_C * (1.0 + 3.0 * GELU_A * (x * x)))


def _rms_scale(x):
    return lax.rsqrt(jnp.mean(x * x, axis=-1, keepdims=True) + EPS)


def _rms_bwd(dxn, x, r):
    return r * dxn - x * ((r * r * r) * jnp.mean(dxn * x, axis=-1, keepdims=True))


def _bucket_table():
    a = np.arange(BLOCK)[:, None]
    j = np.arange(2 * BLOCK)[None, :]
    n = BLOCK + a - j
    valid = (n >= 0) & (n < BLOCK)
    nc = np.maximum(n, 0)
    max_exact = REL_BUCKETS // 2
    nf = np.maximum(nc, 1).astype(np.float32)
    large = max_exact + (
        np.log(nf / np.float32(max_exact)) / np.float32(math.log(BLOCK / max_exact)) * np.float32(REL_BUCKETS - max_exact)
    ).astype(np.int32)
    large = np.minimum(large, REL_BUCKETS - 1)
    bucket = np.where(nc < max_exact, nc, large)
    return np.where(valid, bucket, -1).astype(np.int32)


def _in_proj(x, g1, w_in_t):
    s = x.shape[0]
    tm = min(IO_TOK_TILE, s)

    def body(x_ref, g_ref, w_ref, zuv_ref, qkv_ref, hn_ref):
        xv = x_ref[...]
        hn = ((xv * _rms_scale(xv)) * g_ref[...]).astype(BF)
        hn_ref[...] = hn
        z = _nt(hn, w_ref[...])
        zuv_ref[...] = z[:, : 2 * D_GMLP]
        qkv_ref[...] = z[:, 2 * D_GMLP:].astype(BF)

    return _call(
        body,
        name="in_proj",
        grid=(s // tm,),
        in_specs=[
            pl.BlockSpec((tm, D_MODEL), lambda i: (i, 0)),
            pl.BlockSpec((1, D_MODEL), lambda i: (0, 0)),
            pl.BlockSpec((D_IN, D_MODEL), lambda i: (0, 0)),
        ],
        out_specs=[
            pl.BlockSpec((tm, 2 * D_GMLP), lambda i: (i, 0)),
            pl.BlockSpec((tm, D_ATTN + 2 * D_KV), lambda i: (i, 0)),
            pl.BlockSpec((tm, D_MODEL), lambda i: (i, 0)),
        ],
        out_shape=[
            jax.ShapeDtypeStruct((s, 2 * D_GMLP), F32),
            jax.ShapeDtypeStruct((s, D_ATTN + 2 * D_KV), BF),
            jax.ShapeDtypeStruct((s, D_MODEL), BF),
        ],
        compiler_params=_params(("arbitrary",)),
    )(x, g1, w_in_t)


def _head_rows(h):
    kh, g = divmod(h, Q_PER_KV)
    return kh, slice(g * BLOCK, (g + 1) * BLOCK)


def _build_bias(bias_ref, bucket_ref, table_ref):
    bucket = bucket_ref[...]
    for h in range(N_Q):
        acc = jnp.zeros((BLOCK, 2 * BLOCK), F32)
        for b in range(REL_BUCKETS):
            acc = jnp.where(bucket == b, table_ref[h, b], acc)
        kh, rows = _head_rows(h)
        bias_ref[kh, rows, :] = acc


def _window_masks(i):
    row = lax.broadcasted_iota(jnp.int32, (ROWS4, BLOCK), 0) & (BLOCK - 1)
    col = lax.broadcasted_iota(jnp.int32, (ROWS4, BLOCK), 1)
    return (col > row) & (i > 0), col <= row


def _stack_heads(ref, kh, offset):
    first = offset + kh * Q_PER_KV * HEAD_DIM
    return jnp.concatenate(
        [ref[:, first + g * HEAD_DIM: first + (g + 1) * HEAD_DIM].astype(BF) for g in range(Q_PER_KV)], axis=0)


def _stack_sinks(sink_ref, kh):
    return jnp.concatenate([jnp.full((BLOCK, 1), sink_ref[0, kh * Q_PER_KV + g], F32) for g in range(Q_PER_KV)], axis=0)


def _tril_bf16(w_ref, g):
    row = lax.broadcasted_iota(jnp.int32, (BLOCK, BLOCK), 0)
    col = lax.broadcasted_iota(jnp.int32, (BLOCK, BLOCK), 1)
    return jnp.where(col <= row, w_ref[g], 0.0).astype(BF)


def _attn_probs(q_h, k_prev, k_cur, bias_h, sink, valid_prev, valid_cur):
    l_prev = jnp.where(valid_prev, _nt(q_h, k_prev) * SCALE + bias_h[:, :BLOCK], NEG_INF)
    l_cur = jnp.where(valid_cur, _nt(q_h, k_cur) * SCALE + bias_h[:, BLOCK:], NEG_INF)
    m = jnp.maximum(jnp.maximum(jnp.max(l_prev, axis=-1, keepdims=True), jnp.max(l_cur, axis=-1, keepdims=True)), sink)
    e_prev = jnp.exp(l_prev - m)
    e_cur = jnp.exp(l_cur - m)
    e_sink = jnp.exp(sink - m)
    denom = jnp.sum(e_prev, axis=-1, keepdims=True) + jnp.sum(e_cur, axis=-1, keepdims=True) + e_sink
    return e_prev / denom, e_cur / denom, e_sink / denom


def _mixer_specs(nb):
    cl = lambda i: jnp.minimum(i, nb - 1)
    return [
        pl.BlockSpec((BLOCK, 2 * D_GMLP), lambda i: (cl(i), 0)),
        pl.BlockSpec((BLOCK, D_ATTN), lambda i: (cl(i), 0)),
        pl.BlockSpec((BLOCK, 2 * D_KV), lambda i: (cl(i), D_ATTN // (2 * D_KV))),
        pl.BlockSpec((BLOCK, 2 * D_KV), lambda i: (jnp.maximum(cl(i) - 1, 0), D_ATTN // (2 * D_KV))),
        pl.BlockSpec((1, D_GMLP), lambda i: (0, 0)),
        pl.BlockSpec((GROUPS, BLOCK, BLOCK), lambda i: (0, 0, 0)),
        pl.BlockSpec((GROUPS, BLOCK, 1), lambda i: (0, 0, 0)),
        pl.BlockSpec(memory_space=pltpu.SMEM),
        pl.BlockSpec(memory_space=pltpu.SMEM),
        pl.BlockSpec((BLOCK, 2 * BLOCK), lambda i: (0, 0)),
    ]


def _mixer_fwd(zuv, qkv, gv, w_sp, b_sp, sinks, table, bucket, after=None):
    s = zuv.shape[0]
    nb = s // BLOCK

    def body(zuv_ref, q_ref, kvc_ref, kvp_ref, gv_ref, w_ref, b_ref, sink_ref, table_ref, bucket_ref,
             mix_ref, tanh_ref, prob_ref, psink_ref, bias_ref):
        i = pl.program_id(0)

        @pl.when(i == 0)
        def _():
            _build_bias(bias_ref, bucket_ref, table_ref)

        t = _gelu_tanh(zuv_ref[...])
        tanh_ref[...] = t
        u = _gelu(zuv_ref[:, :D_GMLP], t[:, :D_GMLP])
        vg = _gelu(zuv_ref[:, D_GMLP:], t[:, D_GMLP:])
        for g in range(GROUPS):
            sl = slice(g * GDIM, (g + 1) * GDIM)
            vg_g = vg[:, sl]
            vn = ((vg_g * _rms_scale(vg_g)) * gv_ref[:, sl]).astype(BF)
            sv = _nn(_tril_bf16(w_ref, g), vn) + b_ref[g]
            mix_ref[:, sl] = (u[:, sl] * sv).astype(BF)

        valid_prev, valid_cur = _window_masks(i)
        for kh in range(N_KV):
            ksl = slice(kh * HEAD_DIM, (kh + 1) * HEAD_DIM)
            vsl = slice(D_KV + kh * HEAD_DIM, D_KV + (kh + 1) * HEAD_DIM)
            q4 = _stack_heads(q_ref, kh, 0)
            p_prev, p_cur, p_sink = _attn_probs(
                q4, kvp_ref[:, ksl], kvc_ref[:, ksl], bias_ref[kh], _stack_sinks(sink_ref, kh), valid_prev, valid_cur)
            prob_ref[0, kh, :, :BLOCK] = p_prev
            prob_ref[0, kh, :, BLOCK:] = p_cur
            psink_ref[0, kh] = jnp.broadcast_to(p_sink, (ROWS4, LANES))
            o4 = _nn(p_prev.astype(BF), kvp_ref[:, vsl]) + _nn(p_cur.astype(BF), kvc_ref[:, vsl])
            for g in range(Q_PER_KV):
                first = D_GMLP + (kh * Q_PER_KV + g) * HEAD_DIM
                mix_ref[:, first:first + HEAD_DIM] = o4[g * BLOCK:(g + 1) * BLOCK].astype(BF)

    return _call(
        body,
        name="mixer_fwd",
        after=after,
        grid=(nb,),
        in_specs=_mixer_specs(nb),
        out_specs=[
            pl.BlockSpec((BLOCK, D_MODEL), lambda i: (i, 0)),
            pl.BlockSpec((BLOCK, 2 * D_GMLP), lambda i: (i, 0)),
            pl.BlockSpec((1, N_KV, ROWS4, 2 * BLOCK), lambda i: (i, 0, 0, 0)),
            pl.BlockSpec((1, N_KV, ROWS4, LANES), lambda i: (i, 0, 0, 0)),
        ],
        out_shape=[
            jax.ShapeDtypeStruct((s, D_MODEL), BF),
            jax.ShapeDtypeStruct((s, 2 * D_GMLP), F32),
            jax.ShapeDtypeStruct((nb, N_KV, ROWS4, 2 * BLOCK), F32),
            jax.ShapeDtypeStruct((nb, N_KV, ROWS4, LANES), F32),
        ],
        scratch_shapes=[pltpu.VMEM((N_KV, ROWS4, 2 * BLOCK), F32)],
        compiler_params=_params(("arbitrary",)),
    )(zuv, qkv, qkv, qkv, gv, w_sp, b_sp, sinks, table, bucket)


def _tail(x, mix, p, t, g2, gf, w_out, w_ff1, w_ff2, w_gate, w_proj):
    s = x.shape[0]
    tm = min(TOK_TILE, s)
    n_ff = w_ff1.shape[0]
    fc = D_FF // n_ff
    pc = D_MODEL // N_DEV

    def body(x_ref, mix_ref, p_ref, t_ref, g2_ref, gf_ref, wo_ref, w1_ref, w2_ref, wg_ref, wp_ref,
             small_ref, dh1_ref, dmix_ref, hn2_ref, a_ref, df_ref, dh2_ref, gwo_ref, gwp_ref, gwg_ref, f_ref):
        i = pl.program_id(0)

        @pl.when(i == 0)
        def _():
            small_ref[...] = jnp.zeros_like(small_ref)
            gwo_ref[...] = jnp.zeros_like(gwo_ref)
            gwp_ref[...] = jnp.zeros_like(gwp_ref)
            gwg_ref[...] = jnp.zeros_like(gwg_ref)

        h1 = x_ref[...] + _nn(mix_ref[...], wo_ref[...])
        dh1_ref[...] = h1
        r2 = _rms_scale(h1)
        hn2_ref[...] = ((h1 * r2) * g2_ref[...]).astype(BF)
        h2 = h1
        for c in range(n_ff):
            f = _nn(hn2_ref[...], w1_ref[c])
            f_ref[:, c * fc:(c + 1) * fc] = f
            a = jnp.square(jnp.maximum(f, 0.0)).astype(BF)
            a_ref[:, c * fc:(c + 1) * fc] = a
            h2 = h2 + _nn(a, w2_ref[c * fc:(c + 1) * fc, :])
        h2b = h2.astype(BF)
        gate = jax.nn.sigmoid(_nn(h2b, wg_ref[...]))
        pb = p_ref[...].astype(BF)
        pp = _nn(pb, wp_ref[...])
        h3 = h2 + gate * pp
        rf = _rms_scale(h3)
        gf_v = gf_ref[...]
        err = (h3 * rf) * gf_v - t_ref[...]
        small_ref[2:3, :] += jnp.sum(jnp.sum(err * err, axis=-1, keepdims=True), axis=0, keepdims=True) * (0.5 / D_MODEL)

        dy = err * (1.0 / D_MODEL)
        small_ref[1:2, :] += jnp.sum(dy * (h3 * rf), axis=0, keepdims=True)
        dh3 = _rms_bwd(dy * gf_v, h3, rf)
        gw_proj = _tn(pb, (dh3 * gate).astype(BF))
        for q in range(N_DEV):
            gwp_ref[q] += gw_proj[:, q * pc:(q + 1) * pc]
        dgl = ((dh3 * pp) * (gate * (1.0 - gate))).astype(BF)
        gwg_ref[...] += _tn(h2b, dgl)
        dh2 = dh3 + _nt(dgl, wg_ref[...])
        dmix_ref[...] = dh2
        dh2_ref[...] = dh2.astype(BF)
        dhn2 = jnp.zeros((tm, D_MODEL), F32)
        for c in range(n_ff):
            da = _nt(dh2_ref[...], w2_ref[c * fc:(c + 1) * fc, :])
            df = (da * (2.0 * jnp.maximum(f_ref[:, c * fc:(c + 1) * fc], 0.0))).astype(BF)
            df_ref[:, c * fc:(c + 1) * fc] = df
            dhn2 = dhn2 + _nt(df, w1_ref[c])
        h1 = dh1_ref[...]
        small_ref[0:1, :] += jnp.sum(dhn2 * (h1 * r2), axis=0, keepdims=True)
        dh1 = dmix_ref[...] + _rms_bwd(dhn2 * g2_ref[...], h1, r2)
        dh1_ref[...] = dh1
        dh1b = dh1.astype(BF)
        gwo_ref[...] += _tn(mix_ref[...], dh1b)
        dmix_ref[...] = _nt(dh1b, wo_ref[...])

    tile = lambda cols: pl.BlockSpec((tm, cols), lambda i: (i, 0))
    whole = lambda shape: pl.BlockSpec(shape, lambda i: (0,) * len(shape), pipeline_mode=pl.Buffered(1))
    total = lambda shape: pl.BlockSpec(shape, lambda i: (0,) * len(shape))
    row = pl.BlockSpec((1, D_MODEL), lambda i: (0, 0))
    act = lambda cols, dt: jax.ShapeDtypeStruct((s, cols), dt)
    gw_shapes = [(D_MODEL, D_MODEL), (N_DEV, PLE_DIM, pc), (D_MODEL, D_MODEL)]
    return _call(
        body,
        name="tail",
        grid=(s // tm,),
        in_specs=[tile(D_MODEL), tile(D_MODEL), tile(PLE_DIM), tile(D_MODEL), row, row,
                  whole(w_out.shape), whole(w_ff1.shape), whole(w_ff2.shape), whole(w_gate.shape), whole(w_proj.shape)],
        out_specs=[total((8, D_MODEL)), tile(D_MODEL), tile(D_MODEL), tile(D_MODEL), tile(D_FF), tile(D_FF), tile(D_MODEL),
                   *[total(shape) for shape in gw_shapes]],
        out_shape=[jax.ShapeDtypeStruct((8, D_MODEL), F32),
                   act(D_MODEL, F32), act(D_MODEL, F32), act(D_MODEL, BF), act(D_FF, BF), act(D_FF, BF), act(D_MODEL, BF),
                   *[jax.ShapeDtypeStruct(shape, F32) for shape in gw_shapes]],
        scratch_shapes=[pltpu.VMEM((tm, D_FF), F32)],
        compiler_params=_params(("arbitrary",)),
    )(x, mix, p, t, g2, gf, w_out, w_ff1, w_ff2, w_gate, w_proj)


def _mixer_bwd(zuv, qkv, dmix, saved, gv, w_sp, b_sp, bucket, after=None):
    s = zuv.shape[0]
    nb = s // BLOCK

    def body(zuv_ref, q_ref, kvc_ref, kvp_ref, gv_ref, w_ref, b_ref, bucket_ref, dmix_ref, tanh_ref, prob_ref, psink_ref,
             dzm_ref, dkv_ref, dgv_ref, dw_ref, db_ref, dattn_ref,
             dbias_ref, carry_ref, dsink_acc, db_acc):
        i = pl.program_id(0)

        @pl.when(i == 0)
        def _():
            dbias_ref[...] = jnp.zeros_like(dbias_ref)
            carry_ref[...] = jnp.zeros_like(carry_ref)
            dsink_acc[...] = jnp.zeros_like(dsink_acc)
            dgv_ref[...] = jnp.zeros_like(dgv_ref)
            dw_ref[...] = jnp.zeros_like(dw_ref)
            db_acc[...] = jnp.zeros_like(db_acc)

        @pl.when(i < nb)
        def _():
            u, du_dz = _gelu_and_grad(zuv_ref[:, :D_GMLP], tanh_ref[:, :D_GMLP])
            vg, dvg_dz = _gelu_and_grad(zuv_ref[:, D_GMLP:], tanh_ref[:, D_GMLP:])
            for g in range(GROUPS):
                sl = slice(g * GDIM, (g + 1) * GDIM)
                vg_g = vg[:, sl]
                rg = _rms_scale(vg_g)
                vhat = vg_g * rg
                gain = gv_ref[:, sl]
                vn = (vhat * gain).astype(BF)
                w_g = _tril_bf16(w_ref, g)
                sv = _nn(w_g, vn) + b_ref[g]
                dmix_a = dmix_ref[:, sl]
                dsv = dmix_a * u[:, sl]
                dsvb = dsv.astype(BF)
                db_acc[g] += jnp.sum(dsv, axis=-1, keepdims=True)
                dw_ref[g] += _nt(dsvb, vn)
                dvn = _tn(w_g, dsvb)
                dgv_ref[:, sl] += jnp.sum(dvn * vhat, axis=0, keepdims=True)
                dvg = _rms_bwd(dvn * gain, vg_g, rg)
                dzm_ref[:, sl] = ((dmix_a * sv) * du_dz[:, sl]).astype(BF)
                dzm_ref[:, D_GMLP + g * GDIM: D_GMLP + (g + 1) * GDIM] = (dvg * dvg_dz[:, sl]).astype(BF)

            for kh in range(N_KV):
                ksl = slice(kh * HEAD_DIM, (kh + 1) * HEAD_DIM)
                vsl = slice(D_KV + kh * HEAD_DIM, D_KV + (kh + 1) * HEAD_DIM)
                k_prev, k_cur = kvp_ref[:, ksl], kvc_ref[:, ksl]
                v_prev, v_cur = kvp_ref[:, vsl], kvc_ref[:, vsl]
                q4 = _stack_heads(q_ref, kh, 0)
                p_prev, p_cur, p_sink = prob_ref[0, kh, :, :BLOCK], prob_ref[0, kh, :, BLOCK:], psink_ref[0, kh, :, 0:1]
                do4 = _stack_heads(dmix_ref, kh, D_GMLP)
                dp_prev = _nt(do4, v_prev)
                dp_cur = _nt(do4, v_cur)
                delta = jnp.sum(p_prev * dp_prev, axis=-1, keepdims=True) + jnp.sum(p_cur * dp_cur, axis=-1, keepdims=True)
                ds_prev = p_prev * (dp_prev - delta)
                ds_cur = p_cur * (dp_cur - delta)
                dsink_acc[kh] -= p_sink * delta
                dbias_ref[kh, :, :BLOCK] += ds_prev
                dbias_ref[kh, :, BLOCK:] += ds_cur
                dsb_prev = ds_prev.astype(BF)
                dsb_cur = ds_cur.astype(BF)
                dq4 = (_nn(dsb_prev, k_prev) + _nn(dsb_cur, k_cur)) * SCALE
                for g in range(Q_PER_KV):
                    first = 2 * D_GMLP + (kh * Q_PER_KV + g) * HEAD_DIM
                    dzm_ref[:, first:first + HEAD_DIM] = dq4[g * BLOCK:(g + 1) * BLOCK].astype(BF)
                dkv_ref[:, ksl] = (carry_ref[:, ksl] + _tn(dsb_prev, q4) * SCALE).astype(BF)
                dkv_ref[:, vsl] = (carry_ref[:, vsl] + _tn(p_prev.astype(BF), do4)).astype(BF)
                carry_ref[:, ksl] = _tn(dsb_cur, q4) * SCALE
                carry_ref[:, vsl] = _tn(p_cur.astype(BF), do4)

        @pl.when(i == nb)
        def _():
            dkv_ref[...] = carry_ref[...].astype(BF)
            row = lax.broadcasted_iota(jnp.int32, (BLOCK, BLOCK), 0)
            col = lax.broadcasted_iota(jnp.int32, (BLOCK, BLOCK), 1)
            for g in range(GROUPS):
                dw_ref[g] = jnp.where(col <= row, dw_ref[g], 0.0)
                db_ref[g:g + 1, :] = jnp.sum(jnp.where(col == row, db_acc[g], 0.0), axis=0, keepdims=True)
            bucket = bucket_ref[...]
            for b in range(N_Q, REL_BUCKETS):
                dattn_ref[N_Q, b] = 0.0
            for h in range(N_Q):
                kh, rows = _head_rows(h)
                dattn_ref[N_Q, h] = jnp.sum(dsink_acc[kh, rows, :])
                dbh = dbias_ref[kh, rows, :]
                for b in range(REL_BUCKETS):
                    dattn_ref[h, b] = jnp.sum(jnp.where(bucket == b, dbh, 0.0))

    cl = lambda i: jnp.minimum(i, nb - 1)
    const = lambda shape: pl.BlockSpec(shape, lambda i: (0,) * len(shape))
    return _call(
        body,
        name="mixer_bwd",
        after=after,
        grid=(nb + 1,),
        in_specs=_mixer_specs(nb)[:7] + [
            const((BLOCK, 2 * BLOCK)),
            pl.BlockSpec((BLOCK, D_MODEL), lambda i: (cl(i), 0)),
            pl.BlockSpec((BLOCK, 2 * D_GMLP), lambda i: (cl(i), 0)),
            pl.BlockSpec((1, N_KV, ROWS4, 2 * BLOCK), lambda i: (cl(i), 0, 0, 0)),
            pl.BlockSpec((1, N_KV, ROWS4, LANES), lambda i: (cl(i), 0, 0, 0)),
        ],
        out_specs=[
            pl.BlockSpec((BLOCK, D_MAIN), lambda i: (cl(i), 0)),
            pl.BlockSpec((BLOCK, 2 * D_KV), lambda i: (jnp.maximum(i - 1, 0), 0)),
            const((1, D_GMLP)),
            const((GROUPS, BLOCK, BLOCK)),
            const((GROUPS, BLOCK)),
            pl.BlockSpec(memory_space=pltpu.SMEM),
        ],
        out_shape=[
            jax.ShapeDtypeStruct((s, D_MAIN), BF),
            jax.ShapeDtypeStruct((s, 2 * D_KV), BF),
            jax.ShapeDtypeStruct((1, D_GMLP), F32),
            jax.ShapeDtypeStruct((GROUPS, BLOCK, BLOCK), F32),
            jax.ShapeDtypeStruct((GROUPS, BLOCK), F32),
            jax.ShapeDtypeStruct((N_Q + 1, REL_BUCKETS), F32),
        ],
        scratch_shapes=[
            pltpu.VMEM((N_KV, ROWS4, 2 * BLOCK), F32),
            pltpu.VMEM((BLOCK, 2 * D_KV), F32),
            pltpu.VMEM((N_KV, ROWS4, 1), F32),
            pltpu.VMEM((GROUPS, BLOCK, 1), F32),
        ],
        compiler_params=_params(("arbitrary",)),
    )(zuv, qkv, qkv, qkv, gv, w_sp, b_sp, bucket, dmix, *saved)


def _in_bwd(x, dh1, dzm, dkv, g1, w_in_t, after=None):
    s = x.shape[0]
    tm = min(IO_TOK_TILE, s)

    def body(x_ref, dh1_ref, dzm_ref, dkv_ref, g_ref, w_ref, dx_ref, dg_ref):
        @pl.when(pl.program_id(0) == 0)
        def _():
            dg_ref[...] = jnp.zeros_like(dg_ref)

        dhn = _nn(dzm_ref[...], w_ref[:D_MAIN, :]) + _nn(dkv_ref[...], w_ref[D_MAIN:, :])
        xv = x_ref[...]
        r = _rms_scale(xv)
        dg_ref[...] += jnp.sum(dhn * (xv * r), axis=0, keepdims=True)
        dx_ref[...] = dh1_ref[...] + _rms_bwd(dhn * g_ref[...], xv, r)

    tile = lambda cols: pl.BlockSpec((tm, cols), lambda i: (i, 0))
    row = pl.BlockSpec((1, D_MODEL), lambda i: (0, 0))
    return _call(
        body,
        name="in_bwd",
        after=after,
        grid=(s // tm,),
        in_specs=[tile(D_MODEL), tile(D_MODEL), tile(D_MAIN), tile(2 * D_KV), row, pl.BlockSpec((D_IN, D_MODEL), lambda i: (0, 0))],
        out_specs=[tile(D_MODEL), row],
        out_shape=[jax.ShapeDtypeStruct((s, D_MODEL), F32), jax.ShapeDtypeStruct((1, D_MODEL), F32)],
        compiler_params=_params(("arbitrary",)),
    )(x, dh1, dzm, dkv, g1, w_in_t)


def _wgrad_in(dzm, dkv, hn, after=None):
    s = hn.shape[0]
    tm = 2 * D_KV
    n_main = D_MAIN // tm

    def body(dzm_ref, dkv_ref, hn_ref, o_ref):
        i = pl.program_id(0)

        @pl.when(i < n_main)
        def _():
            o_ref[...] = _tn(dzm_ref[...], hn_ref[...])

        @pl.when(i == n_main)
        def _():
            o_ref[...] = _tn(dkv_ref[...], hn_ref[...])

    return _call(
        body,
        name="wgrad_in",
        after=after,
        grid=(n_main + 1,),
        in_specs=[
            pl.BlockSpec((s, tm), lambda i: (0, jnp.minimum(i, n_main - 1))),
            pl.BlockSpec((s, tm), lambda i: (0, 0)),
            pl.BlockSpec((s, D_MODEL), lambda i: (0, 0)),
        ],
        out_specs=pl.BlockSpec((tm, D_MODEL), lambda i: (i, 0)),
        out_shape=jax.ShapeDtypeStruct((D_IN, D_MODEL), F32),
        compiler_params=_params(("arbitrary",)),
    )(dzm, dkv, hn)


def _wgrad(a, b, tm, tn, name, peer_cols=0, after=None):
    s, m = a.shape
    n = b.shape[1]

    def body(a_ref, b_ref, o_ref, at_ref):
        @pl.when(pl.program_id(1) == 0)
        def _():
            at_ref[...] = a_ref[...].astype(BF).T

        r = _nn(at_ref[...], b_ref[...])
        if peer_cols:
            for q in range(tn // peer_cols):
                o_ref[q] = r[:, q * peer_cols:(q + 1) * peer_cols]
        else:
            o_ref[...] = r

    if peer_cols:
        out_spec = pl.BlockSpec((tn // peer_cols, tm, peer_cols), lambda i, j: (j, i, 0))
        out_shape = jax.ShapeDtypeStruct((n // peer_cols, m, peer_cols), F32)
    else:
        out_spec = pl.BlockSpec((tm, tn), lambda i, j: (i, j))
        out_shape = jax.ShapeDtypeStruct((m, n), F32)
    return _call(
        body,
        name=name,
        after=after,
        grid=(m // tm, n // tn),
        in_specs=[pl.BlockSpec((s, tm), lambda i, j: (0, i)), pl.BlockSpec((s, tn), lambda i, j: (0, j))],
        out_specs=out_spec,
        out_shape=out_shape,
        scratch_shapes=[pltpu.VMEM((tm, s), BF)],
        compiler_params=_params(("arbitrary", "arbitrary")),
    )(a, b)


def _adamw_math(w, g, m, v):
    m_new = ADAM_B1 * m + (1.0 - ADAM_B1) * g
    v_new = ADAM_B2 * v + (1.0 - ADAM_B2) * jnp.square(g)
    m_hat = m_new / (1.0 - ADAM_B1 ** ADAM_STEP)
    v_hat = v_new / (1.0 - ADAM_B2 ** ADAM_STEP)
    delta = -ADAM_LR * (m_hat / (jnp.sqrt(v_hat) + ADAM_EPS) + ADAM_WD * w)
    return delta, m_new, v_new


def _final_adamw(part, recv, w, m, v, name):
    r, c = w.shape
    tr = min(r, 512)

    def body(p_ref, r_ref, w_ref, m_ref, v_ref, g_ref, d_ref, mo_ref, vo_ref):
        g = p_ref[...]
        for j in range(3):
            g = g + r_ref[j].astype(F32)
        g_ref[...] = g
        d_ref[...], mo_ref[...], vo_ref[...] = _adamw_math(w_ref[...], g, m_ref[...], v_ref[...])

    spec = pl.BlockSpec((tr, c), lambda i: (i, 0))
    return _call(
        body,
        name=name,
        grid=(r // tr,),
        in_specs=[spec, pl.BlockSpec((3, tr, c), lambda i: (0, i, 0)), spec, spec, spec],
        out_specs=[spec] * 4,
        out_shape=[jax.ShapeDtypeStruct((r, c), F32)] * 4,
        compiler_params=_params(("arbitrary",)),
    )(part, recv, w, m, v)


def _rs_sum(gs, lands, blocks, name, smalls, small_lands, small_dtypes):
    n, n_small = len(gs), len(smalls)
    steps = pl.cdiv(gs[0].shape[1], 256)
    assert all(g.shape[1] % steps == 0 for g in gs)

    def body(blk_ref, *refs):
        ins, small_ins, outs = refs[:5 * n], refs[5 * n:5 * n + 2 * n_small], refs[5 * n + 2 * n_small:]
        for k in range(n):
            g0_ref, g1_ref, g2_ref, g3_ref, l_ref = ins[5 * k:5 * k + 5]
            outs[2 * k][...] = g0_ref[0] + l_ref[0]
            for j, gj_ref in enumerate((g1_ref, g2_ref, g3_ref)):
                outs[2 * k + 1][j] = (gj_ref[0] + l_ref[j + 1]).astype(BF)
        for q in range(n_small):
            outs[2 * n + q][...] = (small_ins[q][...] + small_ins[n_small + q][...]).astype(small_dtypes[q])

    def whole(a):
        return pl.BlockSpec(a.shape, lambda i, blk: (0,) * a.ndim)

    in_specs, out_specs, out_shape, operands = [], [], [], []
    for g, land in zip(gs, lands):
        _, r, c = g.shape
        tr = r // steps
        in_specs += [pl.BlockSpec((1, tr, c), lambda i, blk, j=j: (blk[j], i, 0)) for j in range(4)]
        in_specs.append(pl.BlockSpec((4, tr, c), lambda i, blk: (0, i, 0)))
        out_specs += [pl.BlockSpec((tr, c), lambda i, blk: (i, 0)), pl.BlockSpec((3, tr, c), lambda i, blk: (0, i, 0))]
        out_shape += [jax.ShapeDtypeStruct((r, c), F32), jax.ShapeDtypeStruct((3, r, c), BF)]
        operands += [g, g, g, g, land]
    in_specs += [whole(a) for a in (*smalls, *small_lands)]
    out_specs += [whole(a) for a in smalls]
    out_shape += [jax.ShapeDtypeStruct(a.shape, dt) for a, dt in zip(smalls, small_dtypes)]
    outs = _call(
        body,
        name=name,
        grid_spec=pltpu.PrefetchScalarGridSpec(num_scalar_prefetch=1, grid=(steps,), in_specs=in_specs, out_specs=out_specs),
        out_shape=out_shape,
        compiler_params=_params(("arbitrary",)),
    )(blocks, *operands, *smalls, *small_lands)
    return [(outs[2 * k], outs[2 * k + 1]) for k in range(n)], list(outs[2 * n:])


def _final_adamw_few(parts, recvs, ws, ms, vs, name):
    n = len(ws)

    def body(*refs):
        ins, outs = refs[:5 * n], refs[5 * n:]
        for k in range(n):
            p_ref, r_ref, w_ref, m_ref, v_ref = (ins[q * n + k] for q in range(5))
            g = p_ref[...]
            for j in range(3):
                g = g + r_ref[j].astype(F32)
            outs[k][...] = g
            outs[n + k][...], outs[2 * n + k][...], outs[3 * n + k][...] = _adamw_math(w_ref[...], g, m_ref[...], v_ref[...])

    outs = _call(
        body,
        name=name,
        out_shape=[jax.ShapeDtypeStruct(w.shape, F32) for w in ws] * 4,
        compiler_params=_params(),
    )(*parts, *recvs, *ws, *ms, *vs)
    return outs[:n], outs[n:2 * n], outs[2 * n:3 * n], outs[3 * n:]


def _adamw_small(own, recv, g1_own, g1_recv, weights, moms, vels, after):
    n_w = len(weights)

    def body(*refs):
        own_refs, recv_refs, g1_own_ref, g1_recv_ref = refs[:5], refs[5:10], refs[10], refs[11]
        w_refs, m_refs, v_refs = refs[12:12 + n_w], refs[12 + n_w:12 + 2 * n_w], refs[12 + 2 * n_w:12 + 3 * n_w]
        outs = refs[12 + 3 * n_w:]
        loss_ref, g_refs, d_refs = outs[0], outs[1:1 + n_w], outs[1 + n_w:1 + 2 * n_w]
        mo_refs, vo_refs = outs[1 + 2 * n_w:1 + 3 * n_w], outs[1 + 3 * n_w:]
        x, y, c = lax.axis_index("x"), lax.axis_index("y"), lax.axis_index("c")

        def in_place_order(values, my_place):
            acc = None
            for place in range(len(values)):
                r = place ^ my_place
                term = values[-1]
                for q in range(len(values) - 2, -1, -1):
                    term = jnp.where(r == q, values[q], term)
                acc = term if acc is None else acc + term
            return acc

        def total(k, *index):
            index = index or (slice(None),) * (len(own_refs[k].shape) - 1)
            across = [own_refs[k][(0, *index)], recv_refs[k][(1, *index)], recv_refs[k][(0, *index)], recv_refs[k][(2, *index)]]
            return in_place_order([v.astype(F32) for v in across], 2 * x + y)

        g1 = in_place_order([g1_own_ref[0]] + [g1_recv_ref[j] for j in range(N_DEV - 1)], 4 * x + 2 * y + c)
        grads = [
            g1, total(1), total(2), total(3),
            total(4, slice(N_Q, None), slice(0, N_Q)), total(4, slice(0, N_Q), slice(None)),
            total(0, slice(0, 1), slice(None)), total(0, slice(1, 2), slice(None))]
        loss_ref[...] = total(0, slice(2, 3), slice(0, 1))
        for k in range(n_w):
            g_refs[k][...] = grads[k]
            d_refs[k][...], mo_refs[k][...], vo_refs[k][...] = _adamw_math(w_refs[k][...], grads[k], m_refs[k][...], v_refs[k][...])

    shapes = [jax.ShapeDtypeStruct(w.shape, F32) for w in weights]
    outs = _call(
        body,
        name="adamw_small",
        after=after,
        in_specs=[pl.BlockSpec(memory_space=pltpu.VMEM)] * (12 + 3 * n_w),
        out_shape=[jax.ShapeDtypeStruct((1, 1), F32)] + shapes * 4,
        compiler_params=_params(),
    )(*own, *recv, g1_own, g1_recv, *weights, *moms, *vels)
    return outs[0], outs[1:1 + n_w], outs[1 + n_w:1 + 2 * n_w], outs[1 + 2 * n_w:1 + 3 * n_w], outs[1 + 3 * n_w:]


def _place():
    x, y, c = lax.axis_index("x"), lax.axis_index("y"), lax.axis_index("c")
    return x, y, c, [(1 - x, y), (x, 1 - y), (1 - x, 1 - y)]


def _dev_index(px, py, pc):
    return 4 * px + 2 * py + pc


HBM_SPEC = pl.BlockSpec(memory_space=pltpu.HBM)
SEM_SPEC = pl.BlockSpec(memory_space=pltpu.SEMAPHORE)
ANY_SPEC = pl.BlockSpec(memory_space=pl.ANY)
DATAFLOW = pltpu.SideEffectType.DATAFLOW_SIDE_EFFECTING


def _hbm(a):
    return pltpu.with_memory_space_constraint(a, pltpu.HBM)


def _ag_piece(land_k, block, half, peer, send_sem, recv_sem):
    ref = land_k.at[_dev_index(*block)]
    if half is not None:
        rows = land_k.shape[1] // 2
        ref = ref.at[pl.ds(half * rows, rows)]
    return pltpu.make_async_remote_copy(
        src_ref=ref, dst_ref=ref, send_sem=send_sem, recv_sem=recv_sem, device_id=peer, device_id_type=MESH)


def _ag_plan():
    x, y, c, _ = _place()
    me, sib = (x, y, c), (x, y, 1 - c)
    xn, yn, diag = (1 - x, y, c), (x, 1 - y, c), (1 - x, 1 - y, c)
    return dict(
        relay_halves=[(me, 0, xn), (me, 1, yn)],
        others=[(me, None, sib), (me, 1, xn), (me, 0, yn)],
        relays=[(xn, 0, yn), (yn, 1, xn)],
        near=[(xn, None, sib), (yn, None, sib)],
        far=[(diag, None, sib)],
    )


def _ag_stage(land, stage, send_sems, recv_sems, act):
    copies = _ag_plan()[stage]
    n = len(copies)
    for k in range(len(land)):
        for j, (block, half, peer) in enumerate(copies):
            cp = _ag_piece(land[k], block, half, peer, send_sems.at[n * k + j], recv_sems.at[n * k + j])
            if act == "start":
                cp.start()
            else:
                cp.wait_send()
                cp.wait_recv()


def _sem_shapes(*counts):
    return [pltpu.SemaphoreType.DMA((n,)) for n in counts for _ in range(2)]


def _ag_start(first, rest):
    shards = list(first) + list(rest)
    k_n, k_first = len(shards), len(first)
    k_rest = k_n - k_first

    def body(*refs):
        ins, sems, land = refs[:k_n], refs[k_n:k_n + 6], refs[k_n + 6:2 * k_n + 6]
        token, stage, wide = refs[2 * k_n + 6], refs[2 * k_n + 7:3 * k_n + 7], refs[3 * k_n + 7:4 * k_n + 7]
        own_sems = refs[4 * k_n + 7]
        x, y, c, chips = _place()
        loads = [pltpu.make_async_copy(ins[k], wide[k], own_sems.at[k]) for k in range(k_n)]
        for cp in loads:
            cp.start()

        def place(ks):
            copies = []
            for k in ks:
                loads[k].wait()
                stage[k][...] = wide[k][...].astype(BF)
                copies.append(pltpu.make_async_copy(stage[k], land[k].at[_dev_index(x, y, c)], own_sems.at[k_n + k]))
                copies[-1].start()
            for cp in copies:
                cp.wait()

        place(range(k_first))
        targets = [(x, y, 1 - c)] + [(*chip, c) for chip in chips]
        for k in range(k_first):
            for j, to in enumerate(targets):
                _ag_piece(land[k], (x, y, c), None, to, sems[0].at[4 * k + j], sems[1].at[4 * k + j]).start()
        place(range(k_first, k_n))
        _ag_stage(land[k_first:], "relay_halves", sems[2], sems[3], "start")
        _ag_stage(land[k_first:], "others", sems[4], sems[5], "start")
        token[...] = jnp.zeros_like(token)

    outs = pl.pallas_call(
        body,
        name="ag_start",
        in_specs=[ANY_SPEC] * k_n,
        out_specs=(*[SEM_SPEC] * 6, *[HBM_SPEC] * k_n, pl.BlockSpec(memory_space=pltpu.VMEM)),
        out_shape=(*_sem_shapes(4 * k_first, 2 * k_rest, 3 * k_rest),
                   *[pltpu.HBM((N_DEV,) + a.shape, BF) for a in shards], jax.ShapeDtypeStruct((8, LANES), F32)),
        scratch_shapes=[pltpu.VMEM(a.shape, dt) for dt in (BF, F32) for a in shards] + [pltpu.SemaphoreType.DMA((2 * k_n,))],
        compiler_params=pltpu.CompilerParams(has_side_effects=DATAFLOW, vmem_limit_bytes=V7X_VMEM_LIMIT),
    )(*shards)
    flying = list(outs[6:6 + k_n])
    return (outs[0], outs[1], flying[:k_first]), (outs[2:6], flying[k_first:]), outs[-1]


def _ag_split_call(lands, waits, starts, after, name):
    k_n = len(lands)
    plan_sizes = dict(relay_halves=2, others=3, relays=2, near=2, far=1)
    n_in, n_out = 2 * len(waits), 2 * len(starts)

    def body(*refs):
        land = refs[:k_n]
        in_sems = refs[k_n:k_n + n_in]
        out_sems, token = refs[len(refs) - 1 - n_out:len(refs) - 1], refs[-1]
        for w, (stage, _, _) in enumerate(waits):
            _ag_stage(land, stage, in_sems[2 * w], in_sems[2 * w + 1], "wait")
            if w < len(starts):
                _ag_stage(land, starts[w], out_sems[2 * w], out_sems[2 * w + 1], "start")
        token[...] = jnp.zeros_like(token)

    outs = pl.pallas_call(
        body,
        name=name,
        in_specs=[HBM_SPEC] * k_n + [SEM_SPEC] * n_in + [ANY_SPEC],
        out_specs=(*[HBM_SPEC] * k_n, *[SEM_SPEC] * n_out, pl.BlockSpec(memory_space=pltpu.VMEM)),
        out_shape=(*[pltpu.HBM(a.shape, a.dtype) for a in lands], *_sem_shapes(*[plan_sizes[s] * k_n for s in starts]),
                   jax.ShapeDtypeStruct((8, LANES), F32)),
        input_output_aliases={k: k for k in range(k_n)},
        compiler_params=pltpu.CompilerParams(has_side_effects=DATAFLOW),
    )(*lands, *[s for _, a, b in waits for s in (a, b)], after)
    return list(outs[:k_n]), list(outs[k_n:k_n + n_out]), outs[-1]


def _ag_mid(lands, send_sems, recv_sems, after, name):
    k_n = len(lands)

    def body(*refs):
        land = refs[:k_n]
        send1, recv1 = refs[k_n], refs[k_n + 1]
        fwd_send, fwd_recv = refs[-2], refs[-1]
        x, y, c, chips = _place()
        sources = [(x, y, 1 - c)] + [(*chip, c) for chip in chips]
        for k in range(k_n):
            mine = land[k].at[_dev_index(x, y, c)]
            for j, frm in enumerate(sources):
                got = land[k].at[_dev_index(*frm)]
                cp = pltpu.make_async_remote_copy(
                    src_ref=mine, dst_ref=got, send_sem=send1.at[4 * k + j], recv_sem=recv1.at[4 * k + j], device_id=frm, device_id_type=MESH)
                cp.wait_send()
                cp.wait_recv()
                if j >= 1:
                    pltpu.make_async_remote_copy(
                        src_ref=got, dst_ref=got, send_sem=fwd_send.at[3 * k + j - 1], recv_sem=fwd_recv.at[3 * k + j - 1],
                        device_id=(x, y, 1 - c), device_id_type=MESH).start()

    outs = pl.pallas_call(
        body,
        name=name,
        in_specs=[HBM_SPEC] * k_n + [SEM_SPEC, SEM_SPEC, ANY_SPEC],
        out_specs=(*[HBM_SPEC] * k_n, SEM_SPEC, SEM_SPEC),
        out_shape=(*[pltpu.HBM(a.shape, a.dtype) for a in lands], pltpu.SemaphoreType.DMA((3 * k_n,)), pltpu.SemaphoreType.DMA((3 * k_n,))),
        input_output_aliases={k: k for k in range(k_n)},
        compiler_params=pltpu.CompilerParams(has_side_effects=DATAFLOW),
    )(*lands, send_sems, recv_sems, after)
    return list(outs[:k_n]), outs[-2], outs[-1]


def _ag_end(lands, fwd_send, fwd_recv, name):
    k_n = len(lands)

    def body(*refs):
        land = refs[:k_n]
        fsend, frecv = refs[k_n], refs[k_n + 1]
        x, y, c, chips = _place()
        for k in range(k_n):
            for j, chip in enumerate(chips):
                cp = pltpu.make_async_remote_copy(
                    src_ref=land[k].at[_dev_index(*chip, c)], dst_ref=land[k].at[_dev_index(*chip, 1 - c)],
                    send_sem=fsend.at[3 * k + j], recv_sem=frecv.at[3 * k + j], device_id=(x, y, 1 - c), device_id_type=MESH)
                cp.wait_send()
                cp.wait_recv()

    outs = pl.pallas_call(
        body,
        name=name,
        in_specs=[HBM_SPEC] * k_n + [SEM_SPEC, SEM_SPEC],
        out_specs=tuple([HBM_SPEC] * k_n),
        out_shape=tuple(pltpu.HBM(a.shape, a.dtype) for a in lands),
        input_output_aliases={k: k for k in range(k_n)},
        compiler_params=pltpu.CompilerParams(has_side_effects=DATAFLOW),
    )(*lands, fwd_send, fwd_recv)
    return list(outs)


def _chips4():
    x, y, c, others = _place()
    return x, y, c, [(x, y)] + others


def _route_sibling(j):
    x, y, c, chips = _chips4()
    return _dev_index(*chips[j], 1 - c), j, (x, y, 1 - c)


def _route_chips(j):
    x, y, c, chips = _chips4()
    return j, j, (*chips[j + 1], c)


def _route_sibling_whole(j):
    x, y, c, _ = _chips4()
    return 0, 0, (x, y, 1 - c)


def _route_chips_whole(j):
    x, y, c, chips = _chips4()
    return 0, j, (*chips[j + 1], c)


def _route_everyone(j):
    x, y, c, _ = _chips4()
    flip = [(j + 1) >> 2 & 1, (j + 1) >> 1 & 1, (j + 1) & 1]
    return 0, j, tuple(1 - v if f else v for v, f in zip((x, y, c), flip))


def _xchg_copies(routes, src, dst, send_sems, recv_sems):
    copies, sem = [], 0
    for k, (route, n) in enumerate(routes):
        for j in range(n):
            si, di, peer = route(j)
            copies.append(pltpu.make_async_remote_copy(
                src_ref=src[k].at[si], dst_ref=dst[k].at[di], send_sem=send_sems.at[sem], recv_sem=recv_sems.at[sem],
                device_id=peer, device_id_type=MESH))
            sem += 1
    return copies


def _xchg_start(srcs, slot_shapes, routes, name, after=None):
    k_n = len(srcs)
    n_in = 2 * k_n + (after is not None)
    n_sem = sum(n for _, n in routes)
    dsts = [lax.empty((n,) + tuple(sh), a.dtype) for sh, a, (_, n) in zip(slot_shapes, srcs, routes)]

    def body(*refs):
        src, dst = refs[:k_n], refs[k_n:2 * k_n]
        send_sems, recv_sems, token = refs[n_in], refs[n_in + 1], refs[-1]
        for cp in _xchg_copies(routes, src, dst, send_sems, recv_sems):
            cp.start()
        token[...] = jnp.zeros_like(token)

    arrays = list(srcs) + dsts
    outs = pl.pallas_call(
        body,
        name=name,
        in_specs=[HBM_SPEC] * (2 * k_n) + [ANY_SPEC] * (n_in - 2 * k_n),
        out_specs=(SEM_SPEC, SEM_SPEC, *[HBM_SPEC] * (2 * k_n), pl.BlockSpec(memory_space=pltpu.VMEM)),
        out_shape=(pltpu.SemaphoreType.DMA((n_sem,)), pltpu.SemaphoreType.DMA((n_sem,)),
                   *[pltpu.HBM(a.shape, a.dtype) for a in arrays], jax.ShapeDtypeStruct((8, LANES), F32)),
        input_output_aliases={i: 2 + i for i in range(2 * k_n)},
        compiler_params=pltpu.CompilerParams(has_side_effects=DATAFLOW),
    )(*[_hbm(a) for a in arrays], *([] if after is None else [after]))
    return outs[0], outs[1], list(outs[2:2 + k_n]), list(outs[2 + k_n:2 + 2 * k_n]), outs[-1]


def _xchg_wait(send_sems, recv_sems, srcs, dsts, routes, after, name):
    k_n = len(srcs)

    def body(*refs):
        src, dst = refs[:k_n], refs[k_n:2 * k_n]
        for cp in _xchg_copies(routes, src, dst, refs[2 * k_n], refs[2 * k_n + 1]):
            cp.wait_send()
            cp.wait_recv()

    arrays = list(srcs) + list(dsts)
    outs = pl.pallas_call(
        body,
        name=name,
        in_specs=[HBM_SPEC] * (2 * k_n) + [SEM_SPEC, SEM_SPEC, ANY_SPEC],
        out_specs=tuple([HBM_SPEC] * (2 * k_n)),
        out_shape=tuple(pltpu.HBM(a.shape, a.dtype) for a in arrays),
        input_output_aliases={i: i for i in range(2 * k_n)},
        compiler_params=pltpu.CompilerParams(has_side_effects=DATAFLOW),
    )(*arrays, send_sems, recv_sems, after)
    return list(outs[:k_n]), list(outs[k_n:])


SMALL = ("norm1_gain", "gmlp_v_gain", "w_spatial", "b_spatial", "attn_sinks", "rel_bias_table", "norm2_gain", "final_gain")
LANES = 128


def _swap_start(grads, smalls, tag, after=None):
    srcs = list(grads) + [a[None] for a in smalls]
    shapes = [g.shape[1:] for g in grads] + [a.shape for a in smalls]
    routes = [(_route_sibling, 4)] * len(grads) + [(_route_sibling_whole, 1)] * len(smalls)
    return _xchg_start(srcs, shapes, routes, f"rs_{tag}_swap_start", after), routes, len(grads)


def _swap_sums(swap, after, tag, small_dtypes=None):
    (send1, recv1, src1, land1, _), routes, n_rs = swap
    x, y, c, chips = _chips4()
    blocks = jnp.stack([_dev_index(*chip, c) for chip in chips]).astype(jnp.int32)
    src1, land1 = _xchg_wait(send1, recv1, src1, land1, routes, after, f"rs_{tag}_swap_wait")
    return _rs_sum(src1[:n_rs], land1[:n_rs], blocks, f"rs_{tag}_sum", src1[n_rs:], land1[n_rs:],
                   small_dtypes or [F32] * (len(src1) - n_rs))


def _chips_start(sums, small_sums, tag, everyone=()):
    sends = [ps[1] for ps in sums] + list(small_sums) + [a[None] for a in everyone]
    routes = [(_route_chips, 3)] * len(sums) + [(_route_chips_whole, 3)] * len(small_sums) + [(_route_everyone, 7)] * len(everyone)
    return _xchg_start(sends, [a.shape[1:] for a in sends], routes, f"rs_{tag}_chips_start"), routes


def _chips_wait(chips, after, tag):
    (send2, recv2, src2, land2, _), routes = chips
    return _xchg_wait(send2, recv2, src2, land2, routes, after, f"rs_{tag}_chips_wait")


def kernel(x, p, norm1_gain, w_in, gmlp_v_gain, w_spatial, b_spatial, attn_sinks, rel_bias_table, w_out, norm2_gain, w_ff1, w_ff2, w_ple_proj, w_ple_gate, final_gain, loss_target, m_norm1_gain, m_w_in, m_gmlp_v_gain, m_w_spatial, m_b_spatial, m_attn_sinks, m_rel_bias_table, m_w_out, m_norm2_gain, m_w_ff1, m_w_ff2, m_w_ple_proj, m_w_ple_gate, m_final_gain, v_norm1_gain, v_w_in, v_gmlp_v_gain, v_w_spatial, v_b_spatial, v_attn_sinks, v_rel_bias_table, v_w_out, v_norm2_gain, v_w_ff1, v_w_ff2, v_w_ple_proj, v_w_ple_gate, v_final_gain):
    args = dict(locals())
    s = x.shape[1]
    big = ("w_in", "w_out", "w_ff1", "w_ff2", "w_ple_proj", "w_ple_gate")

    x2, p2, t2 = x.reshape(s, D_MODEL), p.reshape(s, PLE_DIM), loss_target.reshape(s, D_MODEL)
    g1, gv, w_sp, b_sp, sinks, table, g2, gf = (args[n] for n in SMALL)
    bucket = jnp.asarray(_bucket_table())
    b_col = b_sp.reshape(GROUPS, BLOCK, 1)

    def shard(name):
        return args[name][0].T if name.endswith("w_in") else args[name][0]

    (send_in, recv_in, fly_in), (rest_sems, fly_rest), token = _ag_start([shard(big[0])], [shard(n) for n in big[1:]])
    mid_in, fwd_send_in, fwd_recv_in = _ag_mid(fly_in, send_in, recv_in, token, "ag_mid_w_in")
    g_in = _ag_end(mid_in, fwd_send_in, fwd_recv_in, "ag_end_w_in")[0]
    full_in = g_in.reshape(D_IN, D_MODEL)

    zuv, qkv, hn1 = _in_proj(x2, g1, full_in)
    fly_rest, relay_sems, relayed = _ag_split_call(
        fly_rest, [("relay_halves", *rest_sems[:2])], ["relays"], zuv, "ag_relay")
    mix, *saved = _mixer_fwd(zuv, qkv, gv, w_sp[0], b_col, sinks, table.T, bucket, after=relayed)
    fly_rest, fwd_sems, _ = _ag_split_call(
        fly_rest, [("others", *rest_sems[2:]), ("relays", *relay_sems)], ["near", "far"], mix, "ag_mid_rest")
    g_out, g_ff1, g_ff2, g_proj, g_gate = _ag_split_call(
        fly_rest, [("near", *fwd_sems[:2]), ("far", *fwd_sems[2:])], [], mix, "ag_end_rest")[0]
    full_out, full_ff2, full_gate = g_out.reshape(D_MODEL, D_MODEL), g_ff2.reshape(D_FF, D_MODEL), g_gate.reshape(D_MODEL, D_MODEL)
    full_proj = g_proj.transpose(1, 0, 2).reshape(PLE_DIM, D_MODEL)

    tail_small, dh1, dmix, hn2, a, df, dh2, gw_out, gw_proj, gw_gate = _tail(
        x2, mix, p2, t2, g2, gf.reshape(1, D_MODEL), full_out, g_ff1, full_ff2, full_gate, full_proj)
    gw_out, gw_gate = (g.reshape(N_DEV, D_MODEL // N_DEV, D_MODEL) for g in (gw_out, gw_gate))

    few_names = ("w_out", "w_ple_proj", "w_ple_gate")
    ff1_names = ("w_ff1",) + few_names
    gw_ff1 = _wgrad(hn2, df, D_MODEL, 2 * D_FF // N_DEV, "wgrad_ff1", peer_cols=D_FF // N_DEV)
    ff1_swap = _swap_start([gw_ff1, gw_out, gw_proj, gw_gate], [tail_small], "ff1")
    gw_ff2 = _wgrad(a, dh2, 1024, D_MODEL, "wgrad_ff2", after=ff1_swap[0][4]).reshape(N_DEV, D_FF // N_DEV, D_MODEL)
    ff1_sums, ff1_small = _swap_sums(ff1_swap, gw_ff2, "ff1")
    ff1_chips = _chips_start(ff1_sums, ff1_small, "ff1")
    ff2_swap = _swap_start([gw_ff2], [], "ff2", after=ff1_chips[0][4])
    dzm, dkv, d_gv, d_wsp, d_bsp, d_attn = _mixer_bwd(
        zuv, qkv, dmix, saved, gv, w_sp[0], b_col, bucket, after=ff2_swap[0][4])
    ff2_sums, _ = _swap_sums(ff2_swap, dzm, "ff2")
    ff2_chips = _chips_start(ff2_sums, [], "ff2")
    gw_in = _wgrad_in(dzm, dkv, hn1, after=ff2_chips[0][4]).reshape(N_DEV, D_IN // N_DEV, D_MODEL)
    in_swap = _swap_start([gw_in], [d_gv, d_wsp, d_bsp, d_attn], "in")
    dx, d_g1 = _in_bwd(x2, dh1, dzm, dkv, g1, full_in, after=in_swap[0][4])
    grad_x = dx.reshape(x.shape)
    in_sums, in_small = _swap_sums(in_swap, dx, "in", small_dtypes=[F32, BF, F32, F32])
    in_chips = _chips_start(in_sums, in_small, "in", everyone=[d_g1])

    grads, deltas, new_m, new_v = {}, {}, {}, {}
    after, small_own, small_recv = in_chips[0][4], [], []
    for chips, sums, names, tag in ((ff1_chips, ff1_sums, ff1_names, "ff1"), (ff2_chips, ff2_sums, ("w_ff2",), "ff2"),
                                    (in_chips, in_sums, ("w_in",), "in")):
        sent, recvs = _chips_wait(chips, after, tag)
        small_own += sent[len(names):]
        small_recv += recvs[len(names):]
        tiled = [n for n in names if n not in few_names]
        done = [_final_adamw(part, recv, shard(n), shard("m_" + n), shard("v_" + n), "adamw_" + n)
                for n, (part, _), recv in zip(tiled, sums, recvs)]
        few = names[len(tiled):]
        if few:
            done += zip(*_final_adamw_few(
                [ps[0] for ps in sums[len(tiled):]], recvs[len(tiled):len(names)], [shard(n) for n in few],
                [shard("m_" + n) for n in few], [shard("v_" + n) for n in few], "adamw_few"))
        for n, results in zip(names, done):
            for dst, arr in zip((grads, deltas, new_m, new_v), results):
                dst[n] = (arr.T if n == "w_in" else arr)[None]
            after = results[1]

    views = {"w_spatial": (GROUPS, BLOCK, BLOCK), "b_spatial": (GROUPS, BLOCK), "final_gain": (1, D_MODEL)}
    def view(name, base):
        return args[name].T if base == "rel_bias_table" else args[name].reshape(views.get(base, args[name].shape))

    small_in = [[view(pre + n, n) for n in SMALL] for pre in ("", "m_", "v_")]
    loss, *small_out = _adamw_small(small_own[:-1], small_recv[:-1], small_own[-1], small_recv[-1], *small_in, after=after)
    for dst, arrays in zip((grads, deltas, new_m, new_v), small_out):
        for n, arr in zip(SMALL, arrays):
            dst[n] = arr.T if n == "rel_bias_table" else arr.reshape(args[n].shape)
    loss = loss[0, 0]

    order = ("norm1_gain", "w_in", "gmlp_v_gain", "w_spatial", "b_spatial", "attn_sinks", "rel_bias_table", "w_out",
             "norm2_gain", "w_ff1", "w_ff2", "w_ple_proj", "w_ple_gate", "final_gain")
    return (loss, grad_x, *[grads[n] for n in order], *[deltas[n] for n in order],
            *[new_m[n] for n in order], *[new_v[n] for n in order])
```

```python
import math

import numpy as np
import jax
import jax.numpy as jnp
from jax import lax
from jax.experimental import pallas as pl
from jax.experimental.pallas import tpu as pltpu

F32 = jnp.float32
BF = jnp.bfloat16
MESH = pl.DeviceIdType.MESH
N_DEV = 8

D_MODEL = 1024
PLE_DIM = 256
D_GMLP = 512
GROUPS = 4
GDIM = 128
BLOCK = 128
D_ATTN = 512
HEAD_DIM = 64
N_Q = 8
Q_PER_KV = 4
N_KV = N_Q // Q_PER_KV
ROWS4 = Q_PER_KV * BLOCK
D_KV = 128
D_FF = 4096
D_IN = 1792
D_MAIN = 2 * D_GMLP + D_ATTN
REL_BUCKETS = 32
EPS = 1e-6
NEG_INF = -1e30
SCALE = HEAD_DIM ** -0.5
GELU_C = math.sqrt(2.0 / math.pi)
GELU_A = 0.044715

ADAM_LR = 0.001
ADAM_B1 = 0.9
ADAM_B2 = 0.999
ADAM_EPS = 1e-08
ADAM_WD = 0.01
ADAM_STEP = 10

V7X_VMEM_LIMIT = 61440 * 1024
TOK_TILE = 256
IO_TOK_TILE = 512


def _call(body, after=None, **kw):
    if after is None:
        return pl.pallas_call(body, **kw)
    n_in = len(kw["in_specs"])

    def ordered(*refs):
        body(*refs[:n_in], *refs[n_in + 1:])

    kw["in_specs"] = list(kw["in_specs"]) + [pl.BlockSpec(memory_space=pl.ANY)]
    fn = pl.pallas_call(ordered, **kw)
    return lambda *operands: fn(*operands, after)


def _params(sem=None):
    if sem is None:
        return pltpu.CompilerParams(vmem_limit_bytes=V7X_VMEM_LIMIT)
    return pltpu.CompilerParams(dimension_semantics=sem, vmem_limit_bytes=V7X_VMEM_LIMIT)


def _nn(a, b):
    return jnp.dot(a, b, preferred_element_type=F32)


def _nt(a, b):
    return lax.dot_general(a, b, (((1,), (1,)), ((), ())), preferred_element_type=F32)


def _tn(a, b):
    return lax.dot_general(a, b, (((0,), (0,)), ((), ())), preferred_element_type=F32)


def _gelu_tanh(x):
    return jnp.tanh(GELU_C * (x + GELU_A * (x * x * x)))


def _gelu(x, t):
    return x * (0.5 * (1.0 + t))


def _gelu_and_grad(x, t):
    cdf = 0.5 * (1.0 + t)
    return x * cdf, cdf + 0.5 * x * (1.0 - t * t) * (GELU_C * (1.0 + 3.0 * GELU_A * (x * x)))


def _rms_scale(x):
    return lax.rsqrt(jnp.mean(x * x, axis=-1, keepdims=True) + EPS)


def _rms_bwd(dxn, x, r):
    return r * dxn - x * ((r * r * r) * jnp.mean(dxn * x, axis=-1, keepdims=True))


def _bucket_table():
    a = np.arange(BLOCK)[:, None]
    j = np.arange(2 * BLOCK)[None, :]
    n = BLOCK + a - j
    valid = (n >= 0) & (n < BLOCK)
    nc = np.maximum(n, 0)
    max_exact = REL_BUCKETS // 2
    nf = np.maximum(nc, 1).astype(np.float32)
    large = max_exact + (
        np.log(nf / np.float32(max_exact)) / np.float32(math.log(BLOCK / max_exact)) * np.float32(REL_BUCKETS - max_exact)
    ).astype(np.int32)
    large = np.minimum(large, REL_BUCKETS - 1)
    bucket = np.where(nc < max_exact, nc, large)
    return np.where(valid, bucket, -1).astype(np.int32)


def _in_proj(x, g1, w_in_t):
    s = x.shape[0]
    tm = min(IO_TOK_TILE, s)

    def body(x_ref, g_ref, w_ref, zuv_ref, qkv_ref, hn_ref):
        xv = x_ref[...]
        hn = ((xv * _rms_scale(xv)) * g_ref[...]).astype(BF)
        hn_ref[...] = hn
        z = _nt(hn, w_ref[...])
        zuv_ref[...] = z[:, : 2 * D_GMLP]
        qkv_ref[...] = z[:, 2 * D_GMLP:].astype(BF)

    return _call(
        body,
        name="in_proj",
        grid=(s // tm,),
        in_specs=[
            pl.BlockSpec((tm, D_MODEL), lambda i: (i, 0)),
            pl.BlockSpec((1, D_MODEL), lambda i: (0, 0)),
            pl.BlockSpec((D_IN, D_MODEL), lambda i: (0, 0)),
        ],
        out_specs=[
            pl.BlockSpec((tm, 2 * D_GMLP), lambda i: (i, 0)),
            pl.BlockSpec((tm, D_ATTN + 2 * D_KV), lambda i: (i, 0)),
            pl.BlockSpec((tm, D_MODEL), lambda i: (i, 0)),
        ],
        out_shape=[
            jax.ShapeDtypeStruct((s, 2 * D_GMLP), F32),
            jax.ShapeDtypeStruct((s, D_ATTN + 2 * D_KV), BF),
            jax.ShapeDtypeStruct((s, D_MODEL), BF),
        ],
        compiler_params=_params(("arbitrary",)),
    )(x, g1, w_in_t)


def _head_rows(h):
    kh, g = divmod(h, Q_PER_KV)
    return kh, slice(g * BLOCK, (g + 1) * BLOCK)


def _build_bias(bias_ref, bucket_ref, table_ref):
    bucket = bucket_ref[...]
    for h in range(N_Q):
        acc = jnp.zeros((BLOCK, 2 * BLOCK), F32)
        for b in range(REL_BUCKETS):
            acc = jnp.where(bucket == b, table_ref[h, b], acc)
        kh, rows = _head_rows(h)
        bias_ref[kh, rows, :] = acc


def _window_masks(i):
    row = lax.broadcasted_iota(jnp.int32, (ROWS4, BLOCK), 0) & (BLOCK - 1)
    col = lax.broadcasted_iota(jnp.int32, (ROWS4, BLOCK), 1)
    return (col > row) & (i > 0), col <= row


def _stack_heads(ref, kh, offset):
    first = offset + kh * Q_PER_KV * HEAD_DIM
    return jnp.concatenate(
        [ref[:, first + g * HEAD_DIM: first + (g + 1) * HEAD_DIM].astype(BF) for g in range(Q_PER_KV)], axis=0)


def _stack_sinks(sink_ref, kh):
    return jnp.concatenate([jnp.full((BLOCK, 1), sink_ref[0, kh * Q_PER_KV + g], F32) for g in range(Q_PER_KV)], axis=0)


def _tril_bf16(w_ref, g):
    row = lax.broadcasted_iota(jnp.int32, (BLOCK, BLOCK), 0)
    col = lax.broadcasted_iota(jnp.int32, (BLOCK, BLOCK), 1)
    return jnp.where(col <= row, w_ref[g], 0.0).astype(BF)


def _attn_probs(q_h, k_prev, k_cur, bias_h, sink, valid_prev, valid_cur):
    l_prev = jnp.where(valid_prev, _nt(q_h, k_prev) * SCALE + bias_h[:, :BLOCK], NEG_INF)
    l_cur = jnp.where(valid_cur, _nt(q_h, k_cur) * SCALE + bias_h[:, BLOCK:], NEG_INF)
    m = jnp.maximum(jnp.maximum(jnp.max(l_prev, axis=-1, keepdims=True), jnp.max(l_cur, axis=-1, keepdims=True)), sink)
    e_prev = jnp.exp(l_prev - m)
    e_cur = jnp.exp(l_cur - m)
    e_sink = jnp.exp(sink - m)
    denom = jnp.sum(e_prev, axis=-1, keepdims=True) + jnp.sum(e_cur, axis=-1, keepdims=True) + e_sink
    return e_prev / denom, e_cur / denom, e_sink / denom


def _mixer_specs(nb):
    cl = lambda i: jnp.minimum(i, nb - 1)
    return [
        pl.BlockSpec((BLOCK, 2 * D_GMLP), lambda i: (cl(i), 0)),
        pl.BlockSpec((BLOCK, D_ATTN), lambda i: (cl(i), 0)),
        pl.BlockSpec((BLOCK, 2 * D_KV), lambda i: (cl(i), D_ATTN // (2 * D_KV))),
        pl.BlockSpec((BLOCK, 2 * D_KV), lambda i: (jnp.maximum(cl(i) - 1, 0), D_ATTN // (2 * D_KV))),
        pl.BlockSpec((1, D_GMLP), lambda i: (0, 0)),
        pl.BlockSpec((GROUPS, BLOCK, BLOCK), lambda i: (0, 0, 0)),
        pl.BlockSpec((GROUPS, BLOCK, 1), lambda i: (0, 0, 0)),
        pl.BlockSpec(memory_space=pltpu.SMEM),
        pl.BlockSpec(memory_space=pltpu.SMEM),
        pl.BlockSpec((BLOCK, 2 * BLOCK), lambda i: (0, 0)),
    ]


def _mixer_fwd(zuv, qkv, gv, w_sp, b_sp, sinks, table, bucket, after=None):
    s = zuv.shape[0]
    nb = s // BLOCK

    def body(zuv_ref, q_ref, kvc_ref, kvp_ref, gv_ref, w_ref, b_ref, sink_ref, table_ref, bucket_ref,
             mix_ref, tanh_ref, prob_ref, psink_ref, bias_ref):
        i = pl.program_id(0)

        @pl.when(i == 0)
        def _():
            _build_bias(bias_ref, bucket_ref, table_ref)

        t = _gelu_tanh(zuv_ref[...])
        tanh_ref[...] = t
        u = _gelu(zuv_ref[:, :D_GMLP], t[:, :D_GMLP])
        vg = _gelu(zuv_ref[:, D_GMLP:], t[:, D_GMLP:])
        for g in range(GROUPS):
            sl = slice(g * GDIM, (g + 1) * GDIM)
            vg_g = vg[:, sl]
            vn = ((vg_g * _rms_scale(vg_g)) * gv_ref[:, sl]).astype(BF)
            sv = _nn(_tril_bf16(w_ref, g), vn) + b_ref[g]
            mix_ref[:, sl] = (u[:, sl] * sv).astype(BF)

        valid_prev, valid_cur = _window_masks(i)
        for kh in range(N_KV):
            ksl = slice(kh * HEAD_DIM, (kh + 1) * HEAD_DIM)
            vsl = slice(D_KV + kh * HEAD_DIM, D_KV + (kh + 1) * HEAD_DIM)
            q4 = _stack_heads(q_ref, kh, 0)
            p_prev, p_cur, p_sink = _attn_probs(
                q4, kvp_ref[:, ksl], kvc_ref[:, ksl], bias_ref[kh], _stack_sinks(sink_ref, kh), valid_prev, valid_cur)
            prob_ref[0, kh, :, :BLOCK] = p_prev
            prob_ref[0, kh, :, BLOCK:] = p_cur
            psink_ref[0, kh] = jnp.broadcast_to(p_sink, (ROWS4, LANES))
            o4 = _nn(p_prev.astype(BF), kvp_ref[:, vsl]) + _nn(p_cur.astype(BF), kvc_ref[:, vsl])
            for g in range(Q_PER_KV):
                first = D_GMLP + (kh * Q_PER_KV + g) * HEAD_DIM
                mix_ref[:, first:first + HEAD_DIM] = o4[g * BLOCK:(g + 1) * BLOCK].astype(BF)

    return _call(
        body,
        name="mixer_fwd",
        after=after,
        grid=(nb,),
        in_specs=_mixer_specs(nb),
        out_specs=[
            pl.BlockSpec((BLOCK, D_MODEL), lambda i: (i, 0)),
            pl.BlockSpec((BLOCK, 2 * D_GMLP), lambda i: (i, 0)),
            pl.BlockSpec((1, N_KV, ROWS4, 2 * BLOCK), lambda i: (i, 0, 0, 0)),
            pl.BlockSpec((1, N_KV, ROWS4, LANES), lambda i: (i, 0, 0, 0)),
        ],
        out_shape=[
            jax.ShapeDtypeStruct((s, D_MODEL), BF),
            jax.ShapeDtypeStruct((s, 2 * D_GMLP), F32),
            jax.ShapeDtypeStruct((nb, N_KV, ROWS4, 2 * BLOCK), F32),
            jax.ShapeDtypeStruct((nb, N_KV, ROWS4, LANES), F32),
        ],
        scratch_shapes=[pltpu.VMEM((N_KV, ROWS4, 2 * BLOCK), F32)],
        compiler_params=_params(("arbitrary",)),
    )(zuv, qkv, qkv, qkv, gv, w_sp, b_sp, sinks, table, bucket)


def _tail(x, mix, p, t, g2, gf, w_out, w_ff1, w_ff2, w_gate, w_proj):
    s = x.shape[0]
    tm = min(TOK_TILE, s)
    n_ff = w_ff1.shape[0]
    fc = D_FF // n_ff
    pc = D_MODEL // N_DEV

    def body(x_ref, mix_ref, p_ref, t_ref, g2_ref, gf_ref, wo_ref, w1_ref, w2_ref, wg_ref, wp_ref,
             small_ref, dh1_ref, dmix_ref, hn2_ref, a_ref, df_ref, dh2_ref, gwo_ref, gwp_ref, gwg_ref, f_ref):
        i = pl.program_id(0)

        @pl.when(i == 0)
        def _():
            small_ref[...] = jnp.zeros_like(small_ref)
            gwo_ref[...] = jnp.zeros_like(gwo_ref)
            gwp_ref[...] = jnp.zeros_like(gwp_ref)
            gwg_ref[...] = jnp.zeros_like(gwg_ref)

        h1 = x_ref[...] + _nn(mix_ref[...], wo_ref[...])
        dh1_ref[...] = h1
        r2 = _rms_scale(h1)
        hn2_ref[...] = ((h1 * r2) * g2_ref[...]).astype(BF)
        h2 = h1
        for c in range(n_ff):
            f = _nn(hn2_ref[...], w1_ref[c])
            f_ref[:, c * fc:(c + 1) * fc] = f
            a = jnp.square(jnp.maximum(f, 0.0)).astype(BF)
            a_ref[:, c * fc:(c + 1) * fc] = a
            h2 = h2 + _nn(a, w2_ref[c * fc:(c + 1) * fc, :])
        h2b = h2.astype(BF)
        gate = jax.nn.sigmoid(_nn(h2b, wg_ref[...]))
        pb = p_ref[...].astype(BF)
        pp = jnp.concatenate([_nn(pb, wp_ref[q]) for q in range(N_DEV)], axis=1)
        h3 = h2 + gate * pp
        rf = _rms_scale(h3)
        gf_v = gf_ref[...]
        err = (h3 * rf) * gf_v - t_ref[...]
        small_ref[2:3, :] += jnp.sum(jnp.sum(err * err, axis=-1, keepdims=True), axis=0, keepdims=True) * (0.5 / D_MODEL)

        dy = err * (1.0 / D_MODEL)
        small_ref[1:2, :] += jnp.sum(dy * (h3 * rf), axis=0, keepdims=True)
        dh3 = _rms_bwd(dy * gf_v, h3, rf)
        gw_proj = _tn(pb, (dh3 * gate).astype(BF))
        for q in range(N_DEV):
            gwp_ref[q] += gw_proj[:, q * pc:(q + 1) * pc]
        dgl = ((dh3 * pp) * (gate * (1.0 - gate))).astype(BF)
        gwg_ref[...] += _tn(h2b, dgl)
        dh2 = dh3 + _nt(dgl, wg_ref[...])
        dmix_ref[...] = dh2
        dh2_ref[...] = dh2.astype(BF)
        dhn2 = jnp.zeros((tm, D_MODEL), F32)
        for c in range(n_ff):
            da = _nt(dh2_ref[...], w2_ref[c * fc:(c + 1) * fc, :])
            df = (da * (2.0 * jnp.maximum(f_ref[:, c * fc:(c + 1) * fc], 0.0))).astype(BF)
            df_ref[:, c * fc:(c + 1) * fc] = df
            dhn2 = dhn2 + _nt(df, w1_ref[c])
        h1 = dh1_ref[...]
        small_ref[0:1, :] += jnp.sum(dhn2 * (h1 * r2), axis=0, keepdims=True)
        dh1 = dmix_ref[...] + _rms_bwd(dhn2 * g2_ref[...], h1, r2)
        dh1_ref[...] = dh1
        dh1b = dh1.astype(BF)
        gwo_ref[...] += _tn(mix_ref[...], dh1b)
        dmix_ref[...] = _nt(dh1b, wo_ref[...])

    tile = lambda cols: pl.BlockSpec((tm, cols), lambda i: (i, 0))
    whole = lambda shape: pl.BlockSpec(shape, lambda i: (0,) * len(shape), pipeline_mode=pl.Buffered(1))
    total = lambda shape: pl.BlockSpec(shape, lambda i: (0,) * len(shape))
    row = pl.BlockSpec((1, D_MODEL), lambda i: (0, 0))
    act = lambda cols, dt: jax.ShapeDtypeStruct((s, cols), dt)
    gw_shapes = [(D_MODEL, D_MODEL), (N_DEV, PLE_DIM, pc), (D_MODEL, D_MODEL)]
    return _call(
        body,
        name="tail",
        grid=(s // tm,),
        in_specs=[tile(D_MODEL), tile(D_MODEL), tile(PLE_DIM), tile(D_MODEL), row, row,
                  whole(w_out.shape), whole(w_ff1.shape), whole(w_ff2.shape), whole(w_gate.shape), whole(w_proj.shape)],
        out_specs=[total((8, D_MODEL)), tile(D_MODEL), tile(D_MODEL), tile(D_MODEL), tile(D_FF), tile(D_FF), tile(D_MODEL),
                   *[total(shape) for shape in gw_shapes]],
        out_shape=[jax.ShapeDtypeStruct((8, D_MODEL), F32),
                   act(D_MODEL, F32), act(D_MODEL, F32), act(D_MODEL, BF), act(D_FF, BF), act(D_FF, BF), act(D_MODEL, BF),
                   *[jax.ShapeDtypeStruct(shape, F32) for shape in gw_shapes]],
        scratch_shapes=[pltpu.VMEM((tm, D_FF), F32)],
        compiler_params=_params(("arbitrary",)),
    )(x, mix, p, t, g2, gf, w_out, w_ff1, w_ff2, w_gate, w_proj)


def _mixer_bwd(zuv, qkv, dmix, saved, gv, w_sp, b_sp, bucket, after=None):
    s = zuv.shape[0]
    nb = s // BLOCK

    def body(zuv_ref, q_ref, kvc_ref, kvp_ref, gv_ref, w_ref, b_ref, bucket_ref, dmix_ref, tanh_ref, prob_ref, psink_ref,
             dzm_ref, dkv_ref, dgv_ref, dw_ref, db_ref, dattn_ref,
             dbias_ref, carry_ref, dsink_acc, db_acc):
        i = pl.program_id(0)

        @pl.when(i == 0)
        def _():
            dbias_ref[...] = jnp.zeros_like(dbias_ref)
            carry_ref[...] = jnp.zeros_like(carry_ref)
            dsink_acc[...] = jnp.zeros_like(dsink_acc)
            dgv_ref[...] = jnp.zeros_like(dgv_ref)
            dw_ref[...] = jnp.zeros_like(dw_ref)
            db_acc[...] = jnp.zeros_like(db_acc)

        @pl.when(i < nb)
        def _():
            u, du_dz = _gelu_and_grad(zuv_ref[:, :D_GMLP], tanh_ref[:, :D_GMLP])
            vg, dvg_dz = _gelu_and_grad(zuv_ref[:, D_GMLP:], tanh_ref[:, D_GMLP:])
            for g in range(GROUPS):
                sl = slice(g * GDIM, (g + 1) * GDIM)
                vg_g = vg[:, sl]
                rg = _rms_scale(vg_g)
                vhat = vg_g * rg
                gain = gv_ref[:, sl]
                vn = (vhat * gain).astype(BF)
                w_g = _tril_bf16(w_ref, g)
                sv = _nn(w_g, vn) + b_ref[g]
                dmix_a = dmix_ref[:, sl]
                dsv = dmix_a * u[:, sl]
                dsvb = dsv.astype(BF)
                db_acc[g] += jnp.sum(dsv, axis=-1, keepdims=True)
                dw_ref[g] += _nt(dsvb, vn)
                dvn = _tn(w_g, dsvb)
                dgv_ref[:, sl] += jnp.sum(dvn * vhat, axis=0, keepdims=True)
                dvg = _rms_bwd(dvn * gain, vg_g, rg)
                dzm_ref[:, sl] = ((dmix_a * sv) * du_dz[:, sl]).astype(BF)
                dzm_ref[:, D_GMLP + g * GDIM: D_GMLP + (g + 1) * GDIM] = (dvg * dvg_dz[:, sl]).astype(BF)

            for kh in range(N_KV):
                ksl = slice(kh * HEAD_DIM, (kh + 1) * HEAD_DIM)
                vsl = slice(D_KV + kh * HEAD_DIM, D_KV + (kh + 1) * HEAD_DIM)
                k_prev, k_cur = kvp_ref[:, ksl], kvc_ref[:, ksl]
                v_prev, v_cur = kvp_ref[:, vsl], kvc_ref[:, vsl]
                q4 = _stack_heads(q_ref, kh, 0)
                p_prev, p_cur, p_sink = prob_ref[0, kh, :, :BLOCK], prob_ref[0, kh, :, BLOCK:], psink_ref[0, kh, :, 0:1]
                do4 = _stack_heads(dmix_ref, kh, D_GMLP)
                dp_prev = _nt(do4, v_prev)
                dp_cur = _nt(do4, v_cur)
                delta = jnp.sum(p_prev * dp_prev, axis=-1, keepdims=True) + jnp.sum(p_cur * dp_cur, axis=-1, keepdims=True)
                ds_prev = p_prev * (dp_prev - delta)
                ds_cur = p_cur * (dp_cur - delta)
                dsink_acc[kh] -= p_sink * delta
                dbias_ref[kh, :, :BLOCK] += ds_prev
                dbias_ref[kh, :, BLOCK:] += ds_cur
                dsb_prev = ds_prev.astype(BF)
                dsb_cur = ds_cur.astype(BF)
                dq4 = (_nn(dsb_prev, k_prev) + _nn(dsb_cur, k_cur)) * SCALE
                for g in range(Q_PER_KV):
                    first = 2 * D_GMLP + (kh * Q_PER_KV + g) * HEAD_DIM
                    dzm_ref[:, first:first + HEAD_DIM] = dq4[g * BLOCK:(g + 1) * BLOCK].astype(BF)
                dkv_ref[:, ksl] = (carry_ref[:, ksl] + _tn(dsb_prev, q4) * SCALE).astype(BF)
                dkv_ref[:, vsl] = (carry_ref[:, vsl] + _tn(p_prev.astype(BF), do4)).astype(BF)
                carry_ref[:, ksl] = _tn(dsb_cur, q4) * SCALE
                carry_ref[:, vsl] = _tn(p_cur.astype(BF), do4)

        @pl.when(i == nb)
        def _():
            dkv_ref[...] = carry_ref[...].astype(BF)
            row = lax.broadcasted_iota(jnp.int32, (BLOCK, BLOCK), 0)
            col = lax.broadcasted_iota(jnp.int32, (BLOCK, BLOCK), 1)
            for g in range(GROUPS):
                dw_ref[g] = jnp.where(col <= row, dw_ref[g], 0.0)
                db_ref[g:g + 1, :] = jnp.sum(jnp.where(col == row, db_acc[g], 0.0), axis=0, keepdims=True)
            bucket = bucket_ref[...]
            for b in range(N_Q, REL_BUCKETS):
                dattn_ref[N_Q, b] = 0.0
            for h in range(N_Q):
                kh, rows = _head_rows(h)
                dattn_ref[N_Q, h] = jnp.sum(dsink_acc[kh, rows, :])
                dbh = dbias_ref[kh, rows, :]
                for b in range(REL_BUCKETS):
                    dattn_ref[h, b] = jnp.sum(jnp.where(bucket == b, dbh, 0.0))

    cl = lambda i: jnp.minimum(i, nb - 1)
    const = lambda shape: pl.BlockSpec(shape, lambda i: (0,) * len(shape))
    return _call(
        body,
        name="mixer_bwd",
        after=after,
        grid=(nb + 1,),
        in_specs=_mixer_specs(nb)[:7] + [
            const((BLOCK, 2 * BLOCK)),
            pl.BlockSpec((BLOCK, D_MODEL), lambda i: (cl(i), 0)),
            pl.BlockSpec((BLOCK, 2 * D_GMLP), lambda i: (cl(i), 0)),
            pl.BlockSpec((1, N_KV, ROWS4, 2 * BLOCK), lambda i: (cl(i), 0, 0, 0)),
            pl.BlockSpec((1, N_KV, ROWS4, LANES), lambda i: (cl(i), 0, 0, 0)),
        ],
        out_specs=[
            pl.BlockSpec((BLOCK, D_MAIN), lambda i: (cl(i), 0)),
            pl.BlockSpec((BLOCK, 2 * D_KV), lambda i: (jnp.maximum(i - 1, 0), 0)),
            const((1, D_GMLP)),
            const((GROUPS, BLOCK, BLOCK)),
            const((GROUPS, BLOCK)),
            pl.BlockSpec(memory_space=pltpu.SMEM),
        ],
        out_shape=[
            jax.ShapeDtypeStruct((s, D_MAIN), BF),
            jax.ShapeDtypeStruct((s, 2 * D_KV), BF),
            jax.ShapeDtypeStruct((1, D_GMLP), F32),
            jax.ShapeDtypeStruct((GROUPS, BLOCK, BLOCK), F32),
            jax.ShapeDtypeStruct((GROUPS, BLOCK), F32),
            jax.ShapeDtypeStruct((N_Q + 1, REL_BUCKETS), F32),
        ],
        scratch_shapes=[
            pltpu.VMEM((N_KV, ROWS4, 2 * BLOCK), F32),
            pltpu.VMEM((BLOCK, 2 * D_KV), F32),
            pltpu.VMEM((N_KV, ROWS4, 1), F32),
            pltpu.VMEM((GROUPS, BLOCK, 1), F32),
        ],
        compiler_params=_params(("arbitrary",)),
    )(zuv, qkv, qkv, qkv, gv, w_sp, b_sp, bucket, dmix, *saved)


def _in_bwd(x, dh1, dzm, dkv, g1, w_in_t, after=None):
    s = x.shape[0]
    tm = min(IO_TOK_TILE, s)

    def body(x_ref, dh1_ref, dzm_ref, dkv_ref, g_ref, w_ref, dx_ref, dg_ref):
        @pl.when(pl.program_id(0) == 0)
        def _():
            dg_ref[...] = jnp.zeros_like(dg_ref)

        dhn = _nn(dzm_ref[...], w_ref[:D_MAIN, :]) + _nn(dkv_ref[...], w_ref[D_MAIN:, :])
        xv = x_ref[...]
        r = _rms_scale(xv)
        dg_ref[...] += jnp.sum(dhn * (xv * r), axis=0, keepdims=True)
        dx_ref[...] = dh1_ref[...] + _rms_bwd(dhn * g_ref[...], xv, r)

    tile = lambda cols: pl.BlockSpec((tm, cols), lambda i: (i, 0))
    row = pl.BlockSpec((1, D_MODEL), lambda i: (0, 0))
    return _call(
        body,
        name="in_bwd",
        after=after,
        grid=(s // tm,),
        in_specs=[tile(D_MODEL), tile(D_MODEL), tile(D_MAIN), tile(2 * D_KV), row, pl.BlockSpec((D_IN, D_MODEL), lambda i: (0, 0))],
        out_specs=[tile(D_MODEL), row],
        out_shape=[jax.ShapeDtypeStruct((s, D_MODEL), F32), jax.ShapeDtypeStruct((1, D_MODEL), F32)],
        compiler_params=_params(("arbitrary",)),
    )(x, dh1, dzm, dkv, g1, w_in_t)


def _wgrad_in(dzm, dkv, hn, after=None):
    s = hn.shape[0]
    tm = 2 * D_KV
    n_main = D_MAIN // tm

    def body(dzm_ref, dkv_ref, hn_ref, o_ref):
        i = pl.program_id(0)

        @pl.when(i < n_main)
        def _():
            o_ref[...] = _tn(dzm_ref[...], hn_ref[...])

        @pl.when(i == n_main)
        def _():
            o_ref[...] = _tn(dkv_ref[...], hn_ref[...])

    return _call(
        body,
        name="wgrad_in",
        after=after,
        grid=(n_main + 1,),
        in_specs=[
            pl.BlockSpec((s, tm), lambda i: (0, jnp.minimum(i, n_main - 1))),
            pl.BlockSpec((s, tm), lambda i: (0, 0)),
            pl.BlockSpec((s, D_MODEL), lambda i: (0, 0)),
        ],
        out_specs=pl.BlockSpec((tm, D_MODEL), lambda i: (i, 0)),
        out_shape=jax.ShapeDtypeStruct((D_IN, D_MODEL), F32),
        compiler_params=_params(("arbitrary",)),
    )(dzm, dkv, hn)


def _wgrad(a, b, tm, tn, name, peer_cols=0, after=None):
    s, m = a.shape
    n = b.shape[1]

    def body(a_ref, b_ref, o_ref, at_ref):
        @pl.when(pl.program_id(1) == 0)
        def _():
            at_ref[...] = a_ref[...].astype(BF).T

        r = _nn(at_ref[...], b_ref[...])
        if peer_cols:
            for q in range(tn // peer_cols):
                o_ref[q] = r[:, q * peer_cols:(q + 1) * peer_cols]
        else:
            o_ref[...] = r

    if peer_cols:
        out_spec = pl.BlockSpec((tn // peer_cols, tm, peer_cols), lambda i, j: (j, i, 0))
        out_shape = jax.ShapeDtypeStruct((n // peer_cols, m, peer_cols), F32)
    else:
        out_spec = pl.BlockSpec((tm, tn), lambda i, j: (i, j))
        out_shape = jax.ShapeDtypeStruct((m, n), F32)
    return _call(
        body,
        name=name,
        after=after,
        grid=(m // tm, n // tn),
        in_specs=[pl.BlockSpec((s, tm), lambda i, j: (0, i)), pl.BlockSpec((s, tn), lambda i, j: (0, j))],
        out_specs=out_spec,
        out_shape=out_shape,
        scratch_shapes=[pltpu.VMEM((tm, s), BF)],
        compiler_params=_params(("arbitrary", "arbitrary")),
    )(a, b)


def _adamw_math(w, g, m, v):
    m_new = ADAM_B1 * m + (1.0 - ADAM_B1) * g
    v_new = ADAM_B2 * v + (1.0 - ADAM_B2) * jnp.square(g)
    m_hat = m_new / (1.0 - ADAM_B1 ** ADAM_STEP)
    v_hat = v_new / (1.0 - ADAM_B2 ** ADAM_STEP)
    delta = -ADAM_LR * (m_hat / (jnp.sqrt(v_hat) + ADAM_EPS) + ADAM_WD * w)
    return delta, m_new, v_new


def _final_adamw(part, recv, w, m, v, name):
    r, c = w.shape
    tr = min(r, 512)

    def body(p_ref, r_ref, w_ref, m_ref, v_ref, g_ref, d_ref, mo_ref, vo_ref):
        g = p_ref[...]
        for j in range(3):
            g = g + r_ref[j].astype(F32)
        g_ref[...] = g
        d_ref[...], mo_ref[...], vo_ref[...] = _adamw_math(w_ref[...], g, m_ref[...], v_ref[...])

    spec = pl.BlockSpec((tr, c), lambda i: (i, 0))
    return _call(
        body,
        name=name,
        grid=(r // tr,),
        in_specs=[spec, pl.BlockSpec((3, tr, c), lambda i: (0, i, 0)), spec, spec, spec],
        out_specs=[spec] * 4,
        out_shape=[jax.ShapeDtypeStruct((r, c), F32)] * 4,
        compiler_params=_params(("arbitrary",)),
    )(part, recv, w, m, v)


def _rs_sum(gs, lands, blocks, name, smalls, small_lands, small_dtypes):
    n, n_small = len(gs), len(smalls)
    steps = pl.cdiv(gs[0].shape[1], 256)
    assert all(g.shape[1] % steps == 0 for g in gs)

    def body(blk_ref, *refs):
        ins, small_ins, outs = refs[:5 * n], refs[5 * n:5 * n + 2 * n_small], refs[5 * n + 2 * n_small:]
        for k in range(n):
            g0_ref, g1_ref, g2_ref, g3_ref, l_ref = ins[5 * k:5 * k + 5]
            outs[2 * k][...] = g0_ref[0] + l_ref[0]
            for j, gj_ref in enumerate((g1_ref, g2_ref, g3_ref)):
                outs[2 * k + 1][j] = (gj_ref[0] + l_ref[j + 1]).astype(BF)
        for q in range(n_small):
            outs[2 * n + q][...] = (small_ins[q][...] + small_ins[n_small + q][...]).astype(small_dtypes[q])

    def whole(a):
        return pl.BlockSpec(a.shape, lambda i, blk: (0,) * a.ndim)

    in_specs, out_specs, out_shape, operands = [], [], [], []
    for g, land in zip(gs, lands):
        _, r, c = g.shape
        tr = r // steps
        in_specs += [pl.BlockSpec((1, tr, c), lambda i, blk, j=j: (blk[j], i, 0)) for j in range(4)]
        in_specs.append(pl.BlockSpec((4, tr, c), lambda i, blk: (0, i, 0)))
        out_specs += [pl.BlockSpec((tr, c), lambda i, blk: (i, 0)), pl.BlockSpec((3, tr, c), lambda i, blk: (0, i, 0))]
        out_shape += [jax.ShapeDtypeStruct((r, c), F32), jax.ShapeDtypeStruct((3, r, c), BF)]
        operands += [g, g, g, g, land]
    in_specs += [whole(a) for a in (*smalls, *small_lands)]
    out_specs += [whole(a) for a in smalls]
    out_shape += [jax.ShapeDtypeStruct(a.shape, dt) for a, dt in zip(smalls, small_dtypes)]
    outs = _call(
        body,
        name=name,
        grid_spec=pltpu.PrefetchScalarGridSpec(num_scalar_prefetch=1, grid=(steps,), in_specs=in_specs, out_specs=out_specs),
        out_shape=out_shape,
        compiler_params=_params(("arbitrary",)),
    )(blocks, *operands, *smalls, *small_lands)
    return [(outs[2 * k], outs[2 * k + 1]) for k in range(n)], list(outs[2 * n:])


def _final_adamw_few(parts, recvs, ws, ms, vs, name):
    n = len(ws)

    def body(*refs):
        ins, outs = refs[:5 * n], refs[5 * n:]
        for k in range(n):
            p_ref, r_ref, w_ref, m_ref, v_ref = (ins[q * n + k] for q in range(5))
            g = p_ref[...]
            for j in range(3):
                g = g + r_ref[j].astype(F32)
            outs[k][...] = g
            outs[n + k][...], outs[2 * n + k][...], outs[3 * n + k][...] = _adamw_math(w_ref[...], g, m_ref[...], v_ref[...])

    outs = _call(
        body,
        name=name,
        out_shape=[jax.ShapeDtypeStruct(w.shape, F32) for w in ws] * 4,
        compiler_params=_params(),
    )(*parts, *recvs, *ws, *ms, *vs)
    return outs[:n], outs[n:2 * n], outs[2 * n:3 * n], outs[3 * n:]


def _adamw_small(own, recv, g1_own, g1_recv, weights, moms, vels, after):
    n_w = len(weights)

    def body(*refs):
        own_refs, recv_refs, g1_own_ref, g1_recv_ref = refs[:5], refs[5:10], refs[10], refs[11]
        w_refs, m_refs, v_refs = refs[12:12 + n_w], refs[12 + n_w:12 + 2 * n_w], refs[12 + 2 * n_w:12 + 3 * n_w]
        outs = refs[12 + 3 * n_w:]
        loss_ref, g_refs, d_refs = outs[0], outs[1:1 + n_w], outs[1 + n_w:1 + 2 * n_w]
        mo_refs, vo_refs = outs[1 + 2 * n_w:1 + 3 * n_w], outs[1 + 3 * n_w:]
        x, y, c = lax.axis_index("x"), lax.axis_index("y"), lax.axis_index("c")

        def in_place_order(values, my_place):
            acc = None
            for place in range(len(values)):
                r = place ^ my_place
                term = values[-1]
                for q in range(len(values) - 2, -1, -1):
                    term = jnp.where(r == q, values[q], term)
                acc = term if acc is None else acc + term
            return acc

        def total(k, *index):
            index = index or (slice(None),) * (len(own_refs[k].shape) - 1)
            across = [own_refs[k][(0, *index)], recv_refs[k][(1, *index)], recv_refs[k][(0, *index)], recv_refs[k][(2, *index)]]
            return in_place_order([v.astype(F32) for v in across], 2 * x + y)

        g1 = in_place_order([g1_own_ref[0]] + [g1_recv_ref[j] for j in range(N_DEV - 1)], 4 * x + 2 * y + c)
        grads = [
            g1, total(1), total(2), total(3),
            total(4, slice(N_Q, None), slice(0, N_Q)), total(4, slice(0, N_Q), slice(None)),
            total(0, slice(0, 1), slice(None)), total(0, slice(1, 2), slice(None))]
        loss_ref[...] = total(0, slice(2, 3), slice(0, 1))
        for k in range(n_w):
            g_refs[k][...] = grads[k]
            d_refs[k][...], mo_refs[k][...], vo_refs[k][...] = _adamw_math(w_refs[k][...], grads[k], m_refs[k][...], v_refs[k][...])

    shapes = [jax.ShapeDtypeStruct(w.shape, F32) for w in weights]
    outs = _call(
        body,
        name="adamw_small",
        after=after,
        in_specs=[pl.BlockSpec(memory_space=pltpu.VMEM)] * (12 + 3 * n_w),
        out_shape=[jax.ShapeDtypeStruct((1, 1), F32)] + shapes * 4,
        compiler_params=_params(),
    )(*own, *recv, g1_own, g1_recv, *weights, *moms, *vels)
    return outs[0], outs[1:1 + n_w], outs[1 + n_w:1 + 2 * n_w], outs[1 + 2 * n_w:1 + 3 * n_w], outs[1 + 3 * n_w:]


def _place():
    x, y, c = lax.axis_index("x"), lax.axis_index("y"), lax.axis_index("c")
    return x, y, c, [(1 - x, y), (x, 1 - y), (1 - x, 1 - y)]


def _dev_index(px, py, pc):
    return 4 * px + 2 * py + pc


HBM_SPEC = pl.BlockSpec(memory_space=pltpu.HBM)
SEM_SPEC = pl.BlockSpec(memory_space=pltpu.SEMAPHORE)
ANY_SPEC = pl.BlockSpec(memory_space=pl.ANY)
DATAFLOW = pltpu.SideEffectType.DATAFLOW_SIDE_EFFECTING


def _hbm(a):
    return pltpu.with_memory_space_constraint(a, pltpu.HBM)


def _ag_piece(land_k, block, half, peer, send_sem, recv_sem):
    ref = land_k.at[_dev_index(*block)]
    if half is not None:
        rows = land_k.shape[1] // 2
        ref = ref.at[pl.ds(half * rows, rows)]
    return pltpu.make_async_remote_copy(
        src_ref=ref, dst_ref=ref, send_sem=send_sem, recv_sem=recv_sem, device_id=peer, device_id_type=MESH)


def _ag_plan():
    x, y, c, _ = _place()
    me, sib = (x, y, c), (x, y, 1 - c)
    xn, yn, diag = (1 - x, y, c), (x, 1 - y, c), (1 - x, 1 - y, c)
    return dict(
        relay_halves=[(me, 0, xn), (me, 1, yn)],
        others=[(me, None, sib), (me, 1, xn), (me, 0, yn)],
        relays=[(xn, 0, yn), (yn, 1, xn)],
        near=[(xn, None, sib), (yn, None, sib)],
        far=[(diag, None, sib)],
    )


def _ag_stage(land, stage, send_sems, recv_sems, act):
    copies = _ag_plan()[stage]
    n = len(copies)
    for k in range(len(land)):
        for j, (block, half, peer) in enumerate(copies):
            cp = _ag_piece(land[k], block, half, peer, send_sems.at[n * k + j], recv_sems.at[n * k + j])
            if act == "start":
                cp.start()
            else:
                cp.wait_send()
                cp.wait_recv()


def _sem_shapes(*counts):
    return [pltpu.SemaphoreType.DMA((n,)) for n in counts for _ in range(2)]


def _ag_start(first, rest):
    shards = list(first) + list(rest)
    k_n, k_first = len(shards), len(first)
    k_rest = k_n - k_first

    def body(*refs):
        ins, sems, land = refs[:k_n], refs[k_n:k_n + 6], refs[k_n + 6:2 * k_n + 6]
        token, stage, wide = refs[2 * k_n + 6], refs[2 * k_n + 7:3 * k_n + 7], refs[3 * k_n + 7:4 * k_n + 7]
        own_sems = refs[4 * k_n + 7]
        x, y, c, chips = _place()
        loads = [pltpu.make_async_copy(ins[k], wide[k], own_sems.at[k]) for k in range(k_n)]
        for cp in loads:
            cp.start()

        def place(ks):
            copies = []
            for k in ks:
                loads[k].wait()
                stage[k][...] = wide[k][...].astype(BF)
                copies.append(pltpu.make_async_copy(stage[k], land[k].at[_dev_index(x, y, c)], own_sems.at[k_n + k]))
                copies[-1].start()
            for cp in copies:
                cp.wait()

        place(range(k_first))
        targets = [(x, y, 1 - c)] + [(*chip, c) for chip in chips]
        for k in range(k_first):
            for j, to in enumerate(targets):
                _ag_piece(land[k], (x, y, c), None, to, sems[0].at[4 * k + j], sems[1].at[4 * k + j]).start()
        place(range(k_first, k_n))
        _ag_stage(land[k_first:], "relay_halves", sems[2], sems[3], "start")
        _ag_stage(land[k_first:], "others", sems[4], sems[5], "start")
        token[...] = jnp.zeros_like(token)

    outs = pl.pallas_call(
        body,
        name="ag_start",
        in_specs=[ANY_SPEC] * k_n,
        out_specs=(*[SEM_SPEC] * 6, *[HBM_SPEC] * k_n, pl.BlockSpec(memory_space=pltpu.VMEM)),
        out_shape=(*_sem_shapes(4 * k_first, 2 * k_rest, 3 * k_rest),
                   *[pltpu.HBM((N_DEV,) + a.shape, BF) for a in shards], jax.ShapeDtypeStruct((8, LANES), F32)),
        scratch_shapes=[pltpu.VMEM(a.shape, dt) for dt in (BF, F32) for a in shards] + [pltpu.SemaphoreType.DMA((2 * k_n,))],
        compiler_params=pltpu.CompilerParams(has_side_effects=DATAFLOW, vmem_limit_bytes=V7X_VMEM_LIMIT),
    )(*shards)
    flying = list(outs[6:6 + k_n])
    return (outs[0], outs[1], flying[:k_first]), (outs[2:6], flying[k_first:]), outs[-1]


def _ag_split_call(lands, waits, starts, after, name):
    k_n = len(lands)
    plan_sizes = dict(relay_halves=2, others=3, relays=2, near=2, far=1)
    n_in, n_out = 2 * len(waits), 2 * len(starts)

    def body(*refs):
        land = refs[:k_n]
        in_sems = refs[k_n:k_n + n_in]
        out_sems, token = refs[len(refs) - 1 - n_out:len(refs) - 1], refs[-1]
        for w, (stage, _, _) in enumerate(waits):
            _ag_stage(land, stage, in_sems[2 * w], in_sems[2 * w + 1], "wait")
            if w < len(starts):
                _ag_stage(land, starts[w], out_sems[2 * w], out_sems[2 * w + 1], "start")
        token[...] = jnp.zeros_like(token)

    outs = pl.pallas_call(
        body,
        name=name,
        in_specs=[HBM_SPEC] * k_n + [SEM_SPEC] * n_in + [ANY_SPEC],
        out_specs=(*[HBM_SPEC] * k_n, *[SEM_SPEC] * n_out, pl.BlockSpec(memory_space=pltpu.VMEM)),
        out_shape=(*[pltpu.HBM(a.shape, a.dtype) for a in lands], *_sem_shapes(*[plan_sizes[s] * k_n for s in starts]),
                   jax.ShapeDtypeStruct((8, LANES), F32)),
        input_output_aliases={k: k for k in range(k_n)},
        compiler_params=pltpu.CompilerParams(has_side_effects=DATAFLOW),
    )(*lands, *[s for _, a, b in waits for s in (a, b)], after)
    return list(outs[:k_n]), list(outs[k_n:k_n + n_out]), outs[-1]


def _ag_mid(lands, send_sems, recv_sems, after, name):
    k_n = len(lands)

    def body(*refs):
        land = refs[:k_n]
        send1, recv1 = refs[k_n], refs[k_n + 1]
        fwd_send, fwd_recv = refs[-2], refs[-1]
        x, y, c, chips = _place()
        sources = [(x, y, 1 - c)] + [(*chip, c) for chip in chips]
        for k in range(k_n):
            mine = land[k].at[_dev_index(x, y, c)]
            for j, frm in enumerate(sources):
                got = land[k].at[_dev_index(*frm)]
                cp = pltpu.make_async_remote_copy(
                    src_ref=mine, dst_ref=got, send_sem=send1.at[4 * k + j], recv_sem=recv1.at[4 * k + j], device_id=frm, device_id_type=MESH)
                cp.wait_send()
                cp.wait_recv()
                if j >= 1:
                    pltpu.make_async_remote_copy(
                        src_ref=got, dst_ref=got, send_sem=fwd_send.at[3 * k + j - 1], recv_sem=fwd_recv.at[3 * k + j - 1],
                        device_id=(x, y, 1 - c), device_id_type=MESH).start()

    outs = pl.pallas_call(
        body,
        name=name,
        in_specs=[HBM_SPEC] * k_n + [SEM_SPEC, SEM_SPEC, ANY_SPEC],
        out_specs=(*[HBM_SPEC] * k_n, SEM_SPEC, SEM_SPEC),
        out_shape=(*[pltpu.HBM(a.shape, a.dtype) for a in lands], pltpu.SemaphoreType.DMA((3 * k_n,)), pltpu.SemaphoreType.DMA((3 * k_n,))),
        input_output_aliases={k: k for k in range(k_n)},
        compiler_params=pltpu.CompilerParams(has_side_effects=DATAFLOW),
    )(*lands, send_sems, recv_sems, after)
    return list(outs[:k_n]), outs[-2], outs[-1]


def _ag_end(lands, fwd_send, fwd_recv, name):
    k_n = len(lands)

    def body(*refs):
        land = refs[:k_n]
        fsend, frecv = refs[k_n], refs[k_n + 1]
        x, y, c, chips = _place()
        for k in range(k_n):
            for j, chip in enumerate(chips):
                cp = pltpu.make_async_remote_copy(
                    src_ref=land[k].at[_dev_index(*chip, c)], dst_ref=land[k].at[_dev_index(*chip, 1 - c)],
                    send_sem=fsend.at[3 * k + j], recv_sem=frecv.at[3 * k + j], device_id=(x, y, 1 - c), device_id_type=MESH)
                cp.wait_send()
                cp.wait_recv()

    outs = pl.pallas_call(
        body,
        name=name,
        in_specs=[HBM_SPEC] * k_n + [SEM_SPEC, SEM_SPEC],
        out_specs=tuple([HBM_SPEC] * k_n),
        out_shape=tuple(pltpu.HBM(a.shape, a.dtype) for a in lands),
        input_output_aliases={k: k for k in range(k_n)},
        compiler_params=pltpu.CompilerParams(has_side_effects=DATAFLOW),
    )(*lands, fwd_send, fwd_recv)
    return list(outs)


def _chips4():
    x, y, c, others = _place()
    return x, y, c, [(x, y)] + others


def _route_sibling(j):
    x, y, c, chips = _chips4()
    return _dev_index(*chips[j], 1 - c), j, (x, y, 1 - c)


def _route_chips(j):
    x, y, c, chips = _chips4()
    return j, j, (*chips[j + 1], c)


def _route_sibling_whole(j):
    x, y, c, _ = _chips4()
    return 0, 0, (x, y, 1 - c)


def _route_chips_whole(j):
    x, y, c, chips = _chips4()
    return 0, j, (*chips[j + 1], c)


def _route_everyone(j):
    x, y, c, _ = _chips4()
    flip = [(j + 1) >> 2 & 1, (j + 1) >> 1 & 1, (j + 1) & 1]
    return 0, j, tuple(1 - v if f else v for v, f in zip((x, y, c), flip))


def _xchg_copies(routes, src, dst, send_sems, recv_sems):
    copies, sem = [], 0
    for k, (route, n) in enumerate(routes):
        for j in range(n):
            si, di, peer = route(j)
            copies.append(pltpu.make_async_remote_copy(
                src_ref=src[k].at[si], dst_ref=dst[k].at[di], send_sem=send_sems.at[sem], recv_sem=recv_sems.at[sem],
                device_id=peer, device_id_type=MESH))
            sem += 1
    return copies


def _xchg_start(srcs, slot_shapes, routes, name, after=None):
    k_n = len(srcs)
    n_in = 2 * k_n + (after is not None)
    n_sem = sum(n for _, n in routes)
    dsts = [lax.empty((n,) + tuple(sh), a.dtype) for sh, a, (_, n) in zip(slot_shapes, srcs, routes)]

    def body(*refs):
        src, dst = refs[:k_n], refs[k_n:2 * k_n]
        send_sems, recv_sems, token = refs[n_in], refs[n_in + 1], refs[-1]
        for cp in _xchg_copies(routes, src, dst, send_sems, recv_sems):
            cp.start()
        token[...] = jnp.zeros_like(token)

    arrays = list(srcs) + dsts
    outs = pl.pallas_call(
        body,
        name=name,
        in_specs=[HBM_SPEC] * (2 * k_n) + [ANY_SPEC] * (n_in - 2 * k_n),
        out_specs=(SEM_SPEC, SEM_SPEC, *[HBM_SPEC] * (2 * k_n), pl.BlockSpec(memory_space=pltpu.VMEM)),
        out_shape=(pltpu.SemaphoreType.DMA((n_sem,)), pltpu.SemaphoreType.DMA((n_sem,)),
                   *[pltpu.HBM(a.shape, a.dtype) for a in arrays], jax.ShapeDtypeStruct((8, LANES), F32)),
        input_output_aliases={i: 2 + i for i in range(2 * k_n)},
        compiler_params=pltpu.CompilerParams(has_side_effects=DATAFLOW),
    )(*[_hbm(a) for a in arrays], *([] if after is None else [after]))
    return outs[0], outs[1], list(outs[2:2 + k_n]), list(outs[2 + k_n:2 + 2 * k_n]), outs[-1]


def _xchg_wait(send_sems, recv_sems, srcs, dsts, routes, after, name):
    k_n = len(srcs)

    def body(*refs):
        src, dst = refs[:k_n], refs[k_n:2 * k_n]
        for cp in _xchg_copies(routes, src, dst, refs[2 * k_n], refs[2 * k_n + 1]):
            cp.wait_send()
            cp.wait_recv()

    arrays = list(srcs) + list(dsts)
    outs = pl.pallas_call(
        body,
        name=name,
        in_specs=[HBM_SPEC] * (2 * k_n) + [SEM_SPEC, SEM_SPEC, ANY_SPEC],
        out_specs=tuple([HBM_SPEC] * (2 * k_n)),
        out_shape=tuple(pltpu.HBM(a.shape, a.dtype) for a in arrays),
        input_output_aliases={i: i for i in range(2 * k_n)},
        compiler_params=pltpu.CompilerParams(has_side_effects=DATAFLOW),
    )(*arrays, send_sems, recv_sems, after)
    return list(outs[:k_n]), list(outs[k_n:])


SMALL = ("norm1_gain", "gmlp_v_gain", "w_spatial", "b_spatial", "attn_sinks", "rel_bias_table", "norm2_gain", "final_gain")
LANES = 128


def _swap_start(grads, smalls, tag, after=None):
    srcs = list(grads) + [a[None] for a in smalls]
    shapes = [g.shape[1:] for g in grads] + [a.shape for a in smalls]
    routes = [(_route_sibling, 4)] * len(grads) + [(_route_sibling_whole, 1)] * len(smalls)
    return _xchg_start(srcs, shapes, routes, f"rs_{tag}_swap_start", after), routes, len(grads)


def _swap_sums(swap, after, tag, small_dtypes=None):
    (send1, recv1, src1, land1, _), routes, n_rs = swap
    x, y, c, chips = _chips4()
    blocks = jnp.stack([_dev_index(*chip, c) for chip in chips]).astype(jnp.int32)
    src1, land1 = _xchg_wait(send1, recv1, src1, land1, routes, after, f"rs_{tag}_swap_wait")
    return _rs_sum(src1[:n_rs], land1[:n_rs], blocks, f"rs_{tag}_sum", src1[n_rs:], land1[n_rs:],
                   small_dtypes or [F32] * (len(src1) - n_rs))


def _chips_start(sums, small_sums, tag, everyone=()):
    sends = [ps[1] for ps in sums] + list(small_sums) + [a[None] for a in everyone]
    routes = [(_route_chips, 3)] * len(sums) + [(_route_chips_whole, 3)] * len(small_sums) + [(_route_everyone, 7)] * len(everyone)
    return _xchg_start(sends, [a.shape[1:] for a in sends], routes, f"rs_{tag}_chips_start"), routes


def _chips_wait(chips, after, tag):
    (send2, recv2, src2, land2, _), routes = chips
    return _xchg_wait(send2, recv2, src2, land2, routes, after, f"rs_{tag}_chips_wait")


def kernel(x, p, norm1_gain, w_in, gmlp_v_gain, w_spatial, b_spatial, attn_sinks, rel_bias_table, w_out, norm2_gain, w_ff1, w_ff2, w_ple_proj, w_ple_gate, final_gain, loss_target, m_norm1_gain, m_w_in, m_gmlp_v_gain, m_w_spatial, m_b_spatial, m_attn_sinks, m_rel_bias_table, m_w_out, m_norm2_gain, m_w_ff1, m_w_ff2, m_w_ple_proj, m_w_ple_gate, m_final_gain, v_norm1_gain, v_w_in, v_gmlp_v_gain, v_w_spatial, v_b_spatial, v_attn_sinks, v_rel_bias_table, v_w_out, v_norm2_gain, v_w_ff1, v_w_ff2, v_w_ple_proj, v_w_ple_gate, v_final_gain):
    args = dict(locals())
    s = x.shape[1]
    big = ("w_in", "w_out", "w_ff1", "w_ff2", "w_ple_proj", "w_ple_gate")

    x2, p2, t2 = x.reshape(s, D_MODEL), p.reshape(s, PLE_DIM), loss_target.reshape(s, D_MODEL)
    g1, gv, w_sp, b_sp, sinks, table, g2, gf = (args[n] for n in SMALL)
    bucket = jnp.asarray(_bucket_table())
    b_col = b_sp.reshape(GROUPS, BLOCK, 1)

    def shard(name):
        return args[name][0].T if name.endswith("w_in") else args[name][0]

    (send_in, recv_in, fly_in), (rest_sems, fly_rest), token = _ag_start([shard(big[0])], [shard(n) for n in big[1:]])
    mid_in, fwd_send_in, fwd_recv_in = _ag_mid(fly_in, send_in, recv_in, token, "ag_mid_w_in")
    g_in = _ag_end(mid_in, fwd_send_in, fwd_recv_in, "ag_end_w_in")[0]
    full_in = g_in.reshape(D_IN, D_MODEL)

    zuv, qkv, hn1 = _in_proj(x2, g1, full_in)
    fly_rest, relay_sems, relayed = _ag_split_call(
        fly_rest, [("relay_halves", *rest_sems[:2])], ["relays"], zuv, "ag_relay")
    mix, *saved = _mixer_fwd(zuv, qkv, gv, w_sp[0], b_col, sinks, table.T, bucket, after=relayed)
    fly_rest, fwd_sems, _ = _ag_split_call(
        fly_rest, [("others", *rest_sems[2:]), ("relays", *relay_sems)], ["near", "far"], mix, "ag_mid_rest")
    g_out, g_ff1, g_ff2, g_proj, g_gate = _ag_split_call(
        fly_rest, [("near", *fwd_sems[:2]), ("far", *fwd_sems[2:])], [], mix, "ag_end_rest")[0]
    full_out, full_ff2, full_gate = g_out.reshape(D_MODEL, D_MODEL), g_ff2.reshape(D_FF, D_MODEL), g_gate.reshape(D_MODEL, D_MODEL)

    tail_small, dh1, dmix, hn2, a, df, dh2, gw_out, gw_proj, gw_gate = _tail(
        x2, mix, p2, t2, g2, gf.reshape(1, D_MODEL), full_out, g_ff1, full_ff2, full_gate, g_proj)
    gw_out, gw_gate = (g.reshape(N_DEV, D_MODEL // N_DEV, D_MODEL) for g in (gw_out, gw_gate))

    few_names = ("w_out", "w_ple_proj", "w_ple_gate")
    ff1_names = ("w_ff1",) + few_names
    gw_ff1 = _wgrad(hn2, df, D_MODEL, 2 * D_FF // N_DEV, "wgrad_ff1", peer_cols=D_FF // N_DEV)
    ff1_swap = _swap_start([gw_ff1, gw_out, gw_proj, gw_gate], [tail_small], "ff1")
    gw_ff2 = _wgrad(a, dh2, 1024, D_MODEL, "wgrad_ff2", after=ff1_swap[0][4]).reshape(N_DEV, D_FF // N_DEV, D_MODEL)
    ff1_sums, ff1_small = _swap_sums(ff1_swap, gw_ff2, "ff1")
    ff1_chips = _chips_start(ff1_sums, ff1_small, "ff1")
    ff2_swap = _swap_start([gw_ff2], [], "ff2", after=ff1_chips[0][4])
    dzm, dkv, d_gv, d_wsp, d_bsp, d_attn = _mixer_bwd(
        zuv, qkv, dmix, saved, gv, w_sp[0], b_col, bucket, after=ff2_swap[0][4])
    ff2_sums, _ = _swap_sums(ff2_swap, dzm, "ff2")
    ff2_chips = _chips_start(ff2_sums, [], "ff2")
    gw_in = _wgrad_in(dzm, dkv, hn1, after=ff2_chips[0][4]).reshape(N_DEV, D_IN // N_DEV, D_MODEL)
    in_swap = _swap_start([gw_in], [d_gv, d_wsp, d_bsp, d_attn], "in")
    dx, d_g1 = _in_bwd(x2, dh1, dzm, dkv, g1, full_in, after=in_swap[0][4])
    grad_x = dx.reshape(x.shape)
    in_sums, in_small = _swap_sums(in_swap, dx, "in", small_dtypes=[F32, BF, F32, F32])
    in_chips = _chips_start(in_sums, in_small, "in", everyone=[d_g1])

    grads, deltas, new_m, new_v = {}, {}, {}, {}
    after, small_own, small_recv = in_chips[0][4], [], []
    for chips, sums, names, tag in ((ff1_chips, ff1_sums, ff1_names, "ff1"), (ff2_chips, ff2_sums, ("w_ff2",), "ff2"),
                                    (in_chips, in_sums, ("w_in",), "in")):
        sent, recvs = _chips_wait(chips, after, tag)
        small_own += sent[len(names):]
        small_recv += recvs[len(names):]
        tiled = [n for n in names if n not in few_names]
        done = [_final_adamw(part, recv, shard(n), shard("m_" + n), shard("v_" + n), "adamw_" + n)
                for n, (part, _), recv in zip(tiled, sums, recvs)]
        few = names[len(tiled):]
        if few:
            done += zip(*_final_adamw_few(
                [ps[0] for ps in sums[len(tiled):]], recvs[len(tiled):len(names)], [shard(n) for n in few],
                [shard("m_" + n) for n in few], [shard("v_" + n) for n in few], "adamw_few"))
        for n, results in zip(names, done):
            for dst, arr in zip((grads, deltas, new_m, new_v), results):
                dst[n] = (arr.T if n == "w_in" else arr)[None]
            after = results[1]

    views = {"w_spatial": (GROUPS, BLOCK, BLOCK), "b_spatial": (GROUPS, BLOCK), "final_gain": (1, D_MODEL)}
    def view(name, base):
        return args[name].T if base == "rel_bias_table" else args[name].reshape(views.get(base, args[name].shape))

    small_in = [[view(pre + n, n) for n in SMALL] for pre in ("", "m_", "v_")]
    loss, *small_out = _adamw_small(small_own[:-1], small_recv[:-1], small_own[-1], small_recv[-1], *small_in, after=after)
    for dst, arrays in zip((grads, deltas, new_m, new_v), small_out):
        for n, arr in zip(SMALL, arrays):
            dst[n] = arr.T if n == "rel_bias_table" else arr.reshape(args[n].shape)
    loss = loss[0, 0]

    order = ("norm1_gain", "w_in", "gmlp_v_gain", "w_spatial", "b_spatial", "attn_sinks", "rel_bias_table", "w_out",
             "norm2_gain", "w_ff1", "w_ff2", "w_ple_proj", "w_ple_gate", "final_gain")
    return (loss, grad_x, *[grads[n] for n in order], *[deltas[n] for n in order],
            *[new_m[n] for n in order], *[new_v[n] for n in order])
```

```python
import math

import numpy as np
import jax
import jax.numpy as jnp
from jax import lax
from jax.experimental import pallas as pl
from jax.experimental.pallas import tpu as pltpu

F32 = jnp.float32
BF = jnp.bfloat16
MESH = pl.DeviceIdType.MESH
N_DEV = 8

D_MODEL = 1024
PLE_DIM = 256
D_GMLP = 512
GROUPS = 4
GDIM = 128
BLOCK = 128
D_ATTN = 512
HEAD_DIM = 64
N_Q = 8
Q_PER_KV = 4
N_KV = N_Q // Q_PER_KV
ROWS4 = Q_PER_KV * BLOCK
D_KV = 128
D_FF = 4096
D_IN = 1792
D_MAIN = 2 * D_GMLP + D_ATTN
REL_BUCKETS = 32
EPS = 1e-6
NEG_INF = -1e30
SCALE = HEAD_DIM ** -0.5
GELU_C = math.sqrt(2.0 / math.pi)
GELU_A = 0.044715

ADAM_LR = 0.001
ADAM_B1 = 0.9
ADAM_B2 = 0.999
ADAM_EPS = 1e-08
ADAM_WD = 0.01
ADAM_STEP = 10

V7X_VMEM_LIMIT = 61440 * 1024
CALL_VMEM_LIMIT = 32768 * 1024
TOK_TILE = 256
IO_TOK_TILE = 512


def _call(body, after=None, **kw):
    if after is None:
        return pl.pallas_call(body, **kw)
    n_in = len(kw["in_specs"])

    def ordered(*refs):
        body(*refs[:n_in], *refs[n_in + 1:])

    kw["in_specs"] = list(kw["in_specs"]) + [pl.BlockSpec(memory_space=pl.ANY)]
    fn = pl.pallas_call(ordered, **kw)
    return lambda *operands: fn(*operands, after)


def _params(sem=None, limit=CALL_VMEM_LIMIT):
    if sem is None:
        return pltpu.CompilerParams(vmem_limit_bytes=limit)
    return pltpu.CompilerParams(dimension_semantics=sem, vmem_limit_bytes=limit)


def _nn(a, b):
    return jnp.dot(a, b, preferred_element_type=F32)


def _nt(a, b):
    return lax.dot_general(a, b, (((1,), (1,)), ((), ())), preferred_element_type=F32)


def _tn(a, b):
    return lax.dot_general(a, b, (((0,), (0,)), ((), ())), preferred_element_type=F32)


def _gelu_tanh(x):
    return jnp.tanh(GELU_C * (x + GELU_A * (x * x * x)))


def _gelu(x, t):
    return x * (0.5 * (1.0 + t))


def _gelu_and_grad(x, t):
    cdf = 0.5 * (1.0 + t)
    return x * cdf, cdf + 0.5 * x * (1.0 - t * t) * (GELU_C * (1.0 + 3.0 * GELU_A * (x * x)))


def _rms_scale(x):
    return lax.rsqrt(jnp.mean(x * x, axis=-1, keepdims=True) + EPS)


def _rms_bwd(dxn, x, r):
    return r * dxn - x * ((r * r * r) * jnp.mean(dxn * x, axis=-1, keepdims=True))


def _bucket_table():
    a = np.arange(BLOCK)[:, None]
    j = np.arange(2 * BLOCK)[None, :]
    n = BLOCK + a - j
    valid = (n >= 0) & (n < BLOCK)
    nc = np.maximum(n, 0)
    max_exact = REL_BUCKETS // 2
    nf = np.maximum(nc, 1).astype(np.float32)
    large = max_exact + (
        np.log(nf / np.float32(max_exact)) / np.float32(math.log(BLOCK / max_exact)) * np.float32(REL_BUCKETS - max_exact)
    ).astype(np.int32)
    large = np.minimum(large, REL_BUCKETS - 1)
    bucket = np.where(nc < max_exact, nc, large)
    return np.where(valid, bucket, -1).astype(np.int32)


def _in_proj(x, g1, w_in_t):
    s = x.shape[0]
    tm = min(IO_TOK_TILE, s)

    def body(x_ref, g_ref, w_ref, zuv_ref, qkv_ref, hn_ref):
        xv = x_ref[...]
        hn = ((xv * _rms_scale(xv)) * g_ref[...]).astype(BF)
        hn_ref[...] = hn
        z = _nt(hn, w_ref[...])
        zuv_ref[...] = z[:, : 2 * D_GMLP]
        qkv_ref[...] = z[:, 2 * D_GMLP:].astype(BF)

    return _call(
        body,
        name="in_proj",
        grid=(s // tm,),
        in_specs=[
            pl.BlockSpec((tm, D_MODEL), lambda i: (i, 0)),
            pl.BlockSpec((1, D_MODEL), lambda i: (0, 0)),
            pl.BlockSpec((D_IN, D_MODEL), lambda i: (0, 0)),
        ],
        out_specs=[
            pl.BlockSpec((tm, 2 * D_GMLP), lambda i: (i, 0)),
            pl.BlockSpec((tm, D_ATTN + 2 * D_KV), lambda i: (i, 0)),
            pl.BlockSpec((tm, D_MODEL), lambda i: (i, 0)),
        ],
        out_shape=[
            jax.ShapeDtypeStruct((s, 2 * D_GMLP), F32),
            jax.ShapeDtypeStruct((s, D_ATTN + 2 * D_KV), BF),
            jax.ShapeDtypeStruct((s, D_MODEL), BF),
        ],
        compiler_params=_params(("arbitrary",)),
    )(x, g1, w_in_t)


def _head_rows(h):
    kh, g = divmod(h, Q_PER_KV)
    return kh, slice(g * BLOCK, (g + 1) * BLOCK)


def _build_bias(bias_ref, bucket_ref, table_ref):
    bucket = bucket_ref[...]
    for h in range(N_Q):
        acc = jnp.zeros((BLOCK, 2 * BLOCK), F32)
        for b in range(REL_BUCKETS):
            acc = jnp.where(bucket == b, table_ref[h, b], acc)
        kh, rows = _head_rows(h)
        bias_ref[kh, rows, :] = acc


def _window_masks(i):
    row = lax.broadcasted_iota(jnp.int32, (ROWS4, BLOCK), 0) & (BLOCK - 1)
    col = lax.broadcasted_iota(jnp.int32, (ROWS4, BLOCK), 1)
    return (col > row) & (i > 0), col <= row


def _stack_heads(ref, kh, offset):
    first = offset + kh * Q_PER_KV * HEAD_DIM
    return jnp.concatenate(
        [ref[:, first + g * HEAD_DIM: first + (g + 1) * HEAD_DIM].astype(BF) for g in range(Q_PER_KV)], axis=0)


def _stack_sinks(sink_ref, kh):
    return jnp.concatenate([jnp.full((BLOCK, 1), sink_ref[0, kh * Q_PER_KV + g], F32) for g in range(Q_PER_KV)], axis=0)


def _tril_bf16(w_ref, g):
    row = lax.broadcasted_iota(jnp.int32, (BLOCK, BLOCK), 0)
    col = lax.broadcasted_iota(jnp.int32, (BLOCK, BLOCK), 1)
    return jnp.where(col <= row, w_ref[g], 0.0).astype(BF)


def _attn_probs(q_h, k_prev, k_cur, bias_h, sink, valid_prev, valid_cur):
    l_prev = jnp.where(valid_prev, _nt(q_h, k_prev) * SCALE + bias_h[:, :BLOCK], NEG_INF)
    l_cur = jnp.where(valid_cur, _nt(q_h, k_cur) * SCALE + bias_h[:, BLOCK:], NEG_INF)
    m = jnp.maximum(jnp.maximum(jnp.max(l_prev, axis=-1, keepdims=True), jnp.max(l_cur, axis=-1, keepdims=True)), sink)
    e_prev = jnp.exp(l_prev - m)
    e_cur = jnp.exp(l_cur - m)
    e_sink = jnp.exp(sink - m)
    denom = jnp.sum(e_prev, axis=-1, keepdims=True) + jnp.sum(e_cur, axis=-1, keepdims=True) + e_sink
    return e_prev / denom, e_cur / denom, e_sink / denom


def _mixer_specs(nb):
    cl = lambda i: jnp.minimum(i, nb - 1)
    return [
        pl.BlockSpec((BLOCK, 2 * D_GMLP), lambda i: (cl(i), 0)),
        pl.BlockSpec((BLOCK, D_ATTN), lambda i: (cl(i), 0)),
        pl.BlockSpec((BLOCK, 2 * D_KV), lambda i: (cl(i), D_ATTN // (2 * D_KV))),
        pl.BlockSpec((BLOCK, 2 * D_KV), lambda i: (jnp.maximum(cl(i) - 1, 0), D_ATTN // (2 * D_KV))),
        pl.BlockSpec((1, D_GMLP), lambda i: (0, 0)),
        pl.BlockSpec((GROUPS, BLOCK, BLOCK), lambda i: (0, 0, 0)),
        pl.BlockSpec((GROUPS, BLOCK, 1), lambda i: (0, 0, 0)),
        pl.BlockSpec(memory_space=pltpu.SMEM),
        pl.BlockSpec(memory_space=pltpu.SMEM),
        pl.BlockSpec((BLOCK, 2 * BLOCK), lambda i: (0, 0)),
    ]


def _mixer_fwd(zuv, qkv, gv, w_sp, b_sp, sinks, table, bucket, after=None):
    s = zuv.shape[0]
    nb = s // BLOCK

    def body(zuv_ref, q_ref, kvc_ref, kvp_ref, gv_ref, w_ref, b_ref, sink_ref, table_ref, bucket_ref,
             mix_ref, tanh_ref, prob_ref, psink_ref, bias_ref):
        i = pl.program_id(0)

        @pl.when(i == 0)
        def _():
            _build_bias(bias_ref, bucket_ref, table_ref)

        t = _gelu_tanh(zuv_ref[...])
        tanh_ref[...] = t
        u = _gelu(zuv_ref[:, :D_GMLP], t[:, :D_GMLP])
        vg = _gelu(zuv_ref[:, D_GMLP:], t[:, D_GMLP:])
        for g in range(GROUPS):
            sl = slice(g * GDIM, (g + 1) * GDIM)
            vg_g = vg[:, sl]
            vn = ((vg_g * _rms_scale(vg_g)) * gv_ref[:, sl]).astype(BF)
            sv = _nn(_tril_bf16(w_ref, g), vn) + b_ref[g]
            mix_ref[:, sl] = (u[:, sl] * sv).astype(BF)

        valid_prev, valid_cur = _window_masks(i)
        for kh in range(N_KV):
            ksl = slice(kh * HEAD_DIM, (kh + 1) * HEAD_DIM)
            vsl = slice(D_KV + kh * HEAD_DIM, D_KV + (kh + 1) * HEAD_DIM)
            q4 = _stack_heads(q_ref, kh, 0)
            p_prev, p_cur, p_sink = _attn_probs(
                q4, kvp_ref[:, ksl], kvc_ref[:, ksl], bias_ref[kh], _stack_sinks(sink_ref, kh), valid_prev, valid_cur)
            prob_ref[0, kh, :, :BLOCK] = p_prev
            prob_ref[0, kh, :, BLOCK:] = p_cur
            psink_ref[0, kh] = jnp.broadcast_to(p_sink, (ROWS4, LANES))
            o4 = _nn(p_prev.astype(BF), kvp_ref[:, vsl]) + _nn(p_cur.astype(BF), kvc_ref[:, vsl])
            for g in range(Q_PER_KV):
                first = D_GMLP + (kh * Q_PER_KV + g) * HEAD_DIM
                mix_ref[:, first:first + HEAD_DIM] = o4[g * BLOCK:(g + 1) * BLOCK].astype(BF)

    return _call(
        body,
        name="mixer_fwd",
        after=after,
        grid=(nb,),
        in_specs=_mixer_specs(nb),
        out_specs=[
            pl.BlockSpec((BLOCK, D_MODEL), lambda i: (i, 0)),
            pl.BlockSpec((BLOCK, 2 * D_GMLP), lambda i: (i, 0)),
            pl.BlockSpec((1, N_KV, ROWS4, 2 * BLOCK), lambda i: (i, 0, 0, 0)),
            pl.BlockSpec((1, N_KV, ROWS4, LANES), lambda i: (i, 0, 0, 0)),
        ],
        out_shape=[
            jax.ShapeDtypeStruct((s, D_MODEL), BF),
            jax.ShapeDtypeStruct((s, 2 * D_GMLP), F32),
            jax.ShapeDtypeStruct((nb, N_KV, ROWS4, 2 * BLOCK), F32),
            jax.ShapeDtypeStruct((nb, N_KV, ROWS4, LANES), F32),
        ],
        scratch_shapes=[pltpu.VMEM((N_KV, ROWS4, 2 * BLOCK), F32)],
        compiler_params=_params(("arbitrary",)),
    )(zuv, qkv, qkv, qkv, gv, w_sp, b_sp, sinks, table, bucket)


def _tail(x, mix, p, t, g2, gf, w_out, w_ff1, w_ff2, w_gate, w_proj):
    s = x.shape[0]
    tm = min(TOK_TILE, s)
    n_ff = w_ff1.shape[0]
    fc = D_FF // n_ff
    pc = D_MODEL // N_DEV

    def body(x_ref, mix_ref, p_ref, t_ref, g2_ref, gf_ref, wo_ref, w1_ref, w2_ref, wg_ref, wp_ref,
             small_ref, dh1_ref, dmix_ref, hn2_ref, a_ref, df_ref, dh2_ref, gwo_ref, gwp_ref, gwg_ref, f_ref):
        i = pl.program_id(0)

        @pl.when(i == 0)
        def _():
            small_ref[...] = jnp.zeros_like(small_ref)
            gwo_ref[...] = jnp.zeros_like(gwo_ref)
            gwp_ref[...] = jnp.zeros_like(gwp_ref)
            gwg_ref[...] = jnp.zeros_like(gwg_ref)

        h1 = x_ref[...] + _nn(mix_ref[...], wo_ref[...])
        dh1_ref[...] = h1
        r2 = _rms_scale(h1)
        hn2_ref[...] = ((h1 * r2) * g2_ref[...]).astype(BF)
        h2 = h1
        for c in range(n_ff):
            f = _nn(hn2_ref[...], w1_ref[c])
            f_ref[:, c * fc:(c + 1) * fc] = f
            a = jnp.square(jnp.maximum(f, 0.0)).astype(BF)
            a_ref[:, c * fc:(c + 1) * fc] = a
            h2 = h2 + _nn(a, w2_ref[c * fc:(c + 1) * fc, :])
        h2b = h2.astype(BF)
        gate = jax.nn.sigmoid(_nn(h2b, wg_ref[...]))
        pb = p_ref[...].astype(BF)
        pp = jnp.concatenate([_nn(pb, wp_ref[q]) for q in range(N_DEV)], axis=1)
        h3 = h2 + gate * pp
        rf = _rms_scale(h3)
        gf_v = gf_ref[...]
        err = (h3 * rf) * gf_v - t_ref[...]
        small_ref[2:3, :] += jnp.sum(jnp.sum(err * err, axis=-1, keepdims=True), axis=0, keepdims=True) * (0.5 / D_MODEL)

        dy = err * (1.0 / D_MODEL)
        small_ref[1:2, :] += jnp.sum(dy * (h3 * rf), axis=0, keepdims=True)
        dh3 = _rms_bwd(dy * gf_v, h3, rf)
        gw_proj = _tn(pb, (dh3 * gate).astype(BF))
        for q in range(N_DEV):
            gwp_ref[q] += gw_proj[:, q * pc:(q + 1) * pc]
        dgl = ((dh3 * pp) * (gate * (1.0 - gate))).astype(BF)
        gwg_ref[...] += _tn(h2b, dgl)
        dh2 = dh3 + _nt(dgl, wg_ref[...])
        dmix_ref[...] = dh2
        dh2_ref[...] = dh2.astype(BF)
        dhn2 = jnp.zeros((tm, D_MODEL), F32)
        for c in range(n_ff):
            da = _nt(dh2_ref[...], w2_ref[c * fc:(c + 1) * fc, :])
            df = (da * (2.0 * jnp.maximum(f_ref[:, c * fc:(c + 1) * fc], 0.0))).astype(BF)
            df_ref[:, c * fc:(c + 1) * fc] = df
            dhn2 = dhn2 + _nt(df, w1_ref[c])
        h1 = dh1_ref[...]
        small_ref[0:1, :] += jnp.sum(dhn2 * (h1 * r2), axis=0, keepdims=True)
        dh1 = dmix_ref[...] + _rms_bwd(dhn2 * g2_ref[...], h1, r2)
        dh1_ref[...] = dh1
        dh1b = dh1.astype(BF)
        gwo_ref[...] += _tn(mix_ref[...], dh1b)
        dmix_ref[...] = _nt(dh1b, wo_ref[...])

    tile = lambda cols: pl.BlockSpec((tm, cols), lambda i: (i, 0))
    whole = lambda shape: pl.BlockSpec(shape, lambda i: (0,) * len(shape), pipeline_mode=pl.Buffered(1))
    total = lambda shape: pl.BlockSpec(shape, lambda i: (0,) * len(shape))
    row = pl.BlockSpec((1, D_MODEL), lambda i: (0, 0))
    act = lambda cols, dt: jax.ShapeDtypeStruct((s, cols), dt)
    gw_shapes = [(D_MODEL, D_MODEL), (N_DEV, PLE_DIM, pc), (D_MODEL, D_MODEL)]
    return _call(
        body,
        name="tail",
        grid=(s // tm,),
        in_specs=[tile(D_MODEL), tile(D_MODEL), tile(PLE_DIM), tile(D_MODEL), row, row,
                  whole(w_out.shape), whole(w_ff1.shape), whole(w_ff2.shape), whole(w_gate.shape), whole(w_proj.shape)],
        out_specs=[total((8, D_MODEL)), tile(D_MODEL), tile(D_MODEL), tile(D_MODEL), tile(D_FF), tile(D_FF), tile(D_MODEL),
                   *[total(shape) for shape in gw_shapes]],
        out_shape=[jax.ShapeDtypeStruct((8, D_MODEL), F32),
                   act(D_MODEL, F32), act(D_MODEL, F32), act(D_MODEL, BF), act(D_FF, BF), act(D_FF, BF), act(D_MODEL, BF),
                   *[jax.ShapeDtypeStruct(shape, F32) for shape in gw_shapes]],
        scratch_shapes=[pltpu.VMEM((tm, D_FF), F32)],
        compiler_params=_params(("arbitrary",), V7X_VMEM_LIMIT),
    )(x, mix, p, t, g2, gf, w_out, w_ff1, w_ff2, w_gate, w_proj)


def _mixer_bwd(zuv, qkv, dmix, saved, gv, w_sp, b_sp, bucket, after=None):
    s = zuv.shape[0]
    nb = s // BLOCK

    def body(zuv_ref, q_ref, kvc_ref, kvp_ref, gv_ref, w_ref, b_ref, bucket_ref, dmix_ref, tanh_ref, prob_ref, psink_ref,
             dzm_ref, dkv_ref, dgv_ref, dw_ref, db_ref, dattn_ref,
             dbias_ref, carry_ref, dsink_acc, db_acc):
        i = pl.program_id(0)

        @pl.when(i == 0)
        def _():
            dbias_ref[...] = jnp.zeros_like(dbias_ref)
            carry_ref[...] = jnp.zeros_like(carry_ref)
            dsink_acc[...] = jnp.zeros_like(dsink_acc)
            dgv_ref[...] = jnp.zeros_like(dgv_ref)
            dw_ref[...] = jnp.zeros_like(dw_ref)
            db_acc[...] = jnp.zeros_like(db_acc)

        @pl.when(i < nb)
        def _():
            u, du_dz = _gelu_and_grad(zuv_ref[:, :D_GMLP], tanh_ref[:, :D_GMLP])
            vg, dvg_dz = _gelu_and_grad(zuv_ref[:, D_GMLP:], tanh_ref[:, D_GMLP:])
            for g in range(GROUPS):
                sl = slice(g * GDIM, (g + 1) * GDIM)
                vg_g = vg[:, sl]
                rg = _rms_scale(vg_g)
                vhat = vg_g * rg
                gain = gv_ref[:, sl]
                vn = (vhat * gain).astype(BF)
                w_g = _tril_bf16(w_ref, g)
                sv = _nn(w_g, vn) + b_ref[g]
                dmix_a = dmix_ref[:, sl]
                dsv = dmix_a * u[:, sl]
                dsvb = dsv.astype(BF)
                db_acc[g] += jnp.sum(dsv, axis=-1, keepdims=True)
                dw_ref[g] += _nt(dsvb, vn)
                dvn = _tn(w_g, dsvb)
                dgv_ref[:, sl] += jnp.sum(dvn * vhat, axis=0, keepdims=True)
                dvg = _rms_bwd(dvn * gain, vg_g, rg)
                dzm_ref[:, sl] = ((dmix_a * sv) * du_dz[:, sl]).astype(BF)
                dzm_ref[:, D_GMLP + g * GDIM: D_GMLP + (g + 1) * GDIM] = (dvg * dvg_dz[:, sl]).astype(BF)

            for kh in range(N_KV):
                ksl = slice(kh * HEAD_DIM, (kh + 1) * HEAD_DIM)
                vsl = slice(D_KV + kh * HEAD_DIM, D_KV + (kh + 1) * HEAD_DIM)
                k_prev, k_cur = kvp_ref[:, ksl], kvc_ref[:, ksl]
                v_prev, v_cur = kvp_ref[:, vsl], kvc_ref[:, vsl]
                q4 = _stack_heads(q_ref, kh, 0)
                p_prev, p_cur, p_sink = prob_ref[0, kh, :, :BLOCK], prob_ref[0, kh, :, BLOCK:], psink_ref[0, kh, :, 0:1]
                do4 = _stack_heads(dmix_ref, kh, D_GMLP)
                dp_prev = _nt(do4, v_prev)
                dp_cur = _nt(do4, v_cur)
                delta = jnp.sum(p_prev * dp_prev, axis=-1, keepdims=True) + jnp.sum(p_cur * dp_cur, axis=-1, keepdims=True)
                ds_prev = p_prev * (dp_prev - delta)
                ds_cur = p_cur * (dp_cur - delta)
                dsink_acc[kh] -= p_sink * delta
                dbias_ref[kh, :, :BLOCK] += ds_prev
                dbias_ref[kh, :, BLOCK:] += ds_cur
                dsb_prev = ds_prev.astype(BF)
                dsb_cur = ds_cur.astype(BF)
                dq4 = (_nn(dsb_prev, k_prev) + _nn(dsb_cur, k_cur)) * SCALE
                for g in range(Q_PER_KV):
                    first = 2 * D_GMLP + (kh * Q_PER_KV + g) * HEAD_DIM
                    dzm_ref[:, first:first + HEAD_DIM] = dq4[g * BLOCK:(g + 1) * BLOCK].astype(BF)
                dkv_ref[:, ksl] = (carry_ref[:, ksl] + _tn(dsb_prev, q4) * SCALE).astype(BF)
                dkv_ref[:, vsl] = (carry_ref[:, vsl] + _tn(p_prev.astype(BF), do4)).astype(BF)
                carry_ref[:, ksl] = _tn(dsb_cur, q4) * SCALE
                carry_ref[:, vsl] = _tn(p_cur.astype(BF), do4)

        @pl.when(i == nb)
        def _():
            dkv_ref[...] = carry_ref[...].astype(BF)
            row = lax.broadcasted_iota(jnp.int32, (BLOCK, BLOCK), 0)
            col = lax.broadcasted_iota(jnp.int32, (BLOCK, BLOCK), 1)
            for g in range(GROUPS):
                dw_ref[g] = jnp.where(col <= row, dw_ref[g], 0.0)
                db_ref[g:g + 1, :] = jnp.sum(jnp.where(col == row, db_acc[g], 0.0), axis=0, keepdims=True)
            bucket = bucket_ref[...]
            for b in range(N_Q, REL_BUCKETS):
                dattn_ref[N_Q, b] = 0.0
            for h in range(N_Q):
                kh, rows = _head_rows(h)
                dattn_ref[N_Q, h] = jnp.sum(dsink_acc[kh, rows, :])
                dbh = dbias_ref[kh, rows, :]
                for b in range(REL_BUCKETS):
                    dattn_ref[h, b] = jnp.sum(jnp.where(bucket == b, dbh, 0.0))

    cl = lambda i: jnp.minimum(i, nb - 1)
    const = lambda shape: pl.BlockSpec(shape, lambda i: (0,) * len(shape))
    return _call(
        body,
        name="mixer_bwd",
        after=after,
        grid=(nb + 1,),
        in_specs=_mixer_specs(nb)[:7] + [
            const((BLOCK, 2 * BLOCK)),
            pl.BlockSpec((BLOCK, D_MODEL), lambda i: (cl(i), 0)),
            pl.BlockSpec((BLOCK, 2 * D_GMLP), lambda i: (cl(i), 0)),
            pl.BlockSpec((1, N_KV, ROWS4, 2 * BLOCK), lambda i: (cl(i), 0, 0, 0)),
            pl.BlockSpec((1, N_KV, ROWS4, LANES), lambda i: (cl(i), 0, 0, 0)),
        ],
        out_specs=[
            pl.BlockSpec((BLOCK, D_MAIN), lambda i: (cl(i), 0)),
            pl.BlockSpec((BLOCK, 2 * D_KV), lambda i: (jnp.maximum(i - 1, 0), 0)),
            const((1, D_GMLP)),
            const((GROUPS, BLOCK, BLOCK)),
            const((GROUPS, BLOCK)),
            pl.BlockSpec(memory_space=pltpu.SMEM),
        ],
        out_shape=[
            jax.ShapeDtypeStruct((s, D_MAIN), BF),
            jax.ShapeDtypeStruct((s, 2 * D_KV), BF),
            jax.ShapeDtypeStruct((1, D_GMLP), F32),
            jax.ShapeDtypeStruct((GROUPS, BLOCK, BLOCK), F32),
            jax.ShapeDtypeStruct((GROUPS, BLOCK), F32),
            jax.ShapeDtypeStruct((N_Q + 1, REL_BUCKETS), F32),
        ],
        scratch_shapes=[
            pltpu.VMEM((N_KV, ROWS4, 2 * BLOCK), F32),
            pltpu.VMEM((BLOCK, 2 * D_KV), F32),
            pltpu.VMEM((N_KV, ROWS4, 1), F32),
            pltpu.VMEM((GROUPS, BLOCK, 1), F32),
        ],
        compiler_params=_params(("arbitrary",)),
    )(zuv, qkv, qkv, qkv, gv, w_sp, b_sp, bucket, dmix, *saved)


def _in_bwd(x, dh1, dzm, dkv, g1, w_in_t, after=None):
    s = x.shape[0]
    tm = min(IO_TOK_TILE, s)

    def body(x_ref, dh1_ref, dzm_ref, dkv_ref, g_ref, w_ref, dx_ref, dg_ref):
        @pl.when(pl.program_id(0) == 0)
        def _():
            dg_ref[...] = jnp.zeros_like(dg_ref)

        dhn = _nn(dzm_ref[...], w_ref[:D_MAIN, :]) + _nn(dkv_ref[...], w_ref[D_MAIN:, :])
        xv = x_ref[...]
        r = _rms_scale(xv)
        dg_ref[...] += jnp.sum(dhn * (xv * r), axis=0, keepdims=True)
        dx_ref[...] = dh1_ref[...] + _rms_bwd(dhn * g_ref[...], xv, r)

    tile = lambda cols: pl.BlockSpec((tm, cols), lambda i: (i, 0))
    row = pl.BlockSpec((1, D_MODEL), lambda i: (0, 0))
    return _call(
        body,
        name="in_bwd",
        after=after,
        grid=(s // tm,),
        in_specs=[tile(D_MODEL), tile(D_MODEL), tile(D_MAIN), tile(2 * D_KV), row, pl.BlockSpec((D_IN, D_MODEL), lambda i: (0, 0))],
        out_specs=[tile(D_MODEL), row],
        out_shape=[jax.ShapeDtypeStruct((s, D_MODEL), F32), jax.ShapeDtypeStruct((1, D_MODEL), F32)],
        compiler_params=_params(("arbitrary",)),
    )(x, dh1, dzm, dkv, g1, w_in_t)


def _wgrad_in(dzm, dkv, hn, after=None):
    s = hn.shape[0]
    tm = 2 * D_KV
    n_main = D_MAIN // tm

    def body(dzm_ref, dkv_ref, hn_ref, o_ref):
        i = pl.program_id(0)

        @pl.when(i < n_main)
        def _():
            o_ref[...] = _tn(dzm_ref[...], hn_ref[...])

        @pl.when(i == n_main)
        def _():
            o_ref[...] = _tn(dkv_ref[...], hn_ref[...])

    return _call(
        body,
        name="wgrad_in",
        after=after,
        grid=(n_main + 1,),
        in_specs=[
            pl.BlockSpec((s, tm), lambda i: (0, jnp.minimum(i, n_main - 1))),
            pl.BlockSpec((s, tm), lambda i: (0, 0)),
            pl.BlockSpec((s, D_MODEL), lambda i: (0, 0)),
        ],
        out_specs=pl.BlockSpec((tm, D_MODEL), lambda i: (i, 0)),
        out_shape=jax.ShapeDtypeStruct((D_IN, D_MODEL), F32),
        compiler_params=_params(("arbitrary",)),
    )(dzm, dkv, hn)


def _wgrad(a, b, tm, tn, name, peer_cols=0, after=None):
    s, m = a.shape
    n = b.shape[1]

    def body(a_ref, b_ref, o_ref, at_ref):
        @pl.when(pl.program_id(1) == 0)
        def _():
            at_ref[...] = a_ref[...].astype(BF).T

        r = _nn(at_ref[...], b_ref[...])
        if peer_cols:
            for q in range(tn // peer_cols):
                o_ref[q] = r[:, q * peer_cols:(q + 1) * peer_cols]
        else:
            o_ref[...] = r

    if peer_cols:
        out_spec = pl.BlockSpec((tn // peer_cols, tm, peer_cols), lambda i, j: (j, i, 0))
        out_shape = jax.ShapeDtypeStruct((n // peer_cols, m, peer_cols), F32)
    else:
        out_spec = pl.BlockSpec((tm, tn), lambda i, j: (i, j))
        out_shape = jax.ShapeDtypeStruct((m, n), F32)
    return _call(
        body,
        name=name,
        after=after,
        grid=(m // tm, n // tn),
        in_specs=[pl.BlockSpec((s, tm), lambda i, j: (0, i)), pl.BlockSpec((s, tn), lambda i, j: (0, j))],
        out_specs=out_spec,
        out_shape=out_shape,
        scratch_shapes=[pltpu.VMEM((tm, s), BF)],
        compiler_params=_params(("arbitrary", "arbitrary")),
    )(a, b)


def _adamw_math(w, g, m, v):
    m_new = ADAM_B1 * m + (1.0 - ADAM_B1) * g
    v_new = ADAM_B2 * v + (1.0 - ADAM_B2) * jnp.square(g)
    m_hat = m_new / (1.0 - ADAM_B1 ** ADAM_STEP)
    v_hat = v_new / (1.0 - ADAM_B2 ** ADAM_STEP)
    delta = -ADAM_LR * (m_hat / (jnp.sqrt(v_hat) + ADAM_EPS) + ADAM_WD * w)
    return delta, m_new, v_new


def _final_adamw(part, recv, w, m, v, name):
    r, c = w.shape
    tr = min(r, 512)

    def body(p_ref, r_ref, w_ref, m_ref, v_ref, g_ref, d_ref, mo_ref, vo_ref):
        g = p_ref[...]
        for j in range(3):
            g = g + r_ref[j].astype(F32)
        g_ref[...] = g
        d_ref[...], mo_ref[...], vo_ref[...] = _adamw_math(w_ref[...], g, m_ref[...], v_ref[...])

    spec = pl.BlockSpec((tr, c), lambda i: (i, 0))
    return _call(
        body,
        name=name,
        grid=(r // tr,),
        in_specs=[spec, pl.BlockSpec((3, tr, c), lambda i: (0, i, 0)), spec, spec, spec],
        out_specs=[spec] * 4,
        out_shape=[jax.ShapeDtypeStruct((r, c), F32)] * 4,
        compiler_params=_params(("arbitrary",)),
    )(part, recv, w, m, v)


def _rs_sum(gs, lands, blocks, name, smalls, small_lands, small_dtypes):
    n, n_small = len(gs), len(smalls)
    steps = pl.cdiv(gs[0].shape[1], 256)
    assert all(g.shape[1] % steps == 0 for g in gs)

    def body(blk_ref, *refs):
        ins, small_ins, outs = refs[:5 * n], refs[5 * n:5 * n + 2 * n_small], refs[5 * n + 2 * n_small:]
        for k in range(n):
            g0_ref, g1_ref, g2_ref, g3_ref, l_ref = ins[5 * k:5 * k + 5]
            outs[2 * k][...] = g0_ref[0] + l_ref[0]
            for j, gj_ref in enumerate((g1_ref, g2_ref, g3_ref)):
                outs[2 * k + 1][j] = (gj_ref[0] + l_ref[j + 1]).astype(BF)
        for q in range(n_small):
            outs[2 * n + q][...] = (small_ins[q][...] + small_ins[n_small + q][...]).astype(small_dtypes[q])

    def whole(a):
        return pl.BlockSpec(a.shape, lambda i, blk: (0,) * a.ndim)

    in_specs, out_specs, out_shape, operands = [], [], [], []
    for g, land in zip(gs, lands):
        _, r, c = g.shape
        tr = r // steps
        in_specs += [pl.BlockSpec((1, tr, c), lambda i, blk, j=j: (blk[j], i, 0)) for j in range(4)]
        in_specs.append(pl.BlockSpec((4, tr, c), lambda i, blk: (0, i, 0)))
        out_specs += [pl.BlockSpec((tr, c), lambda i, blk: (i, 0)), pl.BlockSpec((3, tr, c), lambda i, blk: (0, i, 0))]
        out_shape += [jax.ShapeDtypeStruct((r, c), F32), jax.ShapeDtypeStruct((3, r, c), BF)]
        operands += [g, g, g, g, land]
    in_specs += [whole(a) for a in (*smalls, *small_lands)]
    out_specs += [whole(a) for a in smalls]
    out_shape += [jax.ShapeDtypeStruct(a.shape, dt) for a, dt in zip(smalls, small_dtypes)]
    outs = _call(
        body,
        name=name,
        grid_spec=pltpu.PrefetchScalarGridSpec(num_scalar_prefetch=1, grid=(steps,), in_specs=in_specs, out_specs=out_specs),
        out_shape=out_shape,
        compiler_params=_params(("arbitrary",)),
    )(blocks, *operands, *smalls, *small_lands)
    return [(outs[2 * k], outs[2 * k + 1]) for k in range(n)], list(outs[2 * n:])


def _final_adamw_few(parts, recvs, ws, ms, vs, name):
    n = len(ws)

    def body(*refs):
        ins, outs = refs[:5 * n], refs[5 * n:]
        for k in range(n):
            p_ref, r_ref, w_ref, m_ref, v_ref = (ins[q * n + k] for q in range(5))
            g = p_ref[...]
            for j in range(3):
                g = g + r_ref[j].astype(F32)
            outs[k][...] = g
            outs[n + k][...], outs[2 * n + k][...], outs[3 * n + k][...] = _adamw_math(w_ref[...], g, m_ref[...], v_ref[...])

    outs = _call(
        body,
        name=name,
        out_shape=[jax.ShapeDtypeStruct(w.shape, F32) for w in ws] * 4,
        compiler_params=_params(),
    )(*parts, *recvs, *ws, *ms, *vs)
    return outs[:n], outs[n:2 * n], outs[2 * n:3 * n], outs[3 * n:]


def _adamw_small(own, recv, g1_own, g1_recv, weights, moms, vels, after):
    n_w = len(weights)

    def body(*refs):
        own_refs, recv_refs, g1_own_ref, g1_recv_ref = refs[:5], refs[5:10], refs[10], refs[11]
        w_refs, m_refs, v_refs = refs[12:12 + n_w], refs[12 + n_w:12 + 2 * n_w], refs[12 + 2 * n_w:12 + 3 * n_w]
        outs = refs[12 + 3 * n_w:]
        loss_ref, g_refs, d_refs = outs[0], outs[1:1 + n_w], outs[1 + n_w:1 + 2 * n_w]
        mo_refs, vo_refs = outs[1 + 2 * n_w:1 + 3 * n_w], outs[1 + 3 * n_w:]
        x, y, c = lax.axis_index("x"), lax.axis_index("y"), lax.axis_index("c")

        def in_place_order(values, my_place):
            acc = None
            for place in range(len(values)):
                r = place ^ my_place
                term = values[-1]
                for q in range(len(values) - 2, -1, -1):
                    term = jnp.where(r == q, values[q], term)
                acc = term if acc is None else acc + term
            return acc

        def total(k, *index):
            index = index or (slice(None),) * (len(own_refs[k].shape) - 1)
            across = [own_refs[k][(0, *index)], recv_refs[k][(1, *index)], recv_refs[k][(0, *index)], recv_refs[k][(2, *index)]]
            return in_place_order([v.astype(F32) for v in across], 2 * x + y)

        g1 = in_place_order([g1_own_ref[0]] + [g1_recv_ref[j] for j in range(N_DEV - 1)], 4 * x + 2 * y + c)
        grads = [
            g1, total(1), total(2), total(3),
            total(4, slice(N_Q, None), slice(0, N_Q)), total(4, slice(0, N_Q), slice(None)),
            total(0, slice(0, 1), slice(None)), total(0, slice(1, 2), slice(None))]
        loss_ref[...] = total(0, slice(2, 3), slice(0, 1))
        for k in range(n_w):
            g_refs[k][...] = grads[k]
            d_refs[k][...], mo_refs[k][...], vo_refs[k][...] = _adamw_math(w_refs[k][...], grads[k], m_refs[k][...], v_refs[k][...])

    shapes = [jax.ShapeDtypeStruct(w.shape, F32) for w in weights]
    outs = _call(
        body,
        name="adamw_small",
        after=after,
        in_specs=[pl.BlockSpec(memory_space=pltpu.VMEM)] * (12 + 3 * n_w),
        out_shape=[jax.ShapeDtypeStruct((1, 1), F32)] + shapes * 4,
        compiler_params=_params(),
    )(*own, *recv, g1_own, g1_recv, *weights, *moms, *vels)
    return outs[0], outs[1:1 + n_w], outs[1 + n_w:1 + 2 * n_w], outs[1 + 2 * n_w:1 + 3 * n_w], outs[1 + 3 * n_w:]


def _place():
    x, y, c = lax.axis_index("x"), lax.axis_index("y"), lax.axis_index("c")
    return x, y, c, [(1 - x, y), (x, 1 - y), (1 - x, 1 - y)]


def _dev_index(px, py, pc):
    return 4 * px + 2 * py + pc


HBM_SPEC = pl.BlockSpec(memory_space=pltpu.HBM)
SEM_SPEC = pl.BlockSpec(memory_space=pltpu.SEMAPHORE)
ANY_SPEC = pl.BlockSpec(memory_space=pl.ANY)
DATAFLOW = pltpu.SideEffectType.DATAFLOW_SIDE_EFFECTING


def _hbm(a):
    return pltpu.with_memory_space_constraint(a, pltpu.HBM)


def _ag_piece(land_k, block, half, peer, send_sem, recv_sem):
    ref = land_k.at[_dev_index(*block)]
    if half is not None:
        rows = land_k.shape[1] // 2
        ref = ref.at[pl.ds(half * rows, rows)]
    return pltpu.make_async_remote_copy(
        src_ref=ref, dst_ref=ref, send_sem=send_sem, recv_sem=recv_sem, device_id=peer, device_id_type=MESH)


def _ag_plan():
    x, y, c, _ = _place()
    me, sib = (x, y, c), (x, y, 1 - c)
    xn, yn, diag = (1 - x, y, c), (x, 1 - y, c), (1 - x, 1 - y, c)
    return dict(
        relay_halves=[(me, 0, xn), (me, 1, yn)],
        others=[(me, None, sib), (me, 1, xn), (me, 0, yn)],
        relays=[(xn, 0, yn), (yn, 1, xn)],
        near=[(xn, None, sib), (yn, None, sib)],
        far=[(diag, None, sib)],
    )


def _ag_stage(land, stage, send_sems, recv_sems, act):
    copies = _ag_plan()[stage]
    n = len(copies)
    for k in range(len(land)):
        for j, (block, half, peer) in enumerate(copies):
            cp = _ag_piece(land[k], block, half, peer, send_sems.at[n * k + j], recv_sems.at[n * k + j])
            if act == "start":
                cp.start()
            else:
                cp.wait_send()
                cp.wait_recv()


def _sem_shapes(*counts):
    return [pltpu.SemaphoreType.DMA((n,)) for n in counts for _ in range(2)]


def _ag_start(first, rest):
    shards = list(first) + list(rest)
    k_n, k_first = len(shards), len(first)
    k_rest = k_n - k_first

    def body(*refs):
        ins, sems, land = refs[:k_n], refs[k_n:k_n + 6], refs[k_n + 6:2 * k_n + 6]
        token, stage, wide = refs[2 * k_n + 6], refs[2 * k_n + 7:3 * k_n + 7], refs[3 * k_n + 7:4 * k_n + 7]
        own_sems = refs[4 * k_n + 7]
        x, y, c, chips = _place()
        loads = [pltpu.make_async_copy(ins[k], wide[k], own_sems.at[k]) for k in range(k_n)]
        for cp in loads:
            cp.start()

        def place(ks):
            copies = []
            for k in ks:
                loads[k].wait()
                stage[k][...] = wide[k][...].astype(BF)
                copies.append(pltpu.make_async_copy(stage[k], land[k].at[_dev_index(x, y, c)], own_sems.at[k_n + k]))
                copies[-1].start()
            for cp in copies:
                cp.wait()

        place(range(k_first))
        targets = [(x, y, 1 - c)] + [(*chip, c) for chip in chips]
        for k in range(k_first):
            for j, to in enumerate(targets):
                _ag_piece(land[k], (x, y, c), None, to, sems[0].at[4 * k + j], sems[1].at[4 * k + j]).start()
        place(range(k_first, k_n))
        _ag_stage(land[k_first:], "relay_halves", sems[2], sems[3], "start")
        _ag_stage(land[k_first:], "others", sems[4], sems[5], "start")
        token[...] = jnp.zeros_like(token)

    outs = pl.pallas_call(
        body,
        name="ag_start",
        in_specs=[ANY_SPEC] * k_n,
        out_specs=(*[SEM_SPEC] * 6, *[HBM_SPEC] * k_n, pl.BlockSpec(memory_space=pltpu.VMEM)),
        out_shape=(*_sem_shapes(4 * k_first, 2 * k_rest, 3 * k_rest),
                   *[pltpu.HBM((N_DEV,) + a.shape, BF) for a in shards], jax.ShapeDtypeStruct((8, LANES), F32)),
        scratch_shapes=[pltpu.VMEM(a.shape, dt) for dt in (BF, F32) for a in shards] + [pltpu.SemaphoreType.DMA((2 * k_n,))],
        compiler_params=pltpu.CompilerParams(has_side_effects=DATAFLOW, vmem_limit_bytes=CALL_VMEM_LIMIT),
    )(*shards)
    flying = list(outs[6:6 + k_n])
    return (outs[0], outs[1], flying[:k_first]), (outs[2:6], flying[k_first:]), outs[-1]


def _ag_split_call(lands, waits, starts, after, name):
    k_n = len(lands)
    plan_sizes = dict(relay_halves=2, others=3, relays=2, near=2, far=1)
    n_in, n_out = 2 * len(waits), 2 * len(starts)

    def body(*refs):
        land = refs[:k_n]
        in_sems = refs[k_n:k_n + n_in]
        out_sems, token = refs[len(refs) - 1 - n_out:len(refs) - 1], refs[-1]
        for w, (stage, _, _) in enumerate(waits):
            _ag_stage(land, stage, in_sems[2 * w], in_sems[2 * w + 1], "wait")
            if w < len(starts):
                _ag_stage(land, starts[w], out_sems[2 * w], out_sems[2 * w + 1], "start")
        token[...] = jnp.zeros_like(token)

    outs = pl.pallas_call(
        body,
        name=name,
        in_specs=[HBM_SPEC] * k_n + [SEM_SPEC] * n_in + [ANY_SPEC],
        out_specs=(*[HBM_SPEC] * k_n, *[SEM_SPEC] * n_out, pl.BlockSpec(memory_space=pltpu.VMEM)),
        out_shape=(*[pltpu.HBM(a.shape, a.dtype) for a in lands], *_sem_shapes(*[plan_sizes[s] * k_n for s in starts]),
                   jax.ShapeDtypeStruct((8, LANES), F32)),
        input_output_aliases={k: k for k in range(k_n)},
        compiler_params=pltpu.CompilerParams(has_side_effects=DATAFLOW),
    )(*lands, *[s for _, a, b in waits for s in (a, b)], after)
    return list(outs[:k_n]), list(outs[k_n:k_n + n_out]), outs[-1]


def _ag_mid(lands, send_sems, recv_sems, after, name):
    k_n = len(lands)

    def body(*refs):
        land = refs[:k_n]
        send1, recv1 = refs[k_n], refs[k_n + 1]
        fwd_send, fwd_recv = refs[-2], refs[-1]
        x, y, c, chips = _place()
        sources = [(x, y, 1 - c)] + [(*chip, c) for chip in chips]
        for k in range(k_n):
            mine = land[k].at[_dev_index(x, y, c)]
            for j, frm in enumerate(sources):
                got = land[k].at[_dev_index(*frm)]
                cp = pltpu.make_async_remote_copy(
                    src_ref=mine, dst_ref=got, send_sem=send1.at[4 * k + j], recv_sem=recv1.at[4 * k + j], device_id=frm, device_id_type=MESH)
                cp.wait_send()
                cp.wait_recv()
                if j >= 1:
                    pltpu.make_async_remote_copy(
                        src_ref=got, dst_ref=got, send_sem=fwd_send.at[3 * k + j - 1], recv_sem=fwd_recv.at[3 * k + j - 1],
                        device_id=(x, y, 1 - c), device_id_type=MESH).start()

    outs = pl.pallas_call(
        body,
        name=name,
        in_specs=[HBM_SPEC] * k_n + [SEM_SPEC, SEM_SPEC, ANY_SPEC],
        out_specs=(*[HBM_SPEC] * k_n, SEM_SPEC, SEM_SPEC),
        out_shape=(*[pltpu.HBM(a.shape, a.dtype) for a in lands], pltpu.SemaphoreType.DMA((3 * k_n,)), pltpu.SemaphoreType.DMA((3 * k_n,))),
        input_output_aliases={k: k for k in range(k_n)},
        compiler_params=pltpu.CompilerParams(has_side_effects=DATAFLOW),
    )(*lands, send_sems, recv_sems, after)
    return list(outs[:k_n]), outs[-2], outs[-1]


def _ag_end(lands, fwd_send, fwd_recv, name):
    k_n = len(lands)

    def body(*refs):
        land = refs[:k_n]
        fsend, frecv = refs[k_n], refs[k_n + 1]
        x, y, c, chips = _place()
        for k in range(k_n):
            for j, chip in enumerate(chips):
                cp = pltpu.make_async_remote_copy(
                    src_ref=land[k].at[_dev_index(*chip, c)], dst_ref=land[k].at[_dev_index(*chip, 1 - c)],
                    send_sem=fsend.at[3 * k + j], recv_sem=frecv.at[3 * k + j], device_id=(x, y, 1 - c), device_id_type=MESH)
                cp.wait_send()
                cp.wait_recv()

    outs = pl.pallas_call(
        body,
        name=name,
        in_specs=[HBM_SPEC] * k_n + [SEM_SPEC, SEM_SPEC],
        out_specs=tuple([HBM_SPEC] * k_n),
        out_shape=tuple(pltpu.HBM(a.shape, a.dtype) for a in lands),
        input_output_aliases={k: k for k in range(k_n)},
        compiler_params=pltpu.CompilerParams(has_side_effects=DATAFLOW),
    )(*lands, fwd_send, fwd_recv)
    return list(outs)


def _chips4():
    x, y, c, others = _place()
    return x, y, c, [(x, y)] + others


def _route_sibling(j):
    x, y, c, chips = _chips4()
    return _dev_index(*chips[j], 1 - c), j, (x, y, 1 - c)


def _route_chips(j):
    x, y, c, chips = _chips4()
    return j, j, (*chips[j + 1], c)


def _route_sibling_whole(j):
    x, y, c, _ = _chips4()
    return 0, 0, (x, y, 1 - c)


def _route_chips_whole(j):
    x, y, c, chips = _chips4()
    return 0, j, (*chips[j + 1], c)


def _route_everyone(j):
    x, y, c, _ = _chips4()
    flip = [(j + 1) >> 2 & 1, (j + 1) >> 1 & 1, (j + 1) & 1]
    return 0, j, tuple(1 - v if f else v for v, f in zip((x, y, c), flip))


def _xchg_copies(routes, src, dst, send_sems, recv_sems):
    copies, sem = [], 0
    for k, (route, n) in enumerate(routes):
        for j in range(n):
            si, di, peer = route(j)
            copies.append(pltpu.make_async_remote_copy(
                src_ref=src[k].at[si], dst_ref=dst[k].at[di], send_sem=send_sems.at[sem], recv_sem=recv_sems.at[sem],
                device_id=peer, device_id_type=MESH))
            sem += 1
    return copies


def _xchg_start(srcs, slot_shapes, routes, name, after=None):
    k_n = len(srcs)
    n_in = 2 * k_n + (after is not None)
    n_sem = sum(n for _, n in routes)
    dsts = [lax.empty((n,) + tuple(sh), a.dtype) for sh, a, (_, n) in zip(slot_shapes, srcs, routes)]

    def body(*refs):
        src, dst = refs[:k_n], refs[k_n:2 * k_n]
        send_sems, recv_sems, token = refs[n_in], refs[n_in + 1], refs[-1]
        for cp in _xchg_copies(routes, src, dst, send_sems, recv_sems):
            cp.start()
        token[...] = jnp.zeros_like(token)

    arrays = list(srcs) + dsts
    outs = pl.pallas_call(
        body,
        name=name,
        in_specs=[HBM_SPEC] * (2 * k_n) + [ANY_SPEC] * (n_in - 2 * k_n),
        out_specs=(SEM_SPEC, SEM_SPEC, *[HBM_SPEC] * (2 * k_n), pl.BlockSpec(memory_space=pltpu.VMEM)),
        out_shape=(pltpu.SemaphoreType.DMA((n_sem,)), pltpu.SemaphoreType.DMA((n_sem,)),
                   *[pltpu.HBM(a.shape, a.dtype) for a in arrays], jax.ShapeDtypeStruct((8, LANES), F32)),
        input_output_aliases={i: 2 + i for i in range(2 * k_n)},
        compiler_params=pltpu.CompilerParams(has_side_effects=DATAFLOW),
    )(*[_hbm(a) for a in arrays], *([] if after is None else [after]))
    return outs[0], outs[1], list(outs[2:2 + k_n]), list(outs[2 + k_n:2 + 2 * k_n]), outs[-1]


def _xchg_wait(send_sems, recv_sems, srcs, dsts, routes, after, name):
    k_n = len(srcs)

    def body(*refs):
        src, dst = refs[:k_n], refs[k_n:2 * k_n]
        for cp in _xchg_copies(routes, src, dst, refs[2 * k_n], refs[2 * k_n + 1]):
            cp.wait_send()
            cp.wait_recv()

    arrays = list(srcs) + list(dsts)
    outs = pl.pallas_call(
        body,
        name=name,
        in_specs=[HBM_SPEC] * (2 * k_n) + [SEM_SPEC, SEM_SPEC, ANY_SPEC],
        out_specs=tuple([HBM_SPEC] * (2 * k_n)),
        out_shape=tuple(pltpu.HBM(a.shape, a.dtype) for a in arrays),
        input_output_aliases={i: i for i in range(2 * k_n)},
        compiler_params=pltpu.CompilerParams(has_side_effects=DATAFLOW),
    )(*arrays, send_sems, recv_sems, after)
    return list(outs[:k_n]), list(outs[k_n:])


SMALL = ("norm1_gain", "gmlp_v_gain", "w_spatial", "b_spatial", "attn_sinks", "rel_bias_table", "norm2_gain", "final_gain")
LANES = 128


def _swap_start(grads, smalls, tag, after=None):
    srcs = list(grads) + [a[None] for a in smalls]
    shapes = [g.shape[1:] for g in grads] + [a.shape for a in smalls]
    routes = [(_route_sibling, 4)] * len(grads) + [(_route_sibling_whole, 1)] * len(smalls)
    return _xchg_start(srcs, shapes, routes, f"rs_{tag}_swap_start", after), routes, len(grads)


def _swap_sums(swap, after, tag, small_dtypes=None):
    (send1, recv1, src1, land1, _), routes, n_rs = swap
    x, y, c, chips = _chips4()
    blocks = jnp.stack([_dev_index(*chip, c) for chip in chips]).astype(jnp.int32)
    src1, land1 = _xchg_wait(send1, recv1, src1, land1, routes, after, f"rs_{tag}_swap_wait")
    return _rs_sum(src1[:n_rs], land1[:n_rs], blocks, f"rs_{tag}_sum", src1[n_rs:], land1[n_rs:],
                   small_dtypes or [F32] * (len(src1) - n_rs))


def _chips_start(sums, small_sums, tag, everyone=()):
    sends = [ps[1] for ps in sums] + list(small_sums) + [a[None] for a in everyone]
    routes = [(_route_chips, 3)] * len(sums) + [(_route_chips_whole, 3)] * len(small_sums) + [(_route_everyone, 7)] * len(everyone)
    return _xchg_start(sends, [a.shape[1:] for a in sends], routes, f"rs_{tag}_chips_start"), routes


def _chips_wait(chips, after, tag):
    (send2, recv2, src2, land2, _), routes = chips
    return _xchg_wait(send2, recv2, src2, land2, routes, after, f"rs_{tag}_chips_wait")


def kernel(x, p, norm1_gain, w_in, gmlp_v_gain, w_spatial, b_spatial, attn_sinks, rel_bias_table, w_out, norm2_gain, w_ff1, w_ff2, w_ple_proj, w_ple_gate, final_gain, loss_target, m_norm1_gain, m_w_in, m_gmlp_v_gain, m_w_spatial, m_b_spatial, m_attn_sinks, m_rel_bias_table, m_w_out, m_norm2_gain, m_w_ff1, m_w_ff2, m_w_ple_proj, m_w_ple_gate, m_final_gain, v_norm1_gain, v_w_in, v_gmlp_v_gain, v_w_spatial, v_b_spatial, v_attn_sinks, v_rel_bias_table, v_w_out, v_norm2_gain, v_w_ff1, v_w_ff2, v_w_ple_proj, v_w_ple_gate, v_final_gain):
    args = dict(locals())
    s = x.shape[1]
    big = ("w_in", "w_out", "w_ff1", "w_ff2", "w_ple_proj", "w_ple_gate")

    x2, p2, t2 = x.reshape(s, D_MODEL), p.reshape(s, PLE_DIM), loss_target.reshape(s, D_MODEL)
    g1, gv, w_sp, b_sp, sinks, table, g2, gf = (args[n] for n in SMALL)
    bucket = jnp.asarray(_bucket_table())
    b_col = b_sp.reshape(GROUPS, BLOCK, 1)

    def shard(name):
        return args[name][0].T if name.endswith("w_in") else args[name][0]

    (send_in, recv_in, fly_in), (rest_sems, fly_rest), token = _ag_start([shard(big[0])], [shard(n) for n in big[1:]])
    mid_in, fwd_send_in, fwd_recv_in = _ag_mid(fly_in, send_in, recv_in, token, "ag_mid_w_in")
    g_in = _ag_end(mid_in, fwd_send_in, fwd_recv_in, "ag_end_w_in")[0]
    full_in = g_in.reshape(D_IN, D_MODEL)

    zuv, qkv, hn1 = _in_proj(x2, g1, full_in)
    fly_rest, relay_sems, relayed = _ag_split_call(
        fly_rest, [("relay_halves", *rest_sems[:2])], ["relays"], zuv, "ag_relay")
    mix, *saved = _mixer_fwd(zuv, qkv, gv, w_sp[0], b_col, sinks, table.T, bucket, after=relayed)
    fly_rest, fwd_sems, _ = _ag_split_call(
        fly_rest, [("others", *rest_sems[2:]), ("relays", *relay_sems)], ["near", "far"], mix, "ag_mid_rest")
    g_out, g_ff1, g_ff2, g_proj, g_gate = _ag_split_call(
        fly_rest, [("near", *fwd_sems[:2]), ("far", *fwd_sems[2:])], [], mix, "ag_end_rest")[0]
    full_out, full_ff2, full_gate = g_out.reshape(D_MODEL, D_MODEL), g_ff2.reshape(D_FF, D_MODEL), g_gate.reshape(D_MODEL, D_MODEL)

    tail_small, dh1, dmix, hn2, a, df, dh2, gw_out, gw_proj, gw_gate = _tail(
        x2, mix, p2, t2, g2, gf.reshape(1, D_MODEL), full_out, g_ff1, full_ff2, full_gate, g_proj)
    gw_out, gw_gate = (g.reshape(N_DEV, D_MODEL // N_DEV, D_MODEL) for g in (gw_out, gw_gate))

    few_names = ("w_out", "w_ple_proj", "w_ple_gate")
    ff1_names = ("w_ff1",) + few_names
    gw_ff1 = _wgrad(hn2, df, D_MODEL, 2 * D_FF // N_DEV, "wgrad_ff1", peer_cols=D_FF // N_DEV)
    ff1_swap = _swap_start([gw_ff1, gw_out, gw_proj, gw_gate], [tail_small], "ff1")
    gw_ff2 = _wgrad(a, dh2, 1024, D_MODEL, "wgrad_ff2", after=ff1_swap[0][4]).reshape(N_DEV, D_FF // N_DEV, D_MODEL)
    ff1_sums, ff1_small = _swap_sums(ff1_swap, gw_ff2, "ff1")
    ff1_chips = _chips_start(ff1_sums, ff1_small, "ff1")
    ff2_swap = _swap_start([gw_ff2], [], "ff2", after=ff1_chips[0][4])
    dzm, dkv, d_gv, d_wsp, d_bsp, d_attn = _mixer_bwd(
        zuv, qkv, dmix, saved, gv, w_sp[0], b_col, bucket, after=ff2_swap[0][4])
    ff2_sums, _ = _swap_sums(ff2_swap, dzm, "ff2")
    ff2_chips = _chips_start(ff2_sums, [], "ff2")
    gw_in = _wgrad_in(dzm, dkv, hn1, after=ff2_chips[0][4]).reshape(N_DEV, D_IN // N_DEV, D_MODEL)
    in_swap = _swap_start([gw_in], [d_gv, d_wsp, d_bsp, d_attn], "in")
    dx, d_g1 = _in_bwd(x2, dh1, dzm, dkv, g1, full_in, after=in_swap[0][4])
    grad_x = dx.reshape(x.shape)
    in_sums, in_small = _swap_sums(in_swap, dx, "in", small_dtypes=[F32, BF, F32, F32])
    in_chips = _chips_start(in_sums, in_small, "in", everyone=[d_g1])

    grads, deltas, new_m, new_v = {}, {}, {}, {}
    after, small_own, small_recv = in_chips[0][4], [], []
    for chips, sums, names, tag in ((ff1_chips, ff1_sums, ff1_names, "ff1"), (ff2_chips, ff2_sums, ("w_ff2",), "ff2"),
                                    (in_chips, in_sums, ("w_in",), "in")):
        sent, recvs = _chips_wait(chips, after, tag)
        small_own += sent[len(names):]
        small_recv += recvs[len(names):]
        tiled = [n for n in names if n not in few_names]
        done = [_final_adamw(part, recv, shard(n), shard("m_" + n), shard("v_" + n), "adamw_" + n)
                for n, (part, _), recv in zip(tiled, sums, recvs)]
        few = names[len(tiled):]
        if few:
            done += zip(*_final_adamw_few(
                [ps[0] for ps in sums[len(tiled):]], recvs[len(tiled):len(names)], [shard(n) for n in few],
                [shard("m_" + n) for n in few], [shard("v_" + n) for n in few], "adamw_few"))
        for n, results in zip(names, done):
            for dst, arr in zip((grads, deltas, new_m, new_v), results):
                dst[n] = (arr.T if n == "w_in" else arr)[None]
            after = results[1]

    views = {"w_spatial": (GROUPS, BLOCK, BLOCK), "b_spatial": (GROUPS, BLOCK), "final_gain": (1, D_MODEL)}
    def view(name, base):
        return args[name].T if base == "rel_bias_table" else args[name].reshape(views.get(base, args[name].shape))

    small_in = [[view(pre + n, n) for n in SMALL] for pre in ("", "m_", "v_")]
    loss, *small_out = _adamw_small(small_own[:-1], small_recv[:-1], small_own[-1], small_recv[-1], *small_in, after=after)
    for dst, arrays in zip((grads, deltas, new_m, new_v), small_out):
        for n, arr in zip(SMALL, arrays):
            dst[n] = arr.T if n == "rel_bias_table" else arr.reshape(args[n].shape)
    loss = loss[0, 0]

    order = ("norm1_gain", "w_in", "gmlp_v_gain", "w_spatial", "b_spatial", "attn_sinks", "rel_bias_table", "w_out",
             "norm2_gain", "w_ff1", "w_ff2", "w_ple_proj", "w_ple_gate", "final_gain")
    return (loss, grad_x, *[grads[n] for n in order], *[deltas[n] for n in order],
            *[new_m[n] for n in order], *[new_v[n] for n in order])
```

```python
import math

import numpy as np
import jax
import jax.numpy as jnp
from jax import lax
from jax.experimental import pallas as pl
from jax.experimental.pallas import tpu as pltpu

F32 = jnp.float32
BF = jnp.bfloat16
MESH = pl.DeviceIdType.MESH
N_DEV = 8

D_MODEL = 1024
PLE_DIM = 256
D_GMLP = 512
GROUPS = 4
GDIM = 128
BLOCK = 128
D_ATTN = 512
HEAD_DIM = 64
N_Q = 8
Q_PER_KV = 4
N_KV = N_Q // Q_PER_KV
ROWS4 = Q_PER_KV * BLOCK
D_KV = 128
D_FF = 4096
D_IN = 1792
D_MAIN = 2 * D_GMLP + D_ATTN
REL_BUCKETS = 32
EPS = 1e-6
NEG_INF = -1e30
SCALE = HEAD_DIM ** -0.5
GELU_C = math.sqrt(2.0 / math.pi)
GELU_A = 0.044715

ADAM_LR = 0.001
ADAM_B1 = 0.9
ADAM_B2 = 0.999
ADAM_EPS = 1e-08
ADAM_WD = 0.01
ADAM_STEP = 10

V7X_VMEM_LIMIT = 61440 * 1024
CALL_VMEM_LIMIT = 49152 * 1024
TOK_TILE = 256
IO_TOK_TILE = 512


def _call(body, after=None, **kw):
    if after is None:
        return pl.pallas_call(body, **kw)
    n_in = len(kw["in_specs"])

    def ordered(*refs):
        body(*refs[:n_in], *refs[n_in + 1:])

    kw["in_specs"] = list(kw["in_specs"]) + [pl.BlockSpec(memory_space=pl.ANY)]
    fn = pl.pallas_call(ordered, **kw)
    return lambda *operands: fn(*operands, after)


def _params(sem=None, limit=CALL_VMEM_LIMIT):
    if sem is None:
        return pltpu.CompilerParams(vmem_limit_bytes=limit)
    return pltpu.CompilerParams(dimension_semantics=sem, vmem_limit_bytes=limit)


def _nn(a, b):
    return jnp.dot(a, b, preferred_element_type=F32)


def _nt(a, b):
    return lax.dot_general(a, b, (((1,), (1,)), ((), ())), preferred_element_type=F32)


def _tn(a, b):
    return lax.dot_general(a, b, (((0,), (0,)), ((), ())), preferred_element_type=F32)


def _gelu_tanh(x):
    return jnp.tanh(GELU_C * (x + GELU_A * (x * x * x)))


def _gelu(x, t):
    return x * (0.5 * (1.0 + t))


def _gelu_and_grad(x, t):
    cdf = 0.5 * (1.0 + t)
    return x * cdf, cdf + 0.5 * x * (1.0 - t * t) * (GELU_C * (1.0 + 3.0 * GELU_A * (x * x)))


def _rms_scale(x):
    return lax.rsqrt(jnp.mean(x * x, axis=-1, keepdims=True) + EPS)


def _rms_bwd(dxn, x, r):
    return r * dxn - x * ((r * r * r) * jnp.mean(dxn * x, axis=-1, keepdims=True))


def _bucket_table():
    a = np.arange(BLOCK)[:, None]
    j = np.arange(2 * BLOCK)[None, :]
    n = BLOCK + a - j
    valid = (n >= 0) & (n < BLOCK)
    nc = np.maximum(n, 0)
    max_exact = REL_BUCKETS // 2
    nf = np.maximum(nc, 1).astype(np.float32)
    large = max_exact + (
        np.log(nf / np.float32(max_exact)) / np.float32(math.log(BLOCK / max_exact)) * np.float32(REL_BUCKETS - max_exact)
    ).astype(np.int32)
    large = np.minimum(large, REL_BUCKETS - 1)
    bucket = np.where(nc < max_exact, nc, large)
    return np.where(valid, bucket, -1).astype(np.int32)


def _in_proj(x, g1, w_in_t):
    s = x.shape[0]
    tm = min(IO_TOK_TILE, s)

    def body(x_ref, g_ref, w_ref, zuv_ref, qkv_ref, hn_ref):
        xv = x_ref[...]
        hn = ((xv * _rms_scale(xv)) * g_ref[...]).astype(BF)
        hn_ref[...] = hn
        z = _nt(hn, w_ref[...])
        zuv_ref[...] = z[:, : 2 * D_GMLP]
        qkv_ref[...] = z[:, 2 * D_GMLP:].astype(BF)

    return _call(
        body,
        name="in_proj",
        grid=(s // tm,),
        in_specs=[
            pl.BlockSpec((tm, D_MODEL), lambda i: (i, 0)),
            pl.BlockSpec((1, D_MODEL), lambda i: (0, 0)),
            pl.BlockSpec((D_IN, D_MODEL), lambda i: (0, 0)),
        ],
        out_specs=[
            pl.BlockSpec((tm, 2 * D_GMLP), lambda i: (i, 0)),
            pl.BlockSpec((tm, D_ATTN + 2 * D_KV), lambda i: (i, 0)),
            pl.BlockSpec((tm, D_MODEL), lambda i: (i, 0)),
        ],
        out_shape=[
            jax.ShapeDtypeStruct((s, 2 * D_GMLP), F32),
            jax.ShapeDtypeStruct((s, D_ATTN + 2 * D_KV), BF),
            jax.ShapeDtypeStruct((s, D_MODEL), BF),
        ],
        compiler_params=_params(("arbitrary",)),
    )(x, g1, w_in_t)


def _head_rows(h):
    kh, g = divmod(h, Q_PER_KV)
    return kh, slice(g * BLOCK, (g + 1) * BLOCK)


def _build_bias(bias_ref, bucket_ref, table_ref):
    bucket = bucket_ref[...]
    for h in range(N_Q):
        acc = jnp.zeros((BLOCK, 2 * BLOCK), F32)
        for b in range(REL_BUCKETS):
            acc = jnp.where(bucket == b, table_ref[h, b], acc)
        kh, rows = _head_rows(h)
        bias_ref[kh, rows, :] = acc


def _window_masks(i):
    row = lax.broadcasted_iota(jnp.int32, (ROWS4, BLOCK), 0) & (BLOCK - 1)
    col = lax.broadcasted_iota(jnp.int32, (ROWS4, BLOCK), 1)
    return (col > row) & (i > 0), col <= row


def _stack_heads(ref, kh, offset):
    first = offset + kh * Q_PER_KV * HEAD_DIM
    return jnp.concatenate(
        [ref[:, first + g * HEAD_DIM: first + (g + 1) * HEAD_DIM].astype(BF) for g in range(Q_PER_KV)], axis=0)


def _stack_sinks(sink_ref, kh):
    return jnp.concatenate([jnp.full((BLOCK, 1), sink_ref[0, kh * Q_PER_KV + g], F32) for g in range(Q_PER_KV)], axis=0)


def _tril_bf16(w_ref, g):
    row = lax.broadcasted_iota(jnp.int32, (BLOCK, BLOCK), 0)
    col = lax.broadcasted_iota(jnp.int32, (BLOCK, BLOCK), 1)
    return jnp.where(col <= row, w_ref[g], 0.0).astype(BF)


def _attn_probs(q_h, k_prev, k_cur, bias_h, sink, valid_prev, valid_cur):
    l_prev = jnp.where(valid_prev, _nt(q_h, k_prev) * SCALE + bias_h[:, :BLOCK], NEG_INF)
    l_cur = jnp.where(valid_cur, _nt(q_h, k_cur) * SCALE + bias_h[:, BLOCK:], NEG_INF)
    m = jnp.maximum(jnp.maximum(jnp.max(l_prev, axis=-1, keepdims=True), jnp.max(l_cur, axis=-1, keepdims=True)), sink)
    e_prev = jnp.exp(l_prev - m)
    e_cur = jnp.exp(l_cur - m)
    e_sink = jnp.exp(sink - m)
    denom = jnp.sum(e_prev, axis=-1, keepdims=True) + jnp.sum(e_cur, axis=-1, keepdims=True) + e_sink
    return e_prev / denom, e_cur / denom, e_sink / denom


def _mixer_specs(nb):
    cl = lambda i: jnp.minimum(i, nb - 1)
    return [
        pl.BlockSpec((BLOCK, 2 * D_GMLP), lambda i: (cl(i), 0)),
        pl.BlockSpec((BLOCK, D_ATTN), lambda i: (cl(i), 0)),
        pl.BlockSpec((BLOCK, 2 * D_KV), lambda i: (cl(i), D_ATTN // (2 * D_KV))),
        pl.BlockSpec((BLOCK, 2 * D_KV), lambda i: (jnp.maximum(cl(i) - 1, 0), D_ATTN // (2 * D_KV))),
        pl.BlockSpec((1, D_GMLP), lambda i: (0, 0)),
        pl.BlockSpec((GROUPS, BLOCK, BLOCK), lambda i: (0, 0, 0)),
        pl.BlockSpec((GROUPS, BLOCK, 1), lambda i: (0, 0, 0)),
        pl.BlockSpec(memory_space=pltpu.SMEM),
        pl.BlockSpec(memory_space=pltpu.SMEM),
        pl.BlockSpec((BLOCK, 2 * BLOCK), lambda i: (0, 0)),
    ]


def _mixer_fwd(zuv, qkv, gv, w_sp, b_sp, sinks, table, bucket, after=None):
    s = zuv.shape[0]
    nb = s // BLOCK

    def body(zuv_ref, q_ref, kvc_ref, kvp_ref, gv_ref, w_ref, b_ref, sink_ref, table_ref, bucket_ref,
             mix_ref, tanh_ref, prob_ref, psink_ref, bias_ref):
        i = pl.program_id(0)

        @pl.when(i == 0)
        def _():
            _build_bias(bias_ref, bucket_ref, table_ref)

        t = _gelu_tanh(zuv_ref[...])
        tanh_ref[...] = t
        u = _gelu(zuv_ref[:, :D_GMLP], t[:, :D_GMLP])
        vg = _gelu(zuv_ref[:, D_GMLP:], t[:, D_GMLP:])
        for g in range(GROUPS):
            sl = slice(g * GDIM, (g + 1) * GDIM)
            vg_g = vg[:, sl]
            vn = ((vg_g * _rms_scale(vg_g)) * gv_ref[:, sl]).astype(BF)
            sv = _nn(_tril_bf16(w_ref, g), vn) + b_ref[g]
            mix_ref[:, sl] = (u[:, sl] * sv).astype(BF)

        valid_prev, valid_cur = _window_masks(i)
        for kh in range(N_KV):
            ksl = slice(kh * HEAD_DIM, (kh + 1) * HEAD_DIM)
            vsl = slice(D_KV + kh * HEAD_DIM, D_KV + (kh + 1) * HEAD_DIM)
            q4 = _stack_heads(q_ref, kh, 0)
            p_prev, p_cur, p_sink = _attn_probs(
                q4, kvp_ref[:, ksl], kvc_ref[:, ksl], bias_ref[kh], _stack_sinks(sink_ref, kh), valid_prev, valid_cur)
            prob_ref[0, kh, :, :BLOCK] = p_prev
            prob_ref[0, kh, :, BLOCK:] = p_cur
            psink_ref[0, kh] = jnp.broadcast_to(p_sink, (ROWS4, LANES))
            o4 = _nn(p_prev.astype(BF), kvp_ref[:, vsl]) + _nn(p_cur.astype(BF), kvc_ref[:, vsl])
            for g in range(Q_PER_KV):
                first = D_GMLP + (kh * Q_PER_KV + g) * HEAD_DIM
                mix_ref[:, first:first + HEAD_DIM] = o4[g * BLOCK:(g + 1) * BLOCK].astype(BF)

    return _call(
        body,
        name="mixer_fwd",
        after=after,
        grid=(nb,),
        in_specs=_mixer_specs(nb),
        out_specs=[
            pl.BlockSpec((BLOCK, D_MODEL), lambda i: (i, 0)),
            pl.BlockSpec((BLOCK, 2 * D_GMLP), lambda i: (i, 0)),
            pl.BlockSpec((1, N_KV, ROWS4, 2 * BLOCK), lambda i: (i, 0, 0, 0)),
            pl.BlockSpec((1, N_KV, ROWS4, LANES), lambda i: (i, 0, 0, 0)),
        ],
        out_shape=[
            jax.ShapeDtypeStruct((s, D_MODEL), BF),
            jax.ShapeDtypeStruct((s, 2 * D_GMLP), F32),
            jax.ShapeDtypeStruct((nb, N_KV, ROWS4, 2 * BLOCK), F32),
            jax.ShapeDtypeStruct((nb, N_KV, ROWS4, LANES), F32),
        ],
        scratch_shapes=[pltpu.VMEM((N_KV, ROWS4, 2 * BLOCK), F32)],
        compiler_params=_params(("arbitrary",)),
    )(zuv, qkv, qkv, qkv, gv, w_sp, b_sp, sinks, table, bucket)


def _tail(x, mix, p, t, g2, gf, w_out, w_ff1, w_ff2, w_gate, w_proj):
    s = x.shape[0]
    tm = min(TOK_TILE, s)
    n_ff = w_ff1.shape[0]
    fc = D_FF // n_ff
    pc = D_MODEL // N_DEV

    def body(x_ref, mix_ref, p_ref, t_ref, g2_ref, gf_ref, wo_ref, w1_ref, w2_ref, wg_ref, wp_ref,
             small_ref, dh1_ref, dmix_ref, hn2_ref, a_ref, df_ref, dh2_ref, gwo_ref, gwp_ref, gwg_ref, f_ref):
        i = pl.program_id(0)

        @pl.when(i == 0)
        def _():
            small_ref[...] = jnp.zeros_like(small_ref)
            gwo_ref[...] = jnp.zeros_like(gwo_ref)
            gwp_ref[...] = jnp.zeros_like(gwp_ref)
            gwg_ref[...] = jnp.zeros_like(gwg_ref)

        h1 = x_ref[...] + _nn(mix_ref[...], wo_ref[...])
        dh1_ref[...] = h1
        r2 = _rms_scale(h1)
        hn2_ref[...] = ((h1 * r2) * g2_ref[...]).astype(BF)
        h2 = h1
        for c in range(n_ff):
            f = _nn(hn2_ref[...], w1_ref[c])
            f_ref[:, c * fc:(c + 1) * fc] = f
            a = jnp.square(jnp.maximum(f, 0.0)).astype(BF)
            a_ref[:, c * fc:(c + 1) * fc] = a
            h2 = h2 + _nn(a, w2_ref[c * fc:(c + 1) * fc, :])
        h2b = h2.astype(BF)
        gate = jax.nn.sigmoid(_nn(h2b, wg_ref[...]))
        pb = p_ref[...].astype(BF)
        pp = jnp.concatenate([_nn(pb, wp_ref[q]) for q in range(N_DEV)], axis=1)
        h3 = h2 + gate * pp
        rf = _rms_scale(h3)
        gf_v = gf_ref[...]
        err = (h3 * rf) * gf_v - t_ref[...]
        small_ref[2:3, :] += jnp.sum(jnp.sum(err * err, axis=-1, keepdims=True), axis=0, keepdims=True) * (0.5 / D_MODEL)

        dy = err * (1.0 / D_MODEL)
        small_ref[1:2, :] += jnp.sum(dy * (h3 * rf), axis=0, keepdims=True)
        dh3 = _rms_bwd(dy * gf_v, h3, rf)
        gw_proj = _tn(pb, (dh3 * gate).astype(BF))
        for q in range(N_DEV):
            gwp_ref[q] += gw_proj[:, q * pc:(q + 1) * pc]
        dgl = ((dh3 * pp) * (gate * (1.0 - gate))).astype(BF)
        gwg_ref[...] += _tn(h2b, dgl)
        dh2 = dh3 + _nt(dgl, wg_ref[...])
        dmix_ref[...] = dh2
        dh2_ref[...] = dh2.astype(BF)
        dhn2 = jnp.zeros((tm, D_MODEL), F32)
        for c in range(n_ff):
            da = _nt(dh2_ref[...], w2_ref[c * fc:(c + 1) * fc, :])
            df = (da * (2.0 * jnp.maximum(f_ref[:, c * fc:(c + 1) * fc], 0.0))).astype(BF)
            df_ref[:, c * fc:(c + 1) * fc] = df
            dhn2 = dhn2 + _nt(df, w1_ref[c])
        h1 = dh1_ref[...]
        small_ref[0:1, :] += jnp.sum(dhn2 * (h1 * r2), axis=0, keepdims=True)
        dh1 = dmix_ref[...] + _rms_bwd(dhn2 * g2_ref[...], h1, r2)
        dh1_ref[...] = dh1
        dh1b = dh1.astype(BF)
        gwo_ref[...] += _tn(mix_ref[...], dh1b)
        dmix_ref[...] = _nt(dh1b, wo_ref[...])

    tile = lambda cols: pl.BlockSpec((tm, cols), lambda i: (i, 0))
    whole = lambda shape: pl.BlockSpec(shape, lambda i: (0,) * len(shape), pipeline_mode=pl.Buffered(1))
    total = lambda shape: pl.BlockSpec(shape, lambda i: (0,) * len(shape))
    row = pl.BlockSpec((1, D_MODEL), lambda i: (0, 0))
    act = lambda cols, dt: jax.ShapeDtypeStruct((s, cols), dt)
    gw_shapes = [(D_MODEL, D_MODEL), (N_DEV, PLE_DIM, pc), (D_MODEL, D_MODEL)]
    return _call(
        body,
        name="tail",
        grid=(s // tm,),
        in_specs=[tile(D_MODEL), tile(D_MODEL), tile(PLE_DIM), tile(D_MODEL), row, row,
                  whole(w_out.shape), whole(w_ff1.shape), whole(w_ff2.shape), whole(w_gate.shape), whole(w_proj.shape)],
        out_specs=[total((8, D_MODEL)), tile(D_MODEL), tile(D_MODEL), tile(D_MODEL), tile(D_FF), tile(D_FF), tile(D_MODEL),
                   *[total(shape) for shape in gw_shapes]],
        out_shape=[jax.ShapeDtypeStruct((8, D_MODEL), F32),
                   act(D_MODEL, F32), act(D_MODEL, F32), act(D_MODEL, BF), act(D_FF, BF), act(D_FF, BF), act(D_MODEL, BF),
                   *[jax.ShapeDtypeStruct(shape, F32) for shape in gw_shapes]],
        scratch_shapes=[pltpu.VMEM((tm, D_FF), F32)],
        compiler_params=_params(("arbitrary",), V7X_VMEM_LIMIT),
    )(x, mix, p, t, g2, gf, w_out, w_ff1, w_ff2, w_gate, w_proj)


def _mixer_bwd(zuv, qkv, dmix, saved, gv, w_sp, b_sp, bucket, after=None):
    s = zuv.shape[0]
    nb = s // BLOCK

    def body(zuv_ref, q_ref, kvc_ref, kvp_ref, gv_ref, w_ref, b_ref, bucket_ref, dmix_ref, tanh_ref, prob_ref, psink_ref,
             dzm_ref, dkv_ref, dgv_ref, dw_ref, db_ref, dattn_ref,
             dbias_ref, carry_ref, dsink_acc, db_acc):
        i = pl.program_id(0)

        @pl.when(i == 0)
        def _():
            dbias_ref[...] = jnp.zeros_like(dbias_ref)
            carry_ref[...] = jnp.zeros_like(carry_ref)
            dsink_acc[...] = jnp.zeros_like(dsink_acc)
            dgv_ref[...] = jnp.zeros_like(dgv_ref)
            dw_ref[...] = jnp.zeros_like(dw_ref)
            db_acc[...] = jnp.zeros_like(db_acc)

        @pl.when(i < nb)
        def _():
            u, du_dz = _gelu_and_grad(zuv_ref[:, :D_GMLP], tanh_ref[:, :D_GMLP])
            vg, dvg_dz = _gelu_and_grad(zuv_ref[:, D_GMLP:], tanh_ref[:, D_GMLP:])
            for g in range(GROUPS):
                sl = slice(g * GDIM, (g + 1) * GDIM)
                vg_g = vg[:, sl]
                rg = _rms_scale(vg_g)
                vhat = vg_g * rg
                gain = gv_ref[:, sl]
                vn = (vhat * gain).astype(BF)
                w_g = _tril_bf16(w_ref, g)
                sv = _nn(w_g, vn) + b_ref[g]
                dmix_a = dmix_ref[:, sl]
                dsv = dmix_a * u[:, sl]
                dsvb = dsv.astype(BF)
                db_acc[g] += jnp.sum(dsv, axis=-1, keepdims=True)
                dw_ref[g] += _nt(dsvb, vn)
                dvn = _tn(w_g, dsvb)
                dgv_ref[:, sl] += jnp.sum(dvn * vhat, axis=0, keepdims=True)
                dvg = _rms_bwd(dvn * gain, vg_g, rg)
                dzm_ref[:, sl] = ((dmix_a * sv) * du_dz[:, sl]).astype(BF)
                dzm_ref[:, D_GMLP + g * GDIM: D_GMLP + (g + 1) * GDIM] = (dvg * dvg_dz[:, sl]).astype(BF)

            for kh in range(N_KV):
                ksl = slice(kh * HEAD_DIM, (kh + 1) * HEAD_DIM)
                vsl = slice(D_KV + kh * HEAD_DIM, D_KV + (kh + 1) * HEAD_DIM)
                k_prev, k_cur = kvp_ref[:, ksl], kvc_ref[:, ksl]
                v_prev, v_cur = kvp_ref[:, vsl], kvc_ref[:, vsl]
                q4 = _stack_heads(q_ref, kh, 0)
                p_prev, p_cur, p_sink = prob_ref[0, kh, :, :BLOCK], prob_ref[0, kh, :, BLOCK:], psink_ref[0, kh, :, 0:1]
                do4 = _stack_heads(dmix_ref, kh, D_GMLP)
                dp_prev = _nt(do4, v_prev)
                dp_cur = _nt(do4, v_cur)
                delta = jnp.sum(p_prev * dp_prev, axis=-1, keepdims=True) + jnp.sum(p_cur * dp_cur, axis=-1, keepdims=True)
                ds_prev = p_prev * (dp_prev - delta)
                ds_cur = p_cur * (dp_cur - delta)
                dsink_acc[kh] -= p_sink * delta
                dbias_ref[kh, :, :BLOCK] += ds_prev
                dbias_ref[kh, :, BLOCK:] += ds_cur
                dsb_prev = ds_prev.astype(BF)
                dsb_cur = ds_cur.astype(BF)
                dq4 = (_nn(dsb_prev, k_prev) + _nn(dsb_cur, k_cur)) * SCALE
                for g in range(Q_PER_KV):
                    first = 2 * D_GMLP + (kh * Q_PER_KV + g) * HEAD_DIM
                    dzm_ref[:, first:first + HEAD_DIM] = dq4[g * BLOCK:(g + 1) * BLOCK].astype(BF)
                dkv_ref[:, ksl] = (carry_ref[:, ksl] + _tn(dsb_prev, q4) * SCALE).astype(BF)
                dkv_ref[:, vsl] = (carry_ref[:, vsl] + _tn(p_prev.astype(BF), do4)).astype(BF)
                carry_ref[:, ksl] = _tn(dsb_cur, q4) * SCALE
                carry_ref[:, vsl] = _tn(p_cur.astype(BF), do4)

        @pl.when(i == nb)
        def _():
            dkv_ref[...] = carry_ref[...].astype(BF)
            row = lax.broadcasted_iota(jnp.int32, (BLOCK, BLOCK), 0)
            col = lax.broadcasted_iota(jnp.int32, (BLOCK, BLOCK), 1)
            for g in range(GROUPS):
                dw_ref[g] = jnp.where(col <= row, dw_ref[g], 0.0)
                db_ref[g:g + 1, :] = jnp.sum(jnp.where(col == row, db_acc[g], 0.0), axis=0, keepdims=True)
            bucket = bucket_ref[...]
            for b in range(N_Q, REL_BUCKETS):
                dattn_ref[N_Q, b] = 0.0
            for h in range(N_Q):
                kh, rows = _head_rows(h)
                dattn_ref[N_Q, h] = jnp.sum(dsink_acc[kh, rows, :])
                dbh = dbias_ref[kh, rows, :]
                for b in range(REL_BUCKETS):
                    dattn_ref[h, b] = jnp.sum(jnp.where(bucket == b, dbh, 0.0))

    cl = lambda i: jnp.minimum(i, nb - 1)
    const = lambda shape: pl.BlockSpec(shape, lambda i: (0,) * len(shape))
    return _call(
        body,
        name="mixer_bwd",
        after=after,
        grid=(nb + 1,),
        in_specs=_mixer_specs(nb)[:7] + [
            const((BLOCK, 2 * BLOCK)),
            pl.BlockSpec((BLOCK, D_MODEL), lambda i: (cl(i), 0)),
            pl.BlockSpec((BLOCK, 2 * D_GMLP), lambda i: (cl(i), 0)),
            pl.BlockSpec((1, N_KV, ROWS4, 2 * BLOCK), lambda i: (cl(i), 0, 0, 0)),
            pl.BlockSpec((1, N_KV, ROWS4, LANES), lambda i: (cl(i), 0, 0, 0)),
        ],
        out_specs=[
            pl.BlockSpec((BLOCK, D_MAIN), lambda i: (cl(i), 0)),
            pl.BlockSpec((BLOCK, 2 * D_KV), lambda i: (jnp.maximum(i - 1, 0), 0)),
            const((1, D_GMLP)),
            const((GROUPS, BLOCK, BLOCK)),
            const((GROUPS, BLOCK)),
            pl.BlockSpec(memory_space=pltpu.SMEM),
        ],
        out_shape=[
            jax.ShapeDtypeStruct((s, D_MAIN), BF),
            jax.ShapeDtypeStruct((s, 2 * D_KV), BF),
            jax.ShapeDtypeStruct((1, D_GMLP), F32),
            jax.ShapeDtypeStruct((GROUPS, BLOCK, BLOCK), F32),
            jax.ShapeDtypeStruct((GROUPS, BLOCK), F32),
            jax.ShapeDtypeStruct((N_Q + 1, REL_BUCKETS), F32),
        ],
        scratch_shapes=[
            pltpu.VMEM((N_KV, ROWS4, 2 * BLOCK), F32),
            pltpu.VMEM((BLOCK, 2 * D_KV), F32),
            pltpu.VMEM((N_KV, ROWS4, 1), F32),
            pltpu.VMEM((GROUPS, BLOCK, 1), F32),
        ],
        compiler_params=_params(("arbitrary",)),
    )(zuv, qkv, qkv, qkv, gv, w_sp, b_sp, bucket, dmix, *saved)


def _in_bwd(x, dh1, dzm, dkv, g1, w_in_t, after=None):
    s = x.shape[0]
    tm = min(IO_TOK_TILE, s)

    def body(x_ref, dh1_ref, dzm_ref, dkv_ref, g_ref, w_ref, dx_ref, dg_ref):
        @pl.when(pl.program_id(0) == 0)
        def _():
            dg_ref[...] = jnp.zeros_like(dg_ref)

        dhn = _nn(dzm_ref[...], w_ref[:D_MAIN, :]) + _nn(dkv_ref[...], w_ref[D_MAIN:, :])
        xv = x_ref[...]
        r = _rms_scale(xv)
        dg_ref[...] += jnp.sum(dhn * (xv * r), axis=0, keepdims=True)
        dx_ref[...] = dh1_ref[...] + _rms_bwd(dhn * g_ref[...], xv, r)

    tile = lambda cols: pl.BlockSpec((tm, cols), lambda i: (i, 0))
    row = pl.BlockSpec((1, D_MODEL), lambda i: (0, 0))
    return _call(
        body,
        name="in_bwd",
        after=after,
        grid=(s // tm,),
        in_specs=[tile(D_MODEL), tile(D_MODEL), tile(D_MAIN), tile(2 * D_KV), row, pl.BlockSpec((D_IN, D_MODEL), lambda i: (0, 0))],
        out_specs=[tile(D_MODEL), row],
        out_shape=[jax.ShapeDtypeStruct((s, D_MODEL), F32), jax.ShapeDtypeStruct((1, D_MODEL), F32)],
        compiler_params=_params(("arbitrary",)),
    )(x, dh1, dzm, dkv, g1, w_in_t)


def _wgrad_in(dzm, dkv, hn, after=None):
    s = hn.shape[0]
    tm = 2 * D_KV
    n_main = D_MAIN // tm

    def body(dzm_ref, dkv_ref, hn_ref, o_ref):
        i = pl.program_id(0)

        @pl.when(i < n_main)
        def _():
            o_ref[...] = _tn(dzm_ref[...], hn_ref[...])

        @pl.when(i == n_main)
        def _():
            o_ref[...] = _tn(dkv_ref[...], hn_ref[...])

    return _call(
        body,
        name="wgrad_in",
        after=after,
        grid=(n_main + 1,),
        in_specs=[
            pl.BlockSpec((s, tm), lambda i: (0, jnp.minimum(i, n_main - 1))),
            pl.BlockSpec((s, tm), lambda i: (0, 0)),
            pl.BlockSpec((s, D_MODEL), lambda i: (0, 0)),
        ],
        out_specs=pl.BlockSpec((tm, D_MODEL), lambda i: (i, 0)),
        out_shape=jax.ShapeDtypeStruct((D_IN, D_MODEL), F32),
        compiler_params=_params(("arbitrary",)),
    )(dzm, dkv, hn)


def _wgrad(a, b, tm, tn, name, peer_cols=0, after=None):
    s, m = a.shape
    n = b.shape[1]

    def body(a_ref, b_ref, o_ref, at_ref):
        @pl.when(pl.program_id(1) == 0)
        def _():
            at_ref[...] = a_ref[...].astype(BF).T

        r = _nn(at_ref[...], b_ref[...])
        if peer_cols:
            for q in range(tn // peer_cols):
                o_ref[q] = r[:, q * peer_cols:(q + 1) * peer_cols]
        else:
            o_ref[...] = r

    if peer_cols:
        out_spec = pl.BlockSpec((tn // peer_cols, tm, peer_cols), lambda i, j: (j, i, 0))
        out_shape = jax.ShapeDtypeStruct((n // peer_cols, m, peer_cols), F32)
    else:
        out_spec = pl.BlockSpec((tm, tn), lambda i, j: (i, j))
        out_shape = jax.ShapeDtypeStruct((m, n), F32)
    return _call(
        body,
        name=name,
        after=after,
        grid=(m // tm, n // tn),
        in_specs=[pl.BlockSpec((s, tm), lambda i, j: (0, i)), pl.BlockSpec((s, tn), lambda i, j: (0, j))],
        out_specs=out_spec,
        out_shape=out_shape,
        scratch_shapes=[pltpu.VMEM((tm, s), BF)],
        compiler_params=_params(("arbitrary", "arbitrary")),
    )(a, b)


def _adamw_math(w, g, m, v):
    m_new = ADAM_B1 * m + (1.0 - ADAM_B1) * g
    v_new = ADAM_B2 * v + (1.0 - ADAM_B2) * jnp.square(g)
    m_hat = m_new / (1.0 - ADAM_B1 ** ADAM_STEP)
    v_hat = v_new / (1.0 - ADAM_B2 ** ADAM_STEP)
    delta = -ADAM_LR * (m_hat / (jnp.sqrt(v_hat) + ADAM_EPS) + ADAM_WD * w)
    return delta, m_new, v_new


def _final_adamw(part, recv, w, m, v, name):
    r, c = w.shape
    tr = min(r, 512)

    def body(p_ref, r_ref, w_ref, m_ref, v_ref, g_ref, d_ref, mo_ref, vo_ref):
        g = p_ref[...]
        for j in range(3):
            g = g + r_ref[j].astype(F32)
        g_ref[...] = g
        d_ref[...], mo_ref[...], vo_ref[...] = _adamw_math(w_ref[...], g, m_ref[...], v_ref[...])

    spec = pl.BlockSpec((tr, c), lambda i: (i, 0))
    return _call(
        body,
        name=name,
        grid=(r // tr,),
        in_specs=[spec, pl.BlockSpec((3, tr, c), lambda i: (0, i, 0)), spec, spec, spec],
        out_specs=[spec] * 4,
        out_shape=[jax.ShapeDtypeStruct((r, c), F32)] * 4,
        compiler_params=_params(("arbitrary",)),
    )(part, recv, w, m, v)


def _rs_sum(gs, lands, blocks, name, smalls, small_lands, small_dtypes):
    n, n_small = len(gs), len(smalls)
    steps = pl.cdiv(gs[0].shape[1], 256)
    assert all(g.shape[1] % steps == 0 for g in gs)

    def body(blk_ref, *refs):
        ins, small_ins, outs = refs[:5 * n], refs[5 * n:5 * n + 2 * n_small], refs[5 * n + 2 * n_small:]
        for k in range(n):
            g0_ref, g1_ref, g2_ref, g3_ref, l_ref = ins[5 * k:5 * k + 5]
            outs[2 * k][...] = g0_ref[0] + l_ref[0]
            for j, gj_ref in enumerate((g1_ref, g2_ref, g3_ref)):
                outs[2 * k + 1][j] = (gj_ref[0] + l_ref[j + 1]).astype(BF)
        for q in range(n_small):
            outs[2 * n + q][...] = (small_ins[q][...] + small_ins[n_small + q][...]).astype(small_dtypes[q])

    def whole(a):
        return pl.BlockSpec(a.shape, lambda i, blk: (0,) * a.ndim)

    in_specs, out_specs, out_shape, operands = [], [], [], []
    for g, land in zip(gs, lands):
        _, r, c = g.shape
        tr = r // steps
        in_specs += [pl.BlockSpec((1, tr, c), lambda i, blk, j=j: (blk[j], i, 0)) for j in range(4)]
        in_specs.append(pl.BlockSpec((4, tr, c), lambda i, blk: (0, i, 0)))
        out_specs += [pl.BlockSpec((tr, c), lambda i, blk: (i, 0)), pl.BlockSpec((3, tr, c), lambda i, blk: (0, i, 0))]
        out_shape += [jax.ShapeDtypeStruct((r, c), F32), jax.ShapeDtypeStruct((3, r, c), BF)]
        operands += [g, g, g, g, land]
    in_specs += [whole(a) for a in (*smalls, *small_lands)]
    out_specs += [whole(a) for a in smalls]
    out_shape += [jax.ShapeDtypeStruct(a.shape, dt) for a, dt in zip(smalls, small_dtypes)]
    outs = _call(
        body,
        name=name,
        grid_spec=pltpu.PrefetchScalarGridSpec(num_scalar_prefetch=1, grid=(steps,), in_specs=in_specs, out_specs=out_specs),
        out_shape=out_shape,
        compiler_params=_params(("arbitrary",)),
    )(blocks, *operands, *smalls, *small_lands)
    return [(outs[2 * k], outs[2 * k + 1]) for k in range(n)], list(outs[2 * n:])


def _final_adamw_few(parts, recvs, ws, ms, vs, name):
    n = len(ws)

    def body(*refs):
        ins, outs = refs[:5 * n], refs[5 * n:]
        for k in range(n):
            p_ref, r_ref, w_ref, m_ref, v_ref = (ins[q * n + k] for q in range(5))
            g = p_ref[...]
            for j in range(3):
                g = g + r_ref[j].astype(F32)
            outs[k][...] = g
            outs[n + k][...], outs[2 * n + k][...], outs[3 * n + k][...] = _adamw_math(w_ref[...], g, m_ref[...], v_ref[...])

    outs = _call(
        body,
        name=name,
        out_shape=[jax.ShapeDtypeStruct(w.shape, F32) for w in ws] * 4,
        compiler_params=_params(),
    )(*parts, *recvs, *ws, *ms, *vs)
    return outs[:n], outs[n:2 * n], outs[2 * n:3 * n], outs[3 * n:]


def _adamw_small(own, recv, g1_own, g1_recv, weights, moms, vels, after):
    n_w = len(weights)

    def body(*refs):
        own_refs, recv_refs, g1_own_ref, g1_recv_ref = refs[:5], refs[5:10], refs[10], refs[11]
        w_refs, m_refs, v_refs = refs[12:12 + n_w], refs[12 + n_w:12 + 2 * n_w], refs[12 + 2 * n_w:12 + 3 * n_w]
        outs = refs[12 + 3 * n_w:]
        loss_ref, g_refs, d_refs = outs[0], outs[1:1 + n_w], outs[1 + n_w:1 + 2 * n_w]
        mo_refs, vo_refs = outs[1 + 2 * n_w:1 + 3 * n_w], outs[1 + 3 * n_w:]
        x, y, c = lax.axis_index("x"), lax.axis_index("y"), lax.axis_index("c")

        def in_place_order(values, my_place):
            acc = None
            for place in range(len(values)):
                r = place ^ my_place
                term = values[-1]
                for q in range(len(values) - 2, -1, -1):
                    term = jnp.where(r == q, values[q], term)
                acc = term if acc is None else acc + term
            return acc

        def total(k, *index):
            index = index or (slice(None),) * (len(own_refs[k].shape) - 1)
            across = [own_refs[k][(0, *index)], recv_refs[k][(1, *index)], recv_refs[k][(0, *index)], recv_refs[k][(2, *index)]]
            return in_place_order([v.astype(F32) for v in across], 2 * x + y)

        g1 = in_place_order([g1_own_ref[0]] + [g1_recv_ref[j] for j in range(N_DEV - 1)], 4 * x + 2 * y + c)
        grads = [
            g1, total(1), total(2), total(3),
            total(4, slice(N_Q, None), slice(0, N_Q)), total(4, slice(0, N_Q), slice(None)),
            total(0, slice(0, 1), slice(None)), total(0, slice(1, 2), slice(None))]
        loss_ref[...] = total(0, slice(2, 3), slice(0, 1))
        for k in range(n_w):
            g_refs[k][...] = grads[k]
            d_refs[k][...], mo_refs[k][...], vo_refs[k][...] = _adamw_math(w_refs[k][...], grads[k], m_refs[k][...], v_refs[k][...])

    shapes = [jax.ShapeDtypeStruct(w.shape, F32) for w in weights]
    outs = _call(
        body,
        name="adamw_small",
        after=after,
        in_specs=[pl.BlockSpec(memory_space=pltpu.VMEM)] * (12 + 3 * n_w),
        out_shape=[jax.ShapeDtypeStruct((1, 1), F32)] + shapes * 4,
        compiler_params=_params(),
    )(*own, *recv, g1_own, g1_recv, *weights, *moms, *vels)
    return outs[0], outs[1:1 + n_w], outs[1 + n_w:1 + 2 * n_w], outs[1 + 2 * n_w:1 + 3 * n_w], outs[1 + 3 * n_w:]


def _place():
    x, y, c = lax.axis_index("x"), lax.axis_index("y"), lax.axis_index("c")
    return x, y, c, [(1 - x, y), (x, 1 - y), (1 - x, 1 - y)]


def _dev_index(px, py, pc):
    return 4 * px + 2 * py + pc


HBM_SPEC = pl.BlockSpec(memory_space=pltpu.HBM)
SEM_SPEC = pl.BlockSpec(memory_space=pltpu.SEMAPHORE)
ANY_SPEC = pl.BlockSpec(memory_space=pl.ANY)
DATAFLOW = pltpu.SideEffectType.DATAFLOW_SIDE_EFFECTING


def _hbm(a):
    return pltpu.with_memory_space_constraint(a, pltpu.HBM)


def _ag_piece(land_k, block, half, peer, send_sem, recv_sem):
    ref = land_k.at[_dev_index(*block)]
    if half is not None:
        rows = land_k.shape[1] // 2
        ref = ref.at[pl.ds(half * rows, rows)]
    return pltpu.make_async_remote_copy(
        src_ref=ref, dst_ref=ref, send_sem=send_sem, recv_sem=recv_sem, device_id=peer, device_id_type=MESH)


def _ag_plan():
    x, y, c, _ = _place()
    me, sib = (x, y, c), (x, y, 1 - c)
    xn, yn, diag = (1 - x, y, c), (x, 1 - y, c), (1 - x, 1 - y, c)
    return dict(
        relay_halves=[(me, 0, xn), (me, 1, yn)],
        others=[(me, None, sib), (me, 1, xn), (me, 0, yn)],
        relays=[(xn, 0, yn), (yn, 1, xn)],
        near=[(xn, None, sib), (yn, None, sib)],
        far=[(diag, None, sib)],
    )


def _ag_stage(land, stage, send_sems, recv_sems, act):
    copies = _ag_plan()[stage]
    n = len(copies)
    for k in range(len(land)):
        for j, (block, half, peer) in enumerate(copies):
            cp = _ag_piece(land[k], block, half, peer, send_sems.at[n * k + j], recv_sems.at[n * k + j])
            if act == "start":
                cp.start()
            else:
                cp.wait_send()
                cp.wait_recv()


def _sem_shapes(*counts):
    return [pltpu.SemaphoreType.DMA((n,)) for n in counts for _ in range(2)]


def _ag_start(first, rest):
    shards = list(first) + list(rest)
    k_n, k_first = len(shards), len(first)
    k_rest = k_n - k_first

    def body(*refs):
        ins, sems, land = refs[:k_n], refs[k_n:k_n + 6], refs[k_n + 6:2 * k_n + 6]
        token, stage, wide = refs[2 * k_n + 6], refs[2 * k_n + 7:3 * k_n + 7], refs[3 * k_n + 7:4 * k_n + 7]
        own_sems = refs[4 * k_n + 7]
        x, y, c, chips = _place()
        loads = [pltpu.make_async_copy(ins[k], wide[k], own_sems.at[k]) for k in range(k_n)]
        for cp in loads:
            cp.start()

        def place(ks):
            copies = []
            for k in ks:
                loads[k].wait()
                stage[k][...] = wide[k][...].astype(BF)
                copies.append(pltpu.make_async_copy(stage[k], land[k].at[_dev_index(x, y, c)], own_sems.at[k_n + k]))
                copies[-1].start()
            for cp in copies:
                cp.wait()

        place(range(k_first))
        targets = [(x, y, 1 - c)] + [(*chip, c) for chip in chips]
        for k in range(k_first):
            for j, to in enumerate(targets):
                _ag_piece(land[k], (x, y, c), None, to, sems[0].at[4 * k + j], sems[1].at[4 * k + j]).start()
        place(range(k_first, k_n))
        _ag_stage(land[k_first:], "relay_halves", sems[2], sems[3], "start")
        _ag_stage(land[k_first:], "others", sems[4], sems[5], "start")
        token[...] = jnp.zeros_like(token)

    outs = pl.pallas_call(
        body,
        name="ag_start",
        in_specs=[ANY_SPEC] * k_n,
        out_specs=(*[SEM_SPEC] * 6, *[HBM_SPEC] * k_n, pl.BlockSpec(memory_space=pltpu.VMEM)),
        out_shape=(*_sem_shapes(4 * k_first, 2 * k_rest, 3 * k_rest),
                   *[pltpu.HBM((N_DEV,) + a.shape, BF) for a in shards], jax.ShapeDtypeStruct((8, LANES), F32)),
        scratch_shapes=[pltpu.VMEM(a.shape, dt) for dt in (BF, F32) for a in shards] + [pltpu.SemaphoreType.DMA((2 * k_n,))],
        compiler_params=pltpu.CompilerParams(has_side_effects=DATAFLOW, vmem_limit_bytes=CALL_VMEM_LIMIT),
    )(*shards)
    flying = list(outs[6:6 + k_n])
    return (outs[0], outs[1], flying[:k_first]), (outs[2:6], flying[k_first:]), outs[-1]


def _ag_split_call(lands, waits, starts, after, name):
    k_n = len(lands)
    plan_sizes = dict(relay_halves=2, others=3, relays=2, near=2, far=1)
    n_in, n_out = 2 * len(waits), 2 * len(starts)

    def body(*refs):
        land = refs[:k_n]
        in_sems = refs[k_n:k_n + n_in]
        out_sems, token = refs[len(refs) - 1 - n_out:len(refs) - 1], refs[-1]
        for w, (stage, _, _) in enumerate(waits):
            _ag_stage(land, stage, in_sems[2 * w], in_sems[2 * w + 1], "wait")
            if w < len(starts):
                _ag_stage(land, starts[w], out_sems[2 * w], out_sems[2 * w + 1], "start")
        token[...] = jnp.zeros_like(token)

    outs = pl.pallas_call(
        body,
        name=name,
        in_specs=[HBM_SPEC] * k_n + [SEM_SPEC] * n_in + [ANY_SPEC],
        out_specs=(*[HBM_SPEC] * k_n, *[SEM_SPEC] * n_out, pl.BlockSpec(memory_space=pltpu.VMEM)),
        out_shape=(*[pltpu.HBM(a.shape, a.dtype) for a in lands], *_sem_shapes(*[plan_sizes[s] * k_n for s in starts]),
                   jax.ShapeDtypeStruct((8, LANES), F32)),
        input_output_aliases={k: k for k in range(k_n)},
        compiler_params=pltpu.CompilerParams(has_side_effects=DATAFLOW),
    )(*lands, *[s for _, a, b in waits for s in (a, b)], after)
    return list(outs[:k_n]), list(outs[k_n:k_n + n_out]), outs[-1]


def _ag_mid(lands, send_sems, recv_sems, after, name):
    k_n = len(lands)

    def body(*refs):
        land = refs[:k_n]
        send1, recv1 = refs[k_n], refs[k_n + 1]
        fwd_send, fwd_recv = refs[-2], refs[-1]
        x, y, c, chips = _place()
        sources = [(x, y, 1 - c)] + [(*chip, c) for chip in chips]
        for k in range(k_n):
            mine = land[k].at[_dev_index(x, y, c)]
            for j, frm in enumerate(sources):
                got = land[k].at[_dev_index(*frm)]
                cp = pltpu.make_async_remote_copy(
                    src_ref=mine, dst_ref=got, send_sem=send1.at[4 * k + j], recv_sem=recv1.at[4 * k + j], device_id=frm, device_id_type=MESH)
                cp.wait_send()
                cp.wait_recv()
                if j >= 1:
                    pltpu.make_async_remote_copy(
                        src_ref=got, dst_ref=got, send_sem=fwd_send.at[3 * k + j - 1], recv_sem=fwd_recv.at[3 * k + j - 1],
                        device_id=(x, y, 1 - c), device_id_type=MESH).start()

    outs = pl.pallas_call(
        body,
        name=name,
        in_specs=[HBM_SPEC] * k_n + [SEM_SPEC, SEM_SPEC, ANY_SPEC],
        out_specs=(*[HBM_SPEC] * k_n, SEM_SPEC, SEM_SPEC),
        out_shape=(*[pltpu.HBM(a.shape, a.dtype) for a in lands], pltpu.SemaphoreType.DMA((3 * k_n,)), pltpu.SemaphoreType.DMA((3 * k_n,))),
        input_output_aliases={k: k for k in range(k_n)},
        compiler_params=pltpu.CompilerParams(has_side_effects=DATAFLOW),
    )(*lands, send_sems, recv_sems, after)
    return list(outs[:k_n]), outs[-2], outs[-1]


def _ag_end(lands, fwd_send, fwd_recv, name):
    k_n = len(lands)

    def body(*refs):
        land = refs[:k_n]
        fsend, frecv = refs[k_n], refs[k_n + 1]
        x, y, c, chips = _place()
        for k in range(k_n):
            for j, chip in enumerate(chips):
                cp = pltpu.make_async_remote_copy(
                    src_ref=land[k].at[_dev_index(*chip, c)], dst_ref=land[k].at[_dev_index(*chip, 1 - c)],
                    send_sem=fsend.at[3 * k + j], recv_sem=frecv.at[3 * k + j], device_id=(x, y, 1 - c), device_id_type=MESH)
                cp.wait_send()
                cp.wait_recv()

    outs = pl.pallas_call(
        body,
        name=name,
        in_specs=[HBM_SPEC] * k_n + [SEM_SPEC, SEM_SPEC],
        out_specs=tuple([HBM_SPEC] * k_n),
        out_shape=tuple(pltpu.HBM(a.shape, a.dtype) for a in lands),
        input_output_aliases={k: k for k in range(k_n)},
        compiler_params=pltpu.CompilerParams(has_side_effects=DATAFLOW),
    )(*lands, fwd_send, fwd_recv)
    return list(outs)


def _chips4():
    x, y, c, others = _place()
    return x, y, c, [(x, y)] + others


def _route_sibling(j):
    x, y, c, chips = _chips4()
    return _dev_index(*chips[j], 1 - c), j, (x, y, 1 - c)


def _route_chips(j):
    x, y, c, chips = _chips4()
    return j, j, (*chips[j + 1], c)


def _route_sibling_whole(j):
    x, y, c, _ = _chips4()
    return 0, 0, (x, y, 1 - c)


def _route_chips_whole(j):
    x, y, c, chips = _chips4()
    return 0, j, (*chips[j + 1], c)


def _route_everyone(j):
    x, y, c, _ = _chips4()
    flip = [(j + 1) >> 2 & 1, (j + 1) >> 1 & 1, (j + 1) & 1]
    return 0, j, tuple(1 - v if f else v for v, f in zip((x, y, c), flip))


def _xchg_copies(routes, src, dst, send_sems, recv_sems):
    copies, sem = [], 0
    for k, (route, n) in enumerate(routes):
        for j in range(n):
            si, di, peer = route(j)
            copies.append(pltpu.make_async_remote_copy(
                src_ref=src[k].at[si], dst_ref=dst[k].at[di], send_sem=send_sems.at[sem], recv_sem=recv_sems.at[sem],
                device_id=peer, device_id_type=MESH))
            sem += 1
    return copies


def _xchg_start(srcs, slot_shapes, routes, name, after=None):
    k_n = len(srcs)
    n_in = 2 * k_n + (after is not None)
    n_sem = sum(n for _, n in routes)
    dsts = [lax.empty((n,) + tuple(sh), a.dtype) for sh, a, (_, n) in zip(slot_shapes, srcs, routes)]

    def body(*refs):
        src, dst = refs[:k_n], refs[k_n:2 * k_n]
        send_sems, recv_sems, token = refs[n_in], refs[n_in + 1], refs[-1]
        for cp in _xchg_copies(routes, src, dst, send_sems, recv_sems):
            cp.start()
        token[...] = jnp.zeros_like(token)

    arrays = list(srcs) + dsts
    outs = pl.pallas_call(
        body,
        name=name,
        in_specs=[HBM_SPEC] * (2 * k_n) + [ANY_SPEC] * (n_in - 2 * k_n),
        out_specs=(SEM_SPEC, SEM_SPEC, *[HBM_SPEC] * (2 * k_n), pl.BlockSpec(memory_space=pltpu.VMEM)),
        out_shape=(pltpu.SemaphoreType.DMA((n_sem,)), pltpu.SemaphoreType.DMA((n_sem,)),
                   *[pltpu.HBM(a.shape, a.dtype) for a in arrays], jax.ShapeDtypeStruct((8, LANES), F32)),
        input_output_aliases={i: 2 + i for i in range(2 * k_n)},
        compiler_params=pltpu.CompilerParams(has_side_effects=DATAFLOW),
    )(*[_hbm(a) for a in arrays], *([] if after is None else [after]))
    return outs[0], outs[1], list(outs[2:2 + k_n]), list(outs[2 + k_n:2 + 2 * k_n]), outs[-1]


def _xchg_wait(send_sems, recv_sems, srcs, dsts, routes, after, name):
    k_n = len(srcs)

    def body(*refs):
        src, dst = refs[:k_n], refs[k_n:2 * k_n]
        for cp in _xchg_copies(routes, src, dst, refs[2 * k_n], refs[2 * k_n + 1]):
            cp.wait_send()
            cp.wait_recv()

    arrays = list(srcs) + list(dsts)
    outs = pl.pallas_call(
        body,
        name=name,
        in_specs=[HBM_SPEC] * (2 * k_n) + [SEM_SPEC, SEM_SPEC, ANY_SPEC],
        out_specs=tuple([HBM_SPEC] * (2 * k_n)),
        out_shape=tuple(pltpu.HBM(a.shape, a.dtype) for a in arrays),
        input_output_aliases={i: i for i in range(2 * k_n)},
        compiler_params=pltpu.CompilerParams(has_side_effects=DATAFLOW),
    )(*arrays, send_sems, recv_sems, after)
    return list(outs[:k_n]), list(outs[k_n:])


SMALL = ("norm1_gain", "gmlp_v_gain", "w_spatial", "b_spatial", "attn_sinks", "rel_bias_table", "norm2_gain", "final_gain")
LANES = 128


def _swap_start(grads, smalls, tag, after=None):
    srcs = list(grads) + [a[None] for a in smalls]
    shapes = [g.shape[1:] for g in grads] + [a.shape for a in smalls]
    routes = [(_route_sibling, 4)] * len(grads) + [(_route_sibling_whole, 1)] * len(smalls)
    return _xchg_start(srcs, shapes, routes, f"rs_{tag}_swap_start", after), routes, len(grads)


def _swap_sums(swap, after, tag, small_dtypes=None):
    (send1, recv1, src1, land1, _), routes, n_rs = swap
    x, y, c, chips = _chips4()
    blocks = jnp.stack([_dev_index(*chip, c) for chip in chips]).astype(jnp.int32)
    src1, land1 = _xchg_wait(send1, recv1, src1, land1, routes, after, f"rs_{tag}_swap_wait")
    return _rs_sum(src1[:n_rs], land1[:n_rs], blocks, f"rs_{tag}_sum", src1[n_rs:], land1[n_rs:],
                   small_dtypes or [F32] * (len(src1) - n_rs))


def _chips_start(sums, small_sums, tag, everyone=()):
    sends = [ps[1] for ps in sums] + list(small_sums) + [a[None] for a in everyone]
    routes = [(_route_chips, 3)] * len(sums) + [(_route_chips_whole, 3)] * len(small_sums) + [(_route_everyone, 7)] * len(everyone)
    return _xchg_start(sends, [a.shape[1:] for a in sends], routes, f"rs_{tag}_chips_start"), routes


def _chips_wait(chips, after, tag):
    (send2, recv2, src2, land2, _), routes = chips
    return _xchg_wait(send2, recv2, src2, land2, routes, after, f"rs_{tag}_chips_wait")


def kernel(x, p, norm1_gain, w_in, gmlp_v_gain, w_spatial, b_spatial, attn_sinks, rel_bias_table, w_out, norm2_gain, w_ff1, w_ff2, w_ple_proj, w_ple_gate, final_gain, loss_target, m_norm1_gain, m_w_in, m_gmlp_v_gain, m_w_spatial, m_b_spatial, m_attn_sinks, m_rel_bias_table, m_w_out, m_norm2_gain, m_w_ff1, m_w_ff2, m_w_ple_proj, m_w_ple_gate, m_final_gain, v_norm1_gain, v_w_in, v_gmlp_v_gain, v_w_spatial, v_b_spatial, v_attn_sinks, v_rel_bias_table, v_w_out, v_norm2_gain, v_w_ff1, v_w_ff2, v_w_ple_proj, v_w_ple_gate, v_final_gain):
    args = dict(locals())
    s = x.shape[1]
    big = ("w_in", "w_out", "w_ff1", "w_ff2", "w_ple_proj", "w_ple_gate")

    x2, p2, t2 = x.reshape(s, D_MODEL), p.reshape(s, PLE_DIM), loss_target.reshape(s, D_MODEL)
    g1, gv, w_sp, b_sp, sinks, table, g2, gf = (args[n] for n in SMALL)
    bucket = jnp.asarray(_bucket_table())
    b_col = b_sp.reshape(GROUPS, BLOCK, 1)

    def shard(name):
        return args[name][0].T if name.endswith("w_in") else args[name][0]

    (send_in, recv_in, fly_in), (rest_sems, fly_rest), token = _ag_start([shard(big[0])], [shard(n) for n in big[1:]])
    mid_in, fwd_send_in, fwd_recv_in = _ag_mid(fly_in, send_in, recv_in, token, "ag_mid_w_in")
    g_in = _ag_end(mid_in, fwd_send_in, fwd_recv_in, "ag_end_w_in")[0]
    full_in = g_in.reshape(D_IN, D_MODEL)

    zuv, qkv, hn1 = _in_proj(x2, g1, full_in)
    fly_rest, relay_sems, relayed = _ag_split_call(
        fly_rest, [("relay_halves", *rest_sems[:2])], ["relays"], zuv, "ag_relay")
    mix, *saved = _mixer_fwd(zuv, qkv, gv, w_sp[0], b_col, sinks, table.T, bucket, after=relayed)
    fly_rest, fwd_sems, _ = _ag_split_call(
        fly_rest, [("others", *rest_sems[2:]), ("relays", *relay_sems)], ["near", "far"], mix, "ag_mid_rest")
    g_out, g_ff1, g_ff2, g_proj, g_gate = _ag_split_call(
        fly_rest, [("near", *fwd_sems[:2]), ("far", *fwd_sems[2:])], [], mix, "ag_end_rest")[0]
    full_out, full_ff2, full_gate = g_out.reshape(D_MODEL, D_MODEL), g_ff2.reshape(D_FF, D_MODEL), g_gate.reshape(D_MODEL, D_MODEL)

    tail_small, dh1, dmix, hn2, a, df, dh2, gw_out, gw_proj, gw_gate = _tail(
        x2, mix, p2, t2, g2, gf.reshape(1, D_MODEL), full_out, g_ff1, full_ff2, full_gate, g_proj)
    gw_out, gw_gate = (g.reshape(N_DEV, D_MODEL // N_DEV, D_MODEL) for g in (gw_out, gw_gate))

    few_names = ("w_out", "w_ple_proj", "w_ple_gate")
    ff1_names = ("w_ff1",) + few_names
    gw_ff1 = _wgrad(hn2, df, D_MODEL, 2 * D_FF // N_DEV, "wgrad_ff1", peer_cols=D_FF // N_DEV)
    ff1_swap = _swap_start([gw_ff1, gw_out, gw_proj, gw_gate], [tail_small], "ff1")
    gw_ff2 = _wgrad(a, dh2, 1024, D_MODEL, "wgrad_ff2", after=ff1_swap[0][4]).reshape(N_DEV, D_FF // N_DEV, D_MODEL)
    ff1_sums, ff1_small = _swap_sums(ff1_swap, gw_ff2, "ff1")
    ff1_chips = _chips_start(ff1_sums, ff1_small, "ff1")
    ff2_swap = _swap_start([gw_ff2], [], "ff2", after=ff1_chips[0][4])
    dzm, dkv, d_gv, d_wsp, d_bsp, d_attn = _mixer_bwd(
        zuv, qkv, dmix, saved, gv, w_sp[0], b_col, bucket, after=ff2_swap[0][4])
    ff2_sums, _ = _swap_sums(ff2_swap, dzm, "ff2")
    ff2_chips = _chips_start(ff2_sums, [], "ff2")
    gw_in = _wgrad_in(dzm, dkv, hn1, after=ff2_chips[0][4]).reshape(N_DEV, D_IN // N_DEV, D_MODEL)
    in_swap = _swap_start([gw_in], [d_gv, d_wsp, d_bsp, d_attn], "in")
    dx, d_g1 = _in_bwd(x2, dh1, dzm, dkv, g1, full_in, after=in_swap[0][4])
    grad_x = dx.reshape(x.shape)
    in_sums, in_small = _swap_sums(in_swap, dx, "in", small_dtypes=[F32, BF, F32, F32])
    in_chips = _chips_start(in_sums, in_small, "in", everyone=[d_g1])

    grads, deltas, new_m, new_v = {}, {}, {}, {}
    after, small_own, small_recv = in_chips[0][4], [], []
    for chips, sums, names, tag in ((ff1_chips, ff1_sums, ff1_names, "ff1"), (ff2_chips, ff2_sums, ("w_ff2",), "ff2"),
                                    (in_chips, in_sums, ("w_in",), "in")):
        sent, recvs = _chips_wait(chips, after, tag)
        small_own += sent[len(names):]
        small_recv += recvs[len(names):]
        tiled = [n for n in names if n not in few_names]
        done = [_final_adamw(part, recv, shard(n), shard("m_" + n), shard("v_" + n), "adamw_" + n)
                for n, (part, _), recv in zip(tiled, sums, recvs)]
        few = names[len(tiled):]
        if few:
            done += zip(*_final_adamw_few(
                [ps[0] for ps in sums[len(tiled):]], recvs[len(tiled):len(names)], [shard(n) for n in few],
                [shard("m_" + n) for n in few], [shard("v_" + n) for n in few], "adamw_few"))
        for n, results in zip(names, done):
            for dst, arr in zip((grads, deltas, new_m, new_v), results):
                dst[n] = (arr.T if n == "w_in" else arr)[None]
            after = results[1]

    views = {"w_spatial": (GROUPS, BLOCK, BLOCK), "b_spatial": (GROUPS, BLOCK), "final_gain": (1, D_MODEL)}
    def view(name, base):
        return args[name].T if base == "rel_bias_table" else args[name].reshape(views.get(base, args[name].shape))

    small_in = [[view(pre + n, n) for n in SMALL] for pre in ("", "m_", "v_")]
    loss, *small_out = _adamw_small(small_own[:-1], small_recv[:-1], small_own[-1], small_recv[-1], *small_in, after=after)
    for dst, arrays in zip((grads, deltas, new_m, new_v), small_out):
        for n, arr in zip(SMALL, arrays):
            dst[n] = arr.T if n == "rel_bias_table" else arr.reshape(args[n].shape)
    loss = loss[0, 0]

    order = ("norm1_gain", "w_in", "gmlp_v_gain", "w_spatial", "b_spatial", "attn_sinks", "rel_bias_table", "w_out",
             "norm2_gain", "w_ff1", "w_ff2", "w_ple_proj", "w_ple_gate", "final_gain")
    return (loss, grad_x, *[grads[n] for n in order], *[deltas[n] for n in order],
            *[new_m[n] for n in order], *[new_v[n] for n in order])
```
